```python
import math
import jax, jax.numpy as jnp
from jax import lax
import numpy as np

D_MODEL = 1024
BATCH = 8
SEQ = 4096
DEPTH = 1

PLE_DIM = 256
DN_HEADS = 8
DN_DK = 128
DN_DV = 128
DN_CONV = 4
DN_CHUNK = 64
DN_QK_W = DN_HEADS * DN_DK
DN_V_W = DN_HEADS * DN_DV
DN_CONV_CH = 2 * DN_QK_W + DN_V_W
MLA_HEADS = 8
MLA_Q_LORA = 384
MLA_KV_LORA = 256
MLA_NOPE = 128
MLA_ROPE = 64
MLA_V = 128
MLA_V_W = MLA_HEADS * MLA_V
ROPE_BASE = 10000.0
Q_BLOCK = 128
FFN_HIDDEN = -(-8 * D_MODEL // (3 * 256)) * 256
DEEPNORM_ALPHA = (2.0 * DEPTH) ** 0.25
DEEPNORM_BETA = (8.0 * DEPTH) ** -0.25
IN_SIZES = (DN_CONV_CH, DN_V_W, DN_HEADS, DN_HEADS, MLA_Q_LORA, MLA_KV_LORA, MLA_ROPE, D_MODEL, D_MODEL)
D_IN = sum(IN_SIZES)
SPLIT_IDX = tuple(int(v) for v in np.cumsum(IN_SIZES)[:-1])
NEG_BIG = -1e30

kernel_name = 'hybrid_deltanet_mla_deepnorm_block'


def layer_norm(t, g, b, eps=1e-5):
    tf = t.astype(jnp.float32)
    mu = jnp.mean(tf, axis=-1, keepdims=True)
    var = jnp.mean(jnp.square(tf - mu), axis=-1, keepdims=True)
    return ((tf - mu) * lax.rsqrt(var + eps) * g.astype(jnp.float32) + b.astype(jnp.float32)).astype(t.dtype)


def rms_norm(t, w, eps=1e-6):
    tf = t.astype(jnp.float32)
    return tf * lax.rsqrt(jnp.mean(jnp.square(tf), axis=-1, keepdims=True) + eps) * w.astype(jnp.float32)


def l2_normalize(t, eps=1e-6):
    tf = t.astype(jnp.float32)
    return tf * lax.rsqrt(jnp.sum(jnp.square(tf), axis=-1, keepdims=True) + eps)


def rope_tables(positions):
    inv_freq = ROPE_BASE ** (-jnp.arange(0, MLA_ROPE, 2, dtype=jnp.float32) / MLA_ROPE)
    ang = positions.astype(jnp.float32)[..., None] * inv_freq
    return jnp.cos(ang), jnp.sin(ang)


def apply_rope(t, cos, sin):
    t1, t2 = jnp.split(t.astype(jnp.float32), 2, axis=-1)
    return jnp.concatenate([t1 * cos - t2 * sin, t2 * cos + t1 * sin], axis=-1)


def causal_depthwise_conv(t, w):
    width, ch = w.shape
    return lax.conv_general_dilated(t, w[:, None, :].astype(t.dtype), window_strides=(1,), padding=[(width - 1, 0)], dimension_numbers=('NWC', 'WIO', 'NWC'), feature_group_count=ch)


def gated_delta_rule(q, k, v, beta, g):
    b, s, h, dk = q.shape
    dv = v.shape[-1]
    c = DN_CHUNK
    n = s // c

    def chunk(t):
        return jnp.swapaxes(t.reshape((b, n, c) + t.shape[2:]), 2, 3)

    q, k, v, beta, g = (chunk(t) for t in (q, k, v, beta, g))
    g = jnp.cumsum(g, axis=-1)
    tril = jnp.tril(jnp.ones((c, c), dtype=bool))
    strict = jnp.tril(jnp.ones((c, c), dtype=bool), k=-1)
    diff = g[..., :, None] - g[..., None, :]
    decay = jnp.where(tril, jnp.exp(jnp.where(tril, diff, 0.0)), 0.0)
    k_beta = k * beta[..., None]
    l_mat = jnp.where(strict, jnp.einsum('bnhid,bnhjd->bnhij', k_beta, k) * decay, 0.0)
    eye = jnp.eye(c, dtype=q.dtype)
    t_inv = lax.linalg.triangular_solve(eye + l_mat, jnp.broadcast_to(eye, l_mat.shape), left_side=True, lower=True, unit_diagonal=True)
    u = jnp.einsum('bnhij,bnhje->bnhie', t_inv, v * beta[..., None])
    w = jnp.einsum('bnhij,bnhjd->bnhid', t_inv, k_beta * jnp.exp(g)[..., None])
    intra = jnp.where(tril, jnp.einsum('bnhid,bnhjd->bnhij', q, k) * decay, 0.0)
    q_dec = q * jnp.exp(g)[..., None]
    g_last = g[..., -1]
    k_tail = k * jnp.exp(g_last[..., None] - g)[..., None]

    def step(state, xs):
        w_c, u_c, q_c, a_c, kt_c, gl_c = xs
        v_new = u_c - jnp.einsum('bhcd,bhde->bhce', w_c, state)
        o_c = jnp.einsum('bhcd,bhde->bhce', q_c, state) + jnp.einsum('bhij,bhje->bhie', a_c, v_new)
        state = state * jnp.exp(gl_c)[..., None, None] + jnp.einsum('bhcd,bhce->bhde', kt_c, v_new)
        return state, o_c

    xs = tuple(jnp.moveaxis(t, 1, 0) for t in (w, u, q_dec, intra, k_tail, g_last))
    state0 = jnp.zeros((b, h, dk, dv), q.dtype)
    _, o = lax.scan(step, state0, xs)
    return jnp.transpose(o, (1, 0, 3, 2, 4)).reshape(b, s, h, dv)


def mla_attention(q_lat, q_rope, c_kv, k_rope):
    b, s, h, c = q_lat.shape
    nblk = s // Q_BLOCK
    scale = (MLA_NOPE + MLA_ROPE) ** -0.5
    ckv = c_kv.astype(jnp.float32)
    kr = k_rope.astype(jnp.float32)
    key_idx = jnp.arange(s)

    def blocks(t):
        return jnp.swapaxes(t.reshape((b, nblk, Q_BLOCK) + t.shape[2:]), 0, 1)

    def one_block(args):
        ql, qr, blk = args
        sc = (jnp.einsum('bqhc,bkc->bhqk', ql, ckv) + jnp.einsum('bqhr,bkr->bhqk', qr, kr)) * scale
        q_idx = blk * Q_BLOCK + jnp.arange(Q_BLOCK)
        sc = jnp.where(key_idx[None, :] <= q_idx[:, None], sc, NEG_BIG)
        pr = jax.nn.softmax(sc, axis=-1)
        return jnp.einsum('bhqk,bkc->bqhc', pr, ckv)

    out = lax.map(one_block, (blocks(q_lat.astype(jnp.float32)), blocks(q_rope.astype(jnp.float32)), jnp.arange(nblk)))
    return jnp.swapaxes(out, 0, 1).reshape(b, s, h, c)


def _fwd_setup_inputs(seed: int = 0) -> dict:
    key = jax.random.key(seed)
    ks = jax.random.split(key, 32)

    def nrm(k, shape, scale):
        return jax.random.normal(k, shape, jnp.float32) * scale

    x = nrm(ks[0], (BATCH, SEQ, D_MODEL), 1.0)
    p = nrm(ks[1], (DEPTH, BATCH, SEQ, PLE_DIM), 1.0)
    positions = jax.random.randint(ks[2], (BATCH, 1), 0, 1024, dtype=jnp.int32) + jnp.arange(SEQ, dtype=jnp.int32)[None, :]
    w_in = nrm(ks[3], (DEPTH, D_MODEL, D_IN), D_MODEL ** -0.5)
    conv_w = nrm(ks[4], (DEPTH, DN_CONV, DN_CONV_CH), DN_CONV ** -0.5)
    dn_a_log = jnp.log(jax.random.uniform(ks[5], (DEPTH, DN_HEADS), jnp.float32, 1.0, 16.0))
    dt = jnp.exp(jax.random.uniform(ks[6], (DEPTH, DN_HEADS), jnp.float32, math.log(1e-3), math.log(1e-1)))
    dn_dt_bias = dt + jnp.log(-jnp.expm1(-dt))
    dn_norm_w = 1.0 + nrm(ks[7], (DEPTH, DN_DV), 0.01)
    q_norm_w = 1.0 + nrm(ks[8], (DEPTH, MLA_Q_LORA), 0.01)
    w_uq = nrm(ks[9], (DEPTH, MLA_Q_LORA, MLA_HEADS, MLA_NOPE + MLA_ROPE), MLA_Q_LORA ** -0.5)
    kv_norm_w = 1.0 + nrm(ks[10], (DEPTH, MLA_KV_LORA), 0.01)
    w_uk = nrm(ks[11], (DEPTH, MLA_KV_LORA, MLA_HEADS, MLA_NOPE), MLA_KV_LORA ** -0.5)
    w_uv = nrm(ks[12], (DEPTH, MLA_KV_LORA, MLA_HEADS, MLA_V), MLA_KV_LORA ** -0.5)
    w_br_dn = nrm(ks[13], (DEPTH, DN_V_W, D_MODEL), DN_V_W ** -0.5)
    w_br_mla = nrm(ks[14], (DEPTH, MLA_V_W, D_MODEL), MLA_V_W ** -0.5)
    w_o = nrm(ks[15], (DEPTH, D_MODEL, D_MODEL), D_MODEL ** -0.5 * DEEPNORM_BETA)
    ln1_g = 1.0 + nrm(ks[16], (DEPTH, D_MODEL), 0.01)
    ln1_b = nrm(ks[17], (DEPTH, D_MODEL), 0.01)
    w_ffn_in = nrm(ks[18], (DEPTH, D_MODEL, 2 * FFN_HIDDEN), D_MODEL ** -0.5)
    w_ffn_out = nrm(ks[19], (DEPTH, FFN_HIDDEN, D_MODEL), FFN_HIDDEN ** -0.5 * DEEPNORM_BETA)
    w_ple = nrm(ks[20], (DEPTH, PLE_DIM, D_MODEL), PLE_DIM ** -0.5 * DEEPNORM_BETA)
    w_ple_gate = nrm(ks[21], (DEPTH, D_MODEL, D_MODEL), D_MODEL ** -0.5)
    ln2_g = 1.0 + nrm(ks[22], (DEPTH, D_MODEL), 0.01)
    ln2_b = nrm(ks[23], (DEPTH, D_MODEL), 0.01)
    return {'x': x, 'p': p, 'positions': positions, 'w_in': w_in, 'conv_w': conv_w, 'dn_a_log': dn_a_log, 'dn_dt_bias': dn_dt_bias, 'dn_norm_w': dn_norm_w, 'q_norm_w': q_norm_w, 'w_uq': w_uq, 'kv_norm_w': kv_norm_w, 'w_uk': w_uk, 'w_uv': w_uv, 'w_br_dn': w_br_dn, 'w_br_mla': w_br_mla, 'w_o': w_o, 'ln1_g': ln1_g, 'ln1_b': ln1_b, 'w_ffn_in': w_ffn_in, 'w_ffn_out': w_ffn_out, 'w_ple': w_ple, 'w_ple_gate': w_ple_gate, 'ln2_g': ln2_g, 'ln2_b': ln2_b}


def _fwd_reference(x, p, positions, w_in, conv_w, dn_a_log, dn_dt_bias, dn_norm_w, q_norm_w, w_uq, kv_norm_w, w_uk, w_uv, w_br_dn, w_br_mla, w_o, ln1_g, ln1_b, w_ffn_in, w_ffn_out, w_ple, w_ple_gate, ln2_g, ln2_b):
    b, s, _ = x.shape
    cos, sin = rope_tables(positions)
    h = x
    for i in range(DEPTH):
        proj = h @ w_in[i]
        qkv, z, b_raw, a_raw, cq, ckv, kr, gate_dn, gate_mla = jnp.split(proj, SPLIT_IDX, axis=-1)

        qkv = jax.nn.silu(causal_depthwise_conv(qkv, conv_w[i]))
        dq, dk, dv = jnp.split(qkv, [DN_QK_W, 2 * DN_QK_W], axis=-1)
        dq = l2_normalize(dq.reshape(b, s, DN_HEADS, DN_DK)) * (DN_DK ** -0.5)
        dk = l2_normalize(dk.reshape(b, s, DN_HEADS, DN_DK))
        dv = dv.reshape(b, s, DN_HEADS, DN_DV).astype(jnp.float32)
        beta = jax.nn.sigmoid(b_raw.astype(jnp.float32))
        g = -jnp.exp(dn_a_log[i].astype(jnp.float32)) * jax.nn.softplus(a_raw.astype(jnp.float32) + dn_dt_bias[i].astype(jnp.float32))
        o_dn = gated_delta_rule(dq, dk, dv, beta, g)
        o_dn = rms_norm(o_dn, dn_norm_w[i]) * jax.nn.silu(z.reshape(b, s, DN_HEADS, DN_DV).astype(jnp.float32))
        y_dn = o_dn.reshape(b, s, DN_V_W).astype(h.dtype) @ w_br_dn[i]

        c_q = rms_norm(cq, q_norm_w[i]).astype(h.dtype)
        q_full = jnp.einsum('bsc,chd->bshd', c_q, w_uq[i])
        q_nope, q_rope = jnp.split(q_full, [MLA_NOPE], axis=-1)
        q_rope = apply_rope(q_rope, cos[:, :, None, :], sin[:, :, None, :])
        c_kv = rms_norm(ckv, kv_norm_w[i])
        k_rope = apply_rope(kr, cos, sin)
        q_lat = jnp.einsum('bshd,chd->bshc', q_nope.astype(jnp.float32), w_uk[i].astype(jnp.float32))
        out_lat = mla_attention(q_lat, q_rope, c_kv, k_rope)
        o_mla = jnp.einsum('bshc,chd->bshd', out_lat, w_uv[i].astype(jnp.float32))
        y_mla = o_mla.reshape(b, s, MLA_V_W).astype(h.dtype) @ w_br_mla[i]

        mixed = jax.nn.sigmoid(gate_dn) * y_dn + jax.nn.sigmoid(gate_mla) * y_mla
        h = layer_norm(DEEPNORM_ALPHA * h + mixed @ w_o[i], ln1_g[i], ln1_b[i])

        gt, up = jnp.split(h @ w_ffn_in[i], 2, axis=-1)
        ffn = (jax.nn.silu(gt) * up) @ w_ffn_out[i]
        ple = jax.nn.sigmoid(h @ w_ple_gate[i]) * (p[i] @ w_ple[i])
        h = layer_norm(DEEPNORM_ALPHA * h + ffn + ple, ln2_g[i], ln2_b[i])
    return h


import jax as _jax
import jax.numpy as _jnp

TWIN_FORMAT = 'train_step'
FWD_PARAMS = ['x', 'p', 'positions', 'w_in', 'conv_w', 'dn_a_log', 'dn_dt_bias', 'dn_norm_w', 'q_norm_w', 'w_uq', 'kv_norm_w', 'w_uk', 'w_uv', 'w_br_dn', 'w_br_mla', 'w_o', 'ln1_g', 'ln1_b', 'w_ffn_in', 'w_ffn_out', 'w_ple', 'w_ple_gate', 'ln2_g', 'ln2_b']
TWIN_WEIGHTS = ['w_in', 'conv_w', 'dn_a_log', 'dn_dt_bias', 'dn_norm_w', 'q_norm_w', 'w_uq', 'kv_norm_w', 'w_uk', 'w_uv', 'w_br_dn', 'w_br_mla', 'w_o', 'ln1_g', 'ln1_b', 'w_ffn_in', 'w_ffn_out', 'w_ple', 'w_ple_gate', 'ln2_g', 'ln2_b']
TWIN_DIFF_INPUT = 'x'
TWIN_INPUTS = ['x', 'p', 'positions', 'w_in', 'conv_w', 'dn_a_log', 'dn_dt_bias', 'dn_norm_w', 'q_norm_w', 'w_uq', 'kv_norm_w', 'w_uk', 'w_uv', 'w_br_dn', 'w_br_mla', 'w_o', 'ln1_g', 'ln1_b', 'w_ffn_in', 'w_ffn_out', 'w_ple', 'w_ple_gate', 'ln2_g', 'ln2_b', 'loss_target', 'm_w_in', 'm_conv_w', 'm_dn_a_log', 'm_dn_dt_bias', 'm_dn_norm_w', 'm_q_norm_w', 'm_w_uq', 'm_kv_norm_w', 'm_w_uk', 'm_w_uv', 'm_w_br_dn', 'm_w_br_mla', 'm_w_o', 'm_ln1_g', 'm_ln1_b', 'm_w_ffn_in', 'm_w_ffn_out', 'm_w_ple', 'm_w_ple_gate', 'm_ln2_g', 'm_ln2_b', 'v_w_in', 'v_conv_w', 'v_dn_a_log', 'v_dn_dt_bias', 'v_dn_norm_w', 'v_q_norm_w', 'v_w_uq', 'v_kv_norm_w', 'v_w_uk', 'v_w_uv', 'v_w_br_dn', 'v_w_br_mla', 'v_w_o', 'v_ln1_g', 'v_ln1_b', 'v_w_ffn_in', 'v_w_ffn_out', 'v_w_ple', 'v_w_ple_gate', 'v_ln2_g', 'v_ln2_b']
TWIN_OUTPUTS = ['loss', 'grad_x', 'grad_w_in', 'grad_conv_w', 'grad_dn_a_log', 'grad_dn_dt_bias', 'grad_dn_norm_w', 'grad_q_norm_w', 'grad_w_uq', 'grad_kv_norm_w', 'grad_w_uk', 'grad_w_uv', 'grad_w_br_dn', 'grad_w_br_mla', 'grad_w_o', 'grad_ln1_g', 'grad_ln1_b', 'grad_w_ffn_in', 'grad_w_ffn_out', 'grad_w_ple', 'grad_w_ple_gate', 'grad_ln2_g', 'grad_ln2_b', 'delta_w_in', 'delta_conv_w', 'delta_dn_a_log', 'delta_dn_dt_bias', 'delta_dn_norm_w', 'delta_q_norm_w', 'delta_w_uq', 'delta_kv_norm_w', 'delta_w_uk', 'delta_w_uv', 'delta_w_br_dn', 'delta_w_br_mla', 'delta_w_o', 'delta_ln1_g', 'delta_ln1_b', 'delta_w_ffn_in', 'delta_w_ffn_out', 'delta_w_ple', 'delta_w_ple_gate', 'delta_ln2_g', 'delta_ln2_b', 'new_m_w_in', 'new_m_conv_w', 'new_m_dn_a_log', 'new_m_dn_dt_bias', 'new_m_dn_norm_w', 'new_m_q_norm_w', 'new_m_w_uq', 'new_m_kv_norm_w', 'new_m_w_uk', 'new_m_w_uv', 'new_m_w_br_dn', 'new_m_w_br_mla', 'new_m_w_o', 'new_m_ln1_g', 'new_m_ln1_b', 'new_m_w_ffn_in', 'new_m_w_ffn_out', 'new_m_w_ple', 'new_m_w_ple_gate', 'new_m_ln2_g', 'new_m_ln2_b', 'new_v_w_in', 'new_v_conv_w', 'new_v_dn_a_log', 'new_v_dn_dt_bias', 'new_v_dn_norm_w', 'new_v_q_norm_w', 'new_v_w_uq', 'new_v_kv_norm_w', 'new_v_w_uk', 'new_v_w_uv', 'new_v_w_br_dn', 'new_v_w_br_mla', 'new_v_w_o', 'new_v_ln1_g', 'new_v_ln1_b', 'new_v_w_ffn_in', 'new_v_w_ffn_out', 'new_v_w_ple', 'new_v_w_ple_gate', 'new_v_ln2_g', 'new_v_ln2_b']
TWIN_LEAF_KINDS = {'loss': 'loss', 'grad_x': 'grad_x', 'grad_w_in': 'grad_w', 'grad_conv_w': 'grad_w', 'grad_dn_a_log': 'grad_w', 'grad_dn_dt_bias': 'grad_w', 'grad_dn_norm_w': 'grad_w', 'grad_q_norm_w': 'grad_w', 'grad_w_uq': 'grad_w', 'grad_kv_norm_w': 'grad_w', 'grad_w_uk': 'grad_w', 'grad_w_uv': 'grad_w', 'grad_w_br_dn': 'grad_w', 'grad_w_br_mla': 'grad_w', 'grad_w_o': 'grad_w', 'grad_ln1_g': 'grad_w', 'grad_ln1_b': 'grad_w', 'grad_w_ffn_in': 'grad_w', 'grad_w_ffn_out': 'grad_w', 'grad_w_ple': 'grad_w', 'grad_w_ple_gate': 'grad_w', 'grad_ln2_g': 'grad_w', 'grad_ln2_b': 'grad_w', 'delta_w_in': 'delta_w', 'delta_conv_w': 'delta_w', 'delta_dn_a_log': 'delta_w', 'delta_dn_dt_bias': 'delta_w', 'delta_dn_norm_w': 'delta_w', 'delta_q_norm_w': 'delta_w', 'delta_w_uq': 'delta_w', 'delta_kv_norm_w': 'delta_w', 'delta_w_uk': 'delta_w', 'delta_w_uv': 'delta_w', 'delta_w_br_dn': 'delta_w', 'delta_w_br_mla': 'delta_w', 'delta_w_o': 'delta_w', 'delta_ln1_g': 'delta_w', 'delta_ln1_b': 'delta_w', 'delta_w_ffn_in': 'delta_w', 'delta_w_ffn_out': 'delta_w', 'delta_w_ple': 'delta_w', 'delta_w_ple_gate': 'delta_w', 'delta_ln2_g': 'delta_w', 'delta_ln2_b': 'delta_w', 'new_m_w_in': 'new_m', 'new_m_conv_w': 'new_m', 'new_m_dn_a_log': 'new_m', 'new_m_dn_dt_bias': 'new_m', 'new_m_dn_norm_w': 'new_m', 'new_m_q_norm_w': 'new_m', 'new_m_w_uq': 'new_m', 'new_m_kv_norm_w': 'new_m', 'new_m_w_uk': 'new_m', 'new_m_w_uv': 'new_m', 'new_m_w_br_dn': 'new_m', 'new_m_w_br_mla': 'new_m', 'new_m_w_o': 'new_m', 'new_m_ln1_g': 'new_m', 'new_m_ln1_b': 'new_m', 'new_m_w_ffn_in': 'new_m', 'new_m_w_ffn_out': 'new_m', 'new_m_w_ple': 'new_m', 'new_m_w_ple_gate': 'new_m', 'new_m_ln2_g': 'new_m', 'new_m_ln2_b': 'new_m', 'new_v_w_in': 'new_v', 'new_v_conv_w': 'new_v', 'new_v_dn_a_log': 'new_v', 'new_v_dn_dt_bias': 'new_v', 'new_v_dn_norm_w': 'new_v', 'new_v_q_norm_w': 'new_v', 'new_v_w_uq': 'new_v', 'new_v_kv_norm_w': 'new_v', 'new_v_w_uk': 'new_v', 'new_v_w_uv': 'new_v', 'new_v_w_br_dn': 'new_v', 'new_v_w_br_mla': 'new_v', 'new_v_w_o': 'new_v', 'new_v_ln1_g': 'new_v', 'new_v_ln1_b': 'new_v', 'new_v_w_ffn_in': 'new_v', 'new_v_w_ffn_out': 'new_v', 'new_v_w_ple': 'new_v', 'new_v_w_ple_gate': 'new_v', 'new_v_ln2_g': 'new_v', 'new_v_ln2_b': 'new_v'}


def _forward(args):
    return _fwd_reference(*[args[k] for k in FWD_PARAMS])


def _output_shape():
    def fwd():
        inp = _fwd_setup_inputs(0)
        return _fwd_reference(*[inp[k] for k in FWD_PARAMS])
    out = _jax.eval_shape(fwd)
    return out.shape, out.dtype

N_MICROBATCH = 1
ADAM_LR = 0.001
ADAM_B1 = 0.9
ADAM_B2 = 0.999
ADAM_EPS = 1e-08
ADAM_WD = 0.01
ADAM_STEP = 10
PER_EXAMPLE_BATCH_AXIS = {'x': 0, 'p': 1, 'positions': 0, 'loss_target': 0}
SHARED_INPUTS = []
_WEIGHT_DTYPES = {'w_in': _jnp.float32, 'conv_w': _jnp.float32, 'dn_a_log': _jnp.float32, 'dn_dt_bias': _jnp.float32, 'dn_norm_w': _jnp.float32, 'q_norm_w': _jnp.float32, 'w_uq': _jnp.float32, 'kv_norm_w': _jnp.float32, 'w_uk': _jnp.float32, 'w_uv': _jnp.float32, 'w_br_dn': _jnp.float32, 'w_br_mla': _jnp.float32, 'w_o': _jnp.float32, 'ln1_g': _jnp.float32, 'ln1_b': _jnp.float32, 'w_ffn_in': _jnp.float32, 'w_ffn_out': _jnp.float32, 'w_ple': _jnp.float32, 'w_ple_gate': _jnp.float32, 'ln2_g': _jnp.float32, 'ln2_b': _jnp.float32}
MOMENT_SCALE = {'w_in': 1.979670e-02, 'conv_w': 2.231387e-02, 'dn_a_log': 2.421494e-01, 'dn_dt_bias': 2.346934e-01, 'dn_norm_w': 7.785857e-02, 'q_norm_w': 1.421055e-02, 'w_uq': 7.089752e-03, 'kv_norm_w': 2.715638e-02, 'w_uk': 7.216098e-03, 'w_uv': 1.060810e-02, 'w_br_dn': 2.815893e-02, 'w_br_mla': 1.061743e-02, 'w_o': 4.941029e-02, 'ln1_g': 4.774232e-01, 'ln1_b': 2.872173e-01, 'w_ffn_in': 3.001621e-02, 'w_ffn_out': 8.236101e-02, 'w_ple': 7.467033e-02, 'w_ple_gate': 1.730190e-02, 'ln2_g': 3.199438e+01, 'ln2_b': 8.394686e-01}


def _to_microbatches(a, axis):
    t = _jnp.moveaxis(a, axis, 0)
    t = t.reshape((N_MICROBATCH, t.shape[0] // N_MICROBATCH) + t.shape[1:])
    return _jnp.moveaxis(t, 1, axis + 1)


def setup_inputs(seed: int = 0) -> dict:
    inp = _fwd_setup_inputs(seed)
    key = _jax.random.fold_in(_jax.random.key(seed), 7919)
    shape, _ = _output_shape()
    out = dict(inp)
    out["loss_target"] = _jax.random.normal(_jax.random.fold_in(key, 0), shape, _jnp.float32)
    for i, name in enumerate(TWIN_WEIGHTS):
        w = inp[name].astype(_jnp.float32)
        if MOMENT_SCALE is None:
            s = _jnp.sqrt(_jnp.mean(_jnp.square(w)) + 1e-30)
        else:
            s = MOMENT_SCALE[name]
        km, kv = _jax.random.split(_jax.random.fold_in(key, i + 1))
        out[name] = w
        out["m_" + name] = s * _jax.random.normal(km, w.shape, _jnp.float32)
        out["v_" + name] = (s * s) * _jax.random.uniform(kv, w.shape, _jnp.float32, 0.5, 1.5)
    if N_MICROBATCH > 1:
        for name, axis in PER_EXAMPLE_BATCH_AXIS.items():
            out[name] = _to_microbatches(out[name], axis)
    return {'x': out['x'], 'p': out['p'], 'positions': out['positions'], 'w_in': out['w_in'], 'conv_w': out['conv_w'], 'dn_a_log': out['dn_a_log'], 'dn_dt_bias': out['dn_dt_bias'], 'dn_norm_w': out['dn_norm_w'], 'q_norm_w': out['q_norm_w'], 'w_uq': out['w_uq'], 'kv_norm_w': out['kv_norm_w'], 'w_uk': out['w_uk'], 'w_uv': out['w_uv'], 'w_br_dn': out['w_br_dn'], 'w_br_mla': out['w_br_mla'], 'w_o': out['w_o'], 'ln1_g': out['ln1_g'], 'ln1_b': out['ln1_b'], 'w_ffn_in': out['w_ffn_in'], 'w_ffn_out': out['w_ffn_out'], 'w_ple': out['w_ple'], 'w_ple_gate': out['w_ple_gate'], 'ln2_g': out['ln2_g'], 'ln2_b': out['ln2_b'], 'loss_target': out['loss_target'], 'm_w_in': out['m_w_in'], 'm_conv_w': out['m_conv_w'], 'm_dn_a_log': out['m_dn_a_log'], 'm_dn_dt_bias': out['m_dn_dt_bias'], 'm_dn_norm_w': out['m_dn_norm_w'], 'm_q_norm_w': out['m_q_norm_w'], 'm_w_uq': out['m_w_uq'], 'm_kv_norm_w': out['m_kv_norm_w'], 'm_w_uk': out['m_w_uk'], 'm_w_uv': out['m_w_uv'], 'm_w_br_dn': out['m_w_br_dn'], 'm_w_br_mla': out['m_w_br_mla'], 'm_w_o': out['m_w_o'], 'm_ln1_g': out['m_ln1_g'], 'm_ln1_b': out['m_ln1_b'], 'm_w_ffn_in': out['m_w_ffn_in'], 'm_w_ffn_out': out['m_w_ffn_out'], 'm_w_ple': out['m_w_ple'], 'm_w_ple_gate': out['m_w_ple_gate'], 'm_ln2_g': out['m_ln2_g'], 'm_ln2_b': out['m_ln2_b'], 'v_w_in': out['v_w_in'], 'v_conv_w': out['v_conv_w'], 'v_dn_a_log': out['v_dn_a_log'], 'v_dn_dt_bias': out['v_dn_dt_bias'], 'v_dn_norm_w': out['v_dn_norm_w'], 'v_q_norm_w': out['v_q_norm_w'], 'v_w_uq': out['v_w_uq'], 'v_kv_norm_w': out['v_kv_norm_w'], 'v_w_uk': out['v_w_uk'], 'v_w_uv': out['v_w_uv'], 'v_w_br_dn': out['v_w_br_dn'], 'v_w_br_mla': out['v_w_br_mla'], 'v_w_o': out['v_w_o'], 'v_ln1_g': out['v_ln1_g'], 'v_ln1_b': out['v_ln1_b'], 'v_w_ffn_in': out['v_w_ffn_in'], 'v_w_ffn_out': out['v_w_ffn_out'], 'v_w_ple': out['v_w_ple'], 'v_w_ple_gate': out['v_w_ple_gate'], 'v_ln2_g': out['v_ln2_g'], 'v_ln2_b': out['v_ln2_b']}


def _loss(weights, diff, rest, loss_target):
    with _jax.named_scope("forward"):
        args = {**rest, TWIN_DIFF_INPUT: diff, **{k: w.astype(_WEIGHT_DTYPES[k]) for k, w in weights.items()}}
        y = _forward(args)
    with _jax.named_scope("loss_head"):
        err = _jnp.square(y.astype(_jnp.float32) - loss_target)
        return 0.5 * _jnp.sum(_jnp.mean(err, axis=-1)) if err.ndim else 0.5 * err


def _adamw(w, g, m, v):
    m = ADAM_B1 * m + (1.0 - ADAM_B1) * g
    v = ADAM_B2 * v + (1.0 - ADAM_B2) * _jnp.square(g)
    m_hat = m / (1.0 - ADAM_B1 ** ADAM_STEP)
    v_hat = v / (1.0 - ADAM_B2 ** ADAM_STEP)
    delta = -ADAM_LR * (m_hat / (_jnp.sqrt(v_hat) + ADAM_EPS) + ADAM_WD * w)
    return delta, m, v


def reference(x, p, positions, w_in, conv_w, dn_a_log, dn_dt_bias, dn_norm_w, q_norm_w, w_uq, kv_norm_w, w_uk, w_uv, w_br_dn, w_br_mla, w_o, ln1_g, ln1_b, w_ffn_in, w_ffn_out, w_ple, w_ple_gate, ln2_g, ln2_b, loss_target, m_w_in, m_conv_w, m_dn_a_log, m_dn_dt_bias, m_dn_norm_w, m_q_norm_w, m_w_uq, m_kv_norm_w, m_w_uk, m_w_uv, m_w_br_dn, m_w_br_mla, m_w_o, m_ln1_g, m_ln1_b, m_w_ffn_in, m_w_ffn_out, m_w_ple, m_w_ple_gate, m_ln2_g, m_ln2_b, v_w_in, v_conv_w, v_dn_a_log, v_dn_dt_bias, v_dn_norm_w, v_q_norm_w, v_w_uq, v_kv_norm_w, v_w_uk, v_w_uv, v_w_br_dn, v_w_br_mla, v_w_o, v_ln1_g, v_ln1_b, v_w_ffn_in, v_w_ffn_out, v_w_ple, v_w_ple_gate, v_ln2_g, v_ln2_b):
    given = dict(x=x, p=p, positions=positions, w_in=w_in, conv_w=conv_w, dn_a_log=dn_a_log, dn_dt_bias=dn_dt_bias, dn_norm_w=dn_norm_w, q_norm_w=q_norm_w, w_uq=w_uq, kv_norm_w=kv_norm_w, w_uk=w_uk, w_uv=w_uv, w_br_dn=w_br_dn, w_br_mla=w_br_mla, w_o=w_o, ln1_g=ln1_g, ln1_b=ln1_b, w_ffn_in=w_ffn_in, w_ffn_out=w_ffn_out, w_ple=w_ple, w_ple_gate=w_ple_gate, ln2_g=ln2_g, ln2_b=ln2_b, loss_target=loss_target, m_w_in=m_w_in, m_conv_w=m_conv_w, m_dn_a_log=m_dn_a_log, m_dn_dt_bias=m_dn_dt_bias, m_dn_norm_w=m_dn_norm_w, m_q_norm_w=m_q_norm_w, m_w_uq=m_w_uq, m_kv_norm_w=m_kv_norm_w, m_w_uk=m_w_uk, m_w_uv=m_w_uv, m_w_br_dn=m_w_br_dn, m_w_br_mla=m_w_br_mla, m_w_o=m_w_o, m_ln1_g=m_ln1_g, m_ln1_b=m_ln1_b, m_w_ffn_in=m_w_ffn_in, m_w_ffn_out=m_w_ffn_out, m_w_ple=m_w_ple, m_w_ple_gate=m_w_ple_gate, m_ln2_g=m_ln2_g, m_ln2_b=m_ln2_b, v_w_in=v_w_in, v_conv_w=v_conv_w, v_dn_a_log=v_dn_a_log, v_dn_dt_bias=v_dn_dt_bias, v_dn_norm_w=v_dn_norm_w, v_q_norm_w=v_q_norm_w, v_w_uq=v_w_uq, v_kv_norm_w=v_kv_norm_w, v_w_uk=v_w_uk, v_w_uv=v_w_uv, v_w_br_dn=v_w_br_dn, v_w_br_mla=v_w_br_mla, v_w_o=v_w_o, v_ln1_g=v_ln1_g, v_ln1_b=v_ln1_b, v_w_ffn_in=v_w_ffn_in, v_w_ffn_out=v_w_ffn_out, v_w_ple=v_w_ple, v_w_ple_gate=v_w_ple_gate, v_ln2_g=v_ln2_g, v_ln2_b=v_ln2_b)
    weights = {n: given[n] for n in TWIN_WEIGHTS}
    shared = {n: given[n] for n in SHARED_INPUTS}
    per_example = {n: given[n] for n in ['x', 'p', 'positions']}
    grad_fn = _jax.value_and_grad(_loss, argnums=(0, 1))

    def one_microbatch(ex, loss_target):
        ex = dict(ex)
        diff = ex.pop(TWIN_DIFF_INPUT)
        return grad_fn(weights, diff, {**shared, **ex}, loss_target)

    if N_MICROBATCH == 1:
        loss, (grad_w, grad_x) = one_microbatch(per_example, given["loss_target"])
    else:
        def body(carry, xs):
            loss_sum, grad_sum = carry
            l_k, (gw_k, gx_k) = one_microbatch(xs[0], xs[1])
            with _jax.named_scope("update"):
                return (loss_sum + l_k, _jax.tree.map(_jnp.add, grad_sum, gw_k)), gx_k

        init = (_jnp.zeros((), _jnp.float32), _jax.tree.map(_jnp.zeros_like, weights))
        (loss, grad_w), grad_x = _jax.lax.scan(body, init, (per_example, given["loss_target"]))
    with _jax.named_scope("update"):
        delta_w, new_m, new_v = {}, {}, {}
        for n in TWIN_WEIGHTS:
            delta_w[n], new_m[n], new_v[n] = _adamw(weights[n], grad_w[n], given["m_" + n], given["v_" + n])
    return (loss, grad_x, *[grad_w[n] for n in TWIN_WEIGHTS], *[delta_w[n] for n in TWIN_WEIGHTS],
            *[new_m[n] for n in TWIN_WEIGHTS], *[new_v[n] for n in TWIN_WEIGHTS])
```

```python
import functools
import math

import jax
import jax.numpy as jnp
from jax import lax
from jax.experimental import pallas as pl
from jax.experimental.pallas import tpu as pltpu

F32 = jnp.float32
_CDT = jnp.bfloat16
_HI = lax.Precision.HIGHEST
_MESH = pl.DeviceIdType.MESH

D_MODEL = 1024
PLE_DIM = 256
HEADS = 8
DN_DK = 128
DN_CHUNK = 64
DN_CONV = 4
QKV_W = 3 * HEADS * DN_DK
Q_LORA = 384
KV_LORA = 256
NOPE = 128
ROPE = 64
ROPE_PAD = 128
FFN_HIDDEN = 2816
D_IN = 6864
ROPE_BASE = 10000.0
ALPHA = 2.0 ** 0.25
ATT_SCALE = (NOPE + ROPE) ** -0.5
NEG_BIG = -1e30
ADAM_LR, ADAM_B1, ADAM_B2, ADAM_EPS, ADAM_WD, ADAM_STEP = 0.001, 0.9, 0.999, 1e-08, 0.01, 10

LANE = 128
VMEM_LIMIT = 56 * 1024 * 1024
N_SHARD = 4
FLAT_W = 1024
FLAT_HALF = 2688
FLAT_ROWS = 2 * FLAT_HALF
SMALL_ROWS = 24


def _tile(dim, cap):
    if dim <= cap:
        return dim
    t = (cap // LANE) * LANE
    while t >= LANE:
        if dim % t == 0:
            return t
        t -= LANE
    return dim


def _cparams(sem):
    return pltpu.CompilerParams(dimension_semantics=sem, vmem_limit_bytes=VMEM_LIMIT)


def _mm(a, b, *, name, ta=False, tb=False, add=None, add_scale=1.0, out_dtype=F32, heads=None,
        a_head=None, b_head=None, out_head=None, dims=None, tm=512, tn=512, tk=1024):
    m, n, k = dims
    tm, tn, tk = _tile(m, tm), _tile(n, tn), _tile(k, tk)
    nk = k // tk
    hgrid = () if heads is None else (heads,)
    off = len(hgrid)

    def spec(rows, cols, rtile, ctile, rsel, csel, layout):
        def idx(*g):
            h = g[0] if off else 0
            ri, ci = g[off + rsel], g[off + csel]
            if layout == 'lead':
                return (h, ri, ci)
            if layout == 'col':
                return (ri, h * (cols // ctile) + ci)
            return (ri, ci)
        if layout == 'lead':
            return pl.BlockSpec((None, rtile, ctile), idx)
        return pl.BlockSpec((rtile, ctile), idx)

    a_spec = spec(k, m, tk, tm, 2, 0, a_head) if ta else spec(m, k, tm, tk, 0, 2, a_head)
    b_spec = spec(n, k, tn, tk, 1, 2, b_head) if tb else spec(k, n, tk, tn, 2, 1, b_head)
    o_spec = spec(m, n, tm, tn, 0, 1, out_head)
    in_specs = [a_spec, b_spec]
    args = [a, b]
    if add is not None:
        in_specs.append(spec(m, n, tm, tn, 0, 1, out_head))
        args.append(add)
    dn = (((0 if ta else 1,), (1 if tb else 0,)), ((), ()))

    def body(*refs):
        a_ref, b_ref = refs[0], refs[1]
        o_ref, acc_ref = refs[-2], refs[-1]
        kk = pl.program_id(off + 2)

        @pl.when(kk == 0)
        def _():
            if add is not None:
                acc_ref[...] = refs[2][...].astype(F32) * add_scale
            else:
                acc_ref[...] = jnp.zeros_like(acc_ref)

        acc_ref[...] += lax.dot_general(a_ref[...].astype(_CDT), b_ref[...].astype(_CDT), dn,
                                        preferred_element_type=F32)

        @pl.when(kk == nk - 1)
        def _():
            o_ref[...] = acc_ref[...].astype(out_dtype)

    if out_head == 'lead':
        oshape = (heads, m, n)
    elif out_head == 'col':
        oshape = (m, heads * n)
    else:
        oshape = (m, n)
    sem = ("parallel",) * (off + 2) + ("arbitrary",)
    return pl.pallas_call(
        body, name=name, grid=hgrid + (m // tm, n // tn, nk), in_specs=in_specs, out_specs=o_spec,
        out_shape=jax.ShapeDtypeStruct(oshape, out_dtype), scratch_shapes=[pltpu.VMEM((tm, tn), F32)],
        compiler_params=_cparams(sem))(*args)


def _mm2(a, b, **kw):
    ta, tb = kw.get('ta', False), kw.get('tb', False)
    m = a.shape[1] if ta else a.shape[0]
    k = a.shape[0] if ta else a.shape[1]
    n = b.shape[0] if tb else b.shape[1]
    return _mm(a, b, dims=(m, n, k), **kw)


def _rowwise(fn, rows, bcast, outs, reds=(), *, name, tm=256, heads=None):
    t = rows[0][0].shape[0]
    tm = min(tm, t)
    hn = 1 if heads is None else heads
    in_specs, args = [], []
    for arr, width, base, per_head in rows:
        in_specs.append(pl.BlockSpec((tm, width), functools.partial(
            lambda i, h, base, per_head: (i, base + (h if per_head else 0)), base=base, per_head=per_head)))
        args.append(arr)
    for arr in bcast:
        in_specs.append(pl.BlockSpec(arr.shape, lambda i, h: (0, 0)))
        args.append(arr)
    out_specs, out_shape = [], []
    for total, width, per_head, dt in outs:
        out_specs.append(pl.BlockSpec((tm, width), functools.partial(
            lambda i, h, per_head: (i, h if per_head else 0), per_head=per_head)))
        out_shape.append(jax.ShapeDtypeStruct((t, total), dt))
    for shp in reds:
        out_specs.append(pl.BlockSpec(shp, lambda i, h: (0, 0)))
        out_shape.append(jax.ShapeDtypeStruct(shp, F32))
    n_in, n_out, n_red = len(args), len(outs), len(reds)

    def body(*refs):
        i, h = pl.program_id(0), pl.program_id(1)
        vals = fn(h, *[r[...] for r in refs[:n_in]])
        for r, v in zip(refs[n_in:n_in + n_out], vals[:n_out]):
            r[...] = v.astype(r.dtype)
        if n_red:
            @pl.when((i == 0) & (h == 0))
            def _():
                for r in refs[n_in + n_out:]:
                    r[...] = jnp.zeros_like(r)
            for r, v in zip(refs[n_in + n_out:], vals[n_out:]):
                r[...] += v

    sem = ("arbitrary", "arbitrary") if n_red else ("parallel", "parallel")
    res = pl.pallas_call(body, name=name, grid=(t // tm, hn), in_specs=in_specs, out_specs=out_specs,
                         out_shape=out_shape, compiler_params=_cparams(sem))(*args)
    return tuple(res)


def _sigmoid(x):
    return 1.0 / (1.0 + jnp.exp(-x))


def _silu(x):
    return x * _sigmoid(x)


def _softplus(x):
    return jnp.maximum(x, 0.0) + jnp.log(1.0 + jnp.exp(-jnp.abs(x)))


def _layer_norm(t, g, b):
    mu = jnp.mean(t, axis=-1, keepdims=True)
    d = t - mu
    var = jnp.mean(d * d, axis=-1, keepdims=True)
    return d * lax.rsqrt(var + 1e-5) * g + b


def _rms_norm(t, w):
    return t * lax.rsqrt(jnp.mean(t * t, axis=-1, keepdims=True) + 1e-6) * w


def _swap_rope_halves(t):
    lane = lax.broadcasted_iota(jnp.int32, t.shape, 1) % ROPE_PAD
    n = t.shape[1]
    up = pltpu.roll(t, n - ROPE // 2, axis=1)
    dn = pltpu.roll(t, ROPE // 2, axis=1)
    return jnp.where(lane < ROPE // 2, up, jnp.where(lane < ROPE, dn, 0.0))


def _rope(t, cosb, sinb):
    reps = t.shape[1] // ROPE_PAD
    c = jnp.tile(cosb, (1, reps)) if reps > 1 else cosb
    s = jnp.tile(sinb, (1, reps)) if reps > 1 else sinb
    return t * c + _swap_rope_halves(t) * s


def _rope_bwd(d, cosb, sinb):
    reps = d.shape[1] // ROPE_PAD
    c = jnp.tile(cosb, (1, reps)) if reps > 1 else cosb
    s = jnp.tile(sinb, (1, reps)) if reps > 1 else sinb
    return d * c + _swap_rope_halves(d * s)


_CONV_ROWS = 256
_CONV_COLS = 256


def _conv_window(ref, r0, lo, hi, t):
    parts = []
    start, stop = r0 - lo, r0 + _CONV_ROWS + hi
    if start < 0:
        parts.append(jnp.zeros((-start, ref.shape[1]), F32))
        start = 0
    tail = max(stop - t, 0)
    parts.append(ref[start:stop - tail, :].astype(F32))
    if tail:
        parts.append(jnp.zeros((tail, ref.shape[1]), F32))
    return parts[0] if len(parts) == 1 else jnp.concatenate(parts, axis=0)


def _conv_taps(win, w_ref, n_out):
    acc = win[8:8 + n_out] * w_ref[DN_CONV - 1:DN_CONV, :]
    for i in range(DN_CONV - 1):
        acc = acc + pltpu.roll(win, DN_CONV - 1 - i, axis=0)[8:8 + n_out] * w_ref[i:i + 1, :]
    return acc


def _conv_silu(x, w):
    t, ch = x.shape

    def body(x_ref, w_ref, o_ref):
        for r in range(t // _CONV_ROWS):
            r0 = r * _CONV_ROWS
            c = _conv_taps(_conv_window(x_ref, r0, 8, 0, t), w_ref, _CONV_ROWS)
            o_ref[r0:r0 + _CONV_ROWS, :] = _silu(c)

    return pl.pallas_call(
        body, name="conv_silu", grid=(ch // _CONV_COLS,),
        in_specs=[pl.BlockSpec((t, _CONV_COLS), lambda j: (0, j)), pl.BlockSpec((DN_CONV, _CONV_COLS), lambda j: (0, j))],
        out_specs=pl.BlockSpec((t, _CONV_COLS), lambda j: (0, j)),
        out_shape=jax.ShapeDtypeStruct((t, ch), F32), compiler_params=_cparams(("parallel",)))(x, w)


def _conv_silu_bwd(x, w, dy):
    t, ch = x.shape

    def body(x_ref, w_ref, dy_ref, dx_ref, dw_ref):
        dws = [jnp.zeros((1, _CONV_COLS), F32) for _ in range(DN_CONV)]
        for r in range(t // _CONV_ROWS):
            r0 = r * _CONV_ROWS
            n_ext = _CONV_ROWS + 8
            xw = _conv_window(x_ref, r0, 8, 8, t)
            c = _conv_taps(xw, w_ref, n_ext)
            sg = _sigmoid(c)
            ds = _conv_window(dy_ref, r0, 0, 8, t) * (sg * (1.0 + c * (1.0 - sg)))
            dx = ds[:_CONV_ROWS] * w_ref[DN_CONV - 1:DN_CONV, :]
            for i in range(DN_CONV - 1):
                sh = DN_CONV - 1 - i
                dx = dx + pltpu.roll(ds, n_ext - sh, axis=0)[:_CONV_ROWS] * w_ref[i:i + 1, :]
            dx_ref[r0:r0 + _CONV_ROWS, :] = dx.astype(dx_ref.dtype)
            ds0 = ds[:_CONV_ROWS]
            for i in range(DN_CONV):
                sh = DN_CONV - 1 - i
                xs = xw if sh == 0 else pltpu.roll(xw, sh, axis=0)
                dws[i] = dws[i] + jnp.sum(ds0 * xs[8:8 + _CONV_ROWS], axis=0, keepdims=True)
        for i in range(DN_CONV):
            dw_ref[i:i + 1, :] = dws[i]

    blk = pl.BlockSpec((t, _CONV_COLS), lambda j: (0, j))
    wblk = pl.BlockSpec((DN_CONV, _CONV_COLS), lambda j: (0, j))
    return pl.pallas_call(
        body, name="conv_silu_bwd", grid=(ch // _CONV_COLS,), in_specs=[blk, wblk, blk], out_specs=[blk, wblk],
        out_shape=[jax.ShapeDtypeStruct((t, ch), _CDT), jax.ShapeDtypeStruct((DN_CONV, ch), F32)],
        compiler_params=_cparams(("parallel",)))(x, w, dy)


_PA_ROWS = 512


def _bmm(a, b, spec, exact=False):
    if exact:
        return jnp.einsum(spec, a, b, precision=_HI, preferred_element_type=F32)
    return jnp.einsum(spec, a.astype(_CDT), b.astype(_CDT), preferred_element_type=F32)


def _phase_a(h, q, k, v, ba, alog, dtb):
    r = q.shape[0]
    nb = r // DN_CHUNK
    c = DN_CHUNK
    lane = lax.broadcasted_iota(jnp.int32, (1, LANE), 1)
    selb = (lane == h).astype(F32)
    sela = (lane == h + HEADS).astype(F32)
    b_raw = jnp.sum(ba * selb, axis=1, keepdims=True)
    a_raw = jnp.sum(ba * sela, axis=1, keepdims=True)
    al = jnp.sum(alog * selb, axis=1, keepdims=True)
    dt = jnp.sum(dtb * selb, axis=1, keepdims=True)
    beta = jnp.broadcast_to(_sigmoid(b_raw), (r, LANE))
    g = jnp.broadcast_to(-jnp.exp(al) * _softplus(a_raw + dt), (r, LANE))
    qn = q * lax.rsqrt(jnp.sum(q * q, -1, keepdims=True) + 1e-6) * (DN_DK ** -0.5)
    kn = k * lax.rsqrt(jnp.sum(k * k, -1, keepdims=True) + 1e-6)
    q3, k3, v3 = qn.reshape(nb, c, LANE), kn.reshape(nb, c, LANE), v.reshape(nb, c, LANE)
    b3, g3 = beta.reshape(nb, c, LANE), g.reshape(nb, c, LANE)
    ri = lax.broadcasted_iota(jnp.int32, (nb, c, c), 1)
    ci = lax.broadcasted_iota(jnp.int32, (nb, c, c), 2)
    tril, strict = ri >= ci, ri > ci
    gc = _bmm(tril.astype(F32), g3, 'bij,bjd->bid', exact=True)
    onehot = (lax.broadcasted_iota(jnp.int32, (nb, c, LANE), 2) == 0).astype(F32)
    g_row = _bmm(onehot, gc, 'bid,bjd->bij', exact=True)
    diff = gc[:, :, :c] - g_row
    decay = jnp.where(tril, jnp.exp(jnp.where(tril, diff, 0.0)), 0.0)
    kb = k3 * b3
    l_mat = jnp.where(strict, _bmm(kb, k3, 'bid,bjd->bij') * decay, 0.0)
    pw = -l_mat
    t_inv = (ri == ci).astype(F32) + pw
    for _ in range(5):
        pw = _bmm(pw, pw, 'bij,bjk->bik', exact=True)
        t_inv = t_inv + _bmm(t_inv, pw, 'bij,bjk->bik', exact=True)
    eg = jnp.exp(gc)
    u = _bmm(t_inv, v3 * b3, 'bij,bje->bie')
    w = _bmm(t_inv, kb * eg, 'bij,bje->bie')
    intra = jnp.where(tril, _bmm(q3, k3, 'bid,bjd->bij') * decay, 0.0)
    qd = q3 * eg
    gl = jnp.sum(g3, axis=1, keepdims=True)
    kt = k3 * jnp.exp(gl - gc)
    return (u.reshape(r, LANE), w.reshape(r, LANE), qd.reshape(r, LANE), kt.reshape(r, LANE),
            intra.reshape(r, c), gl.reshape(nb, LANE))


def _pa_specs(t):
    rr = min(_PA_ROWS, t)
    nb = rr // DN_CHUNK
    qkv = [pl.BlockSpec((rr, LANE), functools.partial(lambda i, h, o: (i, o + h), o=o)) for o in (0, HEADS, 2 * HEADS)]
    ba = pl.BlockSpec((rr, LANE), lambda i, h: (i, 3))
    vec = pl.BlockSpec((1, LANE), lambda i, h: (0, 0))
    row = pl.BlockSpec((rr, LANE), lambda i, h: (i, h))
    intra = pl.BlockSpec((None, rr, DN_CHUNK), lambda i, h: (h, i, 0))
    gl = pl.BlockSpec((nb, LANE), lambda i, h: (i, h))
    return rr, qkv, ba, vec, row, intra, gl


def _delta_local(qkv_act, pm, alog, dtb):
    t = qkv_act.shape[0]
    rr, qkv, ba, vec, row, intra, gl = _pa_specs(t)

    def body(q, k, v, b, al, dt, *outs):
        vals = _phase_a(pl.program_id(1), q[...], k[...], v[...], b[...], al[...], dt[...])
        for o, val in zip(outs, vals):
            o[...] = val

    wide = jax.ShapeDtypeStruct((t, HEADS * LANE), F32)
    return pl.pallas_call(
        body, name="delta_local", grid=(t // rr, HEADS), in_specs=qkv + [ba, vec, vec],
        out_specs=[row] * 4 + [intra, gl],
        out_shape=[wide] * 4 + [jax.ShapeDtypeStruct((HEADS, t, DN_CHUNK), F32),
                                jax.ShapeDtypeStruct((t // DN_CHUNK, HEADS * LANE), F32)],
        compiler_params=_cparams(("parallel", "parallel")))(qkv_act, qkv_act, qkv_act, pm, alog, dtb)


def _delta_local_bwd(qkv_act, pm, alog, dtb, du, dw, dqd, dkt, dintra, dgl):
    t = qkv_act.shape[0]
    rr, qkv, ba, vec, row, intra, gl = _pa_specs(t)

    def body(q, k, v, b, al, dt, du_r, dw_r, dqd_r, dkt_r, di_r, dgl_r, dq_o, dk_o, dv_o, dba_o, dal_o, ddt_o):
        i, h = pl.program_id(0), pl.program_id(1)
        _, vjp = jax.vjp(lambda *a: _phase_a(h, *a), q[...], k[...], v[...], b[...], al[...], dt[...])
        dq, dk, dv, dba, dal, ddt = vjp((du_r[...], dw_r[...], dqd_r[...], dkt_r[...], di_r[...], dgl_r[...]))
        dq_o[...], dk_o[...], dv_o[...] = dq, dk, dv

        @pl.when(h == 0)
        def _():
            dba_o[...] = jnp.zeros_like(dba_o)

        @pl.when((h == 0) & (i == 0))
        def _():
            dal_o[...] = jnp.zeros_like(dal_o)
            ddt_o[...] = jnp.zeros_like(ddt_o)

        dba_o[...] += dba
        dal_o[...] += dal
        ddt_o[...] += ddt

    wide = jax.ShapeDtypeStruct((t, HEADS * LANE), F32)
    vshape = jax.ShapeDtypeStruct((1, LANE), F32)
    return pl.pallas_call(
        body, name="delta_local_bwd", grid=(t // rr, HEADS),
        in_specs=qkv + [ba, vec, vec] + [row] * 4 + [intra, gl],
        out_specs=[row] * 3 + [pl.BlockSpec((rr, LANE), lambda i, h: (i, 0)), vec, vec],
        out_shape=[wide] * 3 + [jax.ShapeDtypeStruct((t, LANE), F32), vshape, vshape],
        compiler_params=_cparams(("arbitrary", "arbitrary")))(
            qkv_act, qkv_act, qkv_act, pm, alog, dtb, du, dw, dqd, dkt, dintra, dgl)


_SCAN_ROWS = 512


def _dot(a, b, dn):
    return lax.dot_general(a.astype(_CDT), b.astype(_CDT), (dn, ((), ())), preferred_element_type=F32)


_NN = ((1,), (0,))
_NT = ((1,), (1,))
_TN = ((0,), (0,))


def _delta_scan(u, w, qd, kt, intra, gl):
    t = u.shape[0]
    rr = min(_SCAN_ROWS, t)
    nc = rr // DN_CHUNK

    def body(u_ref, w_ref, qd_ref, kt_ref, a_ref, gl_ref, o_ref, sall_ref, s_scr):
        @pl.when(pl.program_id(0) == 0)
        def _():
            s_scr[...] = jnp.zeros_like(s_scr)

        def chunk(c, carry):
            r0 = pl.multiple_of(c * DN_CHUNK, DN_CHUNK)
            rows = pl.ds(r0, DN_CHUNK)
            e = jnp.exp(gl_ref[pl.ds(c, 1), :])
            for h in range(HEADS):
                cs = slice(h * LANE, (h + 1) * LANE)
                s = s_scr[h]
                sall_ref[c, h] = s
                v_new = u_ref[rows, cs] - _dot(w_ref[rows, cs], s, _NN)
                o_ref[rows, cs] = _dot(qd_ref[rows, cs], s, _NN) + _dot(a_ref[h, rows, :], v_new, _NN)
                s_scr[h] = s * e[:, cs] + _dot(kt_ref[rows, cs], v_new, _TN)
            return carry

        lax.fori_loop(0, nc, chunk, 0)

    row = pl.BlockSpec((rr, HEADS * LANE), lambda i: (i, 0))
    return pl.pallas_call(
        body, name="delta_scan", grid=(t // rr,),
        in_specs=[row] * 4 + [pl.BlockSpec((HEADS, rr, DN_CHUNK), lambda i: (0, i, 0)),
                              pl.BlockSpec((nc, HEADS * LANE), lambda i: (i, 0))],
        out_specs=[row, pl.BlockSpec((nc, HEADS, LANE, LANE), lambda i: (i, 0, 0, 0))],
        out_shape=[jax.ShapeDtypeStruct((t, HEADS * LANE), F32),
                   jax.ShapeDtypeStruct((t // DN_CHUNK, HEADS, LANE, LANE), F32)],
        scratch_shapes=[pltpu.VMEM((HEADS, LANE, LANE), F32)],
        compiler_params=_cparams(("arbitrary",)))(u, w, qd, kt, intra, gl)


def _delta_scan_bwd(u, w, qd, kt, intra, gl, sall, do):
    t = u.shape[0]
    rr = min(_SCAN_ROWS, t)
    nc = rr // DN_CHUNK
    ng = t // rr

    def body(u_ref, w_ref, qd_ref, kt_ref, a_ref, gl_ref, sall_ref, do_ref,
             du_ref, dw_ref, dqd_ref, dkt_ref, da_ref, dgl_ref, ds_scr):
        @pl.when(pl.program_id(0) == 0)
        def _():
            ds_scr[...] = jnp.zeros_like(ds_scr)

        def chunk(cc, carry):
            c = nc - 1 - cc
            r0 = pl.multiple_of(c * DN_CHUNK, DN_CHUNK)
            rows = pl.ds(r0, DN_CHUNK)
            e = jnp.exp(gl_ref[pl.ds(c, 1), :])
            for h in range(HEADS):
                cs = slice(h * LANE, (h + 1) * LANE)
                s = sall_ref[c, h]
                ds_out = ds_scr[h]
                w_c, kt_c, qd_c, do_c = w_ref[rows, cs], kt_ref[rows, cs], qd_ref[rows, cs], do_ref[rows, cs]
                a_c = a_ref[h, rows, :]
                v_new = u_ref[rows, cs] - _dot(w_c, s, _NN)
                dv_new = _dot(a_c, do_c, _TN) + _dot(kt_c, ds_out, _NN)
                da_ref[h, rows, :] = _dot(do_c, v_new, _NT)
                dqd_ref[rows, cs] = _dot(do_c, s, _NT)
                dkt_ref[rows, cs] = _dot(v_new, ds_out, _NT)
                du_ref[rows, cs] = dv_new
                dw_ref[rows, cs] = -_dot(dv_new, s, _NT)
                eh = e[:, cs]
                dgl_ref[pl.ds(pl.multiple_of(c * 8, 8), 8), cs] = jnp.broadcast_to(
                    jnp.sum(ds_out * s, axis=0, keepdims=True) * eh, (8, LANE))
                ds_scr[h] = ds_out * eh + _dot(qd_c, do_c, _TN) - _dot(w_c, dv_new, _TN)
            return carry

        lax.fori_loop(0, nc, chunk, 0)

    rev = lambda i: (ng - 1 - i, 0)
    row = pl.BlockSpec((rr, HEADS * LANE), rev)
    a_spec = pl.BlockSpec((HEADS, rr, DN_CHUNK), lambda i: (0, ng - 1 - i, 0))
    gl_spec = pl.BlockSpec((nc, HEADS * LANE), rev)
    wide = jax.ShapeDtypeStruct((t, HEADS * LANE), F32)
    outs = pl.pallas_call(
        body, name="delta_scan_bwd", grid=(ng,),
        in_specs=[row] * 4 + [a_spec, gl_spec, pl.BlockSpec((nc, HEADS, LANE, LANE), lambda i: (ng - 1 - i, 0, 0, 0)), row],
        out_specs=[row] * 4 + [a_spec, pl.BlockSpec((nc * 8, HEADS * LANE), rev)],
        out_shape=[wide] * 4 + [jax.ShapeDtypeStruct((HEADS, t, DN_CHUNK), F32),
                                jax.ShapeDtypeStruct((t // DN_CHUNK * 8, HEADS * LANE), F32)],
        scratch_shapes=[pltpu.VMEM((HEADS, LANE, LANE), F32)],
        compiler_params=_cparams(("arbitrary",)))(u, w, qd, kt, intra, gl, sall, do)
    return tuple(outs[:5]) + (outs[5].reshape(t // DN_CHUNK, 8, HEADS * LANE)[:, 0, :],)


_ATT_TILE = 512


def _att_scores(ql, qr, ckv_ref, kr_ref, j, tk, masked, q0):
    ks = pl.ds(pl.multiple_of(j * tk, tk), tk)
    s = (_dot(ql, ckv_ref[ks, :], _NT) + _dot(qr, kr_ref[ks, :], _NT)) * ATT_SCALE
    if masked:
        qi = q0 + lax.broadcasted_iota(jnp.int32, s.shape, 0)
        ki = j * tk + lax.broadcasted_iota(jnp.int32, s.shape, 1)
        s = jnp.where(ki <= qi, s, NEG_BIG)
    return s, ks


def _attention(ql, qr, ckv, kr):
    t = ckv.shape[0]
    tq = min(_ATT_TILE, t)

    def body(ql_ref, qr_ref, ckv_ref, kr_ref, o_ref, lse_ref, m_scr, l_scr, acc_scr):
        qi = pl.program_id(1)
        q_lat, q_rope = ql_ref[...], qr_ref[...]
        m_scr[...] = jnp.full_like(m_scr, NEG_BIG)
        l_scr[...] = jnp.zeros_like(l_scr)
        acc_scr[...] = jnp.zeros_like(acc_scr)

        def step(j, masked):
            s, ks = _att_scores(q_lat, q_rope, ckv_ref, kr_ref, j, tq, masked, qi * tq)
            m_old = m_scr[...]
            m_new = jnp.maximum(m_old, jnp.max(s, axis=-1, keepdims=True))
            p = jnp.exp(s - m_new)
            alpha = jnp.exp(m_old - m_new)
            l_scr[...] = l_scr[...] * alpha + jnp.sum(p, axis=-1, keepdims=True)
            acc_scr[...] = acc_scr[...] * alpha + _dot(p, ckv_ref[ks, :], _NN)
            m_scr[...] = m_new

        def loop_body(j, carry):
            step(j, False)
            return carry

        lax.fori_loop(0, qi, loop_body, 0)
        step(qi, True)
        o_ref[...] = acc_scr[...] / l_scr[...]
        lse_ref[...] = m_scr[...] + jnp.log(l_scr[...])

    return pl.pallas_call(
        body, name="attention", grid=(HEADS, t // tq),
        in_specs=[pl.BlockSpec((None, tq, KV_LORA), lambda h, i: (h, i, 0)),
                  pl.BlockSpec((tq, ROPE_PAD), lambda h, i: (i, h)),
                  pl.BlockSpec((t, KV_LORA), lambda h, i: (0, 0)),
                  pl.BlockSpec((t, ROPE_PAD), lambda h, i: (0, 0))],
        out_specs=[pl.BlockSpec((None, tq, KV_LORA), lambda h, i: (h, i, 0)),
                   pl.BlockSpec((None, tq, 1), lambda h, i: (h, i, 0))],
        out_shape=[jax.ShapeDtypeStruct((HEADS, t, KV_LORA), F32), jax.ShapeDtypeStruct((HEADS, t, 1), F32)],
        scratch_shapes=[pltpu.VMEM((tq, 1), F32), pltpu.VMEM((tq, 1), F32), pltpu.VMEM((tq, KV_LORA), F32)],
        compiler_params=_cparams(("parallel", "parallel")))(ql, qr, ckv, kr)


def _attention_bwd(ql, qr, ckv, kr, out, lse, dout):
    t = ckv.shape[0]
    tq = min(_ATT_TILE, t)

    def body(ql_ref, qr_ref, ckv_ref, kr_ref, o_ref, lse_ref, do_ref, dql_ref, dqr_ref, dckv_ref, dkr_ref,
             dql_scr, dqr_scr):
        h, qi = pl.program_id(0), pl.program_id(1)

        @pl.when((h == 0) & (qi == 0))
        def _():
            dckv_ref[...] = jnp.zeros_like(dckv_ref)
            dkr_ref[...] = jnp.zeros_like(dkr_ref)

        q_lat, q_rope, d_o = ql_ref[...], qr_ref[...], do_ref[...]
        lse_v = lse_ref[...]
        dsum = jnp.sum(d_o * o_ref[...], axis=-1, keepdims=True)
        dql_scr[...] = jnp.zeros_like(dql_scr)
        dqr_scr[...] = jnp.zeros_like(dqr_scr)

        def step(j, masked):
            s, ks = _att_scores(q_lat, q_rope, ckv_ref, kr_ref, j, tq, masked, qi * tq)
            p = jnp.exp(s - lse_v)
            kv = ckv_ref[ks, :]
            dp = _dot(d_o, kv, _NT)
            ds = p * (dp - dsum) * ATT_SCALE
            dql_scr[...] += _dot(ds, kv, _NN)
            dqr_scr[...] += _dot(ds, kr_ref[ks, :], _NN)
            dckv_ref[ks, :] += _dot(p, d_o, _TN) + _dot(ds, q_lat, _TN)
            dkr_ref[ks, :] += _dot(ds, q_rope, _TN)

        def loop_body(j, carry):
            step(j, False)
            return carry

        lax.fori_loop(0, qi, loop_body, 0)
        step(qi, True)
        dql_ref[...] = dql_scr[...].astype(dql_ref.dtype)
        dqr_ref[...] = dqr_scr[...]

    lat = pl.BlockSpec((None, tq, KV_LORA), lambda h, i: (h, i, 0))
    rope = pl.BlockSpec((tq, ROPE_PAD), lambda h, i: (i, h))
    kfull = pl.BlockSpec((t, KV_LORA), lambda h, i: (0, 0))
    rfull = pl.BlockSpec((t, ROPE_PAD), lambda h, i: (0, 0))
    return pl.pallas_call(
        body, name="attention_bwd", grid=(HEADS, t // tq),
        in_specs=[lat, rope, kfull, rfull, lat, pl.BlockSpec((None, tq, 1), lambda h, i: (h, i, 0)), lat],
        out_specs=[lat, rope, kfull, rfull],
        out_shape=[jax.ShapeDtypeStruct((HEADS, t, KV_LORA), _CDT), jax.ShapeDtypeStruct((t, HEADS * ROPE_PAD), F32),
                   jax.ShapeDtypeStruct((t, KV_LORA), F32), jax.ShapeDtypeStruct((t, ROPE_PAD), F32)],
        scratch_shapes=[pltpu.VMEM((tq, KV_LORA), F32), pltpu.VMEM((tq, ROPE_PAD), F32)],
        compiler_params=_cparams(("arbitrary", "arbitrary")))(ql, qr, ckv, kr, out, lse, dout)


def _gated_norm(o, z, w):
    return _rms_norm(o, w) * _silu(z)


def _mla_pre(ckv, krp, cq, cosb, sinb, qw, kw):
    return _rms_norm(cq, qw), _rms_norm(ckv, kw), _rope(krp, cosb, sinb)


def _merge(gg, y_dn, y_mla):
    return _sigmoid(gg[:, :D_MODEL]) * y_dn + _sigmoid(gg[:, D_MODEL:]) * y_mla


def _ln1(xv, attn_out, g, b):
    return _layer_norm(ALPHA * xv + attn_out, g, b)


def _final(h1, ffn, gate_pre, ple_proj, g, b):
    return _layer_norm(ALPHA * h1 + ffn + _sigmoid(gate_pre) * ple_proj, g, b)


def _swiglu(gt, up):
    return _silu(gt) * up


def _local_step(x, p, cosb, sinb, target, wt, sp):
    t = x.shape[0]
    bf = _CDT
    xb = x.astype(bf)
    g = {}

    qkv_pre = _mm2(xb, wt['qkv'], name="f_qkv")
    z = _mm2(xb, wt['z'], name="f_z")
    gg = _mm2(xb, wt['gg'], name="f_gg")
    pm = _mm2(xb, wt['mla'], name="f_mla")
    qkv_act = _conv_silu(qkv_pre, sp['conv_w'])
    u, w_, qd, kt, intra, gl = _delta_local(qkv_act, pm, sp['a_log'], sp['dt_bias'])
    o_dn, sall = _delta_scan(u, w_, qd, kt, intra, gl)
    (og,) = _rowwise(lambda h, o, zz, w: (_gated_norm(o, zz, w),),
                     [(o_dn, LANE, 0, True), (z, LANE, 0, True)], [sp['dn_norm_w']],
                     [(D_MODEL, LANE, True, bf)], name="f_gated_norm", tm=512, heads=HEADS)
    y_dn = _mm2(og, wt['br_dn'], name="f_br_dn")

    c_q, c_kv, k_rope = _rowwise(
        lambda h, *a: _mla_pre(*a),
        [(pm, KV_LORA, 0, False), (pm, ROPE_PAD, 2, False), (pm, Q_LORA, 2, False),
         (cosb, ROPE_PAD, 0, False), (sinb, ROPE_PAD, 0, False)],
        [sp['q_norm_w'], sp['kv_norm_w']],
        [(Q_LORA, Q_LORA, False, bf), (KV_LORA, KV_LORA, False, bf), (ROPE_PAD, ROPE_PAD, False, bf)], name="f_mla_pre")
    q_nope = _mm2(c_q, wt['uq_nope'], name="f_uq_nope", out_dtype=bf)
    q_rope_pre = _mm2(c_q, wt['uq_rope'], name="f_uq_rope")
    (q_rope,) = _rowwise(lambda h, q, c, s: (_rope(q, c, s),),
                         [(q_rope_pre, HEADS * ROPE_PAD, 0, False), (cosb, ROPE_PAD, 0, False), (sinb, ROPE_PAD, 0, False)],
                         [], [(HEADS * ROPE_PAD, HEADS * ROPE_PAD, False, bf)], name="f_q_rope")
    q_lat = _mm(q_nope, wt['uk'], name="f_q_lat", tb=True, heads=HEADS, a_head='col', b_head='lead', out_head='lead',
                dims=(t, KV_LORA, NOPE), out_dtype=bf)
    out_lat, lse = _attention(q_lat, q_rope, c_kv, k_rope)
    o_mla = _mm(out_lat, wt['uv'], name="f_o_mla", heads=HEADS, a_head='lead', b_head='lead', out_head='col',
                dims=(t, NOPE, KV_LORA), out_dtype=bf)
    y_mla = _mm2(o_mla, wt['br_mla'], name="f_br_mla")

    (mixed,) = _rowwise(lambda h, *a: (_merge(*a),),
                        [(gg, 2 * D_MODEL, 0, False), (y_dn, D_MODEL, 0, False), (y_mla, D_MODEL, 0, False)],
                        [], [(D_MODEL, D_MODEL, False, bf)], name="f_merge")
    attn_out = _mm2(mixed, wt['o'], name="f_o")
    (h1,) = _rowwise(lambda h, *a: (_ln1(*a),), [(x, D_MODEL, 0, False), (attn_out, D_MODEL, 0, False)],
                     [sp['ln1_g'], sp['ln1_b']], [(D_MODEL, D_MODEL, False, F32)], name="f_ln1")
    h1b = h1.astype(bf)
    ffn_in = _mm2(h1b, wt['ffn_in'], name="f_ffn_in")
    (act,) = _rowwise(lambda h, gt, up: (_swiglu(gt, up),),
                      [(ffn_in, FFN_HIDDEN, 0, False), (ffn_in, FFN_HIDDEN, 1, False)], [],
                      [(FFN_HIDDEN, FFN_HIDDEN, False, bf)], name="f_swiglu")
    ffn = _mm2(act, wt['ffn_out'], name="f_ffn_out", tk=1408)
    gate_pre = _mm2(h1b, wt['ple_gate'], name="f_ple_gate")
    pb = p.astype(bf)
    ple_proj = _mm2(pb, wt['ple'], name="f_ple")

    def final_fn(h, h1v, ffnv, gpv, ppv, tgt, gv, bv):
        y, vjp = jax.vjp(_final, h1v, ffnv, gpv, ppv, gv, bv)
        err = y - tgt
        dh1, dffn, dgp, dpp, dg, db = vjp(err * (1.0 / D_MODEL))
        sq = err * err
        lanes = sq[:, :LANE]
        for j in range(1, D_MODEL // LANE):
            lanes = lanes + sq[:, j * LANE:(j + 1) * LANE]
        loss = jnp.sum(lanes, axis=0, keepdims=True) * (0.5 / D_MODEL)
        return dffn, dgp, dpp, dg, db, loss

    dpre2, dgate_pre, dple_proj, g['ln2_g'], g['ln2_b'], loss_lanes = _rowwise(
        final_fn, [(a, D_MODEL, 0, False) for a in (h1, ffn, gate_pre, ple_proj, target)],
        [sp['ln2_g'], sp['ln2_b']],
        [(D_MODEL, D_MODEL, False, F32), (D_MODEL, D_MODEL, False, bf), (D_MODEL, D_MODEL, False, bf)],
        [(1, D_MODEL), (1, D_MODEL), (1, LANE)], name="b_final")
    dpre2b = dpre2.astype(bf)

    g['ple'] = _mm2(pb, dple_proj, ta=True, name="g_ple")
    g['ple_gate'] = _mm2(h1b, dgate_pre, ta=True, name="g_ple_gate")
    g['ffn_out'] = _mm2(act, dpre2b, ta=True, name="g_ffn_out", tm=1408)
    dact = _mm2(dpre2b, wt['ffn_out'], tb=True, name="b_dact", tn=1408)

    def swiglu_bwd(h, gt, up, d):
        _, vjp = jax.vjp(_swiglu, gt, up)
        dgt, dup = vjp(d)
        return (jnp.concatenate([dgt, dup], axis=1),)

    (dffn_in,) = _rowwise(swiglu_bwd, [(ffn_in, FFN_HIDDEN, 0, False), (ffn_in, FFN_HIDDEN, 1, False),
                                       (dact, FFN_HIDDEN, 0, False)], [],
                          [(2 * FFN_HIDDEN, 2 * FFN_HIDDEN, False, bf)], name="b_swiglu")
    g['ffn_in'] = _mm2(h1b, dffn_in, ta=True, name="g_ffn_in")
    dh1 = _mm2(dffn_in, wt['ffn_in'], tb=True, name="b_dh1_ffn", add=dpre2, add_scale=ALPHA)
    dh1 = _mm2(dgate_pre, wt['ple_gate'], tb=True, name="b_dh1_gate", add=dh1)

    def ln1_bwd(h, xv, ao, d, gv, bv):
        _, vjp = jax.vjp(_ln1, xv, ao, gv, bv)
        _, dao, dg, db = vjp(d)
        return dao, dg, db

    dpre1, g['ln1_g'], g['ln1_b'] = _rowwise(
        ln1_bwd, [(x, D_MODEL, 0, False), (attn_out, D_MODEL, 0, False), (dh1, D_MODEL, 0, False)],
        [sp['ln1_g'], sp['ln1_b']], [(D_MODEL, D_MODEL, False, F32)], [(1, D_MODEL), (1, D_MODEL)], name="b_ln1")
    dpre1b = dpre1.astype(bf)

    g['o'] = _mm2(mixed, dpre1b, ta=True, name="g_o")
    dmixed = _mm2(dpre1b, wt['o'], tb=True, name="b_dmixed")

    def merge_bwd(h, ggv, yd, ym, d):
        _, vjp = jax.vjp(_merge, ggv, yd, ym)
        return vjp(d)

    dgg, dy_dn, dy_mla = _rowwise(
        merge_bwd, [(gg, 2 * D_MODEL, 0, False), (y_dn, D_MODEL, 0, False), (y_mla, D_MODEL, 0, False),
                    (dmixed, D_MODEL, 0, False)], [],
        [(2 * D_MODEL, 2 * D_MODEL, False, bf), (D_MODEL, D_MODEL, False, bf), (D_MODEL, D_MODEL, False, bf)],
        name="b_merge")
    g['br_dn'] = _mm2(og, dy_dn, ta=True, name="g_br_dn")
    dog = _mm2(dy_dn, wt['br_dn'], tb=True, name="b_dog")
    g['br_mla'] = _mm2(o_mla, dy_mla, ta=True, name="g_br_mla")
    do_mla = _mm2(dy_mla, wt['br_mla'], tb=True, name="b_do_mla", out_dtype=bf)

    dout_lat = _mm(do_mla, wt['uv'], name="b_dout_lat", tb=True, heads=HEADS, a_head='col', b_head='lead',
                   out_head='lead', dims=(t, KV_LORA, NOPE))
    g['uv'] = _mm(out_lat, do_mla, name="g_uv", ta=True, heads=HEADS, a_head='lead', b_head='col', out_head='lead',
                  dims=(KV_LORA, NOPE, t))
    dq_lat, dq_rope, dckv_att, dkr_att = _attention_bwd(q_lat, q_rope, c_kv, k_rope, out_lat, lse, dout_lat)
    dq_nope = _mm(dq_lat, wt['uk'], name="b_dq_nope", heads=HEADS, a_head='lead', b_head='lead', out_head='col',
                  dims=(t, NOPE, KV_LORA), out_dtype=bf)
    g['uk'] = _mm(dq_lat, q_nope, name="g_uk", ta=True, heads=HEADS, a_head='lead', b_head='col', out_head='lead',
                  dims=(KV_LORA, NOPE, t))
    (dq_rope_pre,) = _rowwise(lambda h, d, c, s: (_rope_bwd(d, c, s),),
                              [(dq_rope, HEADS * ROPE_PAD, 0, False), (cosb, ROPE_PAD, 0, False), (sinb, ROPE_PAD, 0, False)],
                              [], [(HEADS * ROPE_PAD, HEADS * ROPE_PAD, False, bf)], name="b_q_rope")
    g['uq_nope'] = _mm2(c_q, dq_nope, ta=True, name="g_uq_nope")
    g['uq_rope'] = _mm2(c_q, dq_rope_pre, ta=True, name="g_uq_rope")
    dc_q = _mm2(dq_nope, wt['uq_nope'], tb=True, name="b_dcq_nope")
    dc_q = _mm2(dq_rope_pre, wt['uq_rope'], tb=True, name="b_dcq_rope", add=dc_q)

    def gated_norm_bwd(h, o, zz, d, w):
        _, vjp = jax.vjp(_gated_norm, o, zz, w)
        return vjp(d)

    do_dn, dz, g['dn_norm_w'] = _rowwise(
        gated_norm_bwd, [(o_dn, LANE, 0, True), (z, LANE, 0, True), (dog, LANE, 0, True)], [sp['dn_norm_w']],
        [(D_MODEL, LANE, True, F32), (D_MODEL, LANE, True, bf)], [(1, LANE)], name="b_gated_norm", tm=512, heads=HEADS)
    du, dw, dqd, dkt, dintra, dgl = _delta_scan_bwd(u, w_, qd, kt, intra, gl, sall, do_dn)
    dq_a, dk_a, dv_a, dba, g['a_log'], g['dt_bias'] = _delta_local_bwd(
        qkv_act, pm, sp['a_log'], sp['dt_bias'], du, dw, dqd, dkt, dintra, dgl)
    dqkv_act = jnp.concatenate([dq_a, dk_a, dv_a], axis=1)
    dqkv_pre, g['conv_w'] = _conv_silu_bwd(qkv_pre, sp['conv_w'], dqkv_act)

    def mla_pre_bwd(h, ckv, krp, cq, cosv, sinv, dcq, dckv, dkr, qw, kw):
        _, vjp = jax.vjp(lambda a, b, c, d, e: (_rms_norm(c, d), _rms_norm(a, e)), ckv, krp, cq, qw, kw)
        dckv_p, _, dcq_p, dqw, dkw = vjp((dcq, dckv))
        dkr_p = _rope_bwd(dkr, cosv, sinv)
        dpm = jnp.concatenate([dckv_p, dkr_p, jnp.zeros((ckv.shape[0], 3 * LANE), F32), dcq_p], axis=1)
        return dpm, dqw, dkw

    dpm_main, g['q_norm_w'], g['kv_norm_w'] = _rowwise(
        mla_pre_bwd,
        [(pm, KV_LORA, 0, False), (pm, ROPE_PAD, 2, False), (pm, Q_LORA, 2, False),
         (cosb, ROPE_PAD, 0, False), (sinb, ROPE_PAD, 0, False),
         (dc_q, Q_LORA, 0, False), (dckv_att, KV_LORA, 0, False), (dkr_att, ROPE_PAD, 0, False)],
        [sp['q_norm_w'], sp['kv_norm_w']], [(1152, 1152, False, F32)], [(1, Q_LORA), (1, KV_LORA)], name="b_mla_pre")
    (dpm,) = _rowwise(
        lambda h, a, b: (jnp.concatenate([a[:, :3 * LANE], b, a[:, 4 * LANE:]], axis=1),),
        [(dpm_main, 1152, 0, False), (dba, LANE, 0, False)], [], [(1152, 1152, False, bf)], name="b_dpm")

    g['qkv'] = _mm2(xb, dqkv_pre, ta=True, name="g_qkv")
    g['z'] = _mm2(xb, dz, ta=True, name="g_z")
    g['gg'] = _mm2(xb, dgg, ta=True, name="g_gg")
    g['mla'] = _mm2(xb, dpm, ta=True, name="g_mla")
    dx = _mm2(dqkv_pre, wt['qkv'], tb=True, name="b_dx_qkv", add=dpre1, add_scale=ALPHA)
    dx = _mm2(dz, wt['z'], tb=True, name="b_dx_z", add=dx)
    dx = _mm2(dgg, wt['gg'], tb=True, name="b_dx_gg", add=dx)
    dx = _mm2(dpm, wt['mla'], tb=True, name="b_dx_mla", add=dx)
    return loss_lanes, dx, g


_IN_SIZES = (QKV_W, HEADS * DN_DK, HEADS, HEADS, Q_LORA, KV_LORA, ROPE, D_MODEL, D_MODEL)


def _rope_tables(positions):
    inv_freq = ROPE_BASE ** (-jnp.arange(0, ROPE, 2, dtype=F32) / ROPE)
    ang = positions.astype(F32)[:, None] * inv_freq
    cos, sin = jnp.cos(ang), jnp.sin(ang)
    zeros = jnp.zeros((positions.shape[0], ROPE_PAD - ROPE), F32)
    return jnp.concatenate([cos, cos, zeros], axis=1), jnp.concatenate([-sin, sin, zeros], axis=1)


def _prep_weights(full):
    w_in = full['w_in']
    dt = w_in.dtype
    offs = [0]
    for s in _IN_SIZES:
        offs.append(offs[-1] + s)
    qkv, z, wb, wa, cq, ckv, kr, gd, gm = [w_in[:, offs[i]:offs[i + 1]] for i in range(len(_IN_SIZES))]
    zc = lambda n: jnp.zeros((D_MODEL, n), dt)
    w_uq = full['w_uq']
    wt = {
        'qkv': qkv, 'z': z, 'gg': jnp.concatenate([gd, gm], axis=1),
        'mla': jnp.concatenate([ckv, kr, zc(ROPE_PAD - ROPE), wb, wa, zc(LANE - 2 * HEADS), zc(2 * LANE), cq], axis=1),
        'uq_nope': w_uq[:, :, :NOPE].reshape(Q_LORA, HEADS * NOPE),
        'uq_rope': jnp.pad(w_uq[:, :, NOPE:], ((0, 0), (0, 0), (0, ROPE_PAD - ROPE))).reshape(Q_LORA, HEADS * ROPE_PAD),
        'uk': jnp.transpose(full['w_uk'], (1, 0, 2)), 'uv': jnp.transpose(full['w_uv'], (1, 0, 2)),
        'br_dn': full['w_br_dn'], 'br_mla': full['w_br_mla'], 'o': full['w_o'], 'ffn_in': full['w_ffn_in'],
        'ffn_out': full['w_ffn_out'], 'ple': full['w_ple'], 'ple_gate': full['w_ple_gate'],
    }
    return wt


def _prep_small(small):
    pad = lambda v: jnp.pad(v, (0, LANE - v.shape[0]))[None, :]
    return {
        'conv_w': small['conv_w'], 'a_log': pad(small['dn_a_log']), 'dt_bias': pad(small['dn_dt_bias']),
        'dn_norm_w': small['dn_norm_w'][None, :], 'q_norm_w': small['q_norm_w'][None, :],
        'kv_norm_w': small['kv_norm_w'][None, :], 'ln1_g': small['ln1_g'][None, :], 'ln1_b': small['ln1_b'][None, :],
        'ln2_g': small['ln2_g'][None, :], 'ln2_b': small['ln2_b'][None, :],
    }


def _unprep_grads(g):
    mla = g['mla']
    ba0 = KV_LORA + ROPE_PAD
    cq0 = ba0 + 3 * LANE
    w_in = jnp.concatenate([
        g['qkv'], g['z'], mla[:, ba0:ba0 + HEADS], mla[:, ba0 + HEADS:ba0 + 2 * HEADS], mla[:, cq0:cq0 + Q_LORA],
        mla[:, :KV_LORA], mla[:, KV_LORA:KV_LORA + ROPE], g['gg']], axis=1)
    w_uq = jnp.concatenate([g['uq_nope'].reshape(Q_LORA, HEADS, NOPE),
                            g['uq_rope'].reshape(Q_LORA, HEADS, ROPE_PAD)[:, :, :ROPE]], axis=2)
    return {
        'w_in': w_in, 'conv_w': g['conv_w'], 'dn_a_log': g['a_log'][0, :HEADS], 'dn_dt_bias': g['dt_bias'][0, :HEADS],
        'dn_norm_w': g['dn_norm_w'][0], 'q_norm_w': g['q_norm_w'][0], 'w_uq': w_uq, 'kv_norm_w': g['kv_norm_w'][0],
        'w_uk': jnp.transpose(g['uk'], (1, 0, 2)), 'w_uv': jnp.transpose(g['uv'], (1, 0, 2)),
        'w_br_dn': g['br_dn'], 'w_br_mla': g['br_mla'], 'w_o': g['o'], 'ln1_g': g['ln1_g'][0], 'ln1_b': g['ln1_b'][0],
        'w_ffn_in': g['ffn_in'], 'w_ffn_out': g['ffn_out'], 'w_ple': g['ple'], 'w_ple_gate': g['ple_gate'],
        'ln2_g': g['ln2_g'][0], 'ln2_b': g['ln2_b'][0],
    }


_FLAT_PIECES = (
    ('w_in', 1716, (D_MODEL, 1716), 1), ('w_ffn_in', 1408, (D_MODEL, 1408), 1), ('w_ffn_out', 704, (704, D_MODEL), 0),
    ('w_br_dn', 256, (256, D_MODEL), 0), ('w_br_mla', 256, (256, D_MODEL), 0), ('w_o', 256, (256, D_MODEL), 0),
    ('w_ple_gate', 256, (256, D_MODEL), 0), ('w_uq', 144, (96, HEADS, NOPE + ROPE), 0),
    ('w_uk', 64, (64, HEADS, NOPE), 0), ('w_uv', 64, (64, HEADS, NOPE), 0), ('w_ple', 64, (PLE_DIM, 256), 1),
)
_CONV_FLAT_ROWS = 6
_CONV_SHARD = QKV_W // N_SHARD


def _flat_offsets():
    offs, o = {}, 0
    for name, rows, _, _ in _FLAT_PIECES:
        offs[name] = o
        o += rows
    return offs, o


def _pack_shard(shards, conv_w):
    parts = [shards[name].astype(_CDT).reshape(rows, FLAT_W) for name, rows, _, _ in _FLAT_PIECES]
    parts.append(lax.bitcast_convert_type(conv_w, jnp.bfloat16).reshape(_CONV_FLAT_ROWS, FLAT_W).astype(_CDT))
    used = sum(p.shape[0] for p in parts)
    parts.append(jnp.zeros((FLAT_ROWS - used, FLAT_W), _CDT))
    return jnp.concatenate(parts, axis=0)


def _unpack_gathered(gathered):
    offs, used = _flat_offsets()
    full = {}
    for name, rows, shape, axis in _FLAT_PIECES:
        pieces = [gathered[s, offs[name]:offs[name] + rows].reshape(shape) for s in range(N_SHARD)]
        full[name] = jnp.concatenate(pieces, axis=axis)
    conv = [lax.bitcast_convert_type(gathered[s, used:used + _CONV_FLAT_ROWS].astype(jnp.bfloat16).reshape(DN_CONV, _CONV_SHARD, 2), F32)
            for s in range(N_SHARD)]
    return full, jnp.concatenate(conv, axis=1)


def _pack_grads(gw):
    parts = []
    for name, rows, shape, axis in _FLAT_PIECES:
        gfull = gw[name]
        if axis == 1:
            cut = gfull.reshape(gfull.shape[0], N_SHARD, gfull.shape[1] // N_SHARD)
            cut = jnp.transpose(cut, (1, 0, 2))
        else:
            cut = gfull.reshape((N_SHARD, gfull.shape[0] // N_SHARD) + gfull.shape[1:])
        parts.append(cut.reshape(N_SHARD, rows, FLAT_W))
    used = sum(p.shape[1] for p in parts)
    parts.append(jnp.zeros((N_SHARD, FLAT_ROWS - used, FLAT_W), F32))
    return jnp.concatenate(parts, axis=1)


def _unpack_reduced(flat):
    offs, _ = _flat_offsets()
    return {name: flat[offs[name]:offs[name] + rows].reshape(shape) for name, rows, shape, _ in _FLAT_PIECES}


_HBM = pl.BlockSpec(memory_space=pltpu.HBM)


def _place():
    x, y, c = lax.axis_index("x"), lax.axis_index("y"), lax.axis_index("c")
    chips = [(1 - x, y), (x, 1 - y), (1 - x, 1 - y)]
    return x, y, c, chips


def _remote(src, dst, send_sems, recv_sems, k, to):
    return pltpu.make_async_remote_copy(src_ref=src, dst_ref=dst, send_sem=send_sems.at[k], recv_sem=recv_sems.at[k],
                                        device_id=to, device_id_type=_MESH)


def _gather_shards(flat):
    rows, width = flat.shape
    half = rows // 2

    def body(in_ref, out_ref, send_sems, recv_sems, local_sem):
        x, y, c, chips = _place()
        me, sibling = (x, y, c), (x, y, 1 - c)

        def slot(cx, cy, hf):
            return out_ref.at[2 * cx + cy, pl.ds(pl.multiple_of(hf * half, 16), half), :]

        mine = pltpu.make_async_copy(in_ref, out_ref.at[2 * x + y], local_sem)
        mine.start()
        my_half = in_ref.at[pl.ds(pl.multiple_of(c * half, 16), half), :]
        first = [_remote(my_half, slot(x, y, c), send_sems, recv_sems, j, (cx, cy, c)) for j, (cx, cy) in enumerate(chips)]
        for cp in first:
            cp.start()
        passed = [_remote(slot(cx, cy, c), slot(cx, cy, c), send_sems, recv_sems, 3 + j, sibling)
                  for j, (cx, cy) in enumerate(chips)]
        for j, (cx, cy) in enumerate(chips):
            _remote(slot(cx, cy, c), slot(cx, cy, c), send_sems, recv_sems, j, me).wait_recv()
            passed[j].start()
        for j, (cx, cy) in enumerate(chips):
            _remote(slot(cx, cy, 1 - c), slot(cx, cy, 1 - c), send_sems, recv_sems, 3 + j, me).wait_recv()
        for cp in first + passed:
            cp.wait_send()
        mine.wait()

    return pl.pallas_call(
        body, name="gather_shards", out_shape=jax.ShapeDtypeStruct((N_SHARD, rows, width), flat.dtype),
        in_specs=[_HBM], out_specs=_HBM,
        scratch_shapes=[pltpu.SemaphoreType.DMA((6,)), pltpu.SemaphoreType.DMA((6,)), pltpu.SemaphoreType.DMA],
    )(flat)


def _reduce_pair_exchange(gflat):
    _, rows, width = gflat.shape
    half = rows // 2

    def body(g_ref, recv_ref, send_sems, recv_sems):
        x, y, c, _ = _place()
        src = g_ref.at[:, pl.ds(pl.multiple_of((1 - c) * half, 8), half), :]
        cp = _remote(src, recv_ref, send_sems, recv_sems, 0, (x, y, 1 - c))
        cp.start()
        cp.wait()

    return pl.pallas_call(
        body, name="reduce_pair_exchange", out_shape=jax.ShapeDtypeStruct((N_SHARD, half, width), gflat.dtype),
        in_specs=[_HBM], out_specs=_HBM,
        scratch_shapes=[pltpu.SemaphoreType.DMA((1,)), pltpu.SemaphoreType.DMA((1,))],
    )(gflat)


_ADD_ROWS = 384


def _pair_add(gflat, recv, c_arr):
    _, rows, width = gflat.shape
    half = rows // 2
    nt = half // _ADD_ROWS

    def body(c_ref, a_ref, b_ref, o_ref):
        o_ref[...] = a_ref[...] + b_ref[...]

    blk = lambda f: pl.BlockSpec((None, _ADD_ROWS, width), f)
    return pl.pallas_call(
        body, name="pair_add", out_shape=jax.ShapeDtypeStruct((N_SHARD, half, width), gflat.dtype),
        grid_spec=pltpu.PrefetchScalarGridSpec(
            num_scalar_prefetch=1, grid=(N_SHARD, nt),
            in_specs=[blk(lambda s, i, c: (s, c[0] * nt + i, 0)), blk(lambda s, i, c: (s, i, 0))],
            out_specs=blk(lambda s, i, c: (s, i, 0))),
        compiler_params=_cparams(("parallel", "parallel")))(c_arr, gflat, recv)


def _reduce_chip_exchange(part):
    _, half, width = part.shape

    def body(p_ref, out_ref, send_sems, recv_sems, local_sem):
        x, y, c, chips = _place()
        me = 2 * x + y
        mine = pltpu.make_async_copy(p_ref.at[me], out_ref.at[me], local_sem)
        mine.start()
        sends = [_remote(p_ref.at[2 * cx + cy], out_ref.at[me], send_sems, recv_sems, j, (cx, cy, c))
                 for j, (cx, cy) in enumerate(chips)]
        for cp in sends:
            cp.start()
        for j, (cx, cy) in enumerate(chips):
            _remote(p_ref.at[me], out_ref.at[2 * cx + cy], send_sems, recv_sems, j, (x, y, c)).wait_recv()
        for cp in sends:
            cp.wait_send()
        mine.wait()

    return pl.pallas_call(
        body, name="reduce_chip_exchange", out_shape=jax.ShapeDtypeStruct(part.shape, part.dtype),
        in_specs=[_HBM], out_specs=_HBM,
        scratch_shapes=[pltpu.SemaphoreType.DMA((3,)), pltpu.SemaphoreType.DMA((3,)), pltpu.SemaphoreType.DMA],
    )(part)


def _chip_add(parts):
    _, half, width = parts.shape

    def body(a0, a1, a2, a3, o_ref):
        o_ref[...] = ((a0[...] + a1[...]) + a2[...]) + a3[...]

    specs = [pl.BlockSpec((None, _ADD_ROWS, width), functools.partial(lambda i, k: (k, i, 0), k=k)) for k in range(N_SHARD)]
    return pl.pallas_call(
        body, name="chip_add", grid=(half // _ADD_ROWS,), in_specs=specs,
        out_specs=pl.BlockSpec((_ADD_ROWS, width), lambda i: (i, 0)),
        out_shape=jax.ShapeDtypeStruct((half, width), parts.dtype),
        compiler_params=_cparams(("parallel",)))(parts, parts, parts, parts)


def _reduce_pair_share(rhalf):
    half, width = rhalf.shape

    def body(r_ref, out_ref, send_sems, recv_sems, local_sem):
        x, y, c, _ = _place()
        mine_rows = out_ref.at[pl.ds(pl.multiple_of(c * half, 8), half), :]
        other_rows = out_ref.at[pl.ds(pl.multiple_of((1 - c) * half, 8), half), :]
        mine = pltpu.make_async_copy(r_ref, mine_rows, local_sem)
        mine.start()
        cp = _remote(r_ref, mine_rows, send_sems, recv_sems, 0, (x, y, 1 - c))
        cp.start()
        _remote(r_ref, other_rows, send_sems, recv_sems, 0, (x, y, c)).wait_recv()
        cp.wait_send()
        mine.wait()

    return pl.pallas_call(
        body, name="reduce_pair_share", out_shape=jax.ShapeDtypeStruct((2 * half, width), rhalf.dtype),
        in_specs=[_HBM], out_specs=_HBM,
        scratch_shapes=[pltpu.SemaphoreType.DMA((1,)), pltpu.SemaphoreType.DMA((1,)), pltpu.SemaphoreType.DMA],
    )(rhalf)


def _small_allreduce(buf):
    r, width = buf.shape
    n_dev = 8

    def body(x_ref, all_ref, sum_ref, send_sems, recv_sems, local_sem):
        x, y, c, chips = _place()
        me, sibling = (x, y, c), (x, y, 1 - c)

        def rows(px, py, pc):
            return all_ref.at[pl.ds(pl.multiple_of((4 * px + 2 * py + pc) * r, 8), r), :]

        def copy(k, block, to, src=None):
            return _remote(rows(*block) if src is None else src, rows(*block), send_sems, recv_sems, k, to)

        mine = pltpu.make_async_copy(x_ref, rows(*me), local_sem)
        mine.start()
        first = [copy(0, me, sibling, src=x_ref)]
        first += [copy(1 + j, me, (*chip, c), src=x_ref) for j, chip in enumerate(chips)]
        for cp in first:
            cp.start()
        passed = [copy(4 + j, (*chip, c), sibling) for j, chip in enumerate(chips)]
        for j, chip in enumerate(chips):
            copy(1 + j, (*chip, c), me).wait_recv()
            passed[j].start()
        copy(0, sibling, me).wait_recv()
        for j, chip in enumerate(chips):
            copy(4 + j, (*chip, 1 - c), me).wait_recv()
        for cp in first + passed:
            cp.wait_send()
        mine.wait()
        total = all_ref[0:r, :]
        for k in range(1, n_dev):
            total = total + all_ref[k * r:(k + 1) * r, :]
        sum_ref[...] = total

    vm = pl.BlockSpec(memory_space=pltpu.VMEM)
    _, total = pl.pallas_call(
        body, name="small_allreduce",
        out_shape=[jax.ShapeDtypeStruct((n_dev * r, width), buf.dtype), jax.ShapeDtypeStruct((r, width), buf.dtype)],
        in_specs=[vm], out_specs=[vm, vm],
        scratch_shapes=[pltpu.SemaphoreType.DMA((7,)), pltpu.SemaphoreType.DMA((7,)), pltpu.SemaphoreType.DMA],
    )(buf)
    return total


def _row_tile(rows, cap):
    if rows <= cap:
        return rows
    t = (cap // 8) * 8
    while t >= 8:
        if rows % t == 0:
            return t
        t -= 8
    return rows


def _adamw(w, g, m, v, name):
    shape = w.shape
    cols = shape[-1] if len(shape) <= 3 else shape[-2] * shape[-1]
    w2, g2, m2, v2 = (a.reshape(-1, cols) for a in (w, g, m, v))
    rows = w2.shape[0]
    tr = _row_tile(rows, 256)

    def body(w_ref, g_ref, m_ref, v_ref, d_ref, mo_ref, vo_ref):
        gv = g_ref[...]
        mn = ADAM_B1 * m_ref[...] + (1.0 - ADAM_B1) * gv
        vn = ADAM_B2 * v_ref[...] + (1.0 - ADAM_B2) * (gv * gv)
        m_hat = mn / (1.0 - ADAM_B1 ** ADAM_STEP)
        v_hat = vn / (1.0 - ADAM_B2 ** ADAM_STEP)
        d_ref[...] = -ADAM_LR * (m_hat / (jnp.sqrt(v_hat) + ADAM_EPS) + ADAM_WD * w_ref[...])
        mo_ref[...] = mn
        vo_ref[...] = vn

    blk = pl.BlockSpec((tr, cols), lambda i: (i, 0))
    outs = pl.pallas_call(
        body, name=name, grid=(rows // tr,), in_specs=[blk] * 4, out_specs=[blk] * 3,
        out_shape=[jax.ShapeDtypeStruct((rows, cols), F32)] * 3, compiler_params=_cparams(("parallel",)))(w2, g2, m2, v2)
    return tuple(o.reshape(shape) for o in outs)


_WEIGHT_NAMES = ('w_in', 'conv_w', 'dn_a_log', 'dn_dt_bias', 'dn_norm_w', 'q_norm_w', 'w_uq', 'kv_norm_w', 'w_uk',
                 'w_uv', 'w_br_dn', 'w_br_mla', 'w_o', 'ln1_g', 'ln1_b', 'w_ffn_in', 'w_ffn_out', 'w_ple',
                 'w_ple_gate', 'ln2_g', 'ln2_b')
_SMALL_NAMES = ('ln1_g', 'ln1_b', 'ln2_g', 'ln2_b', 'q_norm_w', 'kv_norm_w', 'dn_norm_w', 'dn_a_log', 'dn_dt_bias')
_CONV_SMALL_ROW = len(_SMALL_NAMES)


def _pack_small(gw):
    rows = [jnp.pad(gw[n], (0, FLAT_W - gw[n].shape[0]))[None, :] for n in _SMALL_NAMES]
    rows.append(gw['conv_w'].reshape(DN_CONV * QKV_W // FLAT_W, FLAT_W))
    used = sum(r.shape[0] for r in rows)
    rows.append(jnp.zeros((SMALL_ROWS - used, FLAT_W), F32))
    return jnp.concatenate(rows, axis=0)


def kernel(x, p, positions, w_in, conv_w, dn_a_log, dn_dt_bias, dn_norm_w, q_norm_w, w_uq, kv_norm_w, w_uk, w_uv, w_br_dn, w_br_mla, w_o, ln1_g, ln1_b, w_ffn_in, w_ffn_out, w_ple, w_ple_gate, ln2_g, ln2_b, loss_target, m_w_in, m_conv_w, m_dn_a_log, m_dn_dt_bias, m_dn_norm_w, m_q_norm_w, m_w_uq, m_kv_norm_w, m_w_uk, m_w_uv, m_w_br_dn, m_w_br_mla, m_w_o, m_ln1_g, m_ln1_b, m_w_ffn_in, m_w_ffn_out, m_w_ple, m_w_ple_gate, m_ln2_g, m_ln2_b, v_w_in, v_conv_w, v_dn_a_log, v_dn_dt_bias, v_dn_norm_w, v_q_norm_w, v_w_uq, v_kv_norm_w, v_w_uk, v_w_uv, v_w_br_dn, v_w_br_mla, v_w_o, v_ln1_g, v_ln1_b, v_w_ffn_in, v_w_ffn_out, v_w_ple, v_w_ple_gate, v_ln2_g, v_ln2_b):
    ws = dict(w_in=w_in, conv_w=conv_w, dn_a_log=dn_a_log, dn_dt_bias=dn_dt_bias, dn_norm_w=dn_norm_w, q_norm_w=q_norm_w,
              w_uq=w_uq, kv_norm_w=kv_norm_w, w_uk=w_uk, w_uv=w_uv, w_br_dn=w_br_dn, w_br_mla=w_br_mla, w_o=w_o,
              ln1_g=ln1_g, ln1_b=ln1_b, w_ffn_in=w_ffn_in, w_ffn_out=w_ffn_out, w_ple=w_ple, w_ple_gate=w_ple_gate,
              ln2_g=ln2_g, ln2_b=ln2_b)
    ms = dict(w_in=m_w_in, conv_w=m_conv_w, dn_a_log=m_dn_a_log, dn_dt_bias=m_dn_dt_bias, dn_norm_w=m_dn_norm_w,
              q_norm_w=m_q_norm_w, w_uq=m_w_uq, kv_norm_w=m_kv_norm_w, w_uk=m_w_uk, w_uv=m_w_uv, w_br_dn=m_w_br_dn,
              w_br_mla=m_w_br_mla, w_o=m_w_o, ln1_g=m_ln1_g, ln1_b=m_ln1_b, w_ffn_in=m_w_ffn_in, w_ffn_out=m_w_ffn_out,
              w_ple=m_w_ple, w_ple_gate=m_w_ple_gate, ln2_g=m_ln2_g, ln2_b=m_ln2_b)
    vs = dict(w_in=v_w_in, conv_w=v_conv_w, dn_a_log=v_dn_a_log, dn_dt_bias=v_dn_dt_bias, dn_norm_w=v_dn_norm_w,
              q_norm_w=v_q_norm_w, w_uq=v_w_uq, kv_norm_w=v_kv_norm_w, w_uk=v_w_uk, w_uv=v_w_uv, w_br_dn=v_w_br_dn,
              w_br_mla=v_w_br_mla, w_o=v_w_o, ln1_g=v_ln1_g, ln1_b=v_ln1_b, w_ffn_in=v_w_ffn_in, w_ffn_out=v_w_ffn_out,
              w_ple=v_w_ple, w_ple_gate=v_w_ple_gate, ln2_g=v_ln2_g, ln2_b=v_ln2_b)
    mx, my, mc = lax.axis_index("x"), lax.axis_index("y"), lax.axis_index("c")

    flat = _pack_shard({name: ws[name][0] for name, _, _, _ in _FLAT_PIECES}, conv_w[0])
    full, conv_full = _unpack_gathered(_gather_shards(flat))
    wt = _prep_weights(full)
    small = {n: ws[n][0] for n in _SMALL_NAMES}
    small['conv_w'] = conv_full
    sp = _prep_small(small)
    cosb, sinb = _rope_tables(positions[0])

    loss_lanes, dx, g = _local_step(x[0], p[0, 0], cosb, sinb, loss_target[0], wt, sp)
    gw = _unprep_grads(g)
    loss = lax.psum(jnp.sum(loss_lanes), ("x", "y", "c"))

    gflat = _pack_grads(gw)
    part = _pair_add(gflat, _reduce_pair_exchange(gflat), jnp.reshape(mc, (1,)).astype(jnp.int32))
    reduced = _unpack_reduced(_reduce_pair_share(_chip_add(_reduce_chip_exchange(part))))
    tot = _small_allreduce(_pack_small(gw))
    gred = {name: reduced[name][None] for name, _, _, _ in _FLAT_PIECES}
    for i, n in enumerate(_SMALL_NAMES):
        gred[n] = tot[i, :ws[n].shape[1]][None]
    conv_tot = tot[_CONV_SMALL_ROW:_CONV_SMALL_ROW + DN_CONV * QKV_W // FLAT_W].reshape(DN_CONV, QKV_W)
    gred['conv_w'] = lax.dynamic_slice_in_dim(conv_tot, (2 * mx + my) * _CONV_SHARD, _CONV_SHARD, axis=1)[None]

    deltas, new_m, new_v = {}, {}, {}
    for n in _WEIGHT_NAMES:
        gred[n] = gred[n].reshape(ws[n].shape)
        deltas[n], new_m[n], new_v[n] = _adamw(ws[n], gred[n], ms[n], vs[n], "adamw_" + n)
    return (loss, dx[None], *[gred[n] for n in _WEIGHT_NAMES], *[deltas[n] for n in _WEIGHT_NAMES],
            *[new_m[n] for n in _WEIGHT_NAMES], *[new_v[n] for n in _WEIGHT_NAMES])
```

```python
import functools
import math

import jax
import jax.numpy as jnp
from jax import lax
from jax.experimental import pallas as pl
from jax.experimental.pallas import tpu as pltpu

F32 = jnp.float32
_CDT = jnp.bfloat16
_HI = lax.Precision.HIGHEST
_MESH = pl.DeviceIdType.MESH

D_MODEL = 1024
PLE_DIM = 256
HEADS = 8
DN_DK = 128
DN_CHUNK = 64
DN_CONV = 4
QKV_W = 3 * HEADS * DN_DK
Q_LORA = 384
KV_LORA = 256
NOPE = 128
ROPE = 64
ROPE_PAD = 128
FFN_HIDDEN = 2816
D_IN = 6864
ROPE_BASE = 10000.0
ALPHA = 2.0 ** 0.25
ATT_SCALE = (NOPE + ROPE) ** -0.5
NEG_BIG = -1e30
ADAM_LR, ADAM_B1, ADAM_B2, ADAM_EPS, ADAM_WD, ADAM_STEP = 0.001, 0.9, 0.999, 1e-08, 0.01, 10

LANE = 128
VMEM_LIMIT = 56 * 1024 * 1024
MM_VMEM_BUDGET = 40 * 1024 * 1024
N_SHARD = 4
FLAT_W = 1024
FLAT_HALF = 2688
FLAT_ROWS = 2 * FLAT_HALF
SMALL_ROWS = 88


def _tile(dim, cap):
    if dim <= cap:
        return dim
    t = (cap // LANE) * LANE
    while t >= LANE:
        if dim % t == 0:
            return t
        t -= LANE
    return dim


def _cparams(sem):
    return pltpu.CompilerParams(dimension_semantics=sem, vmem_limit_bytes=VMEM_LIMIT)


def _mm(a, b, *, name, ta=False, tb=False, add=None, add_scale=1.0, out_dtype=F32, heads=None,
        a_head=None, b_head=None, out_head=None, dims=None, tm=1408, tn=1408):
    m, n, k = dims
    tm, tn = _tile(m, tm), _tile(n, tn)
    sa, sb, so = a.dtype.itemsize, b.dtype.itemsize, jnp.dtype(out_dtype).itemsize

    def vmem_need(tk_):
        acc = tm * tn * 4 if tk_ < k else 0
        extra = 2 * tm * tn * 4 if add is not None else 0
        return 2 * (tm * tk_ * sa + tk_ * tn * sb) + 2 * tm * tn * so + acc + extra

    tk = k
    while vmem_need(tk) > MM_VMEM_BUDGET and tk > LANE:
        smaller = _tile(k, tk - LANE)
        if smaller >= tk:
            break
        tk = smaller
    nk = k // tk
    hgrid = () if heads is None else (heads,)
    off = len(hgrid)

    def spec(rows, cols, rtile, ctile, rsel, csel, layout):
        def idx(*g):
            h = g[0] if off else 0
            ri, ci = g[off + rsel], g[off + csel]
            if layout == 'lead':
                return (h, ri, ci)
            if layout == 'col':
                return (ri, h * (cols // ctile) + ci)
            return (ri, ci)
        if layout == 'lead':
            return pl.BlockSpec((None, rtile, ctile), idx)
        return pl.BlockSpec((rtile, ctile), idx)

    a_spec = spec(k, m, tk, tm, 2, 0, a_head) if ta else spec(m, k, tm, tk, 0, 2, a_head)
    b_spec = spec(n, k, tn, tk, 1, 2, b_head) if tb else spec(k, n, tk, tn, 2, 1, b_head)
    o_spec = spec(m, n, tm, tn, 0, 1, out_head)
    in_specs = [a_spec, b_spec]
    args = [a, b]
    if add is not None:
        in_specs.append(spec(m, n, tm, tn, 0, 1, out_head))
        args.append(add)
    dn = (((0 if ta else 1,), (1 if tb else 0,)), ((), ()))

    def body(*refs):
        a_ref, b_ref = refs[0], refs[1]
        prod = lax.dot_general(a_ref[...].astype(_CDT), b_ref[...].astype(_CDT), dn, preferred_element_type=F32)
        if nk == 1:
            o_ref = refs[-1]
            if add is not None:
                prod = prod + refs[2][...].astype(F32) * add_scale
            o_ref[...] = prod.astype(out_dtype)
            return
        o_ref, acc_ref = refs[-2], refs[-1]
        kk = pl.program_id(off + 2)

        @pl.when(kk == 0)
        def _():
            if add is not None:
                acc_ref[...] = refs[2][...].astype(F32) * add_scale
            else:
                acc_ref[...] = jnp.zeros_like(acc_ref)

        acc_ref[...] += prod

        @pl.when(kk == nk - 1)
        def _():
            o_ref[...] = acc_ref[...].astype(out_dtype)

    if out_head == 'lead':
        oshape = (heads, m, n)
    elif out_head == 'col':
        oshape = (m, heads * n)
    else:
        oshape = (m, n)
    sem = ("parallel",) * (off + 2) + ("arbitrary",)
    return pl.pallas_call(
        body, name=name, grid=hgrid + (m // tm, n // tn, nk), in_specs=in_specs, out_specs=o_spec,
        out_shape=jax.ShapeDtypeStruct(oshape, out_dtype),
        scratch_shapes=[pltpu.VMEM((tm, tn), F32)] if nk > 1 else [],
        compiler_params=_cparams(sem))(*args)


def _mm2(a, b, **kw):
    ta, tb = kw.get('ta', False), kw.get('tb', False)
    m = a.shape[1] if ta else a.shape[0]
    k = a.shape[0] if ta else a.shape[1]
    n = b.shape[0] if tb else b.shape[1]
    return _mm(a, b, dims=(m, n, k), **kw)


def _rowwise(fn, rows, bcast, outs, reds=(), *, name, tm=256, heads=None):
    t = rows[0][0].shape[0]
    tm = min(tm, t)
    hn = 1 if heads is None else heads
    in_specs, args = [], []
    for arr, width, base, per_head in rows:
        in_specs.append(pl.BlockSpec((tm, width), functools.partial(
            lambda i, h, base, per_head: (i, base + (h if per_head else 0)), base=base, per_head=per_head)))
        args.append(arr)
    for arr in bcast:
        in_specs.append(pl.BlockSpec(arr.shape, lambda i, h: (0, 0)))
        args.append(arr)
    out_specs, out_shape = [], []
    for total, width, per_head, dt in outs:
        out_specs.append(pl.BlockSpec((tm, width), functools.partial(
            lambda i, h, per_head: (i, h if per_head else 0), per_head=per_head)))
        out_shape.append(jax.ShapeDtypeStruct((t, total), dt))
    for shp in reds:
        out_specs.append(pl.BlockSpec(shp, lambda i, h: (0, 0)))
        out_shape.append(jax.ShapeDtypeStruct(shp, F32))
    n_in, n_out, n_red = len(args), len(outs), len(reds)

    def body(*refs):
        i, h = pl.program_id(0), pl.program_id(1)
        vals = fn(h, *[r[...] for r in refs[:n_in]])
        for r, v in zip(refs[n_in:n_in + n_out], vals[:n_out]):
            r[...] = v.astype(r.dtype)
        if n_red:
            @pl.when((i == 0) & (h == 0))
            def _():
                for r in refs[n_in + n_out:]:
                    r[...] = jnp.zeros_like(r)
            for r, v in zip(refs[n_in + n_out:], vals[n_out:]):
                r[...] += v

    sem = ("arbitrary", "arbitrary") if n_red else ("parallel", "parallel")
    res = pl.pallas_call(body, name=name, grid=(t // tm, hn), in_specs=in_specs, out_specs=out_specs,
                         out_shape=out_shape, compiler_params=_cparams(sem))(*args)
    return tuple(res)


def _sigmoid(x):
    return 1.0 / (1.0 + jnp.exp(-x))


def _silu(x):
    return x * _sigmoid(x)


def _softplus(x):
    return jnp.maximum(x, 0.0) + jnp.log(1.0 + jnp.exp(-jnp.abs(x)))


def _layer_norm(t, g, b):
    mu = jnp.mean(t, axis=-1, keepdims=True)
    d = t - mu
    var = jnp.mean(d * d, axis=-1, keepdims=True)
    return d * lax.rsqrt(var + 1e-5) * g + b


def _rms_norm(t, w):
    return t * lax.rsqrt(jnp.mean(t * t, axis=-1, keepdims=True) + 1e-6) * w


def _swap_rope_halves(t):
    lane = lax.broadcasted_iota(jnp.int32, t.shape, 1) % ROPE_PAD
    n = t.shape[1]
    up = pltpu.roll(t, n - ROPE // 2, axis=1)
    dn = pltpu.roll(t, ROPE // 2, axis=1)
    return jnp.where(lane < ROPE // 2, up, jnp.where(lane < ROPE, dn, 0.0))


def _rope(t, cosb, sinb):
    reps = t.shape[1] // ROPE_PAD
    c = jnp.tile(cosb, (1, reps)) if reps > 1 else cosb
    s = jnp.tile(sinb, (1, reps)) if reps > 1 else sinb
    return t * c + _swap_rope_halves(t) * s


def _rope_bwd(d, cosb, sinb):
    reps = d.shape[1] // ROPE_PAD
    c = jnp.tile(cosb, (1, reps)) if reps > 1 else cosb
    s = jnp.tile(sinb, (1, reps)) if reps > 1 else sinb
    return d * c + _swap_rope_halves(d * s)


_CONV_ROWS = 256
_CONV_COLS = 256


def _conv_window(ref, r0, lo, hi, t):
    parts = []
    start, stop = r0 - lo, r0 + _CONV_ROWS + hi
    if start < 0:
        parts.append(jnp.zeros((-start, ref.shape[1]), F32))
        start = 0
    tail = max(stop - t, 0)
    parts.append(ref[start:stop - tail, :].astype(F32))
    if tail:
        parts.append(jnp.zeros((tail, ref.shape[1]), F32))
    return parts[0] if len(parts) == 1 else jnp.concatenate(parts, axis=0)


def _conv_taps(win, w_ref, n_out):
    acc = win[8:8 + n_out] * w_ref[DN_CONV - 1:DN_CONV, :]
    for i in range(DN_CONV - 1):
        acc = acc + pltpu.roll(win, DN_CONV - 1 - i, axis=0)[8:8 + n_out] * w_ref[i:i + 1, :]
    return acc


def _conv_silu(x, w):
    t, ch = x.shape

    def body(x_ref, w_ref, o_ref):
        for r in range(t // _CONV_ROWS):
            r0 = r * _CONV_ROWS
            c = _conv_taps(_conv_window(x_ref, r0, 8, 0, t), w_ref, _CONV_ROWS)
            o_ref[r0:r0 + _CONV_ROWS, :] = _silu(c)

    return pl.pallas_call(
        body, name="conv_silu", grid=(ch // _CONV_COLS,),
        in_specs=[pl.BlockSpec((t, _CONV_COLS), lambda j: (0, j)), pl.BlockSpec((DN_CONV, _CONV_COLS), lambda j: (0, j))],
        out_specs=pl.BlockSpec((t, _CONV_COLS), lambda j: (0, j)),
        out_shape=jax.ShapeDtypeStruct((t, ch), F32), compiler_params=_cparams(("parallel",)))(x, w)


def _conv_silu_bwd(x, w, dy):
    t, ch = x.shape

    def body(x_ref, w_ref, dy_ref, dx_ref, dw_ref):
        dws = [jnp.zeros((1, _CONV_COLS), F32) for _ in range(DN_CONV)]
        for r in range(t // _CONV_ROWS):
            r0 = r * _CONV_ROWS
            n_ext = _CONV_ROWS + 8
            xw = _conv_window(x_ref, r0, 8, 8, t)
            c = _conv_taps(xw, w_ref, n_ext)
            sg = _sigmoid(c)
            ds = _conv_window(dy_ref, r0, 0, 8, t) * (sg * (1.0 + c * (1.0 - sg)))
            dx = ds[:_CONV_ROWS] * w_ref[DN_CONV - 1:DN_CONV, :]
            for i in range(DN_CONV - 1):
                sh = DN_CONV - 1 - i
                dx = dx + pltpu.roll(ds, n_ext - sh, axis=0)[:_CONV_ROWS] * w_ref[i:i + 1, :]
            dx_ref[r0:r0 + _CONV_ROWS, :] = dx.astype(dx_ref.dtype)
            ds0 = ds[:_CONV_ROWS]
            for i in range(DN_CONV):
                sh = DN_CONV - 1 - i
                xs = xw if sh == 0 else pltpu.roll(xw, sh, axis=0)
                dws[i] = dws[i] + jnp.sum(ds0 * xs[8:8 + _CONV_ROWS], axis=0, keepdims=True)
        for i in range(DN_CONV):
            dw_ref[i:i + 1, :] = dws[i]

    blk = pl.BlockSpec((t, _CONV_COLS), lambda j: (0, j))
    wblk = pl.BlockSpec((DN_CONV, _CONV_COLS), lambda j: (0, j))
    return pl.pallas_call(
        body, name="conv_silu_bwd", grid=(ch // _CONV_COLS,), in_specs=[blk, wblk, blk], out_specs=[blk, wblk],
        out_shape=[jax.ShapeDtypeStruct((t, ch), _CDT), jax.ShapeDtypeStruct((DN_CONV, ch), F32)],
        compiler_params=_cparams(("parallel",)))(x, w, dy)


_PA_ROWS = 512


def _bmm(a, b, spec, exact=False):
    if exact:
        return jnp.einsum(spec, a, b, precision=_HI, preferred_element_type=F32)
    return jnp.einsum(spec, a.astype(_CDT), b.astype(_CDT), preferred_element_type=F32)


def _split16(a):
    hi = a.astype(jnp.bfloat16)
    return hi, (a - hi.astype(F32)).astype(jnp.bfloat16)


def _bmm3(a, b, spec):
    ah, al = _split16(a)
    bh, bl = _split16(b)
    e = lambda p, q: jnp.einsum(spec, p, q, preferred_element_type=F32)
    return e(ah, bh) + (e(ah, bl) + e(al, bh))


def _tri_inverse(l_mat, eye):
    pw = -l_mat
    t_inv = eye + pw
    for _ in range(5):
        pw = _bmm3(pw, pw, 'bij,bjk->bik')
        t_inv = t_inv + _bmm3(t_inv, pw, 'bij,bjk->bik')
    return t_inv


@jax.custom_vjp
def _tri_inverse_saved(l_mat, t_saved):
    return t_saved


def _tri_inverse_saved_fwd(l_mat, t_saved):
    return t_saved, t_saved


def _tri_inverse_saved_bwd(t_saved, dt):
    left = _bmm3(t_saved, dt, 'bji,bjk->bik')
    return -_bmm3(left, t_saved, 'bij,bkj->bik'), jnp.zeros_like(t_saved)


_tri_inverse_saved.defvjp(_tri_inverse_saved_fwd, _tri_inverse_saved_bwd)


def _phase_a(h, q, k, v, ba, alog, dtb, t_saved=None):
    r = q.shape[0]
    nb = r // DN_CHUNK
    c = DN_CHUNK
    lane = lax.broadcasted_iota(jnp.int32, (1, LANE), 1)
    selb = (lane == h).astype(F32)
    sela = (lane == h + HEADS).astype(F32)
    b_raw = jnp.sum(ba * selb, axis=1, keepdims=True)
    a_raw = jnp.sum(ba * sela, axis=1, keepdims=True)
    al = jnp.sum(alog * selb, axis=1, keepdims=True)
    dt = jnp.sum(dtb * selb, axis=1, keepdims=True)
    beta = jnp.broadcast_to(_sigmoid(b_raw), (r, LANE))
    g = jnp.broadcast_to(-jnp.exp(al) * _softplus(a_raw + dt), (r, LANE))
    qn = q * lax.rsqrt(jnp.sum(q * q, -1, keepdims=True) + 1e-6) * (DN_DK ** -0.5)
    kn = k * lax.rsqrt(jnp.sum(k * k, -1, keepdims=True) + 1e-6)
    q3, k3, v3 = qn.reshape(nb, c, LANE), kn.reshape(nb, c, LANE), v.reshape(nb, c, LANE)
    b3, g3 = beta.reshape(nb, c, LANE), g.reshape(nb, c, LANE)
    ri = lax.broadcasted_iota(jnp.int32, (nb, c, c), 1)
    ci = lax.broadcasted_iota(jnp.int32, (nb, c, c), 2)
    tril, strict = ri >= ci, ri > ci
    gc = _bmm(tril.astype(F32), g3, 'bij,bjd->bid', exact=True)
    onehot = (lax.broadcasted_iota(jnp.int32, (nb, c, LANE), 2) == 0).astype(F32)
    g_row = _bmm(onehot, gc, 'bid,bjd->bij', exact=True)
    diff = gc[:, :, :c] - g_row
    decay = jnp.where(tril, jnp.exp(jnp.where(tril, diff, 0.0)), 0.0)
    kb = k3 * b3
    l_mat = jnp.where(strict, _bmm(kb, k3, 'bid,bjd->bij') * decay, 0.0)
    if t_saved is None:
        t_inv = _tri_inverse(l_mat, (ri == ci).astype(F32))
    else:
        t_inv = _tri_inverse_saved(l_mat, t_saved.reshape(nb, c, c))
    eg = jnp.exp(gc)
    u = _bmm(t_inv, v3 * b3, 'bij,bje->bie')
    w = _bmm(t_inv, kb * eg, 'bij,bje->bie')
    intra = jnp.where(tril, _bmm(q3, k3, 'bid,bjd->bij') * decay, 0.0)
    qd = q3 * eg
    gl = jnp.sum(g3, axis=1, keepdims=True)
    kt = k3 * jnp.exp(gl - gc)
    outs = (u.reshape(r, LANE), w.reshape(r, LANE), qd.reshape(r, LANE), kt.reshape(r, LANE),
            intra.reshape(r, c), gl.reshape(nb, LANE))
    return outs + (t_inv.reshape(r, c),) if t_saved is None else outs


def _pa_specs(t):
    rr = min(_PA_ROWS, t)
    nb = rr // DN_CHUNK
    qkv = [pl.BlockSpec((rr, LANE), functools.partial(lambda i, h, o: (i, o + h), o=o)) for o in (0, HEADS, 2 * HEADS)]
    ba = pl.BlockSpec((rr, LANE), lambda i, h: (i, 3))
    vec = pl.BlockSpec((1, LANE), lambda i, h: (0, 0))
    row = pl.BlockSpec((rr, LANE), lambda i, h: (i, h))
    intra = pl.BlockSpec((None, rr, DN_CHUNK), lambda i, h: (h, i, 0))
    gl = pl.BlockSpec((nb, LANE), lambda i, h: (i, h))
    return rr, qkv, ba, vec, row, intra, gl


def _delta_local(qkv_act, pm, alog, dtb):
    t = qkv_act.shape[0]
    rr, qkv, ba, vec, row, intra, gl = _pa_specs(t)

    def body(q, k, v, b, al, dt, *outs):
        vals = _phase_a(pl.program_id(1), q[...], k[...], v[...], b[...], al[...], dt[...])
        for o, val in zip(outs, vals):
            o[...] = val

    wide = jax.ShapeDtypeStruct((t, HEADS * LANE), F32)
    sq = jax.ShapeDtypeStruct((HEADS, t, DN_CHUNK), F32)
    return pl.pallas_call(
        body, name="delta_local", grid=(t // rr, HEADS), in_specs=qkv + [ba, vec, vec],
        out_specs=[row] * 4 + [intra, gl, intra],
        out_shape=[wide] * 4 + [sq, jax.ShapeDtypeStruct((t // DN_CHUNK, HEADS * LANE), F32), sq],
        compiler_params=_cparams(("parallel", "parallel")))(qkv_act, qkv_act, qkv_act, pm, alog, dtb)


def _delta_local_bwd(qkv_act, pm, alog, dtb, t_inv, du, dw, dqd, dkt, dintra, dgl):
    t = qkv_act.shape[0]
    rr, qkv, ba, vec, row, intra, gl = _pa_specs(t)

    def body(q, k, v, b, al, dt, ti, du_r, dw_r, dqd_r, dkt_r, di_r, dgl_r, dq_o, dk_o, dv_o, dba_o, dal_o, ddt_o):
        i, h = pl.program_id(0), pl.program_id(1)
        t_saved = ti[...]
        _, vjp = jax.vjp(lambda *a: _phase_a(h, *a, t_saved=t_saved), q[...], k[...], v[...], b[...], al[...], dt[...])
        dq, dk, dv, dba, dal, ddt = vjp((du_r[...], dw_r[...], dqd_r[...], dkt_r[...], di_r[...], dgl_r[...]))
        dq_o[...], dk_o[...], dv_o[...] = dq, dk, dv

        @pl.when(h == 0)
        def _():
            dba_o[...] = jnp.zeros_like(dba_o)

        @pl.when((h == 0) & (i == 0))
        def _():
            dal_o[...] = jnp.zeros_like(dal_o)
            ddt_o[...] = jnp.zeros_like(ddt_o)

        dba_o[...] += dba
        dal_o[...] += dal
        ddt_o[...] += ddt

    wide = jax.ShapeDtypeStruct((t, HEADS * LANE), F32)
    vshape = jax.ShapeDtypeStruct((1, LANE), F32)
    return pl.pallas_call(
        body, name="delta_local_bwd", grid=(t // rr, HEADS),
        in_specs=qkv + [ba, vec, vec, intra] + [row] * 4 + [intra, gl],
        out_specs=[row] * 3 + [pl.BlockSpec((rr, LANE), lambda i, h: (i, 0)), vec, vec],
        out_shape=[wide] * 3 + [jax.ShapeDtypeStruct((t, LANE), F32), vshape, vshape],
        compiler_params=_cparams(("arbitrary", "arbitrary")))(
            qkv_act, qkv_act, qkv_act, pm, alog, dtb, t_inv, du, dw, dqd, dkt, dintra, dgl)


_SCAN_ROWS = 512


def _dot(a, b, dn):
    return lax.dot_general(a.astype(_CDT), b.astype(_CDT), (dn, ((), ())), preferred_element_type=F32)


_NN = ((1,), (0,))
_NT = ((1,), (1,))
_TN = ((0,), (0,))


def _delta_scan(u, w, qd, kt, intra, gl):
    t = u.shape[0]
    rr = min(_SCAN_ROWS, t)
    nc = rr // DN_CHUNK

    def body(u_ref, w_ref, qd_ref, kt_ref, a_ref, gl_ref, o_ref, sall_ref, s_scr):
        @pl.when(pl.program_id(0) == 0)
        def _():
            s_scr[...] = jnp.zeros_like(s_scr)

        def chunk(c, carry):
            r0 = pl.multiple_of(c * DN_CHUNK, DN_CHUNK)
            rows = pl.ds(r0, DN_CHUNK)
            e = jnp.exp(gl_ref[pl.ds(c, 1), :])
            for h in range(HEADS):
                cs = slice(h * LANE, (h + 1) * LANE)
                s = s_scr[h]
                sall_ref[c, h] = s
                v_new = u_ref[rows, cs] - _dot(w_ref[rows, cs], s, _NN)
                o_ref[rows, cs] = _dot(qd_ref[rows, cs], s, _NN) + _dot(a_ref[h, rows, :], v_new, _NN)
                s_scr[h] = s * e[:, cs] + _dot(kt_ref[rows, cs], v_new, _TN)
            return carry

        lax.fori_loop(0, nc, chunk, 0)

    row = pl.BlockSpec((rr, HEADS * LANE), lambda i: (i, 0))
    return pl.pallas_call(
        body, name="delta_scan", grid=(t // rr,),
        in_specs=[row] * 4 + [pl.BlockSpec((HEADS, rr, DN_CHUNK), lambda i: (0, i, 0)),
                              pl.BlockSpec((nc, HEADS * LANE), lambda i: (i, 0))],
        out_specs=[row, pl.BlockSpec((nc, HEADS, LANE, LANE), lambda i: (i, 0, 0, 0))],
        out_shape=[jax.ShapeDtypeStruct((t, HEADS * LANE), F32),
                   jax.ShapeDtypeStruct((t // DN_CHUNK, HEADS, LANE, LANE), F32)],
        scratch_shapes=[pltpu.VMEM((HEADS, LANE, LANE), F32)],
        compiler_params=_cparams(("arbitrary",)))(u, w, qd, kt, intra, gl)


def _delta_scan_bwd(u, w, qd, kt, intra, gl, sall, do):
    t = u.shape[0]
    rr = min(_SCAN_ROWS, t)
    nc = rr // DN_CHUNK
    ng = t // rr

    def body(u_ref, w_ref, qd_ref, kt_ref, a_ref, gl_ref, sall_ref, do_ref,
             du_ref, dw_ref, dqd_ref, dkt_ref, da_ref, dgl_ref, ds_scr):
        @pl.when(pl.program_id(0) == 0)
        def _():
            ds_scr[...] = jnp.zeros_like(ds_scr)

        def chunk(cc, carry):
            c = nc - 1 - cc
            r0 = pl.multiple_of(c * DN_CHUNK, DN_CHUNK)
            rows = pl.ds(r0, DN_CHUNK)
            e = jnp.exp(gl_ref[pl.ds(c, 1), :])
            for h in range(HEADS):
                cs = slice(h * LANE, (h + 1) * LANE)
                s = sall_ref[c, h]
                ds_out = ds_scr[h]
                w_c, kt_c, qd_c, do_c = w_ref[rows, cs], kt_ref[rows, cs], qd_ref[rows, cs], do_ref[rows, cs]
                a_c = a_ref[h, rows, :]
                v_new = u_ref[rows, cs] - _dot(w_c, s, _NN)
                dv_new = _dot(a_c, do_c, _TN) + _dot(kt_c, ds_out, _NN)
                da_ref[h, rows, :] = _dot(do_c, v_new, _NT)
                dqd_ref[rows, cs] = _dot(do_c, s, _NT)
                dkt_ref[rows, cs] = _dot(v_new, ds_out, _NT)
                du_ref[rows, cs] = dv_new
                dw_ref[rows, cs] = -_dot(dv_new, s, _NT)
                eh = e[:, cs]
                dgl_ref[pl.ds(pl.multiple_of(c * 8, 8), 8), cs] = jnp.broadcast_to(
                    jnp.sum(ds_out * s, axis=0, keepdims=True) * eh, (8, LANE))
                ds_scr[h] = ds_out * eh + _dot(qd_c, do_c, _TN) - _dot(w_c, dv_new, _TN)
            return carry

        lax.fori_loop(0, nc, chunk, 0)

    rev = lambda i: (ng - 1 - i, 0)
    row = pl.BlockSpec((rr, HEADS * LANE), rev)
    a_spec = pl.BlockSpec((HEADS, rr, DN_CHUNK), lambda i: (0, ng - 1 - i, 0))
    gl_spec = pl.BlockSpec((nc, HEADS * LANE), rev)
    wide = jax.ShapeDtypeStruct((t, HEADS * LANE), F32)
    outs = pl.pallas_call(
        body, name="delta_scan_bwd", grid=(ng,),
        in_specs=[row] * 4 + [a_spec, gl_spec, pl.BlockSpec((nc, HEADS, LANE, LANE), lambda i: (ng - 1 - i, 0, 0, 0)), row],
        out_specs=[row] * 4 + [a_spec, pl.BlockSpec((nc * 8, HEADS * LANE), rev)],
        out_shape=[wide] * 4 + [jax.ShapeDtypeStruct((HEADS, t, DN_CHUNK), F32),
                                jax.ShapeDtypeStruct((t // DN_CHUNK * 8, HEADS * LANE), F32)],
        scratch_shapes=[pltpu.VMEM((HEADS, LANE, LANE), F32)],
        compiler_params=_cparams(("arbitrary",)))(u, w, qd, kt, intra, gl, sall, do)
    return tuple(outs[:5]) + (outs[5].reshape(t // DN_CHUNK, 8, HEADS * LANE)[:, 0, :],)


_ATT_TILE = 512


def _att_scores(ql, qr, ckv_ref, kr_ref, j, tk, masked, q0):
    ks = pl.ds(pl.multiple_of(j * tk, tk), tk)
    s = (_dot(ql, ckv_ref[ks, :], _NT) + _dot(qr, kr_ref[ks, :], _NT)) * ATT_SCALE
    if masked:
        qi = q0 + lax.broadcasted_iota(jnp.int32, s.shape, 0)
        ki = j * tk + lax.broadcasted_iota(jnp.int32, s.shape, 1)
        s = jnp.where(ki <= qi, s, NEG_BIG)
    return s, ks


def _attention(ql, qr, ckv, kr):
    t = ckv.shape[0]
    tq = min(_ATT_TILE, t)

    def body(ql_ref, qr_ref, ckv_ref, kr_ref, o_ref, lse_ref, m_scr, l_scr, acc_scr):
        qi = pl.program_id(1)
        q_lat, q_rope = ql_ref[...], qr_ref[...]
        m_scr[...] = jnp.full_like(m_scr, NEG_BIG)
        l_scr[...] = jnp.zeros_like(l_scr)
        acc_scr[...] = jnp.zeros_like(acc_scr)

        def step(j, masked):
            s, ks = _att_scores(q_lat, q_rope, ckv_ref, kr_ref, j, tq, masked, qi * tq)
            m_old = m_scr[...]
            m_new = jnp.maximum(m_old, jnp.max(s, axis=-1, keepdims=True))
            p = jnp.exp(s - m_new)
            alpha = jnp.exp(m_old - m_new)
            l_scr[...] = l_scr[...] * alpha + jnp.sum(p, axis=-1, keepdims=True)
            acc_scr[...] = acc_scr[...] * alpha + _dot(p, ckv_ref[ks, :], _NN)
            m_scr[...] = m_new

        def loop_body(j, carry):
            step(j, False)
            return carry

        lax.fori_loop(0, qi, loop_body, 0)
        step(qi, True)
        o_ref[...] = acc_scr[...] / l_scr[...]
        lse_ref[...] = m_scr[...] + jnp.log(l_scr[...])

    return pl.pallas_call(
        body, name="attention", grid=(HEADS, t // tq),
        in_specs=[pl.BlockSpec((None, tq, KV_LORA), lambda h, i: (h, i, 0)),
                  pl.BlockSpec((tq, ROPE_PAD), lambda h, i: (i, h)),
                  pl.BlockSpec((t, KV_LORA), lambda h, i: (0, 0)),
                  pl.BlockSpec((t, ROPE_PAD), lambda h, i: (0, 0))],
        out_specs=[pl.BlockSpec((None, tq, KV_LORA), lambda h, i: (h, i, 0)),
                   pl.BlockSpec((None, tq, 1), lambda h, i: (h, i, 0))],
        out_shape=[jax.ShapeDtypeStruct((HEADS, t, KV_LORA), F32), jax.ShapeDtypeStruct((HEADS, t, 1), F32)],
        scratch_shapes=[pltpu.VMEM((tq, 1), F32), pltpu.VMEM((tq, 1), F32), pltpu.VMEM((tq, KV_LORA), F32)],
        compiler_params=_cparams(("parallel", "parallel")))(ql, qr, ckv, kr)


def _attention_bwd(ql, qr, ckv, kr, out, lse, dout):
    t = ckv.shape[0]
    tq = min(_ATT_TILE, t)

    def body(ql_ref, qr_ref, ckv_ref, kr_ref, o_ref, lse_ref, do_ref, dql_ref, dqr_ref, dckv_ref, dkr_ref,
             dql_scr, dqr_scr):
        h, qi = pl.program_id(0), pl.program_id(1)

        @pl.when((h == 0) & (qi == 0))
        def _():
            dckv_ref[...] = jnp.zeros_like(dckv_ref)
            dkr_ref[...] = jnp.zeros_like(dkr_ref)

        q_lat, q_rope, d_o = ql_ref[...], qr_ref[...], do_ref[...]
        lse_v = lse_ref[...]
        dsum = jnp.sum(d_o * o_ref[...], axis=-1, keepdims=True)
        dql_scr[...] = jnp.zeros_like(dql_scr)
        dqr_scr[...] = jnp.zeros_like(dqr_scr)

        def step(j, masked):
            s, ks = _att_scores(q_lat, q_rope, ckv_ref, kr_ref, j, tq, masked, qi * tq)
            p = jnp.exp(s - lse_v)
            kv = ckv_ref[ks, :]
            dp = _dot(d_o, kv, _NT)
            ds = p * (dp - dsum) * ATT_SCALE
            dql_scr[...] += _dot(ds, kv, _NN)
            dqr_scr[...] += _dot(ds, kr_ref[ks, :], _NN)
            dckv_ref[ks, :] += _dot(p, d_o, _TN) + _dot(ds, q_lat, _TN)
            dkr_ref[ks, :] += _dot(ds, q_rope, _TN)

        def loop_body(j, carry):
            step(j, False)
            return carry

        lax.fori_loop(0, qi, loop_body, 0)
        step(qi, True)
        dql_ref[...] = dql_scr[...].astype(dql_ref.dtype)
        dqr_ref[...] = dqr_scr[...]

    lat = pl.BlockSpec((None, tq, KV_LORA), lambda h, i: (h, i, 0))
    rope = pl.BlockSpec((tq, ROPE_PAD), lambda h, i: (i, h))
    kfull = pl.BlockSpec((t, KV_LORA), lambda h, i: (0, 0))
    rfull = pl.BlockSpec((t, ROPE_PAD), lambda h, i: (0, 0))
    return pl.pallas_call(
        body, name="attention_bwd", grid=(HEADS, t // tq),
        in_specs=[lat, rope, kfull, rfull, lat, pl.BlockSpec((None, tq, 1), lambda h, i: (h, i, 0)), lat],
        out_specs=[lat, rope, kfull, rfull],
        out_shape=[jax.ShapeDtypeStruct((HEADS, t, KV_LORA), _CDT), jax.ShapeDtypeStruct((t, HEADS * ROPE_PAD), F32),
                   jax.ShapeDtypeStruct((t, KV_LORA), F32), jax.ShapeDtypeStruct((t, ROPE_PAD), F32)],
        scratch_shapes=[pltpu.VMEM((tq, KV_LORA), F32), pltpu.VMEM((tq, ROPE_PAD), F32)],
        compiler_params=_cparams(("arbitrary", "arbitrary")))(ql, qr, ckv, kr, out, lse, dout)


def _gated_norm(o, z, w):
    return _rms_norm(o, w) * _silu(z)


def _mla_pre(ckv, krp, cq, cosb, sinb, qw, kw):
    return _rms_norm(cq, qw), _rms_norm(ckv, kw), _rope(krp, cosb, sinb)


def _merge(gg, y_dn, y_mla):
    return _sigmoid(gg[:, :D_MODEL]) * y_dn + _sigmoid(gg[:, D_MODEL:]) * y_mla


def _ln1(xv, attn_out, g, b):
    return _layer_norm(ALPHA * xv + attn_out, g, b)


def _final(h1, ffn, gate_pre, ple_proj, g, b):
    return _layer_norm(ALPHA * h1 + ffn + _sigmoid(gate_pre) * ple_proj, g, b)


def _swiglu(gt, up):
    return _silu(gt) * up


def _local_step(x, p, cosb, sinb, target, wt, sp):
    t = x.shape[0]
    bf = _CDT
    xb = x.astype(bf)
    g = {}

    qkv_pre = _mm2(xb, wt['qkv'], name="f_qkv")
    z = _mm2(xb, wt['z'], name="f_z")
    gg = _mm2(xb, wt['gg'], name="f_gg")
    pm = _mm2(xb, wt['mla'], name="f_mla")
    qkv_act = _conv_silu(qkv_pre, sp['conv_w'])
    u, w_, qd, kt, intra, gl, t_inv = _delta_local(qkv_act, pm, sp['a_log'], sp['dt_bias'])
    o_dn, sall = _delta_scan(u, w_, qd, kt, intra, gl)
    (og,) = _rowwise(lambda h, o, zz, w: (_gated_norm(o, zz, w),),
                     [(o_dn, LANE, 0, True), (z, LANE, 0, True)], [sp['dn_norm_w']],
                     [(D_MODEL, LANE, True, bf)], name="f_gated_norm", tm=512, heads=HEADS)
    y_dn = _mm2(og, wt['br_dn'], name="f_br_dn")

    c_q, c_kv, k_rope = _rowwise(
        lambda h, *a: _mla_pre(*a),
        [(pm, KV_LORA, 0, False), (pm, ROPE_PAD, 2, False), (pm, Q_LORA, 2, False),
         (cosb, ROPE_PAD, 0, False), (sinb, ROPE_PAD, 0, False)],
        [sp['q_norm_w'], sp['kv_norm_w']],
        [(Q_LORA, Q_LORA, False, bf), (KV_LORA, KV_LORA, False, bf), (ROPE_PAD, ROPE_PAD, False, bf)], name="f_mla_pre")
    q_nope = _mm2(c_q, wt['uq_nope'], name="f_uq_nope", out_dtype=bf)
    q_rope_pre = _mm2(c_q, wt['uq_rope'], name="f_uq_rope")
    (q_rope,) = _rowwise(lambda h, q, c, s: (_rope(q, c, s),),
                         [(q_rope_pre, HEADS * ROPE_PAD, 0, False), (cosb, ROPE_PAD, 0, False), (sinb, ROPE_PAD, 0, False)],
                         [], [(HEADS * ROPE_PAD, HEADS * ROPE_PAD, False, bf)], name="f_q_rope")
    q_lat = _mm(q_nope, wt['uk'], name="f_q_lat", tb=True, heads=HEADS, a_head='col', b_head='lead', out_head='lead',
                dims=(t, KV_LORA, NOPE), out_dtype=bf)
    out_lat, lse = _attention(q_lat, q_rope, c_kv, k_rope)
    o_mla = _mm(out_lat, wt['uv'], name="f_o_mla", heads=HEADS, a_head='lead', b_head='lead', out_head='col',
                dims=(t, NOPE, KV_LORA), out_dtype=bf)
    y_mla = _mm2(o_mla, wt['br_mla'], name="f_br_mla")

    (mixed,) = _rowwise(lambda h, *a: (_merge(*a),),
                        [(gg, 2 * D_MODEL, 0, False), (y_dn, D_MODEL, 0, False), (y_mla, D_MODEL, 0, False)],
                        [], [(D_MODEL, D_MODEL, False, bf)], name="f_merge")
    attn_out = _mm2(mixed, wt['o'], name="f_o")
    h1, h1b = _rowwise(lambda h, *a: (_ln1(*a),) * 2, [(x, D_MODEL, 0, False), (attn_out, D_MODEL, 0, False)],
                       [sp['ln1_g'], sp['ln1_b']], [(D_MODEL, D_MODEL, False, F32), (D_MODEL, D_MODEL, False, bf)],
                       name="f_ln1")
    ffn_in = _mm2(h1b, wt['ffn_in'], name="f_ffn_in")
    (act,) = _rowwise(lambda h, gt, up: (_swiglu(gt, up),),
                      [(ffn_in, FFN_HIDDEN, 0, False), (ffn_in, FFN_HIDDEN, 1, False)], [],
                      [(FFN_HIDDEN, FFN_HIDDEN, False, bf)], name="f_swiglu")
    ffn = _mm2(act, wt['ffn_out'], name="f_ffn_out")
    gate_pre = _mm2(h1b, wt['ple_gate'], name="f_ple_gate")
    pb = p.astype(bf)
    ple_proj = _mm2(pb, wt['ple'], name="f_ple")

    def final_fn(h, h1v, ffnv, gpv, ppv, tgt, gv, bv):
        y, vjp = jax.vjp(_final, h1v, ffnv, gpv, ppv, gv, bv)
        err = y - tgt
        dh1, dffn, dgp, dpp, dg, db = vjp(err * (1.0 / D_MODEL))
        sq = err * err
        lanes = sq[:, :LANE]
        for j in range(1, D_MODEL // LANE):
            lanes = lanes + sq[:, j * LANE:(j + 1) * LANE]
        loss = jnp.sum(lanes, axis=0, keepdims=True) * (0.5 / D_MODEL)
        return dffn, dffn, dgp, dpp, dg, db, loss

    dpre2, dpre2b, dgate_pre, dple_proj, g['ln2_g'], g['ln2_b'], loss_lanes = _rowwise(
        final_fn, [(a, D_MODEL, 0, False) for a in (h1, ffn, gate_pre, ple_proj, target)],
        [sp['ln2_g'], sp['ln2_b']],
        [(D_MODEL, D_MODEL, False, F32)] + [(D_MODEL, D_MODEL, False, bf)] * 3,
        [(1, D_MODEL), (1, D_MODEL), (1, LANE)], name="b_final")

    g['ple'] = _mm2(pb, dple_proj, ta=True, name="g_ple")
    g['ple_gate'] = _mm2(h1b, dgate_pre, ta=True, name="g_ple_gate")
    g['ffn_out'] = _mm2(act, dpre2b, ta=True, name="g_ffn_out")
    dact = _mm2(dpre2b, wt['ffn_out'], tb=True, name="b_dact")

    def swiglu_bwd(h, gt, up, d):
        _, vjp = jax.vjp(_swiglu, gt, up)
        dgt, dup = vjp(d)
        return (jnp.concatenate([dgt, dup], axis=1),)

    (dffn_in,) = _rowwise(swiglu_bwd, [(ffn_in, FFN_HIDDEN, 0, False), (ffn_in, FFN_HIDDEN, 1, False),
                                       (dact, FFN_HIDDEN, 0, False)], [],
                          [(2 * FFN_HIDDEN, 2 * FFN_HIDDEN, False, bf)], name="b_swiglu")
    g['ffn_in'] = _mm2(h1b, dffn_in, ta=True, name="g_ffn_in")
    dh1 = _mm2(dffn_in, wt['ffn_in'], tb=True, name="b_dh1_ffn", add=dpre2, add_scale=ALPHA)
    dh1 = _mm2(dgate_pre, wt['ple_gate'], tb=True, name="b_dh1_gate", add=dh1)

    def ln1_bwd(h, xv, ao, d, gv, bv):
        _, vjp = jax.vjp(_ln1, xv, ao, gv, bv)
        _, dao, dg, db = vjp(d)
        return dao, dao, dg, db

    dpre1, dpre1b, g['ln1_g'], g['ln1_b'] = _rowwise(
        ln1_bwd, [(x, D_MODEL, 0, False), (attn_out, D_MODEL, 0, False), (dh1, D_MODEL, 0, False)],
        [sp['ln1_g'], sp['ln1_b']], [(D_MODEL, D_MODEL, False, F32), (D_MODEL, D_MODEL, False, bf)],
        [(1, D_MODEL), (1, D_MODEL)], name="b_ln1")

    g['o'] = _mm2(mixed, dpre1b, ta=True, name="g_o")
    dmixed = _mm2(dpre1b, wt['o'], tb=True, name="b_dmixed")

    def merge_bwd(h, ggv, yd, ym, d):
        _, vjp = jax.vjp(_merge, ggv, yd, ym)
        return vjp(d)

    dgg, dy_dn, dy_mla = _rowwise(
        merge_bwd, [(gg, 2 * D_MODEL, 0, False), (y_dn, D_MODEL, 0, False), (y_mla, D_MODEL, 0, False),
                    (dmixed, D_MODEL, 0, False)], [],
        [(2 * D_MODEL, 2 * D_MODEL, False, bf), (D_MODEL, D_MODEL, False, bf), (D_MODEL, D_MODEL, False, bf)],
        name="b_merge")
    g['br_dn'] = _mm2(og, dy_dn, ta=True, name="g_br_dn")
    dog = _mm2(dy_dn, wt['br_dn'], tb=True, name="b_dog")
    g['br_mla'] = _mm2(o_mla, dy_mla, ta=True, name="g_br_mla")
    do_mla = _mm2(dy_mla, wt['br_mla'], tb=True, name="b_do_mla", out_dtype=bf)

    dout_lat = _mm(do_mla, wt['uv'], name="b_dout_lat", tb=True, heads=HEADS, a_head='col', b_head='lead',
                   out_head='lead', dims=(t, KV_LORA, NOPE))
    g['uv'] = _mm(out_lat, do_mla, name="g_uv", ta=True, heads=HEADS, a_head='lead', b_head='col', out_head='lead',
                  dims=(KV_LORA, NOPE, t))
    dq_lat, dq_rope, dckv_att, dkr_att = _attention_bwd(q_lat, q_rope, c_kv, k_rope, out_lat, lse, dout_lat)
    dq_nope = _mm(dq_lat, wt['uk'], name="b_dq_nope", heads=HEADS, a_head='lead', b_head='lead', out_head='col',
                  dims=(t, NOPE, KV_LORA), out_dtype=bf)
    g['uk'] = _mm(dq_lat, q_nope, name="g_uk", ta=True, heads=HEADS, a_head='lead', b_head='col', out_head='lead',
                  dims=(KV_LORA, NOPE, t))
    (dq_rope_pre,) = _rowwise(lambda h, d, c, s: (_rope_bwd(d, c, s),),
                              [(dq_rope, HEADS * ROPE_PAD, 0, False), (cosb, ROPE_PAD, 0, False), (sinb, ROPE_PAD, 0, False)],
                              [], [(HEADS * ROPE_PAD, HEADS * ROPE_PAD, False, bf)], name="b_q_rope")
    g['uq_nope'] = _mm2(c_q, dq_nope, ta=True, name="g_uq_nope")
    g['uq_rope'] = _mm2(c_q, dq_rope_pre, ta=True, name="g_uq_rope")
    dc_q = _mm2(dq_nope, wt['uq_nope'], tb=True, name="b_dcq_nope")
    dc_q = _mm2(dq_rope_pre, wt['uq_rope'], tb=True, name="b_dcq_rope", add=dc_q)

    def gated_norm_bwd(h, o, zz, d, w):
        _, vjp = jax.vjp(_gated_norm, o, zz, w)
        return vjp(d)

    do_dn, dz, g['dn_norm_w'] = _rowwise(
        gated_norm_bwd, [(o_dn, LANE, 0, True), (z, LANE, 0, True), (dog, LANE, 0, True)], [sp['dn_norm_w']],
        [(D_MODEL, LANE, True, F32), (D_MODEL, LANE, True, bf)], [(1, LANE)], name="b_gated_norm", tm=512, heads=HEADS)
    du, dw, dqd, dkt, dintra, dgl = _delta_scan_bwd(u, w_, qd, kt, intra, gl, sall, do_dn)
    dq_a, dk_a, dv_a, dba, g['a_log'], g['dt_bias'] = _delta_local_bwd(
        qkv_act, pm, sp['a_log'], sp['dt_bias'], t_inv, du, dw, dqd, dkt, dintra, dgl)
    dqkv_act = jnp.concatenate([dq_a, dk_a, dv_a], axis=1)
    dqkv_pre, g['conv_w'] = _conv_silu_bwd(qkv_pre, sp['conv_w'], dqkv_act)

    def mla_pre_bwd(h, ckv, krp, cq, cosv, sinv, dcq, dckv, dkr, qw, kw):
        _, vjp = jax.vjp(lambda a, b, c, d, e: (_rms_norm(c, d), _rms_norm(a, e)), ckv, krp, cq, qw, kw)
        dckv_p, _, dcq_p, dqw, dkw = vjp((dcq, dckv))
        dkr_p = _rope_bwd(dkr, cosv, sinv)
        dpm = jnp.concatenate([dckv_p, dkr_p, jnp.zeros((ckv.shape[0], 3 * LANE), F32), dcq_p], axis=1)
        return dpm, dqw, dkw

    dpm_main, g['q_norm_w'], g['kv_norm_w'] = _rowwise(
        mla_pre_bwd,
        [(pm, KV_LORA, 0, False), (pm, ROPE_PAD, 2, False), (pm, Q_LORA, 2, False),
         (cosb, ROPE_PAD, 0, False), (sinb, ROPE_PAD, 0, False),
         (dc_q, Q_LORA, 0, False), (dckv_att, KV_LORA, 0, False), (dkr_att, ROPE_PAD, 0, False)],
        [sp['q_norm_w'], sp['kv_norm_w']], [(1152, 1152, False, F32)], [(1, Q_LORA), (1, KV_LORA)], name="b_mla_pre")
    (dpm,) = _rowwise(
        lambda h, a, b: (jnp.concatenate([a[:, :3 * LANE], b, a[:, 4 * LANE:]], axis=1),),
        [(dpm_main, 1152, 0, False), (dba, LANE, 0, False)], [], [(1152, 1152, False, bf)], name="b_dpm")

    g['qkv'] = _mm2(xb, dqkv_pre, ta=True, name="g_qkv")
    g['z'] = _mm2(xb, dz, ta=True, name="g_z")
    g['gg'] = _mm2(xb, dgg, ta=True, name="g_gg")
    g['mla'] = _mm2(xb, dpm, ta=True, name="g_mla")
    dx = _mm2(dqkv_pre, wt['qkv'], tb=True, name="b_dx_qkv", add=dpre1, add_scale=ALPHA)
    dx = _mm2(dz, wt['z'], tb=True, name="b_dx_z", add=dx)
    dx = _mm2(dgg, wt['gg'], tb=True, name="b_dx_gg", add=dx)
    dx = _mm2(dpm, wt['mla'], tb=True, name="b_dx_mla", add=dx)
    return loss_lanes, dx, g


_IN_SIZES = (QKV_W, HEADS * DN_DK, HEADS, HEADS, Q_LORA, KV_LORA, ROPE, D_MODEL, D_MODEL)


def _rope_tables(positions):
    inv_freq = ROPE_BASE ** (-jnp.arange(0, ROPE, 2, dtype=F32) / ROPE)
    ang = positions.astype(F32)[:, None] * inv_freq
    cos, sin = jnp.cos(ang), jnp.sin(ang)
    zeros = jnp.zeros((positions.shape[0], ROPE_PAD - ROPE), F32)
    return jnp.concatenate([cos, cos, zeros], axis=1), jnp.concatenate([-sin, sin, zeros], axis=1)


def _prep_weights(full):
    w_in = full['w_in']
    dt = w_in.dtype
    offs = [0]
    for s in _IN_SIZES:
        offs.append(offs[-1] + s)
    qkv, z, wb, wa, cq, ckv, kr, gd, gm = [w_in[:, offs[i]:offs[i + 1]] for i in range(len(_IN_SIZES))]
    zc = lambda n: jnp.zeros((D_MODEL, n), dt)
    w_uq = full['w_uq']
    wt = {
        'qkv': qkv, 'z': z, 'gg': jnp.concatenate([gd, gm], axis=1),
        'mla': jnp.concatenate([ckv, kr, zc(ROPE_PAD - ROPE), wb, wa, zc(LANE - 2 * HEADS), zc(2 * LANE), cq], axis=1),
        'uq_nope': w_uq[:, :, :NOPE].reshape(Q_LORA, HEADS * NOPE),
        'uq_rope': jnp.pad(w_uq[:, :, NOPE:], ((0, 0), (0, 0), (0, ROPE_PAD - ROPE))).reshape(Q_LORA, HEADS * ROPE_PAD),
        'uk': jnp.transpose(full['w_uk'], (1, 0, 2)), 'uv': jnp.transpose(full['w_uv'], (1, 0, 2)),
        'br_dn': full['w_br_dn'], 'br_mla': full['w_br_mla'], 'o': full['w_o'], 'ffn_in': full['w_ffn_in'],
        'ffn_out': full['w_ffn_out'], 'ple': full['w_ple'], 'ple_gate': full['w_ple_gate'],
    }
    return wt


def _prep_small(small):
    pad = lambda v: jnp.pad(v, (0, LANE - v.shape[0]))[None, :]
    return {
        'conv_w': small['conv_w'], 'a_log': pad(small['dn_a_log']), 'dt_bias': pad(small['dn_dt_bias']),
        'dn_norm_w': small['dn_norm_w'][None, :], 'q_norm_w': small['q_norm_w'][None, :],
        'kv_norm_w': small['kv_norm_w'][None, :], 'ln1_g': small['ln1_g'][None, :], 'ln1_b': small['ln1_b'][None, :],
        'ln2_g': small['ln2_g'][None, :], 'ln2_b': small['ln2_b'][None, :],
    }


def _unprep_grads(g):
    mla = g['mla']
    ba0 = KV_LORA + ROPE_PAD
    cq0 = ba0 + 3 * LANE
    w_in = jnp.concatenate([
        g['qkv'], g['z'], mla[:, ba0:ba0 + HEADS], mla[:, ba0 + HEADS:ba0 + 2 * HEADS], mla[:, cq0:cq0 + Q_LORA],
        mla[:, :KV_LORA], mla[:, KV_LORA:KV_LORA + ROPE], g['gg']], axis=1)
    w_uq = jnp.concatenate([g['uq_nope'].reshape(Q_LORA, HEADS, NOPE),
                            g['uq_rope'].reshape(Q_LORA, HEADS, ROPE_PAD)[:, :, :ROPE]], axis=2)
    return {
        'w_in': w_in, 'conv_w': g['conv_w'], 'dn_a_log': g['a_log'][0, :HEADS], 'dn_dt_bias': g['dt_bias'][0, :HEADS],
        'dn_norm_w': g['dn_norm_w'][0], 'q_norm_w': g['q_norm_w'][0], 'w_uq': w_uq, 'kv_norm_w': g['kv_norm_w'][0],
        'w_uk': jnp.transpose(g['uk'], (1, 0, 2)), 'w_uv': jnp.transpose(g['uv'], (1, 0, 2)),
        'w_br_dn': g['br_dn'], 'w_br_mla': g['br_mla'], 'w_o': g['o'], 'ln1_g': g['ln1_g'][0], 'ln1_b': g['ln1_b'][0],
        'w_ffn_in': g['ffn_in'], 'w_ffn_out': g['ffn_out'], 'w_ple': g['ple'], 'w_ple_gate': g['ple_gate'],
        'ln2_g': g['ln2_g'][0], 'ln2_b': g['ln2_b'][0],
    }


_FLAT_PIECES = (
    ('w_in', 1716, (D_MODEL, 1716), 1), ('w_ffn_in', 1408, (D_MODEL, 1408), 1), ('w_ffn_out', 704, (704, D_MODEL), 0),
    ('w_br_dn', 256, (256, D_MODEL), 0), ('w_br_mla', 256, (256, D_MODEL), 0), ('w_o', 256, (256, D_MODEL), 0),
    ('w_ple_gate', 256, (256, D_MODEL), 0), ('w_uq', 144, (96, HEADS, NOPE + ROPE), 0),
    ('w_uk', 64, (64, HEADS, NOPE), 0), ('w_uv', 64, (64, HEADS, NOPE), 0), ('w_ple', 64, (PLE_DIM, 256), 1),
)
_CONV_FLAT_ROWS = 6
_CONV_SHARD = QKV_W // N_SHARD


def _pad16(rows):
    return -(-rows // 16) * 16


def _flat_offsets():
    offs, o = {}, 0
    for name, rows, _, _ in _FLAT_PIECES:
        offs[name] = o
        o += _pad16(rows)
    return offs, o


def _pad_rows(part, axis):
    rows = part.shape[axis]
    widths = [(0, 0)] * part.ndim
    widths[axis] = (0, _pad16(rows) - rows)
    return jnp.pad(part, widths)


def _pack_shard(shards, conv_w):
    parts = [_pad_rows(shards[name].astype(_CDT).reshape(rows, FLAT_W), 0) for name, rows, _, _ in _FLAT_PIECES]
    conv = lax.bitcast_convert_type(conv_w, jnp.bfloat16).reshape(_CONV_FLAT_ROWS, FLAT_W).astype(_CDT)
    parts.append(_pad_rows(conv, 0))
    used = sum(p.shape[0] for p in parts)
    parts.append(jnp.zeros((FLAT_ROWS - used, FLAT_W), _CDT))
    return jnp.concatenate(parts, axis=0)


def _unpack_gathered(gathered):
    offs, used = _flat_offsets()
    full = {}
    for name, rows, shape, axis in _FLAT_PIECES:
        pieces = [gathered[s, offs[name]:offs[name] + rows].reshape(shape) for s in range(N_SHARD)]
        full[name] = jnp.concatenate(pieces, axis=axis)
    conv = [lax.bitcast_convert_type(gathered[s, used:used + _CONV_FLAT_ROWS].astype(jnp.bfloat16).reshape(DN_CONV, _CONV_SHARD, 2), F32)
            for s in range(N_SHARD)]
    return full, jnp.concatenate(conv, axis=1)


def _pack_grads(gw):
    parts = []
    for name, rows, shape, axis in _FLAT_PIECES:
        gfull = gw[name]
        if axis == 1:
            cut = gfull.reshape(gfull.shape[0], N_SHARD, gfull.shape[1] // N_SHARD)
            cut = jnp.transpose(cut, (1, 0, 2))
        else:
            cut = gfull.reshape((N_SHARD, gfull.shape[0] // N_SHARD) + gfull.shape[1:])
        parts.append(_pad_rows(cut.reshape(N_SHARD, rows, FLAT_W), 1))
    used = sum(p.shape[1] for p in parts)
    parts.append(jnp.zeros((N_SHARD, FLAT_ROWS - used, FLAT_W), F32))
    return jnp.concatenate(parts, axis=1)


def _unpack_reduced(flat):
    offs, _ = _flat_offsets()
    return {name: flat[offs[name]:offs[name] + rows].reshape(shape) for name, rows, shape, _ in _FLAT_PIECES}


_HBM = pl.BlockSpec(memory_space=pltpu.HBM)


def _place():
    x, y, c = lax.axis_index("x"), lax.axis_index("y"), lax.axis_index("c")
    chips = [(1 - x, y), (x, 1 - y), (1 - x, 1 - y)]
    return x, y, c, chips


def _remote(src, dst, send_sems, recv_sems, k, to):
    return pltpu.make_async_remote_copy(src_ref=src, dst_ref=dst, send_sem=send_sems.at[k], recv_sem=recv_sems.at[k],
                                        device_id=to, device_id_type=_MESH)


def _gather_shards(flat):
    rows, width = flat.shape
    half = rows // 2

    def body(in_ref, out_ref, send_sems, recv_sems, local_sem):
        x, y, c, chips = _place()
        me, sibling = (x, y, c), (x, y, 1 - c)

        def slot(cx, cy, hf):
            return out_ref.at[2 * cx + cy, pl.ds(pl.multiple_of(hf * half, 16), half), :]

        mine = pltpu.make_async_copy(in_ref, out_ref.at[2 * x + y], local_sem)
        mine.start()
        my_half = in_ref.at[pl.ds(pl.multiple_of(c * half, 16), half), :]
        first = [_remote(my_half, slot(x, y, c), send_sems, recv_sems, j, (cx, cy, c)) for j, (cx, cy) in enumerate(chips)]
        for cp in first:
            cp.start()
        passed = [_remote(slot(cx, cy, c), slot(cx, cy, c), send_sems, recv_sems, 3 + j, sibling)
                  for j, (cx, cy) in enumerate(chips)]
        for j, (cx, cy) in enumerate(chips):
            _remote(slot(cx, cy, c), slot(cx, cy, c), send_sems, recv_sems, j, me).wait_recv()
            passed[j].start()
        for j, (cx, cy) in enumerate(chips):
            _remote(slot(cx, cy, 1 - c), slot(cx, cy, 1 - c), send_sems, recv_sems, 3 + j, me).wait_recv()
        for cp in first + passed:
            cp.wait_send()
        mine.wait()

    return pl.pallas_call(
        body, name="gather_shards", out_shape=jax.ShapeDtypeStruct((N_SHARD, rows, width), flat.dtype),
        in_specs=[_HBM], out_specs=_HBM,
        scratch_shapes=[pltpu.SemaphoreType.DMA((6,)), pltpu.SemaphoreType.DMA((6,)), pltpu.SemaphoreType.DMA],
    )(flat)


def _reduce_pair_exchange(gflat):
    _, rows, width = gflat.shape
    half = rows // 2

    def body(g_ref, recv_ref, send_sems, recv_sems):
        x, y, c, _ = _place()
        src = g_ref.at[:, pl.ds(pl.multiple_of((1 - c) * half, 8), half), :]
        cp = _remote(src, recv_ref, send_sems, recv_sems, 0, (x, y, 1 - c))
        cp.start()
        cp.wait()

    return pl.pallas_call(
        body, name="reduce_pair_exchange", out_shape=jax.ShapeDtypeStruct((N_SHARD, half, width), gflat.dtype),
        in_specs=[_HBM], out_specs=_HBM,
        scratch_shapes=[pltpu.SemaphoreType.DMA((1,)), pltpu.SemaphoreType.DMA((1,))],
    )(gflat)


_ADD_ROWS = 384


def _pair_add(gflat, recv, c_arr):
    _, rows, width = gflat.shape
    half = rows // 2
    nt = half // _ADD_ROWS

    def body(c_ref, a_ref, b_ref, o_ref):
        o_ref[...] = (a_ref[...] + b_ref[...]).astype(o_ref.dtype)

    blk = lambda f: pl.BlockSpec((None, _ADD_ROWS, width), f)
    return pl.pallas_call(
        body, name="pair_add", out_shape=jax.ShapeDtypeStruct((N_SHARD, half, width), jnp.bfloat16),
        grid_spec=pltpu.PrefetchScalarGridSpec(
            num_scalar_prefetch=1, grid=(N_SHARD, nt),
            in_specs=[blk(lambda s, i, c: (s, c[0] * nt + i, 0)), blk(lambda s, i, c: (s, i, 0))],
            out_specs=blk(lambda s, i, c: (s, i, 0))),
        compiler_params=_cparams(("parallel", "parallel")))(c_arr, gflat, recv)


def _reduce_chip_exchange(part):
    _, half, width = part.shape

    def body(p_ref, out_ref, send_sems, recv_sems, local_sem):
        x, y, c, chips = _place()
        me = 2 * x + y
        mine = pltpu.make_async_copy(p_ref.at[me], out_ref.at[me], local_sem)
        mine.start()
        sends = [_remote(p_ref.at[2 * cx + cy], out_ref.at[me], send_sems, recv_sems, j, (cx, cy, c))
                 for j, (cx, cy) in enumerate(chips)]
        for cp in sends:
            cp.start()
        for j, (cx, cy) in enumerate(chips):
            _remote(p_ref.at[me], out_ref.at[2 * cx + cy], send_sems, recv_sems, j, (x, y, c)).wait_recv()
        for cp in sends:
            cp.wait_send()
        mine.wait()

    return pl.pallas_call(
        body, name="reduce_chip_exchange", out_shape=jax.ShapeDtypeStruct(part.shape, part.dtype),
        in_specs=[_HBM], out_specs=_HBM,
        scratch_shapes=[pltpu.SemaphoreType.DMA((3,)), pltpu.SemaphoreType.DMA((3,)), pltpu.SemaphoreType.DMA],
    )(part)


def _chip_add(parts):
    _, half, width = parts.shape

    def body(a0, a1, a2, a3, o_ref):
        f = lambda r: r[...].astype(F32)
        o_ref[...] = ((f(a0) + f(a1)) + f(a2)) + f(a3)

    specs = [pl.BlockSpec((None, _ADD_ROWS, width), functools.partial(lambda i, k: (k, i, 0), k=k)) for k in range(N_SHARD)]
    return pl.pallas_call(
        body, name="chip_add", grid=(half // _ADD_ROWS,), in_specs=specs,
        out_specs=pl.BlockSpec((_ADD_ROWS, width), lambda i: (i, 0)),
        out_shape=jax.ShapeDtypeStruct((half, width), F32),
        compiler_params=_cparams(("parallel",)))(parts, parts, parts, parts)


def _reduce_pair_share(rhalf):
    half, width = rhalf.shape

    def body(r_ref, out_ref, send_sems, recv_sems, local_sem):
        x, y, c, _ = _place()
        mine_rows = out_ref.at[pl.ds(pl.multiple_of(c * half, 8), half), :]
        other_rows = out_ref.at[pl.ds(pl.multiple_of((1 - c) * half, 8), half), :]
        mine = pltpu.make_async_copy(r_ref, mine_rows, local_sem)
        mine.start()
        cp = _remote(r_ref, mine_rows, send_sems, recv_sems, 0, (x, y, 1 - c))
        cp.start()
        _remote(r_ref, other_rows, send_sems, recv_sems, 0, (x, y, c)).wait_recv()
        cp.wait_send()
        mine.wait()

    return pl.pallas_call(
        body, name="reduce_pair_share", out_shape=jax.ShapeDtypeStruct((2 * half, width), rhalf.dtype),
        in_specs=[_HBM], out_specs=_HBM,
        scratch_shapes=[pltpu.SemaphoreType.DMA((1,)), pltpu.SemaphoreType.DMA((1,)), pltpu.SemaphoreType.DMA],
    )(rhalf)


def _small_allreduce(buf):
    r, width = buf.shape
    n_dev = 8

    def body(x_ref, all_ref, sum_ref, send_sems, recv_sems, local_sem):
        x, y, c, chips = _place()
        me, sibling = (x, y, c), (x, y, 1 - c)

        def rows(px, py, pc):
            return all_ref.at[pl.ds(pl.multiple_of((4 * px + 2 * py + pc) * r, 8), r), :]

        def copy(k, block, to, src=None):
            return _remote(rows(*block) if src is None else src, rows(*block), send_sems, recv_sems, k, to)

        mine = pltpu.make_async_copy(x_ref, rows(*me), local_sem)
        mine.start()
        first = [copy(0, me, sibling, src=x_ref)]
        first += [copy(1 + j, me, (*chip, c), src=x_ref) for j, chip in enumerate(chips)]
        for cp in first:
            cp.start()
        passed = [copy(4 + j, (*chip, c), sibling) for j, chip in enumerate(chips)]
        for j, chip in enumerate(chips):
            copy(1 + j, (*chip, c), me).wait_recv()
            passed[j].start()
        copy(0, sibling, me).wait_recv()
        for j, chip in enumerate(chips):
            copy(4 + j, (*chip, 1 - c), me).wait_recv()
        for cp in first + passed:
            cp.wait_send()
        mine.wait()
        total = all_ref[0:r, :]
        for k in range(1, n_dev):
            total = total + all_ref[k * r:(k + 1) * r, :]
        sum_ref[...] = total

    vm = pl.BlockSpec(memory_space=pltpu.VMEM)
    _, total = pl.pallas_call(
        body, name="small_allreduce",
        out_shape=[jax.ShapeDtypeStruct((n_dev * r, width), buf.dtype), jax.ShapeDtypeStruct((r, width), buf.dtype)],
        in_specs=[vm], out_specs=[vm, vm],
        scratch_shapes=[pltpu.SemaphoreType.DMA((7,)), pltpu.SemaphoreType.DMA((7,)), pltpu.SemaphoreType.DMA],
    )(buf)
    return total


def _row_tile(rows, cap):
    if rows <= cap:
        return rows
    t = (cap // 8) * 8
    while t >= 8:
        if rows % t == 0:
            return t
        t -= 8
    return rows


def _adamw(w, g, m, v, name):
    shape = w.shape
    cols = shape[-1] if len(shape) <= 3 else shape[-2] * shape[-1]
    w2, g2, m2, v2 = (a.reshape(-1, cols) for a in (w, g, m, v))
    rows = w2.shape[0]
    tr = _row_tile(rows, 256)

    def body(w_ref, g_ref, m_ref, v_ref, d_ref, mo_ref, vo_ref):
        gv = g_ref[...]
        mn = ADAM_B1 * m_ref[...] + (1.0 - ADAM_B1) * gv
        vn = ADAM_B2 * v_ref[...] + (1.0 - ADAM_B2) * (gv * gv)
        m_hat = mn / (1.0 - ADAM_B1 ** ADAM_STEP)
        v_hat = vn / (1.0 - ADAM_B2 ** ADAM_STEP)
        d_ref[...] = -ADAM_LR * (m_hat / (jnp.sqrt(v_hat) + ADAM_EPS) + ADAM_WD * w_ref[...])
        mo_ref[...] = mn
        vo_ref[...] = vn

    blk = pl.BlockSpec((tr, cols), lambda i: (i, 0))
    outs = pl.pallas_call(
        body, name=name, grid=(rows // tr,), in_specs=[blk] * 4, out_specs=[blk] * 3,
        out_shape=[jax.ShapeDtypeStruct((rows, cols), F32)] * 3, compiler_params=_cparams(("parallel",)))(w2, g2, m2, v2)
    return tuple(o.reshape(shape) for o in outs)


_WEIGHT_NAMES = ('w_in', 'conv_w', 'dn_a_log', 'dn_dt_bias', 'dn_norm_w', 'q_norm_w', 'w_uq', 'kv_norm_w', 'w_uk',
                 'w_uv', 'w_br_dn', 'w_br_mla', 'w_o', 'ln1_g', 'ln1_b', 'w_ffn_in', 'w_ffn_out', 'w_ple',
                 'w_ple_gate', 'ln2_g', 'ln2_b')
_SMALL_NAMES = ('ln1_g', 'ln1_b', 'ln2_g', 'ln2_b', 'q_norm_w', 'kv_norm_w', 'dn_norm_w', 'dn_a_log', 'dn_dt_bias')
_SMALL_GROUP = 8
_CONV_SMALL_ROW = len(_SMALL_NAMES) * _SMALL_GROUP
_CONV_SMALL_ROWS = DN_CONV * QKV_W // FLAT_W


def _pack_small(gw):
    rows = [jnp.pad(gw[n][None, :], ((0, _SMALL_GROUP - 1), (0, FLAT_W - gw[n].shape[0]))) for n in _SMALL_NAMES]
    rows.append(jnp.pad(gw['conv_w'].reshape(_CONV_SMALL_ROWS, FLAT_W), ((0, SMALL_ROWS - _CONV_SMALL_ROW - _CONV_SMALL_ROWS), (0, 0))))
    return jnp.concatenate(rows, axis=0)


def kernel(x, p, positions, w_in, conv_w, dn_a_log, dn_dt_bias, dn_norm_w, q_norm_w, w_uq, kv_norm_w, w_uk, w_uv, w_br_dn, w_br_mla, w_o, ln1_g, ln1_b, w_ffn_in, w_ffn_out, w_ple, w_ple_gate, ln2_g, ln2_b, loss_target, m_w_in, m_conv_w, m_dn_a_log, m_dn_dt_bias, m_dn_norm_w, m_q_norm_w, m_w_uq, m_kv_norm_w, m_w_uk, m_w_uv, m_w_br_dn, m_w_br_mla, m_w_o, m_ln1_g, m_ln1_b, m_w_ffn_in, m_w_ffn_out, m_w_ple, m_w_ple_gate, m_ln2_g, m_ln2_b, v_w_in, v_conv_w, v_dn_a_log, v_dn_dt_bias, v_dn_norm_w, v_q_norm_w, v_w_uq, v_kv_norm_w, v_w_uk, v_w_uv, v_w_br_dn, v_w_br_mla, v_w_o, v_ln1_g, v_ln1_b, v_w_ffn_in, v_w_ffn_out, v_w_ple, v_w_ple_gate, v_ln2_g, v_ln2_b):
    ws = dict(w_in=w_in, conv_w=conv_w, dn_a_log=dn_a_log, dn_dt_bias=dn_dt_bias, dn_norm_w=dn_norm_w, q_norm_w=q_norm_w,
              w_uq=w_uq, kv_norm_w=kv_norm_w, w_uk=w_uk, w_uv=w_uv, w_br_dn=w_br_dn, w_br_mla=w_br_mla, w_o=w_o,
              ln1_g=ln1_g, ln1_b=ln1_b, w_ffn_in=w_ffn_in, w_ffn_out=w_ffn_out, w_ple=w_ple, w_ple_gate=w_ple_gate,
              ln2_g=ln2_g, ln2_b=ln2_b)
    ms = dict(w_in=m_w_in, conv_w=m_conv_w, dn_a_log=m_dn_a_log, dn_dt_bias=m_dn_dt_bias, dn_norm_w=m_dn_norm_w,
              q_norm_w=m_q_norm_w, w_uq=m_w_uq, kv_norm_w=m_kv_norm_w, w_uk=m_w_uk, w_uv=m_w_uv, w_br_dn=m_w_br_dn,
              w_br_mla=m_w_br_mla, w_o=m_w_o, ln1_g=m_ln1_g, ln1_b=m_ln1_b, w_ffn_in=m_w_ffn_in, w_ffn_out=m_w_ffn_out,
              w_ple=m_w_ple, w_ple_gate=m_w_ple_gate, ln2_g=m_ln2_g, ln2_b=m_ln2_b)
    vs = dict(w_in=v_w_in, conv_w=v_conv_w, dn_a_log=v_dn_a_log, dn_dt_bias=v_dn_dt_bias, dn_norm_w=v_dn_norm_w,
              q_norm_w=v_q_norm_w, w_uq=v_w_uq, kv_norm_w=v_kv_norm_w, w_uk=v_w_uk, w_uv=v_w_uv, w_br_dn=v_w_br_dn,
              w_br_mla=v_w_br_mla, w_o=v_w_o, ln1_g=v_ln1_g, ln1_b=v_ln1_b, w_ffn_in=v_w_ffn_in, w_ffn_out=v_w_ffn_out,
              w_ple=v_w_ple, w_ple_gate=v_w_ple_gate, ln2_g=v_ln2_g, ln2_b=v_ln2_b)
    mx, my, mc = lax.axis_index("x"), lax.axis_index("y"), lax.axis_index("c")

    flat = _pack_shard({name: ws[name][0] for name, _, _, _ in _FLAT_PIECES}, conv_w[0])
    full, conv_full = _unpack_gathered(_gather_shards(flat))
    wt = _prep_weights(full)
    small = {n: ws[n][0] for n in _SMALL_NAMES}
    small['conv_w'] = conv_full
    sp = _prep_small(small)
    cosb, sinb = _rope_tables(positions[0])

    loss_lanes, dx, g = _local_step(x[0], p[0, 0], cosb, sinb, loss_target[0], wt, sp)
    gw = _unprep_grads(g)
    loss = lax.psum(jnp.sum(loss_lanes), ("x", "y", "c"))

    gflat = _pack_grads(gw)
    part = _pair_add(gflat, _reduce_pair_exchange(gflat), jnp.reshape(mc, (1,)).astype(jnp.int32))
    reduced = _unpack_reduced(_reduce_pair_share(_chip_add(_reduce_chip_exchange(part))))
    tot = _small_allreduce(_pack_small(gw))
    gred = {name: reduced[name][None] for name, _, _, _ in _FLAT_PIECES}
    for i, n in enumerate(_SMALL_NAMES):
        gred[n] = tot[i * _SMALL_GROUP, :ws[n].shape[1]][None]
    conv_tot = tot[_CONV_SMALL_ROW:_CONV_SMALL_ROW + _CONV_SMALL_ROWS].reshape(DN_CONV, QKV_W)
    gred['conv_w'] = lax.dynamic_slice_in_dim(conv_tot, (2 * mx + my) * _CONV_SHARD, _CONV_SHARD, axis=1)[None]

    deltas, new_m, new_v = {}, {}, {}
    for n in _WEIGHT_NAMES:
        gred[n] = gred[n].reshape(ws[n].shape)
        deltas[n], new_m[n], new_v[n] = _adamw(ws[n], gred[n], ms[n], vs[n], "adamw_" + n)
    return (loss, dx[None], *[gred[n] for n in _WEIGHT_NAMES], *[deltas[n] for n in _WEIGHT_NAMES],
            *[new_m[n] for n in _WEIGHT_NAMES], *[new_v[n] for n in _WEIGHT_NAMES])
```

```python
import functools
import math

import jax
import jax.numpy as jnp
from jax import lax
from jax.experimental import pallas as pl
from jax.experimental.pallas import tpu as pltpu

F32 = jnp.float32
_CDT = jnp.bfloat16
_HI = lax.Precision.HIGHEST
_MESH = pl.DeviceIdType.MESH

D_MODEL = 1024
PLE_DIM = 256
HEADS = 8
DN_DK = 128
DN_CHUNK = 64
DN_CONV = 4
QKV_W = 3 * HEADS * DN_DK
Q_LORA = 384
KV_LORA = 256
NOPE = 128
ROPE = 64
ROPE_PAD = 128
FFN_HIDDEN = 2816
D_IN = 6864
ROPE_BASE = 10000.0
ALPHA = 2.0 ** 0.25
ATT_SCALE = (NOPE + ROPE) ** -0.5
NEG_BIG = -1e30
ADAM_LR, ADAM_B1, ADAM_B2, ADAM_EPS, ADAM_WD, ADAM_STEP = 0.001, 0.9, 0.999, 1e-08, 0.01, 10

LANE = 128
VMEM_LIMIT = 56 * 1024 * 1024
MM_VMEM_BUDGET = 40 * 1024 * 1024
N_SHARD = 4
FLAT_W = 1024
FLAT_HALF = 2688
FLAT_ROWS = 2 * FLAT_HALF
SMALL_ROWS = 88


def _tile(dim, cap):
    if dim <= cap:
        return dim
    t = (cap // LANE) * LANE
    while t >= LANE:
        if dim % t == 0:
            return t
        t -= LANE
    return dim


def _cparams(sem):
    return pltpu.CompilerParams(dimension_semantics=sem, vmem_limit_bytes=VMEM_LIMIT)


def _mm(a, b, *, name, ta=False, tb=False, add=None, add_scale=1.0, out_dtype=F32, heads=None,
        a_head=None, b_head=None, out_head=None, dims=None, tm=1408, tn=1408):
    m, n, k = dims
    tm, tn = _tile(m, tm), _tile(n, tn)
    sa, sb, so = a.dtype.itemsize, b.dtype.itemsize, jnp.dtype(out_dtype).itemsize

    def vmem_need(tk_):
        acc = tm * tn * 4 if tk_ < k else 0
        extra = 2 * tm * tn * 4 if add is not None else 0
        return 2 * (tm * tk_ * sa + tk_ * tn * sb) + 2 * tm * tn * so + acc + extra

    tk = k
    while vmem_need(tk) > MM_VMEM_BUDGET and tk > LANE:
        smaller = _tile(k, tk - LANE)
        if smaller >= tk:
            break
        tk = smaller
    nk = k // tk
    hgrid = () if heads is None else (heads,)
    off = len(hgrid)

    def spec(rows, cols, rtile, ctile, rsel, csel, layout):
        def idx(*g):
            h = g[0] if off else 0
            ri, ci = g[off + rsel], g[off + csel]
            if layout == 'lead':
                return (h, ri, ci)
            if layout == 'col':
                return (ri, h * (cols // ctile) + ci)
            return (ri, ci)
        if layout == 'lead':
            return pl.BlockSpec((None, rtile, ctile), idx)
        return pl.BlockSpec((rtile, ctile), idx)

    a_spec = spec(k, m, tk, tm, 2, 0, a_head) if ta else spec(m, k, tm, tk, 0, 2, a_head)
    b_spec = spec(n, k, tn, tk, 1, 2, b_head) if tb else spec(k, n, tk, tn, 2, 1, b_head)
    o_spec = spec(m, n, tm, tn, 0, 1, out_head)
    in_specs = [a_spec, b_spec]
    args = [a, b]
    if add is not None:
        in_specs.append(spec(m, n, tm, tn, 0, 1, out_head))
        args.append(add)
    dn = (((0 if ta else 1,), (1 if tb else 0,)), ((), ()))

    def body(*refs):
        a_ref, b_ref = refs[0], refs[1]
        prod = lax.dot_general(a_ref[...].astype(_CDT), b_ref[...].astype(_CDT), dn, preferred_element_type=F32)
        if nk == 1:
            o_ref = refs[-1]
            if add is not None:
                prod = prod + refs[2][...].astype(F32) * add_scale
            o_ref[...] = prod.astype(out_dtype)
            return
        o_ref, acc_ref = refs[-2], refs[-1]
        kk = pl.program_id(off + 2)

        @pl.when(kk == 0)
        def _():
            if add is not None:
                acc_ref[...] = refs[2][...].astype(F32) * add_scale
            else:
                acc_ref[...] = jnp.zeros_like(acc_ref)

        acc_ref[...] += prod

        @pl.when(kk == nk - 1)
        def _():
            o_ref[...] = acc_ref[...].astype(out_dtype)

    if out_head == 'lead':
        oshape = (heads, m, n)
    elif out_head == 'col':
        oshape = (m, heads * n)
    else:
        oshape = (m, n)
    sem = ("parallel",) * (off + 2) + ("arbitrary",)
    return pl.pallas_call(
        body, name=name, grid=hgrid + (m // tm, n // tn, nk), in_specs=in_specs, out_specs=o_spec,
        out_shape=jax.ShapeDtypeStruct(oshape, out_dtype),
        scratch_shapes=[pltpu.VMEM((tm, tn), F32)] if nk > 1 else [],
        compiler_params=_cparams(sem))(*args)


def _mm2(a, b, **kw):
    ta, tb = kw.get('ta', False), kw.get('tb', False)
    m = a.shape[1] if ta else a.shape[0]
    k = a.shape[0] if ta else a.shape[1]
    n = b.shape[0] if tb else b.shape[1]
    return _mm(a, b, dims=(m, n, k), **kw)


def _rowwise(fn, rows, bcast, outs, reds=(), *, name, tm=256, heads=None):
    t = rows[0][0].shape[0]
    tm = min(tm, t)
    hn = 1 if heads is None else heads
    in_specs, args = [], []
    for arr, width, base, per_head in rows:
        in_specs.append(pl.BlockSpec((tm, width), functools.partial(
            lambda i, h, base, per_head: (i, base + (h if per_head else 0)), base=base, per_head=per_head)))
        args.append(arr)
    for arr in bcast:
        in_specs.append(pl.BlockSpec(arr.shape, lambda i, h: (0, 0)))
        args.append(arr)
    out_specs, out_shape = [], []
    for total, width, per_head, dt in outs:
        out_specs.append(pl.BlockSpec((tm, width), functools.partial(
            lambda i, h, per_head: (i, h if per_head else 0), per_head=per_head)))
        out_shape.append(jax.ShapeDtypeStruct((t, total), dt))
    for shp in reds:
        out_specs.append(pl.BlockSpec(shp, lambda i, h: (0, 0)))
        out_shape.append(jax.ShapeDtypeStruct(shp, F32))
    n_in, n_out, n_red = len(args), len(outs), len(reds)

    def body(*refs):
        i, h = pl.program_id(0), pl.program_id(1)
        vals = fn(h, *[r[...] for r in refs[:n_in]])
        for r, v in zip(refs[n_in:n_in + n_out], vals[:n_out]):
            r[...] = v.astype(r.dtype)
        if n_red:
            @pl.when((i == 0) & (h == 0))
            def _():
                for r in refs[n_in + n_out:]:
                    r[...] = jnp.zeros_like(r)
            for r, v in zip(refs[n_in + n_out:], vals[n_out:]):
                r[...] += v

    sem = ("arbitrary", "arbitrary") if n_red else ("parallel", "parallel")
    res = pl.pallas_call(body, name=name, grid=(t // tm, hn), in_specs=in_specs, out_specs=out_specs,
                         out_shape=out_shape, compiler_params=_cparams(sem))(*args)
    return tuple(res)


def _sigmoid(x):
    return 1.0 / (1.0 + jnp.exp(-x))


def _silu(x):
    return x * _sigmoid(x)


def _softplus(x):
    return jnp.maximum(x, 0.0) + jnp.log(1.0 + jnp.exp(-jnp.abs(x)))


def _layer_norm(t, g, b):
    mu = jnp.mean(t, axis=-1, keepdims=True)
    d = t - mu
    var = jnp.mean(d * d, axis=-1, keepdims=True)
    return d * lax.rsqrt(var + 1e-5) * g + b


def _rms_norm(t, w):
    return t * lax.rsqrt(jnp.mean(t * t, axis=-1, keepdims=True) + 1e-6) * w


def _swap_rope_halves(t):
    lane = lax.broadcasted_iota(jnp.int32, t.shape, 1) % ROPE_PAD
    n = t.shape[1]
    up = pltpu.roll(t, n - ROPE // 2, axis=1)
    dn = pltpu.roll(t, ROPE // 2, axis=1)
    return jnp.where(lane < ROPE // 2, up, jnp.where(lane < ROPE, dn, 0.0))


def _rope(t, cosb, sinb):
    reps = t.shape[1] // ROPE_PAD
    c = jnp.tile(cosb, (1, reps)) if reps > 1 else cosb
    s = jnp.tile(sinb, (1, reps)) if reps > 1 else sinb
    return t * c + _swap_rope_halves(t) * s


def _rope_bwd(d, cosb, sinb):
    reps = d.shape[1] // ROPE_PAD
    c = jnp.tile(cosb, (1, reps)) if reps > 1 else cosb
    s = jnp.tile(sinb, (1, reps)) if reps > 1 else sinb
    return d * c + _swap_rope_halves(d * s)


_CONV_ROWS = 256
_CONV_COLS = 256


def _conv_window(ref, r0, lo, hi, t):
    parts = []
    start, stop = r0 - lo, r0 + _CONV_ROWS + hi
    if start < 0:
        parts.append(jnp.zeros((-start, ref.shape[1]), F32))
        start = 0
    tail = max(stop - t, 0)
    parts.append(ref[start:stop - tail, :].astype(F32))
    if tail:
        parts.append(jnp.zeros((tail, ref.shape[1]), F32))
    return parts[0] if len(parts) == 1 else jnp.concatenate(parts, axis=0)


def _conv_taps(win, w_ref, n_out):
    acc = win[8:8 + n_out] * w_ref[DN_CONV - 1:DN_CONV, :]
    for i in range(DN_CONV - 1):
        acc = acc + pltpu.roll(win, DN_CONV - 1 - i, axis=0)[8:8 + n_out] * w_ref[i:i + 1, :]
    return acc


def _conv_silu(x, w):
    t, ch = x.shape

    def body(x_ref, w_ref, o_ref):
        for r in range(t // _CONV_ROWS):
            r0 = r * _CONV_ROWS
            c = _conv_taps(_conv_window(x_ref, r0, 8, 0, t), w_ref, _CONV_ROWS)
            o_ref[r0:r0 + _CONV_ROWS, :] = _silu(c)

    return pl.pallas_call(
        body, name="conv_silu", grid=(ch // _CONV_COLS,),
        in_specs=[pl.BlockSpec((t, _CONV_COLS), lambda j: (0, j)), pl.BlockSpec((DN_CONV, _CONV_COLS), lambda j: (0, j))],
        out_specs=pl.BlockSpec((t, _CONV_COLS), lambda j: (0, j)),
        out_shape=jax.ShapeDtypeStruct((t, ch), F32), compiler_params=_cparams(("parallel",)))(x, w)


def _conv_silu_bwd(x, w, dy):
    t, ch = x.shape

    def body(x_ref, w_ref, dy_ref, dx_ref, dw_ref):
        dws = [jnp.zeros((1, _CONV_COLS), F32) for _ in range(DN_CONV)]
        for r in range(t // _CONV_ROWS):
            r0 = r * _CONV_ROWS
            n_ext = _CONV_ROWS + 8
            xw = _conv_window(x_ref, r0, 8, 8, t)
            c = _conv_taps(xw, w_ref, n_ext)
            sg = _sigmoid(c)
            ds = _conv_window(dy_ref, r0, 0, 8, t) * (sg * (1.0 + c * (1.0 - sg)))
            dx = ds[:_CONV_ROWS] * w_ref[DN_CONV - 1:DN_CONV, :]
            for i in range(DN_CONV - 1):
                sh = DN_CONV - 1 - i
                dx = dx + pltpu.roll(ds, n_ext - sh, axis=0)[:_CONV_ROWS] * w_ref[i:i + 1, :]
            dx_ref[r0:r0 + _CONV_ROWS, :] = dx.astype(dx_ref.dtype)
            ds0 = ds[:_CONV_ROWS]
            for i in range(DN_CONV):
                sh = DN_CONV - 1 - i
                xs = xw if sh == 0 else pltpu.roll(xw, sh, axis=0)
                dws[i] = dws[i] + jnp.sum(ds0 * xs[8:8 + _CONV_ROWS], axis=0, keepdims=True)
        for i in range(DN_CONV):
            dw_ref[i:i + 1, :] = dws[i]

    blk = pl.BlockSpec((t, _CONV_COLS), lambda j: (0, j))
    wblk = pl.BlockSpec((DN_CONV, _CONV_COLS), lambda j: (0, j))
    return pl.pallas_call(
        body, name="conv_silu_bwd", grid=(ch // _CONV_COLS,), in_specs=[blk, wblk, blk], out_specs=[blk, wblk],
        out_shape=[jax.ShapeDtypeStruct((t, ch), _CDT), jax.ShapeDtypeStruct((DN_CONV, ch), F32)],
        compiler_params=_cparams(("parallel",)))(x, w, dy)


_PA_ROWS = 512


def _bmm(a, b, spec, exact=False):
    if exact:
        return jnp.einsum(spec, a, b, precision=_HI, preferred_element_type=F32)
    return jnp.einsum(spec, a.astype(_CDT), b.astype(_CDT), preferred_element_type=F32)


def _split16(a):
    hi = a.astype(jnp.bfloat16)
    return hi, (a - hi.astype(F32)).astype(jnp.bfloat16)


def _bmm3(a, b, spec):
    ah, al = _split16(a)
    bh, bl = _split16(b)
    e = lambda p, q: jnp.einsum(spec, p, q, preferred_element_type=F32)
    return e(ah, bh) + (e(ah, bl) + e(al, bh))


def _tri_inverse(l_mat, eye):
    pw = -l_mat
    t_inv = eye + pw
    for _ in range(5):
        pw = _bmm3(pw, pw, 'bij,bjk->bik')
        t_inv = t_inv + _bmm3(t_inv, pw, 'bij,bjk->bik')
    return t_inv


@jax.custom_vjp
def _tri_inverse_saved(l_mat, t_saved):
    return t_saved


def _tri_inverse_saved_fwd(l_mat, t_saved):
    return t_saved, t_saved


def _tri_inverse_saved_bwd(t_saved, dt):
    left = _bmm3(t_saved, dt, 'bji,bjk->bik')
    return -_bmm3(left, t_saved, 'bij,bkj->bik'), jnp.zeros_like(t_saved)


_tri_inverse_saved.defvjp(_tri_inverse_saved_fwd, _tri_inverse_saved_bwd)


def _phase_a(h, q, k, v, ba, alog, dtb, t_saved=None):
    r = q.shape[0]
    nb = r // DN_CHUNK
    c = DN_CHUNK
    lane = lax.broadcasted_iota(jnp.int32, (1, LANE), 1)
    selb = (lane == h).astype(F32)
    sela = (lane == h + HEADS).astype(F32)
    b_raw = jnp.sum(ba * selb, axis=1, keepdims=True)
    a_raw = jnp.sum(ba * sela, axis=1, keepdims=True)
    al = jnp.sum(alog * selb, axis=1, keepdims=True)
    dt = jnp.sum(dtb * selb, axis=1, keepdims=True)
    beta = jnp.broadcast_to(_sigmoid(b_raw), (r, LANE))
    g = jnp.broadcast_to(-jnp.exp(al) * _softplus(a_raw + dt), (r, LANE))
    qn = q * lax.rsqrt(jnp.sum(q * q, -1, keepdims=True) + 1e-6) * (DN_DK ** -0.5)
    kn = k * lax.rsqrt(jnp.sum(k * k, -1, keepdims=True) + 1e-6)
    q3, k3, v3 = qn.reshape(nb, c, LANE), kn.reshape(nb, c, LANE), v.reshape(nb, c, LANE)
    b3, g3 = beta.reshape(nb, c, LANE), g.reshape(nb, c, LANE)
    ri = lax.broadcasted_iota(jnp.int32, (nb, c, c), 1)
    ci = lax.broadcasted_iota(jnp.int32, (nb, c, c), 2)
    tril, strict = ri >= ci, ri > ci
    gc = _bmm(tril.astype(F32), g3, 'bij,bjd->bid', exact=True)
    onehot = (lax.broadcasted_iota(jnp.int32, (nb, c, LANE), 2) == 0).astype(F32)
    g_row = _bmm(onehot, gc, 'bid,bjd->bij', exact=True)
    diff = gc[:, :, :c] - g_row
    decay = jnp.where(tril, jnp.exp(jnp.where(tril, diff, 0.0)), 0.0)
    kb = k3 * b3
    l_mat = jnp.where(strict, _bmm(kb, k3, 'bid,bjd->bij') * decay, 0.0)
    if t_saved is None:
        t_inv = _tri_inverse(l_mat, (ri == ci).astype(F32))
    else:
        t_inv = _tri_inverse_saved(l_mat, t_saved.reshape(nb, c, c))
    eg = jnp.exp(gc)
    u = _bmm(t_inv, v3 * b3, 'bij,bje->bie')
    w = _bmm(t_inv, kb * eg, 'bij,bje->bie')
    intra = jnp.where(tril, _bmm(q3, k3, 'bid,bjd->bij') * decay, 0.0)
    qd = q3 * eg
    gl = jnp.sum(g3, axis=1, keepdims=True)
    kt = k3 * jnp.exp(gl - gc)
    outs = (u.reshape(r, LANE), w.reshape(r, LANE), qd.reshape(r, LANE), kt.reshape(r, LANE),
            intra.reshape(r, c), gl.reshape(nb, LANE))
    return outs + (t_inv.reshape(r, c),) if t_saved is None else outs


def _pa_specs(t):
    rr = min(_PA_ROWS, t)
    nb = rr // DN_CHUNK
    qkv = [pl.BlockSpec((rr, LANE), functools.partial(lambda i, h, o: (i, o + h), o=o)) for o in (0, HEADS, 2 * HEADS)]
    ba = pl.BlockSpec((rr, LANE), lambda i, h: (i, 3))
    vec = pl.BlockSpec((1, LANE), lambda i, h: (0, 0))
    row = pl.BlockSpec((rr, LANE), lambda i, h: (i, h))
    intra = pl.BlockSpec((None, rr, DN_CHUNK), lambda i, h: (h, i, 0))
    gl = pl.BlockSpec((nb, LANE), lambda i, h: (i, h))
    return rr, qkv, ba, vec, row, intra, gl


def _delta_local(qkv_act, pm, alog, dtb):
    t = qkv_act.shape[0]
    rr, qkv, ba, vec, row, intra, gl = _pa_specs(t)

    def body(q, k, v, b, al, dt, *outs):
        vals = _phase_a(pl.program_id(1), q[...], k[...], v[...], b[...], al[...], dt[...])
        for o, val in zip(outs, vals):
            o[...] = val

    wide = jax.ShapeDtypeStruct((t, HEADS * LANE), F32)
    sq = jax.ShapeDtypeStruct((HEADS, t, DN_CHUNK), F32)
    return pl.pallas_call(
        body, name="delta_local", grid=(t // rr, HEADS), in_specs=qkv + [ba, vec, vec],
        out_specs=[row] * 4 + [intra, gl, intra],
        out_shape=[wide] * 4 + [sq, jax.ShapeDtypeStruct((t // DN_CHUNK, HEADS * LANE), F32), sq],
        compiler_params=_cparams(("parallel", "parallel")))(qkv_act, qkv_act, qkv_act, pm, alog, dtb)


def _delta_local_bwd(qkv_act, pm, alog, dtb, t_inv, du, dw, dqd, dkt, dintra, dgl):
    t = qkv_act.shape[0]
    rr, qkv, ba, vec, row, intra, gl = _pa_specs(t)

    def body(q, k, v, b, al, dt, ti, du_r, dw_r, dqd_r, dkt_r, di_r, dgl_r, dq_o, dk_o, dv_o, dba_o, dal_o, ddt_o):
        i, h = pl.program_id(0), pl.program_id(1)
        t_saved = ti[...]
        _, vjp = jax.vjp(lambda *a: _phase_a(h, *a, t_saved=t_saved), q[...], k[...], v[...], b[...], al[...], dt[...])
        dq, dk, dv, dba, dal, ddt = vjp((du_r[...], dw_r[...], dqd_r[...], dkt_r[...], di_r[...], dgl_r[...]))
        dq_o[...], dk_o[...], dv_o[...] = dq, dk, dv

        @pl.when(h == 0)
        def _():
            dba_o[...] = jnp.zeros_like(dba_o)

        @pl.when((h == 0) & (i == 0))
        def _():
            dal_o[...] = jnp.zeros_like(dal_o)
            ddt_o[...] = jnp.zeros_like(ddt_o)

        dba_o[...] += dba
        dal_o[...] += dal
        ddt_o[...] += ddt

    wide = jax.ShapeDtypeStruct((t, HEADS * LANE), F32)
    vshape = jax.ShapeDtypeStruct((1, LANE), F32)
    return pl.pallas_call(
        body, name="delta_local_bwd", grid=(t // rr, HEADS),
        in_specs=qkv + [ba, vec, vec, intra] + [row] * 4 + [intra, gl],
        out_specs=[row] * 3 + [pl.BlockSpec((rr, LANE), lambda i, h: (i, 0)), vec, vec],
        out_shape=[wide] * 3 + [jax.ShapeDtypeStruct((t, LANE), F32), vshape, vshape],
        compiler_params=_cparams(("arbitrary", "arbitrary")))(
            qkv_act, qkv_act, qkv_act, pm, alog, dtb, t_inv, du, dw, dqd, dkt, dintra, dgl)


_SCAN_ROWS = 512


def _dot(a, b, dn):
    return lax.dot_general(a.astype(_CDT), b.astype(_CDT), (dn, ((), ())), preferred_element_type=F32)


_NN = ((1,), (0,))
_NT = ((1,), (1,))
_TN = ((0,), (0,))


def _delta_scan(u, w, qd, kt, intra, gl):
    t = u.shape[0]
    rr = min(_SCAN_ROWS, t)
    nc = rr // DN_CHUNK

    def body(u_ref, w_ref, qd_ref, kt_ref, a_ref, gl_ref, o_ref, sall_ref, s_scr):
        @pl.when(pl.program_id(0) == 0)
        def _():
            s_scr[...] = jnp.zeros_like(s_scr)

        def chunk(c, carry):
            r0 = pl.multiple_of(c * DN_CHUNK, DN_CHUNK)
            rows = pl.ds(r0, DN_CHUNK)
            e = jnp.exp(gl_ref[pl.ds(c, 1), :])
            for h in range(HEADS):
                cs = slice(h * LANE, (h + 1) * LANE)
                s = s_scr[h]
                sall_ref[c, h] = s
                v_new = u_ref[rows, cs] - _dot(w_ref[rows, cs], s, _NN)
                o_ref[rows, cs] = _dot(qd_ref[rows, cs], s, _NN) + _dot(a_ref[h, rows, :], v_new, _NN)
                s_scr[h] = s * e[:, cs] + _dot(kt_ref[rows, cs], v_new, _TN)
            return carry

        lax.fori_loop(0, nc, chunk, 0)

    row = pl.BlockSpec((rr, HEADS * LANE), lambda i: (i, 0))
    return pl.pallas_call(
        body, name="delta_scan", grid=(t // rr,),
        in_specs=[row] * 4 + [pl.BlockSpec((HEADS, rr, DN_CHUNK), lambda i: (0, i, 0)),
                              pl.BlockSpec((nc, HEADS * LANE), lambda i: (i, 0))],
        out_specs=[row, pl.BlockSpec((nc, HEADS, LANE, LANE), lambda i: (i, 0, 0, 0))],
        out_shape=[jax.ShapeDtypeStruct((t, HEADS * LANE), F32),
                   jax.ShapeDtypeStruct((t // DN_CHUNK, HEADS, LANE, LANE), F32)],
        scratch_shapes=[pltpu.VMEM((HEADS, LANE, LANE), F32)],
        compiler_params=_cparams(("arbitrary",)))(u, w, qd, kt, intra, gl)


def _delta_scan_bwd(u, w, qd, kt, intra, gl, sall, do):
    t = u.shape[0]
    rr = min(_SCAN_ROWS, t)
    nc = rr // DN_CHUNK
    ng = t // rr

    def body(u_ref, w_ref, qd_ref, kt_ref, a_ref, gl_ref, sall_ref, do_ref,
             du_ref, dw_ref, dqd_ref, dkt_ref, da_ref, dgl_ref, ds_scr):
        @pl.when(pl.program_id(0) == 0)
        def _():
            ds_scr[...] = jnp.zeros_like(ds_scr)

        def chunk(cc, carry):
            c = nc - 1 - cc
            r0 = pl.multiple_of(c * DN_CHUNK, DN_CHUNK)
            rows = pl.ds(r0, DN_CHUNK)
            e = jnp.exp(gl_ref[pl.ds(c, 1), :])
            for h in range(HEADS):
                cs = slice(h * LANE, (h + 1) * LANE)
                s = sall_ref[c, h]
                ds_out = ds_scr[h]
                w_c, kt_c, qd_c, do_c = w_ref[rows, cs], kt_ref[rows, cs], qd_ref[rows, cs], do_ref[rows, cs]
                a_c = a_ref[h, rows, :]
                v_new = u_ref[rows, cs] - _dot(w_c, s, _NN)
                dv_new = _dot(a_c, do_c, _TN) + _dot(kt_c, ds_out, _NN)
                da_ref[h, rows, :] = _dot(do_c, v_new, _NT)
                dqd_ref[rows, cs] = _dot(do_c, s, _NT)
                dkt_ref[rows, cs] = _dot(v_new, ds_out, _NT)
                du_ref[rows, cs] = dv_new
                dw_ref[rows, cs] = -_dot(dv_new, s, _NT)
                eh = e[:, cs]
                dgl_ref[pl.ds(pl.multiple_of(c * 8, 8), 8), cs] = jnp.broadcast_to(
                    jnp.sum(ds_out * s, axis=0, keepdims=True) * eh, (8, LANE))
                ds_scr[h] = ds_out * eh + _dot(qd_c, do_c, _TN) - _dot(w_c, dv_new, _TN)
            return carry

        lax.fori_loop(0, nc, chunk, 0)

    rev = lambda i: (ng - 1 - i, 0)
    row = pl.BlockSpec((rr, HEADS * LANE), rev)
    a_spec = pl.BlockSpec((HEADS, rr, DN_CHUNK), lambda i: (0, ng - 1 - i, 0))
    gl_spec = pl.BlockSpec((nc, HEADS * LANE), rev)
    wide = jax.ShapeDtypeStruct((t, HEADS * LANE), F32)
    outs = pl.pallas_call(
        body, name="delta_scan_bwd", grid=(ng,),
        in_specs=[row] * 4 + [a_spec, gl_spec, pl.BlockSpec((nc, HEADS, LANE, LANE), lambda i: (ng - 1 - i, 0, 0, 0)), row],
        out_specs=[row] * 4 + [a_spec, pl.BlockSpec((nc * 8, HEADS * LANE), rev)],
        out_shape=[wide] * 4 + [jax.ShapeDtypeStruct((HEADS, t, DN_CHUNK), F32),
                                jax.ShapeDtypeStruct((t // DN_CHUNK * 8, HEADS * LANE), F32)],
        scratch_shapes=[pltpu.VMEM((HEADS, LANE, LANE), F32)],
        compiler_params=_cparams(("arbitrary",)))(u, w, qd, kt, intra, gl, sall, do)
    return tuple(outs[:5]) + (outs[5].reshape(t // DN_CHUNK, 8, HEADS * LANE)[:, 0, :],)


_ATT_TILE = 512


def _kv_rows(j, tk):
    return pl.ds(pl.multiple_of(j * tk, tk), tk)


def _att_scores(ql, qr, ckv_ref, kr_ref, j, tk):
    ks = _kv_rows(j, tk)
    return (_dot(ql, ckv_ref[ks, :], _NT) + _dot(qr, kr_ref[ks, :], _NT)) * ATT_SCALE


def _diag_mask(s):
    qi = lax.broadcasted_iota(jnp.int32, s.shape, 0)
    ki = lax.broadcasted_iota(jnp.int32, s.shape, 1)
    return jnp.where(ki <= qi, s, NEG_BIG)


def _attention(ql, qr, ckv, kr):
    t = ckv.shape[0]
    tq = min(_ATT_TILE, t)

    def body(ql_ref, qr_ref, ckv_ref, kr_ref, o_ref, lse_ref, m_scr, l_scr, acc_scr, s_scr):
        qi = pl.program_id(1)
        q_lat, q_rope = ql_ref[...], qr_ref[...]
        m_scr[...] = jnp.full_like(m_scr, NEG_BIG)
        l_scr[...] = jnp.zeros_like(l_scr)
        acc_scr[...] = jnp.zeros_like(acc_scr)

        def consume(j, s):
            m_old = m_scr[...]
            m_new = jnp.maximum(m_old, jnp.max(s, axis=-1, keepdims=True))
            p = jnp.exp(s - m_new)
            alpha = jnp.exp(m_old - m_new)
            l_scr[...] = l_scr[...] * alpha + jnp.sum(p, axis=-1, keepdims=True)
            acc_scr[...] = acc_scr[...] * alpha + _dot(p, ckv_ref[_kv_rows(j, tq), :], _NN)
            m_scr[...] = m_new

        s_scr[0] = _att_scores(q_lat, q_rope, ckv_ref, kr_ref, 0, tq)

        def loop_body(j, carry):
            s_scr[(j + 1) % 2] = _att_scores(q_lat, q_rope, ckv_ref, kr_ref, j + 1, tq)
            consume(j, s_scr[j % 2])
            return carry

        lax.fori_loop(0, qi, loop_body, 0)
        consume(qi, _diag_mask(s_scr[qi % 2]))
        o_ref[...] = acc_scr[...] / l_scr[...]
        lse_ref[...] = m_scr[...] + jnp.log(l_scr[...])

    return pl.pallas_call(
        body, name="attention", grid=(HEADS, t // tq),
        in_specs=[pl.BlockSpec((None, tq, KV_LORA), lambda h, i: (h, i, 0)),
                  pl.BlockSpec((tq, ROPE_PAD), lambda h, i: (i, h)),
                  pl.BlockSpec((t, KV_LORA), lambda h, i: (0, 0)),
                  pl.BlockSpec((t, ROPE_PAD), lambda h, i: (0, 0))],
        out_specs=[pl.BlockSpec((None, tq, KV_LORA), lambda h, i: (h, i, 0)),
                   pl.BlockSpec((None, tq, 1), lambda h, i: (h, i, 0))],
        out_shape=[jax.ShapeDtypeStruct((HEADS, t, KV_LORA), F32), jax.ShapeDtypeStruct((HEADS, t, 1), F32)],
        scratch_shapes=[pltpu.VMEM((tq, 1), F32), pltpu.VMEM((tq, 1), F32), pltpu.VMEM((tq, KV_LORA), F32),
                        pltpu.VMEM((2, tq, tq), F32)],
        compiler_params=_cparams(("parallel", "parallel")))(ql, qr, ckv, kr)


def _attention_bwd(ql, qr, ckv, kr, out, lse, dout):
    t = ckv.shape[0]
    tq = min(_ATT_TILE, t)

    def body(ql_ref, qr_ref, ckv_ref, kr_ref, o_ref, lse_ref, do_ref, dql_ref, dqr_ref, dckv_ref, dkr_ref,
             dql_scr, dqr_scr, s_scr, dp_scr):
        h, qi = pl.program_id(0), pl.program_id(1)

        @pl.when((h == 0) & (qi == 0))
        def _():
            dckv_ref[...] = jnp.zeros_like(dckv_ref)
            dkr_ref[...] = jnp.zeros_like(dkr_ref)

        q_lat, q_rope = ql_ref[...], qr_ref[...]
        d_o = do_ref[...].astype(_CDT)
        lse_v = lse_ref[...]
        dsum = jnp.sum(do_ref[...] * o_ref[...], axis=-1, keepdims=True)
        dql_scr[...] = jnp.zeros_like(dql_scr)
        dqr_scr[...] = jnp.zeros_like(dqr_scr)

        def produce(j):
            s_scr[j % 2] = _att_scores(q_lat, q_rope, ckv_ref, kr_ref, j, tq)
            dp_scr[j % 2] = _dot(d_o, ckv_ref[_kv_rows(j, tq), :], _NT)

        def consume(j, s):
            ks = _kv_rows(j, tq)
            p = jnp.exp(s - lse_v)
            ds = (p * (dp_scr[j % 2] - dsum) * ATT_SCALE).astype(_CDT)
            pb = p.astype(_CDT)
            dql_scr[...] += _dot(ds, ckv_ref[ks, :], _NN)
            dqr_scr[...] += _dot(ds, kr_ref[ks, :], _NN)
            dckv_ref[ks, :] += _dot(pb, d_o, _TN) + _dot(ds, q_lat, _TN)
            dkr_ref[ks, :] += _dot(ds, q_rope, _TN)

        produce(0)

        def loop_body(j, carry):
            produce(j + 1)
            consume(j, s_scr[j % 2])
            return carry

        lax.fori_loop(0, qi, loop_body, 0)
        consume(qi, _diag_mask(s_scr[qi % 2]))
        dql_ref[...] = dql_scr[...].astype(dql_ref.dtype)
        dqr_ref[...] = dqr_scr[...]

    lat = pl.BlockSpec((None, tq, KV_LORA), lambda h, i: (h, i, 0))
    rope = pl.BlockSpec((tq, ROPE_PAD), lambda h, i: (i, h))
    kfull = pl.BlockSpec((t, KV_LORA), lambda h, i: (0, 0))
    rfull = pl.BlockSpec((t, ROPE_PAD), lambda h, i: (0, 0))
    return pl.pallas_call(
        body, name="attention_bwd", grid=(HEADS, t // tq),
        in_specs=[lat, rope, kfull, rfull, lat, pl.BlockSpec((None, tq, 1), lambda h, i: (h, i, 0)), lat],
        out_specs=[lat, rope, kfull, rfull],
        out_shape=[jax.ShapeDtypeStruct((HEADS, t, KV_LORA), _CDT), jax.ShapeDtypeStruct((t, HEADS * ROPE_PAD), F32),
                   jax.ShapeDtypeStruct((t, KV_LORA), F32), jax.ShapeDtypeStruct((t, ROPE_PAD), F32)],
        scratch_shapes=[pltpu.VMEM((tq, KV_LORA), F32), pltpu.VMEM((tq, ROPE_PAD), F32),
                        pltpu.VMEM((2, tq, tq), F32), pltpu.VMEM((2, tq, tq), F32)],
        compiler_params=_cparams(("arbitrary", "arbitrary")))(ql, qr, ckv, kr, out, lse, dout)


def _gated_norm(o, z, w):
    return _rms_norm(o, w) * _silu(z)


def _mla_pre(ckv, krp, cq, cosb, sinb, qw, kw):
    return _rms_norm(cq, qw), _rms_norm(ckv, kw), _rope(krp, cosb, sinb)


def _merge(gg, y_dn, y_mla):
    return _sigmoid(gg[:, :D_MODEL]) * y_dn + _sigmoid(gg[:, D_MODEL:]) * y_mla


def _ln1(xv, attn_out, g, b):
    return _layer_norm(ALPHA * xv + attn_out, g, b)


def _final(h1, ffn, gate_pre, ple_proj, g, b):
    return _layer_norm(ALPHA * h1 + ffn + _sigmoid(gate_pre) * ple_proj, g, b)


def _swiglu(gt, up):
    return _silu(gt) * up


def _local_step(x, p, cosb, sinb, target, wt, sp):
    t = x.shape[0]
    bf = _CDT
    xb = x.astype(bf)
    g = {}

    qkv_pre = _mm2(xb, wt['qkv'], name="f_qkv")
    z = _mm2(xb, wt['z'], name="f_z")
    gg = _mm2(xb, wt['gg'], name="f_gg")
    pm = _mm2(xb, wt['mla'], name="f_mla")
    qkv_act = _conv_silu(qkv_pre, sp['conv_w'])
    u, w_, qd, kt, intra, gl, t_inv = _delta_local(qkv_act, pm, sp['a_log'], sp['dt_bias'])
    o_dn, sall = _delta_scan(u, w_, qd, kt, intra, gl)
    (og,) = _rowwise(lambda h, o, zz, w: (_gated_norm(o, zz, w),),
                     [(o_dn, LANE, 0, True), (z, LANE, 0, True)], [sp['dn_norm_w']],
                     [(D_MODEL, LANE, True, bf)], name="f_gated_norm", tm=512, heads=HEADS)
    y_dn = _mm2(og, wt['br_dn'], name="f_br_dn")

    c_q, c_kv, k_rope = _rowwise(
        lambda h, *a: _mla_pre(*a),
        [(pm, KV_LORA, 0, False), (pm, ROPE_PAD, 2, False), (pm, Q_LORA, 2, False),
         (cosb, ROPE_PAD, 0, False), (sinb, ROPE_PAD, 0, False)],
        [sp['q_norm_w'], sp['kv_norm_w']],
        [(Q_LORA, Q_LORA, False, bf), (KV_LORA, KV_LORA, False, bf), (ROPE_PAD, ROPE_PAD, False, bf)], name="f_mla_pre")
    q_nope = _mm2(c_q, wt['uq_nope'], name="f_uq_nope", out_dtype=bf)
    q_rope_pre = _mm2(c_q, wt['uq_rope'], name="f_uq_rope")
    (q_rope,) = _rowwise(lambda h, q, c, s: (_rope(q, c, s),),
                         [(q_rope_pre, HEADS * ROPE_PAD, 0, False), (cosb, ROPE_PAD, 0, False), (sinb, ROPE_PAD, 0, False)],
                         [], [(HEADS * ROPE_PAD, HEADS * ROPE_PAD, False, bf)], name="f_q_rope")
    q_lat = _mm(q_nope, wt['uk'], name="f_q_lat", tb=True, heads=HEADS, a_head='col', b_head='lead', out_head='lead',
                dims=(t, KV_LORA, NOPE), out_dtype=bf)
    out_lat, lse = _attention(q_lat, q_rope, c_kv, k_rope)
    o_mla = _mm(out_lat, wt['uv'], name="f_o_mla", heads=HEADS, a_head='lead', b_head='lead', out_head='col',
                dims=(t, NOPE, KV_LORA), out_dtype=bf)
    y_mla = _mm2(o_mla, wt['br_mla'], name="f_br_mla")

    (mixed,) = _rowwise(lambda h, *a: (_merge(*a),),
                        [(gg, 2 * D_MODEL, 0, False), (y_dn, D_MODEL, 0, False), (y_mla, D_MODEL, 0, False)],
                        [], [(D_MODEL, D_MODEL, False, bf)], name="f_merge")
    attn_out = _mm2(mixed, wt['o'], name="f_o")
    h1, h1b = _rowwise(lambda h, *a: (_ln1(*a),) * 2, [(x, D_MODEL, 0, False), (attn_out, D_MODEL, 0, False)],
                       [sp['ln1_g'], sp['ln1_b']], [(D_MODEL, D_MODEL, False, F32), (D_MODEL, D_MODEL, False, bf)],
                       name="f_ln1")
    ffn_in = _mm2(h1b, wt['ffn_in'], name="f_ffn_in")
    (act,) = _rowwise(lambda h, gt, up: (_swiglu(gt, up),),
                      [(ffn_in, FFN_HIDDEN, 0, False), (ffn_in, FFN_HIDDEN, 1, False)], [],
                      [(FFN_HIDDEN, FFN_HIDDEN, False, bf)], name="f_swiglu")
    ffn = _mm2(act, wt['ffn_out'], name="f_ffn_out")
    gate_pre = _mm2(h1b, wt['ple_gate'], name="f_ple_gate")
    pb = p.astype(bf)
    ple_proj = _mm2(pb, wt['ple'], name="f_ple")

    def final_fn(h, h1v, ffnv, gpv, ppv, tgt, gv, bv):
        y, vjp = jax.vjp(_final, h1v, ffnv, gpv, ppv, gv, bv)
        err = y - tgt
        dh1, dffn, dgp, dpp, dg, db = vjp(err * (1.0 / D_MODEL))
        sq = err * err
        lanes = sq[:, :LANE]
        for j in range(1, D_MODEL // LANE):
            lanes = lanes + sq[:, j * LANE:(j + 1) * LANE]
        loss = jnp.sum(lanes, axis=0, keepdims=True) * (0.5 / D_MODEL)
        return dffn, dffn, dgp, dpp, dg, db, loss

    dpre2, dpre2b, dgate_pre, dple_proj, g['ln2_g'], g['ln2_b'], loss_lanes = _rowwise(
        final_fn, [(a, D_MODEL, 0, False) for a in (h1, ffn, gate_pre, ple_proj, target)],
        [sp['ln2_g'], sp['ln2_b']],
        [(D_MODEL, D_MODEL, False, F32)] + [(D_MODEL, D_MODEL, False, bf)] * 3,
        [(1, D_MODEL), (1, D_MODEL), (1, LANE)], name="b_final")

    g['ple'] = _mm2(pb, dple_proj, ta=True, name="g_ple")
    g['ple_gate'] = _mm2(h1b, dgate_pre, ta=True, name="g_ple_gate")
    g['ffn_out'] = _mm2(act, dpre2b, ta=True, name="g_ffn_out")
    dact = _mm2(dpre2b, wt['ffn_out'], tb=True, name="b_dact")

    def swiglu_bwd(h, gt, up, d):
        _, vjp = jax.vjp(_swiglu, gt, up)
        dgt, dup = vjp(d)
        return (jnp.concatenate([dgt, dup], axis=1),)

    (dffn_in,) = _rowwise(swiglu_bwd, [(ffn_in, FFN_HIDDEN, 0, False), (ffn_in, FFN_HIDDEN, 1, False),
                                       (dact, FFN_HIDDEN, 0, False)], [],
                          [(2 * FFN_HIDDEN, 2 * FFN_HIDDEN, False, bf)], name="b_swiglu")
    g['ffn_in'] = _mm2(h1b, dffn_in, ta=True, name="g_ffn_in")
    dh1 = _mm2(dffn_in, wt['ffn_in'], tb=True, name="b_dh1_ffn", add=dpre2, add_scale=ALPHA)
    dh1 = _mm2(dgate_pre, wt['ple_gate'], tb=True, name="b_dh1_gate", add=dh1)

    def ln1_bwd(h, xv, ao, d, gv, bv):
        _, vjp = jax.vjp(_ln1, xv, ao, gv, bv)
        _, dao, dg, db = vjp(d)
        return dao, dao, dg, db

    dpre1, dpre1b, g['ln1_g'], g['ln1_b'] = _rowwise(
        ln1_bwd, [(x, D_MODEL, 0, False), (attn_out, D_MODEL, 0, False), (dh1, D_MODEL, 0, False)],
        [sp['ln1_g'], sp['ln1_b']], [(D_MODEL, D_MODEL, False, F32), (D_MODEL, D_MODEL, False, bf)],
        [(1, D_MODEL), (1, D_MODEL)], name="b_ln1")

    g['o'] = _mm2(mixed, dpre1b, ta=True, name="g_o")
    dmixed = _mm2(dpre1b, wt['o'], tb=True, name="b_dmixed")

    def merge_bwd(h, ggv, yd, ym, d):
        _, vjp = jax.vjp(_merge, ggv, yd, ym)
        return vjp(d)

    dgg, dy_dn, dy_mla = _rowwise(
        merge_bwd, [(gg, 2 * D_MODEL, 0, False), (y_dn, D_MODEL, 0, False), (y_mla, D_MODEL, 0, False),
                    (dmixed, D_MODEL, 0, False)], [],
        [(2 * D_MODEL, 2 * D_MODEL, False, bf), (D_MODEL, D_MODEL, False, bf), (D_MODEL, D_MODEL, False, bf)],
        name="b_merge")
    g['br_dn'] = _mm2(og, dy_dn, ta=True, name="g_br_dn")
    dog = _mm2(dy_dn, wt['br_dn'], tb=True, name="b_dog")
    g['br_mla'] = _mm2(o_mla, dy_mla, ta=True, name="g_br_mla")
    do_mla = _mm2(dy_mla, wt['br_mla'], tb=True, name="b_do_mla", out_dtype=bf)

    dout_lat = _mm(do_mla, wt['uv'], name="b_dout_lat", tb=True, heads=HEADS, a_head='col', b_head='lead',
                   out_head='lead', dims=(t, KV_LORA, NOPE))
    g['uv'] = _mm(out_lat, do_mla, name="g_uv", ta=True, heads=HEADS, a_head='lead', b_head='col', out_head='lead',
                  dims=(KV_LORA, NOPE, t))
    dq_lat, dq_rope, dckv_att, dkr_att = _attention_bwd(q_lat, q_rope, c_kv, k_rope, out_lat, lse, dout_lat)
    dq_nope = _mm(dq_lat, wt['uk'], name="b_dq_nope", heads=HEADS, a_head='lead', b_head='lead', out_head='col',
                  dims=(t, NOPE, KV_LORA), out_dtype=bf)
    g['uk'] = _mm(dq_lat, q_nope, name="g_uk", ta=True, heads=HEADS, a_head='lead', b_head='col', out_head='lead',
                  dims=(KV_LORA, NOPE, t))
    (dq_rope_pre,) = _rowwise(lambda h, d, c, s: (_rope_bwd(d, c, s),),
                              [(dq_rope, HEADS * ROPE_PAD, 0, False), (cosb, ROPE_PAD, 0, False), (sinb, ROPE_PAD, 0, False)],
                              [], [(HEADS * ROPE_PAD, HEADS * ROPE_PAD, False, bf)], name="b_q_rope")
    g['uq_nope'] = _mm2(c_q, dq_nope, ta=True, name="g_uq_nope")
    g['uq_rope'] = _mm2(c_q, dq_rope_pre, ta=True, name="g_uq_rope")
    dc_q = _mm2(dq_nope, wt['uq_nope'], tb=True, name="b_dcq_nope")
    dc_q = _mm2(dq_rope_pre, wt['uq_rope'], tb=True, name="b_dcq_rope", add=dc_q)

    def gated_norm_bwd(h, o, zz, d, w):
        _, vjp = jax.vjp(_gated_norm, o, zz, w)
        return vjp(d)

    do_dn, dz, g['dn_norm_w'] = _rowwise(
        gated_norm_bwd, [(o_dn, LANE, 0, True), (z, LANE, 0, True), (dog, LANE, 0, True)], [sp['dn_norm_w']],
        [(D_MODEL, LANE, True, F32), (D_MODEL, LANE, True, bf)], [(1, LANE)], name="b_gated_norm", tm=512, heads=HEADS)
    du, dw, dqd, dkt, dintra, dgl = _delta_scan_bwd(u, w_, qd, kt, intra, gl, sall, do_dn)
    dq_a, dk_a, dv_a, dba, g['a_log'], g['dt_bias'] = _delta_local_bwd(
        qkv_act, pm, sp['a_log'], sp['dt_bias'], t_inv, du, dw, dqd, dkt, dintra, dgl)
    dqkv_act = jnp.concatenate([dq_a, dk_a, dv_a], axis=1)
    dqkv_pre, g['conv_w'] = _conv_silu_bwd(qkv_pre, sp['conv_w'], dqkv_act)

    def mla_pre_bwd(h, ckv, krp, cq, cosv, sinv, dcq, dckv, dkr, qw, kw):
        _, vjp = jax.vjp(lambda a, b, c, d, e: (_rms_norm(c, d), _rms_norm(a, e)), ckv, krp, cq, qw, kw)
        dckv_p, _, dcq_p, dqw, dkw = vjp((dcq, dckv))
        dkr_p = _rope_bwd(dkr, cosv, sinv)
        dpm = jnp.concatenate([dckv_p, dkr_p, jnp.zeros((ckv.shape[0], 3 * LANE), F32), dcq_p], axis=1)
        return dpm, dqw, dkw

    dpm_main, g['q_norm_w'], g['kv_norm_w'] = _rowwise(
        mla_pre_bwd,
        [(pm, KV_LORA, 0, False), (pm, ROPE_PAD, 2, False), (pm, Q_LORA, 2, False),
         (cosb, ROPE_PAD, 0, False), (sinb, ROPE_PAD, 0, False),
         (dc_q, Q_LORA, 0, False), (dckv_att, KV_LORA, 0, False), (dkr_att, ROPE_PAD, 0, False)],
        [sp['q_norm_w'], sp['kv_norm_w']], [(1152, 1152, False, F32)], [(1, Q_LORA), (1, KV_LORA)], name="b_mla_pre")
    (dpm,) = _rowwise(
        lambda h, a, b: (jnp.concatenate([a[:, :3 * LANE], b, a[:, 4 * LANE:]], axis=1),),
        [(dpm_main, 1152, 0, False), (dba, LANE, 0, False)], [], [(1152, 1152, False, bf)], name="b_dpm")

    g['qkv'] = _mm2(xb, dqkv_pre, ta=True, name="g_qkv")
    g['z'] = _mm2(xb, dz, ta=True, name="g_z")
    g['gg'] = _mm2(xb, dgg, ta=True, name="g_gg")
    g['mla'] = _mm2(xb, dpm, ta=True, name="g_mla")
    dx = _mm2(dqkv_pre, wt['qkv'], tb=True, name="b_dx_qkv", add=dpre1, add_scale=ALPHA)
    dx = _mm2(dz, wt['z'], tb=True, name="b_dx_z", add=dx)
    dx = _mm2(dgg, wt['gg'], tb=True, name="b_dx_gg", add=dx)
    dx = _mm2(dpm, wt['mla'], tb=True, name="b_dx_mla", add=dx)
    return loss_lanes, dx, g


_IN_SIZES = (QKV_W, HEADS * DN_DK, HEADS, HEADS, Q_LORA, KV_LORA, ROPE, D_MODEL, D_MODEL)


def _rope_tables(positions):
    inv_freq = ROPE_BASE ** (-jnp.arange(0, ROPE, 2, dtype=F32) / ROPE)
    ang = positions.astype(F32)[:, None] * inv_freq
    cos, sin = jnp.cos(ang), jnp.sin(ang)
    zeros = jnp.zeros((positions.shape[0], ROPE_PAD - ROPE), F32)
    return jnp.concatenate([cos, cos, zeros], axis=1), jnp.concatenate([-sin, sin, zeros], axis=1)


def _prep_weights(full):
    w_in = full['w_in']
    dt = w_in.dtype
    offs = [0]
    for s in _IN_SIZES:
        offs.append(offs[-1] + s)
    qkv, z, wb, wa, cq, ckv, kr, gd, gm = [w_in[:, offs[i]:offs[i + 1]] for i in range(len(_IN_SIZES))]
    zc = lambda n: jnp.zeros((D_MODEL, n), dt)
    w_uq = full['w_uq']
    wt = {
        'qkv': qkv, 'z': z, 'gg': jnp.concatenate([gd, gm], axis=1),
        'mla': jnp.concatenate([ckv, kr, zc(ROPE_PAD - ROPE), wb, wa, zc(LANE - 2 * HEADS), zc(2 * LANE), cq], axis=1),
        'uq_nope': w_uq[:, :, :NOPE].reshape(Q_LORA, HEADS * NOPE),
        'uq_rope': jnp.pad(w_uq[:, :, NOPE:], ((0, 0), (0, 0), (0, ROPE_PAD - ROPE))).reshape(Q_LORA, HEADS * ROPE_PAD),
        'uk': jnp.transpose(full['w_uk'], (1, 0, 2)), 'uv': jnp.transpose(full['w_uv'], (1, 0, 2)),
        'br_dn': full['w_br_dn'], 'br_mla': full['w_br_mla'], 'o': full['w_o'], 'ffn_in': full['w_ffn_in'],
        'ffn_out': full['w_ffn_out'], 'ple': full['w_ple'], 'ple_gate': full['w_ple_gate'],
    }
    return wt


def _prep_small(small):
    pad = lambda v: jnp.pad(v, (0, LANE - v.shape[0]))[None, :]
    return {
        'conv_w': small['conv_w'], 'a_log': pad(small['dn_a_log']), 'dt_bias': pad(small['dn_dt_bias']),
        'dn_norm_w': small['dn_norm_w'][None, :], 'q_norm_w': small['q_norm_w'][None, :],
        'kv_norm_w': small['kv_norm_w'][None, :], 'ln1_g': small['ln1_g'][None, :], 'ln1_b': small['ln1_b'][None, :],
        'ln2_g': small['ln2_g'][None, :], 'ln2_b': small['ln2_b'][None, :],
    }


def _unprep_grads(g):
    mla = g['mla']
    ba0 = KV_LORA + ROPE_PAD
    cq0 = ba0 + 3 * LANE
    w_in = jnp.concatenate([
        g['qkv'], g['z'], mla[:, ba0:ba0 + HEADS], mla[:, ba0 + HEADS:ba0 + 2 * HEADS], mla[:, cq0:cq0 + Q_LORA],
        mla[:, :KV_LORA], mla[:, KV_LORA:KV_LORA + ROPE], g['gg']], axis=1)
    w_uq = jnp.concatenate([g['uq_nope'].reshape(Q_LORA, HEADS, NOPE),
                            g['uq_rope'].reshape(Q_LORA, HEADS, ROPE_PAD)[:, :, :ROPE]], axis=2)
    return {
        'w_in': w_in, 'conv_w': g['conv_w'], 'dn_a_log': g['a_log'][0, :HEADS], 'dn_dt_bias': g['dt_bias'][0, :HEADS],
        'dn_norm_w': g['dn_norm_w'][0], 'q_norm_w': g['q_norm_w'][0], 'w_uq': w_uq, 'kv_norm_w': g['kv_norm_w'][0],
        'w_uk': jnp.transpose(g['uk'], (1, 0, 2)), 'w_uv': jnp.transpose(g['uv'], (1, 0, 2)),
        'w_br_dn': g['br_dn'], 'w_br_mla': g['br_mla'], 'w_o': g['o'], 'ln1_g': g['ln1_g'][0], 'ln1_b': g['ln1_b'][0],
        'w_ffn_in': g['ffn_in'], 'w_ffn_out': g['ffn_out'], 'w_ple': g['ple'], 'w_ple_gate': g['ple_gate'],
        'ln2_g': g['ln2_g'][0], 'ln2_b': g['ln2_b'][0],
    }


_FLAT_PIECES = (
    ('w_in', 1716, (D_MODEL, 1716), 1), ('w_ffn_in', 1408, (D_MODEL, 1408), 1), ('w_ffn_out', 704, (704, D_MODEL), 0),
    ('w_br_dn', 256, (256, D_MODEL), 0), ('w_br_mla', 256, (256, D_MODEL), 0), ('w_o', 256, (256, D_MODEL), 0),
    ('w_ple_gate', 256, (256, D_MODEL), 0), ('w_uq', 144, (96, HEADS, NOPE + ROPE), 0),
    ('w_uk', 64, (64, HEADS, NOPE), 0), ('w_uv', 64, (64, HEADS, NOPE), 0), ('w_ple', 64, (PLE_DIM, 256), 1),
)
_CONV_FLAT_ROWS = 6
_CONV_SHARD = QKV_W // N_SHARD


def _pad16(rows):
    return -(-rows // 16) * 16


def _flat_offsets():
    offs, o = {}, 0
    for name, rows, _, _ in _FLAT_PIECES:
        offs[name] = o
        o += _pad16(rows)
    return offs, o


def _pad_rows(part, axis):
    rows = part.shape[axis]
    widths = [(0, 0)] * part.ndim
    widths[axis] = (0, _pad16(rows) - rows)
    return jnp.pad(part, widths)


def _pack_shard(shards, conv_w):
    parts = [_pad_rows(shards[name].astype(_CDT).reshape(rows, FLAT_W), 0) for name, rows, _, _ in _FLAT_PIECES]
    conv = lax.bitcast_convert_type(conv_w, jnp.bfloat16).reshape(_CONV_FLAT_ROWS, FLAT_W).astype(_CDT)
    parts.append(_pad_rows(conv, 0))
    used = sum(p.shape[0] for p in parts)
    parts.append(jnp.zeros((FLAT_ROWS - used, FLAT_W), _CDT))
    return jnp.concatenate(parts, axis=0)


def _unpack_gathered(gathered):
    offs, used = _flat_offsets()
    full = {}
    for name, rows, shape, axis in _FLAT_PIECES:
        pieces = [gathered[s, offs[name]:offs[name] + rows].reshape(shape) for s in range(N_SHARD)]
        full[name] = jnp.concatenate(pieces, axis=axis)
    conv = [lax.bitcast_convert_type(gathered[s, used:used + _CONV_FLAT_ROWS].astype(jnp.bfloat16).reshape(DN_CONV, _CONV_SHARD, 2), F32)
            for s in range(N_SHARD)]
    return full, jnp.concatenate(conv, axis=1)


def _pack_grads(gw):
    parts = []
    for name, rows, shape, axis in _FLAT_PIECES:
        gfull = gw[name]
        if axis == 1:
            cut = gfull.reshape(gfull.shape[0], N_SHARD, gfull.shape[1] // N_SHARD)
            cut = jnp.transpose(cut, (1, 0, 2))
        else:
            cut = gfull.reshape((N_SHARD, gfull.shape[0] // N_SHARD) + gfull.shape[1:])
        parts.append(_pad_rows(cut.reshape(N_SHARD, rows, FLAT_W), 1))
    used = sum(p.shape[1] for p in parts)
    parts.append(jnp.zeros((N_SHARD, FLAT_ROWS - used, FLAT_W), F32))
    return jnp.concatenate(parts, axis=1)


def _unpack_reduced(flat):
    offs, _ = _flat_offsets()
    return {name: flat[offs[name]:offs[name] + rows].reshape(shape) for name, rows, shape, _ in _FLAT_PIECES}


_HBM = pl.BlockSpec(memory_space=pltpu.HBM)


def _place():
    x, y, c = lax.axis_index("x"), lax.axis_index("y"), lax.axis_index("c")
    chips = [(1 - x, y), (x, 1 - y), (1 - x, 1 - y)]
    return x, y, c, chips


def _remote(src, dst, send_sems, recv_sems, k, to):
    return pltpu.make_async_remote_copy(src_ref=src, dst_ref=dst, send_sem=send_sems.at[k], recv_sem=recv_sems.at[k],
                                        device_id=to, device_id_type=_MESH)


def _gather_shards(flat):
    rows, width = flat.shape
    half = rows // 2

    def body(in_ref, out_ref, send_sems, recv_sems):
        x, y, c, chips = _place()
        me, sibling = (x, y, c), (x, y, 1 - c)

        def slot(cx, cy, hf):
            return out_ref.at[2 * cx + cy, pl.ds(pl.multiple_of(hf * half, 16), half), :]

        my_half = in_ref.at[pl.ds(pl.multiple_of(c * half, 16), half), :]
        first = [_remote(my_half, slot(x, y, c), send_sems, recv_sems, j, (cx, cy, c)) for j, (cx, cy) in enumerate(chips)]
        for cp in first:
            cp.start()
        passed = [_remote(slot(cx, cy, c), slot(cx, cy, c), send_sems, recv_sems, 3 + j, sibling)
                  for j, (cx, cy) in enumerate(chips)]
        for j, (cx, cy) in enumerate(chips):
            _remote(slot(cx, cy, c), slot(cx, cy, c), send_sems, recv_sems, j, me).wait_recv()
            passed[j].start()
        for j, (cx, cy) in enumerate(chips):
            _remote(slot(cx, cy, 1 - c), slot(cx, cy, 1 - c), send_sems, recv_sems, 3 + j, me).wait_recv()
        for cp in first + passed:
            cp.wait_send()

    gathered = pl.pallas_call(
        body, name="gather_shards", out_shape=jax.ShapeDtypeStruct((N_SHARD, rows, width), flat.dtype),
        in_specs=[_HBM], out_specs=_HBM,
        scratch_shapes=[pltpu.SemaphoreType.DMA((6,)), pltpu.SemaphoreType.DMA((6,))],
    )(flat)
    me = 2 * lax.axis_index("x") + lax.axis_index("y")
    return lax.dynamic_update_slice(gathered, flat[None], (me, 0, 0))


def _reduce_pair_exchange(gflat):
    _, rows, width = gflat.shape
    half = rows // 2

    def body(g_ref, recv_ref, send_sems, recv_sems):
        x, y, c, _ = _place()
        src = g_ref.at[:, pl.ds(pl.multiple_of((1 - c) * half, 8), half), :]
        cp = _remote(src, recv_ref, send_sems, recv_sems, 0, (x, y, 1 - c))
        cp.start()
        cp.wait()

    return pl.pallas_call(
        body, name="reduce_pair_exchange", out_shape=jax.ShapeDtypeStruct((N_SHARD, half, width), gflat.dtype),
        in_specs=[_HBM], out_specs=_HBM,
        scratch_shapes=[pltpu.SemaphoreType.DMA((1,)), pltpu.SemaphoreType.DMA((1,))],
    )(gflat)


_ADD_ROWS = 384


def _pair_add(gflat, recv, c_arr):
    _, rows, width = gflat.shape
    half = rows // 2
    nt = half // _ADD_ROWS

    def body(c_ref, a_ref, b_ref, o_ref):
        o_ref[...] = (a_ref[...] + b_ref[...]).astype(o_ref.dtype)

    blk = lambda f: pl.BlockSpec((None, _ADD_ROWS, width), f)
    return pl.pallas_call(
        body, name="pair_add", out_shape=jax.ShapeDtypeStruct((N_SHARD, half, width), jnp.bfloat16),
        grid_spec=pltpu.PrefetchScalarGridSpec(
            num_scalar_prefetch=1, grid=(N_SHARD, nt),
            in_specs=[blk(lambda s, i, c: (s, c[0] * nt + i, 0)), blk(lambda s, i, c: (s, i, 0))],
            out_specs=blk(lambda s, i, c: (s, i, 0))),
        compiler_params=_cparams(("parallel", "parallel")))(c_arr, gflat, recv)


def _reduce_chip_exchange(part):
    _, half, width = part.shape

    def body(p_ref, out_ref, send_sems, recv_sems):
        x, y, c, chips = _place()
        sends = [_remote(p_ref.at[2 * cx + cy], out_ref.at[j], send_sems, recv_sems, j, (cx, cy, c))
                 for j, (cx, cy) in enumerate(chips)]
        for cp in sends:
            cp.start()
        for j in range(len(chips)):
            _remote(p_ref.at[0], out_ref.at[j], send_sems, recv_sems, j, (x, y, c)).wait_recv()
        for cp in sends:
            cp.wait_send()

    return pl.pallas_call(
        body, name="reduce_chip_exchange", out_shape=jax.ShapeDtypeStruct((3, half, width), part.dtype),
        in_specs=[_HBM], out_specs=_HBM,
        scratch_shapes=[pltpu.SemaphoreType.DMA((3,)), pltpu.SemaphoreType.DMA((3,))],
    )(part)


def _chip_add(part, recv, me_arr):
    _, half, width = part.shape

    def body(me_ref, own, a0, a1, a2, o_ref):
        f = lambda r: r[...].astype(F32)
        o_ref[...] = ((f(own) + f(a0)) + f(a1)) + f(a2)

    specs = [pl.BlockSpec((None, _ADD_ROWS, width), lambda i, me: (me[0], i, 0))]
    specs += [pl.BlockSpec((None, _ADD_ROWS, width), functools.partial(lambda i, me, k: (k, i, 0), k=k)) for k in range(3)]
    return pl.pallas_call(
        body, name="chip_add", out_shape=jax.ShapeDtypeStruct((half, width), F32),
        grid_spec=pltpu.PrefetchScalarGridSpec(
            num_scalar_prefetch=1, grid=(half // _ADD_ROWS,), in_specs=specs,
            out_specs=pl.BlockSpec((_ADD_ROWS, width), lambda i, me: (i, 0))),
        compiler_params=_cparams(("parallel",)))(me_arr, part, recv, recv, recv)


def _reduce_pair_share(rhalf):
    half, width = rhalf.shape

    def body(r_ref, out_ref, send_sems, recv_sems):
        x, y, c, _ = _place()
        cp = _remote(r_ref, out_ref, send_sems, recv_sems, 0, (x, y, 1 - c))
        cp.start()
        cp.wait()

    return pl.pallas_call(
        body, name="reduce_pair_share", out_shape=jax.ShapeDtypeStruct((half, width), rhalf.dtype),
        in_specs=[_HBM], out_specs=_HBM,
        scratch_shapes=[pltpu.SemaphoreType.DMA((1,)), pltpu.SemaphoreType.DMA((1,))],
    )(rhalf)


def _small_allreduce(buf):
    r, width = buf.shape
    n_dev = 8

    def body(x_ref, all_ref, sum_ref, send_sems, recv_sems, local_sem):
        x, y, c, chips = _place()
        me, sibling = (x, y, c), (x, y, 1 - c)

        def rows(px, py, pc):
            return all_ref.at[pl.ds(pl.multiple_of((4 * px + 2 * py + pc) * r, 8), r), :]

        def copy(k, block, to, src=None):
            return _remote(rows(*block) if src is None else src, rows(*block), send_sems, recv_sems, k, to)

        mine = pltpu.make_async_copy(x_ref, rows(*me), local_sem)
        mine.start()
        first = [copy(0, me, sibling, src=x_ref)]
        first += [copy(1 + j, me, (*chip, c), src=x_ref) for j, chip in enumerate(chips)]
        for cp in first:
            cp.start()
        passed = [copy(4 + j, (*chip, c), sibling) for j, chip in enumerate(chips)]
        for j, chip in enumerate(chips):
            copy(1 + j, (*chip, c), me).wait_recv()
            passed[j].start()
        copy(0, sibling, me).wait_recv()
        for j, chip in enumerate(chips):
            copy(4 + j, (*chip, 1 - c), me).wait_recv()
        for cp in first + passed:
            cp.wait_send()
        mine.wait()
        total = all_ref[0:r, :]
        for k in range(1, n_dev):
            total = total + all_ref[k * r:(k + 1) * r, :]
        sum_ref[...] = total

    vm = pl.BlockSpec(memory_space=pltpu.VMEM)
    _, total = pl.pallas_call(
        body, name="small_allreduce",
        out_shape=[jax.ShapeDtypeStruct((n_dev * r, width), buf.dtype), jax.ShapeDtypeStruct((r, width), buf.dtype)],
        in_specs=[vm], out_specs=[vm, vm],
        scratch_shapes=[pltpu.SemaphoreType.DMA((7,)), pltpu.SemaphoreType.DMA((7,)), pltpu.SemaphoreType.DMA],
    )(buf)
    return total


def _row_tile(rows, cap):
    if rows <= cap:
        return rows
    t = (cap // 8) * 8
    while t >= 8:
        if rows % t == 0:
            return t
        t -= 8
    return rows


def _adamw(w, g, m, v, name):
    shape = w.shape
    cols = shape[-1] if len(shape) <= 3 else shape[-2] * shape[-1]
    w2, g2, m2, v2 = (a.reshape(-1, cols) for a in (w, g, m, v))
    rows = w2.shape[0]
    tr = _row_tile(rows, 256)

    def body(w_ref, g_ref, m_ref, v_ref, d_ref, mo_ref, vo_ref):
        gv = g_ref[...]
        mn = ADAM_B1 * m_ref[...] + (1.0 - ADAM_B1) * gv
        vn = ADAM_B2 * v_ref[...] + (1.0 - ADAM_B2) * (gv * gv)
        m_hat = mn / (1.0 - ADAM_B1 ** ADAM_STEP)
        v_hat = vn / (1.0 - ADAM_B2 ** ADAM_STEP)
        d_ref[...] = -ADAM_LR * (m_hat / (jnp.sqrt(v_hat) + ADAM_EPS) + ADAM_WD * w_ref[...])
        mo_ref[...] = mn
        vo_ref[...] = vn

    blk = pl.BlockSpec((tr, cols), lambda i: (i, 0))
    outs = pl.pallas_call(
        body, name=name, grid=(rows // tr,), in_specs=[blk] * 4, out_specs=[blk] * 3,
        out_shape=[jax.ShapeDtypeStruct((rows, cols), F32)] * 3, compiler_params=_cparams(("parallel",)))(w2, g2, m2, v2)
    return tuple(o.reshape(shape) for o in outs)


_WEIGHT_NAMES = ('w_in', 'conv_w', 'dn_a_log', 'dn_dt_bias', 'dn_norm_w', 'q_norm_w', 'w_uq', 'kv_norm_w', 'w_uk',
                 'w_uv', 'w_br_dn', 'w_br_mla', 'w_o', 'ln1_g', 'ln1_b', 'w_ffn_in', 'w_ffn_out', 'w_ple',
                 'w_ple_gate', 'ln2_g', 'ln2_b')
_SMALL_NAMES = ('ln1_g', 'ln1_b', 'ln2_g', 'ln2_b', 'q_norm_w', 'kv_norm_w', 'dn_norm_w', 'dn_a_log', 'dn_dt_bias')
_SMALL_GROUP = 8
_CONV_SMALL_ROW = len(_SMALL_NAMES) * _SMALL_GROUP
_CONV_SMALL_ROWS = DN_CONV * QKV_W // FLAT_W


def _pack_small(gw):
    rows = [jnp.pad(gw[n][None, :], ((0, _SMALL_GROUP - 1), (0, FLAT_W - gw[n].shape[0]))) for n in _SMALL_NAMES]
    rows.append(jnp.pad(gw['conv_w'].reshape(_CONV_SMALL_ROWS, FLAT_W), ((0, SMALL_ROWS - _CONV_SMALL_ROW - _CONV_SMALL_ROWS), (0, 0))))
    return jnp.concatenate(rows, axis=0)


def kernel(x, p, positions, w_in, conv_w, dn_a_log, dn_dt_bias, dn_norm_w, q_norm_w, w_uq, kv_norm_w, w_uk, w_uv, w_br_dn, w_br_mla, w_o, ln1_g, ln1_b, w_ffn_in, w_ffn_out, w_ple, w_ple_gate, ln2_g, ln2_b, loss_target, m_w_in, m_conv_w, m_dn_a_log, m_dn_dt_bias, m_dn_norm_w, m_q_norm_w, m_w_uq, m_kv_norm_w, m_w_uk, m_w_uv, m_w_br_dn, m_w_br_mla, m_w_o, m_ln1_g, m_ln1_b, m_w_ffn_in, m_w_ffn_out, m_w_ple, m_w_ple_gate, m_ln2_g, m_ln2_b, v_w_in, v_conv_w, v_dn_a_log, v_dn_dt_bias, v_dn_norm_w, v_q_norm_w, v_w_uq, v_kv_norm_w, v_w_uk, v_w_uv, v_w_br_dn, v_w_br_mla, v_w_o, v_ln1_g, v_ln1_b, v_w_ffn_in, v_w_ffn_out, v_w_ple, v_w_ple_gate, v_ln2_g, v_ln2_b):
    ws = dict(w_in=w_in, conv_w=conv_w, dn_a_log=dn_a_log, dn_dt_bias=dn_dt_bias, dn_norm_w=dn_norm_w, q_norm_w=q_norm_w,
              w_uq=w_uq, kv_norm_w=kv_norm_w, w_uk=w_uk, w_uv=w_uv, w_br_dn=w_br_dn, w_br_mla=w_br_mla, w_o=w_o,
              ln1_g=ln1_g, ln1_b=ln1_b, w_ffn_in=w_ffn_in, w_ffn_out=w_ffn_out, w_ple=w_ple, w_ple_gate=w_ple_gate,
              ln2_g=ln2_g, ln2_b=ln2_b)
    ms = dict(w_in=m_w_in, conv_w=m_conv_w, dn_a_log=m_dn_a_log, dn_dt_bias=m_dn_dt_bias, dn_norm_w=m_dn_norm_w,
              q_norm_w=m_q_norm_w, w_uq=m_w_uq, kv_norm_w=m_kv_norm_w, w_uk=m_w_uk, w_uv=m_w_uv, w_br_dn=m_w_br_dn,
              w_br_mla=m_w_br_mla, w_o=m_w_o, ln1_g=m_ln1_g, ln1_b=m_ln1_b, w_ffn_in=m_w_ffn_in, w_ffn_out=m_w_ffn_out,
              w_ple=m_w_ple, w_ple_gate=m_w_ple_gate, ln2_g=m_ln2_g, ln2_b=m_ln2_b)
    vs = dict(w_in=v_w_in, conv_w=v_conv_w, dn_a_log=v_dn_a_log, dn_dt_bias=v_dn_dt_bias, dn_norm_w=v_dn_norm_w,
              q_norm_w=v_q_norm_w, w_uq=v_w_uq, kv_norm_w=v_kv_norm_w, w_uk=v_w_uk, w_uv=v_w_uv, w_br_dn=v_w_br_dn,
              w_br_mla=v_w_br_mla, w_o=v_w_o, ln1_g=v_ln1_g, ln1_b=v_ln1_b, w_ffn_in=v_w_ffn_in, w_ffn_out=v_w_ffn_out,
              w_ple=v_w_ple, w_ple_gate=v_w_ple_gate, ln2_g=v_ln2_g, ln2_b=v_ln2_b)
    mx, my, mc = lax.axis_index("x"), lax.axis_index("y"), lax.axis_index("c")

    flat = _pack_shard({name: ws[name][0] for name, _, _, _ in _FLAT_PIECES}, conv_w[0])
    full, conv_full = _unpack_gathered(_gather_shards(flat))
    wt = _prep_weights(full)
    small = {n: ws[n][0] for n in _SMALL_NAMES}
    small['conv_w'] = conv_full
    sp = _prep_small(small)
    cosb, sinb = _rope_tables(positions[0])

    loss_lanes, dx, g = _local_step(x[0], p[0, 0], cosb, sinb, loss_target[0], wt, sp)
    gw = _unprep_grads(g)
    loss = lax.psum(jnp.sum(loss_lanes), ("x", "y", "c"))

    gflat = _pack_grads(gw)
    part = _pair_add(gflat, _reduce_pair_exchange(gflat), jnp.reshape(mc, (1,)).astype(jnp.int32))
    rhalf = _chip_add(part, _reduce_chip_exchange(part), jnp.reshape(2 * mx + my, (1,)).astype(jnp.int32))
    rfull = lax.dynamic_update_slice(jnp.zeros((FLAT_ROWS, FLAT_W), F32), rhalf, (mc * FLAT_HALF, 0))
    rfull = lax.dynamic_update_slice(rfull, _reduce_pair_share(rhalf), ((1 - mc) * FLAT_HALF, 0))
    reduced = _unpack_reduced(rfull)
    tot = _small_allreduce(_pack_small(gw))
    gred = {name: reduced[name][None] for name, _, _, _ in _FLAT_PIECES}
    for i, n in enumerate(_SMALL_NAMES):
        gred[n] = tot[i * _SMALL_GROUP, :ws[n].shape[1]][None]
    conv_tot = tot[_CONV_SMALL_ROW:_CONV_SMALL_ROW + _CONV_SMALL_ROWS].reshape(DN_CONV, QKV_W)
    gred['conv_w'] = lax.dynamic_slice_in_dim(conv_tot, (2 * mx + my) * _CONV_SHARD, _CONV_SHARD, axis=1)[None]

    deltas, new_m, new_v = {}, {}, {}
    for n in _WEIGHT_NAMES:
        gred[n] = gred[n].reshape(ws[n].shape)
        deltas[n], new_m[n], new_v[n] = _adamw(ws[n], gred[n], ms[n], vs[n], "adamw_" + n)
    return (loss, dx[None], *[gred[n] for n in _WEIGHT_NAMES], *[deltas[n] for n in _WEIGHT_NAMES],
            *[new_m[n] for n in _WEIGHT_NAMES], *[new_v[n] for n in _WEIGHT_NAMES])
```

```python
import functools
import math

import jax
import jax.numpy as jnp
from jax import lax
from jax.experimental import pallas as pl
from jax.experimental.pallas import tpu as pltpu

F32 = jnp.float32
_CDT = jnp.bfloat16
_HI = lax.Precision.HIGHEST
_MESH = pl.DeviceIdType.MESH

D_MODEL = 1024
PLE_DIM = 256
HEADS = 8
DN_DK = 128
DN_CHUNK = 64
DN_CONV = 4
QKV_W = 3 * HEADS * DN_DK
Q_LORA = 384
KV_LORA = 256
NOPE = 128
ROPE = 64
ROPE_PAD = 128
FFN_HIDDEN = 2816
D_IN = 6864
ROPE_BASE = 10000.0
ALPHA = 2.0 ** 0.25
ATT_SCALE = (NOPE + ROPE) ** -0.5
NEG_BIG = -1e30
ADAM_LR, ADAM_B1, ADAM_B2, ADAM_EPS, ADAM_WD, ADAM_STEP = 0.001, 0.9, 0.999, 1e-08, 0.01, 10

LANE = 128
VMEM_LIMIT = 56 * 1024 * 1024
MM_VMEM_BUDGET = 40 * 1024 * 1024
N_SHARD = 4
FLAT_W = 1024
SMALL_ROWS = 88


def _tile(dim, cap):
    if dim <= cap:
        return dim
    t = (cap // LANE) * LANE
    while t >= LANE:
        if dim % t == 0:
            return t
        t -= LANE
    return dim


def _cparams(sem):
    return pltpu.CompilerParams(dimension_semantics=sem, vmem_limit_bytes=VMEM_LIMIT)


def _mm(a, b, *, name, ta=False, tb=False, add=None, add_scale=1.0, out_dtype=F32, heads=None,
        a_head=None, b_head=None, out_head=None, dims=None, tm=1408, tn=1408):
    m, n, k = dims
    tm, tn = _tile(m, tm), _tile(n, tn)
    sa, sb, so = a.dtype.itemsize, b.dtype.itemsize, jnp.dtype(out_dtype).itemsize

    def vmem_need(tk_):
        acc = tm * tn * 4 if tk_ < k else 0
        extra = 2 * tm * tn * 4 if add is not None else 0
        return 2 * (tm * tk_ * sa + tk_ * tn * sb) + 2 * tm * tn * so + acc + extra

    tk = k
    while vmem_need(tk) > MM_VMEM_BUDGET and tk > LANE:
        smaller = _tile(k, tk - LANE)
        if smaller >= tk:
            break
        tk = smaller
    nk = k // tk
    hgrid = () if heads is None else (heads,)
    off = len(hgrid)

    def spec(rows, cols, rtile, ctile, rsel, csel, layout):
        def idx(*g):
            h = g[0] if off else 0
            ri, ci = g[off + rsel], g[off + csel]
            if layout == 'lead':
                return (h, ri, ci)
            if layout == 'col':
                return (ri, h * (cols // ctile) + ci)
            return (ri, ci)
        if layout == 'lead':
            return pl.BlockSpec((None, rtile, ctile), idx)
        return pl.BlockSpec((rtile, ctile), idx)

    a_spec = spec(k, m, tk, tm, 2, 0, a_head) if ta else spec(m, k, tm, tk, 0, 2, a_head)
    b_spec = spec(n, k, tn, tk, 1, 2, b_head) if tb else spec(k, n, tk, tn, 2, 1, b_head)
    o_spec = spec(m, n, tm, tn, 0, 1, out_head)
    in_specs = [a_spec, b_spec]
    args = [a, b]
    if add is not None:
        in_specs.append(spec(m, n, tm, tn, 0, 1, out_head))
        args.append(add)
    dn = (((0 if ta else 1,), (1 if tb else 0,)), ((), ()))

    def body(*refs):
        a_ref, b_ref = refs[0], refs[1]
        prod = lax.dot_general(a_ref[...].astype(_CDT), b_ref[...].astype(_CDT), dn, preferred_element_type=F32)
        if nk == 1:
            o_ref = refs[-1]
            if add is not None:
                prod = prod + refs[2][...].astype(F32) * add_scale
            o_ref[...] = prod.astype(out_dtype)
            return
        o_ref, acc_ref = refs[-2], refs[-1]
        kk = pl.program_id(off + 2)

        @pl.when(kk == 0)
        def _():
            if add is not None:
                acc_ref[...] = refs[2][...].astype(F32) * add_scale
            else:
                acc_ref[...] = jnp.zeros_like(acc_ref)

        acc_ref[...] += prod

        @pl.when(kk == nk - 1)
        def _():
            o_ref[...] = acc_ref[...].astype(out_dtype)

    if out_head == 'lead':
        oshape = (heads, m, n)
    elif out_head == 'col':
        oshape = (m, heads * n)
    else:
        oshape = (m, n)
    sem = ("parallel",) * (off + 2) + ("arbitrary",)
    return pl.pallas_call(
        body, name=name, grid=hgrid + (m // tm, n // tn, nk), in_specs=in_specs, out_specs=o_spec,
        out_shape=jax.ShapeDtypeStruct(oshape, out_dtype),
        scratch_shapes=[pltpu.VMEM((tm, tn), F32)] if nk > 1 else [],
        compiler_params=_cparams(sem))(*args)


def _mm2(a, b, **kw):
    ta, tb = kw.get('ta', False), kw.get('tb', False)
    m = a.shape[1] if ta else a.shape[0]
    k = a.shape[0] if ta else a.shape[1]
    n = b.shape[0] if tb else b.shape[1]
    return _mm(a, b, dims=(m, n, k), **kw)


def _rowwise(fn, rows, bcast, outs, reds=(), *, name, tm=256, heads=None):
    t = rows[0][0].shape[0]
    tm = min(tm, t)
    hn = 1 if heads is None else heads
    in_specs, args = [], []
    for arr, width, base, per_head in rows:
        in_specs.append(pl.BlockSpec((tm, width), functools.partial(
            lambda i, h, base, per_head: (i, base + (h if per_head else 0)), base=base, per_head=per_head)))
        args.append(arr)
    for arr in bcast:
        in_specs.append(pl.BlockSpec(arr.shape, lambda i, h: (0, 0)))
        args.append(arr)
    out_specs, out_shape = [], []
    for total, width, per_head, dt in outs:
        out_specs.append(pl.BlockSpec((tm, width), functools.partial(
            lambda i, h, per_head: (i, h if per_head else 0), per_head=per_head)))
        out_shape.append(jax.ShapeDtypeStruct((t, total), dt))
    for shp in reds:
        out_specs.append(pl.BlockSpec(shp, lambda i, h: (0, 0)))
        out_shape.append(jax.ShapeDtypeStruct(shp, F32))
    n_in, n_out, n_red = len(args), len(outs), len(reds)

    def body(*refs):
        i, h = pl.program_id(0), pl.program_id(1)
        vals = fn(h, *[r[...] for r in refs[:n_in]])
        for r, v in zip(refs[n_in:n_in + n_out], vals[:n_out]):
            r[...] = v.astype(r.dtype)
        if n_red:
            @pl.when((i == 0) & (h == 0))
            def _():
                for r in refs[n_in + n_out:]:
                    r[...] = jnp.zeros_like(r)
            for r, v in zip(refs[n_in + n_out:], vals[n_out:]):
                r[...] += v

    sem = ("arbitrary", "arbitrary") if n_red else ("parallel", "parallel")
    res = pl.pallas_call(body, name=name, grid=(t // tm, hn), in_specs=in_specs, out_specs=out_specs,
                         out_shape=out_shape, compiler_params=_cparams(sem))(*args)
    return tuple(res)


def _sigmoid(x):
    return 1.0 / (1.0 + jnp.exp(-x))


def _silu(x):
    return x * _sigmoid(x)


def _softplus(x):
    return jnp.maximum(x, 0.0) + jnp.log(1.0 + jnp.exp(-jnp.abs(x)))


def _layer_norm(t, g, b):
    mu = jnp.mean(t, axis=-1, keepdims=True)
    d = t - mu
    var = jnp.mean(d * d, axis=-1, keepdims=True)
    return d * lax.rsqrt(var + 1e-5) * g + b


def _rms_norm(t, w):
    return t * lax.rsqrt(jnp.mean(t * t, axis=-1, keepdims=True) + 1e-6) * w


def _swap_rope_halves(t):
    lane = lax.broadcasted_iota(jnp.int32, t.shape, 1) % ROPE_PAD
    n = t.shape[1]
    up = pltpu.roll(t, n - ROPE // 2, axis=1)
    dn = pltpu.roll(t, ROPE // 2, axis=1)
    return jnp.where(lane < ROPE // 2, up, jnp.where(lane < ROPE, dn, 0.0))


def _rope(t, cosb, sinb):
    reps = t.shape[1] // ROPE_PAD
    c = jnp.tile(cosb, (1, reps)) if reps > 1 else cosb
    s = jnp.tile(sinb, (1, reps)) if reps > 1 else sinb
    return t * c + _swap_rope_halves(t) * s


def _rope_bwd(d, cosb, sinb):
    reps = d.shape[1] // ROPE_PAD
    c = jnp.tile(cosb, (1, reps)) if reps > 1 else cosb
    s = jnp.tile(sinb, (1, reps)) if reps > 1 else sinb
    return d * c + _swap_rope_halves(d * s)


_CONV_ROWS = 256
_CONV_COLS = 256


def _conv_window(ref, r0, lo, hi, t):
    parts = []
    start, stop = r0 - lo, r0 + _CONV_ROWS + hi
    if start < 0:
        parts.append(jnp.zeros((-start, ref.shape[1]), F32))
        start = 0
    tail = max(stop - t, 0)
    parts.append(ref[start:stop - tail, :].astype(F32))
    if tail:
        parts.append(jnp.zeros((tail, ref.shape[1]), F32))
    return parts[0] if len(parts) == 1 else jnp.concatenate(parts, axis=0)


def _conv_taps(win, w_ref, n_out):
    acc = win[8:8 + n_out] * w_ref[DN_CONV - 1:DN_CONV, :]
    for i in range(DN_CONV - 1):
        acc = acc + pltpu.roll(win, DN_CONV - 1 - i, axis=0)[8:8 + n_out] * w_ref[i:i + 1, :]
    return acc


def _conv_silu(x, w):
    t, ch = x.shape

    def body(x_ref, w_ref, o_ref):
        for r in range(t // _CONV_ROWS):
            r0 = r * _CONV_ROWS
            c = _conv_taps(_conv_window(x_ref, r0, 8, 0, t), w_ref, _CONV_ROWS)
            o_ref[r0:r0 + _CONV_ROWS, :] = _silu(c)

    return pl.pallas_call(
        body, name="conv_silu", grid=(ch // _CONV_COLS,),
        in_specs=[pl.BlockSpec((t, _CONV_COLS), lambda j: (0, j)), pl.BlockSpec((DN_CONV, _CONV_COLS), lambda j: (0, j))],
        out_specs=pl.BlockSpec((t, _CONV_COLS), lambda j: (0, j)),
        out_shape=jax.ShapeDtypeStruct((t, ch), F32), compiler_params=_cparams(("parallel",)))(x, w)


def _conv_silu_bwd(x, w, dy):
    t, ch = x.shape

    def body(x_ref, w_ref, dy_ref, dx_ref, dw_ref):
        dws = [jnp.zeros((1, _CONV_COLS), F32) for _ in range(DN_CONV)]
        for r in range(t // _CONV_ROWS):
            r0 = r * _CONV_ROWS
            n_ext = _CONV_ROWS + 8
            xw = _conv_window(x_ref, r0, 8, 8, t)
            c = _conv_taps(xw, w_ref, n_ext)
            sg = _sigmoid(c)
            ds = _conv_window(dy_ref, r0, 0, 8, t) * (sg * (1.0 + c * (1.0 - sg)))
            dx = ds[:_CONV_ROWS] * w_ref[DN_CONV - 1:DN_CONV, :]
            for i in range(DN_CONV - 1):
                sh = DN_CONV - 1 - i
                dx = dx + pltpu.roll(ds, n_ext - sh, axis=0)[:_CONV_ROWS] * w_ref[i:i + 1, :]
            dx_ref[r0:r0 + _CONV_ROWS, :] = dx.astype(dx_ref.dtype)
            ds0 = ds[:_CONV_ROWS]
            for i in range(DN_CONV):
                sh = DN_CONV - 1 - i
                xs = xw if sh == 0 else pltpu.roll(xw, sh, axis=0)
                dws[i] = dws[i] + jnp.sum(ds0 * xs[8:8 + _CONV_ROWS], axis=0, keepdims=True)
        for i in range(DN_CONV):
            dw_ref[i:i + 1, :] = dws[i]

    blk = pl.BlockSpec((t, _CONV_COLS), lambda j: (0, j))
    wblk = pl.BlockSpec((DN_CONV, _CONV_COLS), lambda j: (0, j))
    return pl.pallas_call(
        body, name="conv_silu_bwd", grid=(ch // _CONV_COLS,), in_specs=[blk, wblk, blk], out_specs=[blk, wblk],
        out_shape=[jax.ShapeDtypeStruct((t, ch), _CDT), jax.ShapeDtypeStruct((DN_CONV, ch), F32)],
        compiler_params=_cparams(("parallel",)))(x, w, dy)


_PA_ROWS = 512


def _bmm(a, b, spec, exact=False):
    if exact:
        return jnp.einsum(spec, a, b, precision=_HI, preferred_element_type=F32)
    return jnp.einsum(spec, a.astype(_CDT), b.astype(_CDT), preferred_element_type=F32)


def _split16(a):
    hi = a.astype(jnp.bfloat16)
    return hi, (a - hi.astype(F32)).astype(jnp.bfloat16)


def _bmm3(a, b, spec):
    ah, al = _split16(a)
    bh, bl = _split16(b)
    e = lambda p, q: jnp.einsum(spec, p, q, preferred_element_type=F32)
    return e(ah, bh) + (e(ah, bl) + e(al, bh))


def _tri_inverse(l_mat, eye):
    pw = -l_mat
    t_inv = eye + pw
    for _ in range(5):
        pw = _bmm3(pw, pw, 'bij,bjk->bik')
        t_inv = t_inv + _bmm3(t_inv, pw, 'bij,bjk->bik')
    return t_inv


@jax.custom_vjp
def _tri_inverse_saved(l_mat, t_saved):
    return t_saved


def _tri_inverse_saved_fwd(l_mat, t_saved):
    return t_saved, t_saved


def _tri_inverse_saved_bwd(t_saved, dt):
    left = _bmm3(t_saved, dt, 'bji,bjk->bik')
    return -_bmm3(left, t_saved, 'bij,bkj->bik'), jnp.zeros_like(t_saved)


_tri_inverse_saved.defvjp(_tri_inverse_saved_fwd, _tri_inverse_saved_bwd)


def _phase_a(h, q, k, v, ba, alog, dtb, t_saved=None):
    r = q.shape[0]
    nb = r // DN_CHUNK
    c = DN_CHUNK
    lane = lax.broadcasted_iota(jnp.int32, (1, LANE), 1)
    selb = (lane == h).astype(F32)
    sela = (lane == h + HEADS).astype(F32)
    b_raw = jnp.sum(ba * selb, axis=1, keepdims=True)
    a_raw = jnp.sum(ba * sela, axis=1, keepdims=True)
    al = jnp.sum(alog * selb, axis=1, keepdims=True)
    dt = jnp.sum(dtb * selb, axis=1, keepdims=True)
    beta = jnp.broadcast_to(_sigmoid(b_raw), (r, LANE))
    g = jnp.broadcast_to(-jnp.exp(al) * _softplus(a_raw + dt), (r, LANE))
    qn = q * lax.rsqrt(jnp.sum(q * q, -1, keepdims=True) + 1e-6) * (DN_DK ** -0.5)
    kn = k * lax.rsqrt(jnp.sum(k * k, -1, keepdims=True) + 1e-6)
    q3, k3, v3 = qn.reshape(nb, c, LANE), kn.reshape(nb, c, LANE), v.reshape(nb, c, LANE)
    b3, g3 = beta.reshape(nb, c, LANE), g.reshape(nb, c, LANE)
    ri = lax.broadcasted_iota(jnp.int32, (nb, c, c), 1)
    ci = lax.broadcasted_iota(jnp.int32, (nb, c, c), 2)
    tril, strict = ri >= ci, ri > ci
    gc = _bmm(tril.astype(F32), g3, 'bij,bjd->bid', exact=True)
    onehot = (lax.broadcasted_iota(jnp.int32, (nb, c, LANE), 2) == 0).astype(F32)
    g_row = _bmm(onehot, gc, 'bid,bjd->bij', exact=True)
    diff = gc[:, :, :c] - g_row
    decay = jnp.where(tril, jnp.exp(jnp.where(tril, diff, 0.0)), 0.0)
    kb = k3 * b3
    l_mat = jnp.where(strict, _bmm(kb, k3, 'bid,bjd->bij') * decay, 0.0)
    if t_saved is None:
        t_inv = _tri_inverse(l_mat, (ri == ci).astype(F32))
    else:
        t_inv = _tri_inverse_saved(l_mat, t_saved.reshape(nb, c, c))
    eg = jnp.exp(gc)
    u = _bmm(t_inv, v3 * b3, 'bij,bje->bie')
    w = _bmm(t_inv, kb * eg, 'bij,bje->bie')
    intra = jnp.where(tril, _bmm(q3, k3, 'bid,bjd->bij') * decay, 0.0)
    qd = q3 * eg
    gl = jnp.sum(g3, axis=1, keepdims=True)
    kt = k3 * jnp.exp(gl - gc)
    outs = (u.reshape(r, LANE), w.reshape(r, LANE), qd.reshape(r, LANE), kt.reshape(r, LANE),
            intra.reshape(r, c), gl.reshape(nb, LANE))
    return outs + (t_inv.reshape(r, c),) if t_saved is None else outs


def _pa_specs(t):
    rr = min(_PA_ROWS, t)
    nb = rr // DN_CHUNK
    qkv = [pl.BlockSpec((rr, LANE), functools.partial(lambda i, h, o: (i, o + h), o=o)) for o in (0, HEADS, 2 * HEADS)]
    ba = pl.BlockSpec((rr, LANE), lambda i, h: (i, 3))
    vec = pl.BlockSpec((1, LANE), lambda i, h: (0, 0))
    row = pl.BlockSpec((rr, LANE), lambda i, h: (i, h))
    intra = pl.BlockSpec((None, rr, DN_CHUNK), lambda i, h: (h, i, 0))
    gl = pl.BlockSpec((nb, LANE), lambda i, h: (i, h))
    return rr, qkv, ba, vec, row, intra, gl


def _delta_local(qkv_act, pm, alog, dtb):
    t = qkv_act.shape[0]
    rr, qkv, ba, vec, row, intra, gl = _pa_specs(t)

    def body(q, k, v, b, al, dt, *outs):
        vals = _phase_a(pl.program_id(1), q[...], k[...], v[...], b[...], al[...], dt[...])
        for o, val in zip(outs, vals):
            o[...] = val

    wide = jax.ShapeDtypeStruct((t, HEADS * LANE), F32)
    sq = jax.ShapeDtypeStruct((HEADS, t, DN_CHUNK), F32)
    return pl.pallas_call(
        body, name="delta_local", grid=(t // rr, HEADS), in_specs=qkv + [ba, vec, vec],
        out_specs=[row] * 4 + [intra, gl, intra],
        out_shape=[wide] * 4 + [sq, jax.ShapeDtypeStruct((t // DN_CHUNK, HEADS * LANE), F32), sq],
        compiler_params=_cparams(("parallel", "parallel")))(qkv_act, qkv_act, qkv_act, pm, alog, dtb)


def _delta_local_bwd(qkv_act, pm, alog, dtb, t_inv, du, dw, dqd, dkt, dintra, dgl):
    t = qkv_act.shape[0]
    rr, qkv, ba, vec, row, intra, gl = _pa_specs(t)

    def body(q, k, v, b, al, dt, ti, du_r, dw_r, dqd_r, dkt_r, di_r, dgl_r, dq_o, dk_o, dv_o, dba_o, dal_o, ddt_o):
        i, h = pl.program_id(0), pl.program_id(1)
        t_saved = ti[...]
        _, vjp = jax.vjp(lambda *a: _phase_a(h, *a, t_saved=t_saved), q[...], k[...], v[...], b[...], al[...], dt[...])
        dq, dk, dv, dba, dal, ddt = vjp((du_r[...], dw_r[...], dqd_r[...], dkt_r[...], di_r[...], dgl_r[...]))
        dq_o[...], dk_o[...], dv_o[...] = dq, dk, dv

        @pl.when(h == 0)
        def _():
            dba_o[...] = jnp.zeros_like(dba_o)

        @pl.when((h == 0) & (i == 0))
        def _():
            dal_o[...] = jnp.zeros_like(dal_o)
            ddt_o[...] = jnp.zeros_like(ddt_o)

        dba_o[...] += dba
        dal_o[...] += dal
        ddt_o[...] += ddt

    wide = jax.ShapeDtypeStruct((t, HEADS * LANE), F32)
    vshape = jax.ShapeDtypeStruct((1, LANE), F32)
    return pl.pallas_call(
        body, name="delta_local_bwd", grid=(t // rr, HEADS),
        in_specs=qkv + [ba, vec, vec, intra] + [row] * 4 + [intra, gl],
        out_specs=[row] * 3 + [pl.BlockSpec((rr, LANE), lambda i, h: (i, 0)), vec, vec],
        out_shape=[wide] * 3 + [jax.ShapeDtypeStruct((t, LANE), F32), vshape, vshape],
        compiler_params=_cparams(("arbitrary", "arbitrary")))(
            qkv_act, qkv_act, qkv_act, pm, alog, dtb, t_inv, du, dw, dqd, dkt, dintra, dgl)


_SCAN_ROWS = 512


def _dot(a, b, dn):
    return lax.dot_general(a.astype(_CDT), b.astype(_CDT), (dn, ((), ())), preferred_element_type=F32)


_NN = ((1,), (0,))
_NT = ((1,), (1,))
_TN = ((0,), (0,))


def _delta_scan(u, w, qd, kt, intra, gl):
    t = u.shape[0]
    rr = min(_SCAN_ROWS, t)
    nc = rr // DN_CHUNK

    def body(u_ref, w_ref, qd_ref, kt_ref, a_ref, gl_ref, o_ref, sall_ref, s_scr):
        @pl.when(pl.program_id(0) == 0)
        def _():
            s_scr[...] = jnp.zeros_like(s_scr)

        def chunk(c, carry):
            r0 = pl.multiple_of(c * DN_CHUNK, DN_CHUNK)
            rows = pl.ds(r0, DN_CHUNK)
            e = jnp.exp(gl_ref[pl.ds(c, 1), :])
            for h in range(HEADS):
                cs = slice(h * LANE, (h + 1) * LANE)
                s = s_scr[h]
                sall_ref[c, h] = s
                v_new = u_ref[rows, cs] - _dot(w_ref[rows, cs], s, _NN)
                o_ref[rows, cs] = _dot(qd_ref[rows, cs], s, _NN) + _dot(a_ref[h, rows, :], v_new, _NN)
                s_scr[h] = s * e[:, cs] + _dot(kt_ref[rows, cs], v_new, _TN)
            return carry

        lax.fori_loop(0, nc, chunk, 0)

    row = pl.BlockSpec((rr, HEADS * LANE), lambda i: (i, 0))
    return pl.pallas_call(
        body, name="delta_scan", grid=(t // rr,),
        in_specs=[row] * 4 + [pl.BlockSpec((HEADS, rr, DN_CHUNK), lambda i: (0, i, 0)),
                              pl.BlockSpec((nc, HEADS * LANE), lambda i: (i, 0))],
        out_specs=[row, pl.BlockSpec((nc, HEADS, LANE, LANE), lambda i: (i, 0, 0, 0))],
        out_shape=[jax.ShapeDtypeStruct((t, HEADS * LANE), F32),
                   jax.ShapeDtypeStruct((t // DN_CHUNK, HEADS, LANE, LANE), F32)],
        scratch_shapes=[pltpu.VMEM((HEADS, LANE, LANE), F32)],
        compiler_params=_cparams(("arbitrary",)))(u, w, qd, kt, intra, gl)


def _delta_scan_bwd(u, w, qd, kt, intra, gl, sall, do):
    t = u.shape[0]
    rr = min(_SCAN_ROWS, t)
    nc = rr // DN_CHUNK
    ng = t // rr

    def body(u_ref, w_ref, qd_ref, kt_ref, a_ref, gl_ref, sall_ref, do_ref,
             du_ref, dw_ref, dqd_ref, dkt_ref, da_ref, dgl_ref, ds_scr):
        @pl.when(pl.program_id(0) == 0)
        def _():
            ds_scr[...] = jnp.zeros_like(ds_scr)

        def chunk(cc, carry):
            c = nc - 1 - cc
            r0 = pl.multiple_of(c * DN_CHUNK, DN_CHUNK)
            rows = pl.ds(r0, DN_CHUNK)
            e = jnp.exp(gl_ref[pl.ds(c, 1), :])
            for h in range(HEADS):
                cs = slice(h * LANE, (h + 1) * LANE)
                s = sall_ref[c, h]
                ds_out = ds_scr[h]
                w_c, kt_c, qd_c, do_c = w_ref[rows, cs], kt_ref[rows, cs], qd_ref[rows, cs], do_ref[rows, cs]
                a_c = a_ref[h, rows, :]
                v_new = u_ref[rows, cs] - _dot(w_c, s, _NN)
                dv_new = _dot(a_c, do_c, _TN) + _dot(kt_c, ds_out, _NN)
                da_ref[h, rows, :] = _dot(do_c, v_new, _NT)
                dqd_ref[rows, cs] = _dot(do_c, s, _NT)
                dkt_ref[rows, cs] = _dot(v_new, ds_out, _NT)
                du_ref[rows, cs] = dv_new
                dw_ref[rows, cs] = -_dot(dv_new, s, _NT)
                eh = e[:, cs]
                dgl_ref[pl.ds(pl.multiple_of(c * 8, 8), 8), cs] = jnp.broadcast_to(
                    jnp.sum(ds_out * s, axis=0, keepdims=True) * eh, (8, LANE))
                ds_scr[h] = ds_out * eh + _dot(qd_c, do_c, _TN) - _dot(w_c, dv_new, _TN)
            return carry

        lax.fori_loop(0, nc, chunk, 0)

    rev = lambda i: (ng - 1 - i, 0)
    row = pl.BlockSpec((rr, HEADS * LANE), rev)
    a_spec = pl.BlockSpec((HEADS, rr, DN_CHUNK), lambda i: (0, ng - 1 - i, 0))
    gl_spec = pl.BlockSpec((nc, HEADS * LANE), rev)
    wide = jax.ShapeDtypeStruct((t, HEADS * LANE), F32)
    outs = pl.pallas_call(
        body, name="delta_scan_bwd", grid=(ng,),
        in_specs=[row] * 4 + [a_spec, gl_spec, pl.BlockSpec((nc, HEADS, LANE, LANE), lambda i: (ng - 1 - i, 0, 0, 0)), row],
        out_specs=[row] * 4 + [a_spec, pl.BlockSpec((nc * 8, HEADS * LANE), rev)],
        out_shape=[wide] * 4 + [jax.ShapeDtypeStruct((HEADS, t, DN_CHUNK), F32),
                                jax.ShapeDtypeStruct((t // DN_CHUNK * 8, HEADS * LANE), F32)],
        scratch_shapes=[pltpu.VMEM((HEADS, LANE, LANE), F32)],
        compiler_params=_cparams(("arbitrary",)))(u, w, qd, kt, intra, gl, sall, do)
    return tuple(outs[:5]) + (outs[5].reshape(t // DN_CHUNK, 8, HEADS * LANE)[:, 0, :],)


_ATT_TILE = 512


def _kv_rows(j, tk):
    return pl.ds(pl.multiple_of(j * tk, tk), tk)


def _att_scores(ql, qr, ckv_ref, kr_ref, j, tk):
    ks = _kv_rows(j, tk)
    return (_dot(ql, ckv_ref[ks, :], _NT) + _dot(qr, kr_ref[ks, :], _NT)) * ATT_SCALE


def _diag_mask(s):
    qi = lax.broadcasted_iota(jnp.int32, s.shape, 0)
    ki = lax.broadcasted_iota(jnp.int32, s.shape, 1)
    return jnp.where(ki <= qi, s, NEG_BIG)


def _attention(ql, qr, ckv, kr):
    t = ckv.shape[0]
    tq = min(_ATT_TILE, t)

    def body(ql_ref, qr_ref, ckv_ref, kr_ref, o_ref, lse_ref, m_scr, l_scr, acc_scr, s_scr):
        qi = pl.program_id(1)
        q_lat, q_rope = ql_ref[...], qr_ref[...]
        m_scr[...] = jnp.full_like(m_scr, NEG_BIG)
        l_scr[...] = jnp.zeros_like(l_scr)
        acc_scr[...] = jnp.zeros_like(acc_scr)

        def consume(j, s):
            m_old = m_scr[...]
            m_new = jnp.maximum(m_old, jnp.max(s, axis=-1, keepdims=True))
            p = jnp.exp(s - m_new)
            alpha = jnp.exp(m_old - m_new)
            l_scr[...] = l_scr[...] * alpha + jnp.sum(p, axis=-1, keepdims=True)
            acc_scr[...] = acc_scr[...] * alpha + _dot(p, ckv_ref[_kv_rows(j, tq), :], _NN)
            m_scr[...] = m_new

        s_scr[0] = _att_scores(q_lat, q_rope, ckv_ref, kr_ref, 0, tq)

        def loop_body(j, carry):
            s_scr[(j + 1) % 2] = _att_scores(q_lat, q_rope, ckv_ref, kr_ref, j + 1, tq)
            consume(j, s_scr[j % 2])
            return carry

        lax.fori_loop(0, qi, loop_body, 0)
        consume(qi, _diag_mask(s_scr[qi % 2]))
        o_ref[...] = acc_scr[...] / l_scr[...]
        lse_ref[...] = m_scr[...] + jnp.log(l_scr[...])

    return pl.pallas_call(
        body, name="attention", grid=(HEADS, t // tq),
        in_specs=[pl.BlockSpec((None, tq, KV_LORA), lambda h, i: (h, i, 0)),
                  pl.BlockSpec((tq, ROPE_PAD), lambda h, i: (i, h)),
                  pl.BlockSpec((t, KV_LORA), lambda h, i: (0, 0)),
                  pl.BlockSpec((t, ROPE_PAD), lambda h, i: (0, 0))],
        out_specs=[pl.BlockSpec((None, tq, KV_LORA), lambda h, i: (h, i, 0)),
                   pl.BlockSpec((None, tq, 1), lambda h, i: (h, i, 0))],
        out_shape=[jax.ShapeDtypeStruct((HEADS, t, KV_LORA), F32), jax.ShapeDtypeStruct((HEADS, t, 1), F32)],
        scratch_shapes=[pltpu.VMEM((tq, 1), F32), pltpu.VMEM((tq, 1), F32), pltpu.VMEM((tq, KV_LORA), F32),
                        pltpu.VMEM((2, tq, tq), F32)],
        compiler_params=_cparams(("parallel", "parallel")))(ql, qr, ckv, kr)


def _attention_bwd(ql, qr, ckv, kr, out, lse, dout):
    t = ckv.shape[0]
    tq = min(_ATT_TILE, t)

    def body(ql_ref, qr_ref, ckv_ref, kr_ref, o_ref, lse_ref, do_ref, dql_ref, dqr_ref, dckv_ref, dkr_ref,
             dql_scr, dqr_scr):
        h, qi = pl.program_id(0), pl.program_id(1)

        @pl.when((h == 0) & (qi == 0))
        def _():
            dckv_ref[...] = jnp.zeros_like(dckv_ref)
            dkr_ref[...] = jnp.zeros_like(dkr_ref)

        q_lat, q_rope = ql_ref[...], qr_ref[...]
        d_o = do_ref[...].astype(_CDT)
        lse_v = lse_ref[...]
        dsum = jnp.sum(do_ref[...] * o_ref[...], axis=-1, keepdims=True)
        dql_scr[...] = jnp.zeros_like(dql_scr)
        dqr_scr[...] = jnp.zeros_like(dqr_scr)

        def step(j, masked):
            ks = _kv_rows(j, tq)
            s = _att_scores(q_lat, q_rope, ckv_ref, kr_ref, j, tq)
            if masked:
                s = _diag_mask(s)
            p = jnp.exp(s - lse_v)
            kv = ckv_ref[ks, :]
            ds = (p * (_dot(d_o, kv, _NT) - dsum) * ATT_SCALE).astype(_CDT)
            pb = p.astype(_CDT)
            dql_scr[...] += _dot(ds, kv, _NN)
            dqr_scr[...] += _dot(ds, kr_ref[ks, :], _NN)
            dckv_ref[ks, :] += _dot(pb, d_o, _TN) + _dot(ds, q_lat, _TN)
            dkr_ref[ks, :] += _dot(ds, q_rope, _TN)

        def loop_body(j, carry):
            step(j, False)
            return carry

        lax.fori_loop(0, qi, loop_body, 0)
        step(qi, True)
        dql_ref[...] = dql_scr[...].astype(dql_ref.dtype)
        dqr_ref[...] = dqr_scr[...]

    lat = pl.BlockSpec((None, tq, KV_LORA), lambda h, i: (h, i, 0))
    rope = pl.BlockSpec((tq, ROPE_PAD), lambda h, i: (i, h))
    kfull = pl.BlockSpec((t, KV_LORA), lambda h, i: (0, 0))
    rfull = pl.BlockSpec((t, ROPE_PAD), lambda h, i: (0, 0))
    return pl.pallas_call(
        body, name="attention_bwd", grid=(HEADS, t // tq),
        in_specs=[lat, rope, kfull, rfull, lat, pl.BlockSpec((None, tq, 1), lambda h, i: (h, i, 0)), lat],
        out_specs=[lat, rope, kfull, rfull],
        out_shape=[jax.ShapeDtypeStruct((HEADS, t, KV_LORA), _CDT), jax.ShapeDtypeStruct((t, HEADS * ROPE_PAD), F32),
                   jax.ShapeDtypeStruct((t, KV_LORA), F32), jax.ShapeDtypeStruct((t, ROPE_PAD), F32)],
        scratch_shapes=[pltpu.VMEM((tq, KV_LORA), F32), pltpu.VMEM((tq, ROPE_PAD), F32)],
        compiler_params=_cparams(("arbitrary", "arbitrary")))(ql, qr, ckv, kr, out, lse, dout)


def _gated_norm(o, z, w):
    return _rms_norm(o, w) * _silu(z)


def _mla_pre(ckv, krp, cq, cosb, sinb, qw, kw):
    return _rms_norm(cq, qw), _rms_norm(ckv, kw), _rope(krp, cosb, sinb)


def _merge(gg, y_dn, y_mla):
    return _sigmoid(gg[:, :D_MODEL]) * y_dn + _sigmoid(gg[:, D_MODEL:]) * y_mla


def _ln1(xv, attn_out, g, b):
    return _layer_norm(ALPHA * xv + attn_out, g, b)


def _final(h1, ffn, gate_pre, ple_proj, g, b):
    return _layer_norm(ALPHA * h1 + ffn + _sigmoid(gate_pre) * ple_proj, g, b)


def _swiglu(gt, up):
    return _silu(gt) * up


def _local_step(x, p, cosb, sinb, target, wt, sp):
    t = x.shape[0]
    bf = _CDT
    xb = x.astype(bf)
    g = {}

    qkv_pre = _mm2(xb, wt['qkv'], name="f_qkv")
    z = _mm2(xb, wt['z'], name="f_z")
    gg = _mm2(xb, wt['gg'], name="f_gg")
    pm = _mm2(xb, wt['mla'], name="f_mla")
    qkv_act = _conv_silu(qkv_pre, sp['conv_w'])
    u, w_, qd, kt, intra, gl, t_inv = _delta_local(qkv_act, pm, sp['a_log'], sp['dt_bias'])
    o_dn, sall = _delta_scan(u, w_, qd, kt, intra, gl)
    (og,) = _rowwise(lambda h, o, zz, w: (_gated_norm(o, zz, w),),
                     [(o_dn, LANE, 0, True), (z, LANE, 0, True)], [sp['dn_norm_w']],
                     [(D_MODEL, LANE, True, bf)], name="f_gated_norm", tm=512, heads=HEADS)
    y_dn = _mm2(og, wt['br_dn'], name="f_br_dn")

    c_q, c_kv, k_rope = _rowwise(
        lambda h, *a: _mla_pre(*a),
        [(pm, KV_LORA, 0, False), (pm, ROPE_PAD, 2, False), (pm, Q_LORA, 2, False),
         (cosb, ROPE_PAD, 0, False), (sinb, ROPE_PAD, 0, False)],
        [sp['q_norm_w'], sp['kv_norm_w']],
        [(Q_LORA, Q_LORA, False, bf), (KV_LORA, KV_LORA, False, bf), (ROPE_PAD, ROPE_PAD, False, bf)], name="f_mla_pre")
    q_nope = _mm2(c_q, wt['uq_nope'], name="f_uq_nope", out_dtype=bf)
    q_rope_pre = _mm2(c_q, wt['uq_rope'], name="f_uq_rope")
    (q_rope,) = _rowwise(lambda h, q, c, s: (_rope(q, c, s),),
                         [(q_rope_pre, HEADS * ROPE_PAD, 0, False), (cosb, ROPE_PAD, 0, False), (sinb, ROPE_PAD, 0, False)],
                         [], [(HEADS * ROPE_PAD, HEADS * ROPE_PAD, False, bf)], name="f_q_rope")
    q_lat = _mm(q_nope, wt['uk'], name="f_q_lat", tb=True, heads=HEADS, a_head='col', b_head='lead', out_head='lead',
                dims=(t, KV_LORA, NOPE), out_dtype=bf)
    out_lat, lse = _attention(q_lat, q_rope, c_kv, k_rope)
    o_mla = _mm(out_lat, wt['uv'], name="f_o_mla", heads=HEADS, a_head='lead', b_head='lead', out_head='col',
                dims=(t, NOPE, KV_LORA), out_dtype=bf)
    y_mla = _mm2(o_mla, wt['br_mla'], name="f_br_mla")

    (mixed,) = _rowwise(lambda h, *a: (_merge(*a),),
                        [(gg, 2 * D_MODEL, 0, False), (y_dn, D_MODEL, 0, False), (y_mla, D_MODEL, 0, False)],
                        [], [(D_MODEL, D_MODEL, False, bf)], name="f_merge")
    attn_out = _mm2(mixed, wt['o'], name="f_o")
    h1, h1b = _rowwise(lambda h, *a: (_ln1(*a),) * 2, [(x, D_MODEL, 0, False), (attn_out, D_MODEL, 0, False)],
                       [sp['ln1_g'], sp['ln1_b']], [(D_MODEL, D_MODEL, False, F32), (D_MODEL, D_MODEL, False, bf)],
                       name="f_ln1")
    ffn_in = _mm2(h1b, wt['ffn_in'], name="f_ffn_in")
    (act,) = _rowwise(lambda h, gt, up: (_swiglu(gt, up),),
                      [(ffn_in, FFN_HIDDEN, 0, False), (ffn_in, FFN_HIDDEN, 1, False)], [],
                      [(FFN_HIDDEN, FFN_HIDDEN, False, bf)], name="f_swiglu")
    ffn = _mm2(act, wt['ffn_out'], name="f_ffn_out")
    gate_pre = _mm2(h1b, wt['ple_gate'], name="f_ple_gate")
    pb = p.astype(bf)
    ple_proj = _mm2(pb, wt['ple'], name="f_ple")

    def final_fn(h, h1v, ffnv, gpv, ppv, tgt, gv, bv):
        y, vjp = jax.vjp(_final, h1v, ffnv, gpv, ppv, gv, bv)
        err = y - tgt
        dh1, dffn, dgp, dpp, dg, db = vjp(err * (1.0 / D_MODEL))
        sq = err * err
        lanes = sq[:, :LANE]
        for j in range(1, D_MODEL // LANE):
            lanes = lanes + sq[:, j * LANE:(j + 1) * LANE]
        loss = jnp.sum(lanes, axis=0, keepdims=True) * (0.5 / D_MODEL)
        return dffn, dffn, dgp, dpp, dg, db, loss

    dpre2, dpre2b, dgate_pre, dple_proj, g['ln2_g'], g['ln2_b'], loss_lanes = _rowwise(
        final_fn, [(a, D_MODEL, 0, False) for a in (h1, ffn, gate_pre, ple_proj, target)],
        [sp['ln2_g'], sp['ln2_b']],
        [(D_MODEL, D_MODEL, False, F32)] + [(D_MODEL, D_MODEL, False, bf)] * 3,
        [(1, D_MODEL), (1, D_MODEL), (1, LANE)], name="b_final")

    g['ple'] = _mm2(pb, dple_proj, ta=True, name="g_ple")
    g['ple_gate'] = _mm2(h1b, dgate_pre, ta=True, name="g_ple_gate")
    g['ffn_out'] = _mm2(act, dpre2b, ta=True, name="g_ffn_out")
    dact = _mm2(dpre2b, wt['ffn_out'], tb=True, name="b_dact")

    def swiglu_bwd(h, gt, up, d):
        _, vjp = jax.vjp(_swiglu, gt, up)
        dgt, dup = vjp(d)
        return (jnp.concatenate([dgt, dup], axis=1),)

    (dffn_in,) = _rowwise(swiglu_bwd, [(ffn_in, FFN_HIDDEN, 0, False), (ffn_in, FFN_HIDDEN, 1, False),
                                       (dact, FFN_HIDDEN, 0, False)], [],
                          [(2 * FFN_HIDDEN, 2 * FFN_HIDDEN, False, bf)], name="b_swiglu")
    g['ffn_in'] = _mm2(h1b, dffn_in, ta=True, name="g_ffn_in")
    dh1 = _mm2(dffn_in, wt['ffn_in'], tb=True, name="b_dh1_ffn", add=dpre2, add_scale=ALPHA)
    dh1 = _mm2(dgate_pre, wt['ple_gate'], tb=True, name="b_dh1_gate", add=dh1)

    def ln1_bwd(h, xv, ao, d, gv, bv):
        _, vjp = jax.vjp(_ln1, xv, ao, gv, bv)
        _, dao, dg, db = vjp(d)
        return dao, dao, dg, db

    dpre1, dpre1b, g['ln1_g'], g['ln1_b'] = _rowwise(
        ln1_bwd, [(x, D_MODEL, 0, False), (attn_out, D_MODEL, 0, False), (dh1, D_MODEL, 0, False)],
        [sp['ln1_g'], sp['ln1_b']], [(D_MODEL, D_MODEL, False, F32), (D_MODEL, D_MODEL, False, bf)],
        [(1, D_MODEL), (1, D_MODEL)], name="b_ln1")

    g['o'] = _mm2(mixed, dpre1b, ta=True, name="g_o")
    dmixed = _mm2(dpre1b, wt['o'], tb=True, name="b_dmixed")

    def merge_bwd(h, ggv, yd, ym, d):
        _, vjp = jax.vjp(_merge, ggv, yd, ym)
        return vjp(d)

    dgg, dy_dn, dy_mla = _rowwise(
        merge_bwd, [(gg, 2 * D_MODEL, 0, False), (y_dn, D_MODEL, 0, False), (y_mla, D_MODEL, 0, False),
                    (dmixed, D_MODEL, 0, False)], [],
        [(2 * D_MODEL, 2 * D_MODEL, False, bf), (D_MODEL, D_MODEL, False, bf), (D_MODEL, D_MODEL, False, bf)],
        name="b_merge")
    g['br_dn'] = _mm2(og, dy_dn, ta=True, name="g_br_dn")
    dog = _mm2(dy_dn, wt['br_dn'], tb=True, name="b_dog")
    g['br_mla'] = _mm2(o_mla, dy_mla, ta=True, name="g_br_mla")
    do_mla = _mm2(dy_mla, wt['br_mla'], tb=True, name="b_do_mla", out_dtype=bf)

    dout_lat = _mm(do_mla, wt['uv'], name="b_dout_lat", tb=True, heads=HEADS, a_head='col', b_head='lead',
                   out_head='lead', dims=(t, KV_LORA, NOPE))
    g['uv'] = _mm(out_lat, do_mla, name="g_uv", ta=True, heads=HEADS, a_head='lead', b_head='col', out_head='lead',
                  dims=(KV_LORA, NOPE, t))
    dq_lat, dq_rope, dckv_att, dkr_att = _attention_bwd(q_lat, q_rope, c_kv, k_rope, out_lat, lse, dout_lat)
    dq_nope = _mm(dq_lat, wt['uk'], name="b_dq_nope", heads=HEADS, a_head='lead', b_head='lead', out_head='col',
                  dims=(t, NOPE, KV_LORA), out_dtype=bf)
    g['uk'] = _mm(dq_lat, q_nope, name="g_uk", ta=True, heads=HEADS, a_head='lead', b_head='col', out_head='lead',
                  dims=(KV_LORA, NOPE, t))
    (dq_rope_pre,) = _rowwise(lambda h, d, c, s: (_rope_bwd(d, c, s),),
                              [(dq_rope, HEADS * ROPE_PAD, 0, False), (cosb, ROPE_PAD, 0, False), (sinb, ROPE_PAD, 0, False)],
                              [], [(HEADS * ROPE_PAD, HEADS * ROPE_PAD, False, bf)], name="b_q_rope")
    g['uq_nope'] = _mm2(c_q, dq_nope, ta=True, name="g_uq_nope")
    g['uq_rope'] = _mm2(c_q, dq_rope_pre, ta=True, name="g_uq_rope")
    dc_q = _mm2(dq_nope, wt['uq_nope'], tb=True, name="b_dcq_nope")
    dc_q = _mm2(dq_rope_pre, wt['uq_rope'], tb=True, name="b_dcq_rope", add=dc_q)

    def gated_norm_bwd(h, o, zz, d, w):
        _, vjp = jax.vjp(_gated_norm, o, zz, w)
        return vjp(d)

    do_dn, dz, g['dn_norm_w'] = _rowwise(
        gated_norm_bwd, [(o_dn, LANE, 0, True), (z, LANE, 0, True), (dog, LANE, 0, True)], [sp['dn_norm_w']],
        [(D_MODEL, LANE, True, F32), (D_MODEL, LANE, True, bf)], [(1, LANE)], name="b_gated_norm", tm=512, heads=HEADS)
    du, dw, dqd, dkt, dintra, dgl = _delta_scan_bwd(u, w_, qd, kt, intra, gl, sall, do_dn)
    dq_a, dk_a, dv_a, dba, g['a_log'], g['dt_bias'] = _delta_local_bwd(
        qkv_act, pm, sp['a_log'], sp['dt_bias'], t_inv, du, dw, dqd, dkt, dintra, dgl)
    dqkv_act = jnp.concatenate([dq_a, dk_a, dv_a], axis=1)
    dqkv_pre, g['conv_w'] = _conv_silu_bwd(qkv_pre, sp['conv_w'], dqkv_act)

    def mla_pre_bwd(h, ckv, krp, cq, cosv, sinv, dcq, dckv, dkr, qw, kw):
        _, vjp = jax.vjp(lambda a, b, c, d, e: (_rms_norm(c, d), _rms_norm(a, e)), ckv, krp, cq, qw, kw)
        dckv_p, _, dcq_p, dqw, dkw = vjp((dcq, dckv))
        dkr_p = _rope_bwd(dkr, cosv, sinv)
        dpm = jnp.concatenate([dckv_p, dkr_p, jnp.zeros((ckv.shape[0], 3 * LANE), F32), dcq_p], axis=1)
        return dpm, dqw, dkw

    dpm_main, g['q_norm_w'], g['kv_norm_w'] = _rowwise(
        mla_pre_bwd,
        [(pm, KV_LORA, 0, False), (pm, ROPE_PAD, 2, False), (pm, Q_LORA, 2, False),
         (cosb, ROPE_PAD, 0, False), (sinb, ROPE_PAD, 0, False),
         (dc_q, Q_LORA, 0, False), (dckv_att, KV_LORA, 0, False), (dkr_att, ROPE_PAD, 0, False)],
        [sp['q_norm_w'], sp['kv_norm_w']], [(1152, 1152, False, F32)], [(1, Q_LORA), (1, KV_LORA)], name="b_mla_pre")
    (dpm,) = _rowwise(
        lambda h, a, b: (jnp.concatenate([a[:, :3 * LANE], b, a[:, 4 * LANE:]], axis=1),),
        [(dpm_main, 1152, 0, False), (dba, LANE, 0, False)], [], [(1152, 1152, False, bf)], name="b_dpm")

    g['qkv'] = _mm2(xb, dqkv_pre, ta=True, name="g_qkv")
    g['z'] = _mm2(xb, dz, ta=True, name="g_z")
    g['gg'] = _mm2(xb, dgg, ta=True, name="g_gg")
    g['mla'] = _mm2(xb, dpm, ta=True, name="g_mla")
    dx = _mm2(dqkv_pre, wt['qkv'], tb=True, name="b_dx_qkv", add=dpre1, add_scale=ALPHA)
    dx = _mm2(dz, wt['z'], tb=True, name="b_dx_z", add=dx)
    dx = _mm2(dgg, wt['gg'], tb=True, name="b_dx_gg", add=dx)
    dx = _mm2(dpm, wt['mla'], tb=True, name="b_dx_mla", add=dx)
    return loss_lanes, dx, g


_IN_SIZES = (QKV_W, HEADS * DN_DK, HEADS, HEADS, Q_LORA, KV_LORA, ROPE, D_MODEL, D_MODEL)


def _rope_tables(positions):
    inv_freq = ROPE_BASE ** (-jnp.arange(0, ROPE, 2, dtype=F32) / ROPE)
    ang = positions.astype(F32)[:, None] * inv_freq
    cos, sin = jnp.cos(ang), jnp.sin(ang)
    zeros = jnp.zeros((positions.shape[0], ROPE_PAD - ROPE), F32)
    return jnp.concatenate([cos, cos, zeros], axis=1), jnp.concatenate([-sin, sin, zeros], axis=1)


def _prep_weights(full):
    w_in = full['w_in']
    dt = w_in.dtype
    offs = [0]
    for s in _IN_SIZES:
        offs.append(offs[-1] + s)
    qkv, z, wb, wa, cq, ckv, kr, gd, gm = [w_in[:, offs[i]:offs[i + 1]] for i in range(len(_IN_SIZES))]
    zc = lambda n: jnp.zeros((D_MODEL, n), dt)
    w_uq = full['w_uq']
    wt = {
        'qkv': qkv, 'z': z, 'gg': jnp.concatenate([gd, gm], axis=1),
        'mla': jnp.concatenate([ckv, kr, zc(ROPE_PAD - ROPE), wb, wa, zc(LANE - 2 * HEADS), zc(2 * LANE), cq], axis=1),
        'uq_nope': w_uq[:, :, :NOPE].reshape(Q_LORA, HEADS * NOPE),
        'uq_rope': jnp.pad(w_uq[:, :, NOPE:], ((0, 0), (0, 0), (0, ROPE_PAD - ROPE))).reshape(Q_LORA, HEADS * ROPE_PAD),
        'uk': jnp.transpose(full['w_uk'], (1, 0, 2)), 'uv': jnp.transpose(full['w_uv'], (1, 0, 2)),
        'br_dn': full['w_br_dn'], 'br_mla': full['w_br_mla'], 'o': full['w_o'], 'ffn_in': full['w_ffn_in'],
        'ffn_out': full['w_ffn_out'], 'ple': full['w_ple'], 'ple_gate': full['w_ple_gate'],
    }
    return wt


def _prep_small(small):
    pad = lambda v: jnp.pad(v, (0, LANE - v.shape[0]))[None, :]
    return {
        'conv_w': small['conv_w'], 'a_log': pad(small['dn_a_log']), 'dt_bias': pad(small['dn_dt_bias']),
        'dn_norm_w': small['dn_norm_w'][None, :], 'q_norm_w': small['q_norm_w'][None, :],
        'kv_norm_w': small['kv_norm_w'][None, :], 'ln1_g': small['ln1_g'][None, :], 'ln1_b': small['ln1_b'][None, :],
        'ln2_g': small['ln2_g'][None, :], 'ln2_b': small['ln2_b'][None, :],
    }


def _unprep_grads(g):
    mla = g['mla']
    ba0 = KV_LORA + ROPE_PAD
    cq0 = ba0 + 3 * LANE
    w_in = jnp.concatenate([
        g['qkv'], g['z'], mla[:, ba0:ba0 + HEADS], mla[:, ba0 + HEADS:ba0 + 2 * HEADS], mla[:, cq0:cq0 + Q_LORA],
        mla[:, :KV_LORA], mla[:, KV_LORA:KV_LORA + ROPE], g['gg']], axis=1)
    w_uq = jnp.concatenate([g['uq_nope'].reshape(Q_LORA, HEADS, NOPE),
                            g['uq_rope'].reshape(Q_LORA, HEADS, ROPE_PAD)[:, :, :ROPE]], axis=2)
    return {
        'w_in': w_in, 'conv_w': g['conv_w'], 'dn_a_log': g['a_log'][0, :HEADS], 'dn_dt_bias': g['dt_bias'][0, :HEADS],
        'dn_norm_w': g['dn_norm_w'][0], 'q_norm_w': g['q_norm_w'][0], 'w_uq': w_uq, 'kv_norm_w': g['kv_norm_w'][0],
        'w_uk': jnp.transpose(g['uk'], (1, 0, 2)), 'w_uv': jnp.transpose(g['uv'], (1, 0, 2)),
        'w_br_dn': g['br_dn'], 'w_br_mla': g['br_mla'], 'w_o': g['o'], 'ln1_g': g['ln1_g'][0], 'ln1_b': g['ln1_b'][0],
        'w_ffn_in': g['ffn_in'], 'w_ffn_out': g['ffn_out'], 'w_ple': g['ple'], 'w_ple_gate': g['ple_gate'],
        'ln2_g': g['ln2_g'][0], 'ln2_b': g['ln2_b'][0],
    }


_FLATB_PIECES = (
    ('w_ffn_out', 704, (704, D_MODEL)), ('w_br_dn', 256, (256, D_MODEL)), ('w_br_mla', 256, (256, D_MODEL)),
    ('w_o', 256, (256, D_MODEL)), ('w_ple_gate', 256, (256, D_MODEL)), ('w_uq', 144, (96, HEADS, NOPE + ROPE)),
    ('w_uk', 64, (64, HEADS, NOPE)), ('w_uv', 64, (64, HEADS, NOPE)), ('w_ple', 64, (PLE_DIM, 256)),
)
FLATB_ROWS = 2112
W_IN_SHARD = D_IN // N_SHARD
FFN_IN_SHARD = 2 * FFN_HIDDEN // N_SHARD
A_ROWS = D_MODEL + 32
_CONV_SHARD = QKV_W // N_SHARD
_ADD_TILES = (256, 256, 352)


def _flatb_offsets():
    offs, o = {}, 0
    for name, rows, _ in _FLATB_PIECES:
        offs[name] = o
        o += rows
    return offs, o


def _pack_shards(ws, conv_w):
    conv_bits = lax.bitcast_convert_type(conv_w, jnp.bfloat16).reshape(DN_CONV, 2 * _CONV_SHARD).astype(_CDT)
    tail = jnp.pad(conv_bits, ((0, A_ROWS - D_MODEL - DN_CONV), (0, W_IN_SHARD - 2 * _CONV_SHARD)))
    a_buf = jnp.concatenate([ws['w_in'].astype(_CDT), tail], axis=0)
    parts = [ws[name].astype(_CDT).reshape(rows, FLAT_W) for name, rows, _ in _FLATB_PIECES]
    used = sum(p.shape[0] for p in parts)
    parts.append(jnp.zeros((FLATB_ROWS - used, FLAT_W), _CDT))
    return [a_buf, ws['w_ffn_in'].astype(_CDT), jnp.concatenate(parts, axis=0)]


def _unpack_gathered(gathered, local, me):
    pick = lambda b, s: jnp.where(me == s, local[b], gathered[b][s])
    a = [pick(0, s) for s in range(N_SHARD)]
    full = {'w_in': jnp.concatenate([p[:D_MODEL] for p in a], axis=1),
            'w_ffn_in': jnp.concatenate([pick(1, s) for s in range(N_SHARD)], axis=1)}
    conv = [lax.bitcast_convert_type(
        p[D_MODEL:D_MODEL + DN_CONV, :2 * _CONV_SHARD].astype(jnp.bfloat16).reshape(DN_CONV, _CONV_SHARD, 2), F32) for p in a]
    offs, _ = _flatb_offsets()
    fb = [pick(2, s) for s in range(N_SHARD)]
    for name, rows, shape in _FLATB_PIECES:
        pieces = [p[offs[name]:offs[name] + rows].reshape(shape) for p in fb]
        full[name] = jnp.concatenate(pieces, axis=1 if name == 'w_ple' else 0)
    return full, jnp.concatenate(conv, axis=1)


def _pack_grads(gw):
    cols = lambda g, w: jnp.stack([g[:, s * w:(s + 1) * w] for s in range(N_SHARD)])
    parts = []
    for name, rows, _ in _FLATB_PIECES:
        g = gw[name]
        if name == 'w_ple':
            parts.append(cols(g, PLE_DIM).reshape(N_SHARD, rows, FLAT_W))
        else:
            parts.append(g.reshape(N_SHARD, rows, FLAT_W))
    used = sum(p.shape[1] for p in parts)
    parts.append(jnp.zeros((N_SHARD, FLATB_ROWS - used, FLAT_W), F32))
    return [cols(gw['w_in'], W_IN_SHARD), cols(gw['w_ffn_in'], FFN_IN_SHARD), jnp.concatenate(parts, axis=1)]


def _unpack_reduced(mine, theirs, c):
    whole = [jnp.concatenate([jnp.where(c == 0, m, t), jnp.where(c == 0, t, m)], axis=0) for m, t in zip(mine, theirs)]
    out = {'w_in': whole[0], 'w_ffn_in': whole[1]}
    offs, _ = _flatb_offsets()
    for name, rows, shape in _FLATB_PIECES:
        out[name] = whole[2][offs[name]:offs[name] + rows].reshape(shape)
    return out


_HBM = pl.BlockSpec(memory_space=pltpu.HBM)


def _place():
    x, y, c = lax.axis_index("x"), lax.axis_index("y"), lax.axis_index("c")
    chips = [(1 - x, y), (x, 1 - y), (1 - x, 1 - y)]
    return x, y, c, chips


def _remote(src, dst, send_sems, recv_sems, k, to):
    return pltpu.make_async_remote_copy(src_ref=src, dst_ref=dst, send_sem=send_sems.at[k], recv_sem=recv_sems.at[k],
                                        device_id=to, device_id_type=_MESH)


def _half_rows(ref, half, hf, lead=None):
    rows = pl.ds(pl.multiple_of(hf * half, 16), half)
    return ref.at[rows, :] if lead is None else ref.at[lead, rows, :]


def _gather_shards(bufs, name):
    n = len(bufs)
    halves = [b.shape[0] // 2 for b in bufs]

    def body(*refs):
        ins, outs, send_sems, recv_sems = refs[:n], refs[n:2 * n], refs[2 * n], refs[2 * n + 1]
        x, y, c, chips = _place()
        me, sibling = (x, y, c), (x, y, 1 - c)
        slot = lambda b, cx, cy, hf: _half_rows(outs[b], halves[b], hf, lead=2 * cx + cy)
        first = [_remote(_half_rows(ins[b], halves[b], c), slot(b, x, y, c), send_sems, recv_sems, 6 * b + j, (cx, cy, c))
                 for b in range(n) for j, (cx, cy) in enumerate(chips)]
        for cp in first:
            cp.start()
        passed = []
        for j, (cx, cy) in enumerate(chips):
            for b in range(n):
                _remote(slot(b, cx, cy, c), slot(b, cx, cy, c), send_sems, recv_sems, 6 * b + j, me).wait_recv()
                fwd = _remote(slot(b, cx, cy, c), slot(b, cx, cy, c), send_sems, recv_sems, 6 * b + 3 + j, sibling)
                fwd.start()
                passed.append(fwd)
        for j, (cx, cy) in enumerate(chips):
            for b in range(n):
                _remote(slot(b, cx, cy, 1 - c), slot(b, cx, cy, 1 - c), send_sems, recv_sems, 6 * b + 3 + j, me).wait_recv()
        for cp in first + passed:
            cp.wait_send()

    return pl.pallas_call(
        body, name=name, out_shape=[jax.ShapeDtypeStruct((N_SHARD,) + b.shape, b.dtype) for b in bufs],
        in_specs=[_HBM] * n, out_specs=[_HBM] * n,
        scratch_shapes=[pltpu.SemaphoreType.DMA((6 * n,)), pltpu.SemaphoreType.DMA((6 * n,))],
    )(*bufs)


def _reduce_pair_exchange(gbufs, name):
    n = len(gbufs)
    halves = [g.shape[1] // 2 for g in gbufs]

    def body(*refs):
        ins, outs, send_sems, recv_sems = refs[:n], refs[n:2 * n], refs[2 * n], refs[2 * n + 1]
        x, y, c, _ = _place()
        cps = [_remote(ins[b].at[:, pl.ds(pl.multiple_of((1 - c) * halves[b], 16), halves[b]), :], outs[b],
                       send_sems, recv_sems, b, (x, y, 1 - c)) for b in range(n)]
        for cp in cps:
            cp.start()
        for cp in cps:
            cp.wait()

    return pl.pallas_call(
        body, name=name,
        out_shape=[jax.ShapeDtypeStruct((N_SHARD, h, g.shape[2]), g.dtype) for g, h in zip(gbufs, halves)],
        in_specs=[_HBM] * n, out_specs=[_HBM] * n,
        scratch_shapes=[pltpu.SemaphoreType.DMA((n,)), pltpu.SemaphoreType.DMA((n,))],
    )(*gbufs)


def _pair_add(gbuf, recv, c_arr, tr, name):
    _, rows, width = gbuf.shape
    half = rows // 2
    nt = half // tr

    def body(c_ref, a_ref, b_ref, o_ref):
        o_ref[...] = (a_ref[...] + b_ref[...]).astype(o_ref.dtype)

    blk = lambda f: pl.BlockSpec((None, tr, width), f)
    return pl.pallas_call(
        body, name=name, out_shape=jax.ShapeDtypeStruct((N_SHARD, half, width), jnp.bfloat16),
        grid_spec=pltpu.PrefetchScalarGridSpec(
            num_scalar_prefetch=1, grid=(N_SHARD, nt),
            in_specs=[blk(lambda s, i, c: (s, c[0] * nt + i, 0)), blk(lambda s, i, c: (s, i, 0))],
            out_specs=blk(lambda s, i, c: (s, i, 0))),
        compiler_params=_cparams(("parallel", "parallel")))(c_arr, gbuf, recv)


def _reduce_chip_exchange(parts, name):
    n = len(parts)

    def body(*refs):
        ins, outs, send_sems, recv_sems = refs[:n], refs[n:2 * n], refs[2 * n], refs[2 * n + 1]
        x, y, c, chips = _place()
        sends = [_remote(ins[b].at[2 * cx + cy], outs[b].at[j], send_sems, recv_sems, 3 * b + j, (cx, cy, c))
                 for b in range(n) for j, (cx, cy) in enumerate(chips)]
        for cp in sends:
            cp.start()
        for b in range(n):
            for j in range(len(chips)):
                _remote(ins[b].at[0], outs[b].at[j], send_sems, recv_sems, 3 * b + j, (x, y, c)).wait_recv()
        for cp in sends:
            cp.wait_send()

    return pl.pallas_call(
        body, name=name, out_shape=[jax.ShapeDtypeStruct((3,) + p.shape[1:], p.dtype) for p in parts],
        in_specs=[_HBM] * n, out_specs=[_HBM] * n,
        scratch_shapes=[pltpu.SemaphoreType.DMA((3 * n,)), pltpu.SemaphoreType.DMA((3 * n,))],
    )(*parts)


def _chip_add(part, recv, me_arr, tr, name):
    _, half, width = part.shape

    def body(me_ref, own, a0, a1, a2, o_ref):
        f = lambda r: r[...].astype(F32)
        o_ref[...] = ((f(own) + f(a0)) + f(a1)) + f(a2)

    specs = [pl.BlockSpec((None, tr, width), lambda i, me: (me[0], i, 0))]
    specs += [pl.BlockSpec((None, tr, width), functools.partial(lambda i, me, k: (k, i, 0), k=k)) for k in range(3)]
    return pl.pallas_call(
        body, name=name, out_shape=jax.ShapeDtypeStruct((half, width), F32),
        grid_spec=pltpu.PrefetchScalarGridSpec(
            num_scalar_prefetch=1, grid=(half // tr,), in_specs=specs,
            out_specs=pl.BlockSpec((tr, width), lambda i, me: (i, 0))),
        compiler_params=_cparams(("parallel",)))(me_arr, part, recv, recv, recv)


def _reduce_pair_share(rhalves, name):
    n = len(rhalves)

    def body(*refs):
        ins, outs, send_sems, recv_sems = refs[:n], refs[n:2 * n], refs[2 * n], refs[2 * n + 1]
        x, y, c, _ = _place()
        cps = [_remote(ins[b], outs[b], send_sems, recv_sems, b, (x, y, 1 - c)) for b in range(n)]
        for cp in cps:
            cp.start()
        for cp in cps:
            cp.wait()

    return pl.pallas_call(
        body, name=name, out_shape=[jax.ShapeDtypeStruct(r.shape, r.dtype) for r in rhalves],
        in_specs=[_HBM] * n, out_specs=[_HBM] * n,
        scratch_shapes=[pltpu.SemaphoreType.DMA((n,)), pltpu.SemaphoreType.DMA((n,))],
    )(*rhalves)


def _small_allreduce(buf):
    r, width = buf.shape
    n_dev = 8

    def body(x_ref, all_ref, sum_ref, send_sems, recv_sems, local_sem):
        x, y, c, chips = _place()
        me, sibling = (x, y, c), (x, y, 1 - c)

        def rows(px, py, pc):
            return all_ref.at[pl.ds(pl.multiple_of((4 * px + 2 * py + pc) * r, 8), r), :]

        def copy(k, block, to, src=None):
            return _remote(rows(*block) if src is None else src, rows(*block), send_sems, recv_sems, k, to)

        mine = pltpu.make_async_copy(x_ref, rows(*me), local_sem)
        mine.start()
        first = [copy(0, me, sibling, src=x_ref)]
        first += [copy(1 + j, me, (*chip, c), src=x_ref) for j, chip in enumerate(chips)]
        for cp in first:
            cp.start()
        passed = [copy(4 + j, (*chip, c), sibling) for j, chip in enumerate(chips)]
        for j, chip in enumerate(chips):
            copy(1 + j, (*chip, c), me).wait_recv()
            passed[j].start()
        copy(0, sibling, me).wait_recv()
        for j, chip in enumerate(chips):
            copy(4 + j, (*chip, 1 - c), me).wait_recv()
        for cp in first + passed:
            cp.wait_send()
        mine.wait()
        total = all_ref[0:r, :]
        for k in range(1, n_dev):
            total = total + all_ref[k * r:(k + 1) * r, :]
        sum_ref[...] = total

    vm = pl.BlockSpec(memory_space=pltpu.VMEM)
    _, total = pl.pallas_call(
        body, name="small_allreduce",
        out_shape=[jax.ShapeDtypeStruct((n_dev * r, width), buf.dtype), jax.ShapeDtypeStruct((r, width), buf.dtype)],
        in_specs=[vm], out_specs=[vm, vm],
        scratch_shapes=[pltpu.SemaphoreType.DMA((7,)), pltpu.SemaphoreType.DMA((7,)), pltpu.SemaphoreType.DMA],
    )(buf)
    return total


def _row_tile(rows, cap):
    if rows <= cap:
        return rows
    t = (cap // 8) * 8
    while t >= 8:
        if rows % t == 0:
            return t
        t -= 8
    return rows


def _adamw(w, g, m, v, name):
    shape = w.shape
    cols = shape[-1] if len(shape) <= 3 else shape[-2] * shape[-1]
    w2, g2, m2, v2 = (a.reshape(-1, cols) for a in (w, g, m, v))
    rows = w2.shape[0]
    tr = _row_tile(rows, 256)

    def body(w_ref, g_ref, m_ref, v_ref, d_ref, mo_ref, vo_ref):
        gv = g_ref[...]
        mn = ADAM_B1 * m_ref[...] + (1.0 - ADAM_B1) * gv
        vn = ADAM_B2 * v_ref[...] + (1.0 - ADAM_B2) * (gv * gv)
        m_hat = mn / (1.0 - ADAM_B1 ** ADAM_STEP)
        v_hat = vn / (1.0 - ADAM_B2 ** ADAM_STEP)
        d_ref[...] = -ADAM_LR * (m_hat / (jnp.sqrt(v_hat) + ADAM_EPS) + ADAM_WD * w_ref[...])
        mo_ref[...] = mn
        vo_ref[...] = vn

    blk = pl.BlockSpec((tr, cols), lambda i: (i, 0))
    outs = pl.pallas_call(
        body, name=name, grid=(rows // tr,), in_specs=[blk] * 4, out_specs=[blk] * 3,
        out_shape=[jax.ShapeDtypeStruct((rows, cols), F32)] * 3, compiler_params=_cparams(("parallel",)))(w2, g2, m2, v2)
    return tuple(o.reshape(shape) for o in outs)


_WEIGHT_NAMES = ('w_in', 'conv_w', 'dn_a_log', 'dn_dt_bias', 'dn_norm_w', 'q_norm_w', 'w_uq', 'kv_norm_w', 'w_uk',
                 'w_uv', 'w_br_dn', 'w_br_mla', 'w_o', 'ln1_g', 'ln1_b', 'w_ffn_in', 'w_ffn_out', 'w_ple',
                 'w_ple_gate', 'ln2_g', 'ln2_b')
_SMALL_NAMES = ('ln1_g', 'ln1_b', 'ln2_g', 'ln2_b', 'q_norm_w', 'kv_norm_w', 'dn_norm_w', 'dn_a_log', 'dn_dt_bias')
_SMALL_GROUP = 8
_CONV_SMALL_ROW = len(_SMALL_NAMES) * _SMALL_GROUP
_CONV_SMALL_ROWS = DN_CONV * QKV_W // FLAT_W


def _pack_small(gw):
    rows = [jnp.pad(gw[n][None, :], ((0, _SMALL_GROUP - 1), (0, FLAT_W - gw[n].shape[0]))) for n in _SMALL_NAMES]
    rows.append(jnp.pad(gw['conv_w'].reshape(_CONV_SMALL_ROWS, FLAT_W), ((0, SMALL_ROWS - _CONV_SMALL_ROW - _CONV_SMALL_ROWS), (0, 0))))
    return jnp.concatenate(rows, axis=0)


def kernel(x, p, positions, w_in, conv_w, dn_a_log, dn_dt_bias, dn_norm_w, q_norm_w, w_uq, kv_norm_w, w_uk, w_uv, w_br_dn, w_br_mla, w_o, ln1_g, ln1_b, w_ffn_in, w_ffn_out, w_ple, w_ple_gate, ln2_g, ln2_b, loss_target, m_w_in, m_conv_w, m_dn_a_log, m_dn_dt_bias, m_dn_norm_w, m_q_norm_w, m_w_uq, m_kv_norm_w, m_w_uk, m_w_uv, m_w_br_dn, m_w_br_mla, m_w_o, m_ln1_g, m_ln1_b, m_w_ffn_in, m_w_ffn_out, m_w_ple, m_w_ple_gate, m_ln2_g, m_ln2_b, v_w_in, v_conv_w, v_dn_a_log, v_dn_dt_bias, v_dn_norm_w, v_q_norm_w, v_w_uq, v_kv_norm_w, v_w_uk, v_w_uv, v_w_br_dn, v_w_br_mla, v_w_o, v_ln1_g, v_ln1_b, v_w_ffn_in, v_w_ffn_out, v_w_ple, v_w_ple_gate, v_ln2_g, v_ln2_b):
    ws = dict(w_in=w_in, conv_w=conv_w, dn_a_log=dn_a_log, dn_dt_bias=dn_dt_bias, dn_norm_w=dn_norm_w, q_norm_w=q_norm_w,
              w_uq=w_uq, kv_norm_w=kv_norm_w, w_uk=w_uk, w_uv=w_uv, w_br_dn=w_br_dn, w_br_mla=w_br_mla, w_o=w_o,
              ln1_g=ln1_g, ln1_b=ln1_b, w_ffn_in=w_ffn_in, w_ffn_out=w_ffn_out, w_ple=w_ple, w_ple_gate=w_ple_gate,
              ln2_g=ln2_g, ln2_b=ln2_b)
    ms = dict(w_in=m_w_in, conv_w=m_conv_w, dn_a_log=m_dn_a_log, dn_dt_bias=m_dn_dt_bias, dn_norm_w=m_dn_norm_w,
              q_norm_w=m_q_norm_w, w_uq=m_w_uq, kv_norm_w=m_kv_norm_w, w_uk=m_w_uk, w_uv=m_w_uv, w_br_dn=m_w_br_dn,
              w_br_mla=m_w_br_mla, w_o=m_w_o, ln1_g=m_ln1_g, ln1_b=m_ln1_b, w_ffn_in=m_w_ffn_in, w_ffn_out=m_w_ffn_out,
              w_ple=m_w_ple, w_ple_gate=m_w_ple_gate, ln2_g=m_ln2_g, ln2_b=m_ln2_b)
    vs = dict(w_in=v_w_in, conv_w=v_conv_w, dn_a_log=v_dn_a_log, dn_dt_bias=v_dn_dt_bias, dn_norm_w=v_dn_norm_w,
              q_norm_w=v_q_norm_w, w_uq=v_w_uq, kv_norm_w=v_kv_norm_w, w_uk=v_w_uk, w_uv=v_w_uv, w_br_dn=v_w_br_dn,
              w_br_mla=v_w_br_mla, w_o=v_w_o, ln1_g=v_ln1_g, ln1_b=v_ln1_b, w_ffn_in=v_w_ffn_in, w_ffn_out=v_w_ffn_out,
              w_ple=v_w_ple, w_ple_gate=v_w_ple_gate, ln2_g=v_ln2_g, ln2_b=v_ln2_b)
    mx, my, mc = lax.axis_index("x"), lax.axis_index("y"), lax.axis_index("c")

    me_chip = 2 * mx + my
    c_arr = jnp.reshape(mc, (1,)).astype(jnp.int32)
    me_arr = jnp.reshape(me_chip, (1,)).astype(jnp.int32)
    sharded = ('w_in', 'w_ffn_in') + tuple(name for name, _, _ in _FLATB_PIECES)

    local = _pack_shards({name: ws[name][0] for name in sharded}, conv_w[0])
    full, conv_full = _unpack_gathered(_gather_shards(local, "gather_shards"), local, me_chip)
    wt = _prep_weights(full)
    small = {n: ws[n][0] for n in _SMALL_NAMES}
    small['conv_w'] = conv_full
    sp = _prep_small(small)
    cosb, sinb = _rope_tables(positions[0])

    loss_lanes, dx, g = _local_step(x[0], p[0, 0], cosb, sinb, loss_target[0], wt, sp)
    gw = _unprep_grads(g)
    loss = lax.psum(jnp.sum(loss_lanes), ("x", "y", "c"))

    gbufs = _pack_grads(gw)
    got = _reduce_pair_exchange(gbufs, "reduce_pair_exchange")
    parts = [_pair_add(g_, r_, c_arr, tr, "pair_add_%d" % i) for i, (g_, r_, tr) in enumerate(zip(gbufs, got, _ADD_TILES))]
    got = _reduce_chip_exchange(parts, "reduce_chip_exchange")
    mine = [_chip_add(p_, r_, me_arr, tr, "chip_add_%d" % i) for i, (p_, r_, tr) in enumerate(zip(parts, got, _ADD_TILES))]
    reduced = _unpack_reduced(mine, _reduce_pair_share(mine, "reduce_pair_share"), mc)
    tot = _small_allreduce(_pack_small(gw))
    gred = {name: reduced[name][None] for name in sharded}
    for i, n in enumerate(_SMALL_NAMES):
        gred[n] = tot[i * _SMALL_GROUP, :ws[n].shape[1]][None]
    conv_tot = tot[_CONV_SMALL_ROW:_CONV_SMALL_ROW + _CONV_SMALL_ROWS].reshape(DN_CONV, QKV_W)
    gred['conv_w'] = lax.dynamic_slice_in_dim(conv_tot, (2 * mx + my) * _CONV_SHARD, _CONV_SHARD, axis=1)[None]

    deltas, new_m, new_v = {}, {}, {}
    for n in _WEIGHT_NAMES:
        gred[n] = gred[n].reshape(ws[n].shape)
        deltas[n], new_m[n], new_v[n] = _adamw(ws[n], gred[n], ms[n], vs[n], "adamw_" + n)
    return (loss, dx[None], *[gred[n] for n in _WEIGHT_NAMES], *[deltas[n] for n in _WEIGHT_NAMES],
            *[new_m[n] for n in _WEIGHT_NAMES], *[new_v[n] for n in _WEIGHT_NAMES])
```

```python
import functools
import math

import jax
import jax.numpy as jnp
from jax import lax
from jax.experimental import pallas as pl
from jax.experimental.pallas import tpu as pltpu

F32 = jnp.float32
_CDT = jnp.bfloat16
_HI = lax.Precision.HIGHEST
_MESH = pl.DeviceIdType.MESH

D_MODEL = 1024
PLE_DIM = 256
HEADS = 8
DN_DK = 128
DN_CHUNK = 64
DN_CONV = 4
QKV_W = 3 * HEADS * DN_DK
Q_LORA = 384
KV_LORA = 256
NOPE = 128
ROPE = 64
ROPE_PAD = 128
FFN_HIDDEN = 2816
D_IN = 6864
ROPE_BASE = 10000.0
ALPHA = 2.0 ** 0.25
ATT_SCALE = (NOPE + ROPE) ** -0.5
NEG_BIG = -1e30
ADAM_LR, ADAM_B1, ADAM_B2, ADAM_EPS, ADAM_WD, ADAM_STEP = 0.001, 0.9, 0.999, 1e-08, 0.01, 10

LANE = 128
VMEM_LIMIT = 56 * 1024 * 1024
MM_VMEM_BUDGET = 40 * 1024 * 1024
N_SHARD = 4
FLAT_W = 1024
SMALL_ROWS = 88


def _tile(dim, cap):
    if dim <= cap:
        return dim
    t = (cap // LANE) * LANE
    while t >= LANE:
        if dim % t == 0:
            return t
        t -= LANE
    return dim


def _cparams(sem):
    return pltpu.CompilerParams(dimension_semantics=sem, vmem_limit_bytes=VMEM_LIMIT)


def _mm(a, b, *, name, ta=False, tb=False, add=None, add_scale=1.0, out_dtype=F32, heads=None,
        a_head=None, b_head=None, out_head=None, dims=None, tm=1408, tn=1408):
    m, n, k = dims
    tm, tn = _tile(m, tm), _tile(n, tn)
    sa, sb, so = a.dtype.itemsize, b.dtype.itemsize, jnp.dtype(out_dtype).itemsize

    def vmem_need(tk_):
        acc = tm * tn * 4 if tk_ < k else 0
        extra = 2 * tm * tn * 4 if add is not None else 0
        return 2 * (tm * tk_ * sa + tk_ * tn * sb) + 2 * tm * tn * so + acc + extra

    tk = k
    while vmem_need(tk) > MM_VMEM_BUDGET and tk > LANE:
        smaller = _tile(k, tk - LANE)
        if smaller >= tk:
            break
        tk = smaller
    nk = k // tk
    hgrid = () if heads is None else (heads,)
    off = len(hgrid)

    def spec(rows, cols, rtile, ctile, rsel, csel, layout):
        def idx(*g):
            h = g[0] if off else 0
            ri, ci = g[off + rsel], g[off + csel]
            if layout == 'lead':
                return (h, ri, ci)
            if layout == 'col':
                return (ri, h * (cols // ctile) + ci)
            return (ri, ci)
        if layout == 'lead':
            return pl.BlockSpec((None, rtile, ctile), idx)
        return pl.BlockSpec((rtile, ctile), idx)

    a_spec = spec(k, m, tk, tm, 2, 0, a_head) if ta else spec(m, k, tm, tk, 0, 2, a_head)
    b_spec = spec(n, k, tn, tk, 1, 2, b_head) if tb else spec(k, n, tk, tn, 2, 1, b_head)
    o_spec = spec(m, n, tm, tn, 0, 1, out_head)
    in_specs = [a_spec, b_spec]
    args = [a, b]
    if add is not None:
        in_specs.append(spec(m, n, tm, tn, 0, 1, out_head))
        args.append(add)
    dn = (((0 if ta else 1,), (1 if tb else 0,)), ((), ()))

    def body(*refs):
        a_ref, b_ref = refs[0], refs[1]
        prod = lax.dot_general(a_ref[...].astype(_CDT), b_ref[...].astype(_CDT), dn, preferred_element_type=F32)
        if nk == 1:
            o_ref = refs[-1]
            if add is not None:
                prod = prod + refs[2][...].astype(F32) * add_scale
            o_ref[...] = prod.astype(out_dtype)
            return
        o_ref, acc_ref = refs[-2], refs[-1]
        kk = pl.program_id(off + 2)

        @pl.when(kk == 0)
        def _():
            if add is not None:
                acc_ref[...] = refs[2][...].astype(F32) * add_scale
            else:
                acc_ref[...] = jnp.zeros_like(acc_ref)

        acc_ref[...] += prod

        @pl.when(kk == nk - 1)
        def _():
            o_ref[...] = acc_ref[...].astype(out_dtype)

    if out_head == 'lead':
        oshape = (heads, m, n)
    elif out_head == 'col':
        oshape = (m, heads * n)
    else:
        oshape = (m, n)
    sem = ("parallel",) * (off + 2) + ("arbitrary",)
    return pl.pallas_call(
        body, name=name, grid=hgrid + (m // tm, n // tn, nk), in_specs=in_specs, out_specs=o_spec,
        out_shape=jax.ShapeDtypeStruct(oshape, out_dtype),
        scratch_shapes=[pltpu.VMEM((tm, tn), F32)] if nk > 1 else [],
        compiler_params=_cparams(sem))(*args)


def _mm2(a, b, **kw):
    ta, tb = kw.get('ta', False), kw.get('tb', False)
    m = a.shape[1] if ta else a.shape[0]
    k = a.shape[0] if ta else a.shape[1]
    n = b.shape[0] if tb else b.shape[1]
    return _mm(a, b, dims=(m, n, k), **kw)


def _rowwise(fn, rows, bcast, outs, reds=(), *, name, tm=256, heads=None):
    t = rows[0][0].shape[0]
    tm = min(tm, t)
    hn = 1 if heads is None else heads
    in_specs, args = [], []
    for arr, width, base, per_head in rows:
        in_specs.append(pl.BlockSpec((tm, width), functools.partial(
            lambda i, h, base, per_head: (i, base + (h if per_head else 0)), base=base, per_head=per_head)))
        args.append(arr)
    for arr in bcast:
        in_specs.append(pl.BlockSpec(arr.shape, lambda i, h: (0, 0)))
        args.append(arr)
    out_specs, out_shape = [], []
    for total, width, per_head, dt in outs:
        out_specs.append(pl.BlockSpec((tm, width), functools.partial(
            lambda i, h, per_head: (i, h if per_head else 0), per_head=per_head)))
        out_shape.append(jax.ShapeDtypeStruct((t, total), dt))
    for shp in reds:
        out_specs.append(pl.BlockSpec(shp, lambda i, h: (0, 0)))
        out_shape.append(jax.ShapeDtypeStruct(shp, F32))
    n_in, n_out, n_red = len(args), len(outs), len(reds)

    def body(*refs):
        i, h = pl.program_id(0), pl.program_id(1)
        vals = fn(h, *[r[...] for r in refs[:n_in]])
        for r, v in zip(refs[n_in:n_in + n_out], vals[:n_out]):
            r[...] = v.astype(r.dtype)
        if n_red:
            @pl.when((i == 0) & (h == 0))
            def _():
                for r in refs[n_in + n_out:]:
                    r[...] = jnp.zeros_like(r)
            for r, v in zip(refs[n_in + n_out:], vals[n_out:]):
                r[...] += v

    sem = ("arbitrary", "arbitrary") if n_red else ("parallel", "parallel")
    res = pl.pallas_call(body, name=name, grid=(t // tm, hn), in_specs=in_specs, out_specs=out_specs,
                         out_shape=out_shape, compiler_params=_cparams(sem))(*args)
    return tuple(res)


def _sigmoid(x):
    return 1.0 / (1.0 + jnp.exp(-x))


def _silu(x):
    return x * _sigmoid(x)


def _softplus(x):
    return jnp.maximum(x, 0.0) + jnp.log(1.0 + jnp.exp(-jnp.abs(x)))


def _layer_norm(t, g, b):
    mu = jnp.mean(t, axis=-1, keepdims=True)
    d = t - mu
    var = jnp.mean(d * d, axis=-1, keepdims=True)
    return d * lax.rsqrt(var + 1e-5) * g + b


def _rms_norm(t, w):
    return t * lax.rsqrt(jnp.mean(t * t, axis=-1, keepdims=True) + 1e-6) * w


def _swap_rope_halves(t):
    lane = lax.broadcasted_iota(jnp.int32, t.shape, 1) % ROPE_PAD
    n = t.shape[1]
    up = pltpu.roll(t, n - ROPE // 2, axis=1)
    dn = pltpu.roll(t, ROPE // 2, axis=1)
    return jnp.where(lane < ROPE // 2, up, jnp.where(lane < ROPE, dn, 0.0))


def _rope(t, cosb, sinb):
    reps = t.shape[1] // ROPE_PAD
    c = jnp.tile(cosb, (1, reps)) if reps > 1 else cosb
    s = jnp.tile(sinb, (1, reps)) if reps > 1 else sinb
    return t * c + _swap_rope_halves(t) * s


def _rope_bwd(d, cosb, sinb):
    reps = d.shape[1] // ROPE_PAD
    c = jnp.tile(cosb, (1, reps)) if reps > 1 else cosb
    s = jnp.tile(sinb, (1, reps)) if reps > 1 else sinb
    return d * c + _swap_rope_halves(d * s)


_CONV_ROWS = 256
_CONV_COLS = 256


def _conv_window(ref, r0, lo, hi, t):
    parts = []
    start, stop = r0 - lo, r0 + _CONV_ROWS + hi
    if start < 0:
        parts.append(jnp.zeros((-start, ref.shape[1]), F32))
        start = 0
    tail = max(stop - t, 0)
    parts.append(ref[start:stop - tail, :].astype(F32))
    if tail:
        parts.append(jnp.zeros((tail, ref.shape[1]), F32))
    return parts[0] if len(parts) == 1 else jnp.concatenate(parts, axis=0)


def _conv_taps(win, w_ref, n_out):
    acc = win[8:8 + n_out] * w_ref[DN_CONV - 1:DN_CONV, :]
    for i in range(DN_CONV - 1):
        acc = acc + pltpu.roll(win, DN_CONV - 1 - i, axis=0)[8:8 + n_out] * w_ref[i:i + 1, :]
    return acc


def _conv_silu(x, w):
    t, ch = x.shape

    def body(x_ref, w_ref, o_ref):
        for r in range(t // _CONV_ROWS):
            r0 = r * _CONV_ROWS
            c = _conv_taps(_conv_window(x_ref, r0, 8, 0, t), w_ref, _CONV_ROWS)
            o_ref[r0:r0 + _CONV_ROWS, :] = _silu(c)

    return pl.pallas_call(
        body, name="conv_silu", grid=(ch // _CONV_COLS,),
        in_specs=[pl.BlockSpec((t, _CONV_COLS), lambda j: (0, j)), pl.BlockSpec((DN_CONV, _CONV_COLS), lambda j: (0, j))],
        out_specs=pl.BlockSpec((t, _CONV_COLS), lambda j: (0, j)),
        out_shape=jax.ShapeDtypeStruct((t, ch), F32), compiler_params=_cparams(("parallel",)))(x, w)


def _conv_silu_bwd(x, w, dy):
    t, ch = x.shape

    def body(x_ref, w_ref, dy_ref, dx_ref, dw_ref):
        dws = [jnp.zeros((1, _CONV_COLS), F32) for _ in range(DN_CONV)]
        for r in range(t // _CONV_ROWS):
            r0 = r * _CONV_ROWS
            n_ext = _CONV_ROWS + 8
            xw = _conv_window(x_ref, r0, 8, 8, t)
            c = _conv_taps(xw, w_ref, n_ext)
            sg = _sigmoid(c)
            ds = _conv_window(dy_ref, r0, 0, 8, t) * (sg * (1.0 + c * (1.0 - sg)))
            dx = ds[:_CONV_ROWS] * w_ref[DN_CONV - 1:DN_CONV, :]
            for i in range(DN_CONV - 1):
                sh = DN_CONV - 1 - i
                dx = dx + pltpu.roll(ds, n_ext - sh, axis=0)[:_CONV_ROWS] * w_ref[i:i + 1, :]
            dx_ref[r0:r0 + _CONV_ROWS, :] = dx.astype(dx_ref.dtype)
            ds0 = ds[:_CONV_ROWS]
            for i in range(DN_CONV):
                sh = DN_CONV - 1 - i
                xs = xw if sh == 0 else pltpu.roll(xw, sh, axis=0)
                dws[i] = dws[i] + jnp.sum(ds0 * xs[8:8 + _CONV_ROWS], axis=0, keepdims=True)
        for i in range(DN_CONV):
            dw_ref[i:i + 1, :] = dws[i]

    blk = pl.BlockSpec((t, _CONV_COLS), lambda j: (0, j))
    wblk = pl.BlockSpec((DN_CONV, _CONV_COLS), lambda j: (0, j))
    return pl.pallas_call(
        body, name="conv_silu_bwd", grid=(ch // _CONV_COLS,), in_specs=[blk, wblk, blk], out_specs=[blk, wblk],
        out_shape=[jax.ShapeDtypeStruct((t, ch), _CDT), jax.ShapeDtypeStruct((DN_CONV, ch), F32)],
        compiler_params=_cparams(("parallel",)))(x, w, dy)


_PA_ROWS = 512


def _bmm(a, b, spec, exact=False):
    if exact:
        return jnp.einsum(spec, a, b, precision=_HI, preferred_element_type=F32)
    return jnp.einsum(spec, a.astype(_CDT), b.astype(_CDT), preferred_element_type=F32)


def _split16(a):
    hi = a.astype(jnp.bfloat16)
    return hi, (a - hi.astype(F32)).astype(jnp.bfloat16)


def _bmm3(a, b, spec):
    ah, al = _split16(a)
    bh, bl = _split16(b)
    e = lambda p, q: jnp.einsum(spec, p, q, preferred_element_type=F32)
    return e(ah, bh) + (e(ah, bl) + e(al, bh))


def _tri_inverse(l_mat, eye):
    pw = -l_mat
    t_inv = eye + pw
    for _ in range(5):
        pw = _bmm3(pw, pw, 'bij,bjk->bik')
        t_inv = t_inv + _bmm3(t_inv, pw, 'bij,bjk->bik')
    return t_inv


@jax.custom_vjp
def _tri_inverse_saved(l_mat, t_saved):
    return t_saved


def _tri_inverse_saved_fwd(l_mat, t_saved):
    return t_saved, t_saved


def _tri_inverse_saved_bwd(t_saved, dt):
    left = _bmm3(t_saved, dt, 'bji,bjk->bik')
    return -_bmm3(left, t_saved, 'bij,bkj->bik'), jnp.zeros_like(t_saved)


_tri_inverse_saved.defvjp(_tri_inverse_saved_fwd, _tri_inverse_saved_bwd)


def _phase_a(h, q, k, v, ba, alog, dtb, t_saved=None):
    r = q.shape[0]
    nb = r // DN_CHUNK
    c = DN_CHUNK
    lane = lax.broadcasted_iota(jnp.int32, (1, LANE), 1)
    selb = (lane == h).astype(F32)
    sela = (lane == h + HEADS).astype(F32)
    b_raw = jnp.sum(ba * selb, axis=1, keepdims=True)
    a_raw = jnp.sum(ba * sela, axis=1, keepdims=True)
    al = jnp.sum(alog * selb, axis=1, keepdims=True)
    dt = jnp.sum(dtb * selb, axis=1, keepdims=True)
    beta = jnp.broadcast_to(_sigmoid(b_raw), (r, LANE))
    g = jnp.broadcast_to(-jnp.exp(al) * _softplus(a_raw + dt), (r, LANE))
    qn = q * lax.rsqrt(jnp.sum(q * q, -1, keepdims=True) + 1e-6) * (DN_DK ** -0.5)
    kn = k * lax.rsqrt(jnp.sum(k * k, -1, keepdims=True) + 1e-6)
    q3, k3, v3 = qn.reshape(nb, c, LANE), kn.reshape(nb, c, LANE), v.reshape(nb, c, LANE)
    b3, g3 = beta.reshape(nb, c, LANE), g.reshape(nb, c, LANE)
    ri = lax.broadcasted_iota(jnp.int32, (nb, c, c), 1)
    ci = lax.broadcasted_iota(jnp.int32, (nb, c, c), 2)
    tril, strict = ri >= ci, ri > ci
    gc = _bmm(tril.astype(F32), g3, 'bij,bjd->bid', exact=True)
    onehot = (lax.broadcasted_iota(jnp.int32, (nb, c, LANE), 2) == 0).astype(F32)
    g_row = _bmm(onehot, gc, 'bid,bjd->bij', exact=True)
    diff = gc[:, :, :c] - g_row
    decay = jnp.where(tril, jnp.exp(jnp.where(tril, diff, 0.0)), 0.0)
    kb = k3 * b3
    l_mat = jnp.where(strict, _bmm(kb, k3, 'bid,bjd->bij') * decay, 0.0)
    if t_saved is None:
        t_inv = _tri_inverse(l_mat, (ri == ci).astype(F32))
    else:
        t_inv = _tri_inverse_saved(l_mat, t_saved.reshape(nb, c, c))
    eg = jnp.exp(gc)
    u = _bmm(t_inv, v3 * b3, 'bij,bje->bie')
    w = _bmm(t_inv, kb * eg, 'bij,bje->bie')
    intra = jnp.where(tril, _bmm(q3, k3, 'bid,bjd->bij') * decay, 0.0)
    qd = q3 * eg
    gl = jnp.sum(g3, axis=1, keepdims=True)
    kt = k3 * jnp.exp(gl - gc)
    outs = (u.reshape(r, LANE), w.reshape(r, LANE), qd.reshape(r, LANE), kt.reshape(r, LANE),
            intra.reshape(r, c), gl.reshape(nb, LANE))
    return outs + (t_inv.reshape(r, c),) if t_saved is None else outs


def _pa_specs(t):
    rr = min(_PA_ROWS, t)
    nb = rr // DN_CHUNK
    qkv = [pl.BlockSpec((rr, LANE), functools.partial(lambda i, h, o: (i, o + h), o=o)) for o in (0, HEADS, 2 * HEADS)]
    ba = pl.BlockSpec((rr, LANE), lambda i, h: (i, 3))
    vec = pl.BlockSpec((1, LANE), lambda i, h: (0, 0))
    row = pl.BlockSpec((rr, LANE), lambda i, h: (i, h))
    intra = pl.BlockSpec((None, rr, DN_CHUNK), lambda i, h: (h, i, 0))
    gl = pl.BlockSpec((nb, LANE), lambda i, h: (i, h))
    return rr, qkv, ba, vec, row, intra, gl


def _grid_ends(grid):
    first = lambda: functools.reduce(jnp.logical_and, [pl.program_id(a) == 0 for a in range(len(grid))])
    last = lambda: functools.reduce(jnp.logical_and, [pl.program_id(a) == n - 1 for a, n in enumerate(grid)])
    return first, last


def _delta_local(qkv_act, pm, alog, dtb, rider=None):
    t = qkv_act.shape[0]
    rr, qkv, ba, vec, row, intra, gl = _pa_specs(t)

    def body(q, k, v, b, al, dt, *outs):
        vals = _phase_a(pl.program_id(1), q[...], k[...], v[...], b[...], al[...], dt[...])
        for o, val in zip(outs, vals):
            o[...] = val

    wide = jax.ShapeDtypeStruct((t, HEADS * LANE), F32)
    sq = jax.ShapeDtypeStruct((HEADS, t, DN_CHUNK), F32)
    grid = (t // rr, HEADS)
    return _carried_call(
        body, rider, *_grid_ends(grid), name="delta_local", grid=grid, in_specs=qkv + [ba, vec, vec],
        out_specs=[row] * 4 + [intra, gl, intra],
        out_shape=[wide] * 4 + [sq, jax.ShapeDtypeStruct((t // DN_CHUNK, HEADS * LANE), F32), sq],
        scratch_shapes=[], sem=("arbitrary", "arbitrary"), args=(qkv_act, qkv_act, qkv_act, pm, alog, dtb))


def _delta_local_bwd(qkv_act, pm, alog, dtb, t_inv, du, dw, dqd, dkt, dintra, dgl, rider=None):
    t = qkv_act.shape[0]
    rr, qkv, ba, vec, row, intra, gl = _pa_specs(t)

    def body(q, k, v, b, al, dt, ti, du_r, dw_r, dqd_r, dkt_r, di_r, dgl_r, dq_o, dk_o, dv_o, dba_o, dal_o, ddt_o):
        i, h = pl.program_id(0), pl.program_id(1)
        t_saved = ti[...]
        _, vjp = jax.vjp(lambda *a: _phase_a(h, *a, t_saved=t_saved), q[...], k[...], v[...], b[...], al[...], dt[...])
        dq, dk, dv, dba, dal, ddt = vjp((du_r[...], dw_r[...], dqd_r[...], dkt_r[...], di_r[...], dgl_r[...]))
        dq_o[...], dk_o[...], dv_o[...] = dq, dk, dv

        @pl.when(h == 0)
        def _():
            dba_o[...] = jnp.zeros_like(dba_o)

        @pl.when((h == 0) & (i == 0))
        def _():
            dal_o[...] = jnp.zeros_like(dal_o)
            ddt_o[...] = jnp.zeros_like(ddt_o)

        dba_o[...] += dba
        dal_o[...] += dal
        ddt_o[...] += ddt

    wide = jax.ShapeDtypeStruct((t, HEADS * LANE), F32)
    vshape = jax.ShapeDtypeStruct((1, LANE), F32)
    grid = (t // rr, HEADS)
    return _carried_call(
        body, rider, *_grid_ends(grid), name="delta_local_bwd", grid=grid,
        in_specs=qkv + [ba, vec, vec, intra] + [row] * 4 + [intra, gl],
        out_specs=[row] * 3 + [pl.BlockSpec((rr, LANE), lambda i, h: (i, 0)), vec, vec],
        out_shape=[wide] * 3 + [jax.ShapeDtypeStruct((t, LANE), F32), vshape, vshape],
        scratch_shapes=[], sem=("arbitrary", "arbitrary"),
        args=(qkv_act, qkv_act, qkv_act, pm, alog, dtb, t_inv, du, dw, dqd, dkt, dintra, dgl))


_SCAN_ROWS = 512


def _dot(a, b, dn):
    return lax.dot_general(a.astype(_CDT), b.astype(_CDT), (dn, ((), ())), preferred_element_type=F32)


_NN = ((1,), (0,))
_NT = ((1,), (1,))
_TN = ((0,), (0,))


def _delta_scan(u, w, qd, kt, intra, gl, rider=None):
    t = u.shape[0]
    rr = min(_SCAN_ROWS, t)
    nc = rr // DN_CHUNK

    def body(u_ref, w_ref, qd_ref, kt_ref, a_ref, gl_ref, o_ref, sall_ref, s_scr):
        @pl.when(pl.program_id(0) == 0)
        def _():
            s_scr[...] = jnp.zeros_like(s_scr)

        def chunk(c, carry):
            r0 = pl.multiple_of(c * DN_CHUNK, DN_CHUNK)
            rows = pl.ds(r0, DN_CHUNK)
            e = jnp.exp(gl_ref[pl.ds(c, 1), :])
            for h in range(HEADS):
                cs = slice(h * LANE, (h + 1) * LANE)
                s = s_scr[h]
                sall_ref[c, h] = s
                v_new = u_ref[rows, cs] - _dot(w_ref[rows, cs], s, _NN)
                o_ref[rows, cs] = _dot(qd_ref[rows, cs], s, _NN) + _dot(a_ref[h, rows, :], v_new, _NN)
                s_scr[h] = s * e[:, cs] + _dot(kt_ref[rows, cs], v_new, _TN)
            return carry

        lax.fori_loop(0, nc, chunk, 0)

    row = pl.BlockSpec((rr, HEADS * LANE), lambda i: (i, 0))
    grid = (t // rr,)
    return _carried_call(
        body, rider, *_grid_ends(grid), name="delta_scan", grid=grid,
        in_specs=[row] * 4 + [pl.BlockSpec((HEADS, rr, DN_CHUNK), lambda i: (0, i, 0)),
                              pl.BlockSpec((nc, HEADS * LANE), lambda i: (i, 0))],
        out_specs=[row, pl.BlockSpec((nc, HEADS, LANE, LANE), lambda i: (i, 0, 0, 0))],
        out_shape=[jax.ShapeDtypeStruct((t, HEADS * LANE), F32),
                   jax.ShapeDtypeStruct((t // DN_CHUNK, HEADS, LANE, LANE), F32)],
        scratch_shapes=[pltpu.VMEM((HEADS, LANE, LANE), F32)], sem=("arbitrary",), args=(u, w, qd, kt, intra, gl))


def _delta_scan_bwd(u, w, qd, kt, intra, gl, sall, do):
    t = u.shape[0]
    rr = min(_SCAN_ROWS, t)
    nc = rr // DN_CHUNK
    ng = t // rr

    def body(u_ref, w_ref, qd_ref, kt_ref, a_ref, gl_ref, sall_ref, do_ref,
             du_ref, dw_ref, dqd_ref, dkt_ref, da_ref, dgl_ref, ds_scr):
        @pl.when(pl.program_id(0) == 0)
        def _():
            ds_scr[...] = jnp.zeros_like(ds_scr)

        def chunk(cc, carry):
            c = nc - 1 - cc
            r0 = pl.multiple_of(c * DN_CHUNK, DN_CHUNK)
            rows = pl.ds(r0, DN_CHUNK)
            e = jnp.exp(gl_ref[pl.ds(c, 1), :])
            for h in range(HEADS):
                cs = slice(h * LANE, (h + 1) * LANE)
                s = sall_ref[c, h]
                ds_out = ds_scr[h]
                w_c, kt_c, qd_c, do_c = w_ref[rows, cs], kt_ref[rows, cs], qd_ref[rows, cs], do_ref[rows, cs]
                a_c = a_ref[h, rows, :]
                v_new = u_ref[rows, cs] - _dot(w_c, s, _NN)
                dv_new = _dot(a_c, do_c, _TN) + _dot(kt_c, ds_out, _NN)
                da_ref[h, rows, :] = _dot(do_c, v_new, _NT)
                dqd_ref[rows, cs] = _dot(do_c, s, _NT)
                dkt_ref[rows, cs] = _dot(v_new, ds_out, _NT)
                du_ref[rows, cs] = dv_new
                dw_ref[rows, cs] = -_dot(dv_new, s, _NT)
                eh = e[:, cs]
                dgl_ref[pl.ds(pl.multiple_of(c * 8, 8), 8), cs] = jnp.broadcast_to(
                    jnp.sum(ds_out * s, axis=0, keepdims=True) * eh, (8, LANE))
                ds_scr[h] = ds_out * eh + _dot(qd_c, do_c, _TN) - _dot(w_c, dv_new, _TN)
            return carry

        lax.fori_loop(0, nc, chunk, 0)

    rev = lambda i: (ng - 1 - i, 0)
    row = pl.BlockSpec((rr, HEADS * LANE), rev)
    a_spec = pl.BlockSpec((HEADS, rr, DN_CHUNK), lambda i: (0, ng - 1 - i, 0))
    gl_spec = pl.BlockSpec((nc, HEADS * LANE), rev)
    wide = jax.ShapeDtypeStruct((t, HEADS * LANE), F32)
    outs = pl.pallas_call(
        body, name="delta_scan_bwd", grid=(ng,),
        in_specs=[row] * 4 + [a_spec, gl_spec, pl.BlockSpec((nc, HEADS, LANE, LANE), lambda i: (ng - 1 - i, 0, 0, 0)), row],
        out_specs=[row] * 4 + [a_spec, pl.BlockSpec((nc * 8, HEADS * LANE), rev)],
        out_shape=[wide] * 4 + [jax.ShapeDtypeStruct((HEADS, t, DN_CHUNK), F32),
                                jax.ShapeDtypeStruct((t // DN_CHUNK * 8, HEADS * LANE), F32)],
        scratch_shapes=[pltpu.VMEM((HEADS, LANE, LANE), F32)],
        compiler_params=_cparams(("arbitrary",)))(u, w, qd, kt, intra, gl, sall, do)
    return tuple(outs[:5]) + (outs[5].reshape(t // DN_CHUNK, 8, HEADS * LANE)[:, 0, :],)


_ATT_TILE = 512


def _kv_rows(j, tk):
    return pl.ds(pl.multiple_of(j * tk, tk), tk)


def _att_scores(ql, qr, ckv_ref, kr_ref, j, tk):
    ks = _kv_rows(j, tk)
    return (_dot(ql, ckv_ref[ks, :], _NT) + _dot(qr, kr_ref[ks, :], _NT)) * ATT_SCALE


def _diag_mask(s):
    qi = lax.broadcasted_iota(jnp.int32, s.shape, 0)
    ki = lax.broadcasted_iota(jnp.int32, s.shape, 1)
    return jnp.where(ki <= qi, s, NEG_BIG)


def _attention(ql, qr, ckv, kr):
    t = ckv.shape[0]
    tq = min(_ATT_TILE, t)

    def body(ql_ref, qr_ref, ckv_ref, kr_ref, o_ref, lse_ref, m_scr, l_scr, acc_scr, s_scr):
        qi = pl.program_id(1)
        q_lat, q_rope = ql_ref[...], qr_ref[...]
        m_scr[...] = jnp.full_like(m_scr, NEG_BIG)
        l_scr[...] = jnp.zeros_like(l_scr)
        acc_scr[...] = jnp.zeros_like(acc_scr)

        def consume(j, s):
            m_old = m_scr[...]
            m_new = jnp.maximum(m_old, jnp.max(s, axis=-1, keepdims=True))
            p = jnp.exp(s - m_new)
            alpha = jnp.exp(m_old - m_new)
            l_scr[...] = l_scr[...] * alpha + jnp.sum(p, axis=-1, keepdims=True)
            acc_scr[...] = acc_scr[...] * alpha + _dot(p, ckv_ref[_kv_rows(j, tq), :], _NN)
            m_scr[...] = m_new

        s_scr[0] = _att_scores(q_lat, q_rope, ckv_ref, kr_ref, 0, tq)

        def loop_body(j, carry):
            s_scr[(j + 1) % 2] = _att_scores(q_lat, q_rope, ckv_ref, kr_ref, j + 1, tq)
            consume(j, s_scr[j % 2])
            return carry

        lax.fori_loop(0, qi, loop_body, 0)
        consume(qi, _diag_mask(s_scr[qi % 2]))
        o_ref[...] = acc_scr[...] / l_scr[...]
        lse_ref[...] = m_scr[...] + jnp.log(l_scr[...])

    return pl.pallas_call(
        body, name="attention", grid=(HEADS, t // tq),
        in_specs=[pl.BlockSpec((None, tq, KV_LORA), lambda h, i: (h, i, 0)),
                  pl.BlockSpec((tq, ROPE_PAD), lambda h, i: (i, h)),
                  pl.BlockSpec((t, KV_LORA), lambda h, i: (0, 0)),
                  pl.BlockSpec((t, ROPE_PAD), lambda h, i: (0, 0))],
        out_specs=[pl.BlockSpec((None, tq, KV_LORA), lambda h, i: (h, i, 0)),
                   pl.BlockSpec((None, tq, 1), lambda h, i: (h, i, 0))],
        out_shape=[jax.ShapeDtypeStruct((HEADS, t, KV_LORA), F32), jax.ShapeDtypeStruct((HEADS, t, 1), F32)],
        scratch_shapes=[pltpu.VMEM((tq, 1), F32), pltpu.VMEM((tq, 1), F32), pltpu.VMEM((tq, KV_LORA), F32),
                        pltpu.VMEM((2, tq, tq), F32)],
        compiler_params=_cparams(("parallel", "parallel")))(ql, qr, ckv, kr)


def _attention_bwd(ql, qr, ckv, kr, out, lse, dout):
    t = ckv.shape[0]
    tq = min(_ATT_TILE, t)

    def body(ql_ref, qr_ref, ckv_ref, kr_ref, o_ref, lse_ref, do_ref, dql_ref, dqr_ref, dckv_ref, dkr_ref,
             dql_scr, dqr_scr):
        h, qi = pl.program_id(0), pl.program_id(1)

        @pl.when((h == 0) & (qi == 0))
        def _():
            dckv_ref[...] = jnp.zeros_like(dckv_ref)
            dkr_ref[...] = jnp.zeros_like(dkr_ref)

        q_lat, q_rope = ql_ref[...], qr_ref[...]
        d_o = do_ref[...].astype(_CDT)
        lse_v = lse_ref[...]
        dsum = jnp.sum(do_ref[...] * o_ref[...], axis=-1, keepdims=True)
        dql_scr[...] = jnp.zeros_like(dql_scr)
        dqr_scr[...] = jnp.zeros_like(dqr_scr)

        def step(j, masked):
            ks = _kv_rows(j, tq)
            s = _att_scores(q_lat, q_rope, ckv_ref, kr_ref, j, tq)
            if masked:
                s = _diag_mask(s)
            p = jnp.exp(s - lse_v)
            kv = ckv_ref[ks, :]
            ds = (p * (_dot(d_o, kv, _NT) - dsum) * ATT_SCALE).astype(_CDT)
            pb = p.astype(_CDT)
            dql_scr[...] += _dot(ds, kv, _NN)
            dqr_scr[...] += _dot(ds, kr_ref[ks, :], _NN)
            dckv_ref[ks, :] += _dot(pb, d_o, _TN) + _dot(ds, q_lat, _TN)
            dkr_ref[ks, :] += _dot(ds, q_rope, _TN)

        def loop_body(j, carry):
            step(j, False)
            return carry

        lax.fori_loop(0, qi, loop_body, 0)
        step(qi, True)
        dql_ref[...] = dql_scr[...].astype(dql_ref.dtype)
        dqr_ref[...] = dqr_scr[...]

    lat = pl.BlockSpec((None, tq, KV_LORA), lambda h, i: (h, i, 0))
    rope = pl.BlockSpec((tq, ROPE_PAD), lambda h, i: (i, h))
    kfull = pl.BlockSpec((t, KV_LORA), lambda h, i: (0, 0))
    rfull = pl.BlockSpec((t, ROPE_PAD), lambda h, i: (0, 0))
    return pl.pallas_call(
        body, name="attention_bwd", grid=(HEADS, t // tq),
        in_specs=[lat, rope, kfull, rfull, lat, pl.BlockSpec((None, tq, 1), lambda h, i: (h, i, 0)), lat],
        out_specs=[lat, rope, kfull, rfull],
        out_shape=[jax.ShapeDtypeStruct((HEADS, t, KV_LORA), _CDT), jax.ShapeDtypeStruct((t, HEADS * ROPE_PAD), F32),
                   jax.ShapeDtypeStruct((t, KV_LORA), F32), jax.ShapeDtypeStruct((t, ROPE_PAD), F32)],
        scratch_shapes=[pltpu.VMEM((tq, KV_LORA), F32), pltpu.VMEM((tq, ROPE_PAD), F32)],
        compiler_params=_cparams(("arbitrary", "arbitrary")))(ql, qr, ckv, kr, out, lse, dout)


def _gated_norm(o, z, w):
    return _rms_norm(o, w) * _silu(z)


def _mla_pre(ckv, krp, cq, cosb, sinb, qw, kw):
    return _rms_norm(cq, qw), _rms_norm(ckv, kw), _rope(krp, cosb, sinb)


def _merge(gg, y_dn, y_mla):
    return _sigmoid(gg[:, :D_MODEL]) * y_dn + _sigmoid(gg[:, D_MODEL:]) * y_mla


def _ln1(xv, attn_out, g, b):
    return _layer_norm(ALPHA * xv + attn_out, g, b)


def _final(h1, ffn, gate_pre, ple_proj, g, b):
    return _layer_norm(ALPHA * h1 + ffn + _sigmoid(gate_pre) * ple_proj, g, b)


def _swiglu(gt, up):
    return _silu(gt) * up


def _local_step(x, p, cosb, sinb, target, wt, sp, exch):
    t = x.shape[0]
    bf = _CDT
    xb = x.astype(bf)
    g = {}

    qkv_pre = _mm2(xb, wt['qkv'], name="f_qkv")
    z = _mm2(xb, wt['z'], name="f_z")
    gg = _mm2(xb, wt['gg'], name="f_gg")
    pm = _mm2(xb, wt['mla'], name="f_mla")
    qkv_act = _conv_silu(qkv_pre, sp['conv_w'])
    (u, w_, qd, kt, intra, gl, t_inv), sent = _delta_local(qkv_act, pm, sp['a_log'], sp['dt_bias'], rider=exch.gather_send())
    (o_dn, sall), passed = _delta_scan(u, w_, qd, kt, intra, gl, rider=exch.gather_pass(sent))
    wt = dict(wt, **exch.weights(passed))
    (og,) = _rowwise(lambda h, o, zz, w: (_gated_norm(o, zz, w),),
                     [(o_dn, LANE, 0, True), (z, LANE, 0, True)], [sp['dn_norm_w']],
                     [(D_MODEL, LANE, True, bf)], name="f_gated_norm", tm=512, heads=HEADS)
    y_dn = _mm2(og, wt['br_dn'], name="f_br_dn")

    c_q, c_kv, k_rope = _rowwise(
        lambda h, *a: _mla_pre(*a),
        [(pm, KV_LORA, 0, False), (pm, ROPE_PAD, 2, False), (pm, Q_LORA, 2, False),
         (cosb, ROPE_PAD, 0, False), (sinb, ROPE_PAD, 0, False)],
        [sp['q_norm_w'], sp['kv_norm_w']],
        [(Q_LORA, Q_LORA, False, bf), (KV_LORA, KV_LORA, False, bf), (ROPE_PAD, ROPE_PAD, False, bf)], name="f_mla_pre")
    q_nope = _mm2(c_q, wt['uq_nope'], name="f_uq_nope", out_dtype=bf)
    q_rope_pre = _mm2(c_q, wt['uq_rope'], name="f_uq_rope")
    (q_rope,) = _rowwise(lambda h, q, c, s: (_rope(q, c, s),),
                         [(q_rope_pre, HEADS * ROPE_PAD, 0, False), (cosb, ROPE_PAD, 0, False), (sinb, ROPE_PAD, 0, False)],
                         [], [(HEADS * ROPE_PAD, HEADS * ROPE_PAD, False, bf)], name="f_q_rope")
    q_lat = _mm(q_nope, wt['uk'], name="f_q_lat", tb=True, heads=HEADS, a_head='col', b_head='lead', out_head='lead',
                dims=(t, KV_LORA, NOPE), out_dtype=bf)
    out_lat, lse = _attention(q_lat, q_rope, c_kv, k_rope)
    o_mla = _mm(out_lat, wt['uv'], name="f_o_mla", heads=HEADS, a_head='lead', b_head='lead', out_head='col',
                dims=(t, NOPE, KV_LORA), out_dtype=bf)
    y_mla = _mm2(o_mla, wt['br_mla'], name="f_br_mla")

    (mixed,) = _rowwise(lambda h, *a: (_merge(*a),),
                        [(gg, 2 * D_MODEL, 0, False), (y_dn, D_MODEL, 0, False), (y_mla, D_MODEL, 0, False)],
                        [], [(D_MODEL, D_MODEL, False, bf)], name="f_merge")
    attn_out = _mm2(mixed, wt['o'], name="f_o")
    h1, h1b = _rowwise(lambda h, *a: (_ln1(*a),) * 2, [(x, D_MODEL, 0, False), (attn_out, D_MODEL, 0, False)],
                       [sp['ln1_g'], sp['ln1_b']], [(D_MODEL, D_MODEL, False, F32), (D_MODEL, D_MODEL, False, bf)],
                       name="f_ln1")
    ffn_in = _mm2(h1b, wt['ffn_in'], name="f_ffn_in")
    (act,) = _rowwise(lambda h, gt, up: (_swiglu(gt, up),),
                      [(ffn_in, FFN_HIDDEN, 0, False), (ffn_in, FFN_HIDDEN, 1, False)], [],
                      [(FFN_HIDDEN, FFN_HIDDEN, False, bf)], name="f_swiglu")
    ffn = _mm2(act, wt['ffn_out'], name="f_ffn_out")
    gate_pre = _mm2(h1b, wt['ple_gate'], name="f_ple_gate")
    pb = p.astype(bf)
    ple_proj = _mm2(pb, wt['ple'], name="f_ple")

    def final_fn(h, h1v, ffnv, gpv, ppv, tgt, gv, bv):
        y, vjp = jax.vjp(_final, h1v, ffnv, gpv, ppv, gv, bv)
        err = y - tgt
        dh1, dffn, dgp, dpp, dg, db = vjp(err * (1.0 / D_MODEL))
        sq = err * err
        lanes = sq[:, :LANE]
        for j in range(1, D_MODEL // LANE):
            lanes = lanes + sq[:, j * LANE:(j + 1) * LANE]
        loss = jnp.sum(lanes, axis=0, keepdims=True) * (0.5 / D_MODEL)
        return dffn, dffn, dgp, dpp, dg, db, loss

    dpre2, dpre2b, dgate_pre, dple_proj, g['ln2_g'], g['ln2_b'], loss_lanes = _rowwise(
        final_fn, [(a, D_MODEL, 0, False) for a in (h1, ffn, gate_pre, ple_proj, target)],
        [sp['ln2_g'], sp['ln2_b']],
        [(D_MODEL, D_MODEL, False, F32)] + [(D_MODEL, D_MODEL, False, bf)] * 3,
        [(1, D_MODEL), (1, D_MODEL), (1, LANE)], name="b_final")

    g['ple'] = _mm2(pb, dple_proj, ta=True, name="g_ple")
    g['ple_gate'] = _mm2(h1b, dgate_pre, ta=True, name="g_ple_gate")
    g['ffn_out'] = _mm2(act, dpre2b, ta=True, name="g_ffn_out")
    dact = _mm2(dpre2b, wt['ffn_out'], tb=True, name="b_dact")

    def swiglu_bwd(h, gt, up, d):
        _, vjp = jax.vjp(_swiglu, gt, up)
        dgt, dup = vjp(d)
        return (jnp.concatenate([dgt, dup], axis=1),)

    (dffn_in,) = _rowwise(swiglu_bwd, [(ffn_in, FFN_HIDDEN, 0, False), (ffn_in, FFN_HIDDEN, 1, False),
                                       (dact, FFN_HIDDEN, 0, False)], [],
                          [(2 * FFN_HIDDEN, 2 * FFN_HIDDEN, False, bf)], name="b_swiglu")
    g['ffn_in'] = _mm2(h1b, dffn_in, ta=True, name="g_ffn_in")
    dh1 = _mm2(dffn_in, wt['ffn_in'], tb=True, name="b_dh1_ffn", add=dpre2, add_scale=ALPHA)
    dh1 = _mm2(dgate_pre, wt['ple_gate'], tb=True, name="b_dh1_gate", add=dh1)

    def ln1_bwd(h, xv, ao, d, gv, bv):
        _, vjp = jax.vjp(_ln1, xv, ao, gv, bv)
        _, dao, dg, db = vjp(d)
        return dao, dao, dg, db

    dpre1, dpre1b, g['ln1_g'], g['ln1_b'] = _rowwise(
        ln1_bwd, [(x, D_MODEL, 0, False), (attn_out, D_MODEL, 0, False), (dh1, D_MODEL, 0, False)],
        [sp['ln1_g'], sp['ln1_b']], [(D_MODEL, D_MODEL, False, F32), (D_MODEL, D_MODEL, False, bf)],
        [(1, D_MODEL), (1, D_MODEL)], name="b_ln1")

    g['o'] = _mm2(mixed, dpre1b, ta=True, name="g_o")
    dmixed = _mm2(dpre1b, wt['o'], tb=True, name="b_dmixed")

    def merge_bwd(h, ggv, yd, ym, d):
        _, vjp = jax.vjp(_merge, ggv, yd, ym)
        return vjp(d)

    dgg, dy_dn, dy_mla = _rowwise(
        merge_bwd, [(gg, 2 * D_MODEL, 0, False), (y_dn, D_MODEL, 0, False), (y_mla, D_MODEL, 0, False),
                    (dmixed, D_MODEL, 0, False)], [],
        [(2 * D_MODEL, 2 * D_MODEL, False, bf), (D_MODEL, D_MODEL, False, bf), (D_MODEL, D_MODEL, False, bf)],
        name="b_merge")
    g['br_dn'] = _mm2(og, dy_dn, ta=True, name="g_br_dn")
    dog = _mm2(dy_dn, wt['br_dn'], tb=True, name="b_dog")
    g['br_mla'] = _mm2(o_mla, dy_mla, ta=True, name="g_br_mla")
    do_mla = _mm2(dy_mla, wt['br_mla'], tb=True, name="b_do_mla", out_dtype=bf)

    dout_lat = _mm(do_mla, wt['uv'], name="b_dout_lat", tb=True, heads=HEADS, a_head='col', b_head='lead',
                   out_head='lead', dims=(t, KV_LORA, NOPE))
    g['uv'] = _mm(out_lat, do_mla, name="g_uv", ta=True, heads=HEADS, a_head='lead', b_head='col', out_head='lead',
                  dims=(KV_LORA, NOPE, t))
    dq_lat, dq_rope, dckv_att, dkr_att = _attention_bwd(q_lat, q_rope, c_kv, k_rope, out_lat, lse, dout_lat)
    dq_nope = _mm(dq_lat, wt['uk'], name="b_dq_nope", heads=HEADS, a_head='lead', b_head='lead', out_head='col',
                  dims=(t, NOPE, KV_LORA), out_dtype=bf)
    g['uk'] = _mm(dq_lat, q_nope, name="g_uk", ta=True, heads=HEADS, a_head='lead', b_head='col', out_head='lead',
                  dims=(KV_LORA, NOPE, t))
    (dq_rope_pre,) = _rowwise(lambda h, d, c, s: (_rope_bwd(d, c, s),),
                              [(dq_rope, HEADS * ROPE_PAD, 0, False), (cosb, ROPE_PAD, 0, False), (sinb, ROPE_PAD, 0, False)],
                              [], [(HEADS * ROPE_PAD, HEADS * ROPE_PAD, False, bf)], name="b_q_rope")
    g['uq_nope'] = _mm2(c_q, dq_nope, ta=True, name="g_uq_nope")
    g['uq_rope'] = _mm2(c_q, dq_rope_pre, ta=True, name="g_uq_rope")
    dc_q = _mm2(dq_nope, wt['uq_nope'], tb=True, name="b_dcq_nope")
    dc_q = _mm2(dq_rope_pre, wt['uq_rope'], tb=True, name="b_dcq_rope", add=dc_q)

    def gated_norm_bwd(h, o, zz, d, w):
        _, vjp = jax.vjp(_gated_norm, o, zz, w)
        return vjp(d)

    do_dn, dz, g['dn_norm_w'] = _rowwise(
        gated_norm_bwd, [(o_dn, LANE, 0, True), (z, LANE, 0, True), (dog, LANE, 0, True)], [sp['dn_norm_w']],
        [(D_MODEL, LANE, True, F32), (D_MODEL, LANE, True, bf)], [(1, LANE)], name="b_gated_norm", tm=512, heads=HEADS)
    du, dw, dqd, dkt, dintra, dgl = _delta_scan_bwd(u, w_, qd, kt, intra, gl, sall, do_dn)
    (dq_a, dk_a, dv_a, dba, g['a_log'], g['dt_bias']), arrived = _delta_local_bwd(
        qkv_act, pm, sp['a_log'], sp['dt_bias'], t_inv, du, dw, dqd, dkt, dintra, dgl, rider=exch.reduce_send(g))
    exch.reduce_arrived(arrived)
    dqkv_act = jnp.concatenate([dq_a, dk_a, dv_a], axis=1)
    dqkv_pre, g['conv_w'] = _conv_silu_bwd(qkv_pre, sp['conv_w'], dqkv_act)

    def mla_pre_bwd(h, ckv, krp, cq, cosv, sinv, dcq, dckv, dkr, qw, kw):
        _, vjp = jax.vjp(lambda a, b, c, d, e: (_rms_norm(c, d), _rms_norm(a, e)), ckv, krp, cq, qw, kw)
        dckv_p, _, dcq_p, dqw, dkw = vjp((dcq, dckv))
        dkr_p = _rope_bwd(dkr, cosv, sinv)
        dpm = jnp.concatenate([dckv_p, dkr_p, jnp.zeros((ckv.shape[0], 3 * LANE), F32), dcq_p], axis=1)
        return dpm, dqw, dkw

    dpm_main, g['q_norm_w'], g['kv_norm_w'] = _rowwise(
        mla_pre_bwd,
        [(pm, KV_LORA, 0, False), (pm, ROPE_PAD, 2, False), (pm, Q_LORA, 2, False),
         (cosb, ROPE_PAD, 0, False), (sinb, ROPE_PAD, 0, False),
         (dc_q, Q_LORA, 0, False), (dckv_att, KV_LORA, 0, False), (dkr_att, ROPE_PAD, 0, False)],
        [sp['q_norm_w'], sp['kv_norm_w']], [(1152, 1152, False, F32)], [(1, Q_LORA), (1, KV_LORA)], name="b_mla_pre")
    (dpm,) = _rowwise(
        lambda h, a, b: (jnp.concatenate([a[:, :3 * LANE], b, a[:, 4 * LANE:]], axis=1),),
        [(dpm_main, 1152, 0, False), (dba, LANE, 0, False)], [], [(1152, 1152, False, bf)], name="b_dpm")

    g['qkv'] = _mm2(xb, dqkv_pre, ta=True, name="g_qkv")
    g['z'] = _mm2(xb, dz, ta=True, name="g_z")
    g['gg'] = _mm2(xb, dgg, ta=True, name="g_gg")
    g['mla'] = _mm2(xb, dpm, ta=True, name="g_mla")
    dx = _mm2(dqkv_pre, wt['qkv'], tb=True, name="b_dx_qkv", add=dpre1, add_scale=ALPHA)
    dx = _mm2(dz, wt['z'], tb=True, name="b_dx_z", add=dx)
    dx = _mm2(dgg, wt['gg'], tb=True, name="b_dx_gg", add=dx)
    dx = _mm2(dpm, wt['mla'], tb=True, name="b_dx_mla", add=dx)
    return loss_lanes, dx, g


_IN_SIZES = (QKV_W, HEADS * DN_DK, HEADS, HEADS, Q_LORA, KV_LORA, ROPE, D_MODEL, D_MODEL)


def _rope_tables(positions):
    inv_freq = ROPE_BASE ** (-jnp.arange(0, ROPE, 2, dtype=F32) / ROPE)
    ang = positions.astype(F32)[:, None] * inv_freq
    cos, sin = jnp.cos(ang), jnp.sin(ang)
    zeros = jnp.zeros((positions.shape[0], ROPE_PAD - ROPE), F32)
    return jnp.concatenate([cos, cos, zeros], axis=1), jnp.concatenate([-sin, sin, zeros], axis=1)


def _prep_w_in(w_in):
    dt = w_in.dtype
    offs = [0]
    for s in _IN_SIZES:
        offs.append(offs[-1] + s)
    qkv, z, wb, wa, cq, ckv, kr, gd, gm = [w_in[:, offs[i]:offs[i + 1]] for i in range(len(_IN_SIZES))]
    zc = lambda n: jnp.zeros((D_MODEL, n), dt)
    return {
        'qkv': qkv, 'z': z, 'gg': jnp.concatenate([gd, gm], axis=1),
        'mla': jnp.concatenate([ckv, kr, zc(ROPE_PAD - ROPE), wb, wa, zc(LANE - 2 * HEADS), zc(2 * LANE), cq], axis=1),
    }


def _prep_weights(full):
    w_uq = full['w_uq']
    wt = {
        'uq_nope': w_uq[:, :, :NOPE].reshape(Q_LORA, HEADS * NOPE),
        'uq_rope': jnp.pad(w_uq[:, :, NOPE:], ((0, 0), (0, 0), (0, ROPE_PAD - ROPE))).reshape(Q_LORA, HEADS * ROPE_PAD),
        'uk': jnp.transpose(full['w_uk'], (1, 0, 2)), 'uv': jnp.transpose(full['w_uv'], (1, 0, 2)),
        'br_dn': full['w_br_dn'], 'br_mla': full['w_br_mla'], 'o': full['w_o'], 'ffn_in': full['w_ffn_in'],
        'ffn_out': full['w_ffn_out'], 'ple': full['w_ple'], 'ple_gate': full['w_ple_gate'],
    }
    return wt


def _prep_small(small):
    pad = lambda v: jnp.pad(v, (0, LANE - v.shape[0]))[None, :]
    return {
        'conv_w': small['conv_w'], 'a_log': pad(small['dn_a_log']), 'dt_bias': pad(small['dn_dt_bias']),
        'dn_norm_w': small['dn_norm_w'][None, :], 'q_norm_w': small['q_norm_w'][None, :],
        'kv_norm_w': small['kv_norm_w'][None, :], 'ln1_g': small['ln1_g'][None, :], 'ln1_b': small['ln1_b'][None, :],
        'ln2_g': small['ln2_g'][None, :], 'ln2_b': small['ln2_b'][None, :],
    }


def _unprep_grads_late(g):
    mla = g['mla']
    ba0 = KV_LORA + ROPE_PAD
    cq0 = ba0 + 3 * LANE
    w_in = jnp.concatenate([
        g['qkv'], g['z'], mla[:, ba0:ba0 + HEADS], mla[:, ba0 + HEADS:ba0 + 2 * HEADS], mla[:, cq0:cq0 + Q_LORA],
        mla[:, :KV_LORA], mla[:, KV_LORA:KV_LORA + ROPE], g['gg']], axis=1)
    return {
        'w_in': w_in, 'conv_w': g['conv_w'], 'dn_a_log': g['a_log'][0, :HEADS], 'dn_dt_bias': g['dt_bias'][0, :HEADS],
        'dn_norm_w': g['dn_norm_w'][0], 'q_norm_w': g['q_norm_w'][0], 'kv_norm_w': g['kv_norm_w'][0],
        'ln1_g': g['ln1_g'][0], 'ln1_b': g['ln1_b'][0], 'ln2_g': g['ln2_g'][0], 'ln2_b': g['ln2_b'][0],
    }


def _unprep_grads_early(g):
    w_uq = jnp.concatenate([g['uq_nope'].reshape(Q_LORA, HEADS, NOPE),
                            g['uq_rope'].reshape(Q_LORA, HEADS, ROPE_PAD)[:, :, :ROPE]], axis=2)
    return {
        'w_uq': w_uq, 'w_uk': jnp.transpose(g['uk'], (1, 0, 2)), 'w_uv': jnp.transpose(g['uv'], (1, 0, 2)),
        'w_br_dn': g['br_dn'], 'w_br_mla': g['br_mla'], 'w_o': g['o'],
        'w_ffn_in': g['ffn_in'], 'w_ffn_out': g['ffn_out'], 'w_ple': g['ple'], 'w_ple_gate': g['ple_gate'],
    }


_FLATB_PIECES = (
    ('w_ffn_out', 704, (704, D_MODEL)), ('w_br_dn', 256, (256, D_MODEL)), ('w_br_mla', 256, (256, D_MODEL)),
    ('w_o', 256, (256, D_MODEL)), ('w_ple_gate', 256, (256, D_MODEL)), ('w_uq', 144, (96, HEADS, NOPE + ROPE)),
    ('w_uk', 64, (64, HEADS, NOPE)), ('w_uv', 64, (64, HEADS, NOPE)), ('w_ple', 64, (PLE_DIM, 256)),
)
FLATB_ROWS = 2112
W_IN_SHARD = D_IN // N_SHARD
FFN_IN_SHARD = 2 * FFN_HIDDEN // N_SHARD
A_ROWS = D_MODEL + 32
_CONV_SHARD = QKV_W // N_SHARD
_ADD_TILES = (256, 256, 352)


def _flatb_offsets():
    offs, o = {}, 0
    for name, rows, _ in _FLATB_PIECES:
        offs[name] = o
        o += rows
    return offs, o


def _pack_shards(ws, conv_w):
    conv_bits = lax.bitcast_convert_type(conv_w, jnp.bfloat16).reshape(DN_CONV, 2 * _CONV_SHARD).astype(_CDT)
    tail = jnp.pad(conv_bits, ((0, A_ROWS - D_MODEL - DN_CONV), (0, W_IN_SHARD - 2 * _CONV_SHARD)))
    a_buf = jnp.concatenate([ws['w_in'].astype(_CDT), tail], axis=0)
    parts = [ws[name].astype(_CDT).reshape(rows, FLAT_W) for name, rows, _ in _FLATB_PIECES]
    used = sum(p.shape[0] for p in parts)
    parts.append(jnp.zeros((FLATB_ROWS - used, FLAT_W), _CDT))
    return [a_buf, ws['w_ffn_in'].astype(_CDT), jnp.concatenate(parts, axis=0)]


def _unpack_w_in(gathered, local, me):
    a = [jnp.where(me == s, local, gathered[s]) for s in range(N_SHARD)]
    conv = [lax.bitcast_convert_type(
        p[D_MODEL:D_MODEL + DN_CONV, :2 * _CONV_SHARD].astype(jnp.bfloat16).reshape(DN_CONV, _CONV_SHARD, 2), F32) for p in a]
    return jnp.concatenate([p[:D_MODEL] for p in a], axis=1), jnp.concatenate(conv, axis=1)


def _unpack_rest(gathered, local, me):
    pick = lambda b, s: jnp.where(me == s, local[b], gathered[b][s])
    full = {'w_ffn_in': jnp.concatenate([pick(0, s) for s in range(N_SHARD)], axis=1)}
    offs, _ = _flatb_offsets()
    fb = [pick(1, s) for s in range(N_SHARD)]
    for name, rows, shape in _FLATB_PIECES:
        pieces = [p[offs[name]:offs[name] + rows].reshape(shape) for p in fb]
        full[name] = jnp.concatenate(pieces, axis=1 if name == 'w_ple' else 0)
    return full


def _shard_columns(g, w):
    return jnp.stack([g[:, s * w:(s + 1) * w] for s in range(N_SHARD)])


def _pack_grads_rest(gw):
    parts = []
    for name, rows, _ in _FLATB_PIECES:
        g = gw[name]
        if name == 'w_ple':
            parts.append(_shard_columns(g, PLE_DIM).reshape(N_SHARD, rows, FLAT_W))
        else:
            parts.append(g.reshape(N_SHARD, rows, FLAT_W))
    used = sum(p.shape[1] for p in parts)
    parts.append(jnp.zeros((N_SHARD, FLATB_ROWS - used, FLAT_W), F32))
    return [_shard_columns(gw['w_ffn_in'], FFN_IN_SHARD), jnp.concatenate(parts, axis=1)]


def _unpack_reduced(mine, theirs, c):
    whole = [jnp.concatenate([jnp.where(c == 0, m, t), jnp.where(c == 0, t, m)], axis=0) for m, t in zip(mine, theirs)]
    out = {'w_in': whole[0], 'w_ffn_in': whole[1]}
    offs, _ = _flatb_offsets()
    for name, rows, shape in _FLATB_PIECES:
        out[name] = whole[2][offs[name]:offs[name] + rows].reshape(shape)
    return out


_HBM = pl.BlockSpec(memory_space=pltpu.HBM)


def _place():
    x, y, c = lax.axis_index("x"), lax.axis_index("y"), lax.axis_index("c")
    chips = [(1 - x, y), (x, 1 - y), (1 - x, 1 - y)]
    return x, y, c, chips


def _remote(src, dst, send_sems, recv_sems, k, to):
    return pltpu.make_async_remote_copy(src_ref=src, dst_ref=dst, send_sem=send_sems.at[k], recv_sem=recv_sems.at[k],
                                        device_id=to, device_id_type=_MESH)


def _half_rows(ref, half, hf, lead=None):
    rows = pl.ds(pl.multiple_of(hf * half, 16), half)
    return ref.at[rows, :] if lead is None else ref.at[lead, rows, :]


class _Rider:
    def __init__(self, inputs, out_shape, n_sems, copies, aliases=None):
        self.inputs, self.out_shape, self.n_sems, self.copies = list(inputs), list(out_shape), n_sems, copies
        self.aliases = aliases or {}


def _carried_call(body, rider, first, last, *, name, grid, in_specs, out_specs, out_shape, scratch_shapes, sem, args):
    n_in, n_out, n_scr = len(in_specs), len(out_specs), len(scratch_shapes)
    if rider is None:
        res = pl.pallas_call(body, name=name, grid=grid, in_specs=in_specs, out_specs=out_specs, out_shape=out_shape,
                             scratch_shapes=scratch_shapes, compiler_params=_cparams(sem))(*args)
        return list(res), []
    ri, ro = len(rider.inputs), len(rider.out_shape)

    def full_body(*refs):
        own_in, r_in = refs[:n_in], refs[n_in:n_in + ri]
        o0 = n_in + ri
        own_out, r_out = refs[o0:o0 + n_out], refs[o0 + n_out:o0 + n_out + ro]
        s0 = o0 + n_out + ro
        own_scr, send_sems, recv_sems = refs[s0:s0 + n_scr], refs[s0 + n_scr], refs[s0 + n_scr + 1]

        @pl.when(first())
        def _():
            sends, _ = rider.copies(r_in, r_out, send_sems, recv_sems)
            for cp in sends:
                cp.start()

        body(*own_in, *own_out, *own_scr)

        @pl.when(last())
        def _():
            sends, arrivals = rider.copies(r_in, r_out, send_sems, recv_sems)
            for cp in arrivals():
                cp.wait_recv()
            for cp in sends:
                cp.wait_send()

    res = pl.pallas_call(
        full_body, name=name, grid=grid, in_specs=list(in_specs) + [_HBM] * ri, out_specs=list(out_specs) + [_HBM] * ro,
        out_shape=list(out_shape) + rider.out_shape,
        scratch_shapes=list(scratch_shapes) + [pltpu.SemaphoreType.DMA((rider.n_sems,))] * 2,
        input_output_aliases={n_in + i: n_out + o for i, o in rider.aliases.items()},
        compiler_params=_cparams(sem))(*args, *rider.inputs)
    return list(res[:n_out]), list(res[n_out:])


def _ride_gather_send(bufs):
    n = len(bufs)
    halves = [b.shape[0] // 2 for b in bufs]

    def copies(ins, outs, send_sems, recv_sems):
        x, y, c, chips = _place()
        slot = lambda b, cx, cy: _half_rows(outs[b], halves[b], c, lead=2 * cx + cy)
        sends = [_remote(_half_rows(ins[b], halves[b], c), slot(b, x, y), send_sems, recv_sems, 3 * b + j, (cx, cy, c))
                 for b in range(n) for j, (cx, cy) in enumerate(chips)]
        arrivals = lambda: [_remote(slot(b, cx, cy), slot(b, cx, cy), send_sems, recv_sems, 3 * b + j, (x, y, c))
                            for b in range(n) for j, (cx, cy) in enumerate(chips)]
        return sends, arrivals

    return _Rider(bufs, [jax.ShapeDtypeStruct((N_SHARD,) + b.shape, b.dtype) for b in bufs], 3 * n, copies)


def _ride_gather_pass(gathered):
    n = len(gathered)
    halves = [g.shape[1] // 2 for g in gathered]

    def copies(ins, outs, send_sems, recv_sems):
        x, y, c, chips = _place()
        slot = lambda b, cx, cy, hf: _half_rows(outs[b], halves[b], hf, lead=2 * cx + cy)
        sends = [_remote(slot(b, cx, cy, c), slot(b, cx, cy, c), send_sems, recv_sems, 3 * b + j, (x, y, 1 - c))
                 for b in range(n) for j, (cx, cy) in enumerate(chips)]
        arrivals = lambda: [_remote(slot(b, cx, cy, 1 - c), slot(b, cx, cy, 1 - c), send_sems, recv_sems, 3 * b + j, (x, y, c))
                            for b in range(n) for j, (cx, cy) in enumerate(chips)]
        return sends, arrivals

    return _Rider(gathered, [jax.ShapeDtypeStruct(g.shape, g.dtype) for g in gathered], 3 * n, copies,
                  aliases={b: b for b in range(n)})


def _ride_chip_exchange(parts):
    n = len(parts)

    def copies(ins, outs, send_sems, recv_sems):
        x, y, c, chips = _place()
        sends = [_remote(ins[b].at[2 * cx + cy], outs[b].at[j], send_sems, recv_sems, 3 * b + j, (cx, cy, c))
                 for b in range(n) for j, (cx, cy) in enumerate(chips)]
        arrivals = lambda: [_remote(ins[b].at[0], outs[b].at[j], send_sems, recv_sems, 3 * b + j, (x, y, c))
                            for b in range(n) for j in range(len(chips))]
        return sends, arrivals

    return _Rider(parts, [jax.ShapeDtypeStruct((3,) + p.shape[1:], p.dtype) for p in parts], 3 * n, copies)


def _gather_shards(bufs, name):
    n = len(bufs)
    halves = [b.shape[0] // 2 for b in bufs]

    def body(*refs):
        ins, outs, send_sems, recv_sems = refs[:n], refs[n:2 * n], refs[2 * n], refs[2 * n + 1]
        x, y, c, chips = _place()
        me, sibling = (x, y, c), (x, y, 1 - c)
        slot = lambda b, cx, cy, hf: _half_rows(outs[b], halves[b], hf, lead=2 * cx + cy)
        first = [_remote(_half_rows(ins[b], halves[b], c), slot(b, x, y, c), send_sems, recv_sems, 6 * b + j, (cx, cy, c))
                 for b in range(n) for j, (cx, cy) in enumerate(chips)]
        for cp in first:
            cp.start()
        passed = []
        for j, (cx, cy) in enumerate(chips):
            for b in range(n):
                _remote(slot(b, cx, cy, c), slot(b, cx, cy, c), send_sems, recv_sems, 6 * b + j, me).wait_recv()
                fwd = _remote(slot(b, cx, cy, c), slot(b, cx, cy, c), send_sems, recv_sems, 6 * b + 3 + j, sibling)
                fwd.start()
                passed.append(fwd)
        for j, (cx, cy) in enumerate(chips):
            for b in range(n):
                _remote(slot(b, cx, cy, 1 - c), slot(b, cx, cy, 1 - c), send_sems, recv_sems, 6 * b + 3 + j, me).wait_recv()
        for cp in first + passed:
            cp.wait_send()

    return pl.pallas_call(
        body, name=name, out_shape=[jax.ShapeDtypeStruct((N_SHARD,) + b.shape, b.dtype) for b in bufs],
        in_specs=[_HBM] * n, out_specs=[_HBM] * n,
        scratch_shapes=[pltpu.SemaphoreType.DMA((6 * n,)), pltpu.SemaphoreType.DMA((6 * n,))],
    )(*bufs)


def _reduce_pair_exchange(gbufs, name):
    n = len(gbufs)
    halves = [g.shape[1] // 2 for g in gbufs]

    def body(*refs):
        ins, outs, send_sems, recv_sems = refs[:n], refs[n:2 * n], refs[2 * n], refs[2 * n + 1]
        x, y, c, _ = _place()
        cps = [_remote(ins[b].at[:, pl.ds(pl.multiple_of((1 - c) * halves[b], 16), halves[b]), :], outs[b],
                       send_sems, recv_sems, b, (x, y, 1 - c)) for b in range(n)]
        for cp in cps:
            cp.start()
        for cp in cps:
            cp.wait()

    return pl.pallas_call(
        body, name=name,
        out_shape=[jax.ShapeDtypeStruct((N_SHARD, h, g.shape[2]), g.dtype) for g, h in zip(gbufs, halves)],
        in_specs=[_HBM] * n, out_specs=[_HBM] * n,
        scratch_shapes=[pltpu.SemaphoreType.DMA((n,)), pltpu.SemaphoreType.DMA((n,))],
    )(*gbufs)


def _pair_add(gbuf, recv, c_arr, tr, name):
    _, rows, width = gbuf.shape
    half = rows // 2
    nt = half // tr

    def body(c_ref, a_ref, b_ref, o_ref):
        o_ref[...] = (a_ref[...] + b_ref[...]).astype(o_ref.dtype)

    blk = lambda f: pl.BlockSpec((None, tr, width), f)
    return pl.pallas_call(
        body, name=name, out_shape=jax.ShapeDtypeStruct((N_SHARD, half, width), jnp.bfloat16),
        grid_spec=pltpu.PrefetchScalarGridSpec(
            num_scalar_prefetch=1, grid=(N_SHARD, nt),
            in_specs=[blk(lambda s, i, c: (s, c[0] * nt + i, 0)), blk(lambda s, i, c: (s, i, 0))],
            out_specs=blk(lambda s, i, c: (s, i, 0))),
        compiler_params=_cparams(("parallel", "parallel")))(c_arr, gbuf, recv)


def _reduce_chip_exchange(parts, name):
    n = len(parts)

    def body(*refs):
        ins, outs, send_sems, recv_sems = refs[:n], refs[n:2 * n], refs[2 * n], refs[2 * n + 1]
        x, y, c, chips = _place()
        sends = [_remote(ins[b].at[2 * cx + cy], outs[b].at[j], send_sems, recv_sems, 3 * b + j, (cx, cy, c))
                 for b in range(n) for j, (cx, cy) in enumerate(chips)]
        for cp in sends:
            cp.start()
        for b in range(n):
            for j in range(len(chips)):
                _remote(ins[b].at[0], outs[b].at[j], send_sems, recv_sems, 3 * b + j, (x, y, c)).wait_recv()
        for cp in sends:
            cp.wait_send()

    return pl.pallas_call(
        body, name=name, out_shape=[jax.ShapeDtypeStruct((3,) + p.shape[1:], p.dtype) for p in parts],
        in_specs=[_HBM] * n, out_specs=[_HBM] * n,
        scratch_shapes=[pltpu.SemaphoreType.DMA((3 * n,)), pltpu.SemaphoreType.DMA((3 * n,))],
    )(*parts)


def _chip_add(part, recv, me_arr, tr, name):
    _, half, width = part.shape

    def body(me_ref, own, a0, a1, a2, o_ref):
        f = lambda r: r[...].astype(F32)
        o_ref[...] = ((f(own) + f(a0)) + f(a1)) + f(a2)

    specs = [pl.BlockSpec((None, tr, width), lambda i, me: (me[0], i, 0))]
    specs += [pl.BlockSpec((None, tr, width), functools.partial(lambda i, me, k: (k, i, 0), k=k)) for k in range(3)]
    return pl.pallas_call(
        body, name=name, out_shape=jax.ShapeDtypeStruct((half, width), F32),
        grid_spec=pltpu.PrefetchScalarGridSpec(
            num_scalar_prefetch=1, grid=(half // tr,), in_specs=specs,
            out_specs=pl.BlockSpec((tr, width), lambda i, me: (i, 0))),
        compiler_params=_cparams(("parallel",)))(me_arr, part, recv, recv, recv)


def _reduce_pair_share(rhalves, name):
    n = len(rhalves)

    def body(*refs):
        ins, outs, send_sems, recv_sems = refs[:n], refs[n:2 * n], refs[2 * n], refs[2 * n + 1]
        x, y, c, _ = _place()
        cps = [_remote(ins[b], outs[b], send_sems, recv_sems, b, (x, y, 1 - c)) for b in range(n)]
        for cp in cps:
            cp.start()
        for cp in cps:
            cp.wait()

    return pl.pallas_call(
        body, name=name, out_shape=[jax.ShapeDtypeStruct(r.shape, r.dtype) for r in rhalves],
        in_specs=[_HBM] * n, out_specs=[_HBM] * n,
        scratch_shapes=[pltpu.SemaphoreType.DMA((n,)), pltpu.SemaphoreType.DMA((n,))],
    )(*rhalves)


def _small_allreduce(buf):
    r, width = buf.shape
    n_dev = 8

    def body(x_ref, all_ref, sum_ref, send_sems, recv_sems, local_sem):
        x, y, c, chips = _place()
        me, sibling = (x, y, c), (x, y, 1 - c)

        def rows(px, py, pc):
            return all_ref.at[pl.ds(pl.multiple_of((4 * px + 2 * py + pc) * r, 8), r), :]

        def copy(k, block, to, src=None):
            return _remote(rows(*block) if src is None else src, rows(*block), send_sems, recv_sems, k, to)

        mine = pltpu.make_async_copy(x_ref, rows(*me), local_sem)
        mine.start()
        first = [copy(0, me, sibling, src=x_ref)]
        first += [copy(1 + j, me, (*chip, c), src=x_ref) for j, chip in enumerate(chips)]
        for cp in first:
            cp.start()
        passed = [copy(4 + j, (*chip, c), sibling) for j, chip in enumerate(chips)]
        for j, chip in enumerate(chips):
            copy(1 + j, (*chip, c), me).wait_recv()
            passed[j].start()
        copy(0, sibling, me).wait_recv()
        for j, chip in enumerate(chips):
            copy(4 + j, (*chip, 1 - c), me).wait_recv()
        for cp in first + passed:
            cp.wait_send()
        mine.wait()
        total = all_ref[0:r, :]
        for k in range(1, n_dev):
            total = total + all_ref[k * r:(k + 1) * r, :]
        sum_ref[...] = total

    vm = pl.BlockSpec(memory_space=pltpu.VMEM)
    _, total = pl.pallas_call(
        body, name="small_allreduce",
        out_shape=[jax.ShapeDtypeStruct((n_dev * r, width), buf.dtype), jax.ShapeDtypeStruct((r, width), buf.dtype)],
        in_specs=[vm], out_specs=[vm, vm],
        scratch_shapes=[pltpu.SemaphoreType.DMA((7,)), pltpu.SemaphoreType.DMA((7,)), pltpu.SemaphoreType.DMA],
    )(buf)
    return total


def _row_tile(rows, cap):
    if rows <= cap:
        return rows
    t = (cap // 8) * 8
    while t >= 8:
        if rows % t == 0:
            return t
        t -= 8
    return rows


def _adamw(w, g, m, v, name):
    shape = w.shape
    cols = shape[-1] if len(shape) <= 3 else shape[-2] * shape[-1]
    w2, g2, m2, v2 = (a.reshape(-1, cols) for a in (w, g, m, v))
    rows = w2.shape[0]
    tr = _row_tile(rows, 256)

    def body(w_ref, g_ref, m_ref, v_ref, d_ref, mo_ref, vo_ref):
        gv = g_ref[...]
        mn = ADAM_B1 * m_ref[...] + (1.0 - ADAM_B1) * gv
        vn = ADAM_B2 * v_ref[...] + (1.0 - ADAM_B2) * (gv * gv)
        m_hat = mn / (1.0 - ADAM_B1 ** ADAM_STEP)
        v_hat = vn / (1.0 - ADAM_B2 ** ADAM_STEP)
        d_ref[...] = -ADAM_LR * (m_hat / (jnp.sqrt(v_hat) + ADAM_EPS) + ADAM_WD * w_ref[...])
        mo_ref[...] = mn
        vo_ref[...] = vn

    blk = pl.BlockSpec((tr, cols), lambda i: (i, 0))
    outs = pl.pallas_call(
        body, name=name, grid=(rows // tr,), in_specs=[blk] * 4, out_specs=[blk] * 3,
        out_shape=[jax.ShapeDtypeStruct((rows, cols), F32)] * 3, compiler_params=_cparams(("parallel",)))(w2, g2, m2, v2)
    return tuple(o.reshape(shape) for o in outs)


_WEIGHT_NAMES = ('w_in', 'conv_w', 'dn_a_log', 'dn_dt_bias', 'dn_norm_w', 'q_norm_w', 'w_uq', 'kv_norm_w', 'w_uk',
                 'w_uv', 'w_br_dn', 'w_br_mla', 'w_o', 'ln1_g', 'ln1_b', 'w_ffn_in', 'w_ffn_out', 'w_ple',
                 'w_ple_gate', 'ln2_g', 'ln2_b')
_SMALL_NAMES = ('ln1_g', 'ln1_b', 'ln2_g', 'ln2_b', 'q_norm_w', 'kv_norm_w', 'dn_norm_w', 'dn_a_log', 'dn_dt_bias')
_SMALL_GROUP = 8
_CONV_SMALL_ROW = len(_SMALL_NAMES) * _SMALL_GROUP
_CONV_SMALL_ROWS = DN_CONV * QKV_W // FLAT_W


def _pack_small(gw):
    rows = [jnp.pad(gw[n][None, :], ((0, _SMALL_GROUP - 1), (0, FLAT_W - gw[n].shape[0]))) for n in _SMALL_NAMES]
    rows.append(jnp.pad(gw['conv_w'].reshape(_CONV_SMALL_ROWS, FLAT_W), ((0, SMALL_ROWS - _CONV_SMALL_ROW - _CONV_SMALL_ROWS), (0, 0))))
    return jnp.concatenate(rows, axis=0)


class _Exchange:
    def __init__(self, local, me_chip, c_arr):
        self.local, self.me_chip, self.c_arr = local, me_chip, c_arr
        self.parts = self.arrived = None

    def gather_send(self):
        return _ride_gather_send(self.local)

    def gather_pass(self, sent):
        return _ride_gather_pass(sent)

    def weights(self, gathered):
        return _prep_weights(_unpack_rest(gathered, self.local, self.me_chip))

    def reduce_send(self, g):
        gbufs = _pack_grads_rest(_unprep_grads_early(g))
        got = _reduce_pair_exchange(gbufs, "reduce_pair_exchange_rest")
        self.parts = [_pair_add(g_, r_, self.c_arr, tr, "pair_add_%d" % (i + 1))
                      for i, (g_, r_, tr) in enumerate(zip(gbufs, got, _ADD_TILES[1:]))]
        return _ride_chip_exchange(self.parts)

    def reduce_arrived(self, arrived):
        self.arrived = list(arrived)


def kernel(x, p, positions, w_in, conv_w, dn_a_log, dn_dt_bias, dn_norm_w, q_norm_w, w_uq, kv_norm_w, w_uk, w_uv, w_br_dn, w_br_mla, w_o, ln1_g, ln1_b, w_ffn_in, w_ffn_out, w_ple, w_ple_gate, ln2_g, ln2_b, loss_target, m_w_in, m_conv_w, m_dn_a_log, m_dn_dt_bias, m_dn_norm_w, m_q_norm_w, m_w_uq, m_kv_norm_w, m_w_uk, m_w_uv, m_w_br_dn, m_w_br_mla, m_w_o, m_ln1_g, m_ln1_b, m_w_ffn_in, m_w_ffn_out, m_w_ple, m_w_ple_gate, m_ln2_g, m_ln2_b, v_w_in, v_conv_w, v_dn_a_log, v_dn_dt_bias, v_dn_norm_w, v_q_norm_w, v_w_uq, v_kv_norm_w, v_w_uk, v_w_uv, v_w_br_dn, v_w_br_mla, v_w_o, v_ln1_g, v_ln1_b, v_w_ffn_in, v_w_ffn_out, v_w_ple, v_w_ple_gate, v_ln2_g, v_ln2_b):
    ws = dict(w_in=w_in, conv_w=conv_w, dn_a_log=dn_a_log, dn_dt_bias=dn_dt_bias, dn_norm_w=dn_norm_w, q_norm_w=q_norm_w,
              w_uq=w_uq, kv_norm_w=kv_norm_w, w_uk=w_uk, w_uv=w_uv, w_br_dn=w_br_dn, w_br_mla=w_br_mla, w_o=w_o,
              ln1_g=ln1_g, ln1_b=ln1_b, w_ffn_in=w_ffn_in, w_ffn_out=w_ffn_out, w_ple=w_ple, w_ple_gate=w_ple_gate,
              ln2_g=ln2_g, ln2_b=ln2_b)
    ms = dict(w_in=m_w_in, conv_w=m_conv_w, dn_a_log=m_dn_a_log, dn_dt_bias=m_dn_dt_bias, dn_norm_w=m_dn_norm_w,
              q_norm_w=m_q_norm_w, w_uq=m_w_uq, kv_norm_w=m_kv_norm_w, w_uk=m_w_uk, w_uv=m_w_uv, w_br_dn=m_w_br_dn,
              w_br_mla=m_w_br_mla, w_o=m_w_o, ln1_g=m_ln1_g, ln1_b=m_ln1_b, w_ffn_in=m_w_ffn_in, w_ffn_out=m_w_ffn_out,
              w_ple=m_w_ple, w_ple_gate=m_w_ple_gate, ln2_g=m_ln2_g, ln2_b=m_ln2_b)
    vs = dict(w_in=v_w_in, conv_w=v_conv_w, dn_a_log=v_dn_a_log, dn_dt_bias=v_dn_dt_bias, dn_norm_w=v_dn_norm_w,
              q_norm_w=v_q_norm_w, w_uq=v_w_uq, kv_norm_w=v_kv_norm_w, w_uk=v_w_uk, w_uv=v_w_uv, w_br_dn=v_w_br_dn,
              w_br_mla=v_w_br_mla, w_o=v_w_o, ln1_g=v_ln1_g, ln1_b=v_ln1_b, w_ffn_in=v_w_ffn_in, w_ffn_out=v_w_ffn_out,
              w_ple=v_w_ple, w_ple_gate=v_w_ple_gate, ln2_g=v_ln2_g, ln2_b=v_ln2_b)
    mx, my, mc = lax.axis_index("x"), lax.axis_index("y"), lax.axis_index("c")

    me_chip = 2 * mx + my
    c_arr = jnp.reshape(mc, (1,)).astype(jnp.int32)
    me_arr = jnp.reshape(me_chip, (1,)).astype(jnp.int32)
    sharded = ('w_in', 'w_ffn_in') + tuple(name for name, _, _ in _FLATB_PIECES)

    local = _pack_shards({name: ws[name][0] for name in sharded}, conv_w[0])
    (gathered_in,) = _gather_shards(local[:1], "gather_w_in")
    w_in_full, conv_full = _unpack_w_in(gathered_in, local[0], me_chip)
    small = {n: ws[n][0] for n in _SMALL_NAMES}
    small['conv_w'] = conv_full
    sp = _prep_small(small)
    cosb, sinb = _rope_tables(positions[0])
    exch = _Exchange(local[1:], me_chip, c_arr)

    loss_lanes, dx, g = _local_step(x[0], p[0, 0], cosb, sinb, loss_target[0], _prep_w_in(w_in_full), sp, exch)
    gw = _unprep_grads_late(g)
    loss = lax.psum(jnp.sum(loss_lanes), ("x", "y", "c"))

    g_in = [_shard_columns(gw['w_in'], W_IN_SHARD)]
    got = _reduce_pair_exchange(g_in, "reduce_pair_exchange_w_in")
    part_in = _pair_add(g_in[0], got[0], c_arr, _ADD_TILES[0], "pair_add_0")
    parts = [part_in] + exch.parts
    arrived = list(_reduce_chip_exchange([part_in], "reduce_chip_exchange_w_in")) + exch.arrived
    mine = [_chip_add(p_, r_, me_arr, tr, "chip_add_%d" % i) for i, (p_, r_, tr) in enumerate(zip(parts, arrived, _ADD_TILES))]
    reduced = _unpack_reduced(mine, _reduce_pair_share(mine, "reduce_pair_share"), mc)
    tot = _small_allreduce(_pack_small(gw))
    gred = {name: reduced[name][None] for name in sharded}
    for i, n in enumerate(_SMALL_NAMES):
        gred[n] = tot[i * _SMALL_GROUP, :ws[n].shape[1]][None]
    conv_tot = tot[_CONV_SMALL_ROW:_CONV_SMALL_ROW + _CONV_SMALL_ROWS].reshape(DN_CONV, QKV_W)
    gred['conv_w'] = lax.dynamic_slice_in_dim(conv_tot, (2 * mx + my) * _CONV_SHARD, _CONV_SHARD, axis=1)[None]

    deltas, new_m, new_v = {}, {}, {}
    for n in _WEIGHT_NAMES:
        gred[n] = gred[n].reshape(ws[n].shape)
        deltas[n], new_m[n], new_v[n] = _adamw(ws[n], gred[n], ms[n], vs[n], "adamw_" + n)
    return (loss, dx[None], *[gred[n] for n in _WEIGHT_NAMES], *[deltas[n] for n in _WEIGHT_NAMES],
            *[new_m[n] for n in _WEIGHT_NAMES], *[new_v[n] for n in _WEIGHT_NAMES])
```

```python
import functools
import math

import jax
import jax.numpy as jnp
from jax import lax
from jax.experimental import pallas as pl
from jax.experimental.pallas import tpu as pltpu

F32 = jnp.float32
_CDT = jnp.bfloat16
_HI = lax.Precision.HIGHEST
_MESH = pl.DeviceIdType.MESH

D_MODEL = 1024
PLE_DIM = 256
HEADS = 8
DN_DK = 128
DN_CHUNK = 64
DN_CONV = 4
QKV_W = 3 * HEADS * DN_DK
Q_LORA = 384
KV_LORA = 256
NOPE = 128
ROPE = 64
ROPE_PAD = 128
FFN_HIDDEN = 2816
D_IN = 6864
ROPE_BASE = 10000.0
ALPHA = 2.0 ** 0.25
ATT_SCALE = (NOPE + ROPE) ** -0.5
NEG_BIG = -1e30
ADAM_LR, ADAM_B1, ADAM_B2, ADAM_EPS, ADAM_WD, ADAM_STEP = 0.001, 0.9, 0.999, 1e-08, 0.01, 10

LANE = 128
VMEM_LIMIT = 56 * 1024 * 1024
MM_VMEM_BUDGET = 40 * 1024 * 1024
N_SHARD = 4
FLAT_W = 1024
SMALL_ROWS = 88


def _tile(dim, cap):
    if dim <= cap:
        return dim
    t = (cap // LANE) * LANE
    while t >= LANE:
        if dim % t == 0:
            return t
        t -= LANE
    return dim


def _cparams(sem):
    return pltpu.CompilerParams(dimension_semantics=sem, vmem_limit_bytes=VMEM_LIMIT)


def _mm(a, b, *, name, ta=False, tb=False, add=None, add_scale=1.0, out_dtype=F32, heads=None,
        a_head=None, b_head=None, out_head=None, dims=None, tm=1408, tn=1408):
    m, n, k = dims
    tm, tn = _tile(m, tm), _tile(n, tn)
    sa, sb, so = a.dtype.itemsize, b.dtype.itemsize, jnp.dtype(out_dtype).itemsize

    def vmem_need(tk_):
        acc = tm * tn * 4 if tk_ < k else 0
        extra = 2 * tm * tn * 4 if add is not None else 0
        return 2 * (tm * tk_ * sa + tk_ * tn * sb) + 2 * tm * tn * so + acc + extra

    tk = k
    while vmem_need(tk) > MM_VMEM_BUDGET and tk > LANE:
        smaller = _tile(k, tk - LANE)
        if smaller >= tk:
            break
        tk = smaller
    nk = k // tk
    hgrid = () if heads is None else (heads,)
    off = len(hgrid)

    def spec(rows, cols, rtile, ctile, rsel, csel, layout):
        def idx(*g):
            h = g[0] if off else 0
            ri, ci = g[off + rsel], g[off + csel]
            if layout == 'lead':
                return (h, ri, ci)
            if layout == 'col':
                return (ri, h * (cols // ctile) + ci)
            return (ri, ci)
        if layout == 'lead':
            return pl.BlockSpec((None, rtile, ctile), idx)
        return pl.BlockSpec((rtile, ctile), idx)

    a_spec = spec(k, m, tk, tm, 2, 0, a_head) if ta else spec(m, k, tm, tk, 0, 2, a_head)
    b_spec = spec(n, k, tn, tk, 1, 2, b_head) if tb else spec(k, n, tk, tn, 2, 1, b_head)
    o_spec = spec(m, n, tm, tn, 0, 1, out_head)
    in_specs = [a_spec, b_spec]
    args = [a, b]
    if add is not None:
        in_specs.append(spec(m, n, tm, tn, 0, 1, out_head))
        args.append(add)
    dn = (((0 if ta else 1,), (1 if tb else 0,)), ((), ()))

    def body(*refs):
        a_ref, b_ref = refs[0], refs[1]
        prod = lax.dot_general(a_ref[...].astype(_CDT), b_ref[...].astype(_CDT), dn, preferred_element_type=F32)
        if nk == 1:
            o_ref = refs[-1]
            if add is not None:
                prod = prod + refs[2][...].astype(F32) * add_scale
            o_ref[...] = prod.astype(out_dtype)
            return
        o_ref, acc_ref = refs[-2], refs[-1]
        kk = pl.program_id(off + 2)

        @pl.when(kk == 0)
        def _():
            if add is not None:
                acc_ref[...] = refs[2][...].astype(F32) * add_scale
            else:
                acc_ref[...] = jnp.zeros_like(acc_ref)

        acc_ref[...] += prod

        @pl.when(kk == nk - 1)
        def _():
            o_ref[...] = acc_ref[...].astype(out_dtype)

    if out_head == 'lead':
        oshape = (heads, m, n)
    elif out_head == 'col':
        oshape = (m, heads * n)
    else:
        oshape = (m, n)
    sem = ("parallel",) * (off + 2) + ("arbitrary",)
    return pl.pallas_call(
        body, name=name, grid=hgrid + (m // tm, n // tn, nk), in_specs=in_specs, out_specs=o_spec,
        out_shape=jax.ShapeDtypeStruct(oshape, out_dtype),
        scratch_shapes=[pltpu.VMEM((tm, tn), F32)] if nk > 1 else [],
        compiler_params=_cparams(sem))(*args)


def _mm2(a, b, **kw):
    ta, tb = kw.get('ta', False), kw.get('tb', False)
    m = a.shape[1] if ta else a.shape[0]
    k = a.shape[0] if ta else a.shape[1]
    n = b.shape[0] if tb else b.shape[1]
    return _mm(a, b, dims=(m, n, k), **kw)


def _rowwise(fn, rows, bcast, outs, reds=(), *, name, tm=256, heads=None):
    t = rows[0][0].shape[0]
    tm = min(tm, t)
    hn = 1 if heads is None else heads
    in_specs, args = [], []
    for arr, width, base, per_head in rows:
        in_specs.append(pl.BlockSpec((tm, width), functools.partial(
            lambda i, h, base, per_head: (i, base + (h if per_head else 0)), base=base, per_head=per_head)))
        args.append(arr)
    for arr in bcast:
        in_specs.append(pl.BlockSpec(arr.shape, lambda i, h: (0, 0)))
        args.append(arr)
    out_specs, out_shape = [], []
    for total, width, per_head, dt in outs:
        out_specs.append(pl.BlockSpec((tm, width), functools.partial(
            lambda i, h, per_head: (i, h if per_head else 0), per_head=per_head)))
        out_shape.append(jax.ShapeDtypeStruct((t, total), dt))
    for shp in reds:
        out_specs.append(pl.BlockSpec(shp, lambda i, h: (0, 0)))
        out_shape.append(jax.ShapeDtypeStruct(shp, F32))
    n_in, n_out, n_red = len(args), len(outs), len(reds)

    def body(*refs):
        i, h = pl.program_id(0), pl.program_id(1)
        vals = fn(h, *[r[...] for r in refs[:n_in]])
        for r, v in zip(refs[n_in:n_in + n_out], vals[:n_out]):
            r[...] = v.astype(r.dtype)
        if n_red:
            @pl.when((i == 0) & (h == 0))
            def _():
                for r in refs[n_in + n_out:]:
                    r[...] = jnp.zeros_like(r)
            for r, v in zip(refs[n_in + n_out:], vals[n_out:]):
                r[...] += v

    sem = ("arbitrary", "arbitrary") if n_red else ("parallel", "parallel")
    res = pl.pallas_call(body, name=name, grid=(t // tm, hn), in_specs=in_specs, out_specs=out_specs,
                         out_shape=out_shape, compiler_params=_cparams(sem))(*args)
    return tuple(res)


def _sigmoid(x):
    return 1.0 / (1.0 + jnp.exp(-x))


def _silu(x):
    return x * _sigmoid(x)


def _softplus(x):
    return jnp.maximum(x, 0.0) + jnp.log(1.0 + jnp.exp(-jnp.abs(x)))


def _layer_norm(t, g, b):
    mu = jnp.mean(t, axis=-1, keepdims=True)
    d = t - mu
    var = jnp.mean(d * d, axis=-1, keepdims=True)
    return d * lax.rsqrt(var + 1e-5) * g + b


def _rms_norm(t, w):
    return t * lax.rsqrt(jnp.mean(t * t, axis=-1, keepdims=True) + 1e-6) * w


def _swap_rope_halves(t):
    lane = lax.broadcasted_iota(jnp.int32, t.shape, 1) % ROPE_PAD
    n = t.shape[1]
    up = pltpu.roll(t, n - ROPE // 2, axis=1)
    dn = pltpu.roll(t, ROPE // 2, axis=1)
    return jnp.where(lane < ROPE // 2, up, jnp.where(lane < ROPE, dn, 0.0))


def _rope(t, cosb, sinb):
    reps = t.shape[1] // ROPE_PAD
    c = jnp.tile(cosb, (1, reps)) if reps > 1 else cosb
    s = jnp.tile(sinb, (1, reps)) if reps > 1 else sinb
    return t * c + _swap_rope_halves(t) * s


def _rope_bwd(d, cosb, sinb):
    reps = d.shape[1] // ROPE_PAD
    c = jnp.tile(cosb, (1, reps)) if reps > 1 else cosb
    s = jnp.tile(sinb, (1, reps)) if reps > 1 else sinb
    return d * c + _swap_rope_halves(d * s)


_CONV_ROWS = 256
_CONV_COLS = 256


def _conv_window(ref, r0, lo, hi, t):
    parts = []
    start, stop = r0 - lo, r0 + _CONV_ROWS + hi
    if start < 0:
        parts.append(jnp.zeros((-start, ref.shape[1]), F32))
        start = 0
    tail = max(stop - t, 0)
    parts.append(ref[start:stop - tail, :].astype(F32))
    if tail:
        parts.append(jnp.zeros((tail, ref.shape[1]), F32))
    return parts[0] if len(parts) == 1 else jnp.concatenate(parts, axis=0)


def _conv_taps(win, w_ref, n_out):
    acc = win[8:8 + n_out] * w_ref[DN_CONV - 1:DN_CONV, :]
    for i in range(DN_CONV - 1):
        acc = acc + pltpu.roll(win, DN_CONV - 1 - i, axis=0)[8:8 + n_out] * w_ref[i:i + 1, :]
    return acc


def _conv_silu(x, w):
    t, ch = x.shape

    def body(x_ref, w_ref, o_ref):
        for r in range(t // _CONV_ROWS):
            r0 = r * _CONV_ROWS
            c = _conv_taps(_conv_window(x_ref, r0, 8, 0, t), w_ref, _CONV_ROWS)
            o_ref[r0:r0 + _CONV_ROWS, :] = _silu(c)

    return pl.pallas_call(
        body, name="conv_silu", grid=(ch // _CONV_COLS,),
        in_specs=[pl.BlockSpec((t, _CONV_COLS), lambda j: (0, j)), pl.BlockSpec((DN_CONV, _CONV_COLS), lambda j: (0, j))],
        out_specs=pl.BlockSpec((t, _CONV_COLS), lambda j: (0, j)),
        out_shape=jax.ShapeDtypeStruct((t, ch), F32), compiler_params=_cparams(("parallel",)))(x, w)


def _conv_silu_bwd(x, w, dys):
    t, ch = x.shape
    per = ch // len(dys) // _CONV_COLS

    def body(x_ref, w_ref, *rest):
        dy_refs, (dx_ref, dw_ref) = rest[:len(dys)], rest[len(dys):]
        sec = pl.program_id(0) // per
        dws = [jnp.zeros((1, _CONV_COLS), F32) for _ in range(DN_CONV)]
        for r in range(t // _CONV_ROWS):
            r0 = r * _CONV_ROWS
            n_ext = _CONV_ROWS + 8
            xw = _conv_window(x_ref, r0, 8, 8, t)
            c = _conv_taps(xw, w_ref, n_ext)
            sg = _sigmoid(c)
            dy = _conv_window(dy_refs[-1], r0, 0, 8, t)
            for k in range(len(dys) - 2, -1, -1):
                dy = jnp.where(sec == k, _conv_window(dy_refs[k], r0, 0, 8, t), dy)
            ds = dy * (sg * (1.0 + c * (1.0 - sg)))
            dx = ds[:_CONV_ROWS] * w_ref[DN_CONV - 1:DN_CONV, :]
            for i in range(DN_CONV - 1):
                sh = DN_CONV - 1 - i
                dx = dx + pltpu.roll(ds, n_ext - sh, axis=0)[:_CONV_ROWS] * w_ref[i:i + 1, :]
            dx_ref[r0:r0 + _CONV_ROWS, :] = dx.astype(dx_ref.dtype)
            ds0 = ds[:_CONV_ROWS]
            for i in range(DN_CONV):
                sh = DN_CONV - 1 - i
                xs = xw if sh == 0 else pltpu.roll(xw, sh, axis=0)
                dws[i] = dws[i] + jnp.sum(ds0 * xs[8:8 + _CONV_ROWS], axis=0, keepdims=True)
        for i in range(DN_CONV):
            dw_ref[i:i + 1, :] = dws[i]

    blk = pl.BlockSpec((t, _CONV_COLS), lambda j: (0, j))
    wblk = pl.BlockSpec((DN_CONV, _CONV_COLS), lambda j: (0, j))
    dy_specs = [pl.BlockSpec((t, _CONV_COLS), functools.partial(lambda j, k: (0, jnp.clip(j - k * per, 0, per - 1)), k=k))
                for k in range(len(dys))]
    return pl.pallas_call(
        body, name="conv_silu_bwd", grid=(ch // _CONV_COLS,), in_specs=[blk, wblk] + dy_specs, out_specs=[blk, wblk],
        out_shape=[jax.ShapeDtypeStruct((t, ch), _CDT), jax.ShapeDtypeStruct((DN_CONV, ch), F32)],
        compiler_params=_cparams(("arbitrary",)))(x, w, *dys)


_PA_ROWS = 512


def _bmm(a, b, spec, exact=False):
    if exact:
        return jnp.einsum(spec, a, b, precision=_HI, preferred_element_type=F32)
    return jnp.einsum(spec, a.astype(_CDT), b.astype(_CDT), preferred_element_type=F32)


def _split16(a):
    hi = a.astype(jnp.bfloat16)
    return hi, (a - hi.astype(F32)).astype(jnp.bfloat16)


def _bmm3(a, b, spec):
    ah, al = _split16(a)
    bh, bl = _split16(b)
    e = lambda p, q: jnp.einsum(spec, p, q, preferred_element_type=F32)
    return e(ah, bh) + (e(ah, bl) + e(al, bh))


def _tri_inverse(l_mat, eye):
    pw = -l_mat
    t_inv = eye + pw
    for _ in range(5):
        pw = _bmm3(pw, pw, 'bij,bjk->bik')
        t_inv = t_inv + _bmm3(t_inv, pw, 'bij,bjk->bik')
    return t_inv


@jax.custom_vjp
def _tri_inverse_saved(l_mat, t_saved):
    return t_saved


def _tri_inverse_saved_fwd(l_mat, t_saved):
    return t_saved, t_saved


def _tri_inverse_saved_bwd(t_saved, dt):
    left = _bmm3(t_saved, dt, 'bji,bjk->bik')
    return -_bmm3(left, t_saved, 'bij,bkj->bik'), jnp.zeros_like(t_saved)


_tri_inverse_saved.defvjp(_tri_inverse_saved_fwd, _tri_inverse_saved_bwd)


def _phase_a(h, q, k, v, ba, alog, dtb, t_saved=None):
    r = q.shape[0]
    nb = r // DN_CHUNK
    c = DN_CHUNK
    lane = lax.broadcasted_iota(jnp.int32, (1, LANE), 1)
    selb = (lane == h).astype(F32)
    sela = (lane == h + HEADS).astype(F32)
    b_raw = jnp.sum(ba * selb, axis=1, keepdims=True)
    a_raw = jnp.sum(ba * sela, axis=1, keepdims=True)
    al = jnp.sum(alog * selb, axis=1, keepdims=True)
    dt = jnp.sum(dtb * selb, axis=1, keepdims=True)
    beta = jnp.broadcast_to(_sigmoid(b_raw), (r, LANE))
    g = jnp.broadcast_to(-jnp.exp(al) * _softplus(a_raw + dt), (r, LANE))
    qn = q * lax.rsqrt(jnp.sum(q * q, -1, keepdims=True) + 1e-6) * (DN_DK ** -0.5)
    kn = k * lax.rsqrt(jnp.sum(k * k, -1, keepdims=True) + 1e-6)
    q3, k3, v3 = qn.reshape(nb, c, LANE), kn.reshape(nb, c, LANE), v.reshape(nb, c, LANE)
    b3, g3 = beta.reshape(nb, c, LANE), g.reshape(nb, c, LANE)
    ri = lax.broadcasted_iota(jnp.int32, (nb, c, c), 1)
    ci = lax.broadcasted_iota(jnp.int32, (nb, c, c), 2)
    tril, strict = ri >= ci, ri > ci
    gc = _bmm(tril.astype(F32), g3, 'bij,bjd->bid', exact=True)
    onehot = (lax.broadcasted_iota(jnp.int32, (nb, c, LANE), 2) == 0).astype(F32)
    g_row = _bmm(onehot, gc, 'bid,bjd->bij', exact=True)
    diff = gc[:, :, :c] - g_row
    decay = jnp.where(tril, jnp.exp(jnp.where(tril, diff, 0.0)), 0.0)
    kb = k3 * b3
    l_mat = jnp.where(strict, _bmm(kb, k3, 'bid,bjd->bij') * decay, 0.0)
    if t_saved is None:
        t_inv = _tri_inverse(l_mat, (ri == ci).astype(F32))
    else:
        t_inv = _tri_inverse_saved(l_mat, t_saved.reshape(nb, c, c))
    eg = jnp.exp(gc)
    u = _bmm(t_inv, v3 * b3, 'bij,bje->bie')
    w = _bmm(t_inv, kb * eg, 'bij,bje->bie')
    intra = jnp.where(tril, _bmm(q3, k3, 'bid,bjd->bij') * decay, 0.0)
    qd = q3 * eg
    gl = jnp.sum(g3, axis=1, keepdims=True)
    kt = k3 * jnp.exp(gl - gc)
    outs = (u.reshape(r, LANE), w.reshape(r, LANE), qd.reshape(r, LANE), kt.reshape(r, LANE),
            intra.reshape(r, c), gl.reshape(nb, LANE))
    return outs + (t_inv.reshape(r, c),) if t_saved is None else outs


def _pa_specs(t):
    rr = min(_PA_ROWS, t)
    nb = rr // DN_CHUNK
    qkv = [pl.BlockSpec((rr, LANE), functools.partial(lambda i, h, o: (i, o + h), o=o)) for o in (0, HEADS, 2 * HEADS)]
    ba = pl.BlockSpec((rr, LANE), lambda i, h: (i, 3))
    vec = pl.BlockSpec((1, LANE), lambda i, h: (0, 0))
    row = pl.BlockSpec((rr, LANE), lambda i, h: (i, h))
    intra = pl.BlockSpec((None, rr, DN_CHUNK), lambda i, h: (h, i, 0))
    gl = pl.BlockSpec((nb, LANE), lambda i, h: (i, h))
    return rr, qkv, ba, vec, row, intra, gl


def _grid_ends(grid):
    first = lambda: functools.reduce(jnp.logical_and, [pl.program_id(a) == 0 for a in range(len(grid))])
    last = lambda: functools.reduce(jnp.logical_and, [pl.program_id(a) == n - 1 for a, n in enumerate(grid)])
    return first, last


def _delta_local(qkv_act, pm, alog, dtb, rider=None):
    t = qkv_act.shape[0]
    rr, qkv, ba, vec, row, intra, gl = _pa_specs(t)

    def body(q, k, v, b, al, dt, *outs):
        vals = _phase_a(pl.program_id(1), q[...], k[...], v[...], b[...], al[...], dt[...])
        for o, val in zip(outs, vals):
            o[...] = val

    wide = jax.ShapeDtypeStruct((t, HEADS * LANE), F32)
    sq = jax.ShapeDtypeStruct((HEADS, t, DN_CHUNK), F32)
    grid = (t // rr, HEADS)
    return _carried_call(
        body, rider, *_grid_ends(grid), name="delta_local", grid=grid, in_specs=qkv + [ba, vec, vec],
        out_specs=[row] * 4 + [intra, gl, intra],
        out_shape=[wide] * 4 + [sq, jax.ShapeDtypeStruct((t // DN_CHUNK, HEADS * LANE), F32), sq],
        scratch_shapes=[], sem=("arbitrary", "arbitrary"), args=(qkv_act, qkv_act, qkv_act, pm, alog, dtb))


def _delta_local_bwd(qkv_act, pm, alog, dtb, t_inv, du, dw, dqd, dkt, dintra, dgl, rider=None):
    t = qkv_act.shape[0]
    rr, qkv, ba, vec, row, intra, gl = _pa_specs(t)

    def body(q, k, v, b, al, dt, ti, du_r, dw_r, dqd_r, dkt_r, di_r, dgl_r, dq_o, dk_o, dv_o, dba_o, dal_o, ddt_o):
        i, h = pl.program_id(0), pl.program_id(1)
        t_saved = ti[...]
        _, vjp = jax.vjp(lambda *a: _phase_a(h, *a, t_saved=t_saved), q[...], k[...], v[...], b[...], al[...], dt[...])
        dq, dk, dv, dba, dal, ddt = vjp((du_r[...], dw_r[...], dqd_r[...], dkt_r[...], di_r[...], dgl_r[...]))
        dq_o[...], dk_o[...], dv_o[...] = dq, dk, dv

        @pl.when(h == 0)
        def _():
            dba_o[...] = jnp.zeros_like(dba_o)

        @pl.when((h == 0) & (i == 0))
        def _():
            dal_o[...] = jnp.zeros_like(dal_o)
            ddt_o[...] = jnp.zeros_like(ddt_o)

        dba_o[...] += dba
        dal_o[...] += dal
        ddt_o[...] += ddt

    wide = jax.ShapeDtypeStruct((t, HEADS * LANE), F32)
    vshape = jax.ShapeDtypeStruct((1, LANE), F32)
    grid = (t // rr, HEADS)
    return _carried_call(
        body, rider, *_grid_ends(grid), name="delta_local_bwd", grid=grid,
        in_specs=qkv + [ba, vec, vec, intra] + [row] * 4 + [intra, gl],
        out_specs=[row] * 3 + [pl.BlockSpec((rr, LANE), lambda i, h: (i, 0)), vec, vec],
        out_shape=[wide] * 3 + [jax.ShapeDtypeStruct((t, LANE), F32), vshape, vshape],
        scratch_shapes=[], sem=("arbitrary", "arbitrary"),
        args=(qkv_act, qkv_act, qkv_act, pm, alog, dtb, t_inv, du, dw, dqd, dkt, dintra, dgl))


_SCAN_ROWS = 512


def _dot(a, b, dn):
    return lax.dot_general(a.astype(_CDT), b.astype(_CDT), (dn, ((), ())), preferred_element_type=F32)


_NN = ((1,), (0,))
_NT = ((1,), (1,))
_TN = ((0,), (0,))


def _delta_scan(u, w, qd, kt, intra, gl, rider=None):
    t = u.shape[0]
    rr = min(_SCAN_ROWS, t)
    nc = rr // DN_CHUNK

    def body(u_ref, w_ref, qd_ref, kt_ref, a_ref, gl_ref, o_ref, sall_ref, s_scr):
        @pl.when(pl.program_id(0) == 0)
        def _():
            s_scr[...] = jnp.zeros_like(s_scr)

        def chunk(c, carry):
            r0 = pl.multiple_of(c * DN_CHUNK, DN_CHUNK)
            rows = pl.ds(r0, DN_CHUNK)
            e = jnp.exp(gl_ref[pl.ds(c, 1), :])
            states = [s_scr[h] for h in range(HEADS)]
            u_c, w_c, qd_c, kt_c = u_ref[rows, :], w_ref[rows, :], qd_ref[rows, :], kt_ref[rows, :]
            a_c = [a_ref[h, rows, :] for h in range(HEADS)]
            o_new, s_new = [], []
            for h in range(HEADS):
                cs = slice(h * LANE, (h + 1) * LANE)
                s = states[h]
                v_new = u_c[:, cs] - _dot(w_c[:, cs], s, _NN)
                o_new.append(_dot(qd_c[:, cs], s, _NN) + _dot(a_c[h], v_new, _NN))
                s_new.append(s * e[:, cs] + _dot(kt_c[:, cs], v_new, _TN))
            o_ref[rows, :] = jnp.concatenate(o_new, axis=1)
            for h in range(HEADS):
                sall_ref[c, h] = states[h]
                s_scr[h] = s_new[h]
            return carry

        lax.fori_loop(0, nc, chunk, 0)

    row = pl.BlockSpec((rr, HEADS * LANE), lambda i: (i, 0))
    grid = (t // rr,)
    return _carried_call(
        body, rider, *_grid_ends(grid), name="delta_scan", grid=grid,
        in_specs=[row] * 4 + [pl.BlockSpec((HEADS, rr, DN_CHUNK), lambda i: (0, i, 0)),
                              pl.BlockSpec((nc, HEADS * LANE), lambda i: (i, 0))],
        out_specs=[row, pl.BlockSpec((nc, HEADS, LANE, LANE), lambda i: (i, 0, 0, 0))],
        out_shape=[jax.ShapeDtypeStruct((t, HEADS * LANE), F32),
                   jax.ShapeDtypeStruct((t // DN_CHUNK, HEADS, LANE, LANE), F32)],
        scratch_shapes=[pltpu.VMEM((HEADS, LANE, LANE), F32)], sem=("arbitrary",), args=(u, w, qd, kt, intra, gl))


def _delta_scan_bwd(u, w, qd, kt, intra, gl, sall, do, rider=None):
    t = u.shape[0]
    rr = min(_SCAN_ROWS, t)
    nc = rr // DN_CHUNK
    ng = t // rr

    def body(u_ref, w_ref, qd_ref, kt_ref, a_ref, gl_ref, sall_ref, do_ref,
             du_ref, dw_ref, dqd_ref, dkt_ref, da_ref, dgl_ref, ds_scr):
        @pl.when(pl.program_id(0) == 0)
        def _():
            ds_scr[...] = jnp.zeros_like(ds_scr)

        def chunk(cc, carry):
            c = nc - 1 - cc
            r0 = pl.multiple_of(c * DN_CHUNK, DN_CHUNK)
            rows = pl.ds(r0, DN_CHUNK)
            e = jnp.exp(gl_ref[pl.ds(c, 1), :])
            states = [sall_ref[c, h] for h in range(HEADS)]
            ds_outs = [ds_scr[h] for h in range(HEADS)]
            u_a, w_a, kt_a, qd_a, do_a = u_ref[rows, :], w_ref[rows, :], kt_ref[rows, :], qd_ref[rows, :], do_ref[rows, :]
            a_a = [a_ref[h, rows, :] for h in range(HEADS)]
            da, dqd, dkt, du, dw, dgl, ds_new = [], [], [], [], [], [], []
            for h in range(HEADS):
                cs = slice(h * LANE, (h + 1) * LANE)
                s, ds_out = states[h], ds_outs[h]
                w_c, kt_c, qd_c, do_c = w_a[:, cs], kt_a[:, cs], qd_a[:, cs], do_a[:, cs]
                v_new = u_a[:, cs] - _dot(w_c, s, _NN)
                dv_new = _dot(a_a[h], do_c, _TN) + _dot(kt_c, ds_out, _NN)
                da.append(_dot(do_c, v_new, _NT))
                dqd.append(_dot(do_c, s, _NT))
                dkt.append(_dot(v_new, ds_out, _NT))
                du.append(dv_new)
                dw.append(-_dot(dv_new, s, _NT))
                eh = e[:, cs]
                dgl.append(jnp.broadcast_to(jnp.sum(ds_out * s, axis=0, keepdims=True) * eh, (8, LANE)))
                ds_new.append(ds_out * eh + _dot(qd_c, do_c, _TN) - _dot(w_c, dv_new, _TN))
            cat = lambda parts: jnp.concatenate(parts, axis=1)
            dqd_ref[rows, :], dkt_ref[rows, :], du_ref[rows, :], dw_ref[rows, :] = cat(dqd), cat(dkt), cat(du), cat(dw)
            dgl_ref[pl.ds(pl.multiple_of(c * 8, 8), 8), :] = cat(dgl)
            for h in range(HEADS):
                da_ref[h, rows, :] = da[h]
                ds_scr[h] = ds_new[h]
            return carry

        lax.fori_loop(0, nc, chunk, 0)

    rev = lambda i: (ng - 1 - i, 0)
    row = pl.BlockSpec((rr, HEADS * LANE), rev)
    a_spec = pl.BlockSpec((HEADS, rr, DN_CHUNK), lambda i: (0, ng - 1 - i, 0))
    gl_spec = pl.BlockSpec((nc, HEADS * LANE), rev)
    wide = jax.ShapeDtypeStruct((t, HEADS * LANE), F32)
    outs, carried = _carried_call(
        body, rider, *_grid_ends((ng,)), name="delta_scan_bwd", grid=(ng,),
        in_specs=[row] * 4 + [a_spec, gl_spec, pl.BlockSpec((nc, HEADS, LANE, LANE), lambda i: (ng - 1 - i, 0, 0, 0)), row],
        out_specs=[row] * 4 + [a_spec, pl.BlockSpec((nc * 8, HEADS * LANE), rev)],
        out_shape=[wide] * 4 + [jax.ShapeDtypeStruct((HEADS, t, DN_CHUNK), F32),
                                jax.ShapeDtypeStruct((t // DN_CHUNK * 8, HEADS * LANE), F32)],
        scratch_shapes=[pltpu.VMEM((HEADS, LANE, LANE), F32)], sem=("arbitrary",), args=(u, w, qd, kt, intra, gl, sall, do))
    return tuple(outs[:5]) + (outs[5].reshape(t // DN_CHUNK, 8, HEADS * LANE)[:, 0, :],), carried


_ATT_TILE = 512


def _kv_rows(j, tk):
    return pl.ds(pl.multiple_of(j * tk, tk), tk)


def _att_scores(ql, qr, ckv_ref, kr_ref, j, tk):
    ks = _kv_rows(j, tk)
    return (_dot(ql, ckv_ref[ks, :], _NT) + _dot(qr, kr_ref[ks, :], _NT)) * ATT_SCALE


def _diag_mask(s):
    qi = lax.broadcasted_iota(jnp.int32, s.shape, 0)
    ki = lax.broadcasted_iota(jnp.int32, s.shape, 1)
    return jnp.where(ki <= qi, s, NEG_BIG)


def _attention(ql, qr, ckv, kr):
    t = ckv.shape[0]
    tq = min(_ATT_TILE, t)

    def body(ql_ref, qr_ref, ckv_ref, kr_ref, o_ref, lse_ref, m_scr, l_scr, acc_scr, s_scr):
        qi = pl.program_id(1)
        q_lat, q_rope = ql_ref[...], qr_ref[...]
        m_scr[...] = jnp.full_like(m_scr, NEG_BIG)
        l_scr[...] = jnp.zeros_like(l_scr)
        acc_scr[...] = jnp.zeros_like(acc_scr)

        def consume(j, s):
            m_old = m_scr[...]
            m_new = jnp.maximum(m_old, jnp.max(s, axis=-1, keepdims=True))
            p = jnp.exp(s - m_new)
            alpha = jnp.exp(m_old - m_new)
            l_scr[...] = l_scr[...] * alpha + jnp.sum(p, axis=-1, keepdims=True)
            acc_scr[...] = acc_scr[...] * alpha + _dot(p, ckv_ref[_kv_rows(j, tq), :], _NN)
            m_scr[...] = m_new

        s_scr[0] = _att_scores(q_lat, q_rope, ckv_ref, kr_ref, 0, tq)

        def loop_body(j, carry):
            s_scr[(j + 1) % 2] = _att_scores(q_lat, q_rope, ckv_ref, kr_ref, j + 1, tq)
            consume(j, s_scr[j % 2])
            return carry

        lax.fori_loop(0, qi, loop_body, 0)
        consume(qi, _diag_mask(s_scr[qi % 2]))
        o_ref[...] = acc_scr[...] / l_scr[...]
        lse_ref[...] = m_scr[...] + jnp.log(l_scr[...])

    return pl.pallas_call(
        body, name="attention", grid=(HEADS, t // tq),
        in_specs=[pl.BlockSpec((None, tq, KV_LORA), lambda h, i: (h, i, 0)),
                  pl.BlockSpec((tq, ROPE_PAD), lambda h, i: (i, h)),
                  pl.BlockSpec((t, KV_LORA), lambda h, i: (0, 0)),
                  pl.BlockSpec((t, ROPE_PAD), lambda h, i: (0, 0))],
        out_specs=[pl.BlockSpec((None, tq, KV_LORA), lambda h, i: (h, i, 0)),
                   pl.BlockSpec((None, tq, 1), lambda h, i: (h, i, 0))],
        out_shape=[jax.ShapeDtypeStruct((HEADS, t, KV_LORA), F32), jax.ShapeDtypeStruct((HEADS, t, 1), F32)],
        scratch_shapes=[pltpu.VMEM((tq, 1), F32), pltpu.VMEM((tq, 1), F32), pltpu.VMEM((tq, KV_LORA), F32),
                        pltpu.VMEM((2, tq, tq), F32)],
        compiler_params=_cparams(("parallel", "parallel")))(ql, qr, ckv, kr)


def _attention_bwd(ql, qr, ckv, kr, out, lse, dout):
    t = ckv.shape[0]
    tq = min(_ATT_TILE, t)

    def body(ql_ref, qr_ref, ckv_ref, kr_ref, o_ref, lse_ref, do_ref, dql_ref, dqr_ref, dckv_ref, dkr_ref,
             dql_scr, dqr_scr):
        h, qi = pl.program_id(0), pl.program_id(1)

        @pl.when((h == 0) & (qi == 0))
        def _():
            dckv_ref[...] = jnp.zeros_like(dckv_ref)
            dkr_ref[...] = jnp.zeros_like(dkr_ref)

        q_lat, q_rope = ql_ref[...], qr_ref[...]
        d_o = do_ref[...].astype(_CDT)
        lse_v = lse_ref[...]
        dsum = jnp.sum(do_ref[...] * o_ref[...], axis=-1, keepdims=True)
        dql_scr[...] = jnp.zeros_like(dql_scr)
        dqr_scr[...] = jnp.zeros_like(dqr_scr)

        def step(j, masked):
            ks = _kv_rows(j, tq)
            s = _att_scores(q_lat, q_rope, ckv_ref, kr_ref, j, tq)
            if masked:
                s = _diag_mask(s)
            p = jnp.exp(s - lse_v)
            kv = ckv_ref[ks, :]
            ds = (p * (_dot(d_o, kv, _NT) - dsum) * ATT_SCALE).astype(_CDT)
            pb = p.astype(_CDT)
            dql_scr[...] += _dot(ds, kv, _NN)
            dqr_scr[...] += _dot(ds, kr_ref[ks, :], _NN)
            dckv_ref[ks, :] += _dot(pb, d_o, _TN) + _dot(ds, q_lat, _TN)
            dkr_ref[ks, :] += _dot(ds, q_rope, _TN)

        def loop_body(j, carry):
            step(j, False)
            return carry

        lax.fori_loop(0, qi, loop_body, 0)
        step(qi, True)
        dql_ref[...] = dql_scr[...].astype(dql_ref.dtype)
        dqr_ref[...] = dqr_scr[...]

    lat = pl.BlockSpec((None, tq, KV_LORA), lambda h, i: (h, i, 0))
    rope = pl.BlockSpec((tq, ROPE_PAD), lambda h, i: (i, h))
    kfull = pl.BlockSpec((t, KV_LORA), lambda h, i: (0, 0))
    rfull = pl.BlockSpec((t, ROPE_PAD), lambda h, i: (0, 0))
    return pl.pallas_call(
        body, name="attention_bwd", grid=(HEADS, t // tq),
        in_specs=[lat, rope, kfull, rfull, lat, pl.BlockSpec((None, tq, 1), lambda h, i: (h, i, 0)), lat],
        out_specs=[lat, rope, kfull, rfull],
        out_shape=[jax.ShapeDtypeStruct((HEADS, t, KV_LORA), _CDT), jax.ShapeDtypeStruct((t, HEADS * ROPE_PAD), F32),
                   jax.ShapeDtypeStruct((t, KV_LORA), F32), jax.ShapeDtypeStruct((t, ROPE_PAD), F32)],
        scratch_shapes=[pltpu.VMEM((tq, KV_LORA), F32), pltpu.VMEM((tq, ROPE_PAD), F32)],
        compiler_params=_cparams(("arbitrary", "arbitrary")))(ql, qr, ckv, kr, out, lse, dout)


def _gated_norm(o, z, w):
    return _rms_norm(o, w) * _silu(z)


def _mla_pre(ckv, krp, cq, cosb, sinb, qw, kw):
    return _rms_norm(cq, qw), _rms_norm(ckv, kw), _rope(krp, cosb, sinb)


def _merge(gg, y_dn, y_mla):
    return _sigmoid(gg[:, :D_MODEL]) * y_dn + _sigmoid(gg[:, D_MODEL:]) * y_mla


def _ln1(xv, attn_out, g, b):
    return _layer_norm(ALPHA * xv + attn_out, g, b)


def _final(h1, ffn, gate_pre, ple_proj, g, b):
    return _layer_norm(ALPHA * h1 + ffn + _sigmoid(gate_pre) * ple_proj, g, b)


def _swiglu(gt, up):
    return _silu(gt) * up


def _local_step(x, p, cosb, sinb, target, wt, sp, exch):
    t = x.shape[0]
    bf = _CDT
    xb = x.astype(bf)
    g = {}

    qkv_pre = _mm2(xb, wt['qkv'], name="f_qkv")
    z = _mm2(xb, wt['z'], name="f_z")
    gg = _mm2(xb, wt['gg'], name="f_gg")
    pm = _mm2(xb, wt['mla'], name="f_mla")
    qkv_act = _conv_silu(qkv_pre, sp['conv_w'])
    (u, w_, qd, kt, intra, gl, t_inv), sent = _delta_local(qkv_act, pm, sp['a_log'], sp['dt_bias'], rider=exch.gather_send())
    (o_dn, sall), passed = _delta_scan(u, w_, qd, kt, intra, gl, rider=exch.gather_pass(sent))
    wt = dict(wt, **exch.weights(passed))
    (og,) = _rowwise(lambda h, o, zz, w: (_gated_norm(o, zz, w),),
                     [(o_dn, LANE, 0, True), (z, LANE, 0, True)], [sp['dn_norm_w']],
                     [(D_MODEL, LANE, True, bf)], name="f_gated_norm", tm=512, heads=HEADS)
    y_dn = _mm2(og, wt['br_dn'], name="f_br_dn")

    c_q, c_kv, k_rope = _rowwise(
        lambda h, *a: _mla_pre(*a),
        [(pm, KV_LORA, 0, False), (pm, ROPE_PAD, 2, False), (pm, Q_LORA, 2, False),
         (cosb, ROPE_PAD, 0, False), (sinb, ROPE_PAD, 0, False)],
        [sp['q_norm_w'], sp['kv_norm_w']],
        [(Q_LORA, Q_LORA, False, bf), (KV_LORA, KV_LORA, False, bf), (ROPE_PAD, ROPE_PAD, False, bf)], name="f_mla_pre")
    q_nope = _mm2(c_q, wt['uq_nope'], name="f_uq_nope", out_dtype=bf)
    q_rope_pre = _mm2(c_q, wt['uq_rope'], name="f_uq_rope")
    (q_rope,) = _rowwise(lambda h, q, c, s: (_rope(q, c, s),),
                         [(q_rope_pre, HEADS * ROPE_PAD, 0, False), (cosb, ROPE_PAD, 0, False), (sinb, ROPE_PAD, 0, False)],
                         [], [(HEADS * ROPE_PAD, HEADS * ROPE_PAD, False, bf)], name="f_q_rope")
    q_lat = _mm(q_nope, wt['uk'], name="f_q_lat", tb=True, heads=HEADS, a_head='col', b_head='lead', out_head='lead',
                dims=(t, KV_LORA, NOPE), out_dtype=bf)
    out_lat, lse = _attention(q_lat, q_rope, c_kv, k_rope)
    o_mla = _mm(out_lat, wt['uv'], name="f_o_mla", heads=HEADS, a_head='lead', b_head='lead', out_head='col',
                dims=(t, NOPE, KV_LORA), out_dtype=bf)
    y_mla = _mm2(o_mla, wt['br_mla'], name="f_br_mla")

    (mixed,) = _rowwise(lambda h, *a: (_merge(*a),),
                        [(gg, 2 * D_MODEL, 0, False), (y_dn, D_MODEL, 0, False), (y_mla, D_MODEL, 0, False)],
                        [], [(D_MODEL, D_MODEL, False, bf)], name="f_merge")
    attn_out = _mm2(mixed, wt['o'], name="f_o")
    h1, h1b = _rowwise(lambda h, *a: (_ln1(*a),) * 2, [(x, D_MODEL, 0, False), (attn_out, D_MODEL, 0, False)],
                       [sp['ln1_g'], sp['ln1_b']], [(D_MODEL, D_MODEL, False, F32), (D_MODEL, D_MODEL, False, bf)],
                       name="f_ln1")
    ffn_in = _mm2(h1b, wt['ffn_in'], name="f_ffn_in")
    (act,) = _rowwise(lambda h, gt, up: (_swiglu(gt, up),),
                      [(ffn_in, FFN_HIDDEN, 0, False), (ffn_in, FFN_HIDDEN, 1, False)], [],
                      [(FFN_HIDDEN, FFN_HIDDEN, False, bf)], name="f_swiglu")
    ffn = _mm2(act, wt['ffn_out'], name="f_ffn_out")
    gate_pre = _mm2(h1b, wt['ple_gate'], name="f_ple_gate")
    pb = p.astype(bf)
    ple_proj = _mm2(pb, wt['ple'], name="f_ple")

    def final_fn(h, h1v, ffnv, gpv, ppv, tgt, gv, bv):
        y, vjp = jax.vjp(_final, h1v, ffnv, gpv, ppv, gv, bv)
        err = y - tgt
        dh1, dffn, dgp, dpp, dg, db = vjp(err * (1.0 / D_MODEL))
        sq = err * err
        lanes = sq[:, :LANE]
        for j in range(1, D_MODEL // LANE):
            lanes = lanes + sq[:, j * LANE:(j + 1) * LANE]
        loss = jnp.sum(lanes, axis=0, keepdims=True) * (0.5 / D_MODEL)
        return dffn, dffn, dgp, dpp, dg, db, loss

    dpre2, dpre2b, dgate_pre, dple_proj, g['ln2_g'], g['ln2_b'], loss_lanes = _rowwise(
        final_fn, [(a, D_MODEL, 0, False) for a in (h1, ffn, gate_pre, ple_proj, target)],
        [sp['ln2_g'], sp['ln2_b']],
        [(D_MODEL, D_MODEL, False, F32)] + [(D_MODEL, D_MODEL, False, bf)] * 3,
        [(1, D_MODEL), (1, D_MODEL), (1, LANE)], name="b_final")

    g['ple'] = _mm2(pb, dple_proj, ta=True, name="g_ple")
    g['ple_gate'] = _mm2(h1b, dgate_pre, ta=True, name="g_ple_gate")
    g['ffn_out'] = _mm2(act, dpre2b, ta=True, name="g_ffn_out")
    dact = _mm2(dpre2b, wt['ffn_out'], tb=True, name="b_dact")

    def swiglu_bwd(h, gt, up, d):
        _, vjp = jax.vjp(_swiglu, gt, up)
        dgt, dup = vjp(d)
        return (jnp.concatenate([dgt, dup], axis=1),)

    (dffn_in,) = _rowwise(swiglu_bwd, [(ffn_in, FFN_HIDDEN, 0, False), (ffn_in, FFN_HIDDEN, 1, False),
                                       (dact, FFN_HIDDEN, 0, False)], [],
                          [(2 * FFN_HIDDEN, 2 * FFN_HIDDEN, False, bf)], name="b_swiglu")
    g['ffn_in'] = _mm2(h1b, dffn_in, ta=True, name="g_ffn_in")
    dh1 = _mm2(dffn_in, wt['ffn_in'], tb=True, name="b_dh1_ffn", add=dpre2, add_scale=ALPHA)
    dh1 = _mm2(dgate_pre, wt['ple_gate'], tb=True, name="b_dh1_gate", add=dh1)

    def ln1_bwd(h, xv, ao, d, gv, bv):
        _, vjp = jax.vjp(_ln1, xv, ao, gv, bv)
        _, dao, dg, db = vjp(d)
        return dao, dao, dg, db

    dpre1, dpre1b, g['ln1_g'], g['ln1_b'] = _rowwise(
        ln1_bwd, [(x, D_MODEL, 0, False), (attn_out, D_MODEL, 0, False), (dh1, D_MODEL, 0, False)],
        [sp['ln1_g'], sp['ln1_b']], [(D_MODEL, D_MODEL, False, F32), (D_MODEL, D_MODEL, False, bf)],
        [(1, D_MODEL), (1, D_MODEL)], name="b_ln1")

    g['o'] = _mm2(mixed, dpre1b, ta=True, name="g_o")
    dmixed = _mm2(dpre1b, wt['o'], tb=True, name="b_dmixed")

    def merge_bwd(h, ggv, yd, ym, d):
        _, vjp = jax.vjp(_merge, ggv, yd, ym)
        return vjp(d)

    dgg, dy_dn, dy_mla = _rowwise(
        merge_bwd, [(gg, 2 * D_MODEL, 0, False), (y_dn, D_MODEL, 0, False), (y_mla, D_MODEL, 0, False),
                    (dmixed, D_MODEL, 0, False)], [],
        [(2 * D_MODEL, 2 * D_MODEL, False, bf), (D_MODEL, D_MODEL, False, bf), (D_MODEL, D_MODEL, False, bf)],
        name="b_merge")
    g['br_dn'] = _mm2(og, dy_dn, ta=True, name="g_br_dn")
    dog = _mm2(dy_dn, wt['br_dn'], tb=True, name="b_dog")
    g['br_mla'] = _mm2(o_mla, dy_mla, ta=True, name="g_br_mla")
    do_mla = _mm2(dy_mla, wt['br_mla'], tb=True, name="b_do_mla", out_dtype=bf)

    dout_lat = _mm(do_mla, wt['uv'], name="b_dout_lat", tb=True, heads=HEADS, a_head='col', b_head='lead',
                   out_head='lead', dims=(t, KV_LORA, NOPE))
    g['uv'] = _mm(out_lat, do_mla, name="g_uv", ta=True, heads=HEADS, a_head='lead', b_head='col', out_head='lead',
                  dims=(KV_LORA, NOPE, t))
    dq_lat, dq_rope, dckv_att, dkr_att = _attention_bwd(q_lat, q_rope, c_kv, k_rope, out_lat, lse, dout_lat)
    dq_nope = _mm(dq_lat, wt['uk'], name="b_dq_nope", heads=HEADS, a_head='lead', b_head='lead', out_head='col',
                  dims=(t, NOPE, KV_LORA), out_dtype=bf)
    g['uk'] = _mm(dq_lat, q_nope, name="g_uk", ta=True, heads=HEADS, a_head='lead', b_head='col', out_head='lead',
                  dims=(KV_LORA, NOPE, t))
    (dq_rope_pre,) = _rowwise(lambda h, d, c, s: (_rope_bwd(d, c, s),),
                              [(dq_rope, HEADS * ROPE_PAD, 0, False), (cosb, ROPE_PAD, 0, False), (sinb, ROPE_PAD, 0, False)],
                              [], [(HEADS * ROPE_PAD, HEADS * ROPE_PAD, False, bf)], name="b_q_rope")
    g['uq_nope'] = _mm2(c_q, dq_nope, ta=True, name="g_uq_nope")
    g['uq_rope'] = _mm2(c_q, dq_rope_pre, ta=True, name="g_uq_rope")
    dc_q = _mm2(dq_nope, wt['uq_nope'], tb=True, name="b_dcq_nope")
    dc_q = _mm2(dq_rope_pre, wt['uq_rope'], tb=True, name="b_dcq_rope", add=dc_q)

    def gated_norm_bwd(h, o, zz, d, w):
        _, vjp = jax.vjp(_gated_norm, o, zz, w)
        return vjp(d)

    do_dn, dz, g['dn_norm_w'] = _rowwise(
        gated_norm_bwd, [(o_dn, LANE, 0, True), (z, LANE, 0, True), (dog, LANE, 0, True)], [sp['dn_norm_w']],
        [(D_MODEL, LANE, True, F32), (D_MODEL, LANE, True, bf)], [(1, LANE)], name="b_gated_norm", tm=512, heads=HEADS)
    (du, dw, dqd, dkt, dintra, dgl), paired = _delta_scan_bwd(u, w_, qd, kt, intra, gl, sall, do_dn, rider=exch.pair_send(g))
    (dq_a, dk_a, dv_a, dba, g['a_log'], g['dt_bias']), arrived = _delta_local_bwd(
        qkv_act, pm, sp['a_log'], sp['dt_bias'], t_inv, du, dw, dqd, dkt, dintra, dgl, rider=exch.reduce_send(paired))
    exch.reduce_arrived(arrived)
    dqkv_pre, g['conv_w'] = _conv_silu_bwd(qkv_pre, sp['conv_w'], [dq_a, dk_a, dv_a])

    def mla_pre_bwd(h, ckv, krp, cq, cosv, sinv, dcq, dckv, dkr, qw, kw):
        _, vjp = jax.vjp(lambda a, b, c, d, e: (_rms_norm(c, d), _rms_norm(a, e)), ckv, krp, cq, qw, kw)
        dckv_p, _, dcq_p, dqw, dkw = vjp((dcq, dckv))
        dkr_p = _rope_bwd(dkr, cosv, sinv)
        dpm = jnp.concatenate([dckv_p, dkr_p, jnp.zeros((ckv.shape[0], 3 * LANE), F32), dcq_p], axis=1)
        return dpm, dqw, dkw

    dpm_main, g['q_norm_w'], g['kv_norm_w'] = _rowwise(
        mla_pre_bwd,
        [(pm, KV_LORA, 0, False), (pm, ROPE_PAD, 2, False), (pm, Q_LORA, 2, False),
         (cosb, ROPE_PAD, 0, False), (sinb, ROPE_PAD, 0, False),
         (dc_q, Q_LORA, 0, False), (dckv_att, KV_LORA, 0, False), (dkr_att, ROPE_PAD, 0, False)],
        [sp['q_norm_w'], sp['kv_norm_w']], [(1152, 1152, False, F32)], [(1, Q_LORA), (1, KV_LORA)], name="b_mla_pre")
    (dpm,) = _rowwise(
        lambda h, a, b: (jnp.concatenate([a[:, :3 * LANE], b, a[:, 4 * LANE:]], axis=1),),
        [(dpm_main, 1152, 0, False), (dba, LANE, 0, False)], [], [(1152, 1152, False, bf)], name="b_dpm")

    g['qkv'] = _mm2(xb, dqkv_pre, ta=True, name="g_qkv")
    g['z'] = _mm2(xb, dz, ta=True, name="g_z")
    g['gg'] = _mm2(xb, dgg, ta=True, name="g_gg")
    g['mla'] = _mm2(xb, dpm, ta=True, name="g_mla")
    dx = _mm2(dqkv_pre, wt['qkv'], tb=True, name="b_dx_qkv", add=dpre1, add_scale=ALPHA)
    dx = _mm2(dz, wt['z'], tb=True, name="b_dx_z", add=dx)
    dx = _mm2(dgg, wt['gg'], tb=True, name="b_dx_gg", add=dx)
    dx = _mm2(dpm, wt['mla'], tb=True, name="b_dx_mla", add=dx)
    return loss_lanes, dx, g


_IN_SIZES = (QKV_W, HEADS * DN_DK, HEADS, HEADS, Q_LORA, KV_LORA, ROPE, D_MODEL, D_MODEL)


def _rope_tables(positions):
    inv_freq = ROPE_BASE ** (-jnp.arange(0, ROPE, 2, dtype=F32) / ROPE)
    ang = positions.astype(F32)[:, None] * inv_freq
    cos, sin = jnp.cos(ang), jnp.sin(ang)
    zeros = jnp.zeros((positions.shape[0], ROPE_PAD - ROPE), F32)
    return jnp.concatenate([cos, cos, zeros], axis=1), jnp.concatenate([-sin, sin, zeros], axis=1)


def _prep_w_in(w_in):
    dt = w_in.dtype
    offs = [0]
    for s in _IN_SIZES:
        offs.append(offs[-1] + s)
    qkv, z, wb, wa, cq, ckv, kr, gd, gm = [w_in[:, offs[i]:offs[i + 1]] for i in range(len(_IN_SIZES))]
    zc = lambda n: jnp.zeros((D_MODEL, n), dt)
    return {
        'qkv': qkv, 'z': z, 'gg': jnp.concatenate([gd, gm], axis=1),
        'mla': jnp.concatenate([ckv, kr, zc(ROPE_PAD - ROPE), wb, wa, zc(LANE - 2 * HEADS), zc(2 * LANE), cq], axis=1),
    }


def _prep_weights(full):
    w_uq = full['w_uq']
    wt = {
        'uq_nope': w_uq[:, :, :NOPE].reshape(Q_LORA, HEADS * NOPE),
        'uq_rope': jnp.pad(w_uq[:, :, NOPE:], ((0, 0), (0, 0), (0, ROPE_PAD - ROPE))).reshape(Q_LORA, HEADS * ROPE_PAD),
        'uk': jnp.transpose(full['w_uk'], (1, 0, 2)), 'uv': jnp.transpose(full['w_uv'], (1, 0, 2)),
        'br_dn': full['w_br_dn'], 'br_mla': full['w_br_mla'], 'o': full['w_o'], 'ffn_in': full['w_ffn_in'],
        'ffn_out': full['w_ffn_out'], 'ple': full['w_ple'], 'ple_gate': full['w_ple_gate'],
    }
    return wt


def _prep_small(small):
    pad = lambda v: jnp.pad(v, (0, LANE - v.shape[0]))[None, :]
    return {
        'conv_w': small['conv_w'], 'a_log': pad(small['dn_a_log']), 'dt_bias': pad(small['dn_dt_bias']),
        'dn_norm_w': small['dn_norm_w'][None, :], 'q_norm_w': small['q_norm_w'][None, :],
        'kv_norm_w': small['kv_norm_w'][None, :], 'ln1_g': small['ln1_g'][None, :], 'ln1_b': small['ln1_b'][None, :],
        'ln2_g': small['ln2_g'][None, :], 'ln2_b': small['ln2_b'][None, :],
    }


def _unprep_grads_late(g):
    mla = g['mla']
    ba0 = KV_LORA + ROPE_PAD
    cq0 = ba0 + 3 * LANE
    w_in = jnp.concatenate([
        g['qkv'], g['z'], mla[:, ba0:ba0 + HEADS], mla[:, ba0 + HEADS:ba0 + 2 * HEADS], mla[:, cq0:cq0 + Q_LORA],
        mla[:, :KV_LORA], mla[:, KV_LORA:KV_LORA + ROPE], g['gg']], axis=1)
    return {
        'w_in': w_in, 'conv_w': g['conv_w'], 'dn_a_log': g['a_log'][0, :HEADS], 'dn_dt_bias': g['dt_bias'][0, :HEADS],
        'dn_norm_w': g['dn_norm_w'][0], 'q_norm_w': g['q_norm_w'][0], 'kv_norm_w': g['kv_norm_w'][0],
        'ln1_g': g['ln1_g'][0], 'ln1_b': g['ln1_b'][0], 'ln2_g': g['ln2_g'][0], 'ln2_b': g['ln2_b'][0],
    }


def _unprep_grads_early(g):
    w_uq = jnp.concatenate([g['uq_nope'].reshape(Q_LORA, HEADS, NOPE),
                            g['uq_rope'].reshape(Q_LORA, HEADS, ROPE_PAD)[:, :, :ROPE]], axis=2)
    return {
        'w_uq': w_uq, 'w_uk': jnp.transpose(g['uk'], (1, 0, 2)), 'w_uv': jnp.transpose(g['uv'], (1, 0, 2)),
        'w_br_dn': g['br_dn'], 'w_br_mla': g['br_mla'], 'w_o': g['o'],
        'w_ffn_in': g['ffn_in'], 'w_ffn_out': g['ffn_out'], 'w_ple': g['ple'], 'w_ple_gate': g['ple_gate'],
    }


_FLATB_PIECES = (
    ('w_ffn_out', 704, (704, D_MODEL)), ('w_br_dn', 256, (256, D_MODEL)), ('w_br_mla', 256, (256, D_MODEL)),
    ('w_o', 256, (256, D_MODEL)), ('w_ple_gate', 256, (256, D_MODEL)), ('w_uq', 144, (96, HEADS, NOPE + ROPE)),
    ('w_uk', 64, (64, HEADS, NOPE)), ('w_uv', 64, (64, HEADS, NOPE)), ('w_ple', 64, (PLE_DIM, 256)),
)
FLATB_ROWS = 2112
W_IN_SHARD = D_IN // N_SHARD
FFN_IN_SHARD = 2 * FFN_HIDDEN // N_SHARD
A_ROWS = D_MODEL + 32
_CONV_SHARD = QKV_W // N_SHARD
_ADD_TILES = (256, 256, 352)


def _flatb_offsets():
    offs, o = {}, 0
    for name, rows, _ in _FLATB_PIECES:
        offs[name] = o
        o += rows
    return offs, o


def _pack_shards(ws, conv_w):
    conv_bits = lax.bitcast_convert_type(conv_w, jnp.bfloat16).reshape(DN_CONV, 2 * _CONV_SHARD).astype(_CDT)
    tail = jnp.pad(conv_bits, ((0, A_ROWS - D_MODEL - DN_CONV), (0, W_IN_SHARD - 2 * _CONV_SHARD)))
    a_buf = jnp.concatenate([ws['w_in'].astype(_CDT), tail], axis=0)
    parts = [ws[name].astype(_CDT).reshape(rows, FLAT_W) for name, rows, _ in _FLATB_PIECES]
    used = sum(p.shape[0] for p in parts)
    parts.append(jnp.zeros((FLATB_ROWS - used, FLAT_W), _CDT))
    return [a_buf, ws['w_ffn_in'].astype(_CDT), jnp.concatenate(parts, axis=0)]


def _unpack_w_in(gathered, local, me):
    a = [jnp.where(me == s, local, gathered[s]) for s in range(N_SHARD)]
    conv = [lax.bitcast_convert_type(
        p[D_MODEL:D_MODEL + DN_CONV, :2 * _CONV_SHARD].astype(jnp.bfloat16).reshape(DN_CONV, _CONV_SHARD, 2), F32) for p in a]
    return jnp.concatenate([p[:D_MODEL] for p in a], axis=1), jnp.concatenate(conv, axis=1)


def _unpack_rest(gathered, local, me):
    pick = lambda b, s: jnp.where(me == s, local[b], gathered[b][s])
    full = {'w_ffn_in': jnp.concatenate([pick(0, s) for s in range(N_SHARD)], axis=1)}
    offs, _ = _flatb_offsets()
    fb = [pick(1, s) for s in range(N_SHARD)]
    for name, rows, shape in _FLATB_PIECES:
        pieces = [p[offs[name]:offs[name] + rows].reshape(shape) for p in fb]
        full[name] = jnp.concatenate(pieces, axis=1 if name == 'w_ple' else 0)
    return full


def _shard_columns(g, w):
    return jnp.stack([g[:, s * w:(s + 1) * w] for s in range(N_SHARD)])


def _pack_grads_rest(gw):
    parts = []
    for name, rows, _ in _FLATB_PIECES:
        g = gw[name]
        if name == 'w_ple':
            parts.append(_shard_columns(g, PLE_DIM).reshape(N_SHARD, rows, FLAT_W))
        else:
            parts.append(g.reshape(N_SHARD, rows, FLAT_W))
    used = sum(p.shape[1] for p in parts)
    parts.append(jnp.zeros((N_SHARD, FLATB_ROWS - used, FLAT_W), F32))
    return [_shard_columns(gw['w_ffn_in'], FFN_IN_SHARD), jnp.concatenate(parts, axis=1)]


def _unpack_reduced(mine, theirs, c):
    whole = [jnp.concatenate([jnp.where(c == 0, m, t), jnp.where(c == 0, t, m)], axis=0) for m, t in zip(mine, theirs)]
    out = {'w_in': whole[0], 'w_ffn_in': whole[1]}
    offs, _ = _flatb_offsets()
    for name, rows, shape in _FLATB_PIECES:
        out[name] = whole[2][offs[name]:offs[name] + rows].reshape(shape)
    return out


_HBM = pl.BlockSpec(memory_space=pltpu.HBM)


def _place():
    x, y, c = lax.axis_index("x"), lax.axis_index("y"), lax.axis_index("c")
    chips = [(1 - x, y), (x, 1 - y), (1 - x, 1 - y)]
    return x, y, c, chips


def _remote(src, dst, send_sems, recv_sems, k, to):
    return pltpu.make_async_remote_copy(src_ref=src, dst_ref=dst, send_sem=send_sems.at[k], recv_sem=recv_sems.at[k],
                                        device_id=to, device_id_type=_MESH)


def _half_rows(ref, half, hf, lead=None):
    rows = pl.ds(pl.multiple_of(hf * half, 16), half)
    return ref.at[rows, :] if lead is None else ref.at[lead, rows, :]


class _Rider:
    def __init__(self, inputs, out_shape, n_sems, copies, aliases=None):
        self.inputs, self.out_shape, self.n_sems, self.copies = list(inputs), list(out_shape), n_sems, copies
        self.aliases = aliases or {}


def _carried_call(body, rider, first, last, *, name, grid, in_specs, out_specs, out_shape, scratch_shapes, sem, args):
    n_in, n_out, n_scr = len(in_specs), len(out_specs), len(scratch_shapes)
    if rider is None:
        res = pl.pallas_call(body, name=name, grid=grid, in_specs=in_specs, out_specs=out_specs, out_shape=out_shape,
                             scratch_shapes=scratch_shapes, compiler_params=_cparams(sem))(*args)
        return list(res), []
    ri, ro = len(rider.inputs), len(rider.out_shape)

    def full_body(*refs):
        own_in, r_in = refs[:n_in], refs[n_in:n_in + ri]
        o0 = n_in + ri
        own_out, r_out = refs[o0:o0 + n_out], refs[o0 + n_out:o0 + n_out + ro]
        s0 = o0 + n_out + ro
        own_scr, send_sems, recv_sems = refs[s0:s0 + n_scr], refs[s0 + n_scr], refs[s0 + n_scr + 1]

        @pl.when(first())
        def _():
            sends, _ = rider.copies(r_in, r_out, send_sems, recv_sems)
            for cp in sends:
                cp.start()

        body(*own_in, *own_out, *own_scr)

        @pl.when(last())
        def _():
            sends, arrivals = rider.copies(r_in, r_out, send_sems, recv_sems)
            for cp in arrivals():
                cp.wait_recv()
            for cp in sends:
                cp.wait_send()

    res = pl.pallas_call(
        full_body, name=name, grid=grid, in_specs=list(in_specs) + [_HBM] * ri, out_specs=list(out_specs) + [_HBM] * ro,
        out_shape=list(out_shape) + rider.out_shape,
        scratch_shapes=list(scratch_shapes) + [pltpu.SemaphoreType.DMA((rider.n_sems,))] * 2,
        input_output_aliases={n_in + i: n_out + o for i, o in rider.aliases.items()},
        compiler_params=_cparams(sem))(*args, *rider.inputs)
    return list(res[:n_out]), list(res[n_out:])


def _ride_gather_send(bufs):
    n = len(bufs)
    halves = [b.shape[0] // 2 for b in bufs]

    def copies(ins, outs, send_sems, recv_sems):
        x, y, c, chips = _place()
        slot = lambda b, cx, cy: _half_rows(outs[b], halves[b], c, lead=2 * cx + cy)
        sends = [_remote(_half_rows(ins[b], halves[b], c), slot(b, x, y), send_sems, recv_sems, 3 * b + j, (cx, cy, c))
                 for b in range(n) for j, (cx, cy) in enumerate(chips)]
        arrivals = lambda: [_remote(slot(b, cx, cy), slot(b, cx, cy), send_sems, recv_sems, 3 * b + j, (x, y, c))
                            for b in range(n) for j, (cx, cy) in enumerate(chips)]
        return sends, arrivals

    return _Rider(bufs, [jax.ShapeDtypeStruct((N_SHARD,) + b.shape, b.dtype) for b in bufs], 3 * n, copies)


def _ride_gather_pass(gathered):
    n = len(gathered)
    halves = [g.shape[1] // 2 for g in gathered]

    def copies(ins, outs, send_sems, recv_sems):
        x, y, c, chips = _place()
        slot = lambda b, cx, cy, hf: _half_rows(outs[b], halves[b], hf, lead=2 * cx + cy)
        sends = [_remote(slot(b, cx, cy, c), slot(b, cx, cy, c), send_sems, recv_sems, 3 * b + j, (x, y, 1 - c))
                 for b in range(n) for j, (cx, cy) in enumerate(chips)]
        arrivals = lambda: [_remote(slot(b, cx, cy, 1 - c), slot(b, cx, cy, 1 - c), send_sems, recv_sems, 3 * b + j, (x, y, c))
                            for b in range(n) for j, (cx, cy) in enumerate(chips)]
        return sends, arrivals

    return _Rider(gathered, [jax.ShapeDtypeStruct(g.shape, g.dtype) for g in gathered], 3 * n, copies,
                  aliases={b: b for b in range(n)})


def _ride_pair_exchange(gbufs):
    n = len(gbufs)
    halves = [g.shape[1] // 2 for g in gbufs]

    def copies(ins, outs, send_sems, recv_sems):
        x, y, c, _ = _place()
        sends = [_remote(ins[b].at[:, pl.ds(pl.multiple_of((1 - c) * halves[b], 16), halves[b]), :], outs[b],
                         send_sems, recv_sems, b, (x, y, 1 - c)) for b in range(n)]
        arrivals = lambda: [_remote(outs[b], outs[b], send_sems, recv_sems, b, (x, y, c)) for b in range(n)]
        return sends, arrivals

    return _Rider(gbufs, [jax.ShapeDtypeStruct((N_SHARD, h, g.shape[2]), g.dtype) for g, h in zip(gbufs, halves)], n, copies)


def _ride_chip_exchange(parts):
    n = len(parts)

    def copies(ins, outs, send_sems, recv_sems):
        x, y, c, chips = _place()
        sends = [_remote(ins[b].at[2 * cx + cy], outs[b].at[j], send_sems, recv_sems, 3 * b + j, (cx, cy, c))
                 for b in range(n) for j, (cx, cy) in enumerate(chips)]
        arrivals = lambda: [_remote(ins[b].at[0], outs[b].at[j], send_sems, recv_sems, 3 * b + j, (x, y, c))
                            for b in range(n) for j in range(len(chips))]
        return sends, arrivals

    return _Rider(parts, [jax.ShapeDtypeStruct((3,) + p.shape[1:], p.dtype) for p in parts], 3 * n, copies)


def _gather_shards(bufs, name):
    n = len(bufs)
    halves = [b.shape[0] // 2 for b in bufs]

    def body(*refs):
        ins, outs, send_sems, recv_sems = refs[:n], refs[n:2 * n], refs[2 * n], refs[2 * n + 1]
        x, y, c, chips = _place()
        me, sibling = (x, y, c), (x, y, 1 - c)
        slot = lambda b, cx, cy, hf: _half_rows(outs[b], halves[b], hf, lead=2 * cx + cy)
        first = [_remote(_half_rows(ins[b], halves[b], c), slot(b, x, y, c), send_sems, recv_sems, 6 * b + j, (cx, cy, c))
                 for b in range(n) for j, (cx, cy) in enumerate(chips)]
        for cp in first:
            cp.start()
        passed = []
        for j, (cx, cy) in enumerate(chips):
            for b in range(n):
                _remote(slot(b, cx, cy, c), slot(b, cx, cy, c), send_sems, recv_sems, 6 * b + j, me).wait_recv()
                fwd = _remote(slot(b, cx, cy, c), slot(b, cx, cy, c), send_sems, recv_sems, 6 * b + 3 + j, sibling)
                fwd.start()
                passed.append(fwd)
        for j, (cx, cy) in enumerate(chips):
            for b in range(n):
                _remote(slot(b, cx, cy, 1 - c), slot(b, cx, cy, 1 - c), send_sems, recv_sems, 6 * b + 3 + j, me).wait_recv()
        for cp in first + passed:
            cp.wait_send()

    return pl.pallas_call(
        body, name=name, out_shape=[jax.ShapeDtypeStruct((N_SHARD,) + b.shape, b.dtype) for b in bufs],
        in_specs=[_HBM] * n, out_specs=[_HBM] * n,
        scratch_shapes=[pltpu.SemaphoreType.DMA((6 * n,)), pltpu.SemaphoreType.DMA((6 * n,))],
    )(*bufs)


def _reduce_pair_exchange(gbufs, name):
    n = len(gbufs)
    halves = [g.shape[1] // 2 for g in gbufs]

    def body(*refs):
        ins, outs, send_sems, recv_sems = refs[:n], refs[n:2 * n], refs[2 * n], refs[2 * n + 1]
        x, y, c, _ = _place()
        cps = [_remote(ins[b].at[:, pl.ds(pl.multiple_of((1 - c) * halves[b], 16), halves[b]), :], outs[b],
                       send_sems, recv_sems, b, (x, y, 1 - c)) for b in range(n)]
        for cp in cps:
            cp.start()
        for cp in cps:
            cp.wait()

    return pl.pallas_call(
        body, name=name,
        out_shape=[jax.ShapeDtypeStruct((N_SHARD, h, g.shape[2]), g.dtype) for g, h in zip(gbufs, halves)],
        in_specs=[_HBM] * n, out_specs=[_HBM] * n,
        scratch_shapes=[pltpu.SemaphoreType.DMA((n,)), pltpu.SemaphoreType.DMA((n,))],
    )(*gbufs)


def _pair_add(gbuf, recv, c_arr, tr, name):
    _, rows, width = gbuf.shape
    half = rows // 2
    nt = half // tr

    def body(c_ref, a_ref, b_ref, o_ref):
        o_ref[...] = (a_ref[...] + b_ref[...]).astype(o_ref.dtype)

    blk = lambda f: pl.BlockSpec((None, tr, width), f)
    return pl.pallas_call(
        body, name=name, out_shape=jax.ShapeDtypeStruct((N_SHARD, half, width), jnp.bfloat16),
        grid_spec=pltpu.PrefetchScalarGridSpec(
            num_scalar_prefetch=1, grid=(N_SHARD, nt),
            in_specs=[blk(lambda s, i, c: (s, c[0] * nt + i, 0)), blk(lambda s, i, c: (s, i, 0))],
            out_specs=blk(lambda s, i, c: (s, i, 0))),
        compiler_params=_cparams(("parallel", "parallel")))(c_arr, gbuf, recv)


def _reduce_chip_exchange(parts, name):
    n = len(parts)

    def body(*refs):
        ins, outs, send_sems, recv_sems = refs[:n], refs[n:2 * n], refs[2 * n], refs[2 * n + 1]
        x, y, c, chips = _place()
        sends = [_remote(ins[b].at[2 * cx + cy], outs[b].at[j], send_sems, recv_sems, 3 * b + j, (cx, cy, c))
                 for b in range(n) for j, (cx, cy) in enumerate(chips)]
        for cp in sends:
            cp.start()
        for b in range(n):
            for j in range(len(chips)):
                _remote(ins[b].at[0], outs[b].at[j], send_sems, recv_sems, 3 * b + j, (x, y, c)).wait_recv()
        for cp in sends:
            cp.wait_send()

    return pl.pallas_call(
        body, name=name, out_shape=[jax.ShapeDtypeStruct((3,) + p.shape[1:], p.dtype) for p in parts],
        in_specs=[_HBM] * n, out_specs=[_HBM] * n,
        scratch_shapes=[pltpu.SemaphoreType.DMA((3 * n,)), pltpu.SemaphoreType.DMA((3 * n,))],
    )(*parts)


def _chip_add(part, recv, me_arr, tr, name):
    _, half, width = part.shape

    def body(me_ref, own, a0, a1, a2, o_ref):
        f = lambda r: r[...].astype(F32)
        o_ref[...] = ((f(own) + f(a0)) + f(a1)) + f(a2)

    specs = [pl.BlockSpec((None, tr, width), lambda i, me: (me[0], i, 0))]
    specs += [pl.BlockSpec((None, tr, width), functools.partial(lambda i, me, k: (k, i, 0), k=k)) for k in range(3)]
    return pl.pallas_call(
        body, name=name, out_shape=jax.ShapeDtypeStruct((half, width), F32),
        grid_spec=pltpu.PrefetchScalarGridSpec(
            num_scalar_prefetch=1, grid=(half // tr,), in_specs=specs,
            out_specs=pl.BlockSpec((tr, width), lambda i, me: (i, 0))),
        compiler_params=_cparams(("parallel",)))(me_arr, part, recv, recv, recv)


def _reduce_pair_share(rhalves, name):
    n = len(rhalves)

    def body(*refs):
        ins, outs, send_sems, recv_sems = refs[:n], refs[n:2 * n], refs[2 * n], refs[2 * n + 1]
        x, y, c, _ = _place()
        cps = [_remote(ins[b], outs[b], send_sems, recv_sems, b, (x, y, 1 - c)) for b in range(n)]
        for cp in cps:
            cp.start()
        for cp in cps:
            cp.wait()

    return pl.pallas_call(
        body, name=name, out_shape=[jax.ShapeDtypeStruct(r.shape, r.dtype) for r in rhalves],
        in_specs=[_HBM] * n, out_specs=[_HBM] * n,
        scratch_shapes=[pltpu.SemaphoreType.DMA((n,)), pltpu.SemaphoreType.DMA((n,))],
    )(*rhalves)


def _small_allreduce(buf):
    r, width = buf.shape
    n_dev = 8

    def body(x_ref, all_ref, sum_ref, send_sems, recv_sems, local_sem):
        x, y, c, chips = _place()
        me, sibling = (x, y, c), (x, y, 1 - c)

        def rows(px, py, pc):
            return all_ref.at[pl.ds(pl.multiple_of((4 * px + 2 * py + pc) * r, 8), r), :]

        def copy(k, block, to, src=None):
            return _remote(rows(*block) if src is None else src, rows(*block), send_sems, recv_sems, k, to)

        mine = pltpu.make_async_copy(x_ref, rows(*me), local_sem)
        mine.start()
        first = [copy(0, me, sibling, src=x_ref)]
        first += [copy(1 + j, me, (*chip, c), src=x_ref) for j, chip in enumerate(chips)]
        for cp in first:
            cp.start()
        passed = [copy(4 + j, (*chip, c), sibling) for j, chip in enumerate(chips)]
        for j, chip in enumerate(chips):
            copy(1 + j, (*chip, c), me).wait_recv()
            passed[j].start()
        copy(0, sibling, me).wait_recv()
        for j, chip in enumerate(chips):
            copy(4 + j, (*chip, 1 - c), me).wait_recv()
        for cp in first + passed:
            cp.wait_send()
        mine.wait()
        total = all_ref[0:r, :]
        for k in range(1, n_dev):
            total = total + all_ref[k * r:(k + 1) * r, :]
        sum_ref[...] = total

    vm = pl.BlockSpec(memory_space=pltpu.VMEM)
    _, total = pl.pallas_call(
        body, name="small_allreduce",
        out_shape=[jax.ShapeDtypeStruct((n_dev * r, width), buf.dtype), jax.ShapeDtypeStruct((r, width), buf.dtype)],
        in_specs=[vm], out_specs=[vm, vm],
        scratch_shapes=[pltpu.SemaphoreType.DMA((7,)), pltpu.SemaphoreType.DMA((7,)), pltpu.SemaphoreType.DMA],
    )(buf)
    return total


def _row_tile(rows, cap):
    if rows <= cap:
        return rows
    t = (cap // 8) * 8
    while t >= 8:
        if rows % t == 0:
            return t
        t -= 8
    return rows


def _adamw(w, g, m, v, name):
    shape = w.shape
    cols = shape[-1] if len(shape) <= 3 else shape[-2] * shape[-1]
    lead = len(shape) == 3
    w2, g2, m2, v2 = (a if lead else a.reshape(-1, cols) for a in (w, g, m, v))
    rows = shape[1] if lead else w2.shape[0]
    tr = _row_tile(rows, 256)

    def body(w_ref, g_ref, m_ref, v_ref, d_ref, mo_ref, vo_ref):
        gv = g_ref[...]
        mn = ADAM_B1 * m_ref[...] + (1.0 - ADAM_B1) * gv
        vn = ADAM_B2 * v_ref[...] + (1.0 - ADAM_B2) * (gv * gv)
        m_hat = mn / (1.0 - ADAM_B1 ** ADAM_STEP)
        v_hat = vn / (1.0 - ADAM_B2 ** ADAM_STEP)
        d_ref[...] = -ADAM_LR * (m_hat / (jnp.sqrt(v_hat) + ADAM_EPS) + ADAM_WD * w_ref[...])
        mo_ref[...] = mn
        vo_ref[...] = vn

    blk = pl.BlockSpec((None, tr, cols), lambda i: (0, i, 0)) if lead else pl.BlockSpec((tr, cols), lambda i: (i, 0))
    outs = pl.pallas_call(
        body, name=name, grid=(rows // tr,), in_specs=[blk] * 4, out_specs=[blk] * 3,
        out_shape=[jax.ShapeDtypeStruct(w2.shape, F32)] * 3, compiler_params=_cparams(("parallel",)))(w2, g2, m2, v2)
    return tuple(o.reshape(shape) for o in outs)


_WEIGHT_NAMES = ('w_in', 'conv_w', 'dn_a_log', 'dn_dt_bias', 'dn_norm_w', 'q_norm_w', 'w_uq', 'kv_norm_w', 'w_uk',
                 'w_uv', 'w_br_dn', 'w_br_mla', 'w_o', 'ln1_g', 'ln1_b', 'w_ffn_in', 'w_ffn_out', 'w_ple',
                 'w_ple_gate', 'ln2_g', 'ln2_b')
_SMALL_NAMES = ('ln1_g', 'ln1_b', 'ln2_g', 'ln2_b', 'q_norm_w', 'kv_norm_w', 'dn_norm_w', 'dn_a_log', 'dn_dt_bias')
_SMALL_GROUP = 8
_CONV_SMALL_ROW = len(_SMALL_NAMES) * _SMALL_GROUP
_CONV_SMALL_ROWS = DN_CONV * QKV_W // FLAT_W


def _pack_small(gw):
    rows = [jnp.pad(gw[n][None, :], ((0, _SMALL_GROUP - 1), (0, FLAT_W - gw[n].shape[0]))) for n in _SMALL_NAMES]
    rows.append(jnp.pad(gw['conv_w'].reshape(_CONV_SMALL_ROWS, FLAT_W), ((0, SMALL_ROWS - _CONV_SMALL_ROW - _CONV_SMALL_ROWS), (0, 0))))
    return jnp.concatenate(rows, axis=0)


class _Exchange:
    def __init__(self, local, me_chip, c_arr):
        self.local, self.me_chip, self.c_arr = local, me_chip, c_arr
        self.parts = self.arrived = None

    def gather_send(self):
        return _ride_gather_send(self.local)

    def gather_pass(self, sent):
        return _ride_gather_pass(sent)

    def weights(self, gathered):
        return _prep_weights(_unpack_rest(gathered, self.local, self.me_chip))

    def pair_send(self, g):
        self.gbufs = _pack_grads_rest(_unprep_grads_early(g))
        return _ride_pair_exchange(self.gbufs)

    def reduce_send(self, got):
        self.parts = [_pair_add(g_, r_, self.c_arr, tr, "pair_add_%d" % (i + 1))
                      for i, (g_, r_, tr) in enumerate(zip(self.gbufs, got, _ADD_TILES[1:]))]
        return _ride_chip_exchange(self.parts)

    def reduce_arrived(self, arrived):
        self.arrived = list(arrived)


def kernel(x, p, positions, w_in, conv_w, dn_a_log, dn_dt_bias, dn_norm_w, q_norm_w, w_uq, kv_norm_w, w_uk, w_uv, w_br_dn, w_br_mla, w_o, ln1_g, ln1_b, w_ffn_in, w_ffn_out, w_ple, w_ple_gate, ln2_g, ln2_b, loss_target, m_w_in, m_conv_w, m_dn_a_log, m_dn_dt_bias, m_dn_norm_w, m_q_norm_w, m_w_uq, m_kv_norm_w, m_w_uk, m_w_uv, m_w_br_dn, m_w_br_mla, m_w_o, m_ln1_g, m_ln1_b, m_w_ffn_in, m_w_ffn_out, m_w_ple, m_w_ple_gate, m_ln2_g, m_ln2_b, v_w_in, v_conv_w, v_dn_a_log, v_dn_dt_bias, v_dn_norm_w, v_q_norm_w, v_w_uq, v_kv_norm_w, v_w_uk, v_w_uv, v_w_br_dn, v_w_br_mla, v_w_o, v_ln1_g, v_ln1_b, v_w_ffn_in, v_w_ffn_out, v_w_ple, v_w_ple_gate, v_ln2_g, v_ln2_b):
    ws = dict(w_in=w_in, conv_w=conv_w, dn_a_log=dn_a_log, dn_dt_bias=dn_dt_bias, dn_norm_w=dn_norm_w, q_norm_w=q_norm_w,
              w_uq=w_uq, kv_norm_w=kv_norm_w, w_uk=w_uk, w_uv=w_uv, w_br_dn=w_br_dn, w_br_mla=w_br_mla, w_o=w_o,
              ln1_g=ln1_g, ln1_b=ln1_b, w_ffn_in=w_ffn_in, w_ffn_out=w_ffn_out, w_ple=w_ple, w_ple_gate=w_ple_gate,
              ln2_g=ln2_g, ln2_b=ln2_b)
    ms = dict(w_in=m_w_in, conv_w=m_conv_w, dn_a_log=m_dn_a_log, dn_dt_bias=m_dn_dt_bias, dn_norm_w=m_dn_norm_w,
              q_norm_w=m_q_norm_w, w_uq=m_w_uq, kv_norm_w=m_kv_norm_w, w_uk=m_w_uk, w_uv=m_w_uv, w_br_dn=m_w_br_dn,
              w_br_mla=m_w_br_mla, w_o=m_w_o, ln1_g=m_ln1_g, ln1_b=m_ln1_b, w_ffn_in=m_w_ffn_in, w_ffn_out=m_w_ffn_out,
              w_ple=m_w_ple, w_ple_gate=m_w_ple_gate, ln2_g=m_ln2_g, ln2_b=m_ln2_b)
    vs = dict(w_in=v_w_in, conv_w=v_conv_w, dn_a_log=v_dn_a_log, dn_dt_bias=v_dn_dt_bias, dn_norm_w=v_dn_norm_w,
              q_norm_w=v_q_norm_w, w_uq=v_w_uq, kv_norm_w=v_kv_norm_w, w_uk=v_w_uk, w_uv=v_w_uv, w_br_dn=v_w_br_dn,
              w_br_mla=v_w_br_mla, w_o=v_w_o, ln1_g=v_ln1_g, ln1_b=v_ln1_b, w_ffn_in=v_w_ffn_in, w_ffn_out=v_w_ffn_out,
              w_ple=v_w_ple, w_ple_gate=v_w_ple_gate, ln2_g=v_ln2_g, ln2_b=v_ln2_b)
    mx, my, mc = lax.axis_index("x"), lax.axis_index("y"), lax.axis_index("c")

    me_chip = 2 * mx + my
    c_arr = jnp.reshape(mc, (1,)).astype(jnp.int32)
    me_arr = jnp.reshape(me_chip, (1,)).astype(jnp.int32)
    sharded = ('w_in', 'w_ffn_in') + tuple(name for name, _, _ in _FLATB_PIECES)

    local = _pack_shards({name: ws[name][0] for name in sharded}, conv_w[0])
    (gathered_in,) = _gather_shards(local[:1], "gather_w_in")
    w_in_full, conv_full = _unpack_w_in(gathered_in, local[0], me_chip)
    small = {n: ws[n][0] for n in _SMALL_NAMES}
    small['conv_w'] = conv_full
    sp = _prep_small(small)
    cosb, sinb = _rope_tables(positions[0])
    exch = _Exchange(local[1:], me_chip, c_arr)

    loss_lanes, dx, g = _local_step(x[0], p[0, 0], cosb, sinb, loss_target[0], _prep_w_in(w_in_full), sp, exch)
    gw = _unprep_grads_late(g)
    loss = lax.psum(jnp.sum(loss_lanes), ("x", "y", "c"))

    g_in = [_shard_columns(gw['w_in'], W_IN_SHARD)]
    got = _reduce_pair_exchange(g_in, "reduce_pair_exchange_w_in")
    part_in = _pair_add(g_in[0], got[0], c_arr, _ADD_TILES[0], "pair_add_0")
    parts = [part_in] + exch.parts
    arrived = list(_reduce_chip_exchange([part_in], "reduce_chip_exchange_w_in")) + exch.arrived
    mine = [_chip_add(p_, r_, me_arr, tr, "chip_add_%d" % i) for i, (p_, r_, tr) in enumerate(zip(parts, arrived, _ADD_TILES))]
    reduced = _unpack_reduced(mine, _reduce_pair_share(mine, "reduce_pair_share"), mc)
    tot = _small_allreduce(_pack_small(gw))
    gred = {name: reduced[name][None] for name in sharded}
    for i, n in enumerate(_SMALL_NAMES):
        gred[n] = tot[i * _SMALL_GROUP, :ws[n].shape[1]][None]
    conv_tot = tot[_CONV_SMALL_ROW:_CONV_SMALL_ROW + _CONV_SMALL_ROWS].reshape(DN_CONV, QKV_W)
    gred['conv_w'] = lax.dynamic_slice_in_dim(conv_tot, (2 * mx + my) * _CONV_SHARD, _CONV_SHARD, axis=1)[None]

    deltas, new_m, new_v = {}, {}, {}
    for n in _WEIGHT_NAMES:
        gred[n] = gred[n].reshape(ws[n].shape)
        deltas[n], new_m[n], new_v[n] = _adamw(ws[n], gred[n], ms[n], vs[n], "adamw_" + n)
    return (loss, dx[None], *[gred[n] for n in _WEIGHT_NAMES], *[deltas[n] for n in _WEIGHT_NAMES],
            *[new_m[n] for n in _WEIGHT_NAMES], *[new_v[n] for n in _WEIGHT_NAMES])
```

```python
import functools
import math

import jax
import jax.numpy as jnp
from jax import lax
from jax.experimental import pallas as pl
from jax.experimental.pallas import tpu as pltpu

F32 = jnp.float32
_CDT = jnp.bfloat16
_HI = lax.Precision.HIGHEST
_MESH = pl.DeviceIdType.MESH

D_MODEL = 1024
PLE_DIM = 256
HEADS = 8
DN_DK = 128
DN_CHUNK = 64
DN_CONV = 4
QKV_W = 3 * HEADS * DN_DK
Q_LORA = 384
KV_LORA = 256
NOPE = 128
ROPE = 64
ROPE_PAD = 128
FFN_HIDDEN = 2816
D_IN = 6864
ROPE_BASE = 10000.0
ALPHA = 2.0 ** 0.25
ATT_SCALE = (NOPE + ROPE) ** -0.5
NEG_BIG = -1e30
ADAM_LR, ADAM_B1, ADAM_B2, ADAM_EPS, ADAM_WD, ADAM_STEP = 0.001, 0.9, 0.999, 1e-08, 0.01, 10

LANE = 128
VMEM_LIMIT = 56 * 1024 * 1024
MM_VMEM_BUDGET = 40 * 1024 * 1024
N_SHARD = 4
FLAT_W = 1024
SMALL_ROWS = 88


def _tile(dim, cap):
    if dim <= cap:
        return dim
    t = (cap // LANE) * LANE
    while t >= LANE:
        if dim % t == 0:
            return t
        t -= LANE
    return dim


def _cparams(sem):
    return pltpu.CompilerParams(dimension_semantics=sem, vmem_limit_bytes=VMEM_LIMIT)


def _mm(a, b, *, name, ta=False, tb=False, add=None, add_scale=1.0, out_dtype=F32, heads=None,
        a_head=None, b_head=None, out_head=None, dims=None, tm=1408, tn=1408):
    m, n, k = dims
    tm, tn = _tile(m, tm), _tile(n, tn)
    sa, sb, so = a.dtype.itemsize, b.dtype.itemsize, jnp.dtype(out_dtype).itemsize

    def vmem_need(tk_):
        acc = tm * tn * 4 if tk_ < k else 0
        extra = 2 * tm * tn * 4 if add is not None else 0
        return 2 * (tm * tk_ * sa + tk_ * tn * sb) + 2 * tm * tn * so + acc + extra

    tk = k
    while vmem_need(tk) > MM_VMEM_BUDGET and tk > LANE:
        smaller = _tile(k, tk - LANE)
        if smaller >= tk:
            break
        tk = smaller
    nk = k // tk
    hgrid = () if heads is None else (heads,)
    off = len(hgrid)

    def spec(rows, cols, rtile, ctile, rsel, csel, layout):
        def idx(*g):
            h = g[0] if off else 0
            ri, ci = g[off + rsel], g[off + csel]
            if layout == 'lead':
                return (h, ri, ci)
            if layout == 'col':
                return (ri, h * (cols // ctile) + ci)
            return (ri, ci)
        if layout == 'lead':
            return pl.BlockSpec((None, rtile, ctile), idx)
        return pl.BlockSpec((rtile, ctile), idx)

    a_spec = spec(k, m, tk, tm, 2, 0, a_head) if ta else spec(m, k, tm, tk, 0, 2, a_head)
    b_spec = spec(n, k, tn, tk, 1, 2, b_head) if tb else spec(k, n, tk, tn, 2, 1, b_head)
    o_spec = spec(m, n, tm, tn, 0, 1, out_head)
    in_specs = [a_spec, b_spec]
    args = [a, b]
    if add is not None:
        in_specs.append(spec(m, n, tm, tn, 0, 1, out_head))
        args.append(add)
    dn = (((0 if ta else 1,), (1 if tb else 0,)), ((), ()))

    def body(*refs):
        a_ref, b_ref = refs[0], refs[1]
        prod = lax.dot_general(a_ref[...].astype(_CDT), b_ref[...].astype(_CDT), dn, preferred_element_type=F32)
        if nk == 1:
            o_ref = refs[-1]
            if add is not None:
                prod = prod + refs[2][...].astype(F32) * add_scale
            o_ref[...] = prod.astype(out_dtype)
            return
        o_ref, acc_ref = refs[-2], refs[-1]
        kk = pl.program_id(off + 2)

        @pl.when(kk == 0)
        def _():
            if add is not None:
                acc_ref[...] = refs[2][...].astype(F32) * add_scale
            else:
                acc_ref[...] = jnp.zeros_like(acc_ref)

        acc_ref[...] += prod

        @pl.when(kk == nk - 1)
        def _():
            o_ref[...] = acc_ref[...].astype(out_dtype)

    if out_head == 'lead':
        oshape = (heads, m, n)
    elif out_head == 'col':
        oshape = (m, heads * n)
    else:
        oshape = (m, n)
    sem = ("parallel",) * (off + 2) + ("arbitrary",)
    return pl.pallas_call(
        body, name=name, grid=hgrid + (m // tm, n // tn, nk), in_specs=in_specs, out_specs=o_spec,
        out_shape=jax.ShapeDtypeStruct(oshape, out_dtype),
        scratch_shapes=[pltpu.VMEM((tm, tn), F32)] if nk > 1 else [],
        compiler_params=_cparams(sem))(*args)


def _mm2(a, b, **kw):
    ta, tb = kw.get('ta', False), kw.get('tb', False)
    m = a.shape[1] if ta else a.shape[0]
    k = a.shape[0] if ta else a.shape[1]
    n = b.shape[0] if tb else b.shape[1]
    return _mm(a, b, dims=(m, n, k), **kw)


def _rowwise(fn, rows, bcast, outs, reds=(), *, name, tm=256, heads=None):
    t = rows[0][0].shape[0]
    tm = min(tm, t)
    hn = 1 if heads is None else heads
    in_specs, args = [], []
    for arr, width, base, per_head in rows:
        in_specs.append(pl.BlockSpec((tm, width), functools.partial(
            lambda i, h, base, per_head: (i, base + (h if per_head else 0)), base=base, per_head=per_head)))
        args.append(arr)
    for arr in bcast:
        in_specs.append(pl.BlockSpec(arr.shape, lambda i, h: (0, 0)))
        args.append(arr)
    out_specs, out_shape = [], []
    for total, width, per_head, dt in outs:
        out_specs.append(pl.BlockSpec((tm, width), functools.partial(
            lambda i, h, per_head: (i, h if per_head else 0), per_head=per_head)))
        out_shape.append(jax.ShapeDtypeStruct((t, total), dt))
    for shp in reds:
        out_specs.append(pl.BlockSpec(shp, lambda i, h: (0, 0)))
        out_shape.append(jax.ShapeDtypeStruct(shp, F32))
    n_in, n_out, n_red = len(args), len(outs), len(reds)

    def body(*refs):
        i, h = pl.program_id(0), pl.program_id(1)
        vals = fn(h, *[r[...] for r in refs[:n_in]])
        for r, v in zip(refs[n_in:n_in + n_out], vals[:n_out]):
            r[...] = v.astype(r.dtype)
        if n_red:
            @pl.when((i == 0) & (h == 0))
            def _():
                for r in refs[n_in + n_out:]:
                    r[...] = jnp.zeros_like(r)
            for r, v in zip(refs[n_in + n_out:], vals[n_out:]):
                r[...] += v

    sem = ("arbitrary", "arbitrary") if n_red else ("parallel", "parallel")
    res = pl.pallas_call(body, name=name, grid=(t // tm, hn), in_specs=in_specs, out_specs=out_specs,
                         out_shape=out_shape, compiler_params=_cparams(sem))(*args)
    return tuple(res)


def _sigmoid(x):
    return 1.0 / (1.0 + jnp.exp(-x))


def _silu(x):
    return x * _sigmoid(x)


def _softplus(x):
    return jnp.maximum(x, 0.0) + jnp.log(1.0 + jnp.exp(-jnp.abs(x)))


def _layer_norm(t, g, b):
    mu = jnp.mean(t, axis=-1, keepdims=True)
    d = t - mu
    var = jnp.mean(d * d, axis=-1, keepdims=True)
    return d * lax.rsqrt(var + 1e-5) * g + b


def _rms_norm(t, w):
    return t * lax.rsqrt(jnp.mean(t * t, axis=-1, keepdims=True) + 1e-6) * w


def _swap_rope_halves(t):
    lane = lax.broadcasted_iota(jnp.int32, t.shape, 1) % ROPE_PAD
    n = t.shape[1]
    up = pltpu.roll(t, n - ROPE // 2, axis=1)
    dn = pltpu.roll(t, ROPE // 2, axis=1)
    return jnp.where(lane < ROPE // 2, up, jnp.where(lane < ROPE, dn, 0.0))


def _rope(t, cosb, sinb):
    reps = t.shape[1] // ROPE_PAD
    c = jnp.tile(cosb, (1, reps)) if reps > 1 else cosb
    s = jnp.tile(sinb, (1, reps)) if reps > 1 else sinb
    return t * c + _swap_rope_halves(t) * s


def _rope_bwd(d, cosb, sinb):
    reps = d.shape[1] // ROPE_PAD
    c = jnp.tile(cosb, (1, reps)) if reps > 1 else cosb
    s = jnp.tile(sinb, (1, reps)) if reps > 1 else sinb
    return d * c + _swap_rope_halves(d * s)


_CONV_ROWS = 256
_CONV_COLS = 256


def _conv_window(ref, r0, lo, hi, t):
    parts = []
    start, stop = r0 - lo, r0 + _CONV_ROWS + hi
    if start < 0:
        parts.append(jnp.zeros((-start, ref.shape[1]), F32))
        start = 0
    tail = max(stop - t, 0)
    parts.append(ref[start:stop - tail, :].astype(F32))
    if tail:
        parts.append(jnp.zeros((tail, ref.shape[1]), F32))
    return parts[0] if len(parts) == 1 else jnp.concatenate(parts, axis=0)


def _conv_taps(win, w_ref, n_out):
    acc = win[8:8 + n_out] * w_ref[DN_CONV - 1:DN_CONV, :]
    for i in range(DN_CONV - 1):
        acc = acc + pltpu.roll(win, DN_CONV - 1 - i, axis=0)[8:8 + n_out] * w_ref[i:i + 1, :]
    return acc


def _conv_silu(x, w):
    t, ch = x.shape

    def body(x_ref, w_ref, o_ref):
        for r in range(t // _CONV_ROWS):
            r0 = r * _CONV_ROWS
            c = _conv_taps(_conv_window(x_ref, r0, 8, 0, t), w_ref, _CONV_ROWS)
            o_ref[r0:r0 + _CONV_ROWS, :] = _silu(c)

    return pl.pallas_call(
        body, name="conv_silu", grid=(ch // _CONV_COLS,),
        in_specs=[pl.BlockSpec((t, _CONV_COLS), lambda j: (0, j)), pl.BlockSpec((DN_CONV, _CONV_COLS), lambda j: (0, j))],
        out_specs=pl.BlockSpec((t, _CONV_COLS), lambda j: (0, j)),
        out_shape=jax.ShapeDtypeStruct((t, ch), F32), compiler_params=_cparams(("parallel",)))(x, w)


def _conv_silu_bwd(x, w, dys):
    t, ch = x.shape
    per = ch // len(dys) // _CONV_COLS

    def body(x_ref, w_ref, *rest):
        dy_refs, (dx_ref, dw_ref) = rest[:len(dys)], rest[len(dys):]
        sec = pl.program_id(0) // per
        dws = [jnp.zeros((1, _CONV_COLS), F32) for _ in range(DN_CONV)]
        for r in range(t // _CONV_ROWS):
            r0 = r * _CONV_ROWS
            n_ext = _CONV_ROWS + 8
            xw = _conv_window(x_ref, r0, 8, 8, t)
            c = _conv_taps(xw, w_ref, n_ext)
            sg = _sigmoid(c)
            dy = _conv_window(dy_refs[-1], r0, 0, 8, t)
            for k in range(len(dys) - 2, -1, -1):
                dy = jnp.where(sec == k, _conv_window(dy_refs[k], r0, 0, 8, t), dy)
            ds = dy * (sg * (1.0 + c * (1.0 - sg)))
            dx = ds[:_CONV_ROWS] * w_ref[DN_CONV - 1:DN_CONV, :]
            for i in range(DN_CONV - 1):
                sh = DN_CONV - 1 - i
                dx = dx + pltpu.roll(ds, n_ext - sh, axis=0)[:_CONV_ROWS] * w_ref[i:i + 1, :]
            dx_ref[r0:r0 + _CONV_ROWS, :] = dx.astype(dx_ref.dtype)
            ds0 = ds[:_CONV_ROWS]
            for i in range(DN_CONV):
                sh = DN_CONV - 1 - i
                xs = xw if sh == 0 else pltpu.roll(xw, sh, axis=0)
                dws[i] = dws[i] + jnp.sum(ds0 * xs[8:8 + _CONV_ROWS], axis=0, keepdims=True)
        for i in range(DN_CONV):
            dw_ref[i:i + 1, :] = dws[i]

    blk = pl.BlockSpec((t, _CONV_COLS), lambda j: (0, j))
    wblk = pl.BlockSpec((DN_CONV, _CONV_COLS), lambda j: (0, j))
    dy_specs = [pl.BlockSpec((t, _CONV_COLS), functools.partial(lambda j, k: (0, jnp.clip(j - k * per, 0, per - 1)), k=k))
                for k in range(len(dys))]
    return pl.pallas_call(
        body, name="conv_silu_bwd", grid=(ch // _CONV_COLS,), in_specs=[blk, wblk] + dy_specs, out_specs=[blk, wblk],
        out_shape=[jax.ShapeDtypeStruct((t, ch), _CDT), jax.ShapeDtypeStruct((DN_CONV, ch), F32)],
        compiler_params=_cparams(("arbitrary",)))(x, w, *dys)


_PA_ROWS = 512


def _bmm(a, b, spec, exact=False):
    if exact:
        return jnp.einsum(spec, a, b, precision=_HI, preferred_element_type=F32)
    return jnp.einsum(spec, a.astype(_CDT), b.astype(_CDT), preferred_element_type=F32)


def _split16(a):
    hi = a.astype(jnp.bfloat16)
    return hi, (a - hi.astype(F32)).astype(jnp.bfloat16)


def _bmm3(a, b, spec):
    ah, al = _split16(a)
    bh, bl = _split16(b)
    e = lambda p, q: jnp.einsum(spec, p, q, preferred_element_type=F32)
    return e(ah, bh) + (e(ah, bl) + e(al, bh))


def _tri_inverse(l_mat, eye):
    pw = -l_mat
    t_inv = eye + pw
    for _ in range(5):
        pw = _bmm3(pw, pw, 'bij,bjk->bik')
        t_inv = t_inv + _bmm3(t_inv, pw, 'bij,bjk->bik')
    return t_inv


@jax.custom_vjp
def _tri_inverse_saved(l_mat, t_saved):
    return t_saved


def _tri_inverse_saved_fwd(l_mat, t_saved):
    return t_saved, t_saved


def _tri_inverse_saved_bwd(t_saved, dt):
    left = _bmm3(t_saved, dt, 'bji,bjk->bik')
    return -_bmm3(left, t_saved, 'bij,bkj->bik'), jnp.zeros_like(t_saved)


_tri_inverse_saved.defvjp(_tri_inverse_saved_fwd, _tri_inverse_saved_bwd)


def _phase_a(h, q, k, v, ba, alog, dtb, t_saved=None):
    r = q.shape[0]
    nb = r // DN_CHUNK
    c = DN_CHUNK
    lane = lax.broadcasted_iota(jnp.int32, (1, LANE), 1)
    selb = (lane == h).astype(F32)
    sela = (lane == h + HEADS).astype(F32)
    b_raw = jnp.sum(ba * selb, axis=1, keepdims=True)
    a_raw = jnp.sum(ba * sela, axis=1, keepdims=True)
    al = jnp.sum(alog * selb, axis=1, keepdims=True)
    dt = jnp.sum(dtb * selb, axis=1, keepdims=True)
    beta = jnp.broadcast_to(_sigmoid(b_raw), (r, LANE))
    g = jnp.broadcast_to(-jnp.exp(al) * _softplus(a_raw + dt), (r, LANE))
    qn = q * lax.rsqrt(jnp.sum(q * q, -1, keepdims=True) + 1e-6) * (DN_DK ** -0.5)
    kn = k * lax.rsqrt(jnp.sum(k * k, -1, keepdims=True) + 1e-6)
    q3, k3, v3 = qn.reshape(nb, c, LANE), kn.reshape(nb, c, LANE), v.reshape(nb, c, LANE)
    b3, g3 = beta.reshape(nb, c, LANE), g.reshape(nb, c, LANE)
    ri = lax.broadcasted_iota(jnp.int32, (nb, c, c), 1)
    ci = lax.broadcasted_iota(jnp.int32, (nb, c, c), 2)
    tril, strict = ri >= ci, ri > ci
    gc = _bmm(tril.astype(F32), g3, 'bij,bjd->bid', exact=True)
    onehot = (lax.broadcasted_iota(jnp.int32, (nb, c, LANE), 2) == 0).astype(F32)
    g_row = _bmm(onehot, gc, 'bid,bjd->bij', exact=True)
    diff = gc[:, :, :c] - g_row
    decay = jnp.where(tril, jnp.exp(jnp.where(tril, diff, 0.0)), 0.0)
    kb = k3 * b3
    l_mat = jnp.where(strict, _bmm(kb, k3, 'bid,bjd->bij') * decay, 0.0)
    if t_saved is None:
        t_inv = _tri_inverse(l_mat, (ri == ci).astype(F32))
    else:
        t_inv = _tri_inverse_saved(l_mat, t_saved.reshape(nb, c, c))
    eg = jnp.exp(gc)
    u = _bmm(t_inv, v3 * b3, 'bij,bje->bie')
    w = _bmm(t_inv, kb * eg, 'bij,bje->bie')
    intra = jnp.where(tril, _bmm(q3, k3, 'bid,bjd->bij') * decay, 0.0)
    qd = q3 * eg
    gl = jnp.sum(g3, axis=1, keepdims=True)
    kt = k3 * jnp.exp(gl - gc)
    outs = (u.reshape(r, LANE), w.reshape(r, LANE), qd.reshape(r, LANE), kt.reshape(r, LANE),
            intra.reshape(r, c), gl.reshape(nb, LANE))
    return outs + (t_inv.reshape(r, c),) if t_saved is None else outs


def _pa_specs(t):
    rr = min(_PA_ROWS, t)
    nb = rr // DN_CHUNK
    qkv = [pl.BlockSpec((rr, LANE), functools.partial(lambda i, h, o: (i, o + h), o=o)) for o in (0, HEADS, 2 * HEADS)]
    ba = pl.BlockSpec((rr, LANE), lambda i, h: (i, 3))
    vec = pl.BlockSpec((1, LANE), lambda i, h: (0, 0))
    row = pl.BlockSpec((rr, LANE), lambda i, h: (i, h))
    intra = pl.BlockSpec((None, rr, DN_CHUNK), lambda i, h: (h, i, 0))
    gl = pl.BlockSpec((nb, LANE), lambda i, h: (i, h))
    return rr, qkv, ba, vec, row, intra, gl


def _grid_ends(grid):
    first = lambda: functools.reduce(jnp.logical_and, [pl.program_id(a) == 0 for a in range(len(grid))])
    last = lambda: functools.reduce(jnp.logical_and, [pl.program_id(a) == n - 1 for a, n in enumerate(grid)])
    return first, last


def _delta_local(qkv_act, pm, alog, dtb, rider=None):
    t = qkv_act.shape[0]
    rr, qkv, ba, vec, row, intra, gl = _pa_specs(t)

    def body(q, k, v, b, al, dt, *outs):
        vals = _phase_a(pl.program_id(1), q[...], k[...], v[...], b[...], al[...], dt[...])
        for o, val in zip(outs, vals):
            o[...] = val

    wide = jax.ShapeDtypeStruct((t, HEADS * LANE), F32)
    sq = jax.ShapeDtypeStruct((HEADS, t, DN_CHUNK), F32)
    grid = (t // rr, HEADS)
    return _carried_call(
        body, rider, *_grid_ends(grid), name="delta_local", grid=grid, in_specs=qkv + [ba, vec, vec],
        out_specs=[row] * 4 + [intra, gl, intra],
        out_shape=[wide] * 4 + [sq, jax.ShapeDtypeStruct((t // DN_CHUNK, HEADS * LANE), F32), sq],
        scratch_shapes=[], sem=("arbitrary", "arbitrary"), args=(qkv_act, qkv_act, qkv_act, pm, alog, dtb))


def _delta_local_bwd(qkv_act, pm, alog, dtb, t_inv, du, dw, dqd, dkt, dintra, dgl, rider=None):
    t = qkv_act.shape[0]
    rr, qkv, ba, vec, row, intra, gl = _pa_specs(t)

    def body(q, k, v, b, al, dt, ti, du_r, dw_r, dqd_r, dkt_r, di_r, dgl_r, dq_o, dk_o, dv_o, dba_o, dal_o, ddt_o):
        i, h = pl.program_id(0), pl.program_id(1)
        t_saved = ti[...]
        _, vjp = jax.vjp(lambda *a: _phase_a(h, *a, t_saved=t_saved), q[...], k[...], v[...], b[...], al[...], dt[...])
        dq, dk, dv, dba, dal, ddt = vjp((du_r[...], dw_r[...], dqd_r[...], dkt_r[...], di_r[...], dgl_r[...]))
        dq_o[...], dk_o[...], dv_o[...] = dq, dk, dv

        @pl.when(h == 0)
        def _():
            dba_o[...] = jnp.zeros_like(dba_o)

        @pl.when((h == 0) & (i == 0))
        def _():
            dal_o[...] = jnp.zeros_like(dal_o)
            ddt_o[...] = jnp.zeros_like(ddt_o)

        dba_o[...] += dba
        dal_o[...] += dal
        ddt_o[...] += ddt

    wide = jax.ShapeDtypeStruct((t, HEADS * LANE), F32)
    vshape = jax.ShapeDtypeStruct((1, LANE), F32)
    grid = (t // rr, HEADS)
    return _carried_call(
        body, rider, *_grid_ends(grid), name="delta_local_bwd", grid=grid,
        in_specs=qkv + [ba, vec, vec, intra] + [row] * 4 + [intra, gl],
        out_specs=[row] * 3 + [pl.BlockSpec((rr, LANE), lambda i, h: (i, 0)), vec, vec],
        out_shape=[wide] * 3 + [jax.ShapeDtypeStruct((t, LANE), F32), vshape, vshape],
        scratch_shapes=[], sem=("arbitrary", "arbitrary"),
        args=(qkv_act, qkv_act, qkv_act, pm, alog, dtb, t_inv, du, dw, dqd, dkt, dintra, dgl))


_SCAN_ROWS = 512


def _dot(a, b, dn):
    return lax.dot_general(a.astype(_CDT), b.astype(_CDT), (dn, ((), ())), preferred_element_type=F32)


_NN = ((1,), (0,))
_NT = ((1,), (1,))
_TN = ((0,), (0,))


def _delta_scan(u, w, qd, kt, intra, gl, rider=None):
    t = u.shape[0]
    rr = min(_SCAN_ROWS, t)
    nc = rr // DN_CHUNK

    def body(u_ref, w_ref, qd_ref, kt_ref, a_ref, gl_ref, o_ref, sall_ref, s_scr):
        @pl.when(pl.program_id(0) == 0)
        def _():
            s_scr[...] = jnp.zeros_like(s_scr)

        def chunk(c, carry):
            r0 = pl.multiple_of(c * DN_CHUNK, DN_CHUNK)
            rows = pl.ds(r0, DN_CHUNK)
            e = jnp.exp(gl_ref[pl.ds(c, 1), :])
            states = [s_scr[h] for h in range(HEADS)]
            u_c, w_c, qd_c, kt_c = u_ref[rows, :], w_ref[rows, :], qd_ref[rows, :], kt_ref[rows, :]
            a_c = [a_ref[h, rows, :] for h in range(HEADS)]
            o_new, s_new = [], []
            for h in range(HEADS):
                cs = slice(h * LANE, (h + 1) * LANE)
                s = states[h]
                v_new = u_c[:, cs] - _dot(w_c[:, cs], s, _NN)
                o_new.append(_dot(qd_c[:, cs], s, _NN) + _dot(a_c[h], v_new, _NN))
                s_new.append(s * e[:, cs] + _dot(kt_c[:, cs], v_new, _TN))
            o_ref[rows, :] = jnp.concatenate(o_new, axis=1)
            for h in range(HEADS):
                sall_ref[c, h] = states[h]
                s_scr[h] = s_new[h]
            return carry

        lax.fori_loop(0, nc, chunk, 0)

    row = pl.BlockSpec((rr, HEADS * LANE), lambda i: (i, 0))
    grid = (t // rr,)
    return _carried_call(
        body, rider, *_grid_ends(grid), name="delta_scan", grid=grid,
        in_specs=[row] * 4 + [pl.BlockSpec((HEADS, rr, DN_CHUNK), lambda i: (0, i, 0)),
                              pl.BlockSpec((nc, HEADS * LANE), lambda i: (i, 0))],
        out_specs=[row, pl.BlockSpec((nc, HEADS, LANE, LANE), lambda i: (i, 0, 0, 0))],
        out_shape=[jax.ShapeDtypeStruct((t, HEADS * LANE), F32),
                   jax.ShapeDtypeStruct((t // DN_CHUNK, HEADS, LANE, LANE), F32)],
        scratch_shapes=[pltpu.VMEM((HEADS, LANE, LANE), F32)], sem=("arbitrary",), args=(u, w, qd, kt, intra, gl))


def _delta_scan_bwd(u, w, qd, kt, intra, gl, sall, do, rider=None):
    t = u.shape[0]
    rr = min(_SCAN_ROWS, t)
    nc = rr // DN_CHUNK
    ng = t // rr

    def body(u_ref, w_ref, qd_ref, kt_ref, a_ref, gl_ref, sall_ref, do_ref,
             du_ref, dw_ref, dqd_ref, dkt_ref, da_ref, dgl_ref, ds_scr):
        @pl.when(pl.program_id(0) == 0)
        def _():
            ds_scr[...] = jnp.zeros_like(ds_scr)

        def chunk(cc, carry):
            c = nc - 1 - cc
            r0 = pl.multiple_of(c * DN_CHUNK, DN_CHUNK)
            rows = pl.ds(r0, DN_CHUNK)
            e = jnp.exp(gl_ref[pl.ds(c, 1), :])
            states = [sall_ref[c, h] for h in range(HEADS)]
            ds_outs = [ds_scr[h] for h in range(HEADS)]
            u_a, w_a, kt_a, qd_a, do_a = u_ref[rows, :], w_ref[rows, :], kt_ref[rows, :], qd_ref[rows, :], do_ref[rows, :]
            a_a = [a_ref[h, rows, :] for h in range(HEADS)]
            da, dqd, dkt, du, dw, dgl, ds_new = [], [], [], [], [], [], []
            for h in range(HEADS):
                cs = slice(h * LANE, (h + 1) * LANE)
                s, ds_out = states[h], ds_outs[h]
                w_c, kt_c, qd_c, do_c = w_a[:, cs], kt_a[:, cs], qd_a[:, cs], do_a[:, cs]
                v_new = u_a[:, cs] - _dot(w_c, s, _NN)
                dv_new = _dot(a_a[h], do_c, _TN) + _dot(kt_c, ds_out, _NN)
                da.append(_dot(do_c, v_new, _NT))
                dqd.append(_dot(do_c, s, _NT))
                dkt.append(_dot(v_new, ds_out, _NT))
                du.append(dv_new)
                dw.append(-_dot(dv_new, s, _NT))
                eh = e[:, cs]
                dgl.append(jnp.broadcast_to(jnp.sum(ds_out * s, axis=0, keepdims=True) * eh, (8, LANE)))
                ds_new.append(ds_out * eh + _dot(qd_c, do_c, _TN) - _dot(w_c, dv_new, _TN))
            cat = lambda parts: jnp.concatenate(parts, axis=1)
            dqd_ref[rows, :], dkt_ref[rows, :], du_ref[rows, :], dw_ref[rows, :] = cat(dqd), cat(dkt), cat(du), cat(dw)
            dgl_ref[pl.ds(pl.multiple_of(c * 8, 8), 8), :] = cat(dgl)
            for h in range(HEADS):
                da_ref[h, rows, :] = da[h]
                ds_scr[h] = ds_new[h]
            return carry

        lax.fori_loop(0, nc, chunk, 0)

    rev = lambda i: (ng - 1 - i, 0)
    row = pl.BlockSpec((rr, HEADS * LANE), rev)
    a_spec = pl.BlockSpec((HEADS, rr, DN_CHUNK), lambda i: (0, ng - 1 - i, 0))
    gl_spec = pl.BlockSpec((nc, HEADS * LANE), rev)
    wide = jax.ShapeDtypeStruct((t, HEADS * LANE), F32)
    outs, carried = _carried_call(
        body, rider, *_grid_ends((ng,)), name="delta_scan_bwd", grid=(ng,),
        in_specs=[row] * 4 + [a_spec, gl_spec, pl.BlockSpec((nc, HEADS, LANE, LANE), lambda i: (ng - 1 - i, 0, 0, 0)), row],
        out_specs=[row] * 4 + [a_spec, pl.BlockSpec((nc * 8, HEADS * LANE), rev)],
        out_shape=[wide] * 4 + [jax.ShapeDtypeStruct((HEADS, t, DN_CHUNK), F32),
                                jax.ShapeDtypeStruct((t // DN_CHUNK * 8, HEADS * LANE), F32)],
        scratch_shapes=[pltpu.VMEM((HEADS, LANE, LANE), F32)], sem=("arbitrary",), args=(u, w, qd, kt, intra, gl, sall, do))
    return tuple(outs[:5]) + (outs[5].reshape(t // DN_CHUNK, 8, HEADS * LANE)[:, 0, :],), carried


_ATT_TILE = 512


def _kv_rows(j, tk):
    return pl.ds(pl.multiple_of(j * tk, tk), tk)


def _att_scores(ql, qr, ckv_ref, kr_ref, j, tk):
    ks = _kv_rows(j, tk)
    return (_dot(ql, ckv_ref[ks, :], _NT) + _dot(qr, kr_ref[ks, :], _NT)) * ATT_SCALE


def _diag_mask(s):
    qi = lax.broadcasted_iota(jnp.int32, s.shape, 0)
    ki = lax.broadcasted_iota(jnp.int32, s.shape, 1)
    return jnp.where(ki <= qi, s, NEG_BIG)


def _attention(ql, qr, ckv, kr):
    t = ckv.shape[0]
    tq = min(_ATT_TILE, t)

    def body(ql_ref, qr_ref, ckv_ref, kr_ref, o_ref, lse_ref, m_scr, l_scr, acc_scr, s_scr):
        qi = pl.program_id(1)
        q_lat, q_rope = ql_ref[...], qr_ref[...]
        m_scr[...] = jnp.full_like(m_scr, NEG_BIG)
        l_scr[...] = jnp.zeros_like(l_scr)
        acc_scr[...] = jnp.zeros_like(acc_scr)

        def consume(j, s):
            m_old = m_scr[...]
            m_new = jnp.maximum(m_old, jnp.max(s, axis=-1, keepdims=True))
            p = jnp.exp(s - m_new)
            alpha = jnp.exp(m_old - m_new)
            l_scr[...] = l_scr[...] * alpha + jnp.sum(p, axis=-1, keepdims=True)
            acc_scr[...] = acc_scr[...] * alpha + _dot(p, ckv_ref[_kv_rows(j, tq), :], _NN)
            m_scr[...] = m_new

        s_scr[0] = _att_scores(q_lat, q_rope, ckv_ref, kr_ref, 0, tq)

        def loop_body(j, carry):
            s_scr[(j + 1) % 2] = _att_scores(q_lat, q_rope, ckv_ref, kr_ref, j + 1, tq)
            consume(j, s_scr[j % 2])
            return carry

        lax.fori_loop(0, qi, loop_body, 0)
        consume(qi, _diag_mask(s_scr[qi % 2]))
        o_ref[...] = acc_scr[...] / l_scr[...]
        lse_ref[...] = m_scr[...] + jnp.log(l_scr[...])

    return pl.pallas_call(
        body, name="attention", grid=(HEADS, t // tq),
        in_specs=[pl.BlockSpec((None, tq, KV_LORA), lambda h, i: (h, i, 0)),
                  pl.BlockSpec((tq, ROPE_PAD), lambda h, i: (i, h)),
                  pl.BlockSpec((t, KV_LORA), lambda h, i: (0, 0)),
                  pl.BlockSpec((t, ROPE_PAD), lambda h, i: (0, 0))],
        out_specs=[pl.BlockSpec((None, tq, KV_LORA), lambda h, i: (h, i, 0)),
                   pl.BlockSpec((None, tq, 1), lambda h, i: (h, i, 0))],
        out_shape=[jax.ShapeDtypeStruct((HEADS, t, KV_LORA), F32), jax.ShapeDtypeStruct((HEADS, t, 1), F32)],
        scratch_shapes=[pltpu.VMEM((tq, 1), F32), pltpu.VMEM((tq, 1), F32), pltpu.VMEM((tq, KV_LORA), F32),
                        pltpu.VMEM((2, tq, tq), F32)],
        compiler_params=_cparams(("parallel", "parallel")))(ql, qr, ckv, kr)


def _attention_bwd(ql, qr, ckv, kr, out, lse, dout):
    t = ckv.shape[0]
    tq = min(_ATT_TILE, t)

    def body(ql_ref, qr_ref, ckv_ref, kr_ref, o_ref, lse_ref, do_ref, dql_ref, dqr_ref, dckv_ref, dkr_ref,
             dql_scr, dqr_scr):
        h, qi = pl.program_id(0), pl.program_id(1)

        @pl.when((h == 0) & (qi == 0))
        def _():
            dckv_ref[...] = jnp.zeros_like(dckv_ref)
            dkr_ref[...] = jnp.zeros_like(dkr_ref)

        q_lat, q_rope = ql_ref[...], qr_ref[...]
        d_o = do_ref[...].astype(_CDT)
        lse_v = lse_ref[...]
        dsum = jnp.sum(do_ref[...] * o_ref[...], axis=-1, keepdims=True)
        dql_scr[...] = jnp.zeros_like(dql_scr)
        dqr_scr[...] = jnp.zeros_like(dqr_scr)

        def step(j, masked):
            ks = _kv_rows(j, tq)
            s = _att_scores(q_lat, q_rope, ckv_ref, kr_ref, j, tq)
            if masked:
                s = _diag_mask(s)
            p = jnp.exp(s - lse_v)
            kv = ckv_ref[ks, :]
            ds = (p * (_dot(d_o, kv, _NT) - dsum) * ATT_SCALE).astype(_CDT)
            pb = p.astype(_CDT)
            dql_scr[...] += _dot(ds, kv, _NN)
            dqr_scr[...] += _dot(ds, kr_ref[ks, :], _NN)
            dckv_ref[ks, :] += _dot(pb, d_o, _TN) + _dot(ds, q_lat, _TN)
            dkr_ref[ks, :] += _dot(ds, q_rope, _TN)

        def loop_body(j, carry):
            step(j, False)
            return carry

        lax.fori_loop(0, qi, loop_body, 0)
        step(qi, True)
        dql_ref[...] = dql_scr[...].astype(dql_ref.dtype)
        dqr_ref[...] = dqr_scr[...]

    lat = pl.BlockSpec((None, tq, KV_LORA), lambda h, i: (h, i, 0))
    rope = pl.BlockSpec((tq, ROPE_PAD), lambda h, i: (i, h))
    kfull = pl.BlockSpec((t, KV_LORA), lambda h, i: (0, 0))
    rfull = pl.BlockSpec((t, ROPE_PAD), lambda h, i: (0, 0))
    return pl.pallas_call(
        body, name="attention_bwd", grid=(HEADS, t // tq),
        in_specs=[lat, rope, kfull, rfull, lat, pl.BlockSpec((None, tq, 1), lambda h, i: (h, i, 0)), lat],
        out_specs=[lat, rope, kfull, rfull],
        out_shape=[jax.ShapeDtypeStruct((HEADS, t, KV_LORA), _CDT), jax.ShapeDtypeStruct((t, HEADS * ROPE_PAD), F32),
                   jax.ShapeDtypeStruct((t, KV_LORA), F32), jax.ShapeDtypeStruct((t, ROPE_PAD), F32)],
        scratch_shapes=[pltpu.VMEM((tq, KV_LORA), F32), pltpu.VMEM((tq, ROPE_PAD), F32)],
        compiler_params=_cparams(("arbitrary", "arbitrary")))(ql, qr, ckv, kr, out, lse, dout)


def _ffn_in_swiglu(a, w):
    t, k = a.shape
    hid = w.shape[1] // 2
    tm, tn = _tile(t, 1024), _tile(hid, 1408)
    nj = hid // tn

    def body(a_ref, bg_ref, bu_ref, act_ref, gt_ref, up_ref):
        av = a_ref[...].astype(_CDT)
        gt = jnp.dot(av, bg_ref[...].astype(_CDT), preferred_element_type=F32)
        up = jnp.dot(av, bu_ref[...].astype(_CDT), preferred_element_type=F32)
        act_ref[...] = _swiglu(gt, up).astype(act_ref.dtype)
        gt_ref[...] = gt.astype(gt_ref.dtype)
        up_ref[...] = up.astype(up_ref.dtype)

    out = pl.BlockSpec((tm, tn), lambda i, j: (i, j))
    return pl.pallas_call(
        body, name="f_ffn_in_swiglu", grid=(t // tm, nj),
        in_specs=[pl.BlockSpec((tm, k), lambda i, j: (i, 0)), pl.BlockSpec((k, tn), lambda i, j: (0, j)),
                  pl.BlockSpec((k, tn), lambda i, j: (0, nj + j))],
        out_specs=[out] * 3, out_shape=[jax.ShapeDtypeStruct((t, hid), _CDT)] * 3,
        compiler_params=_cparams(("parallel", "parallel")))(a, w, w)


def _gated_norm(o, z, w):
    return _rms_norm(o, w) * _silu(z)


def _gated_norm_heads(o, z, w):
    heads = [_gated_norm(o[:, h * LANE:(h + 1) * LANE], z[:, h * LANE:(h + 1) * LANE], w) for h in range(HEADS)]
    return jnp.concatenate(heads, axis=1)


def _mla_pre(ckv, krp, cq, cosb, sinb, qw, kw):
    return _rms_norm(cq, qw), _rms_norm(ckv, kw), _rope(krp, cosb, sinb)


def _merge(gg, y_dn, y_mla):
    return _sigmoid(gg[:, :D_MODEL]) * y_dn + _sigmoid(gg[:, D_MODEL:]) * y_mla


def _ln1(xv, attn_out, g, b):
    return _layer_norm(ALPHA * xv + attn_out, g, b)


def _final(h1, ffn, gate_pre, ple_proj, g, b):
    return _layer_norm(ALPHA * h1 + ffn + _sigmoid(gate_pre) * ple_proj, g, b)


def _swiglu(gt, up):
    return _silu(gt) * up


def _local_step(x, p, cosb, sinb, target, wt, sp, exch):
    t = x.shape[0]
    bf = _CDT
    xb = x.astype(bf)
    g = {}

    qkv_pre = _mm2(xb, wt['qkv'], name="f_qkv")
    z = _mm2(xb, wt['z'], name="f_z")
    gg = _mm2(xb, wt['gg'], name="f_gg")
    pm = _mm2(xb, wt['mla'], name="f_mla")
    qkv_act = _conv_silu(qkv_pre, sp['conv_w'])
    (u, w_, qd, kt, intra, gl, t_inv), sent = _delta_local(qkv_act, pm, sp['a_log'], sp['dt_bias'], rider=exch.gather_send())
    (o_dn, sall), passed = _delta_scan(u, w_, qd, kt, intra, gl, rider=exch.gather_pass(sent))
    wt = dict(wt, **exch.weights(passed))
    (og,) = _rowwise(lambda h, o, zz, w: (_gated_norm_heads(o, zz, w),),
                     [(o_dn, D_MODEL, 0, False), (z, D_MODEL, 0, False)], [sp['dn_norm_w']],
                     [(D_MODEL, D_MODEL, False, bf)], name="f_gated_norm")
    y_dn = _mm2(og, wt['br_dn'], name="f_br_dn")

    c_q, c_kv, k_rope = _rowwise(
        lambda h, *a: _mla_pre(*a),
        [(pm, KV_LORA, 0, False), (pm, ROPE_PAD, 2, False), (pm, Q_LORA, 2, False),
         (cosb, ROPE_PAD, 0, False), (sinb, ROPE_PAD, 0, False)],
        [sp['q_norm_w'], sp['kv_norm_w']],
        [(Q_LORA, Q_LORA, False, bf), (KV_LORA, KV_LORA, False, bf), (ROPE_PAD, ROPE_PAD, False, bf)], name="f_mla_pre")
    q_nope = _mm2(c_q, wt['uq_nope'], name="f_uq_nope", out_dtype=bf)
    q_rope_pre = _mm2(c_q, wt['uq_rope'], name="f_uq_rope")
    (q_rope,) = _rowwise(lambda h, q, c, s: (_rope(q, c, s),),
                         [(q_rope_pre, HEADS * ROPE_PAD, 0, False), (cosb, ROPE_PAD, 0, False), (sinb, ROPE_PAD, 0, False)],
                         [], [(HEADS * ROPE_PAD, HEADS * ROPE_PAD, False, bf)], name="f_q_rope")
    q_lat = _mm(q_nope, wt['uk'], name="f_q_lat", tb=True, heads=HEADS, a_head='col', b_head='lead', out_head='lead',
                dims=(t, KV_LORA, NOPE), out_dtype=bf)
    out_lat, lse = _attention(q_lat, q_rope, c_kv, k_rope)
    o_mla = _mm(out_lat, wt['uv'], name="f_o_mla", heads=HEADS, a_head='lead', b_head='lead', out_head='col',
                dims=(t, NOPE, KV_LORA), out_dtype=bf)
    y_mla = _mm2(o_mla, wt['br_mla'], name="f_br_mla")

    (mixed,) = _rowwise(lambda h, *a: (_merge(*a),),
                        [(gg, 2 * D_MODEL, 0, False), (y_dn, D_MODEL, 0, False), (y_mla, D_MODEL, 0, False)],
                        [], [(D_MODEL, D_MODEL, False, bf)], name="f_merge")
    attn_out = _mm2(mixed, wt['o'], name="f_o")
    h1, h1b = _rowwise(lambda h, *a: (_ln1(*a),) * 2, [(x, D_MODEL, 0, False), (attn_out, D_MODEL, 0, False)],
                       [sp['ln1_g'], sp['ln1_b']], [(D_MODEL, D_MODEL, False, F32), (D_MODEL, D_MODEL, False, bf)],
                       name="f_ln1")
    act, ffn_gt, ffn_up = _ffn_in_swiglu(h1b, wt['ffn_in'])
    ffn = _mm2(act, wt['ffn_out'], name="f_ffn_out")
    gate_pre = _mm2(h1b, wt['ple_gate'], name="f_ple_gate")
    pb = p.astype(bf)
    ple_proj = _mm2(pb, wt['ple'], name="f_ple")

    def final_fn(h, h1v, ffnv, gpv, ppv, tgt, gv, bv):
        y, vjp = jax.vjp(_final, h1v, ffnv, gpv, ppv, gv, bv)
        err = y - tgt
        dh1, dffn, dgp, dpp, dg, db = vjp(err * (1.0 / D_MODEL))
        sq = err * err
        lanes = sq[:, :LANE]
        for j in range(1, D_MODEL // LANE):
            lanes = lanes + sq[:, j * LANE:(j + 1) * LANE]
        loss = jnp.sum(lanes, axis=0, keepdims=True) * (0.5 / D_MODEL)
        return dffn, dffn, dgp, dpp, dg, db, loss

    dpre2, dpre2b, dgate_pre, dple_proj, g['ln2_g'], g['ln2_b'], loss_lanes = _rowwise(
        final_fn, [(a, D_MODEL, 0, False) for a in (h1, ffn, gate_pre, ple_proj, target)],
        [sp['ln2_g'], sp['ln2_b']],
        [(D_MODEL, D_MODEL, False, F32)] + [(D_MODEL, D_MODEL, False, bf)] * 3,
        [(1, D_MODEL), (1, D_MODEL), (1, LANE)], name="b_final")

    g['ple'] = _mm2(pb, dple_proj, ta=True, name="g_ple")
    g['ple_gate'] = _mm2(h1b, dgate_pre, ta=True, name="g_ple_gate")
    g['ffn_out'] = _mm2(act, dpre2b, ta=True, name="g_ffn_out")
    dact = _mm2(dpre2b, wt['ffn_out'], tb=True, name="b_dact", out_dtype=bf)

    def swiglu_bwd(h, gt, up, d):
        _, vjp = jax.vjp(_swiglu, gt.astype(F32), up.astype(F32))
        dgt, dup = vjp(d.astype(F32))
        return (jnp.concatenate([dgt, dup], axis=1),)

    (dffn_in,) = _rowwise(swiglu_bwd, [(ffn_gt, FFN_HIDDEN, 0, False), (ffn_up, FFN_HIDDEN, 0, False),
                                       (dact, FFN_HIDDEN, 0, False)], [],
                          [(2 * FFN_HIDDEN, 2 * FFN_HIDDEN, False, bf)], name="b_swiglu")
    g['ffn_in'] = _mm2(h1b, dffn_in, ta=True, name="g_ffn_in")
    dh1 = _mm2(dffn_in, wt['ffn_in'], tb=True, name="b_dh1_ffn", add=dpre2, add_scale=ALPHA)
    dh1 = _mm2(dgate_pre, wt['ple_gate'], tb=True, name="b_dh1_gate", add=dh1)

    def ln1_bwd(h, xv, ao, d, gv, bv):
        _, vjp = jax.vjp(_ln1, xv, ao, gv, bv)
        _, dao, dg, db = vjp(d)
        return dao, dao, dg, db

    dpre1, dpre1b, g['ln1_g'], g['ln1_b'] = _rowwise(
        ln1_bwd, [(x, D_MODEL, 0, False), (attn_out, D_MODEL, 0, False), (dh1, D_MODEL, 0, False)],
        [sp['ln1_g'], sp['ln1_b']], [(D_MODEL, D_MODEL, False, F32), (D_MODEL, D_MODEL, False, bf)],
        [(1, D_MODEL), (1, D_MODEL)], name="b_ln1")

    g['o'] = _mm2(mixed, dpre1b, ta=True, name="g_o")
    dmixed = _mm2(dpre1b, wt['o'], tb=True, name="b_dmixed")

    def merge_bwd(h, ggv, yd, ym, d):
        _, vjp = jax.vjp(_merge, ggv, yd, ym)
        return vjp(d)

    dgg, dy_dn, dy_mla = _rowwise(
        merge_bwd, [(gg, 2 * D_MODEL, 0, False), (y_dn, D_MODEL, 0, False), (y_mla, D_MODEL, 0, False),
                    (dmixed, D_MODEL, 0, False)], [],
        [(2 * D_MODEL, 2 * D_MODEL, False, bf), (D_MODEL, D_MODEL, False, bf), (D_MODEL, D_MODEL, False, bf)],
        name="b_merge")
    g['br_dn'] = _mm2(og, dy_dn, ta=True, name="g_br_dn")
    dog = _mm2(dy_dn, wt['br_dn'], tb=True, name="b_dog")
    g['br_mla'] = _mm2(o_mla, dy_mla, ta=True, name="g_br_mla")
    do_mla = _mm2(dy_mla, wt['br_mla'], tb=True, name="b_do_mla", out_dtype=bf)

    dout_lat = _mm(do_mla, wt['uv'], name="b_dout_lat", tb=True, heads=HEADS, a_head='col', b_head='lead',
                   out_head='lead', dims=(t, KV_LORA, NOPE))
    g['uv'] = _mm(out_lat, do_mla, name="g_uv", ta=True, heads=HEADS, a_head='lead', b_head='col', out_head='lead',
                  dims=(KV_LORA, NOPE, t))
    dq_lat, dq_rope, dckv_att, dkr_att = _attention_bwd(q_lat, q_rope, c_kv, k_rope, out_lat, lse, dout_lat)
    dq_nope = _mm(dq_lat, wt['uk'], name="b_dq_nope", heads=HEADS, a_head='lead', b_head='lead', out_head='col',
                  dims=(t, NOPE, KV_LORA), out_dtype=bf)
    g['uk'] = _mm(dq_lat, q_nope, name="g_uk", ta=True, heads=HEADS, a_head='lead', b_head='col', out_head='lead',
                  dims=(KV_LORA, NOPE, t))
    (dq_rope_pre,) = _rowwise(lambda h, d, c, s: (_rope_bwd(d, c, s),),
                              [(dq_rope, HEADS * ROPE_PAD, 0, False), (cosb, ROPE_PAD, 0, False), (sinb, ROPE_PAD, 0, False)],
                              [], [(HEADS * ROPE_PAD, HEADS * ROPE_PAD, False, bf)], name="b_q_rope")
    g['uq_nope'] = _mm2(c_q, dq_nope, ta=True, name="g_uq_nope")
    g['uq_rope'] = _mm2(c_q, dq_rope_pre, ta=True, name="g_uq_rope")
    dc_q = _mm2(dq_nope, wt['uq_nope'], tb=True, name="b_dcq_nope")
    dc_q = _mm2(dq_rope_pre, wt['uq_rope'], tb=True, name="b_dcq_rope", add=dc_q)

    def gated_norm_bwd(h, o, zz, d, w):
        _, vjp = jax.vjp(_gated_norm_heads, o, zz, w)
        return vjp(d)

    do_dn, dz, g['dn_norm_w'] = _rowwise(
        gated_norm_bwd, [(o_dn, D_MODEL, 0, False), (z, D_MODEL, 0, False), (dog, D_MODEL, 0, False)], [sp['dn_norm_w']],
        [(D_MODEL, D_MODEL, False, F32), (D_MODEL, D_MODEL, False, bf)], [(1, LANE)], name="b_gated_norm")
    (du, dw, dqd, dkt, dintra, dgl), paired = _delta_scan_bwd(u, w_, qd, kt, intra, gl, sall, do_dn, rider=exch.pair_send(g))
    (dq_a, dk_a, dv_a, dba, g['a_log'], g['dt_bias']), arrived = _delta_local_bwd(
        qkv_act, pm, sp['a_log'], sp['dt_bias'], t_inv, du, dw, dqd, dkt, dintra, dgl, rider=exch.reduce_send(paired))
    exch.reduce_arrived(arrived)
    dqkv_pre, g['conv_w'] = _conv_silu_bwd(qkv_pre, sp['conv_w'], [dq_a, dk_a, dv_a])

    def mla_pre_bwd(h, ckv, krp, cq, cosv, sinv, dcq, dckv, dkr, qw, kw):
        _, vjp = jax.vjp(lambda a, b, c, d, e: (_rms_norm(c, d), _rms_norm(a, e)), ckv, krp, cq, qw, kw)
        dckv_p, _, dcq_p, dqw, dkw = vjp((dcq, dckv))
        dkr_p = _rope_bwd(dkr, cosv, sinv)
        dpm = jnp.concatenate([dckv_p, dkr_p, jnp.zeros((ckv.shape[0], 3 * LANE), F32), dcq_p], axis=1)
        return dpm, dqw, dkw

    dpm_main, g['q_norm_w'], g['kv_norm_w'] = _rowwise(
        mla_pre_bwd,
        [(pm, KV_LORA, 0, False), (pm, ROPE_PAD, 2, False), (pm, Q_LORA, 2, False),
         (cosb, ROPE_PAD, 0, False), (sinb, ROPE_PAD, 0, False),
         (dc_q, Q_LORA, 0, False), (dckv_att, KV_LORA, 0, False), (dkr_att, ROPE_PAD, 0, False)],
        [sp['q_norm_w'], sp['kv_norm_w']], [(1152, 1152, False, F32)], [(1, Q_LORA), (1, KV_LORA)], name="b_mla_pre")
    (dpm,) = _rowwise(
        lambda h, a, b: (jnp.concatenate([a[:, :3 * LANE], b, a[:, 4 * LANE:]], axis=1),),
        [(dpm_main, 1152, 0, False), (dba, LANE, 0, False)], [], [(1152, 1152, False, bf)], name="b_dpm")

    g['qkv'] = _mm2(xb, dqkv_pre, ta=True, name="g_qkv")
    g['z'] = _mm2(xb, dz, ta=True, name="g_z")
    g['gg'] = _mm2(xb, dgg, ta=True, name="g_gg")
    g['mla'] = _mm2(xb, dpm, ta=True, name="g_mla")
    dx = _mm2(dqkv_pre, wt['qkv'], tb=True, name="b_dx_qkv", add=dpre1, add_scale=ALPHA)
    dx = _mm2(dz, wt['z'], tb=True, name="b_dx_z", add=dx)
    dx = _mm2(dgg, wt['gg'], tb=True, name="b_dx_gg", add=dx)
    dx = _mm2(dpm, wt['mla'], tb=True, name="b_dx_mla", add=dx)
    return loss_lanes, dx, g


_IN_SIZES = (QKV_W, HEADS * DN_DK, HEADS, HEADS, Q_LORA, KV_LORA, ROPE, D_MODEL, D_MODEL)


def _rope_tables(positions):
    inv_freq = ROPE_BASE ** (-jnp.arange(0, ROPE, 2, dtype=F32) / ROPE)
    ang = positions.astype(F32)[:, None] * inv_freq
    cos, sin = jnp.cos(ang), jnp.sin(ang)
    zeros = jnp.zeros((positions.shape[0], ROPE_PAD - ROPE), F32)
    return jnp.concatenate([cos, cos, zeros], axis=1), jnp.concatenate([-sin, sin, zeros], axis=1)


def _prep_w_in(w_in):
    dt = w_in.dtype
    offs = [0]
    for s in _IN_SIZES:
        offs.append(offs[-1] + s)
    qkv, z, wb, wa, cq, ckv, kr, gd, gm = [w_in[:, offs[i]:offs[i + 1]] for i in range(len(_IN_SIZES))]
    zc = lambda n: jnp.zeros((D_MODEL, n), dt)
    return {
        'qkv': qkv, 'z': z, 'gg': jnp.concatenate([gd, gm], axis=1),
        'mla': jnp.concatenate([ckv, kr, zc(ROPE_PAD - ROPE), wb, wa, zc(LANE - 2 * HEADS), zc(2 * LANE), cq], axis=1),
    }


def _prep_weights(full):
    w_uq = full['w_uq']
    wt = {
        'uq_nope': w_uq[:, :, :NOPE].reshape(Q_LORA, HEADS * NOPE),
        'uq_rope': jnp.pad(w_uq[:, :, NOPE:], ((0, 0), (0, 0), (0, ROPE_PAD - ROPE))).reshape(Q_LORA, HEADS * ROPE_PAD),
        'uk': jnp.transpose(full['w_uk'], (1, 0, 2)), 'uv': jnp.transpose(full['w_uv'], (1, 0, 2)),
        'br_dn': full['w_br_dn'], 'br_mla': full['w_br_mla'], 'o': full['w_o'], 'ffn_in': full['w_ffn_in'],
        'ffn_out': full['w_ffn_out'], 'ple': full['w_ple'], 'ple_gate': full['w_ple_gate'],
    }
    return wt


def _prep_small(small):
    pad = lambda v: jnp.pad(v, (0, LANE - v.shape[0]))[None, :]
    return {
        'conv_w': small['conv_w'], 'a_log': pad(small['dn_a_log']), 'dt_bias': pad(small['dn_dt_bias']),
        'dn_norm_w': small['dn_norm_w'][None, :], 'q_norm_w': small['q_norm_w'][None, :],
        'kv_norm_w': small['kv_norm_w'][None, :], 'ln1_g': small['ln1_g'][None, :], 'ln1_b': small['ln1_b'][None, :],
        'ln2_g': small['ln2_g'][None, :], 'ln2_b': small['ln2_b'][None, :],
    }


def _unprep_grads_late(g):
    mla = g['mla']
    ba0 = KV_LORA + ROPE_PAD
    cq0 = ba0 + 3 * LANE
    w_in = jnp.concatenate([
        g['qkv'], g['z'], mla[:, ba0:ba0 + HEADS], mla[:, ba0 + HEADS:ba0 + 2 * HEADS], mla[:, cq0:cq0 + Q_LORA],
        mla[:, :KV_LORA], mla[:, KV_LORA:KV_LORA + ROPE], g['gg']], axis=1)
    return {
        'w_in': w_in, 'conv_w': g['conv_w'], 'dn_a_log': g['a_log'][0, :HEADS], 'dn_dt_bias': g['dt_bias'][0, :HEADS],
        'dn_norm_w': g['dn_norm_w'][0], 'q_norm_w': g['q_norm_w'][0], 'kv_norm_w': g['kv_norm_w'][0],
        'ln1_g': g['ln1_g'][0], 'ln1_b': g['ln1_b'][0], 'ln2_g': g['ln2_g'][0], 'ln2_b': g['ln2_b'][0],
    }


def _unprep_grads_early(g):
    w_uq = jnp.concatenate([g['uq_nope'].reshape(Q_LORA, HEADS, NOPE),
                            g['uq_rope'].reshape(Q_LORA, HEADS, ROPE_PAD)[:, :, :ROPE]], axis=2)
    return {
        'w_uq': w_uq, 'w_uk': jnp.transpose(g['uk'], (1, 0, 2)), 'w_uv': jnp.transpose(g['uv'], (1, 0, 2)),
        'w_br_dn': g['br_dn'], 'w_br_mla': g['br_mla'], 'w_o': g['o'],
        'w_ffn_in': g['ffn_in'], 'w_ffn_out': g['ffn_out'], 'w_ple': g['ple'], 'w_ple_gate': g['ple_gate'],
    }


_FLATB_PIECES = (
    ('w_ffn_out', 704, (704, D_MODEL)), ('w_br_dn', 256, (256, D_MODEL)), ('w_br_mla', 256, (256, D_MODEL)),
    ('w_o', 256, (256, D_MODEL)), ('w_ple_gate', 256, (256, D_MODEL)), ('w_uq', 144, (96, HEADS, NOPE + ROPE)),
    ('w_uk', 64, (64, HEADS, NOPE)), ('w_uv', 64, (64, HEADS, NOPE)), ('w_ple', 64, (PLE_DIM, 256)),
)
FLATB_ROWS = 2112
W_IN_SHARD = D_IN // N_SHARD
FFN_IN_SHARD = 2 * FFN_HIDDEN // N_SHARD
A_ROWS = D_MODEL + 32
_CONV_SHARD = QKV_W // N_SHARD
_ADD_TILES = (256, 256, 352)


def _flatb_offsets():
    offs, o = {}, 0
    for name, rows, _ in _FLATB_PIECES:
        offs[name] = o
        o += rows
    return offs, o


def _pack_shards(ws, conv_w):
    conv_bits = lax.bitcast_convert_type(conv_w, jnp.bfloat16).reshape(DN_CONV, 2 * _CONV_SHARD).astype(_CDT)
    tail = jnp.pad(conv_bits, ((0, A_ROWS - D_MODEL - DN_CONV), (0, W_IN_SHARD - 2 * _CONV_SHARD)))
    a_buf = jnp.concatenate([ws['w_in'].astype(_CDT), tail], axis=0)
    parts = [ws[name].astype(_CDT).reshape(rows, FLAT_W) for name, rows, _ in _FLATB_PIECES]
    used = sum(p.shape[0] for p in parts)
    parts.append(jnp.zeros((FLATB_ROWS - used, FLAT_W), _CDT))
    return [a_buf, ws['w_ffn_in'].astype(_CDT), jnp.concatenate(parts, axis=0)]


def _unpack_w_in(gathered, local, me):
    a = [jnp.where(me == s, local, gathered[s]) for s in range(N_SHARD)]
    conv = [lax.bitcast_convert_type(
        p[D_MODEL:D_MODEL + DN_CONV, :2 * _CONV_SHARD].astype(jnp.bfloat16).reshape(DN_CONV, _CONV_SHARD, 2), F32) for p in a]
    return jnp.concatenate([p[:D_MODEL] for p in a], axis=1), jnp.concatenate(conv, axis=1)


def _unpack_rest(gathered, local, me):
    pick = lambda b, s: jnp.where(me == s, local[b], gathered[b][s])
    full = {'w_ffn_in': jnp.concatenate([pick(0, s) for s in range(N_SHARD)], axis=1)}
    offs, _ = _flatb_offsets()
    fb = [pick(1, s) for s in range(N_SHARD)]
    for name, rows, shape in _FLATB_PIECES:
        pieces = [p[offs[name]:offs[name] + rows].reshape(shape) for p in fb]
        full[name] = jnp.concatenate(pieces, axis=1 if name == 'w_ple' else 0)
    return full


def _shard_columns(g, w):
    return jnp.stack([g[:, s * w:(s + 1) * w] for s in range(N_SHARD)])


def _pack_grads_rest(gw):
    parts = []
    for name, rows, _ in _FLATB_PIECES:
        g = gw[name]
        if name == 'w_ple':
            parts.append(_shard_columns(g, PLE_DIM).reshape(N_SHARD, rows, FLAT_W))
        else:
            parts.append(g.reshape(N_SHARD, rows, FLAT_W))
    used = sum(p.shape[1] for p in parts)
    parts.append(jnp.zeros((N_SHARD, FLATB_ROWS - used, FLAT_W), F32))
    return [_shard_columns(gw['w_ffn_in'], FFN_IN_SHARD), jnp.concatenate(parts, axis=1)]


def _unpack_reduced(mine, theirs, c):
    whole = [jnp.concatenate([jnp.where(c == 0, m, t), jnp.where(c == 0, t, m)], axis=0) for m, t in zip(mine, theirs)]
    out = {'w_in': whole[0], 'w_ffn_in': whole[1]}
    offs, _ = _flatb_offsets()
    for name, rows, shape in _FLATB_PIECES:
        out[name] = whole[2][offs[name]:offs[name] + rows].reshape(shape)
    return out


_HBM = pl.BlockSpec(memory_space=pltpu.HBM)


def _place():
    x, y, c = lax.axis_index("x"), lax.axis_index("y"), lax.axis_index("c")
    chips = [(1 - x, y), (x, 1 - y), (1 - x, 1 - y)]
    return x, y, c, chips


def _remote(src, dst, send_sems, recv_sems, k, to):
    return pltpu.make_async_remote_copy(src_ref=src, dst_ref=dst, send_sem=send_sems.at[k], recv_sem=recv_sems.at[k],
                                        device_id=to, device_id_type=_MESH)


def _half_rows(ref, half, hf, lead=None):
    rows = pl.ds(pl.multiple_of(hf * half, 16), half)
    return ref.at[rows, :] if lead is None else ref.at[lead, rows, :]


class _Rider:
    def __init__(self, inputs, out_shape, n_sems, copies, aliases=None):
        self.inputs, self.out_shape, self.n_sems, self.copies = list(inputs), list(out_shape), n_sems, copies
        self.aliases = aliases or {}


def _carried_call(body, rider, first, last, *, name, grid, in_specs, out_specs, out_shape, scratch_shapes, sem, args):
    n_in, n_out, n_scr = len(in_specs), len(out_specs), len(scratch_shapes)
    if rider is None:
        res = pl.pallas_call(body, name=name, grid=grid, in_specs=in_specs, out_specs=out_specs, out_shape=out_shape,
                             scratch_shapes=scratch_shapes, compiler_params=_cparams(sem))(*args)
        return list(res), []
    ri, ro = len(rider.inputs), len(rider.out_shape)

    def full_body(*refs):
        own_in, r_in = refs[:n_in], refs[n_in:n_in + ri]
        o0 = n_in + ri
        own_out, r_out = refs[o0:o0 + n_out], refs[o0 + n_out:o0 + n_out + ro]
        s0 = o0 + n_out + ro
        own_scr, send_sems, recv_sems = refs[s0:s0 + n_scr], refs[s0 + n_scr], refs[s0 + n_scr + 1]

        @pl.when(first())
        def _():
            sends, _ = rider.copies(r_in, r_out, send_sems, recv_sems)
            for cp in sends:
                cp.start()

        body(*own_in, *own_out, *own_scr)

        @pl.when(last())
        def _():
            sends, arrivals = rider.copies(r_in, r_out, send_sems, recv_sems)
            for cp in arrivals():
                cp.wait_recv()
            for cp in sends:
                cp.wait_send()

    res = pl.pallas_call(
        full_body, name=name, grid=grid, in_specs=list(in_specs) + [_HBM] * ri, out_specs=list(out_specs) + [_HBM] * ro,
        out_shape=list(out_shape) + rider.out_shape,
        scratch_shapes=list(scratch_shapes) + [pltpu.SemaphoreType.DMA((rider.n_sems,))] * 2,
        input_output_aliases={n_in + i: n_out + o for i, o in rider.aliases.items()},
        compiler_params=_cparams(sem))(*args, *rider.inputs)
    return list(res[:n_out]), list(res[n_out:])


def _ride_gather_send(bufs):
    n = len(bufs)
    halves = [b.shape[0] // 2 for b in bufs]

    def copies(ins, outs, send_sems, recv_sems):
        x, y, c, chips = _place()
        slot = lambda b, cx, cy: _half_rows(outs[b], halves[b], c, lead=2 * cx + cy)
        sends = [_remote(_half_rows(ins[b], halves[b], c), slot(b, x, y), send_sems, recv_sems, 3 * b + j, (cx, cy, c))
                 for b in range(n) for j, (cx, cy) in enumerate(chips)]
        arrivals = lambda: [_remote(slot(b, cx, cy), slot(b, cx, cy), send_sems, recv_sems, 3 * b + j, (x, y, c))
                            for b in range(n) for j, (cx, cy) in enumerate(chips)]
        return sends, arrivals

    return _Rider(bufs, [jax.ShapeDtypeStruct((N_SHARD,) + b.shape, b.dtype) for b in bufs], 3 * n, copies)


def _ride_gather_pass(gathered):
    n = len(gathered)
    halves = [g.shape[1] // 2 for g in gathered]

    def copies(ins, outs, send_sems, recv_sems):
        x, y, c, chips = _place()
        slot = lambda b, cx, cy, hf: _half_rows(outs[b], halves[b], hf, lead=2 * cx + cy)
        sends = [_remote(slot(b, cx, cy, c), slot(b, cx, cy, c), send_sems, recv_sems, 3 * b + j, (x, y, 1 - c))
                 for b in range(n) for j, (cx, cy) in enumerate(chips)]
        arrivals = lambda: [_remote(slot(b, cx, cy, 1 - c), slot(b, cx, cy, 1 - c), send_sems, recv_sems, 3 * b + j, (x, y, c))
                            for b in range(n) for j, (cx, cy) in enumerate(chips)]
        return sends, arrivals

    return _Rider(gathered, [jax.ShapeDtypeStruct(g.shape, g.dtype) for g in gathered], 3 * n, copies,
                  aliases={b: b for b in range(n)})


def _ride_pair_exchange(gbufs):
    n = len(gbufs)
    halves = [g.shape[1] // 2 for g in gbufs]

    def copies(ins, outs, send_sems, recv_sems):
        x, y, c, _ = _place()
        sends = [_remote(ins[b].at[:, pl.ds(pl.multiple_of((1 - c) * halves[b], 16), halves[b]), :], outs[b],
                         send_sems, recv_sems, b, (x, y, 1 - c)) for b in range(n)]
        arrivals = lambda: [_remote(outs[b], outs[b], send_sems, recv_sems, b, (x, y, c)) for b in range(n)]
        return sends, arrivals

    return _Rider(gbufs, [jax.ShapeDtypeStruct((N_SHARD, h, g.shape[2]), g.dtype) for g, h in zip(gbufs, halves)], n, copies)


def _ride_chip_exchange(parts):
    n = len(parts)

    def copies(ins, outs, send_sems, recv_sems):
        x, y, c, chips = _place()
        sends = [_remote(ins[b].at[2 * cx + cy], outs[b].at[j], send_sems, recv_sems, 3 * b + j, (cx, cy, c))
                 for b in range(n) for j, (cx, cy) in enumerate(chips)]
        arrivals = lambda: [_remote(ins[b].at[0], outs[b].at[j], send_sems, recv_sems, 3 * b + j, (x, y, c))
                            for b in range(n) for j in range(len(chips))]
        return sends, arrivals

    return _Rider(parts, [jax.ShapeDtypeStruct((3,) + p.shape[1:], p.dtype) for p in parts], 3 * n, copies)


def _gather_shards(bufs, name):
    n = len(bufs)
    halves = [b.shape[0] // 2 for b in bufs]

    def body(*refs):
        ins, outs, send_sems, recv_sems = refs[:n], refs[n:2 * n], refs[2 * n], refs[2 * n + 1]
        x, y, c, chips = _place()
        me, sibling = (x, y, c), (x, y, 1 - c)
        slot = lambda b, cx, cy, hf: _half_rows(outs[b], halves[b], hf, lead=2 * cx + cy)
        first = [_remote(_half_rows(ins[b], halves[b], c), slot(b, x, y, c), send_sems, recv_sems, 6 * b + j, (cx, cy, c))
                 for b in range(n) for j, (cx, cy) in enumerate(chips)]
        for cp in first:
            cp.start()
        passed = []
        for j, (cx, cy) in enumerate(chips):
            for b in range(n):
                _remote(slot(b, cx, cy, c), slot(b, cx, cy, c), send_sems, recv_sems, 6 * b + j, me).wait_recv()
                fwd = _remote(slot(b, cx, cy, c), slot(b, cx, cy, c), send_sems, recv_sems, 6 * b + 3 + j, sibling)
                fwd.start()
                passed.append(fwd)
        for j, (cx, cy) in enumerate(chips):
            for b in range(n):
                _remote(slot(b, cx, cy, 1 - c), slot(b, cx, cy, 1 - c), send_sems, recv_sems, 6 * b + 3 + j, me).wait_recv()
        for cp in first + passed:
            cp.wait_send()

    return pl.pallas_call(
        body, name=name, out_shape=[jax.ShapeDtypeStruct((N_SHARD,) + b.shape, b.dtype) for b in bufs],
        in_specs=[_HBM] * n, out_specs=[_HBM] * n,
        scratch_shapes=[pltpu.SemaphoreType.DMA((6 * n,)), pltpu.SemaphoreType.DMA((6 * n,))],
    )(*bufs)


def _reduce_pair_exchange(gbufs, name):
    n = len(gbufs)
    halves = [g.shape[1] // 2 for g in gbufs]

    def body(*refs):
        ins, outs, send_sems, recv_sems = refs[:n], refs[n:2 * n], refs[2 * n], refs[2 * n + 1]
        x, y, c, _ = _place()
        cps = [_remote(ins[b].at[:, pl.ds(pl.multiple_of((1 - c) * halves[b], 16), halves[b]), :], outs[b],
                       send_sems, recv_sems, b, (x, y, 1 - c)) for b in range(n)]
        for cp in cps:
            cp.start()
        for cp in cps:
            cp.wait()

    return pl.pallas_call(
        body, name=name,
        out_shape=[jax.ShapeDtypeStruct((N_SHARD, h, g.shape[2]), g.dtype) for g, h in zip(gbufs, halves)],
        in_specs=[_HBM] * n, out_specs=[_HBM] * n,
        scratch_shapes=[pltpu.SemaphoreType.DMA((n,)), pltpu.SemaphoreType.DMA((n,))],
    )(*gbufs)


def _pair_add(gbuf, recv, c_arr, tr, name):
    _, rows, width = gbuf.shape
    half = rows // 2
    nt = half // tr

    def body(c_ref, a_ref, b_ref, o_ref):
        o_ref[...] = (a_ref[...] + b_ref[...]).astype(o_ref.dtype)

    blk = lambda f: pl.BlockSpec((None, tr, width), f)
    return pl.pallas_call(
        body, name=name, out_shape=jax.ShapeDtypeStruct((N_SHARD, half, width), jnp.bfloat16),
        grid_spec=pltpu.PrefetchScalarGridSpec(
            num_scalar_prefetch=1, grid=(N_SHARD, nt),
            in_specs=[blk(lambda s, i, c: (s, c[0] * nt + i, 0)), blk(lambda s, i, c: (s, i, 0))],
            out_specs=blk(lambda s, i, c: (s, i, 0))),
        compiler_params=_cparams(("parallel", "parallel")))(c_arr, gbuf, recv)


def _reduce_chip_exchange(parts, name):
    n = len(parts)

    def body(*refs):
        ins, outs, send_sems, recv_sems = refs[:n], refs[n:2 * n], refs[2 * n], refs[2 * n + 1]
        x, y, c, chips = _place()
        sends = [_remote(ins[b].at[2 * cx + cy], outs[b].at[j], send_sems, recv_sems, 3 * b + j, (cx, cy, c))
                 for b in range(n) for j, (cx, cy) in enumerate(chips)]
        for cp in sends:
            cp.start()
        for b in range(n):
            for j in range(len(chips)):
                _remote(ins[b].at[0], outs[b].at[j], send_sems, recv_sems, 3 * b + j, (x, y, c)).wait_recv()
        for cp in sends:
            cp.wait_send()

    return pl.pallas_call(
        body, name=name, out_shape=[jax.ShapeDtypeStruct((3,) + p.shape[1:], p.dtype) for p in parts],
        in_specs=[_HBM] * n, out_specs=[_HBM] * n,
        scratch_shapes=[pltpu.SemaphoreType.DMA((3 * n,)), pltpu.SemaphoreType.DMA((3 * n,))],
    )(*parts)


def _chip_add(part, recv, me_arr, tr, name):
    _, half, width = part.shape

    def body(me_ref, own, a0, a1, a2, o_ref):
        f = lambda r: r[...].astype(F32)
        o_ref[...] = ((f(own) + f(a0)) + f(a1)) + f(a2)

    specs = [pl.BlockSpec((None, tr, width), lambda i, me: (me[0], i, 0))]
    specs += [pl.BlockSpec((None, tr, width), functools.partial(lambda i, me, k: (k, i, 0), k=k)) for k in range(3)]
    return pl.pallas_call(
        body, name=name, out_shape=jax.ShapeDtypeStruct((half, width), F32),
        grid_spec=pltpu.PrefetchScalarGridSpec(
            num_scalar_prefetch=1, grid=(half // tr,), in_specs=specs,
            out_specs=pl.BlockSpec((tr, width), lambda i, me: (i, 0))),
        compiler_params=_cparams(("parallel",)))(me_arr, part, recv, recv, recv)


def _reduce_pair_share(rhalves, name):
    n = len(rhalves)

    def body(*refs):
        ins, outs, send_sems, recv_sems = refs[:n], refs[n:2 * n], refs[2 * n], refs[2 * n + 1]
        x, y, c, _ = _place()
        cps = [_remote(ins[b], outs[b], send_sems, recv_sems, b, (x, y, 1 - c)) for b in range(n)]
        for cp in cps:
            cp.start()
        for cp in cps:
            cp.wait()

    return pl.pallas_call(
        body, name=name, out_shape=[jax.ShapeDtypeStruct(r.shape, r.dtype) for r in rhalves],
        in_specs=[_HBM] * n, out_specs=[_HBM] * n,
        scratch_shapes=[pltpu.SemaphoreType.DMA((n,)), pltpu.SemaphoreType.DMA((n,))],
    )(*rhalves)


def _small_allreduce(buf):
    r, width = buf.shape
    n_dev = 8

    def body(x_ref, all_ref, sum_ref, send_sems, recv_sems, local_sem):
        x, y, c, chips = _place()
        me, sibling = (x, y, c), (x, y, 1 - c)

        def rows(px, py, pc):
            return all_ref.at[pl.ds(pl.multiple_of((4 * px + 2 * py + pc) * r, 8), r), :]

        def copy(k, block, to, src=None):
            return _remote(rows(*block) if src is None else src, rows(*block), send_sems, recv_sems, k, to)

        mine = pltpu.make_async_copy(x_ref, rows(*me), local_sem)
        mine.start()
        first = [copy(0, me, sibling, src=x_ref)]
        first += [copy(1 + j, me, (*chip, c), src=x_ref) for j, chip in enumerate(chips)]
        for cp in first:
            cp.start()
        passed = [copy(4 + j, (*chip, c), sibling) for j, chip in enumerate(chips)]
        for j, chip in enumerate(chips):
            copy(1 + j, (*chip, c), me).wait_recv()
            passed[j].start()
        copy(0, sibling, me).wait_recv()
        for j, chip in enumerate(chips):
            copy(4 + j, (*chip, 1 - c), me).wait_recv()
        for cp in first + passed:
            cp.wait_send()
        mine.wait()
        total = all_ref[0:r, :]
        for k in range(1, n_dev):
            total = total + all_ref[k * r:(k + 1) * r, :]
        sum_ref[...] = total

    vm = pl.BlockSpec(memory_space=pltpu.VMEM)
    _, total = pl.pallas_call(
        body, name="small_allreduce",
        out_shape=[jax.ShapeDtypeStruct((n_dev * r, width), buf.dtype), jax.ShapeDtypeStruct((r, width), buf.dtype)],
        in_specs=[vm], out_specs=[vm, vm],
        scratch_shapes=[pltpu.SemaphoreType.DMA((7,)), pltpu.SemaphoreType.DMA((7,)), pltpu.SemaphoreType.DMA],
    )(buf)
    return total


def _row_tile(rows, cap):
    if rows <= cap:
        return rows
    t = (cap // 8) * 8
    while t >= 8:
        if rows % t == 0:
            return t
        t -= 8
    return rows


def _adamw(w, g, m, v, name):
    shape = w.shape
    cols = shape[-1] if len(shape) <= 3 else shape[-2] * shape[-1]
    lead = len(shape) == 3
    w2, g2, m2, v2 = (a if lead else a.reshape(-1, cols) for a in (w, g, m, v))
    rows = shape[1] if lead else w2.shape[0]
    tr, tc = _row_tile(rows, 256), cols
    if tr == rows and rows > 256:
        tc = _tile(cols, 256)

    def body(w_ref, g_ref, m_ref, v_ref, d_ref, mo_ref, vo_ref):
        gv = g_ref[...]
        mn = ADAM_B1 * m_ref[...] + (1.0 - ADAM_B1) * gv
        vn = ADAM_B2 * v_ref[...] + (1.0 - ADAM_B2) * (gv * gv)
        m_hat = mn / (1.0 - ADAM_B1 ** ADAM_STEP)
        v_hat = vn / (1.0 - ADAM_B2 ** ADAM_STEP)
        d_ref[...] = -ADAM_LR * (m_hat / (jnp.sqrt(v_hat) + ADAM_EPS) + ADAM_WD * w_ref[...])
        mo_ref[...] = mn
        vo_ref[...] = vn

    blk = (pl.BlockSpec((None, tr, tc), lambda i, j: (0, i, j)) if lead else pl.BlockSpec((tr, tc), lambda i, j: (i, j)))
    outs = pl.pallas_call(
        body, name=name, grid=(rows // tr, cols // tc), in_specs=[blk] * 4, out_specs=[blk] * 3,
        out_shape=[jax.ShapeDtypeStruct(w2.shape, F32)] * 3,
        compiler_params=_cparams(("parallel", "parallel")))(w2, g2, m2, v2)
    return tuple(o.reshape(shape) for o in outs)


_WEIGHT_NAMES = ('w_in', 'conv_w', 'dn_a_log', 'dn_dt_bias', 'dn_norm_w', 'q_norm_w', 'w_uq', 'kv_norm_w', 'w_uk',
                 'w_uv', 'w_br_dn', 'w_br_mla', 'w_o', 'ln1_g', 'ln1_b', 'w_ffn_in', 'w_ffn_out', 'w_ple',
                 'w_ple_gate', 'ln2_g', 'ln2_b')
_SMALL_NAMES = ('ln1_g', 'ln1_b', 'ln2_g', 'ln2_b', 'q_norm_w', 'kv_norm_w', 'dn_norm_w', 'dn_a_log', 'dn_dt_bias')
_SMALL_GROUP = 8
_CONV_SMALL_ROW = len(_SMALL_NAMES) * _SMALL_GROUP
_CONV_SMALL_ROWS = DN_CONV * QKV_W // FLAT_W


def _pack_small(gw):
    rows = [jnp.pad(gw[n][None, :], ((0, _SMALL_GROUP - 1), (0, FLAT_W - gw[n].shape[0]))) for n in _SMALL_NAMES]
    rows.append(jnp.pad(gw['conv_w'].reshape(_CONV_SMALL_ROWS, FLAT_W), ((0, SMALL_ROWS - _CONV_SMALL_ROW - _CONV_SMALL_ROWS), (0, 0))))
    return jnp.concatenate(rows, axis=0)


class _Exchange:
    def __init__(self, local, me_chip, c_arr):
        self.local, self.me_chip, self.c_arr = local, me_chip, c_arr
        self.parts = self.arrived = None

    def gather_send(self):
        return _ride_gather_send(self.local)

    def gather_pass(self, sent):
        return _ride_gather_pass(sent)

    def weights(self, gathered):
        return _prep_weights(_unpack_rest(gathered, self.local, self.me_chip))

    def pair_send(self, g):
        self.gbufs = _pack_grads_rest(_unprep_grads_early(g))
        return _ride_pair_exchange(self.gbufs)

    def reduce_send(self, got):
        self.parts = [_pair_add(g_, r_, self.c_arr, tr, "pair_add_%d" % (i + 1))
                      for i, (g_, r_, tr) in enumerate(zip(self.gbufs, got, _ADD_TILES[1:]))]
        return _ride_chip_exchange(self.parts)

    def reduce_arrived(self, arrived):
        self.arrived = list(arrived)


def kernel(x, p, positions, w_in, conv_w, dn_a_log, dn_dt_bias, dn_norm_w, q_norm_w, w_uq, kv_norm_w, w_uk, w_uv, w_br_dn, w_br_mla, w_o, ln1_g, ln1_b, w_ffn_in, w_ffn_out, w_ple, w_ple_gate, ln2_g, ln2_b, loss_target, m_w_in, m_conv_w, m_dn_a_log, m_dn_dt_bias, m_dn_norm_w, m_q_norm_w, m_w_uq, m_kv_norm_w, m_w_uk, m_w_uv, m_w_br_dn, m_w_br_mla, m_w_o, m_ln1_g, m_ln1_b, m_w_ffn_in, m_w_ffn_out, m_w_ple, m_w_ple_gate, m_ln2_g, m_ln2_b, v_w_in, v_conv_w, v_dn_a_log, v_dn_dt_bias, v_dn_norm_w, v_q_norm_w, v_w_uq, v_kv_norm_w, v_w_uk, v_w_uv, v_w_br_dn, v_w_br_mla, v_w_o, v_ln1_g, v_ln1_b, v_w_ffn_in, v_w_ffn_out, v_w_ple, v_w_ple_gate, v_ln2_g, v_ln2_b):
    ws = dict(w_in=w_in, conv_w=conv_w, dn_a_log=dn_a_log, dn_dt_bias=dn_dt_bias, dn_norm_w=dn_norm_w, q_norm_w=q_norm_w,
              w_uq=w_uq, kv_norm_w=kv_norm_w, w_uk=w_uk, w_uv=w_uv, w_br_dn=w_br_dn, w_br_mla=w_br_mla, w_o=w_o,
              ln1_g=ln1_g, ln1_b=ln1_b, w_ffn_in=w_ffn_in, w_ffn_out=w_ffn_out, w_ple=w_ple, w_ple_gate=w_ple_gate,
              ln2_g=ln2_g, ln2_b=ln2_b)
    ms = dict(w_in=m_w_in, conv_w=m_conv_w, dn_a_log=m_dn_a_log, dn_dt_bias=m_dn_dt_bias, dn_norm_w=m_dn_norm_w,
              q_norm_w=m_q_norm_w, w_uq=m_w_uq, kv_norm_w=m_kv_norm_w, w_uk=m_w_uk, w_uv=m_w_uv, w_br_dn=m_w_br_dn,
              w_br_mla=m_w_br_mla, w_o=m_w_o, ln1_g=m_ln1_g, ln1_b=m_ln1_b, w_ffn_in=m_w_ffn_in, w_ffn_out=m_w_ffn_out,
              w_ple=m_w_ple, w_ple_gate=m_w_ple_gate, ln2_g=m_ln2_g, ln2_b=m_ln2_b)
    vs = dict(w_in=v_w_in, conv_w=v_conv_w, dn_a_log=v_dn_a_log, dn_dt_bias=v_dn_dt_bias, dn_norm_w=v_dn_norm_w,
              q_norm_w=v_q_norm_w, w_uq=v_w_uq, kv_norm_w=v_kv_norm_w, w_uk=v_w_uk, w_uv=v_w_uv, w_br_dn=v_w_br_dn,
              w_br_mla=v_w_br_mla, w_o=v_w_o, ln1_g=v_ln1_g, ln1_b=v_ln1_b, w_ffn_in=v_w_ffn_in, w_ffn_out=v_w_ffn_out,
              w_ple=v_w_ple, w_ple_gate=v_w_ple_gate, ln2_g=v_ln2_g, ln2_b=v_ln2_b)
    mx, my, mc = lax.axis_index("x"), lax.axis_index("y"), lax.axis_index("c")

    me_chip = 2 * mx + my
    c_arr = jnp.reshape(mc, (1,)).astype(jnp.int32)
    me_arr = jnp.reshape(me_chip, (1,)).astype(jnp.int32)
    sharded = ('w_in', 'w_ffn_in') + tuple(name for name, _, _ in _FLATB_PIECES)

    local = _pack_shards({name: ws[name][0] for name in sharded}, conv_w[0])
    (gathered_in,) = _gather_shards(local[:1], "gather_w_in")
    w_in_full, conv_full = _unpack_w_in(gathered_in, local[0], me_chip)
    small = {n: ws[n][0] for n in _SMALL_NAMES}
    small['conv_w'] = conv_full
    sp = _prep_small(small)
    cosb, sinb = _rope_tables(positions[0])
    exch = _Exchange(local[1:], me_chip, c_arr)

    loss_lanes, dx, g = _local_step(x[0], p[0, 0], cosb, sinb, loss_target[0], _prep_w_in(w_in_full), sp, exch)
    gw = _unprep_grads_late(g)
    loss = lax.psum(jnp.sum(loss_lanes), ("x", "y", "c"))

    g_in = [_shard_columns(gw['w_in'], W_IN_SHARD)]
    got = _reduce_pair_exchange(g_in, "reduce_pair_exchange_w_in")
    part_in = _pair_add(g_in[0], got[0], c_arr, _ADD_TILES[0], "pair_add_0")
    parts = [part_in] + exch.parts
    arrived = list(_reduce_chip_exchange([part_in], "reduce_chip_exchange_w_in")) + exch.arrived
    mine = [_chip_add(p_, r_, me_arr, tr, "chip_add_%d" % i) for i, (p_, r_, tr) in enumerate(zip(parts, arrived, _ADD_TILES))]
    reduced = _unpack_reduced(mine, _reduce_pair_share(mine, "reduce_pair_share"), mc)
    tot = _small_allreduce(_pack_small(gw))
    gred = {name: reduced[name][None] for name in sharded}
    for i, n in enumerate(_SMALL_NAMES):
        gred[n] = tot[i * _SMALL_GROUP, :ws[n].shape[1]][None]
    conv_tot = tot[_CONV_SMALL_ROW:_CONV_SMALL_ROW + _CONV_SMALL_ROWS].reshape(DN_CONV, QKV_W)
    gred['conv_w'] = lax.dynamic_slice_in_dim(conv_tot, (2 * mx + my) * _CONV_SHARD, _CONV_SHARD, axis=1)[None]

    deltas, new_m, new_v = {}, {}, {}
    for n in _WEIGHT_NAMES:
        if n == 'w_in':
            tr_ = lambda a: jnp.transpose(a, (0, 2, 1))
            g_t = tr_(gred[n].reshape(ws[n].shape))
            outs = _adamw(tr_(ws[n]), g_t, tr_(ms[n]), tr_(vs[n]), "adamw_" + n)
            gred[n] = tr_(g_t)
            deltas[n], new_m[n], new_v[n] = (tr_(o) for o in outs)
            continue
        gred[n] = gred[n].reshape(ws[n].shape)
        deltas[n], new_m[n], new_v[n] = _adamw(ws[n], gred[n], ms[n], vs[n], "adamw_" + n)
    return (loss, dx[None], *[gred[n] for n in _WEIGHT_NAMES], *[deltas[n] for n in _WEIGHT_NAMES],
            *[new_m[n] for n in _WEIGHT_NAMES], *[new_v[n] for n in _WEIGHT_NAMES])
```

```python
import functools
import math

import jax
import jax.numpy as jnp
from jax import lax
from jax.experimental import pallas as pl
from jax.experimental.pallas import tpu as pltpu

F32 = jnp.float32
_CDT = jnp.bfloat16
_HI = lax.Precision.HIGHEST
_MESH = pl.DeviceIdType.MESH

D_MODEL = 1024
PLE_DIM = 256
HEADS = 8
DN_DK = 128
DN_CHUNK = 64
DN_CONV = 4
QKV_W = 3 * HEADS * DN_DK
Q_LORA = 384
KV_LORA = 256
NOPE = 128
ROPE = 64
ROPE_PAD = 128
FFN_HIDDEN = 2816
D_IN = 6864
ROPE_BASE = 10000.0
ALPHA = 2.0 ** 0.25
ATT_SCALE = (NOPE + ROPE) ** -0.5
NEG_BIG = -1e30
ADAM_LR, ADAM_B1, ADAM_B2, ADAM_EPS, ADAM_WD, ADAM_STEP = 0.001, 0.9, 0.999, 1e-08, 0.01, 10

LANE = 128
VMEM_LIMIT = 56 * 1024 * 1024
MM_VMEM_BUDGET = 40 * 1024 * 1024
N_SHARD = 4
FLAT_W = 1024
SMALL_ROWS = 88


def _tile(dim, cap):
    if dim <= cap:
        return dim
    t = (cap // LANE) * LANE
    while t >= LANE:
        if dim % t == 0:
            return t
        t -= LANE
    return dim


def _cparams(sem):
    return pltpu.CompilerParams(dimension_semantics=sem, vmem_limit_bytes=VMEM_LIMIT)


def _mm(a, b, *, name, ta=False, tb=False, add=None, add_scale=1.0, out_dtype=F32, heads=None,
        a_head=None, b_head=None, out_head=None, dims=None, tm=1408, tn=1408):
    m, n, k = dims
    tm, tn = _tile(m, tm), _tile(n, tn)
    sa, sb, so = a.dtype.itemsize, b.dtype.itemsize, jnp.dtype(out_dtype).itemsize

    def vmem_need(tk_):
        acc = tm * tn * 4 if tk_ < k else 0
        extra = 2 * tm * tn * 4 if add is not None else 0
        return 2 * (tm * tk_ * sa + tk_ * tn * sb) + 2 * tm * tn * so + acc + extra

    tk = k
    while vmem_need(tk) > MM_VMEM_BUDGET and tk > LANE:
        smaller = _tile(k, tk - LANE)
        if smaller >= tk:
            break
        tk = smaller
    nk = k // tk
    hgrid = () if heads is None else (heads,)
    off = len(hgrid)

    def spec(rows, cols, rtile, ctile, rsel, csel, layout):
        def idx(*g):
            h = g[0] if off else 0
            ri, ci = g[off + rsel], g[off + csel]
            if layout == 'lead':
                return (h, ri, ci)
            if layout == 'col':
                return (ri, h * (cols // ctile) + ci)
            return (ri, ci)
        if layout == 'lead':
            return pl.BlockSpec((None, rtile, ctile), idx)
        return pl.BlockSpec((rtile, ctile), idx)

    a_spec = spec(k, m, tk, tm, 2, 0, a_head) if ta else spec(m, k, tm, tk, 0, 2, a_head)
    b_spec = spec(n, k, tn, tk, 1, 2, b_head) if tb else spec(k, n, tk, tn, 2, 1, b_head)
    o_spec = spec(m, n, tm, tn, 0, 1, out_head)
    in_specs = [a_spec, b_spec]
    args = [a, b]
    if add is not None:
        in_specs.append(spec(m, n, tm, tn, 0, 1, out_head))
        args.append(add)
    dn = (((0 if ta else 1,), (1 if tb else 0,)), ((), ()))

    def body(*refs):
        a_ref, b_ref = refs[0], refs[1]
        prod = lax.dot_general(a_ref[...].astype(_CDT), b_ref[...].astype(_CDT), dn, preferred_element_type=F32)
        if nk == 1:
            o_ref = refs[-1]
            if add is not None:
                prod = prod + refs[2][...].astype(F32) * add_scale
            o_ref[...] = prod.astype(out_dtype)
            return
        o_ref, acc_ref = refs[-2], refs[-1]
        kk = pl.program_id(off + 2)

        @pl.when(kk == 0)
        def _():
            if add is not None:
                acc_ref[...] = refs[2][...].astype(F32) * add_scale
            else:
                acc_ref[...] = jnp.zeros_like(acc_ref)

        acc_ref[...] += prod

        @pl.when(kk == nk - 1)
        def _():
            o_ref[...] = acc_ref[...].astype(out_dtype)

    if out_head == 'lead':
        oshape = (heads, m, n)
    elif out_head == 'col':
        oshape = (m, heads * n)
    else:
        oshape = (m, n)
    sem = ("parallel",) * (off + 2) + ("arbitrary",)
    return pl.pallas_call(
        body, name=name, grid=hgrid + (m // tm, n // tn, nk), in_specs=in_specs, out_specs=o_spec,
        out_shape=jax.ShapeDtypeStruct(oshape, out_dtype),
        scratch_shapes=[pltpu.VMEM((tm, tn), F32)] if nk > 1 else [],
        compiler_params=_cparams(sem))(*args)


def _mm2(a, b, **kw):
    ta, tb = kw.get('ta', False), kw.get('tb', False)
    m = a.shape[1] if ta else a.shape[0]
    k = a.shape[0] if ta else a.shape[1]
    n = b.shape[0] if tb else b.shape[1]
    return _mm(a, b, dims=(m, n, k), **kw)


def _rowwise(fn, rows, bcast, outs, reds=(), *, name, tm=256, heads=None):
    t = rows[0][0].shape[0]
    tm = min(tm, t)
    hn = 1 if heads is None else heads
    in_specs, args = [], []
    for arr, width, base, per_head in rows:
        in_specs.append(pl.BlockSpec((tm, width), functools.partial(
            lambda i, h, base, per_head: (i, base + (h if per_head else 0)), base=base, per_head=per_head)))
        args.append(arr)
    for arr in bcast:
        in_specs.append(pl.BlockSpec(arr.shape, lambda i, h: (0, 0)))
        args.append(arr)
    out_specs, out_shape = [], []
    for total, width, per_head, dt in outs:
        out_specs.append(pl.BlockSpec((tm, width), functools.partial(
            lambda i, h, per_head: (i, h if per_head else 0), per_head=per_head)))
        out_shape.append(jax.ShapeDtypeStruct((t, total), dt))
    for shp in reds:
        out_specs.append(pl.BlockSpec(shp, lambda i, h: (0, 0)))
        out_shape.append(jax.ShapeDtypeStruct(shp, F32))
    n_in, n_out, n_red = len(args), len(outs), len(reds)

    def body(*refs):
        i, h = pl.program_id(0), pl.program_id(1)
        vals = fn(h, *[r[...] for r in refs[:n_in]])
        for r, v in zip(refs[n_in:n_in + n_out], vals[:n_out]):
            r[...] = v.astype(r.dtype)
        if n_red:
            @pl.when((i == 0) & (h == 0))
            def _():
                for r in refs[n_in + n_out:]:
                    r[...] = jnp.zeros_like(r)
            for r, v in zip(refs[n_in + n_out:], vals[n_out:]):
                r[...] += v

    sem = ("arbitrary", "arbitrary") if n_red else ("parallel", "parallel")
    res = pl.pallas_call(body, name=name, grid=(t // tm, hn), in_specs=in_specs, out_specs=out_specs,
                         out_shape=out_shape, compiler_params=_cparams(sem))(*args)
    return tuple(res)


def _sigmoid(x):
    return 1.0 / (1.0 + jnp.exp(-x))


def _silu(x):
    return x * _sigmoid(x)


def _softplus(x):
    return jnp.maximum(x, 0.0) + jnp.log(1.0 + jnp.exp(-jnp.abs(x)))


def _layer_norm(t, g, b):
    mu = jnp.mean(t, axis=-1, keepdims=True)
    d = t - mu
    var = jnp.mean(d * d, axis=-1, keepdims=True)
    return d * lax.rsqrt(var + 1e-5) * g + b


def _rms_norm(t, w):
    return t * lax.rsqrt(jnp.mean(t * t, axis=-1, keepdims=True) + 1e-6) * w


def _swap_rope_halves(t):
    lane = lax.broadcasted_iota(jnp.int32, t.shape, 1) % ROPE_PAD
    n = t.shape[1]
    up = pltpu.roll(t, n - ROPE // 2, axis=1)
    dn = pltpu.roll(t, ROPE // 2, axis=1)
    return jnp.where(lane < ROPE // 2, up, jnp.where(lane < ROPE, dn, 0.0))


def _rope(t, cosb, sinb):
    reps = t.shape[1] // ROPE_PAD
    c = jnp.tile(cosb, (1, reps)) if reps > 1 else cosb
    s = jnp.tile(sinb, (1, reps)) if reps > 1 else sinb
    return t * c + _swap_rope_halves(t) * s


def _rope_bwd(d, cosb, sinb):
    reps = d.shape[1] // ROPE_PAD
    c = jnp.tile(cosb, (1, reps)) if reps > 1 else cosb
    s = jnp.tile(sinb, (1, reps)) if reps > 1 else sinb
    return d * c + _swap_rope_halves(d * s)


_CONV_ROWS = 256
_CONV_COLS = 256


def _conv_window(ref, r0, lo, hi, t):
    parts = []
    start, stop = r0 - lo, r0 + _CONV_ROWS + hi
    if start < 0:
        parts.append(jnp.zeros((-start, ref.shape[1]), F32))
        start = 0
    tail = max(stop - t, 0)
    parts.append(ref[start:stop - tail, :].astype(F32))
    if tail:
        parts.append(jnp.zeros((tail, ref.shape[1]), F32))
    return parts[0] if len(parts) == 1 else jnp.concatenate(parts, axis=0)


def _conv_taps(win, w_ref, n_out):
    acc = win[8:8 + n_out] * w_ref[DN_CONV - 1:DN_CONV, :]
    for i in range(DN_CONV - 1):
        acc = acc + pltpu.roll(win, DN_CONV - 1 - i, axis=0)[8:8 + n_out] * w_ref[i:i + 1, :]
    return acc


def _conv_silu(x, w):
    t, ch = x.shape

    def body(x_ref, w_ref, o_ref):
        for r in range(t // _CONV_ROWS):
            r0 = r * _CONV_ROWS
            c = _conv_taps(_conv_window(x_ref, r0, 8, 0, t), w_ref, _CONV_ROWS)
            o_ref[r0:r0 + _CONV_ROWS, :] = _silu(c)

    return pl.pallas_call(
        body, name="conv_silu", grid=(ch // _CONV_COLS,),
        in_specs=[pl.BlockSpec((t, _CONV_COLS), lambda j: (0, j)), pl.BlockSpec((DN_CONV, _CONV_COLS), lambda j: (0, j))],
        out_specs=pl.BlockSpec((t, _CONV_COLS), lambda j: (0, j)),
        out_shape=jax.ShapeDtypeStruct((t, ch), F32), compiler_params=_cparams(("parallel",)))(x, w)


def _conv_silu_bwd(x, w, dys):
    t, ch = x.shape
    per = ch // len(dys) // _CONV_COLS

    def body(x_ref, w_ref, *rest):
        dy_refs, (dx_ref, dw_ref) = rest[:len(dys)], rest[len(dys):]
        sec = pl.program_id(0) // per
        dws = [jnp.zeros((1, _CONV_COLS), F32) for _ in range(DN_CONV)]
        for r in range(t // _CONV_ROWS):
            r0 = r * _CONV_ROWS
            n_ext = _CONV_ROWS + 8
            xw = _conv_window(x_ref, r0, 8, 8, t)
            c = _conv_taps(xw, w_ref, n_ext)
            sg = _sigmoid(c)
            dy = _conv_window(dy_refs[-1], r0, 0, 8, t)
            for k in range(len(dys) - 2, -1, -1):
                dy = jnp.where(sec == k, _conv_window(dy_refs[k], r0, 0, 8, t), dy)
            ds = dy * (sg * (1.0 + c * (1.0 - sg)))
            dx = ds[:_CONV_ROWS] * w_ref[DN_CONV - 1:DN_CONV, :]
            for i in range(DN_CONV - 1):
                sh = DN_CONV - 1 - i
                dx = dx + pltpu.roll(ds, n_ext - sh, axis=0)[:_CONV_ROWS] * w_ref[i:i + 1, :]
            dx_ref[r0:r0 + _CONV_ROWS, :] = dx.astype(dx_ref.dtype)
            ds0 = ds[:_CONV_ROWS]
            for i in range(DN_CONV):
                sh = DN_CONV - 1 - i
                xs = xw if sh == 0 else pltpu.roll(xw, sh, axis=0)
                dws[i] = dws[i] + jnp.sum(ds0 * xs[8:8 + _CONV_ROWS], axis=0, keepdims=True)
        for i in range(DN_CONV):
            dw_ref[i:i + 1, :] = dws[i]

    blk = pl.BlockSpec((t, _CONV_COLS), lambda j: (0, j))
    wblk = pl.BlockSpec((DN_CONV, _CONV_COLS), lambda j: (0, j))
    dy_specs = [pl.BlockSpec((t, _CONV_COLS), functools.partial(lambda j, k: (0, jnp.clip(j - k * per, 0, per - 1)), k=k))
                for k in range(len(dys))]
    return pl.pallas_call(
        body, name="conv_silu_bwd", grid=(ch // _CONV_COLS,), in_specs=[blk, wblk] + dy_specs, out_specs=[blk, wblk],
        out_shape=[jax.ShapeDtypeStruct((t, ch), _CDT), jax.ShapeDtypeStruct((DN_CONV, ch), F32)],
        compiler_params=_cparams(("arbitrary",)))(x, w, *dys)


_PA_ROWS = 512


def _bmm(a, b, spec, exact=False):
    if exact:
        return jnp.einsum(spec, a, b, precision=_HI, preferred_element_type=F32)
    return jnp.einsum(spec, a.astype(_CDT), b.astype(_CDT), preferred_element_type=F32)


def _split16(a):
    hi = a.astype(jnp.bfloat16)
    return hi, (a - hi.astype(F32)).astype(jnp.bfloat16)


def _bmm3(a, b, spec):
    ah, al = _split16(a)
    bh, bl = _split16(b)
    e = lambda p, q: jnp.einsum(spec, p, q, preferred_element_type=F32)
    b_axes, out_axes = spec.split(',')[1].split('->')
    k_axis = b_axes.index(out_axes[-1])
    n = b.shape[k_axis]
    if 2 * n == LANE:
        both = e(ah, jnp.concatenate([bh, bl], axis=k_axis))
        return (both[..., :n] + both[..., n:]) + e(al, bh)
    return e(ah, bh) + (e(ah, bl) + e(al, bh))


def _tri_inverse(l_mat, eye):
    pw = -l_mat
    t_inv = eye + pw
    for _ in range(5):
        pw = _bmm3(pw, pw, 'bij,bjk->bik')
        t_inv = t_inv + _bmm3(t_inv, pw, 'bij,bjk->bik')
    return t_inv


@jax.custom_vjp
def _tri_inverse_saved(l_mat, t_saved):
    return t_saved


def _tri_inverse_saved_fwd(l_mat, t_saved):
    return t_saved, t_saved


def _tri_inverse_saved_bwd(t_saved, dt):
    left = _bmm3(t_saved, dt, 'bji,bjk->bik')
    return -_bmm3(left, t_saved, 'bij,bkj->bik'), jnp.zeros_like(t_saved)


_tri_inverse_saved.defvjp(_tri_inverse_saved_fwd, _tri_inverse_saved_bwd)


def _phase_a(h, q, k, v, ba, alog, dtb, t_saved=None):
    r = q.shape[0]
    nb = r // DN_CHUNK
    c = DN_CHUNK
    lane = lax.broadcasted_iota(jnp.int32, (1, LANE), 1)
    selb = (lane == h).astype(F32)
    sela = (lane == h + HEADS).astype(F32)
    b_raw = jnp.sum(ba * selb, axis=1, keepdims=True)
    a_raw = jnp.sum(ba * sela, axis=1, keepdims=True)
    al = jnp.sum(alog * selb, axis=1, keepdims=True)
    dt = jnp.sum(dtb * selb, axis=1, keepdims=True)
    beta = jnp.broadcast_to(_sigmoid(b_raw), (r, LANE))
    g = jnp.broadcast_to(-jnp.exp(al) * _softplus(a_raw + dt), (r, LANE))
    qn = q * lax.rsqrt(jnp.sum(q * q, -1, keepdims=True) + 1e-6) * (DN_DK ** -0.5)
    kn = k * lax.rsqrt(jnp.sum(k * k, -1, keepdims=True) + 1e-6)
    q3, k3, v3 = qn.reshape(nb, c, LANE), kn.reshape(nb, c, LANE), v.reshape(nb, c, LANE)
    b3, g3 = beta.reshape(nb, c, LANE), g.reshape(nb, c, LANE)
    ri = lax.broadcasted_iota(jnp.int32, (nb, c, c), 1)
    ci = lax.broadcasted_iota(jnp.int32, (nb, c, c), 2)
    tril, strict = ri >= ci, ri > ci
    gc = _bmm(tril.astype(F32), g3, 'bij,bjd->bid', exact=True)
    onehot = (lax.broadcasted_iota(jnp.int32, (nb, c, LANE), 2) == 0).astype(F32)
    g_row = _bmm(onehot, gc, 'bid,bjd->bij', exact=True)
    diff = gc[:, :, :c] - g_row
    decay = jnp.where(tril, jnp.exp(jnp.where(tril, diff, 0.0)), 0.0)
    kb = k3 * b3
    l_mat = jnp.where(strict, _bmm(kb, k3, 'bid,bjd->bij') * decay, 0.0)
    if t_saved is None:
        t_inv = _tri_inverse(l_mat, (ri == ci).astype(F32))
    else:
        t_inv = _tri_inverse_saved(l_mat, t_saved.reshape(nb, c, c))
    eg = jnp.exp(gc)
    u = _bmm(t_inv, v3 * b3, 'bij,bje->bie')
    w = _bmm(t_inv, kb * eg, 'bij,bje->bie')
    intra = jnp.where(tril, _bmm(q3, k3, 'bid,bjd->bij') * decay, 0.0)
    qd = q3 * eg
    gl = jnp.sum(g3, axis=1, keepdims=True)
    kt = k3 * jnp.exp(gl - gc)
    outs = (u.reshape(r, LANE), w.reshape(r, LANE), qd.reshape(r, LANE), kt.reshape(r, LANE),
            intra.reshape(r, c), gl.reshape(nb, LANE))
    return outs + (t_inv.reshape(r, c),) if t_saved is None else outs


def _pa_specs(t):
    rr = min(_PA_ROWS, t)
    nb = rr // DN_CHUNK
    qkv = [pl.BlockSpec((rr, LANE), functools.partial(lambda i, h, o: (i, o + h), o=o)) for o in (0, HEADS, 2 * HEADS)]
    ba = pl.BlockSpec((rr, LANE), lambda i, h: (i, 3))
    vec = pl.BlockSpec((1, LANE), lambda i, h: (0, 0))
    row = pl.BlockSpec((rr, LANE), lambda i, h: (i, h))
    intra = pl.BlockSpec((None, rr, DN_CHUNK), lambda i, h: (h, i, 0))
    gl = pl.BlockSpec((nb, LANE), lambda i, h: (i, h))
    return rr, qkv, ba, vec, row, intra, gl


def _grid_ends(grid):
    first = lambda: functools.reduce(jnp.logical_and, [pl.program_id(a) == 0 for a in range(len(grid))])
    last = lambda: functools.reduce(jnp.logical_and, [pl.program_id(a) == n - 1 for a, n in enumerate(grid)])
    return first, last


def _delta_local(qkv_act, pm, alog, dtb, rider=None):
    t = qkv_act.shape[0]
    rr, qkv, ba, vec, row, intra, gl = _pa_specs(t)

    def body(q, k, v, b, al, dt, *outs):
        vals = _phase_a(pl.program_id(1), q[...], k[...], v[...], b[...], al[...], dt[...])
        for o, val in zip(outs, vals):
            o[...] = val

    wide = jax.ShapeDtypeStruct((t, HEADS * LANE), F32)
    sq = jax.ShapeDtypeStruct((HEADS, t, DN_CHUNK), F32)
    grid = (t // rr, HEADS)
    return _carried_call(
        body, rider, *_grid_ends(grid), name="delta_local", grid=grid, in_specs=qkv + [ba, vec, vec],
        out_specs=[row] * 4 + [intra, gl, intra],
        out_shape=[wide] * 4 + [sq, jax.ShapeDtypeStruct((t // DN_CHUNK, HEADS * LANE), F32), sq],
        scratch_shapes=[], sem=("arbitrary", "arbitrary"), args=(qkv_act, qkv_act, qkv_act, pm, alog, dtb))


def _delta_local_bwd(qkv_act, pm, alog, dtb, t_inv, du, dw, dqd, dkt, dintra, dgl, rider=None):
    t = qkv_act.shape[0]
    rr, qkv, ba, vec, row, intra, gl = _pa_specs(t)

    def body(q, k, v, b, al, dt, ti, du_r, dw_r, dqd_r, dkt_r, di_r, dgl_r, dq_o, dk_o, dv_o, dba_o, dal_o, ddt_o):
        i, h = pl.program_id(0), pl.program_id(1)
        t_saved = ti[...]
        _, vjp = jax.vjp(lambda *a: _phase_a(h, *a, t_saved=t_saved), q[...], k[...], v[...], b[...], al[...], dt[...])
        dq, dk, dv, dba, dal, ddt = vjp((du_r[...], dw_r[...], dqd_r[...], dkt_r[...], di_r[...], dgl_r[...]))
        dq_o[...], dk_o[...], dv_o[...] = dq, dk, dv

        @pl.when(h == 0)
        def _():
            dba_o[...] = jnp.zeros_like(dba_o)

        @pl.when((h == 0) & (i == 0))
        def _():
            dal_o[...] = jnp.zeros_like(dal_o)
            ddt_o[...] = jnp.zeros_like(ddt_o)

        dba_o[...] += dba
        dal_o[...] += dal
        ddt_o[...] += ddt

    wide = jax.ShapeDtypeStruct((t, HEADS * LANE), F32)
    vshape = jax.ShapeDtypeStruct((1, LANE), F32)
    grid = (t // rr, HEADS)
    return _carried_call(
        body, rider, *_grid_ends(grid), name="delta_local_bwd", grid=grid,
        in_specs=qkv + [ba, vec, vec, intra] + [row] * 4 + [intra, gl],
        out_specs=[row] * 3 + [pl.BlockSpec((rr, LANE), lambda i, h: (i, 0)), vec, vec],
        out_shape=[wide] * 3 + [jax.ShapeDtypeStruct((t, LANE), F32), vshape, vshape],
        scratch_shapes=[], sem=("arbitrary", "arbitrary"),
        args=(qkv_act, qkv_act, qkv_act, pm, alog, dtb, t_inv, du, dw, dqd, dkt, dintra, dgl))


_SCAN_ROWS = 512


def _dot(a, b, dn):
    return lax.dot_general(a.astype(_CDT), b.astype(_CDT), (dn, ((), ())), preferred_element_type=F32)


_NN = ((1,), (0,))
_NT = ((1,), (1,))
_TN = ((0,), (0,))


def _delta_scan(u, w, qd, kt, intra, gl, rider=None):
    t = u.shape[0]
    rr = min(_SCAN_ROWS, t)
    nc = rr // DN_CHUNK

    def body(u_ref, w_ref, qd_ref, kt_ref, a_ref, gl_ref, o_ref, sall_ref, s_scr):
        @pl.when(pl.program_id(0) == 0)
        def _():
            s_scr[...] = jnp.zeros_like(s_scr)

        def chunk(c, carry):
            r0 = pl.multiple_of(c * DN_CHUNK, DN_CHUNK)
            rows = pl.ds(r0, DN_CHUNK)
            e = jnp.exp(gl_ref[pl.ds(c, 1), :])
            states = [s_scr[h] for h in range(HEADS)]
            u_c, w_c, qd_c, kt_c = u_ref[rows, :], w_ref[rows, :], qd_ref[rows, :], kt_ref[rows, :]
            a_c = [a_ref[h, rows, :] for h in range(HEADS)]
            o_new, s_new = [], []
            for h in range(HEADS):
                cs = slice(h * LANE, (h + 1) * LANE)
                s = states[h]
                v_new = u_c[:, cs] - _dot(w_c[:, cs], s, _NN)
                o_new.append(_dot(qd_c[:, cs], s, _NN) + _dot(a_c[h], v_new, _NN))
                s_new.append(s * e[:, cs] + _dot(kt_c[:, cs], v_new, _TN))
            o_ref[rows, :] = jnp.concatenate(o_new, axis=1)
            for h in range(HEADS):
                sall_ref[c, h] = states[h]
                s_scr[h] = s_new[h]
            return carry

        lax.fori_loop(0, nc, chunk, 0)

    row = pl.BlockSpec((rr, HEADS * LANE), lambda i: (i, 0))
    grid = (t // rr,)
    return _carried_call(
        body, rider, *_grid_ends(grid), name="delta_scan", grid=grid,
        in_specs=[row] * 4 + [pl.BlockSpec((HEADS, rr, DN_CHUNK), lambda i: (0, i, 0)),
                              pl.BlockSpec((nc, HEADS * LANE), lambda i: (i, 0))],
        out_specs=[row, pl.BlockSpec((nc, HEADS, LANE, LANE), lambda i: (i, 0, 0, 0))],
        out_shape=[jax.ShapeDtypeStruct((t, HEADS * LANE), F32),
                   jax.ShapeDtypeStruct((t // DN_CHUNK, HEADS, LANE, LANE), F32)],
        scratch_shapes=[pltpu.VMEM((HEADS, LANE, LANE), F32)], sem=("arbitrary",), args=(u, w, qd, kt, intra, gl))


def _delta_scan_bwd(u, w, qd, kt, intra, gl, sall, do, rider=None):
    t = u.shape[0]
    rr = min(_SCAN_ROWS, t)
    nc = rr // DN_CHUNK
    ng = t // rr

    def body(u_ref, w_ref, qd_ref, kt_ref, a_ref, gl_ref, sall_ref, do_ref,
             du_ref, dw_ref, dqd_ref, dkt_ref, da_ref, dgl_ref, ds_scr):
        @pl.when(pl.program_id(0) == 0)
        def _():
            ds_scr[...] = jnp.zeros_like(ds_scr)

        def chunk(cc, carry):
            c = nc - 1 - cc
            r0 = pl.multiple_of(c * DN_CHUNK, DN_CHUNK)
            rows = pl.ds(r0, DN_CHUNK)
            e = jnp.exp(gl_ref[pl.ds(c, 1), :])
            states = [sall_ref[c, h] for h in range(HEADS)]
            ds_outs = [ds_scr[h] for h in range(HEADS)]
            u_a, w_a, kt_a, qd_a, do_a = u_ref[rows, :], w_ref[rows, :], kt_ref[rows, :], qd_ref[rows, :], do_ref[rows, :]
            a_a = [a_ref[h, rows, :] for h in range(HEADS)]
            da, dqd, dkt, du, dw, dgl, ds_new = [], [], [], [], [], [], []
            for h in range(HEADS):
                cs = slice(h * LANE, (h + 1) * LANE)
                s, ds_out = states[h], ds_outs[h]
                w_c, kt_c, qd_c, do_c = w_a[:, cs], kt_a[:, cs], qd_a[:, cs], do_a[:, cs]
                v_new = u_a[:, cs] - _dot(w_c, s, _NN)
                dv_new = _dot(a_a[h], do_c, _TN) + _dot(kt_c, ds_out, _NN)
                da.append(_dot(do_c, v_new, _NT))
                dqd.append(_dot(do_c, s, _NT))
                dkt.append(_dot(v_new, ds_out, _NT))
                du.append(dv_new)
                dw.append(-_dot(dv_new, s, _NT))
                eh = e[:, cs]
                dgl.append(jnp.broadcast_to(jnp.sum(ds_out * s, axis=0, keepdims=True) * eh, (8, LANE)))
                ds_new.append(ds_out * eh + _dot(qd_c, do_c, _TN) - _dot(w_c, dv_new, _TN))
            cat = lambda parts: jnp.concatenate(parts, axis=1)
            dqd_ref[rows, :], dkt_ref[rows, :], du_ref[rows, :], dw_ref[rows, :] = cat(dqd), cat(dkt), cat(du), cat(dw)
            dgl_ref[pl.ds(pl.multiple_of(c * 8, 8), 8), :] = cat(dgl)
            for h in range(HEADS):
                da_ref[h, rows, :] = da[h]
                ds_scr[h] = ds_new[h]
            return carry

        lax.fori_loop(0, nc, chunk, 0)

    rev = lambda i: (ng - 1 - i, 0)
    row = pl.BlockSpec((rr, HEADS * LANE), rev)
    a_spec = pl.BlockSpec((HEADS, rr, DN_CHUNK), lambda i: (0, ng - 1 - i, 0))
    gl_spec = pl.BlockSpec((nc, HEADS * LANE), rev)
    wide = jax.ShapeDtypeStruct((t, HEADS * LANE), F32)
    outs, carried = _carried_call(
        body, rider, *_grid_ends((ng,)), name="delta_scan_bwd", grid=(ng,),
        in_specs=[row] * 4 + [a_spec, gl_spec, pl.BlockSpec((nc, HEADS, LANE, LANE), lambda i: (ng - 1 - i, 0, 0, 0)), row],
        out_specs=[row] * 4 + [a_spec, pl.BlockSpec((nc * 8, HEADS * LANE), rev)],
        out_shape=[wide] * 4 + [jax.ShapeDtypeStruct((HEADS, t, DN_CHUNK), F32),
                                jax.ShapeDtypeStruct((t // DN_CHUNK * 8, HEADS * LANE), F32)],
        scratch_shapes=[pltpu.VMEM((HEADS, LANE, LANE), F32)], sem=("arbitrary",), args=(u, w, qd, kt, intra, gl, sall, do))
    return tuple(outs[:5]) + (outs[5].reshape(t // DN_CHUNK, 8, HEADS * LANE)[:, 0, :],), carried


_ATT_TILE = 512


def _kv_rows(j, tk):
    return pl.ds(pl.multiple_of(j * tk, tk), tk)


def _att_scores(ql, qr, ckv_ref, kr_ref, j, tk):
    ks = _kv_rows(j, tk)
    return (_dot(ql, ckv_ref[ks, :], _NT) + _dot(qr, kr_ref[ks, :], _NT)) * ATT_SCALE


def _diag_mask(s):
    qi = lax.broadcasted_iota(jnp.int32, s.shape, 0)
    ki = lax.broadcasted_iota(jnp.int32, s.shape, 1)
    return jnp.where(ki <= qi, s, NEG_BIG)


def _attention(ql, qr, ckv, kr):
    t = ckv.shape[0]
    tq = min(_ATT_TILE, t)

    def body(ql_ref, qr_ref, ckv_ref, kr_ref, o_ref, lse_ref, m_scr, l_scr, acc_scr, s_scr):
        qi = pl.program_id(1)
        q_lat, q_rope = ql_ref[...], qr_ref[...]
        m_scr[...] = jnp.full_like(m_scr, NEG_BIG)
        l_scr[...] = jnp.zeros_like(l_scr)
        acc_scr[...] = jnp.zeros_like(acc_scr)

        def consume(j, s):
            m_old = m_scr[...]
            m_new = jnp.maximum(m_old, jnp.max(s, axis=-1, keepdims=True))
            p = jnp.exp(s - m_new)
            alpha = jnp.exp(m_old - m_new)
            l_scr[...] = l_scr[...] * alpha + jnp.sum(p, axis=-1, keepdims=True)
            acc_scr[...] = acc_scr[...] * alpha + _dot(p, ckv_ref[_kv_rows(j, tq), :], _NN)
            m_scr[...] = m_new

        s_scr[0] = _att_scores(q_lat, q_rope, ckv_ref, kr_ref, 0, tq)

        def loop_body(j, carry):
            s_scr[(j + 1) % 2] = _att_scores(q_lat, q_rope, ckv_ref, kr_ref, j + 1, tq)
            consume(j, s_scr[j % 2])
            return carry

        lax.fori_loop(0, qi, loop_body, 0)
        consume(qi, _diag_mask(s_scr[qi % 2]))
        o_ref[...] = acc_scr[...] / l_scr[...]
        lse_ref[...] = m_scr[...] + jnp.log(l_scr[...])

    return pl.pallas_call(
        body, name="attention", grid=(HEADS, t // tq),
        in_specs=[pl.BlockSpec((None, tq, KV_LORA), lambda h, i: (h, i, 0)),
                  pl.BlockSpec((tq, ROPE_PAD), lambda h, i: (i, h)),
                  pl.BlockSpec((t, KV_LORA), lambda h, i: (0, 0)),
                  pl.BlockSpec((t, ROPE_PAD), lambda h, i: (0, 0))],
        out_specs=[pl.BlockSpec((None, tq, KV_LORA), lambda h, i: (h, i, 0)),
                   pl.BlockSpec((None, tq, 1), lambda h, i: (h, i, 0))],
        out_shape=[jax.ShapeDtypeStruct((HEADS, t, KV_LORA), F32), jax.ShapeDtypeStruct((HEADS, t, 1), F32)],
        scratch_shapes=[pltpu.VMEM((tq, 1), F32), pltpu.VMEM((tq, 1), F32), pltpu.VMEM((tq, KV_LORA), F32),
                        pltpu.VMEM((2, tq, tq), F32)],
        compiler_params=_cparams(("parallel", "parallel")))(ql, qr, ckv, kr)


def _attention_bwd(ql, qr, ckv, kr, out, lse, dout):
    t = ckv.shape[0]
    tq = min(_ATT_TILE, t)

    def body(ql_ref, qr_ref, ckv_ref, kr_ref, o_ref, lse_ref, do_ref, dql_ref, dqr_ref, dckv_ref, dkr_ref,
             dql_scr, dqr_scr):
        h, qi = pl.program_id(0), pl.program_id(1)

        @pl.when((h == 0) & (qi == 0))
        def _():
            dckv_ref[...] = jnp.zeros_like(dckv_ref)
            dkr_ref[...] = jnp.zeros_like(dkr_ref)

        q_lat, q_rope = ql_ref[...], qr_ref[...]
        d_o = do_ref[...].astype(_CDT)
        lse_v = lse_ref[...]
        dsum = jnp.sum(do_ref[...] * o_ref[...], axis=-1, keepdims=True)
        dql_scr[...] = jnp.zeros_like(dql_scr)
        dqr_scr[...] = jnp.zeros_like(dqr_scr)

        def step(j, masked):
            ks = _kv_rows(j, tq)
            s = _att_scores(q_lat, q_rope, ckv_ref, kr_ref, j, tq)
            if masked:
                s = _diag_mask(s)
            p = jnp.exp(s - lse_v)
            kv = ckv_ref[ks, :]
            ds = (p * (_dot(d_o, kv, _NT) - dsum) * ATT_SCALE).astype(_CDT)
            pb = p.astype(_CDT)
            dql_scr[...] += _dot(ds, kv, _NN)
            dqr_scr[...] += _dot(ds, kr_ref[ks, :], _NN)
            dckv_ref[ks, :] += _dot(pb, d_o, _TN) + _dot(ds, q_lat, _TN)
            dkr_ref[ks, :] += _dot(ds, q_rope, _TN)

        def loop_body(j, carry):
            step(j, False)
            return carry

        lax.fori_loop(0, qi, loop_body, 0)
        step(qi, True)
        dql_ref[...] = dql_scr[...].astype(dql_ref.dtype)
        dqr_ref[...] = dqr_scr[...]

    lat = pl.BlockSpec((None, tq, KV_LORA), lambda h, i: (h, i, 0))
    rope = pl.BlockSpec((tq, ROPE_PAD), lambda h, i: (i, h))
    kfull = pl.BlockSpec((t, KV_LORA), lambda h, i: (0, 0))
    rfull = pl.BlockSpec((t, ROPE_PAD), lambda h, i: (0, 0))
    return pl.pallas_call(
        body, name="attention_bwd", grid=(HEADS, t // tq),
        in_specs=[lat, rope, kfull, rfull, lat, pl.BlockSpec((None, tq, 1), lambda h, i: (h, i, 0)), lat],
        out_specs=[lat, rope, kfull, rfull],
        out_shape=[jax.ShapeDtypeStruct((HEADS, t, KV_LORA), _CDT), jax.ShapeDtypeStruct((t, HEADS * ROPE_PAD), F32),
                   jax.ShapeDtypeStruct((t, KV_LORA), F32), jax.ShapeDtypeStruct((t, ROPE_PAD), F32)],
        scratch_shapes=[pltpu.VMEM((tq, KV_LORA), F32), pltpu.VMEM((tq, ROPE_PAD), F32)],
        compiler_params=_cparams(("arbitrary", "arbitrary")))(ql, qr, ckv, kr, out, lse, dout)


_DX_ROWS = 256


def _dx_fused(pairs, add, add_scale, rider=None):
    t, d = add.shape
    tm = min(_DX_ROWS, t)
    n = len(pairs)

    def body(*refs):
        acc = refs[2 * n][...] * add_scale
        for i in range(n):
            acc = acc + _dot(refs[i][...], refs[n + i][...], _NT)
        refs[2 * n + 1][...] = acc

    in_specs = [pl.BlockSpec((tm, a.shape[1]), lambda i: (i, 0)) for a, _ in pairs]
    in_specs += [pl.BlockSpec(w.shape, lambda i: (0, 0)) for _, w in pairs]
    row = pl.BlockSpec((tm, d), lambda i: (i, 0))
    grid = (t // tm,)
    (dx,), carried = _carried_call(
        body, rider, *_grid_ends(grid), name="b_dx", grid=grid, in_specs=in_specs + [row], out_specs=[row],
        out_shape=[jax.ShapeDtypeStruct((t, d), F32)], scratch_shapes=[], sem=("arbitrary",),
        args=tuple(a for a, _ in pairs) + tuple(w for _, w in pairs) + (add,))
    return dx, carried


def _ffn_in_swiglu(a, w):
    t, k = a.shape
    hid = w.shape[1] // 2
    tm, tn = _tile(t, 1024), _tile(hid, 1408)
    nj = hid // tn

    def body(a_ref, bg_ref, bu_ref, act_ref, gt_ref, up_ref):
        av = a_ref[...].astype(_CDT)
        gt = jnp.dot(av, bg_ref[...].astype(_CDT), preferred_element_type=F32)
        up = jnp.dot(av, bu_ref[...].astype(_CDT), preferred_element_type=F32)
        act_ref[...] = _swiglu(gt, up).astype(act_ref.dtype)
        gt_ref[...] = gt.astype(gt_ref.dtype)
        up_ref[...] = up.astype(up_ref.dtype)

    out = pl.BlockSpec((tm, tn), lambda i, j: (i, j))
    return pl.pallas_call(
        body, name="f_ffn_in_swiglu", grid=(t // tm, nj),
        in_specs=[pl.BlockSpec((tm, k), lambda i, j: (i, 0)), pl.BlockSpec((k, tn), lambda i, j: (0, j)),
                  pl.BlockSpec((k, tn), lambda i, j: (0, nj + j))],
        out_specs=[out] * 3, out_shape=[jax.ShapeDtypeStruct((t, hid), _CDT)] * 3,
        compiler_params=_cparams(("parallel", "parallel")))(a, w, w)


def _gated_norm(o, z, w):
    return _rms_norm(o, w) * _silu(z)


def _gated_norm_heads(o, z, w):
    heads = [_gated_norm(o[:, h * LANE:(h + 1) * LANE], z[:, h * LANE:(h + 1) * LANE], w) for h in range(HEADS)]
    return jnp.concatenate(heads, axis=1)


def _mla_pre(ckv, krp, cq, cosb, sinb, qw, kw):
    return _rms_norm(cq, qw), _rms_norm(ckv, kw), _rope(krp, cosb, sinb)


def _merge(gg, y_dn, y_mla):
    return _sigmoid(gg[:, :D_MODEL]) * y_dn + _sigmoid(gg[:, D_MODEL:]) * y_mla


def _ln1(xv, attn_out, g, b):
    return _layer_norm(ALPHA * xv + attn_out, g, b)


def _final(h1, ffn, gate_pre, ple_proj, g, b):
    return _layer_norm(ALPHA * h1 + ffn + _sigmoid(gate_pre) * ple_proj, g, b)


def _swiglu(gt, up):
    return _silu(gt) * up


def _local_step(x, p, cosb, sinb, target, wt, sp, exch):
    t = x.shape[0]
    bf = _CDT
    xb = x.astype(bf)
    g = {}

    qkv_pre = _mm2(xb, wt['qkv'], name="f_qkv")
    z = _mm2(xb, wt['z'], name="f_z")
    gg = _mm2(xb, wt['gg'], name="f_gg")
    pm = _mm2(xb, wt['mla'], name="f_mla")
    qkv_act = _conv_silu(qkv_pre, sp['conv_w'])
    (u, w_, qd, kt, intra, gl, t_inv), sent = _delta_local(qkv_act, pm, sp['a_log'], sp['dt_bias'], rider=exch.gather_send())
    (o_dn, sall), passed = _delta_scan(u, w_, qd, kt, intra, gl, rider=exch.gather_pass(sent))
    wt = dict(wt, **exch.weights(passed))
    (og,) = _rowwise(lambda h, o, zz, w: (_gated_norm_heads(o, zz, w),),
                     [(o_dn, D_MODEL, 0, False), (z, D_MODEL, 0, False)], [sp['dn_norm_w']],
                     [(D_MODEL, D_MODEL, False, bf)], name="f_gated_norm")
    y_dn = _mm2(og, wt['br_dn'], name="f_br_dn")

    c_q, c_kv, k_rope = _rowwise(
        lambda h, *a: _mla_pre(*a),
        [(pm, KV_LORA, 0, False), (pm, ROPE_PAD, 2, False), (pm, Q_LORA, 2, False),
         (cosb, ROPE_PAD, 0, False), (sinb, ROPE_PAD, 0, False)],
        [sp['q_norm_w'], sp['kv_norm_w']],
        [(Q_LORA, Q_LORA, False, bf), (KV_LORA, KV_LORA, False, bf), (ROPE_PAD, ROPE_PAD, False, bf)], name="f_mla_pre")
    q_nope = _mm2(c_q, wt['uq_nope'], name="f_uq_nope", out_dtype=bf)
    q_rope_pre = _mm2(c_q, wt['uq_rope'], name="f_uq_rope")
    (q_rope,) = _rowwise(lambda h, q, c, s: (_rope(q, c, s),),
                         [(q_rope_pre, HEADS * ROPE_PAD, 0, False), (cosb, ROPE_PAD, 0, False), (sinb, ROPE_PAD, 0, False)],
                         [], [(HEADS * ROPE_PAD, HEADS * ROPE_PAD, False, bf)], name="f_q_rope")
    q_lat = _mm(q_nope, wt['uk'], name="f_q_lat", tb=True, heads=HEADS, a_head='col', b_head='lead', out_head='lead',
                dims=(t, KV_LORA, NOPE), out_dtype=bf)
    out_lat, lse = _attention(q_lat, q_rope, c_kv, k_rope)
    o_mla = _mm(out_lat, wt['uv'], name="f_o_mla", heads=HEADS, a_head='lead', b_head='lead', out_head='col',
                dims=(t, NOPE, KV_LORA), out_dtype=bf)
    y_mla = _mm2(o_mla, wt['br_mla'], name="f_br_mla")

    (mixed,) = _rowwise(lambda h, *a: (_merge(*a),),
                        [(gg, 2 * D_MODEL, 0, False), (y_dn, D_MODEL, 0, False), (y_mla, D_MODEL, 0, False)],
                        [], [(D_MODEL, D_MODEL, False, bf)], name="f_merge")
    attn_out = _mm2(mixed, wt['o'], name="f_o")
    h1, h1b = _rowwise(lambda h, *a: (_ln1(*a),) * 2, [(x, D_MODEL, 0, False), (attn_out, D_MODEL, 0, False)],
                       [sp['ln1_g'], sp['ln1_b']], [(D_MODEL, D_MODEL, False, F32), (D_MODEL, D_MODEL, False, bf)],
                       name="f_ln1")
    act, ffn_gt, ffn_up = _ffn_in_swiglu(h1b, wt['ffn_in'])
    ffn = _mm2(act, wt['ffn_out'], name="f_ffn_out")
    gate_pre = _mm2(h1b, wt['ple_gate'], name="f_ple_gate")
    pb = p.astype(bf)
    ple_proj = _mm2(pb, wt['ple'], name="f_ple")

    def final_fn(h, h1v, ffnv, gpv, ppv, tgt, gv, bv):
        y, vjp = jax.vjp(_final, h1v, ffnv, gpv, ppv, gv, bv)
        err = y - tgt
        dh1, dffn, dgp, dpp, dg, db = vjp(err * (1.0 / D_MODEL))
        sq = err * err
        lanes = sq[:, :LANE]
        for j in range(1, D_MODEL // LANE):
            lanes = lanes + sq[:, j * LANE:(j + 1) * LANE]
        loss = jnp.sum(lanes, axis=0, keepdims=True) * (0.5 / D_MODEL)
        return dffn, dffn, dgp, dpp, dg, db, loss

    dpre2, dpre2b, dgate_pre, dple_proj, g['ln2_g'], g['ln2_b'], loss_lanes = _rowwise(
        final_fn, [(a, D_MODEL, 0, False) for a in (h1, ffn, gate_pre, ple_proj, target)],
        [sp['ln2_g'], sp['ln2_b']],
        [(D_MODEL, D_MODEL, False, F32)] + [(D_MODEL, D_MODEL, False, bf)] * 3,
        [(1, D_MODEL), (1, D_MODEL), (1, LANE)], name="b_final")

    g['ple'] = _mm2(pb, dple_proj, ta=True, name="g_ple")
    g['ple_gate'] = _mm2(h1b, dgate_pre, ta=True, name="g_ple_gate")
    g['ffn_out'] = _mm2(act, dpre2b, ta=True, name="g_ffn_out")
    dact = _mm2(dpre2b, wt['ffn_out'], tb=True, name="b_dact", out_dtype=bf)

    def swiglu_bwd(h, gt, up, d):
        _, vjp = jax.vjp(_swiglu, gt.astype(F32), up.astype(F32))
        dgt, dup = vjp(d.astype(F32))
        return (jnp.concatenate([dgt, dup], axis=1),)

    (dffn_in,) = _rowwise(swiglu_bwd, [(ffn_gt, FFN_HIDDEN, 0, False), (ffn_up, FFN_HIDDEN, 0, False),
                                       (dact, FFN_HIDDEN, 0, False)], [],
                          [(2 * FFN_HIDDEN, 2 * FFN_HIDDEN, False, bf)], name="b_swiglu")
    g['ffn_in'] = _mm2(h1b, dffn_in, ta=True, name="g_ffn_in")
    dh1 = _mm2(dffn_in, wt['ffn_in'], tb=True, name="b_dh1_ffn", add=dpre2, add_scale=ALPHA)
    dh1 = _mm2(dgate_pre, wt['ple_gate'], tb=True, name="b_dh1_gate", add=dh1)

    def ln1_bwd(h, xv, ao, d, gv, bv):
        _, vjp = jax.vjp(_ln1, xv, ao, gv, bv)
        _, dao, dg, db = vjp(d)
        return dao, dao, dg, db

    dpre1, dpre1b, g['ln1_g'], g['ln1_b'] = _rowwise(
        ln1_bwd, [(x, D_MODEL, 0, False), (attn_out, D_MODEL, 0, False), (dh1, D_MODEL, 0, False)],
        [sp['ln1_g'], sp['ln1_b']], [(D_MODEL, D_MODEL, False, F32), (D_MODEL, D_MODEL, False, bf)],
        [(1, D_MODEL), (1, D_MODEL)], name="b_ln1")

    g['o'] = _mm2(mixed, dpre1b, ta=True, name="g_o")
    dmixed = _mm2(dpre1b, wt['o'], tb=True, name="b_dmixed")

    def merge_bwd(h, ggv, yd, ym, d):
        _, vjp = jax.vjp(_merge, ggv, yd, ym)
        return vjp(d)

    dgg, dy_dn, dy_mla = _rowwise(
        merge_bwd, [(gg, 2 * D_MODEL, 0, False), (y_dn, D_MODEL, 0, False), (y_mla, D_MODEL, 0, False),
                    (dmixed, D_MODEL, 0, False)], [],
        [(2 * D_MODEL, 2 * D_MODEL, False, bf), (D_MODEL, D_MODEL, False, bf), (D_MODEL, D_MODEL, False, bf)],
        name="b_merge")
    g['br_dn'] = _mm2(og, dy_dn, ta=True, name="g_br_dn")
    dog = _mm2(dy_dn, wt['br_dn'], tb=True, name="b_dog")
    g['br_mla'] = _mm2(o_mla, dy_mla, ta=True, name="g_br_mla")
    do_mla = _mm2(dy_mla, wt['br_mla'], tb=True, name="b_do_mla", out_dtype=bf)

    dout_lat = _mm(do_mla, wt['uv'], name="b_dout_lat", tb=True, heads=HEADS, a_head='col', b_head='lead',
                   out_head='lead', dims=(t, KV_LORA, NOPE))
    g['uv'] = _mm(out_lat, do_mla, name="g_uv", ta=True, heads=HEADS, a_head='lead', b_head='col', out_head='lead',
                  dims=(KV_LORA, NOPE, t))
    dq_lat, dq_rope, dckv_att, dkr_att = _attention_bwd(q_lat, q_rope, c_kv, k_rope, out_lat, lse, dout_lat)
    dq_nope = _mm(dq_lat, wt['uk'], name="b_dq_nope", heads=HEADS, a_head='lead', b_head='lead', out_head='col',
                  dims=(t, NOPE, KV_LORA), out_dtype=bf)
    g['uk'] = _mm(dq_lat, q_nope, name="g_uk", ta=True, heads=HEADS, a_head='lead', b_head='col', out_head='lead',
                  dims=(KV_LORA, NOPE, t))
    (dq_rope_pre,) = _rowwise(lambda h, d, c, s: (_rope_bwd(d, c, s),),
                              [(dq_rope, HEADS * ROPE_PAD, 0, False), (cosb, ROPE_PAD, 0, False), (sinb, ROPE_PAD, 0, False)],
                              [], [(HEADS * ROPE_PAD, HEADS * ROPE_PAD, False, bf)], name="b_q_rope")
    g['uq_nope'] = _mm2(c_q, dq_nope, ta=True, name="g_uq_nope")
    g['uq_rope'] = _mm2(c_q, dq_rope_pre, ta=True, name="g_uq_rope")
    dc_q = _mm2(dq_nope, wt['uq_nope'], tb=True, name="b_dcq_nope")
    dc_q = _mm2(dq_rope_pre, wt['uq_rope'], tb=True, name="b_dcq_rope", add=dc_q)

    def gated_norm_bwd(h, o, zz, d, w):
        _, vjp = jax.vjp(_gated_norm_heads, o, zz, w)
        return vjp(d)

    do_dn, dz, g['dn_norm_w'] = _rowwise(
        gated_norm_bwd, [(o_dn, D_MODEL, 0, False), (z, D_MODEL, 0, False), (dog, D_MODEL, 0, False)], [sp['dn_norm_w']],
        [(D_MODEL, D_MODEL, False, F32), (D_MODEL, D_MODEL, False, bf)], [(1, LANE)], name="b_gated_norm")
    (du, dw, dqd, dkt, dintra, dgl), paired = _delta_scan_bwd(u, w_, qd, kt, intra, gl, sall, do_dn, rider=exch.pair_send(g))
    (dq_a, dk_a, dv_a, dba, g['a_log'], g['dt_bias']), arrived = _delta_local_bwd(
        qkv_act, pm, sp['a_log'], sp['dt_bias'], t_inv, du, dw, dqd, dkt, dintra, dgl, rider=exch.reduce_send(paired))
    exch.reduce_arrived(arrived)
    dqkv_pre, g['conv_w'] = _conv_silu_bwd(qkv_pre, sp['conv_w'], [dq_a, dk_a, dv_a])

    def mla_pre_bwd(h, ckv, krp, cq, cosv, sinv, dcq, dckv, dkr, qw, kw):
        _, vjp = jax.vjp(lambda a, b, c, d, e: (_rms_norm(c, d), _rms_norm(a, e)), ckv, krp, cq, qw, kw)
        dckv_p, _, dcq_p, dqw, dkw = vjp((dcq, dckv))
        dkr_p = _rope_bwd(dkr, cosv, sinv)
        dpm = jnp.concatenate([dckv_p, dkr_p, jnp.zeros((ckv.shape[0], 3 * LANE), F32), dcq_p], axis=1)
        return dpm, dqw, dkw

    dpm_main, g['q_norm_w'], g['kv_norm_w'] = _rowwise(
        mla_pre_bwd,
        [(pm, KV_LORA, 0, False), (pm, ROPE_PAD, 2, False), (pm, Q_LORA, 2, False),
         (cosb, ROPE_PAD, 0, False), (sinb, ROPE_PAD, 0, False),
         (dc_q, Q_LORA, 0, False), (dckv_att, KV_LORA, 0, False), (dkr_att, ROPE_PAD, 0, False)],
        [sp['q_norm_w'], sp['kv_norm_w']], [(1152, 1152, False, F32)], [(1, Q_LORA), (1, KV_LORA)], name="b_mla_pre")
    (dpm,) = _rowwise(
        lambda h, a, b: (jnp.concatenate([a[:, :3 * LANE], b, a[:, 4 * LANE:]], axis=1),),
        [(dpm_main, 1152, 0, False), (dba, LANE, 0, False)], [], [(1152, 1152, False, bf)], name="b_dpm")

    g['qkv'] = _mm2(xb, dqkv_pre, ta=True, name="g_qkv")
    g['z'] = _mm2(xb, dz, ta=True, name="g_z")
    g['gg'] = _mm2(xb, dgg, ta=True, name="g_gg")
    g['mla'] = _mm2(xb, dpm, ta=True, name="g_mla")
    dx, arrived = _dx_fused([(dqkv_pre, wt['qkv']), (dz, wt['z']), (dgg, wt['gg']), (dpm, wt['mla'])], dpre1, ALPHA,
                            rider=exch.in_send(g))
    exch.in_arrived(arrived)
    return loss_lanes, dx, g


_IN_SIZES = (QKV_W, HEADS * DN_DK, HEADS, HEADS, Q_LORA, KV_LORA, ROPE, D_MODEL, D_MODEL)


def _rope_tables(positions):
    inv_freq = ROPE_BASE ** (-jnp.arange(0, ROPE, 2, dtype=F32) / ROPE)
    ang = positions.astype(F32)[:, None] * inv_freq
    cos, sin = jnp.cos(ang), jnp.sin(ang)
    zeros = jnp.zeros((positions.shape[0], ROPE_PAD - ROPE), F32)
    return jnp.concatenate([cos, cos, zeros], axis=1), jnp.concatenate([-sin, sin, zeros], axis=1)


def _prep_w_in(w_in):
    dt = w_in.dtype
    offs = [0]
    for s in _IN_SIZES:
        offs.append(offs[-1] + s)
    qkv, z, wb, wa, cq, ckv, kr, gd, gm = [w_in[:, offs[i]:offs[i + 1]] for i in range(len(_IN_SIZES))]
    zc = lambda n: jnp.zeros((D_MODEL, n), dt)
    return {
        'qkv': qkv, 'z': z, 'gg': jnp.concatenate([gd, gm], axis=1),
        'mla': jnp.concatenate([ckv, kr, zc(ROPE_PAD - ROPE), wb, wa, zc(LANE - 2 * HEADS), zc(2 * LANE), cq], axis=1),
    }


def _prep_weights(full):
    w_uq = full['w_uq']
    wt = {
        'uq_nope': w_uq[:, :, :NOPE].reshape(Q_LORA, HEADS * NOPE),
        'uq_rope': jnp.pad(w_uq[:, :, NOPE:], ((0, 0), (0, 0), (0, ROPE_PAD - ROPE))).reshape(Q_LORA, HEADS * ROPE_PAD),
        'uk': jnp.transpose(full['w_uk'], (1, 0, 2)), 'uv': jnp.transpose(full['w_uv'], (1, 0, 2)),
        'br_dn': full['w_br_dn'], 'br_mla': full['w_br_mla'], 'o': full['w_o'], 'ffn_in': full['w_ffn_in'],
        'ffn_out': full['w_ffn_out'], 'ple': full['w_ple'], 'ple_gate': full['w_ple_gate'],
    }
    return wt


def _prep_small(small):
    pad = lambda v: jnp.pad(v, (0, LANE - v.shape[0]))[None, :]
    return {
        'conv_w': small['conv_w'], 'a_log': pad(small['dn_a_log']), 'dt_bias': pad(small['dn_dt_bias']),
        'dn_norm_w': small['dn_norm_w'][None, :], 'q_norm_w': small['q_norm_w'][None, :],
        'kv_norm_w': small['kv_norm_w'][None, :], 'ln1_g': small['ln1_g'][None, :], 'ln1_b': small['ln1_b'][None, :],
        'ln2_g': small['ln2_g'][None, :], 'ln2_b': small['ln2_b'][None, :],
    }


def _w_in_grad(g):
    mla = g['mla']
    ba0 = KV_LORA + ROPE_PAD
    cq0 = ba0 + 3 * LANE
    return jnp.concatenate([
        g['qkv'], g['z'], mla[:, ba0:ba0 + HEADS], mla[:, ba0 + HEADS:ba0 + 2 * HEADS], mla[:, cq0:cq0 + Q_LORA],
        mla[:, :KV_LORA], mla[:, KV_LORA:KV_LORA + ROPE], g['gg']], axis=1)


def _unprep_grads_late(g):
    return {
        'conv_w': g['conv_w'], 'dn_a_log': g['a_log'][0, :HEADS], 'dn_dt_bias': g['dt_bias'][0, :HEADS],
        'dn_norm_w': g['dn_norm_w'][0], 'q_norm_w': g['q_norm_w'][0], 'kv_norm_w': g['kv_norm_w'][0],
        'ln1_g': g['ln1_g'][0], 'ln1_b': g['ln1_b'][0], 'ln2_g': g['ln2_g'][0], 'ln2_b': g['ln2_b'][0],
    }


def _unprep_grads_early(g):
    w_uq = jnp.concatenate([g['uq_nope'].reshape(Q_LORA, HEADS, NOPE),
                            g['uq_rope'].reshape(Q_LORA, HEADS, ROPE_PAD)[:, :, :ROPE]], axis=2)
    return {
        'w_uq': w_uq, 'w_uk': jnp.transpose(g['uk'], (1, 0, 2)), 'w_uv': jnp.transpose(g['uv'], (1, 0, 2)),
        'w_br_dn': g['br_dn'], 'w_br_mla': g['br_mla'], 'w_o': g['o'],
        'w_ffn_in': g['ffn_in'], 'w_ffn_out': g['ffn_out'], 'w_ple': g['ple'], 'w_ple_gate': g['ple_gate'],
    }


_FLATB_PIECES = (
    ('w_ffn_out', 704, (704, D_MODEL)), ('w_br_dn', 256, (256, D_MODEL)), ('w_br_mla', 256, (256, D_MODEL)),
    ('w_o', 256, (256, D_MODEL)), ('w_ple_gate', 256, (256, D_MODEL)), ('w_uq', 144, (96, HEADS, NOPE + ROPE)),
    ('w_uk', 64, (64, HEADS, NOPE)), ('w_uv', 64, (64, HEADS, NOPE)), ('w_ple', 64, (PLE_DIM, 256)),
)
FLATB_ROWS = 2112
W_IN_SHARD = D_IN // N_SHARD
FFN_IN_SHARD = 2 * FFN_HIDDEN // N_SHARD
A_ROWS = D_MODEL + 32
_CONV_SHARD = QKV_W // N_SHARD
_ADD_TILES = (256, 256, 352)


def _flatb_offsets():
    offs, o = {}, 0
    for name, rows, _ in _FLATB_PIECES:
        offs[name] = o
        o += rows
    return offs, o


def _pack_shards(ws, conv_w):
    conv_bits = lax.bitcast_convert_type(conv_w, jnp.bfloat16).reshape(DN_CONV, 2 * _CONV_SHARD).astype(_CDT)
    tail = jnp.pad(conv_bits, ((0, A_ROWS - D_MODEL - DN_CONV), (0, W_IN_SHARD - 2 * _CONV_SHARD)))
    a_buf = jnp.concatenate([ws['w_in'].astype(_CDT), tail], axis=0)
    parts = [ws[name].astype(_CDT).reshape(rows, FLAT_W) for name, rows, _ in _FLATB_PIECES]
    used = sum(p.shape[0] for p in parts)
    parts.append(jnp.zeros((FLATB_ROWS - used, FLAT_W), _CDT))
    return [a_buf, ws['w_ffn_in'].astype(_CDT), jnp.concatenate(parts, axis=0)]


def _unpack_w_in(gathered, local, me):
    a = [jnp.where(me == s, local, gathered[s]) for s in range(N_SHARD)]
    conv = [lax.bitcast_convert_type(
        p[D_MODEL:D_MODEL + DN_CONV, :2 * _CONV_SHARD].astype(jnp.bfloat16).reshape(DN_CONV, _CONV_SHARD, 2), F32) for p in a]
    return jnp.concatenate([p[:D_MODEL] for p in a], axis=1), jnp.concatenate(conv, axis=1)


def _unpack_rest(gathered, local, me):
    pick = lambda b, s: jnp.where(me == s, local[b], gathered[b][s])
    full = {'w_ffn_in': jnp.concatenate([pick(0, s) for s in range(N_SHARD)], axis=1)}
    offs, _ = _flatb_offsets()
    fb = [pick(1, s) for s in range(N_SHARD)]
    for name, rows, shape in _FLATB_PIECES:
        pieces = [p[offs[name]:offs[name] + rows].reshape(shape) for p in fb]
        full[name] = jnp.concatenate(pieces, axis=1 if name == 'w_ple' else 0)
    return full


def _shard_columns(g, w):
    return jnp.stack([g[:, s * w:(s + 1) * w] for s in range(N_SHARD)])


def _pack_grads_rest(gw):
    parts = []
    for name, rows, _ in _FLATB_PIECES:
        g = gw[name]
        if name == 'w_ple':
            parts.append(_shard_columns(g, PLE_DIM).reshape(N_SHARD, rows, FLAT_W))
        else:
            parts.append(g.reshape(N_SHARD, rows, FLAT_W))
    used = sum(p.shape[1] for p in parts)
    parts.append(jnp.zeros((N_SHARD, FLATB_ROWS - used, FLAT_W), F32))
    return [_shard_columns(gw['w_ffn_in'], FFN_IN_SHARD), jnp.concatenate(parts, axis=1)]


def _unpack_reduced(mine, theirs, c):
    whole = [jnp.concatenate([jnp.where(c == 0, m, t), jnp.where(c == 0, t, m)], axis=0) for m, t in zip(mine, theirs)]
    out = {'w_in': whole[0], 'w_ffn_in': whole[1]}
    offs, _ = _flatb_offsets()
    for name, rows, shape in _FLATB_PIECES:
        out[name] = whole[2][offs[name]:offs[name] + rows].reshape(shape)
    return out


_HBM = pl.BlockSpec(memory_space=pltpu.HBM)


def _place():
    x, y, c = lax.axis_index("x"), lax.axis_index("y"), lax.axis_index("c")
    chips = [(1 - x, y), (x, 1 - y), (1 - x, 1 - y)]
    return x, y, c, chips


def _remote(src, dst, send_sems, recv_sems, k, to):
    return pltpu.make_async_remote_copy(src_ref=src, dst_ref=dst, send_sem=send_sems.at[k], recv_sem=recv_sems.at[k],
                                        device_id=to, device_id_type=_MESH)


def _half_rows(ref, half, hf, lead=None):
    rows = pl.ds(pl.multiple_of(hf * half, 16), half)
    return ref.at[rows, :] if lead is None else ref.at[lead, rows, :]


class _Rider:
    def __init__(self, inputs, out_shape, n_sems, copies, aliases=None):
        self.inputs, self.out_shape, self.n_sems, self.copies = list(inputs), list(out_shape), n_sems, copies
        self.aliases = aliases or {}


def _carried_call(body, rider, first, last, *, name, grid, in_specs, out_specs, out_shape, scratch_shapes, sem, args):
    n_in, n_out, n_scr = len(in_specs), len(out_specs), len(scratch_shapes)
    if rider is None:
        res = pl.pallas_call(body, name=name, grid=grid, in_specs=in_specs, out_specs=out_specs, out_shape=out_shape,
                             scratch_shapes=scratch_shapes, compiler_params=_cparams(sem))(*args)
        return list(res), []
    ri, ro = len(rider.inputs), len(rider.out_shape)

    def full_body(*refs):
        own_in, r_in = refs[:n_in], refs[n_in:n_in + ri]
        o0 = n_in + ri
        own_out, r_out = refs[o0:o0 + n_out], refs[o0 + n_out:o0 + n_out + ro]
        s0 = o0 + n_out + ro
        own_scr, send_sems, recv_sems = refs[s0:s0 + n_scr], refs[s0 + n_scr], refs[s0 + n_scr + 1]

        @pl.when(first())
        def _():
            sends, _ = rider.copies(r_in, r_out, send_sems, recv_sems)
            for cp in sends:
                cp.start()

        body(*own_in, *own_out, *own_scr)

        @pl.when(last())
        def _():
            sends, arrivals = rider.copies(r_in, r_out, send_sems, recv_sems)
            for cp in arrivals():
                cp.wait_recv()
            for cp in sends:
                cp.wait_send()

    res = pl.pallas_call(
        full_body, name=name, grid=grid, in_specs=list(in_specs) + [_HBM] * ri, out_specs=list(out_specs) + [_HBM] * ro,
        out_shape=list(out_shape) + rider.out_shape,
        scratch_shapes=list(scratch_shapes) + [pltpu.SemaphoreType.DMA((rider.n_sems,))] * 2,
        input_output_aliases={n_in + i: n_out + o for i, o in rider.aliases.items()},
        compiler_params=_cparams(sem))(*args, *rider.inputs)
    return list(res[:n_out]), list(res[n_out:])


def _ride_gather_send(bufs):
    n = len(bufs)
    halves = [b.shape[0] // 2 for b in bufs]

    def copies(ins, outs, send_sems, recv_sems):
        x, y, c, chips = _place()
        slot = lambda b, cx, cy: _half_rows(outs[b], halves[b], c, lead=2 * cx + cy)
        sends = [_remote(_half_rows(ins[b], halves[b], c), slot(b, x, y), send_sems, recv_sems, 3 * b + j, (cx, cy, c))
                 for b in range(n) for j, (cx, cy) in enumerate(chips)]
        arrivals = lambda: [_remote(slot(b, cx, cy), slot(b, cx, cy), send_sems, recv_sems, 3 * b + j, (x, y, c))
                            for b in range(n) for j, (cx, cy) in enumerate(chips)]
        return sends, arrivals

    return _Rider(bufs, [jax.ShapeDtypeStruct((N_SHARD,) + b.shape, b.dtype) for b in bufs], 3 * n, copies)


def _ride_gather_pass(gathered):
    n = len(gathered)
    halves = [g.shape[1] // 2 for g in gathered]

    def copies(ins, outs, send_sems, recv_sems):
        x, y, c, chips = _place()
        slot = lambda b, cx, cy, hf: _half_rows(outs[b], halves[b], hf, lead=2 * cx + cy)
        sends = [_remote(slot(b, cx, cy, c), slot(b, cx, cy, c), send_sems, recv_sems, 3 * b + j, (x, y, 1 - c))
                 for b in range(n) for j, (cx, cy) in enumerate(chips)]
        arrivals = lambda: [_remote(slot(b, cx, cy, 1 - c), slot(b, cx, cy, 1 - c), send_sems, recv_sems, 3 * b + j, (x, y, c))
                            for b in range(n) for j, (cx, cy) in enumerate(chips)]
        return sends, arrivals

    return _Rider(gathered, [jax.ShapeDtypeStruct(g.shape, g.dtype) for g in gathered], 3 * n, copies,
                  aliases={b: b for b in range(n)})


def _ride_pair_exchange(gbufs):
    n = len(gbufs)
    halves = [g.shape[1] // 2 for g in gbufs]

    def copies(ins, outs, send_sems, recv_sems):
        x, y, c, _ = _place()
        sends = [_remote(ins[b].at[:, pl.ds(pl.multiple_of((1 - c) * halves[b], 16), halves[b]), :], outs[b],
                         send_sems, recv_sems, b, (x, y, 1 - c)) for b in range(n)]
        arrivals = lambda: [_remote(outs[b], outs[b], send_sems, recv_sems, b, (x, y, c)) for b in range(n)]
        return sends, arrivals

    return _Rider(gbufs, [jax.ShapeDtypeStruct((N_SHARD, h, g.shape[2]), g.dtype) for g, h in zip(gbufs, halves)], n, copies)


def _ride_chip_exchange(parts):
    n = len(parts)

    def copies(ins, outs, send_sems, recv_sems):
        x, y, c, chips = _place()
        sends = [_remote(ins[b].at[2 * cx + cy], outs[b].at[j], send_sems, recv_sems, 3 * b + j, (cx, cy, c))
                 for b in range(n) for j, (cx, cy) in enumerate(chips)]
        arrivals = lambda: [_remote(ins[b].at[0], outs[b].at[j], send_sems, recv_sems, 3 * b + j, (x, y, c))
                            for b in range(n) for j in range(len(chips))]
        return sends, arrivals

    return _Rider(parts, [jax.ShapeDtypeStruct((3,) + p.shape[1:], p.dtype) for p in parts], 3 * n, copies)


def _gather_shards(bufs, name):
    n = len(bufs)
    halves = [b.shape[0] // 2 for b in bufs]

    def body(*refs):
        ins, outs, send_sems, recv_sems = refs[:n], refs[n:2 * n], refs[2 * n], refs[2 * n + 1]
        x, y, c, chips = _place()
        me, sibling = (x, y, c), (x, y, 1 - c)
        slot = lambda b, cx, cy, hf: _half_rows(outs[b], halves[b], hf, lead=2 * cx + cy)
        first = [_remote(_half_rows(ins[b], halves[b], c), slot(b, x, y, c), send_sems, recv_sems, 6 * b + j, (cx, cy, c))
                 for b in range(n) for j, (cx, cy) in enumerate(chips)]
        for cp in first:
            cp.start()
        passed = []
        for j, (cx, cy) in enumerate(chips):
            for b in range(n):
                _remote(slot(b, cx, cy, c), slot(b, cx, cy, c), send_sems, recv_sems, 6 * b + j, me).wait_recv()
                fwd = _remote(slot(b, cx, cy, c), slot(b, cx, cy, c), send_sems, recv_sems, 6 * b + 3 + j, sibling)
                fwd.start()
                passed.append(fwd)
        for j, (cx, cy) in enumerate(chips):
            for b in range(n):
                _remote(slot(b, cx, cy, 1 - c), slot(b, cx, cy, 1 - c), send_sems, recv_sems, 6 * b + 3 + j, me).wait_recv()
        for cp in first + passed:
            cp.wait_send()

    return pl.pallas_call(
        body, name=name, out_shape=[jax.ShapeDtypeStruct((N_SHARD,) + b.shape, b.dtype) for b in bufs],
        in_specs=[_HBM] * n, out_specs=[_HBM] * n,
        scratch_shapes=[pltpu.SemaphoreType.DMA((6 * n,)), pltpu.SemaphoreType.DMA((6 * n,))],
    )(*bufs)


def _reduce_pair_exchange(gbufs, name):
    n = len(gbufs)
    halves = [g.shape[1] // 2 for g in gbufs]

    def body(*refs):
        ins, outs, send_sems, recv_sems = refs[:n], refs[n:2 * n], refs[2 * n], refs[2 * n + 1]
        x, y, c, _ = _place()
        cps = [_remote(ins[b].at[:, pl.ds(pl.multiple_of((1 - c) * halves[b], 16), halves[b]), :], outs[b],
                       send_sems, recv_sems, b, (x, y, 1 - c)) for b in range(n)]
        for cp in cps:
            cp.start()
        for cp in cps:
            cp.wait()

    return pl.pallas_call(
        body, name=name,
        out_shape=[jax.ShapeDtypeStruct((N_SHARD, h, g.shape[2]), g.dtype) for g, h in zip(gbufs, halves)],
        in_specs=[_HBM] * n, out_specs=[_HBM] * n,
        scratch_shapes=[pltpu.SemaphoreType.DMA((n,)), pltpu.SemaphoreType.DMA((n,))],
    )(*gbufs)


def _pair_add(gbuf, recv, c_arr, tr, name):
    _, rows, width = gbuf.shape
    half = rows // 2
    nt = half // tr

    def body(c_ref, a_ref, b_ref, o_ref):
        o_ref[...] = (a_ref[...] + b_ref[...]).astype(o_ref.dtype)

    blk = lambda f: pl.BlockSpec((None, tr, width), f)
    return pl.pallas_call(
        body, name=name, out_shape=jax.ShapeDtypeStruct((N_SHARD, half, width), jnp.bfloat16),
        grid_spec=pltpu.PrefetchScalarGridSpec(
            num_scalar_prefetch=1, grid=(N_SHARD, nt),
            in_specs=[blk(lambda s, i, c: (s, c[0] * nt + i, 0)), blk(lambda s, i, c: (s, i, 0))],
            out_specs=blk(lambda s, i, c: (s, i, 0))),
        compiler_params=_cparams(("parallel", "parallel")))(c_arr, gbuf, recv)


def _reduce_chip_exchange(parts, name):
    n = len(parts)

    def body(*refs):
        ins, outs, send_sems, recv_sems = refs[:n], refs[n:2 * n], refs[2 * n], refs[2 * n + 1]
        x, y, c, chips = _place()
        sends = [_remote(ins[b].at[2 * cx + cy], outs[b].at[j], send_sems, recv_sems, 3 * b + j, (cx, cy, c))
                 for b in range(n) for j, (cx, cy) in enumerate(chips)]
        for cp in sends:
            cp.start()
        for b in range(n):
            for j in range(len(chips)):
                _remote(ins[b].at[0], outs[b].at[j], send_sems, recv_sems, 3 * b + j, (x, y, c)).wait_recv()
        for cp in sends:
            cp.wait_send()

    return pl.pallas_call(
        body, name=name, out_shape=[jax.ShapeDtypeStruct((3,) + p.shape[1:], p.dtype) for p in parts],
        in_specs=[_HBM] * n, out_specs=[_HBM] * n,
        scratch_shapes=[pltpu.SemaphoreType.DMA((3 * n,)), pltpu.SemaphoreType.DMA((3 * n,))],
    )(*parts)


def _chip_add(part, recv, me_arr, tr, name):
    _, half, width = part.shape

    def body(me_ref, own, a0, a1, a2, o_ref):
        f = lambda r: r[...].astype(F32)
        o_ref[...] = ((f(own) + f(a0)) + f(a1)) + f(a2)

    specs = [pl.BlockSpec((None, tr, width), lambda i, me: (me[0], i, 0))]
    specs += [pl.BlockSpec((None, tr, width), functools.partial(lambda i, me, k: (k, i, 0), k=k)) for k in range(3)]
    return pl.pallas_call(
        body, name=name, out_shape=jax.ShapeDtypeStruct((half, width), F32),
        grid_spec=pltpu.PrefetchScalarGridSpec(
            num_scalar_prefetch=1, grid=(half // tr,), in_specs=specs,
            out_specs=pl.BlockSpec((tr, width), lambda i, me: (i, 0))),
        compiler_params=_cparams(("parallel",)))(me_arr, part, recv, recv, recv)


def _reduce_pair_share(rhalves, name):
    n = len(rhalves)

    def body(*refs):
        ins, outs, send_sems, recv_sems = refs[:n], refs[n:2 * n], refs[2 * n], refs[2 * n + 1]
        x, y, c, _ = _place()
        cps = [_remote(ins[b], outs[b], send_sems, recv_sems, b, (x, y, 1 - c)) for b in range(n)]
        for cp in cps:
            cp.start()
        for cp in cps:
            cp.wait()

    return pl.pallas_call(
        body, name=name, out_shape=[jax.ShapeDtypeStruct(r.shape, r.dtype) for r in rhalves],
        in_specs=[_HBM] * n, out_specs=[_HBM] * n,
        scratch_shapes=[pltpu.SemaphoreType.DMA((n,)), pltpu.SemaphoreType.DMA((n,))],
    )(*rhalves)


def _small_allreduce(buf):
    r, width = buf.shape
    n_dev = 8

    def body(x_ref, all_ref, sum_ref, send_sems, recv_sems, local_sem):
        x, y, c, chips = _place()
        me, sibling = (x, y, c), (x, y, 1 - c)

        def rows(px, py, pc):
            return all_ref.at[pl.ds(pl.multiple_of((4 * px + 2 * py + pc) * r, 8), r), :]

        def copy(k, block, to, src=None):
            return _remote(rows(*block) if src is None else src, rows(*block), send_sems, recv_sems, k, to)

        mine = pltpu.make_async_copy(x_ref, rows(*me), local_sem)
        mine.start()
        first = [copy(0, me, sibling, src=x_ref)]
        first += [copy(1 + j, me, (*chip, c), src=x_ref) for j, chip in enumerate(chips)]
        for cp in first:
            cp.start()
        passed = [copy(4 + j, (*chip, c), sibling) for j, chip in enumerate(chips)]
        for j, chip in enumerate(chips):
            copy(1 + j, (*chip, c), me).wait_recv()
            passed[j].start()
        copy(0, sibling, me).wait_recv()
        for j, chip in enumerate(chips):
            copy(4 + j, (*chip, 1 - c), me).wait_recv()
        for cp in first + passed:
            cp.wait_send()
        mine.wait()
        total = all_ref[0:r, :]
        for k in range(1, n_dev):
            total = total + all_ref[k * r:(k + 1) * r, :]
        sum_ref[...] = total

    vm = pl.BlockSpec(memory_space=pltpu.VMEM)
    _, total = pl.pallas_call(
        body, name="small_allreduce",
        out_shape=[jax.ShapeDtypeStruct((n_dev * r, width), buf.dtype), jax.ShapeDtypeStruct((r, width), buf.dtype)],
        in_specs=[vm], out_specs=[vm, vm],
        scratch_shapes=[pltpu.SemaphoreType.DMA((7,)), pltpu.SemaphoreType.DMA((7,)), pltpu.SemaphoreType.DMA],
    )(buf)
    return total


def _row_tile(rows, cap):
    if rows <= cap:
        return rows
    t = (cap // 8) * 8
    while t >= 8:
        if rows % t == 0:
            return t
        t -= 8
    return rows


def _adamw(w, g, m, v, name):
    shape = w.shape
    cols = shape[-1] if len(shape) <= 3 else shape[-2] * shape[-1]
    lead = len(shape) == 3
    w2, g2, m2, v2 = (a if lead else a.reshape(-1, cols) for a in (w, g, m, v))
    rows = shape[1] if lead else w2.shape[0]
    tr, tc = _row_tile(rows, 256), cols
    if tr == rows and rows > 256:
        tc = _tile(cols, 256)

    def body(w_ref, g_ref, m_ref, v_ref, d_ref, mo_ref, vo_ref):
        gv = g_ref[...]
        mn = ADAM_B1 * m_ref[...] + (1.0 - ADAM_B1) * gv
        vn = ADAM_B2 * v_ref[...] + (1.0 - ADAM_B2) * (gv * gv)
        m_hat = mn / (1.0 - ADAM_B1 ** ADAM_STEP)
        v_hat = vn / (1.0 - ADAM_B2 ** ADAM_STEP)
        d_ref[...] = -ADAM_LR * (m_hat / (jnp.sqrt(v_hat) + ADAM_EPS) + ADAM_WD * w_ref[...])
        mo_ref[...] = mn
        vo_ref[...] = vn

    blk = (pl.BlockSpec((None, tr, tc), lambda i, j: (0, i, j)) if lead else pl.BlockSpec((tr, tc), lambda i, j: (i, j)))
    outs = pl.pallas_call(
        body, name=name, grid=(rows // tr, cols // tc), in_specs=[blk] * 4, out_specs=[blk] * 3,
        out_shape=[jax.ShapeDtypeStruct(w2.shape, F32)] * 3,
        compiler_params=_cparams(("parallel", "parallel")))(w2, g2, m2, v2)
    return tuple(o.reshape(shape) for o in outs)


_WEIGHT_NAMES = ('w_in', 'conv_w', 'dn_a_log', 'dn_dt_bias', 'dn_norm_w', 'q_norm_w', 'w_uq', 'kv_norm_w', 'w_uk',
                 'w_uv', 'w_br_dn', 'w_br_mla', 'w_o', 'ln1_g', 'ln1_b', 'w_ffn_in', 'w_ffn_out', 'w_ple',
                 'w_ple_gate', 'ln2_g', 'ln2_b')
_SMALL_NAMES = ('ln1_g', 'ln1_b', 'ln2_g', 'ln2_b', 'q_norm_w', 'kv_norm_w', 'dn_norm_w', 'dn_a_log', 'dn_dt_bias')
_SMALL_GROUP = 8
_CONV_SMALL_ROW = len(_SMALL_NAMES) * _SMALL_GROUP
_CONV_SMALL_ROWS = DN_CONV * QKV_W // FLAT_W


def _pack_small(gw):
    rows = [jnp.pad(gw[n][None, :], ((0, _SMALL_GROUP - 1), (0, FLAT_W - gw[n].shape[0]))) for n in _SMALL_NAMES]
    rows.append(jnp.pad(gw['conv_w'].reshape(_CONV_SMALL_ROWS, FLAT_W), ((0, SMALL_ROWS - _CONV_SMALL_ROW - _CONV_SMALL_ROWS), (0, 0))))
    return jnp.concatenate(rows, axis=0)


class _Exchange:
    def __init__(self, local, me_chip, c_arr):
        self.local, self.me_chip, self.c_arr = local, me_chip, c_arr
        self.parts = self.arrived = None

    def gather_send(self):
        return _ride_gather_send(self.local)

    def gather_pass(self, sent):
        return _ride_gather_pass(sent)

    def weights(self, gathered):
        return _prep_weights(_unpack_rest(gathered, self.local, self.me_chip))

    def pair_send(self, g):
        self.gbufs = _pack_grads_rest(_unprep_grads_early(g))
        return _ride_pair_exchange(self.gbufs)

    def reduce_send(self, got):
        self.parts = [_pair_add(g_, r_, self.c_arr, tr, "pair_add_%d" % (i + 1))
                      for i, (g_, r_, tr) in enumerate(zip(self.gbufs, got, _ADD_TILES[1:]))]
        return _ride_chip_exchange(self.parts)

    def reduce_arrived(self, arrived):
        self.arrived = list(arrived)

    def in_send(self, g):
        g_in = [_shard_columns(_w_in_grad(g), W_IN_SHARD)]
        got = _reduce_pair_exchange(g_in, "reduce_pair_exchange_w_in")
        self.part_in = _pair_add(g_in[0], got[0], self.c_arr, _ADD_TILES[0], "pair_add_0")
        return _ride_chip_exchange([self.part_in])

    def in_arrived(self, arrived):
        self.arrived_in = list(arrived)


def kernel(x, p, positions, w_in, conv_w, dn_a_log, dn_dt_bias, dn_norm_w, q_norm_w, w_uq, kv_norm_w, w_uk, w_uv, w_br_dn, w_br_mla, w_o, ln1_g, ln1_b, w_ffn_in, w_ffn_out, w_ple, w_ple_gate, ln2_g, ln2_b, loss_target, m_w_in, m_conv_w, m_dn_a_log, m_dn_dt_bias, m_dn_norm_w, m_q_norm_w, m_w_uq, m_kv_norm_w, m_w_uk, m_w_uv, m_w_br_dn, m_w_br_mla, m_w_o, m_ln1_g, m_ln1_b, m_w_ffn_in, m_w_ffn_out, m_w_ple, m_w_ple_gate, m_ln2_g, m_ln2_b, v_w_in, v_conv_w, v_dn_a_log, v_dn_dt_bias, v_dn_norm_w, v_q_norm_w, v_w_uq, v_kv_norm_w, v_w_uk, v_w_uv, v_w_br_dn, v_w_br_mla, v_w_o, v_ln1_g, v_ln1_b, v_w_ffn_in, v_w_ffn_out, v_w_ple, v_w_ple_gate, v_ln2_g, v_ln2_b):
    ws = dict(w_in=w_in, conv_w=conv_w, dn_a_log=dn_a_log, dn_dt_bias=dn_dt_bias, dn_norm_w=dn_norm_w, q_norm_w=q_norm_w,
              w_uq=w_uq, kv_norm_w=kv_norm_w, w_uk=w_uk, w_uv=w_uv, w_br_dn=w_br_dn, w_br_mla=w_br_mla, w_o=w_o,
              ln1_g=ln1_g, ln1_b=ln1_b, w_ffn_in=w_ffn_in, w_ffn_out=w_ffn_out, w_ple=w_ple, w_ple_gate=w_ple_gate,
              ln2_g=ln2_g, ln2_b=ln2_b)
    ms = dict(w_in=m_w_in, conv_w=m_conv_w, dn_a_log=m_dn_a_log, dn_dt_bias=m_dn_dt_bias, dn_norm_w=m_dn_norm_w,
              q_norm_w=m_q_norm_w, w_uq=m_w_uq, kv_norm_w=m_kv_norm_w, w_uk=m_w_uk, w_uv=m_w_uv, w_br_dn=m_w_br_dn,
              w_br_mla=m_w_br_mla, w_o=m_w_o, ln1_g=m_ln1_g, ln1_b=m_ln1_b, w_ffn_in=m_w_ffn_in, w_ffn_out=m_w_ffn_out,
              w_ple=m_w_ple, w_ple_gate=m_w_ple_gate, ln2_g=m_ln2_g, ln2_b=m_ln2_b)
    vs = dict(w_in=v_w_in, conv_w=v_conv_w, dn_a_log=v_dn_a_log, dn_dt_bias=v_dn_dt_bias, dn_norm_w=v_dn_norm_w,
              q_norm_w=v_q_norm_w, w_uq=v_w_uq, kv_norm_w=v_kv_norm_w, w_uk=v_w_uk, w_uv=v_w_uv, w_br_dn=v_w_br_dn,
              w_br_mla=v_w_br_mla, w_o=v_w_o, ln1_g=v_ln1_g, ln1_b=v_ln1_b, w_ffn_in=v_w_ffn_in, w_ffn_out=v_w_ffn_out,
              w_ple=v_w_ple, w_ple_gate=v_w_ple_gate, ln2_g=v_ln2_g, ln2_b=v_ln2_b)
    mx, my, mc = lax.axis_index("x"), lax.axis_index("y"), lax.axis_index("c")

    me_chip = 2 * mx + my
    c_arr = jnp.reshape(mc, (1,)).astype(jnp.int32)
    me_arr = jnp.reshape(me_chip, (1,)).astype(jnp.int32)
    sharded = ('w_in', 'w_ffn_in') + tuple(name for name, _, _ in _FLATB_PIECES)

    local = _pack_shards({name: ws[name][0] for name in sharded}, conv_w[0])
    (gathered_in,) = _gather_shards(local[:1], "gather_w_in")
    w_in_full, conv_full = _unpack_w_in(gathered_in, local[0], me_chip)
    small = {n: ws[n][0] for n in _SMALL_NAMES}
    small['conv_w'] = conv_full
    sp = _prep_small(small)
    cosb, sinb = _rope_tables(positions[0])
    exch = _Exchange(local[1:], me_chip, c_arr)

    loss_lanes, dx, g = _local_step(x[0], p[0, 0], cosb, sinb, loss_target[0], _prep_w_in(w_in_full), sp, exch)
    gw = _unprep_grads_late(g)
    loss = lax.psum(jnp.sum(loss_lanes), ("x", "y", "c"))

    parts = [exch.part_in] + exch.parts
    arrived = exch.arrived_in + exch.arrived
    mine = [_chip_add(p_, r_, me_arr, tr, "chip_add_%d" % i) for i, (p_, r_, tr) in enumerate(zip(parts, arrived, _ADD_TILES))]
    reduced = _unpack_reduced(mine, _reduce_pair_share(mine, "reduce_pair_share"), mc)
    tot = _small_allreduce(_pack_small(gw))
    gred = {name: reduced[name][None] for name in sharded}
    for i, n in enumerate(_SMALL_NAMES):
        gred[n] = tot[i * _SMALL_GROUP, :ws[n].shape[1]][None]
    conv_tot = tot[_CONV_SMALL_ROW:_CONV_SMALL_ROW + _CONV_SMALL_ROWS].reshape(DN_CONV, QKV_W)
    gred['conv_w'] = lax.dynamic_slice_in_dim(conv_tot, (2 * mx + my) * _CONV_SHARD, _CONV_SHARD, axis=1)[None]

    deltas, new_m, new_v = {}, {}, {}
    for n in _WEIGHT_NAMES:
        if n == 'w_in':
            tr_ = lambda a: jnp.transpose(a, (0, 2, 1))
            g_t = tr_(gred[n].reshape(ws[n].shape))
            outs = _adamw(tr_(ws[n]), g_t, tr_(ms[n]), tr_(vs[n]), "adamw_" + n)
            gred[n] = tr_(g_t)
            deltas[n], new_m[n], new_v[n] = (tr_(o) for o in outs)
            continue
        gred[n] = gred[n].reshape(ws[n].shape)
        deltas[n], new_m[n], new_v[n] = _adamw(ws[n], gred[n], ms[n], vs[n], "adamw_" + n)
    return (loss, dx[None], *[gred[n] for n in _WEIGHT_NAMES], *[deltas[n] for n in _WEIGHT_NAMES],
            *[new_m[n] for n in _WEIGHT_NAMES], *[new_v[n] for n in _WEIGHT_NAMES])
```

```python
import functools
import math

import jax
import jax.numpy as jnp
from jax import lax
from jax.experimental import pallas as pl
from jax.experimental.pallas import tpu as pltpu

F32 = jnp.float32
_CDT = jnp.bfloat16
_HI = lax.Precision.HIGHEST
_MESH = pl.DeviceIdType.MESH

D_MODEL = 1024
PLE_DIM = 256
HEADS = 8
DN_DK = 128
DN_CHUNK = 64
DN_CONV = 4
QKV_W = 3 * HEADS * DN_DK
Q_LORA = 384
KV_LORA = 256
NOPE = 128
ROPE = 64
ROPE_PAD = 128
FFN_HIDDEN = 2816
D_IN = 6864
ROPE_BASE = 10000.0
ALPHA = 2.0 ** 0.25
ATT_SCALE = (NOPE + ROPE) ** -0.5
NEG_BIG = -1e30
ADAM_LR, ADAM_B1, ADAM_B2, ADAM_EPS, ADAM_WD, ADAM_STEP = 0.001, 0.9, 0.999, 1e-08, 0.01, 10

LANE = 128
VMEM_LIMIT = 56 * 1024 * 1024
MM_VMEM_BUDGET = 40 * 1024 * 1024
N_SHARD = 4
FLAT_W = 1024
SMALL_ROWS = 88


def _tile(dim, cap):
    if dim <= cap:
        return dim
    t = (cap // LANE) * LANE
    while t >= LANE:
        if dim % t == 0:
            return t
        t -= LANE
    return dim


def _cparams(sem):
    return pltpu.CompilerParams(dimension_semantics=sem, vmem_limit_bytes=VMEM_LIMIT)


def _mm(a, b, *, name, ta=False, tb=False, add=None, add_scale=1.0, out_dtype=F32, heads=None,
        a_head=None, b_head=None, out_head=None, dims=None, tm=1408, tn=1408):
    m, n, k = dims
    tm, tn = _tile(m, tm), _tile(n, tn)
    sa, sb, so = a.dtype.itemsize, b.dtype.itemsize, jnp.dtype(out_dtype).itemsize

    def vmem_need(tk_):
        acc = tm * tn * 4 if tk_ < k else 0
        extra = 2 * tm * tn * 4 if add is not None else 0
        return 2 * (tm * tk_ * sa + tk_ * tn * sb) + 2 * tm * tn * so + acc + extra

    tk = k
    while vmem_need(tk) > MM_VMEM_BUDGET and tk > LANE:
        smaller = _tile(k, tk - LANE)
        if smaller >= tk:
            break
        tk = smaller
    nk = k // tk
    hgrid = () if heads is None else (heads,)
    off = len(hgrid)

    def spec(rows, cols, rtile, ctile, rsel, csel, layout):
        def idx(*g):
            h = g[0] if off else 0
            ri, ci = g[off + rsel], g[off + csel]
            if layout == 'lead':
                return (h, ri, ci)
            if layout == 'col':
                return (ri, h * (cols // ctile) + ci)
            return (ri, ci)
        if layout == 'lead':
            return pl.BlockSpec((None, rtile, ctile), idx)
        return pl.BlockSpec((rtile, ctile), idx)

    a_spec = spec(k, m, tk, tm, 2, 0, a_head) if ta else spec(m, k, tm, tk, 0, 2, a_head)
    b_spec = spec(n, k, tn, tk, 1, 2, b_head) if tb else spec(k, n, tk, tn, 2, 1, b_head)
    o_spec = spec(m, n, tm, tn, 0, 1, out_head)
    in_specs = [a_spec, b_spec]
    args = [a, b]
    if add is not None:
        in_specs.append(spec(m, n, tm, tn, 0, 1, out_head))
        args.append(add)
    dn = (((0 if ta else 1,), (1 if tb else 0,)), ((), ()))

    def body(*refs):
        a_ref, b_ref = refs[0], refs[1]
        prod = lax.dot_general(a_ref[...].astype(_CDT), b_ref[...].astype(_CDT), dn, preferred_element_type=F32)
        if nk == 1:
            o_ref = refs[-1]
            if add is not None:
                prod = prod + refs[2][...].astype(F32) * add_scale
            o_ref[...] = prod.astype(out_dtype)
            return
        o_ref, acc_ref = refs[-2], refs[-1]
        kk = pl.program_id(off + 2)

        @pl.when(kk == 0)
        def _():
            if add is not None:
                acc_ref[...] = refs[2][...].astype(F32) * add_scale
            else:
                acc_ref[...] = jnp.zeros_like(acc_ref)

        acc_ref[...] += prod

        @pl.when(kk == nk - 1)
        def _():
            o_ref[...] = acc_ref[...].astype(out_dtype)

    if out_head == 'lead':
        oshape = (heads, m, n)
    elif out_head == 'col':
        oshape = (m, heads * n)
    else:
        oshape = (m, n)
    sem = ("parallel",) * (off + 2) + ("arbitrary",)
    return pl.pallas_call(
        body, name=name, grid=hgrid + (m // tm, n // tn, nk), in_specs=in_specs, out_specs=o_spec,
        out_shape=jax.ShapeDtypeStruct(oshape, out_dtype),
        scratch_shapes=[pltpu.VMEM((tm, tn), F32)] if nk > 1 else [],
        compiler_params=_cparams(sem))(*args)


def _mm2(a, b, **kw):
    ta, tb = kw.get('ta', False), kw.get('tb', False)
    m = a.shape[1] if ta else a.shape[0]
    k = a.shape[0] if ta else a.shape[1]
    n = b.shape[0] if tb else b.shape[1]
    return _mm(a, b, dims=(m, n, k), **kw)


def _rowwise(fn, rows, bcast, outs, reds=(), *, name, tm=256, heads=None):
    t = rows[0][0].shape[0]
    tm = min(tm, t)
    hn = 1 if heads is None else heads
    in_specs, args = [], []
    for arr, width, base, per_head in rows:
        in_specs.append(pl.BlockSpec((tm, width), functools.partial(
            lambda i, h, base, per_head: (i, base + (h if per_head else 0)), base=base, per_head=per_head)))
        args.append(arr)
    for arr in bcast:
        in_specs.append(pl.BlockSpec(arr.shape, lambda i, h: (0, 0)))
        args.append(arr)
    out_specs, out_shape = [], []
    for total, width, per_head, dt in outs:
        out_specs.append(pl.BlockSpec((tm, width), functools.partial(
            lambda i, h, per_head: (i, h if per_head else 0), per_head=per_head)))
        out_shape.append(jax.ShapeDtypeStruct((t, total), dt))
    for shp in reds:
        out_specs.append(pl.BlockSpec(shp, lambda i, h: (0, 0)))
        out_shape.append(jax.ShapeDtypeStruct(shp, F32))
    n_in, n_out, n_red = len(args), len(outs), len(reds)

    def body(*refs):
        i, h = pl.program_id(0), pl.program_id(1)
        vals = fn(h, *[r[...] for r in refs[:n_in]])
        for r, v in zip(refs[n_in:n_in + n_out], vals[:n_out]):
            r[...] = v.astype(r.dtype)
        if n_red:
            @pl.when((i == 0) & (h == 0))
            def _():
                for r in refs[n_in + n_out:]:
                    r[...] = jnp.zeros_like(r)
            for r, v in zip(refs[n_in + n_out:], vals[n_out:]):
                r[...] += v

    sem = ("arbitrary", "arbitrary") if n_red else ("parallel", "parallel")
    res = pl.pallas_call(body, name=name, grid=(t // tm, hn), in_specs=in_specs, out_specs=out_specs,
                         out_shape=out_shape, compiler_params=_cparams(sem))(*args)
    return tuple(res)


def _sigmoid(x):
    return 1.0 / (1.0 + jnp.exp(-x))


def _silu(x):
    return x * _sigmoid(x)


def _softplus(x):
    return jnp.maximum(x, 0.0) + jnp.log(1.0 + jnp.exp(-jnp.abs(x)))


def _layer_norm(t, g, b):
    mu = jnp.mean(t, axis=-1, keepdims=True)
    d = t - mu
    var = jnp.mean(d * d, axis=-1, keepdims=True)
    return d * lax.rsqrt(var + 1e-5) * g + b


def _rms_norm(t, w):
    return t * lax.rsqrt(jnp.mean(t * t, axis=-1, keepdims=True) + 1e-6) * w


def _swap_rope_halves(t):
    lane = lax.broadcasted_iota(jnp.int32, t.shape, 1) % ROPE_PAD
    n = t.shape[1]
    up = pltpu.roll(t, n - ROPE // 2, axis=1)
    dn = pltpu.roll(t, ROPE // 2, axis=1)
    return jnp.where(lane < ROPE // 2, up, jnp.where(lane < ROPE, dn, 0.0))


def _rope(t, cosb, sinb):
    reps = t.shape[1] // ROPE_PAD
    c = jnp.tile(cosb, (1, reps)) if reps > 1 else cosb
    s = jnp.tile(sinb, (1, reps)) if reps > 1 else sinb
    return t * c + _swap_rope_halves(t) * s


def _rope_bwd(d, cosb, sinb):
    reps = d.shape[1] // ROPE_PAD
    c = jnp.tile(cosb, (1, reps)) if reps > 1 else cosb
    s = jnp.tile(sinb, (1, reps)) if reps > 1 else sinb
    return d * c + _swap_rope_halves(d * s)


_CONV_ROWS = 256
_CONV_COLS = 256


def _conv_window(ref, r0, lo, hi, t):
    parts = []
    start, stop = r0 - lo, r0 + _CONV_ROWS + hi
    if start < 0:
        parts.append(jnp.zeros((-start, ref.shape[1]), F32))
        start = 0
    tail = max(stop - t, 0)
    parts.append(ref[start:stop - tail, :].astype(F32))
    if tail:
        parts.append(jnp.zeros((tail, ref.shape[1]), F32))
    return parts[0] if len(parts) == 1 else jnp.concatenate(parts, axis=0)


def _conv_taps(win, w_ref, n_out):
    acc = win[8:8 + n_out] * w_ref[DN_CONV - 1:DN_CONV, :]
    for i in range(DN_CONV - 1):
        acc = acc + pltpu.roll(win, DN_CONV - 1 - i, axis=0)[8:8 + n_out] * w_ref[i:i + 1, :]
    return acc


def _conv_silu(x, w):
    t, ch = x.shape

    def body(x_ref, w_ref, o_ref):
        for r in range(t // _CONV_ROWS):
            r0 = r * _CONV_ROWS
            c = _conv_taps(_conv_window(x_ref, r0, 8, 0, t), w_ref, _CONV_ROWS)
            o_ref[r0:r0 + _CONV_ROWS, :] = _silu(c)

    return pl.pallas_call(
        body, name="conv_silu", grid=(ch // _CONV_COLS,),
        in_specs=[pl.BlockSpec((t, _CONV_COLS), lambda j: (0, j)), pl.BlockSpec((DN_CONV, _CONV_COLS), lambda j: (0, j))],
        out_specs=pl.BlockSpec((t, _CONV_COLS), lambda j: (0, j)),
        out_shape=jax.ShapeDtypeStruct((t, ch), F32), compiler_params=_cparams(("parallel",)))(x, w)


def _conv_silu_bwd(x, w, dys):
    t, ch = x.shape
    per = ch // len(dys) // _CONV_COLS

    def body(x_ref, w_ref, *rest):
        dy_refs, (dx_ref, dw_ref) = rest[:len(dys)], rest[len(dys):]
        sec = pl.program_id(0) // per
        dws = [jnp.zeros((1, _CONV_COLS), F32) for _ in range(DN_CONV)]
        for r in range(t // _CONV_ROWS):
            r0 = r * _CONV_ROWS
            n_ext = _CONV_ROWS + 8
            xw = _conv_window(x_ref, r0, 8, 8, t)
            c = _conv_taps(xw, w_ref, n_ext)
            sg = _sigmoid(c)
            dy = _conv_window(dy_refs[-1], r0, 0, 8, t)
            for k in range(len(dys) - 2, -1, -1):
                dy = jnp.where(sec == k, _conv_window(dy_refs[k], r0, 0, 8, t), dy)
            ds = dy * (sg * (1.0 + c * (1.0 - sg)))
            dx = ds[:_CONV_ROWS] * w_ref[DN_CONV - 1:DN_CONV, :]
            for i in range(DN_CONV - 1):
                sh = DN_CONV - 1 - i
                dx = dx + pltpu.roll(ds, n_ext - sh, axis=0)[:_CONV_ROWS] * w_ref[i:i + 1, :]
            dx_ref[r0:r0 + _CONV_ROWS, :] = dx.astype(dx_ref.dtype)
            ds0 = ds[:_CONV_ROWS]
            for i in range(DN_CONV):
                sh = DN_CONV - 1 - i
                xs = xw if sh == 0 else pltpu.roll(xw, sh, axis=0)
                dws[i] = dws[i] + jnp.sum(ds0 * xs[8:8 + _CONV_ROWS], axis=0, keepdims=True)
        for i in range(DN_CONV):
            dw_ref[i:i + 1, :] = dws[i]

    blk = pl.BlockSpec((t, _CONV_COLS), lambda j: (0, j))
    wblk = pl.BlockSpec((DN_CONV, _CONV_COLS), lambda j: (0, j))
    dy_specs = [pl.BlockSpec((t, _CONV_COLS), functools.partial(lambda j, k: (0, jnp.clip(j - k * per, 0, per - 1)), k=k))
                for k in range(len(dys))]
    return pl.pallas_call(
        body, name="conv_silu_bwd", grid=(ch // _CONV_COLS,), in_specs=[blk, wblk] + dy_specs, out_specs=[blk, wblk],
        out_shape=[jax.ShapeDtypeStruct((t, ch), _CDT), jax.ShapeDtypeStruct((DN_CONV, ch), F32)],
        compiler_params=_cparams(("arbitrary",)))(x, w, *dys)


_PA_ROWS = 512


def _bmm(a, b, spec, exact=False):
    if exact:
        return jnp.einsum(spec, a, b, precision=_HI, preferred_element_type=F32)
    return jnp.einsum(spec, a.astype(_CDT), b.astype(_CDT), preferred_element_type=F32)


def _split16(a):
    hi = a.astype(jnp.bfloat16)
    return hi, (a - hi.astype(F32)).astype(jnp.bfloat16)


def _bmm3(a, b, spec):
    ah, al = _split16(a)
    bh, bl = _split16(b)
    e = lambda p, q: jnp.einsum(spec, p, q, preferred_element_type=F32)
    return e(ah, bh) + (e(ah, bl) + e(al, bh))


def _tri_inverse(l_mat, eye):
    pw = -l_mat
    t_inv = eye + pw
    for _ in range(5):
        pw = _bmm3(pw, pw, 'bij,bjk->bik')
        t_inv = t_inv + _bmm3(t_inv, pw, 'bij,bjk->bik')
    return t_inv


@jax.custom_vjp
def _tri_inverse_saved(l_mat, t_saved):
    return t_saved


def _tri_inverse_saved_fwd(l_mat, t_saved):
    return t_saved, t_saved


def _tri_inverse_saved_bwd(t_saved, dt):
    left = _bmm3(t_saved, dt, 'bji,bjk->bik')
    return -_bmm3(left, t_saved, 'bij,bkj->bik'), jnp.zeros_like(t_saved)


_tri_inverse_saved.defvjp(_tri_inverse_saved_fwd, _tri_inverse_saved_bwd)


def _phase_a(h, q, k, v, ba, alog, dtb, t_saved=None):
    r = q.shape[0]
    nb = r // DN_CHUNK
    c = DN_CHUNK
    lane = lax.broadcasted_iota(jnp.int32, (1, LANE), 1)
    selb = (lane == h).astype(F32)
    sela = (lane == h + HEADS).astype(F32)
    b_raw = jnp.sum(ba * selb, axis=1, keepdims=True)
    a_raw = jnp.sum(ba * sela, axis=1, keepdims=True)
    al = jnp.sum(alog * selb, axis=1, keepdims=True)
    dt = jnp.sum(dtb * selb, axis=1, keepdims=True)
    beta = jnp.broadcast_to(_sigmoid(b_raw), (r, LANE))
    g = jnp.broadcast_to(-jnp.exp(al) * _softplus(a_raw + dt), (r, LANE))
    qn = q * lax.rsqrt(jnp.sum(q * q, -1, keepdims=True) + 1e-6) * (DN_DK ** -0.5)
    kn = k * lax.rsqrt(jnp.sum(k * k, -1, keepdims=True) + 1e-6)
    q3, k3, v3 = qn.reshape(nb, c, LANE), kn.reshape(nb, c, LANE), v.reshape(nb, c, LANE)
    b3, g3 = beta.reshape(nb, c, LANE), g.reshape(nb, c, LANE)
    ri = lax.broadcasted_iota(jnp.int32, (nb, c, c), 1)
    ci = lax.broadcasted_iota(jnp.int32, (nb, c, c), 2)
    tril, strict = ri >= ci, ri > ci
    gc = _bmm(tril.astype(F32), g3, 'bij,bjd->bid', exact=True)
    onehot = (lax.broadcasted_iota(jnp.int32, (nb, c, LANE), 2) == 0).astype(F32)
    g_row = _bmm(onehot, gc, 'bid,bjd->bij', exact=True)
    diff = gc[:, :, :c] - g_row
    decay = jnp.where(tril, jnp.exp(jnp.where(tril, diff, 0.0)), 0.0)
    kb = k3 * b3
    l_mat = jnp.where(strict, _bmm(kb, k3, 'bid,bjd->bij') * decay, 0.0)
    if t_saved is None:
        t_inv = _tri_inverse(l_mat, (ri == ci).astype(F32))
    else:
        t_inv = _tri_inverse_saved(l_mat, t_saved.reshape(nb, c, c))
    eg = jnp.exp(gc)
    u = _bmm(t_inv, v3 * b3, 'bij,bje->bie')
    w = _bmm(t_inv, kb * eg, 'bij,bje->bie')
    intra = jnp.where(tril, _bmm(q3, k3, 'bid,bjd->bij') * decay, 0.0)
    qd = q3 * eg
    gl = jnp.sum(g3, axis=1, keepdims=True)
    kt = k3 * jnp.exp(gl - gc)
    outs = (u.reshape(r, LANE), w.reshape(r, LANE), qd.reshape(r, LANE), kt.reshape(r, LANE),
            intra.reshape(r, c), gl.reshape(nb, LANE))
    return outs + (t_inv.reshape(r, c),) if t_saved is None else outs


def _pa_specs(t):
    rr = min(_PA_ROWS, t)
    nb = rr // DN_CHUNK
    qkv = [pl.BlockSpec((rr, LANE), functools.partial(lambda i, h, o: (i, o + h), o=o)) for o in (0, HEADS, 2 * HEADS)]
    ba = pl.BlockSpec((rr, LANE), lambda i, h: (i, 3))
    vec = pl.BlockSpec((1, LANE), lambda i, h: (0, 0))
    row = pl.BlockSpec((rr, LANE), lambda i, h: (i, h))
    intra = pl.BlockSpec((None, rr, DN_CHUNK), lambda i, h: (h, i, 0))
    gl = pl.BlockSpec((nb, LANE), lambda i, h: (i, h))
    return rr, qkv, ba, vec, row, intra, gl


def _grid_ends(grid):
    first = lambda: functools.reduce(jnp.logical_and, [pl.program_id(a) == 0 for a in range(len(grid))])
    last = lambda: functools.reduce(jnp.logical_and, [pl.program_id(a) == n - 1 for a, n in enumerate(grid)])
    return first, last


def _delta_local(qkv_act, pm, alog, dtb, rider=None):
    t = qkv_act.shape[0]
    rr, qkv, ba, vec, row, intra, gl = _pa_specs(t)

    def body(q, k, v, b, al, dt, *outs):
        vals = _phase_a(pl.program_id(1), q[...], k[...], v[...], b[...], al[...], dt[...])
        for o, val in zip(outs, vals):
            o[...] = val

    wide = jax.ShapeDtypeStruct((t, HEADS * LANE), F32)
    sq = jax.ShapeDtypeStruct((HEADS, t, DN_CHUNK), F32)
    grid = (t // rr, HEADS)
    return _carried_call(
        body, rider, *_grid_ends(grid), name="delta_local", grid=grid, in_specs=qkv + [ba, vec, vec],
        out_specs=[row] * 4 + [intra, gl, intra],
        out_shape=[wide] * 4 + [sq, jax.ShapeDtypeStruct((t // DN_CHUNK, HEADS * LANE), F32), sq],
        scratch_shapes=[], sem=("arbitrary", "arbitrary"), args=(qkv_act, qkv_act, qkv_act, pm, alog, dtb))


def _delta_local_bwd(qkv_act, pm, alog, dtb, t_inv, du, dw, dqd, dkt, dintra, dgl, rider=None):
    t = qkv_act.shape[0]
    rr, qkv, ba, vec, row, intra, gl = _pa_specs(t)

    def body(q, k, v, b, al, dt, ti, du_r, dw_r, dqd_r, dkt_r, di_r, dgl_r, dq_o, dk_o, dv_o, dba_o, dal_o, ddt_o):
        i, h = pl.program_id(0), pl.program_id(1)
        t_saved = ti[...]
        _, vjp = jax.vjp(lambda *a: _phase_a(h, *a, t_saved=t_saved), q[...], k[...], v[...], b[...], al[...], dt[...])
        dq, dk, dv, dba, dal, ddt = vjp((du_r[...], dw_r[...], dqd_r[...], dkt_r[...], di_r[...], dgl_r[...]))
        dq_o[...], dk_o[...], dv_o[...] = dq, dk, dv

        @pl.when(h == 0)
        def _():
            dba_o[...] = jnp.zeros_like(dba_o)

        @pl.when((h == 0) & (i == 0))
        def _():
            dal_o[...] = jnp.zeros_like(dal_o)
            ddt_o[...] = jnp.zeros_like(ddt_o)

        dba_o[...] += dba
        dal_o[...] += dal
        ddt_o[...] += ddt

    wide = jax.ShapeDtypeStruct((t, HEADS * LANE), F32)
    vshape = jax.ShapeDtypeStruct((1, LANE), F32)
    grid = (t // rr, HEADS)
    return _carried_call(
        body, rider, *_grid_ends(grid), name="delta_local_bwd", grid=grid,
        in_specs=qkv + [ba, vec, vec, intra] + [row] * 4 + [intra, gl],
        out_specs=[row] * 3 + [pl.BlockSpec((rr, LANE), lambda i, h: (i, 0)), vec, vec],
        out_shape=[wide] * 3 + [jax.ShapeDtypeStruct((t, LANE), F32), vshape, vshape],
        scratch_shapes=[], sem=("arbitrary", "arbitrary"),
        args=(qkv_act, qkv_act, qkv_act, pm, alog, dtb, t_inv, du, dw, dqd, dkt, dintra, dgl))


_SCAN_ROWS = 512


def _dot(a, b, dn):
    return lax.dot_general(a.astype(_CDT), b.astype(_CDT), (dn, ((), ())), preferred_element_type=F32)


_NN = ((1,), (0,))
_NT = ((1,), (1,))
_TN = ((0,), (0,))


def _delta_scan(u, w, qd, kt, intra, gl, rider=None):
    t = u.shape[0]
    rr = min(_SCAN_ROWS, t)
    nc = rr // DN_CHUNK

    def body(u_ref, w_ref, qd_ref, kt_ref, a_ref, gl_ref, o_ref, sall_ref, s_scr):
        @pl.when(pl.program_id(0) == 0)
        def _():
            s_scr[...] = jnp.zeros_like(s_scr)

        def chunk(c, carry):
            r0 = pl.multiple_of(c * DN_CHUNK, DN_CHUNK)
            rows = pl.ds(r0, DN_CHUNK)
            e = jnp.exp(gl_ref[pl.ds(c, 1), :])
            states = [s_scr[h] for h in range(HEADS)]
            u_c, w_c, qd_c, kt_c = u_ref[rows, :], w_ref[rows, :], qd_ref[rows, :], kt_ref[rows, :]
            a_c = [a_ref[h, rows, :] for h in range(HEADS)]
            o_new, s_new = [], []
            for h in range(HEADS):
                cs = slice(h * LANE, (h + 1) * LANE)
                s = states[h]
                v_new = u_c[:, cs] - _dot(w_c[:, cs], s, _NN)
                o_new.append(_dot(qd_c[:, cs], s, _NN) + _dot(a_c[h], v_new, _NN))
                s_new.append(s * e[:, cs] + _dot(kt_c[:, cs], v_new, _TN))
            o_ref[rows, :] = jnp.concatenate(o_new, axis=1)
            for h in range(HEADS):
                sall_ref[c, h] = states[h]
                s_scr[h] = s_new[h]
            return carry

        lax.fori_loop(0, nc, chunk, 0)

    row = pl.BlockSpec((rr, HEADS * LANE), lambda i: (i, 0))
    grid = (t // rr,)
    return _carried_call(
        body, rider, *_grid_ends(grid), name="delta_scan", grid=grid,
        in_specs=[row] * 4 + [pl.BlockSpec((HEADS, rr, DN_CHUNK), lambda i: (0, i, 0)),
                              pl.BlockSpec((nc, HEADS * LANE), lambda i: (i, 0))],
        out_specs=[row, pl.BlockSpec((nc, HEADS, LANE, LANE), lambda i: (i, 0, 0, 0))],
        out_shape=[jax.ShapeDtypeStruct((t, HEADS * LANE), F32),
                   jax.ShapeDtypeStruct((t // DN_CHUNK, HEADS, LANE, LANE), F32)],
        scratch_shapes=[pltpu.VMEM((HEADS, LANE, LANE), F32)], sem=("arbitrary",), args=(u, w, qd, kt, intra, gl))


def _delta_scan_bwd(u, w, qd, kt, intra, gl, sall, do, rider=None):
    t = u.shape[0]
    rr = min(_SCAN_ROWS, t)
    nc = rr // DN_CHUNK
    ng = t // rr

    def body(u_ref, w_ref, qd_ref, kt_ref, a_ref, gl_ref, sall_ref, do_ref,
             du_ref, dw_ref, dqd_ref, dkt_ref, da_ref, dgl_ref, ds_scr):
        @pl.when(pl.program_id(0) == 0)
        def _():
            ds_scr[...] = jnp.zeros_like(ds_scr)

        def chunk(cc, carry):
            c = nc - 1 - cc
            r0 = pl.multiple_of(c * DN_CHUNK, DN_CHUNK)
            rows = pl.ds(r0, DN_CHUNK)
            e = jnp.exp(gl_ref[pl.ds(c, 1), :])
            states = [sall_ref[c, h] for h in range(HEADS)]
            ds_outs = [ds_scr[h] for h in range(HEADS)]
            u_a, w_a, kt_a, qd_a, do_a = u_ref[rows, :], w_ref[rows, :], kt_ref[rows, :], qd_ref[rows, :], do_ref[rows, :]
            a_a = [a_ref[h, rows, :] for h in range(HEADS)]
            da, dqd, dkt, du, dw, dgl, ds_new = [], [], [], [], [], [], []
            for h in range(HEADS):
                cs = slice(h * LANE, (h + 1) * LANE)
                s, ds_out = states[h], ds_outs[h]
                w_c, kt_c, qd_c, do_c = w_a[:, cs], kt_a[:, cs], qd_a[:, cs], do_a[:, cs]
                v_new = u_a[:, cs] - _dot(w_c, s, _NN)
                dv_new = _dot(a_a[h], do_c, _TN) + _dot(kt_c, ds_out, _NN)
                da.append(_dot(do_c, v_new, _NT))
                dqd.append(_dot(do_c, s, _NT))
                dkt.append(_dot(v_new, ds_out, _NT))
                du.append(dv_new)
                dw.append(-_dot(dv_new, s, _NT))
                eh = e[:, cs]
                dgl.append(jnp.broadcast_to(jnp.sum(ds_out * s, axis=0, keepdims=True) * eh, (8, LANE)))
                ds_new.append(ds_out * eh + _dot(qd_c, do_c, _TN) - _dot(w_c, dv_new, _TN))
            cat = lambda parts: jnp.concatenate(parts, axis=1)
            dqd_ref[rows, :], dkt_ref[rows, :], du_ref[rows, :], dw_ref[rows, :] = cat(dqd), cat(dkt), cat(du), cat(dw)
            dgl_ref[pl.ds(pl.multiple_of(c * 8, 8), 8), :] = cat(dgl)
            for h in range(HEADS):
                da_ref[h, rows, :] = da[h]
                ds_scr[h] = ds_new[h]
            return carry

        lax.fori_loop(0, nc, chunk, 0)

    rev = lambda i: (ng - 1 - i, 0)
    row = pl.BlockSpec((rr, HEADS * LANE), rev)
    a_spec = pl.BlockSpec((HEADS, rr, DN_CHUNK), lambda i: (0, ng - 1 - i, 0))
    gl_spec = pl.BlockSpec((nc, HEADS * LANE), rev)
    wide = jax.ShapeDtypeStruct((t, HEADS * LANE), F32)
    outs, carried = _carried_call(
        body, rider, *_grid_ends((ng,)), name="delta_scan_bwd", grid=(ng,),
        in_specs=[row] * 4 + [a_spec, gl_spec, pl.BlockSpec((nc, HEADS, LANE, LANE), lambda i: (ng - 1 - i, 0, 0, 0)), row],
        out_specs=[row] * 4 + [a_spec, pl.BlockSpec((nc * 8, HEADS * LANE), rev)],
        out_shape=[wide] * 4 + [jax.ShapeDtypeStruct((HEADS, t, DN_CHUNK), F32),
                                jax.ShapeDtypeStruct((t // DN_CHUNK * 8, HEADS * LANE), F32)],
        scratch_shapes=[pltpu.VMEM((HEADS, LANE, LANE), F32)], sem=("arbitrary",), args=(u, w, qd, kt, intra, gl, sall, do))
    return tuple(outs[:5]) + (outs[5].reshape(t // DN_CHUNK, 8, HEADS * LANE)[:, 0, :],), carried


_ATT_TILE = 512


def _kv_rows(j, tk):
    return pl.ds(pl.multiple_of(j * tk, tk), tk)


def _att_scores(ql, qr, ckv_ref, kr_ref, j, tk):
    ks = _kv_rows(j, tk)
    return (_dot(ql, ckv_ref[ks, :], _NT) + _dot(qr, kr_ref[ks, :], _NT)) * ATT_SCALE


def _diag_mask(s):
    qi = lax.broadcasted_iota(jnp.int32, s.shape, 0)
    ki = lax.broadcasted_iota(jnp.int32, s.shape, 1)
    return jnp.where(ki <= qi, s, NEG_BIG)


def _attention(ql, qr, ckv, kr):
    t = ckv.shape[0]
    tq = min(_ATT_TILE, t)

    nl = tq // LANE

    def lane_fold(v, op):
        out = v[:, :LANE]
        for k in range(1, nl):
            out = op(out, v[:, k * LANE:(k + 1) * LANE])
        return out

    def body(ql_ref, qr_ref, ckv_ref, kr_ref, o_ref, lse_ref, s_all, m_lanes, l_lanes, acc_scr):
        qi = pl.program_id(1)
        q_lat, q_rope = ql_ref[...], qr_ref[...]
        m_lanes[...] = jnp.full_like(m_lanes, NEG_BIG)

        def scores(j, masked):
            s = _att_scores(q_lat, q_rope, ckv_ref, kr_ref, j, tq)
            if masked:
                s = _diag_mask(s)
            s_all[j] = s
            m_lanes[...] = jnp.maximum(m_lanes[...], lane_fold(s, jnp.maximum))

        def scores_body(j, carry):
            scores(j, False)
            return carry

        lax.fori_loop(0, qi, scores_body, 0)
        scores(qi, True)
        m = jnp.max(m_lanes[...], axis=-1, keepdims=True)
        mb = jnp.broadcast_to(m, (tq, LANE))
        l_lanes[...] = jnp.zeros_like(l_lanes)
        acc_scr[...] = jnp.zeros_like(acc_scr)

        def weigh(j, carry):
            s = s_all[j]
            p = jnp.concatenate([jnp.exp(s[:, k * LANE:(k + 1) * LANE] - mb) for k in range(nl)], axis=1)
            l_lanes[...] += lane_fold(p, jnp.add)
            acc_scr[...] += _dot(p, ckv_ref[_kv_rows(j, tq), :], _NN)
            return carry

        lax.fori_loop(0, qi + 1, weigh, 0)
        l = jnp.sum(l_lanes[...], axis=-1, keepdims=True)
        o_ref[...] = acc_scr[...] / l
        lse_ref[...] = m + jnp.log(l)

    return pl.pallas_call(
        body, name="attention", grid=(HEADS, t // tq),
        in_specs=[pl.BlockSpec((None, tq, KV_LORA), lambda h, i: (h, i, 0)),
                  pl.BlockSpec((tq, ROPE_PAD), lambda h, i: (i, h)),
                  pl.BlockSpec((t, KV_LORA), lambda h, i: (0, 0)),
                  pl.BlockSpec((t, ROPE_PAD), lambda h, i: (0, 0))],
        out_specs=[pl.BlockSpec((None, tq, KV_LORA), lambda h, i: (h, i, 0)),
                   pl.BlockSpec((None, tq, 1), lambda h, i: (h, i, 0))],
        out_shape=[jax.ShapeDtypeStruct((HEADS, t, KV_LORA), F32), jax.ShapeDtypeStruct((HEADS, t, 1), F32)],
        scratch_shapes=[pltpu.VMEM((t // tq, tq, tq), F32), pltpu.VMEM((tq, LANE), F32), pltpu.VMEM((tq, LANE), F32),
                        pltpu.VMEM((tq, KV_LORA), F32)],
        compiler_params=_cparams(("parallel", "parallel")))(ql, qr, ckv, kr)


def _attention_bwd(ql, qr, ckv, kr, out, lse, dout):
    t = ckv.shape[0]
    tq = min(_ATT_TILE, t)

    def body(ql_ref, qr_ref, ckv_ref, kr_ref, o_ref, lse_ref, do_ref, dql_ref, dqr_ref, dckv_ref, dkr_ref,
             dql_scr, dqr_scr):
        h, qi = pl.program_id(0), pl.program_id(1)

        @pl.when((h == 0) & (qi == 0))
        def _():
            dckv_ref[...] = jnp.zeros_like(dckv_ref)
            dkr_ref[...] = jnp.zeros_like(dkr_ref)

        q_lat, q_rope = ql_ref[...], qr_ref[...]
        d_o = do_ref[...].astype(_CDT)
        lse_v = lse_ref[...]
        dsum = jnp.sum(do_ref[...] * o_ref[...], axis=-1, keepdims=True)
        dql_scr[...] = jnp.zeros_like(dql_scr)
        dqr_scr[...] = jnp.zeros_like(dqr_scr)

        def step(j, masked):
            ks = _kv_rows(j, tq)
            s = _att_scores(q_lat, q_rope, ckv_ref, kr_ref, j, tq)
            if masked:
                s = _diag_mask(s)
            p = jnp.exp(s - lse_v)
            kv = ckv_ref[ks, :]
            ds = (p * (_dot(d_o, kv, _NT) - dsum) * ATT_SCALE).astype(_CDT)
            pb = p.astype(_CDT)
            dql_scr[...] += _dot(ds, kv, _NN)
            dqr_scr[...] += _dot(ds, kr_ref[ks, :], _NN)
            dckv_ref[ks, :] += _dot(pb, d_o, _TN) + _dot(ds, q_lat, _TN)
            dkr_ref[ks, :] += _dot(ds, q_rope, _TN)

        def loop_body(j, carry):
            step(j, False)
            return carry

        lax.fori_loop(0, qi, loop_body, 0)
        step(qi, True)
        dql_ref[...] = dql_scr[...].astype(dql_ref.dtype)
        dqr_ref[...] = dqr_scr[...]

    lat = pl.BlockSpec((None, tq, KV_LORA), lambda h, i: (h, i, 0))
    rope = pl.BlockSpec((tq, ROPE_PAD), lambda h, i: (i, h))
    kfull = pl.BlockSpec((t, KV_LORA), lambda h, i: (0, 0))
    rfull = pl.BlockSpec((t, ROPE_PAD), lambda h, i: (0, 0))
    return pl.pallas_call(
        body, name="attention_bwd", grid=(HEADS, t // tq),
        in_specs=[lat, rope, kfull, rfull, lat, pl.BlockSpec((None, tq, 1), lambda h, i: (h, i, 0)), lat],
        out_specs=[lat, rope, kfull, rfull],
        out_shape=[jax.ShapeDtypeStruct((HEADS, t, KV_LORA), _CDT), jax.ShapeDtypeStruct((t, HEADS * ROPE_PAD), F32),
                   jax.ShapeDtypeStruct((t, KV_LORA), F32), jax.ShapeDtypeStruct((t, ROPE_PAD), F32)],
        scratch_shapes=[pltpu.VMEM((tq, KV_LORA), F32), pltpu.VMEM((tq, ROPE_PAD), F32)],
        compiler_params=_cparams(("arbitrary", "arbitrary")))(ql, qr, ckv, kr, out, lse, dout)


_DX_ROWS = 256


def _dx_fused(pairs, add, add_scale, rider=None):
    t, d = add.shape
    tm = min(_DX_ROWS, t)
    n = len(pairs)

    def body(*refs):
        acc = refs[2 * n][...] * add_scale
        for i in range(n):
            acc = acc + _dot(refs[i][...], refs[n + i][...], _NT)
        refs[2 * n + 1][...] = acc

    in_specs = [pl.BlockSpec((tm, a.shape[1]), lambda i: (i, 0)) for a, _ in pairs]
    in_specs += [pl.BlockSpec(w.shape, lambda i: (0, 0)) for _, w in pairs]
    row = pl.BlockSpec((tm, d), lambda i: (i, 0))
    grid = (t // tm,)
    (dx,), carried = _carried_call(
        body, rider, *_grid_ends(grid), name="b_dx", grid=grid, in_specs=in_specs + [row], out_specs=[row],
        out_shape=[jax.ShapeDtypeStruct((t, d), F32)], scratch_shapes=[], sem=("arbitrary",),
        args=tuple(a for a, _ in pairs) + tuple(w for _, w in pairs) + (add,))
    return dx, carried


def _ffn_in_swiglu(a, w):
    t, k = a.shape
    hid = w.shape[1] // 2
    tm, tn = _tile(t, 1024), _tile(hid, 1408)
    nj = hid // tn

    def body(a_ref, bg_ref, bu_ref, act_ref, gt_ref, up_ref):
        av = a_ref[...].astype(_CDT)
        gt = jnp.dot(av, bg_ref[...].astype(_CDT), preferred_element_type=F32)
        up = jnp.dot(av, bu_ref[...].astype(_CDT), preferred_element_type=F32)
        act_ref[...] = _swiglu(gt, up).astype(act_ref.dtype)
        gt_ref[...] = gt.astype(gt_ref.dtype)
        up_ref[...] = up.astype(up_ref.dtype)

    out = pl.BlockSpec((tm, tn), lambda i, j: (i, j))
    return pl.pallas_call(
        body, name="f_ffn_in_swiglu", grid=(t // tm, nj),
        in_specs=[pl.BlockSpec((tm, k), lambda i, j: (i, 0)), pl.BlockSpec((k, tn), lambda i, j: (0, j)),
                  pl.BlockSpec((k, tn), lambda i, j: (0, nj + j))],
        out_specs=[out] * 3, out_shape=[jax.ShapeDtypeStruct((t, hid), _CDT)] * 3,
        compiler_params=_cparams(("parallel", "parallel")))(a, w, w)


def _gated_norm(o, z, w):
    return _rms_norm(o, w) * _silu(z)


def _gated_norm_heads(o, z, w):
    heads = [_gated_norm(o[:, h * LANE:(h + 1) * LANE], z[:, h * LANE:(h + 1) * LANE], w) for h in range(HEADS)]
    return jnp.concatenate(heads, axis=1)


def _mla_pre(ckv, krp, cq, cosb, sinb, qw, kw):
    return _rms_norm(cq, qw), _rms_norm(ckv, kw), _rope(krp, cosb, sinb)


def _merge(gg, y_dn, y_mla):
    return _sigmoid(gg[:, :D_MODEL]) * y_dn + _sigmoid(gg[:, D_MODEL:]) * y_mla


def _ln1(xv, attn_out, g, b):
    return _layer_norm(ALPHA * xv + attn_out, g, b)


def _final(h1, ffn, gate_pre, ple_proj, g, b):
    return _layer_norm(ALPHA * h1 + ffn + _sigmoid(gate_pre) * ple_proj, g, b)


def _swiglu(gt, up):
    return _silu(gt) * up


def _local_step(x, p, cosb, sinb, target, wt, sp, exch):
    t = x.shape[0]
    bf = _CDT
    xb = x.astype(bf)
    g = {}

    qkv_pre = _mm2(xb, wt['qkv'], name="f_qkv")
    z = _mm2(xb, wt['z'], name="f_z")
    gg = _mm2(xb, wt['gg'], name="f_gg")
    pm = _mm2(xb, wt['mla'], name="f_mla")
    qkv_act = _conv_silu(qkv_pre, sp['conv_w'])
    (u, w_, qd, kt, intra, gl, t_inv), sent = _delta_local(qkv_act, pm, sp['a_log'], sp['dt_bias'], rider=exch.gather_send())
    (o_dn, sall), passed = _delta_scan(u, w_, qd, kt, intra, gl, rider=exch.gather_pass(sent))
    wt = dict(wt, **exch.weights(passed))
    (og,) = _rowwise(lambda h, o, zz, w: (_gated_norm_heads(o, zz, w),),
                     [(o_dn, D_MODEL, 0, False), (z, D_MODEL, 0, False)], [sp['dn_norm_w']],
                     [(D_MODEL, D_MODEL, False, bf)], name="f_gated_norm")
    y_dn = _mm2(og, wt['br_dn'], name="f_br_dn")

    c_q, c_kv, k_rope = _rowwise(
        lambda h, *a: _mla_pre(*a),
        [(pm, KV_LORA, 0, False), (pm, ROPE_PAD, 2, False), (pm, Q_LORA, 2, False),
         (cosb, ROPE_PAD, 0, False), (sinb, ROPE_PAD, 0, False)],
        [sp['q_norm_w'], sp['kv_norm_w']],
        [(Q_LORA, Q_LORA, False, bf), (KV_LORA, KV_LORA, False, bf), (ROPE_PAD, ROPE_PAD, False, bf)], name="f_mla_pre")
    q_nope = _mm2(c_q, wt['uq_nope'], name="f_uq_nope", out_dtype=bf)
    q_rope_pre = _mm2(c_q, wt['uq_rope'], name="f_uq_rope")
    (q_rope,) = _rowwise(lambda h, q, c, s: (_rope(q, c, s),),
                         [(q_rope_pre, HEADS * ROPE_PAD, 0, False), (cosb, ROPE_PAD, 0, False), (sinb, ROPE_PAD, 0, False)],
                         [], [(HEADS * ROPE_PAD, HEADS * ROPE_PAD, False, bf)], name="f_q_rope")
    q_lat = _mm(q_nope, wt['uk'], name="f_q_lat", tb=True, heads=HEADS, a_head='col', b_head='lead', out_head='lead',
                dims=(t, KV_LORA, NOPE), out_dtype=bf)
    out_lat, lse = _attention(q_lat, q_rope, c_kv, k_rope)
    o_mla = _mm(out_lat, wt['uv'], name="f_o_mla", heads=HEADS, a_head='lead', b_head='lead', out_head='col',
                dims=(t, NOPE, KV_LORA), out_dtype=bf)
    y_mla = _mm2(o_mla, wt['br_mla'], name="f_br_mla")

    (mixed,) = _rowwise(lambda h, *a: (_merge(*a),),
                        [(gg, 2 * D_MODEL, 0, False), (y_dn, D_MODEL, 0, False), (y_mla, D_MODEL, 0, False)],
                        [], [(D_MODEL, D_MODEL, False, bf)], name="f_merge")
    attn_out = _mm2(mixed, wt['o'], name="f_o")
    h1, h1b = _rowwise(lambda h, *a: (_ln1(*a),) * 2, [(x, D_MODEL, 0, False), (attn_out, D_MODEL, 0, False)],
                       [sp['ln1_g'], sp['ln1_b']], [(D_MODEL, D_MODEL, False, F32), (D_MODEL, D_MODEL, False, bf)],
                       name="f_ln1")
    act, ffn_gt, ffn_up = _ffn_in_swiglu(h1b, wt['ffn_in'])
    ffn = _mm2(act, wt['ffn_out'], name="f_ffn_out")
    gate_pre = _mm2(h1b, wt['ple_gate'], name="f_ple_gate")
    pb = p.astype(bf)
    ple_proj = _mm2(pb, wt['ple'], name="f_ple")

    def final_fn(h, h1v, ffnv, gpv, ppv, tgt, gv, bv):
        y, vjp = jax.vjp(_final, h1v, ffnv, gpv, ppv, gv, bv)
        err = y - tgt
        dh1, dffn, dgp, dpp, dg, db = vjp(err * (1.0 / D_MODEL))
        sq = err * err
        lanes = sq[:, :LANE]
        for j in range(1, D_MODEL // LANE):
            lanes = lanes + sq[:, j * LANE:(j + 1) * LANE]
        loss = jnp.sum(lanes, axis=0, keepdims=True) * (0.5 / D_MODEL)
        return dffn, dffn, dgp, dpp, dg, db, loss

    dpre2, dpre2b, dgate_pre, dple_proj, g['ln2_g'], g['ln2_b'], loss_lanes = _rowwise(
        final_fn, [(a, D_MODEL, 0, False) for a in (h1, ffn, gate_pre, ple_proj, target)],
        [sp['ln2_g'], sp['ln2_b']],
        [(D_MODEL, D_MODEL, False, F32)] + [(D_MODEL, D_MODEL, False, bf)] * 3,
        [(1, D_MODEL), (1, D_MODEL), (1, LANE)], name="b_final")

    g['ple'] = _mm2(pb, dple_proj, ta=True, name="g_ple")
    g['ple_gate'] = _mm2(h1b, dgate_pre, ta=True, name="g_ple_gate")
    g['ffn_out'] = _mm2(act, dpre2b, ta=True, name="g_ffn_out")
    dact = _mm2(dpre2b, wt['ffn_out'], tb=True, name="b_dact", out_dtype=bf)

    def swiglu_bwd(h, gt, up, d):
        _, vjp = jax.vjp(_swiglu, gt.astype(F32), up.astype(F32))
        dgt, dup = vjp(d.astype(F32))
        return (jnp.concatenate([dgt, dup], axis=1),)

    (dffn_in,) = _rowwise(swiglu_bwd, [(ffn_gt, FFN_HIDDEN, 0, False), (ffn_up, FFN_HIDDEN, 0, False),
                                       (dact, FFN_HIDDEN, 0, False)], [],
                          [(2 * FFN_HIDDEN, 2 * FFN_HIDDEN, False, bf)], name="b_swiglu")
    g['ffn_in'] = _mm2(h1b, dffn_in, ta=True, name="g_ffn_in")
    dh1 = _mm2(dffn_in, wt['ffn_in'], tb=True, name="b_dh1_ffn", add=dpre2, add_scale=ALPHA)
    dh1 = _mm2(dgate_pre, wt['ple_gate'], tb=True, name="b_dh1_gate", add=dh1)

    def ln1_bwd(h, xv, ao, d, gv, bv):
        _, vjp = jax.vjp(_ln1, xv, ao, gv, bv)
        _, dao, dg, db = vjp(d)
        return dao, dao, dg, db

    dpre1, dpre1b, g['ln1_g'], g['ln1_b'] = _rowwise(
        ln1_bwd, [(x, D_MODEL, 0, False), (attn_out, D_MODEL, 0, False), (dh1, D_MODEL, 0, False)],
        [sp['ln1_g'], sp['ln1_b']], [(D_MODEL, D_MODEL, False, F32), (D_MODEL, D_MODEL, False, bf)],
        [(1, D_MODEL), (1, D_MODEL)], name="b_ln1")

    g['o'] = _mm2(mixed, dpre1b, ta=True, name="g_o")
    dmixed = _mm2(dpre1b, wt['o'], tb=True, name="b_dmixed")

    def merge_bwd(h, ggv, yd, ym, d):
        _, vjp = jax.vjp(_merge, ggv, yd, ym)
        return vjp(d)

    dgg, dy_dn, dy_mla = _rowwise(
        merge_bwd, [(gg, 2 * D_MODEL, 0, False), (y_dn, D_MODEL, 0, False), (y_mla, D_MODEL, 0, False),
                    (dmixed, D_MODEL, 0, False)], [],
        [(2 * D_MODEL, 2 * D_MODEL, False, bf), (D_MODEL, D_MODEL, False, bf), (D_MODEL, D_MODEL, False, bf)],
        name="b_merge")
    g['br_dn'] = _mm2(og, dy_dn, ta=True, name="g_br_dn")
    dog = _mm2(dy_dn, wt['br_dn'], tb=True, name="b_dog")
    g['br_mla'] = _mm2(o_mla, dy_mla, ta=True, name="g_br_mla")
    do_mla = _mm2(dy_mla, wt['br_mla'], tb=True, name="b_do_mla", out_dtype=bf)

    dout_lat = _mm(do_mla, wt['uv'], name="b_dout_lat", tb=True, heads=HEADS, a_head='col', b_head='lead',
                   out_head='lead', dims=(t, KV_LORA, NOPE))
    g['uv'] = _mm(out_lat, do_mla, name="g_uv", ta=True, heads=HEADS, a_head='lead', b_head='col', out_head='lead',
                  dims=(KV_LORA, NOPE, t))
    dq_lat, dq_rope, dckv_att, dkr_att = _attention_bwd(q_lat, q_rope, c_kv, k_rope, out_lat, lse, dout_lat)
    dq_nope = _mm(dq_lat, wt['uk'], name="b_dq_nope", heads=HEADS, a_head='lead', b_head='lead', out_head='col',
                  dims=(t, NOPE, KV_LORA), out_dtype=bf)
    g['uk'] = _mm(dq_lat, q_nope, name="g_uk", ta=True, heads=HEADS, a_head='lead', b_head='col', out_head='lead',
                  dims=(KV_LORA, NOPE, t))
    (dq_rope_pre,) = _rowwise(lambda h, d, c, s: (_rope_bwd(d, c, s),),
                              [(dq_rope, HEADS * ROPE_PAD, 0, False), (cosb, ROPE_PAD, 0, False), (sinb, ROPE_PAD, 0, False)],
                              [], [(HEADS * ROPE_PAD, HEADS * ROPE_PAD, False, bf)], name="b_q_rope")
    g['uq_nope'] = _mm2(c_q, dq_nope, ta=True, name="g_uq_nope")
    g['uq_rope'] = _mm2(c_q, dq_rope_pre, ta=True, name="g_uq_rope")
    dc_q = _mm2(dq_nope, wt['uq_nope'], tb=True, name="b_dcq_nope")
    dc_q = _mm2(dq_rope_pre, wt['uq_rope'], tb=True, name="b_dcq_rope", add=dc_q)

    def gated_norm_bwd(h, o, zz, d, w):
        _, vjp = jax.vjp(_gated_norm_heads, o, zz, w)
        return vjp(d)

    do_dn, dz, g['dn_norm_w'] = _rowwise(
        gated_norm_bwd, [(o_dn, D_MODEL, 0, False), (z, D_MODEL, 0, False), (dog, D_MODEL, 0, False)], [sp['dn_norm_w']],
        [(D_MODEL, D_MODEL, False, F32), (D_MODEL, D_MODEL, False, bf)], [(1, LANE)], name="b_gated_norm")
    (du, dw, dqd, dkt, dintra, dgl), paired = _delta_scan_bwd(u, w_, qd, kt, intra, gl, sall, do_dn, rider=exch.pair_send(g))
    (dq_a, dk_a, dv_a, dba, g['a_log'], g['dt_bias']), arrived = _delta_local_bwd(
        qkv_act, pm, sp['a_log'], sp['dt_bias'], t_inv, du, dw, dqd, dkt, dintra, dgl, rider=exch.reduce_send(paired))
    exch.reduce_arrived(arrived)
    dqkv_pre, g['conv_w'] = _conv_silu_bwd(qkv_pre, sp['conv_w'], [dq_a, dk_a, dv_a])

    def mla_pre_bwd(h, ckv, krp, cq, cosv, sinv, dcq, dckv, dkr, qw, kw):
        _, vjp = jax.vjp(lambda a, b, c, d, e: (_rms_norm(c, d), _rms_norm(a, e)), ckv, krp, cq, qw, kw)
        dckv_p, _, dcq_p, dqw, dkw = vjp((dcq, dckv))
        dkr_p = _rope_bwd(dkr, cosv, sinv)
        dpm = jnp.concatenate([dckv_p, dkr_p, jnp.zeros((ckv.shape[0], 3 * LANE), F32), dcq_p], axis=1)
        return dpm, dqw, dkw

    dpm_main, g['q_norm_w'], g['kv_norm_w'] = _rowwise(
        mla_pre_bwd,
        [(pm, KV_LORA, 0, False), (pm, ROPE_PAD, 2, False), (pm, Q_LORA, 2, False),
         (cosb, ROPE_PAD, 0, False), (sinb, ROPE_PAD, 0, False),
         (dc_q, Q_LORA, 0, False), (dckv_att, KV_LORA, 0, False), (dkr_att, ROPE_PAD, 0, False)],
        [sp['q_norm_w'], sp['kv_norm_w']], [(1152, 1152, False, F32)], [(1, Q_LORA), (1, KV_LORA)], name="b_mla_pre")
    (dpm,) = _rowwise(
        lambda h, a, b: (jnp.concatenate([a[:, :3 * LANE], b, a[:, 4 * LANE:]], axis=1),),
        [(dpm_main, 1152, 0, False), (dba, LANE, 0, False)], [], [(1152, 1152, False, bf)], name="b_dpm")

    g['qkv'] = _mm2(xb, dqkv_pre, ta=True, name="g_qkv")
    g['z'] = _mm2(xb, dz, ta=True, name="g_z")
    g['gg'] = _mm2(xb, dgg, ta=True, name="g_gg")
    g['mla'] = _mm2(xb, dpm, ta=True, name="g_mla")
    dx, arrived = _dx_fused([(dqkv_pre, wt['qkv']), (dz, wt['z']), (dgg, wt['gg']), (dpm, wt['mla'])], dpre1, ALPHA,
                            rider=exch.in_send(g))
    exch.in_arrived(arrived)
    return loss_lanes, dx, g


_IN_SIZES = (QKV_W, HEADS * DN_DK, HEADS, HEADS, Q_LORA, KV_LORA, ROPE, D_MODEL, D_MODEL)


def _rope_tables(positions):
    inv_freq = ROPE_BASE ** (-jnp.arange(0, ROPE, 2, dtype=F32) / ROPE)
    ang = positions.astype(F32)[:, None] * inv_freq
    cos, sin = jnp.cos(ang), jnp.sin(ang)
    zeros = jnp.zeros((positions.shape[0], ROPE_PAD - ROPE), F32)
    return jnp.concatenate([cos, cos, zeros], axis=1), jnp.concatenate([-sin, sin, zeros], axis=1)


def _prep_w_in(w_in):
    dt = w_in.dtype
    offs = [0]
    for s in _IN_SIZES:
        offs.append(offs[-1] + s)
    qkv, z, wb, wa, cq, ckv, kr, gd, gm = [w_in[:, offs[i]:offs[i + 1]] for i in range(len(_IN_SIZES))]
    zc = lambda n: jnp.zeros((D_MODEL, n), dt)
    return {
        'qkv': qkv, 'z': z, 'gg': jnp.concatenate([gd, gm], axis=1),
        'mla': jnp.concatenate([ckv, kr, zc(ROPE_PAD - ROPE), wb, wa, zc(LANE - 2 * HEADS), zc(2 * LANE), cq], axis=1),
    }


def _prep_weights(full):
    w_uq = full['w_uq']
    wt = {
        'uq_nope': w_uq[:, :, :NOPE].reshape(Q_LORA, HEADS * NOPE),
        'uq_rope': jnp.pad(w_uq[:, :, NOPE:], ((0, 0), (0, 0), (0, ROPE_PAD - ROPE))).reshape(Q_LORA, HEADS * ROPE_PAD),
        'uk': jnp.transpose(full['w_uk'], (1, 0, 2)), 'uv': jnp.transpose(full['w_uv'], (1, 0, 2)),
        'br_dn': full['w_br_dn'], 'br_mla': full['w_br_mla'], 'o': full['w_o'], 'ffn_in': full['w_ffn_in'],
        'ffn_out': full['w_ffn_out'], 'ple': full['w_ple'], 'ple_gate': full['w_ple_gate'],
    }
    return wt


def _prep_small(small):
    pad = lambda v: jnp.pad(v, (0, LANE - v.shape[0]))[None, :]
    return {
        'conv_w': small['conv_w'], 'a_log': pad(small['dn_a_log']), 'dt_bias': pad(small['dn_dt_bias']),
        'dn_norm_w': small['dn_norm_w'][None, :], 'q_norm_w': small['q_norm_w'][None, :],
        'kv_norm_w': small['kv_norm_w'][None, :], 'ln1_g': small['ln1_g'][None, :], 'ln1_b': small['ln1_b'][None, :],
        'ln2_g': small['ln2_g'][None, :], 'ln2_b': small['ln2_b'][None, :],
    }


def _w_in_grad(g):
    mla = g['mla']
    ba0 = KV_LORA + ROPE_PAD
    cq0 = ba0 + 3 * LANE
    return jnp.concatenate([
        g['qkv'], g['z'], mla[:, ba0:ba0 + HEADS], mla[:, ba0 + HEADS:ba0 + 2 * HEADS], mla[:, cq0:cq0 + Q_LORA],
        mla[:, :KV_LORA], mla[:, KV_LORA:KV_LORA + ROPE], g['gg']], axis=1)


def _unprep_grads_late(g):
    return {
        'conv_w': g['conv_w'], 'dn_a_log': g['a_log'][0, :HEADS], 'dn_dt_bias': g['dt_bias'][0, :HEADS],
        'dn_norm_w': g['dn_norm_w'][0], 'q_norm_w': g['q_norm_w'][0], 'kv_norm_w': g['kv_norm_w'][0],
        'ln1_g': g['ln1_g'][0], 'ln1_b': g['ln1_b'][0], 'ln2_g': g['ln2_g'][0], 'ln2_b': g['ln2_b'][0],
    }


def _unprep_grads_early(g):
    w_uq = jnp.concatenate([g['uq_nope'].reshape(Q_LORA, HEADS, NOPE),
                            g['uq_rope'].reshape(Q_LORA, HEADS, ROPE_PAD)[:, :, :ROPE]], axis=2)
    return {
        'w_uq': w_uq, 'w_uk': jnp.transpose(g['uk'], (1, 0, 2)), 'w_uv': jnp.transpose(g['uv'], (1, 0, 2)),
        'w_br_dn': g['br_dn'], 'w_br_mla': g['br_mla'], 'w_o': g['o'],
        'w_ffn_in': g['ffn_in'], 'w_ffn_out': g['ffn_out'], 'w_ple': g['ple'], 'w_ple_gate': g['ple_gate'],
    }


_FLATB_PIECES = (
    ('w_ffn_out', 704, (704, D_MODEL)), ('w_br_dn', 256, (256, D_MODEL)), ('w_br_mla', 256, (256, D_MODEL)),
    ('w_o', 256, (256, D_MODEL)), ('w_ple_gate', 256, (256, D_MODEL)), ('w_uq', 144, (96, HEADS, NOPE + ROPE)),
    ('w_uk', 64, (64, HEADS, NOPE)), ('w_uv', 64, (64, HEADS, NOPE)), ('w_ple', 64, (PLE_DIM, 256)),
)
FLATB_ROWS = 2112
W_IN_SHARD = D_IN // N_SHARD
FFN_IN_SHARD = 2 * FFN_HIDDEN // N_SHARD
A_ROWS = D_MODEL + 32
_CONV_SHARD = QKV_W // N_SHARD
_ADD_TILES = (256, 256, 352)


def _flatb_offsets():
    offs, o = {}, 0
    for name, rows, _ in _FLATB_PIECES:
        offs[name] = o
        o += rows
    return offs, o


def _pack_shards(ws, conv_w):
    conv_bits = lax.bitcast_convert_type(conv_w, jnp.bfloat16).reshape(DN_CONV, 2 * _CONV_SHARD).astype(_CDT)
    tail = jnp.pad(conv_bits, ((0, A_ROWS - D_MODEL - DN_CONV), (0, W_IN_SHARD - 2 * _CONV_SHARD)))
    a_buf = jnp.concatenate([ws['w_in'].astype(_CDT), tail], axis=0)
    parts = [ws[name].astype(_CDT).reshape(rows, FLAT_W) for name, rows, _ in _FLATB_PIECES]
    used = sum(p.shape[0] for p in parts)
    parts.append(jnp.zeros((FLATB_ROWS - used, FLAT_W), _CDT))
    return [a_buf, ws['w_ffn_in'].astype(_CDT), jnp.concatenate(parts, axis=0)]


def _unpack_w_in(gathered, local, me):
    a = [jnp.where(me == s, local, gathered[s]) for s in range(N_SHARD)]
    conv = [lax.bitcast_convert_type(
        p[D_MODEL:D_MODEL + DN_CONV, :2 * _CONV_SHARD].astype(jnp.bfloat16).reshape(DN_CONV, _CONV_SHARD, 2), F32) for p in a]
    return jnp.concatenate([p[:D_MODEL] for p in a], axis=1), jnp.concatenate(conv, axis=1)


def _unpack_rest(gathered, local, me):
    pick = lambda b, s: jnp.where(me == s, local[b], gathered[b][s])
    full = {'w_ffn_in': jnp.concatenate([pick(0, s) for s in range(N_SHARD)], axis=1)}
    offs, _ = _flatb_offsets()
    fb = [pick(1, s) for s in range(N_SHARD)]
    for name, rows, shape in _FLATB_PIECES:
        pieces = [p[offs[name]:offs[name] + rows].reshape(shape) for p in fb]
        full[name] = jnp.concatenate(pieces, axis=1 if name == 'w_ple' else 0)
    return full


def _shard_columns(g, w):
    return jnp.stack([g[:, s * w:(s + 1) * w] for s in range(N_SHARD)])


def _pack_grads_rest(gw):
    parts = []
    for name, rows, _ in _FLATB_PIECES:
        g = gw[name]
        if name == 'w_ple':
            parts.append(_shard_columns(g, PLE_DIM).reshape(N_SHARD, rows, FLAT_W))
        else:
            parts.append(g.reshape(N_SHARD, rows, FLAT_W))
    used = sum(p.shape[1] for p in parts)
    parts.append(jnp.zeros((N_SHARD, FLATB_ROWS - used, FLAT_W), F32))
    return [_shard_columns(gw['w_ffn_in'], FFN_IN_SHARD), jnp.concatenate(parts, axis=1)]


def _unpack_reduced(mine, theirs, c):
    whole = [jnp.concatenate([jnp.where(c == 0, m, t), jnp.where(c == 0, t, m)], axis=0) for m, t in zip(mine, theirs)]
    out = {'w_in': whole[0], 'w_ffn_in': whole[1]}
    offs, _ = _flatb_offsets()
    for name, rows, shape in _FLATB_PIECES:
        out[name] = whole[2][offs[name]:offs[name] + rows].reshape(shape)
    return out


_HBM = pl.BlockSpec(memory_space=pltpu.HBM)


def _place():
    x, y, c = lax.axis_index("x"), lax.axis_index("y"), lax.axis_index("c")
    chips = [(1 - x, y), (x, 1 - y), (1 - x, 1 - y)]
    return x, y, c, chips


def _remote(src, dst, send_sems, recv_sems, k, to):
    return pltpu.make_async_remote_copy(src_ref=src, dst_ref=dst, send_sem=send_sems.at[k], recv_sem=recv_sems.at[k],
                                        device_id=to, device_id_type=_MESH)


def _half_rows(ref, half, hf, lead=None):
    rows = pl.ds(pl.multiple_of(hf * half, 16), half)
    return ref.at[rows, :] if lead is None else ref.at[lead, rows, :]


class _Rider:
    def __init__(self, inputs, out_shape, n_sems, copies, aliases=None):
        self.inputs, self.out_shape, self.n_sems, self.copies = list(inputs), list(out_shape), n_sems, copies
        self.aliases = aliases or {}


def _carried_call(body, rider, first, last, *, name, grid, in_specs, out_specs, out_shape, scratch_shapes, sem, args):
    n_in, n_out, n_scr = len(in_specs), len(out_specs), len(scratch_shapes)
    if rider is None:
        res = pl.pallas_call(body, name=name, grid=grid, in_specs=in_specs, out_specs=out_specs, out_shape=out_shape,
                             scratch_shapes=scratch_shapes, compiler_params=_cparams(sem))(*args)
        return list(res), []
    ri, ro = len(rider.inputs), len(rider.out_shape)

    def full_body(*refs):
        own_in, r_in = refs[:n_in], refs[n_in:n_in + ri]
        o0 = n_in + ri
        own_out, r_out = refs[o0:o0 + n_out], refs[o0 + n_out:o0 + n_out + ro]
        s0 = o0 + n_out + ro
        own_scr, send_sems, recv_sems = refs[s0:s0 + n_scr], refs[s0 + n_scr], refs[s0 + n_scr + 1]

        @pl.when(first())
        def _():
            sends, _ = rider.copies(r_in, r_out, send_sems, recv_sems)
            for cp in sends:
                cp.start()

        body(*own_in, *own_out, *own_scr)

        @pl.when(last())
        def _():
            sends, arrivals = rider.copies(r_in, r_out, send_sems, recv_sems)
            for cp in arrivals():
                cp.wait_recv()
            for cp in sends:
                cp.wait_send()

    res = pl.pallas_call(
        full_body, name=name, grid=grid, in_specs=list(in_specs) + [_HBM] * ri, out_specs=list(out_specs) + [_HBM] * ro,
        out_shape=list(out_shape) + rider.out_shape,
        scratch_shapes=list(scratch_shapes) + [pltpu.SemaphoreType.DMA((rider.n_sems,))] * 2,
        input_output_aliases={n_in + i: n_out + o for i, o in rider.aliases.items()},
        compiler_params=_cparams(sem))(*args, *rider.inputs)
    return list(res[:n_out]), list(res[n_out:])


def _ride_gather_send(bufs):
    n = len(bufs)
    halves = [b.shape[0] // 2 for b in bufs]

    def copies(ins, outs, send_sems, recv_sems):
        x, y, c, chips = _place()
        slot = lambda b, cx, cy: _half_rows(outs[b], halves[b], c, lead=2 * cx + cy)
        sends = [_remote(_half_rows(ins[b], halves[b], c), slot(b, x, y), send_sems, recv_sems, 3 * b + j, (cx, cy, c))
                 for b in range(n) for j, (cx, cy) in enumerate(chips)]
        arrivals = lambda: [_remote(slot(b, cx, cy), slot(b, cx, cy), send_sems, recv_sems, 3 * b + j, (x, y, c))
                            for b in range(n) for j, (cx, cy) in enumerate(chips)]
        return sends, arrivals

    return _Rider(bufs, [jax.ShapeDtypeStruct((N_SHARD,) + b.shape, b.dtype) for b in bufs], 3 * n, copies)


def _ride_gather_pass(gathered):
    n = len(gathered)
    halves = [g.shape[1] // 2 for g in gathered]

    def copies(ins, outs, send_sems, recv_sems):
        x, y, c, chips = _place()
        slot = lambda b, cx, cy, hf: _half_rows(outs[b], halves[b], hf, lead=2 * cx + cy)
        sends = [_remote(slot(b, cx, cy, c), slot(b, cx, cy, c), send_sems, recv_sems, 3 * b + j, (x, y, 1 - c))
                 for b in range(n) for j, (cx, cy) in enumerate(chips)]
        arrivals = lambda: [_remote(slot(b, cx, cy, 1 - c), slot(b, cx, cy, 1 - c), send_sems, recv_sems, 3 * b + j, (x, y, c))
                            for b in range(n) for j, (cx, cy) in enumerate(chips)]
        return sends, arrivals

    return _Rider(gathered, [jax.ShapeDtypeStruct(g.shape, g.dtype) for g in gathered], 3 * n, copies,
                  aliases={b: b for b in range(n)})


def _ride_pair_exchange(gbufs):
    n = len(gbufs)
    halves = [g.shape[1] // 2 for g in gbufs]

    def copies(ins, outs, send_sems, recv_sems):
        x, y, c, _ = _place()
        sends = [_remote(ins[b].at[:, pl.ds(pl.multiple_of((1 - c) * halves[b], 16), halves[b]), :], outs[b],
                         send_sems, recv_sems, b, (x, y, 1 - c)) for b in range(n)]
        arrivals = lambda: [_remote(outs[b], outs[b], send_sems, recv_sems, b, (x, y, c)) for b in range(n)]
        return sends, arrivals

    return _Rider(gbufs, [jax.ShapeDtypeStruct((N_SHARD, h, g.shape[2]), g.dtype) for g, h in zip(gbufs, halves)], n, copies)


def _ride_chip_exchange(parts):
    n = len(parts)

    def copies(ins, outs, send_sems, recv_sems):
        x, y, c, chips = _place()
        sends = [_remote(ins[b].at[2 * cx + cy], outs[b].at[j], send_sems, recv_sems, 3 * b + j, (cx, cy, c))
                 for b in range(n) for j, (cx, cy) in enumerate(chips)]
        arrivals = lambda: [_remote(ins[b].at[0], outs[b].at[j], send_sems, recv_sems, 3 * b + j, (x, y, c))
                            for b in range(n) for j in range(len(chips))]
        return sends, arrivals

    return _Rider(parts, [jax.ShapeDtypeStruct((3,) + p.shape[1:], p.dtype) for p in parts], 3 * n, copies)


def _gather_shards(bufs, name):
    n = len(bufs)
    halves = [b.shape[0] // 2 for b in bufs]

    def body(*refs):
        ins, outs, send_sems, recv_sems = refs[:n], refs[n:2 * n], refs[2 * n], refs[2 * n + 1]
        x, y, c, chips = _place()
        me, sibling = (x, y, c), (x, y, 1 - c)
        slot = lambda b, cx, cy, hf: _half_rows(outs[b], halves[b], hf, lead=2 * cx + cy)
        first = [_remote(_half_rows(ins[b], halves[b], c), slot(b, x, y, c), send_sems, recv_sems, 6 * b + j, (cx, cy, c))
                 for b in range(n) for j, (cx, cy) in enumerate(chips)]
        for cp in first:
            cp.start()
        passed = []
        for j, (cx, cy) in enumerate(chips):
            for b in range(n):
                _remote(slot(b, cx, cy, c), slot(b, cx, cy, c), send_sems, recv_sems, 6 * b + j, me).wait_recv()
                fwd = _remote(slot(b, cx, cy, c), slot(b, cx, cy, c), send_sems, recv_sems, 6 * b + 3 + j, sibling)
                fwd.start()
                passed.append(fwd)
        for j, (cx, cy) in enumerate(chips):
            for b in range(n):
                _remote(slot(b, cx, cy, 1 - c), slot(b, cx, cy, 1 - c), send_sems, recv_sems, 6 * b + 3 + j, me).wait_recv()
        for cp in first + passed:
            cp.wait_send()

    return pl.pallas_call(
        body, name=name, out_shape=[jax.ShapeDtypeStruct((N_SHARD,) + b.shape, b.dtype) for b in bufs],
        in_specs=[_HBM] * n, out_specs=[_HBM] * n,
        scratch_shapes=[pltpu.SemaphoreType.DMA((6 * n,)), pltpu.SemaphoreType.DMA((6 * n,))],
    )(*bufs)


def _reduce_pair_exchange(gbufs, name):
    n = len(gbufs)
    halves = [g.shape[1] // 2 for g in gbufs]

    def body(*refs):
        ins, outs, send_sems, recv_sems = refs[:n], refs[n:2 * n], refs[2 * n], refs[2 * n + 1]
        x, y, c, _ = _place()
        cps = [_remote(ins[b].at[:, pl.ds(pl.multiple_of((1 - c) * halves[b], 16), halves[b]), :], outs[b],
                       send_sems, recv_sems, b, (x, y, 1 - c)) for b in range(n)]
        for cp in cps:
            cp.start()
        for cp in cps:
            cp.wait()

    return pl.pallas_call(
        body, name=name,
        out_shape=[jax.ShapeDtypeStruct((N_SHARD, h, g.shape[2]), g.dtype) for g, h in zip(gbufs, halves)],
        in_specs=[_HBM] * n, out_specs=[_HBM] * n,
        scratch_shapes=[pltpu.SemaphoreType.DMA((n,)), pltpu.SemaphoreType.DMA((n,))],
    )(*gbufs)


def _pair_add(gbuf, recv, c_arr, tr, name):
    _, rows, width = gbuf.shape
    half = rows // 2
    nt = half // tr

    def body(c_ref, a_ref, b_ref, o_ref):
        o_ref[...] = (a_ref[...] + b_ref[...]).astype(o_ref.dtype)

    blk = lambda f: pl.BlockSpec((None, tr, width), f)
    return pl.pallas_call(
        body, name=name, out_shape=jax.ShapeDtypeStruct((N_SHARD, half, width), jnp.bfloat16),
        grid_spec=pltpu.PrefetchScalarGridSpec(
            num_scalar_prefetch=1, grid=(N_SHARD, nt),
            in_specs=[blk(lambda s, i, c: (s, c[0] * nt + i, 0)), blk(lambda s, i, c: (s, i, 0))],
            out_specs=blk(lambda s, i, c: (s, i, 0))),
        compiler_params=_cparams(("parallel", "parallel")))(c_arr, gbuf, recv)


def _reduce_chip_exchange(parts, name):
    n = len(parts)

    def body(*refs):
        ins, outs, send_sems, recv_sems = refs[:n], refs[n:2 * n], refs[2 * n], refs[2 * n + 1]
        x, y, c, chips = _place()
        sends = [_remote(ins[b].at[2 * cx + cy], outs[b].at[j], send_sems, recv_sems, 3 * b + j, (cx, cy, c))
                 for b in range(n) for j, (cx, cy) in enumerate(chips)]
        for cp in sends:
            cp.start()
        for b in range(n):
            for j in range(len(chips)):
                _remote(ins[b].at[0], outs[b].at[j], send_sems, recv_sems, 3 * b + j, (x, y, c)).wait_recv()
        for cp in sends:
            cp.wait_send()

    return pl.pallas_call(
        body, name=name, out_shape=[jax.ShapeDtypeStruct((3,) + p.shape[1:], p.dtype) for p in parts],
        in_specs=[_HBM] * n, out_specs=[_HBM] * n,
        scratch_shapes=[pltpu.SemaphoreType.DMA((3 * n,)), pltpu.SemaphoreType.DMA((3 * n,))],
    )(*parts)


def _chip_add(part, recv, me_arr, tr, name):
    _, half, width = part.shape

    def body(me_ref, own, a0, a1, a2, o_ref):
        f = lambda r: r[...].astype(F32)
        o_ref[...] = ((f(own) + f(a0)) + f(a1)) + f(a2)

    specs = [pl.BlockSpec((None, tr, width), lambda i, me: (me[0], i, 0))]
    specs += [pl.BlockSpec((None, tr, width), functools.partial(lambda i, me, k: (k, i, 0), k=k)) for k in range(3)]
    return pl.pallas_call(
        body, name=name, out_shape=jax.ShapeDtypeStruct((half, width), F32),
        grid_spec=pltpu.PrefetchScalarGridSpec(
            num_scalar_prefetch=1, grid=(half // tr,), in_specs=specs,
            out_specs=pl.BlockSpec((tr, width), lambda i, me: (i, 0))),
        compiler_params=_cparams(("parallel",)))(me_arr, part, recv, recv, recv)


def _reduce_pair_share(rhalves, name):
    n = len(rhalves)

    def body(*refs):
        ins, outs, send_sems, recv_sems = refs[:n], refs[n:2 * n], refs[2 * n], refs[2 * n + 1]
        x, y, c, _ = _place()
        cps = [_remote(ins[b], outs[b], send_sems, recv_sems, b, (x, y, 1 - c)) for b in range(n)]
        for cp in cps:
            cp.start()
        for cp in cps:
            cp.wait()

    return pl.pallas_call(
        body, name=name, out_shape=[jax.ShapeDtypeStruct(r.shape, r.dtype) for r in rhalves],
        in_specs=[_HBM] * n, out_specs=[_HBM] * n,
        scratch_shapes=[pltpu.SemaphoreType.DMA((n,)), pltpu.SemaphoreType.DMA((n,))],
    )(*rhalves)


def _small_allreduce(buf):
    r, width = buf.shape
    n_dev = 8

    def body(x_ref, all_ref, sum_ref, send_sems, recv_sems, local_sem):
        x, y, c, chips = _place()
        me, sibling = (x, y, c), (x, y, 1 - c)

        def rows(px, py, pc):
            return all_ref.at[pl.ds(pl.multiple_of((4 * px + 2 * py + pc) * r, 8), r), :]

        def copy(k, block, to, src=None):
            return _remote(rows(*block) if src is None else src, rows(*block), send_sems, recv_sems, k, to)

        mine = pltpu.make_async_copy(x_ref, rows(*me), local_sem)
        mine.start()
        first = [copy(0, me, sibling, src=x_ref)]
        first += [copy(1 + j, me, (*chip, c), src=x_ref) for j, chip in enumerate(chips)]
        for cp in first:
            cp.start()
        passed = [copy(4 + j, (*chip, c), sibling) for j, chip in enumerate(chips)]
        for j, chip in enumerate(chips):
            copy(1 + j, (*chip, c), me).wait_recv()
            passed[j].start()
        copy(0, sibling, me).wait_recv()
        for j, chip in enumerate(chips):
            copy(4 + j, (*chip, 1 - c), me).wait_recv()
        for cp in first + passed:
            cp.wait_send()
        mine.wait()
        total = all_ref[0:r, :]
        for k in range(1, n_dev):
            total = total + all_ref[k * r:(k + 1) * r, :]
        sum_ref[...] = total

    vm = pl.BlockSpec(memory_space=pltpu.VMEM)
    _, total = pl.pallas_call(
        body, name="small_allreduce",
        out_shape=[jax.ShapeDtypeStruct((n_dev * r, width), buf.dtype), jax.ShapeDtypeStruct((r, width), buf.dtype)],
        in_specs=[vm], out_specs=[vm, vm],
        scratch_shapes=[pltpu.SemaphoreType.DMA((7,)), pltpu.SemaphoreType.DMA((7,)), pltpu.SemaphoreType.DMA],
    )(buf)
    return total


def _row_tile(rows, cap):
    if rows <= cap:
        return rows
    t = (cap // 8) * 8
    while t >= 8:
        if rows % t == 0:
            return t
        t -= 8
    return rows


def _adamw(w, g, m, v, name):
    shape = w.shape
    cols = shape[-1] if len(shape) <= 3 else shape[-2] * shape[-1]
    lead = len(shape) == 3
    w2, g2, m2, v2 = (a if lead else a.reshape(-1, cols) for a in (w, g, m, v))
    rows = shape[1] if lead else w2.shape[0]
    tr, tc = _row_tile(rows, 256), cols
    if tr == rows and rows > 256:
        tc = _tile(cols, 256)

    def body(w_ref, g_ref, m_ref, v_ref, d_ref, mo_ref, vo_ref):
        gv = g_ref[...]
        mn = ADAM_B1 * m_ref[...] + (1.0 - ADAM_B1) * gv
        vn = ADAM_B2 * v_ref[...] + (1.0 - ADAM_B2) * (gv * gv)
        m_hat = mn / (1.0 - ADAM_B1 ** ADAM_STEP)
        v_hat = vn / (1.0 - ADAM_B2 ** ADAM_STEP)
        d_ref[...] = -ADAM_LR * (m_hat / (jnp.sqrt(v_hat) + ADAM_EPS) + ADAM_WD * w_ref[...])
        mo_ref[...] = mn
        vo_ref[...] = vn

    blk = (pl.BlockSpec((None, tr, tc), lambda i, j: (0, i, j)) if lead else pl.BlockSpec((tr, tc), lambda i, j: (i, j)))
    outs = pl.pallas_call(
        body, name=name, grid=(rows // tr, cols // tc), in_specs=[blk] * 4, out_specs=[blk] * 3,
        out_shape=[jax.ShapeDtypeStruct(w2.shape, F32)] * 3,
        compiler_params=_cparams(("parallel", "parallel")))(w2, g2, m2, v2)
    return tuple(o.reshape(shape) for o in outs)


_WEIGHT_NAMES = ('w_in', 'conv_w', 'dn_a_log', 'dn_dt_bias', 'dn_norm_w', 'q_norm_w', 'w_uq', 'kv_norm_w', 'w_uk',
                 'w_uv', 'w_br_dn', 'w_br_mla', 'w_o', 'ln1_g', 'ln1_b', 'w_ffn_in', 'w_ffn_out', 'w_ple',
                 'w_ple_gate', 'ln2_g', 'ln2_b')
_SMALL_NAMES = ('ln1_g', 'ln1_b', 'ln2_g', 'ln2_b', 'q_norm_w', 'kv_norm_w', 'dn_norm_w', 'dn_a_log', 'dn_dt_bias')
_SMALL_GROUP = 8
_CONV_SMALL_ROW = len(_SMALL_NAMES) * _SMALL_GROUP
_CONV_SMALL_ROWS = DN_CONV * QKV_W // FLAT_W


def _pack_small(gw):
    rows = [jnp.pad(gw[n][None, :], ((0, _SMALL_GROUP - 1), (0, FLAT_W - gw[n].shape[0]))) for n in _SMALL_NAMES]
    rows.append(jnp.pad(gw['conv_w'].reshape(_CONV_SMALL_ROWS, FLAT_W), ((0, SMALL_ROWS - _CONV_SMALL_ROW - _CONV_SMALL_ROWS), (0, 0))))
    return jnp.concatenate(rows, axis=0)


class _Exchange:
    def __init__(self, local, me_chip, c_arr):
        self.local, self.me_chip, self.c_arr = local, me_chip, c_arr
        self.parts = self.arrived = None

    def gather_send(self):
        return _ride_gather_send(self.local)

    def gather_pass(self, sent):
        return _ride_gather_pass(sent)

    def weights(self, gathered):
        return _prep_weights(_unpack_rest(gathered, self.local, self.me_chip))

    def pair_send(self, g):
        self.gbufs = _pack_grads_rest(_unprep_grads_early(g))
        return _ride_pair_exchange(self.gbufs)

    def reduce_send(self, got):
        self.parts = [_pair_add(g_, r_, self.c_arr, tr, "pair_add_%d" % (i + 1))
                      for i, (g_, r_, tr) in enumerate(zip(self.gbufs, got, _ADD_TILES[1:]))]
        return _ride_chip_exchange(self.parts)

    def reduce_arrived(self, arrived):
        self.arrived = list(arrived)

    def in_send(self, g):
        g_in = [_shard_columns(_w_in_grad(g), W_IN_SHARD)]
        got = _reduce_pair_exchange(g_in, "reduce_pair_exchange_w_in")
        self.part_in = _pair_add(g_in[0], got[0], self.c_arr, _ADD_TILES[0], "pair_add_0")
        return _ride_chip_exchange([self.part_in])

    def in_arrived(self, arrived):
        self.arrived_in = list(arrived)


def kernel(x, p, positions, w_in, conv_w, dn_a_log, dn_dt_bias, dn_norm_w, q_norm_w, w_uq, kv_norm_w, w_uk, w_uv, w_br_dn, w_br_mla, w_o, ln1_g, ln1_b, w_ffn_in, w_ffn_out, w_ple, w_ple_gate, ln2_g, ln2_b, loss_target, m_w_in, m_conv_w, m_dn_a_log, m_dn_dt_bias, m_dn_norm_w, m_q_norm_w, m_w_uq, m_kv_norm_w, m_w_uk, m_w_uv, m_w_br_dn, m_w_br_mla, m_w_o, m_ln1_g, m_ln1_b, m_w_ffn_in, m_w_ffn_out, m_w_ple, m_w_ple_gate, m_ln2_g, m_ln2_b, v_w_in, v_conv_w, v_dn_a_log, v_dn_dt_bias, v_dn_norm_w, v_q_norm_w, v_w_uq, v_kv_norm_w, v_w_uk, v_w_uv, v_w_br_dn, v_w_br_mla, v_w_o, v_ln1_g, v_ln1_b, v_w_ffn_in, v_w_ffn_out, v_w_ple, v_w_ple_gate, v_ln2_g, v_ln2_b):
    ws = dict(w_in=w_in, conv_w=conv_w, dn_a_log=dn_a_log, dn_dt_bias=dn_dt_bias, dn_norm_w=dn_norm_w, q_norm_w=q_norm_w,
              w_uq=w_uq, kv_norm_w=kv_norm_w, w_uk=w_uk, w_uv=w_uv, w_br_dn=w_br_dn, w_br_mla=w_br_mla, w_o=w_o,
              ln1_g=ln1_g, ln1_b=ln1_b, w_ffn_in=w_ffn_in, w_ffn_out=w_ffn_out, w_ple=w_ple, w_ple_gate=w_ple_gate,
              ln2_g=ln2_g, ln2_b=ln2_b)
    ms = dict(w_in=m_w_in, conv_w=m_conv_w, dn_a_log=m_dn_a_log, dn_dt_bias=m_dn_dt_bias, dn_norm_w=m_dn_norm_w,
              q_norm_w=m_q_norm_w, w_uq=m_w_uq, kv_norm_w=m_kv_norm_w, w_uk=m_w_uk, w_uv=m_w_uv, w_br_dn=m_w_br_dn,
              w_br_mla=m_w_br_mla, w_o=m_w_o, ln1_g=m_ln1_g, ln1_b=m_ln1_b, w_ffn_in=m_w_ffn_in, w_ffn_out=m_w_ffn_out,
              w_ple=m_w_ple, w_ple_gate=m_w_ple_gate, ln2_g=m_ln2_g, ln2_b=m_ln2_b)
    vs = dict(w_in=v_w_in, conv_w=v_conv_w, dn_a_log=v_dn_a_log, dn_dt_bias=v_dn_dt_bias, dn_norm_w=v_dn_norm_w,
              q_norm_w=v_q_norm_w, w_uq=v_w_uq, kv_norm_w=v_kv_norm_w, w_uk=v_w_uk, w_uv=v_w_uv, w_br_dn=v_w_br_dn,
              w_br_mla=v_w_br_mla, w_o=v_w_o, ln1_g=v_ln1_g, ln1_b=v_ln1_b, w_ffn_in=v_w_ffn_in, w_ffn_out=v_w_ffn_out,
              w_ple=v_w_ple, w_ple_gate=v_w_ple_gate, ln2_g=v_ln2_g, ln2_b=v_ln2_b)
    mx, my, mc = lax.axis_index("x"), lax.axis_index("y"), lax.axis_index("c")

    me_chip = 2 * mx + my
    c_arr = jnp.reshape(mc, (1,)).astype(jnp.int32)
    me_arr = jnp.reshape(me_chip, (1,)).astype(jnp.int32)
    sharded = ('w_in', 'w_ffn_in') + tuple(name for name, _, _ in _FLATB_PIECES)

    local = _pack_shards({name: ws[name][0] for name in sharded}, conv_w[0])
    (gathered_in,) = _gather_shards(local[:1], "gather_w_in")
    w_in_full, conv_full = _unpack_w_in(gathered_in, local[0], me_chip)
    small = {n: ws[n][0] for n in _SMALL_NAMES}
    small['conv_w'] = conv_full
    sp = _prep_small(small)
    cosb, sinb = _rope_tables(positions[0])
    exch = _Exchange(local[1:], me_chip, c_arr)

    loss_lanes, dx, g = _local_step(x[0], p[0, 0], cosb, sinb, loss_target[0], _prep_w_in(w_in_full), sp, exch)
    gw = _unprep_grads_late(g)
    loss = lax.psum(jnp.sum(loss_lanes), ("x", "y", "c"))

    parts = [exch.part_in] + exch.parts
    arrived = exch.arrived_in + exch.arrived
    mine = [_chip_add(p_, r_, me_arr, tr, "chip_add_%d" % i) for i, (p_, r_, tr) in enumerate(zip(parts, arrived, _ADD_TILES))]
    reduced = _unpack_reduced(mine, _reduce_pair_share(mine, "reduce_pair_share"), mc)
    tot = _small_allreduce(_pack_small(gw))
    gred = {name: reduced[name][None] for name in sharded}
    for i, n in enumerate(_SMALL_NAMES):
        gred[n] = tot[i * _SMALL_GROUP, :ws[n].shape[1]][None]
    conv_tot = tot[_CONV_SMALL_ROW:_CONV_SMALL_ROW + _CONV_SMALL_ROWS].reshape(DN_CONV, QKV_W)
    gred['conv_w'] = lax.dynamic_slice_in_dim(conv_tot, (2 * mx + my) * _CONV_SHARD, _CONV_SHARD, axis=1)[None]

    deltas, new_m, new_v = {}, {}, {}
    for n in _WEIGHT_NAMES:
        if n == 'w_in':
            tr_ = lambda a: jnp.transpose(a, (0, 2, 1))
            g_t = tr_(gred[n].reshape(ws[n].shape))
            outs = _adamw(tr_(ws[n]), g_t, tr_(ms[n]), tr_(vs[n]), "adamw_" + n)
            gred[n] = tr_(g_t)
            deltas[n], new_m[n], new_v[n] = (tr_(o) for o in outs)
            continue
        gred[n] = gred[n].reshape(ws[n].shape)
        deltas[n], new_m[n], new_v[n] = _adamw(ws[n], gred[n], ms[n], vs[n], "adamw_" + n)
    return (loss, dx[None], *[gred[n] for n in _WEIGHT_NAMES], *[deltas[n] for n in _WEIGHT_NAMES],
            *[new_m[n] for n in _WEIGHT_NAMES], *[new_v[n] for n in _WEIGHT_NAMES])
```

```python
import functools
import math

import jax
import jax.numpy as jnp
from jax import lax
from jax.experimental import pallas as pl
from jax.experimental.pallas import tpu as pltpu

F32 = jnp.float32
_CDT = jnp.bfloat16
_HI = lax.Precision.HIGHEST
_MESH = pl.DeviceIdType.MESH

D_MODEL = 1024
PLE_DIM = 256
HEADS = 8
DN_DK = 128
DN_CHUNK = 64
DN_CONV = 4
QKV_W = 3 * HEADS * DN_DK
Q_LORA = 384
KV_LORA = 256
NOPE = 128
ROPE = 64
ROPE_PAD = 128
FFN_HIDDEN = 2816
D_IN = 6864
ROPE_BASE = 10000.0
ALPHA = 2.0 ** 0.25
ATT_SCALE = (NOPE + ROPE) ** -0.5
NEG_BIG = -1e30
ADAM_LR, ADAM_B1, ADAM_B2, ADAM_EPS, ADAM_WD, ADAM_STEP = 0.001, 0.9, 0.999, 1e-08, 0.01, 10

LANE = 128
VMEM_LIMIT = 56 * 1024 * 1024
MM_VMEM_BUDGET = 40 * 1024 * 1024
N_SHARD = 4
FLAT_W = 1024
SMALL_ROWS = 88


def _tile(dim, cap):
    if dim <= cap:
        return dim
    t = (cap // LANE) * LANE
    while t >= LANE:
        if dim % t == 0:
            return t
        t -= LANE
    return dim


def _cparams(sem):
    return pltpu.CompilerParams(dimension_semantics=sem, vmem_limit_bytes=VMEM_LIMIT)


def _mm(a, b, *, name, ta=False, tb=False, add=None, add_scale=1.0, out_dtype=F32, heads=None,
        a_head=None, b_head=None, out_head=None, dims=None, tm=1408, tn=1408):
    m, n, k = dims
    tm, tn = _tile(m, tm), _tile(n, tn)
    sa, sb, so = a.dtype.itemsize, b.dtype.itemsize, jnp.dtype(out_dtype).itemsize

    def vmem_need(tk_):
        acc = tm * tn * 4 if tk_ < k else 0
        extra = 2 * tm * tn * 4 if add is not None else 0
        return 2 * (tm * tk_ * sa + tk_ * tn * sb) + 2 * tm * tn * so + acc + extra

    tk = k
    while vmem_need(tk) > MM_VMEM_BUDGET and tk > LANE:
        smaller = _tile(k, tk - LANE)
        if smaller >= tk:
            break
        tk = smaller
    nk = k // tk
    hgrid = () if heads is None else (heads,)
    off = len(hgrid)

    def spec(rows, cols, rtile, ctile, rsel, csel, layout):
        def idx(*g):
            h = g[0] if off else 0
            ri, ci = g[off + rsel], g[off + csel]
            if layout == 'lead':
                return (h, ri, ci)
            if layout == 'col':
                return (ri, h * (cols // ctile) + ci)
            return (ri, ci)
        if layout == 'lead':
            return pl.BlockSpec((None, rtile, ctile), idx)
        return pl.BlockSpec((rtile, ctile), idx)

    a_spec = spec(k, m, tk, tm, 2, 0, a_head) if ta else spec(m, k, tm, tk, 0, 2, a_head)
    b_spec = spec(n, k, tn, tk, 1, 2, b_head) if tb else spec(k, n, tk, tn, 2, 1, b_head)
    o_spec = spec(m, n, tm, tn, 0, 1, out_head)
    in_specs = [a_spec, b_spec]
    args = [a, b]
    if add is not None:
        in_specs.append(spec(m, n, tm, tn, 0, 1, out_head))
        args.append(add)
    dn = (((0 if ta else 1,), (1 if tb else 0,)), ((), ()))

    def body(*refs):
        a_ref, b_ref = refs[0], refs[1]
        prod = lax.dot_general(a_ref[...].astype(_CDT), b_ref[...].astype(_CDT), dn, preferred_element_type=F32)
        if nk == 1:
            o_ref = refs[-1]
            if add is not None:
                prod = prod + refs[2][...].astype(F32) * add_scale
            o_ref[...] = prod.astype(out_dtype)
            return
        o_ref, acc_ref = refs[-2], refs[-1]
        kk = pl.program_id(off + 2)

        @pl.when(kk == 0)
        def _():
            if add is not None:
                acc_ref[...] = refs[2][...].astype(F32) * add_scale
            else:
                acc_ref[...] = jnp.zeros_like(acc_ref)

        acc_ref[...] += prod

        @pl.when(kk == nk - 1)
        def _():
            o_ref[...] = acc_ref[...].astype(out_dtype)

    if out_head == 'lead':
        oshape = (heads, m, n)
    elif out_head == 'col':
        oshape = (m, heads * n)
    else:
        oshape = (m, n)
    sem = ("parallel",) * (off + 2) + ("arbitrary",)
    return pl.pallas_call(
        body, name=name, grid=hgrid + (m // tm, n // tn, nk), in_specs=in_specs, out_specs=o_spec,
        out_shape=jax.ShapeDtypeStruct(oshape, out_dtype),
        scratch_shapes=[pltpu.VMEM((tm, tn), F32)] if nk > 1 else [],
        compiler_params=_cparams(sem))(*args)


def _mm2(a, b, **kw):
    ta, tb = kw.get('ta', False), kw.get('tb', False)
    m = a.shape[1] if ta else a.shape[0]
    k = a.shape[0] if ta else a.shape[1]
    n = b.shape[0] if tb else b.shape[1]
    return _mm(a, b, dims=(m, n, k), **kw)


def _rowwise(fn, rows, bcast, outs, reds=(), *, name, tm=256, heads=None):
    t = rows[0][0].shape[0]
    tm = min(tm, t)
    hn = 1 if heads is None else heads
    in_specs, args = [], []
    for arr, width, base, per_head in rows:
        in_specs.append(pl.BlockSpec((tm, width), functools.partial(
            lambda i, h, base, per_head: (i, base + (h if per_head else 0)), base=base, per_head=per_head)))
        args.append(arr)
    for arr in bcast:
        in_specs.append(pl.BlockSpec(arr.shape, lambda i, h: (0, 0)))
        args.append(arr)
    out_specs, out_shape = [], []
    for total, width, per_head, dt in outs:
        out_specs.append(pl.BlockSpec((tm, width), functools.partial(
            lambda i, h, per_head: (i, h if per_head else 0), per_head=per_head)))
        out_shape.append(jax.ShapeDtypeStruct((t, total), dt))
    for shp in reds:
        out_specs.append(pl.BlockSpec(shp, lambda i, h: (0, 0)))
        out_shape.append(jax.ShapeDtypeStruct(shp, F32))
    n_in, n_out, n_red = len(args), len(outs), len(reds)

    def body(*refs):
        i, h = pl.program_id(0), pl.program_id(1)
        vals = fn(h, *[r[...] for r in refs[:n_in]])
        for r, v in zip(refs[n_in:n_in + n_out], vals[:n_out]):
            r[...] = v.astype(r.dtype)
        if n_red:
            @pl.when((i == 0) & (h == 0))
            def _():
                for r in refs[n_in + n_out:]:
                    r[...] = jnp.zeros_like(r)
            for r, v in zip(refs[n_in + n_out:], vals[n_out:]):
                r[...] += v

    sem = ("arbitrary", "arbitrary") if n_red else ("parallel", "parallel")
    res = pl.pallas_call(body, name=name, grid=(t // tm, hn), in_specs=in_specs, out_specs=out_specs,
                         out_shape=out_shape, compiler_params=_cparams(sem))(*args)
    return tuple(res)


def _sigmoid(x):
    return 1.0 / (1.0 + jnp.exp(-x))


def _silu(x):
    return x * _sigmoid(x)


def _softplus(x):
    return jnp.maximum(x, 0.0) + jnp.log(1.0 + jnp.exp(-jnp.abs(x)))


def _layer_norm(t, g, b):
    mu = jnp.mean(t, axis=-1, keepdims=True)
    d = t - mu
    var = jnp.mean(d * d, axis=-1, keepdims=True)
    return d * lax.rsqrt(var + 1e-5) * g + b


def _rms_norm(t, w):
    return t * lax.rsqrt(jnp.mean(t * t, axis=-1, keepdims=True) + 1e-6) * w


def _swap_rope_halves(t):
    lane = lax.broadcasted_iota(jnp.int32, t.shape, 1) % ROPE_PAD
    n = t.shape[1]
    up = pltpu.roll(t, n - ROPE // 2, axis=1)
    dn = pltpu.roll(t, ROPE // 2, axis=1)
    return jnp.where(lane < ROPE // 2, up, jnp.where(lane < ROPE, dn, 0.0))


def _rope(t, cosb, sinb):
    reps = t.shape[1] // ROPE_PAD
    c = jnp.tile(cosb, (1, reps)) if reps > 1 else cosb
    s = jnp.tile(sinb, (1, reps)) if reps > 1 else sinb
    return t * c + _swap_rope_halves(t) * s


def _rope_bwd(d, cosb, sinb):
    reps = d.shape[1] // ROPE_PAD
    c = jnp.tile(cosb, (1, reps)) if reps > 1 else cosb
    s = jnp.tile(sinb, (1, reps)) if reps > 1 else sinb
    return d * c + _swap_rope_halves(d * s)


_CONV_ROWS = 256
_CONV_COLS = 256


def _conv_window(ref, r0, lo, hi, t):
    parts = []
    start, stop = r0 - lo, r0 + _CONV_ROWS + hi
    if start < 0:
        parts.append(jnp.zeros((-start, ref.shape[1]), F32))
        start = 0
    tail = max(stop - t, 0)
    parts.append(ref[start:stop - tail, :].astype(F32))
    if tail:
        parts.append(jnp.zeros((tail, ref.shape[1]), F32))
    return parts[0] if len(parts) == 1 else jnp.concatenate(parts, axis=0)


def _conv_taps(win, w_ref, n_out):
    acc = win[8:8 + n_out] * w_ref[DN_CONV - 1:DN_CONV, :]
    for i in range(DN_CONV - 1):
        acc = acc + pltpu.roll(win, DN_CONV - 1 - i, axis=0)[8:8 + n_out] * w_ref[i:i + 1, :]
    return acc


def _conv_silu(x, w):
    t, ch = x.shape

    def body(x_ref, w_ref, o_ref):
        for r in range(t // _CONV_ROWS):
            r0 = r * _CONV_ROWS
            c = _conv_taps(_conv_window(x_ref, r0, 8, 0, t), w_ref, _CONV_ROWS)
            o_ref[r0:r0 + _CONV_ROWS, :] = _silu(c)

    return pl.pallas_call(
        body, name="conv_silu", grid=(ch // _CONV_COLS,),
        in_specs=[pl.BlockSpec((t, _CONV_COLS), lambda j: (0, j)), pl.BlockSpec((DN_CONV, _CONV_COLS), lambda j: (0, j))],
        out_specs=pl.BlockSpec((t, _CONV_COLS), lambda j: (0, j)),
        out_shape=jax.ShapeDtypeStruct((t, ch), F32), compiler_params=_cparams(("parallel",)))(x, w)


def _conv_silu_bwd(x, w, dys):
    t, ch = x.shape
    per = ch // len(dys) // _CONV_COLS

    def body(x_ref, w_ref, *rest):
        dy_refs, (dx_ref, dw_ref) = rest[:len(dys)], rest[len(dys):]
        sec = pl.program_id(0) // per
        dws = [jnp.zeros((1, _CONV_COLS), F32) for _ in range(DN_CONV)]
        for r in range(t // _CONV_ROWS):
            r0 = r * _CONV_ROWS
            n_ext = _CONV_ROWS + 8
            xw = _conv_window(x_ref, r0, 8, 8, t)
            c = _conv_taps(xw, w_ref, n_ext)
            sg = _sigmoid(c)
            dy = _conv_window(dy_refs[-1], r0, 0, 8, t)
            for k in range(len(dys) - 2, -1, -1):
                dy = jnp.where(sec == k, _conv_window(dy_refs[k], r0, 0, 8, t), dy)
            ds = dy * (sg * (1.0 + c * (1.0 - sg)))
            dx = ds[:_CONV_ROWS] * w_ref[DN_CONV - 1:DN_CONV, :]
            for i in range(DN_CONV - 1):
                sh = DN_CONV - 1 - i
                dx = dx + pltpu.roll(ds, n_ext - sh, axis=0)[:_CONV_ROWS] * w_ref[i:i + 1, :]
            dx_ref[r0:r0 + _CONV_ROWS, :] = dx.astype(dx_ref.dtype)
            ds0 = ds[:_CONV_ROWS]
            for i in range(DN_CONV):
                sh = DN_CONV - 1 - i
                xs = xw if sh == 0 else pltpu.roll(xw, sh, axis=0)
                dws[i] = dws[i] + jnp.sum(ds0 * xs[8:8 + _CONV_ROWS], axis=0, keepdims=True)
        for i in range(DN_CONV):
            dw_ref[i:i + 1, :] = dws[i]

    blk = pl.BlockSpec((t, _CONV_COLS), lambda j: (0, j))
    wblk = pl.BlockSpec((DN_CONV, _CONV_COLS), lambda j: (0, j))
    dy_specs = [pl.BlockSpec((t, _CONV_COLS), functools.partial(lambda j, k: (0, jnp.clip(j - k * per, 0, per - 1)), k=k))
                for k in range(len(dys))]
    return pl.pallas_call(
        body, name="conv_silu_bwd", grid=(ch // _CONV_COLS,), in_specs=[blk, wblk] + dy_specs, out_specs=[blk, wblk],
        out_shape=[jax.ShapeDtypeStruct((t, ch), _CDT), jax.ShapeDtypeStruct((DN_CONV, ch), F32)],
        compiler_params=_cparams(("arbitrary",)))(x, w, *dys)


_PA_ROWS = 512


def _bmm(a, b, spec, exact=False):
    if exact:
        return jnp.einsum(spec, a, b, precision=_HI, preferred_element_type=F32)
    return jnp.einsum(spec, a.astype(_CDT), b.astype(_CDT), preferred_element_type=F32)


def _split16(a):
    hi = a.astype(jnp.bfloat16)
    return hi, (a - hi.astype(F32)).astype(jnp.bfloat16)


def _bmm3(a, b, spec):
    ah, al = _split16(a)
    bh, bl = _split16(b)
    e = lambda p, q: jnp.einsum(spec, p, q, preferred_element_type=F32)
    return e(ah, bh) + (e(ah, bl) + e(al, bh))


def _split3(b):
    b0 = b.astype(jnp.bfloat16)
    r1 = b - b0.astype(F32)
    b1 = r1.astype(jnp.bfloat16)
    return b0, b1, (r1 - b1.astype(F32)).astype(jnp.bfloat16)


@functools.partial(jax.custom_vjp, nondiff_argnums=(2, 3))
def _select_mm(sel, b, spec, spec_t):
    return sum(jnp.einsum(spec, sel, t, preferred_element_type=F32) for t in _split3(b))


def _select_mm_fwd(sel, b, spec, spec_t):
    return _select_mm(sel, b, spec, spec_t), sel


def _select_mm_bwd(spec, spec_t, sel, ct):
    return jnp.zeros_like(sel), sum(jnp.einsum(spec_t, sel, t, preferred_element_type=F32) for t in _split3(ct))


_select_mm.defvjp(_select_mm_fwd, _select_mm_bwd)


def _tri_inverse(l_mat, eye):
    pw = -l_mat
    t_inv = eye + pw
    for _ in range(5):
        pw = _bmm3(pw, pw, 'bij,bjk->bik')
        t_inv = t_inv + _bmm3(t_inv, pw, 'bij,bjk->bik')
    return t_inv


@jax.custom_vjp
def _tri_inverse_saved(l_mat, t_saved):
    return t_saved


def _tri_inverse_saved_fwd(l_mat, t_saved):
    return t_saved, t_saved


def _tri_inverse_saved_bwd(t_saved, dt):
    left = _bmm3(t_saved, dt, 'bji,bjk->bik')
    return -_bmm3(left, t_saved, 'bij,bkj->bik'), jnp.zeros_like(t_saved)


_tri_inverse_saved.defvjp(_tri_inverse_saved_fwd, _tri_inverse_saved_bwd)


def _phase_a(h, q, k, v, ba, alog, dtb, t_saved=None):
    r = q.shape[0]
    nb = r // DN_CHUNK
    c = DN_CHUNK
    lane = lax.broadcasted_iota(jnp.int32, (1, LANE), 1)
    selb = (lane == h).astype(F32)
    sela = (lane == h + HEADS).astype(F32)
    b_raw = jnp.sum(ba * selb, axis=1, keepdims=True)
    a_raw = jnp.sum(ba * sela, axis=1, keepdims=True)
    al = jnp.sum(alog * selb, axis=1, keepdims=True)
    dt = jnp.sum(dtb * selb, axis=1, keepdims=True)
    beta = jnp.broadcast_to(_sigmoid(b_raw), (r, LANE))
    g = jnp.broadcast_to(-jnp.exp(al) * _softplus(a_raw + dt), (r, LANE))
    qn = q * lax.rsqrt(jnp.sum(q * q, -1, keepdims=True) + 1e-6) * (DN_DK ** -0.5)
    kn = k * lax.rsqrt(jnp.sum(k * k, -1, keepdims=True) + 1e-6)
    q3, k3, v3 = qn.reshape(nb, c, LANE), kn.reshape(nb, c, LANE), v.reshape(nb, c, LANE)
    b3, g3 = beta.reshape(nb, c, LANE), g.reshape(nb, c, LANE)
    ri = lax.broadcasted_iota(jnp.int32, (nb, c, c), 1)
    ci = lax.broadcasted_iota(jnp.int32, (nb, c, c), 2)
    tril, strict = ri >= ci, ri > ci
    gc = _select_mm(tril.astype(jnp.bfloat16), g3, 'bij,bjd->bid', 'bij,bid->bjd')
    onehot = (lax.broadcasted_iota(jnp.int32, (nb, c, LANE), 2) == 0).astype(jnp.bfloat16)
    g_row = _select_mm(onehot, gc, 'bid,bjd->bij', 'bid,bij->bjd')
    diff = gc[:, :, :c] - g_row
    decay = jnp.where(tril, jnp.exp(jnp.where(tril, diff, 0.0)), 0.0)
    kb = k3 * b3
    l_mat = jnp.where(strict, _bmm(kb, k3, 'bid,bjd->bij') * decay, 0.0)
    if t_saved is None:
        t_inv = _tri_inverse(l_mat, (ri == ci).astype(F32))
    else:
        t_inv = _tri_inverse_saved(l_mat, t_saved.reshape(nb, c, c))
    eg = jnp.exp(gc)
    u = _bmm(t_inv, v3 * b3, 'bij,bje->bie')
    w = _bmm(t_inv, kb * eg, 'bij,bje->bie')
    intra = jnp.where(tril, _bmm(q3, k3, 'bid,bjd->bij') * decay, 0.0)
    qd = q3 * eg
    gl = jnp.sum(g3, axis=1, keepdims=True)
    kt = k3 * jnp.exp(gl - gc)
    outs = (u.reshape(r, LANE), w.reshape(r, LANE), qd.reshape(r, LANE), kt.reshape(r, LANE),
            intra.reshape(r, c), gl.reshape(nb, LANE))
    return outs + (t_inv.reshape(r, c),) if t_saved is None else outs


def _pa_specs(t):
    rr = min(_PA_ROWS, t)
    nb = rr // DN_CHUNK
    qkv = [pl.BlockSpec((rr, LANE), functools.partial(lambda i, h, o: (i, o + h), o=o)) for o in (0, HEADS, 2 * HEADS)]
    ba = pl.BlockSpec((rr, LANE), lambda i, h: (i, 3))
    vec = pl.BlockSpec((1, LANE), lambda i, h: (0, 0))
    row = pl.BlockSpec((rr, LANE), lambda i, h: (i, h))
    intra = pl.BlockSpec((None, rr, DN_CHUNK), lambda i, h: (h, i, 0))
    gl = pl.BlockSpec((nb, LANE), lambda i, h: (i, h))
    return rr, qkv, ba, vec, row, intra, gl


def _grid_ends(grid):
    first = lambda: functools.reduce(jnp.logical_and, [pl.program_id(a) == 0 for a in range(len(grid))])
    last = lambda: functools.reduce(jnp.logical_and, [pl.program_id(a) == n - 1 for a, n in enumerate(grid)])
    return first, last


def _delta_local(qkv_act, pm, alog, dtb, rider=None):
    t = qkv_act.shape[0]
    rr, qkv, ba, vec, row, intra, gl = _pa_specs(t)

    def body(q, k, v, b, al, dt, *outs):
        vals = _phase_a(pl.program_id(1), q[...], k[...], v[...], b[...], al[...], dt[...])
        for o, val in zip(outs, vals):
            o[...] = val

    wide = jax.ShapeDtypeStruct((t, HEADS * LANE), F32)
    sq = jax.ShapeDtypeStruct((HEADS, t, DN_CHUNK), F32)
    grid = (t // rr, HEADS)
    return _carried_call(
        body, rider, *_grid_ends(grid), name="delta_local", grid=grid, in_specs=qkv + [ba, vec, vec],
        out_specs=[row] * 4 + [intra, gl, intra],
        out_shape=[wide] * 4 + [sq, jax.ShapeDtypeStruct((t // DN_CHUNK, HEADS * LANE), F32), sq],
        scratch_shapes=[], sem=("arbitrary", "arbitrary"), args=(qkv_act, qkv_act, qkv_act, pm, alog, dtb))


def _delta_local_bwd(qkv_act, pm, alog, dtb, t_inv, du, dw, dqd, dkt, dintra, dgl, rider=None):
    t = qkv_act.shape[0]
    rr, qkv, ba, vec, row, intra, gl = _pa_specs(t)

    def body(q, k, v, b, al, dt, ti, du_r, dw_r, dqd_r, dkt_r, di_r, dgl_r, dq_o, dk_o, dv_o, dba_o, dal_o, ddt_o):
        i, h = pl.program_id(0), pl.program_id(1)
        t_saved = ti[...]
        _, vjp = jax.vjp(lambda *a: _phase_a(h, *a, t_saved=t_saved), q[...], k[...], v[...], b[...], al[...], dt[...])
        dq, dk, dv, dba, dal, ddt = vjp((du_r[...], dw_r[...], dqd_r[...], dkt_r[...], di_r[...], dgl_r[...]))
        dq_o[...], dk_o[...], dv_o[...] = dq, dk, dv

        @pl.when(h == 0)
        def _():
            dba_o[...] = jnp.zeros_like(dba_o)

        @pl.when((h == 0) & (i == 0))
        def _():
            dal_o[...] = jnp.zeros_like(dal_o)
            ddt_o[...] = jnp.zeros_like(ddt_o)

        dba_o[...] += dba
        dal_o[...] += dal
        ddt_o[...] += ddt

    wide = jax.ShapeDtypeStruct((t, HEADS * LANE), F32)
    vshape = jax.ShapeDtypeStruct((1, LANE), F32)
    grid = (t // rr, HEADS)
    return _carried_call(
        body, rider, *_grid_ends(grid), name="delta_local_bwd", grid=grid,
        in_specs=qkv + [ba, vec, vec, intra] + [row] * 4 + [intra, gl],
        out_specs=[row] * 3 + [pl.BlockSpec((rr, LANE), lambda i, h: (i, 0)), vec, vec],
        out_shape=[wide] * 3 + [jax.ShapeDtypeStruct((t, LANE), F32), vshape, vshape],
        scratch_shapes=[], sem=("arbitrary", "arbitrary"),
        args=(qkv_act, qkv_act, qkv_act, pm, alog, dtb, t_inv, du, dw, dqd, dkt, dintra, dgl))


_SCAN_ROWS = 512


def _dot(a, b, dn):
    return lax.dot_general(a.astype(_CDT), b.astype(_CDT), (dn, ((), ())), preferred_element_type=F32)


_NN = ((1,), (0,))
_NT = ((1,), (1,))
_TN = ((0,), (0,))


def _delta_scan(u, w, qd, kt, intra, gl, rider=None):
    t = u.shape[0]
    rr = min(_SCAN_ROWS, t)
    nc = rr // DN_CHUNK

    def body(u_ref, w_ref, qd_ref, kt_ref, a_ref, gl_ref, o_ref, sall_ref, s_scr):
        @pl.when(pl.program_id(0) == 0)
        def _():
            s_scr[...] = jnp.zeros_like(s_scr)

        def chunk(c, carry):
            r0 = pl.multiple_of(c * DN_CHUNK, DN_CHUNK)
            rows = pl.ds(r0, DN_CHUNK)
            e = jnp.exp(gl_ref[pl.ds(c, 1), :])
            states = [s_scr[h] for h in range(HEADS)]
            u_c, w_c, qd_c, kt_c = u_ref[rows, :], w_ref[rows, :], qd_ref[rows, :], kt_ref[rows, :]
            a_c = [a_ref[h, rows, :] for h in range(HEADS)]
            o_new, s_new = [], []
            for h in range(HEADS):
                cs = slice(h * LANE, (h + 1) * LANE)
                s = states[h]
                v_new = u_c[:, cs] - _dot(w_c[:, cs], s, _NN)
                o_new.append(_dot(qd_c[:, cs], s, _NN) + _dot(a_c[h], v_new, _NN))
                s_new.append(s * e[:, cs] + _dot(kt_c[:, cs], v_new, _TN))
            o_ref[rows, :] = jnp.concatenate(o_new, axis=1)
            for h in range(HEADS):
                sall_ref[c, h] = states[h]
                s_scr[h] = s_new[h]
            return carry

        lax.fori_loop(0, nc, chunk, 0)

    row = pl.BlockSpec((rr, HEADS * LANE), lambda i: (i, 0))
    grid = (t // rr,)
    return _carried_call(
        body, rider, *_grid_ends(grid), name="delta_scan", grid=grid,
        in_specs=[row] * 4 + [pl.BlockSpec((HEADS, rr, DN_CHUNK), lambda i: (0, i, 0)),
                              pl.BlockSpec((nc, HEADS * LANE), lambda i: (i, 0))],
        out_specs=[row, pl.BlockSpec((nc, HEADS, LANE, LANE), lambda i: (i, 0, 0, 0))],
        out_shape=[jax.ShapeDtypeStruct((t, HEADS * LANE), F32),
                   jax.ShapeDtypeStruct((t // DN_CHUNK, HEADS, LANE, LANE), F32)],
        scratch_shapes=[pltpu.VMEM((HEADS, LANE, LANE), F32)], sem=("arbitrary",), args=(u, w, qd, kt, intra, gl))


def _delta_scan_bwd(u, w, qd, kt, intra, gl, sall, do, rider=None):
    t = u.shape[0]
    rr = min(_SCAN_ROWS, t)
    nc = rr // DN_CHUNK
    ng = t // rr

    def body(u_ref, w_ref, qd_ref, kt_ref, a_ref, gl_ref, sall_ref, do_ref,
             du_ref, dw_ref, dqd_ref, dkt_ref, da_ref, dgl_ref, ds_scr):
        @pl.when(pl.program_id(0) == 0)
        def _():
            ds_scr[...] = jnp.zeros_like(ds_scr)

        def chunk(cc, carry):
            c = nc - 1 - cc
            r0 = pl.multiple_of(c * DN_CHUNK, DN_CHUNK)
            rows = pl.ds(r0, DN_CHUNK)
            e = jnp.exp(gl_ref[pl.ds(c, 1), :])
            states = [sall_ref[c, h] for h in range(HEADS)]
            ds_outs = [ds_scr[h] for h in range(HEADS)]
            u_a, w_a, kt_a, qd_a, do_a = u_ref[rows, :], w_ref[rows, :], kt_ref[rows, :], qd_ref[rows, :], do_ref[rows, :]
            a_a = [a_ref[h, rows, :] for h in range(HEADS)]
            da, dqd, dkt, du, dw, dgl, ds_new = [], [], [], [], [], [], []
            for h in range(HEADS):
                cs = slice(h * LANE, (h + 1) * LANE)
                s, ds_out = states[h], ds_outs[h]
                w_c, kt_c, qd_c, do_c = w_a[:, cs], kt_a[:, cs], qd_a[:, cs], do_a[:, cs]
                v_new = u_a[:, cs] - _dot(w_c, s, _NN)
                dv_new = _dot(a_a[h], do_c, _TN) + _dot(kt_c, ds_out, _NN)
                da.append(_dot(do_c, v_new, _NT))
                dqd.append(_dot(do_c, s, _NT))
                dkt.append(_dot(v_new, ds_out, _NT))
                du.append(dv_new)
                dw.append(-_dot(dv_new, s, _NT))
                eh = e[:, cs]
                dgl.append(jnp.broadcast_to(jnp.sum(ds_out * s, axis=0, keepdims=True) * eh, (8, LANE)))
                ds_new.append(ds_out * eh + _dot(qd_c, do_c, _TN) - _dot(w_c, dv_new, _TN))
            cat = lambda parts: jnp.concatenate(parts, axis=1)
            dqd_ref[rows, :], dkt_ref[rows, :], du_ref[rows, :], dw_ref[rows, :] = cat(dqd), cat(dkt), cat(du), cat(dw)
            dgl_ref[pl.ds(pl.multiple_of(c * 8, 8), 8), :] = cat(dgl)
            for h in range(HEADS):
                da_ref[h, rows, :] = da[h]
                ds_scr[h] = ds_new[h]
            return carry

        lax.fori_loop(0, nc, chunk, 0)

    rev = lambda i: (ng - 1 - i, 0)
    row = pl.BlockSpec((rr, HEADS * LANE), rev)
    a_spec = pl.BlockSpec((HEADS, rr, DN_CHUNK), lambda i: (0, ng - 1 - i, 0))
    gl_spec = pl.BlockSpec((nc, HEADS * LANE), rev)
    wide = jax.ShapeDtypeStruct((t, HEADS * LANE), F32)
    outs, carried = _carried_call(
        body, rider, *_grid_ends((ng,)), name="delta_scan_bwd", grid=(ng,),
        in_specs=[row] * 4 + [a_spec, gl_spec, pl.BlockSpec((nc, HEADS, LANE, LANE), lambda i: (ng - 1 - i, 0, 0, 0)), row],
        out_specs=[row] * 4 + [a_spec, pl.BlockSpec((nc * 8, HEADS * LANE), rev)],
        out_shape=[wide] * 4 + [jax.ShapeDtypeStruct((HEADS, t, DN_CHUNK), F32),
                                jax.ShapeDtypeStruct((t // DN_CHUNK * 8, HEADS * LANE), F32)],
        scratch_shapes=[pltpu.VMEM((HEADS, LANE, LANE), F32)], sem=("arbitrary",), args=(u, w, qd, kt, intra, gl, sall, do))
    return tuple(outs[:5]) + (outs[5].reshape(t // DN_CHUNK, 8, HEADS * LANE)[:, 0, :],), carried


_ATT_TILE = 512


def _kv_rows(j, tk):
    return pl.ds(pl.multiple_of(j * tk, tk), tk)


def _att_scores(ql, qr, ckv_ref, kr_ref, j, tk):
    ks = _kv_rows(j, tk)
    return (_dot(ql, ckv_ref[ks, :], _NT) + _dot(qr, kr_ref[ks, :], _NT)) * ATT_SCALE


def _diag_mask(s):
    qi = lax.broadcasted_iota(jnp.int32, s.shape, 0)
    ki = lax.broadcasted_iota(jnp.int32, s.shape, 1)
    return jnp.where(ki <= qi, s, NEG_BIG)


def _attention(ql, qr, ckv, kr):
    t = ckv.shape[0]
    tq = min(_ATT_TILE, t)

    nl = tq // LANE

    def lane_fold(v, op):
        out = v[:, :LANE]
        for k in range(1, nl):
            out = op(out, v[:, k * LANE:(k + 1) * LANE])
        return out

    def body(ql_ref, qr_ref, ckv_ref, kr_ref, o_ref, lse_ref, s_all, m_lanes, l_lanes, acc_scr):
        qi = pl.program_id(1)
        q_lat, q_rope = ql_ref[...], qr_ref[...]
        m_lanes[...] = jnp.full_like(m_lanes, NEG_BIG)

        def scores(j, masked):
            s = _att_scores(q_lat, q_rope, ckv_ref, kr_ref, j, tq)
            if masked:
                s = _diag_mask(s)
            s_all[j] = s
            m_lanes[...] = jnp.maximum(m_lanes[...], lane_fold(s, jnp.maximum))

        def scores_body(j, carry):
            scores(j, False)
            return carry

        lax.fori_loop(0, qi, scores_body, 0)
        scores(qi, True)
        m = jnp.max(m_lanes[...], axis=-1, keepdims=True)
        mb = jnp.broadcast_to(m, (tq, LANE))
        l_lanes[...] = jnp.zeros_like(l_lanes)
        acc_scr[...] = jnp.zeros_like(acc_scr)

        def weigh(j, carry):
            s = s_all[j]
            p = jnp.concatenate([jnp.exp(s[:, k * LANE:(k + 1) * LANE] - mb) for k in range(nl)], axis=1)
            l_lanes[...] += lane_fold(p, jnp.add)
            acc_scr[...] += _dot(p, ckv_ref[_kv_rows(j, tq), :], _NN)
            return carry

        lax.fori_loop(0, qi + 1, weigh, 0)
        l = jnp.sum(l_lanes[...], axis=-1, keepdims=True)
        o_ref[...] = acc_scr[...] / l
        lse_ref[...] = m + jnp.log(l)

    return pl.pallas_call(
        body, name="attention", grid=(HEADS, t // tq),
        in_specs=[pl.BlockSpec((None, tq, KV_LORA), lambda h, i: (h, i, 0)),
                  pl.BlockSpec((tq, ROPE_PAD), lambda h, i: (i, h)),
                  pl.BlockSpec((t, KV_LORA), lambda h, i: (0, 0)),
                  pl.BlockSpec((t, ROPE_PAD), lambda h, i: (0, 0))],
        out_specs=[pl.BlockSpec((None, tq, KV_LORA), lambda h, i: (h, i, 0)),
                   pl.BlockSpec((None, tq, 1), lambda h, i: (h, i, 0))],
        out_shape=[jax.ShapeDtypeStruct((HEADS, t, KV_LORA), F32), jax.ShapeDtypeStruct((HEADS, t, 1), F32)],
        scratch_shapes=[pltpu.VMEM((t // tq, tq, tq), F32), pltpu.VMEM((tq, LANE), F32), pltpu.VMEM((tq, LANE), F32),
                        pltpu.VMEM((tq, KV_LORA), F32)],
        compiler_params=_cparams(("parallel", "parallel")))(ql, qr, ckv, kr)


def _attention_bwd(ql, qr, ckv, kr, out, lse, dout):
    t = ckv.shape[0]
    tq = min(_ATT_TILE, t)

    def body(ql_ref, qr_ref, ckv_ref, kr_ref, o_ref, lse_ref, do_ref, dql_ref, dqr_ref, dckv_ref, dkr_ref,
             dql_scr, dqr_scr):
        h, qi = pl.program_id(0), pl.program_id(1)

        @pl.when((h == 0) & (qi == 0))
        def _():
            dckv_ref[...] = jnp.zeros_like(dckv_ref)
            dkr_ref[...] = jnp.zeros_like(dkr_ref)

        q_lat, q_rope = ql_ref[...], qr_ref[...]
        d_o = do_ref[...].astype(_CDT)
        lse_v = lse_ref[...]
        dsum = jnp.sum(do_ref[...] * o_ref[...], axis=-1, keepdims=True)
        dql_scr[...] = jnp.zeros_like(dql_scr)
        dqr_scr[...] = jnp.zeros_like(dqr_scr)

        def step(j, masked):
            ks = _kv_rows(j, tq)
            s = _att_scores(q_lat, q_rope, ckv_ref, kr_ref, j, tq)
            if masked:
                s = _diag_mask(s)
            p = jnp.exp(s - lse_v)
            kv = ckv_ref[ks, :]
            ds = (p * (_dot(d_o, kv, _NT) - dsum) * ATT_SCALE).astype(_CDT)
            pb = p.astype(_CDT)
            dql_scr[...] += _dot(ds, kv, _NN)
            dqr_scr[...] += _dot(ds, kr_ref[ks, :], _NN)
            dckv_ref[ks, :] += _dot(pb, d_o, _TN) + _dot(ds, q_lat, _TN)
            dkr_ref[ks, :] += _dot(ds, q_rope, _TN)

        def loop_body(j, carry):
            step(j, False)
            return carry

        lax.fori_loop(0, qi, loop_body, 0)
        step(qi, True)
        dql_ref[...] = dql_scr[...].astype(dql_ref.dtype)
        dqr_ref[...] = dqr_scr[...]

    lat = pl.BlockSpec((None, tq, KV_LORA), lambda h, i: (h, i, 0))
    rope = pl.BlockSpec((tq, ROPE_PAD), lambda h, i: (i, h))
    kfull = pl.BlockSpec((t, KV_LORA), lambda h, i: (0, 0))
    rfull = pl.BlockSpec((t, ROPE_PAD), lambda h, i: (0, 0))
    return pl.pallas_call(
        body, name="attention_bwd", grid=(HEADS, t // tq),
        in_specs=[lat, rope, kfull, rfull, lat, pl.BlockSpec((None, tq, 1), lambda h, i: (h, i, 0)), lat],
        out_specs=[lat, rope, kfull, rfull],
        out_shape=[jax.ShapeDtypeStruct((HEADS, t, KV_LORA), _CDT), jax.ShapeDtypeStruct((t, HEADS * ROPE_PAD), F32),
                   jax.ShapeDtypeStruct((t, KV_LORA), F32), jax.ShapeDtypeStruct((t, ROPE_PAD), F32)],
        scratch_shapes=[pltpu.VMEM((tq, KV_LORA), F32), pltpu.VMEM((tq, ROPE_PAD), F32)],
        compiler_params=_cparams(("arbitrary", "arbitrary")))(ql, qr, ckv, kr, out, lse, dout)


_DX_ROWS = 256


def _dx_fused(pairs, add, add_scale, rider=None):
    t, d = add.shape
    tm = min(_DX_ROWS, t)
    n = len(pairs)

    def body(*refs):
        acc = refs[2 * n][...] * add_scale
        for i in range(n):
            acc = acc + _dot(refs[i][...], refs[n + i][...], _NT)
        refs[2 * n + 1][...] = acc

    in_specs = [pl.BlockSpec((tm, a.shape[1]), lambda i: (i, 0)) for a, _ in pairs]
    in_specs += [pl.BlockSpec(w.shape, lambda i: (0, 0)) for _, w in pairs]
    row = pl.BlockSpec((tm, d), lambda i: (i, 0))
    grid = (t // tm,)
    (dx,), carried = _carried_call(
        body, rider, *_grid_ends(grid), name="b_dx", grid=grid, in_specs=in_specs + [row], out_specs=[row],
        out_shape=[jax.ShapeDtypeStruct((t, d), F32)], scratch_shapes=[], sem=("arbitrary",),
        args=tuple(a for a, _ in pairs) + tuple(w for _, w in pairs) + (add,))
    return dx, carried


def _ffn_in_swiglu(a, w):
    t, k = a.shape
    hid = w.shape[1] // 2
    tm, tn = _tile(t, 1024), _tile(hid, 1408)
    nj = hid // tn

    def body(a_ref, bg_ref, bu_ref, act_ref, gt_ref, up_ref):
        av = a_ref[...].astype(_CDT)
        gt = jnp.dot(av, bg_ref[...].astype(_CDT), preferred_element_type=F32)
        up = jnp.dot(av, bu_ref[...].astype(_CDT), preferred_element_type=F32)
        act_ref[...] = _swiglu(gt, up).astype(act_ref.dtype)
        gt_ref[...] = gt.astype(gt_ref.dtype)
        up_ref[...] = up.astype(up_ref.dtype)

    out = pl.BlockSpec((tm, tn), lambda i, j: (i, j))
    return pl.pallas_call(
        body, name="f_ffn_in_swiglu", grid=(t // tm, nj),
        in_specs=[pl.BlockSpec((tm, k), lambda i, j: (i, 0)), pl.BlockSpec((k, tn), lambda i, j: (0, j)),
                  pl.BlockSpec((k, tn), lambda i, j: (0, nj + j))],
        out_specs=[out] * 3, out_shape=[jax.ShapeDtypeStruct((t, hid), _CDT)] * 3,
        compiler_params=_cparams(("parallel", "parallel")))(a, w, w)


def _gated_norm(o, z, w):
    return _rms_norm(o, w) * _silu(z)


def _gated_norm_heads(o, z, w):
    heads = [_gated_norm(o[:, h * LANE:(h + 1) * LANE], z[:, h * LANE:(h + 1) * LANE], w) for h in range(HEADS)]
    return jnp.concatenate(heads, axis=1)


def _mla_pre(ckv, krp, cq, cosb, sinb, qw, kw):
    return _rms_norm(cq, qw), _rms_norm(ckv, kw), _rope(krp, cosb, sinb)


def _merge(gg, y_dn, y_mla):
    return _sigmoid(gg[:, :D_MODEL]) * y_dn + _sigmoid(gg[:, D_MODEL:]) * y_mla


def _ln1(xv, attn_out, g, b):
    return _layer_norm(ALPHA * xv + attn_out, g, b)


def _final(h1, ffn, gate_pre, ple_proj, g, b):
    return _layer_norm(ALPHA * h1 + ffn + _sigmoid(gate_pre) * ple_proj, g, b)


def _swiglu(gt, up):
    return _silu(gt) * up


def _local_step(x, p, cosb, sinb, target, wt, sp, exch):
    t = x.shape[0]
    bf = _CDT
    xb = x.astype(bf)
    g = {}

    qkv_pre = _mm2(xb, wt['qkv'], name="f_qkv")
    z = _mm2(xb, wt['z'], name="f_z")
    gg = _mm2(xb, wt['gg'], name="f_gg")
    pm = _mm2(xb, wt['mla'], name="f_mla")
    qkv_act = _conv_silu(qkv_pre, sp['conv_w'])
    (u, w_, qd, kt, intra, gl, t_inv), sent = _delta_local(qkv_act, pm, sp['a_log'], sp['dt_bias'], rider=exch.gather_send())
    (o_dn, sall), passed = _delta_scan(u, w_, qd, kt, intra, gl, rider=exch.gather_pass(sent))
    wt = dict(wt, **exch.weights(passed))
    (og,) = _rowwise(lambda h, o, zz, w: (_gated_norm_heads(o, zz, w),),
                     [(o_dn, D_MODEL, 0, False), (z, D_MODEL, 0, False)], [sp['dn_norm_w']],
                     [(D_MODEL, D_MODEL, False, bf)], name="f_gated_norm")
    y_dn = _mm2(og, wt['br_dn'], name="f_br_dn")

    c_q, c_kv, k_rope = _rowwise(
        lambda h, *a: _mla_pre(*a),
        [(pm, KV_LORA, 0, False), (pm, ROPE_PAD, 2, False), (pm, Q_LORA, 2, False),
         (cosb, ROPE_PAD, 0, False), (sinb, ROPE_PAD, 0, False)],
        [sp['q_norm_w'], sp['kv_norm_w']],
        [(Q_LORA, Q_LORA, False, bf), (KV_LORA, KV_LORA, False, bf), (ROPE_PAD, ROPE_PAD, False, bf)], name="f_mla_pre")
    q_nope = _mm2(c_q, wt['uq_nope'], name="f_uq_nope", out_dtype=bf)
    q_rope_pre = _mm2(c_q, wt['uq_rope'], name="f_uq_rope")
    (q_rope,) = _rowwise(lambda h, q, c, s: (_rope(q, c, s),),
                         [(q_rope_pre, HEADS * ROPE_PAD, 0, False), (cosb, ROPE_PAD, 0, False), (sinb, ROPE_PAD, 0, False)],
                         [], [(HEADS * ROPE_PAD, HEADS * ROPE_PAD, False, bf)], name="f_q_rope")
    q_lat = _mm(q_nope, wt['uk'], name="f_q_lat", tb=True, heads=HEADS, a_head='col', b_head='lead', out_head='lead',
                dims=(t, KV_LORA, NOPE), out_dtype=bf)
    out_lat, lse = _attention(q_lat, q_rope, c_kv, k_rope)
    o_mla = _mm(out_lat, wt['uv'], name="f_o_mla", heads=HEADS, a_head='lead', b_head='lead', out_head='col',
                dims=(t, NOPE, KV_LORA), out_dtype=bf)
    y_mla = _mm2(o_mla, wt['br_mla'], name="f_br_mla")

    (mixed,) = _rowwise(lambda h, *a: (_merge(*a),),
                        [(gg, 2 * D_MODEL, 0, False), (y_dn, D_MODEL, 0, False), (y_mla, D_MODEL, 0, False)],
                        [], [(D_MODEL, D_MODEL, False, bf)], name="f_merge")
    attn_out = _mm2(mixed, wt['o'], name="f_o")
    h1, h1b = _rowwise(lambda h, *a: (_ln1(*a),) * 2, [(x, D_MODEL, 0, False), (attn_out, D_MODEL, 0, False)],
                       [sp['ln1_g'], sp['ln1_b']], [(D_MODEL, D_MODEL, False, F32), (D_MODEL, D_MODEL, False, bf)],
                       name="f_ln1")
    act, ffn_gt, ffn_up = _ffn_in_swiglu(h1b, wt['ffn_in'])
    ffn = _mm2(act, wt['ffn_out'], name="f_ffn_out")
    gate_pre = _mm2(h1b, wt['ple_gate'], name="f_ple_gate")
    pb = p.astype(bf)
    ple_proj = _mm2(pb, wt['ple'], name="f_ple")

    def final_fn(h, h1v, ffnv, gpv, ppv, tgt, gv, bv):
        y, vjp = jax.vjp(_final, h1v, ffnv, gpv, ppv, gv, bv)
        err = y - tgt
        dh1, dffn, dgp, dpp, dg, db = vjp(err * (1.0 / D_MODEL))
        sq = err * err
        lanes = sq[:, :LANE]
        for j in range(1, D_MODEL // LANE):
            lanes = lanes + sq[:, j * LANE:(j + 1) * LANE]
        loss = jnp.sum(lanes, axis=0, keepdims=True) * (0.5 / D_MODEL)
        return dffn, dffn, dgp, dpp, dg, db, loss

    dpre2, dpre2b, dgate_pre, dple_proj, g['ln2_g'], g['ln2_b'], loss_lanes = _rowwise(
        final_fn, [(a, D_MODEL, 0, False) for a in (h1, ffn, gate_pre, ple_proj, target)],
        [sp['ln2_g'], sp['ln2_b']],
        [(D_MODEL, D_MODEL, False, F32)] + [(D_MODEL, D_MODEL, False, bf)] * 3,
        [(1, D_MODEL), (1, D_MODEL), (1, LANE)], name="b_final")

    g['ple'] = _mm2(pb, dple_proj, ta=True, name="g_ple")
    g['ple_gate'] = _mm2(h1b, dgate_pre, ta=True, name="g_ple_gate")
    g['ffn_out'] = _mm2(act, dpre2b, ta=True, name="g_ffn_out")
    dact = _mm2(dpre2b, wt['ffn_out'], tb=True, name="b_dact", out_dtype=bf)

    def swiglu_bwd(h, gt, up, d):
        _, vjp = jax.vjp(_swiglu, gt.astype(F32), up.astype(F32))
        dgt, dup = vjp(d.astype(F32))
        return (jnp.concatenate([dgt, dup], axis=1),)

    (dffn_in,) = _rowwise(swiglu_bwd, [(ffn_gt, FFN_HIDDEN, 0, False), (ffn_up, FFN_HIDDEN, 0, False),
                                       (dact, FFN_HIDDEN, 0, False)], [],
                          [(2 * FFN_HIDDEN, 2 * FFN_HIDDEN, False, bf)], name="b_swiglu")
    g['ffn_in'] = _mm2(h1b, dffn_in, ta=True, name="g_ffn_in")
    dh1 = _mm2(dffn_in, wt['ffn_in'], tb=True, name="b_dh1_ffn", add=dpre2, add_scale=ALPHA)
    dh1 = _mm2(dgate_pre, wt['ple_gate'], tb=True, name="b_dh1_gate", add=dh1)

    def ln1_bwd(h, xv, ao, d, gv, bv):
        _, vjp = jax.vjp(_ln1, xv, ao, gv, bv)
        _, dao, dg, db = vjp(d)
        return dao, dao, dg, db

    dpre1, dpre1b, g['ln1_g'], g['ln1_b'] = _rowwise(
        ln1_bwd, [(x, D_MODEL, 0, False), (attn_out, D_MODEL, 0, False), (dh1, D_MODEL, 0, False)],
        [sp['ln1_g'], sp['ln1_b']], [(D_MODEL, D_MODEL, False, F32), (D_MODEL, D_MODEL, False, bf)],
        [(1, D_MODEL), (1, D_MODEL)], name="b_ln1")

    g['o'] = _mm2(mixed, dpre1b, ta=True, name="g_o")
    dmixed = _mm2(dpre1b, wt['o'], tb=True, name="b_dmixed")

    def merge_bwd(h, ggv, yd, ym, d):
        _, vjp = jax.vjp(_merge, ggv, yd, ym)
        return vjp(d)

    dgg, dy_dn, dy_mla = _rowwise(
        merge_bwd, [(gg, 2 * D_MODEL, 0, False), (y_dn, D_MODEL, 0, False), (y_mla, D_MODEL, 0, False),
                    (dmixed, D_MODEL, 0, False)], [],
        [(2 * D_MODEL, 2 * D_MODEL, False, bf), (D_MODEL, D_MODEL, False, bf), (D_MODEL, D_MODEL, False, bf)],
        name="b_merge")
    g['br_dn'] = _mm2(og, dy_dn, ta=True, name="g_br_dn")
    dog = _mm2(dy_dn, wt['br_dn'], tb=True, name="b_dog")
    g['br_mla'] = _mm2(o_mla, dy_mla, ta=True, name="g_br_mla")
    do_mla = _mm2(dy_mla, wt['br_mla'], tb=True, name="b_do_mla", out_dtype=bf)

    dout_lat = _mm(do_mla, wt['uv'], name="b_dout_lat", tb=True, heads=HEADS, a_head='col', b_head='lead',
                   out_head='lead', dims=(t, KV_LORA, NOPE))
    g['uv'] = _mm(out_lat, do_mla, name="g_uv", ta=True, heads=HEADS, a_head='lead', b_head='col', out_head='lead',
                  dims=(KV_LORA, NOPE, t))
    dq_lat, dq_rope, dckv_att, dkr_att = _attention_bwd(q_lat, q_rope, c_kv, k_rope, out_lat, lse, dout_lat)
    dq_nope = _mm(dq_lat, wt['uk'], name="b_dq_nope", heads=HEADS, a_head='lead', b_head='lead', out_head='col',
                  dims=(t, NOPE, KV_LORA), out_dtype=bf)
    g['uk'] = _mm(dq_lat, q_nope, name="g_uk", ta=True, heads=HEADS, a_head='lead', b_head='col', out_head='lead',
                  dims=(KV_LORA, NOPE, t))
    (dq_rope_pre,) = _rowwise(lambda h, d, c, s: (_rope_bwd(d, c, s),),
                              [(dq_rope, HEADS * ROPE_PAD, 0, False), (cosb, ROPE_PAD, 0, False), (sinb, ROPE_PAD, 0, False)],
                              [], [(HEADS * ROPE_PAD, HEADS * ROPE_PAD, False, bf)], name="b_q_rope")
    g['uq_nope'] = _mm2(c_q, dq_nope, ta=True, name="g_uq_nope")
    g['uq_rope'] = _mm2(c_q, dq_rope_pre, ta=True, name="g_uq_rope")
    dc_q = _mm2(dq_nope, wt['uq_nope'], tb=True, name="b_dcq_nope")
    dc_q = _mm2(dq_rope_pre, wt['uq_rope'], tb=True, name="b_dcq_rope", add=dc_q)

    def gated_norm_bwd(h, o, zz, d, w):
        _, vjp = jax.vjp(_gated_norm_heads, o, zz, w)
        return vjp(d)

    do_dn, dz, g['dn_norm_w'] = _rowwise(
        gated_norm_bwd, [(o_dn, D_MODEL, 0, False), (z, D_MODEL, 0, False), (dog, D_MODEL, 0, False)], [sp['dn_norm_w']],
        [(D_MODEL, D_MODEL, False, F32), (D_MODEL, D_MODEL, False, bf)], [(1, LANE)], name="b_gated_norm")
    (du, dw, dqd, dkt, dintra, dgl), paired = _delta_scan_bwd(u, w_, qd, kt, intra, gl, sall, do_dn, rider=exch.pair_send(g))
    (dq_a, dk_a, dv_a, dba, g['a_log'], g['dt_bias']), arrived = _delta_local_bwd(
        qkv_act, pm, sp['a_log'], sp['dt_bias'], t_inv, du, dw, dqd, dkt, dintra, dgl, rider=exch.reduce_send(paired))
    exch.reduce_arrived(arrived)
    dqkv_pre, g['conv_w'] = _conv_silu_bwd(qkv_pre, sp['conv_w'], [dq_a, dk_a, dv_a])

    def mla_pre_bwd(h, ckv, krp, cq, cosv, sinv, dcq, dckv, dkr, qw, kw):
        _, vjp = jax.vjp(lambda a, b, c, d, e: (_rms_norm(c, d), _rms_norm(a, e)), ckv, krp, cq, qw, kw)
        dckv_p, _, dcq_p, dqw, dkw = vjp((dcq, dckv))
        dkr_p = _rope_bwd(dkr, cosv, sinv)
        dpm = jnp.concatenate([dckv_p, dkr_p, jnp.zeros((ckv.shape[0], 3 * LANE), F32), dcq_p], axis=1)
        return dpm, dqw, dkw

    dpm_main, g['q_norm_w'], g['kv_norm_w'] = _rowwise(
        mla_pre_bwd,
        [(pm, KV_LORA, 0, False), (pm, ROPE_PAD, 2, False), (pm, Q_LORA, 2, False),
         (cosb, ROPE_PAD, 0, False), (sinb, ROPE_PAD, 0, False),
         (dc_q, Q_LORA, 0, False), (dckv_att, KV_LORA, 0, False), (dkr_att, ROPE_PAD, 0, False)],
        [sp['q_norm_w'], sp['kv_norm_w']], [(1152, 1152, False, F32)], [(1, Q_LORA), (1, KV_LORA)], name="b_mla_pre")
    (dpm,) = _rowwise(
        lambda h, a, b: (jnp.concatenate([a[:, :3 * LANE], b, a[:, 4 * LANE:]], axis=1),),
        [(dpm_main, 1152, 0, False), (dba, LANE, 0, False)], [], [(1152, 1152, False, bf)], name="b_dpm")

    g['qkv'] = _mm2(xb, dqkv_pre, ta=True, name="g_qkv")
    g['z'] = _mm2(xb, dz, ta=True, name="g_z")
    g['gg'] = _mm2(xb, dgg, ta=True, name="g_gg")
    g['mla'] = _mm2(xb, dpm, ta=True, name="g_mla")
    dx, arrived = _dx_fused([(dqkv_pre, wt['qkv']), (dz, wt['z']), (dgg, wt['gg']), (dpm, wt['mla'])], dpre1, ALPHA,
                            rider=exch.in_send(g))
    exch.in_arrived(arrived)
    return loss_lanes, dx, g


_IN_SIZES = (QKV_W, HEADS * DN_DK, HEADS, HEADS, Q_LORA, KV_LORA, ROPE, D_MODEL, D_MODEL)


def _rope_tables(positions):
    inv_freq = ROPE_BASE ** (-jnp.arange(0, ROPE, 2, dtype=F32) / ROPE)
    ang = positions.astype(F32)[:, None] * inv_freq
    cos, sin = jnp.cos(ang), jnp.sin(ang)
    zeros = jnp.zeros((positions.shape[0], ROPE_PAD - ROPE), F32)
    return jnp.concatenate([cos, cos, zeros], axis=1), jnp.concatenate([-sin, sin, zeros], axis=1)


def _prep_w_in(w_in):
    dt = w_in.dtype
    offs = [0]
    for s in _IN_SIZES:
        offs.append(offs[-1] + s)
    qkv, z, wb, wa, cq, ckv, kr, gd, gm = [w_in[:, offs[i]:offs[i + 1]] for i in range(len(_IN_SIZES))]
    zc = lambda n: jnp.zeros((D_MODEL, n), dt)
    return {
        'qkv': qkv, 'z': z, 'gg': jnp.concatenate([gd, gm], axis=1),
        'mla': jnp.concatenate([ckv, kr, zc(ROPE_PAD - ROPE), wb, wa, zc(LANE - 2 * HEADS), zc(2 * LANE), cq], axis=1),
    }


def _prep_weights(full):
    w_uq = full['w_uq']
    wt = {
        'uq_nope': w_uq[:, :, :NOPE].reshape(Q_LORA, HEADS * NOPE),
        'uq_rope': jnp.pad(w_uq[:, :, NOPE:], ((0, 0), (0, 0), (0, ROPE_PAD - ROPE))).reshape(Q_LORA, HEADS * ROPE_PAD),
        'uk': jnp.transpose(full['w_uk'], (1, 0, 2)), 'uv': jnp.transpose(full['w_uv'], (1, 0, 2)),
        'br_dn': full['w_br_dn'], 'br_mla': full['w_br_mla'], 'o': full['w_o'], 'ffn_in': full['w_ffn_in'],
        'ffn_out': full['w_ffn_out'], 'ple': full['w_ple'], 'ple_gate': full['w_ple_gate'],
    }
    return wt


def _prep_small(small):
    pad = lambda v: jnp.pad(v, (0, LANE - v.shape[0]))[None, :]
    return {
        'conv_w': small['conv_w'], 'a_log': pad(small['dn_a_log']), 'dt_bias': pad(small['dn_dt_bias']),
        'dn_norm_w': small['dn_norm_w'][None, :], 'q_norm_w': small['q_norm_w'][None, :],
        'kv_norm_w': small['kv_norm_w'][None, :], 'ln1_g': small['ln1_g'][None, :], 'ln1_b': small['ln1_b'][None, :],
        'ln2_g': small['ln2_g'][None, :], 'ln2_b': small['ln2_b'][None, :],
    }


def _w_in_grad(g):
    mla = g['mla']
    ba0 = KV_LORA + ROPE_PAD
    cq0 = ba0 + 3 * LANE
    return jnp.concatenate([
        g['qkv'], g['z'], mla[:, ba0:ba0 + HEADS], mla[:, ba0 + HEADS:ba0 + 2 * HEADS], mla[:, cq0:cq0 + Q_LORA],
        mla[:, :KV_LORA], mla[:, KV_LORA:KV_LORA + ROPE], g['gg']], axis=1)


def _unprep_grads_late(g):
    return {
        'conv_w': g['conv_w'], 'dn_a_log': g['a_log'][0, :HEADS], 'dn_dt_bias': g['dt_bias'][0, :HEADS],
        'dn_norm_w': g['dn_norm_w'][0], 'q_norm_w': g['q_norm_w'][0], 'kv_norm_w': g['kv_norm_w'][0],
        'ln1_g': g['ln1_g'][0], 'ln1_b': g['ln1_b'][0], 'ln2_g': g['ln2_g'][0], 'ln2_b': g['ln2_b'][0],
    }


def _unprep_grads_early(g):
    w_uq = jnp.concatenate([g['uq_nope'].reshape(Q_LORA, HEADS, NOPE),
                            g['uq_rope'].reshape(Q_LORA, HEADS, ROPE_PAD)[:, :, :ROPE]], axis=2)
    return {
        'w_uq': w_uq, 'w_uk': jnp.transpose(g['uk'], (1, 0, 2)), 'w_uv': jnp.transpose(g['uv'], (1, 0, 2)),
        'w_br_dn': g['br_dn'], 'w_br_mla': g['br_mla'], 'w_o': g['o'],
        'w_ffn_in': g['ffn_in'], 'w_ffn_out': g['ffn_out'], 'w_ple': g['ple'], 'w_ple_gate': g['ple_gate'],
    }


_FLATB_PIECES = (
    ('w_ffn_out', 704, (704, D_MODEL)), ('w_br_dn', 256, (256, D_MODEL)), ('w_br_mla', 256, (256, D_MODEL)),
    ('w_o', 256, (256, D_MODEL)), ('w_ple_gate', 256, (256, D_MODEL)), ('w_uq', 144, (96, HEADS, NOPE + ROPE)),
    ('w_uk', 64, (64, HEADS, NOPE)), ('w_uv', 64, (64, HEADS, NOPE)), ('w_ple', 64, (PLE_DIM, 256)),
)
FLATB_ROWS = 2112
W_IN_SHARD = D_IN // N_SHARD
FFN_IN_SHARD = 2 * FFN_HIDDEN // N_SHARD
A_ROWS = D_MODEL + 32
_CONV_SHARD = QKV_W // N_SHARD
_ADD_TILES = (256, 256, 352)


def _flatb_offsets():
    offs, o = {}, 0
    for name, rows, _ in _FLATB_PIECES:
        offs[name] = o
        o += rows
    return offs, o


def _pack_shards(ws, conv_w):
    conv_bits = lax.bitcast_convert_type(conv_w, jnp.bfloat16).reshape(DN_CONV, 2 * _CONV_SHARD).astype(_CDT)
    tail = jnp.pad(conv_bits, ((0, A_ROWS - D_MODEL - DN_CONV), (0, W_IN_SHARD - 2 * _CONV_SHARD)))
    a_buf = jnp.concatenate([ws['w_in'].astype(_CDT), tail], axis=0)
    parts = [ws[name].astype(_CDT).reshape(rows, FLAT_W) for name, rows, _ in _FLATB_PIECES]
    used = sum(p.shape[0] for p in parts)
    parts.append(jnp.zeros((FLATB_ROWS - used, FLAT_W), _CDT))
    return [a_buf, ws['w_ffn_in'].astype(_CDT), jnp.concatenate(parts, axis=0)]


def _unpack_w_in(gathered, local, me):
    a = [jnp.where(me == s, local, gathered[s]) for s in range(N_SHARD)]
    conv = [lax.bitcast_convert_type(
        p[D_MODEL:D_MODEL + DN_CONV, :2 * _CONV_SHARD].astype(jnp.bfloat16).reshape(DN_CONV, _CONV_SHARD, 2), F32) for p in a]
    return jnp.concatenate([p[:D_MODEL] for p in a], axis=1), jnp.concatenate(conv, axis=1)


def _unpack_rest(gathered, local, me):
    pick = lambda b, s: jnp.where(me == s, local[b], gathered[b][s])
    full = {'w_ffn_in': jnp.concatenate([pick(0, s) for s in range(N_SHARD)], axis=1)}
    offs, _ = _flatb_offsets()
    fb = [pick(1, s) for s in range(N_SHARD)]
    for name, rows, shape in _FLATB_PIECES:
        pieces = [p[offs[name]:offs[name] + rows].reshape(shape) for p in fb]
        full[name] = jnp.concatenate(pieces, axis=1 if name == 'w_ple' else 0)
    return full


def _shard_columns(g, w):
    return jnp.stack([g[:, s * w:(s + 1) * w] for s in range(N_SHARD)])


def _pack_grads_rest(gw):
    parts = []
    for name, rows, _ in _FLATB_PIECES:
        g = gw[name]
        if name == 'w_ple':
            parts.append(_shard_columns(g, PLE_DIM).reshape(N_SHARD, rows, FLAT_W))
        else:
            parts.append(g.reshape(N_SHARD, rows, FLAT_W))
    used = sum(p.shape[1] for p in parts)
    parts.append(jnp.zeros((N_SHARD, FLATB_ROWS - used, FLAT_W), F32))
    return [_shard_columns(gw['w_ffn_in'], FFN_IN_SHARD), jnp.concatenate(parts, axis=1)]


def _unpack_reduced(mine, theirs, c):
    whole = [jnp.concatenate([jnp.where(c == 0, m, t), jnp.where(c == 0, t, m)], axis=0) for m, t in zip(mine, theirs)]
    out = {'w_in': whole[0], 'w_ffn_in': whole[1]}
    offs, _ = _flatb_offsets()
    for name, rows, shape in _FLATB_PIECES:
        out[name] = whole[2][offs[name]:offs[name] + rows].reshape(shape)
    return out


_HBM = pl.BlockSpec(memory_space=pltpu.HBM)


def _place():
    x, y, c = lax.axis_index("x"), lax.axis_index("y"), lax.axis_index("c")
    chips = [(1 - x, y), (x, 1 - y), (1 - x, 1 - y)]
    return x, y, c, chips


def _remote(src, dst, send_sems, recv_sems, k, to):
    return pltpu.make_async_remote_copy(src_ref=src, dst_ref=dst, send_sem=send_sems.at[k], recv_sem=recv_sems.at[k],
                                        device_id=to, device_id_type=_MESH)


def _half_rows(ref, half, hf, lead=None):
    rows = pl.ds(pl.multiple_of(hf * half, 16), half)
    return ref.at[rows, :] if lead is None else ref.at[lead, rows, :]


class _Rider:
    def __init__(self, inputs, out_shape, n_sems, copies, aliases=None):
        self.inputs, self.out_shape, self.n_sems, self.copies = list(inputs), list(out_shape), n_sems, copies
        self.aliases = aliases or {}


def _carried_call(body, rider, first, last, *, name, grid, in_specs, out_specs, out_shape, scratch_shapes, sem, args):
    n_in, n_out, n_scr = len(in_specs), len(out_specs), len(scratch_shapes)
    if rider is None:
        res = pl.pallas_call(body, name=name, grid=grid, in_specs=in_specs, out_specs=out_specs, out_shape=out_shape,
                             scratch_shapes=scratch_shapes, compiler_params=_cparams(sem))(*args)
        return list(res), []
    ri, ro = len(rider.inputs), len(rider.out_shape)

    def full_body(*refs):
        own_in, r_in = refs[:n_in], refs[n_in:n_in + ri]
        o0 = n_in + ri
        own_out, r_out = refs[o0:o0 + n_out], refs[o0 + n_out:o0 + n_out + ro]
        s0 = o0 + n_out + ro
        own_scr, send_sems, recv_sems = refs[s0:s0 + n_scr], refs[s0 + n_scr], refs[s0 + n_scr + 1]

        @pl.when(first())
        def _():
            sends, _ = rider.copies(r_in, r_out, send_sems, recv_sems)
            for cp in sends:
                cp.start()

        body(*own_in, *own_out, *own_scr)

        @pl.when(last())
        def _():
            sends, arrivals = rider.copies(r_in, r_out, send_sems, recv_sems)
            for cp in arrivals():
                cp.wait_recv()
            for cp in sends:
                cp.wait_send()

    res = pl.pallas_call(
        full_body, name=name, grid=grid, in_specs=list(in_specs) + [_HBM] * ri, out_specs=list(out_specs) + [_HBM] * ro,
        out_shape=list(out_shape) + rider.out_shape,
        scratch_shapes=list(scratch_shapes) + [pltpu.SemaphoreType.DMA((rider.n_sems,))] * 2,
        input_output_aliases={n_in + i: n_out + o for i, o in rider.aliases.items()},
        compiler_params=_cparams(sem))(*args, *rider.inputs)
    return list(res[:n_out]), list(res[n_out:])


def _ride_gather_send(bufs):
    n = len(bufs)
    halves = [b.shape[0] // 2 for b in bufs]

    def copies(ins, outs, send_sems, recv_sems):
        x, y, c, chips = _place()
        slot = lambda b, cx, cy: _half_rows(outs[b], halves[b], c, lead=2 * cx + cy)
        sends = [_remote(_half_rows(ins[b], halves[b], c), slot(b, x, y), send_sems, recv_sems, 3 * b + j, (cx, cy, c))
                 for b in range(n) for j, (cx, cy) in enumerate(chips)]
        arrivals = lambda: [_remote(slot(b, cx, cy), slot(b, cx, cy), send_sems, recv_sems, 3 * b + j, (x, y, c))
                            for b in range(n) for j, (cx, cy) in enumerate(chips)]
        return sends, arrivals

    return _Rider(bufs, [jax.ShapeDtypeStruct((N_SHARD,) + b.shape, b.dtype) for b in bufs], 3 * n, copies)


def _ride_gather_pass(gathered):
    n = len(gathered)
    halves = [g.shape[1] // 2 for g in gathered]

    def copies(ins, outs, send_sems, recv_sems):
        x, y, c, chips = _place()
        slot = lambda b, cx, cy, hf: _half_rows(outs[b], halves[b], hf, lead=2 * cx + cy)
        sends = [_remote(slot(b, cx, cy, c), slot(b, cx, cy, c), send_sems, recv_sems, 3 * b + j, (x, y, 1 - c))
                 for b in range(n) for j, (cx, cy) in enumerate(chips)]
        arrivals = lambda: [_remote(slot(b, cx, cy, 1 - c), slot(b, cx, cy, 1 - c), send_sems, recv_sems, 3 * b + j, (x, y, c))
                            for b in range(n) for j, (cx, cy) in enumerate(chips)]
        return sends, arrivals

    return _Rider(gathered, [jax.ShapeDtypeStruct(g.shape, g.dtype) for g in gathered], 3 * n, copies,
                  aliases={b: b for b in range(n)})


def _ride_pair_exchange(gbufs):
    n = len(gbufs)
    halves = [g.shape[1] // 2 for g in gbufs]

    def copies(ins, outs, send_sems, recv_sems):
        x, y, c, _ = _place()
        sends = [_remote(ins[b].at[:, pl.ds(pl.multiple_of((1 - c) * halves[b], 16), halves[b]), :], outs[b],
                         send_sems, recv_sems, b, (x, y, 1 - c)) for b in range(n)]
        arrivals = lambda: [_remote(outs[b], outs[b], send_sems, recv_sems, b, (x, y, c)) for b in range(n)]
        return sends, arrivals

    return _Rider(gbufs, [jax.ShapeDtypeStruct((N_SHARD, h, g.shape[2]), g.dtype) for g, h in zip(gbufs, halves)], n, copies)


def _ride_chip_exchange(parts):
    n = len(parts)

    def copies(ins, outs, send_sems, recv_sems):
        x, y, c, chips = _place()
        sends = [_remote(ins[b].at[2 * cx + cy], outs[b].at[j], send_sems, recv_sems, 3 * b + j, (cx, cy, c))
                 for b in range(n) for j, (cx, cy) in enumerate(chips)]
        arrivals = lambda: [_remote(ins[b].at[0], outs[b].at[j], send_sems, recv_sems, 3 * b + j, (x, y, c))
                            for b in range(n) for j in range(len(chips))]
        return sends, arrivals

    return _Rider(parts, [jax.ShapeDtypeStruct((3,) + p.shape[1:], p.dtype) for p in parts], 3 * n, copies)


def _gather_shards(bufs, name):
    n = len(bufs)
    halves = [b.shape[0] // 2 for b in bufs]

    def body(*refs):
        ins, outs, send_sems, recv_sems = refs[:n], refs[n:2 * n], refs[2 * n], refs[2 * n + 1]
        x, y, c, chips = _place()
        me, sibling = (x, y, c), (x, y, 1 - c)
        slot = lambda b, cx, cy, hf: _half_rows(outs[b], halves[b], hf, lead=2 * cx + cy)
        first = [_remote(_half_rows(ins[b], halves[b], c), slot(b, x, y, c), send_sems, recv_sems, 6 * b + j, (cx, cy, c))
                 for b in range(n) for j, (cx, cy) in enumerate(chips)]
        for cp in first:
            cp.start()
        passed = []
        for j, (cx, cy) in enumerate(chips):
            for b in range(n):
                _remote(slot(b, cx, cy, c), slot(b, cx, cy, c), send_sems, recv_sems, 6 * b + j, me).wait_recv()
                fwd = _remote(slot(b, cx, cy, c), slot(b, cx, cy, c), send_sems, recv_sems, 6 * b + 3 + j, sibling)
                fwd.start()
                passed.append(fwd)
        for j, (cx, cy) in enumerate(chips):
            for b in range(n):
                _remote(slot(b, cx, cy, 1 - c), slot(b, cx, cy, 1 - c), send_sems, recv_sems, 6 * b + 3 + j, me).wait_recv()
        for cp in first + passed:
            cp.wait_send()

    return pl.pallas_call(
        body, name=name, out_shape=[jax.ShapeDtypeStruct((N_SHARD,) + b.shape, b.dtype) for b in bufs],
        in_specs=[_HBM] * n, out_specs=[_HBM] * n,
        scratch_shapes=[pltpu.SemaphoreType.DMA((6 * n,)), pltpu.SemaphoreType.DMA((6 * n,))],
    )(*bufs)


def _reduce_pair_exchange(gbufs, name):
    n = len(gbufs)
    halves = [g.shape[1] // 2 for g in gbufs]

    def body(*refs):
        ins, outs, send_sems, recv_sems = refs[:n], refs[n:2 * n], refs[2 * n], refs[2 * n + 1]
        x, y, c, _ = _place()
        cps = [_remote(ins[b].at[:, pl.ds(pl.multiple_of((1 - c) * halves[b], 16), halves[b]), :], outs[b],
                       send_sems, recv_sems, b, (x, y, 1 - c)) for b in range(n)]
        for cp in cps:
            cp.start()
        for cp in cps:
            cp.wait()

    return pl.pallas_call(
        body, name=name,
        out_shape=[jax.ShapeDtypeStruct((N_SHARD, h, g.shape[2]), g.dtype) for g, h in zip(gbufs, halves)],
        in_specs=[_HBM] * n, out_specs=[_HBM] * n,
        scratch_shapes=[pltpu.SemaphoreType.DMA((n,)), pltpu.SemaphoreType.DMA((n,))],
    )(*gbufs)


def _pair_add(gbuf, recv, c_arr, tr, name):
    _, rows, width = gbuf.shape
    half = rows // 2
    nt = half // tr

    def body(c_ref, a_ref, b_ref, o_ref):
        o_ref[...] = (a_ref[...] + b_ref[...]).astype(o_ref.dtype)

    blk = lambda f: pl.BlockSpec((None, tr, width), f)
    return pl.pallas_call(
        body, name=name, out_shape=jax.ShapeDtypeStruct((N_SHARD, half, width), jnp.bfloat16),
        grid_spec=pltpu.PrefetchScalarGridSpec(
            num_scalar_prefetch=1, grid=(N_SHARD, nt),
            in_specs=[blk(lambda s, i, c: (s, c[0] * nt + i, 0)), blk(lambda s, i, c: (s, i, 0))],
            out_specs=blk(lambda s, i, c: (s, i, 0))),
        compiler_params=_cparams(("parallel", "parallel")))(c_arr, gbuf, recv)


def _reduce_chip_exchange(parts, name):
    n = len(parts)

    def body(*refs):
        ins, outs, send_sems, recv_sems = refs[:n], refs[n:2 * n], refs[2 * n], refs[2 * n + 1]
        x, y, c, chips = _place()
        sends = [_remote(ins[b].at[2 * cx + cy], outs[b].at[j], send_sems, recv_sems, 3 * b + j, (cx, cy, c))
                 for b in range(n) for j, (cx, cy) in enumerate(chips)]
        for cp in sends:
            cp.start()
        for b in range(n):
            for j in range(len(chips)):
                _remote(ins[b].at[0], outs[b].at[j], send_sems, recv_sems, 3 * b + j, (x, y, c)).wait_recv()
        for cp in sends:
            cp.wait_send()

    return pl.pallas_call(
        body, name=name, out_shape=[jax.ShapeDtypeStruct((3,) + p.shape[1:], p.dtype) for p in parts],
        in_specs=[_HBM] * n, out_specs=[_HBM] * n,
        scratch_shapes=[pltpu.SemaphoreType.DMA((3 * n,)), pltpu.SemaphoreType.DMA((3 * n,))],
    )(*parts)


def _chip_add(part, recv, me_arr, tr, name):
    _, half, width = part.shape

    def body(me_ref, own, a0, a1, a2, o_ref):
        f = lambda r: r[...].astype(F32)
        o_ref[...] = ((f(own) + f(a0)) + f(a1)) + f(a2)

    specs = [pl.BlockSpec((None, tr, width), lambda i, me: (me[0], i, 0))]
    specs += [pl.BlockSpec((None, tr, width), functools.partial(lambda i, me, k: (k, i, 0), k=k)) for k in range(3)]
    return pl.pallas_call(
        body, name=name, out_shape=jax.ShapeDtypeStruct((half, width), F32),
        grid_spec=pltpu.PrefetchScalarGridSpec(
            num_scalar_prefetch=1, grid=(half // tr,), in_specs=specs,
            out_specs=pl.BlockSpec((tr, width), lambda i, me: (i, 0))),
        compiler_params=_cparams(("parallel",)))(me_arr, part, recv, recv, recv)


def _reduce_pair_share(rhalves, name):
    n = len(rhalves)

    def body(*refs):
        ins, outs, send_sems, recv_sems = refs[:n], refs[n:2 * n], refs[2 * n], refs[2 * n + 1]
        x, y, c, _ = _place()
        cps = [_remote(ins[b], outs[b], send_sems, recv_sems, b, (x, y, 1 - c)) for b in range(n)]
        for cp in cps:
            cp.start()
        for cp in cps:
            cp.wait()

    return pl.pallas_call(
        body, name=name, out_shape=[jax.ShapeDtypeStruct(r.shape, r.dtype) for r in rhalves],
        in_specs=[_HBM] * n, out_specs=[_HBM] * n,
        scratch_shapes=[pltpu.SemaphoreType.DMA((n,)), pltpu.SemaphoreType.DMA((n,))],
    )(*rhalves)


def _small_allreduce(buf):
    r, width = buf.shape
    n_dev = 8

    def body(x_ref, all_ref, sum_ref, send_sems, recv_sems, local_sem):
        x, y, c, chips = _place()
        me, sibling = (x, y, c), (x, y, 1 - c)

        def rows(px, py, pc):
            return all_ref.at[pl.ds(pl.multiple_of((4 * px + 2 * py + pc) * r, 8), r), :]

        def copy(k, block, to, src=None):
            return _remote(rows(*block) if src is None else src, rows(*block), send_sems, recv_sems, k, to)

        mine = pltpu.make_async_copy(x_ref, rows(*me), local_sem)
        mine.start()
        first = [copy(0, me, sibling, src=x_ref)]
        first += [copy(1 + j, me, (*chip, c), src=x_ref) for j, chip in enumerate(chips)]
        for cp in first:
            cp.start()
        passed = [copy(4 + j, (*chip, c), sibling) for j, chip in enumerate(chips)]
        for j, chip in enumerate(chips):
            copy(1 + j, (*chip, c), me).wait_recv()
            passed[j].start()
        copy(0, sibling, me).wait_recv()
        for j, chip in enumerate(chips):
            copy(4 + j, (*chip, 1 - c), me).wait_recv()
        for cp in first + passed:
            cp.wait_send()
        mine.wait()
        total = all_ref[0:r, :]
        for k in range(1, n_dev):
            total = total + all_ref[k * r:(k + 1) * r, :]
        sum_ref[...] = total

    vm = pl.BlockSpec(memory_space=pltpu.VMEM)
    _, total = pl.pallas_call(
        body, name="small_allreduce",
        out_shape=[jax.ShapeDtypeStruct((n_dev * r, width), buf.dtype), jax.ShapeDtypeStruct((r, width), buf.dtype)],
        in_specs=[vm], out_specs=[vm, vm],
        scratch_shapes=[pltpu.SemaphoreType.DMA((7,)), pltpu.SemaphoreType.DMA((7,)), pltpu.SemaphoreType.DMA],
    )(buf)
    return total


def _row_tile(rows, cap):
    if rows <= cap:
        return rows
    t = (cap // 8) * 8
    while t >= 8:
        if rows % t == 0:
            return t
        t -= 8
    return rows


def _adamw(w, g, m, v, name):
    shape = w.shape
    cols = shape[-1] if len(shape) <= 3 else shape[-2] * shape[-1]
    lead = len(shape) == 3
    w2, g2, m2, v2 = (a if lead else a.reshape(-1, cols) for a in (w, g, m, v))
    rows = shape[1] if lead else w2.shape[0]
    tr, tc = _row_tile(rows, 256), cols
    if tr == rows and rows > 256:
        tc = _tile(cols, 256)

    def body(w_ref, g_ref, m_ref, v_ref, d_ref, mo_ref, vo_ref):
        gv = g_ref[...]
        mn = ADAM_B1 * m_ref[...] + (1.0 - ADAM_B1) * gv
        vn = ADAM_B2 * v_ref[...] + (1.0 - ADAM_B2) * (gv * gv)
        m_hat = mn / (1.0 - ADAM_B1 ** ADAM_STEP)
        v_hat = vn / (1.0 - ADAM_B2 ** ADAM_STEP)
        d_ref[...] = -ADAM_LR * (m_hat / (jnp.sqrt(v_hat) + ADAM_EPS) + ADAM_WD * w_ref[...])
        mo_ref[...] = mn
        vo_ref[...] = vn

    blk = (pl.BlockSpec((None, tr, tc), lambda i, j: (0, i, j)) if lead else pl.BlockSpec((tr, tc), lambda i, j: (i, j)))
    outs = pl.pallas_call(
        body, name=name, grid=(rows // tr, cols // tc), in_specs=[blk] * 4, out_specs=[blk] * 3,
        out_shape=[jax.ShapeDtypeStruct(w2.shape, F32)] * 3,
        compiler_params=_cparams(("parallel", "parallel")))(w2, g2, m2, v2)
    return tuple(o.reshape(shape) for o in outs)


_WEIGHT_NAMES = ('w_in', 'conv_w', 'dn_a_log', 'dn_dt_bias', 'dn_norm_w', 'q_norm_w', 'w_uq', 'kv_norm_w', 'w_uk',
                 'w_uv', 'w_br_dn', 'w_br_mla', 'w_o', 'ln1_g', 'ln1_b', 'w_ffn_in', 'w_ffn_out', 'w_ple',
                 'w_ple_gate', 'ln2_g', 'ln2_b')
_SMALL_NAMES = ('ln1_g', 'ln1_b', 'ln2_g', 'ln2_b', 'q_norm_w', 'kv_norm_w', 'dn_norm_w', 'dn_a_log', 'dn_dt_bias')
_SMALL_GROUP = 8
_CONV_SMALL_ROW = len(_SMALL_NAMES) * _SMALL_GROUP
_CONV_SMALL_ROWS = DN_CONV * QKV_W // FLAT_W


def _pack_small(gw):
    rows = [jnp.pad(gw[n][None, :], ((0, _SMALL_GROUP - 1), (0, FLAT_W - gw[n].shape[0]))) for n in _SMALL_NAMES]
    rows.append(jnp.pad(gw['conv_w'].reshape(_CONV_SMALL_ROWS, FLAT_W), ((0, SMALL_ROWS - _CONV_SMALL_ROW - _CONV_SMALL_ROWS), (0, 0))))
    return jnp.concatenate(rows, axis=0)


class _Exchange:
    def __init__(self, local, me_chip, c_arr):
        self.local, self.me_chip, self.c_arr = local, me_chip, c_arr
        self.parts = self.arrived = None

    def gather_send(self):
        return _ride_gather_send(self.local)

    def gather_pass(self, sent):
        return _ride_gather_pass(sent)

    def weights(self, gathered):
        return _prep_weights(_unpack_rest(gathered, self.local, self.me_chip))

    def pair_send(self, g):
        self.gbufs = _pack_grads_rest(_unprep_grads_early(g))
        return _ride_pair_exchange(self.gbufs)

    def reduce_send(self, got):
        self.parts = [_pair_add(g_, r_, self.c_arr, tr, "pair_add_%d" % (i + 1))
                      for i, (g_, r_, tr) in enumerate(zip(self.gbufs, got, _ADD_TILES[1:]))]
        return _ride_chip_exchange(self.parts)

    def reduce_arrived(self, arrived):
        self.arrived = list(arrived)

    def in_send(self, g):
        g_in = [_shard_columns(_w_in_grad(g), W_IN_SHARD)]
        got = _reduce_pair_exchange(g_in, "reduce_pair_exchange_w_in")
        self.part_in = _pair_add(g_in[0], got[0], self.c_arr, _ADD_TILES[0], "pair_add_0")
        return _ride_chip_exchange([self.part_in])

    def in_arrived(self, arrived):
        self.arrived_in = list(arrived)


def kernel(x, p, positions, w_in, conv_w, dn_a_log, dn_dt_bias, dn_norm_w, q_norm_w, w_uq, kv_norm_w, w_uk, w_uv, w_br_dn, w_br_mla, w_o, ln1_g, ln1_b, w_ffn_in, w_ffn_out, w_ple, w_ple_gate, ln2_g, ln2_b, loss_target, m_w_in, m_conv_w, m_dn_a_log, m_dn_dt_bias, m_dn_norm_w, m_q_norm_w, m_w_uq, m_kv_norm_w, m_w_uk, m_w_uv, m_w_br_dn, m_w_br_mla, m_w_o, m_ln1_g, m_ln1_b, m_w_ffn_in, m_w_ffn_out, m_w_ple, m_w_ple_gate, m_ln2_g, m_ln2_b, v_w_in, v_conv_w, v_dn_a_log, v_dn_dt_bias, v_dn_norm_w, v_q_norm_w, v_w_uq, v_kv_norm_w, v_w_uk, v_w_uv, v_w_br_dn, v_w_br_mla, v_w_o, v_ln1_g, v_ln1_b, v_w_ffn_in, v_w_ffn_out, v_w_ple, v_w_ple_gate, v_ln2_g, v_ln2_b):
    ws = dict(w_in=w_in, conv_w=conv_w, dn_a_log=dn_a_log, dn_dt_bias=dn_dt_bias, dn_norm_w=dn_norm_w, q_norm_w=q_norm_w,
              w_uq=w_uq, kv_norm_w=kv_norm_w, w_uk=w_uk, w_uv=w_uv, w_br_dn=w_br_dn, w_br_mla=w_br_mla, w_o=w_o,
              ln1_g=ln1_g, ln1_b=ln1_b, w_ffn_in=w_ffn_in, w_ffn_out=w_ffn_out, w_ple=w_ple, w_ple_gate=w_ple_gate,
              ln2_g=ln2_g, ln2_b=ln2_b)
    ms = dict(w_in=m_w_in, conv_w=m_conv_w, dn_a_log=m_dn_a_log, dn_dt_bias=m_dn_dt_bias, dn_norm_w=m_dn_norm_w,
              q_norm_w=m_q_norm_w, w_uq=m_w_uq, kv_norm_w=m_kv_norm_w, w_uk=m_w_uk, w_uv=m_w_uv, w_br_dn=m_w_br_dn,
              w_br_mla=m_w_br_mla, w_o=m_w_o, ln1_g=m_ln1_g, ln1_b=m_ln1_b, w_ffn_in=m_w_ffn_in, w_ffn_out=m_w_ffn_out,
              w_ple=m_w_ple, w_ple_gate=m_w_ple_gate, ln2_g=m_ln2_g, ln2_b=m_ln2_b)
    vs = dict(w_in=v_w_in, conv_w=v_conv_w, dn_a_log=v_dn_a_log, dn_dt_bias=v_dn_dt_bias, dn_norm_w=v_dn_norm_w,
              q_norm_w=v_q_norm_w, w_uq=v_w_uq, kv_norm_w=v_kv_norm_w, w_uk=v_w_uk, w_uv=v_w_uv, w_br_dn=v_w_br_dn,
              w_br_mla=v_w_br_mla, w_o=v_w_o, ln1_g=v_ln1_g, ln1_b=v_ln1_b, w_ffn_in=v_w_ffn_in, w_ffn_out=v_w_ffn_out,
              w_ple=v_w_ple, w_ple_gate=v_w_ple_gate, ln2_g=v_ln2_g, ln2_b=v_ln2_b)
    mx, my, mc = lax.axis_index("x"), lax.axis_index("y"), lax.axis_index("c")

    me_chip = 2 * mx + my
    c_arr = jnp.reshape(mc, (1,)).astype(jnp.int32)
    me_arr = jnp.reshape(me_chip, (1,)).astype(jnp.int32)
    sharded = ('w_in', 'w_ffn_in') + tuple(name for name, _, _ in _FLATB_PIECES)

    local = _pack_shards({name: ws[name][0] for name in sharded}, conv_w[0])
    (gathered_in,) = _gather_shards(local[:1], "gather_w_in")
    w_in_full, conv_full = _unpack_w_in(gathered_in, local[0], me_chip)
    small = {n: ws[n][0] for n in _SMALL_NAMES}
    small['conv_w'] = conv_full
    sp = _prep_small(small)
    cosb, sinb = _rope_tables(positions[0])
    exch = _Exchange(local[1:], me_chip, c_arr)

    loss_lanes, dx, g = _local_step(x[0], p[0, 0], cosb, sinb, loss_target[0], _prep_w_in(w_in_full), sp, exch)
    gw = _unprep_grads_late(g)
    loss = lax.psum(jnp.sum(loss_lanes), ("x", "y", "c"))

    parts = [exch.part_in] + exch.parts
    arrived = exch.arrived_in + exch.arrived
    mine = [_chip_add(p_, r_, me_arr, tr, "chip_add_%d" % i) for i, (p_, r_, tr) in enumerate(zip(parts, arrived, _ADD_TILES))]
    reduced = _unpack_reduced(mine, _reduce_pair_share(mine, "reduce_pair_share"), mc)
    tot = _small_allreduce(_pack_small(gw))
    gred = {name: reduced[name][None] for name in sharded}
    for i, n in enumerate(_SMALL_NAMES):
        gred[n] = tot[i * _SMALL_GROUP, :ws[n].shape[1]][None]
    conv_tot = tot[_CONV_SMALL_ROW:_CONV_SMALL_ROW + _CONV_SMALL_ROWS].reshape(DN_CONV, QKV_W)
    gred['conv_w'] = lax.dynamic_slice_in_dim(conv_tot, (2 * mx + my) * _CONV_SHARD, _CONV_SHARD, axis=1)[None]

    deltas, new_m, new_v = {}, {}, {}
    for n in _WEIGHT_NAMES:
        if n == 'w_in':
            tr_ = lambda a: jnp.transpose(a, (0, 2, 1))
            g_t = tr_(gred[n].reshape(ws[n].shape))
            outs = _adamw(tr_(ws[n]), g_t, tr_(ms[n]), tr_(vs[n]), "adamw_" + n)
            gred[n] = tr_(g_t)
            deltas[n], new_m[n], new_v[n] = (tr_(o) for o in outs)
            continue
        gred[n] = gred[n].reshape(ws[n].shape)
        deltas[n], new_m[n], new_v[n] = _adamw(ws[n], gred[n], ms[n], vs[n], "adamw_" + n)
    return (loss, dx[None], *[gred[n] for n in _WEIGHT_NAMES], *[deltas[n] for n in _WEIGHT_NAMES],
            *[new_m[n] for n in _WEIGHT_NAMES], *[new_v[n] for n in _WEIGHT_NAMES])
```

```python
import functools
import math

import jax
import jax.numpy as jnp
from jax import lax
from jax.experimental import pallas as pl
from jax.experimental.pallas import tpu as pltpu

F32 = jnp.float32
_CDT = jnp.bfloat16
_HI = lax.Precision.HIGHEST
_MESH = pl.DeviceIdType.MESH

D_MODEL = 1024
PLE_DIM = 256
HEADS = 8
DN_DK = 128
DN_CHUNK = 64
DN_CONV = 4
QKV_W = 3 * HEADS * DN_DK
Q_LORA = 384
KV_LORA = 256
NOPE = 128
ROPE = 64
ROPE_PAD = 128
FFN_HIDDEN = 2816
D_IN = 6864
ROPE_BASE = 10000.0
ALPHA = 2.0 ** 0.25
ATT_SCALE = (NOPE + ROPE) ** -0.5
NEG_BIG = -1e30
ADAM_LR, ADAM_B1, ADAM_B2, ADAM_EPS, ADAM_WD, ADAM_STEP = 0.001, 0.9, 0.999, 1e-08, 0.01, 10

LANE = 128
VMEM_LIMIT = 56 * 1024 * 1024
MM_VMEM_BUDGET = 40 * 1024 * 1024
N_SHARD = 4
FLAT_W = 1024
SMALL_ROWS = 88


def _tile(dim, cap):
    if dim <= cap:
        return dim
    t = (cap // LANE) * LANE
    while t >= LANE:
        if dim % t == 0:
            return t
        t -= LANE
    return dim


def _cparams(sem):
    return pltpu.CompilerParams(dimension_semantics=sem, vmem_limit_bytes=VMEM_LIMIT)


def _mm(a, b, *, name, ta=False, tb=False, add=None, add_scale=1.0, out_dtype=F32, heads=None,
        a_head=None, b_head=None, out_head=None, dims=None, tm=1408, tn=1408):
    m, n, k = dims
    tm, tn = _tile(m, tm), _tile(n, tn)
    sa, sb, so = a.dtype.itemsize, b.dtype.itemsize, jnp.dtype(out_dtype).itemsize

    def vmem_need(tk_):
        acc = tm * tn * 4 if tk_ < k else 0
        extra = 2 * tm * tn * 4 if add is not None else 0
        return 2 * (tm * tk_ * sa + tk_ * tn * sb) + 2 * tm * tn * so + acc + extra

    tk = k
    while vmem_need(tk) > MM_VMEM_BUDGET and tk > LANE:
        smaller = _tile(k, tk - LANE)
        if smaller >= tk:
            break
        tk = smaller
    nk = k // tk
    hgrid = () if heads is None else (heads,)
    off = len(hgrid)

    def spec(rows, cols, rtile, ctile, rsel, csel, layout):
        def idx(*g):
            h = g[0] if off else 0
            ri, ci = g[off + rsel], g[off + csel]
            if layout == 'lead':
                return (h, ri, ci)
            if layout == 'col':
                return (ri, h * (cols // ctile) + ci)
            return (ri, ci)
        if layout == 'lead':
            return pl.BlockSpec((None, rtile, ctile), idx)
        return pl.BlockSpec((rtile, ctile), idx)

    a_spec = spec(k, m, tk, tm, 2, 0, a_head) if ta else spec(m, k, tm, tk, 0, 2, a_head)
    b_spec = spec(n, k, tn, tk, 1, 2, b_head) if tb else spec(k, n, tk, tn, 2, 1, b_head)
    o_spec = spec(m, n, tm, tn, 0, 1, out_head)
    in_specs = [a_spec, b_spec]
    args = [a, b]
    if add is not None:
        in_specs.append(spec(m, n, tm, tn, 0, 1, out_head))
        args.append(add)
    dn = (((0 if ta else 1,), (1 if tb else 0,)), ((), ()))

    def body(*refs):
        a_ref, b_ref = refs[0], refs[1]
        prod = lax.dot_general(a_ref[...].astype(_CDT), b_ref[...].astype(_CDT), dn, preferred_element_type=F32)
        if nk == 1:
            o_ref = refs[-1]
            if add is not None:
                prod = prod + refs[2][...].astype(F32) * add_scale
            o_ref[...] = prod.astype(out_dtype)
            return
        o_ref, acc_ref = refs[-2], refs[-1]
        kk = pl.program_id(off + 2)

        @pl.when(kk == 0)
        def _():
            if add is not None:
                acc_ref[...] = refs[2][...].astype(F32) * add_scale
            else:
                acc_ref[...] = jnp.zeros_like(acc_ref)

        acc_ref[...] += prod

        @pl.when(kk == nk - 1)
        def _():
            o_ref[...] = acc_ref[...].astype(out_dtype)

    if out_head == 'lead':
        oshape = (heads, m, n)
    elif out_head == 'col':
        oshape = (m, heads * n)
    else:
        oshape = (m, n)
    sem = ("parallel",) * (off + 2) + ("arbitrary",)
    return pl.pallas_call(
        body, name=name, grid=hgrid + (m // tm, n // tn, nk), in_specs=in_specs, out_specs=o_spec,
        out_shape=jax.ShapeDtypeStruct(oshape, out_dtype),
        scratch_shapes=[pltpu.VMEM((tm, tn), F32)] if nk > 1 else [],
        compiler_params=_cparams(sem))(*args)


def _mm2(a, b, **kw):
    ta, tb = kw.get('ta', False), kw.get('tb', False)
    m = a.shape[1] if ta else a.shape[0]
    k = a.shape[0] if ta else a.shape[1]
    n = b.shape[0] if tb else b.shape[1]
    return _mm(a, b, dims=(m, n, k), **kw)


def _rowwise(fn, rows, bcast, outs, reds=(), *, name, tm=256, heads=None):
    t = rows[0][0].shape[0]
    tm = min(tm, t)
    hn = 1 if heads is None else heads
    in_specs, args = [], []
    for arr, width, base, per_head in rows:
        in_specs.append(pl.BlockSpec((tm, width), functools.partial(
            lambda i, h, base, per_head: (i, base + (h if per_head else 0)), base=base, per_head=per_head)))
        args.append(arr)
    for arr in bcast:
        in_specs.append(pl.BlockSpec(arr.shape, lambda i, h: (0, 0)))
        args.append(arr)
    out_specs, out_shape = [], []
    for total, width, per_head, dt in outs:
        out_specs.append(pl.BlockSpec((tm, width), functools.partial(
            lambda i, h, per_head: (i, h if per_head else 0), per_head=per_head)))
        out_shape.append(jax.ShapeDtypeStruct((t, total), dt))
    for shp in reds:
        out_specs.append(pl.BlockSpec(shp, lambda i, h: (0, 0)))
        out_shape.append(jax.ShapeDtypeStruct(shp, F32))
    n_in, n_out, n_red = len(args), len(outs), len(reds)

    def body(*refs):
        i, h = pl.program_id(0), pl.program_id(1)
        vals = fn(h, *[r[...] for r in refs[:n_in]])
        for r, v in zip(refs[n_in:n_in + n_out], vals[:n_out]):
            r[...] = v.astype(r.dtype)
        if n_red:
            @pl.when((i == 0) & (h == 0))
            def _():
                for r in refs[n_in + n_out:]:
                    r[...] = jnp.zeros_like(r)
            for r, v in zip(refs[n_in + n_out:], vals[n_out:]):
                r[...] += v

    sem = ("arbitrary", "arbitrary") if n_red else ("parallel", "parallel")
    res = pl.pallas_call(body, name=name, grid=(t // tm, hn), in_specs=in_specs, out_specs=out_specs,
                         out_shape=out_shape, compiler_params=_cparams(sem))(*args)
    return tuple(res)


def _sigmoid(x):
    return 1.0 / (1.0 + jnp.exp(-x))


def _silu(x):
    return x * _sigmoid(x)


def _softplus(x):
    return jnp.maximum(x, 0.0) + jnp.log(1.0 + jnp.exp(-jnp.abs(x)))


def _layer_norm(t, g, b):
    mu = jnp.mean(t, axis=-1, keepdims=True)
    d = t - mu
    var = jnp.mean(d * d, axis=-1, keepdims=True)
    return d * lax.rsqrt(var + 1e-5) * g + b


def _rms_norm(t, w):
    return t * lax.rsqrt(jnp.mean(t * t, axis=-1, keepdims=True) + 1e-6) * w


def _swap_rope_halves(t):
    lane = lax.broadcasted_iota(jnp.int32, t.shape, 1) % ROPE_PAD
    n = t.shape[1]
    up = pltpu.roll(t, n - ROPE // 2, axis=1)
    dn = pltpu.roll(t, ROPE // 2, axis=1)
    return jnp.where(lane < ROPE // 2, up, jnp.where(lane < ROPE, dn, 0.0))


def _rope(t, cosb, sinb):
    reps = t.shape[1] // ROPE_PAD
    c = jnp.tile(cosb, (1, reps)) if reps > 1 else cosb
    s = jnp.tile(sinb, (1, reps)) if reps > 1 else sinb
    return t * c + _swap_rope_halves(t) * s


def _rope_bwd(d, cosb, sinb):
    reps = d.shape[1] // ROPE_PAD
    c = jnp.tile(cosb, (1, reps)) if reps > 1 else cosb
    s = jnp.tile(sinb, (1, reps)) if reps > 1 else sinb
    return d * c + _swap_rope_halves(d * s)


_CONV_ROWS = 256
_CONV_COLS = 256


def _conv_window(ref, r0, lo, hi, t):
    parts = []
    start, stop = r0 - lo, r0 + _CONV_ROWS + hi
    if start < 0:
        parts.append(jnp.zeros((-start, ref.shape[1]), F32))
        start = 0
    tail = max(stop - t, 0)
    parts.append(ref[start:stop - tail, :].astype(F32))
    if tail:
        parts.append(jnp.zeros((tail, ref.shape[1]), F32))
    return parts[0] if len(parts) == 1 else jnp.concatenate(parts, axis=0)


def _conv_taps(win, w_ref, n_out):
    acc = win[8:8 + n_out] * w_ref[DN_CONV - 1:DN_CONV, :]
    for i in range(DN_CONV - 1):
        acc = acc + pltpu.roll(win, DN_CONV - 1 - i, axis=0)[8:8 + n_out] * w_ref[i:i + 1, :]
    return acc


def _conv_silu(x, w):
    t, ch = x.shape

    def body(x_ref, w_ref, o_ref):
        for r in range(t // _CONV_ROWS):
            r0 = r * _CONV_ROWS
            c = _conv_taps(_conv_window(x_ref, r0, 8, 0, t), w_ref, _CONV_ROWS)
            o_ref[r0:r0 + _CONV_ROWS, :] = _silu(c)

    return pl.pallas_call(
        body, name="conv_silu", grid=(ch // _CONV_COLS,),
        in_specs=[pl.BlockSpec((t, _CONV_COLS), lambda j: (0, j)), pl.BlockSpec((DN_CONV, _CONV_COLS), lambda j: (0, j))],
        out_specs=pl.BlockSpec((t, _CONV_COLS), lambda j: (0, j)),
        out_shape=jax.ShapeDtypeStruct((t, ch), F32), compiler_params=_cparams(("parallel",)))(x, w)


def _conv_silu_bwd(x, w, dys):
    t, ch = x.shape
    per = ch // len(dys) // _CONV_COLS

    def body(x_ref, w_ref, *rest):
        dy_refs, (dx_ref, dw_ref) = rest[:len(dys)], rest[len(dys):]
        sec = pl.program_id(0) // per
        dws = [jnp.zeros((1, _CONV_COLS), F32) for _ in range(DN_CONV)]
        for r in range(t // _CONV_ROWS):
            r0 = r * _CONV_ROWS
            n_ext = _CONV_ROWS + 8
            xw = _conv_window(x_ref, r0, 8, 8, t)
            c = _conv_taps(xw, w_ref, n_ext)
            sg = _sigmoid(c)
            dy = _conv_window(dy_refs[-1], r0, 0, 8, t)
            for k in range(len(dys) - 2, -1, -1):
                dy = jnp.where(sec == k, _conv_window(dy_refs[k], r0, 0, 8, t), dy)
            ds = dy * (sg * (1.0 + c * (1.0 - sg)))
            x0 = xw[8:8 + _CONV_ROWS]
            dx = jnp.zeros((_CONV_ROWS, _CONV_COLS), F32)
            for i in range(DN_CONV):
                sh = DN_CONV - 1 - i
                ds_up = (ds if sh == 0 else pltpu.roll(ds, n_ext - sh, axis=0))[:_CONV_ROWS]
                dx = dx + ds_up * w_ref[i:i + 1, :]
                dws[i] = dws[i] + jnp.sum(x0 * ds_up, axis=0, keepdims=True)
            dx_ref[r0:r0 + _CONV_ROWS, :] = dx.astype(dx_ref.dtype)
        for i in range(DN_CONV):
            dw_ref[i:i + 1, :] = dws[i]

    blk = pl.BlockSpec((t, _CONV_COLS), lambda j: (0, j))
    wblk = pl.BlockSpec((DN_CONV, _CONV_COLS), lambda j: (0, j))
    dy_specs = [pl.BlockSpec((t, _CONV_COLS), functools.partial(lambda j, k: (0, jnp.clip(j - k * per, 0, per - 1)), k=k))
                for k in range(len(dys))]
    return pl.pallas_call(
        body, name="conv_silu_bwd", grid=(ch // _CONV_COLS,), in_specs=[blk, wblk] + dy_specs, out_specs=[blk, wblk],
        out_shape=[jax.ShapeDtypeStruct((t, ch), _CDT), jax.ShapeDtypeStruct((DN_CONV, ch), F32)],
        compiler_params=_cparams(("arbitrary",)))(x, w, *dys)


_PA_ROWS = 512


def _bmm(a, b, spec, exact=False):
    if exact:
        return jnp.einsum(spec, a, b, precision=_HI, preferred_element_type=F32)
    return jnp.einsum(spec, a.astype(_CDT), b.astype(_CDT), preferred_element_type=F32)


def _split16(a):
    hi = a.astype(jnp.bfloat16)
    return hi, (a - hi.astype(F32)).astype(jnp.bfloat16)


def _bmm3(a, b, spec):
    ah, al = _split16(a)
    bh, bl = _split16(b)
    e = lambda p, q: jnp.einsum(spec, p, q, preferred_element_type=F32)
    return e(ah, bh) + (e(ah, bl) + e(al, bh))


def _split3(b):
    b0 = b.astype(jnp.bfloat16)
    r1 = b - b0.astype(F32)
    b1 = r1.astype(jnp.bfloat16)
    return b0, b1, (r1 - b1.astype(F32)).astype(jnp.bfloat16)


@functools.partial(jax.custom_vjp, nondiff_argnums=(2, 3))
def _select_mm(sel, b, spec, spec_t):
    return sum(jnp.einsum(spec, sel, t, preferred_element_type=F32) for t in _split3(b))


def _select_mm_fwd(sel, b, spec, spec_t):
    return _select_mm(sel, b, spec, spec_t), sel


def _select_mm_bwd(spec, spec_t, sel, ct):
    return jnp.zeros_like(sel), sum(jnp.einsum(spec_t, sel, t, preferred_element_type=F32) for t in _split3(ct))


_select_mm.defvjp(_select_mm_fwd, _select_mm_bwd)


def _tri_inverse(l_mat, eye):
    pw = -l_mat
    t_inv = eye + pw
    for _ in range(5):
        pw = _bmm3(pw, pw, 'bij,bjk->bik')
        t_inv = t_inv + _bmm3(t_inv, pw, 'bij,bjk->bik')
    return t_inv


@jax.custom_vjp
def _tri_inverse_saved(l_mat, t_saved):
    return t_saved


def _tri_inverse_saved_fwd(l_mat, t_saved):
    return t_saved, t_saved


def _tri_inverse_saved_bwd(t_saved, dt):
    left = _bmm3(t_saved, dt, 'bji,bjk->bik')
    return -_bmm3(left, t_saved, 'bij,bkj->bik'), jnp.zeros_like(t_saved)


_tri_inverse_saved.defvjp(_tri_inverse_saved_fwd, _tri_inverse_saved_bwd)


def _phase_a(h, q, k, v, ba, alog, dtb, t_saved=None):
    r = q.shape[0]
    nb = r // DN_CHUNK
    c = DN_CHUNK
    lane = lax.broadcasted_iota(jnp.int32, (1, LANE), 1)
    selb = (lane == h).astype(F32)
    sela = (lane == h + HEADS).astype(F32)
    b_raw = jnp.sum(ba * selb, axis=1, keepdims=True)
    a_raw = jnp.sum(ba * sela, axis=1, keepdims=True)
    al = jnp.sum(alog * selb, axis=1, keepdims=True)
    dt = jnp.sum(dtb * selb, axis=1, keepdims=True)
    beta = jnp.broadcast_to(_sigmoid(b_raw), (r, LANE))
    g = jnp.broadcast_to(-jnp.exp(al) * _softplus(a_raw + dt), (r, LANE))
    qn = q * lax.rsqrt(jnp.sum(q * q, -1, keepdims=True) + 1e-6) * (DN_DK ** -0.5)
    kn = k * lax.rsqrt(jnp.sum(k * k, -1, keepdims=True) + 1e-6)
    q3, k3, v3 = qn.reshape(nb, c, LANE), kn.reshape(nb, c, LANE), v.reshape(nb, c, LANE)
    b3, g3 = beta.reshape(nb, c, LANE), g.reshape(nb, c, LANE)
    ri = lax.broadcasted_iota(jnp.int32, (nb, c, c), 1)
    ci = lax.broadcasted_iota(jnp.int32, (nb, c, c), 2)
    tril, strict = ri >= ci, ri > ci
    gc = _select_mm(tril.astype(jnp.bfloat16), g3, 'bij,bjd->bid', 'bij,bid->bjd')
    onehot = (lax.broadcasted_iota(jnp.int32, (nb, c, LANE), 2) == 0).astype(jnp.bfloat16)
    g_row = _select_mm(onehot, gc, 'bid,bjd->bij', 'bid,bij->bjd')
    diff = gc[:, :, :c] - g_row
    decay = jnp.where(tril, jnp.exp(jnp.where(tril, diff, 0.0)), 0.0)
    kb = k3 * b3
    l_mat = jnp.where(strict, _bmm(kb, k3, 'bid,bjd->bij') * decay, 0.0)
    if t_saved is None:
        t_inv = _tri_inverse(l_mat, (ri == ci).astype(F32))
    else:
        t_inv = _tri_inverse_saved(l_mat, t_saved.reshape(nb, c, c))
    eg = jnp.exp(gc)
    u = _bmm(t_inv, v3 * b3, 'bij,bje->bie')
    w = _bmm(t_inv, kb * eg, 'bij,bje->bie')
    intra = jnp.where(tril, _bmm(q3, k3, 'bid,bjd->bij') * decay, 0.0)
    qd = q3 * eg - _bmm(intra, w, 'bij,bjd->bid')
    au = _bmm(intra, u, 'bij,bje->bie')
    gl = jnp.sum(g3, axis=1, keepdims=True)
    kt = k3 * jnp.exp(gl - gc)
    outs = (u.reshape(r, LANE), w.reshape(r, LANE), qd.reshape(r, LANE), kt.reshape(r, LANE),
            au.reshape(r, LANE), gl.reshape(nb, LANE))
    return outs + (t_inv.reshape(r, c),) if t_saved is None else outs


def _pa_specs(t):
    rr = min(_PA_ROWS, t)
    nb = rr // DN_CHUNK
    qkv = [pl.BlockSpec((rr, LANE), functools.partial(lambda i, h, o: (i, o + h), o=o)) for o in (0, HEADS, 2 * HEADS)]
    ba = pl.BlockSpec((rr, LANE), lambda i, h: (i, 3))
    vec = pl.BlockSpec((1, LANE), lambda i, h: (0, 0))
    row = pl.BlockSpec((rr, LANE), lambda i, h: (i, h))
    intra = pl.BlockSpec((None, rr, DN_CHUNK), lambda i, h: (h, i, 0))
    gl = pl.BlockSpec((nb, LANE), lambda i, h: (i, h))
    return rr, qkv, ba, vec, row, intra, gl


def _grid_ends(grid):
    first = lambda: functools.reduce(jnp.logical_and, [pl.program_id(a) == 0 for a in range(len(grid))])
    last = lambda: functools.reduce(jnp.logical_and, [pl.program_id(a) == n - 1 for a, n in enumerate(grid)])
    return first, last


def _delta_local(qkv_act, pm, alog, dtb, rider=None):
    t = qkv_act.shape[0]
    rr, qkv, ba, vec, row, intra, gl = _pa_specs(t)

    def body(q, k, v, b, al, dt, *outs):
        vals = _phase_a(pl.program_id(1), q[...], k[...], v[...], b[...], al[...], dt[...])
        for o, val in zip(outs, vals):
            o[...] = val

    wide = jax.ShapeDtypeStruct((t, HEADS * LANE), F32)
    sq = jax.ShapeDtypeStruct((HEADS, t, DN_CHUNK), F32)
    grid = (t // rr, HEADS)
    return _carried_call(
        body, rider, *_grid_ends(grid), name="delta_local", grid=grid, in_specs=qkv + [ba, vec, vec],
        out_specs=[row] * 5 + [gl, intra],
        out_shape=[wide] * 5 + [jax.ShapeDtypeStruct((t // DN_CHUNK, HEADS * LANE), F32), sq],
        scratch_shapes=[], sem=("arbitrary", "arbitrary"), args=(qkv_act, qkv_act, qkv_act, pm, alog, dtb))


def _delta_local_bwd(qkv_act, pm, alog, dtb, t_inv, du, dw, dqd, dkt, dintra, dgl, rider=None):
    t = qkv_act.shape[0]
    rr, qkv, ba, vec, row, intra, gl = _pa_specs(t)

    def body(q, k, v, b, al, dt, ti, du_r, dw_r, dqd_r, dkt_r, di_r, dgl_r, dq_o, dk_o, dv_o, dba_o, dal_o, ddt_o):
        i, h = pl.program_id(0), pl.program_id(1)
        t_saved = ti[...]
        _, vjp = jax.vjp(lambda *a: _phase_a(h, *a, t_saved=t_saved), q[...], k[...], v[...], b[...], al[...], dt[...])
        dq, dk, dv, dba, dal, ddt = vjp((du_r[...], dw_r[...], dqd_r[...], dkt_r[...], di_r[...], dgl_r[...]))
        dq_o[...], dk_o[...], dv_o[...] = dq, dk, dv

        @pl.when(h == 0)
        def _():
            dba_o[...] = jnp.zeros_like(dba_o)

        @pl.when((h == 0) & (i == 0))
        def _():
            dal_o[...] = jnp.zeros_like(dal_o)
            ddt_o[...] = jnp.zeros_like(ddt_o)

        dba_o[...] += dba
        dal_o[...] += dal
        ddt_o[...] += ddt

    wide = jax.ShapeDtypeStruct((t, HEADS * LANE), F32)
    vshape = jax.ShapeDtypeStruct((1, LANE), F32)
    grid = (t // rr, HEADS)
    return _carried_call(
        body, rider, *_grid_ends(grid), name="delta_local_bwd", grid=grid,
        in_specs=qkv + [ba, vec, vec, intra] + [row] * 5 + [gl],
        out_specs=[row] * 3 + [pl.BlockSpec((rr, LANE), lambda i, h: (i, 0)), vec, vec],
        out_shape=[wide] * 3 + [jax.ShapeDtypeStruct((t, LANE), F32), vshape, vshape],
        scratch_shapes=[], sem=("arbitrary", "arbitrary"),
        args=(qkv_act, qkv_act, qkv_act, pm, alog, dtb, t_inv, du, dw, dqd, dkt, dintra, dgl))


_SCAN_ROWS = 512


def _dot(a, b, dn):
    return lax.dot_general(a.astype(_CDT), b.astype(_CDT), (dn, ((), ())), preferred_element_type=F32)


_NN = ((1,), (0,))
_NT = ((1,), (1,))
_TN = ((0,), (0,))


def _delta_scan(u, w, qd, kt, au, gl, rider=None):
    t = u.shape[0]
    rr = min(_SCAN_ROWS, t)
    nc = rr // DN_CHUNK

    def body(u_ref, w_ref, qd_ref, kt_ref, au_ref, gl_ref, o_ref, sall_ref, s_scr):
        @pl.when(pl.program_id(0) == 0)
        def _():
            s_scr[...] = jnp.zeros_like(s_scr)

        def chunk(c, carry):
            r0 = pl.multiple_of(c * DN_CHUNK, DN_CHUNK)
            rows = pl.ds(r0, DN_CHUNK)
            e = jnp.exp(gl_ref[pl.ds(c, 1), :])
            states = [s_scr[h] for h in range(HEADS)]
            u_c, w_c, qd_c, kt_c, au_c = u_ref[rows, :], w_ref[rows, :], qd_ref[rows, :], kt_ref[rows, :], au_ref[rows, :]
            o_new, s_new = [], []
            for h in range(HEADS):
                cs = slice(h * LANE, (h + 1) * LANE)
                s = states[h]
                both = _dot(jnp.concatenate([w_c[:, cs], qd_c[:, cs]], axis=0), s, _NN)
                v_new = u_c[:, cs] - both[:DN_CHUNK]
                o_new.append(both[DN_CHUNK:] + au_c[:, cs])
                s_new.append(s * e[:, cs] + _dot(kt_c[:, cs], v_new, _TN))
            o_ref[rows, :] = jnp.concatenate(o_new, axis=1)
            for h in range(HEADS):
                sall_ref[c, h] = states[h]
                s_scr[h] = s_new[h]
            return carry

        lax.fori_loop(0, nc, chunk, 0)

    row = pl.BlockSpec((rr, HEADS * LANE), lambda i: (i, 0))
    grid = (t // rr,)
    return _carried_call(
        body, rider, *_grid_ends(grid), name="delta_scan", grid=grid,
        in_specs=[row] * 5 + [pl.BlockSpec((nc, HEADS * LANE), lambda i: (i, 0))],
        out_specs=[row, pl.BlockSpec((nc, HEADS, LANE, LANE), lambda i: (i, 0, 0, 0))],
        out_shape=[jax.ShapeDtypeStruct((t, HEADS * LANE), F32),
                   jax.ShapeDtypeStruct((t // DN_CHUNK, HEADS, LANE, LANE), F32)],
        scratch_shapes=[pltpu.VMEM((HEADS, LANE, LANE), F32)], sem=("arbitrary",), args=(u, w, qd, kt, au, gl))


def _delta_scan_bwd(u, w, qd, kt, gl, sall, do, rider=None):
    t = u.shape[0]
    rr = min(_SCAN_ROWS, t)
    nc = rr // DN_CHUNK
    ng = t // rr

    def body(u_ref, w_ref, qd_ref, kt_ref, gl_ref, sall_ref, do_ref,
             du_ref, dw_ref, dqd_ref, dkt_ref, dgl_ref, ds_scr):
        @pl.when(pl.program_id(0) == 0)
        def _():
            ds_scr[...] = jnp.zeros_like(ds_scr)

        def chunk(cc, carry):
            c = nc - 1 - cc
            r0 = pl.multiple_of(c * DN_CHUNK, DN_CHUNK)
            rows = pl.ds(r0, DN_CHUNK)
            e = jnp.exp(gl_ref[pl.ds(c, 1), :])
            states = [sall_ref[c, h] for h in range(HEADS)]
            ds_outs = [ds_scr[h] for h in range(HEADS)]
            u_a, w_a, kt_a, qd_a, do_a = u_ref[rows, :], w_ref[rows, :], kt_ref[rows, :], qd_ref[rows, :], do_ref[rows, :]
            dqd, dkt, du, dw, dgl, ds_new = [], [], [], [], [], []
            for h in range(HEADS):
                cs = slice(h * LANE, (h + 1) * LANE)
                s, ds_out = states[h], ds_outs[h]
                w_c, kt_c, qd_c, do_c = w_a[:, cs], kt_a[:, cs], qd_a[:, cs], do_a[:, cs]
                v_new = u_a[:, cs] - _dot(w_c, s, _NN)
                dv_new = _dot(kt_c, ds_out, _NN)
                cots = jnp.concatenate([do_c, dv_new], axis=0)
                both = _dot(cots, s, _NT)
                dqd.append(both[:DN_CHUNK])
                dw.append(-both[DN_CHUNK:])
                dkt.append(_dot(v_new, ds_out, _NT))
                du.append(dv_new)
                eh = e[:, cs]
                dgl.append(jnp.broadcast_to(jnp.sum(ds_out * s, axis=0, keepdims=True) * eh, (8, LANE)))
                ds_new.append(ds_out * eh + _dot(jnp.concatenate([qd_c, -w_c], axis=0), cots, _TN))
            cat = lambda parts: jnp.concatenate(parts, axis=1)
            dqd_ref[rows, :], dkt_ref[rows, :], du_ref[rows, :], dw_ref[rows, :] = cat(dqd), cat(dkt), cat(du), cat(dw)
            dgl_ref[pl.ds(pl.multiple_of(c * 8, 8), 8), :] = cat(dgl)
            for h in range(HEADS):
                ds_scr[h] = ds_new[h]
            return carry

        lax.fori_loop(0, nc, chunk, 0)

    rev = lambda i: (ng - 1 - i, 0)
    row = pl.BlockSpec((rr, HEADS * LANE), rev)
    gl_spec = pl.BlockSpec((nc, HEADS * LANE), rev)
    wide = jax.ShapeDtypeStruct((t, HEADS * LANE), F32)
    outs, carried = _carried_call(
        body, rider, *_grid_ends((ng,)), name="delta_scan_bwd", grid=(ng,),
        in_specs=[row] * 4 + [gl_spec, pl.BlockSpec((nc, HEADS, LANE, LANE), lambda i: (ng - 1 - i, 0, 0, 0)), row],
        out_specs=[row] * 4 + [pl.BlockSpec((nc * 8, HEADS * LANE), rev)],
        out_shape=[wide] * 4 + [jax.ShapeDtypeStruct((t // DN_CHUNK * 8, HEADS * LANE), F32)],
        scratch_shapes=[pltpu.VMEM((HEADS, LANE, LANE), F32)], sem=("arbitrary",), args=(u, w, qd, kt, gl, sall, do))
    return tuple(outs[:4]) + (outs[4].reshape(t // DN_CHUNK, 8, HEADS * LANE)[:, 0, :],), carried


_ATT_TILE = 512


def _kv_rows(j, tk):
    return pl.ds(pl.multiple_of(j * tk, tk), tk)


def _att_scores(ql, qr, ckv_ref, kr_ref, j, tk):
    ks = _kv_rows(j, tk)
    return (_dot(ql, ckv_ref[ks, :], _NT) + _dot(qr, kr_ref[ks, :], _NT)) * ATT_SCALE


def _diag_mask(s):
    qi = lax.broadcasted_iota(jnp.int32, s.shape, 0)
    ki = lax.broadcasted_iota(jnp.int32, s.shape, 1)
    return jnp.where(ki <= qi, s, NEG_BIG)


def _attention(ql, qr, ckv, kr):
    t = ckv.shape[0]
    tq = min(_ATT_TILE, t)

    nl = tq // LANE

    def lane_fold(v, op):
        out = v[:, :LANE]
        for k in range(1, nl):
            out = op(out, v[:, k * LANE:(k + 1) * LANE])
        return out

    def body(ql_ref, qr_ref, ckv_ref, kr_ref, o_ref, lse_ref, s_all, m_lanes, l_lanes, acc_scr):
        qi = pl.program_id(1)
        q_lat, q_rope = ql_ref[...], qr_ref[...]
        m_lanes[...] = jnp.full_like(m_lanes, NEG_BIG)

        def scores(j, masked):
            s = _att_scores(q_lat, q_rope, ckv_ref, kr_ref, j, tq)
            if masked:
                s = _diag_mask(s)
            s_all[j] = s
            m_lanes[...] = jnp.maximum(m_lanes[...], lane_fold(s, jnp.maximum))

        def scores_body(j, carry):
            scores(j, False)
            return carry

        lax.fori_loop(0, qi, scores_body, 0)
        scores(qi, True)
        m = jnp.max(m_lanes[...], axis=-1, keepdims=True)
        mb = jnp.broadcast_to(m, (tq, LANE))
        l_lanes[...] = jnp.zeros_like(l_lanes)
        acc_scr[...] = jnp.zeros_like(acc_scr)

        def weigh(j, carry):
            s = s_all[j]
            p = jnp.concatenate([jnp.exp(s[:, k * LANE:(k + 1) * LANE] - mb) for k in range(nl)], axis=1)
            l_lanes[...] += lane_fold(p, jnp.add)
            acc_scr[...] += _dot(p, ckv_ref[_kv_rows(j, tq), :], _NN)
            return carry

        lax.fori_loop(0, qi + 1, weigh, 0)
        l = jnp.sum(l_lanes[...], axis=-1, keepdims=True)
        o_ref[...] = acc_scr[...] / l
        lse_ref[...] = m + jnp.log(l)

    return pl.pallas_call(
        body, name="attention", grid=(HEADS, t // tq),
        in_specs=[pl.BlockSpec((None, tq, KV_LORA), lambda h, i: (h, i, 0)),
                  pl.BlockSpec((tq, ROPE_PAD), lambda h, i: (i, h)),
                  pl.BlockSpec((t, KV_LORA), lambda h, i: (0, 0)),
                  pl.BlockSpec((t, ROPE_PAD), lambda h, i: (0, 0))],
        out_specs=[pl.BlockSpec((None, tq, KV_LORA), lambda h, i: (h, i, 0)),
                   pl.BlockSpec((None, tq, 1), lambda h, i: (h, i, 0))],
        out_shape=[jax.ShapeDtypeStruct((HEADS, t, KV_LORA), F32), jax.ShapeDtypeStruct((HEADS, t, 1), F32)],
        scratch_shapes=[pltpu.VMEM((t // tq, tq, tq), F32), pltpu.VMEM((tq, LANE), F32), pltpu.VMEM((tq, LANE), F32),
                        pltpu.VMEM((tq, KV_LORA), F32)],
        compiler_params=_cparams(("parallel", "parallel")))(ql, qr, ckv, kr)


def _attention_bwd(ql, qr, ckv, kr, out, lse, dout):
    t = ckv.shape[0]
    tq = min(_ATT_TILE, t)

    def body(ql_ref, qr_ref, ckv_ref, kr_ref, o_ref, lse_ref, do_ref, dql_ref, dqr_ref, dckv_ref, dkr_ref,
             dql_scr, dqr_scr):
        h, qi = pl.program_id(0), pl.program_id(1)

        @pl.when((h == 0) & (qi == 0))
        def _():
            dckv_ref[...] = jnp.zeros_like(dckv_ref)
            dkr_ref[...] = jnp.zeros_like(dkr_ref)

        q_lat, q_rope = ql_ref[...], qr_ref[...]
        d_o = do_ref[...].astype(_CDT)
        lse_v = lse_ref[...]
        dsum = jnp.sum(do_ref[...] * o_ref[...], axis=-1, keepdims=True)
        dql_scr[...] = jnp.zeros_like(dql_scr)
        dqr_scr[...] = jnp.zeros_like(dqr_scr)

        def step(j, masked):
            ks = _kv_rows(j, tq)
            s = _att_scores(q_lat, q_rope, ckv_ref, kr_ref, j, tq)
            if masked:
                s = _diag_mask(s)
            p = jnp.exp(s - lse_v)
            kv = ckv_ref[ks, :]
            ds = (p * (_dot(d_o, kv, _NT) - dsum) * ATT_SCALE).astype(_CDT)
            pb = p.astype(_CDT)
            dql_scr[...] += _dot(ds, kv, _NN)
            dqr_scr[...] += _dot(ds, kr_ref[ks, :], _NN)
            dckv_ref[ks, :] += _dot(pb, d_o, _TN) + _dot(ds, q_lat, _TN)
            dkr_ref[ks, :] += _dot(ds, q_rope, _TN)

        def loop_body(j, carry):
            step(j, False)
            return carry

        lax.fori_loop(0, qi, loop_body, 0)
        step(qi, True)
        dql_ref[...] = dql_scr[...].astype(dql_ref.dtype)
        dqr_ref[...] = dqr_scr[...]

    lat = pl.BlockSpec((None, tq, KV_LORA), lambda h, i: (h, i, 0))
    rope = pl.BlockSpec((tq, ROPE_PAD), lambda h, i: (i, h))
    kfull = pl.BlockSpec((t, KV_LORA), lambda h, i: (0, 0))
    rfull = pl.BlockSpec((t, ROPE_PAD), lambda h, i: (0, 0))
    return pl.pallas_call(
        body, name="attention_bwd", grid=(HEADS, t // tq),
        in_specs=[lat, rope, kfull, rfull, lat, pl.BlockSpec((None, tq, 1), lambda h, i: (h, i, 0)), lat],
        out_specs=[lat, rope, kfull, rfull],
        out_shape=[jax.ShapeDtypeStruct((HEADS, t, KV_LORA), _CDT), jax.ShapeDtypeStruct((t, HEADS * ROPE_PAD), F32),
                   jax.ShapeDtypeStruct((t, KV_LORA), F32), jax.ShapeDtypeStruct((t, ROPE_PAD), F32)],
        scratch_shapes=[pltpu.VMEM((tq, KV_LORA), F32), pltpu.VMEM((tq, ROPE_PAD), F32)],
        compiler_params=_cparams(("arbitrary", "arbitrary")))(ql, qr, ckv, kr, out, lse, dout)


_DX_ROWS = 256


def _dx_fused(pairs, add, add_scale, rider=None):
    t, d = add.shape
    tm = min(_DX_ROWS, t)
    n = len(pairs)

    def body(*refs):
        acc = refs[2 * n][...] * add_scale
        for i in range(n):
            acc = acc + _dot(refs[i][...], refs[n + i][...], _NT)
        refs[2 * n + 1][...] = acc

    in_specs = [pl.BlockSpec((tm, a.shape[1]), lambda i: (i, 0)) for a, _ in pairs]
    in_specs += [pl.BlockSpec(w.shape, lambda i: (0, 0)) for _, w in pairs]
    row = pl.BlockSpec((tm, d), lambda i: (i, 0))
    grid = (t // tm,)
    (dx,), carried = _carried_call(
        body, rider, *_grid_ends(grid), name="b_dx", grid=grid, in_specs=in_specs + [row], out_specs=[row],
        out_shape=[jax.ShapeDtypeStruct((t, d), F32)], scratch_shapes=[], sem=("arbitrary",),
        args=tuple(a for a, _ in pairs) + tuple(w for _, w in pairs) + (add,))
    return dx, carried


def _ffn_in_swiglu(a, w):
    t, k = a.shape
    hid = w.shape[1] // 2
    tm, tn = _tile(t, 1024), _tile(hid, 1408)
    nj = hid // tn

    def body(a_ref, bg_ref, bu_ref, act_ref, gt_ref, up_ref):
        av = a_ref[...].astype(_CDT)
        gt = jnp.dot(av, bg_ref[...].astype(_CDT), preferred_element_type=F32)
        up = jnp.dot(av, bu_ref[...].astype(_CDT), preferred_element_type=F32)
        act_ref[...] = _swiglu(gt, up).astype(act_ref.dtype)
        gt_ref[...] = gt.astype(gt_ref.dtype)
        up_ref[...] = up.astype(up_ref.dtype)

    out = pl.BlockSpec((tm, tn), lambda i, j: (i, j))
    return pl.pallas_call(
        body, name="f_ffn_in_swiglu", grid=(t // tm, nj),
        in_specs=[pl.BlockSpec((tm, k), lambda i, j: (i, 0)), pl.BlockSpec((k, tn), lambda i, j: (0, j)),
                  pl.BlockSpec((k, tn), lambda i, j: (0, nj + j))],
        out_specs=[out] * 3, out_shape=[jax.ShapeDtypeStruct((t, hid), _CDT)] * 3,
        compiler_params=_cparams(("parallel", "parallel")))(a, w, w)


def _gated_norm(o, z, w):
    return _rms_norm(o, w) * _silu(z)


def _gated_norm_heads(o, z, w):
    heads = [_gated_norm(o[:, h * LANE:(h + 1) * LANE], z[:, h * LANE:(h + 1) * LANE], w) for h in range(HEADS)]
    return jnp.concatenate(heads, axis=1)


def _mla_pre(ckv, krp, cq, cosb, sinb, qw, kw):
    return _rms_norm(cq, qw), _rms_norm(ckv, kw), _rope(krp, cosb, sinb)


def _merge(gg, y_dn, y_mla):
    return _sigmoid(gg[:, :D_MODEL]) * y_dn + _sigmoid(gg[:, D_MODEL:]) * y_mla


def _ln1(xv, attn_out, g, b):
    return _layer_norm(ALPHA * xv + attn_out, g, b)


def _final(h1, ffn, gate_pre, ple_proj, g, b):
    return _layer_norm(ALPHA * h1 + ffn + _sigmoid(gate_pre) * ple_proj, g, b)


def _swiglu(gt, up):
    return _silu(gt) * up


def _local_step(x, p, cosb, sinb, target, wt, sp, exch):
    t = x.shape[0]
    bf = _CDT
    xb = x.astype(bf)
    g = {}

    qkv_pre = _mm2(xb, wt['qkv'], name="f_qkv")
    z = _mm2(xb, wt['z'], name="f_z")
    gg = _mm2(xb, wt['gg'], name="f_gg")
    pm = _mm2(xb, wt['mla'], name="f_mla")
    qkv_act = _conv_silu(qkv_pre, sp['conv_w'])
    (u, w_, qd, kt, au, gl, t_inv), sent = _delta_local(qkv_act, pm, sp['a_log'], sp['dt_bias'], rider=exch.gather_send())
    (o_dn, sall), passed = _delta_scan(u, w_, qd, kt, au, gl, rider=exch.gather_pass(sent))
    wt = dict(wt, **exch.weights(passed))
    (og,) = _rowwise(lambda h, o, zz, w: (_gated_norm_heads(o, zz, w),),
                     [(o_dn, D_MODEL, 0, False), (z, D_MODEL, 0, False)], [sp['dn_norm_w']],
                     [(D_MODEL, D_MODEL, False, bf)], name="f_gated_norm")
    y_dn = _mm2(og, wt['br_dn'], name="f_br_dn")

    c_q, c_kv, k_rope = _rowwise(
        lambda h, *a: _mla_pre(*a),
        [(pm, KV_LORA, 0, False), (pm, ROPE_PAD, 2, False), (pm, Q_LORA, 2, False),
         (cosb, ROPE_PAD, 0, False), (sinb, ROPE_PAD, 0, False)],
        [sp['q_norm_w'], sp['kv_norm_w']],
        [(Q_LORA, Q_LORA, False, bf), (KV_LORA, KV_LORA, False, bf), (ROPE_PAD, ROPE_PAD, False, bf)], name="f_mla_pre")
    q_nope = _mm2(c_q, wt['uq_nope'], name="f_uq_nope", out_dtype=bf)
    q_rope_pre = _mm2(c_q, wt['uq_rope'], name="f_uq_rope")
    (q_rope,) = _rowwise(lambda h, q, c, s: (_rope(q, c, s),),
                         [(q_rope_pre, HEADS * ROPE_PAD, 0, False), (cosb, ROPE_PAD, 0, False), (sinb, ROPE_PAD, 0, False)],
                         [], [(HEADS * ROPE_PAD, HEADS * ROPE_PAD, False, bf)], name="f_q_rope")
    q_lat = _mm(q_nope, wt['uk'], name="f_q_lat", tb=True, heads=HEADS, a_head='col', b_head='lead', out_head='lead',
                dims=(t, KV_LORA, NOPE), out_dtype=bf)
    out_lat, lse = _attention(q_lat, q_rope, c_kv, k_rope)
    o_mla = _mm(out_lat, wt['uv'], name="f_o_mla", heads=HEADS, a_head='lead', b_head='lead', out_head='col',
                dims=(t, NOPE, KV_LORA), out_dtype=bf)
    y_mla = _mm2(o_mla, wt['br_mla'], name="f_br_mla")

    (mixed,) = _rowwise(lambda h, *a: (_merge(*a),),
                        [(gg, 2 * D_MODEL, 0, False), (y_dn, D_MODEL, 0, False), (y_mla, D_MODEL, 0, False)],
                        [], [(D_MODEL, D_MODEL, False, bf)], name="f_merge")
    attn_out = _mm2(mixed, wt['o'], name="f_o")
    h1, h1b = _rowwise(lambda h, *a: (_ln1(*a),) * 2, [(x, D_MODEL, 0, False), (attn_out, D_MODEL, 0, False)],
                       [sp['ln1_g'], sp['ln1_b']], [(D_MODEL, D_MODEL, False, F32), (D_MODEL, D_MODEL, False, bf)],
                       name="f_ln1")
    act, ffn_gt, ffn_up = _ffn_in_swiglu(h1b, wt['ffn_in'])
    ffn = _mm2(act, wt['ffn_out'], name="f_ffn_out")
    gate_pre = _mm2(h1b, wt['ple_gate'], name="f_ple_gate")
    pb = p.astype(bf)
    ple_proj = _mm2(pb, wt['ple'], name="f_ple")

    def final_fn(h, h1v, ffnv, gpv, ppv, tgt, gv, bv):
        y, vjp = jax.vjp(_final, h1v, ffnv, gpv, ppv, gv, bv)
        err = y - tgt
        dh1, dffn, dgp, dpp, dg, db = vjp(err * (1.0 / D_MODEL))
        sq = err * err
        lanes = sq[:, :LANE]
        for j in range(1, D_MODEL // LANE):
            lanes = lanes + sq[:, j * LANE:(j + 1) * LANE]
        loss = jnp.sum(lanes, axis=0, keepdims=True) * (0.5 / D_MODEL)
        return dffn, dffn, dgp, dpp, dg, db, loss

    dpre2, dpre2b, dgate_pre, dple_proj, g['ln2_g'], g['ln2_b'], loss_lanes = _rowwise(
        final_fn, [(a, D_MODEL, 0, False) for a in (h1, ffn, gate_pre, ple_proj, target)],
        [sp['ln2_g'], sp['ln2_b']],
        [(D_MODEL, D_MODEL, False, F32)] + [(D_MODEL, D_MODEL, False, bf)] * 3,
        [(1, D_MODEL), (1, D_MODEL), (1, LANE)], name="b_final")

    g['ple'] = _mm2(pb, dple_proj, ta=True, name="g_ple")
    g['ple_gate'] = _mm2(h1b, dgate_pre, ta=True, name="g_ple_gate")
    g['ffn_out'] = _mm2(act, dpre2b, ta=True, name="g_ffn_out")
    dact = _mm2(dpre2b, wt['ffn_out'], tb=True, name="b_dact", out_dtype=bf)

    def swiglu_bwd(h, gt, up, d):
        _, vjp = jax.vjp(_swiglu, gt.astype(F32), up.astype(F32))
        dgt, dup = vjp(d.astype(F32))
        return (jnp.concatenate([dgt, dup], axis=1),)

    (dffn_in,) = _rowwise(swiglu_bwd, [(ffn_gt, FFN_HIDDEN, 0, False), (ffn_up, FFN_HIDDEN, 0, False),
                                       (dact, FFN_HIDDEN, 0, False)], [],
                          [(2 * FFN_HIDDEN, 2 * FFN_HIDDEN, False, bf)], name="b_swiglu")
    g['ffn_in'] = _mm2(h1b, dffn_in, ta=True, name="g_ffn_in")
    dh1 = _mm2(dffn_in, wt['ffn_in'], tb=True, name="b_dh1_ffn", add=dpre2, add_scale=ALPHA)
    dh1 = _mm2(dgate_pre, wt['ple_gate'], tb=True, name="b_dh1_gate", add=dh1)

    def ln1_bwd(h, xv, ao, d, gv, bv):
        _, vjp = jax.vjp(_ln1, xv, ao, gv, bv)
        _, dao, dg, db = vjp(d)
        return dao, dao, dg, db

    dpre1, dpre1b, g['ln1_g'], g['ln1_b'] = _rowwise(
        ln1_bwd, [(x, D_MODEL, 0, False), (attn_out, D_MODEL, 0, False), (dh1, D_MODEL, 0, False)],
        [sp['ln1_g'], sp['ln1_b']], [(D_MODEL, D_MODEL, False, F32), (D_MODEL, D_MODEL, False, bf)],
        [(1, D_MODEL), (1, D_MODEL)], name="b_ln1")

    g['o'] = _mm2(mixed, dpre1b, ta=True, name="g_o")
    dmixed = _mm2(dpre1b, wt['o'], tb=True, name="b_dmixed")

    def merge_bwd(h, ggv, yd, ym, d):
        _, vjp = jax.vjp(_merge, ggv, yd, ym)
        return vjp(d)

    dgg, dy_dn, dy_mla = _rowwise(
        merge_bwd, [(gg, 2 * D_MODEL, 0, False), (y_dn, D_MODEL, 0, False), (y_mla, D_MODEL, 0, False),
                    (dmixed, D_MODEL, 0, False)], [],
        [(2 * D_MODEL, 2 * D_MODEL, False, bf), (D_MODEL, D_MODEL, False, bf), (D_MODEL, D_MODEL, False, bf)],
        name="b_merge")
    g['br_dn'] = _mm2(og, dy_dn, ta=True, name="g_br_dn")
    dog = _mm2(dy_dn, wt['br_dn'], tb=True, name="b_dog")
    g['br_mla'] = _mm2(o_mla, dy_mla, ta=True, name="g_br_mla")
    do_mla = _mm2(dy_mla, wt['br_mla'], tb=True, name="b_do_mla", out_dtype=bf)

    dout_lat = _mm(do_mla, wt['uv'], name="b_dout_lat", tb=True, heads=HEADS, a_head='col', b_head='lead',
                   out_head='lead', dims=(t, KV_LORA, NOPE))
    g['uv'] = _mm(out_lat, do_mla, name="g_uv", ta=True, heads=HEADS, a_head='lead', b_head='col', out_head='lead',
                  dims=(KV_LORA, NOPE, t))
    dq_lat, dq_rope, dckv_att, dkr_att = _attention_bwd(q_lat, q_rope, c_kv, k_rope, out_lat, lse, dout_lat)
    dq_nope = _mm(dq_lat, wt['uk'], name="b_dq_nope", heads=HEADS, a_head='lead', b_head='lead', out_head='col',
                  dims=(t, NOPE, KV_LORA), out_dtype=bf)
    g['uk'] = _mm(dq_lat, q_nope, name="g_uk", ta=True, heads=HEADS, a_head='lead', b_head='col', out_head='lead',
                  dims=(KV_LORA, NOPE, t))
    (dq_rope_pre,) = _rowwise(lambda h, d, c, s: (_rope_bwd(d, c, s),),
                              [(dq_rope, HEADS * ROPE_PAD, 0, False), (cosb, ROPE_PAD, 0, False), (sinb, ROPE_PAD, 0, False)],
                              [], [(HEADS * ROPE_PAD, HEADS * ROPE_PAD, False, bf)], name="b_q_rope")
    g['uq_nope'] = _mm2(c_q, dq_nope, ta=True, name="g_uq_nope")
    g['uq_rope'] = _mm2(c_q, dq_rope_pre, ta=True, name="g_uq_rope")
    dc_q = _mm2(dq_nope, wt['uq_nope'], tb=True, name="b_dcq_nope")
    dc_q = _mm2(dq_rope_pre, wt['uq_rope'], tb=True, name="b_dcq_rope", add=dc_q)

    def gated_norm_bwd(h, o, zz, d, w):
        _, vjp = jax.vjp(_gated_norm_heads, o, zz, w)
        return vjp(d)

    do_dn, dz, g['dn_norm_w'] = _rowwise(
        gated_norm_bwd, [(o_dn, D_MODEL, 0, False), (z, D_MODEL, 0, False), (dog, D_MODEL, 0, False)], [sp['dn_norm_w']],
        [(D_MODEL, D_MODEL, False, F32), (D_MODEL, D_MODEL, False, bf)], [(1, LANE)], name="b_gated_norm")
    (du, dw, dqd, dkt, dgl), paired = _delta_scan_bwd(u, w_, qd, kt, gl, sall, do_dn, rider=exch.pair_send(g))
    (dq_a, dk_a, dv_a, dba, g['a_log'], g['dt_bias']), arrived = _delta_local_bwd(
        qkv_act, pm, sp['a_log'], sp['dt_bias'], t_inv, du, dw, dqd, dkt, do_dn, dgl, rider=exch.reduce_send(paired))
    exch.reduce_arrived(arrived)
    dqkv_pre, g['conv_w'] = _conv_silu_bwd(qkv_pre, sp['conv_w'], [dq_a, dk_a, dv_a])

    def mla_pre_bwd(h, ckv, krp, cq, cosv, sinv, dcq, dckv, dkr, qw, kw):
        _, vjp = jax.vjp(lambda a, b, c, d, e: (_rms_norm(c, d), _rms_norm(a, e)), ckv, krp, cq, qw, kw)
        dckv_p, _, dcq_p, dqw, dkw = vjp((dcq, dckv))
        dkr_p = _rope_bwd(dkr, cosv, sinv)
        dpm = jnp.concatenate([dckv_p, dkr_p, jnp.zeros((ckv.shape[0], 3 * LANE), F32), dcq_p], axis=1)
        return dpm, dqw, dkw

    dpm_main, g['q_norm_w'], g['kv_norm_w'] = _rowwise(
        mla_pre_bwd,
        [(pm, KV_LORA, 0, False), (pm, ROPE_PAD, 2, False), (pm, Q_LORA, 2, False),
         (cosb, ROPE_PAD, 0, False), (sinb, ROPE_PAD, 0, False),
         (dc_q, Q_LORA, 0, False), (dckv_att, KV_LORA, 0, False), (dkr_att, ROPE_PAD, 0, False)],
        [sp['q_norm_w'], sp['kv_norm_w']], [(1152, 1152, False, F32)], [(1, Q_LORA), (1, KV_LORA)], name="b_mla_pre")
    (dpm,) = _rowwise(
        lambda h, a, b: (jnp.concatenate([a[:, :3 * LANE], b, a[:, 4 * LANE:]], axis=1),),
        [(dpm_main, 1152, 0, False), (dba, LANE, 0, False)], [], [(1152, 1152, False, bf)], name="b_dpm")

    g['qkv'] = _mm2(xb, dqkv_pre, ta=True, name="g_qkv")
    g['z'] = _mm2(xb, dz, ta=True, name="g_z")
    g['gg'] = _mm2(xb, dgg, ta=True, name="g_gg")
    g['mla'] = _mm2(xb, dpm, ta=True, name="g_mla")
    dx, arrived = _dx_fused([(dqkv_pre, wt['qkv']), (dz, wt['z']), (dgg, wt['gg']), (dpm, wt['mla'])], dpre1, ALPHA,
                            rider=exch.in_send(g))
    exch.in_arrived(arrived)
    return loss_lanes, dx, g


_IN_SIZES = (QKV_W, HEADS * DN_DK, HEADS, HEADS, Q_LORA, KV_LORA, ROPE, D_MODEL, D_MODEL)


def _rope_tables(positions):
    inv_freq = ROPE_BASE ** (-jnp.arange(0, ROPE, 2, dtype=F32) / ROPE)
    ang = positions.astype(F32)[:, None] * inv_freq
    cos, sin = jnp.cos(ang), jnp.sin(ang)
    zeros = jnp.zeros((positions.shape[0], ROPE_PAD - ROPE), F32)
    return jnp.concatenate([cos, cos, zeros], axis=1), jnp.concatenate([-sin, sin, zeros], axis=1)


def _prep_w_in(w_in):
    dt = w_in.dtype
    offs = [0]
    for s in _IN_SIZES:
        offs.append(offs[-1] + s)
    qkv, z, wb, wa, cq, ckv, kr, gd, gm = [w_in[:, offs[i]:offs[i + 1]] for i in range(len(_IN_SIZES))]
    zc = lambda n: jnp.zeros((D_MODEL, n), dt)
    return {
        'qkv': qkv, 'z': z, 'gg': jnp.concatenate([gd, gm], axis=1),
        'mla': jnp.concatenate([ckv, kr, zc(ROPE_PAD - ROPE), wb, wa, zc(LANE - 2 * HEADS), zc(2 * LANE), cq], axis=1),
    }


def _prep_weights(full):
    w_uq = full['w_uq']
    wt = {
        'uq_nope': w_uq[:, :, :NOPE].reshape(Q_LORA, HEADS * NOPE),
        'uq_rope': jnp.pad(w_uq[:, :, NOPE:], ((0, 0), (0, 0), (0, ROPE_PAD - ROPE))).reshape(Q_LORA, HEADS * ROPE_PAD),
        'uk': jnp.transpose(full['w_uk'], (1, 0, 2)), 'uv': jnp.transpose(full['w_uv'], (1, 0, 2)),
        'br_dn': full['w_br_dn'], 'br_mla': full['w_br_mla'], 'o': full['w_o'], 'ffn_in': full['w_ffn_in'],
        'ffn_out': full['w_ffn_out'], 'ple': full['w_ple'], 'ple_gate': full['w_ple_gate'],
    }
    return wt


def _prep_small(small):
    pad = lambda v: jnp.pad(v, (0, LANE - v.shape[0]))[None, :]
    return {
        'conv_w': small['conv_w'], 'a_log': pad(small['dn_a_log']), 'dt_bias': pad(small['dn_dt_bias']),
        'dn_norm_w': small['dn_norm_w'][None, :], 'q_norm_w': small['q_norm_w'][None, :],
        'kv_norm_w': small['kv_norm_w'][None, :], 'ln1_g': small['ln1_g'][None, :], 'ln1_b': small['ln1_b'][None, :],
        'ln2_g': small['ln2_g'][None, :], 'ln2_b': small['ln2_b'][None, :],
    }


def _w_in_grad(g):
    mla = g['mla']
    ba0 = KV_LORA + ROPE_PAD
    cq0 = ba0 + 3 * LANE
    return jnp.concatenate([
        g['qkv'], g['z'], mla[:, ba0:ba0 + HEADS], mla[:, ba0 + HEADS:ba0 + 2 * HEADS], mla[:, cq0:cq0 + Q_LORA],
        mla[:, :KV_LORA], mla[:, KV_LORA:KV_LORA + ROPE], g['gg']], axis=1)


def _unprep_grads_late(g):
    return {
        'conv_w': g['conv_w'], 'dn_a_log': g['a_log'][0, :HEADS], 'dn_dt_bias': g['dt_bias'][0, :HEADS],
        'dn_norm_w': g['dn_norm_w'][0], 'q_norm_w': g['q_norm_w'][0], 'kv_norm_w': g['kv_norm_w'][0],
        'ln1_g': g['ln1_g'][0], 'ln1_b': g['ln1_b'][0], 'ln2_g': g['ln2_g'][0], 'ln2_b': g['ln2_b'][0],
    }


def _unprep_grads_early(g):
    w_uq = jnp.concatenate([g['uq_nope'].reshape(Q_LORA, HEADS, NOPE),
                            g['uq_rope'].reshape(Q_LORA, HEADS, ROPE_PAD)[:, :, :ROPE]], axis=2)
    return {
        'w_uq': w_uq, 'w_uk': jnp.transpose(g['uk'], (1, 0, 2)), 'w_uv': jnp.transpose(g['uv'], (1, 0, 2)),
        'w_br_dn': g['br_dn'], 'w_br_mla': g['br_mla'], 'w_o': g['o'],
        'w_ffn_in': g['ffn_in'], 'w_ffn_out': g['ffn_out'], 'w_ple': g['ple'], 'w_ple_gate': g['ple_gate'],
    }


_FLATB_PIECES = (
    ('w_ffn_out', 704, (704, D_MODEL)), ('w_br_dn', 256, (256, D_MODEL)), ('w_br_mla', 256, (256, D_MODEL)),
    ('w_o', 256, (256, D_MODEL)), ('w_ple_gate', 256, (256, D_MODEL)), ('w_uq', 144, (96, HEADS, NOPE + ROPE)),
    ('w_uk', 64, (64, HEADS, NOPE)), ('w_uv', 64, (64, HEADS, NOPE)), ('w_ple', 64, (PLE_DIM, 256)),
)
FLATB_ROWS = 2112
W_IN_SHARD = D_IN // N_SHARD
FFN_IN_SHARD = 2 * FFN_HIDDEN // N_SHARD
A_ROWS = D_MODEL + 32
_CONV_SHARD = QKV_W // N_SHARD
_ADD_TILES = (256, 256, 352)


def _flatb_offsets():
    offs, o = {}, 0
    for name, rows, _ in _FLATB_PIECES:
        offs[name] = o
        o += rows
    return offs, o


def _pack_shards(ws, conv_w):
    conv_bits = lax.bitcast_convert_type(conv_w, jnp.bfloat16).reshape(DN_CONV, 2 * _CONV_SHARD).astype(_CDT)
    tail = jnp.pad(conv_bits, ((0, A_ROWS - D_MODEL - DN_CONV), (0, W_IN_SHARD - 2 * _CONV_SHARD)))
    a_buf = jnp.concatenate([ws['w_in'].astype(_CDT), tail], axis=0)
    parts = [ws[name].astype(_CDT).reshape(rows, FLAT_W) for name, rows, _ in _FLATB_PIECES]
    used = sum(p.shape[0] for p in parts)
    parts.append(jnp.zeros((FLATB_ROWS - used, FLAT_W), _CDT))
    return [a_buf, ws['w_ffn_in'].astype(_CDT), jnp.concatenate(parts, axis=0)]


def _unpack_w_in(gathered, local, me):
    a = [jnp.where(me == s, local, gathered[s]) for s in range(N_SHARD)]
    conv = [lax.bitcast_convert_type(
        p[D_MODEL:D_MODEL + DN_CONV, :2 * _CONV_SHARD].astype(jnp.bfloat16).reshape(DN_CONV, _CONV_SHARD, 2), F32) for p in a]
    return jnp.concatenate([p[:D_MODEL] for p in a], axis=1), jnp.concatenate(conv, axis=1)


def _unpack_rest(gathered, local, me):
    pick = lambda b, s: jnp.where(me == s, local[b], gathered[b][s])
    full = {'w_ffn_in': jnp.concatenate([pick(0, s) for s in range(N_SHARD)], axis=1)}
    offs, _ = _flatb_offsets()
    fb = [pick(1, s) for s in range(N_SHARD)]
    for name, rows, shape in _FLATB_PIECES:
        pieces = [p[offs[name]:offs[name] + rows].reshape(shape) for p in fb]
        full[name] = jnp.concatenate(pieces, axis=1 if name == 'w_ple' else 0)
    return full


def _shard_columns(g, w):
    return jnp.stack([g[:, s * w:(s + 1) * w] for s in range(N_SHARD)])


def _pack_grads_rest(gw):
    parts = []
    for name, rows, _ in _FLATB_PIECES:
        g = gw[name]
        if name == 'w_ple':
            parts.append(_shard_columns(g, PLE_DIM).reshape(N_SHARD, rows, FLAT_W))
        else:
            parts.append(g.reshape(N_SHARD, rows, FLAT_W))
    used = sum(p.shape[1] for p in parts)
    parts.append(jnp.zeros((N_SHARD, FLATB_ROWS - used, FLAT_W), F32))
    return [_shard_columns(gw['w_ffn_in'], FFN_IN_SHARD), jnp.concatenate(parts, axis=1)]


def _unpack_reduced(mine, theirs, c):
    whole = [jnp.concatenate([jnp.where(c == 0, m, t), jnp.where(c == 0, t, m)], axis=0) for m, t in zip(mine, theirs)]
    out = {'w_in': whole[0], 'w_ffn_in': whole[1]}
    offs, _ = _flatb_offsets()
    for name, rows, shape in _FLATB_PIECES:
        out[name] = whole[2][offs[name]:offs[name] + rows].reshape(shape)
    return out


_HBM = pl.BlockSpec(memory_space=pltpu.HBM)


def _place():
    x, y, c = lax.axis_index("x"), lax.axis_index("y"), lax.axis_index("c")
    chips = [(1 - x, y), (x, 1 - y), (1 - x, 1 - y)]
    return x, y, c, chips


def _remote(src, dst, send_sems, recv_sems, k, to):
    return pltpu.make_async_remote_copy(src_ref=src, dst_ref=dst, send_sem=send_sems.at[k], recv_sem=recv_sems.at[k],
                                        device_id=to, device_id_type=_MESH)


def _half_rows(ref, half, hf, lead=None):
    rows = pl.ds(pl.multiple_of(hf * half, 16), half)
    return ref.at[rows, :] if lead is None else ref.at[lead, rows, :]


class _Rider:
    def __init__(self, inputs, out_shape, n_sems, copies, aliases=None):
        self.inputs, self.out_shape, self.n_sems, self.copies = list(inputs), list(out_shape), n_sems, copies
        self.aliases = aliases or {}


def _carried_call(body, rider, first, last, *, name, grid, in_specs, out_specs, out_shape, scratch_shapes, sem, args):
    n_in, n_out, n_scr = len(in_specs), len(out_specs), len(scratch_shapes)
    if rider is None:
        res = pl.pallas_call(body, name=name, grid=grid, in_specs=in_specs, out_specs=out_specs, out_shape=out_shape,
                             scratch_shapes=scratch_shapes, compiler_params=_cparams(sem))(*args)
        return list(res), []
    ri, ro = len(rider.inputs), len(rider.out_shape)

    def full_body(*refs):
        own_in, r_in = refs[:n_in], refs[n_in:n_in + ri]
        o0 = n_in + ri
        own_out, r_out = refs[o0:o0 + n_out], refs[o0 + n_out:o0 + n_out + ro]
        s0 = o0 + n_out + ro
        own_scr, send_sems, recv_sems = refs[s0:s0 + n_scr], refs[s0 + n_scr], refs[s0 + n_scr + 1]

        @pl.when(first())
        def _():
            sends, _ = rider.copies(r_in, r_out, send_sems, recv_sems)
            for cp in sends:
                cp.start()

        body(*own_in, *own_out, *own_scr)

        @pl.when(last())
        def _():
            sends, arrivals = rider.copies(r_in, r_out, send_sems, recv_sems)
            for cp in arrivals():
                cp.wait_recv()
            for cp in sends:
                cp.wait_send()

    res = pl.pallas_call(
        full_body, name=name, grid=grid, in_specs=list(in_specs) + [_HBM] * ri, out_specs=list(out_specs) + [_HBM] * ro,
        out_shape=list(out_shape) + rider.out_shape,
        scratch_shapes=list(scratch_shapes) + [pltpu.SemaphoreType.DMA((rider.n_sems,))] * 2,
        input_output_aliases={n_in + i: n_out + o for i, o in rider.aliases.items()},
        compiler_params=_cparams(sem))(*args, *rider.inputs)
    return list(res[:n_out]), list(res[n_out:])


def _ride_gather_send(bufs):
    n = len(bufs)
    halves = [b.shape[0] // 2 for b in bufs]

    def copies(ins, outs, send_sems, recv_sems):
        x, y, c, chips = _place()
        slot = lambda b, cx, cy: _half_rows(outs[b], halves[b], c, lead=2 * cx + cy)
        sends = [_remote(_half_rows(ins[b], halves[b], c), slot(b, x, y), send_sems, recv_sems, 3 * b + j, (cx, cy, c))
                 for b in range(n) for j, (cx, cy) in enumerate(chips)]
        arrivals = lambda: [_remote(slot(b, cx, cy), slot(b, cx, cy), send_sems, recv_sems, 3 * b + j, (x, y, c))
                            for b in range(n) for j, (cx, cy) in enumerate(chips)]
        return sends, arrivals

    return _Rider(bufs, [jax.ShapeDtypeStruct((N_SHARD,) + b.shape, b.dtype) for b in bufs], 3 * n, copies)


def _ride_gather_pass(gathered):
    n = len(gathered)
    halves = [g.shape[1] // 2 for g in gathered]

    def copies(ins, outs, send_sems, recv_sems):
        x, y, c, chips = _place()
        slot = lambda b, cx, cy, hf: _half_rows(outs[b], halves[b], hf, lead=2 * cx + cy)
        sends = [_remote(slot(b, cx, cy, c), slot(b, cx, cy, c), send_sems, recv_sems, 3 * b + j, (x, y, 1 - c))
                 for b in range(n) for j, (cx, cy) in enumerate(chips)]
        arrivals = lambda: [_remote(slot(b, cx, cy, 1 - c), slot(b, cx, cy, 1 - c), send_sems, recv_sems, 3 * b + j, (x, y, c))
                            for b in range(n) for j, (cx, cy) in enumerate(chips)]
        return sends, arrivals

    return _Rider(gathered, [jax.ShapeDtypeStruct(g.shape, g.dtype) for g in gathered], 3 * n, copies,
                  aliases={b: b for b in range(n)})


def _ride_pair_exchange(gbufs):
    n = len(gbufs)
    halves = [g.shape[1] // 2 for g in gbufs]

    def copies(ins, outs, send_sems, recv_sems):
        x, y, c, _ = _place()
        sends = [_remote(ins[b].at[:, pl.ds(pl.multiple_of((1 - c) * halves[b], 16), halves[b]), :], outs[b],
                         send_sems, recv_sems, b, (x, y, 1 - c)) for b in range(n)]
        arrivals = lambda: [_remote(outs[b], outs[b], send_sems, recv_sems, b, (x, y, c)) for b in range(n)]
        return sends, arrivals

    return _Rider(gbufs, [jax.ShapeDtypeStruct((N_SHARD, h, g.shape[2]), g.dtype) for g, h in zip(gbufs, halves)], n, copies)


def _ride_chip_exchange(parts):
    n = len(parts)

    def copies(ins, outs, send_sems, recv_sems):
        x, y, c, chips = _place()
        sends = [_remote(ins[b].at[2 * cx + cy], outs[b].at[j], send_sems, recv_sems, 3 * b + j, (cx, cy, c))
                 for b in range(n) for j, (cx, cy) in enumerate(chips)]
        arrivals = lambda: [_remote(ins[b].at[0], outs[b].at[j], send_sems, recv_sems, 3 * b + j, (x, y, c))
                            for b in range(n) for j in range(len(chips))]
        return sends, arrivals

    return _Rider(parts, [jax.ShapeDtypeStruct((3,) + p.shape[1:], p.dtype) for p in parts], 3 * n, copies)


def _gather_shards(bufs, name):
    n = len(bufs)
    halves = [b.shape[0] // 2 for b in bufs]

    def body(*refs):
        ins, outs, send_sems, recv_sems = refs[:n], refs[n:2 * n], refs[2 * n], refs[2 * n + 1]
        x, y, c, chips = _place()
        me, sibling = (x, y, c), (x, y, 1 - c)
        slot = lambda b, cx, cy, hf: _half_rows(outs[b], halves[b], hf, lead=2 * cx + cy)
        first = [_remote(_half_rows(ins[b], halves[b], c), slot(b, x, y, c), send_sems, recv_sems, 6 * b + j, (cx, cy, c))
                 for b in range(n) for j, (cx, cy) in enumerate(chips)]
        for cp in first:
            cp.start()
        passed = []
        for j, (cx, cy) in enumerate(chips):
            for b in range(n):
                _remote(slot(b, cx, cy, c), slot(b, cx, cy, c), send_sems, recv_sems, 6 * b + j, me).wait_recv()
                fwd = _remote(slot(b, cx, cy, c), slot(b, cx, cy, c), send_sems, recv_sems, 6 * b + 3 + j, sibling)
                fwd.start()
                passed.append(fwd)
        for j, (cx, cy) in enumerate(chips):
            for b in range(n):
                _remote(slot(b, cx, cy, 1 - c), slot(b, cx, cy, 1 - c), send_sems, recv_sems, 6 * b + 3 + j, me).wait_recv()
        for cp in first + passed:
            cp.wait_send()

    return pl.pallas_call(
        body, name=name, out_shape=[jax.ShapeDtypeStruct((N_SHARD,) + b.shape, b.dtype) for b in bufs],
        in_specs=[_HBM] * n, out_specs=[_HBM] * n,
        scratch_shapes=[pltpu.SemaphoreType.DMA((6 * n,)), pltpu.SemaphoreType.DMA((6 * n,))],
    )(*bufs)


def _reduce_pair_exchange(gbufs, name):
    n = len(gbufs)
    halves = [g.shape[1] // 2 for g in gbufs]

    def body(*refs):
        ins, outs, send_sems, recv_sems = refs[:n], refs[n:2 * n], refs[2 * n], refs[2 * n + 1]
        x, y, c, _ = _place()
        cps = [_remote(ins[b].at[:, pl.ds(pl.multiple_of((1 - c) * halves[b], 16), halves[b]), :], outs[b],
                       send_sems, recv_sems, b, (x, y, 1 - c)) for b in range(n)]
        for cp in cps:
            cp.start()
        for cp in cps:
            cp.wait()

    return pl.pallas_call(
        body, name=name,
        out_shape=[jax.ShapeDtypeStruct((N_SHARD, h, g.shape[2]), g.dtype) for g, h in zip(gbufs, halves)],
        in_specs=[_HBM] * n, out_specs=[_HBM] * n,
        scratch_shapes=[pltpu.SemaphoreType.DMA((n,)), pltpu.SemaphoreType.DMA((n,))],
    )(*gbufs)


def _pair_add(gbuf, recv, c_arr, tr, name):
    _, rows, width = gbuf.shape
    half = rows // 2
    nt = half // tr

    def body(c_ref, a_ref, b_ref, o_ref):
        o_ref[...] = (a_ref[...] + b_ref[...]).astype(o_ref.dtype)

    blk = lambda f: pl.BlockSpec((None, tr, width), f)
    return pl.pallas_call(
        body, name=name, out_shape=jax.ShapeDtypeStruct((N_SHARD, half, width), jnp.bfloat16),
        grid_spec=pltpu.PrefetchScalarGridSpec(
            num_scalar_prefetch=1, grid=(N_SHARD, nt),
            in_specs=[blk(lambda s, i, c: (s, c[0] * nt + i, 0)), blk(lambda s, i, c: (s, i, 0))],
            out_specs=blk(lambda s, i, c: (s, i, 0))),
        compiler_params=_cparams(("parallel", "parallel")))(c_arr, gbuf, recv)


def _reduce_chip_exchange(parts, name):
    n = len(parts)

    def body(*refs):
        ins, outs, send_sems, recv_sems = refs[:n], refs[n:2 * n], refs[2 * n], refs[2 * n + 1]
        x, y, c, chips = _place()
        sends = [_remote(ins[b].at[2 * cx + cy], outs[b].at[j], send_sems, recv_sems, 3 * b + j, (cx, cy, c))
                 for b in range(n) for j, (cx, cy) in enumerate(chips)]
        for cp in sends:
            cp.start()
        for b in range(n):
            for j in range(len(chips)):
                _remote(ins[b].at[0], outs[b].at[j], send_sems, recv_sems, 3 * b + j, (x, y, c)).wait_recv()
        for cp in sends:
            cp.wait_send()

    return pl.pallas_call(
        body, name=name, out_shape=[jax.ShapeDtypeStruct((3,) + p.shape[1:], p.dtype) for p in parts],
        in_specs=[_HBM] * n, out_specs=[_HBM] * n,
        scratch_shapes=[pltpu.SemaphoreType.DMA((3 * n,)), pltpu.SemaphoreType.DMA((3 * n,))],
    )(*parts)


def _chip_add(part, recv, me_arr, tr, name):
    _, half, width = part.shape

    def body(me_ref, own, a0, a1, a2, o_ref):
        f = lambda r: r[...].astype(F32)
        o_ref[...] = ((f(own) + f(a0)) + f(a1)) + f(a2)

    specs = [pl.BlockSpec((None, tr, width), lambda i, me: (me[0], i, 0))]
    specs += [pl.BlockSpec((None, tr, width), functools.partial(lambda i, me, k: (k, i, 0), k=k)) for k in range(3)]
    return pl.pallas_call(
        body, name=name, out_shape=jax.ShapeDtypeStruct((half, width), F32),
        grid_spec=pltpu.PrefetchScalarGridSpec(
            num_scalar_prefetch=1, grid=(half // tr,), in_specs=specs,
            out_specs=pl.BlockSpec((tr, width), lambda i, me: (i, 0))),
        compiler_params=_cparams(("parallel",)))(me_arr, part, recv, recv, recv)


def _reduce_pair_share(rhalves, name):
    n = len(rhalves)

    def body(*refs):
        ins, outs, send_sems, recv_sems = refs[:n], refs[n:2 * n], refs[2 * n], refs[2 * n + 1]
        x, y, c, _ = _place()
        cps = [_remote(ins[b], outs[b], send_sems, recv_sems, b, (x, y, 1 - c)) for b in range(n)]
        for cp in cps:
            cp.start()
        for cp in cps:
            cp.wait()

    return pl.pallas_call(
        body, name=name, out_shape=[jax.ShapeDtypeStruct(r.shape, r.dtype) for r in rhalves],
        in_specs=[_HBM] * n, out_specs=[_HBM] * n,
        scratch_shapes=[pltpu.SemaphoreType.DMA((n,)), pltpu.SemaphoreType.DMA((n,))],
    )(*rhalves)


def _small_allreduce(buf):
    r, width = buf.shape
    n_dev = 8

    def body(x_ref, all_ref, sum_ref, send_sems, recv_sems, local_sem):
        x, y, c, chips = _place()
        me, sibling = (x, y, c), (x, y, 1 - c)

        def rows(px, py, pc):
            return all_ref.at[pl.ds(pl.multiple_of((4 * px + 2 * py + pc) * r, 8), r), :]

        def copy(k, block, to, src=None):
            return _remote(rows(*block) if src is None else src, rows(*block), send_sems, recv_sems, k, to)

        mine = pltpu.make_async_copy(x_ref, rows(*me), local_sem)
        mine.start()
        first = [copy(0, me, sibling, src=x_ref)]
        first += [copy(1 + j, me, (*chip, c), src=x_ref) for j, chip in enumerate(chips)]
        for cp in first:
            cp.start()
        passed = [copy(4 + j, (*chip, c), sibling) for j, chip in enumerate(chips)]
        for j, chip in enumerate(chips):
            copy(1 + j, (*chip, c), me).wait_recv()
            passed[j].start()
        copy(0, sibling, me).wait_recv()
        for j, chip in enumerate(chips):
            copy(4 + j, (*chip, 1 - c), me).wait_recv()
        for cp in first + passed:
            cp.wait_send()
        mine.wait()
        total = all_ref[0:r, :]
        for k in range(1, n_dev):
            total = total + all_ref[k * r:(k + 1) * r, :]
        sum_ref[...] = total

    vm = pl.BlockSpec(memory_space=pltpu.VMEM)
    _, total = pl.pallas_call(
        body, name="small_allreduce",
        out_shape=[jax.ShapeDtypeStruct((n_dev * r, width), buf.dtype), jax.ShapeDtypeStruct((r, width), buf.dtype)],
        in_specs=[vm], out_specs=[vm, vm],
        scratch_shapes=[pltpu.SemaphoreType.DMA((7,)), pltpu.SemaphoreType.DMA((7,)), pltpu.SemaphoreType.DMA],
    )(buf)
    return total


def _row_tile(rows, cap):
    if rows <= cap:
        return rows
    t = (cap // 8) * 8
    while t >= 8:
        if rows % t == 0:
            return t
        t -= 8
    return rows


def _adamw(w, g, m, v, name):
    shape = w.shape
    cols = shape[-1] if len(shape) <= 3 else shape[-2] * shape[-1]
    lead = len(shape) == 3
    w2, g2, m2, v2 = (a if lead else a.reshape(-1, cols) for a in (w, g, m, v))
    rows = shape[1] if lead else w2.shape[0]
    tr, tc = _row_tile(rows, 256), cols
    if tr == rows and rows > 256:
        tc = _tile(cols, 256)

    def body(w_ref, g_ref, m_ref, v_ref, d_ref, mo_ref, vo_ref):
        gv = g_ref[...]
        mn = ADAM_B1 * m_ref[...] + (1.0 - ADAM_B1) * gv
        vn = ADAM_B2 * v_ref[...] + (1.0 - ADAM_B2) * (gv * gv)
        m_hat = mn / (1.0 - ADAM_B1 ** ADAM_STEP)
        v_hat = vn / (1.0 - ADAM_B2 ** ADAM_STEP)
        d_ref[...] = -ADAM_LR * (m_hat / (jnp.sqrt(v_hat) + ADAM_EPS) + ADAM_WD * w_ref[...])
        mo_ref[...] = mn
        vo_ref[...] = vn

    blk = (pl.BlockSpec((None, tr, tc), lambda i, j: (0, i, j)) if lead else pl.BlockSpec((tr, tc), lambda i, j: (i, j)))
    outs = pl.pallas_call(
        body, name=name, grid=(rows // tr, cols // tc), in_specs=[blk] * 4, out_specs=[blk] * 3,
        out_shape=[jax.ShapeDtypeStruct(w2.shape, F32)] * 3,
        compiler_params=_cparams(("parallel", "parallel")))(w2, g2, m2, v2)
    return tuple(o.reshape(shape) for o in outs)


_WEIGHT_NAMES = ('w_in', 'conv_w', 'dn_a_log', 'dn_dt_bias', 'dn_norm_w', 'q_norm_w', 'w_uq', 'kv_norm_w', 'w_uk',
                 'w_uv', 'w_br_dn', 'w_br_mla', 'w_o', 'ln1_g', 'ln1_b', 'w_ffn_in', 'w_ffn_out', 'w_ple',
                 'w_ple_gate', 'ln2_g', 'ln2_b')
_SMALL_NAMES = ('ln1_g', 'ln1_b', 'ln2_g', 'ln2_b', 'q_norm_w', 'kv_norm_w', 'dn_norm_w', 'dn_a_log', 'dn_dt_bias')
_SMALL_GROUP = 8
_CONV_SMALL_ROW = len(_SMALL_NAMES) * _SMALL_GROUP
_CONV_SMALL_ROWS = DN_CONV * QKV_W // FLAT_W


def _pack_small(gw):
    rows = [jnp.pad(gw[n][None, :], ((0, _SMALL_GROUP - 1), (0, FLAT_W - gw[n].shape[0]))) for n in _SMALL_NAMES]
    rows.append(jnp.pad(gw['conv_w'].reshape(_CONV_SMALL_ROWS, FLAT_W), ((0, SMALL_ROWS - _CONV_SMALL_ROW - _CONV_SMALL_ROWS), (0, 0))))
    return jnp.concatenate(rows, axis=0)


class _Exchange:
    def __init__(self, local, me_chip, c_arr):
        self.local, self.me_chip, self.c_arr = local, me_chip, c_arr
        self.parts = self.arrived = None

    def gather_send(self):
        return _ride_gather_send(self.local)

    def gather_pass(self, sent):
        return _ride_gather_pass(sent)

    def weights(self, gathered):
        return _prep_weights(_unpack_rest(gathered, self.local, self.me_chip))

    def pair_send(self, g):
        self.gbufs = _pack_grads_rest(_unprep_grads_early(g))
        return _ride_pair_exchange(self.gbufs)

    def reduce_send(self, got):
        self.parts = [_pair_add(g_, r_, self.c_arr, tr, "pair_add_%d" % (i + 1))
                      for i, (g_, r_, tr) in enumerate(zip(self.gbufs, got, _ADD_TILES[1:]))]
        return _ride_chip_exchange(self.parts)

    def reduce_arrived(self, arrived):
        self.arrived = list(arrived)

    def in_send(self, g):
        g_in = [_shard_columns(_w_in_grad(g), W_IN_SHARD)]
        got = _reduce_pair_exchange(g_in, "reduce_pair_exchange_w_in")
        self.part_in = _pair_add(g_in[0], got[0], self.c_arr, _ADD_TILES[0], "pair_add_0")
        return _ride_chip_exchange([self.part_in])

    def in_arrived(self, arrived):
        self.arrived_in = list(arrived)


def kernel(x, p, positions, w_in, conv_w, dn_a_log, dn_dt_bias, dn_norm_w, q_norm_w, w_uq, kv_norm_w, w_uk, w_uv, w_br_dn, w_br_mla, w_o, ln1_g, ln1_b, w_ffn_in, w_ffn_out, w_ple, w_ple_gate, ln2_g, ln2_b, loss_target, m_w_in, m_conv_w, m_dn_a_log, m_dn_dt_bias, m_dn_norm_w, m_q_norm_w, m_w_uq, m_kv_norm_w, m_w_uk, m_w_uv, m_w_br_dn, m_w_br_mla, m_w_o, m_ln1_g, m_ln1_b, m_w_ffn_in, m_w_ffn_out, m_w_ple, m_w_ple_gate, m_ln2_g, m_ln2_b, v_w_in, v_conv_w, v_dn_a_log, v_dn_dt_bias, v_dn_norm_w, v_q_norm_w, v_w_uq, v_kv_norm_w, v_w_uk, v_w_uv, v_w_br_dn, v_w_br_mla, v_w_o, v_ln1_g, v_ln1_b, v_w_ffn_in, v_w_ffn_out, v_w_ple, v_w_ple_gate, v_ln2_g, v_ln2_b):
    ws = dict(w_in=w_in, conv_w=conv_w, dn_a_log=dn_a_log, dn_dt_bias=dn_dt_bias, dn_norm_w=dn_norm_w, q_norm_w=q_norm_w,
              w_uq=w_uq, kv_norm_w=kv_norm_w, w_uk=w_uk, w_uv=w_uv, w_br_dn=w_br_dn, w_br_mla=w_br_mla, w_o=w_o,
              ln1_g=ln1_g, ln1_b=ln1_b, w_ffn_in=w_ffn_in, w_ffn_out=w_ffn_out, w_ple=w_ple, w_ple_gate=w_ple_gate,
              ln2_g=ln2_g, ln2_b=ln2_b)
    ms = dict(w_in=m_w_in, conv_w=m_conv_w, dn_a_log=m_dn_a_log, dn_dt_bias=m_dn_dt_bias, dn_norm_w=m_dn_norm_w,
              q_norm_w=m_q_norm_w, w_uq=m_w_uq, kv_norm_w=m_kv_norm_w, w_uk=m_w_uk, w_uv=m_w_uv, w_br_dn=m_w_br_dn,
              w_br_mla=m_w_br_mla, w_o=m_w_o, ln1_g=m_ln1_g, ln1_b=m_ln1_b, w_ffn_in=m_w_ffn_in, w_ffn_out=m_w_ffn_out,
              w_ple=m_w_ple, w_ple_gate=m_w_ple_gate, ln2_g=m_ln2_g, ln2_b=m_ln2_b)
    vs = dict(w_in=v_w_in, conv_w=v_conv_w, dn_a_log=v_dn_a_log, dn_dt_bias=v_dn_dt_bias, dn_norm_w=v_dn_norm_w,
              q_norm_w=v_q_norm_w, w_uq=v_w_uq, kv_norm_w=v_kv_norm_w, w_uk=v_w_uk, w_uv=v_w_uv, w_br_dn=v_w_br_dn,
              w_br_mla=v_w_br_mla, w_o=v_w_o, ln1_g=v_ln1_g, ln1_b=v_ln1_b, w_ffn_in=v_w_ffn_in, w_ffn_out=v_w_ffn_out,
              w_ple=v_w_ple, w_ple_gate=v_w_ple_gate, ln2_g=v_ln2_g, ln2_b=v_ln2_b)
    mx, my, mc = lax.axis_index("x"), lax.axis_index("y"), lax.axis_index("c")

    me_chip = 2 * mx + my
    c_arr = jnp.reshape(mc, (1,)).astype(jnp.int32)
    me_arr = jnp.reshape(me_chip, (1,)).astype(jnp.int32)
    sharded = ('w_in', 'w_ffn_in') + tuple(name for name, _, _ in _FLATB_PIECES)

    local = _pack_shards({name: ws[name][0] for name in sharded}, conv_w[0])
    (gathered_in,) = _gather_shards(local[:1], "gather_w_in")
    w_in_full, conv_full = _unpack_w_in(gathered_in, local[0], me_chip)
    small = {n: ws[n][0] for n in _SMALL_NAMES}
    small['conv_w'] = conv_full
    sp = _prep_small(small)
    cosb, sinb = _rope_tables(positions[0])
    exch = _Exchange(local[1:], me_chip, c_arr)

    loss_lanes, dx, g = _local_step(x[0], p[0, 0], cosb, sinb, loss_target[0], _prep_w_in(w_in_full), sp, exch)
    gw = _unprep_grads_late(g)
    loss = lax.psum(jnp.sum(loss_lanes), ("x", "y", "c"))

    parts = [exch.part_in] + exch.parts
    arrived = exch.arrived_in + exch.arrived
    mine = [_chip_add(p_, r_, me_arr, tr, "chip_add_%d" % i) for i, (p_, r_, tr) in enumerate(zip(parts, arrived, _ADD_TILES))]
    reduced = _unpack_reduced(mine, _reduce_pair_share(mine, "reduce_pair_share"), mc)
    tot = _small_allreduce(_pack_small(gw))
    gred = {name: reduced[name][None] for name in sharded}
    for i, n in enumerate(_SMALL_NAMES):
        gred[n] = tot[i * _SMALL_GROUP, :ws[n].shape[1]][None]
    conv_tot = tot[_CONV_SMALL_ROW:_CONV_SMALL_ROW + _CONV_SMALL_ROWS].reshape(DN_CONV, QKV_W)
    gred['conv_w'] = lax.dynamic_slice_in_dim(conv_tot, (2 * mx + my) * _CONV_SHARD, _CONV_SHARD, axis=1)[None]

    deltas, new_m, new_v = {}, {}, {}
    for n in _WEIGHT_NAMES:
        if n == 'w_in':
            tr_ = lambda a: jnp.transpose(a, (0, 2, 1))
            g_t = tr_(gred[n].reshape(ws[n].shape))
            outs = _adamw(tr_(ws[n]), g_t, tr_(ms[n]), tr_(vs[n]), "adamw_" + n)
            gred[n] = tr_(g_t)
            deltas[n], new_m[n], new_v[n] = (tr_(o) for o in outs)
            continue
        gred[n] = gred[n].reshape(ws[n].shape)
        deltas[n], new_m[n], new_v[n] = _adamw(ws[n], gred[n], ms[n], vs[n], "adamw_" + n)
    return (loss, dx[None], *[gred[n] for n in _WEIGHT_NAMES], *[deltas[n] for n in _WEIGHT_NAMES],
            *[new_m[n] for n in _WEIGHT_NAMES], *[new_v[n] for n in _WEIGHT_NAMES])
```

```python
import functools
import math

import jax
import jax.numpy as jnp
from jax import lax
from jax.experimental import pallas as pl
from jax.experimental.pallas import tpu as pltpu

F32 = jnp.float32
_CDT = jnp.bfloat16
_HI = lax.Precision.HIGHEST
_MESH = pl.DeviceIdType.MESH

D_MODEL = 1024
PLE_DIM = 256
HEADS = 8
DN_DK = 128
DN_CHUNK = 64
DN_CONV = 4
QKV_W = 3 * HEADS * DN_DK
Q_LORA = 384
KV_LORA = 256
NOPE = 128
ROPE = 64
ROPE_PAD = 128
FFN_HIDDEN = 2816
D_IN = 6864
ROPE_BASE = 10000.0
ALPHA = 2.0 ** 0.25
ATT_SCALE = (NOPE + ROPE) ** -0.5
NEG_BIG = -1e30
ADAM_LR, ADAM_B1, ADAM_B2, ADAM_EPS, ADAM_WD, ADAM_STEP = 0.001, 0.9, 0.999, 1e-08, 0.01, 10

LANE = 128
VMEM_LIMIT = 56 * 1024 * 1024
MM_VMEM_BUDGET = 40 * 1024 * 1024
N_SHARD = 4
FLAT_W = 1024
SMALL_ROWS = 88


def _tile(dim, cap):
    if dim <= cap:
        return dim
    t = (cap // LANE) * LANE
    while t >= LANE:
        if dim % t == 0:
            return t
        t -= LANE
    return dim


def _cparams(sem):
    return pltpu.CompilerParams(dimension_semantics=sem, vmem_limit_bytes=VMEM_LIMIT)


def _mm(a, b, *, name, ta=False, tb=False, add=None, add_scale=1.0, out_dtype=F32, heads=None,
        a_head=None, b_head=None, out_head=None, dims=None, tm=1408, tn=1408):
    m, n, k = dims
    tm, tn = _tile(m, tm), _tile(n, tn)
    sa, sb, so = a.dtype.itemsize, b.dtype.itemsize, jnp.dtype(out_dtype).itemsize

    def vmem_need(tk_):
        acc = tm * tn * 4 if tk_ < k else 0
        extra = 2 * tm * tn * 4 if add is not None else 0
        return 2 * (tm * tk_ * sa + tk_ * tn * sb) + 2 * tm * tn * so + acc + extra

    tk = k
    while vmem_need(tk) > MM_VMEM_BUDGET and tk > LANE:
        smaller = _tile(k, tk - LANE)
        if smaller >= tk:
            break
        tk = smaller
    nk = k // tk
    hgrid = () if heads is None else (heads,)
    off = len(hgrid)

    def spec(rows, cols, rtile, ctile, rsel, csel, layout):
        def idx(*g):
            h = g[0] if off else 0
            ri, ci = g[off + rsel], g[off + csel]
            if layout == 'lead':
                return (h, ri, ci)
            if layout == 'col':
                return (ri, h * (cols // ctile) + ci)
            return (ri, ci)
        if layout == 'lead':
            return pl.BlockSpec((None, rtile, ctile), idx)
        return pl.BlockSpec((rtile, ctile), idx)

    a_spec = spec(k, m, tk, tm, 2, 0, a_head) if ta else spec(m, k, tm, tk, 0, 2, a_head)
    b_spec = spec(n, k, tn, tk, 1, 2, b_head) if tb else spec(k, n, tk, tn, 2, 1, b_head)
    o_spec = spec(m, n, tm, tn, 0, 1, out_head)
    in_specs = [a_spec, b_spec]
    args = [a, b]
    if add is not None:
        in_specs.append(spec(m, n, tm, tn, 0, 1, out_head))
        args.append(add)
    dn = (((0 if ta else 1,), (1 if tb else 0,)), ((), ()))

    def body(*refs):
        a_ref, b_ref = refs[0], refs[1]
        prod = lax.dot_general(a_ref[...].astype(_CDT), b_ref[...].astype(_CDT), dn, preferred_element_type=F32)
        if nk == 1:
            o_ref = refs[-1]
            if add is not None:
                prod = prod + refs[2][...].astype(F32) * add_scale
            o_ref[...] = prod.astype(out_dtype)
            return
        o_ref, acc_ref = refs[-2], refs[-1]
        kk = pl.program_id(off + 2)

        @pl.when(kk == 0)
        def _():
            if add is not None:
                acc_ref[...] = refs[2][...].astype(F32) * add_scale
            else:
                acc_ref[...] = jnp.zeros_like(acc_ref)

        acc_ref[...] += prod

        @pl.when(kk == nk - 1)
        def _():
            o_ref[...] = acc_ref[...].astype(out_dtype)

    if out_head == 'lead':
        oshape = (heads, m, n)
    elif out_head == 'col':
        oshape = (m, heads * n)
    else:
        oshape = (m, n)
    sem = ("parallel",) * (off + 2) + ("arbitrary",)
    return pl.pallas_call(
        body, name=name, grid=hgrid + (m // tm, n // tn, nk), in_specs=in_specs, out_specs=o_spec,
        out_shape=jax.ShapeDtypeStruct(oshape, out_dtype),
        scratch_shapes=[pltpu.VMEM((tm, tn), F32)] if nk > 1 else [],
        compiler_params=_cparams(sem))(*args)


def _mm2(a, b, **kw):
    ta, tb = kw.get('ta', False), kw.get('tb', False)
    m = a.shape[1] if ta else a.shape[0]
    k = a.shape[0] if ta else a.shape[1]
    n = b.shape[0] if tb else b.shape[1]
    return _mm(a, b, dims=(m, n, k), **kw)


def _rowwise(fn, rows, bcast, outs, reds=(), *, name, tm=256, heads=None):
    t = rows[0][0].shape[0]
    tm = min(tm, t)
    hn = 1 if heads is None else heads
    in_specs, args = [], []
    for arr, width, base, per_head in rows:
        in_specs.append(pl.BlockSpec((tm, width), functools.partial(
            lambda i, h, base, per_head: (i, base + (h if per_head else 0)), base=base, per_head=per_head)))
        args.append(arr)
    for arr in bcast:
        in_specs.append(pl.BlockSpec(arr.shape, lambda i, h: (0, 0)))
        args.append(arr)
    out_specs, out_shape = [], []
    for total, width, per_head, dt in outs:
        out_specs.append(pl.BlockSpec((tm, width), functools.partial(
            lambda i, h, per_head: (i, h if per_head else 0), per_head=per_head)))
        out_shape.append(jax.ShapeDtypeStruct((t, total), dt))
    for shp in reds:
        out_specs.append(pl.BlockSpec(shp, lambda i, h: (0, 0)))
        out_shape.append(jax.ShapeDtypeStruct(shp, F32))
    n_in, n_out, n_red = len(args), len(outs), len(reds)

    def body(*refs):
        i, h = pl.program_id(0), pl.program_id(1)
        vals = fn(h, *[r[...] for r in refs[:n_in]])
        for r, v in zip(refs[n_in:n_in + n_out], vals[:n_out]):
            r[...] = v.astype(r.dtype)
        if n_red:
            @pl.when((i == 0) & (h == 0))
            def _():
                for r in refs[n_in + n_out:]:
                    r[...] = jnp.zeros_like(r)
            for r, v in zip(refs[n_in + n_out:], vals[n_out:]):
                r[...] += v

    sem = ("arbitrary", "arbitrary") if n_red else ("parallel", "parallel")
    res = pl.pallas_call(body, name=name, grid=(t // tm, hn), in_specs=in_specs, out_specs=out_specs,
                         out_shape=out_shape, compiler_params=_cparams(sem))(*args)
    return tuple(res)


def _sigmoid(x):
    return 1.0 / (1.0 + jnp.exp(-x))


def _silu(x):
    return x * _sigmoid(x)


def _softplus(x):
    return jnp.maximum(x, 0.0) + jnp.log(1.0 + jnp.exp(-jnp.abs(x)))


def _layer_norm(t, g, b):
    mu = jnp.mean(t, axis=-1, keepdims=True)
    d = t - mu
    var = jnp.mean(d * d, axis=-1, keepdims=True)
    return d * lax.rsqrt(var + 1e-5) * g + b


def _rms_norm(t, w):
    return t * lax.rsqrt(jnp.mean(t * t, axis=-1, keepdims=True) + 1e-6) * w


def _swap_rope_halves(t):
    lane = lax.broadcasted_iota(jnp.int32, t.shape, 1) % ROPE_PAD
    n = t.shape[1]
    up = pltpu.roll(t, n - ROPE // 2, axis=1)
    dn = pltpu.roll(t, ROPE // 2, axis=1)
    return jnp.where(lane < ROPE // 2, up, jnp.where(lane < ROPE, dn, 0.0))


def _rope(t, cosb, sinb):
    reps = t.shape[1] // ROPE_PAD
    c = jnp.tile(cosb, (1, reps)) if reps > 1 else cosb
    s = jnp.tile(sinb, (1, reps)) if reps > 1 else sinb
    return t * c + _swap_rope_halves(t) * s


def _rope_bwd(d, cosb, sinb):
    reps = d.shape[1] // ROPE_PAD
    c = jnp.tile(cosb, (1, reps)) if reps > 1 else cosb
    s = jnp.tile(sinb, (1, reps)) if reps > 1 else sinb
    return d * c + _swap_rope_halves(d * s)


_CONV_ROWS = 256
_CONV_COLS = 256


def _conv_window(ref, r0, lo, hi, t):
    parts = []
    start, stop = r0 - lo, r0 + _CONV_ROWS + hi
    if start < 0:
        parts.append(jnp.zeros((-start, ref.shape[1]), F32))
        start = 0
    tail = max(stop - t, 0)
    parts.append(ref[start:stop - tail, :].astype(F32))
    if tail:
        parts.append(jnp.zeros((tail, ref.shape[1]), F32))
    return parts[0] if len(parts) == 1 else jnp.concatenate(parts, axis=0)


def _conv_taps(win, w_ref, n_out):
    acc = win[8:8 + n_out] * w_ref[DN_CONV - 1:DN_CONV, :]
    for i in range(DN_CONV - 1):
        acc = acc + pltpu.roll(win, DN_CONV - 1 - i, axis=0)[8:8 + n_out] * w_ref[i:i + 1, :]
    return acc


def _conv_silu(x, w):
    t, ch = x.shape

    def body(x_ref, w_ref, o_ref):
        for r in range(t // _CONV_ROWS):
            r0 = r * _CONV_ROWS
            c = _conv_taps(_conv_window(x_ref, r0, 8, 0, t), w_ref, _CONV_ROWS)
            o_ref[r0:r0 + _CONV_ROWS, :] = _silu(c)

    return pl.pallas_call(
        body, name="conv_silu", grid=(ch // _CONV_COLS,),
        in_specs=[pl.BlockSpec((t, _CONV_COLS), lambda j: (0, j)), pl.BlockSpec((DN_CONV, _CONV_COLS), lambda j: (0, j))],
        out_specs=pl.BlockSpec((t, _CONV_COLS), lambda j: (0, j)),
        out_shape=jax.ShapeDtypeStruct((t, ch), F32), compiler_params=_cparams(("parallel",)))(x, w)


def _conv_silu_bwd(x, w, dys):
    t, ch = x.shape
    per = ch // len(dys) // _CONV_COLS

    def body(x_ref, w_ref, *rest):
        dy_refs, (dx_ref, dw_ref) = rest[:len(dys)], rest[len(dys):]
        sec = pl.program_id(0) // per
        dws = [jnp.zeros((1, _CONV_COLS), F32) for _ in range(DN_CONV)]
        for r in range(t // _CONV_ROWS):
            r0 = r * _CONV_ROWS
            n_ext = _CONV_ROWS + 8
            xw = _conv_window(x_ref, r0, 8, 8, t)
            c = _conv_taps(xw, w_ref, n_ext)
            sg = _sigmoid(c)
            dy = _conv_window(dy_refs[-1], r0, 0, 8, t)
            for k in range(len(dys) - 2, -1, -1):
                dy = jnp.where(sec == k, _conv_window(dy_refs[k], r0, 0, 8, t), dy)
            ds = dy * (sg * (1.0 + c * (1.0 - sg)))
            x0 = xw[8:8 + _CONV_ROWS]
            dx = jnp.zeros((_CONV_ROWS, _CONV_COLS), F32)
            for i in range(DN_CONV):
                sh = DN_CONV - 1 - i
                ds_up = (ds if sh == 0 else pltpu.roll(ds, n_ext - sh, axis=0))[:_CONV_ROWS]
                dx = dx + ds_up * w_ref[i:i + 1, :]
                dws[i] = dws[i] + jnp.sum(x0 * ds_up, axis=0, keepdims=True)
            dx_ref[r0:r0 + _CONV_ROWS, :] = dx.astype(dx_ref.dtype)
        for i in range(DN_CONV):
            dw_ref[i:i + 1, :] = dws[i]

    blk = pl.BlockSpec((t, _CONV_COLS), lambda j: (0, j))
    wblk = pl.BlockSpec((DN_CONV, _CONV_COLS), lambda j: (0, j))
    dy_specs = [pl.BlockSpec((t, _CONV_COLS), functools.partial(lambda j, k: (0, jnp.clip(j - k * per, 0, per - 1)), k=k))
                for k in range(len(dys))]
    return pl.pallas_call(
        body, name="conv_silu_bwd", grid=(ch // _CONV_COLS,), in_specs=[blk, wblk] + dy_specs, out_specs=[blk, wblk],
        out_shape=[jax.ShapeDtypeStruct((t, ch), _CDT), jax.ShapeDtypeStruct((DN_CONV, ch), F32)],
        compiler_params=_cparams(("arbitrary",)))(x, w, *dys)


_PA_ROWS = 512


def _bmm(a, b, spec, exact=False):
    if exact:
        return jnp.einsum(spec, a, b, precision=_HI, preferred_element_type=F32)
    return jnp.einsum(spec, a.astype(_CDT), b.astype(_CDT), preferred_element_type=F32)


def _split16(a):
    hi = a.astype(jnp.bfloat16)
    return hi, (a - hi.astype(F32)).astype(jnp.bfloat16)


def _bmm3(a, b, spec):
    ah, al = _split16(a)
    bh, bl = _split16(b)
    e = lambda p, q: jnp.einsum(spec, p, q, preferred_element_type=F32)
    return e(ah, bh) + (e(ah, bl) + e(al, bh))


def _split3(b):
    b0 = b.astype(jnp.bfloat16)
    r1 = b - b0.astype(F32)
    b1 = r1.astype(jnp.bfloat16)
    return b0, b1, (r1 - b1.astype(F32)).astype(jnp.bfloat16)


@functools.partial(jax.custom_vjp, nondiff_argnums=(2, 3))
def _select_mm(sel, b, spec, spec_t):
    return sum(jnp.einsum(spec, sel, t, preferred_element_type=F32) for t in _split3(b))


def _select_mm_fwd(sel, b, spec, spec_t):
    return _select_mm(sel, b, spec, spec_t), sel


def _select_mm_bwd(spec, spec_t, sel, ct):
    return jnp.zeros_like(sel), sum(jnp.einsum(spec_t, sel, t, preferred_element_type=F32) for t in _split3(ct))


_select_mm.defvjp(_select_mm_fwd, _select_mm_bwd)


def _tri_inverse(l_mat, eye):
    pw = -l_mat
    t_inv = eye + pw
    for _ in range(5):
        pw = _bmm3(pw, pw, 'bij,bjk->bik')
        t_inv = t_inv + _bmm3(t_inv, pw, 'bij,bjk->bik')
    return t_inv


@jax.custom_vjp
def _tri_inverse_saved(l_mat, t_saved):
    return t_saved


def _tri_inverse_saved_fwd(l_mat, t_saved):
    return t_saved, t_saved


def _tri_inverse_saved_bwd(t_saved, dt):
    left = _bmm3(t_saved, dt, 'bji,bjk->bik')
    return -_bmm3(left, t_saved, 'bij,bkj->bik'), jnp.zeros_like(t_saved)


_tri_inverse_saved.defvjp(_tri_inverse_saved_fwd, _tri_inverse_saved_bwd)


def _phase_a(h, q, k, v, ba, alog, dtb, t_saved=None):
    r = q.shape[0]
    nb = r // DN_CHUNK
    c = DN_CHUNK
    lane = lax.broadcasted_iota(jnp.int32, (1, LANE), 1)
    selb = (lane == h).astype(F32)
    sela = (lane == h + HEADS).astype(F32)
    b_raw = jnp.sum(ba * selb, axis=1, keepdims=True)
    a_raw = jnp.sum(ba * sela, axis=1, keepdims=True)
    al = jnp.sum(alog * selb, axis=1, keepdims=True)
    dt = jnp.sum(dtb * selb, axis=1, keepdims=True)
    beta = jnp.broadcast_to(_sigmoid(b_raw), (r, LANE))
    g = jnp.broadcast_to(-jnp.exp(al) * _softplus(a_raw + dt), (r, LANE))
    qn = q * lax.rsqrt(jnp.sum(q * q, -1, keepdims=True) + 1e-6) * (DN_DK ** -0.5)
    kn = k * lax.rsqrt(jnp.sum(k * k, -1, keepdims=True) + 1e-6)
    q3, k3, v3 = qn.reshape(nb, c, LANE), kn.reshape(nb, c, LANE), v.reshape(nb, c, LANE)
    b3, g3 = beta.reshape(nb, c, LANE), g.reshape(nb, c, LANE)
    ri = lax.broadcasted_iota(jnp.int32, (nb, c, c), 1)
    ci = lax.broadcasted_iota(jnp.int32, (nb, c, c), 2)
    tril, strict = ri >= ci, ri > ci
    gc = _select_mm(tril.astype(jnp.bfloat16), g3, 'bij,bjd->bid', 'bij,bid->bjd')
    onehot = (lax.broadcasted_iota(jnp.int32, (nb, c, LANE), 2) == 0).astype(jnp.bfloat16)
    g_row = _select_mm(onehot, gc, 'bid,bjd->bij', 'bid,bij->bjd')
    diff = gc[:, :, :c] - g_row
    decay = jnp.where(tril, jnp.exp(jnp.where(tril, diff, 0.0)), 0.0)
    kb = k3 * b3
    l_mat = jnp.where(strict, _bmm(kb, k3, 'bid,bjd->bij') * decay, 0.0)
    if t_saved is None:
        t_inv = _tri_inverse(l_mat, (ri == ci).astype(F32))
    else:
        t_inv = _tri_inverse_saved(l_mat, t_saved.reshape(nb, c, c))
    eg = jnp.exp(gc)
    u = _bmm(t_inv, v3 * b3, 'bij,bje->bie')
    w = _bmm(t_inv, kb * eg, 'bij,bje->bie')
    intra = jnp.where(tril, _bmm(q3, k3, 'bid,bjd->bij') * decay, 0.0)
    qd = q3 * eg
    gl = jnp.sum(g3, axis=1, keepdims=True)
    kt = k3 * jnp.exp(gl - gc)
    outs = (u.reshape(r, LANE), w.reshape(r, LANE), qd.reshape(r, LANE), kt.reshape(r, LANE),
            intra.reshape(r, c), gl.reshape(nb, LANE))
    if t_saved is not None:
        return outs
    qd2 = qd - _bmm(intra, w, 'bij,bjd->bid')
    au = _bmm(intra, u, 'bij,bje->bie')
    return outs + (t_inv.reshape(r, c), qd2.reshape(r, LANE), au.reshape(r, LANE))


def _pa_specs(t):
    rr = min(_PA_ROWS, t)
    nb = rr // DN_CHUNK
    qkv = [pl.BlockSpec((rr, LANE), functools.partial(lambda i, h, o: (i, o + h), o=o)) for o in (0, HEADS, 2 * HEADS)]
    ba = pl.BlockSpec((rr, LANE), lambda i, h: (i, 3))
    vec = pl.BlockSpec((1, LANE), lambda i, h: (0, 0))
    row = pl.BlockSpec((rr, LANE), lambda i, h: (i, h))
    intra = pl.BlockSpec((None, rr, DN_CHUNK), lambda i, h: (h, i, 0))
    gl = pl.BlockSpec((nb, LANE), lambda i, h: (i, h))
    return rr, qkv, ba, vec, row, intra, gl


def _grid_ends(grid):
    first = lambda: functools.reduce(jnp.logical_and, [pl.program_id(a) == 0 for a in range(len(grid))])
    last = lambda: functools.reduce(jnp.logical_and, [pl.program_id(a) == n - 1 for a, n in enumerate(grid)])
    return first, last


def _delta_local(qkv_act, pm, alog, dtb, rider=None):
    t = qkv_act.shape[0]
    rr, qkv, ba, vec, row, intra, gl = _pa_specs(t)

    def body(q, k, v, b, al, dt, *outs):
        vals = _phase_a(pl.program_id(1), q[...], k[...], v[...], b[...], al[...], dt[...])
        for o, val in zip(outs, vals):
            o[...] = val

    wide = jax.ShapeDtypeStruct((t, HEADS * LANE), F32)
    sq = jax.ShapeDtypeStruct((HEADS, t, DN_CHUNK), F32)
    grid = (t // rr, HEADS)
    return _carried_call(
        body, rider, *_grid_ends(grid), name="delta_local", grid=grid, in_specs=qkv + [ba, vec, vec],
        out_specs=[row] * 4 + [intra, gl, intra, row, row],
        out_shape=[wide] * 4 + [sq, jax.ShapeDtypeStruct((t // DN_CHUNK, HEADS * LANE), F32), sq, wide, wide],
        scratch_shapes=[], sem=("arbitrary", "arbitrary"), args=(qkv_act, qkv_act, qkv_act, pm, alog, dtb))


def _delta_local_bwd(qkv_act, pm, alog, dtb, t_inv, du, dw, dqd, dkt, dintra, dgl, rider=None):
    t = qkv_act.shape[0]
    rr, qkv, ba, vec, row, intra, gl = _pa_specs(t)

    def body(q, k, v, b, al, dt, ti, du_r, dw_r, dqd_r, dkt_r, di_r, dgl_r, dq_o, dk_o, dv_o, dba_o, dal_o, ddt_o):
        i, h = pl.program_id(0), pl.program_id(1)
        t_saved = ti[...]
        _, vjp = jax.vjp(lambda *a: _phase_a(h, *a, t_saved=t_saved), q[...], k[...], v[...], b[...], al[...], dt[...])
        dq, dk, dv, dba, dal, ddt = vjp((du_r[...], dw_r[...], dqd_r[...], dkt_r[...], di_r[...], dgl_r[...]))
        dq_o[...], dk_o[...], dv_o[...] = dq, dk, dv

        @pl.when(h == 0)
        def _():
            dba_o[...] = jnp.zeros_like(dba_o)

        @pl.when((h == 0) & (i == 0))
        def _():
            dal_o[...] = jnp.zeros_like(dal_o)
            ddt_o[...] = jnp.zeros_like(ddt_o)

        dba_o[...] += dba
        dal_o[...] += dal
        ddt_o[...] += ddt

    wide = jax.ShapeDtypeStruct((t, HEADS * LANE), F32)
    vshape = jax.ShapeDtypeStruct((1, LANE), F32)
    grid = (t // rr, HEADS)
    return _carried_call(
        body, rider, *_grid_ends(grid), name="delta_local_bwd", grid=grid,
        in_specs=qkv + [ba, vec, vec, intra] + [row] * 4 + [intra, gl],
        out_specs=[row] * 3 + [pl.BlockSpec((rr, LANE), lambda i, h: (i, 0)), vec, vec],
        out_shape=[wide] * 3 + [jax.ShapeDtypeStruct((t, LANE), F32), vshape, vshape],
        scratch_shapes=[], sem=("arbitrary", "arbitrary"),
        args=(qkv_act, qkv_act, qkv_act, pm, alog, dtb, t_inv, du, dw, dqd, dkt, dintra, dgl))


_SCAN_ROWS = 512


def _dot(a, b, dn):
    return lax.dot_general(a.astype(_CDT), b.astype(_CDT), (dn, ((), ())), preferred_element_type=F32)


_NN = ((1,), (0,))
_NT = ((1,), (1,))
_TN = ((0,), (0,))


def _delta_scan(u, w, qd, kt, au, gl, rider=None):
    t = u.shape[0]
    rr = min(_SCAN_ROWS, t)
    nc = rr // DN_CHUNK

    def body(u_ref, w_ref, qd_ref, kt_ref, au_ref, gl_ref, o_ref, sall_ref, s_scr):
        @pl.when(pl.program_id(0) == 0)
        def _():
            s_scr[...] = jnp.zeros_like(s_scr)

        def chunk(c, carry):
            r0 = pl.multiple_of(c * DN_CHUNK, DN_CHUNK)
            rows = pl.ds(r0, DN_CHUNK)
            e = jnp.exp(gl_ref[pl.ds(c, 1), :])
            states = [s_scr[h] for h in range(HEADS)]
            u_c, w_c, qd_c, kt_c, au_c = u_ref[rows, :], w_ref[rows, :], qd_ref[rows, :], kt_ref[rows, :], au_ref[rows, :]
            o_new, s_new = [], []
            for h in range(HEADS):
                cs = slice(h * LANE, (h + 1) * LANE)
                s = states[h]
                both = _dot(jnp.concatenate([w_c[:, cs], qd_c[:, cs]], axis=0), s, _NN)
                v_new = u_c[:, cs] - both[:DN_CHUNK]
                o_new.append(both[DN_CHUNK:] + au_c[:, cs])
                s_new.append(s * e[:, cs] + _dot(kt_c[:, cs], v_new, _TN))
            o_ref[rows, :] = jnp.concatenate(o_new, axis=1)
            for h in range(HEADS):
                sall_ref[c, h] = states[h]
                s_scr[h] = s_new[h]
            return carry

        lax.fori_loop(0, nc, chunk, 0)

    row = pl.BlockSpec((rr, HEADS * LANE), lambda i: (i, 0))
    grid = (t // rr,)
    return _carried_call(
        body, rider, *_grid_ends(grid), name="delta_scan", grid=grid,
        in_specs=[row] * 5 + [pl.BlockSpec((nc, HEADS * LANE), lambda i: (i, 0))],
        out_specs=[row, pl.BlockSpec((nc, HEADS, LANE, LANE), lambda i: (i, 0, 0, 0))],
        out_shape=[jax.ShapeDtypeStruct((t, HEADS * LANE), F32),
                   jax.ShapeDtypeStruct((t // DN_CHUNK, HEADS, LANE, LANE), F32)],
        scratch_shapes=[pltpu.VMEM((HEADS, LANE, LANE), F32)], sem=("arbitrary",), args=(u, w, qd, kt, au, gl))


def _delta_scan_bwd(u, w, qd, kt, intra, gl, sall, do, rider=None):
    t = u.shape[0]
    rr = min(_SCAN_ROWS, t)
    nc = rr // DN_CHUNK
    ng = t // rr

    def body(u_ref, w_ref, qd_ref, kt_ref, a_ref, gl_ref, sall_ref, do_ref,
             du_ref, dw_ref, dqd_ref, dkt_ref, da_ref, dgl_ref, ds_scr):
        @pl.when(pl.program_id(0) == 0)
        def _():
            ds_scr[...] = jnp.zeros_like(ds_scr)

        def chunk(cc, carry):
            c = nc - 1 - cc
            r0 = pl.multiple_of(c * DN_CHUNK, DN_CHUNK)
            rows = pl.ds(r0, DN_CHUNK)
            e = jnp.exp(gl_ref[pl.ds(c, 1), :])
            states = [sall_ref[c, h] for h in range(HEADS)]
            ds_outs = [ds_scr[h] for h in range(HEADS)]
            u_a, w_a, kt_a, qd_a, do_a = u_ref[rows, :], w_ref[rows, :], kt_ref[rows, :], qd_ref[rows, :], do_ref[rows, :]
            a_a = [a_ref[h, rows, :] for h in range(HEADS)]
            da, dqd, dkt, du, dw, dgl, ds_new = [], [], [], [], [], [], []
            for h in range(HEADS):
                cs = slice(h * LANE, (h + 1) * LANE)
                s, ds_out = states[h], ds_outs[h]
                w_c, kt_c, qd_c, do_c = w_a[:, cs], kt_a[:, cs], qd_a[:, cs], do_a[:, cs]
                v_new = u_a[:, cs] - _dot(w_c, s, _NN)
                dv_new = _dot(a_a[h], do_c, _TN) + _dot(kt_c, ds_out, _NN)
                cots = jnp.concatenate([do_c, dv_new], axis=0)
                both = _dot(cots, s, _NT)
                dqd.append(both[:DN_CHUNK])
                dw.append(-both[DN_CHUNK:])
                da.append(_dot(do_c, v_new, _NT))
                dkt.append(_dot(v_new, ds_out, _NT))
                du.append(dv_new)
                eh = e[:, cs]
                dgl.append(jnp.broadcast_to(jnp.sum(ds_out * s, axis=0, keepdims=True) * eh, (8, LANE)))
                ds_new.append(ds_out * eh + _dot(jnp.concatenate([qd_c, -w_c], axis=0), cots, _TN))
            cat = lambda parts: jnp.concatenate(parts, axis=1)
            dqd_ref[rows, :], dkt_ref[rows, :], du_ref[rows, :], dw_ref[rows, :] = cat(dqd), cat(dkt), cat(du), cat(dw)
            dgl_ref[pl.ds(pl.multiple_of(c * 8, 8), 8), :] = cat(dgl)
            for h in range(HEADS):
                da_ref[h, rows, :] = da[h]
                ds_scr[h] = ds_new[h]
            return carry

        lax.fori_loop(0, nc, chunk, 0)

    rev = lambda i: (ng - 1 - i, 0)
    row = pl.BlockSpec((rr, HEADS * LANE), rev)
    a_spec = pl.BlockSpec((HEADS, rr, DN_CHUNK), lambda i: (0, ng - 1 - i, 0))
    gl_spec = pl.BlockSpec((nc, HEADS * LANE), rev)
    wide = jax.ShapeDtypeStruct((t, HEADS * LANE), F32)
    outs, carried = _carried_call(
        body, rider, *_grid_ends((ng,)), name="delta_scan_bwd", grid=(ng,),
        in_specs=[row] * 4 + [a_spec, gl_spec, pl.BlockSpec((nc, HEADS, LANE, LANE), lambda i: (ng - 1 - i, 0, 0, 0)), row],
        out_specs=[row] * 4 + [a_spec, pl.BlockSpec((nc * 8, HEADS * LANE), rev)],
        out_shape=[wide] * 4 + [jax.ShapeDtypeStruct((HEADS, t, DN_CHUNK), F32),
                                jax.ShapeDtypeStruct((t // DN_CHUNK * 8, HEADS * LANE), F32)],
        scratch_shapes=[pltpu.VMEM((HEADS, LANE, LANE), F32)], sem=("arbitrary",), args=(u, w, qd, kt, intra, gl, sall, do))
    return tuple(outs[:5]) + (outs[5].reshape(t // DN_CHUNK, 8, HEADS * LANE)[:, 0, :],), carried


_ATT_TILE = 512


def _kv_rows(j, tk):
    return pl.ds(pl.multiple_of(j * tk, tk), tk)


def _att_scores(ql, qr, ckv_ref, kr_ref, j, tk):
    ks = _kv_rows(j, tk)
    return (_dot(ql, ckv_ref[ks, :], _NT) + _dot(qr, kr_ref[ks, :], _NT)) * ATT_SCALE


def _diag_mask(s):
    qi = lax.broadcasted_iota(jnp.int32, s.shape, 0)
    ki = lax.broadcasted_iota(jnp.int32, s.shape, 1)
    return jnp.where(ki <= qi, s, NEG_BIG)


def _attention(ql, qr, ckv, kr):
    t = ckv.shape[0]
    tq = min(_ATT_TILE, t)

    nl = tq // LANE

    def lane_fold(v, op):
        out = v[:, :LANE]
        for k in range(1, nl):
            out = op(out, v[:, k * LANE:(k + 1) * LANE])
        return out

    def body(ql_ref, qr_ref, ckv_ref, kr_ref, o_ref, lse_ref, s_all, m_lanes, l_lanes, acc_scr):
        qi = pl.program_id(1)
        q_lat, q_rope = ql_ref[...], qr_ref[...]
        m_lanes[...] = jnp.full_like(m_lanes, NEG_BIG)

        def scores(j, masked):
            s = _att_scores(q_lat, q_rope, ckv_ref, kr_ref, j, tq)
            if masked:
                s = _diag_mask(s)
            s_all[j] = s
            m_lanes[...] = jnp.maximum(m_lanes[...], lane_fold(s, jnp.maximum))

        def scores_body(j, carry):
            scores(j, False)
            return carry

        lax.fori_loop(0, qi, scores_body, 0)
        scores(qi, True)
        m = jnp.max(m_lanes[...], axis=-1, keepdims=True)
        mb = jnp.broadcast_to(m, (tq, LANE))
        l_lanes[...] = jnp.zeros_like(l_lanes)
        acc_scr[...] = jnp.zeros_like(acc_scr)

        def weigh(j, carry):
            s = s_all[j]
            p = jnp.concatenate([jnp.exp(s[:, k * LANE:(k + 1) * LANE] - mb) for k in range(nl)], axis=1)
            l_lanes[...] += lane_fold(p, jnp.add)
            acc_scr[...] += _dot(p, ckv_ref[_kv_rows(j, tq), :], _NN)
            return carry

        lax.fori_loop(0, qi + 1, weigh, 0)
        l = jnp.sum(l_lanes[...], axis=-1, keepdims=True)
        o_ref[...] = acc_scr[...] / l
        lse_ref[...] = m + jnp.log(l)

    return pl.pallas_call(
        body, name="attention", grid=(HEADS, t // tq),
        in_specs=[pl.BlockSpec((None, tq, KV_LORA), lambda h, i: (h, i, 0)),
                  pl.BlockSpec((tq, ROPE_PAD), lambda h, i: (i, h)),
                  pl.BlockSpec((t, KV_LORA), lambda h, i: (0, 0)),
                  pl.BlockSpec((t, ROPE_PAD), lambda h, i: (0, 0))],
        out_specs=[pl.BlockSpec((None, tq, KV_LORA), lambda h, i: (h, i, 0)),
                   pl.BlockSpec((None, tq, 1), lambda h, i: (h, i, 0))],
        out_shape=[jax.ShapeDtypeStruct((HEADS, t, KV_LORA), F32), jax.ShapeDtypeStruct((HEADS, t, 1), F32)],
        scratch_shapes=[pltpu.VMEM((t // tq, tq, tq), F32), pltpu.VMEM((tq, LANE), F32), pltpu.VMEM((tq, LANE), F32),
                        pltpu.VMEM((tq, KV_LORA), F32)],
        compiler_params=_cparams(("parallel", "parallel")))(ql, qr, ckv, kr)


def _attention_bwd(ql, qr, ckv, kr, out, lse, dout):
    t = ckv.shape[0]
    tq = min(_ATT_TILE, t)

    def body(ql_ref, qr_ref, ckv_ref, kr_ref, o_ref, lse_ref, do_ref, dql_ref, dqr_ref, dckv_ref, dkr_ref,
             dql_scr, dqr_scr):
        h, qi = pl.program_id(0), pl.program_id(1)

        @pl.when((h == 0) & (qi == 0))
        def _():
            dckv_ref[...] = jnp.zeros_like(dckv_ref)
            dkr_ref[...] = jnp.zeros_like(dkr_ref)

        q_lat, q_rope = ql_ref[...], qr_ref[...]
        d_o = do_ref[...].astype(_CDT)
        lse_v = lse_ref[...]
        dsum = jnp.sum(do_ref[...] * o_ref[...], axis=-1, keepdims=True)
        dql_scr[...] = jnp.zeros_like(dql_scr)
        dqr_scr[...] = jnp.zeros_like(dqr_scr)

        def step(j, masked):
            ks = _kv_rows(j, tq)
            s = _att_scores(q_lat, q_rope, ckv_ref, kr_ref, j, tq)
            if masked:
                s = _diag_mask(s)
            p = jnp.exp(s - lse_v)
            kv = ckv_ref[ks, :]
            ds = (p * (_dot(d_o, kv, _NT) - dsum) * ATT_SCALE).astype(_CDT)
            pb = p.astype(_CDT)
            dql_scr[...] += _dot(ds, kv, _NN)
            dqr_scr[...] += _dot(ds, kr_ref[ks, :], _NN)
            dckv_ref[ks, :] += _dot(pb, d_o, _TN) + _dot(ds, q_lat, _TN)
            dkr_ref[ks, :] += _dot(ds, q_rope, _TN)

        def loop_body(j, carry):
            step(j, False)
            return carry

        lax.fori_loop(0, qi, loop_body, 0)
        step(qi, True)
        dql_ref[...] = dql_scr[...].astype(dql_ref.dtype)
        dqr_ref[...] = dqr_scr[...]

    lat = pl.BlockSpec((None, tq, KV_LORA), lambda h, i: (h, i, 0))
    rope = pl.BlockSpec((tq, ROPE_PAD), lambda h, i: (i, h))
    kfull = pl.BlockSpec((t, KV_LORA), lambda h, i: (0, 0))
    rfull = pl.BlockSpec((t, ROPE_PAD), lambda h, i: (0, 0))
    return pl.pallas_call(
        body, name="attention_bwd", grid=(HEADS, t // tq),
        in_specs=[lat, rope, kfull, rfull, lat, pl.BlockSpec((None, tq, 1), lambda h, i: (h, i, 0)), lat],
        out_specs=[lat, rope, kfull, rfull],
        out_shape=[jax.ShapeDtypeStruct((HEADS, t, KV_LORA), _CDT), jax.ShapeDtypeStruct((t, HEADS * ROPE_PAD), F32),
                   jax.ShapeDtypeStruct((t, KV_LORA), F32), jax.ShapeDtypeStruct((t, ROPE_PAD), F32)],
        scratch_shapes=[pltpu.VMEM((tq, KV_LORA), F32), pltpu.VMEM((tq, ROPE_PAD), F32)],
        compiler_params=_cparams(("arbitrary", "arbitrary")))(ql, qr, ckv, kr, out, lse, dout)


_DX_ROWS = 256


def _dx_fused(pairs, add, add_scale, rider=None):
    t, d = add.shape
    tm = min(_DX_ROWS, t)
    n = len(pairs)

    def body(*refs):
        acc = refs[2 * n][...] * add_scale
        for i in range(n):
            acc = acc + _dot(refs[i][...], refs[n + i][...], _NT)
        refs[2 * n + 1][...] = acc

    in_specs = [pl.BlockSpec((tm, a.shape[1]), lambda i: (i, 0)) for a, _ in pairs]
    in_specs += [pl.BlockSpec(w.shape, lambda i: (0, 0)) for _, w in pairs]
    row = pl.BlockSpec((tm, d), lambda i: (i, 0))
    grid = (t // tm,)
    (dx,), carried = _carried_call(
        body, rider, *_grid_ends(grid), name="b_dx", grid=grid, in_specs=in_specs + [row], out_specs=[row],
        out_shape=[jax.ShapeDtypeStruct((t, d), F32)], scratch_shapes=[], sem=("arbitrary",),
        args=tuple(a for a, _ in pairs) + tuple(w for _, w in pairs) + (add,))
    return dx, carried


def _ffn_in_swiglu(a, w):
    t, k = a.shape
    hid = w.shape[1] // 2
    tm, tn = _tile(t, 1024), _tile(hid, 1408)
    nj = hid // tn

    def body(a_ref, bg_ref, bu_ref, act_ref, gt_ref, up_ref):
        av = a_ref[...].astype(_CDT)
        gt = jnp.dot(av, bg_ref[...].astype(_CDT), preferred_element_type=F32)
        up = jnp.dot(av, bu_ref[...].astype(_CDT), preferred_element_type=F32)
        act_ref[...] = _swiglu(gt, up).astype(act_ref.dtype)
        gt_ref[...] = gt.astype(gt_ref.dtype)
        up_ref[...] = up.astype(up_ref.dtype)

    out = pl.BlockSpec((tm, tn), lambda i, j: (i, j))
    return pl.pallas_call(
        body, name="f_ffn_in_swiglu", grid=(t // tm, nj),
        in_specs=[pl.BlockSpec((tm, k), lambda i, j: (i, 0)), pl.BlockSpec((k, tn), lambda i, j: (0, j)),
                  pl.BlockSpec((k, tn), lambda i, j: (0, nj + j))],
        out_specs=[out] * 3, out_shape=[jax.ShapeDtypeStruct((t, hid), _CDT)] * 3,
        compiler_params=_cparams(("parallel", "parallel")))(a, w, w)


def _gated_norm(o, z, w):
    return _rms_norm(o, w) * _silu(z)


def _gated_norm_heads(o, z, w):
    heads = [_gated_norm(o[:, h * LANE:(h + 1) * LANE], z[:, h * LANE:(h + 1) * LANE], w) for h in range(HEADS)]
    return jnp.concatenate(heads, axis=1)


def _mla_pre(ckv, krp, cq, cosb, sinb, qw, kw):
    return _rms_norm(cq, qw), _rms_norm(ckv, kw), _rope(krp, cosb, sinb)


def _merge(gg, y_dn, y_mla):
    return _sigmoid(gg[:, :D_MODEL]) * y_dn + _sigmoid(gg[:, D_MODEL:]) * y_mla


def _ln1(xv, attn_out, g, b):
    return _layer_norm(ALPHA * xv + attn_out, g, b)


def _final(h1, ffn, gate_pre, ple_proj, g, b):
    return _layer_norm(ALPHA * h1 + ffn + _sigmoid(gate_pre) * ple_proj, g, b)


def _swiglu(gt, up):
    return _silu(gt) * up


def _local_step(x, p, cosb, sinb, target, wt, sp, exch):
    t = x.shape[0]
    bf = _CDT
    xb = x.astype(bf)
    g = {}

    qkv_pre = _mm2(xb, wt['qkv'], name="f_qkv")
    z = _mm2(xb, wt['z'], name="f_z")
    gg = _mm2(xb, wt['gg'], name="f_gg")
    pm = _mm2(xb, wt['mla'], name="f_mla")
    qkv_act = _conv_silu(qkv_pre, sp['conv_w'])
    (u, w_, qd, kt, intra, gl, t_inv, qd2, au), sent = _delta_local(qkv_act, pm, sp['a_log'], sp['dt_bias'],
                                                                    rider=exch.gather_send())
    (o_dn, sall), passed = _delta_scan(u, w_, qd2, kt, au, gl, rider=exch.gather_pass(sent))
    wt = dict(wt, **exch.weights(passed))
    (og,) = _rowwise(lambda h, o, zz, w: (_gated_norm_heads(o, zz, w),),
                     [(o_dn, D_MODEL, 0, False), (z, D_MODEL, 0, False)], [sp['dn_norm_w']],
                     [(D_MODEL, D_MODEL, False, bf)], name="f_gated_norm")
    y_dn = _mm2(og, wt['br_dn'], name="f_br_dn")

    c_q, c_kv, k_rope = _rowwise(
        lambda h, *a: _mla_pre(*a),
        [(pm, KV_LORA, 0, False), (pm, ROPE_PAD, 2, False), (pm, Q_LORA, 2, False),
         (cosb, ROPE_PAD, 0, False), (sinb, ROPE_PAD, 0, False)],
        [sp['q_norm_w'], sp['kv_norm_w']],
        [(Q_LORA, Q_LORA, False, bf), (KV_LORA, KV_LORA, False, bf), (ROPE_PAD, ROPE_PAD, False, bf)], name="f_mla_pre")
    q_nope = _mm2(c_q, wt['uq_nope'], name="f_uq_nope", out_dtype=bf)
    q_rope_pre = _mm2(c_q, wt['uq_rope'], name="f_uq_rope")
    (q_rope,) = _rowwise(lambda h, q, c, s: (_rope(q, c, s),),
                         [(q_rope_pre, HEADS * ROPE_PAD, 0, False), (cosb, ROPE_PAD, 0, False), (sinb, ROPE_PAD, 0, False)],
                         [], [(HEADS * ROPE_PAD, HEADS * ROPE_PAD, False, bf)], name="f_q_rope")
    q_lat = _mm(q_nope, wt['uk'], name="f_q_lat", tb=True, heads=HEADS, a_head='col', b_head='lead', out_head='lead',
                dims=(t, KV_LORA, NOPE), out_dtype=bf)
    out_lat, lse = _attention(q_lat, q_rope, c_kv, k_rope)
    o_mla = _mm(out_lat, wt['uv'], name="f_o_mla", heads=HEADS, a_head='lead', b_head='lead', out_head='col',
                dims=(t, NOPE, KV_LORA), out_dtype=bf)
    y_mla = _mm2(o_mla, wt['br_mla'], name="f_br_mla")

    (mixed,) = _rowwise(lambda h, *a: (_merge(*a),),
                        [(gg, 2 * D_MODEL, 0, False), (y_dn, D_MODEL, 0, False), (y_mla, D_MODEL, 0, False)],
                        [], [(D_MODEL, D_MODEL, False, bf)], name="f_merge")
    attn_out = _mm2(mixed, wt['o'], name="f_o")
    h1, h1b = _rowwise(lambda h, *a: (_ln1(*a),) * 2, [(x, D_MODEL, 0, False), (attn_out, D_MODEL, 0, False)],
                       [sp['ln1_g'], sp['ln1_b']], [(D_MODEL, D_MODEL, False, F32), (D_MODEL, D_MODEL, False, bf)],
                       name="f_ln1")
    act, ffn_gt, ffn_up = _ffn_in_swiglu(h1b, wt['ffn_in'])
    ffn = _mm2(act, wt['ffn_out'], name="f_ffn_out")
    gate_pre = _mm2(h1b, wt['ple_gate'], name="f_ple_gate")
    pb = p.astype(bf)
    ple_proj = _mm2(pb, wt['ple'], name="f_ple")

    def final_fn(h, h1v, ffnv, gpv, ppv, tgt, gv, bv):
        y, vjp = jax.vjp(_final, h1v, ffnv, gpv, ppv, gv, bv)
        err = y - tgt
        dh1, dffn, dgp, dpp, dg, db = vjp(err * (1.0 / D_MODEL))
        sq = err * err
        lanes = sq[:, :LANE]
        for j in range(1, D_MODEL // LANE):
            lanes = lanes + sq[:, j * LANE:(j + 1) * LANE]
        loss = jnp.sum(lanes, axis=0, keepdims=True) * (0.5 / D_MODEL)
        return dffn, dffn, dgp, dpp, dg, db, loss

    dpre2, dpre2b, dgate_pre, dple_proj, g['ln2_g'], g['ln2_b'], loss_lanes = _rowwise(
        final_fn, [(a, D_MODEL, 0, False) for a in (h1, ffn, gate_pre, ple_proj, target)],
        [sp['ln2_g'], sp['ln2_b']],
        [(D_MODEL, D_MODEL, False, F32)] + [(D_MODEL, D_MODEL, False, bf)] * 3,
        [(1, D_MODEL), (1, D_MODEL), (1, LANE)], name="b_final")

    g['ple'] = _mm2(pb, dple_proj, ta=True, name="g_ple")
    g['ple_gate'] = _mm2(h1b, dgate_pre, ta=True, name="g_ple_gate")
    g['ffn_out'] = _mm2(act, dpre2b, ta=True, name="g_ffn_out")
    dact = _mm2(dpre2b, wt['ffn_out'], tb=True, name="b_dact", out_dtype=bf)

    def swiglu_bwd(h, gt, up, d):
        _, vjp = jax.vjp(_swiglu, gt.astype(F32), up.astype(F32))
        dgt, dup = vjp(d.astype(F32))
        return (jnp.concatenate([dgt, dup], axis=1),)

    (dffn_in,) = _rowwise(swiglu_bwd, [(ffn_gt, FFN_HIDDEN, 0, False), (ffn_up, FFN_HIDDEN, 0, False),
                                       (dact, FFN_HIDDEN, 0, False)], [],
                          [(2 * FFN_HIDDEN, 2 * FFN_HIDDEN, False, bf)], name="b_swiglu")
    g['ffn_in'] = _mm2(h1b, dffn_in, ta=True, name="g_ffn_in")
    dh1 = _mm2(dffn_in, wt['ffn_in'], tb=True, name="b_dh1_ffn", add=dpre2, add_scale=ALPHA)
    dh1 = _mm2(dgate_pre, wt['ple_gate'], tb=True, name="b_dh1_gate", add=dh1)

    def ln1_bwd(h, xv, ao, d, gv, bv):
        _, vjp = jax.vjp(_ln1, xv, ao, gv, bv)
        _, dao, dg, db = vjp(d)
        return dao, dao, dg, db

    dpre1, dpre1b, g['ln1_g'], g['ln1_b'] = _rowwise(
        ln1_bwd, [(x, D_MODEL, 0, False), (attn_out, D_MODEL, 0, False), (dh1, D_MODEL, 0, False)],
        [sp['ln1_g'], sp['ln1_b']], [(D_MODEL, D_MODEL, False, F32), (D_MODEL, D_MODEL, False, bf)],
        [(1, D_MODEL), (1, D_MODEL)], name="b_ln1")

    g['o'] = _mm2(mixed, dpre1b, ta=True, name="g_o")
    dmixed = _mm2(dpre1b, wt['o'], tb=True, name="b_dmixed")

    def merge_bwd(h, ggv, yd, ym, d):
        _, vjp = jax.vjp(_merge, ggv, yd, ym)
        return vjp(d)

    dgg, dy_dn, dy_mla = _rowwise(
        merge_bwd, [(gg, 2 * D_MODEL, 0, False), (y_dn, D_MODEL, 0, False), (y_mla, D_MODEL, 0, False),
                    (dmixed, D_MODEL, 0, False)], [],
        [(2 * D_MODEL, 2 * D_MODEL, False, bf), (D_MODEL, D_MODEL, False, bf), (D_MODEL, D_MODEL, False, bf)],
        name="b_merge")
    g['br_dn'] = _mm2(og, dy_dn, ta=True, name="g_br_dn")
    dog = _mm2(dy_dn, wt['br_dn'], tb=True, name="b_dog")
    g['br_mla'] = _mm2(o_mla, dy_mla, ta=True, name="g_br_mla")
    do_mla = _mm2(dy_mla, wt['br_mla'], tb=True, name="b_do_mla", out_dtype=bf)

    dout_lat = _mm(do_mla, wt['uv'], name="b_dout_lat", tb=True, heads=HEADS, a_head='col', b_head='lead',
                   out_head='lead', dims=(t, KV_LORA, NOPE))
    g['uv'] = _mm(out_lat, do_mla, name="g_uv", ta=True, heads=HEADS, a_head='lead', b_head='col', out_head='lead',
                  dims=(KV_LORA, NOPE, t))
    dq_lat, dq_rope, dckv_att, dkr_att = _attention_bwd(q_lat, q_rope, c_kv, k_rope, out_lat, lse, dout_lat)
    dq_nope = _mm(dq_lat, wt['uk'], name="b_dq_nope", heads=HEADS, a_head='lead', b_head='lead', out_head='col',
                  dims=(t, NOPE, KV_LORA), out_dtype=bf)
    g['uk'] = _mm(dq_lat, q_nope, name="g_uk", ta=True, heads=HEADS, a_head='lead', b_head='col', out_head='lead',
                  dims=(KV_LORA, NOPE, t))
    (dq_rope_pre,) = _rowwise(lambda h, d, c, s: (_rope_bwd(d, c, s),),
                              [(dq_rope, HEADS * ROPE_PAD, 0, False), (cosb, ROPE_PAD, 0, False), (sinb, ROPE_PAD, 0, False)],
                              [], [(HEADS * ROPE_PAD, HEADS * ROPE_PAD, False, bf)], name="b_q_rope")
    g['uq_nope'] = _mm2(c_q, dq_nope, ta=True, name="g_uq_nope")
    g['uq_rope'] = _mm2(c_q, dq_rope_pre, ta=True, name="g_uq_rope")
    dc_q = _mm2(dq_nope, wt['uq_nope'], tb=True, name="b_dcq_nope")
    dc_q = _mm2(dq_rope_pre, wt['uq_rope'], tb=True, name="b_dcq_rope", add=dc_q)

    def gated_norm_bwd(h, o, zz, d, w):
        _, vjp = jax.vjp(_gated_norm_heads, o, zz, w)
        return vjp(d)

    do_dn, dz, g['dn_norm_w'] = _rowwise(
        gated_norm_bwd, [(o_dn, D_MODEL, 0, False), (z, D_MODEL, 0, False), (dog, D_MODEL, 0, False)], [sp['dn_norm_w']],
        [(D_MODEL, D_MODEL, False, F32), (D_MODEL, D_MODEL, False, bf)], [(1, LANE)], name="b_gated_norm")
    (du, dw, dqd, dkt, dintra, dgl), paired = _delta_scan_bwd(u, w_, qd, kt, intra, gl, sall, do_dn, rider=exch.pair_send(g))
    (dq_a, dk_a, dv_a, dba, g['a_log'], g['dt_bias']), arrived = _delta_local_bwd(
        qkv_act, pm, sp['a_log'], sp['dt_bias'], t_inv, du, dw, dqd, dkt, dintra, dgl, rider=exch.reduce_send(paired))
    exch.reduce_arrived(arrived)
    dqkv_pre, g['conv_w'] = _conv_silu_bwd(qkv_pre, sp['conv_w'], [dq_a, dk_a, dv_a])

    def mla_pre_bwd(h, ckv, krp, cq, cosv, sinv, dcq, dckv, dkr, qw, kw):
        _, vjp = jax.vjp(lambda a, b, c, d, e: (_rms_norm(c, d), _rms_norm(a, e)), ckv, krp, cq, qw, kw)
        dckv_p, _, dcq_p, dqw, dkw = vjp((dcq, dckv))
        dkr_p = _rope_bwd(dkr, cosv, sinv)
        dpm = jnp.concatenate([dckv_p, dkr_p, jnp.zeros((ckv.shape[0], 3 * LANE), F32), dcq_p], axis=1)
        return dpm, dqw, dkw

    dpm_main, g['q_norm_w'], g['kv_norm_w'] = _rowwise(
        mla_pre_bwd,
        [(pm, KV_LORA, 0, False), (pm, ROPE_PAD, 2, False), (pm, Q_LORA, 2, False),
         (cosb, ROPE_PAD, 0, False), (sinb, ROPE_PAD, 0, False),
         (dc_q, Q_LORA, 0, False), (dckv_att, KV_LORA, 0, False), (dkr_att, ROPE_PAD, 0, False)],
        [sp['q_norm_w'], sp['kv_norm_w']], [(1152, 1152, False, F32)], [(1, Q_LORA), (1, KV_LORA)], name="b_mla_pre")
    (dpm,) = _rowwise(
        lambda h, a, b: (jnp.concatenate([a[:, :3 * LANE], b, a[:, 4 * LANE:]], axis=1),),
        [(dpm_main, 1152, 0, False), (dba, LANE, 0, False)], [], [(1152, 1152, False, bf)], name="b_dpm")

    g['qkv'] = _mm2(xb, dqkv_pre, ta=True, name="g_qkv")
    g['z'] = _mm2(xb, dz, ta=True, name="g_z")
    g['gg'] = _mm2(xb, dgg, ta=True, name="g_gg")
    g['mla'] = _mm2(xb, dpm, ta=True, name="g_mla")
    dx, arrived = _dx_fused([(dqkv_pre, wt['qkv']), (dz, wt['z']), (dgg, wt['gg']), (dpm, wt['mla'])], dpre1, ALPHA,
                            rider=exch.in_send(g))
    exch.in_arrived(arrived)
    return loss_lanes, dx, g


_IN_SIZES = (QKV_W, HEADS * DN_DK, HEADS, HEADS, Q_LORA, KV_LORA, ROPE, D_MODEL, D_MODEL)


def _rope_tables(positions):
    inv_freq = ROPE_BASE ** (-jnp.arange(0, ROPE, 2, dtype=F32) / ROPE)
    ang = positions.astype(F32)[:, None] * inv_freq
    cos, sin = jnp.cos(ang), jnp.sin(ang)
    zeros = jnp.zeros((positions.shape[0], ROPE_PAD - ROPE), F32)
    return jnp.concatenate([cos, cos, zeros], axis=1), jnp.concatenate([-sin, sin, zeros], axis=1)


def _prep_w_in(w_in):
    dt = w_in.dtype
    offs = [0]
    for s in _IN_SIZES:
        offs.append(offs[-1] + s)
    qkv, z, wb, wa, cq, ckv, kr, gd, gm = [w_in[:, offs[i]:offs[i + 1]] for i in range(len(_IN_SIZES))]
    zc = lambda n: jnp.zeros((D_MODEL, n), dt)
    return {
        'qkv': qkv, 'z': z, 'gg': jnp.concatenate([gd, gm], axis=1),
        'mla': jnp.concatenate([ckv, kr, zc(ROPE_PAD - ROPE), wb, wa, zc(LANE - 2 * HEADS), zc(2 * LANE), cq], axis=1),
    }


def _prep_weights(full):
    w_uq = full['w_uq']
    wt = {
        'uq_nope': w_uq[:, :, :NOPE].reshape(Q_LORA, HEADS * NOPE),
        'uq_rope': jnp.pad(w_uq[:, :, NOPE:], ((0, 0), (0, 0), (0, ROPE_PAD - ROPE))).reshape(Q_LORA, HEADS * ROPE_PAD),
        'uk': jnp.transpose(full['w_uk'], (1, 0, 2)), 'uv': jnp.transpose(full['w_uv'], (1, 0, 2)),
        'br_dn': full['w_br_dn'], 'br_mla': full['w_br_mla'], 'o': full['w_o'], 'ffn_in': full['w_ffn_in'],
        'ffn_out': full['w_ffn_out'], 'ple': full['w_ple'], 'ple_gate': full['w_ple_gate'],
    }
    return wt


def _prep_small(small):
    pad = lambda v: jnp.pad(v, (0, LANE - v.shape[0]))[None, :]
    return {
        'conv_w': small['conv_w'], 'a_log': pad(small['dn_a_log']), 'dt_bias': pad(small['dn_dt_bias']),
        'dn_norm_w': small['dn_norm_w'][None, :], 'q_norm_w': small['q_norm_w'][None, :],
        'kv_norm_w': small['kv_norm_w'][None, :], 'ln1_g': small['ln1_g'][None, :], 'ln1_b': small['ln1_b'][None, :],
        'ln2_g': small['ln2_g'][None, :], 'ln2_b': small['ln2_b'][None, :],
    }


def _w_in_grad(g):
    mla = g['mla']
    ba0 = KV_LORA + ROPE_PAD
    cq0 = ba0 + 3 * LANE
    return jnp.concatenate([
        g['qkv'], g['z'], mla[:, ba0:ba0 + HEADS], mla[:, ba0 + HEADS:ba0 + 2 * HEADS], mla[:, cq0:cq0 + Q_LORA],
        mla[:, :KV_LORA], mla[:, KV_LORA:KV_LORA + ROPE], g['gg']], axis=1)


def _unprep_grads_late(g):
    return {
        'conv_w': g['conv_w'], 'dn_a_log': g['a_log'][0, :HEADS], 'dn_dt_bias': g['dt_bias'][0, :HEADS],
        'dn_norm_w': g['dn_norm_w'][0], 'q_norm_w': g['q_norm_w'][0], 'kv_norm_w': g['kv_norm_w'][0],
        'ln1_g': g['ln1_g'][0], 'ln1_b': g['ln1_b'][0], 'ln2_g': g['ln2_g'][0], 'ln2_b': g['ln2_b'][0],
    }


def _unprep_grads_early(g):
    w_uq = jnp.concatenate([g['uq_nope'].reshape(Q_LORA, HEADS, NOPE),
                            g['uq_rope'].reshape(Q_LORA, HEADS, ROPE_PAD)[:, :, :ROPE]], axis=2)
    return {
        'w_uq': w_uq, 'w_uk': jnp.transpose(g['uk'], (1, 0, 2)), 'w_uv': jnp.transpose(g['uv'], (1, 0, 2)),
        'w_br_dn': g['br_dn'], 'w_br_mla': g['br_mla'], 'w_o': g['o'],
        'w_ffn_in': g['ffn_in'], 'w_ffn_out': g['ffn_out'], 'w_ple': g['ple'], 'w_ple_gate': g['ple_gate'],
    }


_FLATB_PIECES = (
    ('w_ffn_out', 704, (704, D_MODEL)), ('w_br_dn', 256, (256, D_MODEL)), ('w_br_mla', 256, (256, D_MODEL)),
    ('w_o', 256, (256, D_MODEL)), ('w_ple_gate', 256, (256, D_MODEL)), ('w_uq', 144, (96, HEADS, NOPE + ROPE)),
    ('w_uk', 64, (64, HEADS, NOPE)), ('w_uv', 64, (64, HEADS, NOPE)), ('w_ple', 64, (PLE_DIM, 256)),
)
FLATB_ROWS = 2112
W_IN_SHARD = D_IN // N_SHARD
FFN_IN_SHARD = 2 * FFN_HIDDEN // N_SHARD
A_ROWS = D_MODEL + 32
_CONV_SHARD = QKV_W // N_SHARD
_ADD_TILES = (256, 256, 352)


def _flatb_offsets():
    offs, o = {}, 0
    for name, rows, _ in _FLATB_PIECES:
        offs[name] = o
        o += rows
    return offs, o


def _pack_shards(ws, conv_w):
    conv_bits = lax.bitcast_convert_type(conv_w, jnp.bfloat16).reshape(DN_CONV, 2 * _CONV_SHARD).astype(_CDT)
    tail = jnp.pad(conv_bits, ((0, A_ROWS - D_MODEL - DN_CONV), (0, W_IN_SHARD - 2 * _CONV_SHARD)))
    a_buf = jnp.concatenate([ws['w_in'].astype(_CDT), tail], axis=0)
    parts = [ws[name].astype(_CDT).reshape(rows, FLAT_W) for name, rows, _ in _FLATB_PIECES]
    used = sum(p.shape[0] for p in parts)
    parts.append(jnp.zeros((FLATB_ROWS - used, FLAT_W), _CDT))
    return [a_buf, ws['w_ffn_in'].astype(_CDT), jnp.concatenate(parts, axis=0)]


def _unpack_w_in(gathered, local, me):
    a = [jnp.where(me == s, local, gathered[s]) for s in range(N_SHARD)]
    conv = [lax.bitcast_convert_type(
        p[D_MODEL:D_MODEL + DN_CONV, :2 * _CONV_SHARD].astype(jnp.bfloat16).reshape(DN_CONV, _CONV_SHARD, 2), F32) for p in a]
    return jnp.concatenate([p[:D_MODEL] for p in a], axis=1), jnp.concatenate(conv, axis=1)


def _unpack_rest(gathered, local, me):
    pick = lambda b, s: jnp.where(me == s, local[b], gathered[b][s])
    full = {'w_ffn_in': jnp.concatenate([pick(0, s) for s in range(N_SHARD)], axis=1)}
    offs, _ = _flatb_offsets()
    fb = [pick(1, s) for s in range(N_SHARD)]
    for name, rows, shape in _FLATB_PIECES:
        pieces = [p[offs[name]:offs[name] + rows].reshape(shape) for p in fb]
        full[name] = jnp.concatenate(pieces, axis=1 if name == 'w_ple' else 0)
    return full


def _shard_columns(g, w):
    return jnp.stack([g[:, s * w:(s + 1) * w] for s in range(N_SHARD)])


def _pack_grads_rest(gw):
    parts = []
    for name, rows, _ in _FLATB_PIECES:
        g = gw[name]
        if name == 'w_ple':
            parts.append(_shard_columns(g, PLE_DIM).reshape(N_SHARD, rows, FLAT_W))
        else:
            parts.append(g.reshape(N_SHARD, rows, FLAT_W))
    used = sum(p.shape[1] for p in parts)
    parts.append(jnp.zeros((N_SHARD, FLATB_ROWS - used, FLAT_W), F32))
    return [_shard_columns(gw['w_ffn_in'], FFN_IN_SHARD), jnp.concatenate(parts, axis=1)]


def _unpack_reduced(mine, theirs, c):
    whole = [jnp.concatenate([jnp.where(c == 0, m, t), jnp.where(c == 0, t, m)], axis=0) for m, t in zip(mine, theirs)]
    out = {'w_in': whole[0], 'w_ffn_in': whole[1]}
    offs, _ = _flatb_offsets()
    for name, rows, shape in _FLATB_PIECES:
        out[name] = whole[2][offs[name]:offs[name] + rows].reshape(shape)
    return out


_HBM = pl.BlockSpec(memory_space=pltpu.HBM)


def _place():
    x, y, c = lax.axis_index("x"), lax.axis_index("y"), lax.axis_index("c")
    chips = [(1 - x, y), (x, 1 - y), (1 - x, 1 - y)]
    return x, y, c, chips


def _remote(src, dst, send_sems, recv_sems, k, to):
    return pltpu.make_async_remote_copy(src_ref=src, dst_ref=dst, send_sem=send_sems.at[k], recv_sem=recv_sems.at[k],
                                        device_id=to, device_id_type=_MESH)


def _half_rows(ref, half, hf, lead=None):
    rows = pl.ds(pl.multiple_of(hf * half, 16), half)
    return ref.at[rows, :] if lead is None else ref.at[lead, rows, :]


class _Rider:
    def __init__(self, inputs, out_shape, n_sems, copies, aliases=None):
        self.inputs, self.out_shape, self.n_sems, self.copies = list(inputs), list(out_shape), n_sems, copies
        self.aliases = aliases or {}


def _carried_call(body, rider, first, last, *, name, grid, in_specs, out_specs, out_shape, scratch_shapes, sem, args):
    n_in, n_out, n_scr = len(in_specs), len(out_specs), len(scratch_shapes)
    if rider is None:
        res = pl.pallas_call(body, name=name, grid=grid, in_specs=in_specs, out_specs=out_specs, out_shape=out_shape,
                             scratch_shapes=scratch_shapes, compiler_params=_cparams(sem))(*args)
        return list(res), []
    ri, ro = len(rider.inputs), len(rider.out_shape)

    def full_body(*refs):
        own_in, r_in = refs[:n_in], refs[n_in:n_in + ri]
        o0 = n_in + ri
        own_out, r_out = refs[o0:o0 + n_out], refs[o0 + n_out:o0 + n_out + ro]
        s0 = o0 + n_out + ro
        own_scr, send_sems, recv_sems = refs[s0:s0 + n_scr], refs[s0 + n_scr], refs[s0 + n_scr + 1]

        @pl.when(first())
        def _():
            sends, _ = rider.copies(r_in, r_out, send_sems, recv_sems)
            for cp in sends:
                cp.start()

        body(*own_in, *own_out, *own_scr)

        @pl.when(last())
        def _():
            sends, arrivals = rider.copies(r_in, r_out, send_sems, recv_sems)
            for cp in arrivals():
                cp.wait_recv()
            for cp in sends:
                cp.wait_send()

    res = pl.pallas_call(
        full_body, name=name, grid=grid, in_specs=list(in_specs) + [_HBM] * ri, out_specs=list(out_specs) + [_HBM] * ro,
        out_shape=list(out_shape) + rider.out_shape,
        scratch_shapes=list(scratch_shapes) + [pltpu.SemaphoreType.DMA((rider.n_sems,))] * 2,
        input_output_aliases={n_in + i: n_out + o for i, o in rider.aliases.items()},
        compiler_params=_cparams(sem))(*args, *rider.inputs)
    return list(res[:n_out]), list(res[n_out:])


def _ride_gather_send(bufs):
    n = len(bufs)
    halves = [b.shape[0] // 2 for b in bufs]

    def copies(ins, outs, send_sems, recv_sems):
        x, y, c, chips = _place()
        slot = lambda b, cx, cy: _half_rows(outs[b], halves[b], c, lead=2 * cx + cy)
        sends = [_remote(_half_rows(ins[b], halves[b], c), slot(b, x, y), send_sems, recv_sems, 3 * b + j, (cx, cy, c))
                 for b in range(n) for j, (cx, cy) in enumerate(chips)]
        arrivals = lambda: [_remote(slot(b, cx, cy), slot(b, cx, cy), send_sems, recv_sems, 3 * b + j, (x, y, c))
                            for b in range(n) for j, (cx, cy) in enumerate(chips)]
        return sends, arrivals

    return _Rider(bufs, [jax.ShapeDtypeStruct((N_SHARD,) + b.shape, b.dtype) for b in bufs], 3 * n, copies)


def _ride_gather_pass(gathered):
    n = len(gathered)
    halves = [g.shape[1] // 2 for g in gathered]

    def copies(ins, outs, send_sems, recv_sems):
        x, y, c, chips = _place()
        slot = lambda b, cx, cy, hf: _half_rows(outs[b], halves[b], hf, lead=2 * cx + cy)
        sends = [_remote(slot(b, cx, cy, c), slot(b, cx, cy, c), send_sems, recv_sems, 3 * b + j, (x, y, 1 - c))
                 for b in range(n) for j, (cx, cy) in enumerate(chips)]
        arrivals = lambda: [_remote(slot(b, cx, cy, 1 - c), slot(b, cx, cy, 1 - c), send_sems, recv_sems, 3 * b + j, (x, y, c))
                            for b in range(n) for j, (cx, cy) in enumerate(chips)]
        return sends, arrivals

    return _Rider(gathered, [jax.ShapeDtypeStruct(g.shape, g.dtype) for g in gathered], 3 * n, copies,
                  aliases={b: b for b in range(n)})


def _ride_pair_exchange(gbufs):
    n = len(gbufs)
    halves = [g.shape[1] // 2 for g in gbufs]

    def copies(ins, outs, send_sems, recv_sems):
        x, y, c, _ = _place()
        sends = [_remote(ins[b].at[:, pl.ds(pl.multiple_of((1 - c) * halves[b], 16), halves[b]), :], outs[b],
                         send_sems, recv_sems, b, (x, y, 1 - c)) for b in range(n)]
        arrivals = lambda: [_remote(outs[b], outs[b], send_sems, recv_sems, b, (x, y, c)) for b in range(n)]
        return sends, arrivals

    return _Rider(gbufs, [jax.ShapeDtypeStruct((N_SHARD, h, g.shape[2]), g.dtype) for g, h in zip(gbufs, halves)], n, copies)


def _ride_chip_exchange(parts):
    n = len(parts)

    def copies(ins, outs, send_sems, recv_sems):
        x, y, c, chips = _place()
        sends = [_remote(ins[b].at[2 * cx + cy], outs[b].at[j], send_sems, recv_sems, 3 * b + j, (cx, cy, c))
                 for b in range(n) for j, (cx, cy) in enumerate(chips)]
        arrivals = lambda: [_remote(ins[b].at[0], outs[b].at[j], send_sems, recv_sems, 3 * b + j, (x, y, c))
                            for b in range(n) for j in range(len(chips))]
        return sends, arrivals

    return _Rider(parts, [jax.ShapeDtypeStruct((3,) + p.shape[1:], p.dtype) for p in parts], 3 * n, copies)


def _gather_shards(bufs, name):
    n = len(bufs)
    halves = [b.shape[0] // 2 for b in bufs]

    def body(*refs):
        ins, outs, send_sems, recv_sems = refs[:n], refs[n:2 * n], refs[2 * n], refs[2 * n + 1]
        x, y, c, chips = _place()
        me, sibling = (x, y, c), (x, y, 1 - c)
        slot = lambda b, cx, cy, hf: _half_rows(outs[b], halves[b], hf, lead=2 * cx + cy)
        first = [_remote(_half_rows(ins[b], halves[b], c), slot(b, x, y, c), send_sems, recv_sems, 6 * b + j, (cx, cy, c))
                 for b in range(n) for j, (cx, cy) in enumerate(chips)]
        for cp in first:
            cp.start()
        passed = []
        for j, (cx, cy) in enumerate(chips):
            for b in range(n):
                _remote(slot(b, cx, cy, c), slot(b, cx, cy, c), send_sems, recv_sems, 6 * b + j, me).wait_recv()
                fwd = _remote(slot(b, cx, cy, c), slot(b, cx, cy, c), send_sems, recv_sems, 6 * b + 3 + j, sibling)
                fwd.start()
                passed.append(fwd)
        for j, (cx, cy) in enumerate(chips):
            for b in range(n):
                _remote(slot(b, cx, cy, 1 - c), slot(b, cx, cy, 1 - c), send_sems, recv_sems, 6 * b + 3 + j, me).wait_recv()
        for cp in first + passed:
            cp.wait_send()

    return pl.pallas_call(
        body, name=name, out_shape=[jax.ShapeDtypeStruct((N_SHARD,) + b.shape, b.dtype) for b in bufs],
        in_specs=[_HBM] * n, out_specs=[_HBM] * n,
        scratch_shapes=[pltpu.SemaphoreType.DMA((6 * n,)), pltpu.SemaphoreType.DMA((6 * n,))],
    )(*bufs)


def _reduce_pair_exchange(gbufs, name):
    n = len(gbufs)
    halves = [g.shape[1] // 2 for g in gbufs]

    def body(*refs):
        ins, outs, send_sems, recv_sems = refs[:n], refs[n:2 * n], refs[2 * n], refs[2 * n + 1]
        x, y, c, _ = _place()
        cps = [_remote(ins[b].at[:, pl.ds(pl.multiple_of((1 - c) * halves[b], 16), halves[b]), :], outs[b],
                       send_sems, recv_sems, b, (x, y, 1 - c)) for b in range(n)]
        for cp in cps:
            cp.start()
        for cp in cps:
            cp.wait()

    return pl.pallas_call(
        body, name=name,
        out_shape=[jax.ShapeDtypeStruct((N_SHARD, h, g.shape[2]), g.dtype) for g, h in zip(gbufs, halves)],
        in_specs=[_HBM] * n, out_specs=[_HBM] * n,
        scratch_shapes=[pltpu.SemaphoreType.DMA((n,)), pltpu.SemaphoreType.DMA((n,))],
    )(*gbufs)


def _pair_add(gbuf, recv, c_arr, tr, name):
    _, rows, width = gbuf.shape
    half = rows // 2
    nt = half // tr

    def body(c_ref, a_ref, b_ref, o_ref):
        o_ref[...] = (a_ref[...] + b_ref[...]).astype(o_ref.dtype)

    blk = lambda f: pl.BlockSpec((None, tr, width), f)
    return pl.pallas_call(
        body, name=name, out_shape=jax.ShapeDtypeStruct((N_SHARD, half, width), jnp.bfloat16),
        grid_spec=pltpu.PrefetchScalarGridSpec(
            num_scalar_prefetch=1, grid=(N_SHARD, nt),
            in_specs=[blk(lambda s, i, c: (s, c[0] * nt + i, 0)), blk(lambda s, i, c: (s, i, 0))],
            out_specs=blk(lambda s, i, c: (s, i, 0))),
        compiler_params=_cparams(("parallel", "parallel")))(c_arr, gbuf, recv)


def _reduce_chip_exchange(parts, name):
    n = len(parts)

    def body(*refs):
        ins, outs, send_sems, recv_sems = refs[:n], refs[n:2 * n], refs[2 * n], refs[2 * n + 1]
        x, y, c, chips = _place()
        sends = [_remote(ins[b].at[2 * cx + cy], outs[b].at[j], send_sems, recv_sems, 3 * b + j, (cx, cy, c))
                 for b in range(n) for j, (cx, cy) in enumerate(chips)]
        for cp in sends:
            cp.start()
        for b in range(n):
            for j in range(len(chips)):
                _remote(ins[b].at[0], outs[b].at[j], send_sems, recv_sems, 3 * b + j, (x, y, c)).wait_recv()
        for cp in sends:
            cp.wait_send()

    return pl.pallas_call(
        body, name=name, out_shape=[jax.ShapeDtypeStruct((3,) + p.shape[1:], p.dtype) for p in parts],
        in_specs=[_HBM] * n, out_specs=[_HBM] * n,
        scratch_shapes=[pltpu.SemaphoreType.DMA((3 * n,)), pltpu.SemaphoreType.DMA((3 * n,))],
    )(*parts)


def _chip_add(part, recv, me_arr, tr, name):
    _, half, width = part.shape

    def body(me_ref, own, a0, a1, a2, o_ref):
        f = lambda r: r[...].astype(F32)
        o_ref[...] = ((f(own) + f(a0)) + f(a1)) + f(a2)

    specs = [pl.BlockSpec((None, tr, width), lambda i, me: (me[0], i, 0))]
    specs += [pl.BlockSpec((None, tr, width), functools.partial(lambda i, me, k: (k, i, 0), k=k)) for k in range(3)]
    return pl.pallas_call(
        body, name=name, out_shape=jax.ShapeDtypeStruct((half, width), F32),
        grid_spec=pltpu.PrefetchScalarGridSpec(
            num_scalar_prefetch=1, grid=(half // tr,), in_specs=specs,
            out_specs=pl.BlockSpec((tr, width), lambda i, me: (i, 0))),
        compiler_params=_cparams(("parallel",)))(me_arr, part, recv, recv, recv)


def _reduce_pair_share(rhalves, name):
    n = len(rhalves)

    def body(*refs):
        ins, outs, send_sems, recv_sems = refs[:n], refs[n:2 * n], refs[2 * n], refs[2 * n + 1]
        x, y, c, _ = _place()
        cps = [_remote(ins[b], outs[b], send_sems, recv_sems, b, (x, y, 1 - c)) for b in range(n)]
        for cp in cps:
            cp.start()
        for cp in cps:
            cp.wait()

    return pl.pallas_call(
        body, name=name, out_shape=[jax.ShapeDtypeStruct(r.shape, r.dtype) for r in rhalves],
        in_specs=[_HBM] * n, out_specs=[_HBM] * n,
        scratch_shapes=[pltpu.SemaphoreType.DMA((n,)), pltpu.SemaphoreType.DMA((n,))],
    )(*rhalves)


def _small_allreduce(buf):
    r, width = buf.shape
    n_dev = 8

    def body(x_ref, all_ref, sum_ref, send_sems, recv_sems, local_sem):
        x, y, c, chips = _place()
        me, sibling = (x, y, c), (x, y, 1 - c)

        def rows(px, py, pc):
            return all_ref.at[pl.ds(pl.multiple_of((4 * px + 2 * py + pc) * r, 8), r), :]

        def copy(k, block, to, src=None):
            return _remote(rows(*block) if src is None else src, rows(*block), send_sems, recv_sems, k, to)

        mine = pltpu.make_async_copy(x_ref, rows(*me), local_sem)
        mine.start()
        first = [copy(0, me, sibling, src=x_ref)]
        first += [copy(1 + j, me, (*chip, c), src=x_ref) for j, chip in enumerate(chips)]
        for cp in first:
            cp.start()
        passed = [copy(4 + j, (*chip, c), sibling) for j, chip in enumerate(chips)]
        for j, chip in enumerate(chips):
            copy(1 + j, (*chip, c), me).wait_recv()
            passed[j].start()
        copy(0, sibling, me).wait_recv()
        for j, chip in enumerate(chips):
            copy(4 + j, (*chip, 1 - c), me).wait_recv()
        for cp in first + passed:
            cp.wait_send()
        mine.wait()
        total = all_ref[0:r, :]
        for k in range(1, n_dev):
            total = total + all_ref[k * r:(k + 1) * r, :]
        sum_ref[...] = total

    vm = pl.BlockSpec(memory_space=pltpu.VMEM)
    _, total = pl.pallas_call(
        body, name="small_allreduce",
        out_shape=[jax.ShapeDtypeStruct((n_dev * r, width), buf.dtype), jax.ShapeDtypeStruct((r, width), buf.dtype)],
        in_specs=[vm], out_specs=[vm, vm],
        scratch_shapes=[pltpu.SemaphoreType.DMA((7,)), pltpu.SemaphoreType.DMA((7,)), pltpu.SemaphoreType.DMA],
    )(buf)
    return total


def _row_tile(rows, cap):
    if rows <= cap:
        return rows
    t = (cap // 8) * 8
    while t >= 8:
        if rows % t == 0:
            return t
        t -= 8
    return rows


def _adamw(w, g, m, v, name):
    shape = w.shape
    cols = shape[-1] if len(shape) <= 3 else shape[-2] * shape[-1]
    lead = len(shape) == 3
    w2, g2, m2, v2 = (a if lead else a.reshape(-1, cols) for a in (w, g, m, v))
    rows = shape[1] if lead else w2.shape[0]
    tr, tc = _row_tile(rows, 256), cols
    if tr == rows and rows > 256:
        tc = _tile(cols, 256)

    def body(w_ref, g_ref, m_ref, v_ref, d_ref, mo_ref, vo_ref):
        gv = g_ref[...]
        mn = ADAM_B1 * m_ref[...] + (1.0 - ADAM_B1) * gv
        vn = ADAM_B2 * v_ref[...] + (1.0 - ADAM_B2) * (gv * gv)
        m_hat = mn / (1.0 - ADAM_B1 ** ADAM_STEP)
        v_hat = vn / (1.0 - ADAM_B2 ** ADAM_STEP)
        d_ref[...] = -ADAM_LR * (m_hat / (jnp.sqrt(v_hat) + ADAM_EPS) + ADAM_WD * w_ref[...])
        mo_ref[...] = mn
        vo_ref[...] = vn

    blk = (pl.BlockSpec((None, tr, tc), lambda i, j: (0, i, j)) if lead else pl.BlockSpec((tr, tc), lambda i, j: (i, j)))
    outs = pl.pallas_call(
        body, name=name, grid=(rows // tr, cols // tc), in_specs=[blk] * 4, out_specs=[blk] * 3,
        out_shape=[jax.ShapeDtypeStruct(w2.shape, F32)] * 3,
        compiler_params=_cparams(("parallel", "parallel")))(w2, g2, m2, v2)
    return tuple(o.reshape(shape) for o in outs)


_WEIGHT_NAMES = ('w_in', 'conv_w', 'dn_a_log', 'dn_dt_bias', 'dn_norm_w', 'q_norm_w', 'w_uq', 'kv_norm_w', 'w_uk',
                 'w_uv', 'w_br_dn', 'w_br_mla', 'w_o', 'ln1_g', 'ln1_b', 'w_ffn_in', 'w_ffn_out', 'w_ple',
                 'w_ple_gate', 'ln2_g', 'ln2_b')
_SMALL_NAMES = ('ln1_g', 'ln1_b', 'ln2_g', 'ln2_b', 'q_norm_w', 'kv_norm_w', 'dn_norm_w', 'dn_a_log', 'dn_dt_bias')
_SMALL_GROUP = 8
_CONV_SMALL_ROW = len(_SMALL_NAMES) * _SMALL_GROUP
_CONV_SMALL_ROWS = DN_CONV * QKV_W // FLAT_W


def _pack_small(gw):
    rows = [jnp.pad(gw[n][None, :], ((0, _SMALL_GROUP - 1), (0, FLAT_W - gw[n].shape[0]))) for n in _SMALL_NAMES]
    rows.append(jnp.pad(gw['conv_w'].reshape(_CONV_SMALL_ROWS, FLAT_W), ((0, SMALL_ROWS - _CONV_SMALL_ROW - _CONV_SMALL_ROWS), (0, 0))))
    return jnp.concatenate(rows, axis=0)


class _Exchange:
    def __init__(self, local, me_chip, c_arr):
        self.local, self.me_chip, self.c_arr = local, me_chip, c_arr
        self.parts = self.arrived = None

    def gather_send(self):
        return _ride_gather_send(self.local)

    def gather_pass(self, sent):
        return _ride_gather_pass(sent)

    def weights(self, gathered):
        return _prep_weights(_unpack_rest(gathered, self.local, self.me_chip))

    def pair_send(self, g):
        self.gbufs = _pack_grads_rest(_unprep_grads_early(g))
        return _ride_pair_exchange(self.gbufs)

    def reduce_send(self, got):
        self.parts = [_pair_add(g_, r_, self.c_arr, tr, "pair_add_%d" % (i + 1))
                      for i, (g_, r_, tr) in enumerate(zip(self.gbufs, got, _ADD_TILES[1:]))]
        return _ride_chip_exchange(self.parts)

    def reduce_arrived(self, arrived):
        self.arrived = list(arrived)

    def in_send(self, g):
        g_in = [_shard_columns(_w_in_grad(g), W_IN_SHARD)]
        got = _reduce_pair_exchange(g_in, "reduce_pair_exchange_w_in")
        self.part_in = _pair_add(g_in[0], got[0], self.c_arr, _ADD_TILES[0], "pair_add_0")
        return _ride_chip_exchange([self.part_in])

    def in_arrived(self, arrived):
        self.arrived_in = list(arrived)


def kernel(x, p, positions, w_in, conv_w, dn_a_log, dn_dt_bias, dn_norm_w, q_norm_w, w_uq, kv_norm_w, w_uk, w_uv, w_br_dn, w_br_mla, w_o, ln1_g, ln1_b, w_ffn_in, w_ffn_out, w_ple, w_ple_gate, ln2_g, ln2_b, loss_target, m_w_in, m_conv_w, m_dn_a_log, m_dn_dt_bias, m_dn_norm_w, m_q_norm_w, m_w_uq, m_kv_norm_w, m_w_uk, m_w_uv, m_w_br_dn, m_w_br_mla, m_w_o, m_ln1_g, m_ln1_b, m_w_ffn_in, m_w_ffn_out, m_w_ple, m_w_ple_gate, m_ln2_g, m_ln2_b, v_w_in, v_conv_w, v_dn_a_log, v_dn_dt_bias, v_dn_norm_w, v_q_norm_w, v_w_uq, v_kv_norm_w, v_w_uk, v_w_uv, v_w_br_dn, v_w_br_mla, v_w_o, v_ln1_g, v_ln1_b, v_w_ffn_in, v_w_ffn_out, v_w_ple, v_w_ple_gate, v_ln2_g, v_ln2_b):
    ws = dict(w_in=w_in, conv_w=conv_w, dn_a_log=dn_a_log, dn_dt_bias=dn_dt_bias, dn_norm_w=dn_norm_w, q_norm_w=q_norm_w,
              w_uq=w_uq, kv_norm_w=kv_norm_w, w_uk=w_uk, w_uv=w_uv, w_br_dn=w_br_dn, w_br_mla=w_br_mla, w_o=w_o,
              ln1_g=ln1_g, ln1_b=ln1_b, w_ffn_in=w_ffn_in, w_ffn_out=w_ffn_out, w_ple=w_ple, w_ple_gate=w_ple_gate,
              ln2_g=ln2_g, ln2_b=ln2_b)
    ms = dict(w_in=m_w_in, conv_w=m_conv_w, dn_a_log=m_dn_a_log, dn_dt_bias=m_dn_dt_bias, dn_norm_w=m_dn_norm_w,
              q_norm_w=m_q_norm_w, w_uq=m_w_uq, kv_norm_w=m_kv_norm_w, w_uk=m_w_uk, w_uv=m_w_uv, w_br_dn=m_w_br_dn,
              w_br_mla=m_w_br_mla, w_o=m_w_o, ln1_g=m_ln1_g, ln1_b=m_ln1_b, w_ffn_in=m_w_ffn_in, w_ffn_out=m_w_ffn_out,
              w_ple=m_w_ple, w_ple_gate=m_w_ple_gate, ln2_g=m_ln2_g, ln2_b=m_ln2_b)
    vs = dict(w_in=v_w_in, conv_w=v_conv_w, dn_a_log=v_dn_a_log, dn_dt_bias=v_dn_dt_bias, dn_norm_w=v_dn_norm_w,
              q_norm_w=v_q_norm_w, w_uq=v_w_uq, kv_norm_w=v_kv_norm_w, w_uk=v_w_uk, w_uv=v_w_uv, w_br_dn=v_w_br_dn,
              w_br_mla=v_w_br_mla, w_o=v_w_o, ln1_g=v_ln1_g, ln1_b=v_ln1_b, w_ffn_in=v_w_ffn_in, w_ffn_out=v_w_ffn_out,
              w_ple=v_w_ple, w_ple_gate=v_w_ple_gate, ln2_g=v_ln2_g, ln2_b=v_ln2_b)
    mx, my, mc = lax.axis_index("x"), lax.axis_index("y"), lax.axis_index("c")

    me_chip = 2 * mx + my
    c_arr = jnp.reshape(mc, (1,)).astype(jnp.int32)
    me_arr = jnp.reshape(me_chip, (1,)).astype(jnp.int32)
    sharded = ('w_in', 'w_ffn_in') + tuple(name for name, _, _ in _FLATB_PIECES)

    local = _pack_shards({name: ws[name][0] for name in sharded}, conv_w[0])
    (gathered_in,) = _gather_shards(local[:1], "gather_w_in")
    w_in_full, conv_full = _unpack_w_in(gathered_in, local[0], me_chip)
    small = {n: ws[n][0] for n in _SMALL_NAMES}
    small['conv_w'] = conv_full
    sp = _prep_small(small)
    cosb, sinb = _rope_tables(positions[0])
    exch = _Exchange(local[1:], me_chip, c_arr)

    loss_lanes, dx, g = _local_step(x[0], p[0, 0], cosb, sinb, loss_target[0], _prep_w_in(w_in_full), sp, exch)
    gw = _unprep_grads_late(g)
    loss = lax.psum(jnp.sum(loss_lanes), ("x", "y", "c"))

    parts = [exch.part_in] + exch.parts
    arrived = exch.arrived_in + exch.arrived
    mine = [_chip_add(p_, r_, me_arr, tr, "chip_add_%d" % i) for i, (p_, r_, tr) in enumerate(zip(parts, arrived, _ADD_TILES))]
    reduced = _unpack_reduced(mine, _reduce_pair_share(mine, "reduce_pair_share"), mc)
    tot = _small_allreduce(_pack_small(gw))
    gred = {name: reduced[name][None] for name in sharded}
    for i, n in enumerate(_SMALL_NAMES):
        gred[n] = tot[i * _SMALL_GROUP, :ws[n].shape[1]][None]
    conv_tot = tot[_CONV_SMALL_ROW:_CONV_SMALL_ROW + _CONV_SMALL_ROWS].reshape(DN_CONV, QKV_W)
    gred['conv_w'] = lax.dynamic_slice_in_dim(conv_tot, (2 * mx + my) * _CONV_SHARD, _CONV_SHARD, axis=1)[None]

    deltas, new_m, new_v = {}, {}, {}
    for n in _WEIGHT_NAMES:
        if n == 'w_in':
            tr_ = lambda a: jnp.transpose(a, (0, 2, 1))
            g_t = tr_(gred[n].reshape(ws[n].shape))
            outs = _adamw(tr_(ws[n]), g_t, tr_(ms[n]), tr_(vs[n]), "adamw_" + n)
            gred[n] = tr_(g_t)
            deltas[n], new_m[n], new_v[n] = (tr_(o) for o in outs)
            continue
        gred[n] = gred[n].reshape(ws[n].shape)
        deltas[n], new_m[n], new_v[n] = _adamw(ws[n], gred[n], ms[n], vs[n], "adamw_" + n)
    return (loss, dx[None], *[gred[n] for n in _WEIGHT_NAMES], *[deltas[n] for n in _WEIGHT_NAMES],
            *[new_m[n] for n in _WEIGHT_NAMES], *[new_v[n] for n in _WEIGHT_NAMES])
```

```python
import functools
import math

import jax
import jax.numpy as jnp
from jax import lax
from jax.experimental import pallas as pl
from jax.experimental.pallas import tpu as pltpu

F32 = jnp.float32
_CDT = jnp.bfloat16
_HI = lax.Precision.HIGHEST
_MESH = pl.DeviceIdType.MESH

D_MODEL = 1024
PLE_DIM = 256
HEADS = 8
DN_DK = 128
DN_CHUNK = 64
DN_CONV = 4
QKV_W = 3 * HEADS * DN_DK
Q_LORA = 384
KV_LORA = 256
NOPE = 128
ROPE = 64
ROPE_PAD = 128
FFN_HIDDEN = 2816
D_IN = 6864
ROPE_BASE = 10000.0
ALPHA = 2.0 ** 0.25
ATT_SCALE = (NOPE + ROPE) ** -0.5
NEG_BIG = -1e30
ADAM_LR, ADAM_B1, ADAM_B2, ADAM_EPS, ADAM_WD, ADAM_STEP = 0.001, 0.9, 0.999, 1e-08, 0.01, 10

LANE = 128
VMEM_LIMIT = 56 * 1024 * 1024
MM_VMEM_BUDGET = 40 * 1024 * 1024
N_SHARD = 4
FLAT_W = 1024
SMALL_ROWS = 88


def _tile(dim, cap):
    if dim <= cap:
        return dim
    t = (cap // LANE) * LANE
    while t >= LANE:
        if dim % t == 0:
            return t
        t -= LANE
    return dim


def _cparams(sem):
    return pltpu.CompilerParams(dimension_semantics=sem, vmem_limit_bytes=VMEM_LIMIT)


def _mm(a, b, *, name, ta=False, tb=False, add=None, add_scale=1.0, out_dtype=F32, heads=None,
        a_head=None, b_head=None, out_head=None, dims=None, tm=1408, tn=1408):
    m, n, k = dims
    tm, tn = _tile(m, tm), _tile(n, tn)
    sa, sb, so = a.dtype.itemsize, b.dtype.itemsize, jnp.dtype(out_dtype).itemsize

    def vmem_need(tk_):
        acc = tm * tn * 4 if tk_ < k else 0
        extra = 2 * tm * tn * 4 if add is not None else 0
        return 2 * (tm * tk_ * sa + tk_ * tn * sb) + 2 * tm * tn * so + acc + extra

    tk = k
    while vmem_need(tk) > MM_VMEM_BUDGET and tk > LANE:
        smaller = _tile(k, tk - LANE)
        if smaller >= tk:
            break
        tk = smaller
    nk = k // tk
    hgrid = () if heads is None else (heads,)
    off = len(hgrid)

    def spec(rows, cols, rtile, ctile, rsel, csel, layout):
        def idx(*g):
            h = g[0] if off else 0
            ri, ci = g[off + rsel], g[off + csel]
            if layout == 'lead':
                return (h, ri, ci)
            if layout == 'col':
                return (ri, h * (cols // ctile) + ci)
            return (ri, ci)
        if layout == 'lead':
            return pl.BlockSpec((None, rtile, ctile), idx)
        return pl.BlockSpec((rtile, ctile), idx)

    a_spec = spec(k, m, tk, tm, 2, 0, a_head) if ta else spec(m, k, tm, tk, 0, 2, a_head)
    b_spec = spec(n, k, tn, tk, 1, 2, b_head) if tb else spec(k, n, tk, tn, 2, 1, b_head)
    o_spec = spec(m, n, tm, tn, 0, 1, out_head)
    in_specs = [a_spec, b_spec]
    args = [a, b]
    if add is not None:
        in_specs.append(spec(m, n, tm, tn, 0, 1, out_head))
        args.append(add)
    dn = (((0 if ta else 1,), (1 if tb else 0,)), ((), ()))

    def body(*refs):
        a_ref, b_ref = refs[0], refs[1]
        prod = lax.dot_general(a_ref[...].astype(_CDT), b_ref[...].astype(_CDT), dn, preferred_element_type=F32)
        if nk == 1:
            o_ref = refs[-1]
            if add is not None:
                prod = prod + refs[2][...].astype(F32) * add_scale
            o_ref[...] = prod.astype(out_dtype)
            return
        o_ref, acc_ref = refs[-2], refs[-1]
        kk = pl.program_id(off + 2)

        @pl.when(kk == 0)
        def _():
            if add is not None:
                acc_ref[...] = refs[2][...].astype(F32) * add_scale
            else:
                acc_ref[...] = jnp.zeros_like(acc_ref)

        acc_ref[...] += prod

        @pl.when(kk == nk - 1)
        def _():
            o_ref[...] = acc_ref[...].astype(out_dtype)

    if out_head == 'lead':
        oshape = (heads, m, n)
    elif out_head == 'col':
        oshape = (m, heads * n)
    else:
        oshape = (m, n)
    sem = ("parallel",) * (off + 2) + ("arbitrary",)
    return pl.pallas_call(
        body, name=name, grid=hgrid + (m // tm, n // tn, nk), in_specs=in_specs, out_specs=o_spec,
        out_shape=jax.ShapeDtypeStruct(oshape, out_dtype),
        scratch_shapes=[pltpu.VMEM((tm, tn), F32)] if nk > 1 else [],
        compiler_params=_cparams(sem))(*args)


def _mm2(a, b, **kw):
    ta, tb = kw.get('ta', False), kw.get('tb', False)
    m = a.shape[1] if ta else a.shape[0]
    k = a.shape[0] if ta else a.shape[1]
    n = b.shape[0] if tb else b.shape[1]
    return _mm(a, b, dims=(m, n, k), **kw)


def _rowwise(fn, rows, bcast, outs, reds=(), *, name, tm=256, heads=None):
    t = rows[0][0].shape[0]
    tm = min(tm, t)
    hn = 1 if heads is None else heads
    in_specs, args = [], []
    for arr, width, base, per_head in rows:
        in_specs.append(pl.BlockSpec((tm, width), functools.partial(
            lambda i, h, base, per_head: (i, base + (h if per_head else 0)), base=base, per_head=per_head)))
        args.append(arr)
    for arr in bcast:
        in_specs.append(pl.BlockSpec(arr.shape, lambda i, h: (0, 0)))
        args.append(arr)
    out_specs, out_shape = [], []
    for total, width, per_head, dt in outs:
        out_specs.append(pl.BlockSpec((tm, width), functools.partial(
            lambda i, h, per_head: (i, h if per_head else 0), per_head=per_head)))
        out_shape.append(jax.ShapeDtypeStruct((t, total), dt))
    for shp in reds:
        out_specs.append(pl.BlockSpec(shp, lambda i, h: (0, 0)))
        out_shape.append(jax.ShapeDtypeStruct(shp, F32))
    n_in, n_out, n_red = len(args), len(outs), len(reds)

    def body(*refs):
        i, h = pl.program_id(0), pl.program_id(1)
        vals = fn(h, *[r[...] for r in refs[:n_in]])
        for r, v in zip(refs[n_in:n_in + n_out], vals[:n_out]):
            r[...] = v.astype(r.dtype)
        if n_red:
            @pl.when((i == 0) & (h == 0))
            def _():
                for r in refs[n_in + n_out:]:
                    r[...] = jnp.zeros_like(r)
            for r, v in zip(refs[n_in + n_out:], vals[n_out:]):
                r[...] += v

    sem = ("arbitrary", "arbitrary") if n_red else ("parallel", "parallel")
    res = pl.pallas_call(body, name=name, grid=(t // tm, hn), in_specs=in_specs, out_specs=out_specs,
                         out_shape=out_shape, compiler_params=_cparams(sem))(*args)
    return tuple(res)


def _sigmoid(x):
    return 1.0 / (1.0 + jnp.exp(-x))


def _silu(x):
    return x * _sigmoid(x)


def _softplus(x):
    return jnp.maximum(x, 0.0) + jnp.log(1.0 + jnp.exp(-jnp.abs(x)))


def _layer_norm(t, g, b):
    mu = jnp.mean(t, axis=-1, keepdims=True)
    d = t - mu
    var = jnp.mean(d * d, axis=-1, keepdims=True)
    return d * lax.rsqrt(var + 1e-5) * g + b


def _rms_norm(t, w):
    return t * lax.rsqrt(jnp.mean(t * t, axis=-1, keepdims=True) + 1e-6) * w


def _swap_rope_halves(t):
    lane = lax.broadcasted_iota(jnp.int32, t.shape, 1) % ROPE_PAD
    n = t.shape[1]
    up = pltpu.roll(t, n - ROPE // 2, axis=1)
    dn = pltpu.roll(t, ROPE // 2, axis=1)
    return jnp.where(lane < ROPE // 2, up, jnp.where(lane < ROPE, dn, 0.0))


def _rope(t, cosb, sinb):
    reps = t.shape[1] // ROPE_PAD
    c = jnp.tile(cosb, (1, reps)) if reps > 1 else cosb
    s = jnp.tile(sinb, (1, reps)) if reps > 1 else sinb
    return t * c + _swap_rope_halves(t) * s


def _rope_bwd(d, cosb, sinb):
    reps = d.shape[1] // ROPE_PAD
    c = jnp.tile(cosb, (1, reps)) if reps > 1 else cosb
    s = jnp.tile(sinb, (1, reps)) if reps > 1 else sinb
    return d * c + _swap_rope_halves(d * s)


_CONV_ROWS = 256
_CONV_COLS = 256


def _conv_window(ref, r0, lo, hi, t):
    parts = []
    start, stop = r0 - lo, r0 + _CONV_ROWS + hi
    if start < 0:
        parts.append(jnp.zeros((-start, ref.shape[1]), F32))
        start = 0
    tail = max(stop - t, 0)
    parts.append(ref[start:stop - tail, :].astype(F32))
    if tail:
        parts.append(jnp.zeros((tail, ref.shape[1]), F32))
    return parts[0] if len(parts) == 1 else jnp.concatenate(parts, axis=0)


def _conv_taps(win, w_ref, n_out):
    acc = win[8:8 + n_out] * w_ref[DN_CONV - 1:DN_CONV, :]
    for i in range(DN_CONV - 1):
        acc = acc + pltpu.roll(win, DN_CONV - 1 - i, axis=0)[8:8 + n_out] * w_ref[i:i + 1, :]
    return acc


def _conv_silu(x, w):
    t, ch = x.shape

    def body(x_ref, w_ref, o_ref):
        for r in range(t // _CONV_ROWS):
            r0 = r * _CONV_ROWS
            c = _conv_taps(_conv_window(x_ref, r0, 8, 0, t), w_ref, _CONV_ROWS)
            o_ref[r0:r0 + _CONV_ROWS, :] = _silu(c)

    return pl.pallas_call(
        body, name="conv_silu", grid=(ch // _CONV_COLS,),
        in_specs=[pl.BlockSpec((t, _CONV_COLS), lambda j: (0, j)), pl.BlockSpec((DN_CONV, _CONV_COLS), lambda j: (0, j))],
        out_specs=pl.BlockSpec((t, _CONV_COLS), lambda j: (0, j)),
        out_shape=jax.ShapeDtypeStruct((t, ch), F32), compiler_params=_cparams(("parallel",)))(x, w)


def _conv_silu_bwd(x, w, dys):
    t, ch = x.shape
    per = ch // len(dys) // _CONV_COLS

    def body(x_ref, w_ref, *rest):
        dy_refs, (dx_ref, dw_ref) = rest[:len(dys)], rest[len(dys):]
        sec = pl.program_id(0) // per
        dws = [jnp.zeros((1, _CONV_COLS), F32) for _ in range(DN_CONV)]
        for r in range(t // _CONV_ROWS):
            r0 = r * _CONV_ROWS
            n_ext = _CONV_ROWS + 8
            xw = _conv_window(x_ref, r0, 8, 8, t)
            c = _conv_taps(xw, w_ref, n_ext)
            sg = _sigmoid(c)
            dy = _conv_window(dy_refs[-1], r0, 0, 8, t)
            for k in range(len(dys) - 2, -1, -1):
                dy = jnp.where(sec == k, _conv_window(dy_refs[k], r0, 0, 8, t), dy)
            ds = dy * (sg * (1.0 + c * (1.0 - sg)))
            x0 = xw[8:8 + _CONV_ROWS]
            dx = jnp.zeros((_CONV_ROWS, _CONV_COLS), F32)
            for i in range(DN_CONV):
                sh = DN_CONV - 1 - i
                ds_up = (ds if sh == 0 else pltpu.roll(ds, n_ext - sh, axis=0))[:_CONV_ROWS]
                dx = dx + ds_up * w_ref[i:i + 1, :]
                dws[i] = dws[i] + jnp.sum(x0 * ds_up, axis=0, keepdims=True)
            dx_ref[r0:r0 + _CONV_ROWS, :] = dx.astype(dx_ref.dtype)
        for i in range(DN_CONV):
            dw_ref[i:i + 1, :] = dws[i]

    blk = pl.BlockSpec((t, _CONV_COLS), lambda j: (0, j))
    wblk = pl.BlockSpec((DN_CONV, _CONV_COLS), lambda j: (0, j))
    dy_specs = [pl.BlockSpec((t, _CONV_COLS), functools.partial(lambda j, k: (0, jnp.clip(j - k * per, 0, per - 1)), k=k))
                for k in range(len(dys))]
    return pl.pallas_call(
        body, name="conv_silu_bwd", grid=(ch // _CONV_COLS,), in_specs=[blk, wblk] + dy_specs, out_specs=[blk, wblk],
        out_shape=[jax.ShapeDtypeStruct((t, ch), _CDT), jax.ShapeDtypeStruct((DN_CONV, ch), F32)],
        compiler_params=_cparams(("arbitrary",)))(x, w, *dys)


_PA_ROWS = 512


def _bmm(a, b, spec, exact=False):
    if exact:
        return jnp.einsum(spec, a, b, precision=_HI, preferred_element_type=F32)
    return jnp.einsum(spec, a.astype(_CDT), b.astype(_CDT), preferred_element_type=F32)


def _split16(a):
    hi = a.astype(jnp.bfloat16)
    return hi, (a - hi.astype(F32)).astype(jnp.bfloat16)


def _bmm3(a, b, spec):
    ah, al = _split16(a)
    bh, bl = _split16(b)
    e = lambda p, q: jnp.einsum(spec, p, q, preferred_element_type=F32)
    return e(ah, bh) + (e(ah, bl) + e(al, bh))


def _split3(b):
    b0 = b.astype(jnp.bfloat16)
    r1 = b - b0.astype(F32)
    b1 = r1.astype(jnp.bfloat16)
    return b0, b1, (r1 - b1.astype(F32)).astype(jnp.bfloat16)


@functools.partial(jax.custom_vjp, nondiff_argnums=(2, 3))
def _select_mm(sel, b, spec, spec_t):
    return sum(jnp.einsum(spec, sel, t, preferred_element_type=F32) for t in _split3(b))


def _select_mm_fwd(sel, b, spec, spec_t):
    return _select_mm(sel, b, spec, spec_t), sel


def _select_mm_bwd(spec, spec_t, sel, ct):
    return jnp.zeros_like(sel), sum(jnp.einsum(spec_t, sel, t, preferred_element_type=F32) for t in _split3(ct))


_select_mm.defvjp(_select_mm_fwd, _select_mm_bwd)


def _tri_inverse(l_mat, eye):
    pw = -l_mat
    t_inv = eye + pw
    for _ in range(5):
        pw = _bmm3(pw, pw, 'bij,bjk->bik')
        t_inv = t_inv + _bmm3(t_inv, pw, 'bij,bjk->bik')
    return t_inv


@jax.custom_vjp
def _tri_inverse_saved(l_mat, t_saved):
    return t_saved


def _tri_inverse_saved_fwd(l_mat, t_saved):
    return t_saved, t_saved


def _tri_inverse_saved_bwd(t_saved, dt):
    left = _bmm3(t_saved, dt, 'bji,bjk->bik')
    return -_bmm3(left, t_saved, 'bij,bkj->bik'), jnp.zeros_like(t_saved)


_tri_inverse_saved.defvjp(_tri_inverse_saved_fwd, _tri_inverse_saved_bwd)


def _phase_a(h, q, k, v, ba, alog, dtb, t_saved=None):
    r = q.shape[0]
    nb = r // DN_CHUNK
    c = DN_CHUNK
    lane = lax.broadcasted_iota(jnp.int32, (1, LANE), 1)
    selb = (lane == h).astype(F32)
    sela = (lane == h + HEADS).astype(F32)
    b_raw = jnp.sum(ba * selb, axis=1, keepdims=True)
    a_raw = jnp.sum(ba * sela, axis=1, keepdims=True)
    al = jnp.sum(alog * selb, axis=1, keepdims=True)
    dt = jnp.sum(dtb * selb, axis=1, keepdims=True)
    beta = jnp.broadcast_to(_sigmoid(b_raw), (r, LANE))
    g = jnp.broadcast_to(-jnp.exp(al) * _softplus(a_raw + dt), (r, LANE))
    qn = q * lax.rsqrt(jnp.sum(q * q, -1, keepdims=True) + 1e-6) * (DN_DK ** -0.5)
    kn = k * lax.rsqrt(jnp.sum(k * k, -1, keepdims=True) + 1e-6)
    q3, k3, v3 = qn.reshape(nb, c, LANE), kn.reshape(nb, c, LANE), v.reshape(nb, c, LANE)
    b3, g3 = beta.reshape(nb, c, LANE), g.reshape(nb, c, LANE)
    ri = lax.broadcasted_iota(jnp.int32, (nb, c, c), 1)
    ci = lax.broadcasted_iota(jnp.int32, (nb, c, c), 2)
    tril, strict = ri >= ci, ri > ci
    gc = _select_mm(tril.astype(jnp.bfloat16), g3, 'bij,bjd->bid', 'bij,bid->bjd')
    onehot = (lax.broadcasted_iota(jnp.int32, (nb, c, LANE), 2) == 0).astype(jnp.bfloat16)
    g_row = _select_mm(onehot, gc, 'bid,bjd->bij', 'bid,bij->bjd')
    diff = gc[:, :, :c] - g_row
    decay = jnp.where(tril, jnp.exp(jnp.where(tril, diff, 0.0)), 0.0)
    kb = k3 * b3
    l_mat = jnp.where(strict, _bmm(kb, k3, 'bid,bjd->bij') * decay, 0.0)
    if t_saved is None:
        t_inv = _tri_inverse(l_mat, (ri == ci).astype(F32))
    else:
        t_inv = _tri_inverse_saved(l_mat, t_saved.reshape(nb, c, c))
    eg = jnp.exp(gc)
    u = _bmm(t_inv, v3 * b3, 'bij,bje->bie')
    w = _bmm(t_inv, kb * eg, 'bij,bje->bie')
    intra = jnp.where(tril, _bmm(q3, k3, 'bid,bjd->bij') * decay, 0.0)
    qd = q3 * eg
    gl = jnp.sum(g3, axis=1, keepdims=True)
    kt = k3 * jnp.exp(gl - gc)
    outs = (u.reshape(r, LANE), w.reshape(r, LANE), qd.reshape(r, LANE), kt.reshape(r, LANE),
            intra.reshape(r, c), gl.reshape(nb, LANE))
    if t_saved is not None:
        return outs
    qd2 = qd - _bmm(intra, w, 'bij,bjd->bid')
    au = _bmm(intra, u, 'bij,bje->bie')
    return outs + (t_inv.reshape(r, c), qd2.reshape(r, LANE), au.reshape(r, LANE))


def _pa_specs(t):
    rr = min(_PA_ROWS, t)
    nb = rr // DN_CHUNK
    qkv = [pl.BlockSpec((rr, LANE), functools.partial(lambda i, h, o: (i, o + h), o=o)) for o in (0, HEADS, 2 * HEADS)]
    ba = pl.BlockSpec((rr, LANE), lambda i, h: (i, 3))
    vec = pl.BlockSpec((1, LANE), lambda i, h: (0, 0))
    row = pl.BlockSpec((rr, LANE), lambda i, h: (i, h))
    intra = pl.BlockSpec((None, rr, DN_CHUNK), lambda i, h: (h, i, 0))
    gl = pl.BlockSpec((nb, LANE), lambda i, h: (i, h))
    return rr, qkv, ba, vec, row, intra, gl


def _grid_ends(grid):
    first = lambda: functools.reduce(jnp.logical_and, [pl.program_id(a) == 0 for a in range(len(grid))])
    last = lambda: functools.reduce(jnp.logical_and, [pl.program_id(a) == n - 1 for a, n in enumerate(grid)])
    return first, last


def _delta_local(qkv_act, pm, alog, dtb, rider=None):
    t = qkv_act.shape[0]
    rr, qkv, ba, vec, row, intra, gl = _pa_specs(t)

    def body(q, k, v, b, al, dt, *outs):
        vals = _phase_a(pl.program_id(1), q[...], k[...], v[...], b[...], al[...], dt[...])
        for o, val in zip(outs, vals):
            o[...] = val

    wide = jax.ShapeDtypeStruct((t, HEADS * LANE), F32)
    sq = jax.ShapeDtypeStruct((HEADS, t, DN_CHUNK), F32)
    grid = (t // rr, HEADS)
    return _carried_call(
        body, rider, *_grid_ends(grid), name="delta_local", grid=grid, in_specs=qkv + [ba, vec, vec],
        out_specs=[row] * 4 + [intra, gl, intra, row, row],
        out_shape=[wide] * 4 + [sq, jax.ShapeDtypeStruct((t // DN_CHUNK, HEADS * LANE), F32), sq, wide, wide],
        scratch_shapes=[], sem=("arbitrary", "arbitrary"), args=(qkv_act, qkv_act, qkv_act, pm, alog, dtb))


def _delta_local_bwd(qkv_act, pm, alog, dtb, t_inv, du, dw, dqd, dkt, dintra, dgl, rider=None):
    t = qkv_act.shape[0]
    rr, qkv, ba, vec, row, intra, gl = _pa_specs(t)

    def body(q, k, v, b, al, dt, ti, du_r, dw_r, dqd_r, dkt_r, di_r, dgl_r, dq_o, dk_o, dv_o, dba_o, dal_o, ddt_o):
        i, h = pl.program_id(0), pl.program_id(1)
        t_saved = ti[...]
        _, vjp = jax.vjp(lambda *a: _phase_a(h, *a, t_saved=t_saved), q[...], k[...], v[...], b[...], al[...], dt[...])
        dq, dk, dv, dba, dal, ddt = vjp((du_r[...], dw_r[...], dqd_r[...], dkt_r[...], di_r[...], dgl_r[...]))
        dq_o[...], dk_o[...], dv_o[...] = dq, dk, dv

        @pl.when(h == 0)
        def _():
            dba_o[...] = jnp.zeros_like(dba_o)

        @pl.when((h == 0) & (i == 0))
        def _():
            dal_o[...] = jnp.zeros_like(dal_o)
            ddt_o[...] = jnp.zeros_like(ddt_o)

        dba_o[...] += dba
        dal_o[...] += dal
        ddt_o[...] += ddt

    wide = jax.ShapeDtypeStruct((t, HEADS * LANE), F32)
    vshape = jax.ShapeDtypeStruct((1, LANE), F32)
    grid = (t // rr, HEADS)
    return _carried_call(
        body, rider, *_grid_ends(grid), name="delta_local_bwd", grid=grid,
        in_specs=qkv + [ba, vec, vec, intra] + [row] * 4 + [intra, gl],
        out_specs=[row] * 3 + [pl.BlockSpec((rr, LANE), lambda i, h: (i, 0)), vec, vec],
        out_shape=[wide] * 3 + [jax.ShapeDtypeStruct((t, LANE), F32), vshape, vshape],
        scratch_shapes=[], sem=("arbitrary", "arbitrary"),
        args=(qkv_act, qkv_act, qkv_act, pm, alog, dtb, t_inv, du, dw, dqd, dkt, dintra, dgl))


_SCAN_ROWS = 512


def _dot(a, b, dn):
    return lax.dot_general(a.astype(_CDT), b.astype(_CDT), (dn, ((), ())), preferred_element_type=F32)


_NN = ((1,), (0,))
_NT = ((1,), (1,))
_TN = ((0,), (0,))


def _delta_scan(u, w, qd, kt, au, gl, rider=None):
    t = u.shape[0]
    rr = min(_SCAN_ROWS, t)
    nc = rr // DN_CHUNK

    def body(u_ref, w_ref, qd_ref, kt_ref, au_ref, gl_ref, o_ref, sall_ref, s_scr):
        @pl.when(pl.program_id(0) == 0)
        def _():
            s_scr[...] = jnp.zeros_like(s_scr)

        def chunk(c, carry):
            r0 = pl.multiple_of(c * DN_CHUNK, DN_CHUNK)
            rows = pl.ds(r0, DN_CHUNK)
            e = jnp.exp(gl_ref[pl.ds(c, 1), :])
            states = [s_scr[h] for h in range(HEADS)]
            u_c, w_c, qd_c, kt_c, au_c = u_ref[rows, :], w_ref[rows, :], qd_ref[rows, :], kt_ref[rows, :], au_ref[rows, :]
            o_new, s_new = [], []
            for h in range(HEADS):
                cs = slice(h * LANE, (h + 1) * LANE)
                s = states[h]
                both = _dot(jnp.concatenate([w_c[:, cs], qd_c[:, cs]], axis=0), s, _NN)
                v_new = u_c[:, cs] - both[:DN_CHUNK]
                o_new.append(both[DN_CHUNK:] + au_c[:, cs])
                s_new.append(s * e[:, cs] + _dot(kt_c[:, cs], v_new, _TN))
            o_ref[rows, :] = jnp.concatenate(o_new, axis=1)
            for h in range(HEADS):
                sall_ref[c, h] = states[h]
                s_scr[h] = s_new[h]
            return carry

        lax.fori_loop(0, nc, chunk, 0)

    row = pl.BlockSpec((rr, HEADS * LANE), lambda i: (i, 0))
    grid = (t // rr,)
    return _carried_call(
        body, rider, *_grid_ends(grid), name="delta_scan", grid=grid,
        in_specs=[row] * 5 + [pl.BlockSpec((nc, HEADS * LANE), lambda i: (i, 0))],
        out_specs=[row, pl.BlockSpec((nc, HEADS, LANE, LANE), lambda i: (i, 0, 0, 0))],
        out_shape=[jax.ShapeDtypeStruct((t, HEADS * LANE), F32),
                   jax.ShapeDtypeStruct((t // DN_CHUNK, HEADS, LANE, LANE), F32)],
        scratch_shapes=[pltpu.VMEM((HEADS, LANE, LANE), F32)], sem=("arbitrary",), args=(u, w, qd, kt, au, gl))


def _delta_scan_bwd(u, w, qd, kt, intra, gl, sall, do, rider=None):
    t = u.shape[0]
    rr = min(_SCAN_ROWS, t)
    nc = rr // DN_CHUNK
    ng = t // rr

    def body(u_ref, w_ref, qd_ref, kt_ref, a_ref, gl_ref, sall_ref, do_ref,
             du_ref, dw_ref, dqd_ref, dkt_ref, da_ref, dgl_ref, ds_scr):
        @pl.when(pl.program_id(0) == 0)
        def _():
            ds_scr[...] = jnp.zeros_like(ds_scr)

        def chunk(cc, carry):
            c = nc - 1 - cc
            r0 = pl.multiple_of(c * DN_CHUNK, DN_CHUNK)
            rows = pl.ds(r0, DN_CHUNK)
            e = jnp.exp(gl_ref[pl.ds(c, 1), :])
            states = [sall_ref[c, h] for h in range(HEADS)]
            ds_outs = [ds_scr[h] for h in range(HEADS)]
            u_a, w_a, kt_a, qd_a, do_a = u_ref[rows, :], w_ref[rows, :], kt_ref[rows, :], qd_ref[rows, :], do_ref[rows, :]
            a_a = [a_ref[h, rows, :] for h in range(HEADS)]
            da, dqd, dkt, du, dw, dgl, ds_new = [], [], [], [], [], [], []
            for h in range(HEADS):
                cs = slice(h * LANE, (h + 1) * LANE)
                s, ds_out = states[h], ds_outs[h]
                w_c, kt_c, qd_c, do_c = w_a[:, cs], kt_a[:, cs], qd_a[:, cs], do_a[:, cs]
                v_new = u_a[:, cs] - _dot(w_c, s, _NN)
                dv_new = _dot(a_a[h], do_c, _TN) + _dot(kt_c, ds_out, _NN)
                cots = jnp.concatenate([do_c, dv_new], axis=0)
                both = _dot(cots, s, _NT)
                dqd.append(both[:DN_CHUNK])
                dw.append(-both[DN_CHUNK:])
                da.append(_dot(do_c, v_new, _NT))
                dkt.append(_dot(v_new, ds_out, _NT))
                du.append(dv_new)
                eh = e[:, cs]
                dgl.append(jnp.broadcast_to(jnp.sum(ds_out * s, axis=0, keepdims=True) * eh, (8, LANE)))
                ds_new.append(ds_out * eh + _dot(jnp.concatenate([qd_c, -w_c], axis=0), cots, _TN))
            cat = lambda parts: jnp.concatenate(parts, axis=1)
            dqd_ref[rows, :], dkt_ref[rows, :], du_ref[rows, :], dw_ref[rows, :] = cat(dqd), cat(dkt), cat(du), cat(dw)
            dgl_ref[pl.ds(pl.multiple_of(c * 8, 8), 8), :] = cat(dgl)
            for h in range(HEADS):
                da_ref[h, rows, :] = da[h]
                ds_scr[h] = ds_new[h]
            return carry

        lax.fori_loop(0, nc, chunk, 0)

    rev = lambda i: (ng - 1 - i, 0)
    row = pl.BlockSpec((rr, HEADS * LANE), rev)
    a_spec = pl.BlockSpec((HEADS, rr, DN_CHUNK), lambda i: (0, ng - 1 - i, 0))
    gl_spec = pl.BlockSpec((nc, HEADS * LANE), rev)
    wide = jax.ShapeDtypeStruct((t, HEADS * LANE), F32)
    outs, carried = _carried_call(
        body, rider, *_grid_ends((ng,)), name="delta_scan_bwd", grid=(ng,),
        in_specs=[row] * 4 + [a_spec, gl_spec, pl.BlockSpec((nc, HEADS, LANE, LANE), lambda i: (ng - 1 - i, 0, 0, 0)), row],
        out_specs=[row] * 4 + [a_spec, pl.BlockSpec((nc * 8, HEADS * LANE), rev)],
        out_shape=[wide] * 4 + [jax.ShapeDtypeStruct((HEADS, t, DN_CHUNK), F32),
                                jax.ShapeDtypeStruct((t // DN_CHUNK * 8, HEADS * LANE), F32)],
        scratch_shapes=[pltpu.VMEM((HEADS, LANE, LANE), F32)], sem=("arbitrary",), args=(u, w, qd, kt, intra, gl, sall, do))
    return tuple(outs[:5]) + (outs[5].reshape(t // DN_CHUNK, 8, HEADS * LANE)[:, 0, :],), carried


_ATT_TILE = 512


def _kv_rows(j, tk):
    return pl.ds(pl.multiple_of(j * tk, tk), tk)


def _att_scores(ql, qr, ckv_ref, kr_ref, j, tk):
    ks = _kv_rows(j, tk)
    return (_dot(ql, ckv_ref[ks, :], _NT) + _dot(qr, kr_ref[ks, :], _NT)) * ATT_SCALE


def _diag_mask(s):
    qi = lax.broadcasted_iota(jnp.int32, s.shape, 0)
    ki = lax.broadcasted_iota(jnp.int32, s.shape, 1)
    return jnp.where(ki <= qi, s, NEG_BIG)


def _attention(ql, qr_pre, cosb, sinb, ckv, kr):
    t = ckv.shape[0]
    tq = min(_ATT_TILE, t)

    nl = tq // LANE

    def lane_fold(v, op):
        out = v[:, :LANE]
        for k in range(1, nl):
            out = op(out, v[:, k * LANE:(k + 1) * LANE])
        return out

    def body(ql_ref, qr_ref, cos_ref, sin_ref, ckv_ref, kr_ref, o_ref, lse_ref, qrope_ref, s_all, m_lanes, l_lanes, acc_scr):
        qi = pl.program_id(1)
        q_lat = ql_ref[...]
        q_rope = _rope(qr_ref[...], cos_ref[...], sin_ref[...]).astype(qrope_ref.dtype)
        qrope_ref[...] = q_rope
        m_lanes[...] = jnp.full_like(m_lanes, NEG_BIG)

        def scores(j, masked):
            s = _att_scores(q_lat, q_rope, ckv_ref, kr_ref, j, tq)
            if masked:
                s = _diag_mask(s)
            s_all[j] = s
            m_lanes[...] = jnp.maximum(m_lanes[...], lane_fold(s, jnp.maximum))

        def scores_body(j, carry):
            scores(j, False)
            return carry

        lax.fori_loop(0, qi, scores_body, 0)
        scores(qi, True)
        m = jnp.max(m_lanes[...], axis=-1, keepdims=True)
        mb = jnp.broadcast_to(m, (tq, LANE))
        l_lanes[...] = jnp.zeros_like(l_lanes)
        acc_scr[...] = jnp.zeros_like(acc_scr)

        def weigh(j, carry):
            s = s_all[j]
            p = jnp.concatenate([jnp.exp(s[:, k * LANE:(k + 1) * LANE] - mb) for k in range(nl)], axis=1)
            l_lanes[...] += lane_fold(p, jnp.add)
            acc_scr[...] += _dot(p, ckv_ref[_kv_rows(j, tq), :], _NN)
            return carry

        lax.fori_loop(0, qi + 1, weigh, 0)
        l = jnp.sum(l_lanes[...], axis=-1, keepdims=True)
        o_ref[...] = acc_scr[...] / l
        lse_ref[...] = m + jnp.log(l)

    return pl.pallas_call(
        body, name="attention", grid=(HEADS, t // tq),
        in_specs=[pl.BlockSpec((None, tq, KV_LORA), lambda h, i: (h, i, 0)),
                  pl.BlockSpec((tq, ROPE_PAD), lambda h, i: (i, h)),
                  pl.BlockSpec((tq, ROPE_PAD), lambda h, i: (i, 0)), pl.BlockSpec((tq, ROPE_PAD), lambda h, i: (i, 0)),
                  pl.BlockSpec((t, KV_LORA), lambda h, i: (0, 0)),
                  pl.BlockSpec((t, ROPE_PAD), lambda h, i: (0, 0))],
        out_specs=[pl.BlockSpec((None, tq, KV_LORA), lambda h, i: (h, i, 0)),
                   pl.BlockSpec((None, tq, 1), lambda h, i: (h, i, 0)),
                   pl.BlockSpec((tq, ROPE_PAD), lambda h, i: (i, h))],
        out_shape=[jax.ShapeDtypeStruct((HEADS, t, KV_LORA), F32), jax.ShapeDtypeStruct((HEADS, t, 1), F32),
                   jax.ShapeDtypeStruct((t, HEADS * ROPE_PAD), _CDT)],
        scratch_shapes=[pltpu.VMEM((t // tq, tq, tq), F32), pltpu.VMEM((tq, LANE), F32), pltpu.VMEM((tq, LANE), F32),
                        pltpu.VMEM((tq, KV_LORA), F32)],
        compiler_params=_cparams(("parallel", "parallel")))(ql, qr_pre, cosb, sinb, ckv, kr)


def _attention_bwd(ql, qr, cosb, sinb, ckv, kr, out, lse, dout):
    t = ckv.shape[0]
    tq = min(_ATT_TILE, t)

    def body(ql_ref, qr_ref, cos_ref, sin_ref, ckv_ref, kr_ref, o_ref, lse_ref, do_ref, dql_ref, dqr_ref, dckv_ref,
             dkr_ref, dql_scr, dqr_scr):
        h, qi = pl.program_id(0), pl.program_id(1)

        @pl.when((h == 0) & (qi == 0))
        def _():
            dckv_ref[...] = jnp.zeros_like(dckv_ref)
            dkr_ref[...] = jnp.zeros_like(dkr_ref)

        q_lat, q_rope = ql_ref[...], qr_ref[...]
        d_o = do_ref[...].astype(_CDT)
        lse_v = lse_ref[...]
        dsum = jnp.sum(do_ref[...] * o_ref[...], axis=-1, keepdims=True)
        dql_scr[...] = jnp.zeros_like(dql_scr)
        dqr_scr[...] = jnp.zeros_like(dqr_scr)

        def step(j, masked):
            ks = _kv_rows(j, tq)
            s = _att_scores(q_lat, q_rope, ckv_ref, kr_ref, j, tq)
            if masked:
                s = _diag_mask(s)
            p = jnp.exp(s - lse_v)
            kv = ckv_ref[ks, :]
            ds = (p * (_dot(d_o, kv, _NT) - dsum) * ATT_SCALE).astype(_CDT)
            pb = p.astype(_CDT)
            dql_scr[...] += _dot(ds, kv, _NN)
            dqr_scr[...] += _dot(ds, kr_ref[ks, :], _NN)
            dckv_ref[ks, :] += _dot(pb, d_o, _TN) + _dot(ds, q_lat, _TN)
            dkr_ref[ks, :] += _dot(ds, q_rope, _TN)

        def loop_body(j, carry):
            step(j, False)
            return carry

        lax.fori_loop(0, qi, loop_body, 0)
        step(qi, True)
        dql_ref[...] = dql_scr[...].astype(dql_ref.dtype)
        dqr_ref[...] = _rope_bwd(dqr_scr[...], cos_ref[...], sin_ref[...]).astype(dqr_ref.dtype)

    lat = pl.BlockSpec((None, tq, KV_LORA), lambda h, i: (h, i, 0))
    rope = pl.BlockSpec((tq, ROPE_PAD), lambda h, i: (i, h))
    table = pl.BlockSpec((tq, ROPE_PAD), lambda h, i: (i, 0))
    kfull = pl.BlockSpec((t, KV_LORA), lambda h, i: (0, 0))
    rfull = pl.BlockSpec((t, ROPE_PAD), lambda h, i: (0, 0))
    return pl.pallas_call(
        body, name="attention_bwd", grid=(HEADS, t // tq),
        in_specs=[lat, rope, table, table, kfull, rfull, lat, pl.BlockSpec((None, tq, 1), lambda h, i: (h, i, 0)), lat],
        out_specs=[lat, rope, kfull, rfull],
        out_shape=[jax.ShapeDtypeStruct((HEADS, t, KV_LORA), _CDT), jax.ShapeDtypeStruct((t, HEADS * ROPE_PAD), _CDT),
                   jax.ShapeDtypeStruct((t, KV_LORA), F32), jax.ShapeDtypeStruct((t, ROPE_PAD), F32)],
        scratch_shapes=[pltpu.VMEM((tq, KV_LORA), F32), pltpu.VMEM((tq, ROPE_PAD), F32)],
        compiler_params=_cparams(("arbitrary", "arbitrary")))(ql, qr, cosb, sinb, ckv, kr, out, lse, dout)


_DX_ROWS = 256


def _dx_fused(pairs, add, add_scale, rider=None):
    t, d = add.shape
    tm = min(_DX_ROWS, t)
    n = len(pairs)

    def body(*refs):
        acc = refs[2 * n][...] * add_scale
        for i in range(n):
            acc = acc + _dot(refs[i][...], refs[n + i][...], _NT)
        refs[2 * n + 1][...] = acc

    in_specs = [pl.BlockSpec((tm, a.shape[1]), lambda i: (i, 0)) for a, _ in pairs]
    in_specs += [pl.BlockSpec(w.shape, lambda i: (0, 0)) for _, w in pairs]
    row = pl.BlockSpec((tm, d), lambda i: (i, 0))
    grid = (t // tm,)
    (dx,), carried = _carried_call(
        body, rider, *_grid_ends(grid), name="b_dx", grid=grid, in_specs=in_specs + [row], out_specs=[row],
        out_shape=[jax.ShapeDtypeStruct((t, d), F32)], scratch_shapes=[], sem=("arbitrary",),
        args=tuple(a for a, _ in pairs) + tuple(w for _, w in pairs) + (add,))
    return dx, carried


def _ffn_in_swiglu(a, w):
    t, k = a.shape
    hid = w.shape[1] // 2
    tm, tn = _tile(t, 1024), _tile(hid, 1408)
    nj = hid // tn

    def body(a_ref, bg_ref, bu_ref, act_ref, gt_ref, up_ref):
        av = a_ref[...].astype(_CDT)
        gt = jnp.dot(av, bg_ref[...].astype(_CDT), preferred_element_type=F32)
        up = jnp.dot(av, bu_ref[...].astype(_CDT), preferred_element_type=F32)
        act_ref[...] = _swiglu(gt, up).astype(act_ref.dtype)
        gt_ref[...] = gt.astype(gt_ref.dtype)
        up_ref[...] = up.astype(up_ref.dtype)

    out = pl.BlockSpec((tm, tn), lambda i, j: (i, j))
    return pl.pallas_call(
        body, name="f_ffn_in_swiglu", grid=(t // tm, nj),
        in_specs=[pl.BlockSpec((tm, k), lambda i, j: (i, 0)), pl.BlockSpec((k, tn), lambda i, j: (0, j)),
                  pl.BlockSpec((k, tn), lambda i, j: (0, nj + j))],
        out_specs=[out] * 3, out_shape=[jax.ShapeDtypeStruct((t, hid), _CDT)] * 3,
        compiler_params=_cparams(("parallel", "parallel")))(a, w, w)


def _gated_norm(o, z, w):
    return _rms_norm(o, w) * _silu(z)


def _gated_norm_heads(o, z, w):
    heads = [_gated_norm(o[:, h * LANE:(h + 1) * LANE], z[:, h * LANE:(h + 1) * LANE], w) for h in range(HEADS)]
    return jnp.concatenate(heads, axis=1)


def _mla_pre(ckv, krp, cq, cosb, sinb, qw, kw):
    return _rms_norm(cq, qw), _rms_norm(ckv, kw), _rope(krp, cosb, sinb)


def _merge(gg, y_dn, y_mla):
    return _sigmoid(gg[:, :D_MODEL]) * y_dn + _sigmoid(gg[:, D_MODEL:]) * y_mla


def _ln1(xv, attn_out, g, b):
    return _layer_norm(ALPHA * xv + attn_out, g, b)


def _final(h1, ffn, gate_pre, ple_proj, g, b):
    return _layer_norm(ALPHA * h1 + ffn + _sigmoid(gate_pre) * ple_proj, g, b)


def _swiglu(gt, up):
    return _silu(gt) * up


def _local_step(x, p, cosb, sinb, target, wt, sp, exch):
    t = x.shape[0]
    bf = _CDT
    xb = x.astype(bf)
    g = {}

    qkv_pre = _mm2(xb, wt['qkv'], name="f_qkv")
    z = _mm2(xb, wt['z'], name="f_z")
    gg = _mm2(xb, wt['gg'], name="f_gg")
    pm = _mm2(xb, wt['mla'], name="f_mla")
    qkv_act = _conv_silu(qkv_pre, sp['conv_w'])
    (u, w_, qd, kt, intra, gl, t_inv, qd2, au), sent = _delta_local(qkv_act, pm, sp['a_log'], sp['dt_bias'],
                                                                    rider=exch.gather_send())
    (o_dn, sall), passed = _delta_scan(u, w_, qd2, kt, au, gl, rider=exch.gather_pass(sent))
    wt = dict(wt, **exch.weights(passed))
    def gated_norm_br(h, o, zz, w, wbr):
        og_v = _gated_norm_heads(o, zz, w).astype(bf)
        return og_v, jnp.dot(og_v, wbr.astype(bf), preferred_element_type=F32)

    og, y_dn = _rowwise(gated_norm_br, [(o_dn, D_MODEL, 0, False), (z, D_MODEL, 0, False)],
                        [sp['dn_norm_w'], wt['br_dn']],
                        [(D_MODEL, D_MODEL, False, bf), (D_MODEL, D_MODEL, False, F32)], name="f_gated_norm_br")

    c_q, c_kv, k_rope = _rowwise(
        lambda h, *a: _mla_pre(*a),
        [(pm, KV_LORA, 0, False), (pm, ROPE_PAD, 2, False), (pm, Q_LORA, 2, False),
         (cosb, ROPE_PAD, 0, False), (sinb, ROPE_PAD, 0, False)],
        [sp['q_norm_w'], sp['kv_norm_w']],
        [(Q_LORA, Q_LORA, False, bf), (KV_LORA, KV_LORA, False, bf), (ROPE_PAD, ROPE_PAD, False, bf)], name="f_mla_pre")
    q_nope = _mm2(c_q, wt['uq_nope'], name="f_uq_nope", out_dtype=bf)
    q_rope_pre = _mm2(c_q, wt['uq_rope'], name="f_uq_rope")
    q_lat = _mm(q_nope, wt['uk'], name="f_q_lat", tb=True, heads=HEADS, a_head='col', b_head='lead', out_head='lead',
                dims=(t, KV_LORA, NOPE), out_dtype=bf)
    out_lat, lse, q_rope = _attention(q_lat, q_rope_pre, cosb, sinb, c_kv, k_rope)
    o_mla = _mm(out_lat, wt['uv'], name="f_o_mla", heads=HEADS, a_head='lead', b_head='lead', out_head='col',
                dims=(t, NOPE, KV_LORA), out_dtype=bf)
    y_mla = _mm2(o_mla, wt['br_mla'], name="f_br_mla")

    def merge_o_ln1(h, ggv, yd, ym, xv, wo, gv, bv):
        mixed_v = _merge(ggv, yd, ym).astype(bf)
        ao = jnp.dot(mixed_v, wo.astype(bf), preferred_element_type=F32)
        h1v = _ln1(xv, ao, gv, bv)
        return mixed_v, ao, h1v, h1v

    mixed, attn_out, h1, h1b = _rowwise(
        merge_o_ln1, [(gg, 2 * D_MODEL, 0, False), (y_dn, D_MODEL, 0, False), (y_mla, D_MODEL, 0, False), (x, D_MODEL, 0, False)],
        [wt['o'], sp['ln1_g'], sp['ln1_b']],
        [(D_MODEL, D_MODEL, False, bf), (D_MODEL, D_MODEL, False, F32), (D_MODEL, D_MODEL, False, F32),
         (D_MODEL, D_MODEL, False, bf)], name="f_merge_o_ln1")
    act, ffn_gt, ffn_up = _ffn_in_swiglu(h1b, wt['ffn_in'])
    ffn = _mm2(act, wt['ffn_out'], name="f_ffn_out")
    gate_pre = _mm2(h1b, wt['ple_gate'], name="f_ple_gate")
    pb = p.astype(bf)
    ple_proj = _mm2(pb, wt['ple'], name="f_ple")

    def final_fn(h, h1v, ffnv, gpv, ppv, tgt, gv, bv):
        y, vjp = jax.vjp(_final, h1v, ffnv, gpv, ppv, gv, bv)
        err = y - tgt
        dh1, dffn, dgp, dpp, dg, db = vjp(err * (1.0 / D_MODEL))
        sq = err * err
        lanes = sq[:, :LANE]
        for j in range(1, D_MODEL // LANE):
            lanes = lanes + sq[:, j * LANE:(j + 1) * LANE]
        loss = jnp.sum(lanes, axis=0, keepdims=True) * (0.5 / D_MODEL)
        return dffn, dffn, dgp, dpp, dg, db, loss

    dpre2, dpre2b, dgate_pre, dple_proj, g['ln2_g'], g['ln2_b'], loss_lanes = _rowwise(
        final_fn, [(a, D_MODEL, 0, False) for a in (h1, ffn, gate_pre, ple_proj, target)],
        [sp['ln2_g'], sp['ln2_b']],
        [(D_MODEL, D_MODEL, False, F32)] + [(D_MODEL, D_MODEL, False, bf)] * 3,
        [(1, D_MODEL), (1, D_MODEL), (1, LANE)], name="b_final")

    g['ple'] = _mm2(pb, dple_proj, ta=True, name="g_ple")
    g['ple_gate'] = _mm2(h1b, dgate_pre, ta=True, name="g_ple_gate")
    g['ffn_out'] = _mm2(act, dpre2b, ta=True, name="g_ffn_out")
    dact = _mm2(dpre2b, wt['ffn_out'], tb=True, name="b_dact", out_dtype=bf)

    def swiglu_bwd(h, gt, up, d):
        _, vjp = jax.vjp(_swiglu, gt.astype(F32), up.astype(F32))
        dgt, dup = vjp(d.astype(F32))
        return (jnp.concatenate([dgt, dup], axis=1),)

    (dffn_in,) = _rowwise(swiglu_bwd, [(ffn_gt, FFN_HIDDEN, 0, False), (ffn_up, FFN_HIDDEN, 0, False),
                                       (dact, FFN_HIDDEN, 0, False)], [],
                          [(2 * FFN_HIDDEN, 2 * FFN_HIDDEN, False, bf)], name="b_swiglu")
    g['ffn_in'] = _mm2(h1b, dffn_in, ta=True, name="g_ffn_in")
    dh1 = _mm2(dffn_in, wt['ffn_in'], tb=True, name="b_dh1_ffn", add=dpre2, add_scale=ALPHA)
    dh1 = _mm2(dgate_pre, wt['ple_gate'], tb=True, name="b_dh1_gate", add=dh1)

    def ln1_bwd(h, xv, ao, d, gv, bv):
        _, vjp = jax.vjp(_ln1, xv, ao, gv, bv)
        _, dao, dg, db = vjp(d)
        return dao, dao, dg, db

    dpre1, dpre1b, g['ln1_g'], g['ln1_b'] = _rowwise(
        ln1_bwd, [(x, D_MODEL, 0, False), (attn_out, D_MODEL, 0, False), (dh1, D_MODEL, 0, False)],
        [sp['ln1_g'], sp['ln1_b']], [(D_MODEL, D_MODEL, False, F32), (D_MODEL, D_MODEL, False, bf)],
        [(1, D_MODEL), (1, D_MODEL)], name="b_ln1")

    g['o'] = _mm2(mixed, dpre1b, ta=True, name="g_o")
    dmixed = _mm2(dpre1b, wt['o'], tb=True, name="b_dmixed")

    def merge_bwd(h, ggv, yd, ym, d):
        _, vjp = jax.vjp(_merge, ggv, yd, ym)
        return vjp(d)

    dgg, dy_dn, dy_mla = _rowwise(
        merge_bwd, [(gg, 2 * D_MODEL, 0, False), (y_dn, D_MODEL, 0, False), (y_mla, D_MODEL, 0, False),
                    (dmixed, D_MODEL, 0, False)], [],
        [(2 * D_MODEL, 2 * D_MODEL, False, bf), (D_MODEL, D_MODEL, False, bf), (D_MODEL, D_MODEL, False, bf)],
        name="b_merge")
    g['br_dn'] = _mm2(og, dy_dn, ta=True, name="g_br_dn")
    dog = _mm2(dy_dn, wt['br_dn'], tb=True, name="b_dog")
    g['br_mla'] = _mm2(o_mla, dy_mla, ta=True, name="g_br_mla")
    do_mla = _mm2(dy_mla, wt['br_mla'], tb=True, name="b_do_mla", out_dtype=bf)

    dout_lat = _mm(do_mla, wt['uv'], name="b_dout_lat", tb=True, heads=HEADS, a_head='col', b_head='lead',
                   out_head='lead', dims=(t, KV_LORA, NOPE))
    g['uv'] = _mm(out_lat, do_mla, name="g_uv", ta=True, heads=HEADS, a_head='lead', b_head='col', out_head='lead',
                  dims=(KV_LORA, NOPE, t))
    dq_lat, dq_rope_pre, dckv_att, dkr_att = _attention_bwd(q_lat, q_rope, cosb, sinb, c_kv, k_rope, out_lat, lse, dout_lat)
    dq_nope = _mm(dq_lat, wt['uk'], name="b_dq_nope", heads=HEADS, a_head='lead', b_head='lead', out_head='col',
                  dims=(t, NOPE, KV_LORA), out_dtype=bf)
    g['uk'] = _mm(dq_lat, q_nope, name="g_uk", ta=True, heads=HEADS, a_head='lead', b_head='col', out_head='lead',
                  dims=(KV_LORA, NOPE, t))
    g['uq_nope'] = _mm2(c_q, dq_nope, ta=True, name="g_uq_nope")
    g['uq_rope'] = _mm2(c_q, dq_rope_pre, ta=True, name="g_uq_rope")
    dc_q = _mm2(dq_nope, wt['uq_nope'], tb=True, name="b_dcq_nope")
    dc_q = _mm2(dq_rope_pre, wt['uq_rope'], tb=True, name="b_dcq_rope", add=dc_q)

    def gated_norm_bwd(h, o, zz, d, w):
        _, vjp = jax.vjp(_gated_norm_heads, o, zz, w)
        return vjp(d)

    do_dn, dz, g['dn_norm_w'] = _rowwise(
        gated_norm_bwd, [(o_dn, D_MODEL, 0, False), (z, D_MODEL, 0, False), (dog, D_MODEL, 0, False)], [sp['dn_norm_w']],
        [(D_MODEL, D_MODEL, False, F32), (D_MODEL, D_MODEL, False, bf)], [(1, LANE)], name="b_gated_norm")
    (du, dw, dqd, dkt, dintra, dgl), paired = _delta_scan_bwd(u, w_, qd, kt, intra, gl, sall, do_dn, rider=exch.pair_send(g))
    (dq_a, dk_a, dv_a, dba, g['a_log'], g['dt_bias']), arrived = _delta_local_bwd(
        qkv_act, pm, sp['a_log'], sp['dt_bias'], t_inv, du, dw, dqd, dkt, dintra, dgl, rider=exch.reduce_send(paired))
    exch.reduce_arrived(arrived)
    dqkv_pre, g['conv_w'] = _conv_silu_bwd(qkv_pre, sp['conv_w'], [dq_a, dk_a, dv_a])

    def mla_pre_bwd(h, ckv, cq, cosv, sinv, dcq, dckv, dkr, dba_v, qw, kw):
        _, vjp = jax.vjp(lambda a, c, d, e: (_rms_norm(c, d), _rms_norm(a, e)), ckv, cq, qw, kw)
        dckv_p, dcq_p, dqw, dkw = vjp((dcq, dckv))
        dkr_p = _rope_bwd(dkr, cosv, sinv)
        dpm = jnp.concatenate([dckv_p, dkr_p, dba_v, jnp.zeros((ckv.shape[0], 2 * LANE), F32), dcq_p], axis=1)
        return dpm, dqw, dkw

    dpm, g['q_norm_w'], g['kv_norm_w'] = _rowwise(
        mla_pre_bwd,
        [(pm, KV_LORA, 0, False), (pm, Q_LORA, 2, False), (cosb, ROPE_PAD, 0, False), (sinb, ROPE_PAD, 0, False),
         (dc_q, Q_LORA, 0, False), (dckv_att, KV_LORA, 0, False), (dkr_att, ROPE_PAD, 0, False), (dba, LANE, 0, False)],
        [sp['q_norm_w'], sp['kv_norm_w']], [(1152, 1152, False, bf)], [(1, Q_LORA), (1, KV_LORA)], name="b_mla_pre")

    g['qkv'] = _mm2(xb, dqkv_pre, ta=True, name="g_qkv")
    g['z'] = _mm2(xb, dz, ta=True, name="g_z")
    g['gg'] = _mm2(xb, dgg, ta=True, name="g_gg")
    g['mla'] = _mm2(xb, dpm, ta=True, name="g_mla")
    dx, arrived = _dx_fused([(dqkv_pre, wt['qkv']), (dz, wt['z']), (dgg, wt['gg']), (dpm, wt['mla'])], dpre1, ALPHA,
                            rider=exch.in_send(g))
    exch.in_arrived(arrived)
    return loss_lanes, dx, g


_IN_SIZES = (QKV_W, HEADS * DN_DK, HEADS, HEADS, Q_LORA, KV_LORA, ROPE, D_MODEL, D_MODEL)


def _rope_tables(positions):
    inv_freq = ROPE_BASE ** (-jnp.arange(0, ROPE, 2, dtype=F32) / ROPE)
    ang = positions.astype(F32)[:, None] * inv_freq
    cos, sin = jnp.cos(ang), jnp.sin(ang)
    zeros = jnp.zeros((positions.shape[0], ROPE_PAD - ROPE), F32)
    return jnp.concatenate([cos, cos, zeros], axis=1), jnp.concatenate([-sin, sin, zeros], axis=1)


def _prep_w_in(w_in):
    dt = w_in.dtype
    offs = [0]
    for s in _IN_SIZES:
        offs.append(offs[-1] + s)
    qkv, z, wb, wa, cq, ckv, kr, gd, gm = [w_in[:, offs[i]:offs[i + 1]] for i in range(len(_IN_SIZES))]
    zc = lambda n: jnp.zeros((D_MODEL, n), dt)
    return {
        'qkv': qkv, 'z': z, 'gg': jnp.concatenate([gd, gm], axis=1),
        'mla': jnp.concatenate([ckv, kr, zc(ROPE_PAD - ROPE), wb, wa, zc(LANE - 2 * HEADS), zc(2 * LANE), cq], axis=1),
    }


def _prep_weights(full):
    w_uq = full['w_uq']
    wt = {
        'uq_nope': w_uq[:, :, :NOPE].reshape(Q_LORA, HEADS * NOPE),
        'uq_rope': jnp.pad(w_uq[:, :, NOPE:], ((0, 0), (0, 0), (0, ROPE_PAD - ROPE))).reshape(Q_LORA, HEADS * ROPE_PAD),
        'uk': jnp.transpose(full['w_uk'], (1, 0, 2)), 'uv': jnp.transpose(full['w_uv'], (1, 0, 2)),
        'br_dn': full['w_br_dn'], 'br_mla': full['w_br_mla'], 'o': full['w_o'], 'ffn_in': full['w_ffn_in'],
        'ffn_out': full['w_ffn_out'], 'ple': full['w_ple'], 'ple_gate': full['w_ple_gate'],
    }
    return wt


def _prep_small(small):
    pad = lambda v: jnp.pad(v, (0, LANE - v.shape[0]))[None, :]
    return {
        'conv_w': small['conv_w'], 'a_log': pad(small['dn_a_log']), 'dt_bias': pad(small['dn_dt_bias']),
        'dn_norm_w': small['dn_norm_w'][None, :], 'q_norm_w': small['q_norm_w'][None, :],
        'kv_norm_w': small['kv_norm_w'][None, :], 'ln1_g': small['ln1_g'][None, :], 'ln1_b': small['ln1_b'][None, :],
        'ln2_g': small['ln2_g'][None, :], 'ln2_b': small['ln2_b'][None, :],
    }


def _w_in_grad(g):
    mla = g['mla']
    ba0 = KV_LORA + ROPE_PAD
    cq0 = ba0 + 3 * LANE
    return jnp.concatenate([
        g['qkv'], g['z'], mla[:, ba0:ba0 + HEADS], mla[:, ba0 + HEADS:ba0 + 2 * HEADS], mla[:, cq0:cq0 + Q_LORA],
        mla[:, :KV_LORA], mla[:, KV_LORA:KV_LORA + ROPE], g['gg']], axis=1)


def _unprep_grads_late(g):
    return {
        'conv_w': g['conv_w'], 'dn_a_log': g['a_log'][0, :HEADS], 'dn_dt_bias': g['dt_bias'][0, :HEADS],
        'dn_norm_w': g['dn_norm_w'][0], 'q_norm_w': g['q_norm_w'][0], 'kv_norm_w': g['kv_norm_w'][0],
        'ln1_g': g['ln1_g'][0], 'ln1_b': g['ln1_b'][0], 'ln2_g': g['ln2_g'][0], 'ln2_b': g['ln2_b'][0],
    }


def _unprep_grads_early(g):
    w_uq = jnp.concatenate([g['uq_nope'].reshape(Q_LORA, HEADS, NOPE),
                            g['uq_rope'].reshape(Q_LORA, HEADS, ROPE_PAD)[:, :, :ROPE]], axis=2)
    return {
        'w_uq': w_uq, 'w_uk': jnp.transpose(g['uk'], (1, 0, 2)), 'w_uv': jnp.transpose(g['uv'], (1, 0, 2)),
        'w_br_dn': g['br_dn'], 'w_br_mla': g['br_mla'], 'w_o': g['o'],
        'w_ffn_in': g['ffn_in'], 'w_ffn_out': g['ffn_out'], 'w_ple': g['ple'], 'w_ple_gate': g['ple_gate'],
    }


_FLATB_PIECES = (
    ('w_ffn_out', 704, (704, D_MODEL)), ('w_br_dn', 256, (256, D_MODEL)), ('w_br_mla', 256, (256, D_MODEL)),
    ('w_o', 256, (256, D_MODEL)), ('w_ple_gate', 256, (256, D_MODEL)), ('w_uq', 144, (96, HEADS, NOPE + ROPE)),
    ('w_uk', 64, (64, HEADS, NOPE)), ('w_uv', 64, (64, HEADS, NOPE)), ('w_ple', 64, (PLE_DIM, 256)),
)
FLATB_ROWS = 2112
W_IN_SHARD = D_IN // N_SHARD
FFN_IN_SHARD = 2 * FFN_HIDDEN // N_SHARD
A_ROWS = D_MODEL + 32
_CONV_SHARD = QKV_W // N_SHARD
_ADD_TILES = (256, 256, 352)


def _flatb_offsets():
    offs, o = {}, 0
    for name, rows, _ in _FLATB_PIECES:
        offs[name] = o
        o += rows
    return offs, o


def _pack_shards(ws, conv_w):
    conv_bits = lax.bitcast_convert_type(conv_w, jnp.bfloat16).reshape(DN_CONV, 2 * _CONV_SHARD).astype(_CDT)
    tail = jnp.pad(conv_bits, ((0, A_ROWS - D_MODEL - DN_CONV), (0, W_IN_SHARD - 2 * _CONV_SHARD)))
    a_buf = jnp.concatenate([ws['w_in'].astype(_CDT), tail], axis=0)
    parts = [ws[name].astype(_CDT).reshape(rows, FLAT_W) for name, rows, _ in _FLATB_PIECES]
    used = sum(p.shape[0] for p in parts)
    parts.append(jnp.zeros((FLATB_ROWS - used, FLAT_W), _CDT))
    return [a_buf, ws['w_ffn_in'].astype(_CDT), jnp.concatenate(parts, axis=0)]


def _unpack_w_in(gathered, local, me):
    a = [jnp.where(me == s, local, gathered[s]) for s in range(N_SHARD)]
    conv = [lax.bitcast_convert_type(
        p[D_MODEL:D_MODEL + DN_CONV, :2 * _CONV_SHARD].astype(jnp.bfloat16).reshape(DN_CONV, _CONV_SHARD, 2), F32) for p in a]
    return jnp.concatenate([p[:D_MODEL] for p in a], axis=1), jnp.concatenate(conv, axis=1)


def _unpack_rest(gathered, local, me):
    pick = lambda b, s: jnp.where(me == s, local[b], gathered[b][s])
    full = {'w_ffn_in': jnp.concatenate([pick(0, s) for s in range(N_SHARD)], axis=1)}
    offs, _ = _flatb_offsets()
    fb = [pick(1, s) for s in range(N_SHARD)]
    for name, rows, shape in _FLATB_PIECES:
        pieces = [p[offs[name]:offs[name] + rows].reshape(shape) for p in fb]
        full[name] = jnp.concatenate(pieces, axis=1 if name == 'w_ple' else 0)
    return full


def _shard_columns(g, w):
    return jnp.stack([g[:, s * w:(s + 1) * w] for s in range(N_SHARD)])


def _pack_grads_rest(gw):
    parts = []
    for name, rows, _ in _FLATB_PIECES:
        g = gw[name]
        if name == 'w_ple':
            parts.append(_shard_columns(g, PLE_DIM).reshape(N_SHARD, rows, FLAT_W))
        else:
            parts.append(g.reshape(N_SHARD, rows, FLAT_W))
    used = sum(p.shape[1] for p in parts)
    parts.append(jnp.zeros((N_SHARD, FLATB_ROWS - used, FLAT_W), F32))
    return [_shard_columns(gw['w_ffn_in'], FFN_IN_SHARD), jnp.concatenate(parts, axis=1)]


def _unpack_reduced(mine, theirs, c):
    whole = [jnp.concatenate([jnp.where(c == 0, m, t), jnp.where(c == 0, t, m)], axis=0) for m, t in zip(mine, theirs)]
    out = {'w_in': whole[0], 'w_ffn_in': whole[1]}
    offs, _ = _flatb_offsets()
    for name, rows, shape in _FLATB_PIECES:
        out[name] = whole[2][offs[name]:offs[name] + rows].reshape(shape)
    return out


_HBM = pl.BlockSpec(memory_space=pltpu.HBM)


def _place():
    x, y, c = lax.axis_index("x"), lax.axis_index("y"), lax.axis_index("c")
    chips = [(1 - x, y), (x, 1 - y), (1 - x, 1 - y)]
    return x, y, c, chips


def _remote(src, dst, send_sems, recv_sems, k, to):
    return pltpu.make_async_remote_copy(src_ref=src, dst_ref=dst, send_sem=send_sems.at[k], recv_sem=recv_sems.at[k],
                                        device_id=to, device_id_type=_MESH)


def _half_rows(ref, half, hf, lead=None):
    rows = pl.ds(pl.multiple_of(hf * half, 16), half)
    return ref.at[rows, :] if lead is None else ref.at[lead, rows, :]


class _Rider:
    def __init__(self, inputs, out_shape, n_sems, copies, aliases=None):
        self.inputs, self.out_shape, self.n_sems, self.copies = list(inputs), list(out_shape), n_sems, copies
        self.aliases = aliases or {}


def _carried_call(body, rider, first, last, *, name, grid, in_specs, out_specs, out_shape, scratch_shapes, sem, args):
    n_in, n_out, n_scr = len(in_specs), len(out_specs), len(scratch_shapes)
    if rider is None:
        res = pl.pallas_call(body, name=name, grid=grid, in_specs=in_specs, out_specs=out_specs, out_shape=out_shape,
                             scratch_shapes=scratch_shapes, compiler_params=_cparams(sem))(*args)
        return list(res), []
    ri, ro = len(rider.inputs), len(rider.out_shape)

    def full_body(*refs):
        own_in, r_in = refs[:n_in], refs[n_in:n_in + ri]
        o0 = n_in + ri
        own_out, r_out = refs[o0:o0 + n_out], refs[o0 + n_out:o0 + n_out + ro]
        s0 = o0 + n_out + ro
        own_scr, send_sems, recv_sems = refs[s0:s0 + n_scr], refs[s0 + n_scr], refs[s0 + n_scr + 1]

        @pl.when(first())
        def _():
            sends, _ = rider.copies(r_in, r_out, send_sems, recv_sems)
            for cp in sends:
                cp.start()

        body(*own_in, *own_out, *own_scr)

        @pl.when(last())
        def _():
            sends, arrivals = rider.copies(r_in, r_out, send_sems, recv_sems)
            for cp in arrivals():
                cp.wait_recv()
            for cp in sends:
                cp.wait_send()

    res = pl.pallas_call(
        full_body, name=name, grid=grid, in_specs=list(in_specs) + [_HBM] * ri, out_specs=list(out_specs) + [_HBM] * ro,
        out_shape=list(out_shape) + rider.out_shape,
        scratch_shapes=list(scratch_shapes) + [pltpu.SemaphoreType.DMA((rider.n_sems,))] * 2,
        input_output_aliases={n_in + i: n_out + o for i, o in rider.aliases.items()},
        compiler_params=_cparams(sem))(*args, *rider.inputs)
    return list(res[:n_out]), list(res[n_out:])


def _ride_gather_send(bufs):
    n = len(bufs)
    halves = [b.shape[0] // 2 for b in bufs]

    def copies(ins, outs, send_sems, recv_sems):
        x, y, c, chips = _place()
        slot = lambda b, cx, cy: _half_rows(outs[b], halves[b], c, lead=2 * cx + cy)
        sends = [_remote(_half_rows(ins[b], halves[b], c), slot(b, x, y), send_sems, recv_sems, 3 * b + j, (cx, cy, c))
                 for b in range(n) for j, (cx, cy) in enumerate(chips)]
        arrivals = lambda: [_remote(slot(b, cx, cy), slot(b, cx, cy), send_sems, recv_sems, 3 * b + j, (x, y, c))
                            for b in range(n) for j, (cx, cy) in enumerate(chips)]
        return sends, arrivals

    return _Rider(bufs, [jax.ShapeDtypeStruct((N_SHARD,) + b.shape, b.dtype) for b in bufs], 3 * n, copies)


def _ride_gather_pass(gathered):
    n = len(gathered)
    halves = [g.shape[1] // 2 for g in gathered]

    def copies(ins, outs, send_sems, recv_sems):
        x, y, c, chips = _place()
        slot = lambda b, cx, cy, hf: _half_rows(outs[b], halves[b], hf, lead=2 * cx + cy)
        sends = [_remote(slot(b, cx, cy, c), slot(b, cx, cy, c), send_sems, recv_sems, 3 * b + j, (x, y, 1 - c))
                 for b in range(n) for j, (cx, cy) in enumerate(chips)]
        arrivals = lambda: [_remote(slot(b, cx, cy, 1 - c), slot(b, cx, cy, 1 - c), send_sems, recv_sems, 3 * b + j, (x, y, c))
                            for b in range(n) for j, (cx, cy) in enumerate(chips)]
        return sends, arrivals

    return _Rider(gathered, [jax.ShapeDtypeStruct(g.shape, g.dtype) for g in gathered], 3 * n, copies,
                  aliases={b: b for b in range(n)})


def _ride_pair_exchange(gbufs):
    n = len(gbufs)
    halves = [g.shape[1] // 2 for g in gbufs]

    def copies(ins, outs, send_sems, recv_sems):
        x, y, c, _ = _place()
        sends = [_remote(ins[b].at[:, pl.ds(pl.multiple_of((1 - c) * halves[b], 16), halves[b]), :], outs[b],
                         send_sems, recv_sems, b, (x, y, 1 - c)) for b in range(n)]
        arrivals = lambda: [_remote(outs[b], outs[b], send_sems, recv_sems, b, (x, y, c)) for b in range(n)]
        return sends, arrivals

    return _Rider(gbufs, [jax.ShapeDtypeStruct((N_SHARD, h, g.shape[2]), g.dtype) for g, h in zip(gbufs, halves)], n, copies)


def _ride_chip_exchange(parts):
    n = len(parts)

    def copies(ins, outs, send_sems, recv_sems):
        x, y, c, chips = _place()
        sends = [_remote(ins[b].at[2 * cx + cy], outs[b].at[j], send_sems, recv_sems, 3 * b + j, (cx, cy, c))
                 for b in range(n) for j, (cx, cy) in enumerate(chips)]
        arrivals = lambda: [_remote(ins[b].at[0], outs[b].at[j], send_sems, recv_sems, 3 * b + j, (x, y, c))
                            for b in range(n) for j in range(len(chips))]
        return sends, arrivals

    return _Rider(parts, [jax.ShapeDtypeStruct((3,) + p.shape[1:], p.dtype) for p in parts], 3 * n, copies)


def _gather_shards(bufs, name):
    n = len(bufs)
    halves = [b.shape[0] // 2 for b in bufs]

    def body(*refs):
        ins, outs, send_sems, recv_sems = refs[:n], refs[n:2 * n], refs[2 * n], refs[2 * n + 1]
        x, y, c, chips = _place()
        me, sibling = (x, y, c), (x, y, 1 - c)
        slot = lambda b, cx, cy, hf: _half_rows(outs[b], halves[b], hf, lead=2 * cx + cy)
        first = [_remote(_half_rows(ins[b], halves[b], c), slot(b, x, y, c), send_sems, recv_sems, 6 * b + j, (cx, cy, c))
                 for b in range(n) for j, (cx, cy) in enumerate(chips)]
        for cp in first:
            cp.start()
        passed = []
        for j, (cx, cy) in enumerate(chips):
            for b in range(n):
                _remote(slot(b, cx, cy, c), slot(b, cx, cy, c), send_sems, recv_sems, 6 * b + j, me).wait_recv()
                fwd = _remote(slot(b, cx, cy, c), slot(b, cx, cy, c), send_sems, recv_sems, 6 * b + 3 + j, sibling)
                fwd.start()
                passed.append(fwd)
        for j, (cx, cy) in enumerate(chips):
            for b in range(n):
                _remote(slot(b, cx, cy, 1 - c), slot(b, cx, cy, 1 - c), send_sems, recv_sems, 6 * b + 3 + j, me).wait_recv()
        for cp in first + passed:
            cp.wait_send()

    return pl.pallas_call(
        body, name=name, out_shape=[jax.ShapeDtypeStruct((N_SHARD,) + b.shape, b.dtype) for b in bufs],
        in_specs=[_HBM] * n, out_specs=[_HBM] * n,
        scratch_shapes=[pltpu.SemaphoreType.DMA((6 * n,)), pltpu.SemaphoreType.DMA((6 * n,))],
    )(*bufs)


def _reduce_pair_exchange(gbufs, name):
    n = len(gbufs)
    halves = [g.shape[1] // 2 for g in gbufs]

    def body(*refs):
        ins, outs, send_sems, recv_sems = refs[:n], refs[n:2 * n], refs[2 * n], refs[2 * n + 1]
        x, y, c, _ = _place()
        cps = [_remote(ins[b].at[:, pl.ds(pl.multiple_of((1 - c) * halves[b], 16), halves[b]), :], outs[b],
                       send_sems, recv_sems, b, (x, y, 1 - c)) for b in range(n)]
        for cp in cps:
            cp.start()
        for cp in cps:
            cp.wait()

    return pl.pallas_call(
        body, name=name,
        out_shape=[jax.ShapeDtypeStruct((N_SHARD, h, g.shape[2]), g.dtype) for g, h in zip(gbufs, halves)],
        in_specs=[_HBM] * n, out_specs=[_HBM] * n,
        scratch_shapes=[pltpu.SemaphoreType.DMA((n,)), pltpu.SemaphoreType.DMA((n,))],
    )(*gbufs)


def _pair_add(gbuf, recv, c_arr, tr, name):
    _, rows, width = gbuf.shape
    half = rows // 2
    nt = half // tr

    def body(c_ref, a_ref, b_ref, o_ref):
        o_ref[...] = (a_ref[...] + b_ref[...]).astype(o_ref.dtype)

    blk = lambda f: pl.BlockSpec((None, tr, width), f)
    return pl.pallas_call(
        body, name=name, out_shape=jax.ShapeDtypeStruct((N_SHARD, half, width), jnp.bfloat16),
        grid_spec=pltpu.PrefetchScalarGridSpec(
            num_scalar_prefetch=1, grid=(N_SHARD, nt),
            in_specs=[blk(lambda s, i, c: (s, c[0] * nt + i, 0)), blk(lambda s, i, c: (s, i, 0))],
            out_specs=blk(lambda s, i, c: (s, i, 0))),
        compiler_params=_cparams(("parallel", "parallel")))(c_arr, gbuf, recv)


def _reduce_chip_exchange(parts, name):
    n = len(parts)

    def body(*refs):
        ins, outs, send_sems, recv_sems = refs[:n], refs[n:2 * n], refs[2 * n], refs[2 * n + 1]
        x, y, c, chips = _place()
        sends = [_remote(ins[b].at[2 * cx + cy], outs[b].at[j], send_sems, recv_sems, 3 * b + j, (cx, cy, c))
                 for b in range(n) for j, (cx, cy) in enumerate(chips)]
        for cp in sends:
            cp.start()
        for b in range(n):
            for j in range(len(chips)):
                _remote(ins[b].at[0], outs[b].at[j], send_sems, recv_sems, 3 * b + j, (x, y, c)).wait_recv()
        for cp in sends:
            cp.wait_send()

    return pl.pallas_call(
        body, name=name, out_shape=[jax.ShapeDtypeStruct((3,) + p.shape[1:], p.dtype) for p in parts],
        in_specs=[_HBM] * n, out_specs=[_HBM] * n,
        scratch_shapes=[pltpu.SemaphoreType.DMA((3 * n,)), pltpu.SemaphoreType.DMA((3 * n,))],
    )(*parts)


def _chip_add(part, recv, me_arr, tr, name):
    _, half, width = part.shape

    def body(me_ref, own, a0, a1, a2, o_ref):
        f = lambda r: r[...].astype(F32)
        o_ref[...] = ((f(own) + f(a0)) + f(a1)) + f(a2)

    specs = [pl.BlockSpec((None, tr, width), lambda i, me: (me[0], i, 0))]
    specs += [pl.BlockSpec((None, tr, width), functools.partial(lambda i, me, k: (k, i, 0), k=k)) for k in range(3)]
    return pl.pallas_call(
        body, name=name, out_shape=jax.ShapeDtypeStruct((half, width), F32),
        grid_spec=pltpu.PrefetchScalarGridSpec(
            num_scalar_prefetch=1, grid=(half // tr,), in_specs=specs,
            out_specs=pl.BlockSpec((tr, width), lambda i, me: (i, 0))),
        compiler_params=_cparams(("parallel",)))(me_arr, part, recv, recv, recv)


def _reduce_pair_share(rhalves, name):
    n = len(rhalves)

    def body(*refs):
        ins, outs, send_sems, recv_sems = refs[:n], refs[n:2 * n], refs[2 * n], refs[2 * n + 1]
        x, y, c, _ = _place()
        cps = [_remote(ins[b], outs[b], send_sems, recv_sems, b, (x, y, 1 - c)) for b in range(n)]
        for cp in cps:
            cp.start()
        for cp in cps:
            cp.wait()

    return pl.pallas_call(
        body, name=name, out_shape=[jax.ShapeDtypeStruct(r.shape, r.dtype) for r in rhalves],
        in_specs=[_HBM] * n, out_specs=[_HBM] * n,
        scratch_shapes=[pltpu.SemaphoreType.DMA((n,)), pltpu.SemaphoreType.DMA((n,))],
    )(*rhalves)


def _small_allreduce(buf):
    r, width = buf.shape
    n_dev = 8

    def body(x_ref, all_ref, sum_ref, send_sems, recv_sems, local_sem):
        x, y, c, chips = _place()
        me, sibling = (x, y, c), (x, y, 1 - c)

        def rows(px, py, pc):
            return all_ref.at[pl.ds(pl.multiple_of((4 * px + 2 * py + pc) * r, 8), r), :]

        def copy(k, block, to, src=None):
            return _remote(rows(*block) if src is None else src, rows(*block), send_sems, recv_sems, k, to)

        mine = pltpu.make_async_copy(x_ref, rows(*me), local_sem)
        mine.start()
        first = [copy(0, me, sibling, src=x_ref)]
        first += [copy(1 + j, me, (*chip, c), src=x_ref) for j, chip in enumerate(chips)]
        for cp in first:
            cp.start()
        passed = [copy(4 + j, (*chip, c), sibling) for j, chip in enumerate(chips)]
        for j, chip in enumerate(chips):
            copy(1 + j, (*chip, c), me).wait_recv()
            passed[j].start()
        copy(0, sibling, me).wait_recv()
        for j, chip in enumerate(chips):
            copy(4 + j, (*chip, 1 - c), me).wait_recv()
        for cp in first + passed:
            cp.wait_send()
        mine.wait()
        total = all_ref[0:r, :]
        for k in range(1, n_dev):
            total = total + all_ref[k * r:(k + 1) * r, :]
        sum_ref[...] = total

    vm = pl.BlockSpec(memory_space=pltpu.VMEM)
    _, total = pl.pallas_call(
        body, name="small_allreduce",
        out_shape=[jax.ShapeDtypeStruct((n_dev * r, width), buf.dtype), jax.ShapeDtypeStruct((r, width), buf.dtype)],
        in_specs=[vm], out_specs=[vm, vm],
        scratch_shapes=[pltpu.SemaphoreType.DMA((7,)), pltpu.SemaphoreType.DMA((7,)), pltpu.SemaphoreType.DMA],
    )(buf)
    return total


def _row_tile(rows, cap):
    if rows <= cap:
        return rows
    t = (cap // 8) * 8
    while t >= 8:
        if rows % t == 0:
            return t
        t -= 8
    return rows


def _adamw(w, g, m, v, name):
    shape = w.shape
    cols = shape[-1] if len(shape) <= 3 else shape[-2] * shape[-1]
    lead = len(shape) == 3
    w2, g2, m2, v2 = (a if lead else a.reshape(-1, cols) for a in (w, g, m, v))
    rows = shape[1] if lead else w2.shape[0]
    tr, tc = _row_tile(rows, 256), cols
    if tr == rows and rows > 256:
        tc = _tile(cols, 256)

    def body(w_ref, g_ref, m_ref, v_ref, d_ref, mo_ref, vo_ref):
        gv = g_ref[...]
        mn = ADAM_B1 * m_ref[...] + (1.0 - ADAM_B1) * gv
        vn = ADAM_B2 * v_ref[...] + (1.0 - ADAM_B2) * (gv * gv)
        m_hat = mn / (1.0 - ADAM_B1 ** ADAM_STEP)
        v_hat = vn / (1.0 - ADAM_B2 ** ADAM_STEP)
        d_ref[...] = -ADAM_LR * (m_hat / (jnp.sqrt(v_hat) + ADAM_EPS) + ADAM_WD * w_ref[...])
        mo_ref[...] = mn
        vo_ref[...] = vn

    blk = (pl.BlockSpec((None, tr, tc), lambda i, j: (0, i, j)) if lead else pl.BlockSpec((tr, tc), lambda i, j: (i, j)))
    outs = pl.pallas_call(
        body, name=name, grid=(rows // tr, cols // tc), in_specs=[blk] * 4, out_specs=[blk] * 3,
        out_shape=[jax.ShapeDtypeStruct(w2.shape, F32)] * 3,
        compiler_params=_cparams(("parallel", "parallel")))(w2, g2, m2, v2)
    return tuple(o.reshape(shape) for o in outs)


_WEIGHT_NAMES = ('w_in', 'conv_w', 'dn_a_log', 'dn_dt_bias', 'dn_norm_w', 'q_norm_w', 'w_uq', 'kv_norm_w', 'w_uk',
                 'w_uv', 'w_br_dn', 'w_br_mla', 'w_o', 'ln1_g', 'ln1_b', 'w_ffn_in', 'w_ffn_out', 'w_ple',
                 'w_ple_gate', 'ln2_g', 'ln2_b')
_SMALL_NAMES = ('ln1_g', 'ln1_b', 'ln2_g', 'ln2_b', 'q_norm_w', 'kv_norm_w', 'dn_norm_w', 'dn_a_log', 'dn_dt_bias')
_SMALL_GROUP = 8
_CONV_SMALL_ROW = len(_SMALL_NAMES) * _SMALL_GROUP
_CONV_SMALL_ROWS = DN_CONV * QKV_W // FLAT_W


def _pack_small(gw):
    rows = [jnp.pad(gw[n][None, :], ((0, _SMALL_GROUP - 1), (0, FLAT_W - gw[n].shape[0]))) for n in _SMALL_NAMES]
    rows.append(jnp.pad(gw['conv_w'].reshape(_CONV_SMALL_ROWS, FLAT_W), ((0, SMALL_ROWS - _CONV_SMALL_ROW - _CONV_SMALL_ROWS), (0, 0))))
    return jnp.concatenate(rows, axis=0)


class _Exchange:
    def __init__(self, local, me_chip, c_arr):
        self.local, self.me_chip, self.c_arr = local, me_chip, c_arr
        self.parts = self.arrived = None

    def gather_send(self):
        return _ride_gather_send(self.local)

    def gather_pass(self, sent):
        return _ride_gather_pass(sent)

    def weights(self, gathered):
        return _prep_weights(_unpack_rest(gathered, self.local, self.me_chip))

    def pair_send(self, g):
        self.gbufs = _pack_grads_rest(_unprep_grads_early(g))
        return _ride_pair_exchange(self.gbufs)

    def reduce_send(self, got):
        self.parts = [_pair_add(g_, r_, self.c_arr, tr, "pair_add_%d" % (i + 1))
                      for i, (g_, r_, tr) in enumerate(zip(self.gbufs, got, _ADD_TILES[1:]))]
        return _ride_chip_exchange(self.parts)

    def reduce_arrived(self, arrived):
        self.arrived = list(arrived)

    def in_send(self, g):
        g_in = [_shard_columns(_w_in_grad(g), W_IN_SHARD)]
        got = _reduce_pair_exchange(g_in, "reduce_pair_exchange_w_in")
        self.part_in = _pair_add(g_in[0], got[0], self.c_arr, _ADD_TILES[0], "pair_add_0")
        return _ride_chip_exchange([self.part_in])

    def in_arrived(self, arrived):
        self.arrived_in = list(arrived)


def kernel(x, p, positions, w_in, conv_w, dn_a_log, dn_dt_bias, dn_norm_w, q_norm_w, w_uq, kv_norm_w, w_uk, w_uv, w_br_dn, w_br_mla, w_o, ln1_g, ln1_b, w_ffn_in, w_ffn_out, w_ple, w_ple_gate, ln2_g, ln2_b, loss_target, m_w_in, m_conv_w, m_dn_a_log, m_dn_dt_bias, m_dn_norm_w, m_q_norm_w, m_w_uq, m_kv_norm_w, m_w_uk, m_w_uv, m_w_br_dn, m_w_br_mla, m_w_o, m_ln1_g, m_ln1_b, m_w_ffn_in, m_w_ffn_out, m_w_ple, m_w_ple_gate, m_ln2_g, m_ln2_b, v_w_in, v_conv_w, v_dn_a_log, v_dn_dt_bias, v_dn_norm_w, v_q_norm_w, v_w_uq, v_kv_norm_w, v_w_uk, v_w_uv, v_w_br_dn, v_w_br_mla, v_w_o, v_ln1_g, v_ln1_b, v_w_ffn_in, v_w_ffn_out, v_w_ple, v_w_ple_gate, v_ln2_g, v_ln2_b):
    ws = dict(w_in=w_in, conv_w=conv_w, dn_a_log=dn_a_log, dn_dt_bias=dn_dt_bias, dn_norm_w=dn_norm_w, q_norm_w=q_norm_w,
              w_uq=w_uq, kv_norm_w=kv_norm_w, w_uk=w_uk, w_uv=w_uv, w_br_dn=w_br_dn, w_br_mla=w_br_mla, w_o=w_o,
              ln1_g=ln1_g, ln1_b=ln1_b, w_ffn_in=w_ffn_in, w_ffn_out=w_ffn_out, w_ple=w_ple, w_ple_gate=w_ple_gate,
              ln2_g=ln2_g, ln2_b=ln2_b)
    ms = dict(w_in=m_w_in, conv_w=m_conv_w, dn_a_log=m_dn_a_log, dn_dt_bias=m_dn_dt_bias, dn_norm_w=m_dn_norm_w,
              q_norm_w=m_q_norm_w, w_uq=m_w_uq, kv_norm_w=m_kv_norm_w, w_uk=m_w_uk, w_uv=m_w_uv, w_br_dn=m_w_br_dn,
              w_br_mla=m_w_br_mla, w_o=m_w_o, ln1_g=m_ln1_g, ln1_b=m_ln1_b, w_ffn_in=m_w_ffn_in, w_ffn_out=m_w_ffn_out,
              w_ple=m_w_ple, w_ple_gate=m_w_ple_gate, ln2_g=m_ln2_g, ln2_b=m_ln2_b)
    vs = dict(w_in=v_w_in, conv_w=v_conv_w, dn_a_log=v_dn_a_log, dn_dt_bias=v_dn_dt_bias, dn_norm_w=v_dn_norm_w,
              q_norm_w=v_q_norm_w, w_uq=v_w_uq, kv_norm_w=v_kv_norm_w, w_uk=v_w_uk, w_uv=v_w_uv, w_br_dn=v_w_br_dn,
              w_br_mla=v_w_br_mla, w_o=v_w_o, ln1_g=v_ln1_g, ln1_b=v_ln1_b, w_ffn_in=v_w_ffn_in, w_ffn_out=v_w_ffn_out,
              w_ple=v_w_ple, w_ple_gate=v_w_ple_gate, ln2_g=v_ln2_g, ln2_b=v_ln2_b)
    mx, my, mc = lax.axis_index("x"), lax.axis_index("y"), lax.axis_index("c")

    me_chip = 2 * mx + my
    c_arr = jnp.reshape(mc, (1,)).astype(jnp.int32)
    me_arr = jnp.reshape(me_chip, (1,)).astype(jnp.int32)
    sharded = ('w_in', 'w_ffn_in') + tuple(name for name, _, _ in _FLATB_PIECES)

    local = _pack_shards({name: ws[name][0] for name in sharded}, conv_w[0])
    (gathered_in,) = _gather_shards(local[:1], "gather_w_in")
    w_in_full, conv_full = _unpack_w_in(gathered_in, local[0], me_chip)
    small = {n: ws[n][0] for n in _SMALL_NAMES}
    small['conv_w'] = conv_full
    sp = _prep_small(small)
    cosb, sinb = _rope_tables(positions[0])
    exch = _Exchange(local[1:], me_chip, c_arr)

    loss_lanes, dx, g = _local_step(x[0], p[0, 0], cosb, sinb, loss_target[0], _prep_w_in(w_in_full), sp, exch)
    gw = _unprep_grads_late(g)
    loss = lax.psum(jnp.sum(loss_lanes), ("x", "y", "c"))

    parts = [exch.part_in] + exch.parts
    arrived = exch.arrived_in + exch.arrived
    mine = [_chip_add(p_, r_, me_arr, tr, "chip_add_%d" % i) for i, (p_, r_, tr) in enumerate(zip(parts, arrived, _ADD_TILES))]
    reduced = _unpack_reduced(mine, _reduce_pair_share(mine, "reduce_pair_share"), mc)
    tot = _small_allreduce(_pack_small(gw))
    gred = {name: reduced[name][None] for name in sharded}
    for i, n in enumerate(_SMALL_NAMES):
        gred[n] = tot[i * _SMALL_GROUP, :ws[n].shape[1]][None]
    conv_tot = tot[_CONV_SMALL_ROW:_CONV_SMALL_ROW + _CONV_SMALL_ROWS].reshape(DN_CONV, QKV_W)
    gred['conv_w'] = lax.dynamic_slice_in_dim(conv_tot, (2 * mx + my) * _CONV_SHARD, _CONV_SHARD, axis=1)[None]

    deltas, new_m, new_v = {}, {}, {}
    for n in _WEIGHT_NAMES:
        if n == 'w_in':
            tr_ = lambda a: jnp.transpose(a, (0, 2, 1))
            g_t = tr_(gred[n].reshape(ws[n].shape))
            outs = _adamw(tr_(ws[n]), g_t, tr_(ms[n]), tr_(vs[n]), "adamw_" + n)
            gred[n] = tr_(g_t)
            deltas[n], new_m[n], new_v[n] = (tr_(o) for o in outs)
            continue
        gred[n] = gred[n].reshape(ws[n].shape)
        deltas[n], new_m[n], new_v[n] = _adamw(ws[n], gred[n], ms[n], vs[n], "adamw_" + n)
    return (loss, dx[None], *[gred[n] for n in _WEIGHT_NAMES], *[deltas[n] for n in _WEIGHT_NAMES],
            *[new_m[n] for n in _WEIGHT_NAMES], *[new_v[n] for n in _WEIGHT_NAMES])
```

```python
import functools
import math

import jax
import jax.numpy as jnp
from jax import lax
from jax.experimental import pallas as pl
from jax.experimental.pallas import tpu as pltpu

F32 = jnp.float32
_CDT = jnp.bfloat16
_HI = lax.Precision.HIGHEST
_MESH = pl.DeviceIdType.MESH

D_MODEL = 1024
PLE_DIM = 256
HEADS = 8
DN_DK = 128
DN_CHUNK = 64
DN_CONV = 4
QKV_W = 3 * HEADS * DN_DK
Q_LORA = 384
KV_LORA = 256
NOPE = 128
ROPE = 64
ROPE_PAD = 128
FFN_HIDDEN = 2816
D_IN = 6864
ROPE_BASE = 10000.0
ALPHA = 2.0 ** 0.25
ATT_SCALE = (NOPE + ROPE) ** -0.5
NEG_BIG = -1e30
ADAM_LR, ADAM_B1, ADAM_B2, ADAM_EPS, ADAM_WD, ADAM_STEP = 0.001, 0.9, 0.999, 1e-08, 0.01, 10

LANE = 128
VMEM_LIMIT = 56 * 1024 * 1024
MM_VMEM_BUDGET = 40 * 1024 * 1024
N_SHARD = 4
FLAT_W = 1024
SMALL_ROWS = 88


def _tile(dim, cap):
    if dim <= cap:
        return dim
    t = (cap // LANE) * LANE
    while t >= LANE:
        if dim % t == 0:
            return t
        t -= LANE
    return dim


def _cparams(sem):
    return pltpu.CompilerParams(dimension_semantics=sem, vmem_limit_bytes=VMEM_LIMIT)


def _mm(a, b, *, name, ta=False, tb=False, add=None, add_scale=1.0, out_dtype=F32, heads=None,
        a_head=None, b_head=None, out_head=None, dims=None, tm=1408, tn=1408):
    m, n, k = dims
    tm, tn = _tile(m, tm), _tile(n, tn)
    sa, sb, so = a.dtype.itemsize, b.dtype.itemsize, jnp.dtype(out_dtype).itemsize

    def vmem_need(tk_):
        acc = tm * tn * 4 if tk_ < k else 0
        extra = 2 * tm * tn * 4 if add is not None else 0
        return 2 * (tm * tk_ * sa + tk_ * tn * sb) + 2 * tm * tn * so + acc + extra

    tk = k
    while vmem_need(tk) > MM_VMEM_BUDGET and tk > LANE:
        smaller = _tile(k, tk - LANE)
        if smaller >= tk:
            break
        tk = smaller
    nk = k // tk
    hgrid = () if heads is None else (heads,)
    off = len(hgrid)

    def spec(rows, cols, rtile, ctile, rsel, csel, layout):
        def idx(*g):
            h = g[0] if off else 0
            ri, ci = g[off + rsel], g[off + csel]
            if layout == 'lead':
                return (h, ri, ci)
            if layout == 'col':
                return (ri, h * (cols // ctile) + ci)
            return (ri, ci)
        if layout == 'lead':
            return pl.BlockSpec((None, rtile, ctile), idx)
        return pl.BlockSpec((rtile, ctile), idx)

    a_spec = spec(k, m, tk, tm, 2, 0, a_head) if ta else spec(m, k, tm, tk, 0, 2, a_head)
    b_spec = spec(n, k, tn, tk, 1, 2, b_head) if tb else spec(k, n, tk, tn, 2, 1, b_head)
    o_spec = spec(m, n, tm, tn, 0, 1, out_head)
    in_specs = [a_spec, b_spec]
    args = [a, b]
    if add is not None:
        in_specs.append(spec(m, n, tm, tn, 0, 1, out_head))
        args.append(add)
    dn = (((0 if ta else 1,), (1 if tb else 0,)), ((), ()))

    def body(*refs):
        a_ref, b_ref = refs[0], refs[1]
        prod = lax.dot_general(a_ref[...].astype(_CDT), b_ref[...].astype(_CDT), dn, preferred_element_type=F32)
        if nk == 1:
            o_ref = refs[-1]
            if add is not None:
                prod = prod + refs[2][...].astype(F32) * add_scale
            o_ref[...] = prod.astype(out_dtype)
            return
        o_ref, acc_ref = refs[-2], refs[-1]
        kk = pl.program_id(off + 2)

        @pl.when(kk == 0)
        def _():
            if add is not None:
                acc_ref[...] = refs[2][...].astype(F32) * add_scale
            else:
                acc_ref[...] = jnp.zeros_like(acc_ref)

        acc_ref[...] += prod

        @pl.when(kk == nk - 1)
        def _():
            o_ref[...] = acc_ref[...].astype(out_dtype)

    if out_head == 'lead':
        oshape = (heads, m, n)
    elif out_head == 'col':
        oshape = (m, heads * n)
    else:
        oshape = (m, n)
    sem = ("parallel",) * (off + 2) + ("arbitrary",)
    return pl.pallas_call(
        body, name=name, grid=hgrid + (m // tm, n // tn, nk), in_specs=in_specs, out_specs=o_spec,
        out_shape=jax.ShapeDtypeStruct(oshape, out_dtype),
        scratch_shapes=[pltpu.VMEM((tm, tn), F32)] if nk > 1 else [],
        compiler_params=_cparams(sem))(*args)


def _mm2(a, b, **kw):
    ta, tb = kw.get('ta', False), kw.get('tb', False)
    m = a.shape[1] if ta else a.shape[0]
    k = a.shape[0] if ta else a.shape[1]
    n = b.shape[0] if tb else b.shape[1]
    return _mm(a, b, dims=(m, n, k), **kw)


def _rowwise(fn, rows, bcast, outs, reds=(), *, name, tm=256, heads=None):
    t = rows[0][0].shape[0]
    tm = min(tm, t)
    hn = 1 if heads is None else heads
    in_specs, args = [], []
    for arr, width, base, per_head in rows:
        in_specs.append(pl.BlockSpec((tm, width), functools.partial(
            lambda i, h, base, per_head: (i, base + (h if per_head else 0)), base=base, per_head=per_head)))
        args.append(arr)
    for arr in bcast:
        in_specs.append(pl.BlockSpec(arr.shape, lambda i, h: (0, 0)))
        args.append(arr)
    out_specs, out_shape = [], []
    for total, width, per_head, dt in outs:
        out_specs.append(pl.BlockSpec((tm, width), functools.partial(
            lambda i, h, per_head: (i, h if per_head else 0), per_head=per_head)))
        out_shape.append(jax.ShapeDtypeStruct((t, total), dt))
    for shp in reds:
        out_specs.append(pl.BlockSpec(shp, lambda i, h: (0, 0)))
        out_shape.append(jax.ShapeDtypeStruct(shp, F32))
    n_in, n_out, n_red = len(args), len(outs), len(reds)

    def body(*refs):
        i, h = pl.program_id(0), pl.program_id(1)
        vals = fn(h, *[r[...] for r in refs[:n_in]])
        for r, v in zip(refs[n_in:n_in + n_out], vals[:n_out]):
            r[...] = v.astype(r.dtype)
        if n_red:
            @pl.when((i == 0) & (h == 0))
            def _():
                for r in refs[n_in + n_out:]:
                    r[...] = jnp.zeros_like(r)
            for r, v in zip(refs[n_in + n_out:], vals[n_out:]):
                r[...] += v

    sem = ("arbitrary", "arbitrary") if n_red else ("parallel", "parallel")
    res = pl.pallas_call(body, name=name, grid=(t // tm, hn), in_specs=in_specs, out_specs=out_specs,
                         out_shape=out_shape, compiler_params=_cparams(sem))(*args)
    return tuple(res)


def _sigmoid(x):
    return 1.0 / (1.0 + jnp.exp(-x))


def _silu(x):
    return x * _sigmoid(x)


def _softplus(x):
    return jnp.maximum(x, 0.0) + jnp.log(1.0 + jnp.exp(-jnp.abs(x)))


def _layer_norm(t, g, b):
    mu = jnp.mean(t, axis=-1, keepdims=True)
    d = t - mu
    var = jnp.mean(d * d, axis=-1, keepdims=True)
    return d * lax.rsqrt(var + 1e-5) * g + b


def _rms_norm(t, w):
    return t * lax.rsqrt(jnp.mean(t * t, axis=-1, keepdims=True) + 1e-6) * w


def _swap_rope_halves(t):
    lane = lax.broadcasted_iota(jnp.int32, t.shape, 1) % ROPE_PAD
    n = t.shape[1]
    up = pltpu.roll(t, n - ROPE // 2, axis=1)
    dn = pltpu.roll(t, ROPE // 2, axis=1)
    return jnp.where(lane < ROPE // 2, up, jnp.where(lane < ROPE, dn, 0.0))


def _rope(t, cosb, sinb):
    reps = t.shape[1] // ROPE_PAD
    c = jnp.tile(cosb, (1, reps)) if reps > 1 else cosb
    s = jnp.tile(sinb, (1, reps)) if reps > 1 else sinb
    return t * c + _swap_rope_halves(t) * s


def _rope_bwd(d, cosb, sinb):
    reps = d.shape[1] // ROPE_PAD
    c = jnp.tile(cosb, (1, reps)) if reps > 1 else cosb
    s = jnp.tile(sinb, (1, reps)) if reps > 1 else sinb
    return d * c + _swap_rope_halves(d * s)


_CONV_ROWS = 256
_CONV_COLS = 256


def _conv_window(ref, r0, lo, hi, t):
    parts = []
    start, stop = r0 - lo, r0 + _CONV_ROWS + hi
    if start < 0:
        parts.append(jnp.zeros((-start, ref.shape[1]), F32))
        start = 0
    tail = max(stop - t, 0)
    parts.append(ref[start:stop - tail, :].astype(F32))
    if tail:
        parts.append(jnp.zeros((tail, ref.shape[1]), F32))
    return parts[0] if len(parts) == 1 else jnp.concatenate(parts, axis=0)


def _conv_taps(win, w_ref, n_out):
    acc = win[8:8 + n_out] * w_ref[DN_CONV - 1:DN_CONV, :]
    for i in range(DN_CONV - 1):
        acc = acc + pltpu.roll(win, DN_CONV - 1 - i, axis=0)[8:8 + n_out] * w_ref[i:i + 1, :]
    return acc


def _conv_silu(x, w):
    t, ch = x.shape

    def body(x_ref, w_ref, o_ref):
        for r in range(t // _CONV_ROWS):
            r0 = r * _CONV_ROWS
            c = _conv_taps(_conv_window(x_ref, r0, 8, 0, t), w_ref, _CONV_ROWS)
            o_ref[r0:r0 + _CONV_ROWS, :] = _silu(c)

    return pl.pallas_call(
        body, name="conv_silu", grid=(ch // _CONV_COLS,),
        in_specs=[pl.BlockSpec((t, _CONV_COLS), lambda j: (0, j)), pl.BlockSpec((DN_CONV, _CONV_COLS), lambda j: (0, j))],
        out_specs=pl.BlockSpec((t, _CONV_COLS), lambda j: (0, j)),
        out_shape=jax.ShapeDtypeStruct((t, ch), F32), compiler_params=_cparams(("parallel",)))(x, w)


def _conv_silu_bwd(x, w, dys):
    t, ch = x.shape
    per = ch // len(dys) // _CONV_COLS

    def body(x_ref, w_ref, *rest):
        dy_refs, (dx_ref, dw_ref) = rest[:len(dys)], rest[len(dys):]
        sec = pl.program_id(0) // per
        dws = [jnp.zeros((1, _CONV_COLS), F32) for _ in range(DN_CONV)]
        for r in range(t // _CONV_ROWS):
            r0 = r * _CONV_ROWS
            n_ext = _CONV_ROWS + 8
            xw = _conv_window(x_ref, r0, 8, 8, t)
            c = _conv_taps(xw, w_ref, n_ext)
            sg = _sigmoid(c)
            dy = _conv_window(dy_refs[-1], r0, 0, 8, t)
            for k in range(len(dys) - 2, -1, -1):
                dy = jnp.where(sec == k, _conv_window(dy_refs[k], r0, 0, 8, t), dy)
            ds = dy * (sg * (1.0 + c * (1.0 - sg)))
            x0 = xw[8:8 + _CONV_ROWS]
            dx = jnp.zeros((_CONV_ROWS, _CONV_COLS), F32)
            for i in range(DN_CONV):
                sh = DN_CONV - 1 - i
                ds_up = (ds if sh == 0 else pltpu.roll(ds, n_ext - sh, axis=0))[:_CONV_ROWS]
                dx = dx + ds_up * w_ref[i:i + 1, :]
                dws[i] = dws[i] + jnp.sum(x0 * ds_up, axis=0, keepdims=True)
            dx_ref[r0:r0 + _CONV_ROWS, :] = dx.astype(dx_ref.dtype)
        for i in range(DN_CONV):
            dw_ref[i:i + 1, :] = dws[i]

    blk = pl.BlockSpec((t, _CONV_COLS), lambda j: (0, j))
    wblk = pl.BlockSpec((DN_CONV, _CONV_COLS), lambda j: (0, j))
    dy_specs = [pl.BlockSpec((t, _CONV_COLS), functools.partial(lambda j, k: (0, jnp.clip(j - k * per, 0, per - 1)), k=k))
                for k in range(len(dys))]
    return pl.pallas_call(
        body, name="conv_silu_bwd", grid=(ch // _CONV_COLS,), in_specs=[blk, wblk] + dy_specs, out_specs=[blk, wblk],
        out_shape=[jax.ShapeDtypeStruct((t, ch), _CDT), jax.ShapeDtypeStruct((DN_CONV, ch), F32)],
        compiler_params=_cparams(("arbitrary",)))(x, w, *dys)


_PA_ROWS = 512


def _bmm(a, b, spec, exact=False):
    if exact:
        return jnp.einsum(spec, a, b, precision=_HI, preferred_element_type=F32)
    return jnp.einsum(spec, a.astype(_CDT), b.astype(_CDT), preferred_element_type=F32)


def _split16(a):
    hi = a.astype(jnp.bfloat16)
    return hi, (a - hi.astype(F32)).astype(jnp.bfloat16)


def _bmm3(a, b, spec):
    ah, al = _split16(a)
    bh, bl = _split16(b)
    e = lambda p, q: jnp.einsum(spec, p, q, preferred_element_type=F32)
    return e(ah, bh) + (e(ah, bl) + e(al, bh))


def _split3(b):
    b0 = b.astype(jnp.bfloat16)
    r1 = b - b0.astype(F32)
    b1 = r1.astype(jnp.bfloat16)
    return b0, b1, (r1 - b1.astype(F32)).astype(jnp.bfloat16)


@functools.partial(jax.custom_vjp, nondiff_argnums=(2, 3))
def _select_mm(sel, b, spec, spec_t):
    return sum(jnp.einsum(spec, sel, t, preferred_element_type=F32) for t in _split3(b))


def _select_mm_fwd(sel, b, spec, spec_t):
    return _select_mm(sel, b, spec, spec_t), sel


def _select_mm_bwd(spec, spec_t, sel, ct):
    return jnp.zeros_like(sel), sum(jnp.einsum(spec_t, sel, t, preferred_element_type=F32) for t in _split3(ct))


_select_mm.defvjp(_select_mm_fwd, _select_mm_bwd)


def _tri_inverse(l_mat, eye):
    pw = -l_mat
    t_inv = eye + pw
    for _ in range(5):
        pw = _bmm3(pw, pw, 'bij,bjk->bik')
        t_inv = t_inv + _bmm3(t_inv, pw, 'bij,bjk->bik')
    return t_inv


@jax.custom_vjp
def _tri_inverse_saved(l_mat, t_saved):
    return t_saved


def _tri_inverse_saved_fwd(l_mat, t_saved):
    return t_saved, t_saved


def _tri_inverse_saved_bwd(t_saved, dt):
    left = _bmm3(t_saved, dt, 'bji,bjk->bik')
    return -_bmm3(left, t_saved, 'bij,bkj->bik'), jnp.zeros_like(t_saved)


_tri_inverse_saved.defvjp(_tri_inverse_saved_fwd, _tri_inverse_saved_bwd)


def _phase_a(h, q, k, v, ba, alog, dtb, t_saved=None):
    r = q.shape[0]
    nb = r // DN_CHUNK
    c = DN_CHUNK
    lane = lax.broadcasted_iota(jnp.int32, (1, LANE), 1)
    selb = (lane == h).astype(F32)
    sela = (lane == h + HEADS).astype(F32)
    b_raw = jnp.sum(ba * selb, axis=1, keepdims=True)
    a_raw = jnp.sum(ba * sela, axis=1, keepdims=True)
    al = jnp.sum(alog * selb, axis=1, keepdims=True)
    dt = jnp.sum(dtb * selb, axis=1, keepdims=True)
    beta = jnp.broadcast_to(_sigmoid(b_raw), (r, LANE))
    g = jnp.broadcast_to(-jnp.exp(al) * _softplus(a_raw + dt), (r, LANE))
    qn = q * lax.rsqrt(jnp.sum(q * q, -1, keepdims=True) + 1e-6) * (DN_DK ** -0.5)
    kn = k * lax.rsqrt(jnp.sum(k * k, -1, keepdims=True) + 1e-6)
    q3, k3, v3 = qn.reshape(nb, c, LANE), kn.reshape(nb, c, LANE), v.reshape(nb, c, LANE)
    b3, g3 = beta.reshape(nb, c, LANE), g.reshape(nb, c, LANE)
    ri = lax.broadcasted_iota(jnp.int32, (nb, c, c), 1)
    ci = lax.broadcasted_iota(jnp.int32, (nb, c, c), 2)
    tril, strict = ri >= ci, ri > ci
    gc = _select_mm(tril.astype(jnp.bfloat16), g3, 'bij,bjd->bid', 'bij,bid->bjd')
    onehot = (lax.broadcasted_iota(jnp.int32, (nb, c, LANE), 2) == 0).astype(jnp.bfloat16)
    g_row = _select_mm(onehot, gc, 'bid,bjd->bij', 'bid,bij->bjd')
    diff = gc[:, :, :c] - g_row
    decay = jnp.where(tril, jnp.exp(jnp.where(tril, diff, 0.0)), 0.0)
    kb = k3 * b3
    l_mat = jnp.where(strict, _bmm(kb, k3, 'bid,bjd->bij') * decay, 0.0)
    if t_saved is None:
        t_inv = _tri_inverse(l_mat, (ri == ci).astype(F32))
    else:
        t_inv = _tri_inverse_saved(l_mat, t_saved.reshape(nb, c, c))
    eg = jnp.exp(gc)
    u = _bmm(t_inv, v3 * b3, 'bij,bje->bie')
    w = _bmm(t_inv, kb * eg, 'bij,bje->bie')
    intra = jnp.where(tril, _bmm(q3, k3, 'bid,bjd->bij') * decay, 0.0)
    qd = q3 * eg
    gl = jnp.sum(g3, axis=1, keepdims=True)
    kt = k3 * jnp.exp(gl - gc)
    outs = (u.reshape(r, LANE), w.reshape(r, LANE), qd.reshape(r, LANE), kt.reshape(r, LANE),
            intra.reshape(r, c), gl.reshape(nb, LANE))
    if t_saved is not None:
        return outs
    qd2 = qd - _bmm(intra, w, 'bij,bjd->bid')
    au = _bmm(intra, u, 'bij,bje->bie')
    return outs + (t_inv.reshape(r, c), qd2.reshape(r, LANE), au.reshape(r, LANE))


def _pa_specs(t):
    rr = min(_PA_ROWS, t)
    nb = rr // DN_CHUNK
    qkv = [pl.BlockSpec((rr, LANE), functools.partial(lambda i, h, o: (i, o + h), o=o)) for o in (0, HEADS, 2 * HEADS)]
    ba = pl.BlockSpec((rr, LANE), lambda i, h: (i, 3))
    vec = pl.BlockSpec((1, LANE), lambda i, h: (0, 0))
    row = pl.BlockSpec((rr, LANE), lambda i, h: (i, h))
    intra = pl.BlockSpec((None, rr, DN_CHUNK), lambda i, h: (h, i, 0))
    gl = pl.BlockSpec((nb, LANE), lambda i, h: (i, h))
    return rr, qkv, ba, vec, row, intra, gl


def _grid_ends(grid):
    first = lambda: functools.reduce(jnp.logical_and, [pl.program_id(a) == 0 for a in range(len(grid))])
    last = lambda: functools.reduce(jnp.logical_and, [pl.program_id(a) == n - 1 for a, n in enumerate(grid)])
    return first, last


def _delta_local(qkv_act, pm, alog, dtb, rider=None):
    t = qkv_act.shape[0]
    rr, qkv, ba, vec, row, intra, gl = _pa_specs(t)

    def body(q, k, v, b, al, dt, *outs):
        vals = _phase_a(pl.program_id(1), q[...], k[...], v[...], b[...], al[...], dt[...])
        for o, val in zip(outs, vals):
            o[...] = val

    wide = jax.ShapeDtypeStruct((t, HEADS * LANE), F32)
    sq = jax.ShapeDtypeStruct((HEADS, t, DN_CHUNK), F32)
    grid = (t // rr, HEADS)
    return _carried_call(
        body, rider, *_grid_ends(grid), name="delta_local", grid=grid, in_specs=qkv + [ba, vec, vec],
        out_specs=[row] * 4 + [intra, gl, intra, row, row],
        out_shape=[wide] * 4 + [sq, jax.ShapeDtypeStruct((t // DN_CHUNK, HEADS * LANE), F32), sq, wide, wide],
        scratch_shapes=[], sem=("arbitrary", "arbitrary"), args=(qkv_act, qkv_act, qkv_act, pm, alog, dtb))


def _delta_local_bwd(qkv_act, pm, alog, dtb, t_inv, du, dw, dqd, dkt, dintra, dgl, rider=None):
    t = qkv_act.shape[0]
    rr, qkv, ba, vec, row, intra, gl = _pa_specs(t)

    def body(q, k, v, b, al, dt, ti, du_r, dw_r, dqd_r, dkt_r, di_r, dgl_r, dq_o, dk_o, dv_o, dba_o, dal_o, ddt_o):
        i, h = pl.program_id(0), pl.program_id(1)
        t_saved = ti[...]
        _, vjp = jax.vjp(lambda *a: _phase_a(h, *a, t_saved=t_saved), q[...], k[...], v[...], b[...], al[...], dt[...])
        dq, dk, dv, dba, dal, ddt = vjp((du_r[...], dw_r[...], dqd_r[...], dkt_r[...], di_r[...], dgl_r[...]))
        dq_o[...], dk_o[...], dv_o[...] = dq, dk, dv

        @pl.when(h == 0)
        def _():
            dba_o[...] = jnp.zeros_like(dba_o)

        @pl.when((h == 0) & (i == 0))
        def _():
            dal_o[...] = jnp.zeros_like(dal_o)
            ddt_o[...] = jnp.zeros_like(ddt_o)

        dba_o[...] += dba
        dal_o[...] += dal
        ddt_o[...] += ddt

    wide = jax.ShapeDtypeStruct((t, HEADS * LANE), F32)
    vshape = jax.ShapeDtypeStruct((1, LANE), F32)
    grid = (t // rr, HEADS)
    return _carried_call(
        body, rider, *_grid_ends(grid), name="delta_local_bwd", grid=grid,
        in_specs=qkv + [ba, vec, vec, intra] + [row] * 4 + [intra, gl],
        out_specs=[row] * 3 + [pl.BlockSpec((rr, LANE), lambda i, h: (i, 0)), vec, vec],
        out_shape=[wide] * 3 + [jax.ShapeDtypeStruct((t, LANE), F32), vshape, vshape],
        scratch_shapes=[], sem=("arbitrary", "arbitrary"),
        args=(qkv_act, qkv_act, qkv_act, pm, alog, dtb, t_inv, du, dw, dqd, dkt, dintra, dgl))


_SCAN_ROWS = 512


def _dot(a, b, dn):
    return lax.dot_general(a.astype(_CDT), b.astype(_CDT), (dn, ((), ())), preferred_element_type=F32)


_NN = ((1,), (0,))
_NT = ((1,), (1,))
_TN = ((0,), (0,))


def _delta_scan(u, w, qd, kt, au, gl, rider=None):
    t = u.shape[0]
    rr = min(_SCAN_ROWS, t)
    nc = rr // DN_CHUNK

    def body(u_ref, w_ref, qd_ref, kt_ref, au_ref, gl_ref, o_ref, sall_ref, s_scr):
        @pl.when(pl.program_id(0) == 0)
        def _():
            s_scr[...] = jnp.zeros_like(s_scr)

        def chunk(c, carry):
            r0 = pl.multiple_of(c * DN_CHUNK, DN_CHUNK)
            rows = pl.ds(r0, DN_CHUNK)
            e = jnp.exp(gl_ref[pl.ds(c, 1), :])
            states = [s_scr[h] for h in range(HEADS)]
            u_c, w_c, qd_c, kt_c, au_c = u_ref[rows, :], w_ref[rows, :], qd_ref[rows, :], kt_ref[rows, :], au_ref[rows, :]
            o_new, s_new = [], []
            for h in range(HEADS):
                cs = slice(h * LANE, (h + 1) * LANE)
                s = states[h]
                both = _dot(jnp.concatenate([w_c[:, cs], qd_c[:, cs]], axis=0), s, _NN)
                v_new = u_c[:, cs] - both[:DN_CHUNK]
                o_new.append(both[DN_CHUNK:] + au_c[:, cs])
                s_new.append(s * e[:, cs] + _dot(kt_c[:, cs], v_new, _TN))
            o_ref[rows, :] = jnp.concatenate(o_new, axis=1)
            for h in range(HEADS):
                sall_ref[c, h] = states[h]
                s_scr[h] = s_new[h]
            return carry

        lax.fori_loop(0, nc, chunk, 0)

    row = pl.BlockSpec((rr, HEADS * LANE), lambda i: (i, 0))
    grid = (t // rr,)
    return _carried_call(
        body, rider, *_grid_ends(grid), name="delta_scan", grid=grid,
        in_specs=[row] * 5 + [pl.BlockSpec((nc, HEADS * LANE), lambda i: (i, 0))],
        out_specs=[row, pl.BlockSpec((nc, HEADS, LANE, LANE), lambda i: (i, 0, 0, 0))],
        out_shape=[jax.ShapeDtypeStruct((t, HEADS * LANE), F32),
                   jax.ShapeDtypeStruct((t // DN_CHUNK, HEADS, LANE, LANE), F32)],
        scratch_shapes=[pltpu.VMEM((HEADS, LANE, LANE), F32)], sem=("arbitrary",), args=(u, w, qd, kt, au, gl))


def _delta_scan_bwd(u, w, qd, kt, intra, gl, sall, do, rider=None):
    t = u.shape[0]
    rr = min(_SCAN_ROWS, t)
    nc = rr // DN_CHUNK
    ng = t // rr

    def body(u_ref, w_ref, qd_ref, kt_ref, a_ref, gl_ref, sall_ref, do_ref,
             du_ref, dw_ref, dqd_ref, dkt_ref, da_ref, dgl_ref, ds_scr):
        @pl.when(pl.program_id(0) == 0)
        def _():
            ds_scr[...] = jnp.zeros_like(ds_scr)

        def chunk(cc, carry):
            c = nc - 1 - cc
            r0 = pl.multiple_of(c * DN_CHUNK, DN_CHUNK)
            rows = pl.ds(r0, DN_CHUNK)
            e = jnp.exp(gl_ref[pl.ds(c, 1), :])
            states = [sall_ref[c, h] for h in range(HEADS)]
            ds_outs = [ds_scr[h] for h in range(HEADS)]
            u_a, w_a, kt_a, qd_a, do_a = u_ref[rows, :], w_ref[rows, :], kt_ref[rows, :], qd_ref[rows, :], do_ref[rows, :]
            a_a = [a_ref[h, rows, :] for h in range(HEADS)]
            da, dqd, dkt, du, dw, dgl, ds_new = [], [], [], [], [], [], []
            for h in range(HEADS):
                cs = slice(h * LANE, (h + 1) * LANE)
                s, ds_out = states[h], ds_outs[h]
                w_c, kt_c, qd_c, do_c = w_a[:, cs], kt_a[:, cs], qd_a[:, cs], do_a[:, cs]
                v_new = u_a[:, cs] - _dot(w_c, s, _NN)
                dv_new = _dot(a_a[h], do_c, _TN) + _dot(kt_c, ds_out, _NN)
                cots = jnp.concatenate([do_c, dv_new], axis=0)
                both = _dot(cots, s, _NT)
                dqd.append(both[:DN_CHUNK])
                dw.append(-both[DN_CHUNK:])
                da.append(_dot(do_c, v_new, _NT))
                dkt.append(_dot(v_new, ds_out, _NT))
                du.append(dv_new)
                eh = e[:, cs]
                dgl.append(jnp.broadcast_to(jnp.sum(ds_out * s, axis=0, keepdims=True) * eh, (8, LANE)))
                ds_new.append(ds_out * eh + _dot(jnp.concatenate([qd_c, -w_c], axis=0), cots, _TN))
            cat = lambda parts: jnp.concatenate(parts, axis=1)
            dqd_ref[rows, :], dkt_ref[rows, :], du_ref[rows, :], dw_ref[rows, :] = cat(dqd), cat(dkt), cat(du), cat(dw)
            dgl_ref[pl.ds(pl.multiple_of(c * 8, 8), 8), :] = cat(dgl)
            for h in range(HEADS):
                da_ref[h, rows, :] = da[h]
                ds_scr[h] = ds_new[h]
            return carry

        lax.fori_loop(0, nc, chunk, 0)

    rev = lambda i: (ng - 1 - i, 0)
    row = pl.BlockSpec((rr, HEADS * LANE), rev)
    a_spec = pl.BlockSpec((HEADS, rr, DN_CHUNK), lambda i: (0, ng - 1 - i, 0))
    gl_spec = pl.BlockSpec((nc, HEADS * LANE), rev)
    wide = jax.ShapeDtypeStruct((t, HEADS * LANE), F32)
    outs, carried = _carried_call(
        body, rider, *_grid_ends((ng,)), name="delta_scan_bwd", grid=(ng,),
        in_specs=[row] * 4 + [a_spec, gl_spec, pl.BlockSpec((nc, HEADS, LANE, LANE), lambda i: (ng - 1 - i, 0, 0, 0)), row],
        out_specs=[row] * 4 + [a_spec, pl.BlockSpec((nc * 8, HEADS * LANE), rev)],
        out_shape=[wide] * 4 + [jax.ShapeDtypeStruct((HEADS, t, DN_CHUNK), F32),
                                jax.ShapeDtypeStruct((t // DN_CHUNK * 8, HEADS * LANE), F32)],
        scratch_shapes=[pltpu.VMEM((HEADS, LANE, LANE), F32)], sem=("arbitrary",), args=(u, w, qd, kt, intra, gl, sall, do))
    return tuple(outs[:5]) + (outs[5].reshape(t // DN_CHUNK, 8, HEADS * LANE)[:, 0, :],), carried


_ATT_TILE = 512


def _kv_rows(j, tk):
    return pl.ds(pl.multiple_of(j * tk, tk), tk)


def _att_scores(ql, qr, ckv_ref, kr_ref, j, tk):
    ks = _kv_rows(j, tk)
    return (_dot(ql, ckv_ref[ks, :], _NT) + _dot(qr, kr_ref[ks, :], _NT)) * ATT_SCALE


def _diag_mask(s):
    qi = lax.broadcasted_iota(jnp.int32, s.shape, 0)
    ki = lax.broadcasted_iota(jnp.int32, s.shape, 1)
    return jnp.where(ki <= qi, s, NEG_BIG)


def _attention(ql, qr_pre, cosb, sinb, ckv, kr):
    t = ckv.shape[0]
    tq = min(_ATT_TILE, t)

    nl = tq // LANE

    def lane_fold(v, op):
        out = v[:, :LANE]
        for k in range(1, nl):
            out = op(out, v[:, k * LANE:(k + 1) * LANE])
        return out

    def body(ql_ref, qr_ref, cos_ref, sin_ref, ckv_ref, kr_ref, o_ref, lse_ref, qrope_ref, s_all, m_lanes, l_lanes, acc_scr):
        qi = pl.program_id(1)
        q_lat = ql_ref[...]
        q_rope = _rope(qr_ref[...], cos_ref[...], sin_ref[...]).astype(qrope_ref.dtype)
        qrope_ref[...] = q_rope
        m_lanes[...] = jnp.full_like(m_lanes, NEG_BIG)

        def scores(j, masked):
            s = _att_scores(q_lat, q_rope, ckv_ref, kr_ref, j, tq)
            if masked:
                s = _diag_mask(s)
            s_all[j] = s
            m_lanes[...] = jnp.maximum(m_lanes[...], lane_fold(s, jnp.maximum))

        def scores_body(j, carry):
            scores(j, False)
            return carry

        lax.fori_loop(0, qi, scores_body, 0)
        scores(qi, True)
        m = jnp.max(m_lanes[...], axis=-1, keepdims=True)
        mb = jnp.broadcast_to(m, (tq, LANE))
        l_lanes[...] = jnp.zeros_like(l_lanes)
        acc_scr[...] = jnp.zeros_like(acc_scr)

        def weigh(j, carry):
            s = s_all[j]
            p = jnp.concatenate([jnp.exp(s[:, k * LANE:(k + 1) * LANE] - mb) for k in range(nl)], axis=1)
            l_lanes[...] += lane_fold(p, jnp.add)
            acc_scr[...] += _dot(p, ckv_ref[_kv_rows(j, tq), :], _NN)
            return carry

        lax.fori_loop(0, qi + 1, weigh, 0)
        l = jnp.sum(l_lanes[...], axis=-1, keepdims=True)
        o_ref[...] = acc_scr[...] / l
        lse_ref[...] = m + jnp.log(l)

    return pl.pallas_call(
        body, name="attention", grid=(HEADS, t // tq),
        in_specs=[pl.BlockSpec((None, tq, KV_LORA), lambda h, i: (h, i, 0)),
                  pl.BlockSpec((tq, ROPE_PAD), lambda h, i: (i, h)),
                  pl.BlockSpec((tq, ROPE_PAD), lambda h, i: (i, 0)), pl.BlockSpec((tq, ROPE_PAD), lambda h, i: (i, 0)),
                  pl.BlockSpec((t, KV_LORA), lambda h, i: (0, 0)),
                  pl.BlockSpec((t, ROPE_PAD), lambda h, i: (0, 0))],
        out_specs=[pl.BlockSpec((None, tq, KV_LORA), lambda h, i: (h, i, 0)),
                   pl.BlockSpec((None, tq, 1), lambda h, i: (h, i, 0)),
                   pl.BlockSpec((tq, ROPE_PAD), lambda h, i: (i, h))],
        out_shape=[jax.ShapeDtypeStruct((HEADS, t, KV_LORA), F32), jax.ShapeDtypeStruct((HEADS, t, 1), F32),
                   jax.ShapeDtypeStruct((t, HEADS * ROPE_PAD), _CDT)],
        scratch_shapes=[pltpu.VMEM((t // tq, tq, tq), F32), pltpu.VMEM((tq, LANE), F32), pltpu.VMEM((tq, LANE), F32),
                        pltpu.VMEM((tq, KV_LORA), F32)],
        compiler_params=_cparams(("parallel", "parallel")))(ql, qr_pre, cosb, sinb, ckv, kr)


def _attention_bwd(ql, qr, cosb, sinb, ckv, kr, out, lse, dout):
    t = ckv.shape[0]
    tq = min(_ATT_TILE, t)

    def body(ql_ref, qr_ref, cos_ref, sin_ref, ckv_ref, kr_ref, o_ref, lse_ref, do_ref, dql_ref, dqr_ref, dckv_ref,
             dkr_ref, dql_scr, dqr_scr):
        h, qi = pl.program_id(0), pl.program_id(1)

        @pl.when((h == 0) & (qi == 0))
        def _():
            dckv_ref[...] = jnp.zeros_like(dckv_ref)
            dkr_ref[...] = jnp.zeros_like(dkr_ref)

        q_lat, q_rope = ql_ref[...], qr_ref[...]
        d_o = do_ref[...].astype(_CDT)
        lse_v = lse_ref[...]
        dsum = jnp.sum(do_ref[...] * o_ref[...], axis=-1, keepdims=True)
        dql_scr[...] = jnp.zeros_like(dql_scr)
        dqr_scr[...] = jnp.zeros_like(dqr_scr)

        def step(j, masked):
            ks = _kv_rows(j, tq)
            s = _att_scores(q_lat, q_rope, ckv_ref, kr_ref, j, tq)
            if masked:
                s = _diag_mask(s)
            p = jnp.exp(s - lse_v)
            kv = ckv_ref[ks, :]
            ds = (p * (_dot(d_o, kv, _NT) - dsum) * ATT_SCALE).astype(_CDT)
            pb = p.astype(_CDT)
            dql_scr[...] += _dot(ds, kv, _NN)
            dqr_scr[...] += _dot(ds, kr_ref[ks, :], _NN)
            dckv_ref[ks, :] += _dot(pb, d_o, _TN) + _dot(ds, q_lat, _TN)
            dkr_ref[ks, :] += _dot(ds, q_rope, _TN)

        def loop_body(j, carry):
            step(j, False)
            return carry

        lax.fori_loop(0, qi, loop_body, 0)
        step(qi, True)
        dql_ref[...] = dql_scr[...].astype(dql_ref.dtype)
        dqr_ref[...] = _rope_bwd(dqr_scr[...], cos_ref[...], sin_ref[...]).astype(dqr_ref.dtype)

    lat = pl.BlockSpec((None, tq, KV_LORA), lambda h, i: (h, i, 0))
    rope = pl.BlockSpec((tq, ROPE_PAD), lambda h, i: (i, h))
    table = pl.BlockSpec((tq, ROPE_PAD), lambda h, i: (i, 0))
    kfull = pl.BlockSpec((t, KV_LORA), lambda h, i: (0, 0))
    rfull = pl.BlockSpec((t, ROPE_PAD), lambda h, i: (0, 0))
    return pl.pallas_call(
        body, name="attention_bwd", grid=(HEADS, t // tq),
        in_specs=[lat, rope, table, table, kfull, rfull, lat, pl.BlockSpec((None, tq, 1), lambda h, i: (h, i, 0)), lat],
        out_specs=[lat, rope, kfull, rfull],
        out_shape=[jax.ShapeDtypeStruct((HEADS, t, KV_LORA), _CDT), jax.ShapeDtypeStruct((t, HEADS * ROPE_PAD), _CDT),
                   jax.ShapeDtypeStruct((t, KV_LORA), F32), jax.ShapeDtypeStruct((t, ROPE_PAD), F32)],
        scratch_shapes=[pltpu.VMEM((tq, KV_LORA), F32), pltpu.VMEM((tq, ROPE_PAD), F32)],
        compiler_params=_cparams(("arbitrary", "arbitrary")))(ql, qr, cosb, sinb, ckv, kr, out, lse, dout)


_DX_ROWS = 256


def _dx_fused(pairs, add, add_scale, rider=None):
    t, d = add.shape
    tm = min(_DX_ROWS, t)
    n = len(pairs)

    def body(*refs):
        acc = refs[2 * n][...] * add_scale
        for i in range(n):
            acc = acc + _dot(refs[i][...], refs[n + i][...], _NT)
        refs[2 * n + 1][...] = acc

    in_specs = [pl.BlockSpec((tm, a.shape[1]), lambda i: (i, 0)) for a, _ in pairs]
    in_specs += [pl.BlockSpec(w.shape, lambda i: (0, 0)) for _, w in pairs]
    row = pl.BlockSpec((tm, d), lambda i: (i, 0))
    grid = (t // tm,)
    (dx,), carried = _carried_call(
        body, rider, *_grid_ends(grid), name="b_dx", grid=grid, in_specs=in_specs + [row], out_specs=[row],
        out_shape=[jax.ShapeDtypeStruct((t, d), F32)], scratch_shapes=[], sem=("arbitrary",),
        args=tuple(a for a, _ in pairs) + tuple(w for _, w in pairs) + (add,))
    return dx, carried


def _ffn_in_swiglu(a, w):
    t, k = a.shape
    hid = w.shape[1] // 2
    tm, tn = _tile(t, 1024), _tile(hid, 1408)
    nj = hid // tn

    def body(a_ref, bg_ref, bu_ref, act_ref, gt_ref, up_ref):
        av = a_ref[...].astype(_CDT)
        gt = jnp.dot(av, bg_ref[...].astype(_CDT), preferred_element_type=F32)
        up = jnp.dot(av, bu_ref[...].astype(_CDT), preferred_element_type=F32)
        act_ref[...] = _swiglu(gt, up).astype(act_ref.dtype)
        gt_ref[...] = gt.astype(gt_ref.dtype)
        up_ref[...] = up.astype(up_ref.dtype)

    out = pl.BlockSpec((tm, tn), lambda i, j: (i, j))
    return pl.pallas_call(
        body, name="f_ffn_in_swiglu", grid=(t // tm, nj),
        in_specs=[pl.BlockSpec((tm, k), lambda i, j: (i, 0)), pl.BlockSpec((k, tn), lambda i, j: (0, j)),
                  pl.BlockSpec((k, tn), lambda i, j: (0, nj + j))],
        out_specs=[out] * 3, out_shape=[jax.ShapeDtypeStruct((t, hid), _CDT)] * 3,
        compiler_params=_cparams(("parallel", "parallel")))(a, w, w)


def _gated_norm(o, z, w):
    return _rms_norm(o, w) * _silu(z)


def _gated_norm_heads(o, z, w):
    heads = [_gated_norm(o[:, h * LANE:(h + 1) * LANE], z[:, h * LANE:(h + 1) * LANE], w) for h in range(HEADS)]
    return jnp.concatenate(heads, axis=1)


def _mla_pre(ckv, krp, cq, cosb, sinb, qw, kw):
    return _rms_norm(cq, qw), _rms_norm(ckv, kw), _rope(krp, cosb, sinb)


def _merge(gg, y_dn, y_mla):
    return _sigmoid(gg[:, :D_MODEL]) * y_dn + _sigmoid(gg[:, D_MODEL:]) * y_mla


def _ln1(xv, attn_out, g, b):
    return _layer_norm(ALPHA * xv + attn_out, g, b)


def _final(h1, ffn, gate_pre, ple_proj, g, b):
    return _layer_norm(ALPHA * h1 + ffn + _sigmoid(gate_pre) * ple_proj, g, b)


def _swiglu(gt, up):
    return _silu(gt) * up


def _local_step(x, p, cosb, sinb, target, wt, sp, exch):
    t = x.shape[0]
    bf = _CDT
    xb = x.astype(bf)
    g = {}

    qkv_pre = _mm2(xb, wt['qkv'], name="f_qkv")
    z = _mm2(xb, wt['z'], name="f_z")
    gg = _mm2(xb, wt['gg'], name="f_gg")
    pm = _mm2(xb, wt['mla'], name="f_mla")
    qkv_act = _conv_silu(qkv_pre, sp['conv_w'])
    (u, w_, qd, kt, intra, gl, t_inv, qd2, au), sent = _delta_local(qkv_act, pm, sp['a_log'], sp['dt_bias'],
                                                                    rider=exch.gather_send())
    (o_dn, sall), passed = _delta_scan(u, w_, qd2, kt, au, gl, rider=exch.gather_pass(sent))
    wt = dict(wt, **exch.weights(passed))
    def gated_norm_br(h, o, zz, w, wbr):
        og_v = _gated_norm_heads(o, zz, w).astype(bf)
        return og_v, jnp.dot(og_v, wbr.astype(bf), preferred_element_type=F32)

    og, y_dn = _rowwise(gated_norm_br, [(o_dn, D_MODEL, 0, False), (z, D_MODEL, 0, False)],
                        [sp['dn_norm_w'], wt['br_dn']],
                        [(D_MODEL, D_MODEL, False, bf), (D_MODEL, D_MODEL, False, F32)], name="f_gated_norm_br")

    c_q, c_kv, k_rope = _rowwise(
        lambda h, *a: _mla_pre(*a),
        [(pm, KV_LORA, 0, False), (pm, ROPE_PAD, 2, False), (pm, Q_LORA, 2, False),
         (cosb, ROPE_PAD, 0, False), (sinb, ROPE_PAD, 0, False)],
        [sp['q_norm_w'], sp['kv_norm_w']],
        [(Q_LORA, Q_LORA, False, bf), (KV_LORA, KV_LORA, False, bf), (ROPE_PAD, ROPE_PAD, False, bf)], name="f_mla_pre")
    q_nope = _mm2(c_q, wt['uq_nope'], name="f_uq_nope", out_dtype=bf)
    q_rope_pre = _mm2(c_q, wt['uq_rope'], name="f_uq_rope")
    q_lat = _mm(q_nope, wt['uk'], name="f_q_lat", tb=True, heads=HEADS, a_head='col', b_head='lead', out_head='lead',
                dims=(t, KV_LORA, NOPE), out_dtype=bf)
    out_lat, lse, q_rope = _attention(q_lat, q_rope_pre, cosb, sinb, c_kv, k_rope)
    o_mla = _mm(out_lat, wt['uv'], name="f_o_mla", heads=HEADS, a_head='lead', b_head='lead', out_head='col',
                dims=(t, NOPE, KV_LORA), out_dtype=bf)
    y_mla = _mm2(o_mla, wt['br_mla'], name="f_br_mla")

    def merge_o_ln1(h, ggv, yd, ym, xv, wo, gv, bv):
        mixed_v = _merge(ggv, yd, ym).astype(bf)
        ao = jnp.dot(mixed_v, wo.astype(bf), preferred_element_type=F32)
        h1v = _ln1(xv, ao, gv, bv)
        return mixed_v, ao, h1v, h1v

    mixed, attn_out, h1, h1b = _rowwise(
        merge_o_ln1, [(gg, 2 * D_MODEL, 0, False), (y_dn, D_MODEL, 0, False), (y_mla, D_MODEL, 0, False), (x, D_MODEL, 0, False)],
        [wt['o'], sp['ln1_g'], sp['ln1_b']],
        [(D_MODEL, D_MODEL, False, bf), (D_MODEL, D_MODEL, False, F32), (D_MODEL, D_MODEL, False, F32),
         (D_MODEL, D_MODEL, False, bf)], name="f_merge_o_ln1")
    act, ffn_gt, ffn_up = _ffn_in_swiglu(h1b, wt['ffn_in'])
    ffn = _mm2(act, wt['ffn_out'], name="f_ffn_out")
    gate_pre = _mm2(h1b, wt['ple_gate'], name="f_ple_gate")
    pb = p.astype(bf)
    ple_proj = _mm2(pb, wt['ple'], name="f_ple")

    def final_fn(h, h1v, ffnv, gpv, ppv, tgt, gt, up, wfo, gv, bv):
        y, vjp = jax.vjp(_final, h1v, ffnv, gpv, ppv, gv, bv)
        err = y - tgt
        dh1, dffn, dgp, dpp, dg, db = vjp(err * (1.0 / D_MODEL))
        sq = err * err
        lanes = sq[:, :LANE]
        for j in range(1, D_MODEL // LANE):
            lanes = lanes + sq[:, j * LANE:(j + 1) * LANE]
        loss = jnp.sum(lanes, axis=0, keepdims=True) * (0.5 / D_MODEL)
        dffn_b = dffn.astype(bf)
        dact = _dot(dffn_b, wfo, _NT).astype(bf).astype(F32)
        _, vjp_s = jax.vjp(_swiglu, gt.astype(F32), up.astype(F32))
        dgt, dup = vjp_s(dact)
        return dffn, dffn_b, dgp, dpp, jnp.concatenate([dgt, dup], axis=1), dg, db, loss

    dpre2, dpre2b, dgate_pre, dple_proj, dffn_in, g['ln2_g'], g['ln2_b'], loss_lanes = _rowwise(
        final_fn, [(a, D_MODEL, 0, False) for a in (h1, ffn, gate_pre, ple_proj, target)]
        + [(ffn_gt, FFN_HIDDEN, 0, False), (ffn_up, FFN_HIDDEN, 0, False)],
        [wt['ffn_out'], sp['ln2_g'], sp['ln2_b']],
        [(D_MODEL, D_MODEL, False, F32)] + [(D_MODEL, D_MODEL, False, bf)] * 3 + [(2 * FFN_HIDDEN, 2 * FFN_HIDDEN, False, bf)],
        [(1, D_MODEL), (1, D_MODEL), (1, LANE)], name="b_final")

    g['ple'] = _mm2(pb, dple_proj, ta=True, name="g_ple")
    g['ple_gate'] = _mm2(h1b, dgate_pre, ta=True, name="g_ple_gate")
    g['ffn_out'] = _mm2(act, dpre2b, ta=True, name="g_ffn_out")
    g['ffn_in'] = _mm2(h1b, dffn_in, ta=True, name="g_ffn_in")
    dh1 = _mm2(dffn_in, wt['ffn_in'], tb=True, name="b_dh1_ffn", add=dpre2, add_scale=ALPHA)
    dh1 = _mm2(dgate_pre, wt['ple_gate'], tb=True, name="b_dh1_gate", add=dh1)

    def attn_out_bwd(h, xv, ao, d, ggv, yd, ym, wo, wbd, wbm, gv, bv):
        _, vjp = jax.vjp(_ln1, xv, ao, gv, bv)
        _, dao, dg, db = vjp(d)
        dao_b = dao.astype(bf)
        _, vjp_m = jax.vjp(_merge, ggv, yd, ym)
        dggv, dyd, dym = vjp_m(_dot(dao_b, wo, _NT))
        dyd_b, dym_b = dyd.astype(bf), dym.astype(bf)
        return dao, dao_b, dggv, dyd_b, dym_b, _dot(dyd_b, wbd, _NT), _dot(dym_b, wbm, _NT), dg, db

    row_d = lambda a: (a, D_MODEL, 0, False)
    dpre1, dpre1b, dgg, dy_dn, dy_mla, dog, do_mla, g['ln1_g'], g['ln1_b'] = _rowwise(
        attn_out_bwd, [row_d(x), row_d(attn_out), row_d(dh1), (gg, 2 * D_MODEL, 0, False), row_d(y_dn), row_d(y_mla)],
        [wt['o'], wt['br_dn'], wt['br_mla'], sp['ln1_g'], sp['ln1_b']],
        [(D_MODEL, D_MODEL, False, F32), (D_MODEL, D_MODEL, False, bf), (2 * D_MODEL, 2 * D_MODEL, False, bf),
         (D_MODEL, D_MODEL, False, bf), (D_MODEL, D_MODEL, False, bf), (D_MODEL, D_MODEL, False, F32),
         (D_MODEL, D_MODEL, False, bf)],
        [(1, D_MODEL), (1, D_MODEL)], name="b_attn_out")
    g['o'] = _mm2(mixed, dpre1b, ta=True, name="g_o")
    g['br_dn'] = _mm2(og, dy_dn, ta=True, name="g_br_dn")
    g['br_mla'] = _mm2(o_mla, dy_mla, ta=True, name="g_br_mla")

    dout_lat = _mm(do_mla, wt['uv'], name="b_dout_lat", tb=True, heads=HEADS, a_head='col', b_head='lead',
                   out_head='lead', dims=(t, KV_LORA, NOPE))
    g['uv'] = _mm(out_lat, do_mla, name="g_uv", ta=True, heads=HEADS, a_head='lead', b_head='col', out_head='lead',
                  dims=(KV_LORA, NOPE, t))
    dq_lat, dq_rope_pre, dckv_att, dkr_att = _attention_bwd(q_lat, q_rope, cosb, sinb, c_kv, k_rope, out_lat, lse, dout_lat)
    dq_nope = _mm(dq_lat, wt['uk'], name="b_dq_nope", heads=HEADS, a_head='lead', b_head='lead', out_head='col',
                  dims=(t, NOPE, KV_LORA), out_dtype=bf)
    g['uk'] = _mm(dq_lat, q_nope, name="g_uk", ta=True, heads=HEADS, a_head='lead', b_head='col', out_head='lead',
                  dims=(KV_LORA, NOPE, t))
    g['uq_nope'] = _mm2(c_q, dq_nope, ta=True, name="g_uq_nope")
    g['uq_rope'] = _mm2(c_q, dq_rope_pre, ta=True, name="g_uq_rope")
    dc_q = _mm2(dq_nope, wt['uq_nope'], tb=True, name="b_dcq_nope")
    dc_q = _mm2(dq_rope_pre, wt['uq_rope'], tb=True, name="b_dcq_rope", add=dc_q)

    def gated_norm_bwd(h, o, zz, d, w):
        _, vjp = jax.vjp(_gated_norm_heads, o, zz, w)
        return vjp(d)

    do_dn, dz, g['dn_norm_w'] = _rowwise(
        gated_norm_bwd, [(o_dn, D_MODEL, 0, False), (z, D_MODEL, 0, False), (dog, D_MODEL, 0, False)], [sp['dn_norm_w']],
        [(D_MODEL, D_MODEL, False, F32), (D_MODEL, D_MODEL, False, bf)], [(1, LANE)], name="b_gated_norm")
    (du, dw, dqd, dkt, dintra, dgl), paired = _delta_scan_bwd(u, w_, qd, kt, intra, gl, sall, do_dn, rider=exch.pair_send(g))
    (dq_a, dk_a, dv_a, dba, g['a_log'], g['dt_bias']), arrived = _delta_local_bwd(
        qkv_act, pm, sp['a_log'], sp['dt_bias'], t_inv, du, dw, dqd, dkt, dintra, dgl, rider=exch.reduce_send(paired))
    exch.reduce_arrived(arrived)
    dqkv_pre, g['conv_w'] = _conv_silu_bwd(qkv_pre, sp['conv_w'], [dq_a, dk_a, dv_a])

    def mla_pre_bwd(h, ckv, cq, cosv, sinv, dcq, dckv, dkr, dba_v, qw, kw):
        _, vjp = jax.vjp(lambda a, c, d, e: (_rms_norm(c, d), _rms_norm(a, e)), ckv, cq, qw, kw)
        dckv_p, dcq_p, dqw, dkw = vjp((dcq, dckv))
        dkr_p = _rope_bwd(dkr, cosv, sinv)
        dpm = jnp.concatenate([dckv_p, dkr_p, dba_v, jnp.zeros((ckv.shape[0], 2 * LANE), F32), dcq_p], axis=1)
        return dpm, dqw, dkw

    dpm, g['q_norm_w'], g['kv_norm_w'] = _rowwise(
        mla_pre_bwd,
        [(pm, KV_LORA, 0, False), (pm, Q_LORA, 2, False), (cosb, ROPE_PAD, 0, False), (sinb, ROPE_PAD, 0, False),
         (dc_q, Q_LORA, 0, False), (dckv_att, KV_LORA, 0, False), (dkr_att, ROPE_PAD, 0, False), (dba, LANE, 0, False)],
        [sp['q_norm_w'], sp['kv_norm_w']], [(1152, 1152, False, bf)], [(1, Q_LORA), (1, KV_LORA)], name="b_mla_pre")

    g['qkv'] = _mm2(xb, dqkv_pre, ta=True, name="g_qkv")
    g['z'] = _mm2(xb, dz, ta=True, name="g_z")
    g['gg'] = _mm2(xb, dgg, ta=True, name="g_gg")
    g['mla'] = _mm2(xb, dpm, ta=True, name="g_mla")
    dx, arrived = _dx_fused([(dqkv_pre, wt['qkv']), (dz, wt['z']), (dgg, wt['gg']), (dpm, wt['mla'])], dpre1, ALPHA,
                            rider=exch.in_send(g))
    exch.in_arrived(arrived)
    return loss_lanes, dx, g


_IN_SIZES = (QKV_W, HEADS * DN_DK, HEADS, HEADS, Q_LORA, KV_LORA, ROPE, D_MODEL, D_MODEL)


def _rope_tables(positions):
    inv_freq = ROPE_BASE ** (-jnp.arange(0, ROPE, 2, dtype=F32) / ROPE)
    ang = positions.astype(F32)[:, None] * inv_freq
    cos, sin = jnp.cos(ang), jnp.sin(ang)
    zeros = jnp.zeros((positions.shape[0], ROPE_PAD - ROPE), F32)
    return jnp.concatenate([cos, cos, zeros], axis=1), jnp.concatenate([-sin, sin, zeros], axis=1)


def _prep_w_in(w_in):
    dt = w_in.dtype
    offs = [0]
    for s in _IN_SIZES:
        offs.append(offs[-1] + s)
    qkv, z, wb, wa, cq, ckv, kr, gd, gm = [w_in[:, offs[i]:offs[i + 1]] for i in range(len(_IN_SIZES))]
    zc = lambda n: jnp.zeros((D_MODEL, n), dt)
    return {
        'qkv': qkv, 'z': z, 'gg': jnp.concatenate([gd, gm], axis=1),
        'mla': jnp.concatenate([ckv, kr, zc(ROPE_PAD - ROPE), wb, wa, zc(LANE - 2 * HEADS), zc(2 * LANE), cq], axis=1),
    }


def _prep_weights(full):
    w_uq = full['w_uq']
    wt = {
        'uq_nope': w_uq[:, :, :NOPE].reshape(Q_LORA, HEADS * NOPE),
        'uq_rope': jnp.pad(w_uq[:, :, NOPE:], ((0, 0), (0, 0), (0, ROPE_PAD - ROPE))).reshape(Q_LORA, HEADS * ROPE_PAD),
        'uk': jnp.transpose(full['w_uk'], (1, 0, 2)), 'uv': jnp.transpose(full['w_uv'], (1, 0, 2)),
        'br_dn': full['w_br_dn'], 'br_mla': full['w_br_mla'], 'o': full['w_o'], 'ffn_in': full['w_ffn_in'],
        'ffn_out': full['w_ffn_out'], 'ple': full['w_ple'], 'ple_gate': full['w_ple_gate'],
    }
    return wt


def _prep_small(small):
    pad = lambda v: jnp.pad(v, (0, LANE - v.shape[0]))[None, :]
    return {
        'conv_w': small['conv_w'], 'a_log': pad(small['dn_a_log']), 'dt_bias': pad(small['dn_dt_bias']),
        'dn_norm_w': small['dn_norm_w'][None, :], 'q_norm_w': small['q_norm_w'][None, :],
        'kv_norm_w': small['kv_norm_w'][None, :], 'ln1_g': small['ln1_g'][None, :], 'ln1_b': small['ln1_b'][None, :],
        'ln2_g': small['ln2_g'][None, :], 'ln2_b': small['ln2_b'][None, :],
    }


def _w_in_grad(g):
    mla = g['mla']
    ba0 = KV_LORA + ROPE_PAD
    cq0 = ba0 + 3 * LANE
    return jnp.concatenate([
        g['qkv'], g['z'], mla[:, ba0:ba0 + HEADS], mla[:, ba0 + HEADS:ba0 + 2 * HEADS], mla[:, cq0:cq0 + Q_LORA],
        mla[:, :KV_LORA], mla[:, KV_LORA:KV_LORA + ROPE], g['gg']], axis=1)


def _unprep_grads_late(g):
    return {
        'conv_w': g['conv_w'], 'dn_a_log': g['a_log'][0, :HEADS], 'dn_dt_bias': g['dt_bias'][0, :HEADS],
        'dn_norm_w': g['dn_norm_w'][0], 'q_norm_w': g['q_norm_w'][0], 'kv_norm_w': g['kv_norm_w'][0],
        'ln1_g': g['ln1_g'][0], 'ln1_b': g['ln1_b'][0], 'ln2_g': g['ln2_g'][0], 'ln2_b': g['ln2_b'][0],
    }


def _unprep_grads_early(g):
    w_uq = jnp.concatenate([g['uq_nope'].reshape(Q_LORA, HEADS, NOPE),
                            g['uq_rope'].reshape(Q_LORA, HEADS, ROPE_PAD)[:, :, :ROPE]], axis=2)
    return {
        'w_uq': w_uq, 'w_uk': jnp.transpose(g['uk'], (1, 0, 2)), 'w_uv': jnp.transpose(g['uv'], (1, 0, 2)),
        'w_br_dn': g['br_dn'], 'w_br_mla': g['br_mla'], 'w_o': g['o'],
        'w_ffn_in': g['ffn_in'], 'w_ffn_out': g['ffn_out'], 'w_ple': g['ple'], 'w_ple_gate': g['ple_gate'],
    }


_FLATB_PIECES = (
    ('w_ffn_out', 704, (704, D_MODEL)), ('w_br_dn', 256, (256, D_MODEL)), ('w_br_mla', 256, (256, D_MODEL)),
    ('w_o', 256, (256, D_MODEL)), ('w_ple_gate', 256, (256, D_MODEL)), ('w_uq', 144, (96, HEADS, NOPE + ROPE)),
    ('w_uk', 64, (64, HEADS, NOPE)), ('w_uv', 64, (64, HEADS, NOPE)), ('w_ple', 64, (PLE_DIM, 256)),
)
FLATB_ROWS = 2112
W_IN_SHARD = D_IN // N_SHARD
FFN_IN_SHARD = 2 * FFN_HIDDEN // N_SHARD
A_ROWS = D_MODEL + 32
_CONV_SHARD = QKV_W // N_SHARD
_ADD_TILES = (256, 256, 352)


def _flatb_offsets():
    offs, o = {}, 0
    for name, rows, _ in _FLATB_PIECES:
        offs[name] = o
        o += rows
    return offs, o


def _pack_shards(ws, conv_w):
    conv_bits = lax.bitcast_convert_type(conv_w, jnp.bfloat16).reshape(DN_CONV, 2 * _CONV_SHARD).astype(_CDT)
    tail = jnp.pad(conv_bits, ((0, A_ROWS - D_MODEL - DN_CONV), (0, W_IN_SHARD - 2 * _CONV_SHARD)))
    a_buf = jnp.concatenate([ws['w_in'].astype(_CDT), tail], axis=0)
    parts = [ws[name].astype(_CDT).reshape(rows, FLAT_W) for name, rows, _ in _FLATB_PIECES]
    used = sum(p.shape[0] for p in parts)
    parts.append(jnp.zeros((FLATB_ROWS - used, FLAT_W), _CDT))
    return [a_buf, ws['w_ffn_in'].astype(_CDT), jnp.concatenate(parts, axis=0)]


def _unpack_w_in(gathered, local, me):
    a = [jnp.where(me == s, local, gathered[s]) for s in range(N_SHARD)]
    conv = [lax.bitcast_convert_type(
        p[D_MODEL:D_MODEL + DN_CONV, :2 * _CONV_SHARD].astype(jnp.bfloat16).reshape(DN_CONV, _CONV_SHARD, 2), F32) for p in a]
    return jnp.concatenate([p[:D_MODEL] for p in a], axis=1), jnp.concatenate(conv, axis=1)


def _unpack_rest(gathered, local, me):
    pick = lambda b, s: jnp.where(me == s, local[b], gathered[b][s])
    full = {'w_ffn_in': jnp.concatenate([pick(0, s) for s in range(N_SHARD)], axis=1)}
    offs, _ = _flatb_offsets()
    fb = [pick(1, s) for s in range(N_SHARD)]
    for name, rows, shape in _FLATB_PIECES:
        pieces = [p[offs[name]:offs[name] + rows].reshape(shape) for p in fb]
        full[name] = jnp.concatenate(pieces, axis=1 if name == 'w_ple' else 0)
    return full


def _shard_columns(g, w):
    return jnp.stack([g[:, s * w:(s + 1) * w] for s in range(N_SHARD)])


def _pack_grads_rest(gw):
    parts = []
    for name, rows, _ in _FLATB_PIECES:
        g = gw[name]
        if name == 'w_ple':
            parts.append(_shard_columns(g, PLE_DIM).reshape(N_SHARD, rows, FLAT_W))
        else:
            parts.append(g.reshape(N_SHARD, rows, FLAT_W))
    used = sum(p.shape[1] for p in parts)
    parts.append(jnp.zeros((N_SHARD, FLATB_ROWS - used, FLAT_W), F32))
    return [_shard_columns(gw['w_ffn_in'], FFN_IN_SHARD), jnp.concatenate(parts, axis=1)]


def _unpack_reduced(mine, theirs, c):
    whole = [jnp.concatenate([jnp.where(c == 0, m, t), jnp.where(c == 0, t, m)], axis=0) for m, t in zip(mine, theirs)]
    out = {'w_in': whole[0], 'w_ffn_in': whole[1]}
    offs, _ = _flatb_offsets()
    for name, rows, shape in _FLATB_PIECES:
        out[name] = whole[2][offs[name]:offs[name] + rows].reshape(shape)
    return out


_HBM = pl.BlockSpec(memory_space=pltpu.HBM)


def _place():
    x, y, c = lax.axis_index("x"), lax.axis_index("y"), lax.axis_index("c")
    chips = [(1 - x, y), (x, 1 - y), (1 - x, 1 - y)]
    return x, y, c, chips


def _remote(src, dst, send_sems, recv_sems, k, to):
    return pltpu.make_async_remote_copy(src_ref=src, dst_ref=dst, send_sem=send_sems.at[k], recv_sem=recv_sems.at[k],
                                        device_id=to, device_id_type=_MESH)


def _half_rows(ref, half, hf, lead=None):
    rows = pl.ds(pl.multiple_of(hf * half, 16), half)
    return ref.at[rows, :] if lead is None else ref.at[lead, rows, :]


class _Rider:
    def __init__(self, inputs, out_shape, n_sems, copies, aliases=None):
        self.inputs, self.out_shape, self.n_sems, self.copies = list(inputs), list(out_shape), n_sems, copies
        self.aliases = aliases or {}


def _carried_call(body, rider, first, last, *, name, grid, in_specs, out_specs, out_shape, scratch_shapes, sem, args):
    n_in, n_out, n_scr = len(in_specs), len(out_specs), len(scratch_shapes)
    if rider is None:
        res = pl.pallas_call(body, name=name, grid=grid, in_specs=in_specs, out_specs=out_specs, out_shape=out_shape,
                             scratch_shapes=scratch_shapes, compiler_params=_cparams(sem))(*args)
        return list(res), []
    ri, ro = len(rider.inputs), len(rider.out_shape)

    def full_body(*refs):
        own_in, r_in = refs[:n_in], refs[n_in:n_in + ri]
        o0 = n_in + ri
        own_out, r_out = refs[o0:o0 + n_out], refs[o0 + n_out:o0 + n_out + ro]
        s0 = o0 + n_out + ro
        own_scr, send_sems, recv_sems = refs[s0:s0 + n_scr], refs[s0 + n_scr], refs[s0 + n_scr + 1]

        @pl.when(first())
        def _():
            sends, _ = rider.copies(r_in, r_out, send_sems, recv_sems)
            for cp in sends:
                cp.start()

        body(*own_in, *own_out, *own_scr)

        @pl.when(last())
        def _():
            sends, arrivals = rider.copies(r_in, r_out, send_sems, recv_sems)
            for cp in arrivals():
                cp.wait_recv()
            for cp in sends:
                cp.wait_send()

    res = pl.pallas_call(
        full_body, name=name, grid=grid, in_specs=list(in_specs) + [_HBM] * ri, out_specs=list(out_specs) + [_HBM] * ro,
        out_shape=list(out_shape) + rider.out_shape,
        scratch_shapes=list(scratch_shapes) + [pltpu.SemaphoreType.DMA((rider.n_sems,))] * 2,
        input_output_aliases={n_in + i: n_out + o for i, o in rider.aliases.items()},
        compiler_params=_cparams(sem))(*args, *rider.inputs)
    return list(res[:n_out]), list(res[n_out:])


def _ride_gather_send(bufs):
    n = len(bufs)
    halves = [b.shape[0] // 2 for b in bufs]

    def copies(ins, outs, send_sems, recv_sems):
        x, y, c, chips = _place()
        slot = lambda b, cx, cy: _half_rows(outs[b], halves[b], c, lead=2 * cx + cy)
        sends = [_remote(_half_rows(ins[b], halves[b], c), slot(b, x, y), send_sems, recv_sems, 3 * b + j, (cx, cy, c))
                 for b in range(n) for j, (cx, cy) in enumerate(chips)]
        arrivals = lambda: [_remote(slot(b, cx, cy), slot(b, cx, cy), send_sems, recv_sems, 3 * b + j, (x, y, c))
                            for b in range(n) for j, (cx, cy) in enumerate(chips)]
        return sends, arrivals

    return _Rider(bufs, [jax.ShapeDtypeStruct((N_SHARD,) + b.shape, b.dtype) for b in bufs], 3 * n, copies)


def _ride_gather_pass(gathered):
    n = len(gathered)
    halves = [g.shape[1] // 2 for g in gathered]

    def copies(ins, outs, send_sems, recv_sems):
        x, y, c, chips = _place()
        slot = lambda b, cx, cy, hf: _half_rows(outs[b], halves[b], hf, lead=2 * cx + cy)
        sends = [_remote(slot(b, cx, cy, c), slot(b, cx, cy, c), send_sems, recv_sems, 3 * b + j, (x, y, 1 - c))
                 for b in range(n) for j, (cx, cy) in enumerate(chips)]
        arrivals = lambda: [_remote(slot(b, cx, cy, 1 - c), slot(b, cx, cy, 1 - c), send_sems, recv_sems, 3 * b + j, (x, y, c))
                            for b in range(n) for j, (cx, cy) in enumerate(chips)]
        return sends, arrivals

    return _Rider(gathered, [jax.ShapeDtypeStruct(g.shape, g.dtype) for g in gathered], 3 * n, copies,
                  aliases={b: b for b in range(n)})


def _ride_pair_exchange(gbufs):
    n = len(gbufs)
    halves = [g.shape[1] // 2 for g in gbufs]

    def copies(ins, outs, send_sems, recv_sems):
        x, y, c, _ = _place()
        sends = [_remote(ins[b].at[:, pl.ds(pl.multiple_of((1 - c) * halves[b], 16), halves[b]), :], outs[b],
                         send_sems, recv_sems, b, (x, y, 1 - c)) for b in range(n)]
        arrivals = lambda: [_remote(outs[b], outs[b], send_sems, recv_sems, b, (x, y, c)) for b in range(n)]
        return sends, arrivals

    return _Rider(gbufs, [jax.ShapeDtypeStruct((N_SHARD, h, g.shape[2]), g.dtype) for g, h in zip(gbufs, halves)], n, copies)


def _ride_chip_exchange(parts):
    n = len(parts)

    def copies(ins, outs, send_sems, recv_sems):
        x, y, c, chips = _place()
        sends = [_remote(ins[b].at[2 * cx + cy], outs[b].at[j], send_sems, recv_sems, 3 * b + j, (cx, cy, c))
                 for b in range(n) for j, (cx, cy) in enumerate(chips)]
        arrivals = lambda: [_remote(ins[b].at[0], outs[b].at[j], send_sems, recv_sems, 3 * b + j, (x, y, c))
                            for b in range(n) for j in range(len(chips))]
        return sends, arrivals

    return _Rider(parts, [jax.ShapeDtypeStruct((3,) + p.shape[1:], p.dtype) for p in parts], 3 * n, copies)


def _gather_shards(bufs, name):
    n = len(bufs)
    halves = [b.shape[0] // 2 for b in bufs]

    def body(*refs):
        ins, outs, send_sems, recv_sems = refs[:n], refs[n:2 * n], refs[2 * n], refs[2 * n + 1]
        x, y, c, chips = _place()
        me, sibling = (x, y, c), (x, y, 1 - c)
        slot = lambda b, cx, cy, hf: _half_rows(outs[b], halves[b], hf, lead=2 * cx + cy)
        first = [_remote(_half_rows(ins[b], halves[b], c), slot(b, x, y, c), send_sems, recv_sems, 6 * b + j, (cx, cy, c))
                 for b in range(n) for j, (cx, cy) in enumerate(chips)]
        for cp in first:
            cp.start()
        passed = []
        for j, (cx, cy) in enumerate(chips):
            for b in range(n):
                _remote(slot(b, cx, cy, c), slot(b, cx, cy, c), send_sems, recv_sems, 6 * b + j, me).wait_recv()
                fwd = _remote(slot(b, cx, cy, c), slot(b, cx, cy, c), send_sems, recv_sems, 6 * b + 3 + j, sibling)
                fwd.start()
                passed.append(fwd)
        for j, (cx, cy) in enumerate(chips):
            for b in range(n):
                _remote(slot(b, cx, cy, 1 - c), slot(b, cx, cy, 1 - c), send_sems, recv_sems, 6 * b + 3 + j, me).wait_recv()
        for cp in first + passed:
            cp.wait_send()

    return pl.pallas_call(
        body, name=name, out_shape=[jax.ShapeDtypeStruct((N_SHARD,) + b.shape, b.dtype) for b in bufs],
        in_specs=[_HBM] * n, out_specs=[_HBM] * n,
        scratch_shapes=[pltpu.SemaphoreType.DMA((6 * n,)), pltpu.SemaphoreType.DMA((6 * n,))],
    )(*bufs)


def _reduce_pair_exchange(gbufs, name):
    n = len(gbufs)
    halves = [g.shape[1] // 2 for g in gbufs]

    def body(*refs):
        ins, outs, send_sems, recv_sems = refs[:n], refs[n:2 * n], refs[2 * n], refs[2 * n + 1]
        x, y, c, _ = _place()
        cps = [_remote(ins[b].at[:, pl.ds(pl.multiple_of((1 - c) * halves[b], 16), halves[b]), :], outs[b],
                       send_sems, recv_sems, b, (x, y, 1 - c)) for b in range(n)]
        for cp in cps:
            cp.start()
        for cp in cps:
            cp.wait()

    return pl.pallas_call(
        body, name=name,
        out_shape=[jax.ShapeDtypeStruct((N_SHARD, h, g.shape[2]), g.dtype) for g, h in zip(gbufs, halves)],
        in_specs=[_HBM] * n, out_specs=[_HBM] * n,
        scratch_shapes=[pltpu.SemaphoreType.DMA((n,)), pltpu.SemaphoreType.DMA((n,))],
    )(*gbufs)


def _pair_add(gbuf, recv, c_arr, tr, name):
    _, rows, width = gbuf.shape
    half = rows // 2
    nt = half // tr

    def body(c_ref, a_ref, b_ref, o_ref):
        o_ref[...] = (a_ref[...] + b_ref[...]).astype(o_ref.dtype)

    blk = lambda f: pl.BlockSpec((None, tr, width), f)
    return pl.pallas_call(
        body, name=name, out_shape=jax.ShapeDtypeStruct((N_SHARD, half, width), jnp.bfloat16),
        grid_spec=pltpu.PrefetchScalarGridSpec(
            num_scalar_prefetch=1, grid=(N_SHARD, nt),
            in_specs=[blk(lambda s, i, c: (s, c[0] * nt + i, 0)), blk(lambda s, i, c: (s, i, 0))],
            out_specs=blk(lambda s, i, c: (s, i, 0))),
        compiler_params=_cparams(("parallel", "parallel")))(c_arr, gbuf, recv)


def _reduce_chip_exchange(parts, name):
    n = len(parts)

    def body(*refs):
        ins, outs, send_sems, recv_sems = refs[:n], refs[n:2 * n], refs[2 * n], refs[2 * n + 1]
        x, y, c, chips = _place()
        sends = [_remote(ins[b].at[2 * cx + cy], outs[b].at[j], send_sems, recv_sems, 3 * b + j, (cx, cy, c))
                 for b in range(n) for j, (cx, cy) in enumerate(chips)]
        for cp in sends:
            cp.start()
        for b in range(n):
            for j in range(len(chips)):
                _remote(ins[b].at[0], outs[b].at[j], send_sems, recv_sems, 3 * b + j, (x, y, c)).wait_recv()
        for cp in sends:
            cp.wait_send()

    return pl.pallas_call(
        body, name=name, out_shape=[jax.ShapeDtypeStruct((3,) + p.shape[1:], p.dtype) for p in parts],
        in_specs=[_HBM] * n, out_specs=[_HBM] * n,
        scratch_shapes=[pltpu.SemaphoreType.DMA((3 * n,)), pltpu.SemaphoreType.DMA((3 * n,))],
    )(*parts)


def _chip_add(part, recv, me_arr, tr, name):
    _, half, width = part.shape

    def body(me_ref, own, a0, a1, a2, o_ref):
        f = lambda r: r[...].astype(F32)
        o_ref[...] = ((f(own) + f(a0)) + f(a1)) + f(a2)

    specs = [pl.BlockSpec((None, tr, width), lambda i, me: (me[0], i, 0))]
    specs += [pl.BlockSpec((None, tr, width), functools.partial(lambda i, me, k: (k, i, 0), k=k)) for k in range(3)]
    return pl.pallas_call(
        body, name=name, out_shape=jax.ShapeDtypeStruct((half, width), F32),
        grid_spec=pltpu.PrefetchScalarGridSpec(
            num_scalar_prefetch=1, grid=(half // tr,), in_specs=specs,
            out_specs=pl.BlockSpec((tr, width), lambda i, me: (i, 0))),
        compiler_params=_cparams(("parallel",)))(me_arr, part, recv, recv, recv)


def _reduce_pair_share(rhalves, name):
    n = len(rhalves)

    def body(*refs):
        ins, outs, send_sems, recv_sems = refs[:n], refs[n:2 * n], refs[2 * n], refs[2 * n + 1]
        x, y, c, _ = _place()
        cps = [_remote(ins[b], outs[b], send_sems, recv_sems, b, (x, y, 1 - c)) for b in range(n)]
        for cp in cps:
            cp.start()
        for cp in cps:
            cp.wait()

    return pl.pallas_call(
        body, name=name, out_shape=[jax.ShapeDtypeStruct(r.shape, r.dtype) for r in rhalves],
        in_specs=[_HBM] * n, out_specs=[_HBM] * n,
        scratch_shapes=[pltpu.SemaphoreType.DMA((n,)), pltpu.SemaphoreType.DMA((n,))],
    )(*rhalves)


def _small_allreduce(buf):
    r, width = buf.shape
    n_dev = 8

    def body(x_ref, all_ref, sum_ref, send_sems, recv_sems, local_sem):
        x, y, c, chips = _place()
        me, sibling = (x, y, c), (x, y, 1 - c)

        def rows(px, py, pc):
            return all_ref.at[pl.ds(pl.multiple_of((4 * px + 2 * py + pc) * r, 8), r), :]

        def copy(k, block, to, src=None):
            return _remote(rows(*block) if src is None else src, rows(*block), send_sems, recv_sems, k, to)

        mine = pltpu.make_async_copy(x_ref, rows(*me), local_sem)
        mine.start()
        first = [copy(0, me, sibling, src=x_ref)]
        first += [copy(1 + j, me, (*chip, c), src=x_ref) for j, chip in enumerate(chips)]
        for cp in first:
            cp.start()
        passed = [copy(4 + j, (*chip, c), sibling) for j, chip in enumerate(chips)]
        for j, chip in enumerate(chips):
            copy(1 + j, (*chip, c), me).wait_recv()
            passed[j].start()
        copy(0, sibling, me).wait_recv()
        for j, chip in enumerate(chips):
            copy(4 + j, (*chip, 1 - c), me).wait_recv()
        for cp in first + passed:
            cp.wait_send()
        mine.wait()
        total = all_ref[0:r, :]
        for k in range(1, n_dev):
            total = total + all_ref[k * r:(k + 1) * r, :]
        sum_ref[...] = total

    vm = pl.BlockSpec(memory_space=pltpu.VMEM)
    _, total = pl.pallas_call(
        body, name="small_allreduce",
        out_shape=[jax.ShapeDtypeStruct((n_dev * r, width), buf.dtype), jax.ShapeDtypeStruct((r, width), buf.dtype)],
        in_specs=[vm], out_specs=[vm, vm],
        scratch_shapes=[pltpu.SemaphoreType.DMA((7,)), pltpu.SemaphoreType.DMA((7,)), pltpu.SemaphoreType.DMA],
    )(buf)
    return total


def _row_tile(rows, cap):
    if rows <= cap:
        return rows
    t = (cap // 8) * 8
    while t >= 8:
        if rows % t == 0:
            return t
        t -= 8
    return rows


def _adamw(w, g, m, v, name):
    shape = w.shape
    cols = shape[-1] if len(shape) <= 3 else shape[-2] * shape[-1]
    lead = len(shape) == 3
    w2, g2, m2, v2 = (a if lead else a.reshape(-1, cols) for a in (w, g, m, v))
    rows = shape[1] if lead else w2.shape[0]
    tr, tc = _row_tile(rows, 256), cols
    if tr == rows and rows > 256:
        tc = _tile(cols, 256)

    def body(w_ref, g_ref, m_ref, v_ref, d_ref, mo_ref, vo_ref):
        gv = g_ref[...]
        mn = ADAM_B1 * m_ref[...] + (1.0 - ADAM_B1) * gv
        vn = ADAM_B2 * v_ref[...] + (1.0 - ADAM_B2) * (gv * gv)
        m_hat = mn / (1.0 - ADAM_B1 ** ADAM_STEP)
        v_hat = vn / (1.0 - ADAM_B2 ** ADAM_STEP)
        d_ref[...] = -ADAM_LR * (m_hat / (jnp.sqrt(v_hat) + ADAM_EPS) + ADAM_WD * w_ref[...])
        mo_ref[...] = mn
        vo_ref[...] = vn

    blk = (pl.BlockSpec((None, tr, tc), lambda i, j: (0, i, j)) if lead else pl.BlockSpec((tr, tc), lambda i, j: (i, j)))
    outs = pl.pallas_call(
        body, name=name, grid=(rows // tr, cols // tc), in_specs=[blk] * 4, out_specs=[blk] * 3,
        out_shape=[jax.ShapeDtypeStruct(w2.shape, F32)] * 3,
        compiler_params=_cparams(("parallel", "parallel")))(w2, g2, m2, v2)
    return tuple(o.reshape(shape) for o in outs)


_WEIGHT_NAMES = ('w_in', 'conv_w', 'dn_a_log', 'dn_dt_bias', 'dn_norm_w', 'q_norm_w', 'w_uq', 'kv_norm_w', 'w_uk',
                 'w_uv', 'w_br_dn', 'w_br_mla', 'w_o', 'ln1_g', 'ln1_b', 'w_ffn_in', 'w_ffn_out', 'w_ple',
                 'w_ple_gate', 'ln2_g', 'ln2_b')
_SMALL_NAMES = ('ln1_g', 'ln1_b', 'ln2_g', 'ln2_b', 'q_norm_w', 'kv_norm_w', 'dn_norm_w', 'dn_a_log', 'dn_dt_bias')
_SMALL_GROUP = 8
_CONV_SMALL_ROW = len(_SMALL_NAMES) * _SMALL_GROUP
_CONV_SMALL_ROWS = DN_CONV * QKV_W // FLAT_W


def _pack_small(gw):
    rows = [jnp.pad(gw[n][None, :], ((0, _SMALL_GROUP - 1), (0, FLAT_W - gw[n].shape[0]))) for n in _SMALL_NAMES]
    rows.append(jnp.pad(gw['conv_w'].reshape(_CONV_SMALL_ROWS, FLAT_W), ((0, SMALL_ROWS - _CONV_SMALL_ROW - _CONV_SMALL_ROWS), (0, 0))))
    return jnp.concatenate(rows, axis=0)


class _Exchange:
    def __init__(self, local, me_chip, c_arr):
        self.local, self.me_chip, self.c_arr = local, me_chip, c_arr
        self.parts = self.arrived = None

    def gather_send(self):
        return _ride_gather_send(self.local)

    def gather_pass(self, sent):
        return _ride_gather_pass(sent)

    def weights(self, gathered):
        return _prep_weights(_unpack_rest(gathered, self.local, self.me_chip))

    def pair_send(self, g):
        self.gbufs = _pack_grads_rest(_unprep_grads_early(g))
        return _ride_pair_exchange(self.gbufs)

    def reduce_send(self, got):
        self.parts = [_pair_add(g_, r_, self.c_arr, tr, "pair_add_%d" % (i + 1))
                      for i, (g_, r_, tr) in enumerate(zip(self.gbufs, got, _ADD_TILES[1:]))]
        return _ride_chip_exchange(self.parts)

    def reduce_arrived(self, arrived):
        self.arrived = list(arrived)

    def in_send(self, g):
        g_in = [_shard_columns(_w_in_grad(g), W_IN_SHARD)]
        got = _reduce_pair_exchange(g_in, "reduce_pair_exchange_w_in")
        self.part_in = _pair_add(g_in[0], got[0], self.c_arr, _ADD_TILES[0], "pair_add_0")
        return _ride_chip_exchange([self.part_in])

    def in_arrived(self, arrived):
        self.arrived_in = list(arrived)


def kernel(x, p, positions, w_in, conv_w, dn_a_log, dn_dt_bias, dn_norm_w, q_norm_w, w_uq, kv_norm_w, w_uk, w_uv, w_br_dn, w_br_mla, w_o, ln1_g, ln1_b, w_ffn_in, w_ffn_out, w_ple, w_ple_gate, ln2_g, ln2_b, loss_target, m_w_in, m_conv_w, m_dn_a_log, m_dn_dt_bias, m_dn_norm_w, m_q_norm_w, m_w_uq, m_kv_norm_w, m_w_uk, m_w_uv, m_w_br_dn, m_w_br_mla, m_w_o, m_ln1_g, m_ln1_b, m_w_ffn_in, m_w_ffn_out, m_w_ple, m_w_ple_gate, m_ln2_g, m_ln2_b, v_w_in, v_conv_w, v_dn_a_log, v_dn_dt_bias, v_dn_norm_w, v_q_norm_w, v_w_uq, v_kv_norm_w, v_w_uk, v_w_uv, v_w_br_dn, v_w_br_mla, v_w_o, v_ln1_g, v_ln1_b, v_w_ffn_in, v_w_ffn_out, v_w_ple, v_w_ple_gate, v_ln2_g, v_ln2_b):
    ws = dict(w_in=w_in, conv_w=conv_w, dn_a_log=dn_a_log, dn_dt_bias=dn_dt_bias, dn_norm_w=dn_norm_w, q_norm_w=q_norm_w,
              w_uq=w_uq, kv_norm_w=kv_norm_w, w_uk=w_uk, w_uv=w_uv, w_br_dn=w_br_dn, w_br_mla=w_br_mla, w_o=w_o,
              ln1_g=ln1_g, ln1_b=ln1_b, w_ffn_in=w_ffn_in, w_ffn_out=w_ffn_out, w_ple=w_ple, w_ple_gate=w_ple_gate,
              ln2_g=ln2_g, ln2_b=ln2_b)
    ms = dict(w_in=m_w_in, conv_w=m_conv_w, dn_a_log=m_dn_a_log, dn_dt_bias=m_dn_dt_bias, dn_norm_w=m_dn_norm_w,
              q_norm_w=m_q_norm_w, w_uq=m_w_uq, kv_norm_w=m_kv_norm_w, w_uk=m_w_uk, w_uv=m_w_uv, w_br_dn=m_w_br_dn,
              w_br_mla=m_w_br_mla, w_o=m_w_o, ln1_g=m_ln1_g, ln1_b=m_ln1_b, w_ffn_in=m_w_ffn_in, w_ffn_out=m_w_ffn_out,
              w_ple=m_w_ple, w_ple_gate=m_w_ple_gate, ln2_g=m_ln2_g, ln2_b=m_ln2_b)
    vs = dict(w_in=v_w_in, conv_w=v_conv_w, dn_a_log=v_dn_a_log, dn_dt_bias=v_dn_dt_bias, dn_norm_w=v_dn_norm_w,
              q_norm_w=v_q_norm_w, w_uq=v_w_uq, kv_norm_w=v_kv_norm_w, w_uk=v_w_uk, w_uv=v_w_uv, w_br_dn=v_w_br_dn,
              w_br_mla=v_w_br_mla, w_o=v_w_o, ln1_g=v_ln1_g, ln1_b=v_ln1_b, w_ffn_in=v_w_ffn_in, w_ffn_out=v_w_ffn_out,
              w_ple=v_w_ple, w_ple_gate=v_w_ple_gate, ln2_g=v_ln2_g, ln2_b=v_ln2_b)
    mx, my, mc = lax.axis_index("x"), lax.axis_index("y"), lax.axis_index("c")

    me_chip = 2 * mx + my
    c_arr = jnp.reshape(mc, (1,)).astype(jnp.int32)
    me_arr = jnp.reshape(me_chip, (1,)).astype(jnp.int32)
    sharded = ('w_in', 'w_ffn_in') + tuple(name for name, _, _ in _FLATB_PIECES)

    local = _pack_shards({name: ws[name][0] for name in sharded}, conv_w[0])
    (gathered_in,) = _gather_shards(local[:1], "gather_w_in")
    w_in_full, conv_full = _unpack_w_in(gathered_in, local[0], me_chip)
    small = {n: ws[n][0] for n in _SMALL_NAMES}
    small['conv_w'] = conv_full
    sp = _prep_small(small)
    cosb, sinb = _rope_tables(positions[0])
    exch = _Exchange(local[1:], me_chip, c_arr)

    loss_lanes, dx, g = _local_step(x[0], p[0, 0], cosb, sinb, loss_target[0], _prep_w_in(w_in_full), sp, exch)
    gw = _unprep_grads_late(g)
    loss = lax.psum(jnp.sum(loss_lanes), ("x", "y", "c"))

    parts = [exch.part_in] + exch.parts
    arrived = exch.arrived_in + exch.arrived
    mine = [_chip_add(p_, r_, me_arr, tr, "chip_add_%d" % i) for i, (p_, r_, tr) in enumerate(zip(parts, arrived, _ADD_TILES))]
    reduced = _unpack_reduced(mine, _reduce_pair_share(mine, "reduce_pair_share"), mc)
    tot = _small_allreduce(_pack_small(gw))
    gred = {name: reduced[name][None] for name in sharded}
    for i, n in enumerate(_SMALL_NAMES):
        gred[n] = tot[i * _SMALL_GROUP, :ws[n].shape[1]][None]
    conv_tot = tot[_CONV_SMALL_ROW:_CONV_SMALL_ROW + _CONV_SMALL_ROWS].reshape(DN_CONV, QKV_W)
    gred['conv_w'] = lax.dynamic_slice_in_dim(conv_tot, (2 * mx + my) * _CONV_SHARD, _CONV_SHARD, axis=1)[None]

    deltas, new_m, new_v = {}, {}, {}
    for n in _WEIGHT_NAMES:
        if n == 'w_in':
            tr_ = lambda a: jnp.transpose(a, (0, 2, 1))
            g_t = tr_(gred[n].reshape(ws[n].shape))
            outs = _adamw(tr_(ws[n]), g_t, tr_(ms[n]), tr_(vs[n]), "adamw_" + n)
            gred[n] = tr_(g_t)
            deltas[n], new_m[n], new_v[n] = (tr_(o) for o in outs)
            continue
        gred[n] = gred[n].reshape(ws[n].shape)
        deltas[n], new_m[n], new_v[n] = _adamw(ws[n], gred[n], ms[n], vs[n], "adamw_" + n)
    return (loss, dx[None], *[gred[n] for n in _WEIGHT_NAMES], *[deltas[n] for n in _WEIGHT_NAMES],
            *[new_m[n] for n in _WEIGHT_NAMES], *[new_v[n] for n in _WEIGHT_NAMES])
```

```python
import functools
import math

import jax
import jax.numpy as jnp
from jax import lax
from jax.experimental import pallas as pl
from jax.experimental.pallas import tpu as pltpu

F32 = jnp.float32
_CDT = jnp.bfloat16
_HI = lax.Precision.HIGHEST
_MESH = pl.DeviceIdType.MESH

D_MODEL = 1024
PLE_DIM = 256
HEADS = 8
DN_DK = 128
DN_CHUNK = 64
DN_CONV = 4
QKV_W = 3 * HEADS * DN_DK
Q_LORA = 384
KV_LORA = 256
NOPE = 128
ROPE = 64
ROPE_PAD = 128
FFN_HIDDEN = 2816
D_IN = 6864
ROPE_BASE = 10000.0
ALPHA = 2.0 ** 0.25
ATT_SCALE = (NOPE + ROPE) ** -0.5
NEG_BIG = -1e30
ADAM_LR, ADAM_B1, ADAM_B2, ADAM_EPS, ADAM_WD, ADAM_STEP = 0.001, 0.9, 0.999, 1e-08, 0.01, 10

LANE = 128
VMEM_LIMIT = 56 * 1024 * 1024
MM_VMEM_BUDGET = 40 * 1024 * 1024
N_SHARD = 4
FLAT_W = 1024
SMALL_ROWS = 88


def _tile(dim, cap):
    if dim <= cap:
        return dim
    t = (cap // LANE) * LANE
    while t >= LANE:
        if dim % t == 0:
            return t
        t -= LANE
    return dim


def _cparams(sem):
    return pltpu.CompilerParams(dimension_semantics=sem, vmem_limit_bytes=VMEM_LIMIT)


def _mm(a, b, *, name, ta=False, tb=False, add=None, add_scale=1.0, out_dtype=F32, heads=None,
        a_head=None, b_head=None, out_head=None, dims=None, tm=1408, tn=1408):
    m, n, k = dims
    tm, tn = _tile(m, tm), _tile(n, tn)
    sa, sb, so = a.dtype.itemsize, b.dtype.itemsize, jnp.dtype(out_dtype).itemsize

    def vmem_need(tk_):
        acc = tm * tn * 4 if tk_ < k else 0
        extra = 2 * tm * tn * 4 if add is not None else 0
        return 2 * (tm * tk_ * sa + tk_ * tn * sb) + 2 * tm * tn * so + acc + extra

    tk = k
    while vmem_need(tk) > MM_VMEM_BUDGET and tk > LANE:
        smaller = _tile(k, tk - LANE)
        if smaller >= tk:
            break
        tk = smaller
    nk = k // tk
    hgrid = () if heads is None else (heads,)
    off = len(hgrid)

    def spec(rows, cols, rtile, ctile, rsel, csel, layout):
        def idx(*g):
            h = g[0] if off else 0
            ri, ci = g[off + rsel], g[off + csel]
            if layout == 'lead':
                return (h, ri, ci)
            if layout == 'col':
                return (ri, h * (cols // ctile) + ci)
            return (ri, ci)
        if layout == 'lead':
            return pl.BlockSpec((None, rtile, ctile), idx)
        return pl.BlockSpec((rtile, ctile), idx)

    a_spec = spec(k, m, tk, tm, 2, 0, a_head) if ta else spec(m, k, tm, tk, 0, 2, a_head)
    b_spec = spec(n, k, tn, tk, 1, 2, b_head) if tb else spec(k, n, tk, tn, 2, 1, b_head)
    o_spec = spec(m, n, tm, tn, 0, 1, out_head)
    in_specs = [a_spec, b_spec]
    args = [a, b]
    if add is not None:
        in_specs.append(spec(m, n, tm, tn, 0, 1, out_head))
        args.append(add)
    dn = (((0 if ta else 1,), (1 if tb else 0,)), ((), ()))

    def body(*refs):
        a_ref, b_ref = refs[0], refs[1]
        prod = lax.dot_general(a_ref[...].astype(_CDT), b_ref[...].astype(_CDT), dn, preferred_element_type=F32)
        if nk == 1:
            o_ref = refs[-1]
            if add is not None:
                prod = prod + refs[2][...].astype(F32) * add_scale
            o_ref[...] = prod.astype(out_dtype)
            return
        o_ref, acc_ref = refs[-2], refs[-1]
        kk = pl.program_id(off + 2)

        @pl.when(kk == 0)
        def _():
            if add is not None:
                acc_ref[...] = refs[2][...].astype(F32) * add_scale
            else:
                acc_ref[...] = jnp.zeros_like(acc_ref)

        acc_ref[...] += prod

        @pl.when(kk == nk - 1)
        def _():
            o_ref[...] = acc_ref[...].astype(out_dtype)

    if out_head == 'lead':
        oshape = (heads, m, n)
    elif out_head == 'col':
        oshape = (m, heads * n)
    else:
        oshape = (m, n)
    sem = ("parallel",) * (off + 2) + ("arbitrary",)
    return pl.pallas_call(
        body, name=name, grid=hgrid + (m // tm, n // tn, nk), in_specs=in_specs, out_specs=o_spec,
        out_shape=jax.ShapeDtypeStruct(oshape, out_dtype),
        scratch_shapes=[pltpu.VMEM((tm, tn), F32)] if nk > 1 else [],
        compiler_params=_cparams(sem))(*args)


def _mm2(a, b, **kw):
    ta, tb = kw.get('ta', False), kw.get('tb', False)
    m = a.shape[1] if ta else a.shape[0]
    k = a.shape[0] if ta else a.shape[1]
    n = b.shape[0] if tb else b.shape[1]
    return _mm(a, b, dims=(m, n, k), **kw)


def _rowwise(fn, rows, bcast, outs, reds=(), *, name, tm=256, heads=None):
    t = rows[0][0].shape[0]
    tm = min(tm, t)
    hn = 1 if heads is None else heads
    in_specs, args = [], []
    for arr, width, base, per_head in rows:
        in_specs.append(pl.BlockSpec((tm, width), functools.partial(
            lambda i, h, base, per_head: (i, base + (h if per_head else 0)), base=base, per_head=per_head)))
        args.append(arr)
    for arr in bcast:
        in_specs.append(pl.BlockSpec(arr.shape, lambda i, h: (0, 0)))
        args.append(arr)
    out_specs, out_shape = [], []
    for total, width, per_head, dt in outs:
        out_specs.append(pl.BlockSpec((tm, width), functools.partial(
            lambda i, h, per_head: (i, h if per_head else 0), per_head=per_head)))
        out_shape.append(jax.ShapeDtypeStruct((t, total), dt))
    for shp in reds:
        out_specs.append(pl.BlockSpec(shp, lambda i, h: (0, 0)))
        out_shape.append(jax.ShapeDtypeStruct(shp, F32))
    n_in, n_out, n_red = len(args), len(outs), len(reds)

    def body(*refs):
        i, h = pl.program_id(0), pl.program_id(1)
        vals = fn(h, *[r[...] for r in refs[:n_in]])
        for r, v in zip(refs[n_in:n_in + n_out], vals[:n_out]):
            r[...] = v.astype(r.dtype)
        if n_red:
            @pl.when((i == 0) & (h == 0))
            def _():
                for r in refs[n_in + n_out:]:
                    r[...] = jnp.zeros_like(r)
            for r, v in zip(refs[n_in + n_out:], vals[n_out:]):
                r[...] += v

    sem = ("arbitrary", "arbitrary") if n_red else ("parallel", "parallel")
    res = pl.pallas_call(body, name=name, grid=(t // tm, hn), in_specs=in_specs, out_specs=out_specs,
                         out_shape=out_shape, compiler_params=_cparams(sem))(*args)
    return tuple(res)


def _sigmoid(x):
    return 1.0 / (1.0 + jnp.exp(-x))


def _silu(x):
    return x * _sigmoid(x)


def _softplus(x):
    return jnp.maximum(x, 0.0) + jnp.log(1.0 + jnp.exp(-jnp.abs(x)))


def _layer_norm(t, g, b):
    mu = jnp.mean(t, axis=-1, keepdims=True)
    d = t - mu
    var = jnp.mean(d * d, axis=-1, keepdims=True)
    return d * lax.rsqrt(var + 1e-5) * g + b


def _rms_norm(t, w):
    return t * lax.rsqrt(jnp.mean(t * t, axis=-1, keepdims=True) + 1e-6) * w


def _swap_rope_halves(t):
    lane = lax.broadcasted_iota(jnp.int32, t.shape, 1) % ROPE_PAD
    n = t.shape[1]
    up = pltpu.roll(t, n - ROPE // 2, axis=1)
    dn = pltpu.roll(t, ROPE // 2, axis=1)
    return jnp.where(lane < ROPE // 2, up, jnp.where(lane < ROPE, dn, 0.0))


def _rope(t, cosb, sinb):
    reps = t.shape[1] // ROPE_PAD
    c = jnp.tile(cosb, (1, reps)) if reps > 1 else cosb
    s = jnp.tile(sinb, (1, reps)) if reps > 1 else sinb
    return t * c + _swap_rope_halves(t) * s


def _rope_bwd(d, cosb, sinb):
    reps = d.shape[1] // ROPE_PAD
    c = jnp.tile(cosb, (1, reps)) if reps > 1 else cosb
    s = jnp.tile(sinb, (1, reps)) if reps > 1 else sinb
    return d * c + _swap_rope_halves(d * s)


_CONV_ROWS = 256
_CONV_COLS = 256


def _conv_window(ref, r0, lo, hi, t):
    parts = []
    start, stop = r0 - lo, r0 + _CONV_ROWS + hi
    if start < 0:
        parts.append(jnp.zeros((-start, ref.shape[1]), F32))
        start = 0
    tail = max(stop - t, 0)
    parts.append(ref[start:stop - tail, :].astype(F32))
    if tail:
        parts.append(jnp.zeros((tail, ref.shape[1]), F32))
    return parts[0] if len(parts) == 1 else jnp.concatenate(parts, axis=0)


def _conv_taps(win, w_ref, n_out):
    acc = win[8:8 + n_out] * w_ref[DN_CONV - 1:DN_CONV, :]
    for i in range(DN_CONV - 1):
        acc = acc + pltpu.roll(win, DN_CONV - 1 - i, axis=0)[8:8 + n_out] * w_ref[i:i + 1, :]
    return acc


def _conv_silu(x, w):
    t, ch = x.shape

    def body(x_ref, w_ref, o_ref):
        for r in range(t // _CONV_ROWS):
            r0 = r * _CONV_ROWS
            c = _conv_taps(_conv_window(x_ref, r0, 8, 0, t), w_ref, _CONV_ROWS)
            o_ref[r0:r0 + _CONV_ROWS, :] = _silu(c)

    return pl.pallas_call(
        body, name="conv_silu", grid=(ch // _CONV_COLS,),
        in_specs=[pl.BlockSpec((t, _CONV_COLS), lambda j: (0, j)), pl.BlockSpec((DN_CONV, _CONV_COLS), lambda j: (0, j))],
        out_specs=pl.BlockSpec((t, _CONV_COLS), lambda j: (0, j)),
        out_shape=jax.ShapeDtypeStruct((t, ch), F32), compiler_params=_cparams(("parallel",)))(x, w)


def _conv_silu_bwd(x, w, dys):
    t, ch = x.shape
    per = ch // len(dys) // _CONV_COLS

    def body(x_ref, w_ref, *rest):
        dy_refs, (dx_ref, dw_ref) = rest[:len(dys)], rest[len(dys):]
        sec = pl.program_id(0) // per
        dws = [jnp.zeros((1, _CONV_COLS), F32) for _ in range(DN_CONV)]
        for r in range(t // _CONV_ROWS):
            r0 = r * _CONV_ROWS
            n_ext = _CONV_ROWS + 8
            xw = _conv_window(x_ref, r0, 8, 8, t)
            c = _conv_taps(xw, w_ref, n_ext)
            sg = _sigmoid(c)
            dy = _conv_window(dy_refs[-1], r0, 0, 8, t)
            for k in range(len(dys) - 2, -1, -1):
                dy = jnp.where(sec == k, _conv_window(dy_refs[k], r0, 0, 8, t), dy)
            ds = dy * (sg * (1.0 + c * (1.0 - sg)))
            x0 = xw[8:8 + _CONV_ROWS]
            dx = jnp.zeros((_CONV_ROWS, _CONV_COLS), F32)
            for i in range(DN_CONV):
                sh = DN_CONV - 1 - i
                ds_up = (ds if sh == 0 else pltpu.roll(ds, n_ext - sh, axis=0))[:_CONV_ROWS]
                dx = dx + ds_up * w_ref[i:i + 1, :]
                dws[i] = dws[i] + jnp.sum(x0 * ds_up, axis=0, keepdims=True)
            dx_ref[r0:r0 + _CONV_ROWS, :] = dx.astype(dx_ref.dtype)
        for i in range(DN_CONV):
            dw_ref[i:i + 1, :] = dws[i]

    blk = pl.BlockSpec((t, _CONV_COLS), lambda j: (0, j))
    wblk = pl.BlockSpec((DN_CONV, _CONV_COLS), lambda j: (0, j))
    dy_specs = [pl.BlockSpec((t, _CONV_COLS), functools.partial(lambda j, k: (0, jnp.clip(j - k * per, 0, per - 1)), k=k))
                for k in range(len(dys))]
    return pl.pallas_call(
        body, name="conv_silu_bwd", grid=(ch // _CONV_COLS,), in_specs=[blk, wblk] + dy_specs, out_specs=[blk, wblk],
        out_shape=[jax.ShapeDtypeStruct((t, ch), _CDT), jax.ShapeDtypeStruct((DN_CONV, ch), F32)],
        compiler_params=_cparams(("arbitrary",)))(x, w, *dys)


_PA_ROWS = 512


def _bmm(a, b, spec, exact=False):
    if exact:
        return jnp.einsum(spec, a, b, precision=_HI, preferred_element_type=F32)
    return jnp.einsum(spec, a.astype(_CDT), b.astype(_CDT), preferred_element_type=F32)


def _split16(a):
    hi = a.astype(jnp.bfloat16)
    return hi, (a - hi.astype(F32)).astype(jnp.bfloat16)


def _bmm3(a, b, spec):
    ah, al = _split16(a)
    bh, bl = _split16(b)
    e = lambda p, q: jnp.einsum(spec, p, q, preferred_element_type=F32)
    return e(ah, bh) + (e(ah, bl) + e(al, bh))


def _split3(b):
    b0 = b.astype(jnp.bfloat16)
    r1 = b - b0.astype(F32)
    b1 = r1.astype(jnp.bfloat16)
    return b0, b1, (r1 - b1.astype(F32)).astype(jnp.bfloat16)


@functools.partial(jax.custom_vjp, nondiff_argnums=(2, 3))
def _select_mm(sel, b, spec, spec_t):
    return sum(jnp.einsum(spec, sel, t, preferred_element_type=F32) for t in _split3(b))


def _select_mm_fwd(sel, b, spec, spec_t):
    return _select_mm(sel, b, spec, spec_t), sel


def _select_mm_bwd(spec, spec_t, sel, ct):
    return jnp.zeros_like(sel), sum(jnp.einsum(spec_t, sel, t, preferred_element_type=F32) for t in _split3(ct))


_select_mm.defvjp(_select_mm_fwd, _select_mm_bwd)


def _tri_inverse(l_mat, eye):
    pw = -l_mat
    t_inv = eye + pw
    for _ in range(5):
        pw = _bmm3(pw, pw, 'bij,bjk->bik')
        t_inv = t_inv + _bmm3(t_inv, pw, 'bij,bjk->bik')
    return t_inv


@jax.custom_vjp
def _tri_inverse_saved(l_mat, t_saved):
    return t_saved


def _tri_inverse_saved_fwd(l_mat, t_saved):
    return t_saved, t_saved


def _tri_inverse_saved_bwd(t_saved, dt):
    left = _bmm3(t_saved, dt, 'bji,bjk->bik')
    return -_bmm3(left, t_saved, 'bij,bkj->bik'), jnp.zeros_like(t_saved)


_tri_inverse_saved.defvjp(_tri_inverse_saved_fwd, _tri_inverse_saved_bwd)


def _phase_a(h, q, k, v, ba, alog, dtb, t_saved=None):
    r = q.shape[0]
    nb = r // DN_CHUNK
    c = DN_CHUNK
    lane = lax.broadcasted_iota(jnp.int32, (1, LANE), 1)
    selb = (lane == h).astype(F32)
    sela = (lane == h + HEADS).astype(F32)
    b_raw = jnp.sum(ba * selb, axis=1, keepdims=True)
    a_raw = jnp.sum(ba * sela, axis=1, keepdims=True)
    al = jnp.sum(alog * selb, axis=1, keepdims=True)
    dt = jnp.sum(dtb * selb, axis=1, keepdims=True)
    beta = jnp.broadcast_to(_sigmoid(b_raw), (r, LANE))
    g = jnp.broadcast_to(-jnp.exp(al) * _softplus(a_raw + dt), (r, LANE))
    qn = q * lax.rsqrt(jnp.sum(q * q, -1, keepdims=True) + 1e-6) * (DN_DK ** -0.5)
    kn = k * lax.rsqrt(jnp.sum(k * k, -1, keepdims=True) + 1e-6)
    q3, k3, v3 = qn.reshape(nb, c, LANE), kn.reshape(nb, c, LANE), v.reshape(nb, c, LANE)
    b3, g3 = beta.reshape(nb, c, LANE), g.reshape(nb, c, LANE)
    ri = lax.broadcasted_iota(jnp.int32, (nb, c, c), 1)
    ci = lax.broadcasted_iota(jnp.int32, (nb, c, c), 2)
    tril, strict = ri >= ci, ri > ci
    gc = _select_mm(tril.astype(jnp.bfloat16), g3, 'bij,bjd->bid', 'bij,bid->bjd')
    onehot = (lax.broadcasted_iota(jnp.int32, (nb, c, LANE), 2) == 0).astype(jnp.bfloat16)
    g_row = _select_mm(onehot, gc, 'bid,bjd->bij', 'bid,bij->bjd')
    diff = gc[:, :, :c] - g_row
    decay = jnp.where(tril, jnp.exp(jnp.where(tril, diff, 0.0)), 0.0)
    kb = k3 * b3
    l_mat = jnp.where(strict, _bmm(kb, k3, 'bid,bjd->bij') * decay, 0.0)
    if t_saved is None:
        t_inv = _tri_inverse(l_mat, (ri == ci).astype(F32))
    else:
        t_inv = _tri_inverse_saved(l_mat, t_saved.reshape(nb, c, c))
    eg = jnp.exp(gc)
    u = _bmm(t_inv, v3 * b3, 'bij,bje->bie')
    w = _bmm(t_inv, kb * eg, 'bij,bje->bie')
    intra = jnp.where(tril, _bmm(q3, k3, 'bid,bjd->bij') * decay, 0.0)
    qd = q3 * eg
    gl = jnp.sum(g3, axis=1, keepdims=True)
    kt = k3 * jnp.exp(gl - gc)
    outs = (u.reshape(r, LANE), w.reshape(r, LANE), qd.reshape(r, LANE), kt.reshape(r, LANE),
            intra.reshape(r, c), gl.reshape(nb, LANE))
    if t_saved is not None:
        return outs
    qd2 = qd - _bmm(intra, w, 'bij,bjd->bid')
    au = _bmm(intra, u, 'bij,bje->bie')
    return outs + (t_inv.reshape(r, c), qd2.reshape(r, LANE), au.reshape(r, LANE))


def _pa_specs(t):
    rr = min(_PA_ROWS, t)
    nb = rr // DN_CHUNK
    qkv = [pl.BlockSpec((rr, LANE), functools.partial(lambda i, h, o: (i, o + h), o=o)) for o in (0, HEADS, 2 * HEADS)]
    ba = pl.BlockSpec((rr, LANE), lambda i, h: (i, 3))
    vec = pl.BlockSpec((1, LANE), lambda i, h: (0, 0))
    row = pl.BlockSpec((rr, LANE), lambda i, h: (i, h))
    intra = pl.BlockSpec((None, rr, DN_CHUNK), lambda i, h: (h, i, 0))
    gl = pl.BlockSpec((nb, LANE), lambda i, h: (i, h))
    return rr, qkv, ba, vec, row, intra, gl


def _grid_ends(grid):
    first = lambda: functools.reduce(jnp.logical_and, [pl.program_id(a) == 0 for a in range(len(grid))])
    last = lambda: functools.reduce(jnp.logical_and, [pl.program_id(a) == n - 1 for a, n in enumerate(grid)])
    return first, last


def _delta_local(qkv_act, pm, alog, dtb, rider=None):
    t = qkv_act.shape[0]
    rr, qkv, ba, vec, row, intra, gl = _pa_specs(t)

    def body(q, k, v, b, al, dt, *outs):
        vals = _phase_a(pl.program_id(1), q[...], k[...], v[...], b[...], al[...], dt[...])
        for o, val in zip(outs, vals):
            o[...] = val

    wide = jax.ShapeDtypeStruct((t, HEADS * LANE), F32)
    sq = jax.ShapeDtypeStruct((HEADS, t, DN_CHUNK), F32)
    grid = (t // rr, HEADS)
    return _carried_call(
        body, rider, *_grid_ends(grid), name="delta_local", grid=grid, in_specs=qkv + [ba, vec, vec],
        out_specs=[row] * 4 + [intra, gl, intra, row, row],
        out_shape=[wide] * 4 + [sq, jax.ShapeDtypeStruct((t // DN_CHUNK, HEADS * LANE), F32), sq, wide, wide],
        scratch_shapes=[], sem=("arbitrary", "arbitrary"), args=(qkv_act, qkv_act, qkv_act, pm, alog, dtb))


def _delta_local_bwd(qkv_act, pm, alog, dtb, t_inv, du, dw, dqd, dkt, dintra, dgl, rider=None):
    t = qkv_act.shape[0]
    rr, qkv, ba, vec, row, intra, gl = _pa_specs(t)

    def body(q, k, v, b, al, dt, ti, du_r, dw_r, dqd_r, dkt_r, di_r, dgl_r, dq_o, dk_o, dv_o, dba_o, dal_o, ddt_o):
        i, h = pl.program_id(0), pl.program_id(1)
        t_saved = ti[...]
        _, vjp = jax.vjp(lambda *a: _phase_a(h, *a, t_saved=t_saved), q[...], k[...], v[...], b[...], al[...], dt[...])
        dq, dk, dv, dba, dal, ddt = vjp((du_r[...], dw_r[...], dqd_r[...], dkt_r[...], di_r[...], dgl_r[...]))
        dq_o[...], dk_o[...], dv_o[...] = dq, dk, dv

        @pl.when(h == 0)
        def _():
            dba_o[...] = jnp.zeros_like(dba_o)

        @pl.when((h == 0) & (i == 0))
        def _():
            dal_o[...] = jnp.zeros_like(dal_o)
            ddt_o[...] = jnp.zeros_like(ddt_o)

        dba_o[...] += dba
        dal_o[...] += dal
        ddt_o[...] += ddt

    wide = jax.ShapeDtypeStruct((t, HEADS * LANE), F32)
    vshape = jax.ShapeDtypeStruct((1, LANE), F32)
    grid = (t // rr, HEADS)
    return _carried_call(
        body, rider, *_grid_ends(grid), name="delta_local_bwd", grid=grid,
        in_specs=qkv + [ba, vec, vec, intra] + [row] * 4 + [intra, gl],
        out_specs=[row] * 3 + [pl.BlockSpec((rr, LANE), lambda i, h: (i, 0)), vec, vec],
        out_shape=[wide] * 3 + [jax.ShapeDtypeStruct((t, LANE), F32), vshape, vshape],
        scratch_shapes=[], sem=("arbitrary", "arbitrary"),
        args=(qkv_act, qkv_act, qkv_act, pm, alog, dtb, t_inv, du, dw, dqd, dkt, dintra, dgl))


_SCAN_ROWS = 512


def _dot(a, b, dn):
    return lax.dot_general(a.astype(_CDT), b.astype(_CDT), (dn, ((), ())), preferred_element_type=F32)


_NN = ((1,), (0,))
_NT = ((1,), (1,))
_TN = ((0,), (0,))


def _delta_scan(u, w, qd, kt, au, gl, rider=None):
    t = u.shape[0]
    rr = min(_SCAN_ROWS, t)
    nc = rr // DN_CHUNK

    def body(u_ref, w_ref, qd_ref, kt_ref, au_ref, gl_ref, o_ref, sall_ref, s_scr):
        @pl.when(pl.program_id(0) == 0)
        def _():
            s_scr[...] = jnp.zeros_like(s_scr)

        def chunk(c, carry):
            r0 = pl.multiple_of(c * DN_CHUNK, DN_CHUNK)
            rows = pl.ds(r0, DN_CHUNK)
            e = jnp.exp(gl_ref[pl.ds(c, 1), :])
            states = [s_scr[h] for h in range(HEADS)]
            u_c, w_c, qd_c, kt_c, au_c = u_ref[rows, :], w_ref[rows, :], qd_ref[rows, :], kt_ref[rows, :], au_ref[rows, :]
            o_new, s_new = [], []
            for h in range(HEADS):
                cs = slice(h * LANE, (h + 1) * LANE)
                s = states[h]
                both = _dot(jnp.concatenate([w_c[:, cs], qd_c[:, cs]], axis=0), s, _NN)
                v_new = u_c[:, cs] - both[:DN_CHUNK]
                o_new.append(both[DN_CHUNK:] + au_c[:, cs])
                s_new.append(s * e[:, cs] + _dot(kt_c[:, cs], v_new, _TN))
            o_ref[rows, :] = jnp.concatenate(o_new, axis=1)
            for h in range(HEADS):
                sall_ref[c, h] = states[h]
                s_scr[h] = s_new[h]
            return carry

        lax.fori_loop(0, nc, chunk, 0)

    row = pl.BlockSpec((rr, HEADS * LANE), lambda i: (i, 0))
    grid = (t // rr,)
    return _carried_call(
        body, rider, *_grid_ends(grid), name="delta_scan", grid=grid,
        in_specs=[row] * 5 + [pl.BlockSpec((nc, HEADS * LANE), lambda i: (i, 0))],
        out_specs=[row, pl.BlockSpec((nc, HEADS, LANE, LANE), lambda i: (i, 0, 0, 0))],
        out_shape=[jax.ShapeDtypeStruct((t, HEADS * LANE), F32),
                   jax.ShapeDtypeStruct((t // DN_CHUNK, HEADS, LANE, LANE), F32)],
        scratch_shapes=[pltpu.VMEM((HEADS, LANE, LANE), F32)], sem=("arbitrary",), args=(u, w, qd, kt, au, gl))


def _delta_scan_bwd(u, w, qd, kt, intra, gl, sall, do, rider=None):
    t = u.shape[0]
    rr = min(_SCAN_ROWS, t)
    nc = rr // DN_CHUNK
    ng = t // rr

    def body(u_ref, w_ref, qd_ref, kt_ref, a_ref, gl_ref, sall_ref, do_ref,
             du_ref, dw_ref, dqd_ref, dkt_ref, da_ref, dgl_ref, ds_scr):
        @pl.when(pl.program_id(0) == 0)
        def _():
            ds_scr[...] = jnp.zeros_like(ds_scr)

        def chunk(cc, carry):
            c = nc - 1 - cc
            r0 = pl.multiple_of(c * DN_CHUNK, DN_CHUNK)
            rows = pl.ds(r0, DN_CHUNK)
            e = jnp.exp(gl_ref[pl.ds(c, 1), :])
            states = [sall_ref[c, h] for h in range(HEADS)]
            ds_outs = [ds_scr[h] for h in range(HEADS)]
            u_a, w_a, kt_a, qd_a, do_a = u_ref[rows, :], w_ref[rows, :], kt_ref[rows, :], qd_ref[rows, :], do_ref[rows, :]
            a_a = [a_ref[h, rows, :] for h in range(HEADS)]
            da, dqd, dkt, du, dw, dgl, ds_new = [], [], [], [], [], [], []
            for h in range(HEADS):
                cs = slice(h * LANE, (h + 1) * LANE)
                s, ds_out = states[h], ds_outs[h]
                w_c, kt_c, qd_c, do_c = w_a[:, cs], kt_a[:, cs], qd_a[:, cs], do_a[:, cs]
                v_new = u_a[:, cs] - _dot(w_c, s, _NN)
                dv_new = _dot(a_a[h], do_c, _TN) + _dot(kt_c, ds_out, _NN)
                cots = jnp.concatenate([do_c, dv_new], axis=0)
                both = _dot(cots, s, _NT)
                dqd.append(both[:DN_CHUNK])
                dw.append(-both[DN_CHUNK:])
                da.append(_dot(do_c, v_new, _NT))
                dkt.append(_dot(v_new, ds_out, _NT))
                du.append(dv_new)
                eh = e[:, cs]
                dgl.append(jnp.broadcast_to(jnp.sum(ds_out * s, axis=0, keepdims=True) * eh, (8, LANE)))
                ds_new.append(ds_out * eh + _dot(jnp.concatenate([qd_c, -w_c], axis=0), cots, _TN))
            cat = lambda parts: jnp.concatenate(parts, axis=1)
            dqd_ref[rows, :], dkt_ref[rows, :], du_ref[rows, :], dw_ref[rows, :] = cat(dqd), cat(dkt), cat(du), cat(dw)
            dgl_ref[pl.ds(pl.multiple_of(c * 8, 8), 8), :] = cat(dgl)
            for h in range(HEADS):
                da_ref[h, rows, :] = da[h]
                ds_scr[h] = ds_new[h]
            return carry

        lax.fori_loop(0, nc, chunk, 0)

    rev = lambda i: (ng - 1 - i, 0)
    row = pl.BlockSpec((rr, HEADS * LANE), rev)
    a_spec = pl.BlockSpec((HEADS, rr, DN_CHUNK), lambda i: (0, ng - 1 - i, 0))
    gl_spec = pl.BlockSpec((nc, HEADS * LANE), rev)
    wide = jax.ShapeDtypeStruct((t, HEADS * LANE), F32)
    outs, carried = _carried_call(
        body, rider, *_grid_ends((ng,)), name="delta_scan_bwd", grid=(ng,),
        in_specs=[row] * 4 + [a_spec, gl_spec, pl.BlockSpec((nc, HEADS, LANE, LANE), lambda i: (ng - 1 - i, 0, 0, 0)), row],
        out_specs=[row] * 4 + [a_spec, pl.BlockSpec((nc * 8, HEADS * LANE), rev)],
        out_shape=[wide] * 4 + [jax.ShapeDtypeStruct((HEADS, t, DN_CHUNK), F32),
                                jax.ShapeDtypeStruct((t // DN_CHUNK * 8, HEADS * LANE), F32)],
        scratch_shapes=[pltpu.VMEM((HEADS, LANE, LANE), F32)], sem=("arbitrary",), args=(u, w, qd, kt, intra, gl, sall, do))
    return tuple(outs[:5]) + (outs[5].reshape(t // DN_CHUNK, 8, HEADS * LANE)[:, 0, :],), carried


_ATT_TILE = 512


def _kv_rows(j, tk):
    return pl.ds(pl.multiple_of(j * tk, tk), tk)


def _att_scores(ql, qr, ckv_ref, kr_ref, j, tk):
    ks = _kv_rows(j, tk)
    return (_dot(ql, ckv_ref[ks, :], _NT) + _dot(qr, kr_ref[ks, :], _NT)) * ATT_SCALE


def _diag_mask(s):
    qi = lax.broadcasted_iota(jnp.int32, s.shape, 0)
    ki = lax.broadcasted_iota(jnp.int32, s.shape, 1)
    return jnp.where(ki <= qi, s, NEG_BIG)


def _attention(ql, qr_pre, cosb, sinb, ckv, kr):
    t = ckv.shape[0]
    tq = min(_ATT_TILE, t)

    nl = tq // LANE

    def lane_fold(v, op):
        out = v[:, :LANE]
        for k in range(1, nl):
            out = op(out, v[:, k * LANE:(k + 1) * LANE])
        return out

    def body(ql_ref, qr_ref, cos_ref, sin_ref, ckv_ref, kr_ref, o_ref, lse_ref, qrope_ref, s_all, m_lanes, l_lanes, acc_scr):
        qi = pl.program_id(1)
        q_lat = ql_ref[...]
        q_rope = _rope(qr_ref[...], cos_ref[...], sin_ref[...]).astype(qrope_ref.dtype)
        qrope_ref[...] = q_rope
        m_lanes[...] = jnp.full_like(m_lanes, NEG_BIG)

        def scores(j, masked):
            s = _att_scores(q_lat, q_rope, ckv_ref, kr_ref, j, tq)
            if masked:
                s = _diag_mask(s)
            s_all[j] = s
            m_lanes[...] = jnp.maximum(m_lanes[...], lane_fold(s, jnp.maximum))

        def scores_body(j, carry):
            scores(j, False)
            return carry

        lax.fori_loop(0, qi, scores_body, 0)
        scores(qi, True)
        m = jnp.max(m_lanes[...], axis=-1, keepdims=True)
        mb = jnp.broadcast_to(m, (tq, LANE))
        l_lanes[...] = jnp.zeros_like(l_lanes)
        acc_scr[...] = jnp.zeros_like(acc_scr)

        def weigh(j, carry):
            s = s_all[j]
            p = jnp.concatenate([jnp.exp(s[:, k * LANE:(k + 1) * LANE] - mb) for k in range(nl)], axis=1)
            l_lanes[...] += lane_fold(p, jnp.add)
            acc_scr[...] += _dot(p, ckv_ref[_kv_rows(j, tq), :], _NN)
            return carry

        lax.fori_loop(0, qi + 1, weigh, 0)
        l = jnp.sum(l_lanes[...], axis=-1, keepdims=True)
        o_ref[...] = acc_scr[...] / l
        lse_ref[...] = m + jnp.log(l)

    return pl.pallas_call(
        body, name="attention", grid=(HEADS, t // tq),
        in_specs=[pl.BlockSpec((None, tq, KV_LORA), lambda h, i: (h, i, 0)),
                  pl.BlockSpec((tq, ROPE_PAD), lambda h, i: (i, h)),
                  pl.BlockSpec((tq, ROPE_PAD), lambda h, i: (i, 0)), pl.BlockSpec((tq, ROPE_PAD), lambda h, i: (i, 0)),
                  pl.BlockSpec((t, KV_LORA), lambda h, i: (0, 0)),
                  pl.BlockSpec((t, ROPE_PAD), lambda h, i: (0, 0))],
        out_specs=[pl.BlockSpec((None, tq, KV_LORA), lambda h, i: (h, i, 0)),
                   pl.BlockSpec((None, tq, 1), lambda h, i: (h, i, 0)),
                   pl.BlockSpec((tq, ROPE_PAD), lambda h, i: (i, h))],
        out_shape=[jax.ShapeDtypeStruct((HEADS, t, KV_LORA), F32), jax.ShapeDtypeStruct((HEADS, t, 1), F32),
                   jax.ShapeDtypeStruct((t, HEADS * ROPE_PAD), _CDT)],
        scratch_shapes=[pltpu.VMEM((t // tq, tq, tq), F32), pltpu.VMEM((tq, LANE), F32), pltpu.VMEM((tq, LANE), F32),
                        pltpu.VMEM((tq, KV_LORA), F32)],
        compiler_params=_cparams(("parallel", "parallel")))(ql, qr_pre, cosb, sinb, ckv, kr)


def _attention_bwd(ql, qr, cosb, sinb, ckv, kr, out, lse, dout):
    t = ckv.shape[0]
    tq = min(_ATT_TILE, t)

    def body(ql_ref, qr_ref, cos_ref, sin_ref, ckv_ref, kr_ref, o_ref, lse_ref, do_ref, dql_ref, dqr_ref, dckv_ref,
             dkr_ref, dql_scr, dqr_scr):
        h, qi = pl.program_id(0), pl.program_id(1)

        @pl.when((h == 0) & (qi == 0))
        def _():
            dckv_ref[...] = jnp.zeros_like(dckv_ref)
            dkr_ref[...] = jnp.zeros_like(dkr_ref)

        q_lat, q_rope = ql_ref[...], qr_ref[...]
        d_o = do_ref[...].astype(_CDT)
        lse_v = lse_ref[...]
        dsum = jnp.sum(do_ref[...] * o_ref[...], axis=-1, keepdims=True)
        dql_scr[...] = jnp.zeros_like(dql_scr)
        dqr_scr[...] = jnp.zeros_like(dqr_scr)

        def step(j, masked):
            ks = _kv_rows(j, tq)
            s = _att_scores(q_lat, q_rope, ckv_ref, kr_ref, j, tq)
            if masked:
                s = _diag_mask(s)
            p = jnp.exp(s - lse_v)
            kv = ckv_ref[ks, :]
            ds = (p * (_dot(d_o, kv, _NT) - dsum) * ATT_SCALE).astype(_CDT)
            pb = p.astype(_CDT)
            dql_scr[...] += _dot(ds, kv, _NN)
            dqr_scr[...] += _dot(ds, kr_ref[ks, :], _NN)
            dckv_ref[ks, :] += _dot(pb, d_o, _TN) + _dot(ds, q_lat, _TN)
            dkr_ref[ks, :] += _dot(ds, q_rope, _TN)

        def loop_body(j, carry):
            step(j, False)
            return carry

        lax.fori_loop(0, qi, loop_body, 0)
        step(qi, True)
        dql_ref[...] = dql_scr[...].astype(dql_ref.dtype)
        dqr_ref[...] = _rope_bwd(dqr_scr[...], cos_ref[...], sin_ref[...]).astype(dqr_ref.dtype)

    lat = pl.BlockSpec((None, tq, KV_LORA), lambda h, i: (h, i, 0))
    rope = pl.BlockSpec((tq, ROPE_PAD), lambda h, i: (i, h))
    table = pl.BlockSpec((tq, ROPE_PAD), lambda h, i: (i, 0))
    kfull = pl.BlockSpec((t, KV_LORA), lambda h, i: (0, 0))
    rfull = pl.BlockSpec((t, ROPE_PAD), lambda h, i: (0, 0))
    return pl.pallas_call(
        body, name="attention_bwd", grid=(HEADS, t // tq),
        in_specs=[lat, rope, table, table, kfull, rfull, lat, pl.BlockSpec((None, tq, 1), lambda h, i: (h, i, 0)), lat],
        out_specs=[lat, rope, kfull, rfull],
        out_shape=[jax.ShapeDtypeStruct((HEADS, t, KV_LORA), _CDT), jax.ShapeDtypeStruct((t, HEADS * ROPE_PAD), _CDT),
                   jax.ShapeDtypeStruct((t, KV_LORA), F32), jax.ShapeDtypeStruct((t, ROPE_PAD), F32)],
        scratch_shapes=[pltpu.VMEM((tq, KV_LORA), F32), pltpu.VMEM((tq, ROPE_PAD), F32)],
        compiler_params=_cparams(("arbitrary", "arbitrary")))(ql, qr, cosb, sinb, ckv, kr, out, lse, dout)


_DX_ROWS = 256


def _dx_fused(pairs, add, add_scale, rider=None, name="b_dx"):
    t, d = add.shape
    tm = min(_DX_ROWS, t)
    n = len(pairs)

    def body(*refs):
        acc = refs[2 * n][...] * add_scale
        for i in range(n):
            acc = acc + _dot(refs[i][...], refs[n + i][...], _NT)
        refs[2 * n + 1][...] = acc

    in_specs = [pl.BlockSpec((tm, a.shape[1]), lambda i: (i, 0)) for a, _ in pairs]
    in_specs += [pl.BlockSpec(w.shape, lambda i: (0, 0)) for _, w in pairs]
    row = pl.BlockSpec((tm, d), lambda i: (i, 0))
    grid = (t // tm,)
    (dx,), carried = _carried_call(
        body, rider, *_grid_ends(grid), name=name, grid=grid, in_specs=in_specs + [row], out_specs=[row],
        out_shape=[jax.ShapeDtypeStruct((t, d), F32)], scratch_shapes=[], sem=("arbitrary",),
        args=tuple(a for a, _ in pairs) + tuple(w for _, w in pairs) + (add,))
    return dx, carried


def _ffn_in_swiglu(a, w):
    t, k = a.shape
    hid = w.shape[1] // 2
    tm, tn = _tile(t, 1024), _tile(hid, 1408)
    nj = hid // tn

    def body(a_ref, bg_ref, bu_ref, act_ref, gt_ref, up_ref):
        av = a_ref[...].astype(_CDT)
        gt = jnp.dot(av, bg_ref[...].astype(_CDT), preferred_element_type=F32)
        up = jnp.dot(av, bu_ref[...].astype(_CDT), preferred_element_type=F32)
        act_ref[...] = _swiglu(gt, up).astype(act_ref.dtype)
        gt_ref[...] = gt.astype(gt_ref.dtype)
        up_ref[...] = up.astype(up_ref.dtype)

    out = pl.BlockSpec((tm, tn), lambda i, j: (i, j))
    return pl.pallas_call(
        body, name="f_ffn_in_swiglu", grid=(t // tm, nj),
        in_specs=[pl.BlockSpec((tm, k), lambda i, j: (i, 0)), pl.BlockSpec((k, tn), lambda i, j: (0, j)),
                  pl.BlockSpec((k, tn), lambda i, j: (0, nj + j))],
        out_specs=[out] * 3, out_shape=[jax.ShapeDtypeStruct((t, hid), _CDT)] * 3,
        compiler_params=_cparams(("parallel", "parallel")))(a, w, w)


def _gated_norm(o, z, w):
    return _rms_norm(o, w) * _silu(z)


def _gated_norm_heads(o, z, w):
    heads = [_gated_norm(o[:, h * LANE:(h + 1) * LANE], z[:, h * LANE:(h + 1) * LANE], w) for h in range(HEADS)]
    return jnp.concatenate(heads, axis=1)


def _mla_pre(ckv, krp, cq, cosb, sinb, qw, kw):
    return _rms_norm(cq, qw), _rms_norm(ckv, kw), _rope(krp, cosb, sinb)


def _merge(gg, y_dn, y_mla):
    return _sigmoid(gg[:, :D_MODEL]) * y_dn + _sigmoid(gg[:, D_MODEL:]) * y_mla


def _ln1(xv, attn_out, g, b):
    return _layer_norm(ALPHA * xv + attn_out, g, b)


def _final(h1, ffn, gate_pre, ple_proj, g, b):
    return _layer_norm(ALPHA * h1 + ffn + _sigmoid(gate_pre) * ple_proj, g, b)


def _swiglu(gt, up):
    return _silu(gt) * up


def _local_step(x, p, cosb, sinb, target, wt, sp, exch):
    t = x.shape[0]
    bf = _CDT
    xb = x.astype(bf)
    g = {}

    qkv_pre = _mm2(xb, wt['qkv'], name="f_qkv")
    z = _mm2(xb, wt['z'], name="f_z")
    gg = _mm2(xb, wt['gg'], name="f_gg")
    pm = _mm2(xb, wt['mla'], name="f_mla")
    qkv_act = _conv_silu(qkv_pre, sp['conv_w'])
    (u, w_, qd, kt, intra, gl, t_inv, qd2, au), sent = _delta_local(qkv_act, pm, sp['a_log'], sp['dt_bias'],
                                                                    rider=exch.gather_send())
    (o_dn, sall), passed = _delta_scan(u, w_, qd2, kt, au, gl, rider=exch.gather_pass(sent))
    wt = dict(wt, **exch.weights(passed))
    def gated_norm_br(h, o, zz, w, wbr):
        og_v = _gated_norm_heads(o, zz, w).astype(bf)
        return og_v, jnp.dot(og_v, wbr.astype(bf), preferred_element_type=F32)

    og, y_dn = _rowwise(gated_norm_br, [(o_dn, D_MODEL, 0, False), (z, D_MODEL, 0, False)],
                        [sp['dn_norm_w'], wt['br_dn']],
                        [(D_MODEL, D_MODEL, False, bf), (D_MODEL, D_MODEL, False, F32)], name="f_gated_norm_br")

    def mla_pre_uq(h, ckv, krp, cq, cosv, sinv, qw, kw, wn, wr):
        c_q_v, c_kv_v, k_rope_v = _mla_pre(ckv, krp, cq, cosv, sinv, qw, kw)
        c_q_b = c_q_v.astype(bf)
        return (c_q_b, c_kv_v, k_rope_v, jnp.dot(c_q_b, wn.astype(bf), preferred_element_type=F32),
                jnp.dot(c_q_b, wr.astype(bf), preferred_element_type=F32))

    c_q, c_kv, k_rope, q_nope, q_rope_pre = _rowwise(
        mla_pre_uq,
        [(pm, KV_LORA, 0, False), (pm, ROPE_PAD, 2, False), (pm, Q_LORA, 2, False),
         (cosb, ROPE_PAD, 0, False), (sinb, ROPE_PAD, 0, False)],
        [sp['q_norm_w'], sp['kv_norm_w'], wt['uq_nope'], wt['uq_rope']],
        [(Q_LORA, Q_LORA, False, bf), (KV_LORA, KV_LORA, False, bf), (ROPE_PAD, ROPE_PAD, False, bf),
         (HEADS * NOPE, HEADS * NOPE, False, bf), (HEADS * ROPE_PAD, HEADS * ROPE_PAD, False, F32)], name="f_mla_pre_uq")
    q_lat = _mm(q_nope, wt['uk'], name="f_q_lat", tb=True, heads=HEADS, a_head='col', b_head='lead', out_head='lead',
                dims=(t, KV_LORA, NOPE), out_dtype=bf)
    out_lat, lse, q_rope = _attention(q_lat, q_rope_pre, cosb, sinb, c_kv, k_rope)
    o_mla = _mm(out_lat, wt['uv'], name="f_o_mla", heads=HEADS, a_head='lead', b_head='lead', out_head='col',
                dims=(t, NOPE, KV_LORA), out_dtype=bf)
    y_mla = _mm2(o_mla, wt['br_mla'], name="f_br_mla")

    def merge_o_ln1(h, ggv, yd, ym, xv, wo, gv, bv):
        mixed_v = _merge(ggv, yd, ym).astype(bf)
        ao = jnp.dot(mixed_v, wo.astype(bf), preferred_element_type=F32)
        h1v = _ln1(xv, ao, gv, bv)
        return mixed_v, ao, h1v, h1v

    mixed, attn_out, h1, h1b = _rowwise(
        merge_o_ln1, [(gg, 2 * D_MODEL, 0, False), (y_dn, D_MODEL, 0, False), (y_mla, D_MODEL, 0, False), (x, D_MODEL, 0, False)],
        [wt['o'], sp['ln1_g'], sp['ln1_b']],
        [(D_MODEL, D_MODEL, False, bf), (D_MODEL, D_MODEL, False, F32), (D_MODEL, D_MODEL, False, F32),
         (D_MODEL, D_MODEL, False, bf)], name="f_merge_o_ln1")
    act, ffn_gt, ffn_up = _ffn_in_swiglu(h1b, wt['ffn_in'])
    pb = p.astype(bf)

    def final_fn(h, h1v, actv, pv, tgt, gt, up, wfo, wpg, wpl, gv, bv):
        ffnv = jnp.dot(actv.astype(bf), wfo.astype(bf), preferred_element_type=F32)
        gpv = jnp.dot(h1v.astype(bf), wpg.astype(bf), preferred_element_type=F32)
        ppv = jnp.dot(pv.astype(bf), wpl.astype(bf), preferred_element_type=F32)
        y, vjp = jax.vjp(_final, h1v, ffnv, gpv, ppv, gv, bv)
        err = y - tgt
        dh1, dffn, dgp, dpp, dg, db = vjp(err * (1.0 / D_MODEL))
        sq = err * err
        lanes = sq[:, :LANE]
        for j in range(1, D_MODEL // LANE):
            lanes = lanes + sq[:, j * LANE:(j + 1) * LANE]
        loss = jnp.sum(lanes, axis=0, keepdims=True) * (0.5 / D_MODEL)
        dffn_b = dffn.astype(bf)
        dact = _dot(dffn_b, wfo, _NT).astype(bf).astype(F32)
        _, vjp_s = jax.vjp(_swiglu, gt.astype(F32), up.astype(F32))
        dgt, dup = vjp_s(dact)
        return dffn, dffn_b, dgp, dpp, jnp.concatenate([dgt, dup], axis=1), dg, db, loss

    dpre2, dpre2b, dgate_pre, dple_proj, dffn_in, g['ln2_g'], g['ln2_b'], loss_lanes = _rowwise(
        final_fn, [(h1, D_MODEL, 0, False), (act, FFN_HIDDEN, 0, False), (pb, PLE_DIM, 0, False), (target, D_MODEL, 0, False),
                   (ffn_gt, FFN_HIDDEN, 0, False), (ffn_up, FFN_HIDDEN, 0, False)],
        [wt['ffn_out'], wt['ple_gate'], wt['ple'], sp['ln2_g'], sp['ln2_b']],
        [(D_MODEL, D_MODEL, False, F32)] + [(D_MODEL, D_MODEL, False, bf)] * 3 + [(2 * FFN_HIDDEN, 2 * FFN_HIDDEN, False, bf)],
        [(1, D_MODEL), (1, D_MODEL), (1, LANE)], name="b_final")

    g['ple'] = _mm2(pb, dple_proj, ta=True, name="g_ple")
    g['ple_gate'] = _mm2(h1b, dgate_pre, ta=True, name="g_ple_gate")
    g['ffn_out'] = _mm2(act, dpre2b, ta=True, name="g_ffn_out")
    g['ffn_in'] = _mm2(h1b, dffn_in, ta=True, name="g_ffn_in")
    dh1, _ = _dx_fused([(dffn_in, wt['ffn_in']), (dgate_pre, wt['ple_gate'])], dpre2, ALPHA, name="b_dh1")

    def attn_out_bwd(h, xv, ao, d, ggv, yd, ym, wo, wbd, wbm, gv, bv):
        _, vjp = jax.vjp(_ln1, xv, ao, gv, bv)
        _, dao, dg, db = vjp(d)
        dao_b = dao.astype(bf)
        _, vjp_m = jax.vjp(_merge, ggv, yd, ym)
        dggv, dyd, dym = vjp_m(_dot(dao_b, wo, _NT))
        dyd_b, dym_b = dyd.astype(bf), dym.astype(bf)
        return dao, dao_b, dggv, dyd_b, dym_b, _dot(dyd_b, wbd, _NT), _dot(dym_b, wbm, _NT), dg, db

    row_d = lambda a: (a, D_MODEL, 0, False)
    dpre1, dpre1b, dgg, dy_dn, dy_mla, dog, do_mla, g['ln1_g'], g['ln1_b'] = _rowwise(
        attn_out_bwd, [row_d(x), row_d(attn_out), row_d(dh1), (gg, 2 * D_MODEL, 0, False), row_d(y_dn), row_d(y_mla)],
        [wt['o'], wt['br_dn'], wt['br_mla'], sp['ln1_g'], sp['ln1_b']],
        [(D_MODEL, D_MODEL, False, F32), (D_MODEL, D_MODEL, False, bf), (2 * D_MODEL, 2 * D_MODEL, False, bf),
         (D_MODEL, D_MODEL, False, bf), (D_MODEL, D_MODEL, False, bf), (D_MODEL, D_MODEL, False, F32),
         (D_MODEL, D_MODEL, False, bf)],
        [(1, D_MODEL), (1, D_MODEL)], name="b_attn_out")
    g['o'] = _mm2(mixed, dpre1b, ta=True, name="g_o")
    g['br_dn'] = _mm2(og, dy_dn, ta=True, name="g_br_dn")
    g['br_mla'] = _mm2(o_mla, dy_mla, ta=True, name="g_br_mla")

    dout_lat = _mm(do_mla, wt['uv'], name="b_dout_lat", tb=True, heads=HEADS, a_head='col', b_head='lead',
                   out_head='lead', dims=(t, KV_LORA, NOPE))
    g['uv'] = _mm(out_lat, do_mla, name="g_uv", ta=True, heads=HEADS, a_head='lead', b_head='col', out_head='lead',
                  dims=(KV_LORA, NOPE, t))
    dq_lat, dq_rope_pre, dckv_att, dkr_att = _attention_bwd(q_lat, q_rope, cosb, sinb, c_kv, k_rope, out_lat, lse, dout_lat)
    dq_nope = _mm(dq_lat, wt['uk'], name="b_dq_nope", heads=HEADS, a_head='lead', b_head='lead', out_head='col',
                  dims=(t, NOPE, KV_LORA), out_dtype=bf)
    g['uk'] = _mm(dq_lat, q_nope, name="g_uk", ta=True, heads=HEADS, a_head='lead', b_head='col', out_head='lead',
                  dims=(KV_LORA, NOPE, t))
    g['uq_nope'] = _mm2(c_q, dq_nope, ta=True, name="g_uq_nope")
    g['uq_rope'] = _mm2(c_q, dq_rope_pre, ta=True, name="g_uq_rope")
    dc_q = _mm2(dq_nope, wt['uq_nope'], tb=True, name="b_dcq_nope")
    dc_q = _mm2(dq_rope_pre, wt['uq_rope'], tb=True, name="b_dcq_rope", add=dc_q)

    def gated_norm_bwd(h, o, zz, d, w):
        _, vjp = jax.vjp(_gated_norm_heads, o, zz, w)
        return vjp(d)

    do_dn, dz, g['dn_norm_w'] = _rowwise(
        gated_norm_bwd, [(o_dn, D_MODEL, 0, False), (z, D_MODEL, 0, False), (dog, D_MODEL, 0, False)], [sp['dn_norm_w']],
        [(D_MODEL, D_MODEL, False, F32), (D_MODEL, D_MODEL, False, bf)], [(1, LANE)], name="b_gated_norm")
    (du, dw, dqd, dkt, dintra, dgl), paired = _delta_scan_bwd(u, w_, qd, kt, intra, gl, sall, do_dn, rider=exch.pair_send(g))
    (dq_a, dk_a, dv_a, dba, g['a_log'], g['dt_bias']), arrived = _delta_local_bwd(
        qkv_act, pm, sp['a_log'], sp['dt_bias'], t_inv, du, dw, dqd, dkt, dintra, dgl, rider=exch.reduce_send(paired))
    exch.reduce_arrived(arrived)
    dqkv_pre, g['conv_w'] = _conv_silu_bwd(qkv_pre, sp['conv_w'], [dq_a, dk_a, dv_a])

    def mla_pre_bwd(h, ckv, cq, cosv, sinv, dcq, dckv, dkr, dba_v, qw, kw):
        _, vjp = jax.vjp(lambda a, c, d, e: (_rms_norm(c, d), _rms_norm(a, e)), ckv, cq, qw, kw)
        dckv_p, dcq_p, dqw, dkw = vjp((dcq, dckv))
        dkr_p = _rope_bwd(dkr, cosv, sinv)
        dpm = jnp.concatenate([dckv_p, dkr_p, dba_v, jnp.zeros((ckv.shape[0], 2 * LANE), F32), dcq_p], axis=1)
        return dpm, dqw, dkw

    dpm, g['q_norm_w'], g['kv_norm_w'] = _rowwise(
        mla_pre_bwd,
        [(pm, KV_LORA, 0, False), (pm, Q_LORA, 2, False), (cosb, ROPE_PAD, 0, False), (sinb, ROPE_PAD, 0, False),
         (dc_q, Q_LORA, 0, False), (dckv_att, KV_LORA, 0, False), (dkr_att, ROPE_PAD, 0, False), (dba, LANE, 0, False)],
        [sp['q_norm_w'], sp['kv_norm_w']], [(1152, 1152, False, bf)], [(1, Q_LORA), (1, KV_LORA)], name="b_mla_pre")

    g['qkv'] = _mm2(xb, dqkv_pre, ta=True, name="g_qkv")
    g['z'] = _mm2(xb, dz, ta=True, name="g_z")
    g['gg'] = _mm2(xb, dgg, ta=True, name="g_gg")
    g['mla'] = _mm2(xb, dpm, ta=True, name="g_mla")
    dx, arrived = _dx_fused([(dqkv_pre, wt['qkv']), (dz, wt['z']), (dgg, wt['gg']), (dpm, wt['mla'])], dpre1, ALPHA,
                            rider=exch.in_send(g))
    exch.in_arrived(arrived)
    return loss_lanes, dx, g


_IN_SIZES = (QKV_W, HEADS * DN_DK, HEADS, HEADS, Q_LORA, KV_LORA, ROPE, D_MODEL, D_MODEL)


def _rope_tables(positions):
    inv_freq = ROPE_BASE ** (-jnp.arange(0, ROPE, 2, dtype=F32) / ROPE)
    ang = positions.astype(F32)[:, None] * inv_freq
    cos, sin = jnp.cos(ang), jnp.sin(ang)
    zeros = jnp.zeros((positions.shape[0], ROPE_PAD - ROPE), F32)
    return jnp.concatenate([cos, cos, zeros], axis=1), jnp.concatenate([-sin, sin, zeros], axis=1)


def _prep_w_in(w_in):
    dt = w_in.dtype
    offs = [0]
    for s in _IN_SIZES:
        offs.append(offs[-1] + s)
    qkv, z, wb, wa, cq, ckv, kr, gd, gm = [w_in[:, offs[i]:offs[i + 1]] for i in range(len(_IN_SIZES))]
    zc = lambda n: jnp.zeros((D_MODEL, n), dt)
    return {
        'qkv': qkv, 'z': z, 'gg': jnp.concatenate([gd, gm], axis=1),
        'mla': jnp.concatenate([ckv, kr, zc(ROPE_PAD - ROPE), wb, wa, zc(LANE - 2 * HEADS), zc(2 * LANE), cq], axis=1),
    }


def _prep_weights(full):
    w_uq = full['w_uq']
    wt = {
        'uq_nope': w_uq[:, :, :NOPE].reshape(Q_LORA, HEADS * NOPE),
        'uq_rope': jnp.pad(w_uq[:, :, NOPE:], ((0, 0), (0, 0), (0, ROPE_PAD - ROPE))).reshape(Q_LORA, HEADS * ROPE_PAD),
        'uk': jnp.transpose(full['w_uk'], (1, 0, 2)), 'uv': jnp.transpose(full['w_uv'], (1, 0, 2)),
        'br_dn': full['w_br_dn'], 'br_mla': full['w_br_mla'], 'o': full['w_o'], 'ffn_in': full['w_ffn_in'],
        'ffn_out': full['w_ffn_out'], 'ple': full['w_ple'], 'ple_gate': full['w_ple_gate'],
    }
    return wt


def _prep_small(small):
    pad = lambda v: jnp.pad(v, (0, LANE - v.shape[0]))[None, :]
    return {
        'conv_w': small['conv_w'], 'a_log': pad(small['dn_a_log']), 'dt_bias': pad(small['dn_dt_bias']),
        'dn_norm_w': small['dn_norm_w'][None, :], 'q_norm_w': small['q_norm_w'][None, :],
        'kv_norm_w': small['kv_norm_w'][None, :], 'ln1_g': small['ln1_g'][None, :], 'ln1_b': small['ln1_b'][None, :],
        'ln2_g': small['ln2_g'][None, :], 'ln2_b': small['ln2_b'][None, :],
    }


def _w_in_grad(g):
    mla = g['mla']
    ba0 = KV_LORA + ROPE_PAD
    cq0 = ba0 + 3 * LANE
    return jnp.concatenate([
        g['qkv'], g['z'], mla[:, ba0:ba0 + HEADS], mla[:, ba0 + HEADS:ba0 + 2 * HEADS], mla[:, cq0:cq0 + Q_LORA],
        mla[:, :KV_LORA], mla[:, KV_LORA:KV_LORA + ROPE], g['gg']], axis=1)


def _unprep_grads_late(g):
    return {
        'conv_w': g['conv_w'], 'dn_a_log': g['a_log'][0, :HEADS], 'dn_dt_bias': g['dt_bias'][0, :HEADS],
        'dn_norm_w': g['dn_norm_w'][0], 'q_norm_w': g['q_norm_w'][0], 'kv_norm_w': g['kv_norm_w'][0],
        'ln1_g': g['ln1_g'][0], 'ln1_b': g['ln1_b'][0], 'ln2_g': g['ln2_g'][0], 'ln2_b': g['ln2_b'][0],
    }


def _unprep_grads_early(g):
    w_uq = jnp.concatenate([g['uq_nope'].reshape(Q_LORA, HEADS, NOPE),
                            g['uq_rope'].reshape(Q_LORA, HEADS, ROPE_PAD)[:, :, :ROPE]], axis=2)
    return {
        'w_uq': w_uq, 'w_uk': jnp.transpose(g['uk'], (1, 0, 2)), 'w_uv': jnp.transpose(g['uv'], (1, 0, 2)),
        'w_br_dn': g['br_dn'], 'w_br_mla': g['br_mla'], 'w_o': g['o'],
        'w_ffn_in': g['ffn_in'], 'w_ffn_out': g['ffn_out'], 'w_ple': g['ple'], 'w_ple_gate': g['ple_gate'],
    }


_FLATB_PIECES = (
    ('w_ffn_out', 704, (704, D_MODEL)), ('w_br_dn', 256, (256, D_MODEL)), ('w_br_mla', 256, (256, D_MODEL)),
    ('w_o', 256, (256, D_MODEL)), ('w_ple_gate', 256, (256, D_MODEL)), ('w_uq', 144, (96, HEADS, NOPE + ROPE)),
    ('w_uk', 64, (64, HEADS, NOPE)), ('w_uv', 64, (64, HEADS, NOPE)), ('w_ple', 64, (PLE_DIM, 256)),
)
FLATB_ROWS = 2112
W_IN_SHARD = D_IN // N_SHARD
FFN_IN_SHARD = 2 * FFN_HIDDEN // N_SHARD
A_ROWS = D_MODEL + 32
_CONV_SHARD = QKV_W // N_SHARD
_ADD_TILES = (256, 256, 352)


def _flatb_offsets():
    offs, o = {}, 0
    for name, rows, _ in _FLATB_PIECES:
        offs[name] = o
        o += rows
    return offs, o


def _pack_shards(ws, conv_w):
    conv_bits = lax.bitcast_convert_type(conv_w, jnp.bfloat16).reshape(DN_CONV, 2 * _CONV_SHARD).astype(_CDT)
    tail = jnp.pad(conv_bits, ((0, A_ROWS - D_MODEL - DN_CONV), (0, W_IN_SHARD - 2 * _CONV_SHARD)))
    a_buf = jnp.concatenate([ws['w_in'].astype(_CDT), tail], axis=0)
    parts = [ws[name].astype(_CDT).reshape(rows, FLAT_W) for name, rows, _ in _FLATB_PIECES]
    used = sum(p.shape[0] for p in parts)
    parts.append(jnp.zeros((FLATB_ROWS - used, FLAT_W), _CDT))
    return [a_buf, ws['w_ffn_in'].astype(_CDT), jnp.concatenate(parts, axis=0)]


def _unpack_w_in(gathered, local, me):
    a = [jnp.where(me == s, local, gathered[s]) for s in range(N_SHARD)]
    conv = [lax.bitcast_convert_type(
        p[D_MODEL:D_MODEL + DN_CONV, :2 * _CONV_SHARD].astype(jnp.bfloat16).reshape(DN_CONV, _CONV_SHARD, 2), F32) for p in a]
    return jnp.concatenate([p[:D_MODEL] for p in a], axis=1), jnp.concatenate(conv, axis=1)


def _unpack_rest(gathered, local, me):
    pick = lambda b, s: jnp.where(me == s, local[b], gathered[b][s])
    full = {'w_ffn_in': jnp.concatenate([pick(0, s) for s in range(N_SHARD)], axis=1)}
    offs, _ = _flatb_offsets()
    fb = [pick(1, s) for s in range(N_SHARD)]
    for name, rows, shape in _FLATB_PIECES:
        pieces = [p[offs[name]:offs[name] + rows].reshape(shape) for p in fb]
        full[name] = jnp.concatenate(pieces, axis=1 if name == 'w_ple' else 0)
    return full


def _shard_columns(g, w):
    return jnp.stack([g[:, s * w:(s + 1) * w] for s in range(N_SHARD)])


def _pack_grads_rest(gw):
    parts = []
    for name, rows, _ in _FLATB_PIECES:
        g = gw[name]
        if name == 'w_ple':
            parts.append(_shard_columns(g, PLE_DIM).reshape(N_SHARD, rows, FLAT_W))
        else:
            parts.append(g.reshape(N_SHARD, rows, FLAT_W))
    used = sum(p.shape[1] for p in parts)
    parts.append(jnp.zeros((N_SHARD, FLATB_ROWS - used, FLAT_W), F32))
    return [_shard_columns(gw['w_ffn_in'], FFN_IN_SHARD), jnp.concatenate(parts, axis=1)]


def _unpack_reduced(mine, theirs, c):
    whole = [jnp.concatenate([jnp.where(c == 0, m, t), jnp.where(c == 0, t, m)], axis=0) for m, t in zip(mine, theirs)]
    out = {'w_in': whole[0], 'w_ffn_in': whole[1]}
    offs, _ = _flatb_offsets()
    for name, rows, shape in _FLATB_PIECES:
        out[name] = whole[2][offs[name]:offs[name] + rows].reshape(shape)
    return out


_HBM = pl.BlockSpec(memory_space=pltpu.HBM)


def _place():
    x, y, c = lax.axis_index("x"), lax.axis_index("y"), lax.axis_index("c")
    chips = [(1 - x, y), (x, 1 - y), (1 - x, 1 - y)]
    return x, y, c, chips


def _remote(src, dst, send_sems, recv_sems, k, to):
    return pltpu.make_async_remote_copy(src_ref=src, dst_ref=dst, send_sem=send_sems.at[k], recv_sem=recv_sems.at[k],
                                        device_id=to, device_id_type=_MESH)


def _half_rows(ref, half, hf, lead=None):
    rows = pl.ds(pl.multiple_of(hf * half, 16), half)
    return ref.at[rows, :] if lead is None else ref.at[lead, rows, :]


class _Rider:
    def __init__(self, inputs, out_shape, n_sems, copies, aliases=None):
        self.inputs, self.out_shape, self.n_sems, self.copies = list(inputs), list(out_shape), n_sems, copies
        self.aliases = aliases or {}


def _carried_call(body, rider, first, last, *, name, grid, in_specs, out_specs, out_shape, scratch_shapes, sem, args):
    n_in, n_out, n_scr = len(in_specs), len(out_specs), len(scratch_shapes)
    if rider is None:
        res = pl.pallas_call(body, name=name, grid=grid, in_specs=in_specs, out_specs=out_specs, out_shape=out_shape,
                             scratch_shapes=scratch_shapes, compiler_params=_cparams(sem))(*args)
        return list(res), []
    ri, ro = len(rider.inputs), len(rider.out_shape)

    def full_body(*refs):
        own_in, r_in = refs[:n_in], refs[n_in:n_in + ri]
        o0 = n_in + ri
        own_out, r_out = refs[o0:o0 + n_out], refs[o0 + n_out:o0 + n_out + ro]
        s0 = o0 + n_out + ro
        own_scr, send_sems, recv_sems = refs[s0:s0 + n_scr], refs[s0 + n_scr], refs[s0 + n_scr + 1]

        @pl.when(first())
        def _():
            sends, _ = rider.copies(r_in, r_out, send_sems, recv_sems)
            for cp in sends:
                cp.start()

        body(*own_in, *own_out, *own_scr)

        @pl.when(last())
        def _():
            sends, arrivals = rider.copies(r_in, r_out, send_sems, recv_sems)
            for cp in arrivals():
                cp.wait_recv()
            for cp in sends:
                cp.wait_send()

    res = pl.pallas_call(
        full_body, name=name, grid=grid, in_specs=list(in_specs) + [_HBM] * ri, out_specs=list(out_specs) + [_HBM] * ro,
        out_shape=list(out_shape) + rider.out_shape,
        scratch_shapes=list(scratch_shapes) + [pltpu.SemaphoreType.DMA((rider.n_sems,))] * 2,
        input_output_aliases={n_in + i: n_out + o for i, o in rider.aliases.items()},
        compiler_params=_cparams(sem))(*args, *rider.inputs)
    return list(res[:n_out]), list(res[n_out:])


def _ride_gather_send(bufs):
    n = len(bufs)
    halves = [b.shape[0] // 2 for b in bufs]

    def copies(ins, outs, send_sems, recv_sems):
        x, y, c, chips = _place()
        slot = lambda b, cx, cy: _half_rows(outs[b], halves[b], c, lead=2 * cx + cy)
        sends = [_remote(_half_rows(ins[b], halves[b], c), slot(b, x, y), send_sems, recv_sems, 3 * b + j, (cx, cy, c))
                 for b in range(n) for j, (cx, cy) in enumerate(chips)]
        arrivals = lambda: [_remote(slot(b, cx, cy), slot(b, cx, cy), send_sems, recv_sems, 3 * b + j, (x, y, c))
                            for b in range(n) for j, (cx, cy) in enumerate(chips)]
        return sends, arrivals

    return _Rider(bufs, [jax.ShapeDtypeStruct((N_SHARD,) + b.shape, b.dtype) for b in bufs], 3 * n, copies)


def _ride_gather_pass(gathered):
    n = len(gathered)
    halves = [g.shape[1] // 2 for g in gathered]

    def copies(ins, outs, send_sems, recv_sems):
        x, y, c, chips = _place()
        slot = lambda b, cx, cy, hf: _half_rows(outs[b], halves[b], hf, lead=2 * cx + cy)
        sends = [_remote(slot(b, cx, cy, c), slot(b, cx, cy, c), send_sems, recv_sems, 3 * b + j, (x, y, 1 - c))
                 for b in range(n) for j, (cx, cy) in enumerate(chips)]
        arrivals = lambda: [_remote(slot(b, cx, cy, 1 - c), slot(b, cx, cy, 1 - c), send_sems, recv_sems, 3 * b + j, (x, y, c))
                            for b in range(n) for j, (cx, cy) in enumerate(chips)]
        return sends, arrivals

    return _Rider(gathered, [jax.ShapeDtypeStruct(g.shape, g.dtype) for g in gathered], 3 * n, copies,
                  aliases={b: b for b in range(n)})


def _ride_pair_exchange(gbufs):
    n = len(gbufs)
    halves = [g.shape[1] // 2 for g in gbufs]

    def copies(ins, outs, send_sems, recv_sems):
        x, y, c, _ = _place()
        sends = [_remote(ins[b].at[:, pl.ds(pl.multiple_of((1 - c) * halves[b], 16), halves[b]), :], outs[b],
                         send_sems, recv_sems, b, (x, y, 1 - c)) for b in range(n)]
        arrivals = lambda: [_remote(outs[b], outs[b], send_sems, recv_sems, b, (x, y, c)) for b in range(n)]
        return sends, arrivals

    return _Rider(gbufs, [jax.ShapeDtypeStruct((N_SHARD, h, g.shape[2]), g.dtype) for g, h in zip(gbufs, halves)], n, copies)


def _ride_chip_exchange(parts):
    n = len(parts)

    def copies(ins, outs, send_sems, recv_sems):
        x, y, c, chips = _place()
        sends = [_remote(ins[b].at[2 * cx + cy], outs[b].at[j], send_sems, recv_sems, 3 * b + j, (cx, cy, c))
                 for b in range(n) for j, (cx, cy) in enumerate(chips)]
        arrivals = lambda: [_remote(ins[b].at[0], outs[b].at[j], send_sems, recv_sems, 3 * b + j, (x, y, c))
                            for b in range(n) for j in range(len(chips))]
        return sends, arrivals

    return _Rider(parts, [jax.ShapeDtypeStruct((3,) + p.shape[1:], p.dtype) for p in parts], 3 * n, copies)


def _gather_shards(bufs, name):
    n = len(bufs)
    halves = [b.shape[0] // 2 for b in bufs]

    def body(*refs):
        ins, outs, send_sems, recv_sems = refs[:n], refs[n:2 * n], refs[2 * n], refs[2 * n + 1]
        x, y, c, chips = _place()
        me, sibling = (x, y, c), (x, y, 1 - c)
        slot = lambda b, cx, cy, hf: _half_rows(outs[b], halves[b], hf, lead=2 * cx + cy)
        first = [_remote(_half_rows(ins[b], halves[b], c), slot(b, x, y, c), send_sems, recv_sems, 6 * b + j, (cx, cy, c))
                 for b in range(n) for j, (cx, cy) in enumerate(chips)]
        for cp in first:
            cp.start()
        passed = []
        for j, (cx, cy) in enumerate(chips):
            for b in range(n):
                _remote(slot(b, cx, cy, c), slot(b, cx, cy, c), send_sems, recv_sems, 6 * b + j, me).wait_recv()
                fwd = _remote(slot(b, cx, cy, c), slot(b, cx, cy, c), send_sems, recv_sems, 6 * b + 3 + j, sibling)
                fwd.start()
                passed.append(fwd)
        for j, (cx, cy) in enumerate(chips):
            for b in range(n):
                _remote(slot(b, cx, cy, 1 - c), slot(b, cx, cy, 1 - c), send_sems, recv_sems, 6 * b + 3 + j, me).wait_recv()
        for cp in first + passed:
            cp.wait_send()

    return pl.pallas_call(
        body, name=name, out_shape=[jax.ShapeDtypeStruct((N_SHARD,) + b.shape, b.dtype) for b in bufs],
        in_specs=[_HBM] * n, out_specs=[_HBM] * n,
        scratch_shapes=[pltpu.SemaphoreType.DMA((6 * n,)), pltpu.SemaphoreType.DMA((6 * n,))],
    )(*bufs)


def _reduce_pair_exchange(gbufs, name):
    n = len(gbufs)
    halves = [g.shape[1] // 2 for g in gbufs]

    def body(*refs):
        ins, outs, send_sems, recv_sems = refs[:n], refs[n:2 * n], refs[2 * n], refs[2 * n + 1]
        x, y, c, _ = _place()
        cps = [_remote(ins[b].at[:, pl.ds(pl.multiple_of((1 - c) * halves[b], 16), halves[b]), :], outs[b],
                       send_sems, recv_sems, b, (x, y, 1 - c)) for b in range(n)]
        for cp in cps:
            cp.start()
        for cp in cps:
            cp.wait()

    return pl.pallas_call(
        body, name=name,
        out_shape=[jax.ShapeDtypeStruct((N_SHARD, h, g.shape[2]), g.dtype) for g, h in zip(gbufs, halves)],
        in_specs=[_HBM] * n, out_specs=[_HBM] * n,
        scratch_shapes=[pltpu.SemaphoreType.DMA((n,)), pltpu.SemaphoreType.DMA((n,))],
    )(*gbufs)


def _pair_add(gbuf, recv, c_arr, tr, name):
    _, rows, width = gbuf.shape
    half = rows // 2
    nt = half // tr

    def body(c_ref, a_ref, b_ref, o_ref):
        o_ref[...] = (a_ref[...] + b_ref[...]).astype(o_ref.dtype)

    blk = lambda f: pl.BlockSpec((None, tr, width), f)
    return pl.pallas_call(
        body, name=name, out_shape=jax.ShapeDtypeStruct((N_SHARD, half, width), jnp.bfloat16),
        grid_spec=pltpu.PrefetchScalarGridSpec(
            num_scalar_prefetch=1, grid=(N_SHARD, nt),
            in_specs=[blk(lambda s, i, c: (s, c[0] * nt + i, 0)), blk(lambda s, i, c: (s, i, 0))],
            out_specs=blk(lambda s, i, c: (s, i, 0))),
        compiler_params=_cparams(("parallel", "parallel")))(c_arr, gbuf, recv)


def _reduce_chip_exchange(parts, name):
    n = len(parts)

    def body(*refs):
        ins, outs, send_sems, recv_sems = refs[:n], refs[n:2 * n], refs[2 * n], refs[2 * n + 1]
        x, y, c, chips = _place()
        sends = [_remote(ins[b].at[2 * cx + cy], outs[b].at[j], send_sems, recv_sems, 3 * b + j, (cx, cy, c))
                 for b in range(n) for j, (cx, cy) in enumerate(chips)]
        for cp in sends:
            cp.start()
        for b in range(n):
            for j in range(len(chips)):
                _remote(ins[b].at[0], outs[b].at[j], send_sems, recv_sems, 3 * b + j, (x, y, c)).wait_recv()
        for cp in sends:
            cp.wait_send()

    return pl.pallas_call(
        body, name=name, out_shape=[jax.ShapeDtypeStruct((3,) + p.shape[1:], p.dtype) for p in parts],
        in_specs=[_HBM] * n, out_specs=[_HBM] * n,
        scratch_shapes=[pltpu.SemaphoreType.DMA((3 * n,)), pltpu.SemaphoreType.DMA((3 * n,))],
    )(*parts)


def _chip_add(part, recv, me_arr, tr, name):
    _, half, width = part.shape

    def body(me_ref, own, a0, a1, a2, o_ref):
        f = lambda r: r[...].astype(F32)
        o_ref[...] = ((f(own) + f(a0)) + f(a1)) + f(a2)

    specs = [pl.BlockSpec((None, tr, width), lambda i, me: (me[0], i, 0))]
    specs += [pl.BlockSpec((None, tr, width), functools.partial(lambda i, me, k: (k, i, 0), k=k)) for k in range(3)]
    return pl.pallas_call(
        body, name=name, out_shape=jax.ShapeDtypeStruct((half, width), F32),
        grid_spec=pltpu.PrefetchScalarGridSpec(
            num_scalar_prefetch=1, grid=(half // tr,), in_specs=specs,
            out_specs=pl.BlockSpec((tr, width), lambda i, me: (i, 0))),
        compiler_params=_cparams(("parallel",)))(me_arr, part, recv, recv, recv)


def _reduce_pair_share(rhalves, name):
    n = len(rhalves)

    def body(*refs):
        ins, outs, send_sems, recv_sems = refs[:n], refs[n:2 * n], refs[2 * n], refs[2 * n + 1]
        x, y, c, _ = _place()
        cps = [_remote(ins[b], outs[b], send_sems, recv_sems, b, (x, y, 1 - c)) for b in range(n)]
        for cp in cps:
            cp.start()
        for cp in cps:
            cp.wait()

    return pl.pallas_call(
        body, name=name, out_shape=[jax.ShapeDtypeStruct(r.shape, r.dtype) for r in rhalves],
        in_specs=[_HBM] * n, out_specs=[_HBM] * n,
        scratch_shapes=[pltpu.SemaphoreType.DMA((n,)), pltpu.SemaphoreType.DMA((n,))],
    )(*rhalves)


def _small_allreduce(buf):
    r, width = buf.shape
    n_dev = 8

    def body(x_ref, all_ref, sum_ref, send_sems, recv_sems, local_sem):
        x, y, c, chips = _place()
        me, sibling = (x, y, c), (x, y, 1 - c)

        def rows(px, py, pc):
            return all_ref.at[pl.ds(pl.multiple_of((4 * px + 2 * py + pc) * r, 8), r), :]

        def copy(k, block, to, src=None):
            return _remote(rows(*block) if src is None else src, rows(*block), send_sems, recv_sems, k, to)

        mine = pltpu.make_async_copy(x_ref, rows(*me), local_sem)
        mine.start()
        first = [copy(0, me, sibling, src=x_ref)]
        first += [copy(1 + j, me, (*chip, c), src=x_ref) for j, chip in enumerate(chips)]
        for cp in first:
            cp.start()
        passed = [copy(4 + j, (*chip, c), sibling) for j, chip in enumerate(chips)]
        for j, chip in enumerate(chips):
            copy(1 + j, (*chip, c), me).wait_recv()
            passed[j].start()
        copy(0, sibling, me).wait_recv()
        for j, chip in enumerate(chips):
            copy(4 + j, (*chip, 1 - c), me).wait_recv()
        for cp in first + passed:
            cp.wait_send()
        mine.wait()
        total = all_ref[0:r, :]
        for k in range(1, n_dev):
            total = total + all_ref[k * r:(k + 1) * r, :]
        sum_ref[...] = total

    vm = pl.BlockSpec(memory_space=pltpu.VMEM)
    _, total = pl.pallas_call(
        body, name="small_allreduce",
        out_shape=[jax.ShapeDtypeStruct((n_dev * r, width), buf.dtype), jax.ShapeDtypeStruct((r, width), buf.dtype)],
        in_specs=[vm], out_specs=[vm, vm],
        scratch_shapes=[pltpu.SemaphoreType.DMA((7,)), pltpu.SemaphoreType.DMA((7,)), pltpu.SemaphoreType.DMA],
    )(buf)
    return total


def _row_tile(rows, cap):
    if rows <= cap:
        return rows
    t = (cap // 8) * 8
    while t >= 8:
        if rows % t == 0:
            return t
        t -= 8
    return rows


def _adamw(w, g, m, v, name):
    shape = w.shape
    cols = shape[-1] if len(shape) <= 3 else shape[-2] * shape[-1]
    lead = len(shape) == 3
    w2, g2, m2, v2 = (a if lead else a.reshape(-1, cols) for a in (w, g, m, v))
    rows = shape[1] if lead else w2.shape[0]
    tr, tc = _row_tile(rows, 256), cols
    if tr == rows and rows > 256:
        tc = _tile(cols, 256)

    def body(w_ref, g_ref, m_ref, v_ref, d_ref, mo_ref, vo_ref):
        gv = g_ref[...]
        mn = ADAM_B1 * m_ref[...] + (1.0 - ADAM_B1) * gv
        vn = ADAM_B2 * v_ref[...] + (1.0 - ADAM_B2) * (gv * gv)
        m_hat = mn / (1.0 - ADAM_B1 ** ADAM_STEP)
        v_hat = vn / (1.0 - ADAM_B2 ** ADAM_STEP)
        d_ref[...] = -ADAM_LR * (m_hat / (jnp.sqrt(v_hat) + ADAM_EPS) + ADAM_WD * w_ref[...])
        mo_ref[...] = mn
        vo_ref[...] = vn

    blk = (pl.BlockSpec((None, tr, tc), lambda i, j: (0, i, j)) if lead else pl.BlockSpec((tr, tc), lambda i, j: (i, j)))
    outs = pl.pallas_call(
        body, name=name, grid=(rows // tr, cols // tc), in_specs=[blk] * 4, out_specs=[blk] * 3,
        out_shape=[jax.ShapeDtypeStruct(w2.shape, F32)] * 3,
        compiler_params=_cparams(("parallel", "parallel")))(w2, g2, m2, v2)
    return tuple(o.reshape(shape) for o in outs)


_WEIGHT_NAMES = ('w_in', 'conv_w', 'dn_a_log', 'dn_dt_bias', 'dn_norm_w', 'q_norm_w', 'w_uq', 'kv_norm_w', 'w_uk',
                 'w_uv', 'w_br_dn', 'w_br_mla', 'w_o', 'ln1_g', 'ln1_b', 'w_ffn_in', 'w_ffn_out', 'w_ple',
                 'w_ple_gate', 'ln2_g', 'ln2_b')
_SMALL_NAMES = ('ln1_g', 'ln1_b', 'ln2_g', 'ln2_b', 'q_norm_w', 'kv_norm_w', 'dn_norm_w', 'dn_a_log', 'dn_dt_bias')
_SMALL_GROUP = 8
_CONV_SMALL_ROW = len(_SMALL_NAMES) * _SMALL_GROUP
_CONV_SMALL_ROWS = DN_CONV * QKV_W // FLAT_W


def _pack_small(gw):
    rows = [jnp.pad(gw[n][None, :], ((0, _SMALL_GROUP - 1), (0, FLAT_W - gw[n].shape[0]))) for n in _SMALL_NAMES]
    rows.append(jnp.pad(gw['conv_w'].reshape(_CONV_SMALL_ROWS, FLAT_W), ((0, SMALL_ROWS - _CONV_SMALL_ROW - _CONV_SMALL_ROWS), (0, 0))))
    return jnp.concatenate(rows, axis=0)


class _Exchange:
    def __init__(self, local, me_chip, c_arr):
        self.local, self.me_chip, self.c_arr = local, me_chip, c_arr
        self.parts = self.arrived = None

    def gather_send(self):
        return _ride_gather_send(self.local)

    def gather_pass(self, sent):
        return _ride_gather_pass(sent)

    def weights(self, gathered):
        return _prep_weights(_unpack_rest(gathered, self.local, self.me_chip))

    def pair_send(self, g):
        self.gbufs = _pack_grads_rest(_unprep_grads_early(g))
        return _ride_pair_exchange(self.gbufs)

    def reduce_send(self, got):
        self.parts = [_pair_add(g_, r_, self.c_arr, tr, "pair_add_%d" % (i + 1))
                      for i, (g_, r_, tr) in enumerate(zip(self.gbufs, got, _ADD_TILES[1:]))]
        return _ride_chip_exchange(self.parts)

    def reduce_arrived(self, arrived):
        self.arrived = list(arrived)

    def in_send(self, g):
        g_in = [_shard_columns(_w_in_grad(g), W_IN_SHARD)]
        got = _reduce_pair_exchange(g_in, "reduce_pair_exchange_w_in")
        self.part_in = _pair_add(g_in[0], got[0], self.c_arr, _ADD_TILES[0], "pair_add_0")
        return _ride_chip_exchange([self.part_in])

    def in_arrived(self, arrived):
        self.arrived_in = list(arrived)


def kernel(x, p, positions, w_in, conv_w, dn_a_log, dn_dt_bias, dn_norm_w, q_norm_w, w_uq, kv_norm_w, w_uk, w_uv, w_br_dn, w_br_mla, w_o, ln1_g, ln1_b, w_ffn_in, w_ffn_out, w_ple, w_ple_gate, ln2_g, ln2_b, loss_target, m_w_in, m_conv_w, m_dn_a_log, m_dn_dt_bias, m_dn_norm_w, m_q_norm_w, m_w_uq, m_kv_norm_w, m_w_uk, m_w_uv, m_w_br_dn, m_w_br_mla, m_w_o, m_ln1_g, m_ln1_b, m_w_ffn_in, m_w_ffn_out, m_w_ple, m_w_ple_gate, m_ln2_g, m_ln2_b, v_w_in, v_conv_w, v_dn_a_log, v_dn_dt_bias, v_dn_norm_w, v_q_norm_w, v_w_uq, v_kv_norm_w, v_w_uk, v_w_uv, v_w_br_dn, v_w_br_mla, v_w_o, v_ln1_g, v_ln1_b, v_w_ffn_in, v_w_ffn_out, v_w_ple, v_w_ple_gate, v_ln2_g, v_ln2_b):
    ws = dict(w_in=w_in, conv_w=conv_w, dn_a_log=dn_a_log, dn_dt_bias=dn_dt_bias, dn_norm_w=dn_norm_w, q_norm_w=q_norm_w,
              w_uq=w_uq, kv_norm_w=kv_norm_w, w_uk=w_uk, w_uv=w_uv, w_br_dn=w_br_dn, w_br_mla=w_br_mla, w_o=w_o,
              ln1_g=ln1_g, ln1_b=ln1_b, w_ffn_in=w_ffn_in, w_ffn_out=w_ffn_out, w_ple=w_ple, w_ple_gate=w_ple_gate,
              ln2_g=ln2_g, ln2_b=ln2_b)
    ms = dict(w_in=m_w_in, conv_w=m_conv_w, dn_a_log=m_dn_a_log, dn_dt_bias=m_dn_dt_bias, dn_norm_w=m_dn_norm_w,
              q_norm_w=m_q_norm_w, w_uq=m_w_uq, kv_norm_w=m_kv_norm_w, w_uk=m_w_uk, w_uv=m_w_uv, w_br_dn=m_w_br_dn,
              w_br_mla=m_w_br_mla, w_o=m_w_o, ln1_g=m_ln1_g, ln1_b=m_ln1_b, w_ffn_in=m_w_ffn_in, w_ffn_out=m_w_ffn_out,
              w_ple=m_w_ple, w_ple_gate=m_w_ple_gate, ln2_g=m_ln2_g, ln2_b=m_ln2_b)
    vs = dict(w_in=v_w_in, conv_w=v_conv_w, dn_a_log=v_dn_a_log, dn_dt_bias=v_dn_dt_bias, dn_norm_w=v_dn_norm_w,
              q_norm_w=v_q_norm_w, w_uq=v_w_uq, kv_norm_w=v_kv_norm_w, w_uk=v_w_uk, w_uv=v_w_uv, w_br_dn=v_w_br_dn,
              w_br_mla=v_w_br_mla, w_o=v_w_o, ln1_g=v_ln1_g, ln1_b=v_ln1_b, w_ffn_in=v_w_ffn_in, w_ffn_out=v_w_ffn_out,
              w_ple=v_w_ple, w_ple_gate=v_w_ple_gate, ln2_g=v_ln2_g, ln2_b=v_ln2_b)
    mx, my, mc = lax.axis_index("x"), lax.axis_index("y"), lax.axis_index("c")

    me_chip = 2 * mx + my
    c_arr = jnp.reshape(mc, (1,)).astype(jnp.int32)
    me_arr = jnp.reshape(me_chip, (1,)).astype(jnp.int32)
    sharded = ('w_in', 'w_ffn_in') + tuple(name for name, _, _ in _FLATB_PIECES)

    local = _pack_shards({name: ws[name][0] for name in sharded}, conv_w[0])
    (gathered_in,) = _gather_shards(local[:1], "gather_w_in")
    w_in_full, conv_full = _unpack_w_in(gathered_in, local[0], me_chip)
    small = {n: ws[n][0] for n in _SMALL_NAMES}
    small['conv_w'] = conv_full
    sp = _prep_small(small)
    cosb, sinb = _rope_tables(positions[0])
    exch = _Exchange(local[1:], me_chip, c_arr)

    loss_lanes, dx, g = _local_step(x[0], p[0, 0], cosb, sinb, loss_target[0], _prep_w_in(w_in_full), sp, exch)
    gw = _unprep_grads_late(g)
    loss = lax.psum(jnp.sum(loss_lanes), ("x", "y", "c"))

    parts = [exch.part_in] + exch.parts
    arrived = exch.arrived_in + exch.arrived
    mine = [_chip_add(p_, r_, me_arr, tr, "chip_add_%d" % i) for i, (p_, r_, tr) in enumerate(zip(parts, arrived, _ADD_TILES))]
    reduced = _unpack_reduced(mine, _reduce_pair_share(mine, "reduce_pair_share"), mc)
    tot = _small_allreduce(_pack_small(gw))
    gred = {name: reduced[name][None] for name in sharded}
    for i, n in enumerate(_SMALL_NAMES):
        gred[n] = tot[i * _SMALL_GROUP, :ws[n].shape[1]][None]
    conv_tot = tot[_CONV_SMALL_ROW:_CONV_SMALL_ROW + _CONV_SMALL_ROWS].reshape(DN_CONV, QKV_W)
    gred['conv_w'] = lax.dynamic_slice_in_dim(conv_tot, (2 * mx + my) * _CONV_SHARD, _CONV_SHARD, axis=1)[None]

    deltas, new_m, new_v = {}, {}, {}
    for n in _WEIGHT_NAMES:
        if n == 'w_in':
            tr_ = lambda a: jnp.transpose(a, (0, 2, 1))
            g_t = tr_(gred[n].reshape(ws[n].shape))
            outs = _adamw(tr_(ws[n]), g_t, tr_(ms[n]), tr_(vs[n]), "adamw_" + n)
            gred[n] = tr_(g_t)
            deltas[n], new_m[n], new_v[n] = (tr_(o) for o in outs)
            continue
        gred[n] = gred[n].reshape(ws[n].shape)
        deltas[n], new_m[n], new_v[n] = _adamw(ws[n], gred[n], ms[n], vs[n], "adamw_" + n)
    return (loss, dx[None], *[gred[n] for n in _WEIGHT_NAMES], *[deltas[n] for n in _WEIGHT_NAMES],
            *[new_m[n] for n in _WEIGHT_NAMES], *[new_v[n] for n in _WEIGHT_NAMES])
```

```python
import functools
import math

import jax
import jax.numpy as jnp
from jax import lax
from jax.experimental import pallas as pl
from jax.experimental.pallas import tpu as pltpu

F32 = jnp.float32
_CDT = jnp.bfloat16
_HI = lax.Precision.HIGHEST
_MESH = pl.DeviceIdType.MESH

D_MODEL = 1024
PLE_DIM = 256
HEADS = 8
DN_DK = 128
DN_CHUNK = 64
DN_CONV = 4
QKV_W = 3 * HEADS * DN_DK
Q_LORA = 384
KV_LORA = 256
NOPE = 128
ROPE = 64
ROPE_PAD = 128
FFN_HIDDEN = 2816
D_IN = 6864
ROPE_BASE = 10000.0
ALPHA = 2.0 ** 0.25
ATT_SCALE = (NOPE + ROPE) ** -0.5
NEG_BIG = -1e30
ADAM_LR, ADAM_B1, ADAM_B2, ADAM_EPS, ADAM_WD, ADAM_STEP = 0.001, 0.9, 0.999, 1e-08, 0.01, 10

LANE = 128
VMEM_LIMIT = 56 * 1024 * 1024
MM_VMEM_BUDGET = 40 * 1024 * 1024
N_SHARD = 4
FLAT_W = 1024
SMALL_ROWS = 88


def _tile(dim, cap):
    if dim <= cap:
        return dim
    t = (cap // LANE) * LANE
    while t >= LANE:
        if dim % t == 0:
            return t
        t -= LANE
    return dim


def _cparams(sem):
    return pltpu.CompilerParams(dimension_semantics=sem, vmem_limit_bytes=VMEM_LIMIT)


def _mm(a, b, *, name, ta=False, tb=False, add=None, add_scale=1.0, out_dtype=F32, heads=None,
        a_head=None, b_head=None, out_head=None, dims=None, tm=1408, tn=1408):
    m, n, k = dims
    tm, tn = _tile(m, tm), _tile(n, tn)
    sa, sb, so = a.dtype.itemsize, b.dtype.itemsize, jnp.dtype(out_dtype).itemsize

    def vmem_need(tk_):
        acc = tm * tn * 4 if tk_ < k else 0
        extra = 2 * tm * tn * 4 if add is not None else 0
        return 2 * (tm * tk_ * sa + tk_ * tn * sb) + 2 * tm * tn * so + acc + extra

    tk = k
    while vmem_need(tk) > MM_VMEM_BUDGET and tk > LANE:
        smaller = _tile(k, tk - LANE)
        if smaller >= tk:
            break
        tk = smaller
    nk = k // tk
    hgrid = () if heads is None else (heads,)
    off = len(hgrid)

    def spec(rows, cols, rtile, ctile, rsel, csel, layout):
        def idx(*g):
            h = g[0] if off else 0
            ri, ci = g[off + rsel], g[off + csel]
            if layout == 'lead':
                return (h, ri, ci)
            if layout == 'col':
                return (ri, h * (cols // ctile) + ci)
            return (ri, ci)
        if layout == 'lead':
            return pl.BlockSpec((None, rtile, ctile), idx)
        return pl.BlockSpec((rtile, ctile), idx)

    a_spec = spec(k, m, tk, tm, 2, 0, a_head) if ta else spec(m, k, tm, tk, 0, 2, a_head)
    b_spec = spec(n, k, tn, tk, 1, 2, b_head) if tb else spec(k, n, tk, tn, 2, 1, b_head)
    o_spec = spec(m, n, tm, tn, 0, 1, out_head)
    in_specs = [a_spec, b_spec]
    args = [a, b]
    if add is not None:
        in_specs.append(spec(m, n, tm, tn, 0, 1, out_head))
        args.append(add)
    dn = (((0 if ta else 1,), (1 if tb else 0,)), ((), ()))

    def body(*refs):
        a_ref, b_ref = refs[0], refs[1]
        prod = lax.dot_general(a_ref[...].astype(_CDT), b_ref[...].astype(_CDT), dn, preferred_element_type=F32)
        if nk == 1:
            o_ref = refs[-1]
            if add is not None:
                prod = prod + refs[2][...].astype(F32) * add_scale
            o_ref[...] = prod.astype(out_dtype)
            return
        o_ref, acc_ref = refs[-2], refs[-1]
        kk = pl.program_id(off + 2)

        @pl.when(kk == 0)
        def _():
            if add is not None:
                acc_ref[...] = refs[2][...].astype(F32) * add_scale
            else:
                acc_ref[...] = jnp.zeros_like(acc_ref)

        acc_ref[...] += prod

        @pl.when(kk == nk - 1)
        def _():
            o_ref[...] = acc_ref[...].astype(out_dtype)

    if out_head == 'lead':
        oshape = (heads, m, n)
    elif out_head == 'col':
        oshape = (m, heads * n)
    else:
        oshape = (m, n)
    sem = ("parallel",) * (off + 2) + ("arbitrary",)
    return pl.pallas_call(
        body, name=name, grid=hgrid + (m // tm, n // tn, nk), in_specs=in_specs, out_specs=o_spec,
        out_shape=jax.ShapeDtypeStruct(oshape, out_dtype),
        scratch_shapes=[pltpu.VMEM((tm, tn), F32)] if nk > 1 else [],
        compiler_params=_cparams(sem))(*args)


def _mm2(a, b, **kw):
    ta, tb = kw.get('ta', False), kw.get('tb', False)
    m = a.shape[1] if ta else a.shape[0]
    k = a.shape[0] if ta else a.shape[1]
    n = b.shape[0] if tb else b.shape[1]
    return _mm(a, b, dims=(m, n, k), **kw)


def _rowwise(fn, rows, bcast, outs, reds=(), *, name, tm=256, heads=None):
    t = rows[0][0].shape[0]
    tm = min(tm, t)
    hn = 1 if heads is None else heads
    in_specs, args = [], []
    for arr, width, base, per_head in rows:
        in_specs.append(pl.BlockSpec((tm, width), functools.partial(
            lambda i, h, base, per_head: (i, base + (h if per_head else 0)), base=base, per_head=per_head)))
        args.append(arr)
    for arr in bcast:
        in_specs.append(pl.BlockSpec(arr.shape, lambda i, h: (0, 0)))
        args.append(arr)
    out_specs, out_shape = [], []
    for total, width, per_head, dt in outs:
        out_specs.append(pl.BlockSpec((tm, width), functools.partial(
            lambda i, h, per_head: (i, h if per_head else 0), per_head=per_head)))
        out_shape.append(jax.ShapeDtypeStruct((t, total), dt))
    for shp in reds:
        out_specs.append(pl.BlockSpec(shp, lambda i, h: (0, 0)))
        out_shape.append(jax.ShapeDtypeStruct(shp, F32))
    n_in, n_out, n_red = len(args), len(outs), len(reds)

    def body(*refs):
        i, h = pl.program_id(0), pl.program_id(1)
        vals = fn(h, *[r[...] for r in refs[:n_in]])
        for r, v in zip(refs[n_in:n_in + n_out], vals[:n_out]):
            r[...] = v.astype(r.dtype)
        if n_red:
            @pl.when((i == 0) & (h == 0))
            def _():
                for r in refs[n_in + n_out:]:
                    r[...] = jnp.zeros_like(r)
            for r, v in zip(refs[n_in + n_out:], vals[n_out:]):
                r[...] += v

    sem = ("arbitrary", "arbitrary") if n_red else ("parallel", "parallel")
    res = pl.pallas_call(body, name=name, grid=(t // tm, hn), in_specs=in_specs, out_specs=out_specs,
                         out_shape=out_shape, compiler_params=_cparams(sem))(*args)
    return tuple(res)


def _sigmoid(x):
    return 1.0 / (1.0 + jnp.exp(-x))


def _silu(x):
    return x * _sigmoid(x)


def _softplus(x):
    return jnp.maximum(x, 0.0) + jnp.log(1.0 + jnp.exp(-jnp.abs(x)))


def _layer_norm(t, g, b):
    mu = jnp.mean(t, axis=-1, keepdims=True)
    d = t - mu
    var = jnp.mean(d * d, axis=-1, keepdims=True)
    return d * lax.rsqrt(var + 1e-5) * g + b


def _rms_norm(t, w):
    return t * lax.rsqrt(jnp.mean(t * t, axis=-1, keepdims=True) + 1e-6) * w


def _swap_rope_halves(t):
    lane = lax.broadcasted_iota(jnp.int32, t.shape, 1) % ROPE_PAD
    n = t.shape[1]
    up = pltpu.roll(t, n - ROPE // 2, axis=1)
    dn = pltpu.roll(t, ROPE // 2, axis=1)
    return jnp.where(lane < ROPE // 2, up, jnp.where(lane < ROPE, dn, 0.0))


def _rope(t, cosb, sinb):
    reps = t.shape[1] // ROPE_PAD
    c = jnp.tile(cosb, (1, reps)) if reps > 1 else cosb
    s = jnp.tile(sinb, (1, reps)) if reps > 1 else sinb
    return t * c + _swap_rope_halves(t) * s


def _rope_bwd(d, cosb, sinb):
    reps = d.shape[1] // ROPE_PAD
    c = jnp.tile(cosb, (1, reps)) if reps > 1 else cosb
    s = jnp.tile(sinb, (1, reps)) if reps > 1 else sinb
    return d * c + _swap_rope_halves(d * s)


_CONV_ROWS = 256
_CONV_COLS = 256


def _conv_window(ref, r0, lo, hi, t):
    parts = []
    start, stop = r0 - lo, r0 + _CONV_ROWS + hi
    if start < 0:
        parts.append(jnp.zeros((-start, ref.shape[1]), F32))
        start = 0
    tail = max(stop - t, 0)
    parts.append(ref[start:stop - tail, :].astype(F32))
    if tail:
        parts.append(jnp.zeros((tail, ref.shape[1]), F32))
    return parts[0] if len(parts) == 1 else jnp.concatenate(parts, axis=0)


def _conv_taps(win, w_ref, n_out):
    acc = win[8:8 + n_out] * w_ref[DN_CONV - 1:DN_CONV, :]
    for i in range(DN_CONV - 1):
        acc = acc + pltpu.roll(win, DN_CONV - 1 - i, axis=0)[8:8 + n_out] * w_ref[i:i + 1, :]
    return acc


def _conv_silu(x, w):
    t, ch = x.shape

    def body(x_ref, w_ref, o_ref):
        for r in range(t // _CONV_ROWS):
            r0 = r * _CONV_ROWS
            c = _conv_taps(_conv_window(x_ref, r0, 8, 0, t), w_ref, _CONV_ROWS)
            o_ref[r0:r0 + _CONV_ROWS, :] = _silu(c)

    return pl.pallas_call(
        body, name="conv_silu", grid=(ch // _CONV_COLS,),
        in_specs=[pl.BlockSpec((t, _CONV_COLS), lambda j: (0, j)), pl.BlockSpec((DN_CONV, _CONV_COLS), lambda j: (0, j))],
        out_specs=pl.BlockSpec((t, _CONV_COLS), lambda j: (0, j)),
        out_shape=jax.ShapeDtypeStruct((t, ch), F32), compiler_params=_cparams(("parallel",)))(x, w)


def _conv_silu_bwd(x, w, dys):
    t, ch = x.shape
    per = ch // len(dys) // _CONV_COLS

    def body(x_ref, w_ref, *rest):
        dy_refs, (dx_ref, dw_ref) = rest[:len(dys)], rest[len(dys):]
        sec = pl.program_id(0) // per
        dws = [jnp.zeros((1, _CONV_COLS), F32) for _ in range(DN_CONV)]
        for r in range(t // _CONV_ROWS):
            r0 = r * _CONV_ROWS
            n_ext = _CONV_ROWS + 8
            xw = _conv_window(x_ref, r0, 8, 8, t)
            c = _conv_taps(xw, w_ref, n_ext)
            sg = _sigmoid(c)
            dy = _conv_window(dy_refs[-1], r0, 0, 8, t)
            for k in range(len(dys) - 2, -1, -1):
                dy = jnp.where(sec == k, _conv_window(dy_refs[k], r0, 0, 8, t), dy)
            ds = dy * (sg * (1.0 + c * (1.0 - sg)))
            x0 = xw[8:8 + _CONV_ROWS]
            dx = jnp.zeros((_CONV_ROWS, _CONV_COLS), F32)
            for i in range(DN_CONV):
                sh = DN_CONV - 1 - i
                ds_up = (ds if sh == 0 else pltpu.roll(ds, n_ext - sh, axis=0))[:_CONV_ROWS]
                dx = dx + ds_up * w_ref[i:i + 1, :]
                dws[i] = dws[i] + jnp.sum(x0 * ds_up, axis=0, keepdims=True)
            dx_ref[r0:r0 + _CONV_ROWS, :] = dx.astype(dx_ref.dtype)
        for i in range(DN_CONV):
            dw_ref[i:i + 1, :] = dws[i]

    blk = pl.BlockSpec((t, _CONV_COLS), lambda j: (0, j))
    wblk = pl.BlockSpec((DN_CONV, _CONV_COLS), lambda j: (0, j))
    dy_specs = [pl.BlockSpec((t, _CONV_COLS), functools.partial(lambda j, k: (0, jnp.clip(j - k * per, 0, per - 1)), k=k))
                for k in range(len(dys))]
    return pl.pallas_call(
        body, name="conv_silu_bwd", grid=(ch // _CONV_COLS,), in_specs=[blk, wblk] + dy_specs, out_specs=[blk, wblk],
        out_shape=[jax.ShapeDtypeStruct((t, ch), _CDT), jax.ShapeDtypeStruct((DN_CONV, ch), F32)],
        compiler_params=_cparams(("arbitrary",)))(x, w, *dys)


_PA_ROWS = 512


def _bmm(a, b, spec, exact=False):
    if exact:
        return jnp.einsum(spec, a, b, precision=_HI, preferred_element_type=F32)
    return jnp.einsum(spec, a.astype(_CDT), b.astype(_CDT), preferred_element_type=F32)


def _split16(a):
    hi = a.astype(jnp.bfloat16)
    return hi, (a - hi.astype(F32)).astype(jnp.bfloat16)


def _bmm3(a, b, spec):
    ah, al = _split16(a)
    bh, bl = _split16(b)
    e = lambda p, q: jnp.einsum(spec, p, q, preferred_element_type=F32)
    return e(ah, bh) + (e(ah, bl) + e(al, bh))


def _split3(b):
    b0 = b.astype(jnp.bfloat16)
    r1 = b - b0.astype(F32)
    b1 = r1.astype(jnp.bfloat16)
    return b0, b1, (r1 - b1.astype(F32)).astype(jnp.bfloat16)


@functools.partial(jax.custom_vjp, nondiff_argnums=(2, 3))
def _select_mm(sel, b, spec, spec_t):
    return sum(jnp.einsum(spec, sel, t, preferred_element_type=F32) for t in _split3(b))


def _select_mm_fwd(sel, b, spec, spec_t):
    return _select_mm(sel, b, spec, spec_t), sel


def _select_mm_bwd(spec, spec_t, sel, ct):
    return jnp.zeros_like(sel), sum(jnp.einsum(spec_t, sel, t, preferred_element_type=F32) for t in _split3(ct))


_select_mm.defvjp(_select_mm_fwd, _select_mm_bwd)


def _tri_inverse(l_mat, eye):
    pw = -l_mat
    t_inv = eye + pw
    for _ in range(5):
        pw = _bmm3(pw, pw, 'bij,bjk->bik')
        t_inv = t_inv + _bmm3(t_inv, pw, 'bij,bjk->bik')
    return t_inv


@jax.custom_vjp
def _tri_inverse_saved(l_mat, t_saved):
    return t_saved


def _tri_inverse_saved_fwd(l_mat, t_saved):
    return t_saved, t_saved


def _tri_inverse_saved_bwd(t_saved, dt):
    left = _bmm3(t_saved, dt, 'bji,bjk->bik')
    return -_bmm3(left, t_saved, 'bij,bkj->bik'), jnp.zeros_like(t_saved)


_tri_inverse_saved.defvjp(_tri_inverse_saved_fwd, _tri_inverse_saved_bwd)


def _phase_a(h, q, k, v, ba, alog, dtb, t_saved=None):
    r = q.shape[0]
    nb = r // DN_CHUNK
    c = DN_CHUNK
    lane = lax.broadcasted_iota(jnp.int32, (1, LANE), 1)
    selb = (lane == h).astype(F32)
    sela = (lane == h + HEADS).astype(F32)
    b_raw = jnp.sum(ba * selb, axis=1, keepdims=True)
    a_raw = jnp.sum(ba * sela, axis=1, keepdims=True)
    al = jnp.sum(alog * selb, axis=1, keepdims=True)
    dt = jnp.sum(dtb * selb, axis=1, keepdims=True)
    beta = jnp.broadcast_to(_sigmoid(b_raw), (r, LANE))
    g = jnp.broadcast_to(-jnp.exp(al) * _softplus(a_raw + dt), (r, LANE))
    qn = q * lax.rsqrt(jnp.sum(q * q, -1, keepdims=True) + 1e-6) * (DN_DK ** -0.5)
    kn = k * lax.rsqrt(jnp.sum(k * k, -1, keepdims=True) + 1e-6)
    q3, k3, v3 = qn.reshape(nb, c, LANE), kn.reshape(nb, c, LANE), v.reshape(nb, c, LANE)
    b3, g3 = beta.reshape(nb, c, LANE), g.reshape(nb, c, LANE)
    ri = lax.broadcasted_iota(jnp.int32, (nb, c, c), 1)
    ci = lax.broadcasted_iota(jnp.int32, (nb, c, c), 2)
    tril, strict = ri >= ci, ri > ci
    gc = _select_mm(tril.astype(jnp.bfloat16), g3, 'bij,bjd->bid', 'bij,bid->bjd')
    onehot = (lax.broadcasted_iota(jnp.int32, (nb, c, LANE), 2) == 0).astype(jnp.bfloat16)
    g_row = _select_mm(onehot, gc, 'bid,bjd->bij', 'bid,bij->bjd')
    diff = gc[:, :, :c] - g_row
    decay = jnp.where(tril, jnp.exp(jnp.where(tril, diff, 0.0)), 0.0)
    kb = k3 * b3
    l_mat = jnp.where(strict, _bmm(kb, k3, 'bid,bjd->bij') * decay, 0.0)
    if t_saved is None:
        t_inv = _tri_inverse(l_mat, (ri == ci).astype(F32))
    else:
        t_inv = _tri_inverse_saved(l_mat, t_saved.reshape(nb, c, c))
    eg = jnp.exp(gc)
    u = _bmm(t_inv, v3 * b3, 'bij,bje->bie')
    w = _bmm(t_inv, kb * eg, 'bij,bje->bie')
    intra = jnp.where(tril, _bmm(q3, k3, 'bid,bjd->bij') * decay, 0.0)
    qd = q3 * eg
    gl = jnp.sum(g3, axis=1, keepdims=True)
    kt = k3 * jnp.exp(gl - gc)
    outs = (u.reshape(r, LANE), w.reshape(r, LANE), qd.reshape(r, LANE), kt.reshape(r, LANE),
            intra.reshape(r, c), gl.reshape(nb, LANE))
    if t_saved is not None:
        return outs
    qd2 = qd - _bmm(intra, w, 'bij,bjd->bid')
    au = _bmm(intra, u, 'bij,bje->bie')
    return outs + (t_inv.reshape(r, c), qd2.reshape(r, LANE), au.reshape(r, LANE))


def _pa_specs(t):
    rr = min(_PA_ROWS, t)
    nb = rr // DN_CHUNK
    qkv = [pl.BlockSpec((rr, LANE), functools.partial(lambda i, h, o: (i, o + h), o=o)) for o in (0, HEADS, 2 * HEADS)]
    ba = pl.BlockSpec((rr, LANE), lambda i, h: (i, 3))
    vec = pl.BlockSpec((1, LANE), lambda i, h: (0, 0))
    row = pl.BlockSpec((rr, LANE), lambda i, h: (i, h))
    intra = pl.BlockSpec((None, rr, DN_CHUNK), lambda i, h: (h, i, 0))
    gl = pl.BlockSpec((nb, LANE), lambda i, h: (i, h))
    return rr, qkv, ba, vec, row, intra, gl


def _grid_ends(grid):
    first = lambda: functools.reduce(jnp.logical_and, [pl.program_id(a) == 0 for a in range(len(grid))])
    last = lambda: functools.reduce(jnp.logical_and, [pl.program_id(a) == n - 1 for a, n in enumerate(grid)])
    return first, last


def _delta_local(qkv_act, pm, alog, dtb, rider=None):
    t = qkv_act.shape[0]
    rr, qkv, ba, vec, row, intra, gl = _pa_specs(t)

    def body(q, k, v, b, al, dt, *outs):
        vals = _phase_a(pl.program_id(1), q[...], k[...], v[...], b[...], al[...], dt[...])
        for o, val in zip(outs, vals):
            o[...] = val

    wide = jax.ShapeDtypeStruct((t, HEADS * LANE), F32)
    sq = jax.ShapeDtypeStruct((HEADS, t, DN_CHUNK), F32)
    grid = (t // rr, HEADS)
    return _carried_call(
        body, rider, *_grid_ends(grid), name="delta_local", grid=grid, in_specs=qkv + [ba, vec, vec],
        out_specs=[row] * 4 + [intra, gl, intra, row, row],
        out_shape=[wide] * 4 + [sq, jax.ShapeDtypeStruct((t // DN_CHUNK, HEADS * LANE), F32), sq, wide, wide],
        scratch_shapes=[], sem=("arbitrary", "arbitrary"), args=(qkv_act, qkv_act, qkv_act, pm, alog, dtb))


def _delta_local_bwd(qkv_act, pm, alog, dtb, t_inv, du, dw, dqd, dkt, dintra, dgl, rider=None):
    t = qkv_act.shape[0]
    rr, qkv, ba, vec, row, intra, gl = _pa_specs(t)

    def body(q, k, v, b, al, dt, ti, du_r, dw_r, dqd_r, dkt_r, di_r, dgl_r, dq_o, dk_o, dv_o, dba_o, dal_o, ddt_o):
        i, h = pl.program_id(0), pl.program_id(1)
        t_saved = ti[...]
        _, vjp = jax.vjp(lambda *a: _phase_a(h, *a, t_saved=t_saved), q[...], k[...], v[...], b[...], al[...], dt[...])
        dq, dk, dv, dba, dal, ddt = vjp((du_r[...], dw_r[...], dqd_r[...], dkt_r[...], di_r[...], dgl_r[...]))
        dq_o[...], dk_o[...], dv_o[...] = dq, dk, dv

        @pl.when(h == 0)
        def _():
            dba_o[...] = jnp.zeros_like(dba_o)

        @pl.when((h == 0) & (i == 0))
        def _():
            dal_o[...] = jnp.zeros_like(dal_o)
            ddt_o[...] = jnp.zeros_like(ddt_o)

        dba_o[...] += dba
        dal_o[...] += dal
        ddt_o[...] += ddt

    wide = jax.ShapeDtypeStruct((t, HEADS * LANE), F32)
    vshape = jax.ShapeDtypeStruct((1, LANE), F32)
    grid = (t // rr, HEADS)
    return _carried_call(
        body, rider, *_grid_ends(grid), name="delta_local_bwd", grid=grid,
        in_specs=qkv + [ba, vec, vec, intra] + [row] * 4 + [intra, gl],
        out_specs=[row] * 3 + [pl.BlockSpec((rr, LANE), lambda i, h: (i, 0)), vec, vec],
        out_shape=[wide] * 3 + [jax.ShapeDtypeStruct((t, LANE), F32), vshape, vshape],
        scratch_shapes=[], sem=("arbitrary", "arbitrary"),
        args=(qkv_act, qkv_act, qkv_act, pm, alog, dtb, t_inv, du, dw, dqd, dkt, dintra, dgl))


_SCAN_ROWS = 512


def _dot(a, b, dn):
    return lax.dot_general(a.astype(_CDT), b.astype(_CDT), (dn, ((), ())), preferred_element_type=F32)


_NN = ((1,), (0,))
_NT = ((1,), (1,))
_TN = ((0,), (0,))


def _delta_scan(u, w, qd, kt, au, gl, rider=None):
    t = u.shape[0]
    rr = min(_SCAN_ROWS, t)
    nc = rr // DN_CHUNK

    def body(u_ref, w_ref, qd_ref, kt_ref, au_ref, gl_ref, o_ref, sall_ref, s_scr):
        @pl.when(pl.program_id(0) == 0)
        def _():
            s_scr[...] = jnp.zeros_like(s_scr)

        def chunk(c, carry):
            r0 = pl.multiple_of(c * DN_CHUNK, DN_CHUNK)
            rows = pl.ds(r0, DN_CHUNK)
            e = jnp.exp(gl_ref[pl.ds(c, 1), :])
            states = [s_scr[h] for h in range(HEADS)]
            u_c, w_c, qd_c, kt_c, au_c = u_ref[rows, :], w_ref[rows, :], qd_ref[rows, :], kt_ref[rows, :], au_ref[rows, :]
            o_new, s_new = [], []
            for h in range(HEADS):
                cs = slice(h * LANE, (h + 1) * LANE)
                s = states[h]
                both = _dot(jnp.concatenate([w_c[:, cs], qd_c[:, cs]], axis=0), s, _NN)
                v_new = u_c[:, cs] - both[:DN_CHUNK]
                o_new.append(both[DN_CHUNK:] + au_c[:, cs])
                s_new.append(s * e[:, cs] + _dot(kt_c[:, cs], v_new, _TN))
            o_ref[rows, :] = jnp.concatenate(o_new, axis=1)
            for h in range(HEADS):
                sall_ref[c, h] = states[h]
                s_scr[h] = s_new[h]
            return carry

        lax.fori_loop(0, nc, chunk, 0)

    row = pl.BlockSpec((rr, HEADS * LANE), lambda i: (i, 0))
    grid = (t // rr,)
    return _carried_call(
        body, rider, *_grid_ends(grid), name="delta_scan", grid=grid,
        in_specs=[row] * 5 + [pl.BlockSpec((nc, HEADS * LANE), lambda i: (i, 0))],
        out_specs=[row, pl.BlockSpec((nc, HEADS, LANE, LANE), lambda i: (i, 0, 0, 0))],
        out_shape=[jax.ShapeDtypeStruct((t, HEADS * LANE), F32),
                   jax.ShapeDtypeStruct((t // DN_CHUNK, HEADS, LANE, LANE), F32)],
        scratch_shapes=[pltpu.VMEM((HEADS, LANE, LANE), F32)], sem=("arbitrary",), args=(u, w, qd, kt, au, gl))


def _delta_scan_bwd(u, w, qd, kt, intra, gl, sall, do, rider=None):
    t = u.shape[0]
    rr = min(_SCAN_ROWS, t)
    nc = rr // DN_CHUNK
    ng = t // rr

    def body(u_ref, w_ref, qd_ref, kt_ref, a_ref, gl_ref, sall_ref, do_ref,
             du_ref, dw_ref, dqd_ref, dkt_ref, da_ref, dgl_ref, ds_scr):
        @pl.when(pl.program_id(0) == 0)
        def _():
            ds_scr[...] = jnp.zeros_like(ds_scr)

        def chunk(cc, carry):
            c = nc - 1 - cc
            r0 = pl.multiple_of(c * DN_CHUNK, DN_CHUNK)
            rows = pl.ds(r0, DN_CHUNK)
            e = jnp.exp(gl_ref[pl.ds(c, 1), :])
            states = [sall_ref[c, h] for h in range(HEADS)]
            ds_outs = [ds_scr[h] for h in range(HEADS)]
            u_a, w_a, kt_a, qd_a, do_a = u_ref[rows, :], w_ref[rows, :], kt_ref[rows, :], qd_ref[rows, :], do_ref[rows, :]
            a_a = [a_ref[h, rows, :] for h in range(HEADS)]
            da, dqd, dkt, du, dw, dgl, ds_new = [], [], [], [], [], [], []
            for h in range(HEADS):
                cs = slice(h * LANE, (h + 1) * LANE)
                s, ds_out = states[h], ds_outs[h]
                w_c, kt_c, qd_c, do_c = w_a[:, cs], kt_a[:, cs], qd_a[:, cs], do_a[:, cs]
                v_new = u_a[:, cs] - _dot(w_c, s, _NN)
                dv_new = _dot(a_a[h], do_c, _TN) + _dot(kt_c, ds_out, _NN)
                cots = jnp.concatenate([do_c, dv_new], axis=0)
                both = _dot(cots, s, _NT)
                dqd.append(both[:DN_CHUNK])
                dw.append(-both[DN_CHUNK:])
                da.append(_dot(do_c, v_new, _NT))
                dkt.append(_dot(v_new, ds_out, _NT))
                du.append(dv_new)
                eh = e[:, cs]
                dgl.append(jnp.broadcast_to(jnp.sum(ds_out * s, axis=0, keepdims=True) * eh, (8, LANE)))
                ds_new.append(ds_out * eh + _dot(jnp.concatenate([qd_c, -w_c], axis=0), cots, _TN))
            cat = lambda parts: jnp.concatenate(parts, axis=1)
            dqd_ref[rows, :], dkt_ref[rows, :], du_ref[rows, :], dw_ref[rows, :] = cat(dqd), cat(dkt), cat(du), cat(dw)
            dgl_ref[pl.ds(pl.multiple_of(c * 8, 8), 8), :] = cat(dgl)
            for h in range(HEADS):
                da_ref[h, rows, :] = da[h]
                ds_scr[h] = ds_new[h]
            return carry

        lax.fori_loop(0, nc, chunk, 0)

    rev = lambda i: (ng - 1 - i, 0)
    row = pl.BlockSpec((rr, HEADS * LANE), rev)
    a_spec = pl.BlockSpec((HEADS, rr, DN_CHUNK), lambda i: (0, ng - 1 - i, 0))
    gl_spec = pl.BlockSpec((nc, HEADS * LANE), rev)
    wide = jax.ShapeDtypeStruct((t, HEADS * LANE), F32)
    outs, carried = _carried_call(
        body, rider, *_grid_ends((ng,)), name="delta_scan_bwd", grid=(ng,),
        in_specs=[row] * 4 + [a_spec, gl_spec, pl.BlockSpec((nc, HEADS, LANE, LANE), lambda i: (ng - 1 - i, 0, 0, 0)), row],
        out_specs=[row] * 4 + [a_spec, pl.BlockSpec((nc * 8, HEADS * LANE), rev)],
        out_shape=[wide] * 4 + [jax.ShapeDtypeStruct((HEADS, t, DN_CHUNK), F32),
                                jax.ShapeDtypeStruct((t // DN_CHUNK * 8, HEADS * LANE), F32)],
        scratch_shapes=[pltpu.VMEM((HEADS, LANE, LANE), F32)], sem=("arbitrary",), args=(u, w, qd, kt, intra, gl, sall, do))
    return tuple(outs[:5]) + (outs[5].reshape(t // DN_CHUNK, 8, HEADS * LANE)[:, 0, :],), carried


_ATT_TILE = 512


def _kv_rows(j, tk):
    return pl.ds(pl.multiple_of(j * tk, tk), tk)


def _att_scores(ql, qr, ckv_ref, kr_ref, j, tk):
    ks = _kv_rows(j, tk)
    return (_dot(ql, ckv_ref[ks, :], _NT) + _dot(qr, kr_ref[ks, :], _NT)) * ATT_SCALE


def _diag_mask(s):
    qi = lax.broadcasted_iota(jnp.int32, s.shape, 0)
    ki = lax.broadcasted_iota(jnp.int32, s.shape, 1)
    return jnp.where(ki <= qi, s, NEG_BIG)


def _attention(qn, qr_pre, cosb, sinb, ckv, kr, wuk, wuv):
    t = ckv.shape[0]
    tq = min(_ATT_TILE, t)

    nl = tq // LANE

    def lane_fold(v, op):
        out = v[:, :LANE]
        for k in range(1, nl):
            out = op(out, v[:, k * LANE:(k + 1) * LANE])
        return out

    def body(qn_ref, qr_ref, cos_ref, sin_ref, ckv_ref, kr_ref, wuk_ref, wuv_ref, o_ref, lse_ref, qrope_ref, omla_ref,
             s_all, m_lanes, l_lanes, acc_scr):
        h, qi = pl.program_id(0), pl.program_id(1)
        q_lat = _dot(qn_ref[...], wuk_ref[h], _NT).astype(_CDT)
        q_rope = _rope(qr_ref[...], cos_ref[...], sin_ref[...]).astype(qrope_ref.dtype)
        qrope_ref[...] = q_rope
        m_lanes[...] = jnp.full_like(m_lanes, NEG_BIG)

        def scores(j, masked):
            s = _att_scores(q_lat, q_rope, ckv_ref, kr_ref, j, tq)
            if masked:
                s = _diag_mask(s)
            s_all[j] = s
            m_lanes[...] = jnp.maximum(m_lanes[...], lane_fold(s, jnp.maximum))

        def scores_body(j, carry):
            scores(j, False)
            return carry

        lax.fori_loop(0, qi, scores_body, 0)
        scores(qi, True)
        m = jnp.max(m_lanes[...], axis=-1, keepdims=True)
        mb = jnp.broadcast_to(m, (tq, LANE))
        l_lanes[...] = jnp.zeros_like(l_lanes)
        acc_scr[...] = jnp.zeros_like(acc_scr)

        def weigh(j, carry):
            s = s_all[j]
            p = jnp.concatenate([jnp.exp(s[:, k * LANE:(k + 1) * LANE] - mb) for k in range(nl)], axis=1)
            l_lanes[...] += lane_fold(p, jnp.add)
            acc_scr[...] += _dot(p, ckv_ref[_kv_rows(j, tq), :], _NN)
            return carry

        lax.fori_loop(0, qi + 1, weigh, 0)
        l = jnp.sum(l_lanes[...], axis=-1, keepdims=True)
        out = acc_scr[...] / l
        o_ref[...] = out
        lse_ref[...] = m + jnp.log(l)
        omla_ref[...] = _dot(out, wuv_ref[h], _NN).astype(omla_ref.dtype)

    col = pl.BlockSpec((tq, LANE), lambda h, i: (i, h))
    table = pl.BlockSpec((tq, ROPE_PAD), lambda h, i: (i, 0))
    wspec = pl.BlockSpec((HEADS, KV_LORA, NOPE), lambda h, i: (0, 0, 0))
    return pl.pallas_call(
        body, name="attention", grid=(HEADS, t // tq),
        in_specs=[col, col, table, table, pl.BlockSpec((t, KV_LORA), lambda h, i: (0, 0)),
                  pl.BlockSpec((t, ROPE_PAD), lambda h, i: (0, 0)), wspec, wspec],
        out_specs=[pl.BlockSpec((None, tq, KV_LORA), lambda h, i: (h, i, 0)),
                   pl.BlockSpec((None, tq, 1), lambda h, i: (h, i, 0)), col, col],
        out_shape=[jax.ShapeDtypeStruct((HEADS, t, KV_LORA), F32), jax.ShapeDtypeStruct((HEADS, t, 1), F32),
                   jax.ShapeDtypeStruct((t, HEADS * ROPE_PAD), _CDT), jax.ShapeDtypeStruct((t, HEADS * NOPE), _CDT)],
        scratch_shapes=[pltpu.VMEM((t // tq, tq, tq), F32), pltpu.VMEM((tq, LANE), F32), pltpu.VMEM((tq, LANE), F32),
                        pltpu.VMEM((tq, KV_LORA), F32)],
        compiler_params=_cparams(("parallel", "parallel")))(qn, qr_pre, cosb, sinb, ckv, kr, wuk, wuv)


def _attention_bwd(qn, qr, cosb, sinb, ckv, kr, wuk, wuv, out, lse, do_mla):
    t = ckv.shape[0]
    tq = min(_ATT_TILE, t)

    def body(qn_ref, qr_ref, cos_ref, sin_ref, ckv_ref, kr_ref, wuk_ref, wuv_ref, o_ref, lse_ref, do_ref,
             dql_ref, dqr_ref, dqn_ref, dckv_ref, dkr_ref, dql_scr, dqr_scr):
        h, qi = pl.program_id(0), pl.program_id(1)

        @pl.when((h == 0) & (qi == 0))
        def _():
            dckv_ref[...] = jnp.zeros_like(dckv_ref)
            dkr_ref[...] = jnp.zeros_like(dkr_ref)

        q_lat = _dot(qn_ref[...], wuk_ref[h], _NT).astype(_CDT)
        q_rope = qr_ref[...]
        d_out = _dot(do_ref[...], wuv_ref[h], _NT)
        d_o = d_out.astype(_CDT)
        lse_v = lse_ref[...]
        dsum = jnp.sum(d_out * o_ref[...], axis=-1, keepdims=True)
        dql_scr[...] = jnp.zeros_like(dql_scr)
        dqr_scr[...] = jnp.zeros_like(dqr_scr)

        def step(j, masked):
            ks = _kv_rows(j, tq)
            s = _att_scores(q_lat, q_rope, ckv_ref, kr_ref, j, tq)
            if masked:
                s = _diag_mask(s)
            p = jnp.exp(s - lse_v)
            kv = ckv_ref[ks, :]
            ds = (p * (_dot(d_o, kv, _NT) - dsum) * ATT_SCALE).astype(_CDT)
            pb = p.astype(_CDT)
            dql_scr[...] += _dot(ds, kv, _NN)
            dqr_scr[...] += _dot(ds, kr_ref[ks, :], _NN)
            dckv_ref[ks, :] += _dot(pb, d_o, _TN) + _dot(ds, q_lat, _TN)
            dkr_ref[ks, :] += _dot(ds, q_rope, _TN)

        def loop_body(j, carry):
            step(j, False)
            return carry

        lax.fori_loop(0, qi, loop_body, 0)
        step(qi, True)
        dql = dql_scr[...].astype(dql_ref.dtype)
        dql_ref[...] = dql
        dqn_ref[...] = _dot(dql, wuk_ref[h], _NN).astype(dqn_ref.dtype)
        dqr_ref[...] = _rope_bwd(dqr_scr[...], cos_ref[...], sin_ref[...]).astype(dqr_ref.dtype)

    lat = pl.BlockSpec((None, tq, KV_LORA), lambda h, i: (h, i, 0))
    col = pl.BlockSpec((tq, LANE), lambda h, i: (i, h))
    table = pl.BlockSpec((tq, ROPE_PAD), lambda h, i: (i, 0))
    kfull = pl.BlockSpec((t, KV_LORA), lambda h, i: (0, 0))
    rfull = pl.BlockSpec((t, ROPE_PAD), lambda h, i: (0, 0))
    wspec = pl.BlockSpec((HEADS, KV_LORA, NOPE), lambda h, i: (0, 0, 0))
    wide = jax.ShapeDtypeStruct((t, HEADS * LANE), _CDT)
    return pl.pallas_call(
        body, name="attention_bwd", grid=(HEADS, t // tq),
        in_specs=[col, col, table, table, kfull, rfull, wspec, wspec, lat,
                  pl.BlockSpec((None, tq, 1), lambda h, i: (h, i, 0)), col],
        out_specs=[lat, col, col, kfull, rfull],
        out_shape=[jax.ShapeDtypeStruct((HEADS, t, KV_LORA), _CDT), wide, wide,
                   jax.ShapeDtypeStruct((t, KV_LORA), F32), jax.ShapeDtypeStruct((t, ROPE_PAD), F32)],
        scratch_shapes=[pltpu.VMEM((tq, KV_LORA), F32), pltpu.VMEM((tq, ROPE_PAD), F32)],
        compiler_params=_cparams(("arbitrary", "arbitrary")))(qn, qr, cosb, sinb, ckv, kr, wuk, wuv, out, lse, do_mla)


_DX_ROWS = 256


def _dx_fused(pairs, add, add_scale, rider=None, name="b_dx"):
    t, d = add.shape
    tm = min(_DX_ROWS, t)
    n = len(pairs)

    def body(*refs):
        acc = refs[2 * n][...] * add_scale
        for i in range(n):
            acc = acc + _dot(refs[i][...], refs[n + i][...], _NT)
        refs[2 * n + 1][...] = acc

    in_specs = [pl.BlockSpec((tm, a.shape[1]), lambda i: (i, 0)) for a, _ in pairs]
    in_specs += [pl.BlockSpec(w.shape, lambda i: (0, 0)) for _, w in pairs]
    row = pl.BlockSpec((tm, d), lambda i: (i, 0))
    grid = (t // tm,)
    (dx,), carried = _carried_call(
        body, rider, *_grid_ends(grid), name=name, grid=grid, in_specs=in_specs + [row], out_specs=[row],
        out_shape=[jax.ShapeDtypeStruct((t, d), F32)], scratch_shapes=[], sem=("arbitrary",),
        args=tuple(a for a, _ in pairs) + tuple(w for _, w in pairs) + (add,))
    return dx, carried


def _ffn_in_swiglu(a, w):
    t, k = a.shape
    hid = w.shape[1] // 2
    tm, tn = _tile(t, 1024), _tile(hid, 1408)
    nj = hid // tn

    def body(a_ref, bg_ref, bu_ref, act_ref, gt_ref, up_ref):
        av = a_ref[...].astype(_CDT)
        gt = jnp.dot(av, bg_ref[...].astype(_CDT), preferred_element_type=F32)
        up = jnp.dot(av, bu_ref[...].astype(_CDT), preferred_element_type=F32)
        act_ref[...] = _swiglu(gt, up).astype(act_ref.dtype)
        gt_ref[...] = gt.astype(gt_ref.dtype)
        up_ref[...] = up.astype(up_ref.dtype)

    out = pl.BlockSpec((tm, tn), lambda i, j: (i, j))
    return pl.pallas_call(
        body, name="f_ffn_in_swiglu", grid=(t // tm, nj),
        in_specs=[pl.BlockSpec((tm, k), lambda i, j: (i, 0)), pl.BlockSpec((k, tn), lambda i, j: (0, j)),
                  pl.BlockSpec((k, tn), lambda i, j: (0, nj + j))],
        out_specs=[out] * 3, out_shape=[jax.ShapeDtypeStruct((t, hid), _CDT)] * 3,
        compiler_params=_cparams(("parallel", "parallel")))(a, w, w)


def _gated_norm(o, z, w):
    return _rms_norm(o, w) * _silu(z)


def _gated_norm_heads(o, z, w):
    heads = [_gated_norm(o[:, h * LANE:(h + 1) * LANE], z[:, h * LANE:(h + 1) * LANE], w) for h in range(HEADS)]
    return jnp.concatenate(heads, axis=1)


def _mla_pre(ckv, krp, cq, cosb, sinb, qw, kw):
    return _rms_norm(cq, qw), _rms_norm(ckv, kw), _rope(krp, cosb, sinb)


def _merge(gg, y_dn, y_mla):
    return _sigmoid(gg[:, :D_MODEL]) * y_dn + _sigmoid(gg[:, D_MODEL:]) * y_mla


def _ln1(xv, attn_out, g, b):
    return _layer_norm(ALPHA * xv + attn_out, g, b)


def _final(h1, ffn, gate_pre, ple_proj, g, b):
    return _layer_norm(ALPHA * h1 + ffn + _sigmoid(gate_pre) * ple_proj, g, b)


def _swiglu(gt, up):
    return _silu(gt) * up


def _local_step(x, p, cosb, sinb, target, wt, sp, exch):
    t = x.shape[0]
    bf = _CDT
    xb = x.astype(bf)
    g = {}

    qkv_pre = _mm2(xb, wt['qkv'], name="f_qkv")
    z = _mm2(xb, wt['z'], name="f_z")
    gg = _mm2(xb, wt['gg'], name="f_gg")
    pm = _mm2(xb, wt['mla'], name="f_mla")
    qkv_act = _conv_silu(qkv_pre, sp['conv_w'])
    (u, w_, qd, kt, intra, gl, t_inv, qd2, au), sent = _delta_local(qkv_act, pm, sp['a_log'], sp['dt_bias'],
                                                                    rider=exch.gather_send())
    (o_dn, sall), passed = _delta_scan(u, w_, qd2, kt, au, gl, rider=exch.gather_pass(sent))
    wt = dict(wt, **exch.weights(passed))
    def gated_norm_br(h, o, zz, w, wbr):
        og_v = _gated_norm_heads(o, zz, w).astype(bf)
        return og_v, jnp.dot(og_v, wbr.astype(bf), preferred_element_type=F32)

    og, y_dn = _rowwise(gated_norm_br, [(o_dn, D_MODEL, 0, False), (z, D_MODEL, 0, False)],
                        [sp['dn_norm_w'], wt['br_dn']],
                        [(D_MODEL, D_MODEL, False, bf), (D_MODEL, D_MODEL, False, F32)], name="f_gated_norm_br")

    def mla_pre_uq(h, ckv, krp, cq, cosv, sinv, qw, kw, wn, wr):
        c_q_v, c_kv_v, k_rope_v = _mla_pre(ckv, krp, cq, cosv, sinv, qw, kw)
        c_q_b = c_q_v.astype(bf)
        return (c_q_b, c_kv_v, k_rope_v, jnp.dot(c_q_b, wn.astype(bf), preferred_element_type=F32),
                jnp.dot(c_q_b, wr.astype(bf), preferred_element_type=F32))

    c_q, c_kv, k_rope, q_nope, q_rope_pre = _rowwise(
        mla_pre_uq,
        [(pm, KV_LORA, 0, False), (pm, ROPE_PAD, 2, False), (pm, Q_LORA, 2, False),
         (cosb, ROPE_PAD, 0, False), (sinb, ROPE_PAD, 0, False)],
        [sp['q_norm_w'], sp['kv_norm_w'], wt['uq_nope'], wt['uq_rope']],
        [(Q_LORA, Q_LORA, False, bf), (KV_LORA, KV_LORA, False, bf), (ROPE_PAD, ROPE_PAD, False, bf),
         (HEADS * NOPE, HEADS * NOPE, False, bf), (HEADS * ROPE_PAD, HEADS * ROPE_PAD, False, F32)], name="f_mla_pre_uq")
    out_lat, lse, q_rope, o_mla = _attention(q_nope, q_rope_pre, cosb, sinb, c_kv, k_rope, wt['uk'], wt['uv'])
    y_mla = _mm2(o_mla, wt['br_mla'], name="f_br_mla")

    def merge_o_ln1(h, ggv, yd, ym, xv, wo, gv, bv):
        mixed_v = _merge(ggv, yd, ym).astype(bf)
        ao = jnp.dot(mixed_v, wo.astype(bf), preferred_element_type=F32)
        h1v = _ln1(xv, ao, gv, bv)
        return mixed_v, ao, h1v, h1v

    mixed, attn_out, h1, h1b = _rowwise(
        merge_o_ln1, [(gg, 2 * D_MODEL, 0, False), (y_dn, D_MODEL, 0, False), (y_mla, D_MODEL, 0, False), (x, D_MODEL, 0, False)],
        [wt['o'], sp['ln1_g'], sp['ln1_b']],
        [(D_MODEL, D_MODEL, False, bf), (D_MODEL, D_MODEL, False, F32), (D_MODEL, D_MODEL, False, F32),
         (D_MODEL, D_MODEL, False, bf)], name="f_merge_o_ln1")
    act, ffn_gt, ffn_up = _ffn_in_swiglu(h1b, wt['ffn_in'])
    pb = p.astype(bf)

    def final_fn(h, h1v, actv, pv, tgt, gt, up, wfo, wpg, wpl, gv, bv):
        ffnv = jnp.dot(actv.astype(bf), wfo.astype(bf), preferred_element_type=F32)
        gpv = jnp.dot(h1v.astype(bf), wpg.astype(bf), preferred_element_type=F32)
        ppv = jnp.dot(pv.astype(bf), wpl.astype(bf), preferred_element_type=F32)
        y, vjp = jax.vjp(_final, h1v, ffnv, gpv, ppv, gv, bv)
        err = y - tgt
        dh1, dffn, dgp, dpp, dg, db = vjp(err * (1.0 / D_MODEL))
        sq = err * err
        lanes = sq[:, :LANE]
        for j in range(1, D_MODEL // LANE):
            lanes = lanes + sq[:, j * LANE:(j + 1) * LANE]
        loss = jnp.sum(lanes, axis=0, keepdims=True) * (0.5 / D_MODEL)
        dffn_b = dffn.astype(bf)
        dact = _dot(dffn_b, wfo, _NT).astype(bf).astype(F32)
        _, vjp_s = jax.vjp(_swiglu, gt.astype(F32), up.astype(F32))
        dgt, dup = vjp_s(dact)
        return dffn, dffn_b, dgp, dpp, jnp.concatenate([dgt, dup], axis=1), dg, db, loss

    dpre2, dpre2b, dgate_pre, dple_proj, dffn_in, g['ln2_g'], g['ln2_b'], loss_lanes = _rowwise(
        final_fn, [(h1, D_MODEL, 0, False), (act, FFN_HIDDEN, 0, False), (pb, PLE_DIM, 0, False), (target, D_MODEL, 0, False),
                   (ffn_gt, FFN_HIDDEN, 0, False), (ffn_up, FFN_HIDDEN, 0, False)],
        [wt['ffn_out'], wt['ple_gate'], wt['ple'], sp['ln2_g'], sp['ln2_b']],
        [(D_MODEL, D_MODEL, False, F32)] + [(D_MODEL, D_MODEL, False, bf)] * 3 + [(2 * FFN_HIDDEN, 2 * FFN_HIDDEN, False, bf)],
        [(1, D_MODEL), (1, D_MODEL), (1, LANE)], name="b_final")

    g['ple'] = _mm2(pb, dple_proj, ta=True, name="g_ple")
    g['ple_gate'] = _mm2(h1b, dgate_pre, ta=True, name="g_ple_gate")
    g['ffn_out'] = _mm2(act, dpre2b, ta=True, name="g_ffn_out")
    g['ffn_in'] = _mm2(h1b, dffn_in, ta=True, name="g_ffn_in")
    dh1, _ = _dx_fused([(dffn_in, wt['ffn_in']), (dgate_pre, wt['ple_gate'])], dpre2, ALPHA, name="b_dh1")

    def attn_out_bwd(h, xv, ao, d, ggv, yd, ym, wo, wbd, wbm, gv, bv):
        _, vjp = jax.vjp(_ln1, xv, ao, gv, bv)
        _, dao, dg, db = vjp(d)
        dao_b = dao.astype(bf)
        _, vjp_m = jax.vjp(_merge, ggv, yd, ym)
        dggv, dyd, dym = vjp_m(_dot(dao_b, wo, _NT))
        dyd_b, dym_b = dyd.astype(bf), dym.astype(bf)
        return dao, dao_b, dggv, dyd_b, dym_b, _dot(dyd_b, wbd, _NT), _dot(dym_b, wbm, _NT), dg, db

    row_d = lambda a: (a, D_MODEL, 0, False)
    dpre1, dpre1b, dgg, dy_dn, dy_mla, dog, do_mla, g['ln1_g'], g['ln1_b'] = _rowwise(
        attn_out_bwd, [row_d(x), row_d(attn_out), row_d(dh1), (gg, 2 * D_MODEL, 0, False), row_d(y_dn), row_d(y_mla)],
        [wt['o'], wt['br_dn'], wt['br_mla'], sp['ln1_g'], sp['ln1_b']],
        [(D_MODEL, D_MODEL, False, F32), (D_MODEL, D_MODEL, False, bf), (2 * D_MODEL, 2 * D_MODEL, False, bf),
         (D_MODEL, D_MODEL, False, bf), (D_MODEL, D_MODEL, False, bf), (D_MODEL, D_MODEL, False, F32),
         (D_MODEL, D_MODEL, False, bf)],
        [(1, D_MODEL), (1, D_MODEL)], name="b_attn_out")
    g['o'] = _mm2(mixed, dpre1b, ta=True, name="g_o")
    g['br_dn'] = _mm2(og, dy_dn, ta=True, name="g_br_dn")
    g['br_mla'] = _mm2(o_mla, dy_mla, ta=True, name="g_br_mla")

    g['uv'] = _mm(out_lat, do_mla, name="g_uv", ta=True, heads=HEADS, a_head='lead', b_head='col', out_head='lead',
                  dims=(KV_LORA, NOPE, t))
    dq_lat, dq_rope_pre, dq_nope, dckv_att, dkr_att = _attention_bwd(
        q_nope, q_rope, cosb, sinb, c_kv, k_rope, wt['uk'], wt['uv'], out_lat, lse, do_mla)
    g['uk'] = _mm(dq_lat, q_nope, name="g_uk", ta=True, heads=HEADS, a_head='lead', b_head='col', out_head='lead',
                  dims=(KV_LORA, NOPE, t))
    g['uq_nope'] = _mm2(c_q, dq_nope, ta=True, name="g_uq_nope")
    g['uq_rope'] = _mm2(c_q, dq_rope_pre, ta=True, name="g_uq_rope")
    dc_q = _mm2(dq_nope, wt['uq_nope'], tb=True, name="b_dcq_nope")
    dc_q = _mm2(dq_rope_pre, wt['uq_rope'], tb=True, name="b_dcq_rope", add=dc_q)

    def gated_norm_bwd(h, o, zz, d, w):
        _, vjp = jax.vjp(_gated_norm_heads, o, zz, w)
        return vjp(d)

    do_dn, dz, g['dn_norm_w'] = _rowwise(
        gated_norm_bwd, [(o_dn, D_MODEL, 0, False), (z, D_MODEL, 0, False), (dog, D_MODEL, 0, False)], [sp['dn_norm_w']],
        [(D_MODEL, D_MODEL, False, F32), (D_MODEL, D_MODEL, False, bf)], [(1, LANE)], name="b_gated_norm")
    (du, dw, dqd, dkt, dintra, dgl), paired = _delta_scan_bwd(u, w_, qd, kt, intra, gl, sall, do_dn, rider=exch.pair_send(g))
    (dq_a, dk_a, dv_a, dba, g['a_log'], g['dt_bias']), arrived = _delta_local_bwd(
        qkv_act, pm, sp['a_log'], sp['dt_bias'], t_inv, du, dw, dqd, dkt, dintra, dgl, rider=exch.reduce_send(paired))
    exch.reduce_arrived(arrived)
    dqkv_pre, g['conv_w'] = _conv_silu_bwd(qkv_pre, sp['conv_w'], [dq_a, dk_a, dv_a])

    def mla_pre_bwd(h, ckv, cq, cosv, sinv, dcq, dckv, dkr, dba_v, qw, kw):
        _, vjp = jax.vjp(lambda a, c, d, e: (_rms_norm(c, d), _rms_norm(a, e)), ckv, cq, qw, kw)
        dckv_p, dcq_p, dqw, dkw = vjp((dcq, dckv))
        dkr_p = _rope_bwd(dkr, cosv, sinv)
        dpm = jnp.concatenate([dckv_p, dkr_p, dba_v, jnp.zeros((ckv.shape[0], 2 * LANE), F32), dcq_p], axis=1)
        return dpm, dqw, dkw

    dpm, g['q_norm_w'], g['kv_norm_w'] = _rowwise(
        mla_pre_bwd,
        [(pm, KV_LORA, 0, False), (pm, Q_LORA, 2, False), (cosb, ROPE_PAD, 0, False), (sinb, ROPE_PAD, 0, False),
         (dc_q, Q_LORA, 0, False), (dckv_att, KV_LORA, 0, False), (dkr_att, ROPE_PAD, 0, False), (dba, LANE, 0, False)],
        [sp['q_norm_w'], sp['kv_norm_w']], [(1152, 1152, False, bf)], [(1, Q_LORA), (1, KV_LORA)], name="b_mla_pre")

    g['qkv'] = _mm2(xb, dqkv_pre, ta=True, name="g_qkv")
    g['z'] = _mm2(xb, dz, ta=True, name="g_z")
    g['gg'] = _mm2(xb, dgg, ta=True, name="g_gg")
    g['mla'] = _mm2(xb, dpm, ta=True, name="g_mla")
    dx, arrived = _dx_fused([(dqkv_pre, wt['qkv']), (dz, wt['z']), (dgg, wt['gg']), (dpm, wt['mla'])], dpre1, ALPHA,
                            rider=exch.in_send(g))
    exch.in_arrived(arrived)
    return loss_lanes, dx, g


_IN_SIZES = (QKV_W, HEADS * DN_DK, HEADS, HEADS, Q_LORA, KV_LORA, ROPE, D_MODEL, D_MODEL)


def _rope_tables(positions):
    inv_freq = ROPE_BASE ** (-jnp.arange(0, ROPE, 2, dtype=F32) / ROPE)
    ang = positions.astype(F32)[:, None] * inv_freq
    cos, sin = jnp.cos(ang), jnp.sin(ang)
    zeros = jnp.zeros((positions.shape[0], ROPE_PAD - ROPE), F32)
    return jnp.concatenate([cos, cos, zeros], axis=1), jnp.concatenate([-sin, sin, zeros], axis=1)


def _prep_w_in(w_in):
    dt = w_in.dtype
    offs = [0]
    for s in _IN_SIZES:
        offs.append(offs[-1] + s)
    qkv, z, wb, wa, cq, ckv, kr, gd, gm = [w_in[:, offs[i]:offs[i + 1]] for i in range(len(_IN_SIZES))]
    zc = lambda n: jnp.zeros((D_MODEL, n), dt)
    return {
        'qkv': qkv, 'z': z, 'gg': jnp.concatenate([gd, gm], axis=1),
        'mla': jnp.concatenate([ckv, kr, zc(ROPE_PAD - ROPE), wb, wa, zc(LANE - 2 * HEADS), zc(2 * LANE), cq], axis=1),
    }


def _prep_weights(full):
    w_uq = full['w_uq']
    wt = {
        'uq_nope': w_uq[:, :, :NOPE].reshape(Q_LORA, HEADS * NOPE),
        'uq_rope': jnp.pad(w_uq[:, :, NOPE:], ((0, 0), (0, 0), (0, ROPE_PAD - ROPE))).reshape(Q_LORA, HEADS * ROPE_PAD),
        'uk': jnp.transpose(full['w_uk'], (1, 0, 2)), 'uv': jnp.transpose(full['w_uv'], (1, 0, 2)),
        'br_dn': full['w_br_dn'], 'br_mla': full['w_br_mla'], 'o': full['w_o'], 'ffn_in': full['w_ffn_in'],
        'ffn_out': full['w_ffn_out'], 'ple': full['w_ple'], 'ple_gate': full['w_ple_gate'],
    }
    return wt


def _prep_small(small):
    pad = lambda v: jnp.pad(v, (0, LANE - v.shape[0]))[None, :]
    return {
        'conv_w': small['conv_w'], 'a_log': pad(small['dn_a_log']), 'dt_bias': pad(small['dn_dt_bias']),
        'dn_norm_w': small['dn_norm_w'][None, :], 'q_norm_w': small['q_norm_w'][None, :],
        'kv_norm_w': small['kv_norm_w'][None, :], 'ln1_g': small['ln1_g'][None, :], 'ln1_b': small['ln1_b'][None, :],
        'ln2_g': small['ln2_g'][None, :], 'ln2_b': small['ln2_b'][None, :],
    }


def _w_in_grad(g):
    mla = g['mla']
    ba0 = KV_LORA + ROPE_PAD
    cq0 = ba0 + 3 * LANE
    return jnp.concatenate([
        g['qkv'], g['z'], mla[:, ba0:ba0 + HEADS], mla[:, ba0 + HEADS:ba0 + 2 * HEADS], mla[:, cq0:cq0 + Q_LORA],
        mla[:, :KV_LORA], mla[:, KV_LORA:KV_LORA + ROPE], g['gg']], axis=1)


def _unprep_grads_late(g):
    return {
        'conv_w': g['conv_w'], 'dn_a_log': g['a_log'][0, :HEADS], 'dn_dt_bias': g['dt_bias'][0, :HEADS],
        'dn_norm_w': g['dn_norm_w'][0], 'q_norm_w': g['q_norm_w'][0], 'kv_norm_w': g['kv_norm_w'][0],
        'ln1_g': g['ln1_g'][0], 'ln1_b': g['ln1_b'][0], 'ln2_g': g['ln2_g'][0], 'ln2_b': g['ln2_b'][0],
    }


def _unprep_grads_early(g):
    w_uq = jnp.concatenate([g['uq_nope'].reshape(Q_LORA, HEADS, NOPE),
                            g['uq_rope'].reshape(Q_LORA, HEADS, ROPE_PAD)[:, :, :ROPE]], axis=2)
    return {
        'w_uq': w_uq, 'w_uk': jnp.transpose(g['uk'], (1, 0, 2)), 'w_uv': jnp.transpose(g['uv'], (1, 0, 2)),
        'w_br_dn': g['br_dn'], 'w_br_mla': g['br_mla'], 'w_o': g['o'],
        'w_ffn_in': g['ffn_in'], 'w_ffn_out': g['ffn_out'], 'w_ple': g['ple'], 'w_ple_gate': g['ple_gate'],
    }


_FLATB_PIECES = (
    ('w_ffn_out', 704, (704, D_MODEL)), ('w_br_dn', 256, (256, D_MODEL)), ('w_br_mla', 256, (256, D_MODEL)),
    ('w_o', 256, (256, D_MODEL)), ('w_ple_gate', 256, (256, D_MODEL)), ('w_uq', 144, (96, HEADS, NOPE + ROPE)),
    ('w_uk', 64, (64, HEADS, NOPE)), ('w_uv', 64, (64, HEADS, NOPE)), ('w_ple', 64, (PLE_DIM, 256)),
)
FLATB_ROWS = 2112
W_IN_SHARD = D_IN // N_SHARD
FFN_IN_SHARD = 2 * FFN_HIDDEN // N_SHARD
A_ROWS = D_MODEL + 32
_CONV_SHARD = QKV_W // N_SHARD
_ADD_TILES = (256, 256, 352)


def _flatb_offsets():
    offs, o = {}, 0
    for name, rows, _ in _FLATB_PIECES:
        offs[name] = o
        o += rows
    return offs, o


def _pack_shards(ws, conv_w):
    conv_bits = lax.bitcast_convert_type(conv_w, jnp.bfloat16).reshape(DN_CONV, 2 * _CONV_SHARD).astype(_CDT)
    tail = jnp.pad(conv_bits, ((0, A_ROWS - D_MODEL - DN_CONV), (0, W_IN_SHARD - 2 * _CONV_SHARD)))
    a_buf = jnp.concatenate([ws['w_in'].astype(_CDT), tail], axis=0)
    parts = [ws[name].astype(_CDT).reshape(rows, FLAT_W) for name, rows, _ in _FLATB_PIECES]
    used = sum(p.shape[0] for p in parts)
    parts.append(jnp.zeros((FLATB_ROWS - used, FLAT_W), _CDT))
    return [a_buf, ws['w_ffn_in'].astype(_CDT), jnp.concatenate(parts, axis=0)]


def _unpack_w_in(gathered, local, me):
    a = [jnp.where(me == s, local, gathered[s]) for s in range(N_SHARD)]
    conv = [lax.bitcast_convert_type(
        p[D_MODEL:D_MODEL + DN_CONV, :2 * _CONV_SHARD].astype(jnp.bfloat16).reshape(DN_CONV, _CONV_SHARD, 2), F32) for p in a]
    return jnp.concatenate([p[:D_MODEL] for p in a], axis=1), jnp.concatenate(conv, axis=1)


def _unpack_rest(gathered, local, me):
    pick = lambda b, s: jnp.where(me == s, local[b], gathered[b][s])
    full = {'w_ffn_in': jnp.concatenate([pick(0, s) for s in range(N_SHARD)], axis=1)}
    offs, _ = _flatb_offsets()
    fb = [pick(1, s) for s in range(N_SHARD)]
    for name, rows, shape in _FLATB_PIECES:
        pieces = [p[offs[name]:offs[name] + rows].reshape(shape) for p in fb]
        full[name] = jnp.concatenate(pieces, axis=1 if name == 'w_ple' else 0)
    return full


def _shard_columns(g, w):
    return jnp.stack([g[:, s * w:(s + 1) * w] for s in range(N_SHARD)])


def _pack_grads_rest(gw):
    parts = []
    for name, rows, _ in _FLATB_PIECES:
        g = gw[name]
        if name == 'w_ple':
            parts.append(_shard_columns(g, PLE_DIM).reshape(N_SHARD, rows, FLAT_W))
        else:
            parts.append(g.reshape(N_SHARD, rows, FLAT_W))
    used = sum(p.shape[1] for p in parts)
    parts.append(jnp.zeros((N_SHARD, FLATB_ROWS - used, FLAT_W), F32))
    return [_shard_columns(gw['w_ffn_in'], FFN_IN_SHARD), jnp.concatenate(parts, axis=1)]


def _unpack_reduced(mine, theirs, c):
    whole = [jnp.concatenate([jnp.where(c == 0, m, t), jnp.where(c == 0, t, m)], axis=0) for m, t in zip(mine, theirs)]
    out = {'w_in': whole[0], 'w_ffn_in': whole[1]}
    offs, _ = _flatb_offsets()
    for name, rows, shape in _FLATB_PIECES:
        out[name] = whole[2][offs[name]:offs[name] + rows].reshape(shape)
    return out


_HBM = pl.BlockSpec(memory_space=pltpu.HBM)


def _place():
    x, y, c = lax.axis_index("x"), lax.axis_index("y"), lax.axis_index("c")
    chips = [(1 - x, y), (x, 1 - y), (1 - x, 1 - y)]
    return x, y, c, chips


def _remote(src, dst, send_sems, recv_sems, k, to):
    return pltpu.make_async_remote_copy(src_ref=src, dst_ref=dst, send_sem=send_sems.at[k], recv_sem=recv_sems.at[k],
                                        device_id=to, device_id_type=_MESH)


def _half_rows(ref, half, hf, lead=None):
    rows = pl.ds(pl.multiple_of(hf * half, 16), half)
    return ref.at[rows, :] if lead is None else ref.at[lead, rows, :]


class _Rider:
    def __init__(self, inputs, out_shape, n_sems, copies, aliases=None):
        self.inputs, self.out_shape, self.n_sems, self.copies = list(inputs), list(out_shape), n_sems, copies
        self.aliases = aliases or {}


def _carried_call(body, rider, first, last, *, name, grid, in_specs, out_specs, out_shape, scratch_shapes, sem, args):
    n_in, n_out, n_scr = len(in_specs), len(out_specs), len(scratch_shapes)
    if rider is None:
        res = pl.pallas_call(body, name=name, grid=grid, in_specs=in_specs, out_specs=out_specs, out_shape=out_shape,
                             scratch_shapes=scratch_shapes, compiler_params=_cparams(sem))(*args)
        return list(res), []
    ri, ro = len(rider.inputs), len(rider.out_shape)

    def full_body(*refs):
        own_in, r_in = refs[:n_in], refs[n_in:n_in + ri]
        o0 = n_in + ri
        own_out, r_out = refs[o0:o0 + n_out], refs[o0 + n_out:o0 + n_out + ro]
        s0 = o0 + n_out + ro
        own_scr, send_sems, recv_sems = refs[s0:s0 + n_scr], refs[s0 + n_scr], refs[s0 + n_scr + 1]

        @pl.when(first())
        def _():
            sends, _ = rider.copies(r_in, r_out, send_sems, recv_sems)
            for cp in sends:
                cp.start()

        body(*own_in, *own_out, *own_scr)

        @pl.when(last())
        def _():
            sends, arrivals = rider.copies(r_in, r_out, send_sems, recv_sems)
            for cp in arrivals():
                cp.wait_recv()
            for cp in sends:
                cp.wait_send()

    res = pl.pallas_call(
        full_body, name=name, grid=grid, in_specs=list(in_specs) + [_HBM] * ri, out_specs=list(out_specs) + [_HBM] * ro,
        out_shape=list(out_shape) + rider.out_shape,
        scratch_shapes=list(scratch_shapes) + [pltpu.SemaphoreType.DMA((rider.n_sems,))] * 2,
        input_output_aliases={n_in + i: n_out + o for i, o in rider.aliases.items()},
        compiler_params=_cparams(sem))(*args, *rider.inputs)
    return list(res[:n_out]), list(res[n_out:])


def _ride_gather_send(bufs):
    n = len(bufs)
    halves = [b.shape[0] // 2 for b in bufs]

    def copies(ins, outs, send_sems, recv_sems):
        x, y, c, chips = _place()
        slot = lambda b, cx, cy: _half_rows(outs[b], halves[b], c, lead=2 * cx + cy)
        sends = [_remote(_half_rows(ins[b], halves[b], c), slot(b, x, y), send_sems, recv_sems, 3 * b + j, (cx, cy, c))
                 for b in range(n) for j, (cx, cy) in enumerate(chips)]
        arrivals = lambda: [_remote(slot(b, cx, cy), slot(b, cx, cy), send_sems, recv_sems, 3 * b + j, (x, y, c))
                            for b in range(n) for j, (cx, cy) in enumerate(chips)]
        return sends, arrivals

    return _Rider(bufs, [jax.ShapeDtypeStruct((N_SHARD,) + b.shape, b.dtype) for b in bufs], 3 * n, copies)


def _ride_gather_pass(gathered):
    n = len(gathered)
    halves = [g.shape[1] // 2 for g in gathered]

    def copies(ins, outs, send_sems, recv_sems):
        x, y, c, chips = _place()
        slot = lambda b, cx, cy, hf: _half_rows(outs[b], halves[b], hf, lead=2 * cx + cy)
        sends = [_remote(slot(b, cx, cy, c), slot(b, cx, cy, c), send_sems, recv_sems, 3 * b + j, (x, y, 1 - c))
                 for b in range(n) for j, (cx, cy) in enumerate(chips)]
        arrivals = lambda: [_remote(slot(b, cx, cy, 1 - c), slot(b, cx, cy, 1 - c), send_sems, recv_sems, 3 * b + j, (x, y, c))
                            for b in range(n) for j, (cx, cy) in enumerate(chips)]
        return sends, arrivals

    return _Rider(gathered, [jax.ShapeDtypeStruct(g.shape, g.dtype) for g in gathered], 3 * n, copies,
                  aliases={b: b for b in range(n)})


def _ride_pair_exchange(gbufs):
    n = len(gbufs)
    halves = [g.shape[1] // 2 for g in gbufs]

    def copies(ins, outs, send_sems, recv_sems):
        x, y, c, _ = _place()
        sends = [_remote(ins[b].at[:, pl.ds(pl.multiple_of((1 - c) * halves[b], 16), halves[b]), :], outs[b],
                         send_sems, recv_sems, b, (x, y, 1 - c)) for b in range(n)]
        arrivals = lambda: [_remote(outs[b], outs[b], send_sems, recv_sems, b, (x, y, c)) for b in range(n)]
        return sends, arrivals

    return _Rider(gbufs, [jax.ShapeDtypeStruct((N_SHARD, h, g.shape[2]), g.dtype) for g, h in zip(gbufs, halves)], n, copies)


def _ride_chip_exchange(parts):
    n = len(parts)

    def copies(ins, outs, send_sems, recv_sems):
        x, y, c, chips = _place()
        sends = [_remote(ins[b].at[2 * cx + cy], outs[b].at[j], send_sems, recv_sems, 3 * b + j, (cx, cy, c))
                 for b in range(n) for j, (cx, cy) in enumerate(chips)]
        arrivals = lambda: [_remote(ins[b].at[0], outs[b].at[j], send_sems, recv_sems, 3 * b + j, (x, y, c))
                            for b in range(n) for j in range(len(chips))]
        return sends, arrivals

    return _Rider(parts, [jax.ShapeDtypeStruct((3,) + p.shape[1:], p.dtype) for p in parts], 3 * n, copies)


def _gather_shards(bufs, name):
    n = len(bufs)
    halves = [b.shape[0] // 2 for b in bufs]

    def body(*refs):
        ins, outs, send_sems, recv_sems = refs[:n], refs[n:2 * n], refs[2 * n], refs[2 * n + 1]
        x, y, c, chips = _place()
        me, sibling = (x, y, c), (x, y, 1 - c)
        slot = lambda b, cx, cy, hf: _half_rows(outs[b], halves[b], hf, lead=2 * cx + cy)
        first = [_remote(_half_rows(ins[b], halves[b], c), slot(b, x, y, c), send_sems, recv_sems, 6 * b + j, (cx, cy, c))
                 for b in range(n) for j, (cx, cy) in enumerate(chips)]
        for cp in first:
            cp.start()
        passed = []
        for j, (cx, cy) in enumerate(chips):
            for b in range(n):
                _remote(slot(b, cx, cy, c), slot(b, cx, cy, c), send_sems, recv_sems, 6 * b + j, me).wait_recv()
                fwd = _remote(slot(b, cx, cy, c), slot(b, cx, cy, c), send_sems, recv_sems, 6 * b + 3 + j, sibling)
                fwd.start()
                passed.append(fwd)
        for j, (cx, cy) in enumerate(chips):
            for b in range(n):
                _remote(slot(b, cx, cy, 1 - c), slot(b, cx, cy, 1 - c), send_sems, recv_sems, 6 * b + 3 + j, me).wait_recv()
        for cp in first + passed:
            cp.wait_send()

    return pl.pallas_call(
        body, name=name, out_shape=[jax.ShapeDtypeStruct((N_SHARD,) + b.shape, b.dtype) for b in bufs],
        in_specs=[_HBM] * n, out_specs=[_HBM] * n,
        scratch_shapes=[pltpu.SemaphoreType.DMA((6 * n,)), pltpu.SemaphoreType.DMA((6 * n,))],
    )(*bufs)


def _reduce_pair_exchange(gbufs, name):
    n = len(gbufs)
    halves = [g.shape[1] // 2 for g in gbufs]

    def body(*refs):
        ins, outs, send_sems, recv_sems = refs[:n], refs[n:2 * n], refs[2 * n], refs[2 * n + 1]
        x, y, c, _ = _place()
        cps = [_remote(ins[b].at[:, pl.ds(pl.multiple_of((1 - c) * halves[b], 16), halves[b]), :], outs[b],
                       send_sems, recv_sems, b, (x, y, 1 - c)) for b in range(n)]
        for cp in cps:
            cp.start()
        for cp in cps:
            cp.wait()

    return pl.pallas_call(
        body, name=name,
        out_shape=[jax.ShapeDtypeStruct((N_SHARD, h, g.shape[2]), g.dtype) for g, h in zip(gbufs, halves)],
        in_specs=[_HBM] * n, out_specs=[_HBM] * n,
        scratch_shapes=[pltpu.SemaphoreType.DMA((n,)), pltpu.SemaphoreType.DMA((n,))],
    )(*gbufs)


def _pair_add(gbuf, recv, c_arr, tr, name):
    _, rows, width = gbuf.shape
    half = rows // 2
    nt = half // tr

    def body(c_ref, a_ref, b_ref, o_ref):
        o_ref[...] = (a_ref[...] + b_ref[...]).astype(o_ref.dtype)

    blk = lambda f: pl.BlockSpec((None, tr, width), f)
    return pl.pallas_call(
        body, name=name, out_shape=jax.ShapeDtypeStruct((N_SHARD, half, width), jnp.bfloat16),
        grid_spec=pltpu.PrefetchScalarGridSpec(
            num_scalar_prefetch=1, grid=(N_SHARD, nt),
            in_specs=[blk(lambda s, i, c: (s, c[0] * nt + i, 0)), blk(lambda s, i, c: (s, i, 0))],
            out_specs=blk(lambda s, i, c: (s, i, 0))),
        compiler_params=_cparams(("parallel", "parallel")))(c_arr, gbuf, recv)


def _reduce_chip_exchange(parts, name):
    n = len(parts)

    def body(*refs):
        ins, outs, send_sems, recv_sems = refs[:n], refs[n:2 * n], refs[2 * n], refs[2 * n + 1]
        x, y, c, chips = _place()
        sends = [_remote(ins[b].at[2 * cx + cy], outs[b].at[j], send_sems, recv_sems, 3 * b + j, (cx, cy, c))
                 for b in range(n) for j, (cx, cy) in enumerate(chips)]
        for cp in sends:
            cp.start()
        for b in range(n):
            for j in range(len(chips)):
                _remote(ins[b].at[0], outs[b].at[j], send_sems, recv_sems, 3 * b + j, (x, y, c)).wait_recv()
        for cp in sends:
            cp.wait_send()

    return pl.pallas_call(
        body, name=name, out_shape=[jax.ShapeDtypeStruct((3,) + p.shape[1:], p.dtype) for p in parts],
        in_specs=[_HBM] * n, out_specs=[_HBM] * n,
        scratch_shapes=[pltpu.SemaphoreType.DMA((3 * n,)), pltpu.SemaphoreType.DMA((3 * n,))],
    )(*parts)


def _chip_add(part, recv, me_arr, tr, name):
    _, half, width = part.shape

    def body(me_ref, own, a0, a1, a2, o_ref):
        f = lambda r: r[...].astype(F32)
        o_ref[...] = ((f(own) + f(a0)) + f(a1)) + f(a2)

    specs = [pl.BlockSpec((None, tr, width), lambda i, me: (me[0], i, 0))]
    specs += [pl.BlockSpec((None, tr, width), functools.partial(lambda i, me, k: (k, i, 0), k=k)) for k in range(3)]
    return pl.pallas_call(
        body, name=name, out_shape=jax.ShapeDtypeStruct((half, width), F32),
        grid_spec=pltpu.PrefetchScalarGridSpec(
            num_scalar_prefetch=1, grid=(half // tr,), in_specs=specs,
            out_specs=pl.BlockSpec((tr, width), lambda i, me: (i, 0))),
        compiler_params=_cparams(("parallel",)))(me_arr, part, recv, recv, recv)


def _reduce_pair_share(rhalves, name):
    n = len(rhalves)

    def body(*refs):
        ins, outs, send_sems, recv_sems = refs[:n], refs[n:2 * n], refs[2 * n], refs[2 * n + 1]
        x, y, c, _ = _place()
        cps = [_remote(ins[b], outs[b], send_sems, recv_sems, b, (x, y, 1 - c)) for b in range(n)]
        for cp in cps:
            cp.start()
        for cp in cps:
            cp.wait()

    return pl.pallas_call(
        body, name=name, out_shape=[jax.ShapeDtypeStruct(r.shape, r.dtype) for r in rhalves],
        in_specs=[_HBM] * n, out_specs=[_HBM] * n,
        scratch_shapes=[pltpu.SemaphoreType.DMA((n,)), pltpu.SemaphoreType.DMA((n,))],
    )(*rhalves)


def _small_allreduce(buf):
    r, width = buf.shape
    n_dev = 8

    def body(x_ref, all_ref, sum_ref, send_sems, recv_sems, local_sem):
        x, y, c, chips = _place()
        me, sibling = (x, y, c), (x, y, 1 - c)

        def rows(px, py, pc):
            return all_ref.at[pl.ds(pl.multiple_of((4 * px + 2 * py + pc) * r, 8), r), :]

        def copy(k, block, to, src=None):
            return _remote(rows(*block) if src is None else src, rows(*block), send_sems, recv_sems, k, to)

        mine = pltpu.make_async_copy(x_ref, rows(*me), local_sem)
        mine.start()
        first = [copy(0, me, sibling, src=x_ref)]
        first += [copy(1 + j, me, (*chip, c), src=x_ref) for j, chip in enumerate(chips)]
        for cp in first:
            cp.start()
        passed = [copy(4 + j, (*chip, c), sibling) for j, chip in enumerate(chips)]
        for j, chip in enumerate(chips):
            copy(1 + j, (*chip, c), me).wait_recv()
            passed[j].start()
        copy(0, sibling, me).wait_recv()
        for j, chip in enumerate(chips):
            copy(4 + j, (*chip, 1 - c), me).wait_recv()
        for cp in first + passed:
            cp.wait_send()
        mine.wait()
        total = all_ref[0:r, :]
        for k in range(1, n_dev):
            total = total + all_ref[k * r:(k + 1) * r, :]
        sum_ref[...] = total

    vm = pl.BlockSpec(memory_space=pltpu.VMEM)
    _, total = pl.pallas_call(
        body, name="small_allreduce",
        out_shape=[jax.ShapeDtypeStruct((n_dev * r, width), buf.dtype), jax.ShapeDtypeStruct((r, width), buf.dtype)],
        in_specs=[vm], out_specs=[vm, vm],
        scratch_shapes=[pltpu.SemaphoreType.DMA((7,)), pltpu.SemaphoreType.DMA((7,)), pltpu.SemaphoreType.DMA],
    )(buf)
    return total


def _row_tile(rows, cap):
    if rows <= cap:
        return rows
    t = (cap // 8) * 8
    while t >= 8:
        if rows % t == 0:
            return t
        t -= 8
    return rows


def _adamw(w, g, m, v, name):
    shape = w.shape
    cols = shape[-1] if len(shape) <= 3 else shape[-2] * shape[-1]
    lead = len(shape) == 3
    w2, g2, m2, v2 = (a if lead else a.reshape(-1, cols) for a in (w, g, m, v))
    rows = shape[1] if lead else w2.shape[0]
    tr, tc = _row_tile(rows, 256), cols
    if tr == rows and rows > 256:
        tc = _tile(cols, 256)

    def body(w_ref, g_ref, m_ref, v_ref, d_ref, mo_ref, vo_ref):
        gv = g_ref[...]
        mn = ADAM_B1 * m_ref[...] + (1.0 - ADAM_B1) * gv
        vn = ADAM_B2 * v_ref[...] + (1.0 - ADAM_B2) * (gv * gv)
        m_hat = mn / (1.0 - ADAM_B1 ** ADAM_STEP)
        v_hat = vn / (1.0 - ADAM_B2 ** ADAM_STEP)
        d_ref[...] = -ADAM_LR * (m_hat / (jnp.sqrt(v_hat) + ADAM_EPS) + ADAM_WD * w_ref[...])
        mo_ref[...] = mn
        vo_ref[...] = vn

    blk = (pl.BlockSpec((None, tr, tc), lambda i, j: (0, i, j)) if lead else pl.BlockSpec((tr, tc), lambda i, j: (i, j)))
    outs = pl.pallas_call(
        body, name=name, grid=(rows // tr, cols // tc), in_specs=[blk] * 4, out_specs=[blk] * 3,
        out_shape=[jax.ShapeDtypeStruct(w2.shape, F32)] * 3,
        compiler_params=_cparams(("parallel", "parallel")))(w2, g2, m2, v2)
    return tuple(o.reshape(shape) for o in outs)


_WEIGHT_NAMES = ('w_in', 'conv_w', 'dn_a_log', 'dn_dt_bias', 'dn_norm_w', 'q_norm_w', 'w_uq', 'kv_norm_w', 'w_uk',
                 'w_uv', 'w_br_dn', 'w_br_mla', 'w_o', 'ln1_g', 'ln1_b', 'w_ffn_in', 'w_ffn_out', 'w_ple',
                 'w_ple_gate', 'ln2_g', 'ln2_b')
_SMALL_NAMES = ('ln1_g', 'ln1_b', 'ln2_g', 'ln2_b', 'q_norm_w', 'kv_norm_w', 'dn_norm_w', 'dn_a_log', 'dn_dt_bias')
_SMALL_GROUP = 8
_CONV_SMALL_ROW = len(_SMALL_NAMES) * _SMALL_GROUP
_CONV_SMALL_ROWS = DN_CONV * QKV_W // FLAT_W


def _pack_small(gw):
    rows = [jnp.pad(gw[n][None, :], ((0, _SMALL_GROUP - 1), (0, FLAT_W - gw[n].shape[0]))) for n in _SMALL_NAMES]
    rows.append(jnp.pad(gw['conv_w'].reshape(_CONV_SMALL_ROWS, FLAT_W), ((0, SMALL_ROWS - _CONV_SMALL_ROW - _CONV_SMALL_ROWS), (0, 0))))
    return jnp.concatenate(rows, axis=0)


class _Exchange:
    def __init__(self, local, me_chip, c_arr):
        self.local, self.me_chip, self.c_arr = local, me_chip, c_arr
        self.parts = self.arrived = None

    def gather_send(self):
        return _ride_gather_send(self.local)

    def gather_pass(self, sent):
        return _ride_gather_pass(sent)

    def weights(self, gathered):
        return _prep_weights(_unpack_rest(gathered, self.local, self.me_chip))

    def pair_send(self, g):
        self.gbufs = _pack_grads_rest(_unprep_grads_early(g))
        return _ride_pair_exchange(self.gbufs)

    def reduce_send(self, got):
        self.parts = [_pair_add(g_, r_, self.c_arr, tr, "pair_add_%d" % (i + 1))
                      for i, (g_, r_, tr) in enumerate(zip(self.gbufs, got, _ADD_TILES[1:]))]
        return _ride_chip_exchange(self.parts)

    def reduce_arrived(self, arrived):
        self.arrived = list(arrived)

    def in_send(self, g):
        g_in = [_shard_columns(_w_in_grad(g), W_IN_SHARD)]
        got = _reduce_pair_exchange(g_in, "reduce_pair_exchange_w_in")
        self.part_in = _pair_add(g_in[0], got[0], self.c_arr, _ADD_TILES[0], "pair_add_0")
        return _ride_chip_exchange([self.part_in])

    def in_arrived(self, arrived):
        self.arrived_in = list(arrived)


def kernel(x, p, positions, w_in, conv_w, dn_a_log, dn_dt_bias, dn_norm_w, q_norm_w, w_uq, kv_norm_w, w_uk, w_uv, w_br_dn, w_br_mla, w_o, ln1_g, ln1_b, w_ffn_in, w_ffn_out, w_ple, w_ple_gate, ln2_g, ln2_b, loss_target, m_w_in, m_conv_w, m_dn_a_log, m_dn_dt_bias, m_dn_norm_w, m_q_norm_w, m_w_uq, m_kv_norm_w, m_w_uk, m_w_uv, m_w_br_dn, m_w_br_mla, m_w_o, m_ln1_g, m_ln1_b, m_w_ffn_in, m_w_ffn_out, m_w_ple, m_w_ple_gate, m_ln2_g, m_ln2_b, v_w_in, v_conv_w, v_dn_a_log, v_dn_dt_bias, v_dn_norm_w, v_q_norm_w, v_w_uq, v_kv_norm_w, v_w_uk, v_w_uv, v_w_br_dn, v_w_br_mla, v_w_o, v_ln1_g, v_ln1_b, v_w_ffn_in, v_w_ffn_out, v_w_ple, v_w_ple_gate, v_ln2_g, v_ln2_b):
    ws = dict(w_in=w_in, conv_w=conv_w, dn_a_log=dn_a_log, dn_dt_bias=dn_dt_bias, dn_norm_w=dn_norm_w, q_norm_w=q_norm_w,
              w_uq=w_uq, kv_norm_w=kv_norm_w, w_uk=w_uk, w_uv=w_uv, w_br_dn=w_br_dn, w_br_mla=w_br_mla, w_o=w_o,
              ln1_g=ln1_g, ln1_b=ln1_b, w_ffn_in=w_ffn_in, w_ffn_out=w_ffn_out, w_ple=w_ple, w_ple_gate=w_ple_gate,
              ln2_g=ln2_g, ln2_b=ln2_b)
    ms = dict(w_in=m_w_in, conv_w=m_conv_w, dn_a_log=m_dn_a_log, dn_dt_bias=m_dn_dt_bias, dn_norm_w=m_dn_norm_w,
              q_norm_w=m_q_norm_w, w_uq=m_w_uq, kv_norm_w=m_kv_norm_w, w_uk=m_w_uk, w_uv=m_w_uv, w_br_dn=m_w_br_dn,
              w_br_mla=m_w_br_mla, w_o=m_w_o, ln1_g=m_ln1_g, ln1_b=m_ln1_b, w_ffn_in=m_w_ffn_in, w_ffn_out=m_w_ffn_out,
              w_ple=m_w_ple, w_ple_gate=m_w_ple_gate, ln2_g=m_ln2_g, ln2_b=m_ln2_b)
    vs = dict(w_in=v_w_in, conv_w=v_conv_w, dn_a_log=v_dn_a_log, dn_dt_bias=v_dn_dt_bias, dn_norm_w=v_dn_norm_w,
              q_norm_w=v_q_norm_w, w_uq=v_w_uq, kv_norm_w=v_kv_norm_w, w_uk=v_w_uk, w_uv=v_w_uv, w_br_dn=v_w_br_dn,
              w_br_mla=v_w_br_mla, w_o=v_w_o, ln1_g=v_ln1_g, ln1_b=v_ln1_b, w_ffn_in=v_w_ffn_in, w_ffn_out=v_w_ffn_out,
              w_ple=v_w_ple, w_ple_gate=v_w_ple_gate, ln2_g=v_ln2_g, ln2_b=v_ln2_b)
    mx, my, mc = lax.axis_index("x"), lax.axis_index("y"), lax.axis_index("c")

    me_chip = 2 * mx + my
    c_arr = jnp.reshape(mc, (1,)).astype(jnp.int32)
    me_arr = jnp.reshape(me_chip, (1,)).astype(jnp.int32)
    sharded = ('w_in', 'w_ffn_in') + tuple(name for name, _, _ in _FLATB_PIECES)

    local = _pack_shards({name: ws[name][0] for name in sharded}, conv_w[0])
    (gathered_in,) = _gather_shards(local[:1], "gather_w_in")
    w_in_full, conv_full = _unpack_w_in(gathered_in, local[0], me_chip)
    small = {n: ws[n][0] for n in _SMALL_NAMES}
    small['conv_w'] = conv_full
    sp = _prep_small(small)
    cosb, sinb = _rope_tables(positions[0])
    exch = _Exchange(local[1:], me_chip, c_arr)

    loss_lanes, dx, g = _local_step(x[0], p[0, 0], cosb, sinb, loss_target[0], _prep_w_in(w_in_full), sp, exch)
    gw = _unprep_grads_late(g)
    loss = lax.psum(jnp.sum(loss_lanes), ("x", "y", "c"))

    parts = [exch.part_in] + exch.parts
    arrived = exch.arrived_in + exch.arrived
    mine = [_chip_add(p_, r_, me_arr, tr, "chip_add_%d" % i) for i, (p_, r_, tr) in enumerate(zip(parts, arrived, _ADD_TILES))]
    reduced = _unpack_reduced(mine, _reduce_pair_share(mine, "reduce_pair_share"), mc)
    tot = _small_allreduce(_pack_small(gw))
    gred = {name: reduced[name][None] for name in sharded}
    for i, n in enumerate(_SMALL_NAMES):
        gred[n] = tot[i * _SMALL_GROUP, :ws[n].shape[1]][None]
    conv_tot = tot[_CONV_SMALL_ROW:_CONV_SMALL_ROW + _CONV_SMALL_ROWS].reshape(DN_CONV, QKV_W)
    gred['conv_w'] = lax.dynamic_slice_in_dim(conv_tot, (2 * mx + my) * _CONV_SHARD, _CONV_SHARD, axis=1)[None]

    deltas, new_m, new_v = {}, {}, {}
    for n in _WEIGHT_NAMES:
        if n == 'w_in':
            tr_ = lambda a: jnp.transpose(a, (0, 2, 1))
            g_t = tr_(gred[n].reshape(ws[n].shape))
            outs = _adamw(tr_(ws[n]), g_t, tr_(ms[n]), tr_(vs[n]), "adamw_" + n)
            gred[n] = tr_(g_t)
            deltas[n], new_m[n], new_v[n] = (tr_(o) for o in outs)
            continue
        gred[n] = gred[n].reshape(ws[n].shape)
        deltas[n], new_m[n], new_v[n] = _adamw(ws[n], gred[n], ms[n], vs[n], "adamw_" + n)
    return (loss, dx[None], *[gred[n] for n in _WEIGHT_NAMES], *[deltas[n] for n in _WEIGHT_NAMES],
            *[new_m[n] for n in _WEIGHT_NAMES], *[new_v[n] for n in _WEIGHT_NAMES])
```

```python
import functools
import math

import jax
import jax.numpy as jnp
from jax import lax
from jax.experimental import pallas as pl
from jax.experimental.pallas import tpu as pltpu

F32 = jnp.float32
_CDT = jnp.bfloat16
_HI = lax.Precision.HIGHEST
_MESH = pl.DeviceIdType.MESH

D_MODEL = 1024
PLE_DIM = 256
HEADS = 8
DN_DK = 128
DN_CHUNK = 64
DN_CONV = 4
QKV_W = 3 * HEADS * DN_DK
Q_LORA = 384
KV_LORA = 256
NOPE = 128
ROPE = 64
ROPE_PAD = 128
FFN_HIDDEN = 2816
D_IN = 6864
ROPE_BASE = 10000.0
ALPHA = 2.0 ** 0.25
ATT_SCALE = (NOPE + ROPE) ** -0.5
NEG_BIG = -1e30
ADAM_LR, ADAM_B1, ADAM_B2, ADAM_EPS, ADAM_WD, ADAM_STEP = 0.001, 0.9, 0.999, 1e-08, 0.01, 10

LANE = 128
VMEM_LIMIT = 56 * 1024 * 1024
MM_VMEM_BUDGET = 40 * 1024 * 1024
N_SHARD = 4
FLAT_W = 1024
SMALL_ROWS = 88


def _tile(dim, cap):
    if dim <= cap:
        return dim
    t = (cap // LANE) * LANE
    while t >= LANE:
        if dim % t == 0:
            return t
        t -= LANE
    return dim


def _cparams(sem):
    return pltpu.CompilerParams(dimension_semantics=sem, vmem_limit_bytes=VMEM_LIMIT)


def _mm(a, b, *, name, ta=False, tb=False, add=None, add_scale=1.0, out_dtype=F32, heads=None,
        a_head=None, b_head=None, out_head=None, dims=None, tm=1408, tn=1408):
    m, n, k = dims
    tm, tn = _tile(m, tm), _tile(n, tn)
    sa, sb, so = a.dtype.itemsize, b.dtype.itemsize, jnp.dtype(out_dtype).itemsize

    def vmem_need(tk_):
        acc = tm * tn * 4 if tk_ < k else 0
        extra = 2 * tm * tn * 4 if add is not None else 0
        return 2 * (tm * tk_ * sa + tk_ * tn * sb) + 2 * tm * tn * so + acc + extra

    tk = k
    while vmem_need(tk) > MM_VMEM_BUDGET and tk > LANE:
        smaller = _tile(k, tk - LANE)
        if smaller >= tk:
            break
        tk = smaller
    nk = k // tk
    hgrid = () if heads is None else (heads,)
    off = len(hgrid)

    def spec(rows, cols, rtile, ctile, rsel, csel, layout):
        def idx(*g):
            h = g[0] if off else 0
            ri, ci = g[off + rsel], g[off + csel]
            if layout == 'lead':
                return (h, ri, ci)
            if layout == 'col':
                return (ri, h * (cols // ctile) + ci)
            return (ri, ci)
        if layout == 'lead':
            return pl.BlockSpec((None, rtile, ctile), idx)
        return pl.BlockSpec((rtile, ctile), idx)

    a_spec = spec(k, m, tk, tm, 2, 0, a_head) if ta else spec(m, k, tm, tk, 0, 2, a_head)
    b_spec = spec(n, k, tn, tk, 1, 2, b_head) if tb else spec(k, n, tk, tn, 2, 1, b_head)
    o_spec = spec(m, n, tm, tn, 0, 1, out_head)
    in_specs = [a_spec, b_spec]
    args = [a, b]
    if add is not None:
        in_specs.append(spec(m, n, tm, tn, 0, 1, out_head))
        args.append(add)
    dn = (((0 if ta else 1,), (1 if tb else 0,)), ((), ()))

    def body(*refs):
        a_ref, b_ref = refs[0], refs[1]
        prod = lax.dot_general(a_ref[...].astype(_CDT), b_ref[...].astype(_CDT), dn, preferred_element_type=F32)
        if nk == 1:
            o_ref = refs[-1]
            if add is not None:
                prod = prod + refs[2][...].astype(F32) * add_scale
            o_ref[...] = prod.astype(out_dtype)
            return
        o_ref, acc_ref = refs[-2], refs[-1]
        kk = pl.program_id(off + 2)

        @pl.when(kk == 0)
        def _():
            if add is not None:
                acc_ref[...] = refs[2][...].astype(F32) * add_scale
            else:
                acc_ref[...] = jnp.zeros_like(acc_ref)

        acc_ref[...] += prod

        @pl.when(kk == nk - 1)
        def _():
            o_ref[...] = acc_ref[...].astype(out_dtype)

    if out_head == 'lead':
        oshape = (heads, m, n)
    elif out_head == 'col':
        oshape = (m, heads * n)
    else:
        oshape = (m, n)
    sem = ("parallel",) * (off + 2) + ("arbitrary",)
    return pl.pallas_call(
        body, name=name, grid=hgrid + (m // tm, n // tn, nk), in_specs=in_specs, out_specs=o_spec,
        out_shape=jax.ShapeDtypeStruct(oshape, out_dtype),
        scratch_shapes=[pltpu.VMEM((tm, tn), F32)] if nk > 1 else [],
        compiler_params=_cparams(sem))(*args)


def _mm2(a, b, **kw):
    ta, tb = kw.get('ta', False), kw.get('tb', False)
    m = a.shape[1] if ta else a.shape[0]
    k = a.shape[0] if ta else a.shape[1]
    n = b.shape[0] if tb else b.shape[1]
    return _mm(a, b, dims=(m, n, k), **kw)


def _rowwise(fn, rows, bcast, outs, reds=(), *, name, tm=256, heads=None):
    t = rows[0][0].shape[0]
    tm = min(tm, t)
    hn = 1 if heads is None else heads
    in_specs, args = [], []
    for arr, width, base, per_head in rows:
        in_specs.append(pl.BlockSpec((tm, width), functools.partial(
            lambda i, h, base, per_head: (i, base + (h if per_head else 0)), base=base, per_head=per_head)))
        args.append(arr)
    for arr in bcast:
        in_specs.append(pl.BlockSpec(arr.shape, lambda i, h: (0, 0)))
        args.append(arr)
    out_specs, out_shape = [], []
    for total, width, per_head, dt in outs:
        out_specs.append(pl.BlockSpec((tm, width), functools.partial(
            lambda i, h, per_head: (i, h if per_head else 0), per_head=per_head)))
        out_shape.append(jax.ShapeDtypeStruct((t, total), dt))
    for shp in reds:
        out_specs.append(pl.BlockSpec(shp, lambda i, h: (0, 0)))
        out_shape.append(jax.ShapeDtypeStruct(shp, F32))
    n_in, n_out, n_red = len(args), len(outs), len(reds)

    def body(*refs):
        i, h = pl.program_id(0), pl.program_id(1)
        vals = fn(h, *[r[...] for r in refs[:n_in]])
        for r, v in zip(refs[n_in:n_in + n_out], vals[:n_out]):
            r[...] = v.astype(r.dtype)
        if n_red:
            @pl.when((i == 0) & (h == 0))
            def _():
                for r in refs[n_in + n_out:]:
                    r[...] = jnp.zeros_like(r)
            for r, v in zip(refs[n_in + n_out:], vals[n_out:]):
                r[...] += v

    sem = ("arbitrary", "arbitrary") if n_red else ("parallel", "parallel")
    res = pl.pallas_call(body, name=name, grid=(t // tm, hn), in_specs=in_specs, out_specs=out_specs,
                         out_shape=out_shape, compiler_params=_cparams(sem))(*args)
    return tuple(res)


def _sigmoid(x):
    return 1.0 / (1.0 + jnp.exp(-x))


def _silu(x):
    return x * _sigmoid(x)


def _softplus(x):
    return jnp.maximum(x, 0.0) + jnp.log(1.0 + jnp.exp(-jnp.abs(x)))


def _layer_norm(t, g, b):
    mu = jnp.mean(t, axis=-1, keepdims=True)
    d = t - mu
    var = jnp.mean(d * d, axis=-1, keepdims=True)
    return d * lax.rsqrt(var + 1e-5) * g + b


def _rms_norm(t, w):
    return t * lax.rsqrt(jnp.mean(t * t, axis=-1, keepdims=True) + 1e-6) * w


def _swap_rope_halves(t):
    lane = lax.broadcasted_iota(jnp.int32, t.shape, 1) % ROPE_PAD
    n = t.shape[1]
    up = pltpu.roll(t, n - ROPE // 2, axis=1)
    dn = pltpu.roll(t, ROPE // 2, axis=1)
    return jnp.where(lane < ROPE // 2, up, jnp.where(lane < ROPE, dn, 0.0))


def _rope(t, cosb, sinb):
    reps = t.shape[1] // ROPE_PAD
    c = jnp.tile(cosb, (1, reps)) if reps > 1 else cosb
    s = jnp.tile(sinb, (1, reps)) if reps > 1 else sinb
    return t * c + _swap_rope_halves(t) * s


def _rope_bwd(d, cosb, sinb):
    reps = d.shape[1] // ROPE_PAD
    c = jnp.tile(cosb, (1, reps)) if reps > 1 else cosb
    s = jnp.tile(sinb, (1, reps)) if reps > 1 else sinb
    return d * c + _swap_rope_halves(d * s)


_CONV_ROWS = 256
_CONV_COLS = 256


def _conv_window(ref, r0, lo, hi, t):
    parts = []
    start, stop = r0 - lo, r0 + _CONV_ROWS + hi
    if start < 0:
        parts.append(jnp.zeros((-start, ref.shape[1]), F32))
        start = 0
    tail = max(stop - t, 0)
    parts.append(ref[start:stop - tail, :].astype(F32))
    if tail:
        parts.append(jnp.zeros((tail, ref.shape[1]), F32))
    return parts[0] if len(parts) == 1 else jnp.concatenate(parts, axis=0)


def _conv_taps(win, w_ref, n_out):
    acc = win[8:8 + n_out] * w_ref[DN_CONV - 1:DN_CONV, :]
    for i in range(DN_CONV - 1):
        acc = acc + pltpu.roll(win, DN_CONV - 1 - i, axis=0)[8:8 + n_out] * w_ref[i:i + 1, :]
    return acc


def _conv_silu(x, w):
    t, ch = x.shape

    def body(x_ref, w_ref, o_ref):
        for r in range(t // _CONV_ROWS):
            r0 = r * _CONV_ROWS
            c = _conv_taps(_conv_window(x_ref, r0, 8, 0, t), w_ref, _CONV_ROWS)
            o_ref[r0:r0 + _CONV_ROWS, :] = _silu(c)

    return pl.pallas_call(
        body, name="conv_silu", grid=(ch // _CONV_COLS,),
        in_specs=[pl.BlockSpec((t, _CONV_COLS), lambda j: (0, j)), pl.BlockSpec((DN_CONV, _CONV_COLS), lambda j: (0, j))],
        out_specs=pl.BlockSpec((t, _CONV_COLS), lambda j: (0, j)),
        out_shape=jax.ShapeDtypeStruct((t, ch), F32), compiler_params=_cparams(("parallel",)))(x, w)


def _conv_silu_bwd(x, w, dys):
    t, ch = x.shape
    per = ch // len(dys) // _CONV_COLS

    def body(x_ref, w_ref, *rest):
        dy_refs, (dx_ref, dw_ref) = rest[:len(dys)], rest[len(dys):]
        sec = pl.program_id(0) // per
        dws = [jnp.zeros((1, _CONV_COLS), F32) for _ in range(DN_CONV)]
        for r in range(t // _CONV_ROWS):
            r0 = r * _CONV_ROWS
            n_ext = _CONV_ROWS + 8
            xw = _conv_window(x_ref, r0, 8, 8, t)
            c = _conv_taps(xw, w_ref, n_ext)
            sg = _sigmoid(c)
            dy = _conv_window(dy_refs[-1], r0, 0, 8, t)
            for k in range(len(dys) - 2, -1, -1):
                dy = jnp.where(sec == k, _conv_window(dy_refs[k], r0, 0, 8, t), dy)
            ds = dy * (sg * (1.0 + c * (1.0 - sg)))
            x0 = xw[8:8 + _CONV_ROWS]
            dx = jnp.zeros((_CONV_ROWS, _CONV_COLS), F32)
            for i in range(DN_CONV):
                sh = DN_CONV - 1 - i
                ds_up = (ds if sh == 0 else pltpu.roll(ds, n_ext - sh, axis=0))[:_CONV_ROWS]
                dx = dx + ds_up * w_ref[i:i + 1, :]
                dws[i] = dws[i] + jnp.sum(x0 * ds_up, axis=0, keepdims=True)
            dx_ref[r0:r0 + _CONV_ROWS, :] = dx.astype(dx_ref.dtype)
        for i in range(DN_CONV):
            dw_ref[i:i + 1, :] = dws[i]

    blk = pl.BlockSpec((t, _CONV_COLS), lambda j: (0, j))
    wblk = pl.BlockSpec((DN_CONV, _CONV_COLS), lambda j: (0, j))
    dy_specs = [pl.BlockSpec((t, _CONV_COLS), functools.partial(lambda j, k: (0, jnp.clip(j - k * per, 0, per - 1)), k=k))
                for k in range(len(dys))]
    return pl.pallas_call(
        body, name="conv_silu_bwd", grid=(ch // _CONV_COLS,), in_specs=[blk, wblk] + dy_specs, out_specs=[blk, wblk],
        out_shape=[jax.ShapeDtypeStruct((t, ch), _CDT), jax.ShapeDtypeStruct((DN_CONV, ch), F32)],
        compiler_params=_cparams(("arbitrary",)))(x, w, *dys)


_PA_ROWS = 512


def _bmm(a, b, spec, exact=False):
    if exact:
        return jnp.einsum(spec, a, b, precision=_HI, preferred_element_type=F32)
    return jnp.einsum(spec, a.astype(_CDT), b.astype(_CDT), preferred_element_type=F32)


def _split16(a):
    hi = a.astype(jnp.bfloat16)
    return hi, (a - hi.astype(F32)).astype(jnp.bfloat16)


def _bmm3(a, b, spec):
    ah, al = _split16(a)
    bh, bl = _split16(b)
    e = lambda p, q: jnp.einsum(spec, p, q, preferred_element_type=F32)
    return e(ah, bh) + (e(ah, bl) + e(al, bh))


def _split3(b):
    b0 = b.astype(jnp.bfloat16)
    r1 = b - b0.astype(F32)
    b1 = r1.astype(jnp.bfloat16)
    return b0, b1, (r1 - b1.astype(F32)).astype(jnp.bfloat16)


@functools.partial(jax.custom_vjp, nondiff_argnums=(2, 3))
def _select_mm(sel, b, spec, spec_t):
    return sum(jnp.einsum(spec, sel, t, preferred_element_type=F32) for t in _split3(b))


def _select_mm_fwd(sel, b, spec, spec_t):
    return _select_mm(sel, b, spec, spec_t), sel


def _select_mm_bwd(spec, spec_t, sel, ct):
    return jnp.zeros_like(sel), sum(jnp.einsum(spec_t, sel, t, preferred_element_type=F32) for t in _split3(ct))


_select_mm.defvjp(_select_mm_fwd, _select_mm_bwd)


def _tri_inverse(l_mat, eye):
    pw = -l_mat
    t_inv = eye + pw
    for _ in range(5):
        pw = _bmm3(pw, pw, 'bij,bjk->bik')
        t_inv = t_inv + _bmm3(t_inv, pw, 'bij,bjk->bik')
    return t_inv


@jax.custom_vjp
def _tri_inverse_saved(l_mat, t_saved):
    return t_saved


def _tri_inverse_saved_fwd(l_mat, t_saved):
    return t_saved, t_saved


def _tri_inverse_saved_bwd(t_saved, dt):
    left = _bmm3(t_saved, dt, 'bji,bjk->bik')
    return -_bmm3(left, t_saved, 'bij,bkj->bik'), jnp.zeros_like(t_saved)


_tri_inverse_saved.defvjp(_tri_inverse_saved_fwd, _tri_inverse_saved_bwd)


def _phase_a(h, q, k, v, ba, alog, dtb, t_saved=None):
    r = q.shape[0]
    nb = r // DN_CHUNK
    c = DN_CHUNK
    lane = lax.broadcasted_iota(jnp.int32, (1, LANE), 1)
    selb = (lane == h).astype(F32)
    sela = (lane == h + HEADS).astype(F32)
    b_raw = jnp.sum(ba * selb, axis=1, keepdims=True)
    a_raw = jnp.sum(ba * sela, axis=1, keepdims=True)
    al = jnp.sum(alog * selb, axis=1, keepdims=True)
    dt = jnp.sum(dtb * selb, axis=1, keepdims=True)
    beta = jnp.broadcast_to(_sigmoid(b_raw), (r, LANE))
    g = jnp.broadcast_to(-jnp.exp(al) * _softplus(a_raw + dt), (r, LANE))
    qn = q * lax.rsqrt(jnp.sum(q * q, -1, keepdims=True) + 1e-6) * (DN_DK ** -0.5)
    kn = k * lax.rsqrt(jnp.sum(k * k, -1, keepdims=True) + 1e-6)
    q3, k3, v3 = qn.reshape(nb, c, LANE), kn.reshape(nb, c, LANE), v.reshape(nb, c, LANE)
    b3, g3 = beta.reshape(nb, c, LANE), g.reshape(nb, c, LANE)
    ri = lax.broadcasted_iota(jnp.int32, (nb, c, c), 1)
    ci = lax.broadcasted_iota(jnp.int32, (nb, c, c), 2)
    tril, strict = ri >= ci, ri > ci
    gc = _select_mm(tril.astype(jnp.bfloat16), g3, 'bij,bjd->bid', 'bij,bid->bjd')
    onehot = (lax.broadcasted_iota(jnp.int32, (nb, c, LANE), 2) == 0).astype(jnp.bfloat16)
    g_row = _select_mm(onehot, gc, 'bid,bjd->bij', 'bid,bij->bjd')
    diff = gc[:, :, :c] - g_row
    decay = jnp.where(tril, jnp.exp(jnp.where(tril, diff, 0.0)), 0.0)
    kb = k3 * b3
    l_mat = jnp.where(strict, _bmm(kb, k3, 'bid,bjd->bij') * decay, 0.0)
    if t_saved is None:
        t_inv = _tri_inverse(l_mat, (ri == ci).astype(F32))
    else:
        t_inv = _tri_inverse_saved(l_mat, t_saved.reshape(nb, c, c))
    eg = jnp.exp(gc)
    u = _bmm(t_inv, v3 * b3, 'bij,bje->bie')
    w = _bmm(t_inv, kb * eg, 'bij,bje->bie')
    intra = jnp.where(tril, _bmm(q3, k3, 'bid,bjd->bij') * decay, 0.0)
    qd = q3 * eg
    gl = jnp.sum(g3, axis=1, keepdims=True)
    kt = k3 * jnp.exp(gl - gc)
    outs = (u.reshape(r, LANE), w.reshape(r, LANE), qd.reshape(r, LANE), kt.reshape(r, LANE),
            intra.reshape(r, c), gl.reshape(nb, LANE))
    if t_saved is not None:
        return outs
    qd2 = qd - _bmm(intra, w, 'bij,bjd->bid')
    au = _bmm(intra, u, 'bij,bje->bie')
    return outs + (t_inv.reshape(r, c), qd2.reshape(r, LANE), au.reshape(r, LANE))


def _pa_specs(t):
    rr = min(_PA_ROWS, t)
    nb = rr // DN_CHUNK
    qkv = [pl.BlockSpec((rr, LANE), functools.partial(lambda i, h, o: (i, o + h), o=o)) for o in (0, HEADS, 2 * HEADS)]
    ba = pl.BlockSpec((rr, LANE), lambda i, h: (i, 3))
    vec = pl.BlockSpec((1, LANE), lambda i, h: (0, 0))
    row = pl.BlockSpec((rr, LANE), lambda i, h: (i, h))
    intra = pl.BlockSpec((None, rr, DN_CHUNK), lambda i, h: (h, i, 0))
    gl = pl.BlockSpec((nb, LANE), lambda i, h: (i, h))
    return rr, qkv, ba, vec, row, intra, gl


def _grid_ends(grid):
    first = lambda: functools.reduce(jnp.logical_and, [pl.program_id(a) == 0 for a in range(len(grid))])
    last = lambda: functools.reduce(jnp.logical_and, [pl.program_id(a) == n - 1 for a, n in enumerate(grid)])
    return first, last


def _delta_local(qkv_act, pm, alog, dtb, rider=None):
    t = qkv_act.shape[0]
    rr, qkv, ba, vec, row, intra, gl = _pa_specs(t)

    def body(q, k, v, b, al, dt, *outs):
        vals = _phase_a(pl.program_id(1), q[...], k[...], v[...], b[...], al[...], dt[...])
        for o, val in zip(outs, vals):
            o[...] = val

    wide = jax.ShapeDtypeStruct((t, HEADS * LANE), F32)
    sq = jax.ShapeDtypeStruct((HEADS, t, DN_CHUNK), F32)
    grid = (t // rr, HEADS)
    return _carried_call(
        body, rider, *_grid_ends(grid), name="delta_local", grid=grid, in_specs=qkv + [ba, vec, vec],
        out_specs=[row] * 4 + [intra, gl, intra, row, row],
        out_shape=[wide] * 4 + [sq, jax.ShapeDtypeStruct((t // DN_CHUNK, HEADS * LANE), F32), sq, wide, wide],
        scratch_shapes=[], sem=("arbitrary", "arbitrary"), args=(qkv_act, qkv_act, qkv_act, pm, alog, dtb))


def _delta_local_bwd(qkv_act, pm, alog, dtb, t_inv, du, dw, dqd, dkt, dintra, dgl, rider=None):
    t = qkv_act.shape[0]
    rr, qkv, ba, vec, row, intra, gl = _pa_specs(t)

    def body(q, k, v, b, al, dt, ti, du_r, dw_r, dqd_r, dkt_r, di_r, dgl_r, dq_o, dk_o, dv_o, dba_o, dal_o, ddt_o):
        i, h = pl.program_id(0), pl.program_id(1)
        t_saved = ti[...]
        _, vjp = jax.vjp(lambda *a: _phase_a(h, *a, t_saved=t_saved), q[...], k[...], v[...], b[...], al[...], dt[...])
        dq, dk, dv, dba, dal, ddt = vjp((du_r[...], dw_r[...], dqd_r[...], dkt_r[...], di_r[...], dgl_r[...]))
        dq_o[...], dk_o[...], dv_o[...] = dq, dk, dv

        @pl.when(h == 0)
        def _():
            dba_o[...] = jnp.zeros_like(dba_o)

        @pl.when((h == 0) & (i == 0))
        def _():
            dal_o[...] = jnp.zeros_like(dal_o)
            ddt_o[...] = jnp.zeros_like(ddt_o)

        dba_o[...] += dba
        dal_o[...] += dal
        ddt_o[...] += ddt

    wide = jax.ShapeDtypeStruct((t, HEADS * LANE), F32)
    vshape = jax.ShapeDtypeStruct((1, LANE), F32)
    grid = (t // rr, HEADS)
    return _carried_call(
        body, rider, *_grid_ends(grid), name="delta_local_bwd", grid=grid,
        in_specs=qkv + [ba, vec, vec, intra] + [row] * 4 + [intra, gl],
        out_specs=[row] * 3 + [pl.BlockSpec((rr, LANE), lambda i, h: (i, 0)), vec, vec],
        out_shape=[wide] * 3 + [jax.ShapeDtypeStruct((t, LANE), F32), vshape, vshape],
        scratch_shapes=[], sem=("arbitrary", "arbitrary"),
        args=(qkv_act, qkv_act, qkv_act, pm, alog, dtb, t_inv, du, dw, dqd, dkt, dintra, dgl))


_SCAN_ROWS = 512


def _dot(a, b, dn):
    return lax.dot_general(a.astype(_CDT), b.astype(_CDT), (dn, ((), ())), preferred_element_type=F32)


_NN = ((1,), (0,))
_NT = ((1,), (1,))
_TN = ((0,), (0,))


def _delta_scan(u, w, qd, kt, au, gl, rider=None):
    t = u.shape[0]
    rr = min(_SCAN_ROWS, t)
    nc = rr // DN_CHUNK

    def body(u_ref, w_ref, qd_ref, kt_ref, au_ref, gl_ref, o_ref, sall_ref, s_scr):
        @pl.when(pl.program_id(0) == 0)
        def _():
            s_scr[...] = jnp.zeros_like(s_scr)

        def chunk(c, carry):
            r0 = pl.multiple_of(c * DN_CHUNK, DN_CHUNK)
            rows = pl.ds(r0, DN_CHUNK)
            e = jnp.exp(gl_ref[pl.ds(c, 1), :])
            states = [s_scr[h] for h in range(HEADS)]
            u_c, w_c, qd_c, kt_c, au_c = u_ref[rows, :], w_ref[rows, :], qd_ref[rows, :], kt_ref[rows, :], au_ref[rows, :]
            o_new, s_new = [], []
            for h in range(HEADS):
                cs = slice(h * LANE, (h + 1) * LANE)
                s = states[h]
                both = _dot(jnp.concatenate([w_c[:, cs], qd_c[:, cs]], axis=0), s, _NN)
                v_new = u_c[:, cs] - both[:DN_CHUNK]
                o_new.append(both[DN_CHUNK:] + au_c[:, cs])
                s_new.append(s * e[:, cs] + _dot(kt_c[:, cs], v_new, _TN))
            o_ref[rows, :] = jnp.concatenate(o_new, axis=1)
            for h in range(HEADS):
                sall_ref[c, h] = states[h]
                s_scr[h] = s_new[h]
            return carry

        lax.fori_loop(0, nc, chunk, 0)

    row = pl.BlockSpec((rr, HEADS * LANE), lambda i: (i, 0))
    grid = (t // rr,)
    return _carried_call(
        body, rider, *_grid_ends(grid), name="delta_scan", grid=grid,
        in_specs=[row] * 5 + [pl.BlockSpec((nc, HEADS * LANE), lambda i: (i, 0))],
        out_specs=[row, pl.BlockSpec((nc, HEADS, LANE, LANE), lambda i: (i, 0, 0, 0))],
        out_shape=[jax.ShapeDtypeStruct((t, HEADS * LANE), F32),
                   jax.ShapeDtypeStruct((t // DN_CHUNK, HEADS, LANE, LANE), F32)],
        scratch_shapes=[pltpu.VMEM((HEADS, LANE, LANE), F32)], sem=("arbitrary",), args=(u, w, qd, kt, au, gl))


def _delta_scan_bwd(u, w, qd, kt, intra, gl, sall, do, rider=None):
    t = u.shape[0]
    rr = min(_SCAN_ROWS, t)
    nc = rr // DN_CHUNK
    ng = t // rr

    def body(u_ref, w_ref, qd_ref, kt_ref, a_ref, gl_ref, sall_ref, do_ref,
             du_ref, dw_ref, dqd_ref, dkt_ref, da_ref, dgl_ref, ds_scr):
        @pl.when(pl.program_id(0) == 0)
        def _():
            ds_scr[...] = jnp.zeros_like(ds_scr)

        def chunk(cc, carry):
            c = nc - 1 - cc
            r0 = pl.multiple_of(c * DN_CHUNK, DN_CHUNK)
            rows = pl.ds(r0, DN_CHUNK)
            e = jnp.exp(gl_ref[pl.ds(c, 1), :])
            states = [sall_ref[c, h] for h in range(HEADS)]
            ds_outs = [ds_scr[h] for h in range(HEADS)]
            u_a, w_a, kt_a, qd_a, do_a = u_ref[rows, :], w_ref[rows, :], kt_ref[rows, :], qd_ref[rows, :], do_ref[rows, :]
            a_a = [a_ref[h, rows, :] for h in range(HEADS)]
            da, dqd, dkt, du, dw, dgl, ds_new = [], [], [], [], [], [], []
            for h in range(HEADS):
                cs = slice(h * LANE, (h + 1) * LANE)
                s, ds_out = states[h], ds_outs[h]
                w_c, kt_c, qd_c, do_c = w_a[:, cs], kt_a[:, cs], qd_a[:, cs], do_a[:, cs]
                v_new = u_a[:, cs] - _dot(w_c, s, _NN)
                dv_new = _dot(a_a[h], do_c, _TN) + _dot(kt_c, ds_out, _NN)
                cots = jnp.concatenate([do_c, dv_new], axis=0)
                both = _dot(cots, s, _NT)
                dqd.append(both[:DN_CHUNK])
                dw.append(-both[DN_CHUNK:])
                da.append(_dot(do_c, v_new, _NT))
                dkt.append(_dot(v_new, ds_out, _NT))
                du.append(dv_new)
                eh = e[:, cs]
                dgl.append(jnp.broadcast_to(jnp.sum(ds_out * s, axis=0, keepdims=True) * eh, (8, LANE)))
                ds_new.append(ds_out * eh + _dot(jnp.concatenate([qd_c, -w_c], axis=0), cots, _TN))
            cat = lambda parts: jnp.concatenate(parts, axis=1)
            dqd_ref[rows, :], dkt_ref[rows, :], du_ref[rows, :], dw_ref[rows, :] = cat(dqd), cat(dkt), cat(du), cat(dw)
            dgl_ref[pl.ds(pl.multiple_of(c * 8, 8), 8), :] = cat(dgl)
            for h in range(HEADS):
                da_ref[h, rows, :] = da[h]
                ds_scr[h] = ds_new[h]
            return carry

        lax.fori_loop(0, nc, chunk, 0)

    rev = lambda i: (ng - 1 - i, 0)
    row = pl.BlockSpec((rr, HEADS * LANE), rev)
    a_spec = pl.BlockSpec((HEADS, rr, DN_CHUNK), lambda i: (0, ng - 1 - i, 0))
    gl_spec = pl.BlockSpec((nc, HEADS * LANE), rev)
    wide = jax.ShapeDtypeStruct((t, HEADS * LANE), F32)
    outs, carried = _carried_call(
        body, rider, *_grid_ends((ng,)), name="delta_scan_bwd", grid=(ng,),
        in_specs=[row] * 4 + [a_spec, gl_spec, pl.BlockSpec((nc, HEADS, LANE, LANE), lambda i: (ng - 1 - i, 0, 0, 0)), row],
        out_specs=[row] * 4 + [a_spec, pl.BlockSpec((nc * 8, HEADS * LANE), rev)],
        out_shape=[wide] * 4 + [jax.ShapeDtypeStruct((HEADS, t, DN_CHUNK), F32),
                                jax.ShapeDtypeStruct((t // DN_CHUNK * 8, HEADS * LANE), F32)],
        scratch_shapes=[pltpu.VMEM((HEADS, LANE, LANE), F32)], sem=("arbitrary",), args=(u, w, qd, kt, intra, gl, sall, do))
    return tuple(outs[:5]) + (outs[5].reshape(t // DN_CHUNK, 8, HEADS * LANE)[:, 0, :],), carried


_ATT_TILE = 512


def _kv_rows(j, tk):
    return pl.ds(pl.multiple_of(j * tk, tk), tk)


def _att_scores(ql, qr, ckv_ref, kr_ref, j, tk):
    ks = _kv_rows(j, tk)
    return (_dot(ql, ckv_ref[ks, :], _NT) + _dot(qr, kr_ref[ks, :], _NT)) * ATT_SCALE


def _diag_mask(s):
    qi = lax.broadcasted_iota(jnp.int32, s.shape, 0)
    ki = lax.broadcasted_iota(jnp.int32, s.shape, 1)
    return jnp.where(ki <= qi, s, NEG_BIG)


def _attention(qn, qr_pre, cosb, sinb, ckv, kr, wuk, wuv):
    t = ckv.shape[0]
    tq = min(_ATT_TILE, t)

    nl = tq // LANE

    def lane_fold(v, op):
        out = v[:, :LANE]
        for k in range(1, nl):
            out = op(out, v[:, k * LANE:(k + 1) * LANE])
        return out

    def body(qn_ref, qr_ref, cos_ref, sin_ref, ckv_ref, kr_ref, wuk_ref, wuv_ref, o_ref, lse_ref, qrope_ref, omla_ref,
             s_all, m_lanes, l_lanes, acc_scr):
        h, qi = pl.program_id(0), pl.program_id(1)
        q_lat = _dot(qn_ref[...], wuk_ref[h], _NT).astype(_CDT)
        q_rope = _rope(qr_ref[...], cos_ref[...], sin_ref[...]).astype(qrope_ref.dtype)
        qrope_ref[...] = q_rope
        m_lanes[...] = jnp.full_like(m_lanes, NEG_BIG)

        def scores(j, masked):
            s = _att_scores(q_lat, q_rope, ckv_ref, kr_ref, j, tq)
            if masked:
                s = _diag_mask(s)
            s_all[j] = s
            m_lanes[...] = jnp.maximum(m_lanes[...], lane_fold(s, jnp.maximum))

        def scores_body(j, carry):
            scores(j, False)
            return carry

        lax.fori_loop(0, qi, scores_body, 0)
        scores(qi, True)
        m = jnp.max(m_lanes[...], axis=-1, keepdims=True)
        mb = jnp.broadcast_to(m, (tq, LANE))
        l_lanes[...] = jnp.zeros_like(l_lanes)
        acc_scr[...] = jnp.zeros_like(acc_scr)

        def weigh(j, carry):
            s = s_all[j]
            p = jnp.concatenate([jnp.exp(s[:, k * LANE:(k + 1) * LANE] - mb) for k in range(nl)], axis=1)
            l_lanes[...] += lane_fold(p, jnp.add)
            acc_scr[...] += _dot(p, ckv_ref[_kv_rows(j, tq), :], _NN)
            return carry

        lax.fori_loop(0, qi + 1, weigh, 0)
        l = jnp.sum(l_lanes[...], axis=-1, keepdims=True)
        out = acc_scr[...] / l
        o_ref[...] = out
        lse_ref[...] = m + jnp.log(l)
        omla_ref[...] = _dot(out, wuv_ref[h], _NN).astype(omla_ref.dtype)

    col = pl.BlockSpec((tq, LANE), lambda h, i: (i, h))
    table = pl.BlockSpec((tq, ROPE_PAD), lambda h, i: (i, 0))
    wspec = pl.BlockSpec((HEADS, KV_LORA, NOPE), lambda h, i: (0, 0, 0))
    return pl.pallas_call(
        body, name="attention", grid=(HEADS, t // tq),
        in_specs=[col, col, table, table, pl.BlockSpec((t, KV_LORA), lambda h, i: (0, 0)),
                  pl.BlockSpec((t, ROPE_PAD), lambda h, i: (0, 0)), wspec, wspec],
        out_specs=[pl.BlockSpec((None, tq, KV_LORA), lambda h, i: (h, i, 0)),
                   pl.BlockSpec((None, tq, 1), lambda h, i: (h, i, 0)), col, col],
        out_shape=[jax.ShapeDtypeStruct((HEADS, t, KV_LORA), F32), jax.ShapeDtypeStruct((HEADS, t, 1), F32),
                   jax.ShapeDtypeStruct((t, HEADS * ROPE_PAD), _CDT), jax.ShapeDtypeStruct((t, HEADS * NOPE), _CDT)],
        scratch_shapes=[pltpu.VMEM((t // tq, tq, tq), F32), pltpu.VMEM((tq, LANE), F32), pltpu.VMEM((tq, LANE), F32),
                        pltpu.VMEM((tq, KV_LORA), F32)],
        compiler_params=_cparams(("parallel", "parallel")))(qn, qr_pre, cosb, sinb, ckv, kr, wuk, wuv)


def _attention_bwd(qn, qr, cosb, sinb, ckv, kr, wuk, wuv, out, lse, do_mla):
    t = ckv.shape[0]
    tq = min(_ATT_TILE, t)

    def body(qn_ref, qr_ref, cos_ref, sin_ref, ckv_ref, kr_ref, wuk_ref, wuv_ref, o_ref, lse_ref, do_ref,
             dql_ref, dqr_ref, dqn_ref, dckv_ref, dkr_ref, dql_scr, dqr_scr):
        h, qi = pl.program_id(0), pl.program_id(1)

        @pl.when((h == 0) & (qi == 0))
        def _():
            dckv_ref[...] = jnp.zeros_like(dckv_ref)
            dkr_ref[...] = jnp.zeros_like(dkr_ref)

        q_lat = _dot(qn_ref[...], wuk_ref[h], _NT).astype(_CDT)
        q_rope = qr_ref[...]
        d_out = _dot(do_ref[...], wuv_ref[h], _NT)
        d_o = d_out.astype(_CDT)
        lse_v = lse_ref[...]
        dsum = jnp.sum(d_out * o_ref[...], axis=-1, keepdims=True)
        dql_scr[...] = jnp.zeros_like(dql_scr)
        dqr_scr[...] = jnp.zeros_like(dqr_scr)

        def step(j, masked):
            ks = _kv_rows(j, tq)
            s = _att_scores(q_lat, q_rope, ckv_ref, kr_ref, j, tq)
            if masked:
                s = _diag_mask(s)
            p = jnp.exp(s - lse_v)
            kv = ckv_ref[ks, :]
            ds = (p * (_dot(d_o, kv, _NT) - dsum) * ATT_SCALE).astype(_CDT)
            pb = p.astype(_CDT)
            dql_scr[...] += _dot(ds, kv, _NN)
            dqr_scr[...] += _dot(ds, kr_ref[ks, :], _NN)
            dckv_ref[ks, :] += _dot(pb, d_o, _TN) + _dot(ds, q_lat, _TN)
            dkr_ref[ks, :] += _dot(ds, q_rope, _TN)

        def loop_body(j, carry):
            step(j, False)
            return carry

        lax.fori_loop(0, qi, loop_body, 0)
        step(qi, True)
        dql = dql_scr[...].astype(dql_ref.dtype)
        dql_ref[...] = dql
        dqn_ref[...] = _dot(dql, wuk_ref[h], _NN).astype(dqn_ref.dtype)
        dqr_ref[...] = _rope_bwd(dqr_scr[...], cos_ref[...], sin_ref[...]).astype(dqr_ref.dtype)

    lat = pl.BlockSpec((None, tq, KV_LORA), lambda h, i: (h, i, 0))
    col = pl.BlockSpec((tq, LANE), lambda h, i: (i, h))
    table = pl.BlockSpec((tq, ROPE_PAD), lambda h, i: (i, 0))
    kfull = pl.BlockSpec((t, KV_LORA), lambda h, i: (0, 0))
    rfull = pl.BlockSpec((t, ROPE_PAD), lambda h, i: (0, 0))
    wspec = pl.BlockSpec((HEADS, KV_LORA, NOPE), lambda h, i: (0, 0, 0))
    wide = jax.ShapeDtypeStruct((t, HEADS * LANE), _CDT)
    return pl.pallas_call(
        body, name="attention_bwd", grid=(HEADS, t // tq),
        in_specs=[col, col, table, table, kfull, rfull, wspec, wspec, lat,
                  pl.BlockSpec((None, tq, 1), lambda h, i: (h, i, 0)), col],
        out_specs=[lat, col, col, kfull, rfull],
        out_shape=[jax.ShapeDtypeStruct((HEADS, t, KV_LORA), _CDT), wide, wide,
                   jax.ShapeDtypeStruct((t, KV_LORA), F32), jax.ShapeDtypeStruct((t, ROPE_PAD), F32)],
        scratch_shapes=[pltpu.VMEM((tq, KV_LORA), F32), pltpu.VMEM((tq, ROPE_PAD), F32)],
        compiler_params=_cparams(("arbitrary", "arbitrary")))(qn, qr, cosb, sinb, ckv, kr, wuk, wuv, out, lse, do_mla)


_DX_ROWS = 256


def _dx_fused(pairs, add, add_scale, rider=None, name="b_dx"):
    t, d = add.shape
    tm = min(_DX_ROWS, t)
    n = len(pairs)

    def body(*refs):
        acc = refs[2 * n][...] * add_scale
        for i in range(n):
            acc = acc + _dot(refs[i][...], refs[n + i][...], _NT)
        refs[2 * n + 1][...] = acc

    in_specs = [pl.BlockSpec((tm, a.shape[1]), lambda i: (i, 0)) for a, _ in pairs]
    in_specs += [pl.BlockSpec(w.shape, lambda i: (0, 0)) for _, w in pairs]
    row = pl.BlockSpec((tm, d), lambda i: (i, 0))
    grid = (t // tm,)
    (dx,), carried = _carried_call(
        body, rider, *_grid_ends(grid), name=name, grid=grid, in_specs=in_specs + [row], out_specs=[row],
        out_shape=[jax.ShapeDtypeStruct((t, d), F32)], scratch_shapes=[], sem=("arbitrary",),
        args=tuple(a for a, _ in pairs) + tuple(w for _, w in pairs) + (add,))
    return dx, carried


def _ffn_in_swiglu(a, w):
    t, k = a.shape
    hid = w.shape[1] // 2
    tm, tn = _tile(t, 1024), _tile(hid, 1408)
    nj = hid // tn

    def body(a_ref, bg_ref, bu_ref, act_ref, gt_ref, up_ref):
        av = a_ref[...].astype(_CDT)
        gt = jnp.dot(av, bg_ref[...].astype(_CDT), preferred_element_type=F32)
        up = jnp.dot(av, bu_ref[...].astype(_CDT), preferred_element_type=F32)
        act_ref[...] = _swiglu(gt, up).astype(act_ref.dtype)
        gt_ref[...] = gt.astype(gt_ref.dtype)
        up_ref[...] = up.astype(up_ref.dtype)

    out = pl.BlockSpec((tm, tn), lambda i, j: (i, j))
    return pl.pallas_call(
        body, name="f_ffn_in_swiglu", grid=(t // tm, nj),
        in_specs=[pl.BlockSpec((tm, k), lambda i, j: (i, 0)), pl.BlockSpec((k, tn), lambda i, j: (0, j)),
                  pl.BlockSpec((k, tn), lambda i, j: (0, nj + j))],
        out_specs=[out] * 3, out_shape=[jax.ShapeDtypeStruct((t, hid), _CDT)] * 3,
        compiler_params=_cparams(("parallel", "parallel")))(a, w, w)


def _gated_norm(o, z, w):
    return _rms_norm(o, w) * _silu(z)


def _gated_norm_heads(o, z, w):
    heads = [_gated_norm(o[:, h * LANE:(h + 1) * LANE], z[:, h * LANE:(h + 1) * LANE], w) for h in range(HEADS)]
    return jnp.concatenate(heads, axis=1)


def _mla_pre(ckv, krp, cq, cosb, sinb, qw, kw):
    return _rms_norm(cq, qw), _rms_norm(ckv, kw), _rope(krp, cosb, sinb)


def _merge(gg, y_dn, y_mla):
    return _sigmoid(gg[:, :D_MODEL]) * y_dn + _sigmoid(gg[:, D_MODEL:]) * y_mla


def _ln1(xv, attn_out, g, b):
    return _layer_norm(ALPHA * xv + attn_out, g, b)


def _final(h1, ffn, gate_pre, ple_proj, g, b):
    return _layer_norm(ALPHA * h1 + ffn + _sigmoid(gate_pre) * ple_proj, g, b)


def _swiglu(gt, up):
    return _silu(gt) * up


def _local_step(x, p, cosb, sinb, target, wt, sp, exch):
    t = x.shape[0]
    bf = _CDT
    xb = x.astype(bf)
    g = {}

    qkv_pre = _mm2(xb, wt['qkv'], name="f_qkv")
    z = _mm2(xb, wt['z'], name="f_z")
    gg = _mm2(xb, wt['gg'], name="f_gg")
    pm = _mm2(xb, wt['mla'], name="f_mla")
    qkv_act = _conv_silu(qkv_pre, sp['conv_w'])
    (u, w_, qd, kt, intra, gl, t_inv, qd2, au), sent = _delta_local(qkv_act, pm, sp['a_log'], sp['dt_bias'],
                                                                    rider=exch.gather_send())
    (o_dn, sall), passed = _delta_scan(u, w_, qd2, kt, au, gl, rider=exch.gather_pass(sent))
    wt = dict(wt, **exch.weights(passed))
    def gated_norm_br(h, o, zz, w, wbr):
        og_v = _gated_norm_heads(o, zz, w).astype(bf)
        return og_v, jnp.dot(og_v, wbr.astype(bf), preferred_element_type=F32)

    og, y_dn = _rowwise(gated_norm_br, [(o_dn, D_MODEL, 0, False), (z, D_MODEL, 0, False)],
                        [sp['dn_norm_w'], wt['br_dn']],
                        [(D_MODEL, D_MODEL, False, bf), (D_MODEL, D_MODEL, False, F32)], name="f_gated_norm_br")

    def mla_pre_uq(h, ckv, krp, cq, cosv, sinv, qw, kw, wn, wr):
        c_q_v, c_kv_v, k_rope_v = _mla_pre(ckv, krp, cq, cosv, sinv, qw, kw)
        c_q_b = c_q_v.astype(bf)
        return (c_q_b, c_kv_v, k_rope_v, jnp.dot(c_q_b, wn.astype(bf), preferred_element_type=F32),
                jnp.dot(c_q_b, wr.astype(bf), preferred_element_type=F32))

    c_q, c_kv, k_rope, q_nope, q_rope_pre = _rowwise(
        mla_pre_uq,
        [(pm, KV_LORA, 0, False), (pm, ROPE_PAD, 2, False), (pm, Q_LORA, 2, False),
         (cosb, ROPE_PAD, 0, False), (sinb, ROPE_PAD, 0, False)],
        [sp['q_norm_w'], sp['kv_norm_w'], wt['uq_nope'], wt['uq_rope']],
        [(Q_LORA, Q_LORA, False, bf), (KV_LORA, KV_LORA, False, bf), (ROPE_PAD, ROPE_PAD, False, bf),
         (HEADS * NOPE, HEADS * NOPE, False, bf), (HEADS * ROPE_PAD, HEADS * ROPE_PAD, False, F32)], name="f_mla_pre_uq")
    out_lat, lse, q_rope, o_mla = _attention(q_nope, q_rope_pre, cosb, sinb, c_kv, k_rope, wt['uk'], wt['uv'])
    y_mla = _mm2(o_mla, wt['br_mla'], name="f_br_mla")

    def merge_o_ln1(h, ggv, yd, ym, xv, wo, gv, bv):
        mixed_v = _merge(ggv, yd, ym).astype(bf)
        ao = jnp.dot(mixed_v, wo.astype(bf), preferred_element_type=F32)
        h1v = _ln1(xv, ao, gv, bv)
        return mixed_v, ao, h1v, h1v

    mixed, attn_out, h1, h1b = _rowwise(
        merge_o_ln1, [(gg, 2 * D_MODEL, 0, False), (y_dn, D_MODEL, 0, False), (y_mla, D_MODEL, 0, False), (x, D_MODEL, 0, False)],
        [wt['o'], sp['ln1_g'], sp['ln1_b']],
        [(D_MODEL, D_MODEL, False, bf), (D_MODEL, D_MODEL, False, F32), (D_MODEL, D_MODEL, False, F32),
         (D_MODEL, D_MODEL, False, bf)], name="f_merge_o_ln1")
    act, ffn_gt, ffn_up = _ffn_in_swiglu(h1b, wt['ffn_in'])
    pb = p.astype(bf)

    def final_fn(h, h1v, actv, pv, tgt, gt, up, wfo, wpg, wpl, gv, bv):
        ffnv = jnp.dot(actv.astype(bf), wfo.astype(bf), preferred_element_type=F32)
        gpv = jnp.dot(h1v.astype(bf), wpg.astype(bf), preferred_element_type=F32)
        ppv = jnp.dot(pv.astype(bf), wpl.astype(bf), preferred_element_type=F32)
        y, vjp = jax.vjp(_final, h1v, ffnv, gpv, ppv, gv, bv)
        err = y - tgt
        dh1, dffn, dgp, dpp, dg, db = vjp(err * (1.0 / D_MODEL))
        sq = err * err
        lanes = sq[:, :LANE]
        for j in range(1, D_MODEL // LANE):
            lanes = lanes + sq[:, j * LANE:(j + 1) * LANE]
        loss = jnp.sum(lanes, axis=0, keepdims=True) * (0.5 / D_MODEL)
        dffn_b = dffn.astype(bf)
        dact = _dot(dffn_b, wfo, _NT).astype(bf).astype(F32)
        _, vjp_s = jax.vjp(_swiglu, gt.astype(F32), up.astype(F32))
        dgt, dup = vjp_s(dact)
        return dffn, dffn_b, dgp, dpp, jnp.concatenate([dgt, dup], axis=1), dg, db, loss

    dpre2, dpre2b, dgate_pre, dple_proj, dffn_in, g['ln2_g'], g['ln2_b'], loss_lanes = _rowwise(
        final_fn, [(h1, D_MODEL, 0, False), (act, FFN_HIDDEN, 0, False), (pb, PLE_DIM, 0, False), (target, D_MODEL, 0, False),
                   (ffn_gt, FFN_HIDDEN, 0, False), (ffn_up, FFN_HIDDEN, 0, False)],
        [wt['ffn_out'], wt['ple_gate'], wt['ple'], sp['ln2_g'], sp['ln2_b']],
        [(D_MODEL, D_MODEL, False, F32)] + [(D_MODEL, D_MODEL, False, bf)] * 3 + [(2 * FFN_HIDDEN, 2 * FFN_HIDDEN, False, bf)],
        [(1, D_MODEL), (1, D_MODEL), (1, LANE)], name="b_final")

    g['ple'] = _mm2(pb, dple_proj, ta=True, name="g_ple")
    g['ple_gate'] = _mm2(h1b, dgate_pre, ta=True, name="g_ple_gate")
    g['ffn_out'] = _mm2(act, dpre2b, ta=True, name="g_ffn_out")
    g['ffn_in'] = _mm2(h1b, dffn_in, ta=True, name="g_ffn_in")
    dh1, _ = _dx_fused([(dffn_in, wt['ffn_in']), (dgate_pre, wt['ple_gate'])], dpre2, ALPHA, name="b_dh1")

    def attn_out_bwd(h, xv, ao, d, ggv, yd, ym, o, zz, wo, wbd, wbm, gv, bv, nw):
        _, vjp = jax.vjp(_ln1, xv, ao, gv, bv)
        _, dao, dg, db = vjp(d)
        dao_b = dao.astype(bf)
        _, vjp_m = jax.vjp(_merge, ggv, yd, ym)
        dggv, dyd, dym = vjp_m(_dot(dao_b, wo, _NT))
        dyd_b, dym_b = dyd.astype(bf), dym.astype(bf)
        _, vjp_n = jax.vjp(_gated_norm_heads, o, zz, nw)
        do_v, dz_v, dnw = vjp_n(_dot(dyd_b, wbd, _NT))
        return dao, dao_b, dggv, dyd_b, dym_b, do_v, dz_v, _dot(dym_b, wbm, _NT), dg, db, dnw

    row_d = lambda a: (a, D_MODEL, 0, False)
    dpre1, dpre1b, dgg, dy_dn, dy_mla, do_dn, dz, do_mla, g['ln1_g'], g['ln1_b'], g['dn_norm_w'] = _rowwise(
        attn_out_bwd, [row_d(x), row_d(attn_out), row_d(dh1), (gg, 2 * D_MODEL, 0, False), row_d(y_dn), row_d(y_mla),
                       row_d(o_dn), row_d(z)],
        [wt['o'], wt['br_dn'], wt['br_mla'], sp['ln1_g'], sp['ln1_b'], sp['dn_norm_w']],
        [(D_MODEL, D_MODEL, False, F32), (D_MODEL, D_MODEL, False, bf), (2 * D_MODEL, 2 * D_MODEL, False, bf),
         (D_MODEL, D_MODEL, False, bf), (D_MODEL, D_MODEL, False, bf), (D_MODEL, D_MODEL, False, F32),
         (D_MODEL, D_MODEL, False, bf), (D_MODEL, D_MODEL, False, bf)],
        [(1, D_MODEL), (1, D_MODEL), (1, LANE)], name="b_attn_out")
    g['o'] = _mm2(mixed, dpre1b, ta=True, name="g_o")
    g['br_dn'] = _mm2(og, dy_dn, ta=True, name="g_br_dn")
    g['br_mla'] = _mm2(o_mla, dy_mla, ta=True, name="g_br_mla")

    g['uv'] = _mm(out_lat, do_mla, name="g_uv", ta=True, heads=HEADS, a_head='lead', b_head='col', out_head='lead',
                  dims=(KV_LORA, NOPE, t))
    dq_lat, dq_rope_pre, dq_nope, dckv_att, dkr_att = _attention_bwd(
        q_nope, q_rope, cosb, sinb, c_kv, k_rope, wt['uk'], wt['uv'], out_lat, lse, do_mla)
    g['uk'] = _mm(dq_lat, q_nope, name="g_uk", ta=True, heads=HEADS, a_head='lead', b_head='col', out_head='lead',
                  dims=(KV_LORA, NOPE, t))
    g['uq_nope'] = _mm2(c_q, dq_nope, ta=True, name="g_uq_nope")
    g['uq_rope'] = _mm2(c_q, dq_rope_pre, ta=True, name="g_uq_rope")
    dc_q = _mm2(dq_nope, wt['uq_nope'], tb=True, name="b_dcq_nope")
    dc_q = _mm2(dq_rope_pre, wt['uq_rope'], tb=True, name="b_dcq_rope", add=dc_q)

    (du, dw, dqd, dkt, dintra, dgl), paired = _delta_scan_bwd(u, w_, qd, kt, intra, gl, sall, do_dn, rider=exch.pair_send(g))
    (dq_a, dk_a, dv_a, dba, g['a_log'], g['dt_bias']), arrived = _delta_local_bwd(
        qkv_act, pm, sp['a_log'], sp['dt_bias'], t_inv, du, dw, dqd, dkt, dintra, dgl, rider=exch.reduce_send(paired))
    exch.reduce_arrived(arrived)
    dqkv_pre, g['conv_w'] = _conv_silu_bwd(qkv_pre, sp['conv_w'], [dq_a, dk_a, dv_a])

    def mla_pre_bwd(h, ckv, cq, cosv, sinv, dcq, dckv, dkr, dba_v, qw, kw):
        _, vjp = jax.vjp(lambda a, c, d, e: (_rms_norm(c, d), _rms_norm(a, e)), ckv, cq, qw, kw)
        dckv_p, dcq_p, dqw, dkw = vjp((dcq, dckv))
        dkr_p = _rope_bwd(dkr, cosv, sinv)
        dpm = jnp.concatenate([dckv_p, dkr_p, dba_v, jnp.zeros((ckv.shape[0], 2 * LANE), F32), dcq_p], axis=1)
        return dpm, dqw, dkw

    dpm, g['q_norm_w'], g['kv_norm_w'] = _rowwise(
        mla_pre_bwd,
        [(pm, KV_LORA, 0, False), (pm, Q_LORA, 2, False), (cosb, ROPE_PAD, 0, False), (sinb, ROPE_PAD, 0, False),
         (dc_q, Q_LORA, 0, False), (dckv_att, KV_LORA, 0, False), (dkr_att, ROPE_PAD, 0, False), (dba, LANE, 0, False)],
        [sp['q_norm_w'], sp['kv_norm_w']], [(1152, 1152, False, bf)], [(1, Q_LORA), (1, KV_LORA)], name="b_mla_pre")

    g['qkv'] = _mm2(xb, dqkv_pre, ta=True, name="g_qkv")
    g['z'] = _mm2(xb, dz, ta=True, name="g_z")
    g['gg'] = _mm2(xb, dgg, ta=True, name="g_gg")
    g['mla'] = _mm2(xb, dpm, ta=True, name="g_mla")
    dx, arrived = _dx_fused([(dqkv_pre, wt['qkv']), (dz, wt['z']), (dgg, wt['gg']), (dpm, wt['mla'])], dpre1, ALPHA,
                            rider=exch.in_send(g))
    exch.in_arrived(arrived)
    return loss_lanes, dx, g


_IN_SIZES = (QKV_W, HEADS * DN_DK, HEADS, HEADS, Q_LORA, KV_LORA, ROPE, D_MODEL, D_MODEL)


def _rope_tables(positions):
    inv_freq = ROPE_BASE ** (-jnp.arange(0, ROPE, 2, dtype=F32) / ROPE)
    ang = positions.astype(F32)[:, None] * inv_freq
    cos, sin = jnp.cos(ang), jnp.sin(ang)
    zeros = jnp.zeros((positions.shape[0], ROPE_PAD - ROPE), F32)
    return jnp.concatenate([cos, cos, zeros], axis=1), jnp.concatenate([-sin, sin, zeros], axis=1)


def _prep_w_in(w_in):
    dt = w_in.dtype
    offs = [0]
    for s in _IN_SIZES:
        offs.append(offs[-1] + s)
    qkv, z, wb, wa, cq, ckv, kr, gd, gm = [w_in[:, offs[i]:offs[i + 1]] for i in range(len(_IN_SIZES))]
    zc = lambda n: jnp.zeros((D_MODEL, n), dt)
    return {
        'qkv': qkv, 'z': z, 'gg': jnp.concatenate([gd, gm], axis=1),
        'mla': jnp.concatenate([ckv, kr, zc(ROPE_PAD - ROPE), wb, wa, zc(LANE - 2 * HEADS), zc(2 * LANE), cq], axis=1),
    }


def _prep_weights(full):
    w_uq = full['w_uq']
    wt = {
        'uq_nope': w_uq[:, :, :NOPE].reshape(Q_LORA, HEADS * NOPE),
        'uq_rope': jnp.pad(w_uq[:, :, NOPE:], ((0, 0), (0, 0), (0, ROPE_PAD - ROPE))).reshape(Q_LORA, HEADS * ROPE_PAD),
        'uk': jnp.transpose(full['w_uk'], (1, 0, 2)), 'uv': jnp.transpose(full['w_uv'], (1, 0, 2)),
        'br_dn': full['w_br_dn'], 'br_mla': full['w_br_mla'], 'o': full['w_o'], 'ffn_in': full['w_ffn_in'],
        'ffn_out': full['w_ffn_out'], 'ple': full['w_ple'], 'ple_gate': full['w_ple_gate'],
    }
    return wt


def _prep_small(small):
    pad = lambda v: jnp.pad(v, (0, LANE - v.shape[0]))[None, :]
    return {
        'conv_w': small['conv_w'], 'a_log': pad(small['dn_a_log']), 'dt_bias': pad(small['dn_dt_bias']),
        'dn_norm_w': small['dn_norm_w'][None, :], 'q_norm_w': small['q_norm_w'][None, :],
        'kv_norm_w': small['kv_norm_w'][None, :], 'ln1_g': small['ln1_g'][None, :], 'ln1_b': small['ln1_b'][None, :],
        'ln2_g': small['ln2_g'][None, :], 'ln2_b': small['ln2_b'][None, :],
    }


def _w_in_grad(g):
    mla = g['mla']
    ba0 = KV_LORA + ROPE_PAD
    cq0 = ba0 + 3 * LANE
    return jnp.concatenate([
        g['qkv'], g['z'], mla[:, ba0:ba0 + HEADS], mla[:, ba0 + HEADS:ba0 + 2 * HEADS], mla[:, cq0:cq0 + Q_LORA],
        mla[:, :KV_LORA], mla[:, KV_LORA:KV_LORA + ROPE], g['gg']], axis=1)


def _unprep_grads_late(g):
    return {
        'conv_w': g['conv_w'], 'dn_a_log': g['a_log'][0, :HEADS], 'dn_dt_bias': g['dt_bias'][0, :HEADS],
        'dn_norm_w': g['dn_norm_w'][0], 'q_norm_w': g['q_norm_w'][0], 'kv_norm_w': g['kv_norm_w'][0],
        'ln1_g': g['ln1_g'][0], 'ln1_b': g['ln1_b'][0], 'ln2_g': g['ln2_g'][0], 'ln2_b': g['ln2_b'][0],
    }


def _unprep_grads_early(g):
    w_uq = jnp.concatenate([g['uq_nope'].reshape(Q_LORA, HEADS, NOPE),
                            g['uq_rope'].reshape(Q_LORA, HEADS, ROPE_PAD)[:, :, :ROPE]], axis=2)
    return {
        'w_uq': w_uq, 'w_uk': jnp.transpose(g['uk'], (1, 0, 2)), 'w_uv': jnp.transpose(g['uv'], (1, 0, 2)),
        'w_br_dn': g['br_dn'], 'w_br_mla': g['br_mla'], 'w_o': g['o'],
        'w_ffn_in': g['ffn_in'], 'w_ffn_out': g['ffn_out'], 'w_ple': g['ple'], 'w_ple_gate': g['ple_gate'],
    }


_FLATB_PIECES = (
    ('w_ffn_out', 704, (704, D_MODEL)), ('w_br_dn', 256, (256, D_MODEL)), ('w_br_mla', 256, (256, D_MODEL)),
    ('w_o', 256, (256, D_MODEL)), ('w_ple_gate', 256, (256, D_MODEL)), ('w_uq', 144, (96, HEADS, NOPE + ROPE)),
    ('w_uk', 64, (64, HEADS, NOPE)), ('w_uv', 64, (64, HEADS, NOPE)), ('w_ple', 64, (PLE_DIM, 256)),
)
FLATB_ROWS = 2112
W_IN_SHARD = D_IN // N_SHARD
FFN_IN_SHARD = 2 * FFN_HIDDEN // N_SHARD
A_ROWS = D_MODEL + 32
_CONV_SHARD = QKV_W // N_SHARD
_ADD_TILES = (256, 256, 352)


def _flatb_offsets():
    offs, o = {}, 0
    for name, rows, _ in _FLATB_PIECES:
        offs[name] = o
        o += rows
    return offs, o


def _pack_shards(ws, conv_w):
    conv_bits = lax.bitcast_convert_type(conv_w, jnp.bfloat16).reshape(DN_CONV, 2 * _CONV_SHARD).astype(_CDT)
    tail = jnp.pad(conv_bits, ((0, A_ROWS - D_MODEL - DN_CONV), (0, W_IN_SHARD - 2 * _CONV_SHARD)))
    a_buf = jnp.concatenate([ws['w_in'].astype(_CDT), tail], axis=0)
    parts = [ws[name].astype(_CDT).reshape(rows, FLAT_W) for name, rows, _ in _FLATB_PIECES]
    used = sum(p.shape[0] for p in parts)
    parts.append(jnp.zeros((FLATB_ROWS - used, FLAT_W), _CDT))
    return [a_buf, ws['w_ffn_in'].astype(_CDT), jnp.concatenate(parts, axis=0)]


def _unpack_w_in(gathered, local, me):
    a = [jnp.where(me == s, local, gathered[s]) for s in range(N_SHARD)]
    conv = [lax.bitcast_convert_type(
        p[D_MODEL:D_MODEL + DN_CONV, :2 * _CONV_SHARD].astype(jnp.bfloat16).reshape(DN_CONV, _CONV_SHARD, 2), F32) for p in a]
    return jnp.concatenate([p[:D_MODEL] for p in a], axis=1), jnp.concatenate(conv, axis=1)


def _unpack_rest(gathered, local, me):
    pick = lambda b, s: jnp.where(me == s, local[b], gathered[b][s])
    full = {'w_ffn_in': jnp.concatenate([pick(0, s) for s in range(N_SHARD)], axis=1)}
    offs, _ = _flatb_offsets()
    fb = [pick(1, s) for s in range(N_SHARD)]
    for name, rows, shape in _FLATB_PIECES:
        pieces = [p[offs[name]:offs[name] + rows].reshape(shape) for p in fb]
        full[name] = jnp.concatenate(pieces, axis=1 if name == 'w_ple' else 0)
    return full


def _shard_columns(g, w):
    return jnp.stack([g[:, s * w:(s + 1) * w] for s in range(N_SHARD)])


def _pack_grads_rest(gw):
    parts = []
    for name, rows, _ in _FLATB_PIECES:
        g = gw[name]
        if name == 'w_ple':
            parts.append(_shard_columns(g, PLE_DIM).reshape(N_SHARD, rows, FLAT_W))
        else:
            parts.append(g.reshape(N_SHARD, rows, FLAT_W))
    used = sum(p.shape[1] for p in parts)
    parts.append(jnp.zeros((N_SHARD, FLATB_ROWS - used, FLAT_W), F32))
    return [_shard_columns(gw['w_ffn_in'], FFN_IN_SHARD), jnp.concatenate(parts, axis=1)]


def _unpack_reduced(mine, theirs, c):
    whole = [jnp.concatenate([jnp.where(c == 0, m, t), jnp.where(c == 0, t, m)], axis=0) for m, t in zip(mine, theirs)]
    out = {'w_in': whole[0], 'w_ffn_in': whole[1]}
    offs, _ = _flatb_offsets()
    for name, rows, shape in _FLATB_PIECES:
        out[name] = whole[2][offs[name]:offs[name] + rows].reshape(shape)
    return out


_HBM = pl.BlockSpec(memory_space=pltpu.HBM)


def _place():
    x, y, c = lax.axis_index("x"), lax.axis_index("y"), lax.axis_index("c")
    chips = [(1 - x, y), (x, 1 - y), (1 - x, 1 - y)]
    return x, y, c, chips


def _remote(src, dst, send_sems, recv_sems, k, to):
    return pltpu.make_async_remote_copy(src_ref=src, dst_ref=dst, send_sem=send_sems.at[k], recv_sem=recv_sems.at[k],
                                        device_id=to, device_id_type=_MESH)


def _half_rows(ref, half, hf, lead=None):
    rows = pl.ds(pl.multiple_of(hf * half, 16), half)
    return ref.at[rows, :] if lead is None else ref.at[lead, rows, :]


class _Rider:
    def __init__(self, inputs, out_shape, n_sems, copies, aliases=None):
        self.inputs, self.out_shape, self.n_sems, self.copies = list(inputs), list(out_shape), n_sems, copies
        self.aliases = aliases or {}


def _carried_call(body, rider, first, last, *, name, grid, in_specs, out_specs, out_shape, scratch_shapes, sem, args):
    n_in, n_out, n_scr = len(in_specs), len(out_specs), len(scratch_shapes)
    if rider is None:
        res = pl.pallas_call(body, name=name, grid=grid, in_specs=in_specs, out_specs=out_specs, out_shape=out_shape,
                             scratch_shapes=scratch_shapes, compiler_params=_cparams(sem))(*args)
        return list(res), []
    ri, ro = len(rider.inputs), len(rider.out_shape)

    def full_body(*refs):
        own_in, r_in = refs[:n_in], refs[n_in:n_in + ri]
        o0 = n_in + ri
        own_out, r_out = refs[o0:o0 + n_out], refs[o0 + n_out:o0 + n_out + ro]
        s0 = o0 + n_out + ro
        own_scr, send_sems, recv_sems = refs[s0:s0 + n_scr], refs[s0 + n_scr], refs[s0 + n_scr + 1]

        @pl.when(first())
        def _():
            sends, _ = rider.copies(r_in, r_out, send_sems, recv_sems)
            for cp in sends:
                cp.start()

        body(*own_in, *own_out, *own_scr)

        @pl.when(last())
        def _():
            sends, arrivals = rider.copies(r_in, r_out, send_sems, recv_sems)
            for cp in arrivals():
                cp.wait_recv()
            for cp in sends:
                cp.wait_send()

    res = pl.pallas_call(
        full_body, name=name, grid=grid, in_specs=list(in_specs) + [_HBM] * ri, out_specs=list(out_specs) + [_HBM] * ro,
        out_shape=list(out_shape) + rider.out_shape,
        scratch_shapes=list(scratch_shapes) + [pltpu.SemaphoreType.DMA((rider.n_sems,))] * 2,
        input_output_aliases={n_in + i: n_out + o for i, o in rider.aliases.items()},
        compiler_params=_cparams(sem))(*args, *rider.inputs)
    return list(res[:n_out]), list(res[n_out:])


def _ride_gather_send(bufs):
    n = len(bufs)
    halves = [b.shape[0] // 2 for b in bufs]

    def copies(ins, outs, send_sems, recv_sems):
        x, y, c, chips = _place()
        slot = lambda b, cx, cy: _half_rows(outs[b], halves[b], c, lead=2 * cx + cy)
        sends = [_remote(_half_rows(ins[b], halves[b], c), slot(b, x, y), send_sems, recv_sems, 3 * b + j, (cx, cy, c))
                 for b in range(n) for j, (cx, cy) in enumerate(chips)]
        arrivals = lambda: [_remote(slot(b, cx, cy), slot(b, cx, cy), send_sems, recv_sems, 3 * b + j, (x, y, c))
                            for b in range(n) for j, (cx, cy) in enumerate(chips)]
        return sends, arrivals

    return _Rider(bufs, [jax.ShapeDtypeStruct((N_SHARD,) + b.shape, b.dtype) for b in bufs], 3 * n, copies)


def _ride_gather_pass(gathered):
    n = len(gathered)
    halves = [g.shape[1] // 2 for g in gathered]

    def copies(ins, outs, send_sems, recv_sems):
        x, y, c, chips = _place()
        slot = lambda b, cx, cy, hf: _half_rows(outs[b], halves[b], hf, lead=2 * cx + cy)
        sends = [_remote(slot(b, cx, cy, c), slot(b, cx, cy, c), send_sems, recv_sems, 3 * b + j, (x, y, 1 - c))
                 for b in range(n) for j, (cx, cy) in enumerate(chips)]
        arrivals = lambda: [_remote(slot(b, cx, cy, 1 - c), slot(b, cx, cy, 1 - c), send_sems, recv_sems, 3 * b + j, (x, y, c))
                            for b in range(n) for j, (cx, cy) in enumerate(chips)]
        return sends, arrivals

    return _Rider(gathered, [jax.ShapeDtypeStruct(g.shape, g.dtype) for g in gathered], 3 * n, copies,
                  aliases={b: b for b in range(n)})


class _SemSlice:
    def __init__(self, sems, off):
        self.sems, self.off = sems, off

    @property
    def at(self):
        sems, off = self.sems, self.off

        class _At:
            def __getitem__(self, k):
                return sems.at[off + k]

        return _At()


def _join_riders(r1, r2):
    n_in, n_out = len(r1.inputs), len(r1.out_shape)

    def copies(ins, outs, send_sems, recv_sems):
        s1, a1 = r1.copies(ins[:n_in], outs[:n_out], send_sems, recv_sems)
        s2, a2 = r2.copies(ins[n_in:], outs[n_out:], _SemSlice(send_sems, r1.n_sems), _SemSlice(recv_sems, r1.n_sems))
        return s1 + s2, lambda: a1() + a2()

    aliases = dict(r1.aliases)
    aliases.update({n_in + i: n_out + o for i, o in r2.aliases.items()})
    return _Rider(r1.inputs + r2.inputs, r1.out_shape + r2.out_shape, r1.n_sems + r2.n_sems, copies, aliases)


def _ride_small_gather(buf):
    def copies(ins, outs, send_sems, recv_sems):
        x, y, c, _ = _place()
        flip = lambda v, d: 1 - v if d else v
        sends, peers = [], []
        for dx in (0, 1):
            for dy in (0, 1):
                for dc in (0, 1):
                    if dx or dy or dc:
                        k = 4 * dx + 2 * dy + dc - 1
                        px, py, pc = flip(x, dx), flip(y, dy), flip(c, dc)
                        sends.append(_remote(ins[0], outs[0].at[4 * x + 2 * y + c], send_sems, recv_sems, k, (px, py, pc)))
                        peers.append((k, 4 * px + 2 * py + pc))
        arrivals = lambda: [_remote(ins[0], outs[0].at[slot], send_sems, recv_sems, k, (x, y, c)) for k, slot in peers]
        return sends, arrivals

    return _Rider([buf], [jax.ShapeDtypeStruct((8,) + buf.shape, buf.dtype)], 7, copies)


def _small_sum(gathered, buf, me_arr):
    n, r, width = gathered.shape

    def body(me_ref, g_ref, b_ref, o_ref):
        total = jnp.zeros((r, width), F32)
        for d in range(n):
            total = total + jnp.where(me_ref[0] == d, b_ref[...], g_ref[d])
        o_ref[...] = total

    return pl.pallas_call(
        body, name="small_sum", out_shape=jax.ShapeDtypeStruct((r, width), F32),
        grid_spec=pltpu.PrefetchScalarGridSpec(
            num_scalar_prefetch=1, grid=(1,),
            in_specs=[pl.BlockSpec((n, r, width), lambda i, me: (0, 0, 0)), pl.BlockSpec((r, width), lambda i, me: (0, 0))],
            out_specs=pl.BlockSpec((r, width), lambda i, me: (0, 0))),
        compiler_params=_cparams(("arbitrary",)))(me_arr, gathered, buf)


def _ride_pair_exchange(gbufs):
    n = len(gbufs)
    halves = [g.shape[1] // 2 for g in gbufs]

    def copies(ins, outs, send_sems, recv_sems):
        x, y, c, _ = _place()
        sends = [_remote(ins[b].at[:, pl.ds(pl.multiple_of((1 - c) * halves[b], 16), halves[b]), :], outs[b],
                         send_sems, recv_sems, b, (x, y, 1 - c)) for b in range(n)]
        arrivals = lambda: [_remote(outs[b], outs[b], send_sems, recv_sems, b, (x, y, c)) for b in range(n)]
        return sends, arrivals

    return _Rider(gbufs, [jax.ShapeDtypeStruct((N_SHARD, h, g.shape[2]), g.dtype) for g, h in zip(gbufs, halves)], n, copies)


def _ride_chip_exchange(parts):
    n = len(parts)

    def copies(ins, outs, send_sems, recv_sems):
        x, y, c, chips = _place()
        sends = [_remote(ins[b].at[2 * cx + cy], outs[b].at[j], send_sems, recv_sems, 3 * b + j, (cx, cy, c))
                 for b in range(n) for j, (cx, cy) in enumerate(chips)]
        arrivals = lambda: [_remote(ins[b].at[0], outs[b].at[j], send_sems, recv_sems, 3 * b + j, (x, y, c))
                            for b in range(n) for j in range(len(chips))]
        return sends, arrivals

    return _Rider(parts, [jax.ShapeDtypeStruct((3,) + p.shape[1:], p.dtype) for p in parts], 3 * n, copies)


def _gather_shards(bufs, name):
    n = len(bufs)
    halves = [b.shape[0] // 2 for b in bufs]

    def body(*refs):
        ins, outs, send_sems, recv_sems = refs[:n], refs[n:2 * n], refs[2 * n], refs[2 * n + 1]
        x, y, c, chips = _place()
        me, sibling = (x, y, c), (x, y, 1 - c)
        slot = lambda b, cx, cy, hf: _half_rows(outs[b], halves[b], hf, lead=2 * cx + cy)
        first = [_remote(_half_rows(ins[b], halves[b], c), slot(b, x, y, c), send_sems, recv_sems, 6 * b + j, (cx, cy, c))
                 for b in range(n) for j, (cx, cy) in enumerate(chips)]
        for cp in first:
            cp.start()
        passed = []
        for j, (cx, cy) in enumerate(chips):
            for b in range(n):
                _remote(slot(b, cx, cy, c), slot(b, cx, cy, c), send_sems, recv_sems, 6 * b + j, me).wait_recv()
                fwd = _remote(slot(b, cx, cy, c), slot(b, cx, cy, c), send_sems, recv_sems, 6 * b + 3 + j, sibling)
                fwd.start()
                passed.append(fwd)
        for j, (cx, cy) in enumerate(chips):
            for b in range(n):
                _remote(slot(b, cx, cy, 1 - c), slot(b, cx, cy, 1 - c), send_sems, recv_sems, 6 * b + 3 + j, me).wait_recv()
        for cp in first + passed:
            cp.wait_send()

    return pl.pallas_call(
        body, name=name, out_shape=[jax.ShapeDtypeStruct((N_SHARD,) + b.shape, b.dtype) for b in bufs],
        in_specs=[_HBM] * n, out_specs=[_HBM] * n,
        scratch_shapes=[pltpu.SemaphoreType.DMA((6 * n,)), pltpu.SemaphoreType.DMA((6 * n,))],
    )(*bufs)


def _reduce_pair_exchange(gbufs, name):
    n = len(gbufs)
    halves = [g.shape[1] // 2 for g in gbufs]

    def body(*refs):
        ins, outs, send_sems, recv_sems = refs[:n], refs[n:2 * n], refs[2 * n], refs[2 * n + 1]
        x, y, c, _ = _place()
        cps = [_remote(ins[b].at[:, pl.ds(pl.multiple_of((1 - c) * halves[b], 16), halves[b]), :], outs[b],
                       send_sems, recv_sems, b, (x, y, 1 - c)) for b in range(n)]
        for cp in cps:
            cp.start()
        for cp in cps:
            cp.wait()

    return pl.pallas_call(
        body, name=name,
        out_shape=[jax.ShapeDtypeStruct((N_SHARD, h, g.shape[2]), g.dtype) for g, h in zip(gbufs, halves)],
        in_specs=[_HBM] * n, out_specs=[_HBM] * n,
        scratch_shapes=[pltpu.SemaphoreType.DMA((n,)), pltpu.SemaphoreType.DMA((n,))],
    )(*gbufs)


def _pair_add(gbuf, recv, c_arr, tr, name):
    _, rows, width = gbuf.shape
    half = rows // 2
    nt = half // tr

    def body(c_ref, a_ref, b_ref, o_ref):
        o_ref[...] = (a_ref[...] + b_ref[...]).astype(o_ref.dtype)

    blk = lambda f: pl.BlockSpec((None, tr, width), f)
    return pl.pallas_call(
        body, name=name, out_shape=jax.ShapeDtypeStruct((N_SHARD, half, width), jnp.bfloat16),
        grid_spec=pltpu.PrefetchScalarGridSpec(
            num_scalar_prefetch=1, grid=(N_SHARD, nt),
            in_specs=[blk(lambda s, i, c: (s, c[0] * nt + i, 0)), blk(lambda s, i, c: (s, i, 0))],
            out_specs=blk(lambda s, i, c: (s, i, 0))),
        compiler_params=_cparams(("parallel", "parallel")))(c_arr, gbuf, recv)


def _reduce_chip_exchange(parts, name):
    n = len(parts)

    def body(*refs):
        ins, outs, send_sems, recv_sems = refs[:n], refs[n:2 * n], refs[2 * n], refs[2 * n + 1]
        x, y, c, chips = _place()
        sends = [_remote(ins[b].at[2 * cx + cy], outs[b].at[j], send_sems, recv_sems, 3 * b + j, (cx, cy, c))
                 for b in range(n) for j, (cx, cy) in enumerate(chips)]
        for cp in sends:
            cp.start()
        for b in range(n):
            for j in range(len(chips)):
                _remote(ins[b].at[0], outs[b].at[j], send_sems, recv_sems, 3 * b + j, (x, y, c)).wait_recv()
        for cp in sends:
            cp.wait_send()

    return pl.pallas_call(
        body, name=name, out_shape=[jax.ShapeDtypeStruct((3,) + p.shape[1:], p.dtype) for p in parts],
        in_specs=[_HBM] * n, out_specs=[_HBM] * n,
        scratch_shapes=[pltpu.SemaphoreType.DMA((3 * n,)), pltpu.SemaphoreType.DMA((3 * n,))],
    )(*parts)


def _chip_add(part, recv, me_arr, tr, name):
    _, half, width = part.shape

    def body(me_ref, own, a0, a1, a2, o_ref):
        f = lambda r: r[...].astype(F32)
        o_ref[...] = ((f(own) + f(a0)) + f(a1)) + f(a2)

    specs = [pl.BlockSpec((None, tr, width), lambda i, me: (me[0], i, 0))]
    specs += [pl.BlockSpec((None, tr, width), functools.partial(lambda i, me, k: (k, i, 0), k=k)) for k in range(3)]
    return pl.pallas_call(
        body, name=name, out_shape=jax.ShapeDtypeStruct((half, width), F32),
        grid_spec=pltpu.PrefetchScalarGridSpec(
            num_scalar_prefetch=1, grid=(half // tr,), in_specs=specs,
            out_specs=pl.BlockSpec((tr, width), lambda i, me: (i, 0))),
        compiler_params=_cparams(("parallel",)))(me_arr, part, recv, recv, recv)


def _reduce_pair_share(rhalves, name):
    n = len(rhalves)

    def body(*refs):
        ins, outs, send_sems, recv_sems = refs[:n], refs[n:2 * n], refs[2 * n], refs[2 * n + 1]
        x, y, c, _ = _place()
        cps = [_remote(ins[b], outs[b], send_sems, recv_sems, b, (x, y, 1 - c)) for b in range(n)]
        for cp in cps:
            cp.start()
        for cp in cps:
            cp.wait()

    return pl.pallas_call(
        body, name=name, out_shape=[jax.ShapeDtypeStruct(r.shape, r.dtype) for r in rhalves],
        in_specs=[_HBM] * n, out_specs=[_HBM] * n,
        scratch_shapes=[pltpu.SemaphoreType.DMA((n,)), pltpu.SemaphoreType.DMA((n,))],
    )(*rhalves)


def _small_allreduce(buf):
    r, width = buf.shape
    n_dev = 8

    def body(x_ref, all_ref, sum_ref, send_sems, recv_sems, local_sem):
        x, y, c, chips = _place()
        me, sibling = (x, y, c), (x, y, 1 - c)

        def rows(px, py, pc):
            return all_ref.at[pl.ds(pl.multiple_of((4 * px + 2 * py + pc) * r, 8), r), :]

        def copy(k, block, to, src=None):
            return _remote(rows(*block) if src is None else src, rows(*block), send_sems, recv_sems, k, to)

        mine = pltpu.make_async_copy(x_ref, rows(*me), local_sem)
        mine.start()
        first = [copy(0, me, sibling, src=x_ref)]
        first += [copy(1 + j, me, (*chip, c), src=x_ref) for j, chip in enumerate(chips)]
        for cp in first:
            cp.start()
        passed = [copy(4 + j, (*chip, c), sibling) for j, chip in enumerate(chips)]
        for j, chip in enumerate(chips):
            copy(1 + j, (*chip, c), me).wait_recv()
            passed[j].start()
        copy(0, sibling, me).wait_recv()
        for j, chip in enumerate(chips):
            copy(4 + j, (*chip, 1 - c), me).wait_recv()
        for cp in first + passed:
            cp.wait_send()
        mine.wait()
        total = all_ref[0:r, :]
        for k in range(1, n_dev):
            total = total + all_ref[k * r:(k + 1) * r, :]
        sum_ref[...] = total

    vm = pl.BlockSpec(memory_space=pltpu.VMEM)
    _, total = pl.pallas_call(
        body, name="small_allreduce",
        out_shape=[jax.ShapeDtypeStruct((n_dev * r, width), buf.dtype), jax.ShapeDtypeStruct((r, width), buf.dtype)],
        in_specs=[vm], out_specs=[vm, vm],
        scratch_shapes=[pltpu.SemaphoreType.DMA((7,)), pltpu.SemaphoreType.DMA((7,)), pltpu.SemaphoreType.DMA],
    )(buf)
    return total


def _row_tile(rows, cap):
    if rows <= cap:
        return rows
    t = (cap // 8) * 8
    while t >= 8:
        if rows % t == 0:
            return t
        t -= 8
    return rows


def _adamw(w, g, m, v, name):
    shape = w.shape
    cols = shape[-1] if len(shape) <= 3 else shape[-2] * shape[-1]
    lead = len(shape) == 3
    w2, g2, m2, v2 = (a if lead else a.reshape(-1, cols) for a in (w, g, m, v))
    rows = shape[1] if lead else w2.shape[0]
    tr, tc = _row_tile(rows, 256), cols
    if tr == rows and rows > 256:
        tc = _tile(cols, 256)

    def body(w_ref, g_ref, m_ref, v_ref, d_ref, mo_ref, vo_ref):
        gv = g_ref[...]
        mn = ADAM_B1 * m_ref[...] + (1.0 - ADAM_B1) * gv
        vn = ADAM_B2 * v_ref[...] + (1.0 - ADAM_B2) * (gv * gv)
        m_hat = mn / (1.0 - ADAM_B1 ** ADAM_STEP)
        v_hat = vn / (1.0 - ADAM_B2 ** ADAM_STEP)
        d_ref[...] = -ADAM_LR * (m_hat / (jnp.sqrt(v_hat) + ADAM_EPS) + ADAM_WD * w_ref[...])
        mo_ref[...] = mn
        vo_ref[...] = vn

    blk = (pl.BlockSpec((None, tr, tc), lambda i, j: (0, i, j)) if lead else pl.BlockSpec((tr, tc), lambda i, j: (i, j)))
    outs = pl.pallas_call(
        body, name=name, grid=(rows // tr, cols // tc), in_specs=[blk] * 4, out_specs=[blk] * 3,
        out_shape=[jax.ShapeDtypeStruct(w2.shape, F32)] * 3,
        compiler_params=_cparams(("parallel", "parallel")))(w2, g2, m2, v2)
    return tuple(o.reshape(shape) for o in outs)


_WEIGHT_NAMES = ('w_in', 'conv_w', 'dn_a_log', 'dn_dt_bias', 'dn_norm_w', 'q_norm_w', 'w_uq', 'kv_norm_w', 'w_uk',
                 'w_uv', 'w_br_dn', 'w_br_mla', 'w_o', 'ln1_g', 'ln1_b', 'w_ffn_in', 'w_ffn_out', 'w_ple',
                 'w_ple_gate', 'ln2_g', 'ln2_b')
_SMALL_NAMES = ('ln1_g', 'ln1_b', 'ln2_g', 'ln2_b', 'q_norm_w', 'kv_norm_w', 'dn_norm_w', 'dn_a_log', 'dn_dt_bias')
_SMALL_GROUP = 8
_CONV_SMALL_ROW = len(_SMALL_NAMES) * _SMALL_GROUP
_CONV_SMALL_ROWS = DN_CONV * QKV_W // FLAT_W


def _pack_small(gw):
    rows = [jnp.pad(gw[n][None, :], ((0, _SMALL_GROUP - 1), (0, FLAT_W - gw[n].shape[0]))) for n in _SMALL_NAMES]
    rows.append(jnp.pad(gw['conv_w'].reshape(_CONV_SMALL_ROWS, FLAT_W), ((0, SMALL_ROWS - _CONV_SMALL_ROW - _CONV_SMALL_ROWS), (0, 0))))
    return jnp.concatenate(rows, axis=0)


class _Exchange:
    def __init__(self, local, me_chip, c_arr):
        self.local, self.me_chip, self.c_arr = local, me_chip, c_arr
        self.parts = self.arrived = None

    def gather_send(self):
        return _ride_gather_send(self.local)

    def gather_pass(self, sent):
        return _ride_gather_pass(sent)

    def weights(self, gathered):
        return _prep_weights(_unpack_rest(gathered, self.local, self.me_chip))

    def pair_send(self, g):
        self.gbufs = _pack_grads_rest(_unprep_grads_early(g))
        return _ride_pair_exchange(self.gbufs)

    def reduce_send(self, got):
        self.parts = [_pair_add(g_, r_, self.c_arr, tr, "pair_add_%d" % (i + 1))
                      for i, (g_, r_, tr) in enumerate(zip(self.gbufs, got, _ADD_TILES[1:]))]
        return _ride_chip_exchange(self.parts)

    def reduce_arrived(self, arrived):
        self.arrived = list(arrived)

    def in_send(self, g):
        g_in = [_shard_columns(_w_in_grad(g), W_IN_SHARD)]
        got = _reduce_pair_exchange(g_in, "reduce_pair_exchange_w_in")
        self.part_in = _pair_add(g_in[0], got[0], self.c_arr, _ADD_TILES[0], "pair_add_0")
        self.small = _pack_small(_unprep_grads_late(g))
        return _join_riders(_ride_chip_exchange([self.part_in]), _ride_small_gather(self.small))

    def in_arrived(self, arrived):
        self.arrived_in, self.small_gathered = list(arrived[:1]), arrived[1]


def kernel(x, p, positions, w_in, conv_w, dn_a_log, dn_dt_bias, dn_norm_w, q_norm_w, w_uq, kv_norm_w, w_uk, w_uv, w_br_dn, w_br_mla, w_o, ln1_g, ln1_b, w_ffn_in, w_ffn_out, w_ple, w_ple_gate, ln2_g, ln2_b, loss_target, m_w_in, m_conv_w, m_dn_a_log, m_dn_dt_bias, m_dn_norm_w, m_q_norm_w, m_w_uq, m_kv_norm_w, m_w_uk, m_w_uv, m_w_br_dn, m_w_br_mla, m_w_o, m_ln1_g, m_ln1_b, m_w_ffn_in, m_w_ffn_out, m_w_ple, m_w_ple_gate, m_ln2_g, m_ln2_b, v_w_in, v_conv_w, v_dn_a_log, v_dn_dt_bias, v_dn_norm_w, v_q_norm_w, v_w_uq, v_kv_norm_w, v_w_uk, v_w_uv, v_w_br_dn, v_w_br_mla, v_w_o, v_ln1_g, v_ln1_b, v_w_ffn_in, v_w_ffn_out, v_w_ple, v_w_ple_gate, v_ln2_g, v_ln2_b):
    ws = dict(w_in=w_in, conv_w=conv_w, dn_a_log=dn_a_log, dn_dt_bias=dn_dt_bias, dn_norm_w=dn_norm_w, q_norm_w=q_norm_w,
              w_uq=w_uq, kv_norm_w=kv_norm_w, w_uk=w_uk, w_uv=w_uv, w_br_dn=w_br_dn, w_br_mla=w_br_mla, w_o=w_o,
              ln1_g=ln1_g, ln1_b=ln1_b, w_ffn_in=w_ffn_in, w_ffn_out=w_ffn_out, w_ple=w_ple, w_ple_gate=w_ple_gate,
              ln2_g=ln2_g, ln2_b=ln2_b)
    ms = dict(w_in=m_w_in, conv_w=m_conv_w, dn_a_log=m_dn_a_log, dn_dt_bias=m_dn_dt_bias, dn_norm_w=m_dn_norm_w,
              q_norm_w=m_q_norm_w, w_uq=m_w_uq, kv_norm_w=m_kv_norm_w, w_uk=m_w_uk, w_uv=m_w_uv, w_br_dn=m_w_br_dn,
              w_br_mla=m_w_br_mla, w_o=m_w_o, ln1_g=m_ln1_g, ln1_b=m_ln1_b, w_ffn_in=m_w_ffn_in, w_ffn_out=m_w_ffn_out,
              w_ple=m_w_ple, w_ple_gate=m_w_ple_gate, ln2_g=m_ln2_g, ln2_b=m_ln2_b)
    vs = dict(w_in=v_w_in, conv_w=v_conv_w, dn_a_log=v_dn_a_log, dn_dt_bias=v_dn_dt_bias, dn_norm_w=v_dn_norm_w,
              q_norm_w=v_q_norm_w, w_uq=v_w_uq, kv_norm_w=v_kv_norm_w, w_uk=v_w_uk, w_uv=v_w_uv, w_br_dn=v_w_br_dn,
              w_br_mla=v_w_br_mla, w_o=v_w_o, ln1_g=v_ln1_g, ln1_b=v_ln1_b, w_ffn_in=v_w_ffn_in, w_ffn_out=v_w_ffn_out,
              w_ple=v_w_ple, w_ple_gate=v_w_ple_gate, ln2_g=v_ln2_g, ln2_b=v_ln2_b)
    mx, my, mc = lax.axis_index("x"), lax.axis_index("y"), lax.axis_index("c")

    me_chip = 2 * mx + my
    c_arr = jnp.reshape(mc, (1,)).astype(jnp.int32)
    me_arr = jnp.reshape(me_chip, (1,)).astype(jnp.int32)
    sharded = ('w_in', 'w_ffn_in') + tuple(name for name, _, _ in _FLATB_PIECES)

    local = _pack_shards({name: ws[name][0] for name in sharded}, conv_w[0])
    (gathered_in,) = _gather_shards(local[:1], "gather_w_in")
    w_in_full, conv_full = _unpack_w_in(gathered_in, local[0], me_chip)
    small = {n: ws[n][0] for n in _SMALL_NAMES}
    small['conv_w'] = conv_full
    sp = _prep_small(small)
    cosb, sinb = _rope_tables(positions[0])
    exch = _Exchange(local[1:], me_chip, c_arr)

    loss_lanes, dx, g = _local_step(x[0], p[0, 0], cosb, sinb, loss_target[0], _prep_w_in(w_in_full), sp, exch)
    loss = lax.psum(jnp.sum(loss_lanes), ("x", "y", "c"))

    parts = [exch.part_in] + exch.parts
    arrived = exch.arrived_in + exch.arrived
    mine = [_chip_add(p_, r_, me_arr, tr, "chip_add_%d" % i) for i, (p_, r_, tr) in enumerate(zip(parts, arrived, _ADD_TILES))]
    reduced = _unpack_reduced(mine, _reduce_pair_share(mine, "reduce_pair_share"), mc)
    tot = _small_sum(exch.small_gathered, exch.small, jnp.reshape(4 * mx + 2 * my + mc, (1,)).astype(jnp.int32))
    gred = {name: reduced[name][None] for name in sharded}
    for i, n in enumerate(_SMALL_NAMES):
        gred[n] = tot[i * _SMALL_GROUP, :ws[n].shape[1]][None]
    conv_tot = tot[_CONV_SMALL_ROW:_CONV_SMALL_ROW + _CONV_SMALL_ROWS].reshape(DN_CONV, QKV_W)
    gred['conv_w'] = lax.dynamic_slice_in_dim(conv_tot, (2 * mx + my) * _CONV_SHARD, _CONV_SHARD, axis=1)[None]

    deltas, new_m, new_v = {}, {}, {}
    for n in _WEIGHT_NAMES:
        if n == 'w_in':
            tr_ = lambda a: jnp.transpose(a, (0, 2, 1))
            g_t = tr_(gred[n].reshape(ws[n].shape))
            outs = _adamw(tr_(ws[n]), g_t, tr_(ms[n]), tr_(vs[n]), "adamw_" + n)
            gred[n] = tr_(g_t)
            deltas[n], new_m[n], new_v[n] = (tr_(o) for o in outs)
            continue
        gred[n] = gred[n].reshape(ws[n].shape)
        deltas[n], new_m[n], new_v[n] = _adamw(ws[n], gred[n], ms[n], vs[n], "adamw_" + n)
    return (loss, dx[None], *[gred[n] for n in _WEIGHT_NAMES], *[deltas[n] for n in _WEIGHT_NAMES],
            *[new_m[n] for n in _WEIGHT_NAMES], *[new_v[n] for n in _WEIGHT_NAMES])
```

```python
import functools
import math

import jax
import jax.numpy as jnp
from jax import lax
from jax.experimental import pallas as pl
from jax.experimental.pallas import tpu as pltpu

F32 = jnp.float32
_CDT = jnp.bfloat16
_HI = lax.Precision.HIGHEST
_MESH = pl.DeviceIdType.MESH

D_MODEL = 1024
PLE_DIM = 256
HEADS = 8
DN_DK = 128
DN_CHUNK = 64
DN_CONV = 4
QKV_W = 3 * HEADS * DN_DK
Q_LORA = 384
KV_LORA = 256
NOPE = 128
ROPE = 64
ROPE_PAD = 128
FFN_HIDDEN = 2816
D_IN = 6864
ROPE_BASE = 10000.0
ALPHA = 2.0 ** 0.25
ATT_SCALE = (NOPE + ROPE) ** -0.5
NEG_BIG = -1e30
ADAM_LR, ADAM_B1, ADAM_B2, ADAM_EPS, ADAM_WD, ADAM_STEP = 0.001, 0.9, 0.999, 1e-08, 0.01, 10

LANE = 128
VMEM_LIMIT = 56 * 1024 * 1024
MM_VMEM_BUDGET = 40 * 1024 * 1024
N_SHARD = 4
FLAT_W = 1024
SMALL_ROWS = 88


def _tile(dim, cap):
    if dim <= cap:
        return dim
    t = (cap // LANE) * LANE
    while t >= LANE:
        if dim % t == 0:
            return t
        t -= LANE
    return dim


def _cparams(sem):
    return pltpu.CompilerParams(dimension_semantics=sem, vmem_limit_bytes=VMEM_LIMIT)


def _mm(a, b, *, name, ta=False, tb=False, add=None, add_scale=1.0, out_dtype=F32, heads=None,
        a_head=None, b_head=None, out_head=None, dims=None, tm=1408, tn=1408, rider=None):
    m, n, k = dims
    tm, tn = _tile(m, tm), _tile(n, tn)
    sa, sb, so = a.dtype.itemsize, b.dtype.itemsize, jnp.dtype(out_dtype).itemsize

    def vmem_need(tk_):
        acc = tm * tn * 4 if tk_ < k else 0
        extra = 2 * tm * tn * 4 if add is not None else 0
        return 2 * (tm * tk_ * sa + tk_ * tn * sb) + 2 * tm * tn * so + acc + extra

    tk = k
    while vmem_need(tk) > MM_VMEM_BUDGET and tk > LANE:
        smaller = _tile(k, tk - LANE)
        if smaller >= tk:
            break
        tk = smaller
    nk = k // tk
    hgrid = () if heads is None else (heads,)
    off = len(hgrid)

    def spec(rows, cols, rtile, ctile, rsel, csel, layout):
        def idx(*g):
            h = g[0] if off else 0
            ri, ci = g[off + rsel], g[off + csel]
            if layout == 'lead':
                return (h, ri, ci)
            if layout == 'col':
                return (ri, h * (cols // ctile) + ci)
            return (ri, ci)
        if layout == 'lead':
            return pl.BlockSpec((None, rtile, ctile), idx)
        return pl.BlockSpec((rtile, ctile), idx)

    a_spec = spec(k, m, tk, tm, 2, 0, a_head) if ta else spec(m, k, tm, tk, 0, 2, a_head)
    b_spec = spec(n, k, tn, tk, 1, 2, b_head) if tb else spec(k, n, tk, tn, 2, 1, b_head)
    o_spec = spec(m, n, tm, tn, 0, 1, out_head)
    in_specs = [a_spec, b_spec]
    args = [a, b]
    if add is not None:
        in_specs.append(spec(m, n, tm, tn, 0, 1, out_head))
        args.append(add)
    dn = (((0 if ta else 1,), (1 if tb else 0,)), ((), ()))

    def body(*refs):
        a_ref, b_ref = refs[0], refs[1]
        prod = lax.dot_general(a_ref[...].astype(_CDT), b_ref[...].astype(_CDT), dn, preferred_element_type=F32)
        if nk == 1:
            o_ref = refs[-1]
            if add is not None:
                prod = prod + refs[2][...].astype(F32) * add_scale
            o_ref[...] = prod.astype(out_dtype)
            return
        o_ref, acc_ref = refs[-2], refs[-1]
        kk = pl.program_id(off + 2)

        @pl.when(kk == 0)
        def _():
            if add is not None:
                acc_ref[...] = refs[2][...].astype(F32) * add_scale
            else:
                acc_ref[...] = jnp.zeros_like(acc_ref)

        acc_ref[...] += prod

        @pl.when(kk == nk - 1)
        def _():
            o_ref[...] = acc_ref[...].astype(out_dtype)

    if out_head == 'lead':
        oshape = (heads, m, n)
    elif out_head == 'col':
        oshape = (m, heads * n)
    else:
        oshape = (m, n)
    grid = hgrid + (m // tm, n // tn, nk)
    scratch = [pltpu.VMEM((tm, tn), F32)] if nk > 1 else []
    if rider is not None:
        (out,), carried = _carried_call(
            body, rider, *_grid_ends(grid), name=name, grid=grid, in_specs=in_specs, out_specs=[o_spec],
            out_shape=[jax.ShapeDtypeStruct(oshape, out_dtype)], scratch_shapes=scratch,
            sem=("arbitrary",) * len(grid), args=tuple(args))
        return out, carried
    sem = ("parallel",) * (off + 2) + ("arbitrary",)
    return pl.pallas_call(
        body, name=name, grid=grid, in_specs=in_specs, out_specs=o_spec,
        out_shape=jax.ShapeDtypeStruct(oshape, out_dtype), scratch_shapes=scratch,
        compiler_params=_cparams(sem))(*args)


def _mm2(a, b, **kw):
    ta, tb = kw.get('ta', False), kw.get('tb', False)
    m = a.shape[1] if ta else a.shape[0]
    k = a.shape[0] if ta else a.shape[1]
    n = b.shape[0] if tb else b.shape[1]
    return _mm(a, b, dims=(m, n, k), **kw)


def _rowwise(fn, rows, bcast, outs, reds=(), *, name, tm=256, heads=None):
    t = rows[0][0].shape[0]
    tm = min(tm, t)
    hn = 1 if heads is None else heads
    in_specs, args = [], []
    for arr, width, base, per_head in rows:
        in_specs.append(pl.BlockSpec((tm, width), functools.partial(
            lambda i, h, base, per_head: (i, base + (h if per_head else 0)), base=base, per_head=per_head)))
        args.append(arr)
    for arr in bcast:
        in_specs.append(pl.BlockSpec(arr.shape, lambda i, h: (0, 0)))
        args.append(arr)
    out_specs, out_shape = [], []
    for total, width, per_head, dt in outs:
        out_specs.append(pl.BlockSpec((tm, width), functools.partial(
            lambda i, h, per_head: (i, h if per_head else 0), per_head=per_head)))
        out_shape.append(jax.ShapeDtypeStruct((t, total), dt))
    for shp in reds:
        out_specs.append(pl.BlockSpec(shp, lambda i, h: (0, 0)))
        out_shape.append(jax.ShapeDtypeStruct(shp, F32))
    n_in, n_out, n_red = len(args), len(outs), len(reds)

    def body(*refs):
        i, h = pl.program_id(0), pl.program_id(1)
        vals = fn(h, *[r[...] for r in refs[:n_in]])
        for r, v in zip(refs[n_in:n_in + n_out], vals[:n_out]):
            r[...] = v.astype(r.dtype)
        if n_red:
            @pl.when((i == 0) & (h == 0))
            def _():
                for r in refs[n_in + n_out:]:
                    r[...] = jnp.zeros_like(r)
            for r, v in zip(refs[n_in + n_out:], vals[n_out:]):
                r[...] += v

    sem = ("arbitrary", "arbitrary") if n_red else ("parallel", "parallel")
    res = pl.pallas_call(body, name=name, grid=(t // tm, hn), in_specs=in_specs, out_specs=out_specs,
                         out_shape=out_shape, compiler_params=_cparams(sem))(*args)
    return tuple(res)


def _sigmoid(x):
    return 1.0 / (1.0 + jnp.exp(-x))


def _silu(x):
    return x * _sigmoid(x)


def _softplus(x):
    return jnp.maximum(x, 0.0) + jnp.log(1.0 + jnp.exp(-jnp.abs(x)))


def _layer_norm(t, g, b):
    mu = jnp.mean(t, axis=-1, keepdims=True)
    d = t - mu
    var = jnp.mean(d * d, axis=-1, keepdims=True)
    return d * lax.rsqrt(var + 1e-5) * g + b


def _rms_norm(t, w):
    return t * lax.rsqrt(jnp.mean(t * t, axis=-1, keepdims=True) + 1e-6) * w


def _swap_rope_halves(t):
    lane = lax.broadcasted_iota(jnp.int32, t.shape, 1) % ROPE_PAD
    n = t.shape[1]
    up = pltpu.roll(t, n - ROPE // 2, axis=1)
    dn = pltpu.roll(t, ROPE // 2, axis=1)
    return jnp.where(lane < ROPE // 2, up, jnp.where(lane < ROPE, dn, 0.0))


def _rope(t, cosb, sinb):
    reps = t.shape[1] // ROPE_PAD
    c = jnp.tile(cosb, (1, reps)) if reps > 1 else cosb
    s = jnp.tile(sinb, (1, reps)) if reps > 1 else sinb
    return t * c + _swap_rope_halves(t) * s


def _rope_bwd(d, cosb, sinb):
    reps = d.shape[1] // ROPE_PAD
    c = jnp.tile(cosb, (1, reps)) if reps > 1 else cosb
    s = jnp.tile(sinb, (1, reps)) if reps > 1 else sinb
    return d * c + _swap_rope_halves(d * s)


_CONV_ROWS = 256
_CONV_COLS = 256


def _conv_window(ref, r0, lo, hi, t):
    parts = []
    start, stop = r0 - lo, r0 + _CONV_ROWS + hi
    if start < 0:
        parts.append(jnp.zeros((-start, ref.shape[1]), F32))
        start = 0
    tail = max(stop - t, 0)
    parts.append(ref[start:stop - tail, :].astype(F32))
    if tail:
        parts.append(jnp.zeros((tail, ref.shape[1]), F32))
    return parts[0] if len(parts) == 1 else jnp.concatenate(parts, axis=0)


def _conv_taps(win, w_ref, n_out):
    acc = win[8:8 + n_out] * w_ref[DN_CONV - 1:DN_CONV, :]
    for i in range(DN_CONV - 1):
        acc = acc + pltpu.roll(win, DN_CONV - 1 - i, axis=0)[8:8 + n_out] * w_ref[i:i + 1, :]
    return acc


def _conv_silu(x, w):
    t, ch = x.shape

    def body(x_ref, w_ref, o_ref):
        for r in range(t // _CONV_ROWS):
            r0 = r * _CONV_ROWS
            c = _conv_taps(_conv_window(x_ref, r0, 8, 0, t), w_ref, _CONV_ROWS)
            o_ref[r0:r0 + _CONV_ROWS, :] = _silu(c)

    return pl.pallas_call(
        body, name="conv_silu", grid=(ch // _CONV_COLS,),
        in_specs=[pl.BlockSpec((t, _CONV_COLS), lambda j: (0, j)), pl.BlockSpec((DN_CONV, _CONV_COLS), lambda j: (0, j))],
        out_specs=pl.BlockSpec((t, _CONV_COLS), lambda j: (0, j)),
        out_shape=jax.ShapeDtypeStruct((t, ch), F32), compiler_params=_cparams(("parallel",)))(x, w)


def _conv_silu_bwd(x, w, dys):
    t, ch = x.shape
    per = ch // len(dys) // _CONV_COLS

    def body(x_ref, w_ref, *rest):
        dy_refs, (dx_ref, dw_ref) = rest[:len(dys)], rest[len(dys):]
        sec = pl.program_id(0) // per
        dws = [jnp.zeros((1, _CONV_COLS), F32) for _ in range(DN_CONV)]
        for r in range(t // _CONV_ROWS):
            r0 = r * _CONV_ROWS
            n_ext = _CONV_ROWS + 8
            xw = _conv_window(x_ref, r0, 8, 8, t)
            c = _conv_taps(xw, w_ref, n_ext)
            sg = _sigmoid(c)
            dy = _conv_window(dy_refs[-1], r0, 0, 8, t)
            for k in range(len(dys) - 2, -1, -1):
                dy = jnp.where(sec == k, _conv_window(dy_refs[k], r0, 0, 8, t), dy)
            ds = dy * (sg * (1.0 + c * (1.0 - sg)))
            x0 = xw[8:8 + _CONV_ROWS]
            dx = jnp.zeros((_CONV_ROWS, _CONV_COLS), F32)
            for i in range(DN_CONV):
                sh = DN_CONV - 1 - i
                ds_up = (ds if sh == 0 else pltpu.roll(ds, n_ext - sh, axis=0))[:_CONV_ROWS]
                dx = dx + ds_up * w_ref[i:i + 1, :]
                dws[i] = dws[i] + jnp.sum(x0 * ds_up, axis=0, keepdims=True)
            dx_ref[r0:r0 + _CONV_ROWS, :] = dx.astype(dx_ref.dtype)
        for i in range(DN_CONV):
            dw_ref[i:i + 1, :] = dws[i]

    blk = pl.BlockSpec((t, _CONV_COLS), lambda j: (0, j))
    wblk = pl.BlockSpec((DN_CONV, _CONV_COLS), lambda j: (0, j))
    dy_specs = [pl.BlockSpec((t, _CONV_COLS), functools.partial(lambda j, k: (0, jnp.clip(j - k * per, 0, per - 1)), k=k))
                for k in range(len(dys))]
    return pl.pallas_call(
        body, name="conv_silu_bwd", grid=(ch // _CONV_COLS,), in_specs=[blk, wblk] + dy_specs, out_specs=[blk, wblk],
        out_shape=[jax.ShapeDtypeStruct((t, ch), _CDT), jax.ShapeDtypeStruct((DN_CONV, ch), F32)],
        compiler_params=_cparams(("arbitrary",)))(x, w, *dys)


_PA_ROWS = 512


def _bmm(a, b, spec, exact=False):
    if exact:
        return jnp.einsum(spec, a, b, precision=_HI, preferred_element_type=F32)
    return jnp.einsum(spec, a.astype(_CDT), b.astype(_CDT), preferred_element_type=F32)


def _split16(a):
    hi = a.astype(jnp.bfloat16)
    return hi, (a - hi.astype(F32)).astype(jnp.bfloat16)


def _bmm3(a, b, spec):
    ah, al = _split16(a)
    bh, bl = _split16(b)
    e = lambda p, q: jnp.einsum(spec, p, q, preferred_element_type=F32)
    return e(ah, bh) + (e(ah, bl) + e(al, bh))


def _split3(b):
    b0 = b.astype(jnp.bfloat16)
    r1 = b - b0.astype(F32)
    b1 = r1.astype(jnp.bfloat16)
    return b0, b1, (r1 - b1.astype(F32)).astype(jnp.bfloat16)


@functools.partial(jax.custom_vjp, nondiff_argnums=(2, 3))
def _select_mm(sel, b, spec, spec_t):
    return sum(jnp.einsum(spec, sel, t, preferred_element_type=F32) for t in _split3(b))


def _select_mm_fwd(sel, b, spec, spec_t):
    return _select_mm(sel, b, spec, spec_t), sel


def _select_mm_bwd(spec, spec_t, sel, ct):
    return jnp.zeros_like(sel), sum(jnp.einsum(spec_t, sel, t, preferred_element_type=F32) for t in _split3(ct))


_select_mm.defvjp(_select_mm_fwd, _select_mm_bwd)


def _tri_inverse(l_mat, eye):
    pw = -l_mat
    t_inv = eye + pw
    for _ in range(5):
        pw = _bmm3(pw, pw, 'bij,bjk->bik')
        t_inv = t_inv + _bmm3(t_inv, pw, 'bij,bjk->bik')
    return t_inv


@jax.custom_vjp
def _tri_inverse_saved(l_mat, t_saved):
    return t_saved


def _tri_inverse_saved_fwd(l_mat, t_saved):
    return t_saved, t_saved


def _tri_inverse_saved_bwd(t_saved, dt):
    left = _bmm3(t_saved, dt, 'bji,bjk->bik')
    return -_bmm3(left, t_saved, 'bij,bkj->bik'), jnp.zeros_like(t_saved)


_tri_inverse_saved.defvjp(_tri_inverse_saved_fwd, _tri_inverse_saved_bwd)


def _phase_a(h, q, k, v, ba, alog, dtb, t_saved=None):
    r = q.shape[0]
    nb = r // DN_CHUNK
    c = DN_CHUNK
    lane = lax.broadcasted_iota(jnp.int32, (1, LANE), 1)
    selb = (lane == h).astype(F32)
    sela = (lane == h + HEADS).astype(F32)
    b_raw = jnp.sum(ba * selb, axis=1, keepdims=True)
    a_raw = jnp.sum(ba * sela, axis=1, keepdims=True)
    al = jnp.sum(alog * selb, axis=1, keepdims=True)
    dt = jnp.sum(dtb * selb, axis=1, keepdims=True)
    beta = jnp.broadcast_to(_sigmoid(b_raw), (r, LANE))
    g = jnp.broadcast_to(-jnp.exp(al) * _softplus(a_raw + dt), (r, LANE))
    qn = q * lax.rsqrt(jnp.sum(q * q, -1, keepdims=True) + 1e-6) * (DN_DK ** -0.5)
    kn = k * lax.rsqrt(jnp.sum(k * k, -1, keepdims=True) + 1e-6)
    q3, k3, v3 = qn.reshape(nb, c, LANE), kn.reshape(nb, c, LANE), v.reshape(nb, c, LANE)
    b3, g3 = beta.reshape(nb, c, LANE), g.reshape(nb, c, LANE)
    ri = lax.broadcasted_iota(jnp.int32, (nb, c, c), 1)
    ci = lax.broadcasted_iota(jnp.int32, (nb, c, c), 2)
    tril, strict = ri >= ci, ri > ci
    gc = _select_mm(tril.astype(jnp.bfloat16), g3, 'bij,bjd->bid', 'bij,bid->bjd')
    onehot = (lax.broadcasted_iota(jnp.int32, (nb, c, LANE), 2) == 0).astype(jnp.bfloat16)
    g_row = _select_mm(onehot, gc, 'bid,bjd->bij', 'bid,bij->bjd')
    diff = gc[:, :, :c] - g_row
    decay = jnp.where(tril, jnp.exp(jnp.where(tril, diff, 0.0)), 0.0)
    kb = k3 * b3
    l_mat = jnp.where(strict, _bmm(kb, k3, 'bid,bjd->bij') * decay, 0.0)
    if t_saved is None:
        t_inv = _tri_inverse(l_mat, (ri == ci).astype(F32))
    else:
        t_inv = _tri_inverse_saved(l_mat, t_saved.reshape(nb, c, c))
    eg = jnp.exp(gc)
    u = _bmm(t_inv, v3 * b3, 'bij,bje->bie')
    w = _bmm(t_inv, kb * eg, 'bij,bje->bie')
    intra = jnp.where(tril, _bmm(q3, k3, 'bid,bjd->bij') * decay, 0.0)
    qd = q3 * eg
    gl = jnp.sum(g3, axis=1, keepdims=True)
    kt = k3 * jnp.exp(gl - gc)
    outs = (u.reshape(r, LANE), w.reshape(r, LANE), qd.reshape(r, LANE), kt.reshape(r, LANE),
            intra.reshape(r, c), gl.reshape(nb, LANE))
    if t_saved is not None:
        return outs
    qd2 = qd - _bmm(intra, w, 'bij,bjd->bid')
    au = _bmm(intra, u, 'bij,bje->bie')
    return outs + (t_inv.reshape(r, c), qd2.reshape(r, LANE), au.reshape(r, LANE))


def _pa_specs(t):
    rr = min(_PA_ROWS, t)
    nb = rr // DN_CHUNK
    qkv = [pl.BlockSpec((rr, LANE), functools.partial(lambda i, h, o: (i, o + h), o=o)) for o in (0, HEADS, 2 * HEADS)]
    ba = pl.BlockSpec((rr, LANE), lambda i, h: (i, 3))
    vec = pl.BlockSpec((1, LANE), lambda i, h: (0, 0))
    row = pl.BlockSpec((rr, LANE), lambda i, h: (i, h))
    intra = pl.BlockSpec((None, rr, DN_CHUNK), lambda i, h: (h, i, 0))
    gl = pl.BlockSpec((nb, LANE), lambda i, h: (i, h))
    return rr, qkv, ba, vec, row, intra, gl


def _grid_ends(grid):
    first = lambda: functools.reduce(jnp.logical_and, [pl.program_id(a) == 0 for a in range(len(grid))])
    last = lambda: functools.reduce(jnp.logical_and, [pl.program_id(a) == n - 1 for a, n in enumerate(grid)])
    return first, last


def _delta_local(qkv_act, pm, alog, dtb, rider=None):
    t = qkv_act.shape[0]
    rr, qkv, ba, vec, row, intra, gl = _pa_specs(t)

    def body(q, k, v, b, al, dt, *outs):
        vals = _phase_a(pl.program_id(1), q[...], k[...], v[...], b[...], al[...], dt[...])
        for o, val in zip(outs, vals):
            o[...] = val

    wide = jax.ShapeDtypeStruct((t, HEADS * LANE), F32)
    sq = jax.ShapeDtypeStruct((HEADS, t, DN_CHUNK), F32)
    grid = (t // rr, HEADS)
    return _carried_call(
        body, rider, *_grid_ends(grid), name="delta_local", grid=grid, in_specs=qkv + [ba, vec, vec],
        out_specs=[row] * 4 + [intra, gl, intra, row, row],
        out_shape=[wide] * 4 + [sq, jax.ShapeDtypeStruct((t // DN_CHUNK, HEADS * LANE), F32), sq, wide, wide],
        scratch_shapes=[], sem=("arbitrary", "arbitrary"), args=(qkv_act, qkv_act, qkv_act, pm, alog, dtb))


def _delta_local_bwd(qkv_act, pm, alog, dtb, t_inv, du, dw, dqd, dkt, dintra, dgl, rider=None):
    t = qkv_act.shape[0]
    rr, qkv, ba, vec, row, intra, gl = _pa_specs(t)

    def body(q, k, v, b, al, dt, ti, du_r, dw_r, dqd_r, dkt_r, di_r, dgl_r, dq_o, dk_o, dv_o, dba_o, dal_o, ddt_o):
        i, h = pl.program_id(0), pl.program_id(1)
        t_saved = ti[...]
        _, vjp = jax.vjp(lambda *a: _phase_a(h, *a, t_saved=t_saved), q[...], k[...], v[...], b[...], al[...], dt[...])
        dq, dk, dv, dba, dal, ddt = vjp((du_r[...], dw_r[...], dqd_r[...], dkt_r[...], di_r[...], dgl_r[...]))
        dq_o[...], dk_o[...], dv_o[...] = dq, dk, dv

        @pl.when(h == 0)
        def _():
            dba_o[...] = jnp.zeros_like(dba_o)

        @pl.when((h == 0) & (i == 0))
        def _():
            dal_o[...] = jnp.zeros_like(dal_o)
            ddt_o[...] = jnp.zeros_like(ddt_o)

        dba_o[...] += dba
        dal_o[...] += dal
        ddt_o[...] += ddt

    wide = jax.ShapeDtypeStruct((t, HEADS * LANE), F32)
    vshape = jax.ShapeDtypeStruct((1, LANE), F32)
    grid = (t // rr, HEADS)
    return _carried_call(
        body, rider, *_grid_ends(grid), name="delta_local_bwd", grid=grid,
        in_specs=qkv + [ba, vec, vec, intra] + [row] * 4 + [intra, gl],
        out_specs=[row] * 3 + [pl.BlockSpec((rr, LANE), lambda i, h: (i, 0)), vec, vec],
        out_shape=[wide] * 3 + [jax.ShapeDtypeStruct((t, LANE), F32), vshape, vshape],
        scratch_shapes=[], sem=("arbitrary", "arbitrary"),
        args=(qkv_act, qkv_act, qkv_act, pm, alog, dtb, t_inv, du, dw, dqd, dkt, dintra, dgl))


_SCAN_ROWS = 512


def _dot(a, b, dn):
    return lax.dot_general(a.astype(_CDT), b.astype(_CDT), (dn, ((), ())), preferred_element_type=F32)


_NN = ((1,), (0,))
_NT = ((1,), (1,))
_TN = ((0,), (0,))


def _delta_scan(u, w, qd, kt, au, gl, rider=None):
    t = u.shape[0]
    rr = min(_SCAN_ROWS, t)
    nc = rr // DN_CHUNK

    def body(u_ref, w_ref, qd_ref, kt_ref, au_ref, gl_ref, o_ref, sall_ref, s_scr):
        @pl.when(pl.program_id(0) == 0)
        def _():
            s_scr[...] = jnp.zeros_like(s_scr)

        def chunk(c, carry):
            r0 = pl.multiple_of(c * DN_CHUNK, DN_CHUNK)
            rows = pl.ds(r0, DN_CHUNK)
            e = jnp.exp(gl_ref[pl.ds(c, 1), :])
            states = [s_scr[h] for h in range(HEADS)]
            u_c, w_c, qd_c, kt_c, au_c = u_ref[rows, :], w_ref[rows, :], qd_ref[rows, :], kt_ref[rows, :], au_ref[rows, :]
            o_new, s_new = [], []
            for h in range(HEADS):
                cs = slice(h * LANE, (h + 1) * LANE)
                s = states[h]
                both = _dot(jnp.concatenate([w_c[:, cs], qd_c[:, cs]], axis=0), s, _NN)
                v_new = u_c[:, cs] - both[:DN_CHUNK]
                o_new.append(both[DN_CHUNK:] + au_c[:, cs])
                s_new.append(s * e[:, cs] + _dot(kt_c[:, cs], v_new, _TN))
            o_ref[rows, :] = jnp.concatenate(o_new, axis=1)
            for h in range(HEADS):
                sall_ref[c, h] = states[h]
                s_scr[h] = s_new[h]
            return carry

        lax.fori_loop(0, nc, chunk, 0)

    row = pl.BlockSpec((rr, HEADS * LANE), lambda i: (i, 0))
    grid = (t // rr,)
    return _carried_call(
        body, rider, *_grid_ends(grid), name="delta_scan", grid=grid,
        in_specs=[row] * 5 + [pl.BlockSpec((nc, HEADS * LANE), lambda i: (i, 0))],
        out_specs=[row, pl.BlockSpec((nc, HEADS, LANE, LANE), lambda i: (i, 0, 0, 0))],
        out_shape=[jax.ShapeDtypeStruct((t, HEADS * LANE), F32),
                   jax.ShapeDtypeStruct((t // DN_CHUNK, HEADS, LANE, LANE), F32)],
        scratch_shapes=[pltpu.VMEM((HEADS, LANE, LANE), F32)], sem=("arbitrary",), args=(u, w, qd, kt, au, gl))


def _delta_scan_bwd(u, w, qd, kt, intra, gl, sall, do, rider=None):
    t = u.shape[0]
    rr = min(_SCAN_ROWS, t)
    nc = rr // DN_CHUNK
    ng = t // rr

    def body(u_ref, w_ref, qd_ref, kt_ref, a_ref, gl_ref, sall_ref, do_ref,
             du_ref, dw_ref, dqd_ref, dkt_ref, da_ref, dgl_ref, ds_scr):
        @pl.when(pl.program_id(0) == 0)
        def _():
            ds_scr[...] = jnp.zeros_like(ds_scr)

        def chunk(cc, carry):
            c = nc - 1 - cc
            r0 = pl.multiple_of(c * DN_CHUNK, DN_CHUNK)
            rows = pl.ds(r0, DN_CHUNK)
            e = jnp.exp(gl_ref[pl.ds(c, 1), :])
            states = [sall_ref[c, h] for h in range(HEADS)]
            ds_outs = [ds_scr[h] for h in range(HEADS)]
            u_a, w_a, kt_a, qd_a, do_a = u_ref[rows, :], w_ref[rows, :], kt_ref[rows, :], qd_ref[rows, :], do_ref[rows, :]
            a_a = [a_ref[h, rows, :] for h in range(HEADS)]
            da, dqd, dkt, du, dw, dgl, ds_new = [], [], [], [], [], [], []
            for h in range(HEADS):
                cs = slice(h * LANE, (h + 1) * LANE)
                s, ds_out = states[h], ds_outs[h]
                w_c, kt_c, qd_c, do_c = w_a[:, cs], kt_a[:, cs], qd_a[:, cs], do_a[:, cs]
                v_new = u_a[:, cs] - _dot(w_c, s, _NN)
                dv_new = _dot(a_a[h], do_c, _TN) + _dot(kt_c, ds_out, _NN)
                cots = jnp.concatenate([do_c, dv_new], axis=0)
                both = _dot(cots, s, _NT)
                dqd.append(both[:DN_CHUNK])
                dw.append(-both[DN_CHUNK:])
                da.append(_dot(do_c, v_new, _NT))
                dkt.append(_dot(v_new, ds_out, _NT))
                du.append(dv_new)
                eh = e[:, cs]
                dgl.append(jnp.broadcast_to(jnp.sum(ds_out * s, axis=0, keepdims=True) * eh, (8, LANE)))
                ds_new.append(ds_out * eh + _dot(jnp.concatenate([qd_c, -w_c], axis=0), cots, _TN))
            cat = lambda parts: jnp.concatenate(parts, axis=1)
            dqd_ref[rows, :], dkt_ref[rows, :], du_ref[rows, :], dw_ref[rows, :] = cat(dqd), cat(dkt), cat(du), cat(dw)
            dgl_ref[pl.ds(pl.multiple_of(c * 8, 8), 8), :] = cat(dgl)
            for h in range(HEADS):
                da_ref[h, rows, :] = da[h]
                ds_scr[h] = ds_new[h]
            return carry

        lax.fori_loop(0, nc, chunk, 0)

    rev = lambda i: (ng - 1 - i, 0)
    row = pl.BlockSpec((rr, HEADS * LANE), rev)
    a_spec = pl.BlockSpec((HEADS, rr, DN_CHUNK), lambda i: (0, ng - 1 - i, 0))
    gl_spec = pl.BlockSpec((nc, HEADS * LANE), rev)
    wide = jax.ShapeDtypeStruct((t, HEADS * LANE), F32)
    outs, carried = _carried_call(
        body, rider, *_grid_ends((ng,)), name="delta_scan_bwd", grid=(ng,),
        in_specs=[row] * 4 + [a_spec, gl_spec, pl.BlockSpec((nc, HEADS, LANE, LANE), lambda i: (ng - 1 - i, 0, 0, 0)), row],
        out_specs=[row] * 4 + [a_spec, pl.BlockSpec((nc * 8, HEADS * LANE), rev)],
        out_shape=[wide] * 4 + [jax.ShapeDtypeStruct((HEADS, t, DN_CHUNK), F32),
                                jax.ShapeDtypeStruct((t // DN_CHUNK * 8, HEADS * LANE), F32)],
        scratch_shapes=[pltpu.VMEM((HEADS, LANE, LANE), F32)], sem=("arbitrary",), args=(u, w, qd, kt, intra, gl, sall, do))
    return tuple(outs[:5]) + (outs[5].reshape(t // DN_CHUNK, 8, HEADS * LANE)[:, 0, :],), carried


_ATT_TILE = 512


def _kv_rows(j, tk):
    return pl.ds(pl.multiple_of(j * tk, tk), tk)


def _att_scores(ql, qr, ckv_ref, kr_ref, j, tk):
    ks = _kv_rows(j, tk)
    return (_dot(ql, ckv_ref[ks, :], _NT) + _dot(qr, kr_ref[ks, :], _NT)) * ATT_SCALE


def _diag_mask(s):
    qi = lax.broadcasted_iota(jnp.int32, s.shape, 0)
    ki = lax.broadcasted_iota(jnp.int32, s.shape, 1)
    return jnp.where(ki <= qi, s, NEG_BIG)


def _attention(qn, qr_pre, cosb, sinb, ckv, kr, wuk, wuv):
    t = ckv.shape[0]
    tq = min(_ATT_TILE, t)

    nl = tq // LANE

    def lane_fold(v, op):
        out = v[:, :LANE]
        for k in range(1, nl):
            out = op(out, v[:, k * LANE:(k + 1) * LANE])
        return out

    def body(qn_ref, qr_ref, cos_ref, sin_ref, ckv_ref, kr_ref, wuk_ref, wuv_ref, o_ref, lse_ref, qrope_ref, omla_ref,
             s_all, m_lanes, l_lanes, acc_scr):
        h, qi = pl.program_id(0), pl.program_id(1)
        q_lat = _dot(qn_ref[...], wuk_ref[h], _NT).astype(_CDT)
        q_rope = _rope(qr_ref[...], cos_ref[...], sin_ref[...]).astype(qrope_ref.dtype)
        qrope_ref[...] = q_rope
        m_lanes[...] = jnp.full_like(m_lanes, NEG_BIG)

        def scores(j, masked):
            s = _att_scores(q_lat, q_rope, ckv_ref, kr_ref, j, tq)
            if masked:
                s = _diag_mask(s)
            s_all[j] = s
            m_lanes[...] = jnp.maximum(m_lanes[...], lane_fold(s, jnp.maximum))

        def scores_body(j, carry):
            scores(j, False)
            return carry

        lax.fori_loop(0, qi, scores_body, 0)
        scores(qi, True)
        m = jnp.max(m_lanes[...], axis=-1, keepdims=True)
        mb = jnp.broadcast_to(m, (tq, LANE))
        l_lanes[...] = jnp.zeros_like(l_lanes)
        acc_scr[...] = jnp.zeros_like(acc_scr)

        def weigh(j, carry):
            s = s_all[j]
            p = jnp.concatenate([jnp.exp(s[:, k * LANE:(k + 1) * LANE] - mb) for k in range(nl)], axis=1)
            l_lanes[...] += lane_fold(p, jnp.add)
            acc_scr[...] += _dot(p, ckv_ref[_kv_rows(j, tq), :], _NN)
            return carry

        lax.fori_loop(0, qi + 1, weigh, 0)
        l = jnp.sum(l_lanes[...], axis=-1, keepdims=True)
        out = acc_scr[...] / l
        o_ref[...] = out
        lse_ref[...] = m + jnp.log(l)
        omla_ref[...] = _dot(out, wuv_ref[h], _NN).astype(omla_ref.dtype)

    col = pl.BlockSpec((tq, LANE), lambda h, i: (i, h))
    table = pl.BlockSpec((tq, ROPE_PAD), lambda h, i: (i, 0))
    wspec = pl.BlockSpec((HEADS, KV_LORA, NOPE), lambda h, i: (0, 0, 0))
    return pl.pallas_call(
        body, name="attention", grid=(HEADS, t // tq),
        in_specs=[col, col, table, table, pl.BlockSpec((t, KV_LORA), lambda h, i: (0, 0)),
                  pl.BlockSpec((t, ROPE_PAD), lambda h, i: (0, 0)), wspec, wspec],
        out_specs=[pl.BlockSpec((None, tq, KV_LORA), lambda h, i: (h, i, 0)),
                   pl.BlockSpec((None, tq, 1), lambda h, i: (h, i, 0)), col, col],
        out_shape=[jax.ShapeDtypeStruct((HEADS, t, KV_LORA), F32), jax.ShapeDtypeStruct((HEADS, t, 1), F32),
                   jax.ShapeDtypeStruct((t, HEADS * ROPE_PAD), _CDT), jax.ShapeDtypeStruct((t, HEADS * NOPE), _CDT)],
        scratch_shapes=[pltpu.VMEM((t // tq, tq, tq), F32), pltpu.VMEM((tq, LANE), F32), pltpu.VMEM((tq, LANE), F32),
                        pltpu.VMEM((tq, KV_LORA), F32)],
        compiler_params=_cparams(("parallel", "parallel")))(qn, qr_pre, cosb, sinb, ckv, kr, wuk, wuv)


def _attention_bwd(qn, qr, cosb, sinb, ckv, kr, wuk, wuv, out, lse, do_mla):
    t = ckv.shape[0]
    tq = min(_ATT_TILE, t)

    def body(qn_ref, qr_ref, cos_ref, sin_ref, ckv_ref, kr_ref, wuk_ref, wuv_ref, o_ref, lse_ref, do_ref,
             dql_ref, dqr_ref, dqn_ref, dckv_ref, dkr_ref, dql_scr, dqr_scr):
        h, qi = pl.program_id(0), pl.program_id(1)

        @pl.when((h == 0) & (qi == 0))
        def _():
            dckv_ref[...] = jnp.zeros_like(dckv_ref)
            dkr_ref[...] = jnp.zeros_like(dkr_ref)

        q_lat = _dot(qn_ref[...], wuk_ref[h], _NT).astype(_CDT)
        q_rope = qr_ref[...]
        d_out = _dot(do_ref[...], wuv_ref[h], _NT)
        d_o = d_out.astype(_CDT)
        lse_v = lse_ref[...]
        dsum = jnp.sum(d_out * o_ref[...], axis=-1, keepdims=True)
        dql_scr[...] = jnp.zeros_like(dql_scr)
        dqr_scr[...] = jnp.zeros_like(dqr_scr)

        def step(j, masked):
            ks = _kv_rows(j, tq)
            s = _att_scores(q_lat, q_rope, ckv_ref, kr_ref, j, tq)
            if masked:
                s = _diag_mask(s)
            p = jnp.exp(s - lse_v)
            kv = ckv_ref[ks, :]
            ds = (p * (_dot(d_o, kv, _NT) - dsum) * ATT_SCALE).astype(_CDT)
            pb = p.astype(_CDT)
            dql_scr[...] += _dot(ds, kv, _NN)
            dqr_scr[...] += _dot(ds, kr_ref[ks, :], _NN)
            dckv_ref[ks, :] += _dot(pb, d_o, _TN) + _dot(ds, q_lat, _TN)
            dkr_ref[ks, :] += _dot(ds, q_rope, _TN)

        def loop_body(j, carry):
            step(j, False)
            return carry

        lax.fori_loop(0, qi, loop_body, 0)
        step(qi, True)
        dql = dql_scr[...].astype(dql_ref.dtype)
        dql_ref[...] = dql
        dqn_ref[...] = _dot(dql, wuk_ref[h], _NN).astype(dqn_ref.dtype)
        dqr_ref[...] = _rope_bwd(dqr_scr[...], cos_ref[...], sin_ref[...]).astype(dqr_ref.dtype)

    lat = pl.BlockSpec((None, tq, KV_LORA), lambda h, i: (h, i, 0))
    col = pl.BlockSpec((tq, LANE), lambda h, i: (i, h))
    table = pl.BlockSpec((tq, ROPE_PAD), lambda h, i: (i, 0))
    kfull = pl.BlockSpec((t, KV_LORA), lambda h, i: (0, 0))
    rfull = pl.BlockSpec((t, ROPE_PAD), lambda h, i: (0, 0))
    wspec = pl.BlockSpec((HEADS, KV_LORA, NOPE), lambda h, i: (0, 0, 0))
    wide = jax.ShapeDtypeStruct((t, HEADS * LANE), _CDT)
    return pl.pallas_call(
        body, name="attention_bwd", grid=(HEADS, t // tq),
        in_specs=[col, col, table, table, kfull, rfull, wspec, wspec, lat,
                  pl.BlockSpec((None, tq, 1), lambda h, i: (h, i, 0)), col],
        out_specs=[lat, col, col, kfull, rfull],
        out_shape=[jax.ShapeDtypeStruct((HEADS, t, KV_LORA), _CDT), wide, wide,
                   jax.ShapeDtypeStruct((t, KV_LORA), F32), jax.ShapeDtypeStruct((t, ROPE_PAD), F32)],
        scratch_shapes=[pltpu.VMEM((tq, KV_LORA), F32), pltpu.VMEM((tq, ROPE_PAD), F32)],
        compiler_params=_cparams(("arbitrary", "arbitrary")))(qn, qr, cosb, sinb, ckv, kr, wuk, wuv, out, lse, do_mla)


_DX_ROWS = 256


def _dx_fused(pairs, add, add_scale, rider=None, name="b_dx"):
    t, d = add.shape
    tm = min(_DX_ROWS, t)
    n = len(pairs)

    def body(*refs):
        acc = refs[2 * n][...] * add_scale
        for i in range(n):
            acc = acc + _dot(refs[i][...], refs[n + i][...], _NT)
        refs[2 * n + 1][...] = acc

    in_specs = [pl.BlockSpec((tm, a.shape[1]), lambda i: (i, 0)) for a, _ in pairs]
    in_specs += [pl.BlockSpec(w.shape, lambda i: (0, 0)) for _, w in pairs]
    row = pl.BlockSpec((tm, d), lambda i: (i, 0))
    grid = (t // tm,)
    (dx,), carried = _carried_call(
        body, rider, *_grid_ends(grid), name=name, grid=grid, in_specs=in_specs + [row], out_specs=[row],
        out_shape=[jax.ShapeDtypeStruct((t, d), F32)], scratch_shapes=[], sem=("arbitrary",),
        args=tuple(a for a, _ in pairs) + tuple(w for _, w in pairs) + (add,))
    return dx, carried


def _ffn_in_swiglu(a, w):
    t, k = a.shape
    hid = w.shape[1] // 2
    tm, tn = _tile(t, 1024), _tile(hid, 1408)
    nj = hid // tn

    def body(a_ref, bg_ref, bu_ref, act_ref, gt_ref, up_ref):
        av = a_ref[...].astype(_CDT)
        gt = jnp.dot(av, bg_ref[...].astype(_CDT), preferred_element_type=F32)
        up = jnp.dot(av, bu_ref[...].astype(_CDT), preferred_element_type=F32)
        act_ref[...] = _swiglu(gt, up).astype(act_ref.dtype)
        gt_ref[...] = gt.astype(gt_ref.dtype)
        up_ref[...] = up.astype(up_ref.dtype)

    out = pl.BlockSpec((tm, tn), lambda i, j: (i, j))
    return pl.pallas_call(
        body, name="f_ffn_in_swiglu", grid=(t // tm, nj),
        in_specs=[pl.BlockSpec((tm, k), lambda i, j: (i, 0)), pl.BlockSpec((k, tn), lambda i, j: (0, j)),
                  pl.BlockSpec((k, tn), lambda i, j: (0, nj + j))],
        out_specs=[out] * 3, out_shape=[jax.ShapeDtypeStruct((t, hid), _CDT)] * 3,
        compiler_params=_cparams(("parallel", "parallel")))(a, w, w)


def _gated_norm(o, z, w):
    return _rms_norm(o, w) * _silu(z)


def _gated_norm_heads(o, z, w):
    heads = [_gated_norm(o[:, h * LANE:(h + 1) * LANE], z[:, h * LANE:(h + 1) * LANE], w) for h in range(HEADS)]
    return jnp.concatenate(heads, axis=1)


def _mla_pre(ckv, krp, cq, cosb, sinb, qw, kw):
    return _rms_norm(cq, qw), _rms_norm(ckv, kw), _rope(krp, cosb, sinb)


def _merge(gg, y_dn, y_mla):
    return _sigmoid(gg[:, :D_MODEL]) * y_dn + _sigmoid(gg[:, D_MODEL:]) * y_mla


def _ln1(xv, attn_out, g, b):
    return _layer_norm(ALPHA * xv + attn_out, g, b)


def _final(h1, ffn, gate_pre, ple_proj, g, b):
    return _layer_norm(ALPHA * h1 + ffn + _sigmoid(gate_pre) * ple_proj, g, b)


def _swiglu(gt, up):
    return _silu(gt) * up


def _local_step(x, p, cosb, sinb, target, wt, sp, exch):
    t = x.shape[0]
    bf = _CDT
    xb = x.astype(bf)
    g = {}

    qkv_pre = _mm2(xb, wt['qkv'], name="f_qkv")
    z = _mm2(xb, wt['z'], name="f_z")
    gg = _mm2(xb, wt['gg'], name="f_gg")
    pm = _mm2(xb, wt['mla'], name="f_mla")
    qkv_act = _conv_silu(qkv_pre, sp['conv_w'])
    (u, w_, qd, kt, intra, gl, t_inv, qd2, au), sent = _delta_local(qkv_act, pm, sp['a_log'], sp['dt_bias'],
                                                                    rider=exch.gather_send())
    (o_dn, sall), passed = _delta_scan(u, w_, qd2, kt, au, gl, rider=exch.gather_pass(sent))
    wt = dict(wt, **exch.weights(passed))
    def gated_norm_br(h, o, zz, w, wbr):
        og_v = _gated_norm_heads(o, zz, w).astype(bf)
        return og_v, jnp.dot(og_v, wbr.astype(bf), preferred_element_type=F32)

    og, y_dn = _rowwise(gated_norm_br, [(o_dn, D_MODEL, 0, False), (z, D_MODEL, 0, False)],
                        [sp['dn_norm_w'], wt['br_dn']],
                        [(D_MODEL, D_MODEL, False, bf), (D_MODEL, D_MODEL, False, F32)], name="f_gated_norm_br")

    def mla_pre_uq(h, ckv, krp, cq, cosv, sinv, qw, kw, wn, wr):
        c_q_v, c_kv_v, k_rope_v = _mla_pre(ckv, krp, cq, cosv, sinv, qw, kw)
        c_q_b = c_q_v.astype(bf)
        return (c_q_b, c_kv_v, k_rope_v, jnp.dot(c_q_b, wn.astype(bf), preferred_element_type=F32),
                jnp.dot(c_q_b, wr.astype(bf), preferred_element_type=F32))

    c_q, c_kv, k_rope, q_nope, q_rope_pre = _rowwise(
        mla_pre_uq,
        [(pm, KV_LORA, 0, False), (pm, ROPE_PAD, 2, False), (pm, Q_LORA, 2, False),
         (cosb, ROPE_PAD, 0, False), (sinb, ROPE_PAD, 0, False)],
        [sp['q_norm_w'], sp['kv_norm_w'], wt['uq_nope'], wt['uq_rope']],
        [(Q_LORA, Q_LORA, False, bf), (KV_LORA, KV_LORA, False, bf), (ROPE_PAD, ROPE_PAD, False, bf),
         (HEADS * NOPE, HEADS * NOPE, False, bf), (HEADS * ROPE_PAD, HEADS * ROPE_PAD, False, F32)], name="f_mla_pre_uq")
    out_lat, lse, q_rope, o_mla = _attention(q_nope, q_rope_pre, cosb, sinb, c_kv, k_rope, wt['uk'], wt['uv'])
    y_mla = _mm2(o_mla, wt['br_mla'], name="f_br_mla")

    def merge_o_ln1(h, ggv, yd, ym, xv, wo, gv, bv):
        mixed_v = _merge(ggv, yd, ym).astype(bf)
        ao = jnp.dot(mixed_v, wo.astype(bf), preferred_element_type=F32)
        h1v = _ln1(xv, ao, gv, bv)
        return mixed_v, ao, h1v, h1v

    mixed, attn_out, h1, h1b = _rowwise(
        merge_o_ln1, [(gg, 2 * D_MODEL, 0, False), (y_dn, D_MODEL, 0, False), (y_mla, D_MODEL, 0, False), (x, D_MODEL, 0, False)],
        [wt['o'], sp['ln1_g'], sp['ln1_b']],
        [(D_MODEL, D_MODEL, False, bf), (D_MODEL, D_MODEL, False, F32), (D_MODEL, D_MODEL, False, F32),
         (D_MODEL, D_MODEL, False, bf)], name="f_merge_o_ln1")
    act, ffn_gt, ffn_up = _ffn_in_swiglu(h1b, wt['ffn_in'])
    pb = p.astype(bf)

    def final_fn(h, h1v, actv, pv, tgt, gt, up, wfo, wpg, wpl, gv, bv):
        ffnv = jnp.dot(actv.astype(bf), wfo.astype(bf), preferred_element_type=F32)
        gpv = jnp.dot(h1v.astype(bf), wpg.astype(bf), preferred_element_type=F32)
        ppv = jnp.dot(pv.astype(bf), wpl.astype(bf), preferred_element_type=F32)
        y, vjp = jax.vjp(_final, h1v, ffnv, gpv, ppv, gv, bv)
        err = y - tgt
        dh1, dffn, dgp, dpp, dg, db = vjp(err * (1.0 / D_MODEL))
        sq = err * err
        lanes = sq[:, :LANE]
        for j in range(1, D_MODEL // LANE):
            lanes = lanes + sq[:, j * LANE:(j + 1) * LANE]
        loss = jnp.sum(lanes, axis=0, keepdims=True) * (0.5 / D_MODEL)
        dffn_b = dffn.astype(bf)
        dact = _dot(dffn_b, wfo, _NT).astype(bf).astype(F32)
        _, vjp_s = jax.vjp(_swiglu, gt.astype(F32), up.astype(F32))
        dgt, dup = vjp_s(dact)
        return dffn, dffn_b, dgp, dpp, jnp.concatenate([dgt, dup], axis=1), dg, db, loss

    dpre2, dpre2b, dgate_pre, dple_proj, dffn_in, g['ln2_g'], g['ln2_b'], loss_lanes = _rowwise(
        final_fn, [(h1, D_MODEL, 0, False), (act, FFN_HIDDEN, 0, False), (pb, PLE_DIM, 0, False), (target, D_MODEL, 0, False),
                   (ffn_gt, FFN_HIDDEN, 0, False), (ffn_up, FFN_HIDDEN, 0, False)],
        [wt['ffn_out'], wt['ple_gate'], wt['ple'], sp['ln2_g'], sp['ln2_b']],
        [(D_MODEL, D_MODEL, False, F32)] + [(D_MODEL, D_MODEL, False, bf)] * 3 + [(2 * FFN_HIDDEN, 2 * FFN_HIDDEN, False, bf)],
        [(1, D_MODEL), (1, D_MODEL), (1, LANE)], name="b_final")

    g['ple'] = _mm2(pb, dple_proj, ta=True, name="g_ple")
    g['ple_gate'] = _mm2(h1b, dgate_pre, ta=True, name="g_ple_gate")
    g['ffn_out'] = _mm2(act, dpre2b, ta=True, name="g_ffn_out")
    g['ffn_in'] = _mm2(h1b, dffn_in, ta=True, name="g_ffn_in")
    dh1, _ = _dx_fused([(dffn_in, wt['ffn_in']), (dgate_pre, wt['ple_gate'])], dpre2, ALPHA, name="b_dh1")

    def attn_out_bwd(h, xv, ao, d, ggv, yd, ym, o, zz, wo, wbd, wbm, gv, bv, nw):
        _, vjp = jax.vjp(_ln1, xv, ao, gv, bv)
        _, dao, dg, db = vjp(d)
        dao_b = dao.astype(bf)
        _, vjp_m = jax.vjp(_merge, ggv, yd, ym)
        dggv, dyd, dym = vjp_m(_dot(dao_b, wo, _NT))
        dyd_b, dym_b = dyd.astype(bf), dym.astype(bf)
        _, vjp_n = jax.vjp(_gated_norm_heads, o, zz, nw)
        do_v, dz_v, dnw = vjp_n(_dot(dyd_b, wbd, _NT))
        return dao, dao_b, dggv, dyd_b, dym_b, do_v, dz_v, _dot(dym_b, wbm, _NT), dg, db, dnw

    row_d = lambda a: (a, D_MODEL, 0, False)
    dpre1, dpre1b, dgg, dy_dn, dy_mla, do_dn, dz, do_mla, g['ln1_g'], g['ln1_b'], g['dn_norm_w'] = _rowwise(
        attn_out_bwd, [row_d(x), row_d(attn_out), row_d(dh1), (gg, 2 * D_MODEL, 0, False), row_d(y_dn), row_d(y_mla),
                       row_d(o_dn), row_d(z)],
        [wt['o'], wt['br_dn'], wt['br_mla'], sp['ln1_g'], sp['ln1_b'], sp['dn_norm_w']],
        [(D_MODEL, D_MODEL, False, F32), (D_MODEL, D_MODEL, False, bf), (2 * D_MODEL, 2 * D_MODEL, False, bf),
         (D_MODEL, D_MODEL, False, bf), (D_MODEL, D_MODEL, False, bf), (D_MODEL, D_MODEL, False, F32),
         (D_MODEL, D_MODEL, False, bf), (D_MODEL, D_MODEL, False, bf)],
        [(1, D_MODEL), (1, D_MODEL), (1, LANE)], name="b_attn_out")
    g['o'] = _mm2(mixed, dpre1b, ta=True, name="g_o")
    g['br_dn'] = _mm2(og, dy_dn, ta=True, name="g_br_dn")
    g['br_mla'] = _mm2(o_mla, dy_mla, ta=True, name="g_br_mla")

    g['uv'] = _mm(out_lat, do_mla, name="g_uv", ta=True, heads=HEADS, a_head='lead', b_head='col', out_head='lead',
                  dims=(KV_LORA, NOPE, t))
    dq_lat, dq_rope_pre, dq_nope, dckv_att, dkr_att = _attention_bwd(
        q_nope, q_rope, cosb, sinb, c_kv, k_rope, wt['uk'], wt['uv'], out_lat, lse, do_mla)
    g['uk'] = _mm(dq_lat, q_nope, name="g_uk", ta=True, heads=HEADS, a_head='lead', b_head='col', out_head='lead',
                  dims=(KV_LORA, NOPE, t))
    g['uq_nope'] = _mm2(c_q, dq_nope, ta=True, name="g_uq_nope")
    g['uq_rope'] = _mm2(c_q, dq_rope_pre, ta=True, name="g_uq_rope")
    dc_q = _mm2(dq_nope, wt['uq_nope'], tb=True, name="b_dcq_nope")
    dc_q = _mm2(dq_rope_pre, wt['uq_rope'], tb=True, name="b_dcq_rope", add=dc_q)

    (du, dw, dqd, dkt, dintra, dgl), paired = _delta_scan_bwd(u, w_, qd, kt, intra, gl, sall, do_dn, rider=exch.pair_send(g))
    (dq_a, dk_a, dv_a, dba, g['a_log'], g['dt_bias']), arrived = _delta_local_bwd(
        qkv_act, pm, sp['a_log'], sp['dt_bias'], t_inv, du, dw, dqd, dkt, dintra, dgl, rider=exch.reduce_send(paired))
    exch.reduce_arrived(arrived)
    dqkv_pre, g['conv_w'] = _conv_silu_bwd(qkv_pre, sp['conv_w'], [dq_a, dk_a, dv_a])

    def mla_pre_bwd(h, ckv, cq, cosv, sinv, dcq, dckv, dkr, dba_v, qw, kw):
        _, vjp = jax.vjp(lambda a, c, d, e: (_rms_norm(c, d), _rms_norm(a, e)), ckv, cq, qw, kw)
        dckv_p, dcq_p, dqw, dkw = vjp((dcq, dckv))
        dkr_p = _rope_bwd(dkr, cosv, sinv)
        dpm = jnp.concatenate([dckv_p, dkr_p, dba_v, jnp.zeros((ckv.shape[0], 2 * LANE), F32), dcq_p], axis=1)
        return dpm, dqw, dkw

    dpm, g['q_norm_w'], g['kv_norm_w'] = _rowwise(
        mla_pre_bwd,
        [(pm, KV_LORA, 0, False), (pm, Q_LORA, 2, False), (cosb, ROPE_PAD, 0, False), (sinb, ROPE_PAD, 0, False),
         (dc_q, Q_LORA, 0, False), (dckv_att, KV_LORA, 0, False), (dkr_att, ROPE_PAD, 0, False), (dba, LANE, 0, False)],
        [sp['q_norm_w'], sp['kv_norm_w']], [(1152, 1152, False, bf)], [(1, Q_LORA), (1, KV_LORA)], name="b_mla_pre")

    rider = exch.small_send(g)
    if rider is None:
        g['qkv'] = _mm2(xb, dqkv_pre, ta=True, name="g_qkv")
    else:
        g['qkv'], got = _mm2(xb, dqkv_pre, ta=True, name="g_qkv", rider=rider)
        exch.small_arrived(got)
    g['z'] = _mm2(xb, dz, ta=True, name="g_z")
    g['gg'] = _mm2(xb, dgg, ta=True, name="g_gg")
    g['mla'] = _mm2(xb, dpm, ta=True, name="g_mla")
    dx, arrived = _dx_fused([(dqkv_pre, wt['qkv']), (dz, wt['z']), (dgg, wt['gg']), (dpm, wt['mla'])], dpre1, ALPHA,
                            rider=exch.in_send(g))
    exch.in_arrived(arrived)
    return loss_lanes, dx, g


_IN_SIZES = (QKV_W, HEADS * DN_DK, HEADS, HEADS, Q_LORA, KV_LORA, ROPE, D_MODEL, D_MODEL)


def _rope_tables(positions):
    inv_freq = ROPE_BASE ** (-jnp.arange(0, ROPE, 2, dtype=F32) / ROPE)
    ang = positions.astype(F32)[:, None] * inv_freq
    cos, sin = jnp.cos(ang), jnp.sin(ang)
    zeros = jnp.zeros((positions.shape[0], ROPE_PAD - ROPE), F32)
    return jnp.concatenate([cos, cos, zeros], axis=1), jnp.concatenate([-sin, sin, zeros], axis=1)


def _prep_w_in(w_in):
    dt = w_in.dtype
    offs = [0]
    for s in _IN_SIZES:
        offs.append(offs[-1] + s)
    qkv, z, wb, wa, cq, ckv, kr, gd, gm = [w_in[:, offs[i]:offs[i + 1]] for i in range(len(_IN_SIZES))]
    zc = lambda n: jnp.zeros((D_MODEL, n), dt)
    return {
        'qkv': qkv, 'z': z, 'gg': jnp.concatenate([gd, gm], axis=1),
        'mla': jnp.concatenate([ckv, kr, zc(ROPE_PAD - ROPE), wb, wa, zc(LANE - 2 * HEADS), zc(2 * LANE), cq], axis=1),
    }


def _prep_weights(full):
    w_uq = full['w_uq']
    wt = {
        'uq_nope': w_uq[:, :, :NOPE].reshape(Q_LORA, HEADS * NOPE),
        'uq_rope': jnp.pad(w_uq[:, :, NOPE:], ((0, 0), (0, 0), (0, ROPE_PAD - ROPE))).reshape(Q_LORA, HEADS * ROPE_PAD),
        'uk': jnp.transpose(full['w_uk'], (1, 0, 2)), 'uv': jnp.transpose(full['w_uv'], (1, 0, 2)),
        'br_dn': full['w_br_dn'], 'br_mla': full['w_br_mla'], 'o': full['w_o'], 'ffn_in': full['w_ffn_in'],
        'ffn_out': full['w_ffn_out'], 'ple': full['w_ple'], 'ple_gate': full['w_ple_gate'],
    }
    return wt


def _prep_small(small):
    pad = lambda v: jnp.pad(v, (0, LANE - v.shape[0]))[None, :]
    return {
        'conv_w': small['conv_w'], 'a_log': pad(small['dn_a_log']), 'dt_bias': pad(small['dn_dt_bias']),
        'dn_norm_w': small['dn_norm_w'][None, :], 'q_norm_w': small['q_norm_w'][None, :],
        'kv_norm_w': small['kv_norm_w'][None, :], 'ln1_g': small['ln1_g'][None, :], 'ln1_b': small['ln1_b'][None, :],
        'ln2_g': small['ln2_g'][None, :], 'ln2_b': small['ln2_b'][None, :],
    }


def _w_in_grad(g):
    mla = g['mla']
    ba0 = KV_LORA + ROPE_PAD
    cq0 = ba0 + 3 * LANE
    return jnp.concatenate([
        g['qkv'], g['z'], mla[:, ba0:ba0 + HEADS], mla[:, ba0 + HEADS:ba0 + 2 * HEADS], mla[:, cq0:cq0 + Q_LORA],
        mla[:, :KV_LORA], mla[:, KV_LORA:KV_LORA + ROPE], g['gg']], axis=1)


def _unprep_grads_late(g):
    return {
        'conv_w': g['conv_w'], 'dn_a_log': g['a_log'][0, :HEADS], 'dn_dt_bias': g['dt_bias'][0, :HEADS],
        'dn_norm_w': g['dn_norm_w'][0], 'q_norm_w': g['q_norm_w'][0], 'kv_norm_w': g['kv_norm_w'][0],
        'ln1_g': g['ln1_g'][0], 'ln1_b': g['ln1_b'][0], 'ln2_g': g['ln2_g'][0], 'ln2_b': g['ln2_b'][0],
    }


def _unprep_grads_early(g):
    w_uq = jnp.concatenate([g['uq_nope'].reshape(Q_LORA, HEADS, NOPE),
                            g['uq_rope'].reshape(Q_LORA, HEADS, ROPE_PAD)[:, :, :ROPE]], axis=2)
    return {
        'w_uq': w_uq, 'w_uk': jnp.transpose(g['uk'], (1, 0, 2)), 'w_uv': jnp.transpose(g['uv'], (1, 0, 2)),
        'w_br_dn': g['br_dn'], 'w_br_mla': g['br_mla'], 'w_o': g['o'],
        'w_ffn_in': g['ffn_in'], 'w_ffn_out': g['ffn_out'], 'w_ple': g['ple'], 'w_ple_gate': g['ple_gate'],
    }


_FLATB_PIECES = (
    ('w_ffn_out', 704, (704, D_MODEL)), ('w_br_dn', 256, (256, D_MODEL)), ('w_br_mla', 256, (256, D_MODEL)),
    ('w_o', 256, (256, D_MODEL)), ('w_ple_gate', 256, (256, D_MODEL)), ('w_uq', 144, (96, HEADS, NOPE + ROPE)),
    ('w_uk', 64, (64, HEADS, NOPE)), ('w_uv', 64, (64, HEADS, NOPE)), ('w_ple', 64, (PLE_DIM, 256)),
)
FLATB_ROWS = 2112
W_IN_SHARD = D_IN // N_SHARD
FFN_IN_SHARD = 2 * FFN_HIDDEN // N_SHARD
A_ROWS = D_MODEL + 32
_CONV_SHARD = QKV_W // N_SHARD
_ADD_TILES = (256, 256, 352)


def _flatb_offsets():
    offs, o = {}, 0
    for name, rows, _ in _FLATB_PIECES:
        offs[name] = o
        o += rows
    return offs, o


def _pack_shards(ws, conv_w):
    conv_bits = lax.bitcast_convert_type(conv_w, jnp.bfloat16).reshape(DN_CONV, 2 * _CONV_SHARD).astype(_CDT)
    tail = jnp.pad(conv_bits, ((0, A_ROWS - D_MODEL - DN_CONV), (0, W_IN_SHARD - 2 * _CONV_SHARD)))
    a_buf = jnp.concatenate([ws['w_in'].astype(_CDT), tail], axis=0)
    parts = [ws[name].astype(_CDT).reshape(rows, FLAT_W) for name, rows, _ in _FLATB_PIECES]
    used = sum(p.shape[0] for p in parts)
    parts.append(jnp.zeros((FLATB_ROWS - used, FLAT_W), _CDT))
    return [a_buf, ws['w_ffn_in'].astype(_CDT), jnp.concatenate(parts, axis=0)]


def _unpack_w_in(gathered, local, me):
    a = [jnp.where(me == s, local, gathered[s]) for s in range(N_SHARD)]
    conv = [lax.bitcast_convert_type(
        p[D_MODEL:D_MODEL + DN_CONV, :2 * _CONV_SHARD].astype(jnp.bfloat16).reshape(DN_CONV, _CONV_SHARD, 2), F32) for p in a]
    return jnp.concatenate([p[:D_MODEL] for p in a], axis=1), jnp.concatenate(conv, axis=1)


def _unpack_rest(gathered, local, me):
    pick = lambda b, s: jnp.where(me == s, local[b], gathered[b][s])
    full = {'w_ffn_in': jnp.concatenate([pick(0, s) for s in range(N_SHARD)], axis=1)}
    offs, _ = _flatb_offsets()
    fb = [pick(1, s) for s in range(N_SHARD)]
    for name, rows, shape in _FLATB_PIECES:
        pieces = [p[offs[name]:offs[name] + rows].reshape(shape) for p in fb]
        full[name] = jnp.concatenate(pieces, axis=1 if name == 'w_ple' else 0)
    return full


def _shard_columns(g, w):
    return jnp.stack([g[:, s * w:(s + 1) * w] for s in range(N_SHARD)])


def _pack_grads_rest(gw):
    parts = []
    for name, rows, _ in _FLATB_PIECES:
        g = gw[name]
        if name == 'w_ple':
            parts.append(_shard_columns(g, PLE_DIM).reshape(N_SHARD, rows, FLAT_W))
        else:
            parts.append(g.reshape(N_SHARD, rows, FLAT_W))
    used = sum(p.shape[1] for p in parts)
    parts.append(jnp.zeros((N_SHARD, FLATB_ROWS - used, FLAT_W), F32))
    return [_shard_columns(gw['w_ffn_in'], FFN_IN_SHARD), jnp.concatenate(parts, axis=1)]


def _unpack_reduced(mine, theirs, c):
    whole = [jnp.concatenate([jnp.where(c == 0, m, t), jnp.where(c == 0, t, m)], axis=0) for m, t in zip(mine, theirs)]
    out = {'w_in': whole[0], 'w_ffn_in': whole[1]}
    offs, _ = _flatb_offsets()
    for name, rows, shape in _FLATB_PIECES:
        out[name] = whole[2][offs[name]:offs[name] + rows].reshape(shape)
    return out


_HBM = pl.BlockSpec(memory_space=pltpu.HBM)


def _place():
    x, y, c = lax.axis_index("x"), lax.axis_index("y"), lax.axis_index("c")
    chips = [(1 - x, y), (x, 1 - y), (1 - x, 1 - y)]
    return x, y, c, chips


def _remote(src, dst, send_sems, recv_sems, k, to):
    return pltpu.make_async_remote_copy(src_ref=src, dst_ref=dst, send_sem=send_sems.at[k], recv_sem=recv_sems.at[k],
                                        device_id=to, device_id_type=_MESH)


def _half_rows(ref, half, hf, lead=None):
    rows = pl.ds(pl.multiple_of(hf * half, 16), half)
    return ref.at[rows, :] if lead is None else ref.at[lead, rows, :]


class _Rider:
    def __init__(self, inputs, out_shape, n_sems, copies, aliases=None):
        self.inputs, self.out_shape, self.n_sems, self.copies = list(inputs), list(out_shape), n_sems, copies
        self.aliases = aliases or {}


def _carried_call(body, rider, first, last, *, name, grid, in_specs, out_specs, out_shape, scratch_shapes, sem, args):
    n_in, n_out, n_scr = len(in_specs), len(out_specs), len(scratch_shapes)
    if rider is None:
        res = pl.pallas_call(body, name=name, grid=grid, in_specs=in_specs, out_specs=out_specs, out_shape=out_shape,
                             scratch_shapes=scratch_shapes, compiler_params=_cparams(sem))(*args)
        return list(res), []
    ri, ro = len(rider.inputs), len(rider.out_shape)

    def full_body(*refs):
        own_in, r_in = refs[:n_in], refs[n_in:n_in + ri]
        o0 = n_in + ri
        own_out, r_out = refs[o0:o0 + n_out], refs[o0 + n_out:o0 + n_out + ro]
        s0 = o0 + n_out + ro
        own_scr, send_sems, recv_sems = refs[s0:s0 + n_scr], refs[s0 + n_scr], refs[s0 + n_scr + 1]

        @pl.when(first())
        def _():
            sends, _ = rider.copies(r_in, r_out, send_sems, recv_sems)
            for cp in sends:
                cp.start()

        body(*own_in, *own_out, *own_scr)

        @pl.when(last())
        def _():
            sends, arrivals = rider.copies(r_in, r_out, send_sems, recv_sems)
            for cp in arrivals():
                cp.wait_recv()
            for cp in sends:
                cp.wait_send()

    res = pl.pallas_call(
        full_body, name=name, grid=grid, in_specs=list(in_specs) + [_HBM] * ri, out_specs=list(out_specs) + [_HBM] * ro,
        out_shape=list(out_shape) + rider.out_shape,
        scratch_shapes=list(scratch_shapes) + [pltpu.SemaphoreType.DMA((rider.n_sems,))] * 2,
        input_output_aliases={n_in + i: n_out + o for i, o in rider.aliases.items()},
        compiler_params=_cparams(sem))(*args, *rider.inputs)
    return list(res[:n_out]), list(res[n_out:])


def _ride_gather_send(bufs):
    n = len(bufs)
    halves = [b.shape[0] // 2 for b in bufs]

    def copies(ins, outs, send_sems, recv_sems):
        x, y, c, chips = _place()
        slot = lambda b, cx, cy: _half_rows(outs[b], halves[b], c, lead=2 * cx + cy)
        sends = [_remote(_half_rows(ins[b], halves[b], c), slot(b, x, y), send_sems, recv_sems, 3 * b + j, (cx, cy, c))
                 for b in range(n) for j, (cx, cy) in enumerate(chips)]
        arrivals = lambda: [_remote(slot(b, cx, cy), slot(b, cx, cy), send_sems, recv_sems, 3 * b + j, (x, y, c))
                            for b in range(n) for j, (cx, cy) in enumerate(chips)]
        return sends, arrivals

    return _Rider(bufs, [jax.ShapeDtypeStruct((N_SHARD,) + b.shape, b.dtype) for b in bufs], 3 * n, copies)


def _ride_gather_pass(gathered):
    n = len(gathered)
    halves = [g.shape[1] // 2 for g in gathered]

    def copies(ins, outs, send_sems, recv_sems):
        x, y, c, chips = _place()
        slot = lambda b, cx, cy, hf: _half_rows(outs[b], halves[b], hf, lead=2 * cx + cy)
        sends = [_remote(slot(b, cx, cy, c), slot(b, cx, cy, c), send_sems, recv_sems, 3 * b + j, (x, y, 1 - c))
                 for b in range(n) for j, (cx, cy) in enumerate(chips)]
        arrivals = lambda: [_remote(slot(b, cx, cy, 1 - c), slot(b, cx, cy, 1 - c), send_sems, recv_sems, 3 * b + j, (x, y, c))
                            for b in range(n) for j, (cx, cy) in enumerate(chips)]
        return sends, arrivals

    return _Rider(gathered, [jax.ShapeDtypeStruct(g.shape, g.dtype) for g in gathered], 3 * n, copies,
                  aliases={b: b for b in range(n)})


class _SemSlice:
    def __init__(self, sems, off):
        self.sems, self.off = sems, off

    @property
    def at(self):
        sems, off = self.sems, self.off

        class _At:
            def __getitem__(self, k):
                return sems.at[off + k]

        return _At()


def _join_riders(r1, r2):
    n_in, n_out = len(r1.inputs), len(r1.out_shape)

    def copies(ins, outs, send_sems, recv_sems):
        s1, a1 = r1.copies(ins[:n_in], outs[:n_out], send_sems, recv_sems)
        s2, a2 = r2.copies(ins[n_in:], outs[n_out:], _SemSlice(send_sems, r1.n_sems), _SemSlice(recv_sems, r1.n_sems))
        return s1 + s2, lambda: a1() + a2()

    aliases = dict(r1.aliases)
    aliases.update({n_in + i: n_out + o for i, o in r2.aliases.items()})
    return _Rider(r1.inputs + r2.inputs, r1.out_shape + r2.out_shape, r1.n_sems + r2.n_sems, copies, aliases)


def _ride_small_gather(buf):
    def copies(ins, outs, send_sems, recv_sems):
        x, y, c, _ = _place()
        flip = lambda v, d: 1 - v if d else v
        sends, peers = [], []
        for dx in (0, 1):
            for dy in (0, 1):
                for dc in (0, 1):
                    if dx or dy or dc:
                        k = 4 * dx + 2 * dy + dc - 1
                        px, py, pc = flip(x, dx), flip(y, dy), flip(c, dc)
                        sends.append(_remote(ins[0], outs[0].at[4 * x + 2 * y + c], send_sems, recv_sems, k, (px, py, pc)))
                        peers.append((k, 4 * px + 2 * py + pc))
        arrivals = lambda: [_remote(ins[0], outs[0].at[slot], send_sems, recv_sems, k, (x, y, c)) for k, slot in peers]
        return sends, arrivals

    return _Rider([buf], [jax.ShapeDtypeStruct((8,) + buf.shape, buf.dtype)], 7, copies)


def _small_sum(gathered, buf, me_arr):
    n, r, width = gathered.shape

    def body(me_ref, g_ref, b_ref, o_ref):
        total = jnp.zeros((r, width), F32)
        for d in range(n):
            total = total + jnp.where(me_ref[0] == d, b_ref[...], g_ref[d])
        o_ref[...] = total

    return pl.pallas_call(
        body, name="small_sum", out_shape=jax.ShapeDtypeStruct((r, width), F32),
        grid_spec=pltpu.PrefetchScalarGridSpec(
            num_scalar_prefetch=1, grid=(1,),
            in_specs=[pl.BlockSpec((n, r, width), lambda i, me: (0, 0, 0)), pl.BlockSpec((r, width), lambda i, me: (0, 0))],
            out_specs=pl.BlockSpec((r, width), lambda i, me: (0, 0))),
        compiler_params=_cparams(("arbitrary",)))(me_arr, gathered, buf)


def _ride_pair_exchange(gbufs):
    n = len(gbufs)
    halves = [g.shape[1] // 2 for g in gbufs]

    def copies(ins, outs, send_sems, recv_sems):
        x, y, c, _ = _place()
        sends = [_remote(ins[b].at[:, pl.ds(pl.multiple_of((1 - c) * halves[b], 16), halves[b]), :], outs[b],
                         send_sems, recv_sems, b, (x, y, 1 - c)) for b in range(n)]
        arrivals = lambda: [_remote(outs[b], outs[b], send_sems, recv_sems, b, (x, y, c)) for b in range(n)]
        return sends, arrivals

    return _Rider(gbufs, [jax.ShapeDtypeStruct((N_SHARD, h, g.shape[2]), g.dtype) for g, h in zip(gbufs, halves)], n, copies)


def _ride_chip_exchange(parts):
    n = len(parts)

    def copies(ins, outs, send_sems, recv_sems):
        x, y, c, chips = _place()
        sends = [_remote(ins[b].at[2 * cx + cy], outs[b].at[j], send_sems, recv_sems, 3 * b + j, (cx, cy, c))
                 for b in range(n) for j, (cx, cy) in enumerate(chips)]
        arrivals = lambda: [_remote(ins[b].at[0], outs[b].at[j], send_sems, recv_sems, 3 * b + j, (x, y, c))
                            for b in range(n) for j in range(len(chips))]
        return sends, arrivals

    return _Rider(parts, [jax.ShapeDtypeStruct((3,) + p.shape[1:], p.dtype) for p in parts], 3 * n, copies)


def _gather_shards(bufs, name):
    n = len(bufs)
    halves = [b.shape[0] // 2 for b in bufs]

    def body(*refs):
        ins, outs, send_sems, recv_sems = refs[:n], refs[n:2 * n], refs[2 * n], refs[2 * n + 1]
        x, y, c, chips = _place()
        me, sibling = (x, y, c), (x, y, 1 - c)
        slot = lambda b, cx, cy, hf: _half_rows(outs[b], halves[b], hf, lead=2 * cx + cy)
        first = [_remote(_half_rows(ins[b], halves[b], c), slot(b, x, y, c), send_sems, recv_sems, 6 * b + j, (cx, cy, c))
                 for b in range(n) for j, (cx, cy) in enumerate(chips)]
        for cp in first:
            cp.start()
        passed = []
        for j, (cx, cy) in enumerate(chips):
            for b in range(n):
                _remote(slot(b, cx, cy, c), slot(b, cx, cy, c), send_sems, recv_sems, 6 * b + j, me).wait_recv()
                fwd = _remote(slot(b, cx, cy, c), slot(b, cx, cy, c), send_sems, recv_sems, 6 * b + 3 + j, sibling)
                fwd.start()
                passed.append(fwd)
        for j, (cx, cy) in enumerate(chips):
            for b in range(n):
                _remote(slot(b, cx, cy, 1 - c), slot(b, cx, cy, 1 - c), send_sems, recv_sems, 6 * b + 3 + j, me).wait_recv()
        for cp in first + passed:
            cp.wait_send()

    return pl.pallas_call(
        body, name=name, out_shape=[jax.ShapeDtypeStruct((N_SHARD,) + b.shape, b.dtype) for b in bufs],
        in_specs=[_HBM] * n, out_specs=[_HBM] * n,
        scratch_shapes=[pltpu.SemaphoreType.DMA((6 * n,)), pltpu.SemaphoreType.DMA((6 * n,))],
    )(*bufs)


def _reduce_pair_exchange(gbufs, name):
    n = len(gbufs)
    halves = [g.shape[1] // 2 for g in gbufs]

    def body(*refs):
        ins, outs, send_sems, recv_sems = refs[:n], refs[n:2 * n], refs[2 * n], refs[2 * n + 1]
        x, y, c, _ = _place()
        cps = [_remote(ins[b].at[:, pl.ds(pl.multiple_of((1 - c) * halves[b], 16), halves[b]), :], outs[b],
                       send_sems, recv_sems, b, (x, y, 1 - c)) for b in range(n)]
        for cp in cps:
            cp.start()
        for cp in cps:
            cp.wait()

    return pl.pallas_call(
        body, name=name,
        out_shape=[jax.ShapeDtypeStruct((N_SHARD, h, g.shape[2]), g.dtype) for g, h in zip(gbufs, halves)],
        in_specs=[_HBM] * n, out_specs=[_HBM] * n,
        scratch_shapes=[pltpu.SemaphoreType.DMA((n,)), pltpu.SemaphoreType.DMA((n,))],
    )(*gbufs)


def _pair_add(gbuf, recv, c_arr, tr, name):
    _, rows, width = gbuf.shape
    half = rows // 2
    nt = half // tr

    def body(c_ref, a_ref, b_ref, o_ref):
        o_ref[...] = (a_ref[...] + b_ref[...]).astype(o_ref.dtype)

    blk = lambda f: pl.BlockSpec((None, tr, width), f)
    return pl.pallas_call(
        body, name=name, out_shape=jax.ShapeDtypeStruct((N_SHARD, half, width), jnp.bfloat16),
        grid_spec=pltpu.PrefetchScalarGridSpec(
            num_scalar_prefetch=1, grid=(N_SHARD, nt),
            in_specs=[blk(lambda s, i, c: (s, c[0] * nt + i, 0)), blk(lambda s, i, c: (s, i, 0))],
            out_specs=blk(lambda s, i, c: (s, i, 0))),
        compiler_params=_cparams(("parallel", "parallel")))(c_arr, gbuf, recv)


def _reduce_chip_exchange(parts, name):
    n = len(parts)

    def body(*refs):
        ins, outs, send_sems, recv_sems = refs[:n], refs[n:2 * n], refs[2 * n], refs[2 * n + 1]
        x, y, c, chips = _place()
        sends = [_remote(ins[b].at[2 * cx + cy], outs[b].at[j], send_sems, recv_sems, 3 * b + j, (cx, cy, c))
                 for b in range(n) for j, (cx, cy) in enumerate(chips)]
        for cp in sends:
            cp.start()
        for b in range(n):
            for j in range(len(chips)):
                _remote(ins[b].at[0], outs[b].at[j], send_sems, recv_sems, 3 * b + j, (x, y, c)).wait_recv()
        for cp in sends:
            cp.wait_send()

    return pl.pallas_call(
        body, name=name, out_shape=[jax.ShapeDtypeStruct((3,) + p.shape[1:], p.dtype) for p in parts],
        in_specs=[_HBM] * n, out_specs=[_HBM] * n,
        scratch_shapes=[pltpu.SemaphoreType.DMA((3 * n,)), pltpu.SemaphoreType.DMA((3 * n,))],
    )(*parts)


def _chip_add(part, recv, me_arr, tr, name):
    _, half, width = part.shape

    def body(me_ref, own, a0, a1, a2, o_ref):
        f = lambda r: r[...].astype(F32)
        o_ref[...] = ((f(own) + f(a0)) + f(a1)) + f(a2)

    specs = [pl.BlockSpec((None, tr, width), lambda i, me: (me[0], i, 0))]
    specs += [pl.BlockSpec((None, tr, width), functools.partial(lambda i, me, k: (k, i, 0), k=k)) for k in range(3)]
    return pl.pallas_call(
        body, name=name, out_shape=jax.ShapeDtypeStruct((half, width), F32),
        grid_spec=pltpu.PrefetchScalarGridSpec(
            num_scalar_prefetch=1, grid=(half // tr,), in_specs=specs,
            out_specs=pl.BlockSpec((tr, width), lambda i, me: (i, 0))),
        compiler_params=_cparams(("parallel",)))(me_arr, part, recv, recv, recv)


def _reduce_pair_share(rhalves, name):
    n = len(rhalves)

    def body(*refs):
        ins, outs, send_sems, recv_sems = refs[:n], refs[n:2 * n], refs[2 * n], refs[2 * n + 1]
        x, y, c, _ = _place()
        cps = [_remote(ins[b], outs[b], send_sems, recv_sems, b, (x, y, 1 - c)) for b in range(n)]
        for cp in cps:
            cp.start()
        for cp in cps:
            cp.wait()

    return pl.pallas_call(
        body, name=name, out_shape=[jax.ShapeDtypeStruct(r.shape, r.dtype) for r in rhalves],
        in_specs=[_HBM] * n, out_specs=[_HBM] * n,
        scratch_shapes=[pltpu.SemaphoreType.DMA((n,)), pltpu.SemaphoreType.DMA((n,))],
    )(*rhalves)


def _small_allreduce(buf):
    r, width = buf.shape
    n_dev = 8

    def body(x_ref, all_ref, sum_ref, send_sems, recv_sems, local_sem):
        x, y, c, chips = _place()
        me, sibling = (x, y, c), (x, y, 1 - c)

        def rows(px, py, pc):
            return all_ref.at[pl.ds(pl.multiple_of((4 * px + 2 * py + pc) * r, 8), r), :]

        def copy(k, block, to, src=None):
            return _remote(rows(*block) if src is None else src, rows(*block), send_sems, recv_sems, k, to)

        mine = pltpu.make_async_copy(x_ref, rows(*me), local_sem)
        mine.start()
        first = [copy(0, me, sibling, src=x_ref)]
        first += [copy(1 + j, me, (*chip, c), src=x_ref) for j, chip in enumerate(chips)]
        for cp in first:
            cp.start()
        passed = [copy(4 + j, (*chip, c), sibling) for j, chip in enumerate(chips)]
        for j, chip in enumerate(chips):
            copy(1 + j, (*chip, c), me).wait_recv()
            passed[j].start()
        copy(0, sibling, me).wait_recv()
        for j, chip in enumerate(chips):
            copy(4 + j, (*chip, 1 - c), me).wait_recv()
        for cp in first + passed:
            cp.wait_send()
        mine.wait()
        total = all_ref[0:r, :]
        for k in range(1, n_dev):
            total = total + all_ref[k * r:(k + 1) * r, :]
        sum_ref[...] = total

    vm = pl.BlockSpec(memory_space=pltpu.VMEM)
    _, total = pl.pallas_call(
        body, name="small_allreduce",
        out_shape=[jax.ShapeDtypeStruct((n_dev * r, width), buf.dtype), jax.ShapeDtypeStruct((r, width), buf.dtype)],
        in_specs=[vm], out_specs=[vm, vm],
        scratch_shapes=[pltpu.SemaphoreType.DMA((7,)), pltpu.SemaphoreType.DMA((7,)), pltpu.SemaphoreType.DMA],
    )(buf)
    return total


def _row_tile(rows, cap):
    if rows <= cap:
        return rows
    t = (cap // 8) * 8
    while t >= 8:
        if rows % t == 0:
            return t
        t -= 8
    return rows


def _adamw(w, g, m, v, name):
    shape = w.shape
    cols = shape[-1] if len(shape) <= 3 else shape[-2] * shape[-1]
    lead = len(shape) == 3
    w2, g2, m2, v2 = (a if lead else a.reshape(-1, cols) for a in (w, g, m, v))
    rows = shape[1] if lead else w2.shape[0]
    tr, tc = _row_tile(rows, 256), cols
    if tr == rows and rows > 256:
        tc = _tile(cols, 256)

    def body(w_ref, g_ref, m_ref, v_ref, d_ref, mo_ref, vo_ref):
        gv = g_ref[...]
        mn = ADAM_B1 * m_ref[...] + (1.0 - ADAM_B1) * gv
        vn = ADAM_B2 * v_ref[...] + (1.0 - ADAM_B2) * (gv * gv)
        m_hat = mn / (1.0 - ADAM_B1 ** ADAM_STEP)
        v_hat = vn / (1.0 - ADAM_B2 ** ADAM_STEP)
        d_ref[...] = -ADAM_LR * (m_hat / (jnp.sqrt(v_hat) + ADAM_EPS) + ADAM_WD * w_ref[...])
        mo_ref[...] = mn
        vo_ref[...] = vn

    blk = (pl.BlockSpec((None, tr, tc), lambda i, j: (0, i, j)) if lead else pl.BlockSpec((tr, tc), lambda i, j: (i, j)))
    outs = pl.pallas_call(
        body, name=name, grid=(rows // tr, cols // tc), in_specs=[blk] * 4, out_specs=[blk] * 3,
        out_shape=[jax.ShapeDtypeStruct(w2.shape, F32)] * 3,
        compiler_params=_cparams(("parallel", "parallel")))(w2, g2, m2, v2)
    return tuple(o.reshape(shape) for o in outs)


_WEIGHT_NAMES = ('w_in', 'conv_w', 'dn_a_log', 'dn_dt_bias', 'dn_norm_w', 'q_norm_w', 'w_uq', 'kv_norm_w', 'w_uk',
                 'w_uv', 'w_br_dn', 'w_br_mla', 'w_o', 'ln1_g', 'ln1_b', 'w_ffn_in', 'w_ffn_out', 'w_ple',
                 'w_ple_gate', 'ln2_g', 'ln2_b')
_SMALL_NAMES = ('ln1_g', 'ln1_b', 'ln2_g', 'ln2_b', 'q_norm_w', 'kv_norm_w', 'dn_norm_w', 'dn_a_log', 'dn_dt_bias')
_SMALL_GROUP = 8
_CONV_SMALL_ROW = len(_SMALL_NAMES) * _SMALL_GROUP
_CONV_SMALL_ROWS = DN_CONV * QKV_W // FLAT_W


def _pack_small(gw):
    rows = [jnp.pad(gw[n][None, :], ((0, _SMALL_GROUP - 1), (0, FLAT_W - gw[n].shape[0]))) for n in _SMALL_NAMES]
    rows.append(jnp.pad(gw['conv_w'].reshape(_CONV_SMALL_ROWS, FLAT_W), ((0, SMALL_ROWS - _CONV_SMALL_ROW - _CONV_SMALL_ROWS), (0, 0))))
    return jnp.concatenate(rows, axis=0)


class _Exchange:
    def __init__(self, local, me_chip, c_arr):
        self.local, self.me_chip, self.c_arr = local, me_chip, c_arr
        self.parts = self.arrived = None

    def gather_send(self):
        return _ride_gather_send(self.local)

    def gather_pass(self, sent):
        return _ride_gather_pass(sent)

    def weights(self, gathered):
        return _prep_weights(_unpack_rest(gathered, self.local, self.me_chip))

    def pair_send(self, g):
        self.gbufs = _pack_grads_rest(_unprep_grads_early(g))
        return _ride_pair_exchange(self.gbufs)

    def reduce_send(self, got):
        self.parts = [_pair_add(g_, r_, self.c_arr, tr, "pair_add_%d" % (i + 1))
                      for i, (g_, r_, tr) in enumerate(zip(self.gbufs, got, _ADD_TILES[1:]))]
        return _ride_chip_exchange(self.parts)

    def reduce_arrived(self, arrived):
        self.arrived = list(arrived)

    def in_send(self, g):
        g_in = [_shard_columns(_w_in_grad(g), W_IN_SHARD)]
        got = _reduce_pair_exchange(g_in, "reduce_pair_exchange_w_in")
        self.part_in = _pair_add(g_in[0], got[0], self.c_arr, _ADD_TILES[0], "pair_add_0")
        return _ride_chip_exchange([self.part_in])

    def in_arrived(self, arrived):
        self.arrived_in = list(arrived)

    def small_send(self, g):
        self.small = _pack_small(_unprep_grads_late(g))
        return _ride_small_gather(self.small)

    def small_arrived(self, arrived):
        self.small_gathered = arrived[0]


def kernel(x, p, positions, w_in, conv_w, dn_a_log, dn_dt_bias, dn_norm_w, q_norm_w, w_uq, kv_norm_w, w_uk, w_uv, w_br_dn, w_br_mla, w_o, ln1_g, ln1_b, w_ffn_in, w_ffn_out, w_ple, w_ple_gate, ln2_g, ln2_b, loss_target, m_w_in, m_conv_w, m_dn_a_log, m_dn_dt_bias, m_dn_norm_w, m_q_norm_w, m_w_uq, m_kv_norm_w, m_w_uk, m_w_uv, m_w_br_dn, m_w_br_mla, m_w_o, m_ln1_g, m_ln1_b, m_w_ffn_in, m_w_ffn_out, m_w_ple, m_w_ple_gate, m_ln2_g, m_ln2_b, v_w_in, v_conv_w, v_dn_a_log, v_dn_dt_bias, v_dn_norm_w, v_q_norm_w, v_w_uq, v_kv_norm_w, v_w_uk, v_w_uv, v_w_br_dn, v_w_br_mla, v_w_o, v_ln1_g, v_ln1_b, v_w_ffn_in, v_w_ffn_out, v_w_ple, v_w_ple_gate, v_ln2_g, v_ln2_b):
    ws = dict(w_in=w_in, conv_w=conv_w, dn_a_log=dn_a_log, dn_dt_bias=dn_dt_bias, dn_norm_w=dn_norm_w, q_norm_w=q_norm_w,
              w_uq=w_uq, kv_norm_w=kv_norm_w, w_uk=w_uk, w_uv=w_uv, w_br_dn=w_br_dn, w_br_mla=w_br_mla, w_o=w_o,
              ln1_g=ln1_g, ln1_b=ln1_b, w_ffn_in=w_ffn_in, w_ffn_out=w_ffn_out, w_ple=w_ple, w_ple_gate=w_ple_gate,
              ln2_g=ln2_g, ln2_b=ln2_b)
    ms = dict(w_in=m_w_in, conv_w=m_conv_w, dn_a_log=m_dn_a_log, dn_dt_bias=m_dn_dt_bias, dn_norm_w=m_dn_norm_w,
              q_norm_w=m_q_norm_w, w_uq=m_w_uq, kv_norm_w=m_kv_norm_w, w_uk=m_w_uk, w_uv=m_w_uv, w_br_dn=m_w_br_dn,
              w_br_mla=m_w_br_mla, w_o=m_w_o, ln1_g=m_ln1_g, ln1_b=m_ln1_b, w_ffn_in=m_w_ffn_in, w_ffn_out=m_w_ffn_out,
              w_ple=m_w_ple, w_ple_gate=m_w_ple_gate, ln2_g=m_ln2_g, ln2_b=m_ln2_b)
    vs = dict(w_in=v_w_in, conv_w=v_conv_w, dn_a_log=v_dn_a_log, dn_dt_bias=v_dn_dt_bias, dn_norm_w=v_dn_norm_w,
              q_norm_w=v_q_norm_w, w_uq=v_w_uq, kv_norm_w=v_kv_norm_w, w_uk=v_w_uk, w_uv=v_w_uv, w_br_dn=v_w_br_dn,
              w_br_mla=v_w_br_mla, w_o=v_w_o, ln1_g=v_ln1_g, ln1_b=v_ln1_b, w_ffn_in=v_w_ffn_in, w_ffn_out=v_w_ffn_out,
              w_ple=v_w_ple, w_ple_gate=v_w_ple_gate, ln2_g=v_ln2_g, ln2_b=v_ln2_b)
    mx, my, mc = lax.axis_index("x"), lax.axis_index("y"), lax.axis_index("c")

    me_chip = 2 * mx + my
    c_arr = jnp.reshape(mc, (1,)).astype(jnp.int32)
    me_arr = jnp.reshape(me_chip, (1,)).astype(jnp.int32)
    sharded = ('w_in', 'w_ffn_in') + tuple(name for name, _, _ in _FLATB_PIECES)

    local = _pack_shards({name: ws[name][0] for name in sharded}, conv_w[0])
    (gathered_in,) = _gather_shards(local[:1], "gather_w_in")
    w_in_full, conv_full = _unpack_w_in(gathered_in, local[0], me_chip)
    small = {n: ws[n][0] for n in _SMALL_NAMES}
    small['conv_w'] = conv_full
    sp = _prep_small(small)
    cosb, sinb = _rope_tables(positions[0])
    exch = _Exchange(local[1:], me_chip, c_arr)

    loss_lanes, dx, g = _local_step(x[0], p[0, 0], cosb, sinb, loss_target[0], _prep_w_in(w_in_full), sp, exch)
    loss = lax.psum(jnp.sum(loss_lanes), ("x", "y", "c"))

    parts = [exch.part_in] + exch.parts
    arrived = exch.arrived_in + exch.arrived
    mine = [_chip_add(p_, r_, me_arr, tr, "chip_add_%d" % i) for i, (p_, r_, tr) in enumerate(zip(parts, arrived, _ADD_TILES))]
    reduced = _unpack_reduced(mine, _reduce_pair_share(mine, "reduce_pair_share"), mc)
    tot = _small_sum(exch.small_gathered, exch.small, jnp.reshape(4 * mx + 2 * my + mc, (1,)).astype(jnp.int32))
    gred = {name: reduced[name][None] for name in sharded}
    for i, n in enumerate(_SMALL_NAMES):
        gred[n] = tot[i * _SMALL_GROUP, :ws[n].shape[1]][None]
    conv_tot = tot[_CONV_SMALL_ROW:_CONV_SMALL_ROW + _CONV_SMALL_ROWS].reshape(DN_CONV, QKV_W)
    gred['conv_w'] = lax.dynamic_slice_in_dim(conv_tot, (2 * mx + my) * _CONV_SHARD, _CONV_SHARD, axis=1)[None]

    deltas, new_m, new_v = {}, {}, {}
    for n in _WEIGHT_NAMES:
        if n == 'w_in':
            tr_ = lambda a: jnp.transpose(a, (0, 2, 1))
            g_t = tr_(gred[n].reshape(ws[n].shape))
            outs = _adamw(tr_(ws[n]), g_t, tr_(ms[n]), tr_(vs[n]), "adamw_" + n)
            gred[n] = tr_(g_t)
            deltas[n], new_m[n], new_v[n] = (tr_(o) for o in outs)
            continue
        gred[n] = gred[n].reshape(ws[n].shape)
        deltas[n], new_m[n], new_v[n] = _adamw(ws[n], gred[n], ms[n], vs[n], "adamw_" + n)
    return (loss, dx[None], *[gred[n] for n in _WEIGHT_NAMES], *[deltas[n] for n in _WEIGHT_NAMES],
            *[new_m[n] for n in _WEIGHT_NAMES], *[new_v[n] for n in _WEIGHT_NAMES])
```

```python
import functools
import math

import jax
import jax.numpy as jnp
from jax import lax
from jax.experimental import pallas as pl
from jax.experimental.pallas import tpu as pltpu

F32 = jnp.float32
_CDT = jnp.bfloat16
_HI = lax.Precision.HIGHEST
_MESH = pl.DeviceIdType.MESH

D_MODEL = 1024
PLE_DIM = 256
HEADS = 8
DN_DK = 128
DN_CHUNK = 64
DN_CONV = 4
QKV_W = 3 * HEADS * DN_DK
Q_LORA = 384
KV_LORA = 256
NOPE = 128
ROPE = 64
ROPE_PAD = 128
FFN_HIDDEN = 2816
D_IN = 6864
ROPE_BASE = 10000.0
ALPHA = 2.0 ** 0.25
ATT_SCALE = (NOPE + ROPE) ** -0.5
NEG_BIG = -1e30
ADAM_LR, ADAM_B1, ADAM_B2, ADAM_EPS, ADAM_WD, ADAM_STEP = 0.001, 0.9, 0.999, 1e-08, 0.01, 10

LANE = 128
VMEM_LIMIT = 56 * 1024 * 1024
MM_VMEM_BUDGET = 40 * 1024 * 1024
N_SHARD = 4
FLAT_W = 1024
SMALL_ROWS = 96


def _tile(dim, cap):
    if dim <= cap:
        return dim
    t = (cap // LANE) * LANE
    while t >= LANE:
        if dim % t == 0:
            return t
        t -= LANE
    return dim


def _cparams(sem):
    return pltpu.CompilerParams(dimension_semantics=sem, vmem_limit_bytes=VMEM_LIMIT)


def _mm(a, b, *, name, ta=False, tb=False, add=None, add_scale=1.0, out_dtype=F32, heads=None,
        a_head=None, b_head=None, out_head=None, dims=None, tm=1408, tn=1408, rider=None):
    m, n, k = dims
    tm, tn = _tile(m, tm), _tile(n, tn)
    sa, sb, so = a.dtype.itemsize, b.dtype.itemsize, jnp.dtype(out_dtype).itemsize

    def vmem_need(tk_):
        acc = tm * tn * 4 if tk_ < k else 0
        extra = 2 * tm * tn * 4 if add is not None else 0
        return 2 * (tm * tk_ * sa + tk_ * tn * sb) + 2 * tm * tn * so + acc + extra

    tk = k
    while vmem_need(tk) > MM_VMEM_BUDGET and tk > LANE:
        smaller = _tile(k, tk - LANE)
        if smaller >= tk:
            break
        tk = smaller
    nk = k // tk
    hgrid = () if heads is None else (heads,)
    off = len(hgrid)

    def spec(rows, cols, rtile, ctile, rsel, csel, layout):
        def idx(*g):
            h = g[0] if off else 0
            ri, ci = g[off + rsel], g[off + csel]
            if layout == 'lead':
                return (h, ri, ci)
            if layout == 'col':
                return (ri, h * (cols // ctile) + ci)
            return (ri, ci)
        if layout == 'lead':
            return pl.BlockSpec((None, rtile, ctile), idx)
        return pl.BlockSpec((rtile, ctile), idx)

    a_spec = spec(k, m, tk, tm, 2, 0, a_head) if ta else spec(m, k, tm, tk, 0, 2, a_head)
    b_spec = spec(n, k, tn, tk, 1, 2, b_head) if tb else spec(k, n, tk, tn, 2, 1, b_head)
    o_spec = spec(m, n, tm, tn, 0, 1, out_head)
    in_specs = [a_spec, b_spec]
    args = [a, b]
    if add is not None:
        in_specs.append(spec(m, n, tm, tn, 0, 1, out_head))
        args.append(add)
    dn = (((0 if ta else 1,), (1 if tb else 0,)), ((), ()))

    def body(*refs):
        a_ref, b_ref = refs[0], refs[1]
        prod = lax.dot_general(a_ref[...].astype(_CDT), b_ref[...].astype(_CDT), dn, preferred_element_type=F32)
        if nk == 1:
            o_ref = refs[-1]
            if add is not None:
                prod = prod + refs[2][...].astype(F32) * add_scale
            o_ref[...] = prod.astype(out_dtype)
            return
        o_ref, acc_ref = refs[-2], refs[-1]
        kk = pl.program_id(off + 2)

        @pl.when(kk == 0)
        def _():
            if add is not None:
                acc_ref[...] = refs[2][...].astype(F32) * add_scale
            else:
                acc_ref[...] = jnp.zeros_like(acc_ref)

        acc_ref[...] += prod

        @pl.when(kk == nk - 1)
        def _():
            o_ref[...] = acc_ref[...].astype(out_dtype)

    if out_head == 'lead':
        oshape = (heads, m, n)
    elif out_head == 'col':
        oshape = (m, heads * n)
    else:
        oshape = (m, n)
    grid = hgrid + (m // tm, n // tn, nk)
    scratch = [pltpu.VMEM((tm, tn), F32)] if nk > 1 else []
    if rider is not None:
        (out,), carried = _carried_call(
            body, rider, *_grid_ends(grid), name=name, grid=grid, in_specs=in_specs, out_specs=[o_spec],
            out_shape=[jax.ShapeDtypeStruct(oshape, out_dtype)], scratch_shapes=scratch,
            sem=("arbitrary",) * len(grid), args=tuple(args))
        return out, carried
    sem = ("parallel",) * (off + 2) + ("arbitrary",)
    return pl.pallas_call(
        body, name=name, grid=grid, in_specs=in_specs, out_specs=o_spec,
        out_shape=jax.ShapeDtypeStruct(oshape, out_dtype), scratch_shapes=scratch,
        compiler_params=_cparams(sem))(*args)


def _mm2(a, b, **kw):
    ta, tb = kw.get('ta', False), kw.get('tb', False)
    m = a.shape[1] if ta else a.shape[0]
    k = a.shape[0] if ta else a.shape[1]
    n = b.shape[0] if tb else b.shape[1]
    return _mm(a, b, dims=(m, n, k), **kw)


def _rowwise(fn, rows, bcast, outs, reds=(), *, name, tm=256, heads=None):
    t = rows[0][0].shape[0]
    tm = min(tm, t)
    hn = 1 if heads is None else heads
    in_specs, args = [], []
    for arr, width, base, per_head in rows:
        in_specs.append(pl.BlockSpec((tm, width), functools.partial(
            lambda i, h, base, per_head: (i, base + (h if per_head else 0)), base=base, per_head=per_head)))
        args.append(arr)
    for arr in bcast:
        in_specs.append(pl.BlockSpec(arr.shape, lambda i, h: (0, 0)))
        args.append(arr)
    out_specs, out_shape = [], []
    for total, width, per_head, dt in outs:
        out_specs.append(pl.BlockSpec((tm, width), functools.partial(
            lambda i, h, per_head: (i, h if per_head else 0), per_head=per_head)))
        out_shape.append(jax.ShapeDtypeStruct((t, total), dt))
    for shp in reds:
        out_specs.append(pl.BlockSpec(shp, lambda i, h: (0, 0)))
        out_shape.append(jax.ShapeDtypeStruct(shp, F32))
    n_in, n_out, n_red = len(args), len(outs), len(reds)

    def body(*refs):
        i, h = pl.program_id(0), pl.program_id(1)
        vals = fn(h, *[r[...] for r in refs[:n_in]])
        for r, v in zip(refs[n_in:n_in + n_out], vals[:n_out]):
            r[...] = v.astype(r.dtype)
        if n_red:
            @pl.when((i == 0) & (h == 0))
            def _():
                for r in refs[n_in + n_out:]:
                    r[...] = jnp.zeros_like(r)
            for r, v in zip(refs[n_in + n_out:], vals[n_out:]):
                r[...] += v

    sem = ("arbitrary", "arbitrary") if n_red else ("parallel", "parallel")
    res = pl.pallas_call(body, name=name, grid=(t // tm, hn), in_specs=in_specs, out_specs=out_specs,
                         out_shape=out_shape, compiler_params=_cparams(sem))(*args)
    return tuple(res)


def _sigmoid(x):
    return 1.0 / (1.0 + jnp.exp(-x))


def _silu(x):
    return x * _sigmoid(x)


def _softplus(x):
    return jnp.maximum(x, 0.0) + jnp.log(1.0 + jnp.exp(-jnp.abs(x)))


def _layer_norm(t, g, b):
    mu = jnp.mean(t, axis=-1, keepdims=True)
    d = t - mu
    var = jnp.mean(d * d, axis=-1, keepdims=True)
    return d * lax.rsqrt(var + 1e-5) * g + b


def _rms_norm(t, w):
    return t * lax.rsqrt(jnp.mean(t * t, axis=-1, keepdims=True) + 1e-6) * w


def _swap_rope_halves(t):
    lane = lax.broadcasted_iota(jnp.int32, t.shape, 1) % ROPE_PAD
    n = t.shape[1]
    up = pltpu.roll(t, n - ROPE // 2, axis=1)
    dn = pltpu.roll(t, ROPE // 2, axis=1)
    return jnp.where(lane < ROPE // 2, up, jnp.where(lane < ROPE, dn, 0.0))


def _rope(t, cosb, sinb):
    reps = t.shape[1] // ROPE_PAD
    c = jnp.tile(cosb, (1, reps)) if reps > 1 else cosb
    s = jnp.tile(sinb, (1, reps)) if reps > 1 else sinb
    return t * c + _swap_rope_halves(t) * s


def _rope_bwd(d, cosb, sinb):
    reps = d.shape[1] // ROPE_PAD
    c = jnp.tile(cosb, (1, reps)) if reps > 1 else cosb
    s = jnp.tile(sinb, (1, reps)) if reps > 1 else sinb
    return d * c + _swap_rope_halves(d * s)


_CONV_ROWS = 256
_CONV_COLS = 256


def _conv_window(ref, r0, lo, hi, t):
    parts = []
    start, stop = r0 - lo, r0 + _CONV_ROWS + hi
    if start < 0:
        parts.append(jnp.zeros((-start, ref.shape[1]), F32))
        start = 0
    tail = max(stop - t, 0)
    parts.append(ref[start:stop - tail, :].astype(F32))
    if tail:
        parts.append(jnp.zeros((tail, ref.shape[1]), F32))
    return parts[0] if len(parts) == 1 else jnp.concatenate(parts, axis=0)


def _conv_taps(win, w_ref, n_out):
    acc = win[8:8 + n_out] * w_ref[DN_CONV - 1:DN_CONV, :]
    for i in range(DN_CONV - 1):
        acc = acc + pltpu.roll(win, DN_CONV - 1 - i, axis=0)[8:8 + n_out] * w_ref[i:i + 1, :]
    return acc


def _conv_silu(x, w):
    t, ch = x.shape

    def body(x_ref, w_ref, o_ref):
        for r in range(t // _CONV_ROWS):
            r0 = r * _CONV_ROWS
            c = _conv_taps(_conv_window(x_ref, r0, 8, 0, t), w_ref, _CONV_ROWS)
            o_ref[r0:r0 + _CONV_ROWS, :] = _silu(c)

    return pl.pallas_call(
        body, name="conv_silu", grid=(ch // _CONV_COLS,),
        in_specs=[pl.BlockSpec((t, _CONV_COLS), lambda j: (0, j)), pl.BlockSpec((DN_CONV, _CONV_COLS), lambda j: (0, j))],
        out_specs=pl.BlockSpec((t, _CONV_COLS), lambda j: (0, j)),
        out_shape=jax.ShapeDtypeStruct((t, ch), F32), compiler_params=_cparams(("parallel",)))(x, w)


def _conv_silu_bwd(x, w, dys):
    t, ch = x.shape
    per = ch // len(dys) // _CONV_COLS

    def body(x_ref, w_ref, *rest):
        dy_refs, (dx_ref, dw_ref) = rest[:len(dys)], rest[len(dys):]
        sec = pl.program_id(0) // per
        dws = [jnp.zeros((1, _CONV_COLS), F32) for _ in range(DN_CONV)]
        for r in range(t // _CONV_ROWS):
            r0 = r * _CONV_ROWS
            n_ext = _CONV_ROWS + 8
            xw = _conv_window(x_ref, r0, 8, 8, t)
            c = _conv_taps(xw, w_ref, n_ext)
            sg = _sigmoid(c)
            dy = _conv_window(dy_refs[-1], r0, 0, 8, t)
            for k in range(len(dys) - 2, -1, -1):
                dy = jnp.where(sec == k, _conv_window(dy_refs[k], r0, 0, 8, t), dy)
            ds = dy * (sg * (1.0 + c * (1.0 - sg)))
            x0 = xw[8:8 + _CONV_ROWS]
            dx = jnp.zeros((_CONV_ROWS, _CONV_COLS), F32)
            for i in range(DN_CONV):
                sh = DN_CONV - 1 - i
                ds_up = (ds if sh == 0 else pltpu.roll(ds, n_ext - sh, axis=0))[:_CONV_ROWS]
                dx = dx + ds_up * w_ref[i:i + 1, :]
                dws[i] = dws[i] + jnp.sum(x0 * ds_up, axis=0, keepdims=True)
            dx_ref[r0:r0 + _CONV_ROWS, :] = dx.astype(dx_ref.dtype)
        for i in range(DN_CONV):
            dw_ref[i:i + 1, :] = dws[i]

    blk = pl.BlockSpec((t, _CONV_COLS), lambda j: (0, j))
    wblk = pl.BlockSpec((DN_CONV, _CONV_COLS), lambda j: (0, j))
    dy_specs = [pl.BlockSpec((t, _CONV_COLS), functools.partial(lambda j, k: (0, jnp.clip(j - k * per, 0, per - 1)), k=k))
                for k in range(len(dys))]
    return pl.pallas_call(
        body, name="conv_silu_bwd", grid=(ch // _CONV_COLS,), in_specs=[blk, wblk] + dy_specs, out_specs=[blk, wblk],
        out_shape=[jax.ShapeDtypeStruct((t, ch), _CDT), jax.ShapeDtypeStruct((DN_CONV, ch), F32)],
        compiler_params=_cparams(("arbitrary",)))(x, w, *dys)


_PA_ROWS = 512


def _bmm(a, b, spec, exact=False):
    if exact:
        return jnp.einsum(spec, a, b, precision=_HI, preferred_element_type=F32)
    return jnp.einsum(spec, a.astype(_CDT), b.astype(_CDT), preferred_element_type=F32)


def _split16(a):
    hi = a.astype(jnp.bfloat16)
    return hi, (a - hi.astype(F32)).astype(jnp.bfloat16)


def _bmm3(a, b, spec):
    ah, al = _split16(a)
    bh, bl = _split16(b)
    e = lambda p, q: jnp.einsum(spec, p, q, preferred_element_type=F32)
    return e(ah, bh) + (e(ah, bl) + e(al, bh))


def _split3(b):
    b0 = b.astype(jnp.bfloat16)
    r1 = b - b0.astype(F32)
    b1 = r1.astype(jnp.bfloat16)
    return b0, b1, (r1 - b1.astype(F32)).astype(jnp.bfloat16)


@functools.partial(jax.custom_vjp, nondiff_argnums=(2, 3))
def _select_mm(sel, b, spec, spec_t):
    return sum(jnp.einsum(spec, sel, t, preferred_element_type=F32) for t in _split3(b))


def _select_mm_fwd(sel, b, spec, spec_t):
    return _select_mm(sel, b, spec, spec_t), sel


def _select_mm_bwd(spec, spec_t, sel, ct):
    return jnp.zeros_like(sel), sum(jnp.einsum(spec_t, sel, t, preferred_element_type=F32) for t in _split3(ct))


_select_mm.defvjp(_select_mm_fwd, _select_mm_bwd)


def _tri_inverse(l_mat, eye):
    pw = -l_mat
    t_inv = eye + pw
    for _ in range(5):
        pw = _bmm3(pw, pw, 'bij,bjk->bik')
        t_inv = t_inv + _bmm3(t_inv, pw, 'bij,bjk->bik')
    return t_inv


@jax.custom_vjp
def _tri_inverse_saved(l_mat, t_saved):
    return t_saved


def _tri_inverse_saved_fwd(l_mat, t_saved):
    return t_saved, t_saved


def _tri_inverse_saved_bwd(t_saved, dt):
    left = _bmm3(t_saved, dt, 'bji,bjk->bik')
    return -_bmm3(left, t_saved, 'bij,bkj->bik'), jnp.zeros_like(t_saved)


_tri_inverse_saved.defvjp(_tri_inverse_saved_fwd, _tri_inverse_saved_bwd)


def _phase_a(h, q, k, v, ba, alog, dtb, t_saved=None):
    r = q.shape[0]
    nb = r // DN_CHUNK
    c = DN_CHUNK
    lane = lax.broadcasted_iota(jnp.int32, (1, LANE), 1)
    selb = (lane == h).astype(F32)
    sela = (lane == h + HEADS).astype(F32)
    b_raw = jnp.sum(ba * selb, axis=1, keepdims=True)
    a_raw = jnp.sum(ba * sela, axis=1, keepdims=True)
    al = jnp.sum(alog * selb, axis=1, keepdims=True)
    dt = jnp.sum(dtb * selb, axis=1, keepdims=True)
    beta = jnp.broadcast_to(_sigmoid(b_raw), (r, LANE))
    g = jnp.broadcast_to(-jnp.exp(al) * _softplus(a_raw + dt), (r, LANE))
    qn = q * lax.rsqrt(jnp.sum(q * q, -1, keepdims=True) + 1e-6) * (DN_DK ** -0.5)
    kn = k * lax.rsqrt(jnp.sum(k * k, -1, keepdims=True) + 1e-6)
    q3, k3, v3 = qn.reshape(nb, c, LANE), kn.reshape(nb, c, LANE), v.reshape(nb, c, LANE)
    b3, g3 = beta.reshape(nb, c, LANE), g.reshape(nb, c, LANE)
    ri = lax.broadcasted_iota(jnp.int32, (nb, c, c), 1)
    ci = lax.broadcasted_iota(jnp.int32, (nb, c, c), 2)
    tril, strict = ri >= ci, ri > ci
    gc = _select_mm(tril.astype(jnp.bfloat16), g3, 'bij,bjd->bid', 'bij,bid->bjd')
    onehot = (lax.broadcasted_iota(jnp.int32, (nb, c, LANE), 2) == 0).astype(jnp.bfloat16)
    g_row = _select_mm(onehot, gc, 'bid,bjd->bij', 'bid,bij->bjd')
    diff = gc[:, :, :c] - g_row
    decay = jnp.where(tril, jnp.exp(jnp.where(tril, diff, 0.0)), 0.0)
    kb = k3 * b3
    l_mat = jnp.where(strict, _bmm(kb, k3, 'bid,bjd->bij') * decay, 0.0)
    if t_saved is None:
        t_inv = _tri_inverse(l_mat, (ri == ci).astype(F32))
    else:
        t_inv = _tri_inverse_saved(l_mat, t_saved.reshape(nb, c, c))
    eg = jnp.exp(gc)
    u = _bmm(t_inv, v3 * b3, 'bij,bje->bie')
    w = _bmm(t_inv, kb * eg, 'bij,bje->bie')
    intra = jnp.where(tril, _bmm(q3, k3, 'bid,bjd->bij') * decay, 0.0)
    qd = q3 * eg
    gl = jnp.sum(g3, axis=1, keepdims=True)
    kt = k3 * jnp.exp(gl - gc)
    outs = (u.reshape(r, LANE), w.reshape(r, LANE), qd.reshape(r, LANE), kt.reshape(r, LANE),
            intra.reshape(r, c), gl.reshape(nb, LANE))
    if t_saved is not None:
        return outs
    qd2 = qd - _bmm(intra, w, 'bij,bjd->bid')
    au = _bmm(intra, u, 'bij,bje->bie')
    return outs + (t_inv.reshape(r, c), qd2.reshape(r, LANE), au.reshape(r, LANE))


def _pa_specs(t):
    rr = min(_PA_ROWS, t)
    nb = rr // DN_CHUNK
    qkv = [pl.BlockSpec((rr, LANE), functools.partial(lambda i, h, o: (i, o + h), o=o)) for o in (0, HEADS, 2 * HEADS)]
    ba = pl.BlockSpec((rr, LANE), lambda i, h: (i, 3))
    vec = pl.BlockSpec((1, LANE), lambda i, h: (0, 0))
    row = pl.BlockSpec((rr, LANE), lambda i, h: (i, h))
    intra = pl.BlockSpec((None, rr, DN_CHUNK), lambda i, h: (h, i, 0))
    gl = pl.BlockSpec((nb, LANE), lambda i, h: (i, h))
    return rr, qkv, ba, vec, row, intra, gl


def _grid_ends(grid):
    first = lambda: functools.reduce(jnp.logical_and, [pl.program_id(a) == 0 for a in range(len(grid))])
    last = lambda: functools.reduce(jnp.logical_and, [pl.program_id(a) == n - 1 for a, n in enumerate(grid)])
    return first, last


def _delta_local(qkv_act, pm, alog, dtb, rider=None):
    t = qkv_act.shape[0]
    rr, qkv, ba, vec, row, intra, gl = _pa_specs(t)

    def body(q, k, v, b, al, dt, *outs):
        vals = _phase_a(pl.program_id(1), q[...], k[...], v[...], b[...], al[...], dt[...])
        for o, val in zip(outs, vals):
            o[...] = val

    wide = jax.ShapeDtypeStruct((t, HEADS * LANE), F32)
    sq = jax.ShapeDtypeStruct((HEADS, t, DN_CHUNK), F32)
    grid = (t // rr, HEADS)
    return _carried_call(
        body, rider, *_grid_ends(grid), name="delta_local", grid=grid, in_specs=qkv + [ba, vec, vec],
        out_specs=[row] * 4 + [intra, gl, intra, row, row],
        out_shape=[wide] * 4 + [sq, jax.ShapeDtypeStruct((t // DN_CHUNK, HEADS * LANE), F32), sq, wide, wide],
        scratch_shapes=[], sem=("arbitrary", "arbitrary"), args=(qkv_act, qkv_act, qkv_act, pm, alog, dtb))


def _delta_local_bwd(qkv_act, pm, alog, dtb, t_inv, du, dw, dqd, dkt, dintra, dgl, rider=None):
    t = qkv_act.shape[0]
    rr, qkv, ba, vec, row, intra, gl = _pa_specs(t)

    def body(q, k, v, b, al, dt, ti, du_r, dw_r, dqd_r, dkt_r, di_r, dgl_r, dq_o, dk_o, dv_o, dba_o, dal_o, ddt_o):
        i, h = pl.program_id(0), pl.program_id(1)
        t_saved = ti[...]
        _, vjp = jax.vjp(lambda *a: _phase_a(h, *a, t_saved=t_saved), q[...], k[...], v[...], b[...], al[...], dt[...])
        dq, dk, dv, dba, dal, ddt = vjp((du_r[...], dw_r[...], dqd_r[...], dkt_r[...], di_r[...], dgl_r[...]))
        dq_o[...], dk_o[...], dv_o[...] = dq, dk, dv

        @pl.when(h == 0)
        def _():
            dba_o[...] = jnp.zeros_like(dba_o)

        @pl.when((h == 0) & (i == 0))
        def _():
            dal_o[...] = jnp.zeros_like(dal_o)
            ddt_o[...] = jnp.zeros_like(ddt_o)

        dba_o[...] += dba
        dal_o[...] += dal
        ddt_o[...] += ddt

    wide = jax.ShapeDtypeStruct((t, HEADS * LANE), F32)
    vshape = jax.ShapeDtypeStruct((1, LANE), F32)
    grid = (t // rr, HEADS)
    return _carried_call(
        body, rider, *_grid_ends(grid), name="delta_local_bwd", grid=grid,
        in_specs=qkv + [ba, vec, vec, intra] + [row] * 4 + [intra, gl],
        out_specs=[row] * 3 + [pl.BlockSpec((rr, LANE), lambda i, h: (i, 0)), vec, vec],
        out_shape=[wide] * 3 + [jax.ShapeDtypeStruct((t, LANE), F32), vshape, vshape],
        scratch_shapes=[], sem=("arbitrary", "arbitrary"),
        args=(qkv_act, qkv_act, qkv_act, pm, alog, dtb, t_inv, du, dw, dqd, dkt, dintra, dgl))


_SCAN_ROWS = 512


def _dot(a, b, dn):
    return lax.dot_general(a.astype(_CDT), b.astype(_CDT), (dn, ((), ())), preferred_element_type=F32)


_NN = ((1,), (0,))
_NT = ((1,), (1,))
_TN = ((0,), (0,))


def _delta_scan(u, w, qd, kt, au, gl, rider=None):
    t = u.shape[0]
    rr = min(_SCAN_ROWS, t)
    nc = rr // DN_CHUNK

    def body(u_ref, w_ref, qd_ref, kt_ref, au_ref, gl_ref, o_ref, sall_ref, s_scr):
        @pl.when(pl.program_id(0) == 0)
        def _():
            s_scr[...] = jnp.zeros_like(s_scr)

        def chunk(c, carry):
            r0 = pl.multiple_of(c * DN_CHUNK, DN_CHUNK)
            rows = pl.ds(r0, DN_CHUNK)
            e = jnp.exp(gl_ref[pl.ds(c, 1), :])
            states = [s_scr[h] for h in range(HEADS)]
            u_c, w_c, qd_c, kt_c, au_c = u_ref[rows, :], w_ref[rows, :], qd_ref[rows, :], kt_ref[rows, :], au_ref[rows, :]
            o_new, s_new = [], []
            for h in range(HEADS):
                cs = slice(h * LANE, (h + 1) * LANE)
                s = states[h]
                both = _dot(jnp.concatenate([w_c[:, cs], qd_c[:, cs]], axis=0), s, _NN)
                v_new = u_c[:, cs] - both[:DN_CHUNK]
                o_new.append(both[DN_CHUNK:] + au_c[:, cs])
                s_new.append(s * e[:, cs] + _dot(kt_c[:, cs], v_new, _TN))
            o_ref[rows, :] = jnp.concatenate(o_new, axis=1)
            for h in range(HEADS):
                sall_ref[c, h] = states[h]
                s_scr[h] = s_new[h]
            return carry

        lax.fori_loop(0, nc, chunk, 0)

    row = pl.BlockSpec((rr, HEADS * LANE), lambda i: (i, 0))
    grid = (t // rr,)
    return _carried_call(
        body, rider, *_grid_ends(grid), name="delta_scan", grid=grid,
        in_specs=[row] * 5 + [pl.BlockSpec((nc, HEADS * LANE), lambda i: (i, 0))],
        out_specs=[row, pl.BlockSpec((nc, HEADS, LANE, LANE), lambda i: (i, 0, 0, 0))],
        out_shape=[jax.ShapeDtypeStruct((t, HEADS * LANE), F32),
                   jax.ShapeDtypeStruct((t // DN_CHUNK, HEADS, LANE, LANE), F32)],
        scratch_shapes=[pltpu.VMEM((HEADS, LANE, LANE), F32)], sem=("arbitrary",), args=(u, w, qd, kt, au, gl))


def _delta_scan_bwd(u, w, qd, kt, intra, gl, sall, do, rider=None):
    t = u.shape[0]
    rr = min(_SCAN_ROWS, t)
    nc = rr // DN_CHUNK
    ng = t // rr

    def body(u_ref, w_ref, qd_ref, kt_ref, a_ref, gl_ref, sall_ref, do_ref,
             du_ref, dw_ref, dqd_ref, dkt_ref, da_ref, dgl_ref, ds_scr):
        @pl.when(pl.program_id(0) == 0)
        def _():
            ds_scr[...] = jnp.zeros_like(ds_scr)

        def chunk(cc, carry):
            c = nc - 1 - cc
            r0 = pl.multiple_of(c * DN_CHUNK, DN_CHUNK)
            rows = pl.ds(r0, DN_CHUNK)
            e = jnp.exp(gl_ref[pl.ds(c, 1), :])
            states = [sall_ref[c, h] for h in range(HEADS)]
            ds_outs = [ds_scr[h] for h in range(HEADS)]
            u_a, w_a, kt_a, qd_a, do_a = u_ref[rows, :], w_ref[rows, :], kt_ref[rows, :], qd_ref[rows, :], do_ref[rows, :]
            a_a = [a_ref[h, rows, :] for h in range(HEADS)]
            da, dqd, dkt, du, dw, dgl, ds_new = [], [], [], [], [], [], []
            for h in range(HEADS):
                cs = slice(h * LANE, (h + 1) * LANE)
                s, ds_out = states[h], ds_outs[h]
                w_c, kt_c, qd_c, do_c = w_a[:, cs], kt_a[:, cs], qd_a[:, cs], do_a[:, cs]
                v_new = u_a[:, cs] - _dot(w_c, s, _NN)
                dv_new = _dot(a_a[h], do_c, _TN) + _dot(kt_c, ds_out, _NN)
                cots = jnp.concatenate([do_c, dv_new], axis=0)
                both = _dot(cots, s, _NT)
                dqd.append(both[:DN_CHUNK])
                dw.append(-both[DN_CHUNK:])
                da.append(_dot(do_c, v_new, _NT))
                dkt.append(_dot(v_new, ds_out, _NT))
                du.append(dv_new)
                eh = e[:, cs]
                dgl.append(jnp.broadcast_to(jnp.sum(ds_out * s, axis=0, keepdims=True) * eh, (8, LANE)))
                ds_new.append(ds_out * eh + _dot(jnp.concatenate([qd_c, -w_c], axis=0), cots, _TN))
            cat = lambda parts: jnp.concatenate(parts, axis=1)
            dqd_ref[rows, :], dkt_ref[rows, :], du_ref[rows, :], dw_ref[rows, :] = cat(dqd), cat(dkt), cat(du), cat(dw)
            dgl_ref[pl.ds(pl.multiple_of(c * 8, 8), 8), :] = cat(dgl)
            for h in range(HEADS):
                da_ref[h, rows, :] = da[h]
                ds_scr[h] = ds_new[h]
            return carry

        lax.fori_loop(0, nc, chunk, 0)

    rev = lambda i: (ng - 1 - i, 0)
    row = pl.BlockSpec((rr, HEADS * LANE), rev)
    a_spec = pl.BlockSpec((HEADS, rr, DN_CHUNK), lambda i: (0, ng - 1 - i, 0))
    gl_spec = pl.BlockSpec((nc, HEADS * LANE), rev)
    wide = jax.ShapeDtypeStruct((t, HEADS * LANE), F32)
    outs, carried = _carried_call(
        body, rider, *_grid_ends((ng,)), name="delta_scan_bwd", grid=(ng,),
        in_specs=[row] * 4 + [a_spec, gl_spec, pl.BlockSpec((nc, HEADS, LANE, LANE), lambda i: (ng - 1 - i, 0, 0, 0)), row],
        out_specs=[row] * 4 + [a_spec, pl.BlockSpec((nc * 8, HEADS * LANE), rev)],
        out_shape=[wide] * 4 + [jax.ShapeDtypeStruct((HEADS, t, DN_CHUNK), F32),
                                jax.ShapeDtypeStruct((t // DN_CHUNK * 8, HEADS * LANE), F32)],
        scratch_shapes=[pltpu.VMEM((HEADS, LANE, LANE), F32)], sem=("arbitrary",), args=(u, w, qd, kt, intra, gl, sall, do))
    return tuple(outs[:5]) + (outs[5].reshape(t // DN_CHUNK, 8, HEADS * LANE)[:, 0, :],), carried


_ATT_TILE = 512


def _kv_rows(j, tk):
    return pl.ds(pl.multiple_of(j * tk, tk), tk)


def _att_scores(ql, qr, ckv_ref, kr_ref, j, tk):
    ks = _kv_rows(j, tk)
    return (_dot(ql, ckv_ref[ks, :], _NT) + _dot(qr, kr_ref[ks, :], _NT)) * ATT_SCALE


def _diag_mask(s):
    qi = lax.broadcasted_iota(jnp.int32, s.shape, 0)
    ki = lax.broadcasted_iota(jnp.int32, s.shape, 1)
    return jnp.where(ki <= qi, s, NEG_BIG)


def _attention(qn, qr_pre, cosb, sinb, ckv, kr, wuk, wuv):
    t = ckv.shape[0]
    tq = min(_ATT_TILE, t)

    nl = tq // LANE

    def lane_fold(v, op):
        out = v[:, :LANE]
        for k in range(1, nl):
            out = op(out, v[:, k * LANE:(k + 1) * LANE])
        return out

    def body(qn_ref, qr_ref, cos_ref, sin_ref, ckv_ref, kr_ref, wuk_ref, wuv_ref, o_ref, lse_ref, qrope_ref, omla_ref,
             s_all, m_lanes, l_lanes, acc_scr):
        h, qi = pl.program_id(0), pl.program_id(1)
        q_lat = _dot(qn_ref[...], wuk_ref[h], _NT).astype(_CDT)
        q_rope = _rope(qr_ref[...], cos_ref[...], sin_ref[...]).astype(qrope_ref.dtype)
        qrope_ref[...] = q_rope
        m_lanes[...] = jnp.full_like(m_lanes, NEG_BIG)

        def scores(j, masked):
            s = _att_scores(q_lat, q_rope, ckv_ref, kr_ref, j, tq)
            if masked:
                s = _diag_mask(s)
            s_all[j] = s
            m_lanes[...] = jnp.maximum(m_lanes[...], lane_fold(s, jnp.maximum))

        def scores_body(j, carry):
            scores(j, False)
            return carry

        lax.fori_loop(0, qi, scores_body, 0)
        scores(qi, True)
        m = jnp.max(m_lanes[...], axis=-1, keepdims=True)
        mb = jnp.broadcast_to(m, (tq, LANE))
        l_lanes[...] = jnp.zeros_like(l_lanes)
        acc_scr[...] = jnp.zeros_like(acc_scr)

        def weigh(j, carry):
            s = s_all[j]
            p = jnp.concatenate([jnp.exp(s[:, k * LANE:(k + 1) * LANE] - mb) for k in range(nl)], axis=1)
            l_lanes[...] += lane_fold(p, jnp.add)
            acc_scr[...] += _dot(p, ckv_ref[_kv_rows(j, tq), :], _NN)
            return carry

        lax.fori_loop(0, qi + 1, weigh, 0)
        l = jnp.sum(l_lanes[...], axis=-1, keepdims=True)
        out = acc_scr[...] / l
        o_ref[...] = out
        lse_ref[...] = m + jnp.log(l)
        omla_ref[...] = _dot(out, wuv_ref[h], _NN).astype(omla_ref.dtype)

    col = pl.BlockSpec((tq, LANE), lambda h, i: (i, h))
    table = pl.BlockSpec((tq, ROPE_PAD), lambda h, i: (i, 0))
    wspec = pl.BlockSpec((HEADS, KV_LORA, NOPE), lambda h, i: (0, 0, 0))
    return pl.pallas_call(
        body, name="attention", grid=(HEADS, t // tq),
        in_specs=[col, col, table, table, pl.BlockSpec((t, KV_LORA), lambda h, i: (0, 0)),
                  pl.BlockSpec((t, ROPE_PAD), lambda h, i: (0, 0)), wspec, wspec],
        out_specs=[pl.BlockSpec((None, tq, KV_LORA), lambda h, i: (h, i, 0)),
                   pl.BlockSpec((None, tq, 1), lambda h, i: (h, i, 0)), col, col],
        out_shape=[jax.ShapeDtypeStruct((HEADS, t, KV_LORA), F32), jax.ShapeDtypeStruct((HEADS, t, 1), F32),
                   jax.ShapeDtypeStruct((t, HEADS * ROPE_PAD), _CDT), jax.ShapeDtypeStruct((t, HEADS * NOPE), _CDT)],
        scratch_shapes=[pltpu.VMEM((t // tq, tq, tq), F32), pltpu.VMEM((tq, LANE), F32), pltpu.VMEM((tq, LANE), F32),
                        pltpu.VMEM((tq, KV_LORA), F32)],
        compiler_params=_cparams(("parallel", "parallel")))(qn, qr_pre, cosb, sinb, ckv, kr, wuk, wuv)


def _attention_bwd(qn, qr, cosb, sinb, ckv, kr, wuk, wuv, out, lse, do_mla):
    t = ckv.shape[0]
    tq = min(_ATT_TILE, t)

    def body(qn_ref, qr_ref, cos_ref, sin_ref, ckv_ref, kr_ref, wuk_ref, wuv_ref, o_ref, lse_ref, do_ref,
             dql_ref, dqr_ref, dqn_ref, dckv_ref, dkr_ref, dql_scr, dqr_scr):
        h, qi = pl.program_id(0), pl.program_id(1)

        @pl.when((h == 0) & (qi == 0))
        def _():
            dckv_ref[...] = jnp.zeros_like(dckv_ref)
            dkr_ref[...] = jnp.zeros_like(dkr_ref)

        q_lat = _dot(qn_ref[...], wuk_ref[h], _NT).astype(_CDT)
        q_rope = qr_ref[...]
        d_out = _dot(do_ref[...], wuv_ref[h], _NT)
        d_o = d_out.astype(_CDT)
        lse_v = lse_ref[...]
        dsum = jnp.sum(d_out * o_ref[...], axis=-1, keepdims=True)
        dql_scr[...] = jnp.zeros_like(dql_scr)
        dqr_scr[...] = jnp.zeros_like(dqr_scr)

        def step(j, masked):
            ks = _kv_rows(j, tq)
            s = _att_scores(q_lat, q_rope, ckv_ref, kr_ref, j, tq)
            if masked:
                s = _diag_mask(s)
            p = jnp.exp(s - lse_v)
            kv = ckv_ref[ks, :]
            ds = (p * (_dot(d_o, kv, _NT) - dsum) * ATT_SCALE).astype(_CDT)
            pb = p.astype(_CDT)
            dql_scr[...] += _dot(ds, kv, _NN)
            dqr_scr[...] += _dot(ds, kr_ref[ks, :], _NN)
            dckv_ref[ks, :] += _dot(pb, d_o, _TN) + _dot(ds, q_lat, _TN)
            dkr_ref[ks, :] += _dot(ds, q_rope, _TN)

        def loop_body(j, carry):
            step(j, False)
            return carry

        lax.fori_loop(0, qi, loop_body, 0)
        step(qi, True)
        dql = dql_scr[...].astype(dql_ref.dtype)
        dql_ref[...] = dql
        dqn_ref[...] = _dot(dql, wuk_ref[h], _NN).astype(dqn_ref.dtype)
        dqr_ref[...] = _rope_bwd(dqr_scr[...], cos_ref[...], sin_ref[...]).astype(dqr_ref.dtype)

    lat = pl.BlockSpec((None, tq, KV_LORA), lambda h, i: (h, i, 0))
    col = pl.BlockSpec((tq, LANE), lambda h, i: (i, h))
    table = pl.BlockSpec((tq, ROPE_PAD), lambda h, i: (i, 0))
    kfull = pl.BlockSpec((t, KV_LORA), lambda h, i: (0, 0))
    rfull = pl.BlockSpec((t, ROPE_PAD), lambda h, i: (0, 0))
    wspec = pl.BlockSpec((HEADS, KV_LORA, NOPE), lambda h, i: (0, 0, 0))
    wide = jax.ShapeDtypeStruct((t, HEADS * LANE), _CDT)
    return pl.pallas_call(
        body, name="attention_bwd", grid=(HEADS, t // tq),
        in_specs=[col, col, table, table, kfull, rfull, wspec, wspec, lat,
                  pl.BlockSpec((None, tq, 1), lambda h, i: (h, i, 0)), col],
        out_specs=[lat, col, col, kfull, rfull],
        out_shape=[jax.ShapeDtypeStruct((HEADS, t, KV_LORA), _CDT), wide, wide,
                   jax.ShapeDtypeStruct((t, KV_LORA), F32), jax.ShapeDtypeStruct((t, ROPE_PAD), F32)],
        scratch_shapes=[pltpu.VMEM((tq, KV_LORA), F32), pltpu.VMEM((tq, ROPE_PAD), F32)],
        compiler_params=_cparams(("arbitrary", "arbitrary")))(qn, qr, cosb, sinb, ckv, kr, wuk, wuv, out, lse, do_mla)


_DX_ROWS = 256


def _dx_fused(pairs, add, add_scale, rider=None, name="b_dx"):
    t, d = add.shape
    tm = min(_DX_ROWS, t)
    n = len(pairs)

    def body(*refs):
        acc = refs[2 * n][...] * add_scale
        for i in range(n):
            acc = acc + _dot(refs[i][...], refs[n + i][...], _NT)
        refs[2 * n + 1][...] = acc

    in_specs = [pl.BlockSpec((tm, a.shape[1]), lambda i: (i, 0)) for a, _ in pairs]
    in_specs += [pl.BlockSpec(w.shape, lambda i: (0, 0)) for _, w in pairs]
    row = pl.BlockSpec((tm, d), lambda i: (i, 0))
    grid = (t // tm,)
    (dx,), carried = _carried_call(
        body, rider, *_grid_ends(grid), name=name, grid=grid, in_specs=in_specs + [row], out_specs=[row],
        out_shape=[jax.ShapeDtypeStruct((t, d), F32)], scratch_shapes=[], sem=("arbitrary",),
        args=tuple(a for a, _ in pairs) + tuple(w for _, w in pairs) + (add,))
    return dx, carried


def _ffn_in_swiglu(a, w):
    t, k = a.shape
    hid = w.shape[1] // 2
    tm, tn = _tile(t, 1024), _tile(hid, 1408)
    nj = hid // tn

    def body(a_ref, bg_ref, bu_ref, act_ref, gt_ref, up_ref):
        av = a_ref[...].astype(_CDT)
        gt = jnp.dot(av, bg_ref[...].astype(_CDT), preferred_element_type=F32)
        up = jnp.dot(av, bu_ref[...].astype(_CDT), preferred_element_type=F32)
        act_ref[...] = _swiglu(gt, up).astype(act_ref.dtype)
        gt_ref[...] = gt.astype(gt_ref.dtype)
        up_ref[...] = up.astype(up_ref.dtype)

    out = pl.BlockSpec((tm, tn), lambda i, j: (i, j))
    return pl.pallas_call(
        body, name="f_ffn_in_swiglu", grid=(t // tm, nj),
        in_specs=[pl.BlockSpec((tm, k), lambda i, j: (i, 0)), pl.BlockSpec((k, tn), lambda i, j: (0, j)),
                  pl.BlockSpec((k, tn), lambda i, j: (0, nj + j))],
        out_specs=[out] * 3, out_shape=[jax.ShapeDtypeStruct((t, hid), _CDT)] * 3,
        compiler_params=_cparams(("parallel", "parallel")))(a, w, w)


def _gated_norm(o, z, w):
    return _rms_norm(o, w) * _silu(z)


def _gated_norm_heads(o, z, w):
    heads = [_gated_norm(o[:, h * LANE:(h + 1) * LANE], z[:, h * LANE:(h + 1) * LANE], w) for h in range(HEADS)]
    return jnp.concatenate(heads, axis=1)


def _mla_pre(ckv, krp, cq, cosb, sinb, qw, kw):
    return _rms_norm(cq, qw), _rms_norm(ckv, kw), _rope(krp, cosb, sinb)


def _merge(gg, y_dn, y_mla):
    return _sigmoid(gg[:, :D_MODEL]) * y_dn + _sigmoid(gg[:, D_MODEL:]) * y_mla


def _ln1(xv, attn_out, g, b):
    return _layer_norm(ALPHA * xv + attn_out, g, b)


def _final(h1, ffn, gate_pre, ple_proj, g, b):
    return _layer_norm(ALPHA * h1 + ffn + _sigmoid(gate_pre) * ple_proj, g, b)


def _swiglu(gt, up):
    return _silu(gt) * up


def _local_step(x, p, cosb, sinb, target, wt, sp, exch):
    t = x.shape[0]
    bf = _CDT
    xb = x.astype(bf)
    g = {}

    qkv_pre = _mm2(xb, wt['qkv'], name="f_qkv")
    z = _mm2(xb, wt['z'], name="f_z")
    gg = _mm2(xb, wt['gg'], name="f_gg")
    pm = _mm2(xb, wt['mla'], name="f_mla")
    qkv_act = _conv_silu(qkv_pre, sp['conv_w'])
    (u, w_, qd, kt, intra, gl, t_inv, qd2, au), sent = _delta_local(qkv_act, pm, sp['a_log'], sp['dt_bias'],
                                                                    rider=exch.gather_send())
    (o_dn, sall), passed = _delta_scan(u, w_, qd2, kt, au, gl, rider=exch.gather_pass(sent))
    wt = dict(wt, **exch.weights(passed))
    def gated_norm_br(h, o, zz, w, wbr):
        og_v = _gated_norm_heads(o, zz, w).astype(bf)
        return og_v, jnp.dot(og_v, wbr.astype(bf), preferred_element_type=F32)

    og, y_dn = _rowwise(gated_norm_br, [(o_dn, D_MODEL, 0, False), (z, D_MODEL, 0, False)],
                        [sp['dn_norm_w'], wt['br_dn']],
                        [(D_MODEL, D_MODEL, False, bf), (D_MODEL, D_MODEL, False, F32)], name="f_gated_norm_br")

    def mla_pre_uq(h, ckv, krp, cq, cosv, sinv, qw, kw, wn, wr):
        c_q_v, c_kv_v, k_rope_v = _mla_pre(ckv, krp, cq, cosv, sinv, qw, kw)
        c_q_b = c_q_v.astype(bf)
        return (c_q_b, c_kv_v, k_rope_v, jnp.dot(c_q_b, wn.astype(bf), preferred_element_type=F32),
                jnp.dot(c_q_b, wr.astype(bf), preferred_element_type=F32))

    c_q, c_kv, k_rope, q_nope, q_rope_pre = _rowwise(
        mla_pre_uq,
        [(pm, KV_LORA, 0, False), (pm, ROPE_PAD, 2, False), (pm, Q_LORA, 2, False),
         (cosb, ROPE_PAD, 0, False), (sinb, ROPE_PAD, 0, False)],
        [sp['q_norm_w'], sp['kv_norm_w'], wt['uq_nope'], wt['uq_rope']],
        [(Q_LORA, Q_LORA, False, bf), (KV_LORA, KV_LORA, False, bf), (ROPE_PAD, ROPE_PAD, False, bf),
         (HEADS * NOPE, HEADS * NOPE, False, bf), (HEADS * ROPE_PAD, HEADS * ROPE_PAD, False, F32)], name="f_mla_pre_uq")
    out_lat, lse, q_rope, o_mla = _attention(q_nope, q_rope_pre, cosb, sinb, c_kv, k_rope, wt['uk'], wt['uv'])
    y_mla = _mm2(o_mla, wt['br_mla'], name="f_br_mla")

    def merge_o_ln1(h, ggv, yd, ym, xv, wo, gv, bv):
        mixed_v = _merge(ggv, yd, ym).astype(bf)
        ao = jnp.dot(mixed_v, wo.astype(bf), preferred_element_type=F32)
        h1v = _ln1(xv, ao, gv, bv)
        return mixed_v, ao, h1v, h1v

    mixed, attn_out, h1, h1b = _rowwise(
        merge_o_ln1, [(gg, 2 * D_MODEL, 0, False), (y_dn, D_MODEL, 0, False), (y_mla, D_MODEL, 0, False), (x, D_MODEL, 0, False)],
        [wt['o'], sp['ln1_g'], sp['ln1_b']],
        [(D_MODEL, D_MODEL, False, bf), (D_MODEL, D_MODEL, False, F32), (D_MODEL, D_MODEL, False, F32),
         (D_MODEL, D_MODEL, False, bf)], name="f_merge_o_ln1")
    act, ffn_gt, ffn_up = _ffn_in_swiglu(h1b, wt['ffn_in'])
    pb = p.astype(bf)

    def final_fn(h, h1v, actv, pv, tgt, gt, up, wfo, wpg, wpl, gv, bv):
        ffnv = jnp.dot(actv.astype(bf), wfo.astype(bf), preferred_element_type=F32)
        gpv = jnp.dot(h1v.astype(bf), wpg.astype(bf), preferred_element_type=F32)
        ppv = jnp.dot(pv.astype(bf), wpl.astype(bf), preferred_element_type=F32)
        y, vjp = jax.vjp(_final, h1v, ffnv, gpv, ppv, gv, bv)
        err = y - tgt
        dh1, dffn, dgp, dpp, dg, db = vjp(err * (1.0 / D_MODEL))
        sq = err * err
        lanes = sq[:, :LANE]
        for j in range(1, D_MODEL // LANE):
            lanes = lanes + sq[:, j * LANE:(j + 1) * LANE]
        loss = jnp.sum(lanes, axis=0, keepdims=True) * (0.5 / D_MODEL)
        dffn_b = dffn.astype(bf)
        dact = _dot(dffn_b, wfo, _NT).astype(bf).astype(F32)
        _, vjp_s = jax.vjp(_swiglu, gt.astype(F32), up.astype(F32))
        dgt, dup = vjp_s(dact)
        return dffn, dffn_b, dgp, dpp, jnp.concatenate([dgt, dup], axis=1), dg, db, loss

    dpre2, dpre2b, dgate_pre, dple_proj, dffn_in, g['ln2_g'], g['ln2_b'], loss_lanes = _rowwise(
        final_fn, [(h1, D_MODEL, 0, False), (act, FFN_HIDDEN, 0, False), (pb, PLE_DIM, 0, False), (target, D_MODEL, 0, False),
                   (ffn_gt, FFN_HIDDEN, 0, False), (ffn_up, FFN_HIDDEN, 0, False)],
        [wt['ffn_out'], wt['ple_gate'], wt['ple'], sp['ln2_g'], sp['ln2_b']],
        [(D_MODEL, D_MODEL, False, F32)] + [(D_MODEL, D_MODEL, False, bf)] * 3 + [(2 * FFN_HIDDEN, 2 * FFN_HIDDEN, False, bf)],
        [(1, D_MODEL), (1, D_MODEL), (1, LANE)], name="b_final")
    g['loss_lanes'] = loss_lanes

    g['ple'] = _mm2(pb, dple_proj, ta=True, name="g_ple")
    g['ple_gate'] = _mm2(h1b, dgate_pre, ta=True, name="g_ple_gate")
    g['ffn_out'] = _mm2(act, dpre2b, ta=True, name="g_ffn_out")
    g['ffn_in'] = _mm2(h1b, dffn_in, ta=True, name="g_ffn_in")
    dh1, _ = _dx_fused([(dffn_in, wt['ffn_in']), (dgate_pre, wt['ple_gate'])], dpre2, ALPHA, name="b_dh1")

    def attn_out_bwd(h, xv, ao, d, ggv, yd, ym, o, zz, wo, wbd, wbm, gv, bv, nw):
        _, vjp = jax.vjp(_ln1, xv, ao, gv, bv)
        _, dao, dg, db = vjp(d)
        dao_b = dao.astype(bf)
        _, vjp_m = jax.vjp(_merge, ggv, yd, ym)
        dggv, dyd, dym = vjp_m(_dot(dao_b, wo, _NT))
        dyd_b, dym_b = dyd.astype(bf), dym.astype(bf)
        _, vjp_n = jax.vjp(_gated_norm_heads, o, zz, nw)
        do_v, dz_v, dnw = vjp_n(_dot(dyd_b, wbd, _NT))
        return dao, dao_b, dggv, dyd_b, dym_b, do_v, dz_v, _dot(dym_b, wbm, _NT), dg, db, dnw

    row_d = lambda a: (a, D_MODEL, 0, False)
    dpre1, dpre1b, dgg, dy_dn, dy_mla, do_dn, dz, do_mla, g['ln1_g'], g['ln1_b'], g['dn_norm_w'] = _rowwise(
        attn_out_bwd, [row_d(x), row_d(attn_out), row_d(dh1), (gg, 2 * D_MODEL, 0, False), row_d(y_dn), row_d(y_mla),
                       row_d(o_dn), row_d(z)],
        [wt['o'], wt['br_dn'], wt['br_mla'], sp['ln1_g'], sp['ln1_b'], sp['dn_norm_w']],
        [(D_MODEL, D_MODEL, False, F32), (D_MODEL, D_MODEL, False, bf), (2 * D_MODEL, 2 * D_MODEL, False, bf),
         (D_MODEL, D_MODEL, False, bf), (D_MODEL, D_MODEL, False, bf), (D_MODEL, D_MODEL, False, F32),
         (D_MODEL, D_MODEL, False, bf), (D_MODEL, D_MODEL, False, bf)],
        [(1, D_MODEL), (1, D_MODEL), (1, LANE)], name="b_attn_out")
    g['o'] = _mm2(mixed, dpre1b, ta=True, name="g_o")
    g['br_dn'] = _mm2(og, dy_dn, ta=True, name="g_br_dn")
    g['br_mla'] = _mm2(o_mla, dy_mla, ta=True, name="g_br_mla")

    g['uv'] = _mm(out_lat, do_mla, name="g_uv", ta=True, heads=HEADS, a_head='lead', b_head='col', out_head='lead',
                  dims=(KV_LORA, NOPE, t))
    dq_lat, dq_rope_pre, dq_nope, dckv_att, dkr_att = _attention_bwd(
        q_nope, q_rope, cosb, sinb, c_kv, k_rope, wt['uk'], wt['uv'], out_lat, lse, do_mla)
    g['uk'] = _mm(dq_lat, q_nope, name="g_uk", ta=True, heads=HEADS, a_head='lead', b_head='col', out_head='lead',
                  dims=(KV_LORA, NOPE, t))
    g['uq_nope'] = _mm2(c_q, dq_nope, ta=True, name="g_uq_nope")
    g['uq_rope'] = _mm2(c_q, dq_rope_pre, ta=True, name="g_uq_rope")
    dc_q = _mm2(dq_nope, wt['uq_nope'], tb=True, name="b_dcq_nope")
    dc_q = _mm2(dq_rope_pre, wt['uq_rope'], tb=True, name="b_dcq_rope", add=dc_q)

    (du, dw, dqd, dkt, dintra, dgl), paired = _delta_scan_bwd(u, w_, qd, kt, intra, gl, sall, do_dn, rider=exch.pair_send(g))
    (dq_a, dk_a, dv_a, dba, g['a_log'], g['dt_bias']), arrived = _delta_local_bwd(
        qkv_act, pm, sp['a_log'], sp['dt_bias'], t_inv, du, dw, dqd, dkt, dintra, dgl, rider=exch.reduce_send(paired))
    exch.reduce_arrived(arrived)
    dqkv_pre, g['conv_w'] = _conv_silu_bwd(qkv_pre, sp['conv_w'], [dq_a, dk_a, dv_a])

    def mla_pre_bwd(h, ckv, cq, cosv, sinv, dcq, dckv, dkr, dba_v, qw, kw):
        _, vjp = jax.vjp(lambda a, c, d, e: (_rms_norm(c, d), _rms_norm(a, e)), ckv, cq, qw, kw)
        dckv_p, dcq_p, dqw, dkw = vjp((dcq, dckv))
        dkr_p = _rope_bwd(dkr, cosv, sinv)
        dpm = jnp.concatenate([dckv_p, dkr_p, dba_v, jnp.zeros((ckv.shape[0], 2 * LANE), F32), dcq_p], axis=1)
        return dpm, dqw, dkw

    dpm, g['q_norm_w'], g['kv_norm_w'] = _rowwise(
        mla_pre_bwd,
        [(pm, KV_LORA, 0, False), (pm, Q_LORA, 2, False), (cosb, ROPE_PAD, 0, False), (sinb, ROPE_PAD, 0, False),
         (dc_q, Q_LORA, 0, False), (dckv_att, KV_LORA, 0, False), (dkr_att, ROPE_PAD, 0, False), (dba, LANE, 0, False)],
        [sp['q_norm_w'], sp['kv_norm_w']], [(1152, 1152, False, bf)], [(1, Q_LORA), (1, KV_LORA)], name="b_mla_pre")

    rider = exch.small_send(g)
    if rider is None:
        g['qkv'] = _mm2(xb, dqkv_pre, ta=True, name="g_qkv")
    else:
        g['qkv'], got = _mm2(xb, dqkv_pre, ta=True, name="g_qkv", rider=rider)
        exch.small_arrived(got)
    g['z'] = _mm2(xb, dz, ta=True, name="g_z")
    g['gg'] = _mm2(xb, dgg, ta=True, name="g_gg")
    g['mla'] = _mm2(xb, dpm, ta=True, name="g_mla")
    dx, arrived = _dx_fused([(dqkv_pre, wt['qkv']), (dz, wt['z']), (dgg, wt['gg']), (dpm, wt['mla'])], dpre1, ALPHA,
                            rider=exch.in_send(g))
    exch.in_arrived(arrived)
    return loss_lanes, dx, g


_IN_SIZES = (QKV_W, HEADS * DN_DK, HEADS, HEADS, Q_LORA, KV_LORA, ROPE, D_MODEL, D_MODEL)


def _rope_tables(positions):
    inv_freq = ROPE_BASE ** (-jnp.arange(0, ROPE, 2, dtype=F32) / ROPE)
    ang = positions.astype(F32)[:, None] * inv_freq
    cos, sin = jnp.cos(ang), jnp.sin(ang)
    zeros = jnp.zeros((positions.shape[0], ROPE_PAD - ROPE), F32)
    return jnp.concatenate([cos, cos, zeros], axis=1), jnp.concatenate([-sin, sin, zeros], axis=1)


def _prep_w_in(w_in):
    dt = w_in.dtype
    offs = [0]
    for s in _IN_SIZES:
        offs.append(offs[-1] + s)
    qkv, z, wb, wa, cq, ckv, kr, gd, gm = [w_in[:, offs[i]:offs[i + 1]] for i in range(len(_IN_SIZES))]
    zc = lambda n: jnp.zeros((D_MODEL, n), dt)
    return {
        'qkv': qkv, 'z': z, 'gg': jnp.concatenate([gd, gm], axis=1),
        'mla': jnp.concatenate([ckv, kr, zc(ROPE_PAD - ROPE), wb, wa, zc(LANE - 2 * HEADS), zc(2 * LANE), cq], axis=1),
    }


def _prep_weights(full):
    w_uq = full['w_uq']
    wt = {
        'uq_nope': w_uq[:, :, :NOPE].reshape(Q_LORA, HEADS * NOPE),
        'uq_rope': jnp.pad(w_uq[:, :, NOPE:], ((0, 0), (0, 0), (0, ROPE_PAD - ROPE))).reshape(Q_LORA, HEADS * ROPE_PAD),
        'uk': jnp.transpose(full['w_uk'], (1, 0, 2)), 'uv': jnp.transpose(full['w_uv'], (1, 0, 2)),
        'br_dn': full['w_br_dn'], 'br_mla': full['w_br_mla'], 'o': full['w_o'], 'ffn_in': full['w_ffn_in'],
        'ffn_out': full['w_ffn_out'], 'ple': full['w_ple'], 'ple_gate': full['w_ple_gate'],
    }
    return wt


def _prep_small(small):
    pad = lambda v: jnp.pad(v, (0, LANE - v.shape[0]))[None, :]
    return {
        'conv_w': small['conv_w'], 'a_log': pad(small['dn_a_log']), 'dt_bias': pad(small['dn_dt_bias']),
        'dn_norm_w': small['dn_norm_w'][None, :], 'q_norm_w': small['q_norm_w'][None, :],
        'kv_norm_w': small['kv_norm_w'][None, :], 'ln1_g': small['ln1_g'][None, :], 'ln1_b': small['ln1_b'][None, :],
        'ln2_g': small['ln2_g'][None, :], 'ln2_b': small['ln2_b'][None, :],
    }


def _w_in_grad(g):
    mla = g['mla']
    ba0 = KV_LORA + ROPE_PAD
    cq0 = ba0 + 3 * LANE
    return jnp.concatenate([
        g['qkv'], g['z'], mla[:, ba0:ba0 + HEADS], mla[:, ba0 + HEADS:ba0 + 2 * HEADS], mla[:, cq0:cq0 + Q_LORA],
        mla[:, :KV_LORA], mla[:, KV_LORA:KV_LORA + ROPE], g['gg']], axis=1)


def _unprep_grads_late(g):
    return {
        'conv_w': g['conv_w'], 'dn_a_log': g['a_log'][0, :HEADS], 'dn_dt_bias': g['dt_bias'][0, :HEADS],
        'dn_norm_w': g['dn_norm_w'][0], 'q_norm_w': g['q_norm_w'][0], 'kv_norm_w': g['kv_norm_w'][0],
        'ln1_g': g['ln1_g'][0], 'ln1_b': g['ln1_b'][0], 'ln2_g': g['ln2_g'][0], 'ln2_b': g['ln2_b'][0],
    }


def _unprep_grads_early(g):
    w_uq = jnp.concatenate([g['uq_nope'].reshape(Q_LORA, HEADS, NOPE),
                            g['uq_rope'].reshape(Q_LORA, HEADS, ROPE_PAD)[:, :, :ROPE]], axis=2)
    return {
        'w_uq': w_uq, 'w_uk': jnp.transpose(g['uk'], (1, 0, 2)), 'w_uv': jnp.transpose(g['uv'], (1, 0, 2)),
        'w_br_dn': g['br_dn'], 'w_br_mla': g['br_mla'], 'w_o': g['o'],
        'w_ffn_in': g['ffn_in'], 'w_ffn_out': g['ffn_out'], 'w_ple': g['ple'], 'w_ple_gate': g['ple_gate'],
    }


_FLATB_PIECES = (
    ('w_ffn_out', 704, (704, D_MODEL)), ('w_br_dn', 256, (256, D_MODEL)), ('w_br_mla', 256, (256, D_MODEL)),
    ('w_o', 256, (256, D_MODEL)), ('w_ple_gate', 256, (256, D_MODEL)), ('w_uq', 144, (96, HEADS, NOPE + ROPE)),
    ('w_uk', 64, (64, HEADS, NOPE)), ('w_uv', 64, (64, HEADS, NOPE)), ('w_ple', 64, (PLE_DIM, 256)),
)
FLATB_ROWS = 2112
W_IN_SHARD = D_IN // N_SHARD
FFN_IN_SHARD = 2 * FFN_HIDDEN // N_SHARD
A_ROWS = D_MODEL + 32
_CONV_SHARD = QKV_W // N_SHARD
_ADD_TILES = (256, 256, 352)


def _flatb_offsets():
    offs, o = {}, 0
    for name, rows, _ in _FLATB_PIECES:
        offs[name] = o
        o += rows
    return offs, o


def _pack_shards(ws, conv_w):
    conv_bits = lax.bitcast_convert_type(conv_w, jnp.bfloat16).reshape(DN_CONV, 2 * _CONV_SHARD).astype(_CDT)
    tail = jnp.pad(conv_bits, ((0, A_ROWS - D_MODEL - DN_CONV), (0, W_IN_SHARD - 2 * _CONV_SHARD)))
    a_buf = jnp.concatenate([ws['w_in'].astype(_CDT), tail], axis=0)
    parts = [ws[name].astype(_CDT).reshape(rows, FLAT_W) for name, rows, _ in _FLATB_PIECES]
    used = sum(p.shape[0] for p in parts)
    parts.append(jnp.zeros((FLATB_ROWS - used, FLAT_W), _CDT))
    return [a_buf, ws['w_ffn_in'].astype(_CDT), jnp.concatenate(parts, axis=0)]


def _unpack_w_in(gathered, local, me):
    a = [jnp.where(me == s, local, gathered[s]) for s in range(N_SHARD)]
    conv = [lax.bitcast_convert_type(
        p[D_MODEL:D_MODEL + DN_CONV, :2 * _CONV_SHARD].astype(jnp.bfloat16).reshape(DN_CONV, _CONV_SHARD, 2), F32) for p in a]
    return jnp.concatenate([p[:D_MODEL] for p in a], axis=1), jnp.concatenate(conv, axis=1)


def _unpack_rest(gathered, local, me):
    pick = lambda b, s: jnp.where(me == s, local[b], gathered[b][s])
    full = {'w_ffn_in': jnp.concatenate([pick(0, s) for s in range(N_SHARD)], axis=1)}
    offs, _ = _flatb_offsets()
    fb = [pick(1, s) for s in range(N_SHARD)]
    for name, rows, shape in _FLATB_PIECES:
        pieces = [p[offs[name]:offs[name] + rows].reshape(shape) for p in fb]
        full[name] = jnp.concatenate(pieces, axis=1 if name == 'w_ple' else 0)
    return full


def _shard_columns(g, w):
    return jnp.stack([g[:, s * w:(s + 1) * w] for s in range(N_SHARD)])


def _pack_grads_rest(gw):
    parts = []
    for name, rows, _ in _FLATB_PIECES:
        g = gw[name]
        if name == 'w_ple':
            parts.append(_shard_columns(g, PLE_DIM).reshape(N_SHARD, rows, FLAT_W))
        else:
            parts.append(g.reshape(N_SHARD, rows, FLAT_W))
    used = sum(p.shape[1] for p in parts)
    parts.append(jnp.zeros((N_SHARD, FLATB_ROWS - used, FLAT_W), F32))
    return [_shard_columns(gw['w_ffn_in'], FFN_IN_SHARD), jnp.concatenate(parts, axis=1)]


def _unpack_reduced(mine, theirs, c):
    whole = [jnp.concatenate([jnp.where(c == 0, m, t), jnp.where(c == 0, t, m)], axis=0) for m, t in zip(mine, theirs)]
    out = {'w_in': whole[0], 'w_ffn_in': whole[1]}
    offs, _ = _flatb_offsets()
    for name, rows, shape in _FLATB_PIECES:
        out[name] = whole[2][offs[name]:offs[name] + rows].reshape(shape)
    return out


_HBM = pl.BlockSpec(memory_space=pltpu.HBM)


def _place():
    x, y, c = lax.axis_index("x"), lax.axis_index("y"), lax.axis_index("c")
    chips = [(1 - x, y), (x, 1 - y), (1 - x, 1 - y)]
    return x, y, c, chips


def _remote(src, dst, send_sems, recv_sems, k, to):
    return pltpu.make_async_remote_copy(src_ref=src, dst_ref=dst, send_sem=send_sems.at[k], recv_sem=recv_sems.at[k],
                                        device_id=to, device_id_type=_MESH)


def _half_rows(ref, half, hf, lead=None):
    rows = pl.ds(pl.multiple_of(hf * half, 16), half)
    return ref.at[rows, :] if lead is None else ref.at[lead, rows, :]


class _Rider:
    def __init__(self, inputs, out_shape, n_sems, copies, aliases=None):
        self.inputs, self.out_shape, self.n_sems, self.copies = list(inputs), list(out_shape), n_sems, copies
        self.aliases = aliases or {}


def _carried_call(body, rider, first, last, *, name, grid, in_specs, out_specs, out_shape, scratch_shapes, sem, args):
    n_in, n_out, n_scr = len(in_specs), len(out_specs), len(scratch_shapes)
    if rider is None:
        res = pl.pallas_call(body, name=name, grid=grid, in_specs=in_specs, out_specs=out_specs, out_shape=out_shape,
                             scratch_shapes=scratch_shapes, compiler_params=_cparams(sem))(*args)
        return list(res), []
    ri, ro = len(rider.inputs), len(rider.out_shape)

    def full_body(*refs):
        own_in, r_in = refs[:n_in], refs[n_in:n_in + ri]
        o0 = n_in + ri
        own_out, r_out = refs[o0:o0 + n_out], refs[o0 + n_out:o0 + n_out + ro]
        s0 = o0 + n_out + ro
        own_scr, send_sems, recv_sems = refs[s0:s0 + n_scr], refs[s0 + n_scr], refs[s0 + n_scr + 1]

        @pl.when(first())
        def _():
            sends, _ = rider.copies(r_in, r_out, send_sems, recv_sems)
            for cp in sends:
                cp.start()

        body(*own_in, *own_out, *own_scr)

        @pl.when(last())
        def _():
            sends, arrivals = rider.copies(r_in, r_out, send_sems, recv_sems)
            for cp in arrivals():
                cp.wait_recv()
            for cp in sends:
                cp.wait_send()

    res = pl.pallas_call(
        full_body, name=name, grid=grid, in_specs=list(in_specs) + [_HBM] * ri, out_specs=list(out_specs) + [_HBM] * ro,
        out_shape=list(out_shape) + rider.out_shape,
        scratch_shapes=list(scratch_shapes) + [pltpu.SemaphoreType.DMA((rider.n_sems,))] * 2,
        input_output_aliases={n_in + i: n_out + o for i, o in rider.aliases.items()},
        compiler_params=_cparams(sem))(*args, *rider.inputs)
    return list(res[:n_out]), list(res[n_out:])


def _ride_gather_send(bufs):
    n = len(bufs)
    halves = [b.shape[0] // 2 for b in bufs]

    def copies(ins, outs, send_sems, recv_sems):
        x, y, c, chips = _place()
        slot = lambda b, cx, cy: _half_rows(outs[b], halves[b], c, lead=2 * cx + cy)
        sends = [_remote(_half_rows(ins[b], halves[b], c), slot(b, x, y), send_sems, recv_sems, 3 * b + j, (cx, cy, c))
                 for b in range(n) for j, (cx, cy) in enumerate(chips)]
        arrivals = lambda: [_remote(slot(b, cx, cy), slot(b, cx, cy), send_sems, recv_sems, 3 * b + j, (x, y, c))
                            for b in range(n) for j, (cx, cy) in enumerate(chips)]
        return sends, arrivals

    return _Rider(bufs, [jax.ShapeDtypeStruct((N_SHARD,) + b.shape, b.dtype) for b in bufs], 3 * n, copies)


def _ride_gather_pass(gathered):
    n = len(gathered)
    halves = [g.shape[1] // 2 for g in gathered]

    def copies(ins, outs, send_sems, recv_sems):
        x, y, c, chips = _place()
        slot = lambda b, cx, cy, hf: _half_rows(outs[b], halves[b], hf, lead=2 * cx + cy)
        sends = [_remote(slot(b, cx, cy, c), slot(b, cx, cy, c), send_sems, recv_sems, 3 * b + j, (x, y, 1 - c))
                 for b in range(n) for j, (cx, cy) in enumerate(chips)]
        arrivals = lambda: [_remote(slot(b, cx, cy, 1 - c), slot(b, cx, cy, 1 - c), send_sems, recv_sems, 3 * b + j, (x, y, c))
                            for b in range(n) for j, (cx, cy) in enumerate(chips)]
        return sends, arrivals

    return _Rider(gathered, [jax.ShapeDtypeStruct(g.shape, g.dtype) for g in gathered], 3 * n, copies,
                  aliases={b: b for b in range(n)})


class _SemSlice:
    def __init__(self, sems, off):
        self.sems, self.off = sems, off

    @property
    def at(self):
        sems, off = self.sems, self.off

        class _At:
            def __getitem__(self, k):
                return sems.at[off + k]

        return _At()


def _join_riders(r1, r2):
    n_in, n_out = len(r1.inputs), len(r1.out_shape)

    def copies(ins, outs, send_sems, recv_sems):
        s1, a1 = r1.copies(ins[:n_in], outs[:n_out], send_sems, recv_sems)
        s2, a2 = r2.copies(ins[n_in:], outs[n_out:], _SemSlice(send_sems, r1.n_sems), _SemSlice(recv_sems, r1.n_sems))
        return s1 + s2, lambda: a1() + a2()

    aliases = dict(r1.aliases)
    aliases.update({n_in + i: n_out + o for i, o in r2.aliases.items()})
    return _Rider(r1.inputs + r2.inputs, r1.out_shape + r2.out_shape, r1.n_sems + r2.n_sems, copies, aliases)


def _ride_small_gather(buf):
    def copies(ins, outs, send_sems, recv_sems):
        x, y, c, _ = _place()
        flip = lambda v, d: 1 - v if d else v
        sends, peers = [], []
        for dx in (0, 1):
            for dy in (0, 1):
                for dc in (0, 1):
                    if dx or dy or dc:
                        k = 4 * dx + 2 * dy + dc - 1
                        px, py, pc = flip(x, dx), flip(y, dy), flip(c, dc)
                        sends.append(_remote(ins[0], outs[0].at[4 * x + 2 * y + c], send_sems, recv_sems, k, (px, py, pc)))
                        peers.append((k, 4 * px + 2 * py + pc))
        arrivals = lambda: [_remote(ins[0], outs[0].at[slot], send_sems, recv_sems, k, (x, y, c)) for k, slot in peers]
        return sends, arrivals

    return _Rider([buf], [jax.ShapeDtypeStruct((8,) + buf.shape, buf.dtype)], 7, copies)


def _small_sum(gathered, buf, me_arr):
    n, r, width = gathered.shape

    def body(me_ref, g_ref, b_ref, o_ref):
        total = jnp.zeros((r, width), F32)
        for d in range(n):
            total = total + jnp.where(me_ref[0] == d, b_ref[...], g_ref[d])
        o_ref[...] = total

    return pl.pallas_call(
        body, name="small_sum", out_shape=jax.ShapeDtypeStruct((r, width), F32),
        grid_spec=pltpu.PrefetchScalarGridSpec(
            num_scalar_prefetch=1, grid=(1,),
            in_specs=[pl.BlockSpec((n, r, width), lambda i, me: (0, 0, 0)), pl.BlockSpec((r, width), lambda i, me: (0, 0))],
            out_specs=pl.BlockSpec((r, width), lambda i, me: (0, 0))),
        compiler_params=_cparams(("arbitrary",)))(me_arr, gathered, buf)


def _ride_pair_exchange(gbufs):
    n = len(gbufs)
    halves = [g.shape[1] // 2 for g in gbufs]

    def copies(ins, outs, send_sems, recv_sems):
        x, y, c, _ = _place()
        sends = [_remote(ins[b].at[:, pl.ds(pl.multiple_of((1 - c) * halves[b], 16), halves[b]), :], outs[b],
                         send_sems, recv_sems, b, (x, y, 1 - c)) for b in range(n)]
        arrivals = lambda: [_remote(outs[b], outs[b], send_sems, recv_sems, b, (x, y, c)) for b in range(n)]
        return sends, arrivals

    return _Rider(gbufs, [jax.ShapeDtypeStruct((N_SHARD, h, g.shape[2]), g.dtype) for g, h in zip(gbufs, halves)], n, copies)


def _ride_chip_exchange(parts):
    n = len(parts)

    def copies(ins, outs, send_sems, recv_sems):
        x, y, c, chips = _place()
        sends = [_remote(ins[b].at[2 * cx + cy], outs[b].at[j], send_sems, recv_sems, 3 * b + j, (cx, cy, c))
                 for b in range(n) for j, (cx, cy) in enumerate(chips)]
        arrivals = lambda: [_remote(ins[b].at[0], outs[b].at[j], send_sems, recv_sems, 3 * b + j, (x, y, c))
                            for b in range(n) for j in range(len(chips))]
        return sends, arrivals

    return _Rider(parts, [jax.ShapeDtypeStruct((3,) + p.shape[1:], p.dtype) for p in parts], 3 * n, copies)


def _gather_shards(bufs, name):
    n = len(bufs)
    halves = [b.shape[0] // 2 for b in bufs]

    def body(*refs):
        ins, outs, send_sems, recv_sems = refs[:n], refs[n:2 * n], refs[2 * n], refs[2 * n + 1]
        x, y, c, chips = _place()
        me, sibling = (x, y, c), (x, y, 1 - c)
        slot = lambda b, cx, cy, hf: _half_rows(outs[b], halves[b], hf, lead=2 * cx + cy)
        first = [_remote(_half_rows(ins[b], halves[b], c), slot(b, x, y, c), send_sems, recv_sems, 6 * b + j, (cx, cy, c))
                 for b in range(n) for j, (cx, cy) in enumerate(chips)]
        for cp in first:
            cp.start()
        passed = []
        for j, (cx, cy) in enumerate(chips):
            for b in range(n):
                _remote(slot(b, cx, cy, c), slot(b, cx, cy, c), send_sems, recv_sems, 6 * b + j, me).wait_recv()
                fwd = _remote(slot(b, cx, cy, c), slot(b, cx, cy, c), send_sems, recv_sems, 6 * b + 3 + j, sibling)
                fwd.start()
                passed.append(fwd)
        for j, (cx, cy) in enumerate(chips):
            for b in range(n):
                _remote(slot(b, cx, cy, 1 - c), slot(b, cx, cy, 1 - c), send_sems, recv_sems, 6 * b + 3 + j, me).wait_recv()
        for cp in first + passed:
            cp.wait_send()

    return pl.pallas_call(
        body, name=name, out_shape=[jax.ShapeDtypeStruct((N_SHARD,) + b.shape, b.dtype) for b in bufs],
        in_specs=[_HBM] * n, out_specs=[_HBM] * n,
        scratch_shapes=[pltpu.SemaphoreType.DMA((6 * n,)), pltpu.SemaphoreType.DMA((6 * n,))],
    )(*bufs)


def _reduce_pair_exchange(gbufs, name):
    n = len(gbufs)
    halves = [g.shape[1] // 2 for g in gbufs]

    def body(*refs):
        ins, outs, send_sems, recv_sems = refs[:n], refs[n:2 * n], refs[2 * n], refs[2 * n + 1]
        x, y, c, _ = _place()
        cps = [_remote(ins[b].at[:, pl.ds(pl.multiple_of((1 - c) * halves[b], 16), halves[b]), :], outs[b],
                       send_sems, recv_sems, b, (x, y, 1 - c)) for b in range(n)]
        for cp in cps:
            cp.start()
        for cp in cps:
            cp.wait()

    return pl.pallas_call(
        body, name=name,
        out_shape=[jax.ShapeDtypeStruct((N_SHARD, h, g.shape[2]), g.dtype) for g, h in zip(gbufs, halves)],
        in_specs=[_HBM] * n, out_specs=[_HBM] * n,
        scratch_shapes=[pltpu.SemaphoreType.DMA((n,)), pltpu.SemaphoreType.DMA((n,))],
    )(*gbufs)


def _pair_add(gbuf, recv, c_arr, tr, name):
    _, rows, width = gbuf.shape
    half = rows // 2
    nt = half // tr

    def body(c_ref, a_ref, b_ref, o_ref):
        o_ref[...] = (a_ref[...] + b_ref[...]).astype(o_ref.dtype)

    blk = lambda f: pl.BlockSpec((None, tr, width), f)
    return pl.pallas_call(
        body, name=name, out_shape=jax.ShapeDtypeStruct((N_SHARD, half, width), jnp.bfloat16),
        grid_spec=pltpu.PrefetchScalarGridSpec(
            num_scalar_prefetch=1, grid=(N_SHARD, nt),
            in_specs=[blk(lambda s, i, c: (s, c[0] * nt + i, 0)), blk(lambda s, i, c: (s, i, 0))],
            out_specs=blk(lambda s, i, c: (s, i, 0))),
        compiler_params=_cparams(("parallel", "parallel")))(c_arr, gbuf, recv)


def _reduce_chip_exchange(parts, name):
    n = len(parts)

    def body(*refs):
        ins, outs, send_sems, recv_sems = refs[:n], refs[n:2 * n], refs[2 * n], refs[2 * n + 1]
        x, y, c, chips = _place()
        sends = [_remote(ins[b].at[2 * cx + cy], outs[b].at[j], send_sems, recv_sems, 3 * b + j, (cx, cy, c))
                 for b in range(n) for j, (cx, cy) in enumerate(chips)]
        for cp in sends:
            cp.start()
        for b in range(n):
            for j in range(len(chips)):
                _remote(ins[b].at[0], outs[b].at[j], send_sems, recv_sems, 3 * b + j, (x, y, c)).wait_recv()
        for cp in sends:
            cp.wait_send()

    return pl.pallas_call(
        body, name=name, out_shape=[jax.ShapeDtypeStruct((3,) + p.shape[1:], p.dtype) for p in parts],
        in_specs=[_HBM] * n, out_specs=[_HBM] * n,
        scratch_shapes=[pltpu.SemaphoreType.DMA((3 * n,)), pltpu.SemaphoreType.DMA((3 * n,))],
    )(*parts)


def _chip_add(part, recv, me_arr, tr, name):
    _, half, width = part.shape

    def body(me_ref, own, a0, a1, a2, o_ref):
        f = lambda r: r[...].astype(F32)
        o_ref[...] = ((f(own) + f(a0)) + f(a1)) + f(a2)

    specs = [pl.BlockSpec((None, tr, width), lambda i, me: (me[0], i, 0))]
    specs += [pl.BlockSpec((None, tr, width), functools.partial(lambda i, me, k: (k, i, 0), k=k)) for k in range(3)]
    return pl.pallas_call(
        body, name=name, out_shape=jax.ShapeDtypeStruct((half, width), F32),
        grid_spec=pltpu.PrefetchScalarGridSpec(
            num_scalar_prefetch=1, grid=(half // tr,), in_specs=specs,
            out_specs=pl.BlockSpec((tr, width), lambda i, me: (i, 0))),
        compiler_params=_cparams(("parallel",)))(me_arr, part, recv, recv, recv)


def _reduce_pair_share(rhalves, name):
    n = len(rhalves)

    def body(*refs):
        ins, outs, send_sems, recv_sems = refs[:n], refs[n:2 * n], refs[2 * n], refs[2 * n + 1]
        x, y, c, _ = _place()
        cps = [_remote(ins[b], outs[b], send_sems, recv_sems, b, (x, y, 1 - c)) for b in range(n)]
        for cp in cps:
            cp.start()
        for cp in cps:
            cp.wait()

    return pl.pallas_call(
        body, name=name, out_shape=[jax.ShapeDtypeStruct(r.shape, r.dtype) for r in rhalves],
        in_specs=[_HBM] * n, out_specs=[_HBM] * n,
        scratch_shapes=[pltpu.SemaphoreType.DMA((n,)), pltpu.SemaphoreType.DMA((n,))],
    )(*rhalves)


def _small_allreduce(buf):
    r, width = buf.shape
    n_dev = 8

    def body(x_ref, all_ref, sum_ref, send_sems, recv_sems, local_sem):
        x, y, c, chips = _place()
        me, sibling = (x, y, c), (x, y, 1 - c)

        def rows(px, py, pc):
            return all_ref.at[pl.ds(pl.multiple_of((4 * px + 2 * py + pc) * r, 8), r), :]

        def copy(k, block, to, src=None):
            return _remote(rows(*block) if src is None else src, rows(*block), send_sems, recv_sems, k, to)

        mine = pltpu.make_async_copy(x_ref, rows(*me), local_sem)
        mine.start()
        first = [copy(0, me, sibling, src=x_ref)]
        first += [copy(1 + j, me, (*chip, c), src=x_ref) for j, chip in enumerate(chips)]
        for cp in first:
            cp.start()
        passed = [copy(4 + j, (*chip, c), sibling) for j, chip in enumerate(chips)]
        for j, chip in enumerate(chips):
            copy(1 + j, (*chip, c), me).wait_recv()
            passed[j].start()
        copy(0, sibling, me).wait_recv()
        for j, chip in enumerate(chips):
            copy(4 + j, (*chip, 1 - c), me).wait_recv()
        for cp in first + passed:
            cp.wait_send()
        mine.wait()
        total = all_ref[0:r, :]
        for k in range(1, n_dev):
            total = total + all_ref[k * r:(k + 1) * r, :]
        sum_ref[...] = total

    vm = pl.BlockSpec(memory_space=pltpu.VMEM)
    _, total = pl.pallas_call(
        body, name="small_allreduce",
        out_shape=[jax.ShapeDtypeStruct((n_dev * r, width), buf.dtype), jax.ShapeDtypeStruct((r, width), buf.dtype)],
        in_specs=[vm], out_specs=[vm, vm],
        scratch_shapes=[pltpu.SemaphoreType.DMA((7,)), pltpu.SemaphoreType.DMA((7,)), pltpu.SemaphoreType.DMA],
    )(buf)
    return total


def _row_tile(rows, cap):
    if rows <= cap:
        return rows
    t = (cap // 8) * 8
    while t >= 8:
        if rows % t == 0:
            return t
        t -= 8
    return rows


def _adamw(w, g, m, v, name):
    shape = w.shape
    cols = shape[-1] if len(shape) <= 3 else shape[-2] * shape[-1]
    lead = len(shape) == 3
    w2, g2, m2, v2 = (a if lead else a.reshape(-1, cols) for a in (w, g, m, v))
    rows = shape[1] if lead else w2.shape[0]
    tr, tc = _row_tile(rows, 256), cols
    if tr == rows and rows > 256:
        tc = _tile(cols, 256)

    def body(w_ref, g_ref, m_ref, v_ref, d_ref, mo_ref, vo_ref):
        gv = g_ref[...]
        mn = ADAM_B1 * m_ref[...] + (1.0 - ADAM_B1) * gv
        vn = ADAM_B2 * v_ref[...] + (1.0 - ADAM_B2) * (gv * gv)
        m_hat = mn / (1.0 - ADAM_B1 ** ADAM_STEP)
        v_hat = vn / (1.0 - ADAM_B2 ** ADAM_STEP)
        d_ref[...] = -ADAM_LR * (m_hat / (jnp.sqrt(v_hat) + ADAM_EPS) + ADAM_WD * w_ref[...])
        mo_ref[...] = mn
        vo_ref[...] = vn

    blk = (pl.BlockSpec((None, tr, tc), lambda i, j: (0, i, j)) if lead else pl.BlockSpec((tr, tc), lambda i, j: (i, j)))
    outs = pl.pallas_call(
        body, name=name, grid=(rows // tr, cols // tc), in_specs=[blk] * 4, out_specs=[blk] * 3,
        out_shape=[jax.ShapeDtypeStruct(w2.shape, F32)] * 3,
        compiler_params=_cparams(("parallel", "parallel")))(w2, g2, m2, v2)
    return tuple(o.reshape(shape) for o in outs)


_WEIGHT_NAMES = ('w_in', 'conv_w', 'dn_a_log', 'dn_dt_bias', 'dn_norm_w', 'q_norm_w', 'w_uq', 'kv_norm_w', 'w_uk',
                 'w_uv', 'w_br_dn', 'w_br_mla', 'w_o', 'ln1_g', 'ln1_b', 'w_ffn_in', 'w_ffn_out', 'w_ple',
                 'w_ple_gate', 'ln2_g', 'ln2_b')
_SMALL_NAMES = ('ln1_g', 'ln1_b', 'ln2_g', 'ln2_b', 'q_norm_w', 'kv_norm_w', 'dn_norm_w', 'dn_a_log', 'dn_dt_bias')
_SMALL_GROUP = 8
_CONV_SMALL_ROW = len(_SMALL_NAMES) * _SMALL_GROUP
_CONV_SMALL_ROWS = DN_CONV * QKV_W // FLAT_W


_LOSS_SMALL_ROW = _CONV_SMALL_ROW + 16


def _pack_small(gw, loss_lanes):
    rows = [jnp.pad(gw[n][None, :], ((0, _SMALL_GROUP - 1), (0, FLAT_W - gw[n].shape[0]))) for n in _SMALL_NAMES]
    rows.append(jnp.pad(gw['conv_w'].reshape(_CONV_SMALL_ROWS, FLAT_W), ((0, 16 - _CONV_SMALL_ROWS), (0, 0))))
    rows.append(jnp.pad(loss_lanes, ((0, _SMALL_GROUP - 1), (0, FLAT_W - LANE))))
    return jnp.concatenate(rows, axis=0)


class _Exchange:
    def __init__(self, local, me_chip, c_arr):
        self.local, self.me_chip, self.c_arr = local, me_chip, c_arr
        self.parts = self.arrived = None

    def gather_send(self):
        return _ride_gather_send(self.local)

    def gather_pass(self, sent):
        return _ride_gather_pass(sent)

    def weights(self, gathered):
        return _prep_weights(_unpack_rest(gathered, self.local, self.me_chip))

    def pair_send(self, g):
        self.gbufs = _pack_grads_rest(_unprep_grads_early(g))
        return _ride_pair_exchange(self.gbufs)

    def reduce_send(self, got):
        self.parts = [_pair_add(g_, r_, self.c_arr, tr, "pair_add_%d" % (i + 1))
                      for i, (g_, r_, tr) in enumerate(zip(self.gbufs, got, _ADD_TILES[1:]))]
        return _ride_chip_exchange(self.parts)

    def reduce_arrived(self, arrived):
        self.arrived = list(arrived)

    def in_send(self, g):
        g_in = [_shard_columns(_w_in_grad(g), W_IN_SHARD)]
        got = _reduce_pair_exchange(g_in, "reduce_pair_exchange_w_in")
        self.part_in = _pair_add(g_in[0], got[0], self.c_arr, _ADD_TILES[0], "pair_add_0")
        return _ride_chip_exchange([self.part_in])

    def in_arrived(self, arrived):
        self.arrived_in = list(arrived)

    def small_send(self, g):
        self.small = _pack_small(_unprep_grads_late(g), g['loss_lanes'])
        return _ride_small_gather(self.small)

    def small_arrived(self, arrived):
        self.small_gathered = arrived[0]


def kernel(x, p, positions, w_in, conv_w, dn_a_log, dn_dt_bias, dn_norm_w, q_norm_w, w_uq, kv_norm_w, w_uk, w_uv, w_br_dn, w_br_mla, w_o, ln1_g, ln1_b, w_ffn_in, w_ffn_out, w_ple, w_ple_gate, ln2_g, ln2_b, loss_target, m_w_in, m_conv_w, m_dn_a_log, m_dn_dt_bias, m_dn_norm_w, m_q_norm_w, m_w_uq, m_kv_norm_w, m_w_uk, m_w_uv, m_w_br_dn, m_w_br_mla, m_w_o, m_ln1_g, m_ln1_b, m_w_ffn_in, m_w_ffn_out, m_w_ple, m_w_ple_gate, m_ln2_g, m_ln2_b, v_w_in, v_conv_w, v_dn_a_log, v_dn_dt_bias, v_dn_norm_w, v_q_norm_w, v_w_uq, v_kv_norm_w, v_w_uk, v_w_uv, v_w_br_dn, v_w_br_mla, v_w_o, v_ln1_g, v_ln1_b, v_w_ffn_in, v_w_ffn_out, v_w_ple, v_w_ple_gate, v_ln2_g, v_ln2_b):
    ws = dict(w_in=w_in, conv_w=conv_w, dn_a_log=dn_a_log, dn_dt_bias=dn_dt_bias, dn_norm_w=dn_norm_w, q_norm_w=q_norm_w,
              w_uq=w_uq, kv_norm_w=kv_norm_w, w_uk=w_uk, w_uv=w_uv, w_br_dn=w_br_dn, w_br_mla=w_br_mla, w_o=w_o,
              ln1_g=ln1_g, ln1_b=ln1_b, w_ffn_in=w_ffn_in, w_ffn_out=w_ffn_out, w_ple=w_ple, w_ple_gate=w_ple_gate,
              ln2_g=ln2_g, ln2_b=ln2_b)
    ms = dict(w_in=m_w_in, conv_w=m_conv_w, dn_a_log=m_dn_a_log, dn_dt_bias=m_dn_dt_bias, dn_norm_w=m_dn_norm_w,
              q_norm_w=m_q_norm_w, w_uq=m_w_uq, kv_norm_w=m_kv_norm_w, w_uk=m_w_uk, w_uv=m_w_uv, w_br_dn=m_w_br_dn,
              w_br_mla=m_w_br_mla, w_o=m_w_o, ln1_g=m_ln1_g, ln1_b=m_ln1_b, w_ffn_in=m_w_ffn_in, w_ffn_out=m_w_ffn_out,
              w_ple=m_w_ple, w_ple_gate=m_w_ple_gate, ln2_g=m_ln2_g, ln2_b=m_ln2_b)
    vs = dict(w_in=v_w_in, conv_w=v_conv_w, dn_a_log=v_dn_a_log, dn_dt_bias=v_dn_dt_bias, dn_norm_w=v_dn_norm_w,
              q_norm_w=v_q_norm_w, w_uq=v_w_uq, kv_norm_w=v_kv_norm_w, w_uk=v_w_uk, w_uv=v_w_uv, w_br_dn=v_w_br_dn,
              w_br_mla=v_w_br_mla, w_o=v_w_o, ln1_g=v_ln1_g, ln1_b=v_ln1_b, w_ffn_in=v_w_ffn_in, w_ffn_out=v_w_ffn_out,
              w_ple=v_w_ple, w_ple_gate=v_w_ple_gate, ln2_g=v_ln2_g, ln2_b=v_ln2_b)
    mx, my, mc = lax.axis_index("x"), lax.axis_index("y"), lax.axis_index("c")

    me_chip = 2 * mx + my
    c_arr = jnp.reshape(mc, (1,)).astype(jnp.int32)
    me_arr = jnp.reshape(me_chip, (1,)).astype(jnp.int32)
    sharded = ('w_in', 'w_ffn_in') + tuple(name for name, _, _ in _FLATB_PIECES)

    local = _pack_shards({name: ws[name][0] for name in sharded}, conv_w[0])
    (gathered_in,) = _gather_shards(local[:1], "gather_w_in")
    w_in_full, conv_full = _unpack_w_in(gathered_in, local[0], me_chip)
    small = {n: ws[n][0] for n in _SMALL_NAMES}
    small['conv_w'] = conv_full
    sp = _prep_small(small)
    cosb, sinb = _rope_tables(positions[0])
    exch = _Exchange(local[1:], me_chip, c_arr)

    loss_lanes, dx, g = _local_step(x[0], p[0, 0], cosb, sinb, loss_target[0], _prep_w_in(w_in_full), sp, exch)

    parts = [exch.part_in] + exch.parts
    arrived = exch.arrived_in + exch.arrived
    mine = [_chip_add(p_, r_, me_arr, tr, "chip_add_%d" % i) for i, (p_, r_, tr) in enumerate(zip(parts, arrived, _ADD_TILES))]
    reduced = _unpack_reduced(mine, _reduce_pair_share(mine, "reduce_pair_share"), mc)
    tot = _small_sum(exch.small_gathered, exch.small, jnp.reshape(4 * mx + 2 * my + mc, (1,)).astype(jnp.int32))
    loss = jnp.sum(tot[_LOSS_SMALL_ROW, :LANE])
    gred = {name: reduced[name][None] for name in sharded}
    for i, n in enumerate(_SMALL_NAMES):
        gred[n] = tot[i * _SMALL_GROUP, :ws[n].shape[1]][None]
    conv_tot = tot[_CONV_SMALL_ROW:_CONV_SMALL_ROW + _CONV_SMALL_ROWS].reshape(DN_CONV, QKV_W)
    gred['conv_w'] = lax.dynamic_slice_in_dim(conv_tot, (2 * mx + my) * _CONV_SHARD, _CONV_SHARD, axis=1)[None]

    deltas, new_m, new_v = {}, {}, {}
    for n in _WEIGHT_NAMES:
        if n == 'w_in':
            tr_ = lambda a: jnp.transpose(a, (0, 2, 1))
            g_t = tr_(gred[n].reshape(ws[n].shape))
            outs = _adamw(tr_(ws[n]), g_t, tr_(ms[n]), tr_(vs[n]), "adamw_" + n)
            gred[n] = tr_(g_t)
            deltas[n], new_m[n], new_v[n] = (tr_(o) for o in outs)
            continue
        gred[n] = gred[n].reshape(ws[n].shape)
        deltas[n], new_m[n], new_v[n] = _adamw(ws[n], gred[n], ms[n], vs[n], "adamw_" + n)
    return (loss, dx[None], *[gred[n] for n in _WEIGHT_NAMES], *[deltas[n] for n in _WEIGHT_NAMES],
            *[new_m[n] for n in _WEIGHT_NAMES], *[new_v[n] for n in _WEIGHT_NAMES])
```

```python
import functools

import jax
import jax.numpy as jnp
from jax import lax
from jax.experimental import pallas as pl
from jax.experimental.pallas import tpu as pltpu

F32 = jnp.float32
_CDT = jnp.bfloat16
_MESH = pl.DeviceIdType.MESH

D_MODEL = 1024
PLE_DIM = 256
HEADS = 8
DN_DK = 128
DN_CHUNK = 64
DN_CONV = 4
QKV_W = 3 * HEADS * DN_DK
Q_LORA = 384
KV_LORA = 256
NOPE = 128
ROPE = 64
ROPE_PAD = 128
FFN_HIDDEN = 2816
D_IN = 6864
ROPE_BASE = 10000.0
ALPHA = 2.0 ** 0.25
ATT_SCALE = (NOPE + ROPE) ** -0.5
NEG_BIG = -1e30
ADAM_LR, ADAM_B1, ADAM_B2, ADAM_EPS, ADAM_WD, ADAM_STEP = 0.001, 0.9, 0.999, 1e-08, 0.01, 10

LANE = 128
VMEM_LIMIT = 56 * 1024 * 1024
MM_VMEM_BUDGET = 40 * 1024 * 1024
N_SHARD = 4
FLAT_W = 1024
SMALL_ROWS = 96


def _tile(dim, cap):
    if dim <= cap:
        return dim
    t = (cap // LANE) * LANE
    while t >= LANE:
        if dim % t == 0:
            return t
        t -= LANE
    return dim


def _cparams(sem):
    return pltpu.CompilerParams(dimension_semantics=sem, vmem_limit_bytes=VMEM_LIMIT)


def _mm(a, b, *, name, ta=False, tb=False, add=None, add_scale=1.0, out_dtype=F32, heads=None,
        a_head=None, b_head=None, out_head=None, dims=None, tm=1408, tn=1408, rider=None):
    m, n, k = dims
    tm, tn = _tile(m, tm), _tile(n, tn)
    sa, sb, so = a.dtype.itemsize, b.dtype.itemsize, jnp.dtype(out_dtype).itemsize

    def vmem_need(tk_):
        acc = tm * tn * 4 if tk_ < k else 0
        extra = 2 * tm * tn * 4 if add is not None else 0
        return 2 * (tm * tk_ * sa + tk_ * tn * sb) + 2 * tm * tn * so + acc + extra

    tk = k
    while vmem_need(tk) > MM_VMEM_BUDGET and tk > LANE:
        smaller = _tile(k, tk - LANE)
        if smaller >= tk:
            break
        tk = smaller
    nk = k // tk
    hgrid = () if heads is None else (heads,)
    off = len(hgrid)

    def spec(rows, cols, rtile, ctile, rsel, csel, layout):
        def idx(*g):
            h = g[0] if off else 0
            ri, ci = g[off + rsel], g[off + csel]
            if layout == 'lead':
                return (h, ri, ci)
            if layout == 'col':
                return (ri, h * (cols // ctile) + ci)
            return (ri, ci)
        if layout == 'lead':
            return pl.BlockSpec((None, rtile, ctile), idx)
        return pl.BlockSpec((rtile, ctile), idx)

    a_spec = spec(k, m, tk, tm, 2, 0, a_head) if ta else spec(m, k, tm, tk, 0, 2, a_head)
    b_spec = spec(n, k, tn, tk, 1, 2, b_head) if tb else spec(k, n, tk, tn, 2, 1, b_head)
    o_spec = spec(m, n, tm, tn, 0, 1, out_head)
    in_specs = [a_spec, b_spec]
    args = [a, b]
    if add is not None:
        in_specs.append(spec(m, n, tm, tn, 0, 1, out_head))
        args.append(add)
    dn = (((0 if ta else 1,), (1 if tb else 0,)), ((), ()))

    def body(*refs):
        a_ref, b_ref = refs[0], refs[1]
        prod = lax.dot_general(a_ref[...].astype(_CDT), b_ref[...].astype(_CDT), dn, preferred_element_type=F32)
        if nk == 1:
            o_ref = refs[-1]
            if add is not None:
                prod = prod + refs[2][...].astype(F32) * add_scale
            o_ref[...] = prod.astype(out_dtype)
            return
        o_ref, acc_ref = refs[-2], refs[-1]
        kk = pl.program_id(off + 2)

        @pl.when(kk == 0)
        def _():
            if add is not None:
                acc_ref[...] = refs[2][...].astype(F32) * add_scale
            else:
                acc_ref[...] = jnp.zeros_like(acc_ref)

        acc_ref[...] += prod

        @pl.when(kk == nk - 1)
        def _():
            o_ref[...] = acc_ref[...].astype(out_dtype)

    if out_head == 'lead':
        oshape = (heads, m, n)
    elif out_head == 'col':
        oshape = (m, heads * n)
    else:
        oshape = (m, n)
    grid = hgrid + (m // tm, n // tn, nk)
    scratch = [pltpu.VMEM((tm, tn), F32)] if nk > 1 else []
    if rider is not None:
        (out,), carried = _carried_call(
            body, rider, *_grid_ends(grid), name=name, grid=grid, in_specs=in_specs, out_specs=[o_spec],
            out_shape=[jax.ShapeDtypeStruct(oshape, out_dtype)], scratch_shapes=scratch,
            sem=("arbitrary",) * len(grid), args=tuple(args))
        return out, carried
    sem = ("parallel",) * (off + 2) + ("arbitrary",)
    return pl.pallas_call(
        body, name=name, grid=grid, in_specs=in_specs, out_specs=o_spec,
        out_shape=jax.ShapeDtypeStruct(oshape, out_dtype), scratch_shapes=scratch,
        compiler_params=_cparams(sem))(*args)


def _mm2(a, b, **kw):
    ta, tb = kw.get('ta', False), kw.get('tb', False)
    m = a.shape[1] if ta else a.shape[0]
    k = a.shape[0] if ta else a.shape[1]
    n = b.shape[0] if tb else b.shape[1]
    return _mm(a, b, dims=(m, n, k), **kw)


def _rowwise(fn, rows, bcast, outs, reds=(), *, name, tm=256, heads=None):
    t = rows[0][0].shape[0]
    tm = min(tm, t)
    hn = 1 if heads is None else heads
    in_specs, args = [], []
    for arr, width, base, per_head in rows:
        in_specs.append(pl.BlockSpec((tm, width), functools.partial(
            lambda i, h, base, per_head: (i, base + (h if per_head else 0)), base=base, per_head=per_head)))
        args.append(arr)
    for arr in bcast:
        in_specs.append(pl.BlockSpec(arr.shape, lambda i, h: (0, 0)))
        args.append(arr)
    out_specs, out_shape = [], []
    for total, width, per_head, dt in outs:
        out_specs.append(pl.BlockSpec((tm, width), functools.partial(
            lambda i, h, per_head: (i, h if per_head else 0), per_head=per_head)))
        out_shape.append(jax.ShapeDtypeStruct((t, total), dt))
    for shp in reds:
        out_specs.append(pl.BlockSpec(shp, lambda i, h: (0, 0)))
        out_shape.append(jax.ShapeDtypeStruct(shp, F32))
    n_in, n_out, n_red = len(args), len(outs), len(reds)

    def body(*refs):
        i, h = pl.program_id(0), pl.program_id(1)
        vals = fn(h, *[r[...] for r in refs[:n_in]])
        for r, v in zip(refs[n_in:n_in + n_out], vals[:n_out]):
            r[...] = v.astype(r.dtype)
        if n_red:
            @pl.when((i == 0) & (h == 0))
            def _():
                for r in refs[n_in + n_out:]:
                    r[...] = jnp.zeros_like(r)
            for r, v in zip(refs[n_in + n_out:], vals[n_out:]):
                r[...] += v

    sem = ("arbitrary", "arbitrary") if n_red else ("parallel", "parallel")
    res = pl.pallas_call(body, name=name, grid=(t // tm, hn), in_specs=in_specs, out_specs=out_specs,
                         out_shape=out_shape, compiler_params=_cparams(sem))(*args)
    return tuple(res)


def _sigmoid(x):
    return 1.0 / (1.0 + jnp.exp(-x))


def _silu(x):
    return x * _sigmoid(x)


def _softplus(x):
    return jnp.maximum(x, 0.0) + jnp.log(1.0 + jnp.exp(-jnp.abs(x)))


def _layer_norm(t, g, b):
    mu = jnp.mean(t, axis=-1, keepdims=True)
    d = t - mu
    var = jnp.mean(d * d, axis=-1, keepdims=True)
    return d * lax.rsqrt(var + 1e-5) * g + b


def _rms_norm(t, w):
    return t * lax.rsqrt(jnp.mean(t * t, axis=-1, keepdims=True) + 1e-6) * w


def _swap_rope_halves(t):
    lane = lax.broadcasted_iota(jnp.int32, t.shape, 1) % ROPE_PAD
    n = t.shape[1]
    up = pltpu.roll(t, n - ROPE // 2, axis=1)
    dn = pltpu.roll(t, ROPE // 2, axis=1)
    return jnp.where(lane < ROPE // 2, up, jnp.where(lane < ROPE, dn, 0.0))


def _rope(t, cosb, sinb):
    reps = t.shape[1] // ROPE_PAD
    c = jnp.tile(cosb, (1, reps)) if reps > 1 else cosb
    s = jnp.tile(sinb, (1, reps)) if reps > 1 else sinb
    return t * c + _swap_rope_halves(t) * s


def _rope_bwd(d, cosb, sinb):
    reps = d.shape[1] // ROPE_PAD
    c = jnp.tile(cosb, (1, reps)) if reps > 1 else cosb
    s = jnp.tile(sinb, (1, reps)) if reps > 1 else sinb
    return d * c + _swap_rope_halves(d * s)


_CONV_ROWS = 256
_CONV_COLS = 256


def _conv_window(ref, r0, lo, hi, t):
    parts = []
    start, stop = r0 - lo, r0 + _CONV_ROWS + hi
    if start < 0:
        parts.append(jnp.zeros((-start, ref.shape[1]), F32))
        start = 0
    tail = max(stop - t, 0)
    parts.append(ref[start:stop - tail, :].astype(F32))
    if tail:
        parts.append(jnp.zeros((tail, ref.shape[1]), F32))
    return parts[0] if len(parts) == 1 else jnp.concatenate(parts, axis=0)


def _conv_taps(win, w_ref, n_out):
    acc = win[8:8 + n_out] * w_ref[DN_CONV - 1:DN_CONV, :]
    for i in range(DN_CONV - 1):
        acc = acc + pltpu.roll(win, DN_CONV - 1 - i, axis=0)[8:8 + n_out] * w_ref[i:i + 1, :]
    return acc


def _conv_silu(x, w):
    t, ch = x.shape

    def body(x_ref, w_ref, o_ref):
        for r in range(t // _CONV_ROWS):
            r0 = r * _CONV_ROWS
            c = _conv_taps(_conv_window(x_ref, r0, 8, 0, t), w_ref, _CONV_ROWS)
            o_ref[r0:r0 + _CONV_ROWS, :] = _silu(c)

    return pl.pallas_call(
        body, name="conv_silu", grid=(ch // _CONV_COLS,),
        in_specs=[pl.BlockSpec((t, _CONV_COLS), lambda j: (0, j)), pl.BlockSpec((DN_CONV, _CONV_COLS), lambda j: (0, j))],
        out_specs=pl.BlockSpec((t, _CONV_COLS), lambda j: (0, j)),
        out_shape=jax.ShapeDtypeStruct((t, ch), F32), compiler_params=_cparams(("parallel",)))(x, w)


def _conv_silu_bwd(x, w, dys):
    t, ch = x.shape
    per = ch // len(dys) // _CONV_COLS

    def body(x_ref, w_ref, *rest):
        dy_refs, (dx_ref, dw_ref) = rest[:len(dys)], rest[len(dys):]
        sec = pl.program_id(0) // per
        dws = [jnp.zeros((1, _CONV_COLS), F32) for _ in range(DN_CONV)]
        for r in range(t // _CONV_ROWS):
            r0 = r * _CONV_ROWS
            n_ext = _CONV_ROWS + 8
            xw = _conv_window(x_ref, r0, 8, 8, t)
            c = _conv_taps(xw, w_ref, n_ext)
            sg = _sigmoid(c)
            dy = _conv_window(dy_refs[-1], r0, 0, 8, t)
            for k in range(len(dys) - 2, -1, -1):
                dy = jnp.where(sec == k, _conv_window(dy_refs[k], r0, 0, 8, t), dy)
            ds = dy * (sg * (1.0 + c * (1.0 - sg)))
            x0 = xw[8:8 + _CONV_ROWS]
            dx = jnp.zeros((_CONV_ROWS, _CONV_COLS), F32)
            for i in range(DN_CONV):
                sh = DN_CONV - 1 - i
                ds_up = (ds if sh == 0 else pltpu.roll(ds, n_ext - sh, axis=0))[:_CONV_ROWS]
                dx = dx + ds_up * w_ref[i:i + 1, :]
                dws[i] = dws[i] + jnp.sum(x0 * ds_up, axis=0, keepdims=True)
            dx_ref[r0:r0 + _CONV_ROWS, :] = dx.astype(dx_ref.dtype)
        for i in range(DN_CONV):
            dw_ref[i:i + 1, :] = dws[i]

    blk = pl.BlockSpec((t, _CONV_COLS), lambda j: (0, j))
    wblk = pl.BlockSpec((DN_CONV, _CONV_COLS), lambda j: (0, j))
    dy_specs = [pl.BlockSpec((t, _CONV_COLS), functools.partial(lambda j, k: (0, jnp.clip(j - k * per, 0, per - 1)), k=k))
                for k in range(len(dys))]
    return pl.pallas_call(
        body, name="conv_silu_bwd", grid=(ch // _CONV_COLS,), in_specs=[blk, wblk] + dy_specs, out_specs=[blk, wblk],
        out_shape=[jax.ShapeDtypeStruct((t, ch), _CDT), jax.ShapeDtypeStruct((DN_CONV, ch), F32)],
        compiler_params=_cparams(("arbitrary",)))(x, w, *dys)


_PA_ROWS = 512


def _bmm(a, b, spec):
    return jnp.einsum(spec, a.astype(_CDT), b.astype(_CDT), preferred_element_type=F32)


def _split16(a):
    hi = a.astype(jnp.bfloat16)
    return hi, (a - hi.astype(F32)).astype(jnp.bfloat16)


def _bmm3(a, b, spec):
    ah, al = _split16(a)
    bh, bl = _split16(b)
    e = lambda p, q: jnp.einsum(spec, p, q, preferred_element_type=F32)
    return e(ah, bh) + (e(ah, bl) + e(al, bh))


def _split3(b):
    b0 = b.astype(jnp.bfloat16)
    r1 = b - b0.astype(F32)
    b1 = r1.astype(jnp.bfloat16)
    return b0, b1, (r1 - b1.astype(F32)).astype(jnp.bfloat16)


@functools.partial(jax.custom_vjp, nondiff_argnums=(2, 3))
def _select_mm(sel, b, spec, spec_t):
    return sum(jnp.einsum(spec, sel, t, preferred_element_type=F32) for t in _split3(b))


def _select_mm_fwd(sel, b, spec, spec_t):
    return _select_mm(sel, b, spec, spec_t), sel


def _select_mm_bwd(spec, spec_t, sel, ct):
    return jnp.zeros_like(sel), sum(jnp.einsum(spec_t, sel, t, preferred_element_type=F32) for t in _split3(ct))


_select_mm.defvjp(_select_mm_fwd, _select_mm_bwd)


def _tri_inverse(l_mat, eye):
    pw = -l_mat
    t_inv = eye + pw
    for _ in range(5):
        pw = _bmm3(pw, pw, 'bij,bjk->bik')
        t_inv = t_inv + _bmm3(t_inv, pw, 'bij,bjk->bik')
    return t_inv


@jax.custom_vjp
def _tri_inverse_saved(l_mat, t_saved):
    return t_saved


def _tri_inverse_saved_fwd(l_mat, t_saved):
    return t_saved, t_saved


def _tri_inverse_saved_bwd(t_saved, dt):
    left = _bmm3(t_saved, dt, 'bji,bjk->bik')
    return -_bmm3(left, t_saved, 'bij,bkj->bik'), jnp.zeros_like(t_saved)


_tri_inverse_saved.defvjp(_tri_inverse_saved_fwd, _tri_inverse_saved_bwd)


def _phase_a(h, q, k, v, ba, alog, dtb, t_saved=None):
    r = q.shape[0]
    nb = r // DN_CHUNK
    c = DN_CHUNK
    lane = lax.broadcasted_iota(jnp.int32, (1, LANE), 1)
    selb = (lane == h).astype(F32)
    sela = (lane == h + HEADS).astype(F32)
    b_raw = jnp.sum(ba * selb, axis=1, keepdims=True)
    a_raw = jnp.sum(ba * sela, axis=1, keepdims=True)
    al = jnp.sum(alog * selb, axis=1, keepdims=True)
    dt = jnp.sum(dtb * selb, axis=1, keepdims=True)
    beta = jnp.broadcast_to(_sigmoid(b_raw), (r, LANE))
    g = jnp.broadcast_to(-jnp.exp(al) * _softplus(a_raw + dt), (r, LANE))
    qn = q * lax.rsqrt(jnp.sum(q * q, -1, keepdims=True) + 1e-6) * (DN_DK ** -0.5)
    kn = k * lax.rsqrt(jnp.sum(k * k, -1, keepdims=True) + 1e-6)
    q3, k3, v3 = qn.reshape(nb, c, LANE), kn.reshape(nb, c, LANE), v.reshape(nb, c, LANE)
    b3, g3 = beta.reshape(nb, c, LANE), g.reshape(nb, c, LANE)
    ri = lax.broadcasted_iota(jnp.int32, (nb, c, c), 1)
    ci = lax.broadcasted_iota(jnp.int32, (nb, c, c), 2)
    tril, strict = ri >= ci, ri > ci
    gc = _select_mm(tril.astype(jnp.bfloat16), g3, 'bij,bjd->bid', 'bij,bid->bjd')
    onehot = (lax.broadcasted_iota(jnp.int32, (nb, c, LANE), 2) == 0).astype(jnp.bfloat16)
    g_row = _select_mm(onehot, gc, 'bid,bjd->bij', 'bid,bij->bjd')
    diff = gc[:, :, :c] - g_row
    decay = jnp.where(tril, jnp.exp(jnp.where(tril, diff, 0.0)), 0.0)
    kb = k3 * b3
    l_mat = jnp.where(strict, _bmm(kb, k3, 'bid,bjd->bij') * decay, 0.0)
    if t_saved is None:
        t_inv = _tri_inverse(l_mat, (ri == ci).astype(F32))
    else:
        t_inv = _tri_inverse_saved(l_mat, t_saved.reshape(nb, c, c))
    eg = jnp.exp(gc)
    u = _bmm(t_inv, v3 * b3, 'bij,bje->bie')
    w = _bmm(t_inv, kb * eg, 'bij,bje->bie')
    intra = jnp.where(tril, _bmm(q3, k3, 'bid,bjd->bij') * decay, 0.0)
    qd = q3 * eg
    gl = jnp.sum(g3, axis=1, keepdims=True)
    kt = k3 * jnp.exp(gl - gc)
    outs = (u.reshape(r, LANE), w.reshape(r, LANE), qd.reshape(r, LANE), kt.reshape(r, LANE),
            intra.reshape(r, c), gl.reshape(nb, LANE))
    if t_saved is not None:
        return outs
    qd2 = qd - _bmm(intra, w, 'bij,bjd->bid')
    au = _bmm(intra, u, 'bij,bje->bie')
    return outs + (t_inv.reshape(r, c), qd2.reshape(r, LANE), au.reshape(r, LANE))


def _pa_specs(t):
    rr = min(_PA_ROWS, t)
    nb = rr // DN_CHUNK
    qkv = [pl.BlockSpec((rr, LANE), functools.partial(lambda i, h, o: (i, o + h), o=o)) for o in (0, HEADS, 2 * HEADS)]
    ba = pl.BlockSpec((rr, LANE), lambda i, h: (i, 3))
    vec = pl.BlockSpec((1, LANE), lambda i, h: (0, 0))
    row = pl.BlockSpec((rr, LANE), lambda i, h: (i, h))
    intra = pl.BlockSpec((None, rr, DN_CHUNK), lambda i, h: (h, i, 0))
    gl = pl.BlockSpec((nb, LANE), lambda i, h: (i, h))
    return rr, qkv, ba, vec, row, intra, gl


def _grid_ends(grid):
    first = lambda: functools.reduce(jnp.logical_and, [pl.program_id(a) == 0 for a in range(len(grid))])
    last = lambda: functools.reduce(jnp.logical_and, [pl.program_id(a) == n - 1 for a, n in enumerate(grid)])
    return first, last


def _delta_local(qkv_act, pm, alog, dtb, rider=None):
    t = qkv_act.shape[0]
    rr, qkv, ba, vec, row, intra, gl = _pa_specs(t)

    def body(q, k, v, b, al, dt, *outs):
        vals = _phase_a(pl.program_id(1), q[...], k[...], v[...], b[...], al[...], dt[...])
        for o, val in zip(outs, vals):
            o[...] = val

    wide = jax.ShapeDtypeStruct((t, HEADS * LANE), F32)
    sq = jax.ShapeDtypeStruct((HEADS, t, DN_CHUNK), F32)
    grid = (t // rr, HEADS)
    return _carried_call(
        body, rider, *_grid_ends(grid), name="delta_local", grid=grid, in_specs=qkv + [ba, vec, vec],
        out_specs=[row] * 4 + [intra, gl, intra, row, row],
        out_shape=[wide] * 4 + [sq, jax.ShapeDtypeStruct((t // DN_CHUNK, HEADS * LANE), F32), sq, wide, wide],
        scratch_shapes=[], sem=("arbitrary", "arbitrary"), args=(qkv_act, qkv_act, qkv_act, pm, alog, dtb))


def _delta_local_bwd(qkv_act, pm, alog, dtb, t_inv, du, dw, dqd, dkt, dintra, dgl, rider=None):
    t = qkv_act.shape[0]
    rr, qkv, ba, vec, row, intra, gl = _pa_specs(t)

    def body(q, k, v, b, al, dt, ti, du_r, dw_r, dqd_r, dkt_r, di_r, dgl_r, dq_o, dk_o, dv_o, dba_o, dal_o, ddt_o):
        i, h = pl.program_id(0), pl.program_id(1)
        t_saved = ti[...]
        _, vjp = jax.vjp(lambda *a: _phase_a(h, *a, t_saved=t_saved), q[...], k[...], v[...], b[...], al[...], dt[...])
        dq, dk, dv, dba, dal, ddt = vjp((du_r[...], dw_r[...], dqd_r[...], dkt_r[...], di_r[...], dgl_r[...]))
        dq_o[...], dk_o[...], dv_o[...] = dq, dk, dv

        @pl.when(h == 0)
        def _():
            dba_o[...] = jnp.zeros_like(dba_o)

        @pl.when((h == 0) & (i == 0))
        def _():
            dal_o[...] = jnp.zeros_like(dal_o)
            ddt_o[...] = jnp.zeros_like(ddt_o)

        dba_o[...] += dba
        dal_o[...] += dal
        ddt_o[...] += ddt

    wide = jax.ShapeDtypeStruct((t, HEADS * LANE), F32)
    vshape = jax.ShapeDtypeStruct((1, LANE), F32)
    grid = (t // rr, HEADS)
    return _carried_call(
        body, rider, *_grid_ends(grid), name="delta_local_bwd", grid=grid,
        in_specs=qkv + [ba, vec, vec, intra] + [row] * 4 + [intra, gl],
        out_specs=[row] * 3 + [pl.BlockSpec((rr, LANE), lambda i, h: (i, 0)), vec, vec],
        out_shape=[wide] * 3 + [jax.ShapeDtypeStruct((t, LANE), F32), vshape, vshape],
        scratch_shapes=[], sem=("arbitrary", "arbitrary"),
        args=(qkv_act, qkv_act, qkv_act, pm, alog, dtb, t_inv, du, dw, dqd, dkt, dintra, dgl))


_SCAN_ROWS = 512


def _dot(a, b, dn):
    return lax.dot_general(a.astype(_CDT), b.astype(_CDT), (dn, ((), ())), preferred_element_type=F32)


_NN = ((1,), (0,))
_NT = ((1,), (1,))
_TN = ((0,), (0,))


def _delta_scan(u, w, qd, kt, au, gl, rider=None):
    t = u.shape[0]
    rr = min(_SCAN_ROWS, t)
    nc = rr // DN_CHUNK

    def body(u_ref, w_ref, qd_ref, kt_ref, au_ref, gl_ref, o_ref, sall_ref, s_scr):
        @pl.when(pl.program_id(0) == 0)
        def _():
            s_scr[...] = jnp.zeros_like(s_scr)

        def chunk(c, carry):
            r0 = pl.multiple_of(c * DN_CHUNK, DN_CHUNK)
            rows = pl.ds(r0, DN_CHUNK)
            e = jnp.exp(gl_ref[pl.ds(c, 1), :])
            states = [s_scr[h] for h in range(HEADS)]
            u_c, w_c, qd_c, kt_c, au_c = u_ref[rows, :], w_ref[rows, :], qd_ref[rows, :], kt_ref[rows, :], au_ref[rows, :]
            o_new, s_new = [], []
            for h in range(HEADS):
                cs = slice(h * LANE, (h + 1) * LANE)
                s = states[h]
                both = _dot(jnp.concatenate([w_c[:, cs], qd_c[:, cs]], axis=0), s, _NN)
                v_new = u_c[:, cs] - both[:DN_CHUNK]
                o_new.append(both[DN_CHUNK:] + au_c[:, cs])
                s_new.append(s * e[:, cs] + _dot(kt_c[:, cs], v_new, _TN))
            o_ref[rows, :] = jnp.concatenate(o_new, axis=1)
            for h in range(HEADS):
                sall_ref[c, h] = states[h]
                s_scr[h] = s_new[h]
            return carry

        lax.fori_loop(0, nc, chunk, 0)

    row = pl.BlockSpec((rr, HEADS * LANE), lambda i: (i, 0))
    grid = (t // rr,)
    return _carried_call(
        body, rider, *_grid_ends(grid), name="delta_scan", grid=grid,
        in_specs=[row] * 5 + [pl.BlockSpec((nc, HEADS * LANE), lambda i: (i, 0))],
        out_specs=[row, pl.BlockSpec((nc, HEADS, LANE, LANE), lambda i: (i, 0, 0, 0))],
        out_shape=[jax.ShapeDtypeStruct((t, HEADS * LANE), F32),
                   jax.ShapeDtypeStruct((t // DN_CHUNK, HEADS, LANE, LANE), F32)],
        scratch_shapes=[pltpu.VMEM((HEADS, LANE, LANE), F32)], sem=("arbitrary",), args=(u, w, qd, kt, au, gl))


def _delta_scan_bwd(u, w, qd, kt, intra, gl, sall, do, rider=None):
    t = u.shape[0]
    rr = min(_SCAN_ROWS, t)
    nc = rr // DN_CHUNK
    ng = t // rr

    def body(u_ref, w_ref, qd_ref, kt_ref, a_ref, gl_ref, sall_ref, do_ref,
             du_ref, dw_ref, dqd_ref, dkt_ref, da_ref, dgl_ref, ds_scr):
        @pl.when(pl.program_id(0) == 0)
        def _():
            ds_scr[...] = jnp.zeros_like(ds_scr)

        def chunk(cc, carry):
            c = nc - 1 - cc
            r0 = pl.multiple_of(c * DN_CHUNK, DN_CHUNK)
            rows = pl.ds(r0, DN_CHUNK)
            e = jnp.exp(gl_ref[pl.ds(c, 1), :])
            states = [sall_ref[c, h] for h in range(HEADS)]
            ds_outs = [ds_scr[h] for h in range(HEADS)]
            u_a, w_a, kt_a, qd_a, do_a = u_ref[rows, :], w_ref[rows, :], kt_ref[rows, :], qd_ref[rows, :], do_ref[rows, :]
            a_a = [a_ref[h, rows, :] for h in range(HEADS)]
            da, dqd, dkt, du, dw, dgl, ds_new = [], [], [], [], [], [], []
            for h in range(HEADS):
                cs = slice(h * LANE, (h + 1) * LANE)
                s, ds_out = states[h], ds_outs[h]
                w_c, kt_c, qd_c, do_c = w_a[:, cs], kt_a[:, cs], qd_a[:, cs], do_a[:, cs]
                v_new = u_a[:, cs] - _dot(w_c, s, _NN)
                dv_new = _dot(a_a[h], do_c, _TN) + _dot(kt_c, ds_out, _NN)
                cots = jnp.concatenate([do_c, dv_new], axis=0)
                both = _dot(cots, s, _NT)
                dqd.append(both[:DN_CHUNK])
                dw.append(-both[DN_CHUNK:])
                da.append(_dot(do_c, v_new, _NT))
                dkt.append(_dot(v_new, ds_out, _NT))
                du.append(dv_new)
                eh = e[:, cs]
                dgl.append(jnp.broadcast_to(jnp.sum(ds_out * s, axis=0, keepdims=True) * eh, (8, LANE)))
                ds_new.append(ds_out * eh + _dot(jnp.concatenate([qd_c, -w_c], axis=0), cots, _TN))
            cat = lambda parts: jnp.concatenate(parts, axis=1)
            dqd_ref[rows, :], dkt_ref[rows, :], du_ref[rows, :], dw_ref[rows, :] = cat(dqd), cat(dkt), cat(du), cat(dw)
            dgl_ref[pl.ds(pl.multiple_of(c * 8, 8), 8), :] = cat(dgl)
            for h in range(HEADS):
                da_ref[h, rows, :] = da[h]
                ds_scr[h] = ds_new[h]
            return carry

        lax.fori_loop(0, nc, chunk, 0)

    rev = lambda i: (ng - 1 - i, 0)
    row = pl.BlockSpec((rr, HEADS * LANE), rev)
    a_spec = pl.BlockSpec((HEADS, rr, DN_CHUNK), lambda i: (0, ng - 1 - i, 0))
    gl_spec = pl.BlockSpec((nc, HEADS * LANE), rev)
    wide = jax.ShapeDtypeStruct((t, HEADS * LANE), F32)
    outs, carried = _carried_call(
        body, rider, *_grid_ends((ng,)), name="delta_scan_bwd", grid=(ng,),
        in_specs=[row] * 4 + [a_spec, gl_spec, pl.BlockSpec((nc, HEADS, LANE, LANE), lambda i: (ng - 1 - i, 0, 0, 0)), row],
        out_specs=[row] * 4 + [a_spec, pl.BlockSpec((nc * 8, HEADS * LANE), rev)],
        out_shape=[wide] * 4 + [jax.ShapeDtypeStruct((HEADS, t, DN_CHUNK), F32),
                                jax.ShapeDtypeStruct((t // DN_CHUNK * 8, HEADS * LANE), F32)],
        scratch_shapes=[pltpu.VMEM((HEADS, LANE, LANE), F32)], sem=("arbitrary",), args=(u, w, qd, kt, intra, gl, sall, do))
    return tuple(outs[:5]) + (outs[5].reshape(t // DN_CHUNK, 8, HEADS * LANE)[:, 0, :],), carried


_ATT_TILE = 512


def _kv_rows(j, tk):
    return pl.ds(pl.multiple_of(j * tk, tk), tk)


def _att_scores(ql, qr, ckv_ref, kr_ref, j, tk):
    ks = _kv_rows(j, tk)
    return (_dot(ql, ckv_ref[ks, :], _NT) + _dot(qr, kr_ref[ks, :], _NT)) * ATT_SCALE


def _diag_mask(s):
    qi = lax.broadcasted_iota(jnp.int32, s.shape, 0)
    ki = lax.broadcasted_iota(jnp.int32, s.shape, 1)
    return jnp.where(ki <= qi, s, NEG_BIG)


def _attention(qn, qr_pre, cosb, sinb, ckv, kr, wuk, wuv):
    t = ckv.shape[0]
    tq = min(_ATT_TILE, t)

    nl = tq // LANE

    def lane_fold(v, op):
        out = v[:, :LANE]
        for k in range(1, nl):
            out = op(out, v[:, k * LANE:(k + 1) * LANE])
        return out

    def body(qn_ref, qr_ref, cos_ref, sin_ref, ckv_ref, kr_ref, wuk_ref, wuv_ref, o_ref, lse_ref, qrope_ref, omla_ref,
             s_all, m_lanes, l_lanes, acc_scr):
        h, qi = pl.program_id(0), pl.program_id(1)
        q_lat = _dot(qn_ref[...], wuk_ref[h], _NT).astype(_CDT)
        q_rope = _rope(qr_ref[...], cos_ref[...], sin_ref[...]).astype(qrope_ref.dtype)
        qrope_ref[...] = q_rope
        m_lanes[...] = jnp.full_like(m_lanes, NEG_BIG)

        def scores(j, masked):
            s = _att_scores(q_lat, q_rope, ckv_ref, kr_ref, j, tq)
            if masked:
                s = _diag_mask(s)
            s_all[j] = s
            m_lanes[...] = jnp.maximum(m_lanes[...], lane_fold(s, jnp.maximum))

        def scores_body(j, carry):
            scores(j, False)
            return carry

        lax.fori_loop(0, qi, scores_body, 0)
        scores(qi, True)
        m = jnp.max(m_lanes[...], axis=-1, keepdims=True)
        mb = jnp.broadcast_to(m, (tq, LANE))
        l_lanes[...] = jnp.zeros_like(l_lanes)
        acc_scr[...] = jnp.zeros_like(acc_scr)

        def weigh(j, carry):
            s = s_all[j]
            p = jnp.concatenate([jnp.exp(s[:, k * LANE:(k + 1) * LANE] - mb) for k in range(nl)], axis=1)
            l_lanes[...] += lane_fold(p, jnp.add)
            acc_scr[...] += _dot(p, ckv_ref[_kv_rows(j, tq), :], _NN)
            return carry

        lax.fori_loop(0, qi + 1, weigh, 0)
        l = jnp.sum(l_lanes[...], axis=-1, keepdims=True)
        out = acc_scr[...] / l
        o_ref[...] = out
        lse_ref[...] = m + jnp.log(l)
        omla_ref[...] = _dot(out, wuv_ref[h], _NN).astype(omla_ref.dtype)

    col = pl.BlockSpec((tq, LANE), lambda h, i: (i, h))
    table = pl.BlockSpec((tq, ROPE_PAD), lambda h, i: (i, 0))
    wspec = pl.BlockSpec((HEADS, KV_LORA, NOPE), lambda h, i: (0, 0, 0))
    return pl.pallas_call(
        body, name="attention", grid=(HEADS, t // tq),
        in_specs=[col, col, table, table, pl.BlockSpec((t, KV_LORA), lambda h, i: (0, 0)),
                  pl.BlockSpec((t, ROPE_PAD), lambda h, i: (0, 0)), wspec, wspec],
        out_specs=[pl.BlockSpec((None, tq, KV_LORA), lambda h, i: (h, i, 0)),
                   pl.BlockSpec((None, tq, 1), lambda h, i: (h, i, 0)), col, col],
        out_shape=[jax.ShapeDtypeStruct((HEADS, t, KV_LORA), F32), jax.ShapeDtypeStruct((HEADS, t, 1), F32),
                   jax.ShapeDtypeStruct((t, HEADS * ROPE_PAD), _CDT), jax.ShapeDtypeStruct((t, HEADS * NOPE), _CDT)],
        scratch_shapes=[pltpu.VMEM((t // tq, tq, tq), F32), pltpu.VMEM((tq, LANE), F32), pltpu.VMEM((tq, LANE), F32),
                        pltpu.VMEM((tq, KV_LORA), F32)],
        compiler_params=_cparams(("parallel", "parallel")))(qn, qr_pre, cosb, sinb, ckv, kr, wuk, wuv)


def _attention_bwd(qn, qr, cosb, sinb, ckv, kr, wuk, wuv, out, lse, do_mla):
    t = ckv.shape[0]
    tq = min(_ATT_TILE, t)

    def body(qn_ref, qr_ref, cos_ref, sin_ref, ckv_ref, kr_ref, wuk_ref, wuv_ref, o_ref, lse_ref, do_ref,
             dql_ref, dqr_ref, dqn_ref, dckv_ref, dkr_ref, dql_scr, dqr_scr):
        h, qi = pl.program_id(0), pl.program_id(1)

        @pl.when((h == 0) & (qi == 0))
        def _():
            dckv_ref[...] = jnp.zeros_like(dckv_ref)
            dkr_ref[...] = jnp.zeros_like(dkr_ref)

        q_lat = _dot(qn_ref[...], wuk_ref[h], _NT).astype(_CDT)
        q_rope = qr_ref[...]
        d_out = _dot(do_ref[...], wuv_ref[h], _NT)
        d_o = d_out.astype(_CDT)
        lse_v = lse_ref[...]
        dsum = jnp.sum(d_out * o_ref[...], axis=-1, keepdims=True)
        dql_scr[...] = jnp.zeros_like(dql_scr)
        dqr_scr[...] = jnp.zeros_like(dqr_scr)

        def step(j, masked):
            ks = _kv_rows(j, tq)
            s = _att_scores(q_lat, q_rope, ckv_ref, kr_ref, j, tq)
            if masked:
                s = _diag_mask(s)
            p = jnp.exp(s - lse_v)
            kv = ckv_ref[ks, :]
            ds = (p * (_dot(d_o, kv, _NT) - dsum) * ATT_SCALE).astype(_CDT)
            pb = p.astype(_CDT)
            dql_scr[...] += _dot(ds, kv, _NN)
            dqr_scr[...] += _dot(ds, kr_ref[ks, :], _NN)
            dckv_ref[ks, :] += _dot(pb, d_o, _TN) + _dot(ds, q_lat, _TN)
            dkr_ref[ks, :] += _dot(ds, q_rope, _TN)

        def loop_body(j, carry):
            step(j, False)
            return carry

        lax.fori_loop(0, qi, loop_body, 0)
        step(qi, True)
        dql = dql_scr[...].astype(dql_ref.dtype)
        dql_ref[...] = dql
        dqn_ref[...] = _dot(dql, wuk_ref[h], _NN).astype(dqn_ref.dtype)
        dqr_ref[...] = _rope_bwd(dqr_scr[...], cos_ref[...], sin_ref[...]).astype(dqr_ref.dtype)

    lat = pl.BlockSpec((None, tq, KV_LORA), lambda h, i: (h, i, 0))
    col = pl.BlockSpec((tq, LANE), lambda h, i: (i, h))
    table = pl.BlockSpec((tq, ROPE_PAD), lambda h, i: (i, 0))
    kfull = pl.BlockSpec((t, KV_LORA), lambda h, i: (0, 0))
    rfull = pl.BlockSpec((t, ROPE_PAD), lambda h, i: (0, 0))
    wspec = pl.BlockSpec((HEADS, KV_LORA, NOPE), lambda h, i: (0, 0, 0))
    wide = jax.ShapeDtypeStruct((t, HEADS * LANE), _CDT)
    return pl.pallas_call(
        body, name="attention_bwd", grid=(HEADS, t // tq),
        in_specs=[col, col, table, table, kfull, rfull, wspec, wspec, lat,
                  pl.BlockSpec((None, tq, 1), lambda h, i: (h, i, 0)), col],
        out_specs=[lat, col, col, kfull, rfull],
        out_shape=[jax.ShapeDtypeStruct((HEADS, t, KV_LORA), _CDT), wide, wide,
                   jax.ShapeDtypeStruct((t, KV_LORA), F32), jax.ShapeDtypeStruct((t, ROPE_PAD), F32)],
        scratch_shapes=[pltpu.VMEM((tq, KV_LORA), F32), pltpu.VMEM((tq, ROPE_PAD), F32)],
        compiler_params=_cparams(("arbitrary", "arbitrary")))(qn, qr, cosb, sinb, ckv, kr, wuk, wuv, out, lse, do_mla)


_DX_ROWS = 256


def _dx_fused(pairs, add, add_scale, rider=None, name="b_dx"):
    t, d = add.shape
    tm = min(_DX_ROWS, t)
    n = len(pairs)

    def body(*refs):
        acc = refs[2 * n][...] * add_scale
        for i in range(n):
            acc = acc + _dot(refs[i][...], refs[n + i][...], _NT)
        refs[2 * n + 1][...] = acc

    in_specs = [pl.BlockSpec((tm, a.shape[1]), lambda i: (i, 0)) for a, _ in pairs]
    in_specs += [pl.BlockSpec(w.shape, lambda i: (0, 0)) for _, w in pairs]
    row = pl.BlockSpec((tm, d), lambda i: (i, 0))
    grid = (t // tm,)
    (dx,), carried = _carried_call(
        body, rider, *_grid_ends(grid), name=name, grid=grid, in_specs=in_specs + [row], out_specs=[row],
        out_shape=[jax.ShapeDtypeStruct((t, d), F32)], scratch_shapes=[], sem=("arbitrary",),
        args=tuple(a for a, _ in pairs) + tuple(w for _, w in pairs) + (add,))
    return dx, carried


def _ffn_in_swiglu(a, w):
    t, k = a.shape
    hid = w.shape[1] // 2
    tm, tn = _tile(t, 1024), _tile(hid, 1408)
    nj = hid // tn

    def body(a_ref, bg_ref, bu_ref, act_ref, gt_ref, up_ref):
        av = a_ref[...].astype(_CDT)
        gt = jnp.dot(av, bg_ref[...].astype(_CDT), preferred_element_type=F32)
        up = jnp.dot(av, bu_ref[...].astype(_CDT), preferred_element_type=F32)
        act_ref[...] = _swiglu(gt, up).astype(act_ref.dtype)
        gt_ref[...] = gt.astype(gt_ref.dtype)
        up_ref[...] = up.astype(up_ref.dtype)

    out = pl.BlockSpec((tm, tn), lambda i, j: (i, j))
    return pl.pallas_call(
        body, name="f_ffn_in_swiglu", grid=(t // tm, nj),
        in_specs=[pl.BlockSpec((tm, k), lambda i, j: (i, 0)), pl.BlockSpec((k, tn), lambda i, j: (0, j)),
                  pl.BlockSpec((k, tn), lambda i, j: (0, nj + j))],
        out_specs=[out] * 3, out_shape=[jax.ShapeDtypeStruct((t, hid), _CDT)] * 3,
        compiler_params=_cparams(("parallel", "parallel")))(a, w, w)


def _gated_norm(o, z, w):
    return _rms_norm(o, w) * _silu(z)


def _gated_norm_heads(o, z, w):
    heads = [_gated_norm(o[:, h * LANE:(h + 1) * LANE], z[:, h * LANE:(h + 1) * LANE], w) for h in range(HEADS)]
    return jnp.concatenate(heads, axis=1)


def _mla_pre(ckv, krp, cq, cosb, sinb, qw, kw):
    return _rms_norm(cq, qw), _rms_norm(ckv, kw), _rope(krp, cosb, sinb)


def _merge(gg, y_dn, y_mla):
    return _sigmoid(gg[:, :D_MODEL]) * y_dn + _sigmoid(gg[:, D_MODEL:]) * y_mla


def _ln1(xv, attn_out, g, b):
    return _layer_norm(ALPHA * xv + attn_out, g, b)


def _final(h1, ffn, gate_pre, ple_proj, g, b):
    return _layer_norm(ALPHA * h1 + ffn + _sigmoid(gate_pre) * ple_proj, g, b)


def _swiglu(gt, up):
    return _silu(gt) * up


def _local_step(x, p, cosb, sinb, target, wt, sp, exch):
    t = x.shape[0]
    bf = _CDT
    xb = x.astype(bf)
    g = {}

    qkv_pre = _mm2(xb, wt['qkv'], name="f_qkv")
    z = _mm2(xb, wt['z'], name="f_z")
    gg = _mm2(xb, wt['gg'], name="f_gg")
    pm = _mm2(xb, wt['mla'], name="f_mla")
    qkv_act = _conv_silu(qkv_pre, sp['conv_w'])
    (u, w_, qd, kt, intra, gl, t_inv, qd2, au), sent = _delta_local(qkv_act, pm, sp['a_log'], sp['dt_bias'],
                                                                    rider=exch.gather_send())
    (o_dn, sall), passed = _delta_scan(u, w_, qd2, kt, au, gl, rider=exch.gather_pass(sent))
    wt = dict(wt, **exch.weights(passed))
    def gated_norm_br(h, o, zz, w, wbr):
        og_v = _gated_norm_heads(o, zz, w).astype(bf)
        return og_v, jnp.dot(og_v, wbr.astype(bf), preferred_element_type=F32)

    og, y_dn = _rowwise(gated_norm_br, [(o_dn, D_MODEL, 0, False), (z, D_MODEL, 0, False)],
                        [sp['dn_norm_w'], wt['br_dn']],
                        [(D_MODEL, D_MODEL, False, bf), (D_MODEL, D_MODEL, False, F32)], name="f_gated_norm_br", tm=512)

    def mla_pre_uq(h, ckv, krp, cq, cosv, sinv, qw, kw, wn, wr):
        c_q_v, c_kv_v, k_rope_v = _mla_pre(ckv, krp, cq, cosv, sinv, qw, kw)
        c_q_b = c_q_v.astype(bf)
        return (c_q_b, c_kv_v, k_rope_v, jnp.dot(c_q_b, wn.astype(bf), preferred_element_type=F32),
                jnp.dot(c_q_b, wr.astype(bf), preferred_element_type=F32))

    c_q, c_kv, k_rope, q_nope, q_rope_pre = _rowwise(
        mla_pre_uq,
        [(pm, KV_LORA, 0, False), (pm, ROPE_PAD, 2, False), (pm, Q_LORA, 2, False),
         (cosb, ROPE_PAD, 0, False), (sinb, ROPE_PAD, 0, False)],
        [sp['q_norm_w'], sp['kv_norm_w'], wt['uq_nope'], wt['uq_rope']],
        [(Q_LORA, Q_LORA, False, bf), (KV_LORA, KV_LORA, False, bf), (ROPE_PAD, ROPE_PAD, False, bf),
         (HEADS * NOPE, HEADS * NOPE, False, bf), (HEADS * ROPE_PAD, HEADS * ROPE_PAD, False, F32)], name="f_mla_pre_uq",
        tm=512)
    out_lat, lse, q_rope, o_mla = _attention(q_nope, q_rope_pre, cosb, sinb, c_kv, k_rope, wt['uk'], wt['uv'])
    y_mla = _mm2(o_mla, wt['br_mla'], name="f_br_mla")

    def merge_o_ln1(h, ggv, yd, ym, xv, wo, gv, bv):
        mixed_v = _merge(ggv, yd, ym).astype(bf)
        ao = jnp.dot(mixed_v, wo.astype(bf), preferred_element_type=F32)
        h1v = _ln1(xv, ao, gv, bv)
        return mixed_v, ao, h1v, h1v

    mixed, attn_out, h1, h1b = _rowwise(
        merge_o_ln1, [(gg, 2 * D_MODEL, 0, False), (y_dn, D_MODEL, 0, False), (y_mla, D_MODEL, 0, False), (x, D_MODEL, 0, False)],
        [wt['o'], sp['ln1_g'], sp['ln1_b']],
        [(D_MODEL, D_MODEL, False, bf), (D_MODEL, D_MODEL, False, F32), (D_MODEL, D_MODEL, False, F32),
         (D_MODEL, D_MODEL, False, bf)], name="f_merge_o_ln1", tm=512)
    act, ffn_gt, ffn_up = _ffn_in_swiglu(h1b, wt['ffn_in'])
    pb = p.astype(bf)

    def final_fn(h, h1v, actv, pv, tgt, gt, up, wfo, wpg, wpl, gv, bv):
        ffnv = jnp.dot(actv.astype(bf), wfo.astype(bf), preferred_element_type=F32)
        gpv = jnp.dot(h1v.astype(bf), wpg.astype(bf), preferred_element_type=F32)
        ppv = jnp.dot(pv.astype(bf), wpl.astype(bf), preferred_element_type=F32)
        y, vjp = jax.vjp(_final, h1v, ffnv, gpv, ppv, gv, bv)
        err = y - tgt
        dh1, dffn, dgp, dpp, dg, db = vjp(err * (1.0 / D_MODEL))
        sq = err * err
        lanes = sq[:, :LANE]
        for j in range(1, D_MODEL // LANE):
            lanes = lanes + sq[:, j * LANE:(j + 1) * LANE]
        loss = jnp.sum(lanes, axis=0, keepdims=True) * (0.5 / D_MODEL)
        dffn_b = dffn.astype(bf)
        dact = _dot(dffn_b, wfo, _NT).astype(bf).astype(F32)
        _, vjp_s = jax.vjp(_swiglu, gt.astype(F32), up.astype(F32))
        dgt, dup = vjp_s(dact)
        return dffn, dffn_b, dgp, dpp, jnp.concatenate([dgt, dup], axis=1), dg, db, loss

    dpre2, dpre2b, dgate_pre, dple_proj, dffn_in, g['ln2_g'], g['ln2_b'], loss_lanes = _rowwise(
        final_fn, [(h1, D_MODEL, 0, False), (act, FFN_HIDDEN, 0, False), (pb, PLE_DIM, 0, False), (target, D_MODEL, 0, False),
                   (ffn_gt, FFN_HIDDEN, 0, False), (ffn_up, FFN_HIDDEN, 0, False)],
        [wt['ffn_out'], wt['ple_gate'], wt['ple'], sp['ln2_g'], sp['ln2_b']],
        [(D_MODEL, D_MODEL, False, F32)] + [(D_MODEL, D_MODEL, False, bf)] * 3 + [(2 * FFN_HIDDEN, 2 * FFN_HIDDEN, False, bf)],
        [(1, D_MODEL), (1, D_MODEL), (1, LANE)], name="b_final")
    g['loss_lanes'] = loss_lanes

    g['ple'] = _mm2(pb, dple_proj, ta=True, name="g_ple")
    g['ple_gate'] = _mm2(h1b, dgate_pre, ta=True, name="g_ple_gate")
    g['ffn_out'] = _mm2(act, dpre2b, ta=True, name="g_ffn_out")
    g['ffn_in'] = _mm2(h1b, dffn_in, ta=True, name="g_ffn_in")
    dh1, _ = _dx_fused([(dffn_in, wt['ffn_in']), (dgate_pre, wt['ple_gate'])], dpre2, ALPHA, name="b_dh1")

    def attn_out_bwd(h, xv, ao, d, ggv, yd, ym, o, zz, wo, wbd, wbm, gv, bv, nw):
        _, vjp = jax.vjp(_ln1, xv, ao, gv, bv)
        _, dao, dg, db = vjp(d)
        dao_b = dao.astype(bf)
        _, vjp_m = jax.vjp(_merge, ggv, yd, ym)
        dggv, dyd, dym = vjp_m(_dot(dao_b, wo, _NT))
        dyd_b, dym_b = dyd.astype(bf), dym.astype(bf)
        _, vjp_n = jax.vjp(_gated_norm_heads, o, zz, nw)
        do_v, dz_v, dnw = vjp_n(_dot(dyd_b, wbd, _NT))
        return dao, dao_b, dggv, dyd_b, dym_b, do_v, dz_v, _dot(dym_b, wbm, _NT), dg, db, dnw

    row_d = lambda a: (a, D_MODEL, 0, False)
    dpre1, dpre1b, dgg, dy_dn, dy_mla, do_dn, dz, do_mla, g['ln1_g'], g['ln1_b'], g['dn_norm_w'] = _rowwise(
        attn_out_bwd, [row_d(x), row_d(attn_out), row_d(dh1), (gg, 2 * D_MODEL, 0, False), row_d(y_dn), row_d(y_mla),
                       row_d(o_dn), row_d(z)],
        [wt['o'], wt['br_dn'], wt['br_mla'], sp['ln1_g'], sp['ln1_b'], sp['dn_norm_w']],
        [(D_MODEL, D_MODEL, False, F32), (D_MODEL, D_MODEL, False, bf), (2 * D_MODEL, 2 * D_MODEL, False, bf),
         (D_MODEL, D_MODEL, False, bf), (D_MODEL, D_MODEL, False, bf), (D_MODEL, D_MODEL, False, F32),
         (D_MODEL, D_MODEL, False, bf), (D_MODEL, D_MODEL, False, bf)],
        [(1, D_MODEL), (1, D_MODEL), (1, LANE)], name="b_attn_out")
    g['o'] = _mm2(mixed, dpre1b, ta=True, name="g_o")
    g['br_dn'] = _mm2(og, dy_dn, ta=True, name="g_br_dn")
    g['br_mla'] = _mm2(o_mla, dy_mla, ta=True, name="g_br_mla")

    g['uv'] = _mm(out_lat, do_mla, name="g_uv", ta=True, heads=HEADS, a_head='lead', b_head='col', out_head='lead',
                  dims=(KV_LORA, NOPE, t))
    dq_lat, dq_rope_pre, dq_nope, dckv_att, dkr_att = _attention_bwd(
        q_nope, q_rope, cosb, sinb, c_kv, k_rope, wt['uk'], wt['uv'], out_lat, lse, do_mla)
    g['uk'] = _mm(dq_lat, q_nope, name="g_uk", ta=True, heads=HEADS, a_head='lead', b_head='col', out_head='lead',
                  dims=(KV_LORA, NOPE, t))
    g['uq_nope'] = _mm2(c_q, dq_nope, ta=True, name="g_uq_nope")
    g['uq_rope'] = _mm2(c_q, dq_rope_pre, ta=True, name="g_uq_rope")
    dc_q = _mm2(dq_nope, wt['uq_nope'], tb=True, name="b_dcq_nope")
    dc_q = _mm2(dq_rope_pre, wt['uq_rope'], tb=True, name="b_dcq_rope", add=dc_q)

    (du, dw, dqd, dkt, dintra, dgl), paired = _delta_scan_bwd(u, w_, qd, kt, intra, gl, sall, do_dn, rider=exch.pair_send(g))
    (dq_a, dk_a, dv_a, dba, g['a_log'], g['dt_bias']), arrived = _delta_local_bwd(
        qkv_act, pm, sp['a_log'], sp['dt_bias'], t_inv, du, dw, dqd, dkt, dintra, dgl, rider=exch.reduce_send(paired))
    exch.reduce_arrived(arrived)
    dqkv_pre, g['conv_w'] = _conv_silu_bwd(qkv_pre, sp['conv_w'], [dq_a, dk_a, dv_a])

    def mla_pre_bwd(h, ckv, cq, cosv, sinv, dcq, dckv, dkr, dba_v, qw, kw):
        _, vjp = jax.vjp(lambda a, c, d, e: (_rms_norm(c, d), _rms_norm(a, e)), ckv, cq, qw, kw)
        dckv_p, dcq_p, dqw, dkw = vjp((dcq, dckv))
        dkr_p = _rope_bwd(dkr, cosv, sinv)
        dpm = jnp.concatenate([dckv_p, dkr_p, dba_v, jnp.zeros((ckv.shape[0], 2 * LANE), F32), dcq_p], axis=1)
        return dpm, dqw, dkw

    dpm, g['q_norm_w'], g['kv_norm_w'] = _rowwise(
        mla_pre_bwd,
        [(pm, KV_LORA, 0, False), (pm, Q_LORA, 2, False), (cosb, ROPE_PAD, 0, False), (sinb, ROPE_PAD, 0, False),
         (dc_q, Q_LORA, 0, False), (dckv_att, KV_LORA, 0, False), (dkr_att, ROPE_PAD, 0, False), (dba, LANE, 0, False)],
        [sp['q_norm_w'], sp['kv_norm_w']], [(1152, 1152, False, bf)], [(1, Q_LORA), (1, KV_LORA)], name="b_mla_pre")

    rider = exch.small_send(g)
    if rider is None:
        g['qkv'] = _mm2(xb, dqkv_pre, ta=True, name="g_qkv")
    else:
        g['qkv'], got = _mm2(xb, dqkv_pre, ta=True, name="g_qkv", rider=rider)
        exch.small_arrived(got)
    g['z'] = _mm2(xb, dz, ta=True, name="g_z")
    g['gg'] = _mm2(xb, dgg, ta=True, name="g_gg")
    g['mla'] = _mm2(xb, dpm, ta=True, name="g_mla")
    dx, arrived = _dx_fused([(dqkv_pre, wt['qkv']), (dz, wt['z']), (dgg, wt['gg']), (dpm, wt['mla'])], dpre1, ALPHA,
                            rider=exch.in_send(g))
    exch.in_arrived(arrived)
    return loss_lanes, dx, g


_IN_SIZES = (QKV_W, HEADS * DN_DK, HEADS, HEADS, Q_LORA, KV_LORA, ROPE, D_MODEL, D_MODEL)


def _rope_tables(positions):
    inv_freq = ROPE_BASE ** (-jnp.arange(0, ROPE, 2, dtype=F32) / ROPE)
    ang = positions.astype(F32)[:, None] * inv_freq
    cos, sin = jnp.cos(ang), jnp.sin(ang)
    zeros = jnp.zeros((positions.shape[0], ROPE_PAD - ROPE), F32)
    return jnp.concatenate([cos, cos, zeros], axis=1), jnp.concatenate([-sin, sin, zeros], axis=1)


def _prep_w_in(w_in):
    dt = w_in.dtype
    offs = [0]
    for s in _IN_SIZES:
        offs.append(offs[-1] + s)
    qkv, z, wb, wa, cq, ckv, kr, gd, gm = [w_in[:, offs[i]:offs[i + 1]] for i in range(len(_IN_SIZES))]
    zc = lambda n: jnp.zeros((D_MODEL, n), dt)
    return {
        'qkv': qkv, 'z': z, 'gg': jnp.concatenate([gd, gm], axis=1),
        'mla': jnp.concatenate([ckv, kr, zc(ROPE_PAD - ROPE), wb, wa, zc(LANE - 2 * HEADS), zc(2 * LANE), cq], axis=1),
    }


def _prep_weights(full):
    w_uq = full['w_uq']
    wt = {
        'uq_nope': w_uq[:, :, :NOPE].reshape(Q_LORA, HEADS * NOPE),
        'uq_rope': jnp.pad(w_uq[:, :, NOPE:], ((0, 0), (0, 0), (0, ROPE_PAD - ROPE))).reshape(Q_LORA, HEADS * ROPE_PAD),
        'uk': jnp.transpose(full['w_uk'], (1, 0, 2)), 'uv': jnp.transpose(full['w_uv'], (1, 0, 2)),
        'br_dn': full['w_br_dn'], 'br_mla': full['w_br_mla'], 'o': full['w_o'], 'ffn_in': full['w_ffn_in'],
        'ffn_out': full['w_ffn_out'], 'ple': full['w_ple'], 'ple_gate': full['w_ple_gate'],
    }
    return wt


def _prep_small(small):
    pad = lambda v: jnp.pad(v, (0, LANE - v.shape[0]))[None, :]
    return {
        'conv_w': small['conv_w'], 'a_log': pad(small['dn_a_log']), 'dt_bias': pad(small['dn_dt_bias']),
        'dn_norm_w': small['dn_norm_w'][None, :], 'q_norm_w': small['q_norm_w'][None, :],
        'kv_norm_w': small['kv_norm_w'][None, :], 'ln1_g': small['ln1_g'][None, :], 'ln1_b': small['ln1_b'][None, :],
        'ln2_g': small['ln2_g'][None, :], 'ln2_b': small['ln2_b'][None, :],
    }


def _w_in_grad(g):
    mla = g['mla']
    ba0 = KV_LORA + ROPE_PAD
    cq0 = ba0 + 3 * LANE
    return jnp.concatenate([
        g['qkv'], g['z'], mla[:, ba0:ba0 + HEADS], mla[:, ba0 + HEADS:ba0 + 2 * HEADS], mla[:, cq0:cq0 + Q_LORA],
        mla[:, :KV_LORA], mla[:, KV_LORA:KV_LORA + ROPE], g['gg']], axis=1)


def _unprep_grads_late(g):
    return {
        'conv_w': g['conv_w'], 'dn_a_log': g['a_log'][0, :HEADS], 'dn_dt_bias': g['dt_bias'][0, :HEADS],
        'dn_norm_w': g['dn_norm_w'][0], 'q_norm_w': g['q_norm_w'][0], 'kv_norm_w': g['kv_norm_w'][0],
        'ln1_g': g['ln1_g'][0], 'ln1_b': g['ln1_b'][0], 'ln2_g': g['ln2_g'][0], 'ln2_b': g['ln2_b'][0],
    }


def _unprep_grads_early(g):
    w_uq = jnp.concatenate([g['uq_nope'].reshape(Q_LORA, HEADS, NOPE),
                            g['uq_rope'].reshape(Q_LORA, HEADS, ROPE_PAD)[:, :, :ROPE]], axis=2)
    return {
        'w_uq': w_uq, 'w_uk': jnp.transpose(g['uk'], (1, 0, 2)), 'w_uv': jnp.transpose(g['uv'], (1, 0, 2)),
        'w_br_dn': g['br_dn'], 'w_br_mla': g['br_mla'], 'w_o': g['o'],
        'w_ffn_in': g['ffn_in'], 'w_ffn_out': g['ffn_out'], 'w_ple': g['ple'], 'w_ple_gate': g['ple_gate'],
    }


_FLATB_PIECES = (
    ('w_ffn_out', 704, (704, D_MODEL)), ('w_br_dn', 256, (256, D_MODEL)), ('w_br_mla', 256, (256, D_MODEL)),
    ('w_o', 256, (256, D_MODEL)), ('w_ple_gate', 256, (256, D_MODEL)), ('w_uq', 144, (96, HEADS, NOPE + ROPE)),
    ('w_uk', 64, (64, HEADS, NOPE)), ('w_uv', 64, (64, HEADS, NOPE)), ('w_ple', 64, (PLE_DIM, 256)),
)
FLATB_ROWS = 2112
W_IN_SHARD = D_IN // N_SHARD
FFN_IN_SHARD = 2 * FFN_HIDDEN // N_SHARD
A_ROWS = D_MODEL + 32
_CONV_SHARD = QKV_W // N_SHARD
_ADD_TILES = (256, 256, 352)


def _flatb_offsets():
    offs, o = {}, 0
    for name, rows, _ in _FLATB_PIECES:
        offs[name] = o
        o += rows
    return offs, o


def _pack_shards(ws, conv_w):
    conv_bits = lax.bitcast_convert_type(conv_w, jnp.bfloat16).reshape(DN_CONV, 2 * _CONV_SHARD).astype(_CDT)
    tail = jnp.pad(conv_bits, ((0, A_ROWS - D_MODEL - DN_CONV), (0, W_IN_SHARD - 2 * _CONV_SHARD)))
    a_buf = jnp.concatenate([ws['w_in'].astype(_CDT), tail], axis=0)
    parts = [ws[name].astype(_CDT).reshape(rows, FLAT_W) for name, rows, _ in _FLATB_PIECES]
    used = sum(p.shape[0] for p in parts)
    parts.append(jnp.zeros((FLATB_ROWS - used, FLAT_W), _CDT))
    return [a_buf, ws['w_ffn_in'].astype(_CDT), jnp.concatenate(parts, axis=0)]


def _unpack_w_in(gathered, local, me):
    a = [jnp.where(me == s, local, gathered[s]) for s in range(N_SHARD)]
    conv = [lax.bitcast_convert_type(
        p[D_MODEL:D_MODEL + DN_CONV, :2 * _CONV_SHARD].astype(jnp.bfloat16).reshape(DN_CONV, _CONV_SHARD, 2), F32) for p in a]
    return jnp.concatenate([p[:D_MODEL] for p in a], axis=1), jnp.concatenate(conv, axis=1)


def _unpack_rest(gathered, local, me):
    pick = lambda b, s: jnp.where(me == s, local[b], gathered[b][s])
    full = {'w_ffn_in': jnp.concatenate([pick(0, s) for s in range(N_SHARD)], axis=1)}
    offs, _ = _flatb_offsets()
    fb = [pick(1, s) for s in range(N_SHARD)]
    for name, rows, shape in _FLATB_PIECES:
        pieces = [p[offs[name]:offs[name] + rows].reshape(shape) for p in fb]
        full[name] = jnp.concatenate(pieces, axis=1 if name == 'w_ple' else 0)
    return full


def _shard_columns(g, w):
    return jnp.stack([g[:, s * w:(s + 1) * w] for s in range(N_SHARD)])


def _pack_grads_rest(gw):
    parts = []
    for name, rows, _ in _FLATB_PIECES:
        g = gw[name]
        if name == 'w_ple':
            parts.append(_shard_columns(g, PLE_DIM).reshape(N_SHARD, rows, FLAT_W))
        else:
            parts.append(g.reshape(N_SHARD, rows, FLAT_W))
    used = sum(p.shape[1] for p in parts)
    parts.append(jnp.zeros((N_SHARD, FLATB_ROWS - used, FLAT_W), F32))
    return [_shard_columns(gw['w_ffn_in'], FFN_IN_SHARD), jnp.concatenate(parts, axis=1)]


def _unpack_reduced(mine, theirs, c):
    whole = [jnp.concatenate([jnp.where(c == 0, m, t), jnp.where(c == 0, t, m)], axis=0) for m, t in zip(mine, theirs)]
    out = {'w_in': whole[0], 'w_ffn_in': whole[1]}
    offs, _ = _flatb_offsets()
    for name, rows, shape in _FLATB_PIECES:
        out[name] = whole[2][offs[name]:offs[name] + rows].reshape(shape)
    return out


_HBM = pl.BlockSpec(memory_space=pltpu.HBM)


def _place():
    x, y, c = lax.axis_index("x"), lax.axis_index("y"), lax.axis_index("c")
    chips = [(1 - x, y), (x, 1 - y), (1 - x, 1 - y)]
    return x, y, c, chips


def _remote(src, dst, send_sems, recv_sems, k, to):
    return pltpu.make_async_remote_copy(src_ref=src, dst_ref=dst, send_sem=send_sems.at[k], recv_sem=recv_sems.at[k],
                                        device_id=to, device_id_type=_MESH)


def _half_rows(ref, half, hf, lead=None):
    rows = pl.ds(pl.multiple_of(hf * half, 16), half)
    return ref.at[rows, :] if lead is None else ref.at[lead, rows, :]


class _Rider:
    def __init__(self, inputs, out_shape, n_sems, copies, aliases=None):
        self.inputs, self.out_shape, self.n_sems, self.copies = list(inputs), list(out_shape), n_sems, copies
        self.aliases = aliases or {}


def _carried_call(body, rider, first, last, *, name, grid, in_specs, out_specs, out_shape, scratch_shapes, sem, args):
    n_in, n_out, n_scr = len(in_specs), len(out_specs), len(scratch_shapes)
    if rider is None:
        res = pl.pallas_call(body, name=name, grid=grid, in_specs=in_specs, out_specs=out_specs, out_shape=out_shape,
                             scratch_shapes=scratch_shapes, compiler_params=_cparams(sem))(*args)
        return list(res), []
    ri, ro = len(rider.inputs), len(rider.out_shape)

    def full_body(*refs):
        own_in, r_in = refs[:n_in], refs[n_in:n_in + ri]
        o0 = n_in + ri
        own_out, r_out = refs[o0:o0 + n_out], refs[o0 + n_out:o0 + n_out + ro]
        s0 = o0 + n_out + ro
        own_scr, send_sems, recv_sems = refs[s0:s0 + n_scr], refs[s0 + n_scr], refs[s0 + n_scr + 1]

        @pl.when(first())
        def _():
            sends, _ = rider.copies(r_in, r_out, send_sems, recv_sems)
            for cp in sends:
                cp.start()

        body(*own_in, *own_out, *own_scr)

        @pl.when(last())
        def _():
            sends, arrivals = rider.copies(r_in, r_out, send_sems, recv_sems)
            for cp in arrivals():
                cp.wait_recv()
            for cp in sends:
                cp.wait_send()

    res = pl.pallas_call(
        full_body, name=name, grid=grid, in_specs=list(in_specs) + [_HBM] * ri, out_specs=list(out_specs) + [_HBM] * ro,
        out_shape=list(out_shape) + rider.out_shape,
        scratch_shapes=list(scratch_shapes) + [pltpu.SemaphoreType.DMA((rider.n_sems,))] * 2,
        input_output_aliases={n_in + i: n_out + o for i, o in rider.aliases.items()},
        compiler_params=_cparams(sem))(*args, *rider.inputs)
    return list(res[:n_out]), list(res[n_out:])


def _ride_gather_send(bufs):
    n = len(bufs)
    halves = [b.shape[0] // 2 for b in bufs]

    def copies(ins, outs, send_sems, recv_sems):
        x, y, c, chips = _place()
        slot = lambda b, cx, cy: _half_rows(outs[b], halves[b], c, lead=2 * cx + cy)
        sends = [_remote(_half_rows(ins[b], halves[b], c), slot(b, x, y), send_sems, recv_sems, 3 * b + j, (cx, cy, c))
                 for b in range(n) for j, (cx, cy) in enumerate(chips)]
        arrivals = lambda: [_remote(slot(b, cx, cy), slot(b, cx, cy), send_sems, recv_sems, 3 * b + j, (x, y, c))
                            for b in range(n) for j, (cx, cy) in enumerate(chips)]
        return sends, arrivals

    return _Rider(bufs, [jax.ShapeDtypeStruct((N_SHARD,) + b.shape, b.dtype) for b in bufs], 3 * n, copies)


def _ride_gather_pass(gathered):
    n = len(gathered)
    halves = [g.shape[1] // 2 for g in gathered]

    def copies(ins, outs, send_sems, recv_sems):
        x, y, c, chips = _place()
        slot = lambda b, cx, cy, hf: _half_rows(outs[b], halves[b], hf, lead=2 * cx + cy)
        sends = [_remote(slot(b, cx, cy, c), slot(b, cx, cy, c), send_sems, recv_sems, 3 * b + j, (x, y, 1 - c))
                 for b in range(n) for j, (cx, cy) in enumerate(chips)]
        arrivals = lambda: [_remote(slot(b, cx, cy, 1 - c), slot(b, cx, cy, 1 - c), send_sems, recv_sems, 3 * b + j, (x, y, c))
                            for b in range(n) for j, (cx, cy) in enumerate(chips)]
        return sends, arrivals

    return _Rider(gathered, [jax.ShapeDtypeStruct(g.shape, g.dtype) for g in gathered], 3 * n, copies,
                  aliases={b: b for b in range(n)})


def _ride_small_gather(buf):
    def copies(ins, outs, send_sems, recv_sems):
        x, y, c, _ = _place()
        flip = lambda v, d: 1 - v if d else v
        sends, peers = [], []
        for dx in (0, 1):
            for dy in (0, 1):
                for dc in (0, 1):
                    if dx or dy or dc:
                        k = 4 * dx + 2 * dy + dc - 1
                        px, py, pc = flip(x, dx), flip(y, dy), flip(c, dc)
                        sends.append(_remote(ins[0], outs[0].at[4 * x + 2 * y + c], send_sems, recv_sems, k, (px, py, pc)))
                        peers.append((k, 4 * px + 2 * py + pc))
        arrivals = lambda: [_remote(ins[0], outs[0].at[slot], send_sems, recv_sems, k, (x, y, c)) for k, slot in peers]
        return sends, arrivals

    return _Rider([buf], [jax.ShapeDtypeStruct((8,) + buf.shape, buf.dtype)], 7, copies)


def _small_sum(gathered, buf, me_arr):
    n, r, width = gathered.shape

    def body(me_ref, g_ref, b_ref, o_ref):
        total = jnp.zeros((r, width), F32)
        for d in range(n):
            total = total + jnp.where(me_ref[0] == d, b_ref[...], g_ref[d])
        o_ref[...] = total

    return pl.pallas_call(
        body, name="small_sum", out_shape=jax.ShapeDtypeStruct((r, width), F32),
        grid_spec=pltpu.PrefetchScalarGridSpec(
            num_scalar_prefetch=1, grid=(1,),
            in_specs=[pl.BlockSpec((n, r, width), lambda i, me: (0, 0, 0)), pl.BlockSpec((r, width), lambda i, me: (0, 0))],
            out_specs=pl.BlockSpec((r, width), lambda i, me: (0, 0))),
        compiler_params=_cparams(("arbitrary",)))(me_arr, gathered, buf)


def _ride_pair_exchange(gbufs):
    n = len(gbufs)
    halves = [g.shape[1] // 2 for g in gbufs]

    def copies(ins, outs, send_sems, recv_sems):
        x, y, c, _ = _place()
        sends = [_remote(ins[b].at[:, pl.ds(pl.multiple_of((1 - c) * halves[b], 16), halves[b]), :], outs[b],
                         send_sems, recv_sems, b, (x, y, 1 - c)) for b in range(n)]
        arrivals = lambda: [_remote(outs[b], outs[b], send_sems, recv_sems, b, (x, y, c)) for b in range(n)]
        return sends, arrivals

    return _Rider(gbufs, [jax.ShapeDtypeStruct((N_SHARD, h, g.shape[2]), g.dtype) for g, h in zip(gbufs, halves)], n, copies)


def _ride_chip_exchange(parts):
    n = len(parts)

    def copies(ins, outs, send_sems, recv_sems):
        x, y, c, chips = _place()
        sends = [_remote(ins[b].at[2 * cx + cy], outs[b].at[j], send_sems, recv_sems, 3 * b + j, (cx, cy, c))
                 for b in range(n) for j, (cx, cy) in enumerate(chips)]
        arrivals = lambda: [_remote(ins[b].at[0], outs[b].at[j], send_sems, recv_sems, 3 * b + j, (x, y, c))
                            for b in range(n) for j in range(len(chips))]
        return sends, arrivals

    return _Rider(parts, [jax.ShapeDtypeStruct((3,) + p.shape[1:], p.dtype) for p in parts], 3 * n, copies)


def _gather_shards(bufs, name):
    n = len(bufs)
    halves = [b.shape[0] // 2 for b in bufs]

    def body(*refs):
        ins, outs, send_sems, recv_sems = refs[:n], refs[n:2 * n], refs[2 * n], refs[2 * n + 1]
        x, y, c, chips = _place()
        me, sibling = (x, y, c), (x, y, 1 - c)
        slot = lambda b, cx, cy, hf: _half_rows(outs[b], halves[b], hf, lead=2 * cx + cy)
        first = [_remote(_half_rows(ins[b], halves[b], c), slot(b, x, y, c), send_sems, recv_sems, 6 * b + j, (cx, cy, c))
                 for b in range(n) for j, (cx, cy) in enumerate(chips)]
        for cp in first:
            cp.start()
        passed = []
        for j, (cx, cy) in enumerate(chips):
            for b in range(n):
                _remote(slot(b, cx, cy, c), slot(b, cx, cy, c), send_sems, recv_sems, 6 * b + j, me).wait_recv()
                fwd = _remote(slot(b, cx, cy, c), slot(b, cx, cy, c), send_sems, recv_sems, 6 * b + 3 + j, sibling)
                fwd.start()
                passed.append(fwd)
        for j, (cx, cy) in enumerate(chips):
            for b in range(n):
                _remote(slot(b, cx, cy, 1 - c), slot(b, cx, cy, 1 - c), send_sems, recv_sems, 6 * b + 3 + j, me).wait_recv()
        for cp in first + passed:
            cp.wait_send()

    return pl.pallas_call(
        body, name=name, out_shape=[jax.ShapeDtypeStruct((N_SHARD,) + b.shape, b.dtype) for b in bufs],
        in_specs=[_HBM] * n, out_specs=[_HBM] * n,
        scratch_shapes=[pltpu.SemaphoreType.DMA((6 * n,)), pltpu.SemaphoreType.DMA((6 * n,))],
    )(*bufs)


def _reduce_pair_exchange(gbufs, name):
    n = len(gbufs)
    halves = [g.shape[1] // 2 for g in gbufs]

    def body(*refs):
        ins, outs, send_sems, recv_sems = refs[:n], refs[n:2 * n], refs[2 * n], refs[2 * n + 1]
        x, y, c, _ = _place()
        cps = [_remote(ins[b].at[:, pl.ds(pl.multiple_of((1 - c) * halves[b], 16), halves[b]), :], outs[b],
                       send_sems, recv_sems, b, (x, y, 1 - c)) for b in range(n)]
        for cp in cps:
            cp.start()
        for cp in cps:
            cp.wait()

    return pl.pallas_call(
        body, name=name,
        out_shape=[jax.ShapeDtypeStruct((N_SHARD, h, g.shape[2]), g.dtype) for g, h in zip(gbufs, halves)],
        in_specs=[_HBM] * n, out_specs=[_HBM] * n,
        scratch_shapes=[pltpu.SemaphoreType.DMA((n,)), pltpu.SemaphoreType.DMA((n,))],
    )(*gbufs)


def _pair_add(gbuf, recv, c_arr, tr, name):
    _, rows, width = gbuf.shape
    half = rows // 2
    nt = half // tr

    def body(c_ref, a_ref, b_ref, o_ref):
        o_ref[...] = (a_ref[...] + b_ref[...]).astype(o_ref.dtype)

    blk = lambda f: pl.BlockSpec((None, tr, width), f)
    return pl.pallas_call(
        body, name=name, out_shape=jax.ShapeDtypeStruct((N_SHARD, half, width), jnp.bfloat16),
        grid_spec=pltpu.PrefetchScalarGridSpec(
            num_scalar_prefetch=1, grid=(N_SHARD, nt),
            in_specs=[blk(lambda s, i, c: (s, c[0] * nt + i, 0)), blk(lambda s, i, c: (s, i, 0))],
            out_specs=blk(lambda s, i, c: (s, i, 0))),
        compiler_params=_cparams(("parallel", "parallel")))(c_arr, gbuf, recv)


def _chip_add(part, recv, me_arr, tr, name):
    _, half, width = part.shape

    def body(me_ref, own, a0, a1, a2, o_ref):
        f = lambda r: r[...].astype(F32)
        o_ref[...] = ((f(own) + f(a0)) + f(a1)) + f(a2)

    specs = [pl.BlockSpec((None, tr, width), lambda i, me: (me[0], i, 0))]
    specs += [pl.BlockSpec((None, tr, width), functools.partial(lambda i, me, k: (k, i, 0), k=k)) for k in range(3)]
    return pl.pallas_call(
        body, name=name, out_shape=jax.ShapeDtypeStruct((half, width), F32),
        grid_spec=pltpu.PrefetchScalarGridSpec(
            num_scalar_prefetch=1, grid=(half // tr,), in_specs=specs,
            out_specs=pl.BlockSpec((tr, width), lambda i, me: (i, 0))),
        compiler_params=_cparams(("parallel",)))(me_arr, part, recv, recv, recv)


def _reduce_pair_share(rhalves, name):
    n = len(rhalves)

    def body(*refs):
        ins, outs, send_sems, recv_sems = refs[:n], refs[n:2 * n], refs[2 * n], refs[2 * n + 1]
        x, y, c, _ = _place()
        cps = [_remote(ins[b], outs[b], send_sems, recv_sems, b, (x, y, 1 - c)) for b in range(n)]
        for cp in cps:
            cp.start()
        for cp in cps:
            cp.wait()

    return pl.pallas_call(
        body, name=name, out_shape=[jax.ShapeDtypeStruct(r.shape, r.dtype) for r in rhalves],
        in_specs=[_HBM] * n, out_specs=[_HBM] * n,
        scratch_shapes=[pltpu.SemaphoreType.DMA((n,)), pltpu.SemaphoreType.DMA((n,))],
    )(*rhalves)


def _row_tile(rows, cap):
    if rows <= cap:
        return rows
    t = (cap // 8) * 8
    while t >= 8:
        if rows % t == 0:
            return t
        t -= 8
    return rows


def _adamw(w, g, m, v, name):
    shape = w.shape
    cols = shape[-1] if len(shape) <= 3 else shape[-2] * shape[-1]
    lead = len(shape) == 3
    w2, g2, m2, v2 = (a if lead else a.reshape(-1, cols) for a in (w, g, m, v))
    rows = shape[1] if lead else w2.shape[0]
    tr, tc = _row_tile(rows, 256), cols
    if tr == rows and rows > 256:
        tc = _tile(cols, 256)

    def body(w_ref, g_ref, m_ref, v_ref, d_ref, mo_ref, vo_ref):
        gv = g_ref[...]
        mn = ADAM_B1 * m_ref[...] + (1.0 - ADAM_B1) * gv
        vn = ADAM_B2 * v_ref[...] + (1.0 - ADAM_B2) * (gv * gv)
        m_hat = mn / (1.0 - ADAM_B1 ** ADAM_STEP)
        v_hat = vn / (1.0 - ADAM_B2 ** ADAM_STEP)
        d_ref[...] = -ADAM_LR * (m_hat / (jnp.sqrt(v_hat) + ADAM_EPS) + ADAM_WD * w_ref[...])
        mo_ref[...] = mn
        vo_ref[...] = vn

    blk = (pl.BlockSpec((None, tr, tc), lambda i, j: (0, i, j)) if lead else pl.BlockSpec((tr, tc), lambda i, j: (i, j)))
    outs = pl.pallas_call(
        body, name=name, grid=(rows // tr, cols // tc), in_specs=[blk] * 4, out_specs=[blk] * 3,
        out_shape=[jax.ShapeDtypeStruct(w2.shape, F32)] * 3,
        compiler_params=_cparams(("parallel", "parallel")))(w2, g2, m2, v2)
    return tuple(o.reshape(shape) for o in outs)


_WEIGHT_NAMES = ('w_in', 'conv_w', 'dn_a_log', 'dn_dt_bias', 'dn_norm_w', 'q_norm_w', 'w_uq', 'kv_norm_w', 'w_uk',
                 'w_uv', 'w_br_dn', 'w_br_mla', 'w_o', 'ln1_g', 'ln1_b', 'w_ffn_in', 'w_ffn_out', 'w_ple',
                 'w_ple_gate', 'ln2_g', 'ln2_b')
_SMALL_NAMES = ('ln1_g', 'ln1_b', 'ln2_g', 'ln2_b', 'q_norm_w', 'kv_norm_w', 'dn_norm_w', 'dn_a_log', 'dn_dt_bias')
_SMALL_GROUP = 8
_CONV_SMALL_ROW = len(_SMALL_NAMES) * _SMALL_GROUP
_CONV_SMALL_ROWS = DN_CONV * QKV_W // FLAT_W


_LOSS_SMALL_ROW = _CONV_SMALL_ROW + 16


def _pack_small(gw, loss_lanes):
    rows = [jnp.pad(gw[n][None, :], ((0, _SMALL_GROUP - 1), (0, FLAT_W - gw[n].shape[0]))) for n in _SMALL_NAMES]
    rows.append(jnp.pad(gw['conv_w'].reshape(_CONV_SMALL_ROWS, FLAT_W), ((0, 16 - _CONV_SMALL_ROWS), (0, 0))))
    rows.append(jnp.pad(loss_lanes, ((0, _SMALL_GROUP - 1), (0, FLAT_W - LANE))))
    return jnp.concatenate(rows, axis=0)


class _Exchange:
    def __init__(self, local, me_chip, c_arr):
        self.local, self.me_chip, self.c_arr = local, me_chip, c_arr
        self.parts = self.arrived = None

    def gather_send(self):
        return _ride_gather_send(self.local)

    def gather_pass(self, sent):
        return _ride_gather_pass(sent)

    def weights(self, gathered):
        return _prep_weights(_unpack_rest(gathered, self.local, self.me_chip))

    def pair_send(self, g):
        self.gbufs = _pack_grads_rest(_unprep_grads_early(g))
        return _ride_pair_exchange(self.gbufs)

    def reduce_send(self, got):
        self.parts = [_pair_add(g_, r_, self.c_arr, tr, "pair_add_%d" % (i + 1))
                      for i, (g_, r_, tr) in enumerate(zip(self.gbufs, got, _ADD_TILES[1:]))]
        return _ride_chip_exchange(self.parts)

    def reduce_arrived(self, arrived):
        self.arrived = list(arrived)

    def in_send(self, g):
        g_in = [_shard_columns(_w_in_grad(g), W_IN_SHARD)]
        got = _reduce_pair_exchange(g_in, "reduce_pair_exchange_w_in")
        self.part_in = _pair_add(g_in[0], got[0], self.c_arr, _ADD_TILES[0], "pair_add_0")
        return _ride_chip_exchange([self.part_in])

    def in_arrived(self, arrived):
        self.arrived_in = list(arrived)

    def small_send(self, g):
        self.small = _pack_small(_unprep_grads_late(g), g['loss_lanes'])
        return _ride_small_gather(self.small)

    def small_arrived(self, arrived):
        self.small_gathered = arrived[0]


def kernel(x, p, positions, w_in, conv_w, dn_a_log, dn_dt_bias, dn_norm_w, q_norm_w, w_uq, kv_norm_w, w_uk, w_uv, w_br_dn, w_br_mla, w_o, ln1_g, ln1_b, w_ffn_in, w_ffn_out, w_ple, w_ple_gate, ln2_g, ln2_b, loss_target, m_w_in, m_conv_w, m_dn_a_log, m_dn_dt_bias, m_dn_norm_w, m_q_norm_w, m_w_uq, m_kv_norm_w, m_w_uk, m_w_uv, m_w_br_dn, m_w_br_mla, m_w_o, m_ln1_g, m_ln1_b, m_w_ffn_in, m_w_ffn_out, m_w_ple, m_w_ple_gate, m_ln2_g, m_ln2_b, v_w_in, v_conv_w, v_dn_a_log, v_dn_dt_bias, v_dn_norm_w, v_q_norm_w, v_w_uq, v_kv_norm_w, v_w_uk, v_w_uv, v_w_br_dn, v_w_br_mla, v_w_o, v_ln1_g, v_ln1_b, v_w_ffn_in, v_w_ffn_out, v_w_ple, v_w_ple_gate, v_ln2_g, v_ln2_b):
    ws = dict(w_in=w_in, conv_w=conv_w, dn_a_log=dn_a_log, dn_dt_bias=dn_dt_bias, dn_norm_w=dn_norm_w, q_norm_w=q_norm_w,
              w_uq=w_uq, kv_norm_w=kv_norm_w, w_uk=w_uk, w_uv=w_uv, w_br_dn=w_br_dn, w_br_mla=w_br_mla, w_o=w_o,
              ln1_g=ln1_g, ln1_b=ln1_b, w_ffn_in=w_ffn_in, w_ffn_out=w_ffn_out, w_ple=w_ple, w_ple_gate=w_ple_gate,
              ln2_g=ln2_g, ln2_b=ln2_b)
    ms = dict(w_in=m_w_in, conv_w=m_conv_w, dn_a_log=m_dn_a_log, dn_dt_bias=m_dn_dt_bias, dn_norm_w=m_dn_norm_w,
              q_norm_w=m_q_norm_w, w_uq=m_w_uq, kv_norm_w=m_kv_norm_w, w_uk=m_w_uk, w_uv=m_w_uv, w_br_dn=m_w_br_dn,
              w_br_mla=m_w_br_mla, w_o=m_w_o, ln1_g=m_ln1_g, ln1_b=m_ln1_b, w_ffn_in=m_w_ffn_in, w_ffn_out=m_w_ffn_out,
              w_ple=m_w_ple, w_ple_gate=m_w_ple_gate, ln2_g=m_ln2_g, ln2_b=m_ln2_b)
    vs = dict(w_in=v_w_in, conv_w=v_conv_w, dn_a_log=v_dn_a_log, dn_dt_bias=v_dn_dt_bias, dn_norm_w=v_dn_norm_w,
              q_norm_w=v_q_norm_w, w_uq=v_w_uq, kv_norm_w=v_kv_norm_w, w_uk=v_w_uk, w_uv=v_w_uv, w_br_dn=v_w_br_dn,
              w_br_mla=v_w_br_mla, w_o=v_w_o, ln1_g=v_ln1_g, ln1_b=v_ln1_b, w_ffn_in=v_w_ffn_in, w_ffn_out=v_w_ffn_out,
              w_ple=v_w_ple, w_ple_gate=v_w_ple_gate, ln2_g=v_ln2_g, ln2_b=v_ln2_b)
    mx, my, mc = lax.axis_index("x"), lax.axis_index("y"), lax.axis_index("c")

    me_chip = 2 * mx + my
    c_arr = jnp.reshape(mc, (1,)).astype(jnp.int32)
    me_arr = jnp.reshape(me_chip, (1,)).astype(jnp.int32)
    sharded = ('w_in', 'w_ffn_in') + tuple(name for name, _, _ in _FLATB_PIECES)

    local = _pack_shards({name: ws[name][0] for name in sharded}, conv_w[0])
    (gathered_in,) = _gather_shards(local[:1], "gather_w_in")
    w_in_full, conv_full = _unpack_w_in(gathered_in, local[0], me_chip)
    small = {n: ws[n][0] for n in _SMALL_NAMES}
    small['conv_w'] = conv_full
    sp = _prep_small(small)
    cosb, sinb = _rope_tables(positions[0])
    exch = _Exchange(local[1:], me_chip, c_arr)

    loss_lanes, dx, g = _local_step(x[0], p[0, 0], cosb, sinb, loss_target[0], _prep_w_in(w_in_full), sp, exch)

    parts = [exch.part_in] + exch.parts
    arrived = exch.arrived_in + exch.arrived
    mine = [_chip_add(p_, r_, me_arr, tr, "chip_add_%d" % i) for i, (p_, r_, tr) in enumerate(zip(parts, arrived, _ADD_TILES))]
    reduced = _unpack_reduced(mine, _reduce_pair_share(mine, "reduce_pair_share"), mc)
    tot = _small_sum(exch.small_gathered, exch.small, jnp.reshape(4 * mx + 2 * my + mc, (1,)).astype(jnp.int32))
    loss = jnp.sum(tot[_LOSS_SMALL_ROW, :LANE])
    gred = {name: reduced[name][None] for name in sharded}
    for i, n in enumerate(_SMALL_NAMES):
        gred[n] = tot[i * _SMALL_GROUP, :ws[n].shape[1]][None]
    conv_tot = tot[_CONV_SMALL_ROW:_CONV_SMALL_ROW + _CONV_SMALL_ROWS].reshape(DN_CONV, QKV_W)
    gred['conv_w'] = lax.dynamic_slice_in_dim(conv_tot, (2 * mx + my) * _CONV_SHARD, _CONV_SHARD, axis=1)[None]

    deltas, new_m, new_v = {}, {}, {}
    for n in _WEIGHT_NAMES:
        if n == 'w_in':
            tr_ = lambda a: jnp.transpose(a, (0, 2, 1))
            g_t = tr_(gred[n].reshape(ws[n].shape))
            outs = _adamw(tr_(ws[n]), g_t, tr_(ms[n]), tr_(vs[n]), "adamw_" + n)
            gred[n] = tr_(g_t)
            deltas[n], new_m[n], new_v[n] = (tr_(o) for o in outs)
            continue
        gred[n] = gred[n].reshape(ws[n].shape)
        deltas[n], new_m[n], new_v[n] = _adamw(ws[n], gred[n], ms[n], vs[n], "adamw_" + n)
    return (loss, dx[None], *[gred[n] for n in _WEIGHT_NAMES], *[deltas[n] for n in _WEIGHT_NAMES],
            *[new_m[n] for n in _WEIGHT_NAMES], *[new_v[n] for n in _WEIGHT_NAMES])
```

```python
import functools

import jax
import jax.numpy as jnp
from jax import lax
from jax.experimental import pallas as pl
from jax.experimental.pallas import tpu as pltpu

F32 = jnp.float32
_CDT = jnp.bfloat16
_MESH = pl.DeviceIdType.MESH

D_MODEL = 1024
PLE_DIM = 256
HEADS = 8
DN_DK = 128
DN_CHUNK = 64
DN_CONV = 4
QKV_W = 3 * HEADS * DN_DK
Q_LORA = 384
KV_LORA = 256
NOPE = 128
ROPE = 64
ROPE_PAD = 128
FFN_HIDDEN = 2816
D_IN = 6864
ROPE_BASE = 10000.0
ALPHA = 2.0 ** 0.25
ATT_SCALE = (NOPE + ROPE) ** -0.5
NEG_BIG = -1e30
ADAM_LR, ADAM_B1, ADAM_B2, ADAM_EPS, ADAM_WD, ADAM_STEP = 0.001, 0.9, 0.999, 1e-08, 0.01, 10

LANE = 128
VMEM_LIMIT = 56 * 1024 * 1024
MM_VMEM_BUDGET = 40 * 1024 * 1024
N_SHARD = 4
FLAT_W = 1024
SMALL_ROWS = 96


def _tile(dim, cap):
    if dim <= cap:
        return dim
    t = (cap // LANE) * LANE
    while t >= LANE:
        if dim % t == 0:
            return t
        t -= LANE
    return dim


def _cparams(sem):
    return pltpu.CompilerParams(dimension_semantics=sem, vmem_limit_bytes=VMEM_LIMIT)


def _mm(a, b, *, name, ta=False, tb=False, add=None, add_scale=1.0, out_dtype=F32, heads=None,
        a_head=None, b_head=None, out_head=None, dims=None, tm=1408, tn=1408, rider=None):
    m, n, k = dims
    tm, tn = _tile(m, tm), _tile(n, tn)
    sa, sb, so = a.dtype.itemsize, b.dtype.itemsize, jnp.dtype(out_dtype).itemsize

    def vmem_need(tk_):
        acc = tm * tn * 4 if tk_ < k else 0
        extra = 2 * tm * tn * 4 if add is not None else 0
        return 2 * (tm * tk_ * sa + tk_ * tn * sb) + 2 * tm * tn * so + acc + extra

    tk = k
    while vmem_need(tk) > MM_VMEM_BUDGET and tk > LANE:
        smaller = _tile(k, tk - LANE)
        if smaller >= tk:
            break
        tk = smaller
    nk = k // tk
    hgrid = () if heads is None else (heads,)
    off = len(hgrid)

    def spec(rows, cols, rtile, ctile, rsel, csel, layout):
        def idx(*g):
            h = g[0] if off else 0
            ri, ci = g[off + rsel], g[off + csel]
            if layout == 'lead':
                return (h, ri, ci)
            if layout == 'col':
                return (ri, h * (cols // ctile) + ci)
            return (ri, ci)
        if layout == 'lead':
            return pl.BlockSpec((None, rtile, ctile), idx)
        return pl.BlockSpec((rtile, ctile), idx)

    a_spec = spec(k, m, tk, tm, 2, 0, a_head) if ta else spec(m, k, tm, tk, 0, 2, a_head)
    b_spec = spec(n, k, tn, tk, 1, 2, b_head) if tb else spec(k, n, tk, tn, 2, 1, b_head)
    o_spec = spec(m, n, tm, tn, 0, 1, out_head)
    in_specs = [a_spec, b_spec]
    args = [a, b]
    if add is not None:
        in_specs.append(spec(m, n, tm, tn, 0, 1, out_head))
        args.append(add)
    dn = (((0 if ta else 1,), (1 if tb else 0,)), ((), ()))

    def body(*refs):
        a_ref, b_ref = refs[0], refs[1]
        prod = lax.dot_general(a_ref[...].astype(_CDT), b_ref[...].astype(_CDT), dn, preferred_element_type=F32)
        if nk == 1:
            o_ref = refs[-1]
            if add is not None:
                prod = prod + refs[2][...].astype(F32) * add_scale
            o_ref[...] = prod.astype(out_dtype)
            return
        o_ref, acc_ref = refs[-2], refs[-1]
        kk = pl.program_id(off + 2)

        @pl.when(kk == 0)
        def _():
            if add is not None:
                acc_ref[...] = refs[2][...].astype(F32) * add_scale
            else:
                acc_ref[...] = jnp.zeros_like(acc_ref)

        acc_ref[...] += prod

        @pl.when(kk == nk - 1)
        def _():
            o_ref[...] = acc_ref[...].astype(out_dtype)

    if out_head == 'lead':
        oshape = (heads, m, n)
    elif out_head == 'col':
        oshape = (m, heads * n)
    else:
        oshape = (m, n)
    grid = hgrid + (m // tm, n // tn, nk)
    scratch = [pltpu.VMEM((tm, tn), F32)] if nk > 1 else []
    if rider is not None:
        (out,), carried = _carried_call(
            body, rider, *_grid_ends(grid), name=name, grid=grid, in_specs=in_specs, out_specs=[o_spec],
            out_shape=[jax.ShapeDtypeStruct(oshape, out_dtype)], scratch_shapes=scratch,
            sem=("arbitrary",) * len(grid), args=tuple(args))
        return out, carried
    sem = ("parallel",) * (off + 2) + ("arbitrary",)
    return pl.pallas_call(
        body, name=name, grid=grid, in_specs=in_specs, out_specs=o_spec,
        out_shape=jax.ShapeDtypeStruct(oshape, out_dtype), scratch_shapes=scratch,
        compiler_params=_cparams(sem))(*args)


def _mm2(a, b, **kw):
    ta, tb = kw.get('ta', False), kw.get('tb', False)
    m = a.shape[1] if ta else a.shape[0]
    k = a.shape[0] if ta else a.shape[1]
    n = b.shape[0] if tb else b.shape[1]
    return _mm(a, b, dims=(m, n, k), **kw)


def _rowwise(fn, rows, bcast, outs, reds=(), *, name, tm=256, heads=None):
    t = rows[0][0].shape[0]
    tm = min(tm, t)
    hn = 1 if heads is None else heads
    in_specs, args = [], []
    for arr, width, base, per_head in rows:
        in_specs.append(pl.BlockSpec((tm, width), functools.partial(
            lambda i, h, base, per_head: (i, base + (h if per_head else 0)), base=base, per_head=per_head)))
        args.append(arr)
    for arr in bcast:
        in_specs.append(pl.BlockSpec(arr.shape, lambda i, h: (0, 0)))
        args.append(arr)
    out_specs, out_shape = [], []
    for total, width, per_head, dt in outs:
        out_specs.append(pl.BlockSpec((tm, width), functools.partial(
            lambda i, h, per_head: (i, h if per_head else 0), per_head=per_head)))
        out_shape.append(jax.ShapeDtypeStruct((t, total), dt))
    for shp in reds:
        out_specs.append(pl.BlockSpec(shp, lambda i, h: (0, 0)))
        out_shape.append(jax.ShapeDtypeStruct(shp, F32))
    n_in, n_out, n_red = len(args), len(outs), len(reds)

    def body(*refs):
        i, h = pl.program_id(0), pl.program_id(1)
        vals = fn(h, *[r[...] for r in refs[:n_in]])
        for r, v in zip(refs[n_in:n_in + n_out], vals[:n_out]):
            r[...] = v.astype(r.dtype)
        if n_red:
            @pl.when((i == 0) & (h == 0))
            def _():
                for r in refs[n_in + n_out:]:
                    r[...] = jnp.zeros_like(r)
            for r, v in zip(refs[n_in + n_out:], vals[n_out:]):
                r[...] += v

    sem = ("arbitrary", "arbitrary") if n_red else ("parallel", "parallel")
    res = pl.pallas_call(body, name=name, grid=(t // tm, hn), in_specs=in_specs, out_specs=out_specs,
                         out_shape=out_shape, compiler_params=_cparams(sem))(*args)
    return tuple(res)


def _sigmoid(x):
    return 1.0 / (1.0 + jnp.exp(-x))


def _silu(x):
    return x * _sigmoid(x)


def _softplus(x):
    return jnp.maximum(x, 0.0) + jnp.log(1.0 + jnp.exp(-jnp.abs(x)))


def _layer_norm(t, g, b):
    mu = jnp.mean(t, axis=-1, keepdims=True)
    d = t - mu
    var = jnp.mean(d * d, axis=-1, keepdims=True)
    return d * lax.rsqrt(var + 1e-5) * g + b


def _rms_norm(t, w):
    return t * lax.rsqrt(jnp.mean(t * t, axis=-1, keepdims=True) + 1e-6) * w


def _swap_rope_halves(t):
    lane = lax.broadcasted_iota(jnp.int32, t.shape, 1) % ROPE_PAD
    n = t.shape[1]
    up = pltpu.roll(t, n - ROPE // 2, axis=1)
    dn = pltpu.roll(t, ROPE // 2, axis=1)
    return jnp.where(lane < ROPE // 2, up, jnp.where(lane < ROPE, dn, 0.0))


def _rope(t, cosb, sinb):
    reps = t.shape[1] // ROPE_PAD
    c = jnp.tile(cosb, (1, reps)) if reps > 1 else cosb
    s = jnp.tile(sinb, (1, reps)) if reps > 1 else sinb
    return t * c + _swap_rope_halves(t) * s


def _rope_bwd(d, cosb, sinb):
    reps = d.shape[1] // ROPE_PAD
    c = jnp.tile(cosb, (1, reps)) if reps > 1 else cosb
    s = jnp.tile(sinb, (1, reps)) if reps > 1 else sinb
    return d * c + _swap_rope_halves(d * s)


_CONV_ROWS = 256
_CONV_COLS = 256


def _conv_window(ref, r0, lo, hi, t):
    parts = []
    start, stop = r0 - lo, r0 + _CONV_ROWS + hi
    if start < 0:
        parts.append(jnp.zeros((-start, ref.shape[1]), F32))
        start = 0
    tail = max(stop - t, 0)
    parts.append(ref[start:stop - tail, :].astype(F32))
    if tail:
        parts.append(jnp.zeros((tail, ref.shape[1]), F32))
    return parts[0] if len(parts) == 1 else jnp.concatenate(parts, axis=0)


def _conv_taps(win, w_ref, n_out):
    acc = win[8:8 + n_out] * w_ref[DN_CONV - 1:DN_CONV, :]
    for i in range(DN_CONV - 1):
        acc = acc + pltpu.roll(win, DN_CONV - 1 - i, axis=0)[8:8 + n_out] * w_ref[i:i + 1, :]
    return acc


def _conv_silu(x, w):
    t, ch = x.shape

    def body(x_ref, w_ref, o_ref):
        for r in range(t // _CONV_ROWS):
            r0 = r * _CONV_ROWS
            c = _conv_taps(_conv_window(x_ref, r0, 8, 0, t), w_ref, _CONV_ROWS)
            o_ref[r0:r0 + _CONV_ROWS, :] = _silu(c)

    return pl.pallas_call(
        body, name="conv_silu", grid=(ch // _CONV_COLS,),
        in_specs=[pl.BlockSpec((t, _CONV_COLS), lambda j: (0, j)), pl.BlockSpec((DN_CONV, _CONV_COLS), lambda j: (0, j))],
        out_specs=pl.BlockSpec((t, _CONV_COLS), lambda j: (0, j)),
        out_shape=jax.ShapeDtypeStruct((t, ch), F32), compiler_params=_cparams(("parallel",)))(x, w)


def _conv_silu_bwd(x, w, dys):
    t, ch = x.shape
    per = ch // len(dys) // _CONV_COLS

    def body(x_ref, w_ref, *rest):
        dy_refs, (dx_ref, dw_ref) = rest[:len(dys)], rest[len(dys):]
        sec = pl.program_id(0) // per
        dws = [jnp.zeros((1, _CONV_COLS), F32) for _ in range(DN_CONV)]
        for r in range(t // _CONV_ROWS):
            r0 = r * _CONV_ROWS
            n_ext = _CONV_ROWS + 8
            xw = _conv_window(x_ref, r0, 8, 8, t)
            c = _conv_taps(xw, w_ref, n_ext)
            sg = _sigmoid(c)
            dy = _conv_window(dy_refs[-1], r0, 0, 8, t)
            for k in range(len(dys) - 2, -1, -1):
                dy = jnp.where(sec == k, _conv_window(dy_refs[k], r0, 0, 8, t), dy)
            ds = dy * (sg * (1.0 + c * (1.0 - sg)))
            x0 = xw[8:8 + _CONV_ROWS]
            dx = jnp.zeros((_CONV_ROWS, _CONV_COLS), F32)
            for i in range(DN_CONV):
                sh = DN_CONV - 1 - i
                ds_up = (ds if sh == 0 else pltpu.roll(ds, n_ext - sh, axis=0))[:_CONV_ROWS]
                dx = dx + ds_up * w_ref[i:i + 1, :]
                dws[i] = dws[i] + jnp.sum(x0 * ds_up, axis=0, keepdims=True)
            dx_ref[r0:r0 + _CONV_ROWS, :] = dx.astype(dx_ref.dtype)
        for i in range(DN_CONV):
            dw_ref[i:i + 1, :] = dws[i]

    blk = pl.BlockSpec((t, _CONV_COLS), lambda j: (0, j))
    wblk = pl.BlockSpec((DN_CONV, _CONV_COLS), lambda j: (0, j))
    dy_specs = [pl.BlockSpec((t, _CONV_COLS), functools.partial(lambda j, k: (0, jnp.clip(j - k * per, 0, per - 1)), k=k))
                for k in range(len(dys))]
    return pl.pallas_call(
        body, name="conv_silu_bwd", grid=(ch // _CONV_COLS,), in_specs=[blk, wblk] + dy_specs, out_specs=[blk, wblk],
        out_shape=[jax.ShapeDtypeStruct((t, ch), _CDT), jax.ShapeDtypeStruct((DN_CONV, ch), F32)],
        compiler_params=_cparams(("arbitrary",)))(x, w, *dys)


_PA_ROWS = 1024


def _bmm(a, b, spec):
    return jnp.einsum(spec, a.astype(_CDT), b.astype(_CDT), preferred_element_type=F32)


def _split16(a):
    hi = a.astype(jnp.bfloat16)
    return hi, (a - hi.astype(F32)).astype(jnp.bfloat16)


def _bmm3(a, b, spec):
    ah, al = _split16(a)
    bh, bl = _split16(b)
    e = lambda p, q: jnp.einsum(spec, p, q, preferred_element_type=F32)
    return e(ah, bh) + (e(ah, bl) + e(al, bh))


def _split3(b):
    b0 = b.astype(jnp.bfloat16)
    r1 = b - b0.astype(F32)
    b1 = r1.astype(jnp.bfloat16)
    return b0, b1, (r1 - b1.astype(F32)).astype(jnp.bfloat16)


@functools.partial(jax.custom_vjp, nondiff_argnums=(2, 3))
def _select_mm(sel, b, spec, spec_t):
    return sum(jnp.einsum(spec, sel, t, preferred_element_type=F32) for t in _split3(b))


def _select_mm_fwd(sel, b, spec, spec_t):
    return _select_mm(sel, b, spec, spec_t), sel


def _select_mm_bwd(spec, spec_t, sel, ct):
    return jnp.zeros_like(sel), sum(jnp.einsum(spec_t, sel, t, preferred_element_type=F32) for t in _split3(ct))


_select_mm.defvjp(_select_mm_fwd, _select_mm_bwd)


def _tri_inverse(l_mat, eye):
    pw = -l_mat
    t_inv = eye + pw
    for _ in range(5):
        pw = _bmm3(pw, pw, 'bij,bjk->bik')
        t_inv = t_inv + _bmm3(t_inv, pw, 'bij,bjk->bik')
    return t_inv


@jax.custom_vjp
def _tri_inverse_saved(l_mat, t_saved):
    return t_saved


def _tri_inverse_saved_fwd(l_mat, t_saved):
    return t_saved, t_saved


def _tri_inverse_saved_bwd(t_saved, dt):
    left = _bmm3(t_saved, dt, 'bji,bjk->bik')
    return -_bmm3(left, t_saved, 'bij,bkj->bik'), jnp.zeros_like(t_saved)


_tri_inverse_saved.defvjp(_tri_inverse_saved_fwd, _tri_inverse_saved_bwd)


def _phase_a(h, q, k, v, ba, alog, dtb, t_saved=None):
    r = q.shape[0]
    nb = r // DN_CHUNK
    c = DN_CHUNK
    lane = lax.broadcasted_iota(jnp.int32, (1, LANE), 1)
    selb = (lane == h).astype(F32)
    sela = (lane == h + HEADS).astype(F32)
    b_raw = jnp.sum(ba * selb, axis=1, keepdims=True)
    a_raw = jnp.sum(ba * sela, axis=1, keepdims=True)
    al = jnp.sum(alog * selb, axis=1, keepdims=True)
    dt = jnp.sum(dtb * selb, axis=1, keepdims=True)
    beta = jnp.broadcast_to(_sigmoid(b_raw), (r, LANE))
    g = jnp.broadcast_to(-jnp.exp(al) * _softplus(a_raw + dt), (r, LANE))
    qn = q * lax.rsqrt(jnp.sum(q * q, -1, keepdims=True) + 1e-6) * (DN_DK ** -0.5)
    kn = k * lax.rsqrt(jnp.sum(k * k, -1, keepdims=True) + 1e-6)
    q3, k3, v3 = qn.reshape(nb, c, LANE), kn.reshape(nb, c, LANE), v.reshape(nb, c, LANE)
    b3, g3 = beta.reshape(nb, c, LANE), g.reshape(nb, c, LANE)
    ri = lax.broadcasted_iota(jnp.int32, (nb, c, c), 1)
    ci = lax.broadcasted_iota(jnp.int32, (nb, c, c), 2)
    tril, strict = ri >= ci, ri > ci
    gc = _select_mm(tril.astype(jnp.bfloat16), g3, 'bij,bjd->bid', 'bij,bid->bjd')
    onehot = (lax.broadcasted_iota(jnp.int32, (nb, c, LANE), 2) == 0).astype(jnp.bfloat16)
    g_row = _select_mm(onehot, gc, 'bid,bjd->bij', 'bid,bij->bjd')
    diff = gc[:, :, :c] - g_row
    decay = jnp.where(tril, jnp.exp(jnp.where(tril, diff, 0.0)), 0.0)
    kb = k3 * b3
    l_mat = jnp.where(strict, _bmm(kb, k3, 'bid,bjd->bij') * decay, 0.0)
    if t_saved is None:
        t_inv = _tri_inverse(l_mat, (ri == ci).astype(F32))
    else:
        t_inv = _tri_inverse_saved(l_mat, t_saved.reshape(nb, c, c))
    eg = jnp.exp(gc)
    u = _bmm(t_inv, v3 * b3, 'bij,bje->bie')
    w = _bmm(t_inv, kb * eg, 'bij,bje->bie')
    intra = jnp.where(tril, _bmm(q3, k3, 'bid,bjd->bij') * decay, 0.0)
    qd = q3 * eg
    gl = jnp.sum(g3, axis=1, keepdims=True)
    kt = k3 * jnp.exp(gl - gc)
    outs = (u.reshape(r, LANE), w.reshape(r, LANE), qd.reshape(r, LANE), kt.reshape(r, LANE),
            intra.reshape(r, c), gl.reshape(nb, LANE))
    if t_saved is not None:
        return outs
    qd2 = qd - _bmm(intra, w, 'bij,bjd->bid')
    au = _bmm(intra, u, 'bij,bje->bie')
    return outs + (t_inv.reshape(r, c), qd2.reshape(r, LANE), au.reshape(r, LANE))


def _pa_specs(t):
    rr = min(_PA_ROWS, t)
    nb = rr // DN_CHUNK
    qkv = [pl.BlockSpec((rr, LANE), functools.partial(lambda i, h, o: (i, o + h), o=o)) for o in (0, HEADS, 2 * HEADS)]
    ba = pl.BlockSpec((rr, LANE), lambda i, h: (i, 3))
    vec = pl.BlockSpec((1, LANE), lambda i, h: (0, 0))
    row = pl.BlockSpec((rr, LANE), lambda i, h: (i, h))
    intra = pl.BlockSpec((None, rr, DN_CHUNK), lambda i, h: (h, i, 0))
    gl = pl.BlockSpec((nb, LANE), lambda i, h: (i, h))
    return rr, qkv, ba, vec, row, intra, gl


def _grid_ends(grid):
    first = lambda: functools.reduce(jnp.logical_and, [pl.program_id(a) == 0 for a in range(len(grid))])
    last = lambda: functools.reduce(jnp.logical_and, [pl.program_id(a) == n - 1 for a, n in enumerate(grid)])
    return first, last


def _delta_local(qkv_act, pm, alog, dtb, rider=None):
    t = qkv_act.shape[0]
    rr, qkv, ba, vec, row, intra, gl = _pa_specs(t)

    def body(q, k, v, b, al, dt, *outs):
        vals = _phase_a(pl.program_id(1), q[...], k[...], v[...], b[...], al[...], dt[...])
        for o, val in zip(outs, vals):
            o[...] = val

    wide = jax.ShapeDtypeStruct((t, HEADS * LANE), F32)
    sq = jax.ShapeDtypeStruct((HEADS, t, DN_CHUNK), F32)
    grid = (t // rr, HEADS)
    return _carried_call(
        body, rider, *_grid_ends(grid), name="delta_local", grid=grid, in_specs=qkv + [ba, vec, vec],
        out_specs=[row] * 4 + [intra, gl, intra, row, row],
        out_shape=[wide] * 4 + [sq, jax.ShapeDtypeStruct((t // DN_CHUNK, HEADS * LANE), F32), sq, wide, wide],
        scratch_shapes=[], sem=("arbitrary", "arbitrary"), args=(qkv_act, qkv_act, qkv_act, pm, alog, dtb))


def _delta_local_bwd(qkv_act, pm, alog, dtb, t_inv, du, dw, dqd, dkt, dintra, dgl, rider=None):
    t = qkv_act.shape[0]
    rr, qkv, ba, vec, row, intra, gl = _pa_specs(t)

    def body(q, k, v, b, al, dt, ti, du_r, dw_r, dqd_r, dkt_r, di_r, dgl_r, dq_o, dk_o, dv_o, dba_o, dal_o, ddt_o):
        i, h = pl.program_id(0), pl.program_id(1)
        t_saved = ti[...]
        _, vjp = jax.vjp(lambda *a: _phase_a(h, *a, t_saved=t_saved), q[...], k[...], v[...], b[...], al[...], dt[...])
        dq, dk, dv, dba, dal, ddt = vjp((du_r[...], dw_r[...], dqd_r[...], dkt_r[...], di_r[...], dgl_r[...]))
        dq_o[...], dk_o[...], dv_o[...] = dq, dk, dv

        @pl.when(h == 0)
        def _():
            dba_o[...] = jnp.zeros_like(dba_o)

        @pl.when((h == 0) & (i == 0))
        def _():
            dal_o[...] = jnp.zeros_like(dal_o)
            ddt_o[...] = jnp.zeros_like(ddt_o)

        dba_o[...] += dba
        dal_o[...] += dal
        ddt_o[...] += ddt

    wide = jax.ShapeDtypeStruct((t, HEADS * LANE), F32)
    vshape = jax.ShapeDtypeStruct((1, LANE), F32)
    grid = (t // rr, HEADS)
    return _carried_call(
        body, rider, *_grid_ends(grid), name="delta_local_bwd", grid=grid,
        in_specs=qkv + [ba, vec, vec, intra] + [row] * 4 + [intra, gl],
        out_specs=[row] * 3 + [pl.BlockSpec((rr, LANE), lambda i, h: (i, 0)), vec, vec],
        out_shape=[wide] * 3 + [jax.ShapeDtypeStruct((t, LANE), F32), vshape, vshape],
        scratch_shapes=[], sem=("arbitrary", "arbitrary"),
        args=(qkv_act, qkv_act, qkv_act, pm, alog, dtb, t_inv, du, dw, dqd, dkt, dintra, dgl))


_SCAN_ROWS = 512


def _dot(a, b, dn):
    return lax.dot_general(a.astype(_CDT), b.astype(_CDT), (dn, ((), ())), preferred_element_type=F32)


_NN = ((1,), (0,))
_NT = ((1,), (1,))
_TN = ((0,), (0,))


def _delta_scan(u, w, qd, kt, au, gl, rider=None):
    t = u.shape[0]
    rr = min(_SCAN_ROWS, t)
    nc = rr // DN_CHUNK

    def body(u_ref, w_ref, qd_ref, kt_ref, au_ref, gl_ref, o_ref, sall_ref, s_scr):
        @pl.when(pl.program_id(0) == 0)
        def _():
            s_scr[...] = jnp.zeros_like(s_scr)

        def chunk(c, carry):
            r0 = pl.multiple_of(c * DN_CHUNK, DN_CHUNK)
            rows = pl.ds(r0, DN_CHUNK)
            e = jnp.exp(gl_ref[pl.ds(c, 1), :])
            states = [s_scr[h] for h in range(HEADS)]
            u_c, w_c, qd_c, kt_c, au_c = u_ref[rows, :], w_ref[rows, :], qd_ref[rows, :], kt_ref[rows, :], au_ref[rows, :]
            o_new, s_new = [], []
            for h in range(HEADS):
                cs = slice(h * LANE, (h + 1) * LANE)
                s = states[h]
                both = _dot(jnp.concatenate([w_c[:, cs], qd_c[:, cs]], axis=0), s, _NN)
                v_new = u_c[:, cs] - both[:DN_CHUNK]
                o_new.append(both[DN_CHUNK:] + au_c[:, cs])
                s_new.append(s * e[:, cs] + _dot(kt_c[:, cs], v_new, _TN))
            o_ref[rows, :] = jnp.concatenate(o_new, axis=1)
            for h in range(HEADS):
                sall_ref[c, h] = states[h]
                s_scr[h] = s_new[h]
            return carry

        lax.fori_loop(0, nc, chunk, 0)

    row = pl.BlockSpec((rr, HEADS * LANE), lambda i: (i, 0))
    grid = (t // rr,)
    return _carried_call(
        body, rider, *_grid_ends(grid), name="delta_scan", grid=grid,
        in_specs=[row] * 5 + [pl.BlockSpec((nc, HEADS * LANE), lambda i: (i, 0))],
        out_specs=[row, pl.BlockSpec((nc, HEADS, LANE, LANE), lambda i: (i, 0, 0, 0))],
        out_shape=[jax.ShapeDtypeStruct((t, HEADS * LANE), F32),
                   jax.ShapeDtypeStruct((t // DN_CHUNK, HEADS, LANE, LANE), F32)],
        scratch_shapes=[pltpu.VMEM((HEADS, LANE, LANE), F32)], sem=("arbitrary",), args=(u, w, qd, kt, au, gl))


def _delta_scan_bwd(u, w, qd, kt, intra, gl, sall, do, rider=None):
    t = u.shape[0]
    rr = min(_SCAN_ROWS, t)
    nc = rr // DN_CHUNK
    ng = t // rr

    def body(u_ref, w_ref, qd_ref, kt_ref, a_ref, gl_ref, sall_ref, do_ref,
             du_ref, dw_ref, dqd_ref, dkt_ref, da_ref, dgl_ref, ds_scr):
        @pl.when(pl.program_id(0) == 0)
        def _():
            ds_scr[...] = jnp.zeros_like(ds_scr)

        def chunk(cc, carry):
            c = nc - 1 - cc
            r0 = pl.multiple_of(c * DN_CHUNK, DN_CHUNK)
            rows = pl.ds(r0, DN_CHUNK)
            e = jnp.exp(gl_ref[pl.ds(c, 1), :])
            states = [sall_ref[c, h] for h in range(HEADS)]
            ds_outs = [ds_scr[h] for h in range(HEADS)]
            u_a, w_a, kt_a, qd_a, do_a = u_ref[rows, :], w_ref[rows, :], kt_ref[rows, :], qd_ref[rows, :], do_ref[rows, :]
            a_a = [a_ref[h, rows, :] for h in range(HEADS)]
            da, dqd, dkt, du, dw, dgl, ds_new = [], [], [], [], [], [], []
            for h in range(HEADS):
                cs = slice(h * LANE, (h + 1) * LANE)
                s, ds_out = states[h], ds_outs[h]
                w_c, kt_c, qd_c, do_c = w_a[:, cs], kt_a[:, cs], qd_a[:, cs], do_a[:, cs]
                v_new = u_a[:, cs] - _dot(w_c, s, _NN)
                dv_new = _dot(a_a[h], do_c, _TN) + _dot(kt_c, ds_out, _NN)
                cots = jnp.concatenate([do_c, dv_new], axis=0)
                both = _dot(cots, s, _NT)
                dqd.append(both[:DN_CHUNK])
                dw.append(-both[DN_CHUNK:])
                da.append(_dot(do_c, v_new, _NT))
                dkt.append(_dot(v_new, ds_out, _NT))
                du.append(dv_new)
                eh = e[:, cs]
                dgl.append(jnp.broadcast_to(jnp.sum(ds_out * s, axis=0, keepdims=True) * eh, (8, LANE)))
                ds_new.append(ds_out * eh + _dot(jnp.concatenate([qd_c, -w_c], axis=0), cots, _TN))
            cat = lambda parts: jnp.concatenate(parts, axis=1)
            dqd_ref[rows, :], dkt_ref[rows, :], du_ref[rows, :], dw_ref[rows, :] = cat(dqd), cat(dkt), cat(du), cat(dw)
            dgl_ref[pl.ds(pl.multiple_of(c * 8, 8), 8), :] = cat(dgl)
            for h in range(HEADS):
                da_ref[h, rows, :] = da[h]
                ds_scr[h] = ds_new[h]
            return carry

        lax.fori_loop(0, nc, chunk, 0)

    rev = lambda i: (ng - 1 - i, 0)
    row = pl.BlockSpec((rr, HEADS * LANE), rev)
    a_spec = pl.BlockSpec((HEADS, rr, DN_CHUNK), lambda i: (0, ng - 1 - i, 0))
    gl_spec = pl.BlockSpec((nc, HEADS * LANE), rev)
    wide = jax.ShapeDtypeStruct((t, HEADS * LANE), F32)
    outs, carried = _carried_call(
        body, rider, *_grid_ends((ng,)), name="delta_scan_bwd", grid=(ng,),
        in_specs=[row] * 4 + [a_spec, gl_spec, pl.BlockSpec((nc, HEADS, LANE, LANE), lambda i: (ng - 1 - i, 0, 0, 0)), row],
        out_specs=[row] * 4 + [a_spec, pl.BlockSpec((nc * 8, HEADS * LANE), rev)],
        out_shape=[wide] * 4 + [jax.ShapeDtypeStruct((HEADS, t, DN_CHUNK), F32),
                                jax.ShapeDtypeStruct((t // DN_CHUNK * 8, HEADS * LANE), F32)],
        scratch_shapes=[pltpu.VMEM((HEADS, LANE, LANE), F32)], sem=("arbitrary",), args=(u, w, qd, kt, intra, gl, sall, do))
    return tuple(outs[:5]) + (outs[5].reshape(t // DN_CHUNK, 8, HEADS * LANE)[:, 0, :],), carried


_ATT_TILE = 512


def _kv_rows(j, tk):
    return pl.ds(pl.multiple_of(j * tk, tk), tk)


def _att_scores(ql, qr, ckv_ref, kr_ref, j, tk):
    ks = _kv_rows(j, tk)
    return (_dot(ql, ckv_ref[ks, :], _NT) + _dot(qr, kr_ref[ks, :], _NT)) * ATT_SCALE


def _diag_mask(s):
    qi = lax.broadcasted_iota(jnp.int32, s.shape, 0)
    ki = lax.broadcasted_iota(jnp.int32, s.shape, 1)
    return jnp.where(ki <= qi, s, NEG_BIG)


def _attention(qn, qr_pre, cosb, sinb, ckv, kr, wuk, wuv):
    t = ckv.shape[0]
    tq = min(_ATT_TILE, t)

    nl = tq // LANE

    def lane_fold(v, op):
        out = v[:, :LANE]
        for k in range(1, nl):
            out = op(out, v[:, k * LANE:(k + 1) * LANE])
        return out

    def body(qn_ref, qr_ref, cos_ref, sin_ref, ckv_ref, kr_ref, wuk_ref, wuv_ref, o_ref, lse_ref, qrope_ref, omla_ref,
             s_all, m_lanes, l_lanes, acc_scr):
        h, qi = pl.program_id(0), pl.program_id(1)
        q_lat = _dot(qn_ref[...], wuk_ref[h], _NT).astype(_CDT)
        q_rope = _rope(qr_ref[...], cos_ref[...], sin_ref[...]).astype(qrope_ref.dtype)
        qrope_ref[...] = q_rope
        m_lanes[...] = jnp.full_like(m_lanes, NEG_BIG)

        def scores(j, masked):
            s = _att_scores(q_lat, q_rope, ckv_ref, kr_ref, j, tq)
            if masked:
                s = _diag_mask(s)
            s_all[j] = s
            m_lanes[...] = jnp.maximum(m_lanes[...], lane_fold(s, jnp.maximum))

        def scores_body(j, carry):
            scores(j, False)
            return carry

        lax.fori_loop(0, qi, scores_body, 0)
        scores(qi, True)
        m = jnp.max(m_lanes[...], axis=-1, keepdims=True)
        mb = jnp.broadcast_to(m, (tq, LANE))
        l_lanes[...] = jnp.zeros_like(l_lanes)
        acc_scr[...] = jnp.zeros_like(acc_scr)

        def weigh(j, carry):
            s = s_all[j]
            p = jnp.concatenate([jnp.exp(s[:, k * LANE:(k + 1) * LANE] - mb) for k in range(nl)], axis=1)
            l_lanes[...] += lane_fold(p, jnp.add)
            acc_scr[...] += _dot(p, ckv_ref[_kv_rows(j, tq), :], _NN)
            return carry

        lax.fori_loop(0, qi + 1, weigh, 0)
        l = jnp.sum(l_lanes[...], axis=-1, keepdims=True)
        out = acc_scr[...] / l
        o_ref[...] = out
        lse_ref[...] = m + jnp.log(l)
        omla_ref[...] = _dot(out, wuv_ref[h], _NN).astype(omla_ref.dtype)

    col = pl.BlockSpec((tq, LANE), lambda h, i: (i, h))
    table = pl.BlockSpec((tq, ROPE_PAD), lambda h, i: (i, 0))
    wspec = pl.BlockSpec((HEADS, KV_LORA, NOPE), lambda h, i: (0, 0, 0))
    return pl.pallas_call(
        body, name="attention", grid=(HEADS, t // tq),
        in_specs=[col, col, table, table, pl.BlockSpec((t, KV_LORA), lambda h, i: (0, 0)),
                  pl.BlockSpec((t, ROPE_PAD), lambda h, i: (0, 0)), wspec, wspec],
        out_specs=[pl.BlockSpec((None, tq, KV_LORA), lambda h, i: (h, i, 0)),
                   pl.BlockSpec((None, tq, 1), lambda h, i: (h, i, 0)), col, col],
        out_shape=[jax.ShapeDtypeStruct((HEADS, t, KV_LORA), F32), jax.ShapeDtypeStruct((HEADS, t, 1), F32),
                   jax.ShapeDtypeStruct((t, HEADS * ROPE_PAD), _CDT), jax.ShapeDtypeStruct((t, HEADS * NOPE), _CDT)],
        scratch_shapes=[pltpu.VMEM((t // tq, tq, tq), F32), pltpu.VMEM((tq, LANE), F32), pltpu.VMEM((tq, LANE), F32),
                        pltpu.VMEM((tq, KV_LORA), F32)],
        compiler_params=_cparams(("parallel", "parallel")))(qn, qr_pre, cosb, sinb, ckv, kr, wuk, wuv)


def _attention_bwd(qn, qr, cosb, sinb, ckv, kr, wuk, wuv, out, lse, do_mla):
    t = ckv.shape[0]
    tq = min(_ATT_TILE, t)

    def body(qn_ref, qr_ref, cos_ref, sin_ref, ckv_ref, kr_ref, wuk_ref, wuv_ref, o_ref, lse_ref, do_ref,
             dql_ref, dqr_ref, dqn_ref, dckv_ref, dkr_ref, dql_scr, dqr_scr):
        h, qi = pl.program_id(0), pl.program_id(1)

        @pl.when((h == 0) & (qi == 0))
        def _():
            dckv_ref[...] = jnp.zeros_like(dckv_ref)
            dkr_ref[...] = jnp.zeros_like(dkr_ref)

        q_lat = _dot(qn_ref[...], wuk_ref[h], _NT).astype(_CDT)
        q_rope = qr_ref[...]
        d_out = _dot(do_ref[...], wuv_ref[h], _NT)
        d_o = d_out.astype(_CDT)
        lse_v = lse_ref[...]
        dsum = jnp.sum(d_out * o_ref[...], axis=-1, keepdims=True)
        dql_scr[...] = jnp.zeros_like(dql_scr)
        dqr_scr[...] = jnp.zeros_like(dqr_scr)

        def step(j, masked):
            ks = _kv_rows(j, tq)
            s = _att_scores(q_lat, q_rope, ckv_ref, kr_ref, j, tq)
            if masked:
                s = _diag_mask(s)
            p = jnp.exp(s - lse_v)
            kv = ckv_ref[ks, :]
            ds = (p * (_dot(d_o, kv, _NT) - dsum) * ATT_SCALE).astype(_CDT)
            pb = p.astype(_CDT)
            dql_scr[...] += _dot(ds, kv, _NN)
            dqr_scr[...] += _dot(ds, kr_ref[ks, :], _NN)
            dckv_ref[ks, :] += _dot(pb, d_o, _TN) + _dot(ds, q_lat, _TN)
            dkr_ref[ks, :] += _dot(ds, q_rope, _TN)

        def loop_body(j, carry):
            step(j, False)
            return carry

        lax.fori_loop(0, qi, loop_body, 0)
        step(qi, True)
        dql = dql_scr[...].astype(dql_ref.dtype)
        dql_ref[...] = dql
        dqn_ref[...] = _dot(dql, wuk_ref[h], _NN).astype(dqn_ref.dtype)
        dqr_ref[...] = _rope_bwd(dqr_scr[...], cos_ref[...], sin_ref[...]).astype(dqr_ref.dtype)

    lat = pl.BlockSpec((None, tq, KV_LORA), lambda h, i: (h, i, 0))
    col = pl.BlockSpec((tq, LANE), lambda h, i: (i, h))
    table = pl.BlockSpec((tq, ROPE_PAD), lambda h, i: (i, 0))
    kfull = pl.BlockSpec((t, KV_LORA), lambda h, i: (0, 0))
    rfull = pl.BlockSpec((t, ROPE_PAD), lambda h, i: (0, 0))
    wspec = pl.BlockSpec((HEADS, KV_LORA, NOPE), lambda h, i: (0, 0, 0))
    wide = jax.ShapeDtypeStruct((t, HEADS * LANE), _CDT)
    return pl.pallas_call(
        body, name="attention_bwd", grid=(HEADS, t // tq),
        in_specs=[col, col, table, table, kfull, rfull, wspec, wspec, lat,
                  pl.BlockSpec((None, tq, 1), lambda h, i: (h, i, 0)), col],
        out_specs=[lat, col, col, kfull, rfull],
        out_shape=[jax.ShapeDtypeStruct((HEADS, t, KV_LORA), _CDT), wide, wide,
                   jax.ShapeDtypeStruct((t, KV_LORA), F32), jax.ShapeDtypeStruct((t, ROPE_PAD), F32)],
        scratch_shapes=[pltpu.VMEM((tq, KV_LORA), F32), pltpu.VMEM((tq, ROPE_PAD), F32)],
        compiler_params=_cparams(("arbitrary", "arbitrary")))(qn, qr, cosb, sinb, ckv, kr, wuk, wuv, out, lse, do_mla)


_DX_ROWS = 256


def _dx_fused(pairs, add, add_scale, rider=None, name="b_dx"):
    t, d = add.shape
    tm = min(_DX_ROWS, t)
    n = len(pairs)

    def body(*refs):
        acc = refs[2 * n][...] * add_scale
        for i in range(n):
            acc = acc + _dot(refs[i][...], refs[n + i][...], _NT)
        refs[2 * n + 1][...] = acc

    in_specs = [pl.BlockSpec((tm, a.shape[1]), lambda i: (i, 0)) for a, _ in pairs]
    in_specs += [pl.BlockSpec(w.shape, lambda i: (0, 0)) for _, w in pairs]
    row = pl.BlockSpec((tm, d), lambda i: (i, 0))
    grid = (t // tm,)
    (dx,), carried = _carried_call(
        body, rider, *_grid_ends(grid), name=name, grid=grid, in_specs=in_specs + [row], out_specs=[row],
        out_shape=[jax.ShapeDtypeStruct((t, d), F32)], scratch_shapes=[], sem=("arbitrary",),
        args=tuple(a for a, _ in pairs) + tuple(w for _, w in pairs) + (add,))
    return dx, carried


def _ffn_in_swiglu(a, w):
    t, k = a.shape
    hid = w.shape[1] // 2
    tm, tn = _tile(t, 1024), _tile(hid, 1408)
    nj = hid // tn

    def body(a_ref, bg_ref, bu_ref, act_ref, gt_ref, up_ref):
        av = a_ref[...].astype(_CDT)
        gt = jnp.dot(av, bg_ref[...].astype(_CDT), preferred_element_type=F32)
        up = jnp.dot(av, bu_ref[...].astype(_CDT), preferred_element_type=F32)
        act_ref[...] = _swiglu(gt, up).astype(act_ref.dtype)
        gt_ref[...] = gt.astype(gt_ref.dtype)
        up_ref[...] = up.astype(up_ref.dtype)

    out = pl.BlockSpec((tm, tn), lambda i, j: (i, j))
    return pl.pallas_call(
        body, name="f_ffn_in_swiglu", grid=(t // tm, nj),
        in_specs=[pl.BlockSpec((tm, k), lambda i, j: (i, 0)), pl.BlockSpec((k, tn), lambda i, j: (0, j)),
                  pl.BlockSpec((k, tn), lambda i, j: (0, nj + j))],
        out_specs=[out] * 3, out_shape=[jax.ShapeDtypeStruct((t, hid), _CDT)] * 3,
        compiler_params=_cparams(("parallel", "parallel")))(a, w, w)


def _gated_norm(o, z, w):
    return _rms_norm(o, w) * _silu(z)


def _gated_norm_heads(o, z, w):
    heads = [_gated_norm(o[:, h * LANE:(h + 1) * LANE], z[:, h * LANE:(h + 1) * LANE], w) for h in range(HEADS)]
    return jnp.concatenate(heads, axis=1)


def _mla_pre(ckv, krp, cq, cosb, sinb, qw, kw):
    return _rms_norm(cq, qw), _rms_norm(ckv, kw), _rope(krp, cosb, sinb)


def _merge(gg, y_dn, y_mla):
    return _sigmoid(gg[:, :D_MODEL]) * y_dn + _sigmoid(gg[:, D_MODEL:]) * y_mla


def _ln1(xv, attn_out, g, b):
    return _layer_norm(ALPHA * xv + attn_out, g, b)


def _final(h1, ffn, gate_pre, ple_proj, g, b):
    return _layer_norm(ALPHA * h1 + ffn + _sigmoid(gate_pre) * ple_proj, g, b)


def _swiglu(gt, up):
    return _silu(gt) * up


def _local_step(x, p, cosb, sinb, target, wt, sp, exch):
    t = x.shape[0]
    bf = _CDT
    xb = x.astype(bf)
    g = {}

    qkv_pre = _mm2(xb, wt['qkv'], name="f_qkv")
    z = _mm2(xb, wt['z'], name="f_z")
    gg = _mm2(xb, wt['gg'], name="f_gg")
    pm = _mm2(xb, wt['mla'], name="f_mla")
    qkv_act = _conv_silu(qkv_pre, sp['conv_w'])
    (u, w_, qd, kt, intra, gl, t_inv, qd2, au), sent = _delta_local(qkv_act, pm, sp['a_log'], sp['dt_bias'],
                                                                    rider=exch.gather_send())
    (o_dn, sall), passed = _delta_scan(u, w_, qd2, kt, au, gl, rider=exch.gather_pass(sent))
    wt = dict(wt, **exch.weights(passed))
    def gated_norm_br(h, o, zz, w, wbr):
        og_v = _gated_norm_heads(o, zz, w).astype(bf)
        return og_v, jnp.dot(og_v, wbr.astype(bf), preferred_element_type=F32)

    og, y_dn = _rowwise(gated_norm_br, [(o_dn, D_MODEL, 0, False), (z, D_MODEL, 0, False)],
                        [sp['dn_norm_w'], wt['br_dn']],
                        [(D_MODEL, D_MODEL, False, bf), (D_MODEL, D_MODEL, False, F32)], name="f_gated_norm_br", tm=512)

    def mla_pre_uq(h, ckv, krp, cq, cosv, sinv, qw, kw, wn, wr):
        c_q_v, c_kv_v, k_rope_v = _mla_pre(ckv, krp, cq, cosv, sinv, qw, kw)
        c_q_b = c_q_v.astype(bf)
        return (c_q_b, c_kv_v, k_rope_v, jnp.dot(c_q_b, wn.astype(bf), preferred_element_type=F32),
                jnp.dot(c_q_b, wr.astype(bf), preferred_element_type=F32))

    c_q, c_kv, k_rope, q_nope, q_rope_pre = _rowwise(
        mla_pre_uq,
        [(pm, KV_LORA, 0, False), (pm, ROPE_PAD, 2, False), (pm, Q_LORA, 2, False),
         (cosb, ROPE_PAD, 0, False), (sinb, ROPE_PAD, 0, False)],
        [sp['q_norm_w'], sp['kv_norm_w'], wt['uq_nope'], wt['uq_rope']],
        [(Q_LORA, Q_LORA, False, bf), (KV_LORA, KV_LORA, False, bf), (ROPE_PAD, ROPE_PAD, False, bf),
         (HEADS * NOPE, HEADS * NOPE, False, bf), (HEADS * ROPE_PAD, HEADS * ROPE_PAD, False, F32)], name="f_mla_pre_uq",
        tm=512)
    out_lat, lse, q_rope, o_mla = _attention(q_nope, q_rope_pre, cosb, sinb, c_kv, k_rope, wt['uk'], wt['uv'])
    y_mla = _mm2(o_mla, wt['br_mla'], name="f_br_mla")

    def merge_o_ln1(h, ggv, yd, ym, xv, wo, gv, bv):
        mixed_v = _merge(ggv, yd, ym).astype(bf)
        ao = jnp.dot(mixed_v, wo.astype(bf), preferred_element_type=F32)
        h1v = _ln1(xv, ao, gv, bv)
        return mixed_v, ao, h1v, h1v

    mixed, attn_out, h1, h1b = _rowwise(
        merge_o_ln1, [(gg, 2 * D_MODEL, 0, False), (y_dn, D_MODEL, 0, False), (y_mla, D_MODEL, 0, False), (x, D_MODEL, 0, False)],
        [wt['o'], sp['ln1_g'], sp['ln1_b']],
        [(D_MODEL, D_MODEL, False, bf), (D_MODEL, D_MODEL, False, F32), (D_MODEL, D_MODEL, False, F32),
         (D_MODEL, D_MODEL, False, bf)], name="f_merge_o_ln1", tm=512)
    act, ffn_gt, ffn_up = _ffn_in_swiglu(h1b, wt['ffn_in'])
    pb = p.astype(bf)

    def final_fn(h, h1v, actv, pv, tgt, gt, up, wfo, wpg, wpl, gv, bv):
        ffnv = jnp.dot(actv.astype(bf), wfo.astype(bf), preferred_element_type=F32)
        gpv = jnp.dot(h1v.astype(bf), wpg.astype(bf), preferred_element_type=F32)
        ppv = jnp.dot(pv.astype(bf), wpl.astype(bf), preferred_element_type=F32)
        y, vjp = jax.vjp(_final, h1v, ffnv, gpv, ppv, gv, bv)
        err = y - tgt
        dh1, dffn, dgp, dpp, dg, db = vjp(err * (1.0 / D_MODEL))
        sq = err * err
        lanes = sq[:, :LANE]
        for j in range(1, D_MODEL // LANE):
            lanes = lanes + sq[:, j * LANE:(j + 1) * LANE]
        loss = jnp.sum(lanes, axis=0, keepdims=True) * (0.5 / D_MODEL)
        dffn_b = dffn.astype(bf)
        dact = _dot(dffn_b, wfo, _NT).astype(bf).astype(F32)
        _, vjp_s = jax.vjp(_swiglu, gt.astype(F32), up.astype(F32))
        dgt, dup = vjp_s(dact)
        return dffn, dffn_b, dgp, dpp, jnp.concatenate([dgt, dup], axis=1), dg, db, loss

    dpre2, dpre2b, dgate_pre, dple_proj, dffn_in, g['ln2_g'], g['ln2_b'], loss_lanes = _rowwise(
        final_fn, [(h1, D_MODEL, 0, False), (act, FFN_HIDDEN, 0, False), (pb, PLE_DIM, 0, False), (target, D_MODEL, 0, False),
                   (ffn_gt, FFN_HIDDEN, 0, False), (ffn_up, FFN_HIDDEN, 0, False)],
        [wt['ffn_out'], wt['ple_gate'], wt['ple'], sp['ln2_g'], sp['ln2_b']],
        [(D_MODEL, D_MODEL, False, F32)] + [(D_MODEL, D_MODEL, False, bf)] * 3 + [(2 * FFN_HIDDEN, 2 * FFN_HIDDEN, False, bf)],
        [(1, D_MODEL), (1, D_MODEL), (1, LANE)], name="b_final")
    g['loss_lanes'] = loss_lanes

    g['ple'] = _mm2(pb, dple_proj, ta=True, name="g_ple")
    g['ple_gate'] = _mm2(h1b, dgate_pre, ta=True, name="g_ple_gate")
    g['ffn_out'] = _mm2(act, dpre2b, ta=True, name="g_ffn_out")
    g['ffn_in'] = _mm2(h1b, dffn_in, ta=True, name="g_ffn_in")
    dh1, _ = _dx_fused([(dffn_in, wt['ffn_in']), (dgate_pre, wt['ple_gate'])], dpre2, ALPHA, name="b_dh1")

    def attn_out_bwd(h, xv, ao, d, ggv, yd, ym, o, zz, wo, wbd, wbm, gv, bv, nw):
        _, vjp = jax.vjp(_ln1, xv, ao, gv, bv)
        _, dao, dg, db = vjp(d)
        dao_b = dao.astype(bf)
        _, vjp_m = jax.vjp(_merge, ggv, yd, ym)
        dggv, dyd, dym = vjp_m(_dot(dao_b, wo, _NT))
        dyd_b, dym_b = dyd.astype(bf), dym.astype(bf)
        _, vjp_n = jax.vjp(_gated_norm_heads, o, zz, nw)
        do_v, dz_v, dnw = vjp_n(_dot(dyd_b, wbd, _NT))
        return dao, dao_b, dggv, dyd_b, dym_b, do_v, dz_v, _dot(dym_b, wbm, _NT), dg, db, dnw

    row_d = lambda a: (a, D_MODEL, 0, False)
    dpre1, dpre1b, dgg, dy_dn, dy_mla, do_dn, dz, do_mla, g['ln1_g'], g['ln1_b'], g['dn_norm_w'] = _rowwise(
        attn_out_bwd, [row_d(x), row_d(attn_out), row_d(dh1), (gg, 2 * D_MODEL, 0, False), row_d(y_dn), row_d(y_mla),
                       row_d(o_dn), row_d(z)],
        [wt['o'], wt['br_dn'], wt['br_mla'], sp['ln1_g'], sp['ln1_b'], sp['dn_norm_w']],
        [(D_MODEL, D_MODEL, False, F32), (D_MODEL, D_MODEL, False, bf), (2 * D_MODEL, 2 * D_MODEL, False, bf),
         (D_MODEL, D_MODEL, False, bf), (D_MODEL, D_MODEL, False, bf), (D_MODEL, D_MODEL, False, F32),
         (D_MODEL, D_MODEL, False, bf), (D_MODEL, D_MODEL, False, bf)],
        [(1, D_MODEL), (1, D_MODEL), (1, LANE)], name="b_attn_out")
    g['o'] = _mm2(mixed, dpre1b, ta=True, name="g_o")
    g['br_dn'] = _mm2(og, dy_dn, ta=True, name="g_br_dn")
    g['br_mla'] = _mm2(o_mla, dy_mla, ta=True, name="g_br_mla")

    g['uv'] = _mm(out_lat, do_mla, name="g_uv", ta=True, heads=HEADS, a_head='lead', b_head='col', out_head='lead',
                  dims=(KV_LORA, NOPE, t))
    dq_lat, dq_rope_pre, dq_nope, dckv_att, dkr_att = _attention_bwd(
        q_nope, q_rope, cosb, sinb, c_kv, k_rope, wt['uk'], wt['uv'], out_lat, lse, do_mla)
    g['uk'] = _mm(dq_lat, q_nope, name="g_uk", ta=True, heads=HEADS, a_head='lead', b_head='col', out_head='lead',
                  dims=(KV_LORA, NOPE, t))
    g['uq_nope'] = _mm2(c_q, dq_nope, ta=True, name="g_uq_nope")
    g['uq_rope'] = _mm2(c_q, dq_rope_pre, ta=True, name="g_uq_rope")
    dc_q = _mm2(dq_nope, wt['uq_nope'], tb=True, name="b_dcq_nope")
    dc_q = _mm2(dq_rope_pre, wt['uq_rope'], tb=True, name="b_dcq_rope", add=dc_q)

    (du, dw, dqd, dkt, dintra, dgl), paired = _delta_scan_bwd(u, w_, qd, kt, intra, gl, sall, do_dn, rider=exch.pair_send(g))
    (dq_a, dk_a, dv_a, dba, g['a_log'], g['dt_bias']), arrived = _delta_local_bwd(
        qkv_act, pm, sp['a_log'], sp['dt_bias'], t_inv, du, dw, dqd, dkt, dintra, dgl, rider=exch.reduce_send(paired))
    exch.reduce_arrived(arrived)
    dqkv_pre, g['conv_w'] = _conv_silu_bwd(qkv_pre, sp['conv_w'], [dq_a, dk_a, dv_a])

    def mla_pre_bwd(h, ckv, cq, cosv, sinv, dcq, dckv, dkr, dba_v, qw, kw):
        _, vjp = jax.vjp(lambda a, c, d, e: (_rms_norm(c, d), _rms_norm(a, e)), ckv, cq, qw, kw)
        dckv_p, dcq_p, dqw, dkw = vjp((dcq, dckv))
        dkr_p = _rope_bwd(dkr, cosv, sinv)
        dpm = jnp.concatenate([dckv_p, dkr_p, dba_v, jnp.zeros((ckv.shape[0], 2 * LANE), F32), dcq_p], axis=1)
        return dpm, dqw, dkw

    dpm, g['q_norm_w'], g['kv_norm_w'] = _rowwise(
        mla_pre_bwd,
        [(pm, KV_LORA, 0, False), (pm, Q_LORA, 2, False), (cosb, ROPE_PAD, 0, False), (sinb, ROPE_PAD, 0, False),
         (dc_q, Q_LORA, 0, False), (dckv_att, KV_LORA, 0, False), (dkr_att, ROPE_PAD, 0, False), (dba, LANE, 0, False)],
        [sp['q_norm_w'], sp['kv_norm_w']], [(1152, 1152, False, bf)], [(1, Q_LORA), (1, KV_LORA)], name="b_mla_pre")

    rider = exch.small_send(g)
    if rider is None:
        g['qkv'] = _mm2(xb, dqkv_pre, ta=True, name="g_qkv")
    else:
        g['qkv'], got = _mm2(xb, dqkv_pre, ta=True, name="g_qkv", rider=rider)
        exch.small_arrived(got)
    g['z'] = _mm2(xb, dz, ta=True, name="g_z")
    g['gg'] = _mm2(xb, dgg, ta=True, name="g_gg")
    g['mla'] = _mm2(xb, dpm, ta=True, name="g_mla")
    dx, arrived = _dx_fused([(dqkv_pre, wt['qkv']), (dz, wt['z']), (dgg, wt['gg']), (dpm, wt['mla'])], dpre1, ALPHA,
                            rider=exch.in_send(g))
    exch.in_arrived(arrived)
    return loss_lanes, dx, g


_IN_SIZES = (QKV_W, HEADS * DN_DK, HEADS, HEADS, Q_LORA, KV_LORA, ROPE, D_MODEL, D_MODEL)


def _rope_tables(positions):
    inv_freq = ROPE_BASE ** (-jnp.arange(0, ROPE, 2, dtype=F32) / ROPE)
    ang = positions.astype(F32)[:, None] * inv_freq
    cos, sin = jnp.cos(ang), jnp.sin(ang)
    zeros = jnp.zeros((positions.shape[0], ROPE_PAD - ROPE), F32)
    return jnp.concatenate([cos, cos, zeros], axis=1), jnp.concatenate([-sin, sin, zeros], axis=1)


def _prep_w_in(w_in):
    dt = w_in.dtype
    offs = [0]
    for s in _IN_SIZES:
        offs.append(offs[-1] + s)
    qkv, z, wb, wa, cq, ckv, kr, gd, gm = [w_in[:, offs[i]:offs[i + 1]] for i in range(len(_IN_SIZES))]
    zc = lambda n: jnp.zeros((D_MODEL, n), dt)
    return {
        'qkv': qkv, 'z': z, 'gg': jnp.concatenate([gd, gm], axis=1),
        'mla': jnp.concatenate([ckv, kr, zc(ROPE_PAD - ROPE), wb, wa, zc(LANE - 2 * HEADS), zc(2 * LANE), cq], axis=1),
    }


def _prep_weights(full):
    w_uq = full['w_uq']
    wt = {
        'uq_nope': w_uq[:, :, :NOPE].reshape(Q_LORA, HEADS * NOPE),
        'uq_rope': jnp.pad(w_uq[:, :, NOPE:], ((0, 0), (0, 0), (0, ROPE_PAD - ROPE))).reshape(Q_LORA, HEADS * ROPE_PAD),
        'uk': jnp.transpose(full['w_uk'], (1, 0, 2)), 'uv': jnp.transpose(full['w_uv'], (1, 0, 2)),
        'br_dn': full['w_br_dn'], 'br_mla': full['w_br_mla'], 'o': full['w_o'], 'ffn_in': full['w_ffn_in'],
        'ffn_out': full['w_ffn_out'], 'ple': full['w_ple'], 'ple_gate': full['w_ple_gate'],
    }
    return wt


def _prep_small(small):
    pad = lambda v: jnp.pad(v, (0, LANE - v.shape[0]))[None, :]
    return {
        'conv_w': small['conv_w'], 'a_log': pad(small['dn_a_log']), 'dt_bias': pad(small['dn_dt_bias']),
        'dn_norm_w': small['dn_norm_w'][None, :], 'q_norm_w': small['q_norm_w'][None, :],
        'kv_norm_w': small['kv_norm_w'][None, :], 'ln1_g': small['ln1_g'][None, :], 'ln1_b': small['ln1_b'][None, :],
        'ln2_g': small['ln2_g'][None, :], 'ln2_b': small['ln2_b'][None, :],
    }


def _w_in_grad(g):
    mla = g['mla']
    ba0 = KV_LORA + ROPE_PAD
    cq0 = ba0 + 3 * LANE
    return jnp.concatenate([
        g['qkv'], g['z'], mla[:, ba0:ba0 + HEADS], mla[:, ba0 + HEADS:ba0 + 2 * HEADS], mla[:, cq0:cq0 + Q_LORA],
        mla[:, :KV_LORA], mla[:, KV_LORA:KV_LORA + ROPE], g['gg']], axis=1)


def _unprep_grads_late(g):
    return {
        'conv_w': g['conv_w'], 'dn_a_log': g['a_log'][0, :HEADS], 'dn_dt_bias': g['dt_bias'][0, :HEADS],
        'dn_norm_w': g['dn_norm_w'][0], 'q_norm_w': g['q_norm_w'][0], 'kv_norm_w': g['kv_norm_w'][0],
        'ln1_g': g['ln1_g'][0], 'ln1_b': g['ln1_b'][0], 'ln2_g': g['ln2_g'][0], 'ln2_b': g['ln2_b'][0],
    }


def _unprep_grads_early(g):
    w_uq = jnp.concatenate([g['uq_nope'].reshape(Q_LORA, HEADS, NOPE),
                            g['uq_rope'].reshape(Q_LORA, HEADS, ROPE_PAD)[:, :, :ROPE]], axis=2)
    return {
        'w_uq': w_uq, 'w_uk': jnp.transpose(g['uk'], (1, 0, 2)), 'w_uv': jnp.transpose(g['uv'], (1, 0, 2)),
        'w_br_dn': g['br_dn'], 'w_br_mla': g['br_mla'], 'w_o': g['o'],
        'w_ffn_in': g['ffn_in'], 'w_ffn_out': g['ffn_out'], 'w_ple': g['ple'], 'w_ple_gate': g['ple_gate'],
    }


_FLATB_PIECES = (
    ('w_ffn_out', 704, (704, D_MODEL)), ('w_br_dn', 256, (256, D_MODEL)), ('w_br_mla', 256, (256, D_MODEL)),
    ('w_o', 256, (256, D_MODEL)), ('w_ple_gate', 256, (256, D_MODEL)), ('w_uq', 144, (96, HEADS, NOPE + ROPE)),
    ('w_uk', 64, (64, HEADS, NOPE)), ('w_uv', 64, (64, HEADS, NOPE)), ('w_ple', 64, (PLE_DIM, 256)),
)
FLATB_ROWS = 2112
W_IN_SHARD = D_IN // N_SHARD
FFN_IN_SHARD = 2 * FFN_HIDDEN // N_SHARD
A_ROWS = D_MODEL + 32
_CONV_SHARD = QKV_W // N_SHARD
_ADD_TILES = (256, 256, 352)


def _flatb_offsets():
    offs, o = {}, 0
    for name, rows, _ in _FLATB_PIECES:
        offs[name] = o
        o += rows
    return offs, o


def _pack_shards(ws, conv_w):
    conv_bits = lax.bitcast_convert_type(conv_w, jnp.bfloat16).reshape(DN_CONV, 2 * _CONV_SHARD).astype(_CDT)
    tail = jnp.pad(conv_bits, ((0, A_ROWS - D_MODEL - DN_CONV), (0, W_IN_SHARD - 2 * _CONV_SHARD)))
    a_buf = jnp.concatenate([ws['w_in'].astype(_CDT), tail], axis=0)
    parts = [ws[name].astype(_CDT).reshape(rows, FLAT_W) for name, rows, _ in _FLATB_PIECES]
    used = sum(p.shape[0] for p in parts)
    parts.append(jnp.zeros((FLATB_ROWS - used, FLAT_W), _CDT))
    return [a_buf, ws['w_ffn_in'].astype(_CDT), jnp.concatenate(parts, axis=0)]


def _unpack_w_in(gathered, local, me):
    a = [jnp.where(me == s, local, gathered[s]) for s in range(N_SHARD)]
    conv = [lax.bitcast_convert_type(
        p[D_MODEL:D_MODEL + DN_CONV, :2 * _CONV_SHARD].astype(jnp.bfloat16).reshape(DN_CONV, _CONV_SHARD, 2), F32) for p in a]
    return jnp.concatenate([p[:D_MODEL] for p in a], axis=1), jnp.concatenate(conv, axis=1)


def _unpack_rest(gathered, local, me):
    pick = lambda b, s: jnp.where(me == s, local[b], gathered[b][s])
    full = {'w_ffn_in': jnp.concatenate([pick(0, s) for s in range(N_SHARD)], axis=1)}
    offs, _ = _flatb_offsets()
    fb = [pick(1, s) for s in range(N_SHARD)]
    for name, rows, shape in _FLATB_PIECES:
        pieces = [p[offs[name]:offs[name] + rows].reshape(shape) for p in fb]
        full[name] = jnp.concatenate(pieces, axis=1 if name == 'w_ple' else 0)
    return full


def _shard_columns(g, w):
    return jnp.stack([g[:, s * w:(s + 1) * w] for s in range(N_SHARD)])


def _pack_grads_rest(gw):
    parts = []
    for name, rows, _ in _FLATB_PIECES:
        g = gw[name]
        if name == 'w_ple':
            parts.append(_shard_columns(g, PLE_DIM).reshape(N_SHARD, rows, FLAT_W))
        else:
            parts.append(g.reshape(N_SHARD, rows, FLAT_W))
    used = sum(p.shape[1] for p in parts)
    parts.append(jnp.zeros((N_SHARD, FLATB_ROWS - used, FLAT_W), F32))
    return [_shard_columns(gw['w_ffn_in'], FFN_IN_SHARD), jnp.concatenate(parts, axis=1)]


def _unpack_reduced(mine, theirs, c):
    whole = [jnp.concatenate([jnp.where(c == 0, m, t), jnp.where(c == 0, t, m)], axis=0) for m, t in zip(mine, theirs)]
    out = {'w_in': whole[0], 'w_ffn_in': whole[1]}
    offs, _ = _flatb_offsets()
    for name, rows, shape in _FLATB_PIECES:
        out[name] = whole[2][offs[name]:offs[name] + rows].reshape(shape)
    return out


_HBM = pl.BlockSpec(memory_space=pltpu.HBM)


def _place():
    x, y, c = lax.axis_index("x"), lax.axis_index("y"), lax.axis_index("c")
    chips = [(1 - x, y), (x, 1 - y), (1 - x, 1 - y)]
    return x, y, c, chips


def _remote(src, dst, send_sems, recv_sems, k, to):
    return pltpu.make_async_remote_copy(src_ref=src, dst_ref=dst, send_sem=send_sems.at[k], recv_sem=recv_sems.at[k],
                                        device_id=to, device_id_type=_MESH)


def _half_rows(ref, half, hf, lead=None):
    rows = pl.ds(pl.multiple_of(hf * half, 16), half)
    return ref.at[rows, :] if lead is None else ref.at[lead, rows, :]


class _Rider:
    def __init__(self, inputs, out_shape, n_sems, copies, aliases=None):
        self.inputs, self.out_shape, self.n_sems, self.copies = list(inputs), list(out_shape), n_sems, copies
        self.aliases = aliases or {}


def _carried_call(body, rider, first, last, *, name, grid, in_specs, out_specs, out_shape, scratch_shapes, sem, args):
    n_in, n_out, n_scr = len(in_specs), len(out_specs), len(scratch_shapes)
    if rider is None:
        res = pl.pallas_call(body, name=name, grid=grid, in_specs=in_specs, out_specs=out_specs, out_shape=out_shape,
                             scratch_shapes=scratch_shapes, compiler_params=_cparams(sem))(*args)
        return list(res), []
    ri, ro = len(rider.inputs), len(rider.out_shape)

    def full_body(*refs):
        own_in, r_in = refs[:n_in], refs[n_in:n_in + ri]
        o0 = n_in + ri
        own_out, r_out = refs[o0:o0 + n_out], refs[o0 + n_out:o0 + n_out + ro]
        s0 = o0 + n_out + ro
        own_scr, send_sems, recv_sems = refs[s0:s0 + n_scr], refs[s0 + n_scr], refs[s0 + n_scr + 1]

        @pl.when(first())
        def _():
            sends, _ = rider.copies(r_in, r_out, send_sems, recv_sems)
            for cp in sends:
                cp.start()

        body(*own_in, *own_out, *own_scr)

        @pl.when(last())
        def _():
            sends, arrivals = rider.copies(r_in, r_out, send_sems, recv_sems)
            for cp in arrivals():
                cp.wait_recv()
            for cp in sends:
                cp.wait_send()

    res = pl.pallas_call(
        full_body, name=name, grid=grid, in_specs=list(in_specs) + [_HBM] * ri, out_specs=list(out_specs) + [_HBM] * ro,
        out_shape=list(out_shape) + rider.out_shape,
        scratch_shapes=list(scratch_shapes) + [pltpu.SemaphoreType.DMA((rider.n_sems,))] * 2,
        input_output_aliases={n_in + i: n_out + o for i, o in rider.aliases.items()},
        compiler_params=_cparams(sem))(*args, *rider.inputs)
    return list(res[:n_out]), list(res[n_out:])


def _ride_gather_send(bufs):
    n = len(bufs)
    halves = [b.shape[0] // 2 for b in bufs]

    def copies(ins, outs, send_sems, recv_sems):
        x, y, c, chips = _place()
        slot = lambda b, cx, cy: _half_rows(outs[b], halves[b], c, lead=2 * cx + cy)
        sends = [_remote(_half_rows(ins[b], halves[b], c), slot(b, x, y), send_sems, recv_sems, 3 * b + j, (cx, cy, c))
                 for b in range(n) for j, (cx, cy) in enumerate(chips)]
        arrivals = lambda: [_remote(slot(b, cx, cy), slot(b, cx, cy), send_sems, recv_sems, 3 * b + j, (x, y, c))
                            for b in range(n) for j, (cx, cy) in enumerate(chips)]
        return sends, arrivals

    return _Rider(bufs, [jax.ShapeDtypeStruct((N_SHARD,) + b.shape, b.dtype) for b in bufs], 3 * n, copies)


def _ride_gather_pass(gathered):
    n = len(gathered)
    halves = [g.shape[1] // 2 for g in gathered]

    def copies(ins, outs, send_sems, recv_sems):
        x, y, c, chips = _place()
        slot = lambda b, cx, cy, hf: _half_rows(outs[b], halves[b], hf, lead=2 * cx + cy)
        sends = [_remote(slot(b, cx, cy, c), slot(b, cx, cy, c), send_sems, recv_sems, 3 * b + j, (x, y, 1 - c))
                 for b in range(n) for j, (cx, cy) in enumerate(chips)]
        arrivals = lambda: [_remote(slot(b, cx, cy, 1 - c), slot(b, cx, cy, 1 - c), send_sems, recv_sems, 3 * b + j, (x, y, c))
                            for b in range(n) for j, (cx, cy) in enumerate(chips)]
        return sends, arrivals

    return _Rider(gathered, [jax.ShapeDtypeStruct(g.shape, g.dtype) for g in gathered], 3 * n, copies,
                  aliases={b: b for b in range(n)})


def _ride_small_gather(buf):
    def copies(ins, outs, send_sems, recv_sems):
        x, y, c, _ = _place()
        flip = lambda v, d: 1 - v if d else v
        sends, peers = [], []
        for dx in (0, 1):
            for dy in (0, 1):
                for dc in (0, 1):
                    if dx or dy or dc:
                        k = 4 * dx + 2 * dy + dc - 1
                        px, py, pc = flip(x, dx), flip(y, dy), flip(c, dc)
                        sends.append(_remote(ins[0], outs[0].at[4 * x + 2 * y + c], send_sems, recv_sems, k, (px, py, pc)))
                        peers.append((k, 4 * px + 2 * py + pc))
        arrivals = lambda: [_remote(ins[0], outs[0].at[slot], send_sems, recv_sems, k, (x, y, c)) for k, slot in peers]
        return sends, arrivals

    return _Rider([buf], [jax.ShapeDtypeStruct((8,) + buf.shape, buf.dtype)], 7, copies)


def _small_sum(gathered, buf, me_arr):
    n, r, width = gathered.shape

    def body(me_ref, g_ref, b_ref, o_ref):
        total = jnp.zeros((r, width), F32)
        for d in range(n):
            total = total + jnp.where(me_ref[0] == d, b_ref[...], g_ref[d])
        o_ref[...] = total

    return pl.pallas_call(
        body, name="small_sum", out_shape=jax.ShapeDtypeStruct((r, width), F32),
        grid_spec=pltpu.PrefetchScalarGridSpec(
            num_scalar_prefetch=1, grid=(1,),
            in_specs=[pl.BlockSpec((n, r, width), lambda i, me: (0, 0, 0)), pl.BlockSpec((r, width), lambda i, me: (0, 0))],
            out_specs=pl.BlockSpec((r, width), lambda i, me: (0, 0))),
        compiler_params=_cparams(("arbitrary",)))(me_arr, gathered, buf)


def _ride_pair_exchange(gbufs):
    n = len(gbufs)
    halves = [g.shape[1] // 2 for g in gbufs]

    def copies(ins, outs, send_sems, recv_sems):
        x, y, c, _ = _place()
        sends = [_remote(ins[b].at[:, pl.ds(pl.multiple_of((1 - c) * halves[b], 16), halves[b]), :], outs[b],
                         send_sems, recv_sems, b, (x, y, 1 - c)) for b in range(n)]
        arrivals = lambda: [_remote(outs[b], outs[b], send_sems, recv_sems, b, (x, y, c)) for b in range(n)]
        return sends, arrivals

    return _Rider(gbufs, [jax.ShapeDtypeStruct((N_SHARD, h, g.shape[2]), g.dtype) for g, h in zip(gbufs, halves)], n, copies)


def _ride_chip_exchange(parts):
    n = len(parts)

    def copies(ins, outs, send_sems, recv_sems):
        x, y, c, chips = _place()
        sends = [_remote(ins[b].at[2 * cx + cy], outs[b].at[j], send_sems, recv_sems, 3 * b + j, (cx, cy, c))
                 for b in range(n) for j, (cx, cy) in enumerate(chips)]
        arrivals = lambda: [_remote(ins[b].at[0], outs[b].at[j], send_sems, recv_sems, 3 * b + j, (x, y, c))
                            for b in range(n) for j in range(len(chips))]
        return sends, arrivals

    return _Rider(parts, [jax.ShapeDtypeStruct((3,) + p.shape[1:], p.dtype) for p in parts], 3 * n, copies)


def _gather_shards(bufs, name):
    n = len(bufs)
    halves = [b.shape[0] // 2 for b in bufs]

    def body(*refs):
        ins, outs, send_sems, recv_sems = refs[:n], refs[n:2 * n], refs[2 * n], refs[2 * n + 1]
        x, y, c, chips = _place()
        me, sibling = (x, y, c), (x, y, 1 - c)
        slot = lambda b, cx, cy, hf: _half_rows(outs[b], halves[b], hf, lead=2 * cx + cy)
        first = [_remote(_half_rows(ins[b], halves[b], c), slot(b, x, y, c), send_sems, recv_sems, 6 * b + j, (cx, cy, c))
                 for b in range(n) for j, (cx, cy) in enumerate(chips)]
        for cp in first:
            cp.start()
        passed = []
        for j, (cx, cy) in enumerate(chips):
            for b in range(n):
                _remote(slot(b, cx, cy, c), slot(b, cx, cy, c), send_sems, recv_sems, 6 * b + j, me).wait_recv()
                fwd = _remote(slot(b, cx, cy, c), slot(b, cx, cy, c), send_sems, recv_sems, 6 * b + 3 + j, sibling)
                fwd.start()
                passed.append(fwd)
        for j, (cx, cy) in enumerate(chips):
            for b in range(n):
                _remote(slot(b, cx, cy, 1 - c), slot(b, cx, cy, 1 - c), send_sems, recv_sems, 6 * b + 3 + j, me).wait_recv()
        for cp in first + passed:
            cp.wait_send()

    return pl.pallas_call(
        body, name=name, out_shape=[jax.ShapeDtypeStruct((N_SHARD,) + b.shape, b.dtype) for b in bufs],
        in_specs=[_HBM] * n, out_specs=[_HBM] * n,
        scratch_shapes=[pltpu.SemaphoreType.DMA((6 * n,)), pltpu.SemaphoreType.DMA((6 * n,))],
    )(*bufs)


def _reduce_pair_exchange(gbufs, name):
    n = len(gbufs)
    halves = [g.shape[1] // 2 for g in gbufs]

    def body(*refs):
        ins, outs, send_sems, recv_sems = refs[:n], refs[n:2 * n], refs[2 * n], refs[2 * n + 1]
        x, y, c, _ = _place()
        cps = [_remote(ins[b].at[:, pl.ds(pl.multiple_of((1 - c) * halves[b], 16), halves[b]), :], outs[b],
                       send_sems, recv_sems, b, (x, y, 1 - c)) for b in range(n)]
        for cp in cps:
            cp.start()
        for cp in cps:
            cp.wait()

    return pl.pallas_call(
        body, name=name,
        out_shape=[jax.ShapeDtypeStruct((N_SHARD, h, g.shape[2]), g.dtype) for g, h in zip(gbufs, halves)],
        in_specs=[_HBM] * n, out_specs=[_HBM] * n,
        scratch_shapes=[pltpu.SemaphoreType.DMA((n,)), pltpu.SemaphoreType.DMA((n,))],
    )(*gbufs)


def _pair_add(gbuf, recv, c_arr, tr, name):
    _, rows, width = gbuf.shape
    half = rows // 2
    nt = half // tr

    def body(c_ref, a_ref, b_ref, o_ref):
        o_ref[...] = (a_ref[...] + b_ref[...]).astype(o_ref.dtype)

    blk = lambda f: pl.BlockSpec((None, tr, width), f)
    return pl.pallas_call(
        body, name=name, out_shape=jax.ShapeDtypeStruct((N_SHARD, half, width), jnp.bfloat16),
        grid_spec=pltpu.PrefetchScalarGridSpec(
            num_scalar_prefetch=1, grid=(N_SHARD, nt),
            in_specs=[blk(lambda s, i, c: (s, c[0] * nt + i, 0)), blk(lambda s, i, c: (s, i, 0))],
            out_specs=blk(lambda s, i, c: (s, i, 0))),
        compiler_params=_cparams(("parallel", "parallel")))(c_arr, gbuf, recv)


def _chip_add(part, recv, me_arr, tr, name):
    _, half, width = part.shape

    def body(me_ref, own, a0, a1, a2, o_ref):
        f = lambda r: r[...].astype(F32)
        o_ref[...] = ((f(own) + f(a0)) + f(a1)) + f(a2)

    specs = [pl.BlockSpec((None, tr, width), lambda i, me: (me[0], i, 0))]
    specs += [pl.BlockSpec((None, tr, width), functools.partial(lambda i, me, k: (k, i, 0), k=k)) for k in range(3)]
    return pl.pallas_call(
        body, name=name, out_shape=jax.ShapeDtypeStruct((half, width), F32),
        grid_spec=pltpu.PrefetchScalarGridSpec(
            num_scalar_prefetch=1, grid=(half // tr,), in_specs=specs,
            out_specs=pl.BlockSpec((tr, width), lambda i, me: (i, 0))),
        compiler_params=_cparams(("parallel",)))(me_arr, part, recv, recv, recv)


def _reduce_pair_share(rhalves, name):
    n = len(rhalves)

    def body(*refs):
        ins, outs, send_sems, recv_sems = refs[:n], refs[n:2 * n], refs[2 * n], refs[2 * n + 1]
        x, y, c, _ = _place()
        cps = [_remote(ins[b], outs[b], send_sems, recv_sems, b, (x, y, 1 - c)) for b in range(n)]
        for cp in cps:
            cp.start()
        for cp in cps:
            cp.wait()

    return pl.pallas_call(
        body, name=name, out_shape=[jax.ShapeDtypeStruct(r.shape, r.dtype) for r in rhalves],
        in_specs=[_HBM] * n, out_specs=[_HBM] * n,
        scratch_shapes=[pltpu.SemaphoreType.DMA((n,)), pltpu.SemaphoreType.DMA((n,))],
    )(*rhalves)


def _row_tile(rows, cap):
    if rows <= cap:
        return rows
    t = (cap // 8) * 8
    while t >= 8:
        if rows % t == 0:
            return t
        t -= 8
    return rows


def _adamw(w, g, m, v, name):
    shape = w.shape
    cols = shape[-1] if len(shape) <= 3 else shape[-2] * shape[-1]
    lead = len(shape) == 3
    w2, g2, m2, v2 = (a if lead else a.reshape(-1, cols) for a in (w, g, m, v))
    rows = shape[1] if lead else w2.shape[0]
    tr, tc = _row_tile(rows, 256), cols
    if tr == rows and rows > 256:
        tc = _tile(cols, 256)

    def body(w_ref, g_ref, m_ref, v_ref, d_ref, mo_ref, vo_ref):
        gv = g_ref[...]
        mn = ADAM_B1 * m_ref[...] + (1.0 - ADAM_B1) * gv
        vn = ADAM_B2 * v_ref[...] + (1.0 - ADAM_B2) * (gv * gv)
        m_hat = mn / (1.0 - ADAM_B1 ** ADAM_STEP)
        v_hat = vn / (1.0 - ADAM_B2 ** ADAM_STEP)
        d_ref[...] = -ADAM_LR * (m_hat / (jnp.sqrt(v_hat) + ADAM_EPS) + ADAM_WD * w_ref[...])
        mo_ref[...] = mn
        vo_ref[...] = vn

    blk = (pl.BlockSpec((None, tr, tc), lambda i, j: (0, i, j)) if lead else pl.BlockSpec((tr, tc), lambda i, j: (i, j)))
    outs = pl.pallas_call(
        body, name=name, grid=(rows // tr, cols // tc), in_specs=[blk] * 4, out_specs=[blk] * 3,
        out_shape=[jax.ShapeDtypeStruct(w2.shape, F32)] * 3,
        compiler_params=_cparams(("parallel", "parallel")))(w2, g2, m2, v2)
    return tuple(o.reshape(shape) for o in outs)


_WEIGHT_NAMES = ('w_in', 'conv_w', 'dn_a_log', 'dn_dt_bias', 'dn_norm_w', 'q_norm_w', 'w_uq', 'kv_norm_w', 'w_uk',
                 'w_uv', 'w_br_dn', 'w_br_mla', 'w_o', 'ln1_g', 'ln1_b', 'w_ffn_in', 'w_ffn_out', 'w_ple',
                 'w_ple_gate', 'ln2_g', 'ln2_b')
_SMALL_NAMES = ('ln1_g', 'ln1_b', 'ln2_g', 'ln2_b', 'q_norm_w', 'kv_norm_w', 'dn_norm_w', 'dn_a_log', 'dn_dt_bias')
_SMALL_GROUP = 8
_CONV_SMALL_ROW = len(_SMALL_NAMES) * _SMALL_GROUP
_CONV_SMALL_ROWS = DN_CONV * QKV_W // FLAT_W


_LOSS_SMALL_ROW = _CONV_SMALL_ROW + 16


def _pack_small(gw, loss_lanes):
    rows = [jnp.pad(gw[n][None, :], ((0, _SMALL_GROUP - 1), (0, FLAT_W - gw[n].shape[0]))) for n in _SMALL_NAMES]
    rows.append(jnp.pad(gw['conv_w'].reshape(_CONV_SMALL_ROWS, FLAT_W), ((0, 16 - _CONV_SMALL_ROWS), (0, 0))))
    rows.append(jnp.pad(loss_lanes, ((0, _SMALL_GROUP - 1), (0, FLAT_W - LANE))))
    return jnp.concatenate(rows, axis=0)


class _Exchange:
    def __init__(self, local, me_chip, c_arr):
        self.local, self.me_chip, self.c_arr = local, me_chip, c_arr
        self.parts = self.arrived = None

    def gather_send(self):
        return _ride_gather_send(self.local)

    def gather_pass(self, sent):
        return _ride_gather_pass(sent)

    def weights(self, gathered):
        return _prep_weights(_unpack_rest(gathered, self.local, self.me_chip))

    def pair_send(self, g):
        self.gbufs = _pack_grads_rest(_unprep_grads_early(g))
        return _ride_pair_exchange(self.gbufs)

    def reduce_send(self, got):
        self.parts = [_pair_add(g_, r_, self.c_arr, tr, "pair_add_%d" % (i + 1))
                      for i, (g_, r_, tr) in enumerate(zip(self.gbufs, got, _ADD_TILES[1:]))]
        return _ride_chip_exchange(self.parts)

    def reduce_arrived(self, arrived):
        self.arrived = list(arrived)

    def in_send(self, g):
        g_in = [_shard_columns(_w_in_grad(g), W_IN_SHARD)]
        got = _reduce_pair_exchange(g_in, "reduce_pair_exchange_w_in")
        self.part_in = _pair_add(g_in[0], got[0], self.c_arr, _ADD_TILES[0], "pair_add_0")
        return _ride_chip_exchange([self.part_in])

    def in_arrived(self, arrived):
        self.arrived_in = list(arrived)

    def small_send(self, g):
        self.small = _pack_small(_unprep_grads_late(g), g['loss_lanes'])
        return _ride_small_gather(self.small)

    def small_arrived(self, arrived):
        self.small_gathered = arrived[0]


def kernel(x, p, positions, w_in, conv_w, dn_a_log, dn_dt_bias, dn_norm_w, q_norm_w, w_uq, kv_norm_w, w_uk, w_uv, w_br_dn, w_br_mla, w_o, ln1_g, ln1_b, w_ffn_in, w_ffn_out, w_ple, w_ple_gate, ln2_g, ln2_b, loss_target, m_w_in, m_conv_w, m_dn_a_log, m_dn_dt_bias, m_dn_norm_w, m_q_norm_w, m_w_uq, m_kv_norm_w, m_w_uk, m_w_uv, m_w_br_dn, m_w_br_mla, m_w_o, m_ln1_g, m_ln1_b, m_w_ffn_in, m_w_ffn_out, m_w_ple, m_w_ple_gate, m_ln2_g, m_ln2_b, v_w_in, v_conv_w, v_dn_a_log, v_dn_dt_bias, v_dn_norm_w, v_q_norm_w, v_w_uq, v_kv_norm_w, v_w_uk, v_w_uv, v_w_br_dn, v_w_br_mla, v_w_o, v_ln1_g, v_ln1_b, v_w_ffn_in, v_w_ffn_out, v_w_ple, v_w_ple_gate, v_ln2_g, v_ln2_b):
    ws = dict(w_in=w_in, conv_w=conv_w, dn_a_log=dn_a_log, dn_dt_bias=dn_dt_bias, dn_norm_w=dn_norm_w, q_norm_w=q_norm_w,
              w_uq=w_uq, kv_norm_w=kv_norm_w, w_uk=w_uk, w_uv=w_uv, w_br_dn=w_br_dn, w_br_mla=w_br_mla, w_o=w_o,
              ln1_g=ln1_g, ln1_b=ln1_b, w_ffn_in=w_ffn_in, w_ffn_out=w_ffn_out, w_ple=w_ple, w_ple_gate=w_ple_gate,
              ln2_g=ln2_g, ln2_b=ln2_b)
    ms = dict(w_in=m_w_in, conv_w=m_conv_w, dn_a_log=m_dn_a_log, dn_dt_bias=m_dn_dt_bias, dn_norm_w=m_dn_norm_w,
              q_norm_w=m_q_norm_w, w_uq=m_w_uq, kv_norm_w=m_kv_norm_w, w_uk=m_w_uk, w_uv=m_w_uv, w_br_dn=m_w_br_dn,
              w_br_mla=m_w_br_mla, w_o=m_w_o, ln1_g=m_ln1_g, ln1_b=m_ln1_b, w_ffn_in=m_w_ffn_in, w_ffn_out=m_w_ffn_out,
              w_ple=m_w_ple, w_ple_gate=m_w_ple_gate, ln2_g=m_ln2_g, ln2_b=m_ln2_b)
    vs = dict(w_in=v_w_in, conv_w=v_conv_w, dn_a_log=v_dn_a_log, dn_dt_bias=v_dn_dt_bias, dn_norm_w=v_dn_norm_w,
              q_norm_w=v_q_norm_w, w_uq=v_w_uq, kv_norm_w=v_kv_norm_w, w_uk=v_w_uk, w_uv=v_w_uv, w_br_dn=v_w_br_dn,
              w_br_mla=v_w_br_mla, w_o=v_w_o, ln1_g=v_ln1_g, ln1_b=v_ln1_b, w_ffn_in=v_w_ffn_in, w_ffn_out=v_w_ffn_out,
              w_ple=v_w_ple, w_ple_gate=v_w_ple_gate, ln2_g=v_ln2_g, ln2_b=v_ln2_b)
    mx, my, mc = lax.axis_index("x"), lax.axis_index("y"), lax.axis_index("c")

    me_chip = 2 * mx + my
    c_arr = jnp.reshape(mc, (1,)).astype(jnp.int32)
    me_arr = jnp.reshape(me_chip, (1,)).astype(jnp.int32)
    sharded = ('w_in', 'w_ffn_in') + tuple(name for name, _, _ in _FLATB_PIECES)

    local = _pack_shards({name: ws[name][0] for name in sharded}, conv_w[0])
    (gathered_in,) = _gather_shards(local[:1], "gather_w_in")
    w_in_full, conv_full = _unpack_w_in(gathered_in, local[0], me_chip)
    small = {n: ws[n][0] for n in _SMALL_NAMES}
    small['conv_w'] = conv_full
    sp = _prep_small(small)
    cosb, sinb = _rope_tables(positions[0])
    exch = _Exchange(local[1:], me_chip, c_arr)

    loss_lanes, dx, g = _local_step(x[0], p[0, 0], cosb, sinb, loss_target[0], _prep_w_in(w_in_full), sp, exch)

    parts = [exch.part_in] + exch.parts
    arrived = exch.arrived_in + exch.arrived
    mine = [_chip_add(p_, r_, me_arr, tr, "chip_add_%d" % i) for i, (p_, r_, tr) in enumerate(zip(parts, arrived, _ADD_TILES))]
    reduced = _unpack_reduced(mine, _reduce_pair_share(mine, "reduce_pair_share"), mc)
    tot = _small_sum(exch.small_gathered, exch.small, jnp.reshape(4 * mx + 2 * my + mc, (1,)).astype(jnp.int32))
    loss = jnp.sum(tot[_LOSS_SMALL_ROW, :LANE])
    gred = {name: reduced[name][None] for name in sharded}
    for i, n in enumerate(_SMALL_NAMES):
        gred[n] = tot[i * _SMALL_GROUP, :ws[n].shape[1]][None]
    conv_tot = tot[_CONV_SMALL_ROW:_CONV_SMALL_ROW + _CONV_SMALL_ROWS].reshape(DN_CONV, QKV_W)
    gred['conv_w'] = lax.dynamic_slice_in_dim(conv_tot, (2 * mx + my) * _CONV_SHARD, _CONV_SHARD, axis=1)[None]

    deltas, new_m, new_v = {}, {}, {}
    for n in _WEIGHT_NAMES:
        if n == 'w_in':
            tr_ = lambda a: jnp.transpose(a, (0, 2, 1))
            g_t = tr_(gred[n].reshape(ws[n].shape))
            outs = _adamw(tr_(ws[n]), g_t, tr_(ms[n]), tr_(vs[n]), "adamw_" + n)
            gred[n] = tr_(g_t)
            deltas[n], new_m[n], new_v[n] = (tr_(o) for o in outs)
            continue
        gred[n] = gred[n].reshape(ws[n].shape)
        deltas[n], new_m[n], new_v[n] = _adamw(ws[n], gred[n], ms[n], vs[n], "adamw_" + n)
    return (loss, dx[None], *[gred[n] for n in _WEIGHT_NAMES], *[deltas[n] for n in _WEIGHT_NAMES],
            *[new_m[n] for n in _WEIGHT_NAMES], *[new_v[n] for n in _WEIGHT_NAMES])
```

```python
import functools

import jax
import jax.numpy as jnp
from jax import lax
from jax.experimental import pallas as pl
from jax.experimental.pallas import tpu as pltpu

F32 = jnp.float32
_CDT = jnp.bfloat16
_MESH = pl.DeviceIdType.MESH

D_MODEL = 1024
PLE_DIM = 256
HEADS = 8
DN_DK = 128
DN_CHUNK = 64
DN_CONV = 4
QKV_W = 3 * HEADS * DN_DK
Q_LORA = 384
KV_LORA = 256
NOPE = 128
ROPE = 64
ROPE_PAD = 128
FFN_HIDDEN = 2816
D_IN = 6864
ROPE_BASE = 10000.0
ALPHA = 2.0 ** 0.25
ATT_SCALE = (NOPE + ROPE) ** -0.5
NEG_BIG = -1e30
ADAM_LR, ADAM_B1, ADAM_B2, ADAM_EPS, ADAM_WD, ADAM_STEP = 0.001, 0.9, 0.999, 1e-08, 0.01, 10

LANE = 128
VMEM_LIMIT = 56 * 1024 * 1024
MM_VMEM_BUDGET = 40 * 1024 * 1024
N_SHARD = 4
FLAT_W = 1024
SMALL_ROWS = 96


def _tile(dim, cap):
    if dim <= cap:
        return dim
    t = (cap // LANE) * LANE
    while t >= LANE:
        if dim % t == 0:
            return t
        t -= LANE
    return dim


def _cparams(sem):
    return pltpu.CompilerParams(dimension_semantics=sem, vmem_limit_bytes=VMEM_LIMIT)


def _mm(a, b, *, name, ta=False, tb=False, add=None, add_scale=1.0, out_dtype=F32, heads=None,
        a_head=None, b_head=None, out_head=None, dims=None, tm=1408, tn=1408, rider=None):
    m, n, k = dims
    tm, tn = _tile(m, tm), _tile(n, tn)
    sa, sb, so = a.dtype.itemsize, b.dtype.itemsize, jnp.dtype(out_dtype).itemsize

    def vmem_need(tk_):
        acc = tm * tn * 4 if tk_ < k else 0
        extra = 2 * tm * tn * 4 if add is not None else 0
        return 2 * (tm * tk_ * sa + tk_ * tn * sb) + 2 * tm * tn * so + acc + extra

    tk = k
    while vmem_need(tk) > MM_VMEM_BUDGET and tk > LANE:
        smaller = _tile(k, tk - LANE)
        if smaller >= tk:
            break
        tk = smaller
    nk = k // tk
    hgrid = () if heads is None else (heads,)
    off = len(hgrid)

    def spec(rows, cols, rtile, ctile, rsel, csel, layout):
        def idx(*g):
            h = g[0] if off else 0
            ri, ci = g[off + rsel], g[off + csel]
            if layout == 'lead':
                return (h, ri, ci)
            if layout == 'col':
                return (ri, h * (cols // ctile) + ci)
            return (ri, ci)
        if layout == 'lead':
            return pl.BlockSpec((None, rtile, ctile), idx)
        return pl.BlockSpec((rtile, ctile), idx)

    a_spec = spec(k, m, tk, tm, 2, 0, a_head) if ta else spec(m, k, tm, tk, 0, 2, a_head)
    b_spec = spec(n, k, tn, tk, 1, 2, b_head) if tb else spec(k, n, tk, tn, 2, 1, b_head)
    o_spec = spec(m, n, tm, tn, 0, 1, out_head)
    in_specs = [a_spec, b_spec]
    args = [a, b]
    if add is not None:
        in_specs.append(spec(m, n, tm, tn, 0, 1, out_head))
        args.append(add)
    dn = (((0 if ta else 1,), (1 if tb else 0,)), ((), ()))

    def body(*refs):
        a_ref, b_ref = refs[0], refs[1]
        prod = lax.dot_general(a_ref[...].astype(_CDT), b_ref[...].astype(_CDT), dn, preferred_element_type=F32)
        if nk == 1:
            o_ref = refs[-1]
            if add is not None:
                prod = prod + refs[2][...].astype(F32) * add_scale
            o_ref[...] = prod.astype(out_dtype)
            return
        o_ref, acc_ref = refs[-2], refs[-1]
        kk = pl.program_id(off + 2)

        @pl.when(kk == 0)
        def _():
            if add is not None:
                acc_ref[...] = refs[2][...].astype(F32) * add_scale
            else:
                acc_ref[...] = jnp.zeros_like(acc_ref)

        acc_ref[...] += prod

        @pl.when(kk == nk - 1)
        def _():
            o_ref[...] = acc_ref[...].astype(out_dtype)

    if out_head == 'lead':
        oshape = (heads, m, n)
    elif out_head == 'col':
        oshape = (m, heads * n)
    else:
        oshape = (m, n)
    grid = hgrid + (m // tm, n // tn, nk)
    scratch = [pltpu.VMEM((tm, tn), F32)] if nk > 1 else []
    if rider is not None:
        (out,), carried = _carried_call(
            body, rider, *_grid_ends(grid), name=name, grid=grid, in_specs=in_specs, out_specs=[o_spec],
            out_shape=[jax.ShapeDtypeStruct(oshape, out_dtype)], scratch_shapes=scratch,
            sem=("arbitrary",) * len(grid), args=tuple(args))
        return out, carried
    sem = ("parallel",) * (off + 2) + ("arbitrary",)
    return pl.pallas_call(
        body, name=name, grid=grid, in_specs=in_specs, out_specs=o_spec,
        out_shape=jax.ShapeDtypeStruct(oshape, out_dtype), scratch_shapes=scratch,
        compiler_params=_cparams(sem))(*args)


def _mm2(a, b, **kw):
    ta, tb = kw.get('ta', False), kw.get('tb', False)
    m = a.shape[1] if ta else a.shape[0]
    k = a.shape[0] if ta else a.shape[1]
    n = b.shape[0] if tb else b.shape[1]
    return _mm(a, b, dims=(m, n, k), **kw)


def _rowwise(fn, rows, bcast, outs, reds=(), *, name, tm=256, heads=None):
    t = rows[0][0].shape[0]
    tm = min(tm, t)
    hn = 1 if heads is None else heads
    in_specs, args = [], []
    for arr, width, base, per_head in rows:
        in_specs.append(pl.BlockSpec((tm, width), functools.partial(
            lambda i, h, base, per_head: (i, base + (h if per_head else 0)), base=base, per_head=per_head)))
        args.append(arr)
    for arr in bcast:
        in_specs.append(pl.BlockSpec(arr.shape, lambda i, h: (0, 0)))
        args.append(arr)
    out_specs, out_shape = [], []
    for total, width, per_head, dt in outs:
        out_specs.append(pl.BlockSpec((tm, width), functools.partial(
            lambda i, h, per_head: (i, h if per_head else 0), per_head=per_head)))
        out_shape.append(jax.ShapeDtypeStruct((t, total), dt))
    for shp in reds:
        out_specs.append(pl.BlockSpec(shp, lambda i, h: (0, 0)))
        out_shape.append(jax.ShapeDtypeStruct(shp, F32))
    n_in, n_out, n_red = len(args), len(outs), len(reds)

    def body(*refs):
        i, h = pl.program_id(0), pl.program_id(1)
        vals = fn(h, *[r[...] for r in refs[:n_in]])
        for r, v in zip(refs[n_in:n_in + n_out], vals[:n_out]):
            r[...] = v.astype(r.dtype)
        if n_red:
            @pl.when((i == 0) & (h == 0))
            def _():
                for r in refs[n_in + n_out:]:
                    r[...] = jnp.zeros_like(r)
            for r, v in zip(refs[n_in + n_out:], vals[n_out:]):
                r[...] += v

    sem = ("arbitrary", "arbitrary") if n_red else ("parallel", "parallel")
    res = pl.pallas_call(body, name=name, grid=(t // tm, hn), in_specs=in_specs, out_specs=out_specs,
                         out_shape=out_shape, compiler_params=_cparams(sem))(*args)
    return tuple(res)


def _sigmoid(x):
    return 1.0 / (1.0 + jnp.exp(-x))


def _silu(x):
    return x * _sigmoid(x)


def _softplus(x):
    return jnp.maximum(x, 0.0) + jnp.log(1.0 + jnp.exp(-jnp.abs(x)))


def _layer_norm(t, g, b):
    mu = jnp.mean(t, axis=-1, keepdims=True)
    d = t - mu
    var = jnp.mean(d * d, axis=-1, keepdims=True)
    return d * lax.rsqrt(var + 1e-5) * g + b


def _rms_norm(t, w):
    return t * lax.rsqrt(jnp.mean(t * t, axis=-1, keepdims=True) + 1e-6) * w


def _swap_rope_halves(t):
    lane = lax.broadcasted_iota(jnp.int32, t.shape, 1) % ROPE_PAD
    n = t.shape[1]
    up = pltpu.roll(t, n - ROPE // 2, axis=1)
    dn = pltpu.roll(t, ROPE // 2, axis=1)
    return jnp.where(lane < ROPE // 2, up, jnp.where(lane < ROPE, dn, 0.0))


def _rope(t, cosb, sinb):
    reps = t.shape[1] // ROPE_PAD
    c = jnp.tile(cosb, (1, reps)) if reps > 1 else cosb
    s = jnp.tile(sinb, (1, reps)) if reps > 1 else sinb
    return t * c + _swap_rope_halves(t) * s


def _rope_bwd(d, cosb, sinb):
    reps = d.shape[1] // ROPE_PAD
    c = jnp.tile(cosb, (1, reps)) if reps > 1 else cosb
    s = jnp.tile(sinb, (1, reps)) if reps > 1 else sinb
    return d * c + _swap_rope_halves(d * s)


_CONV_ROWS = 256
_CONV_COLS = 256


def _conv_window(ref, r0, lo, hi, t):
    parts = []
    start, stop = r0 - lo, r0 + _CONV_ROWS + hi
    if start < 0:
        parts.append(jnp.zeros((-start, ref.shape[1]), F32))
        start = 0
    tail = max(stop - t, 0)
    parts.append(ref[start:stop - tail, :].astype(F32))
    if tail:
        parts.append(jnp.zeros((tail, ref.shape[1]), F32))
    return parts[0] if len(parts) == 1 else jnp.concatenate(parts, axis=0)


def _conv_taps(win, w_ref, n_out):
    acc = win[8:8 + n_out] * w_ref[DN_CONV - 1:DN_CONV, :]
    for i in range(DN_CONV - 1):
        acc = acc + pltpu.roll(win, DN_CONV - 1 - i, axis=0)[8:8 + n_out] * w_ref[i:i + 1, :]
    return acc


def _conv_silu(x, w):
    t, ch = x.shape

    def body(x_ref, w_ref, o_ref):
        for r in range(t // _CONV_ROWS):
            r0 = r * _CONV_ROWS
            c = _conv_taps(_conv_window(x_ref, r0, 8, 0, t), w_ref, _CONV_ROWS)
            o_ref[r0:r0 + _CONV_ROWS, :] = _silu(c)

    return pl.pallas_call(
        body, name="conv_silu", grid=(ch // _CONV_COLS,),
        in_specs=[pl.BlockSpec((t, _CONV_COLS), lambda j: (0, j)), pl.BlockSpec((DN_CONV, _CONV_COLS), lambda j: (0, j))],
        out_specs=pl.BlockSpec((t, _CONV_COLS), lambda j: (0, j)),
        out_shape=jax.ShapeDtypeStruct((t, ch), F32), compiler_params=_cparams(("parallel",)))(x, w)


def _conv_silu_bwd(x, w, dys):
    t, ch = x.shape
    per = ch // len(dys) // _CONV_COLS

    def body(x_ref, w_ref, *rest):
        dy_refs, (dx_ref, dw_ref) = rest[:len(dys)], rest[len(dys):]
        sec = pl.program_id(0) // per
        dws = [jnp.zeros((1, _CONV_COLS), F32) for _ in range(DN_CONV)]
        for r in range(t // _CONV_ROWS):
            r0 = r * _CONV_ROWS
            n_ext = _CONV_ROWS + 8
            xw = _conv_window(x_ref, r0, 8, 8, t)
            c = _conv_taps(xw, w_ref, n_ext)
            sg = _sigmoid(c)
            dy = _conv_window(dy_refs[-1], r0, 0, 8, t)
            for k in range(len(dys) - 2, -1, -1):
                dy = jnp.where(sec == k, _conv_window(dy_refs[k], r0, 0, 8, t), dy)
            ds = dy * (sg * (1.0 + c * (1.0 - sg)))
            x0 = xw[8:8 + _CONV_ROWS]
            dx = jnp.zeros((_CONV_ROWS, _CONV_COLS), F32)
            for i in range(DN_CONV):
                sh = DN_CONV - 1 - i
                ds_up = (ds if sh == 0 else pltpu.roll(ds, n_ext - sh, axis=0))[:_CONV_ROWS]
                dx = dx + ds_up * w_ref[i:i + 1, :]
                dws[i] = dws[i] + jnp.sum(x0 * ds_up, axis=0, keepdims=True)
            dx_ref[r0:r0 + _CONV_ROWS, :] = dx.astype(dx_ref.dtype)
        for i in range(DN_CONV):
            dw_ref[i:i + 1, :] = dws[i]

    blk = pl.BlockSpec((t, _CONV_COLS), lambda j: (0, j))
    wblk = pl.BlockSpec((DN_CONV, _CONV_COLS), lambda j: (0, j))
    dy_specs = [pl.BlockSpec((t, _CONV_COLS), functools.partial(lambda j, k: (0, jnp.clip(j - k * per, 0, per - 1)), k=k))
                for k in range(len(dys))]
    return pl.pallas_call(
        body, name="conv_silu_bwd", grid=(ch // _CONV_COLS,), in_specs=[blk, wblk] + dy_specs, out_specs=[blk, wblk],
        out_shape=[jax.ShapeDtypeStruct((t, ch), _CDT), jax.ShapeDtypeStruct((DN_CONV, ch), F32)],
        compiler_params=_cparams(("arbitrary",)))(x, w, *dys)


_PA_ROWS = 1024
_PA_ROWS_FWD = 2048


def _bmm(a, b, spec):
    return jnp.einsum(spec, a.astype(_CDT), b.astype(_CDT), preferred_element_type=F32)


def _split16(a):
    hi = a.astype(jnp.bfloat16)
    return hi, (a - hi.astype(F32)).astype(jnp.bfloat16)


def _bmm3(a, b, spec):
    ah, al = _split16(a)
    bh, bl = _split16(b)
    e = lambda p, q: jnp.einsum(spec, p, q, preferred_element_type=F32)
    return e(ah, bh) + (e(ah, bl) + e(al, bh))


def _split3(b):
    b0 = b.astype(jnp.bfloat16)
    r1 = b - b0.astype(F32)
    b1 = r1.astype(jnp.bfloat16)
    return b0, b1, (r1 - b1.astype(F32)).astype(jnp.bfloat16)


@functools.partial(jax.custom_vjp, nondiff_argnums=(2, 3))
def _select_mm(sel, b, spec, spec_t):
    return sum(jnp.einsum(spec, sel, t, preferred_element_type=F32) for t in _split3(b))


def _select_mm_fwd(sel, b, spec, spec_t):
    return _select_mm(sel, b, spec, spec_t), sel


def _select_mm_bwd(spec, spec_t, sel, ct):
    return jnp.zeros_like(sel), sum(jnp.einsum(spec_t, sel, t, preferred_element_type=F32) for t in _split3(ct))


_select_mm.defvjp(_select_mm_fwd, _select_mm_bwd)


def _tri_inverse(l_mat, eye):
    pw = -l_mat
    t_inv = eye + pw
    for _ in range(5):
        pw = _bmm3(pw, pw, 'bij,bjk->bik')
        t_inv = t_inv + _bmm3(t_inv, pw, 'bij,bjk->bik')
    return t_inv


@jax.custom_vjp
def _tri_inverse_saved(l_mat, t_saved):
    return t_saved


def _tri_inverse_saved_fwd(l_mat, t_saved):
    return t_saved, t_saved


def _tri_inverse_saved_bwd(t_saved, dt):
    left = _bmm3(t_saved, dt, 'bji,bjk->bik')
    return -_bmm3(left, t_saved, 'bij,bkj->bik'), jnp.zeros_like(t_saved)


_tri_inverse_saved.defvjp(_tri_inverse_saved_fwd, _tri_inverse_saved_bwd)


def _phase_a(h, q, k, v, ba, alog, dtb, t_saved=None):
    r = q.shape[0]
    nb = r // DN_CHUNK
    c = DN_CHUNK
    lane = lax.broadcasted_iota(jnp.int32, (1, LANE), 1)
    selb = (lane == h).astype(F32)
    sela = (lane == h + HEADS).astype(F32)
    b_raw = jnp.sum(ba * selb, axis=1, keepdims=True)
    a_raw = jnp.sum(ba * sela, axis=1, keepdims=True)
    al = jnp.sum(alog * selb, axis=1, keepdims=True)
    dt = jnp.sum(dtb * selb, axis=1, keepdims=True)
    beta = jnp.broadcast_to(_sigmoid(b_raw), (r, LANE))
    g = jnp.broadcast_to(-jnp.exp(al) * _softplus(a_raw + dt), (r, LANE))
    qn = q * lax.rsqrt(jnp.sum(q * q, -1, keepdims=True) + 1e-6) * (DN_DK ** -0.5)
    kn = k * lax.rsqrt(jnp.sum(k * k, -1, keepdims=True) + 1e-6)
    q3, k3, v3 = qn.reshape(nb, c, LANE), kn.reshape(nb, c, LANE), v.reshape(nb, c, LANE)
    b3, g3 = beta.reshape(nb, c, LANE), g.reshape(nb, c, LANE)
    ri = lax.broadcasted_iota(jnp.int32, (nb, c, c), 1)
    ci = lax.broadcasted_iota(jnp.int32, (nb, c, c), 2)
    tril, strict = ri >= ci, ri > ci
    gc = _select_mm(tril.astype(jnp.bfloat16), g3, 'bij,bjd->bid', 'bij,bid->bjd')
    onehot = (lax.broadcasted_iota(jnp.int32, (nb, c, LANE), 2) == 0).astype(jnp.bfloat16)
    g_row = _select_mm(onehot, gc, 'bid,bjd->bij', 'bid,bij->bjd')
    diff = gc[:, :, :c] - g_row
    decay = jnp.where(tril, jnp.exp(jnp.where(tril, diff, 0.0)), 0.0)
    kb = k3 * b3
    l_mat = jnp.where(strict, _bmm(kb, k3, 'bid,bjd->bij') * decay, 0.0)
    if t_saved is None:
        t_inv = _tri_inverse(l_mat, (ri == ci).astype(F32))
    else:
        t_inv = _tri_inverse_saved(l_mat, t_saved.reshape(nb, c, c))
    eg = jnp.exp(gc)
    u = _bmm(t_inv, v3 * b3, 'bij,bje->bie')
    w = _bmm(t_inv, kb * eg, 'bij,bje->bie')
    intra = jnp.where(tril, _bmm(q3, k3, 'bid,bjd->bij') * decay, 0.0)
    qd = q3 * eg
    gl = jnp.sum(g3, axis=1, keepdims=True)
    kt = k3 * jnp.exp(gl - gc)
    outs = (u.reshape(r, LANE), w.reshape(r, LANE), qd.reshape(r, LANE), kt.reshape(r, LANE),
            intra.reshape(r, c), gl.reshape(nb, LANE))
    if t_saved is not None:
        return outs
    qd2 = qd - _bmm(intra, w, 'bij,bjd->bid')
    au = _bmm(intra, u, 'bij,bje->bie')
    return outs + (t_inv.reshape(r, c), qd2.reshape(r, LANE), au.reshape(r, LANE))


def _pa_specs(t, rows):
    rr = min(rows, t)
    nb = rr // DN_CHUNK
    qkv = [pl.BlockSpec((rr, LANE), functools.partial(lambda i, h, o: (i, o + h), o=o)) for o in (0, HEADS, 2 * HEADS)]
    ba = pl.BlockSpec((rr, LANE), lambda i, h: (i, 3))
    vec = pl.BlockSpec((1, LANE), lambda i, h: (0, 0))
    row = pl.BlockSpec((rr, LANE), lambda i, h: (i, h))
    intra = pl.BlockSpec((None, rr, DN_CHUNK), lambda i, h: (h, i, 0))
    gl = pl.BlockSpec((nb, LANE), lambda i, h: (i, h))
    return rr, qkv, ba, vec, row, intra, gl


def _grid_ends(grid):
    first = lambda: functools.reduce(jnp.logical_and, [pl.program_id(a) == 0 for a in range(len(grid))])
    last = lambda: functools.reduce(jnp.logical_and, [pl.program_id(a) == n - 1 for a, n in enumerate(grid)])
    return first, last


def _delta_local(qkv_act, pm, alog, dtb, rider=None):
    t = qkv_act.shape[0]
    rr, qkv, ba, vec, row, intra, gl = _pa_specs(t, _PA_ROWS_FWD)

    def body(q, k, v, b, al, dt, *outs):
        vals = _phase_a(pl.program_id(1), q[...], k[...], v[...], b[...], al[...], dt[...])
        for o, val in zip(outs, vals):
            o[...] = val

    wide = jax.ShapeDtypeStruct((t, HEADS * LANE), F32)
    sq = jax.ShapeDtypeStruct((HEADS, t, DN_CHUNK), F32)
    grid = (t // rr, HEADS)
    return _carried_call(
        body, rider, *_grid_ends(grid), name="delta_local", grid=grid, in_specs=qkv + [ba, vec, vec],
        out_specs=[row] * 4 + [intra, gl, intra, row, row],
        out_shape=[wide] * 4 + [sq, jax.ShapeDtypeStruct((t // DN_CHUNK, HEADS * LANE), F32), sq, wide, wide],
        scratch_shapes=[], sem=("arbitrary", "arbitrary"), args=(qkv_act, qkv_act, qkv_act, pm, alog, dtb))


def _delta_local_bwd(qkv_act, pm, alog, dtb, t_inv, du, dw, dqd, dkt, dintra, dgl, rider=None):
    t = qkv_act.shape[0]
    rr, qkv, ba, vec, row, intra, gl = _pa_specs(t, _PA_ROWS)

    def body(q, k, v, b, al, dt, ti, du_r, dw_r, dqd_r, dkt_r, di_r, dgl_r, dq_o, dk_o, dv_o, dba_o, dal_o, ddt_o):
        i, h = pl.program_id(0), pl.program_id(1)
        t_saved = ti[...]
        _, vjp = jax.vjp(lambda *a: _phase_a(h, *a, t_saved=t_saved), q[...], k[...], v[...], b[...], al[...], dt[...])
        dq, dk, dv, dba, dal, ddt = vjp((du_r[...], dw_r[...], dqd_r[...], dkt_r[...], di_r[...], dgl_r[...]))
        dq_o[...], dk_o[...], dv_o[...] = dq, dk, dv

        @pl.when(h == 0)
        def _():
            dba_o[...] = jnp.zeros_like(dba_o)

        @pl.when((h == 0) & (i == 0))
        def _():
            dal_o[...] = jnp.zeros_like(dal_o)
            ddt_o[...] = jnp.zeros_like(ddt_o)

        dba_o[...] += dba
        dal_o[...] += dal
        ddt_o[...] += ddt

    wide = jax.ShapeDtypeStruct((t, HEADS * LANE), F32)
    vshape = jax.ShapeDtypeStruct((1, LANE), F32)
    grid = (t // rr, HEADS)
    return _carried_call(
        body, rider, *_grid_ends(grid), name="delta_local_bwd", grid=grid,
        in_specs=qkv + [ba, vec, vec, intra] + [row] * 4 + [intra, gl],
        out_specs=[row] * 3 + [pl.BlockSpec((rr, LANE), lambda i, h: (i, 0)), vec, vec],
        out_shape=[wide] * 3 + [jax.ShapeDtypeStruct((t, LANE), F32), vshape, vshape],
        scratch_shapes=[], sem=("arbitrary", "arbitrary"),
        args=(qkv_act, qkv_act, qkv_act, pm, alog, dtb, t_inv, du, dw, dqd, dkt, dintra, dgl))


_SCAN_ROWS = 512


def _dot(a, b, dn):
    return lax.dot_general(a.astype(_CDT), b.astype(_CDT), (dn, ((), ())), preferred_element_type=F32)


_NN = ((1,), (0,))
_NT = ((1,), (1,))
_TN = ((0,), (0,))


def _delta_scan(u, w, qd, kt, au, gl, rider=None):
    t = u.shape[0]
    rr = min(_SCAN_ROWS, t)
    nc = rr // DN_CHUNK

    def body(u_ref, w_ref, qd_ref, kt_ref, au_ref, gl_ref, o_ref, sall_ref, s_scr):
        @pl.when(pl.program_id(0) == 0)
        def _():
            s_scr[...] = jnp.zeros_like(s_scr)

        def chunk(c, carry):
            r0 = pl.multiple_of(c * DN_CHUNK, DN_CHUNK)
            rows = pl.ds(r0, DN_CHUNK)
            e = jnp.exp(gl_ref[pl.ds(c, 1), :])
            states = [s_scr[h] for h in range(HEADS)]
            u_c, w_c, qd_c, kt_c, au_c = u_ref[rows, :], w_ref[rows, :], qd_ref[rows, :], kt_ref[rows, :], au_ref[rows, :]
            o_new, s_new = [], []
            for h in range(HEADS):
                cs = slice(h * LANE, (h + 1) * LANE)
                s = states[h]
                both = _dot(jnp.concatenate([w_c[:, cs], qd_c[:, cs]], axis=0), s, _NN)
                v_new = u_c[:, cs] - both[:DN_CHUNK]
                o_new.append(both[DN_CHUNK:] + au_c[:, cs])
                s_new.append(s * e[:, cs] + _dot(kt_c[:, cs], v_new, _TN))
            o_ref[rows, :] = jnp.concatenate(o_new, axis=1)
            for h in range(HEADS):
                sall_ref[c, h] = states[h]
                s_scr[h] = s_new[h]
            return carry

        lax.fori_loop(0, nc, chunk, 0)

    row = pl.BlockSpec((rr, HEADS * LANE), lambda i: (i, 0))
    grid = (t // rr,)
    return _carried_call(
        body, rider, *_grid_ends(grid), name="delta_scan", grid=grid,
        in_specs=[row] * 5 + [pl.BlockSpec((nc, HEADS * LANE), lambda i: (i, 0))],
        out_specs=[row, pl.BlockSpec((nc, HEADS, LANE, LANE), lambda i: (i, 0, 0, 0))],
        out_shape=[jax.ShapeDtypeStruct((t, HEADS * LANE), F32),
                   jax.ShapeDtypeStruct((t // DN_CHUNK, HEADS, LANE, LANE), F32)],
        scratch_shapes=[pltpu.VMEM((HEADS, LANE, LANE), F32)], sem=("arbitrary",), args=(u, w, qd, kt, au, gl))


def _delta_scan_bwd(u, w, qd, kt, intra, gl, sall, do, rider=None):
    t = u.shape[0]
    rr = min(_SCAN_ROWS, t)
    nc = rr // DN_CHUNK
    ng = t // rr

    def body(u_ref, w_ref, qd_ref, kt_ref, a_ref, gl_ref, sall_ref, do_ref,
             du_ref, dw_ref, dqd_ref, dkt_ref, da_ref, dgl_ref, ds_scr):
        @pl.when(pl.program_id(0) == 0)
        def _():
            ds_scr[...] = jnp.zeros_like(ds_scr)

        def chunk(cc, carry):
            c = nc - 1 - cc
            r0 = pl.multiple_of(c * DN_CHUNK, DN_CHUNK)
            rows = pl.ds(r0, DN_CHUNK)
            e = jnp.exp(gl_ref[pl.ds(c, 1), :])
            states = [sall_ref[c, h] for h in range(HEADS)]
            ds_outs = [ds_scr[h] for h in range(HEADS)]
            u_a, w_a, kt_a, qd_a, do_a = u_ref[rows, :], w_ref[rows, :], kt_ref[rows, :], qd_ref[rows, :], do_ref[rows, :]
            a_a = [a_ref[h, rows, :] for h in range(HEADS)]
            da, dqd, dkt, du, dw, dgl, ds_new = [], [], [], [], [], [], []
            for h in range(HEADS):
                cs = slice(h * LANE, (h + 1) * LANE)
                s, ds_out = states[h], ds_outs[h]
                w_c, kt_c, qd_c, do_c = w_a[:, cs], kt_a[:, cs], qd_a[:, cs], do_a[:, cs]
                v_new = u_a[:, cs] - _dot(w_c, s, _NN)
                dv_new = _dot(a_a[h], do_c, _TN) + _dot(kt_c, ds_out, _NN)
                cots = jnp.concatenate([do_c, dv_new], axis=0)
                both = _dot(cots, s, _NT)
                dqd.append(both[:DN_CHUNK])
                dw.append(-both[DN_CHUNK:])
                da.append(_dot(do_c, v_new, _NT))
                dkt.append(_dot(v_new, ds_out, _NT))
                du.append(dv_new)
                eh = e[:, cs]
                dgl.append(jnp.broadcast_to(jnp.sum(ds_out * s, axis=0, keepdims=True) * eh, (8, LANE)))
                ds_new.append(ds_out * eh + _dot(jnp.concatenate([qd_c, -w_c], axis=0), cots, _TN))
            cat = lambda parts: jnp.concatenate(parts, axis=1)
            dqd_ref[rows, :], dkt_ref[rows, :], du_ref[rows, :], dw_ref[rows, :] = cat(dqd), cat(dkt), cat(du), cat(dw)
            dgl_ref[pl.ds(pl.multiple_of(c * 8, 8), 8), :] = cat(dgl)
            for h in range(HEADS):
                da_ref[h, rows, :] = da[h]
                ds_scr[h] = ds_new[h]
            return carry

        lax.fori_loop(0, nc, chunk, 0)

    rev = lambda i: (ng - 1 - i, 0)
    row = pl.BlockSpec((rr, HEADS * LANE), rev)
    a_spec = pl.BlockSpec((HEADS, rr, DN_CHUNK), lambda i: (0, ng - 1 - i, 0))
    gl_spec = pl.BlockSpec((nc, HEADS * LANE), rev)
    wide = jax.ShapeDtypeStruct((t, HEADS * LANE), F32)
    outs, carried = _carried_call(
        body, rider, *_grid_ends((ng,)), name="delta_scan_bwd", grid=(ng,),
        in_specs=[row] * 4 + [a_spec, gl_spec, pl.BlockSpec((nc, HEADS, LANE, LANE), lambda i: (ng - 1 - i, 0, 0, 0)), row],
        out_specs=[row] * 4 + [a_spec, pl.BlockSpec((nc * 8, HEADS * LANE), rev)],
        out_shape=[wide] * 4 + [jax.ShapeDtypeStruct((HEADS, t, DN_CHUNK), F32),
                                jax.ShapeDtypeStruct((t // DN_CHUNK * 8, HEADS * LANE), F32)],
        scratch_shapes=[pltpu.VMEM((HEADS, LANE, LANE), F32)], sem=("arbitrary",), args=(u, w, qd, kt, intra, gl, sall, do))
    return tuple(outs[:5]) + (outs[5].reshape(t // DN_CHUNK, 8, HEADS * LANE)[:, 0, :],), carried


_ATT_TILE = 512


def _kv_rows(j, tk):
    return pl.ds(pl.multiple_of(j * tk, tk), tk)


def _att_scores(ql, qr, ckv_ref, kr_ref, j, tk):
    ks = _kv_rows(j, tk)
    return (_dot(ql, ckv_ref[ks, :], _NT) + _dot(qr, kr_ref[ks, :], _NT)) * ATT_SCALE


def _diag_mask(s):
    qi = lax.broadcasted_iota(jnp.int32, s.shape, 0)
    ki = lax.broadcasted_iota(jnp.int32, s.shape, 1)
    return jnp.where(ki <= qi, s, NEG_BIG)


def _attention(qn, qr_pre, cosb, sinb, ckv, kr, wuk, wuv):
    t = ckv.shape[0]
    tq = min(_ATT_TILE, t)

    nl = tq // LANE

    def lane_fold(v, op):
        out = v[:, :LANE]
        for k in range(1, nl):
            out = op(out, v[:, k * LANE:(k + 1) * LANE])
        return out

    def body(qn_ref, qr_ref, cos_ref, sin_ref, ckv_ref, kr_ref, wuk_ref, wuv_ref, o_ref, lse_ref, qrope_ref, omla_ref,
             s_all, m_lanes, l_lanes, acc_scr):
        h, qi = pl.program_id(0), pl.program_id(1)
        q_lat = _dot(qn_ref[...], wuk_ref[h], _NT).astype(_CDT)
        q_rope = _rope(qr_ref[...], cos_ref[...], sin_ref[...]).astype(qrope_ref.dtype)
        qrope_ref[...] = q_rope
        m_lanes[...] = jnp.full_like(m_lanes, NEG_BIG)

        def scores(j, masked):
            s = _att_scores(q_lat, q_rope, ckv_ref, kr_ref, j, tq)
            if masked:
                s = _diag_mask(s)
            s_all[j] = s
            m_lanes[...] = jnp.maximum(m_lanes[...], lane_fold(s, jnp.maximum))

        def scores_body(j, carry):
            scores(j, False)
            return carry

        lax.fori_loop(0, qi, scores_body, 0)
        scores(qi, True)
        m = jnp.max(m_lanes[...], axis=-1, keepdims=True)
        mb = jnp.broadcast_to(m, (tq, LANE))
        l_lanes[...] = jnp.zeros_like(l_lanes)
        acc_scr[...] = jnp.zeros_like(acc_scr)

        def weigh(j, carry):
            s = s_all[j]
            p = jnp.concatenate([jnp.exp(s[:, k * LANE:(k + 1) * LANE] - mb) for k in range(nl)], axis=1)
            l_lanes[...] += lane_fold(p, jnp.add)
            acc_scr[...] += _dot(p, ckv_ref[_kv_rows(j, tq), :], _NN)
            return carry

        lax.fori_loop(0, qi + 1, weigh, 0)
        l = jnp.sum(l_lanes[...], axis=-1, keepdims=True)
        out = acc_scr[...] / l
        o_ref[...] = out
        lse_ref[...] = m + jnp.log(l)
        omla_ref[...] = _dot(out, wuv_ref[h], _NN).astype(omla_ref.dtype)

    col = pl.BlockSpec((tq, LANE), lambda h, i: (i, h))
    table = pl.BlockSpec((tq, ROPE_PAD), lambda h, i: (i, 0))
    wspec = pl.BlockSpec((HEADS, KV_LORA, NOPE), lambda h, i: (0, 0, 0))
    return pl.pallas_call(
        body, name="attention", grid=(HEADS, t // tq),
        in_specs=[col, col, table, table, pl.BlockSpec((t, KV_LORA), lambda h, i: (0, 0)),
                  pl.BlockSpec((t, ROPE_PAD), lambda h, i: (0, 0)), wspec, wspec],
        out_specs=[pl.BlockSpec((None, tq, KV_LORA), lambda h, i: (h, i, 0)),
                   pl.BlockSpec((None, tq, 1), lambda h, i: (h, i, 0)), col, col],
        out_shape=[jax.ShapeDtypeStruct((HEADS, t, KV_LORA), F32), jax.ShapeDtypeStruct((HEADS, t, 1), F32),
                   jax.ShapeDtypeStruct((t, HEADS * ROPE_PAD), _CDT), jax.ShapeDtypeStruct((t, HEADS * NOPE), _CDT)],
        scratch_shapes=[pltpu.VMEM((t // tq, tq, tq), F32), pltpu.VMEM((tq, LANE), F32), pltpu.VMEM((tq, LANE), F32),
                        pltpu.VMEM((tq, KV_LORA), F32)],
        compiler_params=_cparams(("parallel", "parallel")))(qn, qr_pre, cosb, sinb, ckv, kr, wuk, wuv)


def _attention_bwd(qn, qr, cosb, sinb, ckv, kr, wuk, wuv, out, lse, do_mla):
    t = ckv.shape[0]
    tq = min(_ATT_TILE, t)

    def body(qn_ref, qr_ref, cos_ref, sin_ref, ckv_ref, kr_ref, wuk_ref, wuv_ref, o_ref, lse_ref, do_ref,
             dql_ref, dqr_ref, dqn_ref, dckv_ref, dkr_ref, dql_scr, dqr_scr):
        h, qi = pl.program_id(0), pl.program_id(1)

        @pl.when((h == 0) & (qi == 0))
        def _():
            dckv_ref[...] = jnp.zeros_like(dckv_ref)
            dkr_ref[...] = jnp.zeros_like(dkr_ref)

        q_lat = _dot(qn_ref[...], wuk_ref[h], _NT).astype(_CDT)
        q_rope = qr_ref[...]
        d_out = _dot(do_ref[...], wuv_ref[h], _NT)
        d_o = d_out.astype(_CDT)
        lse_v = lse_ref[...]
        dsum = jnp.sum(d_out * o_ref[...], axis=-1, keepdims=True)
        dql_scr[...] = jnp.zeros_like(dql_scr)
        dqr_scr[...] = jnp.zeros_like(dqr_scr)

        def step(j, masked):
            ks = _kv_rows(j, tq)
            s = _att_scores(q_lat, q_rope, ckv_ref, kr_ref, j, tq)
            if masked:
                s = _diag_mask(s)
            p = jnp.exp(s - lse_v)
            kv = ckv_ref[ks, :]
            ds = (p * (_dot(d_o, kv, _NT) - dsum) * ATT_SCALE).astype(_CDT)
            pb = p.astype(_CDT)
            dql_scr[...] += _dot(ds, kv, _NN)
            dqr_scr[...] += _dot(ds, kr_ref[ks, :], _NN)
            dckv_ref[ks, :] += _dot(pb, d_o, _TN) + _dot(ds, q_lat, _TN)
            dkr_ref[ks, :] += _dot(ds, q_rope, _TN)

        def loop_body(j, carry):
            step(j, False)
            return carry

        lax.fori_loop(0, qi, loop_body, 0)
        step(qi, True)
        dql = dql_scr[...].astype(dql_ref.dtype)
        dql_ref[...] = dql
        dqn_ref[...] = _dot(dql, wuk_ref[h], _NN).astype(dqn_ref.dtype)
        dqr_ref[...] = _rope_bwd(dqr_scr[...], cos_ref[...], sin_ref[...]).astype(dqr_ref.dtype)

    lat = pl.BlockSpec((None, tq, KV_LORA), lambda h, i: (h, i, 0))
    col = pl.BlockSpec((tq, LANE), lambda h, i: (i, h))
    table = pl.BlockSpec((tq, ROPE_PAD), lambda h, i: (i, 0))
    kfull = pl.BlockSpec((t, KV_LORA), lambda h, i: (0, 0))
    rfull = pl.BlockSpec((t, ROPE_PAD), lambda h, i: (0, 0))
    wspec = pl.BlockSpec((HEADS, KV_LORA, NOPE), lambda h, i: (0, 0, 0))
    wide = jax.ShapeDtypeStruct((t, HEADS * LANE), _CDT)
    return pl.pallas_call(
        body, name="attention_bwd", grid=(HEADS, t // tq),
        in_specs=[col, col, table, table, kfull, rfull, wspec, wspec, lat,
                  pl.BlockSpec((None, tq, 1), lambda h, i: (h, i, 0)), col],
        out_specs=[lat, col, col, kfull, rfull],
        out_shape=[jax.ShapeDtypeStruct((HEADS, t, KV_LORA), _CDT), wide, wide,
                   jax.ShapeDtypeStruct((t, KV_LORA), F32), jax.ShapeDtypeStruct((t, ROPE_PAD), F32)],
        scratch_shapes=[pltpu.VMEM((tq, KV_LORA), F32), pltpu.VMEM((tq, ROPE_PAD), F32)],
        compiler_params=_cparams(("arbitrary", "arbitrary")))(qn, qr, cosb, sinb, ckv, kr, wuk, wuv, out, lse, do_mla)


_DX_ROWS = 256


def _dx_fused(pairs, add, add_scale, rider=None, name="b_dx"):
    t, d = add.shape
    tm = min(_DX_ROWS, t)
    n = len(pairs)

    def body(*refs):
        acc = refs[2 * n][...] * add_scale
        for i in range(n):
            acc = acc + _dot(refs[i][...], refs[n + i][...], _NT)
        refs[2 * n + 1][...] = acc

    in_specs = [pl.BlockSpec((tm, a.shape[1]), lambda i: (i, 0)) for a, _ in pairs]
    in_specs += [pl.BlockSpec(w.shape, lambda i: (0, 0)) for _, w in pairs]
    row = pl.BlockSpec((tm, d), lambda i: (i, 0))
    grid = (t // tm,)
    (dx,), carried = _carried_call(
        body, rider, *_grid_ends(grid), name=name, grid=grid, in_specs=in_specs + [row], out_specs=[row],
        out_shape=[jax.ShapeDtypeStruct((t, d), F32)], scratch_shapes=[], sem=("arbitrary",),
        args=tuple(a for a, _ in pairs) + tuple(w for _, w in pairs) + (add,))
    return dx, carried


def _ffn_in_swiglu(a, w):
    t, k = a.shape
    hid = w.shape[1] // 2
    tm, tn = _tile(t, 1024), _tile(hid, 1408)
    nj = hid // tn

    def body(a_ref, bg_ref, bu_ref, act_ref, gt_ref, up_ref):
        av = a_ref[...].astype(_CDT)
        gt = jnp.dot(av, bg_ref[...].astype(_CDT), preferred_element_type=F32)
        up = jnp.dot(av, bu_ref[...].astype(_CDT), preferred_element_type=F32)
        act_ref[...] = _swiglu(gt, up).astype(act_ref.dtype)
        gt_ref[...] = gt.astype(gt_ref.dtype)
        up_ref[...] = up.astype(up_ref.dtype)

    out = pl.BlockSpec((tm, tn), lambda i, j: (i, j))
    return pl.pallas_call(
        body, name="f_ffn_in_swiglu", grid=(t // tm, nj),
        in_specs=[pl.BlockSpec((tm, k), lambda i, j: (i, 0)), pl.BlockSpec((k, tn), lambda i, j: (0, j)),
                  pl.BlockSpec((k, tn), lambda i, j: (0, nj + j))],
        out_specs=[out] * 3, out_shape=[jax.ShapeDtypeStruct((t, hid), _CDT)] * 3,
        compiler_params=_cparams(("parallel", "parallel")))(a, w, w)


def _gated_norm(o, z, w):
    return _rms_norm(o, w) * _silu(z)


def _gated_norm_heads(o, z, w):
    heads = [_gated_norm(o[:, h * LANE:(h + 1) * LANE], z[:, h * LANE:(h + 1) * LANE], w) for h in range(HEADS)]
    return jnp.concatenate(heads, axis=1)


def _mla_pre(ckv, krp, cq, cosb, sinb, qw, kw):
    return _rms_norm(cq, qw), _rms_norm(ckv, kw), _rope(krp, cosb, sinb)


def _merge(gg, y_dn, y_mla):
    return _sigmoid(gg[:, :D_MODEL]) * y_dn + _sigmoid(gg[:, D_MODEL:]) * y_mla


def _ln1(xv, attn_out, g, b):
    return _layer_norm(ALPHA * xv + attn_out, g, b)


def _final(h1, ffn, gate_pre, ple_proj, g, b):
    return _layer_norm(ALPHA * h1 + ffn + _sigmoid(gate_pre) * ple_proj, g, b)


def _swiglu(gt, up):
    return _silu(gt) * up


def _local_step(x, p, cosb, sinb, target, wt, sp, exch):
    t = x.shape[0]
    bf = _CDT
    xb = x.astype(bf)
    g = {}

    qkv_pre = _mm2(xb, wt['qkv'], name="f_qkv")
    z = _mm2(xb, wt['z'], name="f_z")
    gg = _mm2(xb, wt['gg'], name="f_gg")
    pm = _mm2(xb, wt['mla'], name="f_mla")
    qkv_act = _conv_silu(qkv_pre, sp['conv_w'])
    (u, w_, qd, kt, intra, gl, t_inv, qd2, au), sent = _delta_local(qkv_act, pm, sp['a_log'], sp['dt_bias'],
                                                                    rider=exch.gather_send())
    (o_dn, sall), passed = _delta_scan(u, w_, qd2, kt, au, gl, rider=exch.gather_pass(sent))
    wt = dict(wt, **exch.weights(passed))
    def gated_norm_br(h, o, zz, w, wbr):
        og_v = _gated_norm_heads(o, zz, w).astype(bf)
        return og_v, jnp.dot(og_v, wbr.astype(bf), preferred_element_type=F32)

    og, y_dn = _rowwise(gated_norm_br, [(o_dn, D_MODEL, 0, False), (z, D_MODEL, 0, False)],
                        [sp['dn_norm_w'], wt['br_dn']],
                        [(D_MODEL, D_MODEL, False, bf), (D_MODEL, D_MODEL, False, F32)], name="f_gated_norm_br", tm=512)

    def mla_pre_uq(h, ckv, krp, cq, cosv, sinv, qw, kw, wn, wr):
        c_q_v, c_kv_v, k_rope_v = _mla_pre(ckv, krp, cq, cosv, sinv, qw, kw)
        c_q_b = c_q_v.astype(bf)
        return (c_q_b, c_kv_v, k_rope_v, jnp.dot(c_q_b, wn.astype(bf), preferred_element_type=F32),
                jnp.dot(c_q_b, wr.astype(bf), preferred_element_type=F32))

    c_q, c_kv, k_rope, q_nope, q_rope_pre = _rowwise(
        mla_pre_uq,
        [(pm, KV_LORA, 0, False), (pm, ROPE_PAD, 2, False), (pm, Q_LORA, 2, False),
         (cosb, ROPE_PAD, 0, False), (sinb, ROPE_PAD, 0, False)],
        [sp['q_norm_w'], sp['kv_norm_w'], wt['uq_nope'], wt['uq_rope']],
        [(Q_LORA, Q_LORA, False, bf), (KV_LORA, KV_LORA, False, bf), (ROPE_PAD, ROPE_PAD, False, bf),
         (HEADS * NOPE, HEADS * NOPE, False, bf), (HEADS * ROPE_PAD, HEADS * ROPE_PAD, False, F32)], name="f_mla_pre_uq",
        tm=512)
    out_lat, lse, q_rope, o_mla = _attention(q_nope, q_rope_pre, cosb, sinb, c_kv, k_rope, wt['uk'], wt['uv'])
    y_mla = _mm2(o_mla, wt['br_mla'], name="f_br_mla")

    def merge_o_ln1(h, ggv, yd, ym, xv, wo, gv, bv):
        mixed_v = _merge(ggv, yd, ym).astype(bf)
        ao = jnp.dot(mixed_v, wo.astype(bf), preferred_element_type=F32)
        h1v = _ln1(xv, ao, gv, bv)
        return mixed_v, ao, h1v, h1v

    mixed, attn_out, h1, h1b = _rowwise(
        merge_o_ln1, [(gg, 2 * D_MODEL, 0, False), (y_dn, D_MODEL, 0, False), (y_mla, D_MODEL, 0, False), (x, D_MODEL, 0, False)],
        [wt['o'], sp['ln1_g'], sp['ln1_b']],
        [(D_MODEL, D_MODEL, False, bf), (D_MODEL, D_MODEL, False, F32), (D_MODEL, D_MODEL, False, F32),
         (D_MODEL, D_MODEL, False, bf)], name="f_merge_o_ln1", tm=512)
    act, ffn_gt, ffn_up = _ffn_in_swiglu(h1b, wt['ffn_in'])
    pb = p.astype(bf)

    def final_fn(h, h1v, actv, pv, tgt, gt, up, wfo, wpg, wpl, gv, bv):
        ffnv = jnp.dot(actv.astype(bf), wfo.astype(bf), preferred_element_type=F32)
        gpv = jnp.dot(h1v.astype(bf), wpg.astype(bf), preferred_element_type=F32)
        ppv = jnp.dot(pv.astype(bf), wpl.astype(bf), preferred_element_type=F32)
        y, vjp = jax.vjp(_final, h1v, ffnv, gpv, ppv, gv, bv)
        err = y - tgt
        dh1, dffn, dgp, dpp, dg, db = vjp(err * (1.0 / D_MODEL))
        sq = err * err
        lanes = sq[:, :LANE]
        for j in range(1, D_MODEL // LANE):
            lanes = lanes + sq[:, j * LANE:(j + 1) * LANE]
        loss = jnp.sum(lanes, axis=0, keepdims=True) * (0.5 / D_MODEL)
        dffn_b = dffn.astype(bf)
        dact = _dot(dffn_b, wfo, _NT).astype(bf).astype(F32)
        _, vjp_s = jax.vjp(_swiglu, gt.astype(F32), up.astype(F32))
        dgt, dup = vjp_s(dact)
        return dffn, dffn_b, dgp, dpp, jnp.concatenate([dgt, dup], axis=1), dg, db, loss

    dpre2, dpre2b, dgate_pre, dple_proj, dffn_in, g['ln2_g'], g['ln2_b'], loss_lanes = _rowwise(
        final_fn, [(h1, D_MODEL, 0, False), (act, FFN_HIDDEN, 0, False), (pb, PLE_DIM, 0, False), (target, D_MODEL, 0, False),
                   (ffn_gt, FFN_HIDDEN, 0, False), (ffn_up, FFN_HIDDEN, 0, False)],
        [wt['ffn_out'], wt['ple_gate'], wt['ple'], sp['ln2_g'], sp['ln2_b']],
        [(D_MODEL, D_MODEL, False, F32)] + [(D_MODEL, D_MODEL, False, bf)] * 3 + [(2 * FFN_HIDDEN, 2 * FFN_HIDDEN, False, bf)],
        [(1, D_MODEL), (1, D_MODEL), (1, LANE)], name="b_final")
    g['loss_lanes'] = loss_lanes

    g['ple'] = _mm2(pb, dple_proj, ta=True, name="g_ple")
    g['ple_gate'] = _mm2(h1b, dgate_pre, ta=True, name="g_ple_gate")
    g['ffn_out'] = _mm2(act, dpre2b, ta=True, name="g_ffn_out")
    g['ffn_in'] = _mm2(h1b, dffn_in, ta=True, name="g_ffn_in")
    dh1, _ = _dx_fused([(dffn_in, wt['ffn_in']), (dgate_pre, wt['ple_gate'])], dpre2, ALPHA, name="b_dh1")

    def attn_out_bwd(h, xv, ao, d, ggv, yd, ym, o, zz, wo, wbd, wbm, gv, bv, nw):
        _, vjp = jax.vjp(_ln1, xv, ao, gv, bv)
        _, dao, dg, db = vjp(d)
        dao_b = dao.astype(bf)
        _, vjp_m = jax.vjp(_merge, ggv, yd, ym)
        dggv, dyd, dym = vjp_m(_dot(dao_b, wo, _NT))
        dyd_b, dym_b = dyd.astype(bf), dym.astype(bf)
        _, vjp_n = jax.vjp(_gated_norm_heads, o, zz, nw)
        do_v, dz_v, dnw = vjp_n(_dot(dyd_b, wbd, _NT))
        return dao, dao_b, dggv, dyd_b, dym_b, do_v, dz_v, _dot(dym_b, wbm, _NT), dg, db, dnw

    row_d = lambda a: (a, D_MODEL, 0, False)
    dpre1, dpre1b, dgg, dy_dn, dy_mla, do_dn, dz, do_mla, g['ln1_g'], g['ln1_b'], g['dn_norm_w'] = _rowwise(
        attn_out_bwd, [row_d(x), row_d(attn_out), row_d(dh1), (gg, 2 * D_MODEL, 0, False), row_d(y_dn), row_d(y_mla),
                       row_d(o_dn), row_d(z)],
        [wt['o'], wt['br_dn'], wt['br_mla'], sp['ln1_g'], sp['ln1_b'], sp['dn_norm_w']],
        [(D_MODEL, D_MODEL, False, F32), (D_MODEL, D_MODEL, False, bf), (2 * D_MODEL, 2 * D_MODEL, False, bf),
         (D_MODEL, D_MODEL, False, bf), (D_MODEL, D_MODEL, False, bf), (D_MODEL, D_MODEL, False, F32),
         (D_MODEL, D_MODEL, False, bf), (D_MODEL, D_MODEL, False, bf)],
        [(1, D_MODEL), (1, D_MODEL), (1, LANE)], name="b_attn_out")
    g['o'] = _mm2(mixed, dpre1b, ta=True, name="g_o")
    g['br_dn'] = _mm2(og, dy_dn, ta=True, name="g_br_dn")
    g['br_mla'] = _mm2(o_mla, dy_mla, ta=True, name="g_br_mla")

    g['uv'] = _mm(out_lat, do_mla, name="g_uv", ta=True, heads=HEADS, a_head='lead', b_head='col', out_head='lead',
                  dims=(KV_LORA, NOPE, t))
    dq_lat, dq_rope_pre, dq_nope, dckv_att, dkr_att = _attention_bwd(
        q_nope, q_rope, cosb, sinb, c_kv, k_rope, wt['uk'], wt['uv'], out_lat, lse, do_mla)
    g['uk'] = _mm(dq_lat, q_nope, name="g_uk", ta=True, heads=HEADS, a_head='lead', b_head='col', out_head='lead',
                  dims=(KV_LORA, NOPE, t))
    g['uq_nope'] = _mm2(c_q, dq_nope, ta=True, name="g_uq_nope")
    g['uq_rope'] = _mm2(c_q, dq_rope_pre, ta=True, name="g_uq_rope")
    dc_q = _mm2(dq_nope, wt['uq_nope'], tb=True, name="b_dcq_nope")
    dc_q = _mm2(dq_rope_pre, wt['uq_rope'], tb=True, name="b_dcq_rope", add=dc_q)

    (du, dw, dqd, dkt, dintra, dgl), paired = _delta_scan_bwd(u, w_, qd, kt, intra, gl, sall, do_dn, rider=exch.pair_send(g))
    (dq_a, dk_a, dv_a, dba, g['a_log'], g['dt_bias']), arrived = _delta_local_bwd(
        qkv_act, pm, sp['a_log'], sp['dt_bias'], t_inv, du, dw, dqd, dkt, dintra, dgl, rider=exch.reduce_send(paired))
    exch.reduce_arrived(arrived)
    dqkv_pre, g['conv_w'] = _conv_silu_bwd(qkv_pre, sp['conv_w'], [dq_a, dk_a, dv_a])

    def mla_pre_bwd(h, ckv, cq, cosv, sinv, dcq, dckv, dkr, dba_v, qw, kw):
        _, vjp = jax.vjp(lambda a, c, d, e: (_rms_norm(c, d), _rms_norm(a, e)), ckv, cq, qw, kw)
        dckv_p, dcq_p, dqw, dkw = vjp((dcq, dckv))
        dkr_p = _rope_bwd(dkr, cosv, sinv)
        dpm = jnp.concatenate([dckv_p, dkr_p, dba_v, jnp.zeros((ckv.shape[0], 2 * LANE), F32), dcq_p], axis=1)
        return dpm, dqw, dkw

    dpm, g['q_norm_w'], g['kv_norm_w'] = _rowwise(
        mla_pre_bwd,
        [(pm, KV_LORA, 0, False), (pm, Q_LORA, 2, False), (cosb, ROPE_PAD, 0, False), (sinb, ROPE_PAD, 0, False),
         (dc_q, Q_LORA, 0, False), (dckv_att, KV_LORA, 0, False), (dkr_att, ROPE_PAD, 0, False), (dba, LANE, 0, False)],
        [sp['q_norm_w'], sp['kv_norm_w']], [(1152, 1152, False, bf)], [(1, Q_LORA), (1, KV_LORA)], name="b_mla_pre")

    rider = exch.small_send(g)
    if rider is None:
        g['qkv'] = _mm2(xb, dqkv_pre, ta=True, name="g_qkv")
    else:
        g['qkv'], got = _mm2(xb, dqkv_pre, ta=True, name="g_qkv", rider=rider)
        exch.small_arrived(got)
    g['z'] = _mm2(xb, dz, ta=True, name="g_z")
    g['gg'] = _mm2(xb, dgg, ta=True, name="g_gg")
    g['mla'] = _mm2(xb, dpm, ta=True, name="g_mla")
    dx, arrived = _dx_fused([(dqkv_pre, wt['qkv']), (dz, wt['z']), (dgg, wt['gg']), (dpm, wt['mla'])], dpre1, ALPHA,
                            rider=exch.in_send(g))
    exch.in_arrived(arrived)
    return loss_lanes, dx, g


_IN_SIZES = (QKV_W, HEADS * DN_DK, HEADS, HEADS, Q_LORA, KV_LORA, ROPE, D_MODEL, D_MODEL)


def _rope_tables(positions):
    inv_freq = ROPE_BASE ** (-jnp.arange(0, ROPE, 2, dtype=F32) / ROPE)
    ang = positions.astype(F32)[:, None] * inv_freq
    cos, sin = jnp.cos(ang), jnp.sin(ang)
    zeros = jnp.zeros((positions.shape[0], ROPE_PAD - ROPE), F32)
    return jnp.concatenate([cos, cos, zeros], axis=1), jnp.concatenate([-sin, sin, zeros], axis=1)


def _prep_w_in(w_in):
    dt = w_in.dtype
    offs = [0]
    for s in _IN_SIZES:
        offs.append(offs[-1] + s)
    qkv, z, wb, wa, cq, ckv, kr, gd, gm = [w_in[:, offs[i]:offs[i + 1]] for i in range(len(_IN_SIZES))]
    zc = lambda n: jnp.zeros((D_MODEL, n), dt)
    return {
        'qkv': qkv, 'z': z, 'gg': jnp.concatenate([gd, gm], axis=1),
        'mla': jnp.concatenate([ckv, kr, zc(ROPE_PAD - ROPE), wb, wa, zc(LANE - 2 * HEADS), zc(2 * LANE), cq], axis=1),
    }


def _prep_weights(full):
    w_uq = full['w_uq']
    wt = {
        'uq_nope': w_uq[:, :, :NOPE].reshape(Q_LORA, HEADS * NOPE),
        'uq_rope': jnp.pad(w_uq[:, :, NOPE:], ((0, 0), (0, 0), (0, ROPE_PAD - ROPE))).reshape(Q_LORA, HEADS * ROPE_PAD),
        'uk': jnp.transpose(full['w_uk'], (1, 0, 2)), 'uv': jnp.transpose(full['w_uv'], (1, 0, 2)),
        'br_dn': full['w_br_dn'], 'br_mla': full['w_br_mla'], 'o': full['w_o'], 'ffn_in': full['w_ffn_in'],
        'ffn_out': full['w_ffn_out'], 'ple': full['w_ple'], 'ple_gate': full['w_ple_gate'],
    }
    return wt


def _prep_small(small):
    pad = lambda v: jnp.pad(v, (0, LANE - v.shape[0]))[None, :]
    return {
        'conv_w': small['conv_w'], 'a_log': pad(small['dn_a_log']), 'dt_bias': pad(small['dn_dt_bias']),
        'dn_norm_w': small['dn_norm_w'][None, :], 'q_norm_w': small['q_norm_w'][None, :],
        'kv_norm_w': small['kv_norm_w'][None, :], 'ln1_g': small['ln1_g'][None, :], 'ln1_b': small['ln1_b'][None, :],
        'ln2_g': small['ln2_g'][None, :], 'ln2_b': small['ln2_b'][None, :],
    }


def _w_in_grad(g):
    mla = g['mla']
    ba0 = KV_LORA + ROPE_PAD
    cq0 = ba0 + 3 * LANE
    return jnp.concatenate([
        g['qkv'], g['z'], mla[:, ba0:ba0 + HEADS], mla[:, ba0 + HEADS:ba0 + 2 * HEADS], mla[:, cq0:cq0 + Q_LORA],
        mla[:, :KV_LORA], mla[:, KV_LORA:KV_LORA + ROPE], g['gg']], axis=1)


def _unprep_grads_late(g):
    return {
        'conv_w': g['conv_w'], 'dn_a_log': g['a_log'][0, :HEADS], 'dn_dt_bias': g['dt_bias'][0, :HEADS],
        'dn_norm_w': g['dn_norm_w'][0], 'q_norm_w': g['q_norm_w'][0], 'kv_norm_w': g['kv_norm_w'][0],
        'ln1_g': g['ln1_g'][0], 'ln1_b': g['ln1_b'][0], 'ln2_g': g['ln2_g'][0], 'ln2_b': g['ln2_b'][0],
    }


def _unprep_grads_early(g):
    w_uq = jnp.concatenate([g['uq_nope'].reshape(Q_LORA, HEADS, NOPE),
                            g['uq_rope'].reshape(Q_LORA, HEADS, ROPE_PAD)[:, :, :ROPE]], axis=2)
    return {
        'w_uq': w_uq, 'w_uk': jnp.transpose(g['uk'], (1, 0, 2)), 'w_uv': jnp.transpose(g['uv'], (1, 0, 2)),
        'w_br_dn': g['br_dn'], 'w_br_mla': g['br_mla'], 'w_o': g['o'],
        'w_ffn_in': g['ffn_in'], 'w_ffn_out': g['ffn_out'], 'w_ple': g['ple'], 'w_ple_gate': g['ple_gate'],
    }


_FLATB_PIECES = (
    ('w_ffn_out', 704, (704, D_MODEL)), ('w_br_dn', 256, (256, D_MODEL)), ('w_br_mla', 256, (256, D_MODEL)),
    ('w_o', 256, (256, D_MODEL)), ('w_ple_gate', 256, (256, D_MODEL)), ('w_uq', 144, (96, HEADS, NOPE + ROPE)),
    ('w_uk', 64, (64, HEADS, NOPE)), ('w_uv', 64, (64, HEADS, NOPE)), ('w_ple', 64, (PLE_DIM, 256)),
)
FLATB_ROWS = 2112
W_IN_SHARD = D_IN // N_SHARD
FFN_IN_SHARD = 2 * FFN_HIDDEN // N_SHARD
A_ROWS = D_MODEL + 32
_CONV_SHARD = QKV_W // N_SHARD
_ADD_TILES = (256, 256, 352)


def _flatb_offsets():
    offs, o = {}, 0
    for name, rows, _ in _FLATB_PIECES:
        offs[name] = o
        o += rows
    return offs, o


def _pack_shards(ws, conv_w):
    conv_bits = lax.bitcast_convert_type(conv_w, jnp.bfloat16).reshape(DN_CONV, 2 * _CONV_SHARD).astype(_CDT)
    tail = jnp.pad(conv_bits, ((0, A_ROWS - D_MODEL - DN_CONV), (0, W_IN_SHARD - 2 * _CONV_SHARD)))
    a_buf = jnp.concatenate([ws['w_in'].astype(_CDT), tail], axis=0)
    parts = [ws[name].astype(_CDT).reshape(rows, FLAT_W) for name, rows, _ in _FLATB_PIECES]
    used = sum(p.shape[0] for p in parts)
    parts.append(jnp.zeros((FLATB_ROWS - used, FLAT_W), _CDT))
    return [a_buf, ws['w_ffn_in'].astype(_CDT), jnp.concatenate(parts, axis=0)]


def _unpack_w_in(gathered, local, me):
    a = [jnp.where(me == s, local, gathered[s]) for s in range(N_SHARD)]
    conv = [lax.bitcast_convert_type(
        p[D_MODEL:D_MODEL + DN_CONV, :2 * _CONV_SHARD].astype(jnp.bfloat16).reshape(DN_CONV, _CONV_SHARD, 2), F32) for p in a]
    return jnp.concatenate([p[:D_MODEL] for p in a], axis=1), jnp.concatenate(conv, axis=1)


def _unpack_rest(gathered, local, me):
    pick = lambda b, s: jnp.where(me == s, local[b], gathered[b][s])
    full = {'w_ffn_in': jnp.concatenate([pick(0, s) for s in range(N_SHARD)], axis=1)}
    offs, _ = _flatb_offsets()
    fb = [pick(1, s) for s in range(N_SHARD)]
    for name, rows, shape in _FLATB_PIECES:
        pieces = [p[offs[name]:offs[name] + rows].reshape(shape) for p in fb]
        full[name] = jnp.concatenate(pieces, axis=1 if name == 'w_ple' else 0)
    return full


def _shard_columns(g, w):
    return jnp.stack([g[:, s * w:(s + 1) * w] for s in range(N_SHARD)])


def _pack_grads_rest(gw):
    parts = []
    for name, rows, _ in _FLATB_PIECES:
        g = gw[name]
        if name == 'w_ple':
            parts.append(_shard_columns(g, PLE_DIM).reshape(N_SHARD, rows, FLAT_W))
        else:
            parts.append(g.reshape(N_SHARD, rows, FLAT_W))
    used = sum(p.shape[1] for p in parts)
    parts.append(jnp.zeros((N_SHARD, FLATB_ROWS - used, FLAT_W), F32))
    return [_shard_columns(gw['w_ffn_in'], FFN_IN_SHARD), jnp.concatenate(parts, axis=1)]


def _unpack_reduced(mine, theirs, c):
    whole = [jnp.concatenate([jnp.where(c == 0, m, t), jnp.where(c == 0, t, m)], axis=0) for m, t in zip(mine, theirs)]
    out = {'w_in': whole[0], 'w_ffn_in': whole[1]}
    offs, _ = _flatb_offsets()
    for name, rows, shape in _FLATB_PIECES:
        out[name] = whole[2][offs[name]:offs[name] + rows].reshape(shape)
    return out


_HBM = pl.BlockSpec(memory_space=pltpu.HBM)


def _place():
    x, y, c = lax.axis_index("x"), lax.axis_index("y"), lax.axis_index("c")
    chips = [(1 - x, y), (x, 1 - y), (1 - x, 1 - y)]
    return x, y, c, chips


def _remote(src, dst, send_sems, recv_sems, k, to):
    return pltpu.make_async_remote_copy(src_ref=src, dst_ref=dst, send_sem=send_sems.at[k], recv_sem=recv_sems.at[k],
                                        device_id=to, device_id_type=_MESH)


def _half_rows(ref, half, hf, lead=None):
    rows = pl.ds(pl.multiple_of(hf * half, 16), half)
    return ref.at[rows, :] if lead is None else ref.at[lead, rows, :]


class _Rider:
    def __init__(self, inputs, out_shape, n_sems, copies, aliases=None):
        self.inputs, self.out_shape, self.n_sems, self.copies = list(inputs), list(out_shape), n_sems, copies
        self.aliases = aliases or {}


def _carried_call(body, rider, first, last, *, name, grid, in_specs, out_specs, out_shape, scratch_shapes, sem, args):
    n_in, n_out, n_scr = len(in_specs), len(out_specs), len(scratch_shapes)
    if rider is None:
        res = pl.pallas_call(body, name=name, grid=grid, in_specs=in_specs, out_specs=out_specs, out_shape=out_shape,
                             scratch_shapes=scratch_shapes, compiler_params=_cparams(sem))(*args)
        return list(res), []
    ri, ro = len(rider.inputs), len(rider.out_shape)

    def full_body(*refs):
        own_in, r_in = refs[:n_in], refs[n_in:n_in + ri]
        o0 = n_in + ri
        own_out, r_out = refs[o0:o0 + n_out], refs[o0 + n_out:o0 + n_out + ro]
        s0 = o0 + n_out + ro
        own_scr, send_sems, recv_sems = refs[s0:s0 + n_scr], refs[s0 + n_scr], refs[s0 + n_scr + 1]

        @pl.when(first())
        def _():
            sends, _ = rider.copies(r_in, r_out, send_sems, recv_sems)
            for cp in sends:
                cp.start()

        body(*own_in, *own_out, *own_scr)

        @pl.when(last())
        def _():
            sends, arrivals = rider.copies(r_in, r_out, send_sems, recv_sems)
            for cp in arrivals():
                cp.wait_recv()
            for cp in sends:
                cp.wait_send()

    res = pl.pallas_call(
        full_body, name=name, grid=grid, in_specs=list(in_specs) + [_HBM] * ri, out_specs=list(out_specs) + [_HBM] * ro,
        out_shape=list(out_shape) + rider.out_shape,
        scratch_shapes=list(scratch_shapes) + [pltpu.SemaphoreType.DMA((rider.n_sems,))] * 2,
        input_output_aliases={n_in + i: n_out + o for i, o in rider.aliases.items()},
        compiler_params=_cparams(sem))(*args, *rider.inputs)
    return list(res[:n_out]), list(res[n_out:])


def _ride_gather_send(bufs):
    n = len(bufs)
    halves = [b.shape[0] // 2 for b in bufs]

    def copies(ins, outs, send_sems, recv_sems):
        x, y, c, chips = _place()
        slot = lambda b, cx, cy: _half_rows(outs[b], halves[b], c, lead=2 * cx + cy)
        sends = [_remote(_half_rows(ins[b], halves[b], c), slot(b, x, y), send_sems, recv_sems, 3 * b + j, (cx, cy, c))
                 for b in range(n) for j, (cx, cy) in enumerate(chips)]
        arrivals = lambda: [_remote(slot(b, cx, cy), slot(b, cx, cy), send_sems, recv_sems, 3 * b + j, (x, y, c))
                            for b in range(n) for j, (cx, cy) in enumerate(chips)]
        return sends, arrivals

    return _Rider(bufs, [jax.ShapeDtypeStruct((N_SHARD,) + b.shape, b.dtype) for b in bufs], 3 * n, copies)


def _ride_gather_pass(gathered):
    n = len(gathered)
    halves = [g.shape[1] // 2 for g in gathered]

    def copies(ins, outs, send_sems, recv_sems):
        x, y, c, chips = _place()
        slot = lambda b, cx, cy, hf: _half_rows(outs[b], halves[b], hf, lead=2 * cx + cy)
        sends = [_remote(slot(b, cx, cy, c), slot(b, cx, cy, c), send_sems, recv_sems, 3 * b + j, (x, y, 1 - c))
                 for b in range(n) for j, (cx, cy) in enumerate(chips)]
        arrivals = lambda: [_remote(slot(b, cx, cy, 1 - c), slot(b, cx, cy, 1 - c), send_sems, recv_sems, 3 * b + j, (x, y, c))
                            for b in range(n) for j, (cx, cy) in enumerate(chips)]
        return sends, arrivals

    return _Rider(gathered, [jax.ShapeDtypeStruct(g.shape, g.dtype) for g in gathered], 3 * n, copies,
                  aliases={b: b for b in range(n)})


def _ride_small_gather(buf):
    def copies(ins, outs, send_sems, recv_sems):
        x, y, c, _ = _place()
        flip = lambda v, d: 1 - v if d else v
        sends, peers = [], []
        for dx in (0, 1):
            for dy in (0, 1):
                for dc in (0, 1):
                    if dx or dy or dc:
                        k = 4 * dx + 2 * dy + dc - 1
                        px, py, pc = flip(x, dx), flip(y, dy), flip(c, dc)
                        sends.append(_remote(ins[0], outs[0].at[4 * x + 2 * y + c], send_sems, recv_sems, k, (px, py, pc)))
                        peers.append((k, 4 * px + 2 * py + pc))
        arrivals = lambda: [_remote(ins[0], outs[0].at[slot], send_sems, recv_sems, k, (x, y, c)) for k, slot in peers]
        return sends, arrivals

    return _Rider([buf], [jax.ShapeDtypeStruct((8,) + buf.shape, buf.dtype)], 7, copies)


def _small_sum(gathered, buf, me_arr):
    n, r, width = gathered.shape

    def body(me_ref, g_ref, b_ref, o_ref):
        total = jnp.zeros((r, width), F32)
        for d in range(n):
            total = total + jnp.where(me_ref[0] == d, b_ref[...], g_ref[d])
        o_ref[...] = total

    return pl.pallas_call(
        body, name="small_sum", out_shape=jax.ShapeDtypeStruct((r, width), F32),
        grid_spec=pltpu.PrefetchScalarGridSpec(
            num_scalar_prefetch=1, grid=(1,),
            in_specs=[pl.BlockSpec((n, r, width), lambda i, me: (0, 0, 0)), pl.BlockSpec((r, width), lambda i, me: (0, 0))],
            out_specs=pl.BlockSpec((r, width), lambda i, me: (0, 0))),
        compiler_params=_cparams(("arbitrary",)))(me_arr, gathered, buf)


def _ride_pair_exchange(gbufs):
    n = len(gbufs)
    halves = [g.shape[1] // 2 for g in gbufs]

    def copies(ins, outs, send_sems, recv_sems):
        x, y, c, _ = _place()
        sends = [_remote(ins[b].at[:, pl.ds(pl.multiple_of((1 - c) * halves[b], 16), halves[b]), :], outs[b],
                         send_sems, recv_sems, b, (x, y, 1 - c)) for b in range(n)]
        arrivals = lambda: [_remote(outs[b], outs[b], send_sems, recv_sems, b, (x, y, c)) for b in range(n)]
        return sends, arrivals

    return _Rider(gbufs, [jax.ShapeDtypeStruct((N_SHARD, h, g.shape[2]), g.dtype) for g, h in zip(gbufs, halves)], n, copies)


def _ride_chip_exchange(parts):
    n = len(parts)

    def copies(ins, outs, send_sems, recv_sems):
        x, y, c, chips = _place()
        sends = [_remote(ins[b].at[2 * cx + cy], outs[b].at[j], send_sems, recv_sems, 3 * b + j, (cx, cy, c))
                 for b in range(n) for j, (cx, cy) in enumerate(chips)]
        arrivals = lambda: [_remote(ins[b].at[0], outs[b].at[j], send_sems, recv_sems, 3 * b + j, (x, y, c))
                            for b in range(n) for j in range(len(chips))]
        return sends, arrivals

    return _Rider(parts, [jax.ShapeDtypeStruct((3,) + p.shape[1:], p.dtype) for p in parts], 3 * n, copies)


def _gather_shards(bufs, name):
    n = len(bufs)
    halves = [b.shape[0] // 2 for b in bufs]

    def body(*refs):
        ins, outs, send_sems, recv_sems = refs[:n], refs[n:2 * n], refs[2 * n], refs[2 * n + 1]
        x, y, c, chips = _place()
        me, sibling = (x, y, c), (x, y, 1 - c)
        slot = lambda b, cx, cy, hf: _half_rows(outs[b], halves[b], hf, lead=2 * cx + cy)
        first = [_remote(_half_rows(ins[b], halves[b], c), slot(b, x, y, c), send_sems, recv_sems, 6 * b + j, (cx, cy, c))
                 for b in range(n) for j, (cx, cy) in enumerate(chips)]
        for cp in first:
            cp.start()
        passed = []
        for j, (cx, cy) in enumerate(chips):
            for b in range(n):
                _remote(slot(b, cx, cy, c), slot(b, cx, cy, c), send_sems, recv_sems, 6 * b + j, me).wait_recv()
                fwd = _remote(slot(b, cx, cy, c), slot(b, cx, cy, c), send_sems, recv_sems, 6 * b + 3 + j, sibling)
                fwd.start()
                passed.append(fwd)
        for j, (cx, cy) in enumerate(chips):
            for b in range(n):
                _remote(slot(b, cx, cy, 1 - c), slot(b, cx, cy, 1 - c), send_sems, recv_sems, 6 * b + 3 + j, me).wait_recv()
        for cp in first + passed:
            cp.wait_send()

    return pl.pallas_call(
        body, name=name, out_shape=[jax.ShapeDtypeStruct((N_SHARD,) + b.shape, b.dtype) for b in bufs],
        in_specs=[_HBM] * n, out_specs=[_HBM] * n,
        scratch_shapes=[pltpu.SemaphoreType.DMA((6 * n,)), pltpu.SemaphoreType.DMA((6 * n,))],
    )(*bufs)


def _reduce_pair_exchange(gbufs, name):
    n = len(gbufs)
    halves = [g.shape[1] // 2 for g in gbufs]

    def body(*refs):
        ins, outs, send_sems, recv_sems = refs[:n], refs[n:2 * n], refs[2 * n], refs[2 * n + 1]
        x, y, c, _ = _place()
        cps = [_remote(ins[b].at[:, pl.ds(pl.multiple_of((1 - c) * halves[b], 16), halves[b]), :], outs[b],
                       send_sems, recv_sems, b, (x, y, 1 - c)) for b in range(n)]
        for cp in cps:
            cp.start()
        for cp in cps:
            cp.wait()

    return pl.pallas_call(
        body, name=name,
        out_shape=[jax.ShapeDtypeStruct((N_SHARD, h, g.shape[2]), g.dtype) for g, h in zip(gbufs, halves)],
        in_specs=[_HBM] * n, out_specs=[_HBM] * n,
        scratch_shapes=[pltpu.SemaphoreType.DMA((n,)), pltpu.SemaphoreType.DMA((n,))],
    )(*gbufs)


def _pair_add(gbuf, recv, c_arr, tr, name):
    _, rows, width = gbuf.shape
    half = rows // 2
    nt = half // tr

    def body(c_ref, a_ref, b_ref, o_ref):
        o_ref[...] = (a_ref[...] + b_ref[...]).astype(o_ref.dtype)

    blk = lambda f: pl.BlockSpec((None, tr, width), f)
    return pl.pallas_call(
        body, name=name, out_shape=jax.ShapeDtypeStruct((N_SHARD, half, width), jnp.bfloat16),
        grid_spec=pltpu.PrefetchScalarGridSpec(
            num_scalar_prefetch=1, grid=(N_SHARD, nt),
            in_specs=[blk(lambda s, i, c: (s, c[0] * nt + i, 0)), blk(lambda s, i, c: (s, i, 0))],
            out_specs=blk(lambda s, i, c: (s, i, 0))),
        compiler_params=_cparams(("parallel", "parallel")))(c_arr, gbuf, recv)


def _chip_add(part, recv, me_arr, tr, name):
    _, half, width = part.shape

    def body(me_ref, own, a0, a1, a2, o_ref):
        f = lambda r: r[...].astype(F32)
        o_ref[...] = ((f(own) + f(a0)) + f(a1)) + f(a2)

    specs = [pl.BlockSpec((None, tr, width), lambda i, me: (me[0], i, 0))]
    specs += [pl.BlockSpec((None, tr, width), functools.partial(lambda i, me, k: (k, i, 0), k=k)) for k in range(3)]
    return pl.pallas_call(
        body, name=name, out_shape=jax.ShapeDtypeStruct((half, width), F32),
        grid_spec=pltpu.PrefetchScalarGridSpec(
            num_scalar_prefetch=1, grid=(half // tr,), in_specs=specs,
            out_specs=pl.BlockSpec((tr, width), lambda i, me: (i, 0))),
        compiler_params=_cparams(("parallel",)))(me_arr, part, recv, recv, recv)


def _reduce_pair_share(rhalves, name):
    n = len(rhalves)

    def body(*refs):
        ins, outs, send_sems, recv_sems = refs[:n], refs[n:2 * n], refs[2 * n], refs[2 * n + 1]
        x, y, c, _ = _place()
        cps = [_remote(ins[b], outs[b], send_sems, recv_sems, b, (x, y, 1 - c)) for b in range(n)]
        for cp in cps:
            cp.start()
        for cp in cps:
            cp.wait()

    return pl.pallas_call(
        body, name=name, out_shape=[jax.ShapeDtypeStruct(r.shape, r.dtype) for r in rhalves],
        in_specs=[_HBM] * n, out_specs=[_HBM] * n,
        scratch_shapes=[pltpu.SemaphoreType.DMA((n,)), pltpu.SemaphoreType.DMA((n,))],
    )(*rhalves)


def _row_tile(rows, cap):
    if rows <= cap:
        return rows
    t = (cap // 8) * 8
    while t >= 8:
        if rows % t == 0:
            return t
        t -= 8
    return rows


def _adamw(w, g, m, v, name):
    shape = w.shape
    cols = shape[-1] if len(shape) <= 3 else shape[-2] * shape[-1]
    lead = len(shape) == 3
    w2, g2, m2, v2 = (a if lead else a.reshape(-1, cols) for a in (w, g, m, v))
    rows = shape[1] if lead else w2.shape[0]
    tr, tc = _row_tile(rows, 256), cols
    if tr == rows and rows > 256:
        tc = _tile(cols, 256)

    def body(w_ref, g_ref, m_ref, v_ref, d_ref, mo_ref, vo_ref):
        gv = g_ref[...]
        mn = ADAM_B1 * m_ref[...] + (1.0 - ADAM_B1) * gv
        vn = ADAM_B2 * v_ref[...] + (1.0 - ADAM_B2) * (gv * gv)
        m_hat = mn / (1.0 - ADAM_B1 ** ADAM_STEP)
        v_hat = vn / (1.0 - ADAM_B2 ** ADAM_STEP)
        d_ref[...] = -ADAM_LR * (m_hat / (jnp.sqrt(v_hat) + ADAM_EPS) + ADAM_WD * w_ref[...])
        mo_ref[...] = mn
        vo_ref[...] = vn

    blk = (pl.BlockSpec((None, tr, tc), lambda i, j: (0, i, j)) if lead else pl.BlockSpec((tr, tc), lambda i, j: (i, j)))
    outs = pl.pallas_call(
        body, name=name, grid=(rows // tr, cols // tc), in_specs=[blk] * 4, out_specs=[blk] * 3,
        out_shape=[jax.ShapeDtypeStruct(w2.shape, F32)] * 3,
        compiler_params=_cparams(("parallel", "parallel")))(w2, g2, m2, v2)
    return tuple(o.reshape(shape) for o in outs)


_WEIGHT_NAMES = ('w_in', 'conv_w', 'dn_a_log', 'dn_dt_bias', 'dn_norm_w', 'q_norm_w', 'w_uq', 'kv_norm_w', 'w_uk',
                 'w_uv', 'w_br_dn', 'w_br_mla', 'w_o', 'ln1_g', 'ln1_b', 'w_ffn_in', 'w_ffn_out', 'w_ple',
                 'w_ple_gate', 'ln2_g', 'ln2_b')
_SMALL_NAMES = ('ln1_g', 'ln1_b', 'ln2_g', 'ln2_b', 'q_norm_w', 'kv_norm_w', 'dn_norm_w', 'dn_a_log', 'dn_dt_bias')
_SMALL_GROUP = 8
_CONV_SMALL_ROW = len(_SMALL_NAMES) * _SMALL_GROUP
_CONV_SMALL_ROWS = DN_CONV * QKV_W // FLAT_W


_LOSS_SMALL_ROW = _CONV_SMALL_ROW + 16


def _pack_small(gw, loss_lanes):
    rows = [jnp.pad(gw[n][None, :], ((0, _SMALL_GROUP - 1), (0, FLAT_W - gw[n].shape[0]))) for n in _SMALL_NAMES]
    rows.append(jnp.pad(gw['conv_w'].reshape(_CONV_SMALL_ROWS, FLAT_W), ((0, 16 - _CONV_SMALL_ROWS), (0, 0))))
    rows.append(jnp.pad(loss_lanes, ((0, _SMALL_GROUP - 1), (0, FLAT_W - LANE))))
    return jnp.concatenate(rows, axis=0)


class _Exchange:
    def __init__(self, local, me_chip, c_arr):
        self.local, self.me_chip, self.c_arr = local, me_chip, c_arr
        self.parts = self.arrived = None

    def gather_send(self):
        return _ride_gather_send(self.local)

    def gather_pass(self, sent):
        return _ride_gather_pass(sent)

    def weights(self, gathered):
        return _prep_weights(_unpack_rest(gathered, self.local, self.me_chip))

    def pair_send(self, g):
        self.gbufs = _pack_grads_rest(_unprep_grads_early(g))
        return _ride_pair_exchange(self.gbufs)

    def reduce_send(self, got):
        self.parts = [_pair_add(g_, r_, self.c_arr, tr, "pair_add_%d" % (i + 1))
                      for i, (g_, r_, tr) in enumerate(zip(self.gbufs, got, _ADD_TILES[1:]))]
        return _ride_chip_exchange(self.parts)

    def reduce_arrived(self, arrived):
        self.arrived = list(arrived)

    def in_send(self, g):
        g_in = [_shard_columns(_w_in_grad(g), W_IN_SHARD)]
        got = _reduce_pair_exchange(g_in, "reduce_pair_exchange_w_in")
        self.part_in = _pair_add(g_in[0], got[0], self.c_arr, _ADD_TILES[0], "pair_add_0")
        return _ride_chip_exchange([self.part_in])

    def in_arrived(self, arrived):
        self.arrived_in = list(arrived)

    def small_send(self, g):
        self.small = _pack_small(_unprep_grads_late(g), g['loss_lanes'])
        return _ride_small_gather(self.small)

    def small_arrived(self, arrived):
        self.small_gathered = arrived[0]


def kernel(x, p, positions, w_in, conv_w, dn_a_log, dn_dt_bias, dn_norm_w, q_norm_w, w_uq, kv_norm_w, w_uk, w_uv, w_br_dn, w_br_mla, w_o, ln1_g, ln1_b, w_ffn_in, w_ffn_out, w_ple, w_ple_gate, ln2_g, ln2_b, loss_target, m_w_in, m_conv_w, m_dn_a_log, m_dn_dt_bias, m_dn_norm_w, m_q_norm_w, m_w_uq, m_kv_norm_w, m_w_uk, m_w_uv, m_w_br_dn, m_w_br_mla, m_w_o, m_ln1_g, m_ln1_b, m_w_ffn_in, m_w_ffn_out, m_w_ple, m_w_ple_gate, m_ln2_g, m_ln2_b, v_w_in, v_conv_w, v_dn_a_log, v_dn_dt_bias, v_dn_norm_w, v_q_norm_w, v_w_uq, v_kv_norm_w, v_w_uk, v_w_uv, v_w_br_dn, v_w_br_mla, v_w_o, v_ln1_g, v_ln1_b, v_w_ffn_in, v_w_ffn_out, v_w_ple, v_w_ple_gate, v_ln2_g, v_ln2_b):
    ws = dict(w_in=w_in, conv_w=conv_w, dn_a_log=dn_a_log, dn_dt_bias=dn_dt_bias, dn_norm_w=dn_norm_w, q_norm_w=q_norm_w,
              w_uq=w_uq, kv_norm_w=kv_norm_w, w_uk=w_uk, w_uv=w_uv, w_br_dn=w_br_dn, w_br_mla=w_br_mla, w_o=w_o,
              ln1_g=ln1_g, ln1_b=ln1_b, w_ffn_in=w_ffn_in, w_ffn_out=w_ffn_out, w_ple=w_ple, w_ple_gate=w_ple_gate,
              ln2_g=ln2_g, ln2_b=ln2_b)
    ms = dict(w_in=m_w_in, conv_w=m_conv_w, dn_a_log=m_dn_a_log, dn_dt_bias=m_dn_dt_bias, dn_norm_w=m_dn_norm_w,
              q_norm_w=m_q_norm_w, w_uq=m_w_uq, kv_norm_w=m_kv_norm_w, w_uk=m_w_uk, w_uv=m_w_uv, w_br_dn=m_w_br_dn,
              w_br_mla=m_w_br_mla, w_o=m_w_o, ln1_g=m_ln1_g, ln1_b=m_ln1_b, w_ffn_in=m_w_ffn_in, w_ffn_out=m_w_ffn_out,
              w_ple=m_w_ple, w_ple_gate=m_w_ple_gate, ln2_g=m_ln2_g, ln2_b=m_ln2_b)
    vs = dict(w_in=v_w_in, conv_w=v_conv_w, dn_a_log=v_dn_a_log, dn_dt_bias=v_dn_dt_bias, dn_norm_w=v_dn_norm_w,
              q_norm_w=v_q_norm_w, w_uq=v_w_uq, kv_norm_w=v_kv_norm_w, w_uk=v_w_uk, w_uv=v_w_uv, w_br_dn=v_w_br_dn,
              w_br_mla=v_w_br_mla, w_o=v_w_o, ln1_g=v_ln1_g, ln1_b=v_ln1_b, w_ffn_in=v_w_ffn_in, w_ffn_out=v_w_ffn_out,
              w_ple=v_w_ple, w_ple_gate=v_w_ple_gate, ln2_g=v_ln2_g, ln2_b=v_ln2_b)
    mx, my, mc = lax.axis_index("x"), lax.axis_index("y"), lax.axis_index("c")

    me_chip = 2 * mx + my
    c_arr = jnp.reshape(mc, (1,)).astype(jnp.int32)
    me_arr = jnp.reshape(me_chip, (1,)).astype(jnp.int32)
    sharded = ('w_in', 'w_ffn_in') + tuple(name for name, _, _ in _FLATB_PIECES)

    local = _pack_shards({name: ws[name][0] for name in sharded}, conv_w[0])
    (gathered_in,) = _gather_shards(local[:1], "gather_w_in")
    w_in_full, conv_full = _unpack_w_in(gathered_in, local[0], me_chip)
    small = {n: ws[n][0] for n in _SMALL_NAMES}
    small['conv_w'] = conv_full
    sp = _prep_small(small)
    cosb, sinb = _rope_tables(positions[0])
    exch = _Exchange(local[1:], me_chip, c_arr)

    loss_lanes, dx, g = _local_step(x[0], p[0, 0], cosb, sinb, loss_target[0], _prep_w_in(w_in_full), sp, exch)

    parts = [exch.part_in] + exch.parts
    arrived = exch.arrived_in + exch.arrived
    mine = [_chip_add(p_, r_, me_arr, tr, "chip_add_%d" % i) for i, (p_, r_, tr) in enumerate(zip(parts, arrived, _ADD_TILES))]
    reduced = _unpack_reduced(mine, _reduce_pair_share(mine, "reduce_pair_share"), mc)
    tot = _small_sum(exch.small_gathered, exch.small, jnp.reshape(4 * mx + 2 * my + mc, (1,)).astype(jnp.int32))
    loss = jnp.sum(tot[_LOSS_SMALL_ROW, :LANE])
    gred = {name: reduced[name][None] for name in sharded}
    for i, n in enumerate(_SMALL_NAMES):
        gred[n] = tot[i * _SMALL_GROUP, :ws[n].shape[1]][None]
    conv_tot = tot[_CONV_SMALL_ROW:_CONV_SMALL_ROW + _CONV_SMALL_ROWS].reshape(DN_CONV, QKV_W)
    gred['conv_w'] = lax.dynamic_slice_in_dim(conv_tot, (2 * mx + my) * _CONV_SHARD, _CONV_SHARD, axis=1)[None]

    deltas, new_m, new_v = {}, {}, {}
    for n in _WEIGHT_NAMES:
        if n == 'w_in':
            tr_ = lambda a: jnp.transpose(a, (0, 2, 1))
            g_t = tr_(gred[n].reshape(ws[n].shape))
            outs = _adamw(tr_(ws[n]), g_t, tr_(ms[n]), tr_(vs[n]), "adamw_" + n)
            gred[n] = tr_(g_t)
            deltas[n], new_m[n], new_v[n] = (tr_(o) for o in outs)
            continue
        gred[n] = gred[n].reshape(ws[n].shape)
        deltas[n], new_m[n], new_v[n] = _adamw(ws[n], gred[n], ms[n], vs[n], "adamw_" + n)
    return (loss, dx[None], *[gred[n] for n in _WEIGHT_NAMES], *[deltas[n] for n in _WEIGHT_NAMES],
            *[new_m[n] for n in _WEIGHT_NAMES], *[new_v[n] for n in _WEIGHT_NAMES])
```

```python
import functools

import jax
import jax.numpy as jnp
from jax import lax
from jax.experimental import pallas as pl
from jax.experimental.pallas import tpu as pltpu

F32 = jnp.float32
_CDT = jnp.bfloat16
_MESH = pl.DeviceIdType.MESH

D_MODEL = 1024
PLE_DIM = 256
HEADS = 8
DN_DK = 128
DN_CHUNK = 64
DN_CONV = 4
QKV_W = 3 * HEADS * DN_DK
Q_LORA = 384
KV_LORA = 256
NOPE = 128
ROPE = 64
ROPE_PAD = 128
FFN_HIDDEN = 2816
D_IN = 6864
ROPE_BASE = 10000.0
ALPHA = 2.0 ** 0.25
ATT_SCALE = (NOPE + ROPE) ** -0.5
NEG_BIG = -1e30
ADAM_LR, ADAM_B1, ADAM_B2, ADAM_EPS, ADAM_WD, ADAM_STEP = 0.001, 0.9, 0.999, 1e-08, 0.01, 10

LANE = 128
VMEM_LIMIT = 56 * 1024 * 1024
MM_VMEM_BUDGET = 40 * 1024 * 1024
N_SHARD = 4
FLAT_W = 1024
SMALL_ROWS = 96


def _tile(dim, cap):
    if dim <= cap:
        return dim
    t = (cap // LANE) * LANE
    while t >= LANE:
        if dim % t == 0:
            return t
        t -= LANE
    return dim


def _cparams(sem):
    return pltpu.CompilerParams(dimension_semantics=sem, vmem_limit_bytes=VMEM_LIMIT)


def _mm(a, b, *, name, ta=False, tb=False, add=None, add_scale=1.0, out_dtype=F32, heads=None,
        a_head=None, b_head=None, out_head=None, dims=None, tm=1408, tn=1408, rider=None):
    m, n, k = dims
    tm, tn = _tile(m, tm), _tile(n, tn)
    sa, sb, so = a.dtype.itemsize, b.dtype.itemsize, jnp.dtype(out_dtype).itemsize

    def vmem_need(tk_):
        acc = tm * tn * 4 if tk_ < k else 0
        extra = 2 * tm * tn * 4 if add is not None else 0
        return 2 * (tm * tk_ * sa + tk_ * tn * sb) + 2 * tm * tn * so + acc + extra

    tk = k
    while vmem_need(tk) > MM_VMEM_BUDGET and tk > LANE:
        smaller = _tile(k, tk - LANE)
        if smaller >= tk:
            break
        tk = smaller
    nk = k // tk
    hgrid = () if heads is None else (heads,)
    off = len(hgrid)

    def spec(rows, cols, rtile, ctile, rsel, csel, layout):
        def idx(*g):
            h = g[0] if off else 0
            ri, ci = g[off + rsel], g[off + csel]
            if layout == 'lead':
                return (h, ri, ci)
            if layout == 'col':
                return (ri, h * (cols // ctile) + ci)
            return (ri, ci)
        if layout == 'lead':
            return pl.BlockSpec((None, rtile, ctile), idx)
        return pl.BlockSpec((rtile, ctile), idx)

    a_spec = spec(k, m, tk, tm, 2, 0, a_head) if ta else spec(m, k, tm, tk, 0, 2, a_head)
    b_spec = spec(n, k, tn, tk, 1, 2, b_head) if tb else spec(k, n, tk, tn, 2, 1, b_head)
    o_spec = spec(m, n, tm, tn, 0, 1, out_head)
    in_specs = [a_spec, b_spec]
    args = [a, b]
    if add is not None:
        in_specs.append(spec(m, n, tm, tn, 0, 1, out_head))
        args.append(add)
    dn = (((0 if ta else 1,), (1 if tb else 0,)), ((), ()))

    def body(*refs):
        a_ref, b_ref = refs[0], refs[1]
        prod = lax.dot_general(a_ref[...].astype(_CDT), b_ref[...].astype(_CDT), dn, preferred_element_type=F32)
        if nk == 1:
            o_ref = refs[-1]
            if add is not None:
                prod = prod + refs[2][...].astype(F32) * add_scale
            o_ref[...] = prod.astype(out_dtype)
            return
        o_ref, acc_ref = refs[-2], refs[-1]
        kk = pl.program_id(off + 2)

        @pl.when(kk == 0)
        def _():
            if add is not None:
                acc_ref[...] = refs[2][...].astype(F32) * add_scale
            else:
                acc_ref[...] = jnp.zeros_like(acc_ref)

        acc_ref[...] += prod

        @pl.when(kk == nk - 1)
        def _():
            o_ref[...] = acc_ref[...].astype(out_dtype)

    if out_head == 'lead':
        oshape = (heads, m, n)
    elif out_head == 'col':
        oshape = (m, heads * n)
    else:
        oshape = (m, n)
    grid = hgrid + (m // tm, n // tn, nk)
    scratch = [pltpu.VMEM((tm, tn), F32)] if nk > 1 else []
    if rider is not None:
        (out,), carried = _carried_call(
            body, rider, *_grid_ends(grid), name=name, grid=grid, in_specs=in_specs, out_specs=[o_spec],
            out_shape=[jax.ShapeDtypeStruct(oshape, out_dtype)], scratch_shapes=scratch,
            sem=("arbitrary",) * len(grid), args=tuple(args))
        return out, carried
    sem = ("parallel",) * (off + 2) + ("arbitrary",)
    return pl.pallas_call(
        body, name=name, grid=grid, in_specs=in_specs, out_specs=o_spec,
        out_shape=jax.ShapeDtypeStruct(oshape, out_dtype), scratch_shapes=scratch,
        compiler_params=_cparams(sem))(*args)


def _mm2(a, b, **kw):
    ta, tb = kw.get('ta', False), kw.get('tb', False)
    m = a.shape[1] if ta else a.shape[0]
    k = a.shape[0] if ta else a.shape[1]
    n = b.shape[0] if tb else b.shape[1]
    return _mm(a, b, dims=(m, n, k), **kw)


def _rowwise(fn, rows, bcast, outs, reds=(), *, name, tm=256, heads=None):
    t = rows[0][0].shape[0]
    tm = min(tm, t)
    hn = 1 if heads is None else heads
    in_specs, args = [], []
    for arr, width, base, per_head in rows:
        in_specs.append(pl.BlockSpec((tm, width), functools.partial(
            lambda i, h, base, per_head: (i, base + (h if per_head else 0)), base=base, per_head=per_head)))
        args.append(arr)
    for arr in bcast:
        in_specs.append(pl.BlockSpec(arr.shape, lambda i, h: (0, 0)))
        args.append(arr)
    out_specs, out_shape = [], []
    for total, width, per_head, dt in outs:
        out_specs.append(pl.BlockSpec((tm, width), functools.partial(
            lambda i, h, per_head: (i, h if per_head else 0), per_head=per_head)))
        out_shape.append(jax.ShapeDtypeStruct((t, total), dt))
    for shp in reds:
        out_specs.append(pl.BlockSpec(shp, lambda i, h: (0, 0)))
        out_shape.append(jax.ShapeDtypeStruct(shp, F32))
    n_in, n_out, n_red = len(args), len(outs), len(reds)

    def body(*refs):
        i, h = pl.program_id(0), pl.program_id(1)
        vals = fn(h, *[r[...] for r in refs[:n_in]])
        for r, v in zip(refs[n_in:n_in + n_out], vals[:n_out]):
            r[...] = v.astype(r.dtype)
        if n_red:
            @pl.when((i == 0) & (h == 0))
            def _():
                for r in refs[n_in + n_out:]:
                    r[...] = jnp.zeros_like(r)
            for r, v in zip(refs[n_in + n_out:], vals[n_out:]):
                r[...] += v

    sem = ("arbitrary", "arbitrary") if n_red else ("parallel", "parallel")
    res = pl.pallas_call(body, name=name, grid=(t // tm, hn), in_specs=in_specs, out_specs=out_specs,
                         out_shape=out_shape, compiler_params=_cparams(sem))(*args)
    return tuple(res)


def _sigmoid(x):
    return 1.0 / (1.0 + jnp.exp(-x))


def _silu(x):
    return x * _sigmoid(x)


def _softplus(x):
    return jnp.maximum(x, 0.0) + jnp.log(1.0 + jnp.exp(-jnp.abs(x)))


def _layer_norm(t, g, b):
    mu = jnp.mean(t, axis=-1, keepdims=True)
    d = t - mu
    var = jnp.mean(d * d, axis=-1, keepdims=True)
    return d * lax.rsqrt(var + 1e-5) * g + b


def _rms_norm(t, w):
    return t * lax.rsqrt(jnp.mean(t * t, axis=-1, keepdims=True) + 1e-6) * w


def _swap_rope_halves(t):
    lane = lax.broadcasted_iota(jnp.int32, t.shape, 1) % ROPE_PAD
    n = t.shape[1]
    up = pltpu.roll(t, n - ROPE // 2, axis=1)
    dn = pltpu.roll(t, ROPE // 2, axis=1)
    return jnp.where(lane < ROPE // 2, up, jnp.where(lane < ROPE, dn, 0.0))


def _rope(t, cosb, sinb):
    reps = t.shape[1] // ROPE_PAD
    c = jnp.tile(cosb, (1, reps)) if reps > 1 else cosb
    s = jnp.tile(sinb, (1, reps)) if reps > 1 else sinb
    return t * c + _swap_rope_halves(t) * s


def _rope_bwd(d, cosb, sinb):
    reps = d.shape[1] // ROPE_PAD
    c = jnp.tile(cosb, (1, reps)) if reps > 1 else cosb
    s = jnp.tile(sinb, (1, reps)) if reps > 1 else sinb
    return d * c + _swap_rope_halves(d * s)


_CONV_ROWS = 256
_CONV_COLS = 256


def _conv_window(ref, r0, lo, hi, t):
    parts = []
    start, stop = r0 - lo, r0 + _CONV_ROWS + hi
    if start < 0:
        parts.append(jnp.zeros((-start, ref.shape[1]), F32))
        start = 0
    tail = max(stop - t, 0)
    parts.append(ref[start:stop - tail, :].astype(F32))
    if tail:
        parts.append(jnp.zeros((tail, ref.shape[1]), F32))
    return parts[0] if len(parts) == 1 else jnp.concatenate(parts, axis=0)


def _conv_taps(win, w_ref, n_out):
    acc = win[8:8 + n_out] * w_ref[DN_CONV - 1:DN_CONV, :]
    for i in range(DN_CONV - 1):
        acc = acc + pltpu.roll(win, DN_CONV - 1 - i, axis=0)[8:8 + n_out] * w_ref[i:i + 1, :]
    return acc


def _conv_silu(x, w):
    t, ch = x.shape

    def body(x_ref, w_ref, o_ref):
        for r in range(t // _CONV_ROWS):
            r0 = r * _CONV_ROWS
            c = _conv_taps(_conv_window(x_ref, r0, 8, 0, t), w_ref, _CONV_ROWS)
            o_ref[r0:r0 + _CONV_ROWS, :] = _silu(c)

    return pl.pallas_call(
        body, name="conv_silu", grid=(ch // _CONV_COLS,),
        in_specs=[pl.BlockSpec((t, _CONV_COLS), lambda j: (0, j)), pl.BlockSpec((DN_CONV, _CONV_COLS), lambda j: (0, j))],
        out_specs=pl.BlockSpec((t, _CONV_COLS), lambda j: (0, j)),
        out_shape=jax.ShapeDtypeStruct((t, ch), F32), compiler_params=_cparams(("parallel",)))(x, w)


def _conv_silu_bwd(x, w, dys):
    t, ch = x.shape
    per = ch // len(dys) // _CONV_COLS

    def body(x_ref, w_ref, *rest):
        dy_refs, (dx_ref, dw_ref) = rest[:len(dys)], rest[len(dys):]
        sec = pl.program_id(0) // per
        dws = [jnp.zeros((1, _CONV_COLS), F32) for _ in range(DN_CONV)]
        for r in range(t // _CONV_ROWS):
            r0 = r * _CONV_ROWS
            n_ext = _CONV_ROWS + 8
            xw = _conv_window(x_ref, r0, 8, 8, t)
            c = _conv_taps(xw, w_ref, n_ext)
            sg = _sigmoid(c)
            dy = _conv_window(dy_refs[-1], r0, 0, 8, t)
            for k in range(len(dys) - 2, -1, -1):
                dy = jnp.where(sec == k, _conv_window(dy_refs[k], r0, 0, 8, t), dy)
            ds = dy * (sg * (1.0 + c * (1.0 - sg)))
            x0 = xw[8:8 + _CONV_ROWS]
            dx = jnp.zeros((_CONV_ROWS, _CONV_COLS), F32)
            for i in range(DN_CONV):
                sh = DN_CONV - 1 - i
                ds_up = (ds if sh == 0 else pltpu.roll(ds, n_ext - sh, axis=0))[:_CONV_ROWS]
                dx = dx + ds_up * w_ref[i:i + 1, :]
                dws[i] = dws[i] + jnp.sum(x0 * ds_up, axis=0, keepdims=True)
            dx_ref[r0:r0 + _CONV_ROWS, :] = dx.astype(dx_ref.dtype)
        for i in range(DN_CONV):
            dw_ref[i:i + 1, :] = dws[i]

    blk = pl.BlockSpec((t, _CONV_COLS), lambda j: (0, j))
    wblk = pl.BlockSpec((DN_CONV, _CONV_COLS), lambda j: (0, j))
    dy_specs = [pl.BlockSpec((t, _CONV_COLS), functools.partial(lambda j, k: (0, jnp.clip(j - k * per, 0, per - 1)), k=k))
                for k in range(len(dys))]
    return pl.pallas_call(
        body, name="conv_silu_bwd", grid=(ch // _CONV_COLS,), in_specs=[blk, wblk] + dy_specs, out_specs=[blk, wblk],
        out_shape=[jax.ShapeDtypeStruct((t, ch), _CDT), jax.ShapeDtypeStruct((DN_CONV, ch), F32)],
        compiler_params=_cparams(("arbitrary",)))(x, w, *dys)


_PA_ROWS = 1024
_PA_ROWS_FWD = 1024


def _bmm(a, b, spec):
    return jnp.einsum(spec, a.astype(_CDT), b.astype(_CDT), preferred_element_type=F32)


def _split16(a):
    hi = a.astype(jnp.bfloat16)
    return hi, (a - hi.astype(F32)).astype(jnp.bfloat16)


def _bmm3(a, b, spec):
    ah, al = _split16(a)
    bh, bl = _split16(b)
    e = lambda p, q: jnp.einsum(spec, p, q, preferred_element_type=F32)
    return e(ah, bh) + (e(ah, bl) + e(al, bh))


def _split3(b):
    b0 = b.astype(jnp.bfloat16)
    r1 = b - b0.astype(F32)
    b1 = r1.astype(jnp.bfloat16)
    return b0, b1, (r1 - b1.astype(F32)).astype(jnp.bfloat16)


@functools.partial(jax.custom_vjp, nondiff_argnums=(2, 3))
def _select_mm(sel, b, spec, spec_t):
    return sum(jnp.einsum(spec, sel, t, preferred_element_type=F32) for t in _split3(b))


def _select_mm_fwd(sel, b, spec, spec_t):
    return _select_mm(sel, b, spec, spec_t), sel


def _select_mm_bwd(spec, spec_t, sel, ct):
    return jnp.zeros_like(sel), sum(jnp.einsum(spec_t, sel, t, preferred_element_type=F32) for t in _split3(ct))


_select_mm.defvjp(_select_mm_fwd, _select_mm_bwd)


def _tri_inverse(l_mat, eye):
    pw = -l_mat
    t_inv = eye + pw
    for _ in range(5):
        pw = _bmm3(pw, pw, 'bij,bjk->bik')
        t_inv = t_inv + _bmm3(t_inv, pw, 'bij,bjk->bik')
    return t_inv


@jax.custom_vjp
def _tri_inverse_saved(l_mat, t_saved):
    return t_saved


def _tri_inverse_saved_fwd(l_mat, t_saved):
    return t_saved, t_saved


def _tri_inverse_saved_bwd(t_saved, dt):
    left = _bmm3(t_saved, dt, 'bji,bjk->bik')
    return -_bmm3(left, t_saved, 'bij,bkj->bik'), jnp.zeros_like(t_saved)


_tri_inverse_saved.defvjp(_tri_inverse_saved_fwd, _tri_inverse_saved_bwd)


def _phase_a(h, q, k, v, ba, alog, dtb, t_saved=None):
    r = q.shape[0]
    nb = r // DN_CHUNK
    c = DN_CHUNK
    lane = lax.broadcasted_iota(jnp.int32, (1, LANE), 1)
    selb = (lane == h).astype(F32)
    sela = (lane == h + HEADS).astype(F32)
    b_raw = jnp.sum(ba * selb, axis=1, keepdims=True)
    a_raw = jnp.sum(ba * sela, axis=1, keepdims=True)
    al = jnp.sum(alog * selb, axis=1, keepdims=True)
    dt = jnp.sum(dtb * selb, axis=1, keepdims=True)
    beta = jnp.broadcast_to(_sigmoid(b_raw), (r, LANE))
    g = jnp.broadcast_to(-jnp.exp(al) * _softplus(a_raw + dt), (r, LANE))
    qn = q * lax.rsqrt(jnp.sum(q * q, -1, keepdims=True) + 1e-6) * (DN_DK ** -0.5)
    kn = k * lax.rsqrt(jnp.sum(k * k, -1, keepdims=True) + 1e-6)
    q3, k3, v3 = qn.reshape(nb, c, LANE), kn.reshape(nb, c, LANE), v.reshape(nb, c, LANE)
    b3, g3 = beta.reshape(nb, c, LANE), g.reshape(nb, c, LANE)
    ri = lax.broadcasted_iota(jnp.int32, (nb, c, c), 1)
    ci = lax.broadcasted_iota(jnp.int32, (nb, c, c), 2)
    tril, strict = ri >= ci, ri > ci
    gc = _select_mm(tril.astype(jnp.bfloat16), g3, 'bij,bjd->bid', 'bij,bid->bjd')
    onehot = (lax.broadcasted_iota(jnp.int32, (nb, c, LANE), 2) == 0).astype(jnp.bfloat16)
    g_row = _select_mm(onehot, gc, 'bid,bjd->bij', 'bid,bij->bjd')
    diff = gc[:, :, :c] - g_row
    decay = jnp.where(tril, jnp.exp(jnp.where(tril, diff, 0.0)), 0.0)
    kb = k3 * b3
    l_mat = jnp.where(strict, _bmm(kb, k3, 'bid,bjd->bij') * decay, 0.0)
    if t_saved is None:
        t_inv = _tri_inverse(l_mat, (ri == ci).astype(F32))
    else:
        t_inv = _tri_inverse_saved(l_mat, t_saved.reshape(nb, c, c))
    eg = jnp.exp(gc)
    u = _bmm(t_inv, v3 * b3, 'bij,bje->bie')
    w = _bmm(t_inv, kb * eg, 'bij,bje->bie')
    intra = jnp.where(tril, _bmm(q3, k3, 'bid,bjd->bij') * decay, 0.0)
    qd = q3 * eg
    gl = jnp.sum(g3, axis=1, keepdims=True)
    kt = k3 * jnp.exp(gl - gc)
    outs = (u.reshape(r, LANE), w.reshape(r, LANE), qd.reshape(r, LANE), kt.reshape(r, LANE),
            intra.reshape(r, c), gl.reshape(nb, LANE))
    if t_saved is not None:
        return outs
    qd2 = qd - _bmm(intra, w, 'bij,bjd->bid')
    au = _bmm(intra, u, 'bij,bje->bie')
    return outs + (t_inv.reshape(r, c), qd2.reshape(r, LANE), au.reshape(r, LANE))


def _pa_specs(t, rows):
    rr = min(rows, t)
    nb = rr // DN_CHUNK
    qkv = [pl.BlockSpec((rr, LANE), functools.partial(lambda i, h, o: (i, o + h), o=o)) for o in (0, HEADS, 2 * HEADS)]
    ba = pl.BlockSpec((rr, LANE), lambda i, h: (i, 3))
    vec = pl.BlockSpec((1, LANE), lambda i, h: (0, 0))
    row = pl.BlockSpec((rr, LANE), lambda i, h: (i, h))
    intra = pl.BlockSpec((None, rr, DN_CHUNK), lambda i, h: (h, i, 0))
    gl = pl.BlockSpec((nb, LANE), lambda i, h: (i, h))
    return rr, qkv, ba, vec, row, intra, gl


def _grid_ends(grid):
    first = lambda: functools.reduce(jnp.logical_and, [pl.program_id(a) == 0 for a in range(len(grid))])
    last = lambda: functools.reduce(jnp.logical_and, [pl.program_id(a) == n - 1 for a, n in enumerate(grid)])
    return first, last


def _delta_local(qkv_act, pm, alog, dtb, rider=None):
    t = qkv_act.shape[0]
    rr, qkv, ba, vec, row, intra, gl = _pa_specs(t, _PA_ROWS_FWD)

    def body(q, k, v, b, al, dt, *outs):
        vals = _phase_a(pl.program_id(1), q[...], k[...], v[...], b[...], al[...], dt[...])
        for o, val in zip(outs, vals):
            o[...] = val

    wide = jax.ShapeDtypeStruct((t, HEADS * LANE), F32)
    sq = jax.ShapeDtypeStruct((HEADS, t, DN_CHUNK), F32)
    grid = (t // rr, HEADS)
    return _carried_call(
        body, rider, *_grid_ends(grid), name="delta_local", grid=grid, in_specs=qkv + [ba, vec, vec],
        out_specs=[row] * 4 + [intra, gl, intra, row, row],
        out_shape=[wide] * 4 + [sq, jax.ShapeDtypeStruct((t // DN_CHUNK, HEADS * LANE), F32), sq, wide, wide],
        scratch_shapes=[], sem=("arbitrary", "arbitrary"), args=(qkv_act, qkv_act, qkv_act, pm, alog, dtb))


def _delta_local_bwd(qkv_act, pm, alog, dtb, t_inv, du, dw, dqd, dkt, dintra, dgl, rider=None):
    t = qkv_act.shape[0]
    rr, qkv, ba, vec, row, intra, gl = _pa_specs(t, _PA_ROWS)

    def body(q, k, v, b, al, dt, ti, du_r, dw_r, dqd_r, dkt_r, di_r, dgl_r, dq_o, dk_o, dv_o, dba_o, dal_o, ddt_o):
        i, h = pl.program_id(0), pl.program_id(1)
        t_saved = ti[...]
        _, vjp = jax.vjp(lambda *a: _phase_a(h, *a, t_saved=t_saved), q[...], k[...], v[...], b[...], al[...], dt[...])
        dq, dk, dv, dba, dal, ddt = vjp((du_r[...], dw_r[...], dqd_r[...], dkt_r[...], di_r[...], dgl_r[...]))
        dq_o[...], dk_o[...], dv_o[...] = dq, dk, dv

        @pl.when(h == 0)
        def _():
            dba_o[...] = jnp.zeros_like(dba_o)

        @pl.when((h == 0) & (i == 0))
        def _():
            dal_o[...] = jnp.zeros_like(dal_o)
            ddt_o[...] = jnp.zeros_like(ddt_o)

        dba_o[...] += dba
        dal_o[...] += dal
        ddt_o[...] += ddt

    wide = jax.ShapeDtypeStruct((t, HEADS * LANE), F32)
    vshape = jax.ShapeDtypeStruct((1, LANE), F32)
    grid = (t // rr, HEADS)
    return _carried_call(
        body, rider, *_grid_ends(grid), name="delta_local_bwd", grid=grid,
        in_specs=qkv + [ba, vec, vec, intra] + [row] * 4 + [intra, gl],
        out_specs=[row] * 3 + [pl.BlockSpec((rr, LANE), lambda i, h: (i, 0)), vec, vec],
        out_shape=[wide] * 3 + [jax.ShapeDtypeStruct((t, LANE), F32), vshape, vshape],
        scratch_shapes=[], sem=("arbitrary", "arbitrary"),
        args=(qkv_act, qkv_act, qkv_act, pm, alog, dtb, t_inv, du, dw, dqd, dkt, dintra, dgl))


_SCAN_ROWS = 512


def _dot(a, b, dn):
    return lax.dot_general(a.astype(_CDT), b.astype(_CDT), (dn, ((), ())), preferred_element_type=F32)


_NN = ((1,), (0,))
_NT = ((1,), (1,))
_TN = ((0,), (0,))


def _delta_scan(u, w, qd, kt, au, gl, rider=None):
    t = u.shape[0]
    rr = min(_SCAN_ROWS, t)
    nc = rr // DN_CHUNK

    def body(u_ref, w_ref, qd_ref, kt_ref, au_ref, gl_ref, o_ref, sall_ref, s_scr):
        @pl.when(pl.program_id(0) == 0)
        def _():
            s_scr[...] = jnp.zeros_like(s_scr)

        def chunk(c, carry):
            r0 = pl.multiple_of(c * DN_CHUNK, DN_CHUNK)
            rows = pl.ds(r0, DN_CHUNK)
            e = jnp.exp(gl_ref[pl.ds(c, 1), :])
            states = [s_scr[h] for h in range(HEADS)]
            u_c, w_c, qd_c, kt_c, au_c = u_ref[rows, :], w_ref[rows, :], qd_ref[rows, :], kt_ref[rows, :], au_ref[rows, :]
            o_new, s_new = [], []
            for h in range(HEADS):
                cs = slice(h * LANE, (h + 1) * LANE)
                s = states[h]
                both = _dot(jnp.concatenate([w_c[:, cs], qd_c[:, cs]], axis=0), s, _NN)
                v_new = u_c[:, cs] - both[:DN_CHUNK]
                o_new.append(both[DN_CHUNK:] + au_c[:, cs])
                s_new.append(s * e[:, cs] + _dot(kt_c[:, cs], v_new, _TN))
            o_ref[rows, :] = jnp.concatenate(o_new, axis=1)
            for h in range(HEADS):
                sall_ref[c, h] = states[h]
                s_scr[h] = s_new[h]
            return carry

        lax.fori_loop(0, nc, chunk, 0)

    row = pl.BlockSpec((rr, HEADS * LANE), lambda i: (i, 0))
    grid = (t // rr,)
    return _carried_call(
        body, rider, *_grid_ends(grid), name="delta_scan", grid=grid,
        in_specs=[row] * 5 + [pl.BlockSpec((nc, HEADS * LANE), lambda i: (i, 0))],
        out_specs=[row, pl.BlockSpec((nc, HEADS, LANE, LANE), lambda i: (i, 0, 0, 0))],
        out_shape=[jax.ShapeDtypeStruct((t, HEADS * LANE), F32),
                   jax.ShapeDtypeStruct((t // DN_CHUNK, HEADS, LANE, LANE), F32)],
        scratch_shapes=[pltpu.VMEM((HEADS, LANE, LANE), F32)], sem=("arbitrary",), args=(u, w, qd, kt, au, gl))


def _delta_scan_bwd(u, w, qd, kt, intra, gl, sall, do, rider=None):
    t = u.shape[0]
    rr = min(_SCAN_ROWS, t)
    nc = rr // DN_CHUNK
    ng = t // rr

    def body(u_ref, w_ref, qd_ref, kt_ref, a_ref, gl_ref, sall_ref, do_ref,
             du_ref, dw_ref, dqd_ref, dkt_ref, da_ref, dgl_ref, ds_scr):
        @pl.when(pl.program_id(0) == 0)
        def _():
            ds_scr[...] = jnp.zeros_like(ds_scr)

        def chunk(cc, carry):
            c = nc - 1 - cc
            r0 = pl.multiple_of(c * DN_CHUNK, DN_CHUNK)
            rows = pl.ds(r0, DN_CHUNK)
            e = jnp.exp(gl_ref[pl.ds(c, 1), :])
            states = [sall_ref[c, h] for h in range(HEADS)]
            ds_outs = [ds_scr[h] for h in range(HEADS)]
            u_a, w_a, kt_a, qd_a, do_a = u_ref[rows, :], w_ref[rows, :], kt_ref[rows, :], qd_ref[rows, :], do_ref[rows, :]
            a_a = [a_ref[h, rows, :] for h in range(HEADS)]
            da, dqd, dkt, du, dw, dgl, ds_new = [], [], [], [], [], [], []
            for h in range(HEADS):
                cs = slice(h * LANE, (h + 1) * LANE)
                s, ds_out = states[h], ds_outs[h]
                w_c, kt_c, qd_c, do_c = w_a[:, cs], kt_a[:, cs], qd_a[:, cs], do_a[:, cs]
                v_new = u_a[:, cs] - _dot(w_c, s, _NN)
                dv_new = _dot(a_a[h], do_c, _TN) + _dot(kt_c, ds_out, _NN)
                cots = jnp.concatenate([do_c, dv_new], axis=0)
                both = _dot(cots, s, _NT)
                dqd.append(both[:DN_CHUNK])
                dw.append(-both[DN_CHUNK:])
                da.append(_dot(do_c, v_new, _NT))
                dkt.append(_dot(v_new, ds_out, _NT))
                du.append(dv_new)
                eh = e[:, cs]
                dgl.append(jnp.broadcast_to(jnp.sum(ds_out * s, axis=0, keepdims=True) * eh, (8, LANE)))
                ds_new.append(ds_out * eh + _dot(jnp.concatenate([qd_c, -w_c], axis=0), cots, _TN))
            cat = lambda parts: jnp.concatenate(parts, axis=1)
            dqd_ref[rows, :], dkt_ref[rows, :], du_ref[rows, :], dw_ref[rows, :] = cat(dqd), cat(dkt), cat(du), cat(dw)
            dgl_ref[pl.ds(pl.multiple_of(c * 8, 8), 8), :] = cat(dgl)
            for h in range(HEADS):
                da_ref[h, rows, :] = da[h]
                ds_scr[h] = ds_new[h]
            return carry

        lax.fori_loop(0, nc, chunk, 0)

    rev = lambda i: (ng - 1 - i, 0)
    row = pl.BlockSpec((rr, HEADS * LANE), rev)
    a_spec = pl.BlockSpec((HEADS, rr, DN_CHUNK), lambda i: (0, ng - 1 - i, 0))
    gl_spec = pl.BlockSpec((nc, HEADS * LANE), rev)
    wide = jax.ShapeDtypeStruct((t, HEADS * LANE), F32)
    outs, carried = _carried_call(
        body, rider, *_grid_ends((ng,)), name="delta_scan_bwd", grid=(ng,),
        in_specs=[row] * 4 + [a_spec, gl_spec, pl.BlockSpec((nc, HEADS, LANE, LANE), lambda i: (ng - 1 - i, 0, 0, 0)), row],
        out_specs=[row] * 4 + [a_spec, pl.BlockSpec((nc * 8, HEADS * LANE), rev)],
        out_shape=[wide] * 4 + [jax.ShapeDtypeStruct((HEADS, t, DN_CHUNK), F32),
                                jax.ShapeDtypeStruct((t // DN_CHUNK * 8, HEADS * LANE), F32)],
        scratch_shapes=[pltpu.VMEM((HEADS, LANE, LANE), F32)], sem=("arbitrary",), args=(u, w, qd, kt, intra, gl, sall, do))
    return tuple(outs[:5]) + (outs[5].reshape(t // DN_CHUNK, 8, HEADS * LANE)[:, 0, :],), carried


_ATT_TILE = 512
_ATT_BWD_HEADS = 2


def _kv_rows(j, tk):
    return pl.ds(pl.multiple_of(j * tk, tk), tk)


def _att_scores(ql, qr, ckv_ref, kr_ref, j, tk):
    ks = _kv_rows(j, tk)
    return (_dot(ql, ckv_ref[ks, :], _NT) + _dot(qr, kr_ref[ks, :], _NT)) * ATT_SCALE


def _diag_mask(s):
    qi = lax.broadcasted_iota(jnp.int32, s.shape, 0) % s.shape[1]
    ki = lax.broadcasted_iota(jnp.int32, s.shape, 1)
    return jnp.where(ki <= qi, s, NEG_BIG)


def _attention(qn, qr_pre, cosb, sinb, ckv, kr, wuk, wuv):
    t = ckv.shape[0]
    tq = min(_ATT_TILE, t)

    nl = tq // LANE

    def lane_fold(v, op):
        out = v[:, :LANE]
        for k in range(1, nl):
            out = op(out, v[:, k * LANE:(k + 1) * LANE])
        return out

    def body(qn_ref, qr_ref, cos_ref, sin_ref, ckv_ref, kr_ref, wuk_ref, wuv_ref, o_ref, lse_ref, qrope_ref, omla_ref,
             s_all, m_lanes, l_lanes, acc_scr):
        h, qi = pl.program_id(0), pl.program_id(1)
        q_lat = _dot(qn_ref[...], wuk_ref[h], _NT).astype(_CDT)
        q_rope = _rope(qr_ref[...], cos_ref[...], sin_ref[...]).astype(qrope_ref.dtype)
        qrope_ref[...] = q_rope
        m_lanes[...] = jnp.full_like(m_lanes, NEG_BIG)

        def scores(j, masked):
            s = _att_scores(q_lat, q_rope, ckv_ref, kr_ref, j, tq)
            if masked:
                s = _diag_mask(s)
            s_all[j] = s
            m_lanes[...] = jnp.maximum(m_lanes[...], lane_fold(s, jnp.maximum))

        def scores_body(j, carry):
            scores(j, False)
            return carry

        lax.fori_loop(0, qi, scores_body, 0)
        scores(qi, True)
        m = jnp.max(m_lanes[...], axis=-1, keepdims=True)
        mb = jnp.broadcast_to(m, (tq, LANE))
        l_lanes[...] = jnp.zeros_like(l_lanes)
        acc_scr[...] = jnp.zeros_like(acc_scr)

        def weigh(j, carry):
            s = s_all[j]
            p = jnp.concatenate([jnp.exp(s[:, k * LANE:(k + 1) * LANE] - mb) for k in range(nl)], axis=1)
            l_lanes[...] += lane_fold(p, jnp.add)
            acc_scr[...] += _dot(p, ckv_ref[_kv_rows(j, tq), :], _NN)
            return carry

        lax.fori_loop(0, qi + 1, weigh, 0)
        l = jnp.sum(l_lanes[...], axis=-1, keepdims=True)
        out = acc_scr[...] / l
        o_ref[...] = out
        lse_ref[...] = m + jnp.log(l)
        omla_ref[...] = _dot(out, wuv_ref[h], _NN).astype(omla_ref.dtype)

    col = pl.BlockSpec((tq, LANE), lambda h, i: (i, h))
    table = pl.BlockSpec((tq, ROPE_PAD), lambda h, i: (i, 0))
    wspec = pl.BlockSpec((HEADS, KV_LORA, NOPE), lambda h, i: (0, 0, 0))
    return pl.pallas_call(
        body, name="attention", grid=(HEADS, t // tq),
        in_specs=[col, col, table, table, pl.BlockSpec((t, KV_LORA), lambda h, i: (0, 0)),
                  pl.BlockSpec((t, ROPE_PAD), lambda h, i: (0, 0)), wspec, wspec],
        out_specs=[pl.BlockSpec((None, tq, KV_LORA), lambda h, i: (h, i, 0)),
                   pl.BlockSpec((None, tq, 1), lambda h, i: (h, i, 0)), col, col],
        out_shape=[jax.ShapeDtypeStruct((HEADS, t, KV_LORA), F32), jax.ShapeDtypeStruct((HEADS, t, 1), F32),
                   jax.ShapeDtypeStruct((t, HEADS * ROPE_PAD), _CDT), jax.ShapeDtypeStruct((t, HEADS * NOPE), _CDT)],
        scratch_shapes=[pltpu.VMEM((t // tq, tq, tq), F32), pltpu.VMEM((tq, LANE), F32), pltpu.VMEM((tq, LANE), F32),
                        pltpu.VMEM((tq, KV_LORA), F32)],
        compiler_params=_cparams(("parallel", "parallel")))(qn, qr_pre, cosb, sinb, ckv, kr, wuk, wuv)


def _attention_bwd(qn, qr, cosb, sinb, ckv, kr, wuk, wuv, out, lse, do_mla):
    t = ckv.shape[0]
    tq = min(_ATT_TILE, t)
    hp = _ATT_BWD_HEADS

    def body(qn_ref, qr_ref, cos_ref, sin_ref, ckv_ref, kr_ref, wuk_ref, wuv_ref, o_ref, lse_ref, do_ref,
             dql_ref, dqr_ref, dqn_ref, dckv_ref, dkr_ref, dql_scr, dqr_scr):
        hg, qi = pl.program_id(0), pl.program_id(1)

        @pl.when((hg == 0) & (qi == 0))
        def _():
            dckv_ref[...] = jnp.zeros_like(dckv_ref)
            dkr_ref[...] = jnp.zeros_like(dkr_ref)

        lanes = lambda ref, k: ref[:, k * LANE:(k + 1) * LANE]
        rows = lambda parts: jnp.concatenate(parts, axis=0)
        q_lat = rows([_dot(lanes(qn_ref, k), wuk_ref[hg * hp + k], _NT) for k in range(hp)]).astype(_CDT)
        q_rope = rows([lanes(qr_ref, k) for k in range(hp)])
        d_out = rows([_dot(lanes(do_ref, k), wuv_ref[hg * hp + k], _NT) for k in range(hp)])
        d_o = d_out.astype(_CDT)
        lse_v = rows([lse_ref[k] for k in range(hp)])
        dsum = jnp.sum(d_out * rows([o_ref[k] for k in range(hp)]), axis=-1, keepdims=True)
        dql_scr[...] = jnp.zeros_like(dql_scr)
        dqr_scr[...] = jnp.zeros_like(dqr_scr)

        def step(j, masked):
            ks = _kv_rows(j, tq)
            s = _att_scores(q_lat, q_rope, ckv_ref, kr_ref, j, tq)
            if masked:
                s = _diag_mask(s)
            p = jnp.exp(s - lse_v)
            kv = ckv_ref[ks, :]
            ds = (p * (_dot(d_o, kv, _NT) - dsum) * ATT_SCALE).astype(_CDT)
            pb = p.astype(_CDT)
            dql_scr[...] += _dot(ds, kv, _NN)
            dqr_scr[...] += _dot(ds, kr_ref[ks, :], _NN)
            dckv_ref[ks, :] += _dot(pb, d_o, _TN) + _dot(ds, q_lat, _TN)
            dkr_ref[ks, :] += _dot(ds, q_rope, _TN)

        def loop_body(j, carry):
            step(j, False)
            return carry

        lax.fori_loop(0, qi, loop_body, 0)
        step(qi, True)
        cols = lambda parts: jnp.concatenate(parts, axis=1)
        part = lambda v, k: v[k * tq:(k + 1) * tq]
        dql = dql_scr[...].astype(dql_ref.dtype)
        dqr = dqr_scr[...]
        for k in range(hp):
            dql_ref[k] = part(dql, k)
        dqn_ref[...] = cols([_dot(part(dql, k), wuk_ref[hg * hp + k], _NN) for k in range(hp)]).astype(dqn_ref.dtype)
        dqr_ref[...] = cols([_rope_bwd(part(dqr, k), cos_ref[...], sin_ref[...]) for k in range(hp)]).astype(dqr_ref.dtype)

    lat = pl.BlockSpec((hp, tq, KV_LORA), lambda h, i: (h, i, 0))
    col = pl.BlockSpec((tq, hp * LANE), lambda h, i: (i, h))
    table = pl.BlockSpec((tq, ROPE_PAD), lambda h, i: (i, 0))
    kfull = pl.BlockSpec((t, KV_LORA), lambda h, i: (0, 0))
    rfull = pl.BlockSpec((t, ROPE_PAD), lambda h, i: (0, 0))
    wspec = pl.BlockSpec((HEADS, KV_LORA, NOPE), lambda h, i: (0, 0, 0))
    wide = jax.ShapeDtypeStruct((t, HEADS * LANE), _CDT)
    return pl.pallas_call(
        body, name="attention_bwd", grid=(HEADS // hp, t // tq),
        in_specs=[col, col, table, table, kfull, rfull, wspec, wspec, lat,
                  pl.BlockSpec((hp, tq, 1), lambda h, i: (h, i, 0)), col],
        out_specs=[lat, col, col, kfull, rfull],
        out_shape=[jax.ShapeDtypeStruct((HEADS, t, KV_LORA), _CDT), wide, wide,
                   jax.ShapeDtypeStruct((t, KV_LORA), F32), jax.ShapeDtypeStruct((t, ROPE_PAD), F32)],
        scratch_shapes=[pltpu.VMEM((hp * tq, KV_LORA), F32), pltpu.VMEM((hp * tq, ROPE_PAD), F32)],
        compiler_params=_cparams(("arbitrary", "arbitrary")))(qn, qr, cosb, sinb, ckv, kr, wuk, wuv, out, lse, do_mla)


_DX_ROWS = 256


def _dx_fused(pairs, add, add_scale, rider=None, name="b_dx"):
    t, d = add.shape
    tm = min(_DX_ROWS, t)
    n = len(pairs)

    def body(*refs):
        acc = refs[2 * n][...] * add_scale
        for i in range(n):
            acc = acc + _dot(refs[i][...], refs[n + i][...], _NT)
        refs[2 * n + 1][...] = acc

    in_specs = [pl.BlockSpec((tm, a.shape[1]), lambda i: (i, 0)) for a, _ in pairs]
    in_specs += [pl.BlockSpec(w.shape, lambda i: (0, 0)) for _, w in pairs]
    row = pl.BlockSpec((tm, d), lambda i: (i, 0))
    grid = (t // tm,)
    (dx,), carried = _carried_call(
        body, rider, *_grid_ends(grid), name=name, grid=grid, in_specs=in_specs + [row], out_specs=[row],
        out_shape=[jax.ShapeDtypeStruct((t, d), F32)], scratch_shapes=[], sem=("arbitrary",),
        args=tuple(a for a, _ in pairs) + tuple(w for _, w in pairs) + (add,))
    return dx, carried


def _ffn_in_swiglu(a, w):
    t, k = a.shape
    hid = w.shape[1] // 2
    tm, tn = _tile(t, 1024), _tile(hid, 1408)
    nj = hid // tn

    def body(a_ref, bg_ref, bu_ref, act_ref, gt_ref, up_ref):
        av = a_ref[...].astype(_CDT)
        gt = jnp.dot(av, bg_ref[...].astype(_CDT), preferred_element_type=F32)
        up = jnp.dot(av, bu_ref[...].astype(_CDT), preferred_element_type=F32)
        act_ref[...] = _swiglu(gt, up).astype(act_ref.dtype)
        gt_ref[...] = gt.astype(gt_ref.dtype)
        up_ref[...] = up.astype(up_ref.dtype)

    out = pl.BlockSpec((tm, tn), lambda i, j: (i, j))
    return pl.pallas_call(
        body, name="f_ffn_in_swiglu", grid=(t // tm, nj),
        in_specs=[pl.BlockSpec((tm, k), lambda i, j: (i, 0)), pl.BlockSpec((k, tn), lambda i, j: (0, j)),
                  pl.BlockSpec((k, tn), lambda i, j: (0, nj + j))],
        out_specs=[out] * 3, out_shape=[jax.ShapeDtypeStruct((t, hid), _CDT)] * 3,
        compiler_params=_cparams(("parallel", "parallel")))(a, w, w)


def _gated_norm(o, z, w):
    return _rms_norm(o, w) * _silu(z)


def _gated_norm_heads(o, z, w):
    heads = [_gated_norm(o[:, h * LANE:(h + 1) * LANE], z[:, h * LANE:(h + 1) * LANE], w) for h in range(HEADS)]
    return jnp.concatenate(heads, axis=1)


def _mla_pre(ckv, krp, cq, cosb, sinb, qw, kw):
    return _rms_norm(cq, qw), _rms_norm(ckv, kw), _rope(krp, cosb, sinb)


def _merge(gg, y_dn, y_mla):
    return _sigmoid(gg[:, :D_MODEL]) * y_dn + _sigmoid(gg[:, D_MODEL:]) * y_mla


def _ln1(xv, attn_out, g, b):
    return _layer_norm(ALPHA * xv + attn_out, g, b)


def _final(h1, ffn, gate_pre, ple_proj, g, b):
    return _layer_norm(ALPHA * h1 + ffn + _sigmoid(gate_pre) * ple_proj, g, b)


def _swiglu(gt, up):
    return _silu(gt) * up


def _local_step(x, p, cosb, sinb, target, wt, sp, exch):
    t = x.shape[0]
    bf = _CDT
    xb = x.astype(bf)
    g = {}

    qkv_pre = _mm2(xb, wt['qkv'], name="f_qkv")
    z = _mm2(xb, wt['z'], name="f_z")
    gg = _mm2(xb, wt['gg'], name="f_gg")
    pm = _mm2(xb, wt['mla'], name="f_mla")
    qkv_act = _conv_silu(qkv_pre, sp['conv_w'])
    (u, w_, qd, kt, intra, gl, t_inv, qd2, au), sent = _delta_local(qkv_act, pm, sp['a_log'], sp['dt_bias'],
                                                                    rider=exch.gather_send())
    (o_dn, sall), passed = _delta_scan(u, w_, qd2, kt, au, gl, rider=exch.gather_pass(sent))
    wt = dict(wt, **exch.weights(passed))
    def gated_norm_br(h, o, zz, w, wbr):
        og_v = _gated_norm_heads(o, zz, w).astype(bf)
        return og_v, jnp.dot(og_v, wbr.astype(bf), preferred_element_type=F32)

    og, y_dn = _rowwise(gated_norm_br, [(o_dn, D_MODEL, 0, False), (z, D_MODEL, 0, False)],
                        [sp['dn_norm_w'], wt['br_dn']],
                        [(D_MODEL, D_MODEL, False, bf), (D_MODEL, D_MODEL, False, F32)], name="f_gated_norm_br", tm=512)

    def mla_pre_uq(h, ckv, krp, cq, cosv, sinv, qw, kw, wn, wr):
        c_q_v, c_kv_v, k_rope_v = _mla_pre(ckv, krp, cq, cosv, sinv, qw, kw)
        c_q_b = c_q_v.astype(bf)
        return (c_q_b, c_kv_v, k_rope_v, jnp.dot(c_q_b, wn.astype(bf), preferred_element_type=F32),
                jnp.dot(c_q_b, wr.astype(bf), preferred_element_type=F32))

    c_q, c_kv, k_rope, q_nope, q_rope_pre = _rowwise(
        mla_pre_uq,
        [(pm, KV_LORA, 0, False), (pm, ROPE_PAD, 2, False), (pm, Q_LORA, 2, False),
         (cosb, ROPE_PAD, 0, False), (sinb, ROPE_PAD, 0, False)],
        [sp['q_norm_w'], sp['kv_norm_w'], wt['uq_nope'], wt['uq_rope']],
        [(Q_LORA, Q_LORA, False, bf), (KV_LORA, KV_LORA, False, bf), (ROPE_PAD, ROPE_PAD, False, bf),
         (HEADS * NOPE, HEADS * NOPE, False, bf), (HEADS * ROPE_PAD, HEADS * ROPE_PAD, False, F32)], name="f_mla_pre_uq",
        tm=512)
    out_lat, lse, q_rope, o_mla = _attention(q_nope, q_rope_pre, cosb, sinb, c_kv, k_rope, wt['uk'], wt['uv'])
    y_mla = _mm2(o_mla, wt['br_mla'], name="f_br_mla")

    def merge_o_ln1(h, ggv, yd, ym, xv, wo, gv, bv):
        mixed_v = _merge(ggv, yd, ym).astype(bf)
        ao = jnp.dot(mixed_v, wo.astype(bf), preferred_element_type=F32)
        h1v = _ln1(xv, ao, gv, bv)
        return mixed_v, ao, h1v, h1v

    mixed, attn_out, h1, h1b = _rowwise(
        merge_o_ln1, [(gg, 2 * D_MODEL, 0, False), (y_dn, D_MODEL, 0, False), (y_mla, D_MODEL, 0, False), (x, D_MODEL, 0, False)],
        [wt['o'], sp['ln1_g'], sp['ln1_b']],
        [(D_MODEL, D_MODEL, False, bf), (D_MODEL, D_MODEL, False, F32), (D_MODEL, D_MODEL, False, F32),
         (D_MODEL, D_MODEL, False, bf)], name="f_merge_o_ln1", tm=512)
    act, ffn_gt, ffn_up = _ffn_in_swiglu(h1b, wt['ffn_in'])
    pb = p.astype(bf)

    def final_fn(h, h1v, actv, pv, tgt, gt, up, wfo, wpg, wpl, gv, bv):
        ffnv = jnp.dot(actv.astype(bf), wfo.astype(bf), preferred_element_type=F32)
        gpv = jnp.dot(h1v.astype(bf), wpg.astype(bf), preferred_element_type=F32)
        ppv = jnp.dot(pv.astype(bf), wpl.astype(bf), preferred_element_type=F32)
        y, vjp = jax.vjp(_final, h1v, ffnv, gpv, ppv, gv, bv)
        err = y - tgt
        dh1, dffn, dgp, dpp, dg, db = vjp(err * (1.0 / D_MODEL))
        sq = err * err
        lanes = sq[:, :LANE]
        for j in range(1, D_MODEL // LANE):
            lanes = lanes + sq[:, j * LANE:(j + 1) * LANE]
        loss = jnp.sum(lanes, axis=0, keepdims=True) * (0.5 / D_MODEL)
        dffn_b = dffn.astype(bf)
        dact = _dot(dffn_b, wfo, _NT).astype(bf).astype(F32)
        _, vjp_s = jax.vjp(_swiglu, gt.astype(F32), up.astype(F32))
        dgt, dup = vjp_s(dact)
        return dffn, dffn_b, dgp, dpp, jnp.concatenate([dgt, dup], axis=1), dg, db, loss

    dpre2, dpre2b, dgate_pre, dple_proj, dffn_in, g['ln2_g'], g['ln2_b'], loss_lanes = _rowwise(
        final_fn, [(h1, D_MODEL, 0, False), (act, FFN_HIDDEN, 0, False), (pb, PLE_DIM, 0, False), (target, D_MODEL, 0, False),
                   (ffn_gt, FFN_HIDDEN, 0, False), (ffn_up, FFN_HIDDEN, 0, False)],
        [wt['ffn_out'], wt['ple_gate'], wt['ple'], sp['ln2_g'], sp['ln2_b']],
        [(D_MODEL, D_MODEL, False, F32)] + [(D_MODEL, D_MODEL, False, bf)] * 3 + [(2 * FFN_HIDDEN, 2 * FFN_HIDDEN, False, bf)],
        [(1, D_MODEL), (1, D_MODEL), (1, LANE)], name="b_final")
    g['loss_lanes'] = loss_lanes

    g['ple'] = _mm2(pb, dple_proj, ta=True, name="g_ple")
    g['ple_gate'] = _mm2(h1b, dgate_pre, ta=True, name="g_ple_gate")
    g['ffn_out'] = _mm2(act, dpre2b, ta=True, name="g_ffn_out")
    g['ffn_in'] = _mm2(h1b, dffn_in, ta=True, name="g_ffn_in")
    dh1, _ = _dx_fused([(dffn_in, wt['ffn_in']), (dgate_pre, wt['ple_gate'])], dpre2, ALPHA, name="b_dh1")

    def attn_out_bwd(h, xv, ao, d, ggv, yd, ym, o, zz, wo, wbd, wbm, gv, bv, nw):
        _, vjp = jax.vjp(_ln1, xv, ao, gv, bv)
        _, dao, dg, db = vjp(d)
        dao_b = dao.astype(bf)
        _, vjp_m = jax.vjp(_merge, ggv, yd, ym)
        dggv, dyd, dym = vjp_m(_dot(dao_b, wo, _NT))
        dyd_b, dym_b = dyd.astype(bf), dym.astype(bf)
        _, vjp_n = jax.vjp(_gated_norm_heads, o, zz, nw)
        do_v, dz_v, dnw = vjp_n(_dot(dyd_b, wbd, _NT))
        return dao, dao_b, dggv, dyd_b, dym_b, do_v, dz_v, _dot(dym_b, wbm, _NT), dg, db, dnw

    row_d = lambda a: (a, D_MODEL, 0, False)
    dpre1, dpre1b, dgg, dy_dn, dy_mla, do_dn, dz, do_mla, g['ln1_g'], g['ln1_b'], g['dn_norm_w'] = _rowwise(
        attn_out_bwd, [row_d(x), row_d(attn_out), row_d(dh1), (gg, 2 * D_MODEL, 0, False), row_d(y_dn), row_d(y_mla),
                       row_d(o_dn), row_d(z)],
        [wt['o'], wt['br_dn'], wt['br_mla'], sp['ln1_g'], sp['ln1_b'], sp['dn_norm_w']],
        [(D_MODEL, D_MODEL, False, F32), (D_MODEL, D_MODEL, False, bf), (2 * D_MODEL, 2 * D_MODEL, False, bf),
         (D_MODEL, D_MODEL, False, bf), (D_MODEL, D_MODEL, False, bf), (D_MODEL, D_MODEL, False, F32),
         (D_MODEL, D_MODEL, False, bf), (D_MODEL, D_MODEL, False, bf)],
        [(1, D_MODEL), (1, D_MODEL), (1, LANE)], name="b_attn_out")
    g['o'] = _mm2(mixed, dpre1b, ta=True, name="g_o")
    g['br_dn'] = _mm2(og, dy_dn, ta=True, name="g_br_dn")
    g['br_mla'] = _mm2(o_mla, dy_mla, ta=True, name="g_br_mla")

    g['uv'] = _mm(out_lat, do_mla, name="g_uv", ta=True, heads=HEADS, a_head='lead', b_head='col', out_head='lead',
                  dims=(KV_LORA, NOPE, t))
    dq_lat, dq_rope_pre, dq_nope, dckv_att, dkr_att = _attention_bwd(
        q_nope, q_rope, cosb, sinb, c_kv, k_rope, wt['uk'], wt['uv'], out_lat, lse, do_mla)
    g['uk'] = _mm(dq_lat, q_nope, name="g_uk", ta=True, heads=HEADS, a_head='lead', b_head='col', out_head='lead',
                  dims=(KV_LORA, NOPE, t))
    g['uq_nope'] = _mm2(c_q, dq_nope, ta=True, name="g_uq_nope")
    g['uq_rope'] = _mm2(c_q, dq_rope_pre, ta=True, name="g_uq_rope")
    dc_q = _mm2(dq_nope, wt['uq_nope'], tb=True, name="b_dcq_nope")
    dc_q = _mm2(dq_rope_pre, wt['uq_rope'], tb=True, name="b_dcq_rope", add=dc_q)

    (du, dw, dqd, dkt, dintra, dgl), paired = _delta_scan_bwd(u, w_, qd, kt, intra, gl, sall, do_dn, rider=exch.pair_send(g))
    (dq_a, dk_a, dv_a, dba, g['a_log'], g['dt_bias']), arrived = _delta_local_bwd(
        qkv_act, pm, sp['a_log'], sp['dt_bias'], t_inv, du, dw, dqd, dkt, dintra, dgl, rider=exch.reduce_send(paired))
    exch.reduce_arrived(arrived)
    dqkv_pre, g['conv_w'] = _conv_silu_bwd(qkv_pre, sp['conv_w'], [dq_a, dk_a, dv_a])

    def mla_pre_bwd(h, ckv, cq, cosv, sinv, dcq, dckv, dkr, dba_v, qw, kw):
        _, vjp = jax.vjp(lambda a, c, d, e: (_rms_norm(c, d), _rms_norm(a, e)), ckv, cq, qw, kw)
        dckv_p, dcq_p, dqw, dkw = vjp((dcq, dckv))
        dkr_p = _rope_bwd(dkr, cosv, sinv)
        dpm = jnp.concatenate([dckv_p, dkr_p, dba_v, jnp.zeros((ckv.shape[0], 2 * LANE), F32), dcq_p], axis=1)
        return dpm, dqw, dkw

    dpm, g['q_norm_w'], g['kv_norm_w'] = _rowwise(
        mla_pre_bwd,
        [(pm, KV_LORA, 0, False), (pm, Q_LORA, 2, False), (cosb, ROPE_PAD, 0, False), (sinb, ROPE_PAD, 0, False),
         (dc_q, Q_LORA, 0, False), (dckv_att, KV_LORA, 0, False), (dkr_att, ROPE_PAD, 0, False), (dba, LANE, 0, False)],
        [sp['q_norm_w'], sp['kv_norm_w']], [(1152, 1152, False, bf)], [(1, Q_LORA), (1, KV_LORA)], name="b_mla_pre")

    rider = exch.small_send(g)
    if rider is None:
        g['qkv'] = _mm2(xb, dqkv_pre, ta=True, name="g_qkv")
    else:
        g['qkv'], got = _mm2(xb, dqkv_pre, ta=True, name="g_qkv", rider=rider)
        exch.small_arrived(got)
    g['z'] = _mm2(xb, dz, ta=True, name="g_z")
    g['gg'] = _mm2(xb, dgg, ta=True, name="g_gg")
    g['mla'] = _mm2(xb, dpm, ta=True, name="g_mla")
    dx, arrived = _dx_fused([(dqkv_pre, wt['qkv']), (dz, wt['z']), (dgg, wt['gg']), (dpm, wt['mla'])], dpre1, ALPHA,
                            rider=exch.in_send(g))
    exch.in_arrived(arrived)
    return loss_lanes, dx, g


_IN_SIZES = (QKV_W, HEADS * DN_DK, HEADS, HEADS, Q_LORA, KV_LORA, ROPE, D_MODEL, D_MODEL)


def _rope_tables(positions):
    inv_freq = ROPE_BASE ** (-jnp.arange(0, ROPE, 2, dtype=F32) / ROPE)
    ang = positions.astype(F32)[:, None] * inv_freq
    cos, sin = jnp.cos(ang), jnp.sin(ang)
    zeros = jnp.zeros((positions.shape[0], ROPE_PAD - ROPE), F32)
    return jnp.concatenate([cos, cos, zeros], axis=1), jnp.concatenate([-sin, sin, zeros], axis=1)


def _prep_w_in(w_in):
    dt = w_in.dtype
    offs = [0]
    for s in _IN_SIZES:
        offs.append(offs[-1] + s)
    qkv, z, wb, wa, cq, ckv, kr, gd, gm = [w_in[:, offs[i]:offs[i + 1]] for i in range(len(_IN_SIZES))]
    zc = lambda n: jnp.zeros((D_MODEL, n), dt)
    return {
        'qkv': qkv, 'z': z, 'gg': jnp.concatenate([gd, gm], axis=1),
        'mla': jnp.concatenate([ckv, kr, zc(ROPE_PAD - ROPE), wb, wa, zc(LANE - 2 * HEADS), zc(2 * LANE), cq], axis=1),
    }


def _prep_weights(full):
    w_uq = full['w_uq']
    wt = {
        'uq_nope': w_uq[:, :, :NOPE].reshape(Q_LORA, HEADS * NOPE),
        'uq_rope': jnp.pad(w_uq[:, :, NOPE:], ((0, 0), (0, 0), (0, ROPE_PAD - ROPE))).reshape(Q_LORA, HEADS * ROPE_PAD),
        'uk': jnp.transpose(full['w_uk'], (1, 0, 2)), 'uv': jnp.transpose(full['w_uv'], (1, 0, 2)),
        'br_dn': full['w_br_dn'], 'br_mla': full['w_br_mla'], 'o': full['w_o'], 'ffn_in': full['w_ffn_in'],
        'ffn_out': full['w_ffn_out'], 'ple': full['w_ple'], 'ple_gate': full['w_ple_gate'],
    }
    return wt


def _prep_small(small):
    pad = lambda v: jnp.pad(v, (0, LANE - v.shape[0]))[None, :]
    return {
        'conv_w': small['conv_w'], 'a_log': pad(small['dn_a_log']), 'dt_bias': pad(small['dn_dt_bias']),
        'dn_norm_w': small['dn_norm_w'][None, :], 'q_norm_w': small['q_norm_w'][None, :],
        'kv_norm_w': small['kv_norm_w'][None, :], 'ln1_g': small['ln1_g'][None, :], 'ln1_b': small['ln1_b'][None, :],
        'ln2_g': small['ln2_g'][None, :], 'ln2_b': small['ln2_b'][None, :],
    }


def _w_in_grad(g):
    mla = g['mla']
    ba0 = KV_LORA + ROPE_PAD
    cq0 = ba0 + 3 * LANE
    return jnp.concatenate([
        g['qkv'], g['z'], mla[:, ba0:ba0 + HEADS], mla[:, ba0 + HEADS:ba0 + 2 * HEADS], mla[:, cq0:cq0 + Q_LORA],
        mla[:, :KV_LORA], mla[:, KV_LORA:KV_LORA + ROPE], g['gg']], axis=1)


def _unprep_grads_late(g):
    return {
        'conv_w': g['conv_w'], 'dn_a_log': g['a_log'][0, :HEADS], 'dn_dt_bias': g['dt_bias'][0, :HEADS],
        'dn_norm_w': g['dn_norm_w'][0], 'q_norm_w': g['q_norm_w'][0], 'kv_norm_w': g['kv_norm_w'][0],
        'ln1_g': g['ln1_g'][0], 'ln1_b': g['ln1_b'][0], 'ln2_g': g['ln2_g'][0], 'ln2_b': g['ln2_b'][0],
    }


def _unprep_grads_early(g):
    w_uq = jnp.concatenate([g['uq_nope'].reshape(Q_LORA, HEADS, NOPE),
                            g['uq_rope'].reshape(Q_LORA, HEADS, ROPE_PAD)[:, :, :ROPE]], axis=2)
    return {
        'w_uq': w_uq, 'w_uk': jnp.transpose(g['uk'], (1, 0, 2)), 'w_uv': jnp.transpose(g['uv'], (1, 0, 2)),
        'w_br_dn': g['br_dn'], 'w_br_mla': g['br_mla'], 'w_o': g['o'],
        'w_ffn_in': g['ffn_in'], 'w_ffn_out': g['ffn_out'], 'w_ple': g['ple'], 'w_ple_gate': g['ple_gate'],
    }


_FLATB_PIECES = (
    ('w_ffn_out', 704, (704, D_MODEL)), ('w_br_dn', 256, (256, D_MODEL)), ('w_br_mla', 256, (256, D_MODEL)),
    ('w_o', 256, (256, D_MODEL)), ('w_ple_gate', 256, (256, D_MODEL)), ('w_uq', 144, (96, HEADS, NOPE + ROPE)),
    ('w_uk', 64, (64, HEADS, NOPE)), ('w_uv', 64, (64, HEADS, NOPE)), ('w_ple', 64, (PLE_DIM, 256)),
)
FLATB_ROWS = 2112
W_IN_SHARD = D_IN // N_SHARD
FFN_IN_SHARD = 2 * FFN_HIDDEN // N_SHARD
A_ROWS = D_MODEL + 32
_CONV_SHARD = QKV_W // N_SHARD
_ADD_TILES = (256, 256, 352)


def _flatb_offsets():
    offs, o = {}, 0
    for name, rows, _ in _FLATB_PIECES:
        offs[name] = o
        o += rows
    return offs, o


def _pack_shards(ws, conv_w):
    conv_bits = lax.bitcast_convert_type(conv_w, jnp.bfloat16).reshape(DN_CONV, 2 * _CONV_SHARD).astype(_CDT)
    tail = jnp.pad(conv_bits, ((0, A_ROWS - D_MODEL - DN_CONV), (0, W_IN_SHARD - 2 * _CONV_SHARD)))
    a_buf = jnp.concatenate([ws['w_in'].astype(_CDT), tail], axis=0)
    parts = [ws[name].astype(_CDT).reshape(rows, FLAT_W) for name, rows, _ in _FLATB_PIECES]
    used = sum(p.shape[0] for p in parts)
    parts.append(jnp.zeros((FLATB_ROWS - used, FLAT_W), _CDT))
    return [a_buf, ws['w_ffn_in'].astype(_CDT), jnp.concatenate(parts, axis=0)]


def _unpack_w_in(gathered, local, me):
    a = [jnp.where(me == s, local, gathered[s]) for s in range(N_SHARD)]
    conv = [lax.bitcast_convert_type(
        p[D_MODEL:D_MODEL + DN_CONV, :2 * _CONV_SHARD].astype(jnp.bfloat16).reshape(DN_CONV, _CONV_SHARD, 2), F32) for p in a]
    return jnp.concatenate([p[:D_MODEL] for p in a], axis=1), jnp.concatenate(conv, axis=1)


def _unpack_rest(gathered, local, me):
    pick = lambda b, s: jnp.where(me == s, local[b], gathered[b][s])
    full = {'w_ffn_in': jnp.concatenate([pick(0, s) for s in range(N_SHARD)], axis=1)}
    offs, _ = _flatb_offsets()
    fb = [pick(1, s) for s in range(N_SHARD)]
    for name, rows, shape in _FLATB_PIECES:
        pieces = [p[offs[name]:offs[name] + rows].reshape(shape) for p in fb]
        full[name] = jnp.concatenate(pieces, axis=1 if name == 'w_ple' else 0)
    return full


def _shard_columns(g, w):
    return jnp.stack([g[:, s * w:(s + 1) * w] for s in range(N_SHARD)])


def _pack_grads_rest(gw):
    parts = []
    for name, rows, _ in _FLATB_PIECES:
        g = gw[name]
        if name == 'w_ple':
            parts.append(_shard_columns(g, PLE_DIM).reshape(N_SHARD, rows, FLAT_W))
        else:
            parts.append(g.reshape(N_SHARD, rows, FLAT_W))
    used = sum(p.shape[1] for p in parts)
    parts.append(jnp.zeros((N_SHARD, FLATB_ROWS - used, FLAT_W), F32))
    return [_shard_columns(gw['w_ffn_in'], FFN_IN_SHARD), jnp.concatenate(parts, axis=1)]


def _unpack_reduced(mine, theirs, c):
    whole = [jnp.concatenate([jnp.where(c == 0, m, t), jnp.where(c == 0, t, m)], axis=0) for m, t in zip(mine, theirs)]
    out = {'w_in': whole[0], 'w_ffn_in': whole[1]}
    offs, _ = _flatb_offsets()
    for name, rows, shape in _FLATB_PIECES:
        out[name] = whole[2][offs[name]:offs[name] + rows].reshape(shape)
    return out


_HBM = pl.BlockSpec(memory_space=pltpu.HBM)


def _place():
    x, y, c = lax.axis_index("x"), lax.axis_index("y"), lax.axis_index("c")
    chips = [(1 - x, y), (x, 1 - y), (1 - x, 1 - y)]
    return x, y, c, chips


def _remote(src, dst, send_sems, recv_sems, k, to):
    return pltpu.make_async_remote_copy(src_ref=src, dst_ref=dst, send_sem=send_sems.at[k], recv_sem=recv_sems.at[k],
                                        device_id=to, device_id_type=_MESH)


def _half_rows(ref, half, hf, lead=None):
    rows = pl.ds(pl.multiple_of(hf * half, 16), half)
    return ref.at[rows, :] if lead is None else ref.at[lead, rows, :]


class _Rider:
    def __init__(self, inputs, out_shape, n_sems, copies, aliases=None):
        self.inputs, self.out_shape, self.n_sems, self.copies = list(inputs), list(out_shape), n_sems, copies
        self.aliases = aliases or {}


def _carried_call(body, rider, first, last, *, name, grid, in_specs, out_specs, out_shape, scratch_shapes, sem, args):
    n_in, n_out, n_scr = len(in_specs), len(out_specs), len(scratch_shapes)
    if rider is None:
        res = pl.pallas_call(body, name=name, grid=grid, in_specs=in_specs, out_specs=out_specs, out_shape=out_shape,
                             scratch_shapes=scratch_shapes, compiler_params=_cparams(sem))(*args)
        return list(res), []
    ri, ro = len(rider.inputs), len(rider.out_shape)

    def full_body(*refs):
        own_in, r_in = refs[:n_in], refs[n_in:n_in + ri]
        o0 = n_in + ri
        own_out, r_out = refs[o0:o0 + n_out], refs[o0 + n_out:o0 + n_out + ro]
        s0 = o0 + n_out + ro
        own_scr, send_sems, recv_sems = refs[s0:s0 + n_scr], refs[s0 + n_scr], refs[s0 + n_scr + 1]

        @pl.when(first())
        def _():
            sends, _ = rider.copies(r_in, r_out, send_sems, recv_sems)
            for cp in sends:
                cp.start()

        body(*own_in, *own_out, *own_scr)

        @pl.when(last())
        def _():
            sends, arrivals = rider.copies(r_in, r_out, send_sems, recv_sems)
            for cp in arrivals():
                cp.wait_recv()
            for cp in sends:
                cp.wait_send()

    res = pl.pallas_call(
        full_body, name=name, grid=grid, in_specs=list(in_specs) + [_HBM] * ri, out_specs=list(out_specs) + [_HBM] * ro,
        out_shape=list(out_shape) + rider.out_shape,
        scratch_shapes=list(scratch_shapes) + [pltpu.SemaphoreType.DMA((rider.n_sems,))] * 2,
        input_output_aliases={n_in + i: n_out + o for i, o in rider.aliases.items()},
        compiler_params=_cparams(sem))(*args, *rider.inputs)
    return list(res[:n_out]), list(res[n_out:])


def _ride_gather_send(bufs):
    n = len(bufs)
    halves = [b.shape[0] // 2 for b in bufs]

    def copies(ins, outs, send_sems, recv_sems):
        x, y, c, chips = _place()
        slot = lambda b, cx, cy: _half_rows(outs[b], halves[b], c, lead=2 * cx + cy)
        sends = [_remote(_half_rows(ins[b], halves[b], c), slot(b, x, y), send_sems, recv_sems, 3 * b + j, (cx, cy, c))
                 for b in range(n) for j, (cx, cy) in enumerate(chips)]
        arrivals = lambda: [_remote(slot(b, cx, cy), slot(b, cx, cy), send_sems, recv_sems, 3 * b + j, (x, y, c))
                            for b in range(n) for j, (cx, cy) in enumerate(chips)]
        return sends, arrivals

    return _Rider(bufs, [jax.ShapeDtypeStruct((N_SHARD,) + b.shape, b.dtype) for b in bufs], 3 * n, copies)


def _ride_gather_pass(gathered):
    n = len(gathered)
    halves = [g.shape[1] // 2 for g in gathered]

    def copies(ins, outs, send_sems, recv_sems):
        x, y, c, chips = _place()
        slot = lambda b, cx, cy, hf: _half_rows(outs[b], halves[b], hf, lead=2 * cx + cy)
        sends = [_remote(slot(b, cx, cy, c), slot(b, cx, cy, c), send_sems, recv_sems, 3 * b + j, (x, y, 1 - c))
                 for b in range(n) for j, (cx, cy) in enumerate(chips)]
        arrivals = lambda: [_remote(slot(b, cx, cy, 1 - c), slot(b, cx, cy, 1 - c), send_sems, recv_sems, 3 * b + j, (x, y, c))
                            for b in range(n) for j, (cx, cy) in enumerate(chips)]
        return sends, arrivals

    return _Rider(gathered, [jax.ShapeDtypeStruct(g.shape, g.dtype) for g in gathered], 3 * n, copies,
                  aliases={b: b for b in range(n)})


def _ride_small_gather(buf):
    def copies(ins, outs, send_sems, recv_sems):
        x, y, c, _ = _place()
        flip = lambda v, d: 1 - v if d else v
        sends, peers = [], []
        for dx in (0, 1):
            for dy in (0, 1):
                for dc in (0, 1):
                    if dx or dy or dc:
                        k = 4 * dx + 2 * dy + dc - 1
                        px, py, pc = flip(x, dx), flip(y, dy), flip(c, dc)
                        sends.append(_remote(ins[0], outs[0].at[4 * x + 2 * y + c], send_sems, recv_sems, k, (px, py, pc)))
                        peers.append((k, 4 * px + 2 * py + pc))
        arrivals = lambda: [_remote(ins[0], outs[0].at[slot], send_sems, recv_sems, k, (x, y, c)) for k, slot in peers]
        return sends, arrivals

    return _Rider([buf], [jax.ShapeDtypeStruct((8,) + buf.shape, buf.dtype)], 7, copies)


def _small_sum(gathered, buf, me_arr):
    n, r, width = gathered.shape

    def body(me_ref, g_ref, b_ref, o_ref):
        total = jnp.zeros((r, width), F32)
        for d in range(n):
            total = total + jnp.where(me_ref[0] == d, b_ref[...], g_ref[d])
        o_ref[...] = total

    return pl.pallas_call(
        body, name="small_sum", out_shape=jax.ShapeDtypeStruct((r, width), F32),
        grid_spec=pltpu.PrefetchScalarGridSpec(
            num_scalar_prefetch=1, grid=(1,),
            in_specs=[pl.BlockSpec((n, r, width), lambda i, me: (0, 0, 0)), pl.BlockSpec((r, width), lambda i, me: (0, 0))],
            out_specs=pl.BlockSpec((r, width), lambda i, me: (0, 0))),
        compiler_params=_cparams(("arbitrary",)))(me_arr, gathered, buf)


def _ride_pair_exchange(gbufs):
    n = len(gbufs)
    halves = [g.shape[1] // 2 for g in gbufs]

    def copies(ins, outs, send_sems, recv_sems):
        x, y, c, _ = _place()
        sends = [_remote(ins[b].at[:, pl.ds(pl.multiple_of((1 - c) * halves[b], 16), halves[b]), :], outs[b],
                         send_sems, recv_sems, b, (x, y, 1 - c)) for b in range(n)]
        arrivals = lambda: [_remote(outs[b], outs[b], send_sems, recv_sems, b, (x, y, c)) for b in range(n)]
        return sends, arrivals

    return _Rider(gbufs, [jax.ShapeDtypeStruct((N_SHARD, h, g.shape[2]), g.dtype) for g, h in zip(gbufs, halves)], n, copies)


def _ride_chip_exchange(parts):
    n = len(parts)

    def copies(ins, outs, send_sems, recv_sems):
        x, y, c, chips = _place()
        sends = [_remote(ins[b].at[2 * cx + cy], outs[b].at[j], send_sems, recv_sems, 3 * b + j, (cx, cy, c))
                 for b in range(n) for j, (cx, cy) in enumerate(chips)]
        arrivals = lambda: [_remote(ins[b].at[0], outs[b].at[j], send_sems, recv_sems, 3 * b + j, (x, y, c))
                            for b in range(n) for j in range(len(chips))]
        return sends, arrivals

    return _Rider(parts, [jax.ShapeDtypeStruct((3,) + p.shape[1:], p.dtype) for p in parts], 3 * n, copies)


def _gather_shards(bufs, name):
    n = len(bufs)
    halves = [b.shape[0] // 2 for b in bufs]

    def body(*refs):
        ins, outs, send_sems, recv_sems = refs[:n], refs[n:2 * n], refs[2 * n], refs[2 * n + 1]
        x, y, c, chips = _place()
        me, sibling = (x, y, c), (x, y, 1 - c)
        slot = lambda b, cx, cy, hf: _half_rows(outs[b], halves[b], hf, lead=2 * cx + cy)
        first = [_remote(_half_rows(ins[b], halves[b], c), slot(b, x, y, c), send_sems, recv_sems, 6 * b + j, (cx, cy, c))
                 for b in range(n) for j, (cx, cy) in enumerate(chips)]
        for cp in first:
            cp.start()
        passed = []
        for j, (cx, cy) in enumerate(chips):
            for b in range(n):
                _remote(slot(b, cx, cy, c), slot(b, cx, cy, c), send_sems, recv_sems, 6 * b + j, me).wait_recv()
                fwd = _remote(slot(b, cx, cy, c), slot(b, cx, cy, c), send_sems, recv_sems, 6 * b + 3 + j, sibling)
                fwd.start()
                passed.append(fwd)
        for j, (cx, cy) in enumerate(chips):
            for b in range(n):
                _remote(slot(b, cx, cy, 1 - c), slot(b, cx, cy, 1 - c), send_sems, recv_sems, 6 * b + 3 + j, me).wait_recv()
        for cp in first + passed:
            cp.wait_send()

    return pl.pallas_call(
        body, name=name, out_shape=[jax.ShapeDtypeStruct((N_SHARD,) + b.shape, b.dtype) for b in bufs],
        in_specs=[_HBM] * n, out_specs=[_HBM] * n,
        scratch_shapes=[pltpu.SemaphoreType.DMA((6 * n,)), pltpu.SemaphoreType.DMA((6 * n,))],
    )(*bufs)


def _reduce_pair_exchange(gbufs, name):
    n = len(gbufs)
    halves = [g.shape[1] // 2 for g in gbufs]

    def body(*refs):
        ins, outs, send_sems, recv_sems = refs[:n], refs[n:2 * n], refs[2 * n], refs[2 * n + 1]
        x, y, c, _ = _place()
        cps = [_remote(ins[b].at[:, pl.ds(pl.multiple_of((1 - c) * halves[b], 16), halves[b]), :], outs[b],
                       send_sems, recv_sems, b, (x, y, 1 - c)) for b in range(n)]
        for cp in cps:
            cp.start()
        for cp in cps:
            cp.wait()

    return pl.pallas_call(
        body, name=name,
        out_shape=[jax.ShapeDtypeStruct((N_SHARD, h, g.shape[2]), g.dtype) for g, h in zip(gbufs, halves)],
        in_specs=[_HBM] * n, out_specs=[_HBM] * n,
        scratch_shapes=[pltpu.SemaphoreType.DMA((n,)), pltpu.SemaphoreType.DMA((n,))],
    )(*gbufs)


def _pair_add(gbuf, recv, c_arr, tr, name):
    _, rows, width = gbuf.shape
    half = rows // 2
    nt = half // tr

    def body(c_ref, a_ref, b_ref, o_ref):
        o_ref[...] = (a_ref[...] + b_ref[...]).astype(o_ref.dtype)

    blk = lambda f: pl.BlockSpec((None, tr, width), f)
    return pl.pallas_call(
        body, name=name, out_shape=jax.ShapeDtypeStruct((N_SHARD, half, width), jnp.bfloat16),
        grid_spec=pltpu.PrefetchScalarGridSpec(
            num_scalar_prefetch=1, grid=(N_SHARD, nt),
            in_specs=[blk(lambda s, i, c: (s, c[0] * nt + i, 0)), blk(lambda s, i, c: (s, i, 0))],
            out_specs=blk(lambda s, i, c: (s, i, 0))),
        compiler_params=_cparams(("parallel", "parallel")))(c_arr, gbuf, recv)


def _chip_add(part, recv, me_arr, tr, name):
    _, half, width = part.shape

    def body(me_ref, own, a0, a1, a2, o_ref):
        f = lambda r: r[...].astype(F32)
        o_ref[...] = ((f(own) + f(a0)) + f(a1)) + f(a2)

    specs = [pl.BlockSpec((None, tr, width), lambda i, me: (me[0], i, 0))]
    specs += [pl.BlockSpec((None, tr, width), functools.partial(lambda i, me, k: (k, i, 0), k=k)) for k in range(3)]
    return pl.pallas_call(
        body, name=name, out_shape=jax.ShapeDtypeStruct((half, width), F32),
        grid_spec=pltpu.PrefetchScalarGridSpec(
            num_scalar_prefetch=1, grid=(half // tr,), in_specs=specs,
            out_specs=pl.BlockSpec((tr, width), lambda i, me: (i, 0))),
        compiler_params=_cparams(("parallel",)))(me_arr, part, recv, recv, recv)


def _reduce_pair_share(rhalves, name):
    n = len(rhalves)

    def body(*refs):
        ins, outs, send_sems, recv_sems = refs[:n], refs[n:2 * n], refs[2 * n], refs[2 * n + 1]
        x, y, c, _ = _place()
        cps = [_remote(ins[b], outs[b], send_sems, recv_sems, b, (x, y, 1 - c)) for b in range(n)]
        for cp in cps:
            cp.start()
        for cp in cps:
            cp.wait()

    return pl.pallas_call(
        body, name=name, out_shape=[jax.ShapeDtypeStruct(r.shape, r.dtype) for r in rhalves],
        in_specs=[_HBM] * n, out_specs=[_HBM] * n,
        scratch_shapes=[pltpu.SemaphoreType.DMA((n,)), pltpu.SemaphoreType.DMA((n,))],
    )(*rhalves)


def _row_tile(rows, cap):
    if rows <= cap:
        return rows
    t = (cap // 8) * 8
    while t >= 8:
        if rows % t == 0:
            return t
        t -= 8
    return rows


def _adamw(w, g, m, v, name):
    shape = w.shape
    cols = shape[-1] if len(shape) <= 3 else shape[-2] * shape[-1]
    lead = len(shape) == 3
    w2, g2, m2, v2 = (a if lead else a.reshape(-1, cols) for a in (w, g, m, v))
    rows = shape[1] if lead else w2.shape[0]
    tr, tc = _row_tile(rows, 256), cols
    if tr == rows and rows > 256:
        tc = _tile(cols, 256)

    def body(w_ref, g_ref, m_ref, v_ref, d_ref, mo_ref, vo_ref):
        gv = g_ref[...]
        mn = ADAM_B1 * m_ref[...] + (1.0 - ADAM_B1) * gv
        vn = ADAM_B2 * v_ref[...] + (1.0 - ADAM_B2) * (gv * gv)
        m_hat = mn / (1.0 - ADAM_B1 ** ADAM_STEP)
        v_hat = vn / (1.0 - ADAM_B2 ** ADAM_STEP)
        d_ref[...] = -ADAM_LR * (m_hat / (jnp.sqrt(v_hat) + ADAM_EPS) + ADAM_WD * w_ref[...])
        mo_ref[...] = mn
        vo_ref[...] = vn

    blk = (pl.BlockSpec((None, tr, tc), lambda i, j: (0, i, j)) if lead else pl.BlockSpec((tr, tc), lambda i, j: (i, j)))
    outs = pl.pallas_call(
        body, name=name, grid=(rows // tr, cols // tc), in_specs=[blk] * 4, out_specs=[blk] * 3,
        out_shape=[jax.ShapeDtypeStruct(w2.shape, F32)] * 3,
        compiler_params=_cparams(("parallel", "parallel")))(w2, g2, m2, v2)
    return tuple(o.reshape(shape) for o in outs)


_WEIGHT_NAMES = ('w_in', 'conv_w', 'dn_a_log', 'dn_dt_bias', 'dn_norm_w', 'q_norm_w', 'w_uq', 'kv_norm_w', 'w_uk',
                 'w_uv', 'w_br_dn', 'w_br_mla', 'w_o', 'ln1_g', 'ln1_b', 'w_ffn_in', 'w_ffn_out', 'w_ple',
                 'w_ple_gate', 'ln2_g', 'ln2_b')
_SMALL_NAMES = ('ln1_g', 'ln1_b', 'ln2_g', 'ln2_b', 'q_norm_w', 'kv_norm_w', 'dn_norm_w', 'dn_a_log', 'dn_dt_bias')
_SMALL_GROUP = 8
_CONV_SMALL_ROW = len(_SMALL_NAMES) * _SMALL_GROUP
_CONV_SMALL_ROWS = DN_CONV * QKV_W // FLAT_W


_LOSS_SMALL_ROW = _CONV_SMALL_ROW + 16


def _pack_small(gw, loss_lanes):
    rows = [jnp.pad(gw[n][None, :], ((0, _SMALL_GROUP - 1), (0, FLAT_W - gw[n].shape[0]))) for n in _SMALL_NAMES]
    rows.append(jnp.pad(gw['conv_w'].reshape(_CONV_SMALL_ROWS, FLAT_W), ((0, 16 - _CONV_SMALL_ROWS), (0, 0))))
    rows.append(jnp.pad(loss_lanes, ((0, _SMALL_GROUP - 1), (0, FLAT_W - LANE))))
    return jnp.concatenate(rows, axis=0)


class _Exchange:
    def __init__(self, local, me_chip, c_arr):
        self.local, self.me_chip, self.c_arr = local, me_chip, c_arr
        self.parts = self.arrived = None

    def gather_send(self):
        return _ride_gather_send(self.local)

    def gather_pass(self, sent):
        return _ride_gather_pass(sent)

    def weights(self, gathered):
        return _prep_weights(_unpack_rest(gathered, self.local, self.me_chip))

    def pair_send(self, g):
        self.gbufs = _pack_grads_rest(_unprep_grads_early(g))
        return _ride_pair_exchange(self.gbufs)

    def reduce_send(self, got):
        self.parts = [_pair_add(g_, r_, self.c_arr, tr, "pair_add_%d" % (i + 1))
                      for i, (g_, r_, tr) in enumerate(zip(self.gbufs, got, _ADD_TILES[1:]))]
        return _ride_chip_exchange(self.parts)

    def reduce_arrived(self, arrived):
        self.arrived = list(arrived)

    def in_send(self, g):
        g_in = [_shard_columns(_w_in_grad(g), W_IN_SHARD)]
        got = _reduce_pair_exchange(g_in, "reduce_pair_exchange_w_in")
        self.part_in = _pair_add(g_in[0], got[0], self.c_arr, _ADD_TILES[0], "pair_add_0")
        return _ride_chip_exchange([self.part_in])

    def in_arrived(self, arrived):
        self.arrived_in = list(arrived)

    def small_send(self, g):
        self.small = _pack_small(_unprep_grads_late(g), g['loss_lanes'])
        return _ride_small_gather(self.small)

    def small_arrived(self, arrived):
        self.small_gathered = arrived[0]


def kernel(x, p, positions, w_in, conv_w, dn_a_log, dn_dt_bias, dn_norm_w, q_norm_w, w_uq, kv_norm_w, w_uk, w_uv, w_br_dn, w_br_mla, w_o, ln1_g, ln1_b, w_ffn_in, w_ffn_out, w_ple, w_ple_gate, ln2_g, ln2_b, loss_target, m_w_in, m_conv_w, m_dn_a_log, m_dn_dt_bias, m_dn_norm_w, m_q_norm_w, m_w_uq, m_kv_norm_w, m_w_uk, m_w_uv, m_w_br_dn, m_w_br_mla, m_w_o, m_ln1_g, m_ln1_b, m_w_ffn_in, m_w_ffn_out, m_w_ple, m_w_ple_gate, m_ln2_g, m_ln2_b, v_w_in, v_conv_w, v_dn_a_log, v_dn_dt_bias, v_dn_norm_w, v_q_norm_w, v_w_uq, v_kv_norm_w, v_w_uk, v_w_uv, v_w_br_dn, v_w_br_mla, v_w_o, v_ln1_g, v_ln1_b, v_w_ffn_in, v_w_ffn_out, v_w_ple, v_w_ple_gate, v_ln2_g, v_ln2_b):
    ws = dict(w_in=w_in, conv_w=conv_w, dn_a_log=dn_a_log, dn_dt_bias=dn_dt_bias, dn_norm_w=dn_norm_w, q_norm_w=q_norm_w,
              w_uq=w_uq, kv_norm_w=kv_norm_w, w_uk=w_uk, w_uv=w_uv, w_br_dn=w_br_dn, w_br_mla=w_br_mla, w_o=w_o,
              ln1_g=ln1_g, ln1_b=ln1_b, w_ffn_in=w_ffn_in, w_ffn_out=w_ffn_out, w_ple=w_ple, w_ple_gate=w_ple_gate,
              ln2_g=ln2_g, ln2_b=ln2_b)
    ms = dict(w_in=m_w_in, conv_w=m_conv_w, dn_a_log=m_dn_a_log, dn_dt_bias=m_dn_dt_bias, dn_norm_w=m_dn_norm_w,
              q_norm_w=m_q_norm_w, w_uq=m_w_uq, kv_norm_w=m_kv_norm_w, w_uk=m_w_uk, w_uv=m_w_uv, w_br_dn=m_w_br_dn,
              w_br_mla=m_w_br_mla, w_o=m_w_o, ln1_g=m_ln1_g, ln1_b=m_ln1_b, w_ffn_in=m_w_ffn_in, w_ffn_out=m_w_ffn_out,
              w_ple=m_w_ple, w_ple_gate=m_w_ple_gate, ln2_g=m_ln2_g, ln2_b=m_ln2_b)
    vs = dict(w_in=v_w_in, conv_w=v_conv_w, dn_a_log=v_dn_a_log, dn_dt_bias=v_dn_dt_bias, dn_norm_w=v_dn_norm_w,
              q_norm_w=v_q_norm_w, w_uq=v_w_uq, kv_norm_w=v_kv_norm_w, w_uk=v_w_uk, w_uv=v_w_uv, w_br_dn=v_w_br_dn,
              w_br_mla=v_w_br_mla, w_o=v_w_o, ln1_g=v_ln1_g, ln1_b=v_ln1_b, w_ffn_in=v_w_ffn_in, w_ffn_out=v_w_ffn_out,
              w_ple=v_w_ple, w_ple_gate=v_w_ple_gate, ln2_g=v_ln2_g, ln2_b=v_ln2_b)
    mx, my, mc = lax.axis_index("x"), lax.axis_index("y"), lax.axis_index("c")

    me_chip = 2 * mx + my
    c_arr = jnp.reshape(mc, (1,)).astype(jnp.int32)
    me_arr = jnp.reshape(me_chip, (1,)).astype(jnp.int32)
    sharded = ('w_in', 'w_ffn_in') + tuple(name for name, _, _ in _FLATB_PIECES)

    local = _pack_shards({name: ws[name][0] for name in sharded}, conv_w[0])
    (gathered_in,) = _gather_shards(local[:1], "gather_w_in")
    w_in_full, conv_full = _unpack_w_in(gathered_in, local[0], me_chip)
    small = {n: ws[n][0] for n in _SMALL_NAMES}
    small['conv_w'] = conv_full
    sp = _prep_small(small)
    cosb, sinb = _rope_tables(positions[0])
    exch = _Exchange(local[1:], me_chip, c_arr)

    loss_lanes, dx, g = _local_step(x[0], p[0, 0], cosb, sinb, loss_target[0], _prep_w_in(w_in_full), sp, exch)

    parts = [exch.part_in] + exch.parts
    arrived = exch.arrived_in + exch.arrived
    mine = [_chip_add(p_, r_, me_arr, tr, "chip_add_%d" % i) for i, (p_, r_, tr) in enumerate(zip(parts, arrived, _ADD_TILES))]
    reduced = _unpack_reduced(mine, _reduce_pair_share(mine, "reduce_pair_share"), mc)
    tot = _small_sum(exch.small_gathered, exch.small, jnp.reshape(4 * mx + 2 * my + mc, (1,)).astype(jnp.int32))
    loss = jnp.sum(tot[_LOSS_SMALL_ROW, :LANE])
    gred = {name: reduced[name][None] for name in sharded}
    for i, n in enumerate(_SMALL_NAMES):
        gred[n] = tot[i * _SMALL_GROUP, :ws[n].shape[1]][None]
    conv_tot = tot[_CONV_SMALL_ROW:_CONV_SMALL_ROW + _CONV_SMALL_ROWS].reshape(DN_CONV, QKV_W)
    gred['conv_w'] = lax.dynamic_slice_in_dim(conv_tot, (2 * mx + my) * _CONV_SHARD, _CONV_SHARD, axis=1)[None]

    deltas, new_m, new_v = {}, {}, {}
    for n in _WEIGHT_NAMES:
        if n == 'w_in':
            tr_ = lambda a: jnp.transpose(a, (0, 2, 1))
            g_t = tr_(gred[n].reshape(ws[n].shape))
            outs = _adamw(tr_(ws[n]), g_t, tr_(ms[n]), tr_(vs[n]), "adamw_" + n)
            gred[n] = tr_(g_t)
            deltas[n], new_m[n], new_v[n] = (tr_(o) for o in outs)
            continue
        gred[n] = gred[n].reshape(ws[n].shape)
        deltas[n], new_m[n], new_v[n] = _adamw(ws[n], gred[n], ms[n], vs[n], "adamw_" + n)
    return (loss, dx[None], *[gred[n] for n in _WEIGHT_NAMES], *[deltas[n] for n in _WEIGHT_NAMES],
            *[new_m[n] for n in _WEIGHT_NAMES], *[new_v[n] for n in _WEIGHT_NAMES])
```

```python
import functools

import jax
import jax.numpy as jnp
from jax import lax
from jax.experimental import pallas as pl
from jax.experimental.pallas import tpu as pltpu

F32 = jnp.float32
_CDT = jnp.bfloat16
_MESH = pl.DeviceIdType.MESH

D_MODEL = 1024
PLE_DIM = 256
HEADS = 8
DN_DK = 128
DN_CHUNK = 64
DN_CONV = 4
QKV_W = 3 * HEADS * DN_DK
Q_LORA = 384
KV_LORA = 256
NOPE = 128
ROPE = 64
ROPE_PAD = 128
FFN_HIDDEN = 2816
D_IN = 6864
ROPE_BASE = 10000.0
ALPHA = 2.0 ** 0.25
ATT_SCALE = (NOPE + ROPE) ** -0.5
NEG_BIG = -1e30
ADAM_LR, ADAM_B1, ADAM_B2, ADAM_EPS, ADAM_WD, ADAM_STEP = 0.001, 0.9, 0.999, 1e-08, 0.01, 10

LANE = 128
VMEM_LIMIT = 56 * 1024 * 1024
MM_VMEM_BUDGET = 40 * 1024 * 1024
N_SHARD = 4
FLAT_W = 1024
SMALL_ROWS = 96


def _tile(dim, cap):
    if dim <= cap:
        return dim
    t = (cap // LANE) * LANE
    while t >= LANE:
        if dim % t == 0:
            return t
        t -= LANE
    return dim


def _cparams(sem):
    return pltpu.CompilerParams(dimension_semantics=sem, vmem_limit_bytes=VMEM_LIMIT)


def _mm(a, b, *, name, ta=False, tb=False, add=None, add_scale=1.0, out_dtype=F32, heads=None,
        a_head=None, b_head=None, out_head=None, dims=None, tm=1408, tn=1408, rider=None):
    m, n, k = dims
    tm, tn = _tile(m, tm), _tile(n, tn)
    sa, sb, so = a.dtype.itemsize, b.dtype.itemsize, jnp.dtype(out_dtype).itemsize

    def vmem_need(tk_):
        acc = tm * tn * 4 if tk_ < k else 0
        extra = 2 * tm * tn * 4 if add is not None else 0
        return 2 * (tm * tk_ * sa + tk_ * tn * sb) + 2 * tm * tn * so + acc + extra

    tk = k
    while vmem_need(tk) > MM_VMEM_BUDGET and tk > LANE:
        smaller = _tile(k, tk - LANE)
        if smaller >= tk:
            break
        tk = smaller
    nk = k // tk
    hgrid = () if heads is None else (heads,)
    off = len(hgrid)

    def spec(rows, cols, rtile, ctile, rsel, csel, layout):
        def idx(*g):
            h = g[0] if off else 0
            ri, ci = g[off + rsel], g[off + csel]
            if layout == 'lead':
                return (h, ri, ci)
            if layout == 'col':
                return (ri, h * (cols // ctile) + ci)
            return (ri, ci)
        if layout == 'lead':
            return pl.BlockSpec((None, rtile, ctile), idx)
        return pl.BlockSpec((rtile, ctile), idx)

    a_spec = spec(k, m, tk, tm, 2, 0, a_head) if ta else spec(m, k, tm, tk, 0, 2, a_head)
    b_spec = spec(n, k, tn, tk, 1, 2, b_head) if tb else spec(k, n, tk, tn, 2, 1, b_head)
    o_spec = spec(m, n, tm, tn, 0, 1, out_head)
    in_specs = [a_spec, b_spec]
    args = [a, b]
    if add is not None:
        in_specs.append(spec(m, n, tm, tn, 0, 1, out_head))
        args.append(add)
    dn = (((0 if ta else 1,), (1 if tb else 0,)), ((), ()))

    def body(*refs):
        a_ref, b_ref = refs[0], refs[1]
        prod = lax.dot_general(a_ref[...].astype(_CDT), b_ref[...].astype(_CDT), dn, preferred_element_type=F32)
        if nk == 1:
            o_ref = refs[-1]
            if add is not None:
                prod = prod + refs[2][...].astype(F32) * add_scale
            o_ref[...] = prod.astype(out_dtype)
            return
        o_ref, acc_ref = refs[-2], refs[-1]
        kk = pl.program_id(off + 2)

        @pl.when(kk == 0)
        def _():
            if add is not None:
                acc_ref[...] = refs[2][...].astype(F32) * add_scale
            else:
                acc_ref[...] = jnp.zeros_like(acc_ref)

        acc_ref[...] += prod

        @pl.when(kk == nk - 1)
        def _():
            o_ref[...] = acc_ref[...].astype(out_dtype)

    if out_head == 'lead':
        oshape = (heads, m, n)
    elif out_head == 'col':
        oshape = (m, heads * n)
    else:
        oshape = (m, n)
    grid = hgrid + (m // tm, n // tn, nk)
    scratch = [pltpu.VMEM((tm, tn), F32)] if nk > 1 else []
    if rider is not None:
        (out,), carried = _carried_call(
            body, rider, *_grid_ends(grid), name=name, grid=grid, in_specs=in_specs, out_specs=[o_spec],
            out_shape=[jax.ShapeDtypeStruct(oshape, out_dtype)], scratch_shapes=scratch,
            sem=("arbitrary",) * len(grid), args=tuple(args))
        return out, carried
    sem = ("parallel",) * (off + 2) + ("arbitrary",)
    return pl.pallas_call(
        body, name=name, grid=grid, in_specs=in_specs, out_specs=o_spec,
        out_shape=jax.ShapeDtypeStruct(oshape, out_dtype), scratch_shapes=scratch,
        compiler_params=_cparams(sem))(*args)


def _mm2(a, b, **kw):
    ta, tb = kw.get('ta', False), kw.get('tb', False)
    m = a.shape[1] if ta else a.shape[0]
    k = a.shape[0] if ta else a.shape[1]
    n = b.shape[0] if tb else b.shape[1]
    return _mm(a, b, dims=(m, n, k), **kw)


def _rowwise(fn, rows, bcast, outs, reds=(), *, name, tm=256, heads=None):
    t = rows[0][0].shape[0]
    tm = min(tm, t)
    hn = 1 if heads is None else heads
    in_specs, args = [], []
    for arr, width, base, per_head in rows:
        in_specs.append(pl.BlockSpec((tm, width), functools.partial(
            lambda i, h, base, per_head: (i, base + (h if per_head else 0)), base=base, per_head=per_head)))
        args.append(arr)
    for arr in bcast:
        in_specs.append(pl.BlockSpec(arr.shape, lambda i, h: (0, 0)))
        args.append(arr)
    out_specs, out_shape = [], []
    for total, width, per_head, dt in outs:
        out_specs.append(pl.BlockSpec((tm, width), functools.partial(
            lambda i, h, per_head: (i, h if per_head else 0), per_head=per_head)))
        out_shape.append(jax.ShapeDtypeStruct((t, total), dt))
    for shp in reds:
        out_specs.append(pl.BlockSpec(shp, lambda i, h: (0, 0)))
        out_shape.append(jax.ShapeDtypeStruct(shp, F32))
    n_in, n_out, n_red = len(args), len(outs), len(reds)

    def body(*refs):
        i, h = pl.program_id(0), pl.program_id(1)
        vals = fn(h, *[r[...] for r in refs[:n_in]])
        for r, v in zip(refs[n_in:n_in + n_out], vals[:n_out]):
            r[...] = v.astype(r.dtype)
        if n_red:
            @pl.when((i == 0) & (h == 0))
            def _():
                for r in refs[n_in + n_out:]:
                    r[...] = jnp.zeros_like(r)
            for r, v in zip(refs[n_in + n_out:], vals[n_out:]):
                r[...] += v

    sem = ("arbitrary", "arbitrary") if n_red else ("parallel", "parallel")
    res = pl.pallas_call(body, name=name, grid=(t // tm, hn), in_specs=in_specs, out_specs=out_specs,
                         out_shape=out_shape, compiler_params=_cparams(sem))(*args)
    return tuple(res)


def _sigmoid(x):
    return 1.0 / (1.0 + jnp.exp(-x))


def _silu(x):
    return x * _sigmoid(x)


def _softplus(x):
    return jnp.maximum(x, 0.0) + jnp.log(1.0 + jnp.exp(-jnp.abs(x)))


def _layer_norm(t, g, b):
    mu = jnp.mean(t, axis=-1, keepdims=True)
    d = t - mu
    var = jnp.mean(d * d, axis=-1, keepdims=True)
    return d * lax.rsqrt(var + 1e-5) * g + b


def _rms_norm(t, w):
    return t * lax.rsqrt(jnp.mean(t * t, axis=-1, keepdims=True) + 1e-6) * w


def _swap_rope_halves(t):
    lane = lax.broadcasted_iota(jnp.int32, t.shape, 1) % ROPE_PAD
    n = t.shape[1]
    up = pltpu.roll(t, n - ROPE // 2, axis=1)
    dn = pltpu.roll(t, ROPE // 2, axis=1)
    return jnp.where(lane < ROPE // 2, up, jnp.where(lane < ROPE, dn, 0.0))


def _rope(t, cosb, sinb):
    reps = t.shape[1] // ROPE_PAD
    c = jnp.tile(cosb, (1, reps)) if reps > 1 else cosb
    s = jnp.tile(sinb, (1, reps)) if reps > 1 else sinb
    return t * c + _swap_rope_halves(t) * s


def _rope_bwd(d, cosb, sinb):
    reps = d.shape[1] // ROPE_PAD
    c = jnp.tile(cosb, (1, reps)) if reps > 1 else cosb
    s = jnp.tile(sinb, (1, reps)) if reps > 1 else sinb
    return d * c + _swap_rope_halves(d * s)


_CONV_ROWS = 256
_CONV_COLS = 256


def _conv_window(ref, r0, lo, hi, t):
    parts = []
    start, stop = r0 - lo, r0 + _CONV_ROWS + hi
    if start < 0:
        parts.append(jnp.zeros((-start, ref.shape[1]), F32))
        start = 0
    tail = max(stop - t, 0)
    parts.append(ref[start:stop - tail, :].astype(F32))
    if tail:
        parts.append(jnp.zeros((tail, ref.shape[1]), F32))
    return parts[0] if len(parts) == 1 else jnp.concatenate(parts, axis=0)


def _conv_taps(win, w_ref, n_out):
    acc = win[8:8 + n_out] * w_ref[DN_CONV - 1:DN_CONV, :]
    for i in range(DN_CONV - 1):
        acc = acc + pltpu.roll(win, DN_CONV - 1 - i, axis=0)[8:8 + n_out] * w_ref[i:i + 1, :]
    return acc


def _conv_silu(x, w):
    t, ch = x.shape

    def body(x_ref, w_ref, o_ref):
        for r in range(t // _CONV_ROWS):
            r0 = r * _CONV_ROWS
            c = _conv_taps(_conv_window(x_ref, r0, 8, 0, t), w_ref, _CONV_ROWS)
            o_ref[r0:r0 + _CONV_ROWS, :] = _silu(c)

    return pl.pallas_call(
        body, name="conv_silu", grid=(ch // _CONV_COLS,),
        in_specs=[pl.BlockSpec((t, _CONV_COLS), lambda j: (0, j)), pl.BlockSpec((DN_CONV, _CONV_COLS), lambda j: (0, j))],
        out_specs=pl.BlockSpec((t, _CONV_COLS), lambda j: (0, j)),
        out_shape=jax.ShapeDtypeStruct((t, ch), F32), compiler_params=_cparams(("parallel",)))(x, w)


def _conv_silu_bwd(x, w, dys):
    t, ch = x.shape
    per = ch // len(dys) // _CONV_COLS

    def body(x_ref, w_ref, *rest):
        dy_refs, (dx_ref, dw_ref) = rest[:len(dys)], rest[len(dys):]
        sec = pl.program_id(0) // per
        dws = [jnp.zeros((1, _CONV_COLS), F32) for _ in range(DN_CONV)]
        for r in range(t // _CONV_ROWS):
            r0 = r * _CONV_ROWS
            n_ext = _CONV_ROWS + 8
            xw = _conv_window(x_ref, r0, 8, 8, t)
            c = _conv_taps(xw, w_ref, n_ext)
            sg = _sigmoid(c)
            dy = _conv_window(dy_refs[-1], r0, 0, 8, t)
            for k in range(len(dys) - 2, -1, -1):
                dy = jnp.where(sec == k, _conv_window(dy_refs[k], r0, 0, 8, t), dy)
            ds = dy * (sg * (1.0 + c * (1.0 - sg)))
            x0 = xw[8:8 + _CONV_ROWS]
            dx = jnp.zeros((_CONV_ROWS, _CONV_COLS), F32)
            for i in range(DN_CONV):
                sh = DN_CONV - 1 - i
                ds_up = (ds if sh == 0 else pltpu.roll(ds, n_ext - sh, axis=0))[:_CONV_ROWS]
                dx = dx + ds_up * w_ref[i:i + 1, :]
                dws[i] = dws[i] + jnp.sum(x0 * ds_up, axis=0, keepdims=True)
            dx_ref[r0:r0 + _CONV_ROWS, :] = dx.astype(dx_ref.dtype)
        for i in range(DN_CONV):
            dw_ref[i:i + 1, :] = dws[i]

    blk = pl.BlockSpec((t, _CONV_COLS), lambda j: (0, j))
    wblk = pl.BlockSpec((DN_CONV, _CONV_COLS), lambda j: (0, j))
    dy_specs = [pl.BlockSpec((t, _CONV_COLS), functools.partial(lambda j, k: (0, jnp.clip(j - k * per, 0, per - 1)), k=k))
                for k in range(len(dys))]
    return pl.pallas_call(
        body, name="conv_silu_bwd", grid=(ch // _CONV_COLS,), in_specs=[blk, wblk] + dy_specs, out_specs=[blk, wblk],
        out_shape=[jax.ShapeDtypeStruct((t, ch), _CDT), jax.ShapeDtypeStruct((DN_CONV, ch), F32)],
        compiler_params=_cparams(("arbitrary",)))(x, w, *dys)


_PA_ROWS = 1024
_PA_ROWS_FWD = 1024


def _bmm(a, b, spec):
    return jnp.einsum(spec, a.astype(_CDT), b.astype(_CDT), preferred_element_type=F32)


def _split16(a):
    hi = a.astype(jnp.bfloat16)
    return hi, (a - hi.astype(F32)).astype(jnp.bfloat16)


def _bmm3(a, b, spec):
    ah, al = _split16(a)
    bh, bl = _split16(b)
    e = lambda p, q: jnp.einsum(spec, p, q, preferred_element_type=F32)
    return e(ah, bh) + (e(ah, bl) + e(al, bh))


def _split3(b):
    b0 = b.astype(jnp.bfloat16)
    r1 = b - b0.astype(F32)
    b1 = r1.astype(jnp.bfloat16)
    return b0, b1, (r1 - b1.astype(F32)).astype(jnp.bfloat16)


@functools.partial(jax.custom_vjp, nondiff_argnums=(2, 3))
def _select_mm(sel, b, spec, spec_t):
    return sum(jnp.einsum(spec, sel, t, preferred_element_type=F32) for t in _split3(b))


def _select_mm_fwd(sel, b, spec, spec_t):
    return _select_mm(sel, b, spec, spec_t), sel


def _select_mm_bwd(spec, spec_t, sel, ct):
    return jnp.zeros_like(sel), sum(jnp.einsum(spec_t, sel, t, preferred_element_type=F32) for t in _split3(ct))


_select_mm.defvjp(_select_mm_fwd, _select_mm_bwd)


def _tri_inverse(l_mat, eye):
    pw = -l_mat
    t_inv = eye + pw
    for _ in range(5):
        pw = _bmm3(pw, pw, 'bij,bjk->bik')
        t_inv = t_inv + _bmm3(t_inv, pw, 'bij,bjk->bik')
    return t_inv


@jax.custom_vjp
def _tri_inverse_saved(l_mat, t_saved):
    return t_saved


def _tri_inverse_saved_fwd(l_mat, t_saved):
    return t_saved, t_saved


def _tri_inverse_saved_bwd(t_saved, dt):
    left = _bmm3(t_saved, dt, 'bji,bjk->bik')
    return -_bmm3(left, t_saved, 'bij,bkj->bik'), jnp.zeros_like(t_saved)


_tri_inverse_saved.defvjp(_tri_inverse_saved_fwd, _tri_inverse_saved_bwd)


def _phase_a(h, q, k, v, ba, alog, dtb, t_saved=None):
    r = q.shape[0]
    nb = r // DN_CHUNK
    c = DN_CHUNK
    lane = lax.broadcasted_iota(jnp.int32, (1, LANE), 1)
    selb = (lane == h).astype(F32)
    sela = (lane == h + HEADS).astype(F32)
    b_raw = jnp.sum(ba * selb, axis=1, keepdims=True)
    a_raw = jnp.sum(ba * sela, axis=1, keepdims=True)
    al = jnp.sum(alog * selb, axis=1, keepdims=True)
    dt = jnp.sum(dtb * selb, axis=1, keepdims=True)
    beta = jnp.broadcast_to(_sigmoid(b_raw), (r, LANE))
    g = jnp.broadcast_to(-jnp.exp(al) * _softplus(a_raw + dt), (r, LANE))
    qn = q * lax.rsqrt(jnp.sum(q * q, -1, keepdims=True) + 1e-6) * (DN_DK ** -0.5)
    kn = k * lax.rsqrt(jnp.sum(k * k, -1, keepdims=True) + 1e-6)
    q3, k3, v3 = qn.reshape(nb, c, LANE), kn.reshape(nb, c, LANE), v.reshape(nb, c, LANE)
    b3, g3 = beta.reshape(nb, c, LANE), g.reshape(nb, c, LANE)
    ri = lax.broadcasted_iota(jnp.int32, (nb, c, c), 1)
    ci = lax.broadcasted_iota(jnp.int32, (nb, c, c), 2)
    tril, strict = ri >= ci, ri > ci
    gc = _select_mm(tril.astype(jnp.bfloat16), g3, 'bij,bjd->bid', 'bij,bid->bjd')
    onehot = (lax.broadcasted_iota(jnp.int32, (nb, c, LANE), 2) == 0).astype(jnp.bfloat16)
    g_row = _select_mm(onehot, gc, 'bid,bjd->bij', 'bid,bij->bjd')
    diff = gc[:, :, :c] - g_row
    decay = jnp.where(tril, jnp.exp(jnp.where(tril, diff, 0.0)), 0.0)
    kb = k3 * b3
    l_mat = jnp.where(strict, _bmm(kb, k3, 'bid,bjd->bij') * decay, 0.0)
    if t_saved is None:
        t_inv = _tri_inverse(l_mat, (ri == ci).astype(F32))
    else:
        t_inv = _tri_inverse_saved(l_mat, t_saved.reshape(nb, c, c))
    eg = jnp.exp(gc)
    u = _bmm(t_inv, v3 * b3, 'bij,bje->bie')
    w = _bmm(t_inv, kb * eg, 'bij,bje->bie')
    intra = jnp.where(tril, _bmm(q3, k3, 'bid,bjd->bij') * decay, 0.0)
    qd = q3 * eg
    gl = jnp.sum(g3, axis=1, keepdims=True)
    kt = k3 * jnp.exp(gl - gc)
    outs = (u.reshape(r, LANE), w.reshape(r, LANE), qd.reshape(r, LANE), kt.reshape(r, LANE),
            intra.reshape(r, c), gl.reshape(nb, LANE))
    if t_saved is not None:
        return outs
    qd2 = qd - _bmm(intra, w, 'bij,bjd->bid')
    au = _bmm(intra, u, 'bij,bje->bie')
    return outs + (t_inv.reshape(r, c), qd2.reshape(r, LANE), au.reshape(r, LANE))


def _pa_specs(t, rows):
    rr = min(rows, t)
    nb = rr // DN_CHUNK
    qkv = [pl.BlockSpec((rr, LANE), functools.partial(lambda i, h, o: (i, o + h), o=o)) for o in (0, HEADS, 2 * HEADS)]
    ba = pl.BlockSpec((rr, LANE), lambda i, h: (i, 3))
    vec = pl.BlockSpec((1, LANE), lambda i, h: (0, 0))
    row = pl.BlockSpec((rr, LANE), lambda i, h: (i, h))
    intra = pl.BlockSpec((None, rr, DN_CHUNK), lambda i, h: (h, i, 0))
    gl = pl.BlockSpec((nb, LANE), lambda i, h: (i, h))
    return rr, qkv, ba, vec, row, intra, gl


def _grid_ends(grid):
    first = lambda: functools.reduce(jnp.logical_and, [pl.program_id(a) == 0 for a in range(len(grid))])
    last = lambda: functools.reduce(jnp.logical_and, [pl.program_id(a) == n - 1 for a, n in enumerate(grid)])
    return first, last


def _delta_local(qkv_act, pm, alog, dtb, rider=None):
    t = qkv_act.shape[0]
    rr, qkv, ba, vec, row, intra, gl = _pa_specs(t, _PA_ROWS_FWD)

    def body(q, k, v, b, al, dt, *outs):
        vals = _phase_a(pl.program_id(1), q[...], k[...], v[...], b[...], al[...], dt[...])
        for o, val in zip(outs, vals):
            o[...] = val

    wide = jax.ShapeDtypeStruct((t, HEADS * LANE), F32)
    sq = jax.ShapeDtypeStruct((HEADS, t, DN_CHUNK), F32)
    grid = (t // rr, HEADS)
    return _carried_call(
        body, rider, *_grid_ends(grid), name="delta_local", grid=grid, in_specs=qkv + [ba, vec, vec],
        out_specs=[row] * 4 + [intra, gl, intra, row, row],
        out_shape=[wide] * 4 + [sq, jax.ShapeDtypeStruct((t // DN_CHUNK, HEADS * LANE), F32), sq, wide, wide],
        scratch_shapes=[], sem=("arbitrary", "arbitrary"), args=(qkv_act, qkv_act, qkv_act, pm, alog, dtb))


def _delta_local_bwd(qkv_act, pm, alog, dtb, t_inv, du, dw, dqd, dkt, dintra, dgl, rider=None):
    t = qkv_act.shape[0]
    rr, qkv, ba, vec, row, intra, gl = _pa_specs(t, _PA_ROWS)

    def body(q, k, v, b, al, dt, ti, du_r, dw_r, dqd_r, dkt_r, di_r, dgl_r, dq_o, dk_o, dv_o, dba_o, dal_o, ddt_o):
        i, h = pl.program_id(0), pl.program_id(1)
        t_saved = ti[...]
        _, vjp = jax.vjp(lambda *a: _phase_a(h, *a, t_saved=t_saved), q[...], k[...], v[...], b[...], al[...], dt[...])
        dq, dk, dv, dba, dal, ddt = vjp((du_r[...], dw_r[...], dqd_r[...], dkt_r[...], di_r[...], dgl_r[...]))
        dq_o[...], dk_o[...], dv_o[...] = dq, dk, dv

        @pl.when(h == 0)
        def _():
            dba_o[...] = jnp.zeros_like(dba_o)

        @pl.when((h == 0) & (i == 0))
        def _():
            dal_o[...] = jnp.zeros_like(dal_o)
            ddt_o[...] = jnp.zeros_like(ddt_o)

        dba_o[...] += dba
        dal_o[...] += dal
        ddt_o[...] += ddt

    wide = jax.ShapeDtypeStruct((t, HEADS * LANE), F32)
    vshape = jax.ShapeDtypeStruct((1, LANE), F32)
    grid = (t // rr, HEADS)
    return _carried_call(
        body, rider, *_grid_ends(grid), name="delta_local_bwd", grid=grid,
        in_specs=qkv + [ba, vec, vec, intra] + [row] * 4 + [intra, gl],
        out_specs=[row] * 3 + [pl.BlockSpec((rr, LANE), lambda i, h: (i, 0)), vec, vec],
        out_shape=[wide] * 3 + [jax.ShapeDtypeStruct((t, LANE), F32), vshape, vshape],
        scratch_shapes=[], sem=("arbitrary", "arbitrary"),
        args=(qkv_act, qkv_act, qkv_act, pm, alog, dtb, t_inv, du, dw, dqd, dkt, dintra, dgl))


_SCAN_ROWS = 512


def _dot(a, b, dn):
    return lax.dot_general(a.astype(_CDT), b.astype(_CDT), (dn, ((), ())), preferred_element_type=F32)


_NN = ((1,), (0,))
_NT = ((1,), (1,))
_TN = ((0,), (0,))


def _delta_scan(u, w, qd, kt, au, gl, rider=None):
    t = u.shape[0]
    rr = min(_SCAN_ROWS, t)
    nc = rr // DN_CHUNK

    def body(u_ref, w_ref, qd_ref, kt_ref, au_ref, gl_ref, o_ref, sall_ref, s_scr):
        @pl.when(pl.program_id(0) == 0)
        def _():
            s_scr[...] = jnp.zeros_like(s_scr)

        def chunk(c, carry):
            r0 = pl.multiple_of(c * DN_CHUNK, DN_CHUNK)
            rows = pl.ds(r0, DN_CHUNK)
            e = jnp.exp(gl_ref[pl.ds(c, 1), :])
            states = [s_scr[h] for h in range(HEADS)]
            u_c, w_c, qd_c, kt_c, au_c = u_ref[rows, :], w_ref[rows, :], qd_ref[rows, :], kt_ref[rows, :], au_ref[rows, :]
            o_new, s_new = [], []
            for h in range(HEADS):
                cs = slice(h * LANE, (h + 1) * LANE)
                s = states[h]
                both = _dot(jnp.concatenate([w_c[:, cs], qd_c[:, cs]], axis=0), s, _NN)
                v_new = u_c[:, cs] - both[:DN_CHUNK]
                o_new.append(both[DN_CHUNK:] + au_c[:, cs])
                s_new.append(s * e[:, cs] + _dot(kt_c[:, cs], v_new, _TN))
            o_ref[rows, :] = jnp.concatenate(o_new, axis=1)
            for h in range(HEADS):
                sall_ref[c, h] = states[h]
                s_scr[h] = s_new[h]
            return carry

        lax.fori_loop(0, nc, chunk, 0)

    row = pl.BlockSpec((rr, HEADS * LANE), lambda i: (i, 0))
    grid = (t // rr,)
    return _carried_call(
        body, rider, *_grid_ends(grid), name="delta_scan", grid=grid,
        in_specs=[row] * 5 + [pl.BlockSpec((nc, HEADS * LANE), lambda i: (i, 0))],
        out_specs=[row, pl.BlockSpec((nc, HEADS, LANE, LANE), lambda i: (i, 0, 0, 0))],
        out_shape=[jax.ShapeDtypeStruct((t, HEADS * LANE), F32),
                   jax.ShapeDtypeStruct((t // DN_CHUNK, HEADS, LANE, LANE), F32)],
        scratch_shapes=[pltpu.VMEM((HEADS, LANE, LANE), F32)], sem=("arbitrary",), args=(u, w, qd, kt, au, gl))


def _delta_scan_bwd(u, w, qd, kt, intra, gl, sall, do, rider=None):
    t = u.shape[0]
    rr = min(_SCAN_ROWS, t)
    nc = rr // DN_CHUNK
    ng = t // rr

    def body(u_ref, w_ref, qd_ref, kt_ref, a_ref, gl_ref, sall_ref, do_ref,
             du_ref, dw_ref, dqd_ref, dkt_ref, da_ref, dgl_ref, ds_scr):
        @pl.when(pl.program_id(0) == 0)
        def _():
            ds_scr[...] = jnp.zeros_like(ds_scr)

        def chunk(cc, carry):
            c = nc - 1 - cc
            r0 = pl.multiple_of(c * DN_CHUNK, DN_CHUNK)
            rows = pl.ds(r0, DN_CHUNK)
            e = jnp.exp(gl_ref[pl.ds(c, 1), :])
            states = [sall_ref[c, h] for h in range(HEADS)]
            ds_outs = [ds_scr[h] for h in range(HEADS)]
            u_a, w_a, kt_a, qd_a, do_a = u_ref[rows, :], w_ref[rows, :], kt_ref[rows, :], qd_ref[rows, :], do_ref[rows, :]
            a_a = [a_ref[h, rows, :] for h in range(HEADS)]
            da, dqd, dkt, du, dw, dgl, ds_new = [], [], [], [], [], [], []
            for h in range(HEADS):
                cs = slice(h * LANE, (h + 1) * LANE)
                s, ds_out = states[h], ds_outs[h]
                w_c, kt_c, qd_c, do_c = w_a[:, cs], kt_a[:, cs], qd_a[:, cs], do_a[:, cs]
                v_new = u_a[:, cs] - _dot(w_c, s, _NN)
                dv_new = _dot(a_a[h], do_c, _TN) + _dot(kt_c, ds_out, _NN)
                cots = jnp.concatenate([do_c, dv_new], axis=0)
                both = _dot(cots, s, _NT)
                dqd.append(both[:DN_CHUNK])
                dw.append(-both[DN_CHUNK:])
                da.append(_dot(do_c, v_new, _NT))
                dkt.append(_dot(v_new, ds_out, _NT))
                du.append(dv_new)
                eh = e[:, cs]
                dgl.append(jnp.broadcast_to(jnp.sum(ds_out * s, axis=0, keepdims=True) * eh, (8, LANE)))
                ds_new.append(ds_out * eh + _dot(jnp.concatenate([qd_c, -w_c], axis=0), cots, _TN))
            cat = lambda parts: jnp.concatenate(parts, axis=1)
            dqd_ref[rows, :], dkt_ref[rows, :], du_ref[rows, :], dw_ref[rows, :] = cat(dqd), cat(dkt), cat(du), cat(dw)
            dgl_ref[pl.ds(pl.multiple_of(c * 8, 8), 8), :] = cat(dgl)
            for h in range(HEADS):
                da_ref[h, rows, :] = da[h]
                ds_scr[h] = ds_new[h]
            return carry

        lax.fori_loop(0, nc, chunk, 0)

    rev = lambda i: (ng - 1 - i, 0)
    row = pl.BlockSpec((rr, HEADS * LANE), rev)
    a_spec = pl.BlockSpec((HEADS, rr, DN_CHUNK), lambda i: (0, ng - 1 - i, 0))
    gl_spec = pl.BlockSpec((nc, HEADS * LANE), rev)
    wide = jax.ShapeDtypeStruct((t, HEADS * LANE), F32)
    outs, carried = _carried_call(
        body, rider, *_grid_ends((ng,)), name="delta_scan_bwd", grid=(ng,),
        in_specs=[row] * 4 + [a_spec, gl_spec, pl.BlockSpec((nc, HEADS, LANE, LANE), lambda i: (ng - 1 - i, 0, 0, 0)), row],
        out_specs=[row] * 4 + [a_spec, pl.BlockSpec((nc * 8, HEADS * LANE), rev)],
        out_shape=[wide] * 4 + [jax.ShapeDtypeStruct((HEADS, t, DN_CHUNK), F32),
                                jax.ShapeDtypeStruct((t // DN_CHUNK * 8, HEADS * LANE), F32)],
        scratch_shapes=[pltpu.VMEM((HEADS, LANE, LANE), F32)], sem=("arbitrary",), args=(u, w, qd, kt, intra, gl, sall, do))
    return tuple(outs[:5]) + (outs[5].reshape(t // DN_CHUNK, 8, HEADS * LANE)[:, 0, :],), carried


_ATT_TILE = 512
_ATT_FWD_HEADS = 2
_ATT_BWD_HEADS = 4


def _kv_rows(j, tk):
    return pl.ds(pl.multiple_of(j * tk, tk), tk)


def _att_scores(ql, qr, ckv_ref, kr_ref, j, tk):
    ks = _kv_rows(j, tk)
    return (_dot(ql, ckv_ref[ks, :], _NT) + _dot(qr, kr_ref[ks, :], _NT)) * ATT_SCALE


def _diag_mask(s):
    qi = lax.broadcasted_iota(jnp.int32, s.shape, 0) % s.shape[1]
    ki = lax.broadcasted_iota(jnp.int32, s.shape, 1)
    return jnp.where(ki <= qi, s, NEG_BIG)


def _attention(qn, qr_pre, cosb, sinb, ckv, kr, wuk, wuv):
    t = ckv.shape[0]
    tq = min(_ATT_TILE, t)
    hp = _ATT_FWD_HEADS
    nl = tq // LANE

    def lane_fold(v, op):
        out = v[:, :LANE]
        for k in range(1, nl):
            out = op(out, v[:, k * LANE:(k + 1) * LANE])
        return out

    def body(qn_ref, qr_ref, cos_ref, sin_ref, ckv_ref, kr_ref, wuk_ref, wuv_ref, o_ref, lse_ref, qrope_ref, omla_ref,
             s_all, m_lanes, l_lanes, acc_scr):
        hg, qi = pl.program_id(0), pl.program_id(1)
        lanes = lambda ref, k: ref[:, k * LANE:(k + 1) * LANE]
        rows = lambda parts: jnp.concatenate(parts, axis=0)
        part = lambda v, k: v[k * tq:(k + 1) * tq]
        q_lat = rows([_dot(lanes(qn_ref, k), wuk_ref[hg * hp + k], _NT) for k in range(hp)]).astype(_CDT)
        roped = [_rope(lanes(qr_ref, k), cos_ref[...], sin_ref[...]).astype(qrope_ref.dtype) for k in range(hp)]
        qrope_ref[...] = jnp.concatenate(roped, axis=1)
        q_rope = rows(roped)
        m_lanes[...] = jnp.full_like(m_lanes, NEG_BIG)

        def scores(j, masked):
            s = _att_scores(q_lat, q_rope, ckv_ref, kr_ref, j, tq)
            if masked:
                s = _diag_mask(s)
            s_all[j] = s
            m_lanes[...] = jnp.maximum(m_lanes[...], lane_fold(s, jnp.maximum))

        def scores_body(j, carry):
            scores(j, False)
            return carry

        lax.fori_loop(0, qi, scores_body, 0)
        scores(qi, True)
        m = jnp.max(m_lanes[...], axis=-1, keepdims=True)
        mb = jnp.broadcast_to(m, (hp * tq, LANE))
        l_lanes[...] = jnp.zeros_like(l_lanes)
        acc_scr[...] = jnp.zeros_like(acc_scr)

        def weigh(j, carry):
            s = s_all[j]
            p = jnp.concatenate([jnp.exp(s[:, k * LANE:(k + 1) * LANE] - mb) for k in range(nl)], axis=1)
            l_lanes[...] += lane_fold(p, jnp.add)
            acc_scr[...] += _dot(p, ckv_ref[_kv_rows(j, tq), :], _NN)
            return carry

        lax.fori_loop(0, qi + 1, weigh, 0)
        l = jnp.sum(l_lanes[...], axis=-1, keepdims=True)
        out = acc_scr[...] / l
        lse_v = m + jnp.log(l)
        for k in range(hp):
            o_ref[k] = part(out, k)
            lse_ref[k] = part(lse_v, k)
        omla_ref[...] = jnp.concatenate(
            [_dot(part(out, k), wuv_ref[hg * hp + k], _NN) for k in range(hp)], axis=1).astype(omla_ref.dtype)

    col = pl.BlockSpec((tq, hp * LANE), lambda h, i: (i, h))
    table = pl.BlockSpec((tq, ROPE_PAD), lambda h, i: (i, 0))
    wspec = pl.BlockSpec((HEADS, KV_LORA, NOPE), lambda h, i: (0, 0, 0))
    return pl.pallas_call(
        body, name="attention", grid=(HEADS // hp, t // tq),
        in_specs=[col, col, table, table, pl.BlockSpec((t, KV_LORA), lambda h, i: (0, 0)),
                  pl.BlockSpec((t, ROPE_PAD), lambda h, i: (0, 0)), wspec, wspec],
        out_specs=[pl.BlockSpec((hp, tq, KV_LORA), lambda h, i: (h, i, 0)),
                   pl.BlockSpec((hp, tq, 1), lambda h, i: (h, i, 0)), col, col],
        out_shape=[jax.ShapeDtypeStruct((HEADS, t, KV_LORA), F32), jax.ShapeDtypeStruct((HEADS, t, 1), F32),
                   jax.ShapeDtypeStruct((t, HEADS * ROPE_PAD), _CDT), jax.ShapeDtypeStruct((t, HEADS * NOPE), _CDT)],
        scratch_shapes=[pltpu.VMEM((t // tq, hp * tq, tq), F32), pltpu.VMEM((hp * tq, LANE), F32),
                        pltpu.VMEM((hp * tq, LANE), F32), pltpu.VMEM((hp * tq, KV_LORA), F32)],
        compiler_params=_cparams(("parallel", "parallel")))(qn, qr_pre, cosb, sinb, ckv, kr, wuk, wuv)


def _attention_bwd(qn, qr, cosb, sinb, ckv, kr, wuk, wuv, out, lse, do_mla):
    t = ckv.shape[0]
    tq = min(_ATT_TILE, t)
    hp = _ATT_BWD_HEADS

    def body(qn_ref, qr_ref, cos_ref, sin_ref, ckv_ref, kr_ref, wuk_ref, wuv_ref, o_ref, lse_ref, do_ref,
             dql_ref, dqr_ref, dqn_ref, dckv_ref, dkr_ref, dql_scr, dqr_scr):
        hg, qi = pl.program_id(0), pl.program_id(1)

        @pl.when((hg == 0) & (qi == 0))
        def _():
            dckv_ref[...] = jnp.zeros_like(dckv_ref)
            dkr_ref[...] = jnp.zeros_like(dkr_ref)

        lanes = lambda ref, k: ref[:, k * LANE:(k + 1) * LANE]
        rows = lambda parts: jnp.concatenate(parts, axis=0)
        q_lat = rows([_dot(lanes(qn_ref, k), wuk_ref[hg * hp + k], _NT) for k in range(hp)]).astype(_CDT)
        q_rope = rows([lanes(qr_ref, k) for k in range(hp)])
        d_out = rows([_dot(lanes(do_ref, k), wuv_ref[hg * hp + k], _NT) for k in range(hp)])
        d_o = d_out.astype(_CDT)
        lse_v = rows([lse_ref[k] for k in range(hp)])
        dsum = jnp.sum(d_out * rows([o_ref[k] for k in range(hp)]), axis=-1, keepdims=True)
        dql_scr[...] = jnp.zeros_like(dql_scr)
        dqr_scr[...] = jnp.zeros_like(dqr_scr)

        def step(j, masked):
            ks = _kv_rows(j, tq)
            s = _att_scores(q_lat, q_rope, ckv_ref, kr_ref, j, tq)
            if masked:
                s = _diag_mask(s)
            p = jnp.exp(s - lse_v)
            kv = ckv_ref[ks, :]
            ds = (p * (_dot(d_o, kv, _NT) - dsum) * ATT_SCALE).astype(_CDT)
            pb = p.astype(_CDT)
            dql_scr[...] += _dot(ds, kv, _NN)
            dqr_scr[...] += _dot(ds, kr_ref[ks, :], _NN)
            dckv_ref[ks, :] += _dot(pb, d_o, _TN) + _dot(ds, q_lat, _TN)
            dkr_ref[ks, :] += _dot(ds, q_rope, _TN)

        def loop_body(j, carry):
            step(j, False)
            return carry

        lax.fori_loop(0, qi, loop_body, 0)
        step(qi, True)
        cols = lambda parts: jnp.concatenate(parts, axis=1)
        part = lambda v, k: v[k * tq:(k + 1) * tq]
        dql = dql_scr[...].astype(dql_ref.dtype)
        dqr = dqr_scr[...]
        for k in range(hp):
            dql_ref[k] = part(dql, k)
        dqn_ref[...] = cols([_dot(part(dql, k), wuk_ref[hg * hp + k], _NN) for k in range(hp)]).astype(dqn_ref.dtype)
        dqr_ref[...] = cols([_rope_bwd(part(dqr, k), cos_ref[...], sin_ref[...]) for k in range(hp)]).astype(dqr_ref.dtype)

    lat = pl.BlockSpec((hp, tq, KV_LORA), lambda h, i: (h, i, 0))
    col = pl.BlockSpec((tq, hp * LANE), lambda h, i: (i, h))
    table = pl.BlockSpec((tq, ROPE_PAD), lambda h, i: (i, 0))
    kfull = pl.BlockSpec((t, KV_LORA), lambda h, i: (0, 0))
    rfull = pl.BlockSpec((t, ROPE_PAD), lambda h, i: (0, 0))
    wspec = pl.BlockSpec((HEADS, KV_LORA, NOPE), lambda h, i: (0, 0, 0))
    wide = jax.ShapeDtypeStruct((t, HEADS * LANE), _CDT)
    return pl.pallas_call(
        body, name="attention_bwd", grid=(HEADS // hp, t // tq),
        in_specs=[col, col, table, table, kfull, rfull, wspec, wspec, lat,
                  pl.BlockSpec((hp, tq, 1), lambda h, i: (h, i, 0)), col],
        out_specs=[lat, col, col, kfull, rfull],
        out_shape=[jax.ShapeDtypeStruct((HEADS, t, KV_LORA), _CDT), wide, wide,
                   jax.ShapeDtypeStruct((t, KV_LORA), F32), jax.ShapeDtypeStruct((t, ROPE_PAD), F32)],
        scratch_shapes=[pltpu.VMEM((hp * tq, KV_LORA), F32), pltpu.VMEM((hp * tq, ROPE_PAD), F32)],
        compiler_params=_cparams(("arbitrary", "arbitrary")))(qn, qr, cosb, sinb, ckv, kr, wuk, wuv, out, lse, do_mla)


_DX_ROWS = 256


def _dx_fused(pairs, add, add_scale, rider=None, name="b_dx"):
    t, d = add.shape
    tm = min(_DX_ROWS, t)
    n = len(pairs)

    def body(*refs):
        acc = refs[2 * n][...] * add_scale
        for i in range(n):
            acc = acc + _dot(refs[i][...], refs[n + i][...], _NT)
        refs[2 * n + 1][...] = acc

    in_specs = [pl.BlockSpec((tm, a.shape[1]), lambda i: (i, 0)) for a, _ in pairs]
    in_specs += [pl.BlockSpec(w.shape, lambda i: (0, 0)) for _, w in pairs]
    row = pl.BlockSpec((tm, d), lambda i: (i, 0))
    grid = (t // tm,)
    (dx,), carried = _carried_call(
        body, rider, *_grid_ends(grid), name=name, grid=grid, in_specs=in_specs + [row], out_specs=[row],
        out_shape=[jax.ShapeDtypeStruct((t, d), F32)], scratch_shapes=[], sem=("arbitrary",),
        args=tuple(a for a, _ in pairs) + tuple(w for _, w in pairs) + (add,))
    return dx, carried


def _ffn_in_swiglu(a, w):
    t, k = a.shape
    hid = w.shape[1] // 2
    tm, tn = _tile(t, 1024), _tile(hid, 1408)
    nj = hid // tn

    def body(a_ref, bg_ref, bu_ref, act_ref, gt_ref, up_ref):
        av = a_ref[...].astype(_CDT)
        gt = jnp.dot(av, bg_ref[...].astype(_CDT), preferred_element_type=F32)
        up = jnp.dot(av, bu_ref[...].astype(_CDT), preferred_element_type=F32)
        act_ref[...] = _swiglu(gt, up).astype(act_ref.dtype)
        gt_ref[...] = gt.astype(gt_ref.dtype)
        up_ref[...] = up.astype(up_ref.dtype)

    out = pl.BlockSpec((tm, tn), lambda i, j: (i, j))
    return pl.pallas_call(
        body, name="f_ffn_in_swiglu", grid=(t // tm, nj),
        in_specs=[pl.BlockSpec((tm, k), lambda i, j: (i, 0)), pl.BlockSpec((k, tn), lambda i, j: (0, j)),
                  pl.BlockSpec((k, tn), lambda i, j: (0, nj + j))],
        out_specs=[out] * 3, out_shape=[jax.ShapeDtypeStruct((t, hid), _CDT)] * 3,
        compiler_params=_cparams(("parallel", "parallel")))(a, w, w)


def _gated_norm(o, z, w):
    return _rms_norm(o, w) * _silu(z)


def _gated_norm_heads(o, z, w):
    heads = [_gated_norm(o[:, h * LANE:(h + 1) * LANE], z[:, h * LANE:(h + 1) * LANE], w) for h in range(HEADS)]
    return jnp.concatenate(heads, axis=1)


def _mla_pre(ckv, krp, cq, cosb, sinb, qw, kw):
    return _rms_norm(cq, qw), _rms_norm(ckv, kw), _rope(krp, cosb, sinb)


def _merge(gg, y_dn, y_mla):
    return _sigmoid(gg[:, :D_MODEL]) * y_dn + _sigmoid(gg[:, D_MODEL:]) * y_mla


def _ln1(xv, attn_out, g, b):
    return _layer_norm(ALPHA * xv + attn_out, g, b)


def _final(h1, ffn, gate_pre, ple_proj, g, b):
    return _layer_norm(ALPHA * h1 + ffn + _sigmoid(gate_pre) * ple_proj, g, b)


def _swiglu(gt, up):
    return _silu(gt) * up


def _local_step(x, p, cosb, sinb, target, wt, sp, exch):
    t = x.shape[0]
    bf = _CDT
    xb = x.astype(bf)
    g = {}

    qkv_pre = _mm2(xb, wt['qkv'], name="f_qkv")
    z = _mm2(xb, wt['z'], name="f_z")
    gg = _mm2(xb, wt['gg'], name="f_gg")
    pm = _mm2(xb, wt['mla'], name="f_mla")
    qkv_act = _conv_silu(qkv_pre, sp['conv_w'])
    (u, w_, qd, kt, intra, gl, t_inv, qd2, au), sent = _delta_local(qkv_act, pm, sp['a_log'], sp['dt_bias'],
                                                                    rider=exch.gather_send())
    (o_dn, sall), passed = _delta_scan(u, w_, qd2, kt, au, gl, rider=exch.gather_pass(sent))
    wt = dict(wt, **exch.weights(passed))
    def gated_norm_br(h, o, zz, w, wbr):
        og_v = _gated_norm_heads(o, zz, w).astype(bf)
        return og_v, jnp.dot(og_v, wbr.astype(bf), preferred_element_type=F32)

    og, y_dn = _rowwise(gated_norm_br, [(o_dn, D_MODEL, 0, False), (z, D_MODEL, 0, False)],
                        [sp['dn_norm_w'], wt['br_dn']],
                        [(D_MODEL, D_MODEL, False, bf), (D_MODEL, D_MODEL, False, F32)], name="f_gated_norm_br", tm=512)

    def mla_pre_uq(h, ckv, krp, cq, cosv, sinv, qw, kw, wn, wr):
        c_q_v, c_kv_v, k_rope_v = _mla_pre(ckv, krp, cq, cosv, sinv, qw, kw)
        c_q_b = c_q_v.astype(bf)
        return (c_q_b, c_kv_v, k_rope_v, jnp.dot(c_q_b, wn.astype(bf), preferred_element_type=F32),
                jnp.dot(c_q_b, wr.astype(bf), preferred_element_type=F32))

    c_q, c_kv, k_rope, q_nope, q_rope_pre = _rowwise(
        mla_pre_uq,
        [(pm, KV_LORA, 0, False), (pm, ROPE_PAD, 2, False), (pm, Q_LORA, 2, False),
         (cosb, ROPE_PAD, 0, False), (sinb, ROPE_PAD, 0, False)],
        [sp['q_norm_w'], sp['kv_norm_w'], wt['uq_nope'], wt['uq_rope']],
        [(Q_LORA, Q_LORA, False, bf), (KV_LORA, KV_LORA, False, bf), (ROPE_PAD, ROPE_PAD, False, bf),
         (HEADS * NOPE, HEADS * NOPE, False, bf), (HEADS * ROPE_PAD, HEADS * ROPE_PAD, False, F32)], name="f_mla_pre_uq",
        tm=512)
    out_lat, lse, q_rope, o_mla = _attention(q_nope, q_rope_pre, cosb, sinb, c_kv, k_rope, wt['uk'], wt['uv'])
    y_mla = _mm2(o_mla, wt['br_mla'], name="f_br_mla")

    def merge_o_ln1(h, ggv, yd, ym, xv, wo, gv, bv):
        mixed_v = _merge(ggv, yd, ym).astype(bf)
        ao = jnp.dot(mixed_v, wo.astype(bf), preferred_element_type=F32)
        h1v = _ln1(xv, ao, gv, bv)
        return mixed_v, ao, h1v, h1v

    mixed, attn_out, h1, h1b = _rowwise(
        merge_o_ln1, [(gg, 2 * D_MODEL, 0, False), (y_dn, D_MODEL, 0, False), (y_mla, D_MODEL, 0, False), (x, D_MODEL, 0, False)],
        [wt['o'], sp['ln1_g'], sp['ln1_b']],
        [(D_MODEL, D_MODEL, False, bf), (D_MODEL, D_MODEL, False, F32), (D_MODEL, D_MODEL, False, F32),
         (D_MODEL, D_MODEL, False, bf)], name="f_merge_o_ln1", tm=512)
    act, ffn_gt, ffn_up = _ffn_in_swiglu(h1b, wt['ffn_in'])
    pb = p.astype(bf)

    def final_fn(h, h1v, actv, pv, tgt, gt, up, wfo, wpg, wpl, gv, bv):
        ffnv = jnp.dot(actv.astype(bf), wfo.astype(bf), preferred_element_type=F32)
        gpv = jnp.dot(h1v.astype(bf), wpg.astype(bf), preferred_element_type=F32)
        ppv = jnp.dot(pv.astype(bf), wpl.astype(bf), preferred_element_type=F32)
        y, vjp = jax.vjp(_final, h1v, ffnv, gpv, ppv, gv, bv)
        err = y - tgt
        dh1, dffn, dgp, dpp, dg, db = vjp(err * (1.0 / D_MODEL))
        sq = err * err
        lanes = sq[:, :LANE]
        for j in range(1, D_MODEL // LANE):
            lanes = lanes + sq[:, j * LANE:(j + 1) * LANE]
        loss = jnp.sum(lanes, axis=0, keepdims=True) * (0.5 / D_MODEL)
        dffn_b = dffn.astype(bf)
        dact = _dot(dffn_b, wfo, _NT).astype(bf).astype(F32)
        _, vjp_s = jax.vjp(_swiglu, gt.astype(F32), up.astype(F32))
        dgt, dup = vjp_s(dact)
        return dffn, dffn_b, dgp, dpp, jnp.concatenate([dgt, dup], axis=1), dg, db, loss

    dpre2, dpre2b, dgate_pre, dple_proj, dffn_in, g['ln2_g'], g['ln2_b'], loss_lanes = _rowwise(
        final_fn, [(h1, D_MODEL, 0, False), (act, FFN_HIDDEN, 0, False), (pb, PLE_DIM, 0, False), (target, D_MODEL, 0, False),
                   (ffn_gt, FFN_HIDDEN, 0, False), (ffn_up, FFN_HIDDEN, 0, False)],
        [wt['ffn_out'], wt['ple_gate'], wt['ple'], sp['ln2_g'], sp['ln2_b']],
        [(D_MODEL, D_MODEL, False, F32)] + [(D_MODEL, D_MODEL, False, bf)] * 3 + [(2 * FFN_HIDDEN, 2 * FFN_HIDDEN, False, bf)],
        [(1, D_MODEL), (1, D_MODEL), (1, LANE)], name="b_final")
    g['loss_lanes'] = loss_lanes

    g['ple'] = _mm2(pb, dple_proj, ta=True, name="g_ple")
    g['ple_gate'] = _mm2(h1b, dgate_pre, ta=True, name="g_ple_gate")
    g['ffn_out'] = _mm2(act, dpre2b, ta=True, name="g_ffn_out")
    g['ffn_in'] = _mm2(h1b, dffn_in, ta=True, name="g_ffn_in")
    dh1, _ = _dx_fused([(dffn_in, wt['ffn_in']), (dgate_pre, wt['ple_gate'])], dpre2, ALPHA, name="b_dh1")

    def attn_out_bwd(h, xv, ao, d, ggv, yd, ym, o, zz, wo, wbd, wbm, gv, bv, nw):
        _, vjp = jax.vjp(_ln1, xv, ao, gv, bv)
        _, dao, dg, db = vjp(d)
        dao_b = dao.astype(bf)
        _, vjp_m = jax.vjp(_merge, ggv, yd, ym)
        dggv, dyd, dym = vjp_m(_dot(dao_b, wo, _NT))
        dyd_b, dym_b = dyd.astype(bf), dym.astype(bf)
        _, vjp_n = jax.vjp(_gated_norm_heads, o, zz, nw)
        do_v, dz_v, dnw = vjp_n(_dot(dyd_b, wbd, _NT))
        return dao, dao_b, dggv, dyd_b, dym_b, do_v, dz_v, _dot(dym_b, wbm, _NT), dg, db, dnw

    row_d = lambda a: (a, D_MODEL, 0, False)
    dpre1, dpre1b, dgg, dy_dn, dy_mla, do_dn, dz, do_mla, g['ln1_g'], g['ln1_b'], g['dn_norm_w'] = _rowwise(
        attn_out_bwd, [row_d(x), row_d(attn_out), row_d(dh1), (gg, 2 * D_MODEL, 0, False), row_d(y_dn), row_d(y_mla),
                       row_d(o_dn), row_d(z)],
        [wt['o'], wt['br_dn'], wt['br_mla'], sp['ln1_g'], sp['ln1_b'], sp['dn_norm_w']],
        [(D_MODEL, D_MODEL, False, F32), (D_MODEL, D_MODEL, False, bf), (2 * D_MODEL, 2 * D_MODEL, False, bf),
         (D_MODEL, D_MODEL, False, bf), (D_MODEL, D_MODEL, False, bf), (D_MODEL, D_MODEL, False, F32),
         (D_MODEL, D_MODEL, False, bf), (D_MODEL, D_MODEL, False, bf)],
        [(1, D_MODEL), (1, D_MODEL), (1, LANE)], name="b_attn_out")
    g['o'] = _mm2(mixed, dpre1b, ta=True, name="g_o")
    g['br_dn'] = _mm2(og, dy_dn, ta=True, name="g_br_dn")
    g['br_mla'] = _mm2(o_mla, dy_mla, ta=True, name="g_br_mla")

    g['uv'] = _mm(out_lat, do_mla, name="g_uv", ta=True, heads=HEADS, a_head='lead', b_head='col', out_head='lead',
                  dims=(KV_LORA, NOPE, t))
    dq_lat, dq_rope_pre, dq_nope, dckv_att, dkr_att = _attention_bwd(
        q_nope, q_rope, cosb, sinb, c_kv, k_rope, wt['uk'], wt['uv'], out_lat, lse, do_mla)
    g['uk'] = _mm(dq_lat, q_nope, name="g_uk", ta=True, heads=HEADS, a_head='lead', b_head='col', out_head='lead',
                  dims=(KV_LORA, NOPE, t))
    g['uq_nope'] = _mm2(c_q, dq_nope, ta=True, name="g_uq_nope")
    g['uq_rope'] = _mm2(c_q, dq_rope_pre, ta=True, name="g_uq_rope")
    dc_q = _mm2(dq_nope, wt['uq_nope'], tb=True, name="b_dcq_nope")
    dc_q = _mm2(dq_rope_pre, wt['uq_rope'], tb=True, name="b_dcq_rope", add=dc_q)

    (du, dw, dqd, dkt, dintra, dgl), paired = _delta_scan_bwd(u, w_, qd, kt, intra, gl, sall, do_dn, rider=exch.pair_send(g))
    (dq_a, dk_a, dv_a, dba, g['a_log'], g['dt_bias']), arrived = _delta_local_bwd(
        qkv_act, pm, sp['a_log'], sp['dt_bias'], t_inv, du, dw, dqd, dkt, dintra, dgl, rider=exch.reduce_send(paired))
    exch.reduce_arrived(arrived)
    dqkv_pre, g['conv_w'] = _conv_silu_bwd(qkv_pre, sp['conv_w'], [dq_a, dk_a, dv_a])

    def mla_pre_bwd(h, ckv, cq, cosv, sinv, dcq, dckv, dkr, dba_v, qw, kw):
        _, vjp = jax.vjp(lambda a, c, d, e: (_rms_norm(c, d), _rms_norm(a, e)), ckv, cq, qw, kw)
        dckv_p, dcq_p, dqw, dkw = vjp((dcq, dckv))
        dkr_p = _rope_bwd(dkr, cosv, sinv)
        dpm = jnp.concatenate([dckv_p, dkr_p, dba_v, jnp.zeros((ckv.shape[0], 2 * LANE), F32), dcq_p], axis=1)
        return dpm, dqw, dkw

    dpm, g['q_norm_w'], g['kv_norm_w'] = _rowwise(
        mla_pre_bwd,
        [(pm, KV_LORA, 0, False), (pm, Q_LORA, 2, False), (cosb, ROPE_PAD, 0, False), (sinb, ROPE_PAD, 0, False),
         (dc_q, Q_LORA, 0, False), (dckv_att, KV_LORA, 0, False), (dkr_att, ROPE_PAD, 0, False), (dba, LANE, 0, False)],
        [sp['q_norm_w'], sp['kv_norm_w']], [(1152, 1152, False, bf)], [(1, Q_LORA), (1, KV_LORA)], name="b_mla_pre")

    rider = exch.small_send(g)
    if rider is None:
        g['qkv'] = _mm2(xb, dqkv_pre, ta=True, name="g_qkv")
    else:
        g['qkv'], got = _mm2(xb, dqkv_pre, ta=True, name="g_qkv", rider=rider)
        exch.small_arrived(got)
    g['z'] = _mm2(xb, dz, ta=True, name="g_z")
    g['gg'] = _mm2(xb, dgg, ta=True, name="g_gg")
    g['mla'] = _mm2(xb, dpm, ta=True, name="g_mla")
    dx, arrived = _dx_fused([(dqkv_pre, wt['qkv']), (dz, wt['z']), (dgg, wt['gg']), (dpm, wt['mla'])], dpre1, ALPHA,
                            rider=exch.in_send(g))
    exch.in_arrived(arrived)
    return loss_lanes, dx, g


_IN_SIZES = (QKV_W, HEADS * DN_DK, HEADS, HEADS, Q_LORA, KV_LORA, ROPE, D_MODEL, D_MODEL)


def _rope_tables(positions):
    inv_freq = ROPE_BASE ** (-jnp.arange(0, ROPE, 2, dtype=F32) / ROPE)
    ang = positions.astype(F32)[:, None] * inv_freq
    cos, sin = jnp.cos(ang), jnp.sin(ang)
    zeros = jnp.zeros((positions.shape[0], ROPE_PAD - ROPE), F32)
    return jnp.concatenate([cos, cos, zeros], axis=1), jnp.concatenate([-sin, sin, zeros], axis=1)


def _prep_w_in(w_in):
    dt = w_in.dtype
    offs = [0]
    for s in _IN_SIZES:
        offs.append(offs[-1] + s)
    qkv, z, wb, wa, cq, ckv, kr, gd, gm = [w_in[:, offs[i]:offs[i + 1]] for i in range(len(_IN_SIZES))]
    zc = lambda n: jnp.zeros((D_MODEL, n), dt)
    return {
        'qkv': qkv, 'z': z, 'gg': jnp.concatenate([gd, gm], axis=1),
        'mla': jnp.concatenate([ckv, kr, zc(ROPE_PAD - ROPE), wb, wa, zc(LANE - 2 * HEADS), zc(2 * LANE), cq], axis=1),
    }


def _prep_weights(full):
    w_uq = full['w_uq']
    wt = {
        'uq_nope': w_uq[:, :, :NOPE].reshape(Q_LORA, HEADS * NOPE),
        'uq_rope': jnp.pad(w_uq[:, :, NOPE:], ((0, 0), (0, 0), (0, ROPE_PAD - ROPE))).reshape(Q_LORA, HEADS * ROPE_PAD),
        'uk': jnp.transpose(full['w_uk'], (1, 0, 2)), 'uv': jnp.transpose(full['w_uv'], (1, 0, 2)),
        'br_dn': full['w_br_dn'], 'br_mla': full['w_br_mla'], 'o': full['w_o'], 'ffn_in': full['w_ffn_in'],
        'ffn_out': full['w_ffn_out'], 'ple': full['w_ple'], 'ple_gate': full['w_ple_gate'],
    }
    return wt


def _prep_small(small):
    pad = lambda v: jnp.pad(v, (0, LANE - v.shape[0]))[None, :]
    return {
        'conv_w': small['conv_w'], 'a_log': pad(small['dn_a_log']), 'dt_bias': pad(small['dn_dt_bias']),
        'dn_norm_w': small['dn_norm_w'][None, :], 'q_norm_w': small['q_norm_w'][None, :],
        'kv_norm_w': small['kv_norm_w'][None, :], 'ln1_g': small['ln1_g'][None, :], 'ln1_b': small['ln1_b'][None, :],
        'ln2_g': small['ln2_g'][None, :], 'ln2_b': small['ln2_b'][None, :],
    }


def _w_in_grad(g):
    mla = g['mla']
    ba0 = KV_LORA + ROPE_PAD
    cq0 = ba0 + 3 * LANE
    return jnp.concatenate([
        g['qkv'], g['z'], mla[:, ba0:ba0 + HEADS], mla[:, ba0 + HEADS:ba0 + 2 * HEADS], mla[:, cq0:cq0 + Q_LORA],
        mla[:, :KV_LORA], mla[:, KV_LORA:KV_LORA + ROPE], g['gg']], axis=1)


def _unprep_grads_late(g):
    return {
        'conv_w': g['conv_w'], 'dn_a_log': g['a_log'][0, :HEADS], 'dn_dt_bias': g['dt_bias'][0, :HEADS],
        'dn_norm_w': g['dn_norm_w'][0], 'q_norm_w': g['q_norm_w'][0], 'kv_norm_w': g['kv_norm_w'][0],
        'ln1_g': g['ln1_g'][0], 'ln1_b': g['ln1_b'][0], 'ln2_g': g['ln2_g'][0], 'ln2_b': g['ln2_b'][0],
    }


def _unprep_grads_early(g):
    w_uq = jnp.concatenate([g['uq_nope'].reshape(Q_LORA, HEADS, NOPE),
                            g['uq_rope'].reshape(Q_LORA, HEADS, ROPE_PAD)[:, :, :ROPE]], axis=2)
    return {
        'w_uq': w_uq, 'w_uk': jnp.transpose(g['uk'], (1, 0, 2)), 'w_uv': jnp.transpose(g['uv'], (1, 0, 2)),
        'w_br_dn': g['br_dn'], 'w_br_mla': g['br_mla'], 'w_o': g['o'],
        'w_ffn_in': g['ffn_in'], 'w_ffn_out': g['ffn_out'], 'w_ple': g['ple'], 'w_ple_gate': g['ple_gate'],
    }


_FLATB_PIECES = (
    ('w_ffn_out', 704, (704, D_MODEL)), ('w_br_dn', 256, (256, D_MODEL)), ('w_br_mla', 256, (256, D_MODEL)),
    ('w_o', 256, (256, D_MODEL)), ('w_ple_gate', 256, (256, D_MODEL)), ('w_uq', 144, (96, HEADS, NOPE + ROPE)),
    ('w_uk', 64, (64, HEADS, NOPE)), ('w_uv', 64, (64, HEADS, NOPE)), ('w_ple', 64, (PLE_DIM, 256)),
)
FLATB_ROWS = 2112
W_IN_SHARD = D_IN // N_SHARD
FFN_IN_SHARD = 2 * FFN_HIDDEN // N_SHARD
A_ROWS = D_MODEL + 32
_CONV_SHARD = QKV_W // N_SHARD
_ADD_TILES = (256, 256, 352)


def _flatb_offsets():
    offs, o = {}, 0
    for name, rows, _ in _FLATB_PIECES:
        offs[name] = o
        o += rows
    return offs, o


def _pack_shards(ws, conv_w):
    conv_bits = lax.bitcast_convert_type(conv_w, jnp.bfloat16).reshape(DN_CONV, 2 * _CONV_SHARD).astype(_CDT)
    tail = jnp.pad(conv_bits, ((0, A_ROWS - D_MODEL - DN_CONV), (0, W_IN_SHARD - 2 * _CONV_SHARD)))
    a_buf = jnp.concatenate([ws['w_in'].astype(_CDT), tail], axis=0)
    parts = [ws[name].astype(_CDT).reshape(rows, FLAT_W) for name, rows, _ in _FLATB_PIECES]
    used = sum(p.shape[0] for p in parts)
    parts.append(jnp.zeros((FLATB_ROWS - used, FLAT_W), _CDT))
    return [a_buf, ws['w_ffn_in'].astype(_CDT), jnp.concatenate(parts, axis=0)]


def _unpack_w_in(gathered, local, me):
    a = [jnp.where(me == s, local, gathered[s]) for s in range(N_SHARD)]
    conv = [lax.bitcast_convert_type(
        p[D_MODEL:D_MODEL + DN_CONV, :2 * _CONV_SHARD].astype(jnp.bfloat16).reshape(DN_CONV, _CONV_SHARD, 2), F32) for p in a]
    return jnp.concatenate([p[:D_MODEL] for p in a], axis=1), jnp.concatenate(conv, axis=1)


def _unpack_rest(gathered, local, me):
    pick = lambda b, s: jnp.where(me == s, local[b], gathered[b][s])
    full = {'w_ffn_in': jnp.concatenate([pick(0, s) for s in range(N_SHARD)], axis=1)}
    offs, _ = _flatb_offsets()
    fb = [pick(1, s) for s in range(N_SHARD)]
    for name, rows, shape in _FLATB_PIECES:
        pieces = [p[offs[name]:offs[name] + rows].reshape(shape) for p in fb]
        full[name] = jnp.concatenate(pieces, axis=1 if name == 'w_ple' else 0)
    return full


def _shard_columns(g, w):
    return jnp.stack([g[:, s * w:(s + 1) * w] for s in range(N_SHARD)])


def _pack_grads_rest(gw):
    parts = []
    for name, rows, _ in _FLATB_PIECES:
        g = gw[name]
        if name == 'w_ple':
            parts.append(_shard_columns(g, PLE_DIM).reshape(N_SHARD, rows, FLAT_W))
        else:
            parts.append(g.reshape(N_SHARD, rows, FLAT_W))
    used = sum(p.shape[1] for p in parts)
    parts.append(jnp.zeros((N_SHARD, FLATB_ROWS - used, FLAT_W), F32))
    return [_shard_columns(gw['w_ffn_in'], FFN_IN_SHARD), jnp.concatenate(parts, axis=1)]


def _unpack_reduced(mine, theirs, c):
    whole = [jnp.concatenate([jnp.where(c == 0, m, t), jnp.where(c == 0, t, m)], axis=0) for m, t in zip(mine, theirs)]
    out = {'w_in': whole[0], 'w_ffn_in': whole[1]}
    offs, _ = _flatb_offsets()
    for name, rows, shape in _FLATB_PIECES:
        out[name] = whole[2][offs[name]:offs[name] + rows].reshape(shape)
    return out


_HBM = pl.BlockSpec(memory_space=pltpu.HBM)


def _place():
    x, y, c = lax.axis_index("x"), lax.axis_index("y"), lax.axis_index("c")
    chips = [(1 - x, y), (x, 1 - y), (1 - x, 1 - y)]
    return x, y, c, chips


def _remote(src, dst, send_sems, recv_sems, k, to):
    return pltpu.make_async_remote_copy(src_ref=src, dst_ref=dst, send_sem=send_sems.at[k], recv_sem=recv_sems.at[k],
                                        device_id=to, device_id_type=_MESH)


def _half_rows(ref, half, hf, lead=None):
    rows = pl.ds(pl.multiple_of(hf * half, 16), half)
    return ref.at[rows, :] if lead is None else ref.at[lead, rows, :]


class _Rider:
    def __init__(self, inputs, out_shape, n_sems, copies, aliases=None):
        self.inputs, self.out_shape, self.n_sems, self.copies = list(inputs), list(out_shape), n_sems, copies
        self.aliases = aliases or {}


def _carried_call(body, rider, first, last, *, name, grid, in_specs, out_specs, out_shape, scratch_shapes, sem, args):
    n_in, n_out, n_scr = len(in_specs), len(out_specs), len(scratch_shapes)
    if rider is None:
        res = pl.pallas_call(body, name=name, grid=grid, in_specs=in_specs, out_specs=out_specs, out_shape=out_shape,
                             scratch_shapes=scratch_shapes, compiler_params=_cparams(sem))(*args)
        return list(res), []
    ri, ro = len(rider.inputs), len(rider.out_shape)

    def full_body(*refs):
        own_in, r_in = refs[:n_in], refs[n_in:n_in + ri]
        o0 = n_in + ri
        own_out, r_out = refs[o0:o0 + n_out], refs[o0 + n_out:o0 + n_out + ro]
        s0 = o0 + n_out + ro
        own_scr, send_sems, recv_sems = refs[s0:s0 + n_scr], refs[s0 + n_scr], refs[s0 + n_scr + 1]

        @pl.when(first())
        def _():
            sends, _ = rider.copies(r_in, r_out, send_sems, recv_sems)
            for cp in sends:
                cp.start()

        body(*own_in, *own_out, *own_scr)

        @pl.when(last())
        def _():
            sends, arrivals = rider.copies(r_in, r_out, send_sems, recv_sems)
            for cp in arrivals():
                cp.wait_recv()
            for cp in sends:
                cp.wait_send()

    res = pl.pallas_call(
        full_body, name=name, grid=grid, in_specs=list(in_specs) + [_HBM] * ri, out_specs=list(out_specs) + [_HBM] * ro,
        out_shape=list(out_shape) + rider.out_shape,
        scratch_shapes=list(scratch_shapes) + [pltpu.SemaphoreType.DMA((rider.n_sems,))] * 2,
        input_output_aliases={n_in + i: n_out + o for i, o in rider.aliases.items()},
        compiler_params=_cparams(sem))(*args, *rider.inputs)
    return list(res[:n_out]), list(res[n_out:])


def _ride_gather_send(bufs):
    n = len(bufs)
    halves = [b.shape[0] // 2 for b in bufs]

    def copies(ins, outs, send_sems, recv_sems):
        x, y, c, chips = _place()
        slot = lambda b, cx, cy: _half_rows(outs[b], halves[b], c, lead=2 * cx + cy)
        sends = [_remote(_half_rows(ins[b], halves[b], c), slot(b, x, y), send_sems, recv_sems, 3 * b + j, (cx, cy, c))
                 for b in range(n) for j, (cx, cy) in enumerate(chips)]
        arrivals = lambda: [_remote(slot(b, cx, cy), slot(b, cx, cy), send_sems, recv_sems, 3 * b + j, (x, y, c))
                            for b in range(n) for j, (cx, cy) in enumerate(chips)]
        return sends, arrivals

    return _Rider(bufs, [jax.ShapeDtypeStruct((N_SHARD,) + b.shape, b.dtype) for b in bufs], 3 * n, copies)


def _ride_gather_pass(gathered):
    n = len(gathered)
    halves = [g.shape[1] // 2 for g in gathered]

    def copies(ins, outs, send_sems, recv_sems):
        x, y, c, chips = _place()
        slot = lambda b, cx, cy, hf: _half_rows(outs[b], halves[b], hf, lead=2 * cx + cy)
        sends = [_remote(slot(b, cx, cy, c), slot(b, cx, cy, c), send_sems, recv_sems, 3 * b + j, (x, y, 1 - c))
                 for b in range(n) for j, (cx, cy) in enumerate(chips)]
        arrivals = lambda: [_remote(slot(b, cx, cy, 1 - c), slot(b, cx, cy, 1 - c), send_sems, recv_sems, 3 * b + j, (x, y, c))
                            for b in range(n) for j, (cx, cy) in enumerate(chips)]
        return sends, arrivals

    return _Rider(gathered, [jax.ShapeDtypeStruct(g.shape, g.dtype) for g in gathered], 3 * n, copies,
                  aliases={b: b for b in range(n)})


def _ride_small_gather(buf):
    def copies(ins, outs, send_sems, recv_sems):
        x, y, c, _ = _place()
        flip = lambda v, d: 1 - v if d else v
        sends, peers = [], []
        for dx in (0, 1):
            for dy in (0, 1):
                for dc in (0, 1):
                    if dx or dy or dc:
                        k = 4 * dx + 2 * dy + dc - 1
                        px, py, pc = flip(x, dx), flip(y, dy), flip(c, dc)
                        sends.append(_remote(ins[0], outs[0].at[4 * x + 2 * y + c], send_sems, recv_sems, k, (px, py, pc)))
                        peers.append((k, 4 * px + 2 * py + pc))
        arrivals = lambda: [_remote(ins[0], outs[0].at[slot], send_sems, recv_sems, k, (x, y, c)) for k, slot in peers]
        return sends, arrivals

    return _Rider([buf], [jax.ShapeDtypeStruct((8,) + buf.shape, buf.dtype)], 7, copies)


def _small_sum(gathered, buf, me_arr):
    n, r, width = gathered.shape

    def body(me_ref, g_ref, b_ref, o_ref):
        total = jnp.zeros((r, width), F32)
        for d in range(n):
            total = total + jnp.where(me_ref[0] == d, b_ref[...], g_ref[d])
        o_ref[...] = total

    return pl.pallas_call(
        body, name="small_sum", out_shape=jax.ShapeDtypeStruct((r, width), F32),
        grid_spec=pltpu.PrefetchScalarGridSpec(
            num_scalar_prefetch=1, grid=(1,),
            in_specs=[pl.BlockSpec((n, r, width), lambda i, me: (0, 0, 0)), pl.BlockSpec((r, width), lambda i, me: (0, 0))],
            out_specs=pl.BlockSpec((r, width), lambda i, me: (0, 0))),
        compiler_params=_cparams(("arbitrary",)))(me_arr, gathered, buf)


def _ride_pair_exchange(gbufs):
    n = len(gbufs)
    halves = [g.shape[1] // 2 for g in gbufs]

    def copies(ins, outs, send_sems, recv_sems):
        x, y, c, _ = _place()
        sends = [_remote(ins[b].at[:, pl.ds(pl.multiple_of((1 - c) * halves[b], 16), halves[b]), :], outs[b],
                         send_sems, recv_sems, b, (x, y, 1 - c)) for b in range(n)]
        arrivals = lambda: [_remote(outs[b], outs[b], send_sems, recv_sems, b, (x, y, c)) for b in range(n)]
        return sends, arrivals

    return _Rider(gbufs, [jax.ShapeDtypeStruct((N_SHARD, h, g.shape[2]), g.dtype) for g, h in zip(gbufs, halves)], n, copies)


def _ride_chip_exchange(parts):
    n = len(parts)

    def copies(ins, outs, send_sems, recv_sems):
        x, y, c, chips = _place()
        sends = [_remote(ins[b].at[2 * cx + cy], outs[b].at[j], send_sems, recv_sems, 3 * b + j, (cx, cy, c))
                 for b in range(n) for j, (cx, cy) in enumerate(chips)]
        arrivals = lambda: [_remote(ins[b].at[0], outs[b].at[j], send_sems, recv_sems, 3 * b + j, (x, y, c))
                            for b in range(n) for j in range(len(chips))]
        return sends, arrivals

    return _Rider(parts, [jax.ShapeDtypeStruct((3,) + p.shape[1:], p.dtype) for p in parts], 3 * n, copies)


def _gather_shards(bufs, name):
    n = len(bufs)
    halves = [b.shape[0] // 2 for b in bufs]

    def body(*refs):
        ins, outs, send_sems, recv_sems = refs[:n], refs[n:2 * n], refs[2 * n], refs[2 * n + 1]
        x, y, c, chips = _place()
        me, sibling = (x, y, c), (x, y, 1 - c)
        slot = lambda b, cx, cy, hf: _half_rows(outs[b], halves[b], hf, lead=2 * cx + cy)
        first = [_remote(_half_rows(ins[b], halves[b], c), slot(b, x, y, c), send_sems, recv_sems, 6 * b + j, (cx, cy, c))
                 for b in range(n) for j, (cx, cy) in enumerate(chips)]
        for cp in first:
            cp.start()
        passed = []
        for j, (cx, cy) in enumerate(chips):
            for b in range(n):
                _remote(slot(b, cx, cy, c), slot(b, cx, cy, c), send_sems, recv_sems, 6 * b + j, me).wait_recv()
                fwd = _remote(slot(b, cx, cy, c), slot(b, cx, cy, c), send_sems, recv_sems, 6 * b + 3 + j, sibling)
                fwd.start()
                passed.append(fwd)
        for j, (cx, cy) in enumerate(chips):
            for b in range(n):
                _remote(slot(b, cx, cy, 1 - c), slot(b, cx, cy, 1 - c), send_sems, recv_sems, 6 * b + 3 + j, me).wait_recv()
        for cp in first + passed:
            cp.wait_send()

    return pl.pallas_call(
        body, name=name, out_shape=[jax.ShapeDtypeStruct((N_SHARD,) + b.shape, b.dtype) for b in bufs],
        in_specs=[_HBM] * n, out_specs=[_HBM] * n,
        scratch_shapes=[pltpu.SemaphoreType.DMA((6 * n,)), pltpu.SemaphoreType.DMA((6 * n,))],
    )(*bufs)


def _reduce_pair_exchange(gbufs, name):
    n = len(gbufs)
    halves = [g.shape[1] // 2 for g in gbufs]

    def body(*refs):
        ins, outs, send_sems, recv_sems = refs[:n], refs[n:2 * n], refs[2 * n], refs[2 * n + 1]
        x, y, c, _ = _place()
        cps = [_remote(ins[b].at[:, pl.ds(pl.multiple_of((1 - c) * halves[b], 16), halves[b]), :], outs[b],
                       send_sems, recv_sems, b, (x, y, 1 - c)) for b in range(n)]
        for cp in cps:
            cp.start()
        for cp in cps:
            cp.wait()

    return pl.pallas_call(
        body, name=name,
        out_shape=[jax.ShapeDtypeStruct((N_SHARD, h, g.shape[2]), g.dtype) for g, h in zip(gbufs, halves)],
        in_specs=[_HBM] * n, out_specs=[_HBM] * n,
        scratch_shapes=[pltpu.SemaphoreType.DMA((n,)), pltpu.SemaphoreType.DMA((n,))],
    )(*gbufs)


def _pair_add(gbuf, recv, c_arr, tr, name):
    _, rows, width = gbuf.shape
    half = rows // 2
    nt = half // tr

    def body(c_ref, a_ref, b_ref, o_ref):
        o_ref[...] = (a_ref[...] + b_ref[...]).astype(o_ref.dtype)

    blk = lambda f: pl.BlockSpec((None, tr, width), f)
    return pl.pallas_call(
        body, name=name, out_shape=jax.ShapeDtypeStruct((N_SHARD, half, width), jnp.bfloat16),
        grid_spec=pltpu.PrefetchScalarGridSpec(
            num_scalar_prefetch=1, grid=(N_SHARD, nt),
            in_specs=[blk(lambda s, i, c: (s, c[0] * nt + i, 0)), blk(lambda s, i, c: (s, i, 0))],
            out_specs=blk(lambda s, i, c: (s, i, 0))),
        compiler_params=_cparams(("parallel", "parallel")))(c_arr, gbuf, recv)


def _chip_add(part, recv, me_arr, tr, name):
    _, half, width = part.shape

    def body(me_ref, own, a0, a1, a2, o_ref):
        f = lambda r: r[...].astype(F32)
        o_ref[...] = ((f(own) + f(a0)) + f(a1)) + f(a2)

    specs = [pl.BlockSpec((None, tr, width), lambda i, me: (me[0], i, 0))]
    specs += [pl.BlockSpec((None, tr, width), functools.partial(lambda i, me, k: (k, i, 0), k=k)) for k in range(3)]
    return pl.pallas_call(
        body, name=name, out_shape=jax.ShapeDtypeStruct((half, width), F32),
        grid_spec=pltpu.PrefetchScalarGridSpec(
            num_scalar_prefetch=1, grid=(half // tr,), in_specs=specs,
            out_specs=pl.BlockSpec((tr, width), lambda i, me: (i, 0))),
        compiler_params=_cparams(("parallel",)))(me_arr, part, recv, recv, recv)


def _reduce_pair_share(rhalves, name):
    n = len(rhalves)

    def body(*refs):
        ins, outs, send_sems, recv_sems = refs[:n], refs[n:2 * n], refs[2 * n], refs[2 * n + 1]
        x, y, c, _ = _place()
        cps = [_remote(ins[b], outs[b], send_sems, recv_sems, b, (x, y, 1 - c)) for b in range(n)]
        for cp in cps:
            cp.start()
        for cp in cps:
            cp.wait()

    return pl.pallas_call(
        body, name=name, out_shape=[jax.ShapeDtypeStruct(r.shape, r.dtype) for r in rhalves],
        in_specs=[_HBM] * n, out_specs=[_HBM] * n,
        scratch_shapes=[pltpu.SemaphoreType.DMA((n,)), pltpu.SemaphoreType.DMA((n,))],
    )(*rhalves)


def _row_tile(rows, cap):
    if rows <= cap:
        return rows
    t = (cap // 8) * 8
    while t >= 8:
        if rows % t == 0:
            return t
        t -= 8
    return rows


def _adamw(w, g, m, v, name):
    shape = w.shape
    cols = shape[-1] if len(shape) <= 3 else shape[-2] * shape[-1]
    lead = len(shape) == 3
    w2, g2, m2, v2 = (a if lead else a.reshape(-1, cols) for a in (w, g, m, v))
    rows = shape[1] if lead else w2.shape[0]
    tr, tc = _row_tile(rows, 256), cols
    if tr == rows and rows > 256:
        tc = _tile(cols, 256)

    def body(w_ref, g_ref, m_ref, v_ref, d_ref, mo_ref, vo_ref):
        gv = g_ref[...]
        mn = ADAM_B1 * m_ref[...] + (1.0 - ADAM_B1) * gv
        vn = ADAM_B2 * v_ref[...] + (1.0 - ADAM_B2) * (gv * gv)
        m_hat = mn / (1.0 - ADAM_B1 ** ADAM_STEP)
        v_hat = vn / (1.0 - ADAM_B2 ** ADAM_STEP)
        d_ref[...] = -ADAM_LR * (m_hat / (jnp.sqrt(v_hat) + ADAM_EPS) + ADAM_WD * w_ref[...])
        mo_ref[...] = mn
        vo_ref[...] = vn

    blk = (pl.BlockSpec((None, tr, tc), lambda i, j: (0, i, j)) if lead else pl.BlockSpec((tr, tc), lambda i, j: (i, j)))
    outs = pl.pallas_call(
        body, name=name, grid=(rows // tr, cols // tc), in_specs=[blk] * 4, out_specs=[blk] * 3,
        out_shape=[jax.ShapeDtypeStruct(w2.shape, F32)] * 3,
        compiler_params=_cparams(("parallel", "parallel")))(w2, g2, m2, v2)
    return tuple(o.reshape(shape) for o in outs)


_WEIGHT_NAMES = ('w_in', 'conv_w', 'dn_a_log', 'dn_dt_bias', 'dn_norm_w', 'q_norm_w', 'w_uq', 'kv_norm_w', 'w_uk',
                 'w_uv', 'w_br_dn', 'w_br_mla', 'w_o', 'ln1_g', 'ln1_b', 'w_ffn_in', 'w_ffn_out', 'w_ple',
                 'w_ple_gate', 'ln2_g', 'ln2_b')
_SMALL_NAMES = ('ln1_g', 'ln1_b', 'ln2_g', 'ln2_b', 'q_norm_w', 'kv_norm_w', 'dn_norm_w', 'dn_a_log', 'dn_dt_bias')
_SMALL_GROUP = 8
_CONV_SMALL_ROW = len(_SMALL_NAMES) * _SMALL_GROUP
_CONV_SMALL_ROWS = DN_CONV * QKV_W // FLAT_W


_LOSS_SMALL_ROW = _CONV_SMALL_ROW + 16


def _pack_small(gw, loss_lanes):
    rows = [jnp.pad(gw[n][None, :], ((0, _SMALL_GROUP - 1), (0, FLAT_W - gw[n].shape[0]))) for n in _SMALL_NAMES]
    rows.append(jnp.pad(gw['conv_w'].reshape(_CONV_SMALL_ROWS, FLAT_W), ((0, 16 - _CONV_SMALL_ROWS), (0, 0))))
    rows.append(jnp.pad(loss_lanes, ((0, _SMALL_GROUP - 1), (0, FLAT_W - LANE))))
    return jnp.concatenate(rows, axis=0)


class _Exchange:
    def __init__(self, local, me_chip, c_arr):
        self.local, self.me_chip, self.c_arr = local, me_chip, c_arr
        self.parts = self.arrived = None

    def gather_send(self):
        return _ride_gather_send(self.local)

    def gather_pass(self, sent):
        return _ride_gather_pass(sent)

    def weights(self, gathered):
        return _prep_weights(_unpack_rest(gathered, self.local, self.me_chip))

    def pair_send(self, g):
        self.gbufs = _pack_grads_rest(_unprep_grads_early(g))
        return _ride_pair_exchange(self.gbufs)

    def reduce_send(self, got):
        self.parts = [_pair_add(g_, r_, self.c_arr, tr, "pair_add_%d" % (i + 1))
                      for i, (g_, r_, tr) in enumerate(zip(self.gbufs, got, _ADD_TILES[1:]))]
        return _ride_chip_exchange(self.parts)

    def reduce_arrived(self, arrived):
        self.arrived = list(arrived)

    def in_send(self, g):
        g_in = [_shard_columns(_w_in_grad(g), W_IN_SHARD)]
        got = _reduce_pair_exchange(g_in, "reduce_pair_exchange_w_in")
        self.part_in = _pair_add(g_in[0], got[0], self.c_arr, _ADD_TILES[0], "pair_add_0")
        return _ride_chip_exchange([self.part_in])

    def in_arrived(self, arrived):
        self.arrived_in = list(arrived)

    def small_send(self, g):
        self.small = _pack_small(_unprep_grads_late(g), g['loss_lanes'])
        return _ride_small_gather(self.small)

    def small_arrived(self, arrived):
        self.small_gathered = arrived[0]


def kernel(x, p, positions, w_in, conv_w, dn_a_log, dn_dt_bias, dn_norm_w, q_norm_w, w_uq, kv_norm_w, w_uk, w_uv, w_br_dn, w_br_mla, w_o, ln1_g, ln1_b, w_ffn_in, w_ffn_out, w_ple, w_ple_gate, ln2_g, ln2_b, loss_target, m_w_in, m_conv_w, m_dn_a_log, m_dn_dt_bias, m_dn_norm_w, m_q_norm_w, m_w_uq, m_kv_norm_w, m_w_uk, m_w_uv, m_w_br_dn, m_w_br_mla, m_w_o, m_ln1_g, m_ln1_b, m_w_ffn_in, m_w_ffn_out, m_w_ple, m_w_ple_gate, m_ln2_g, m_ln2_b, v_w_in, v_conv_w, v_dn_a_log, v_dn_dt_bias, v_dn_norm_w, v_q_norm_w, v_w_uq, v_kv_norm_w, v_w_uk, v_w_uv, v_w_br_dn, v_w_br_mla, v_w_o, v_ln1_g, v_ln1_b, v_w_ffn_in, v_w_ffn_out, v_w_ple, v_w_ple_gate, v_ln2_g, v_ln2_b):
    ws = dict(w_in=w_in, conv_w=conv_w, dn_a_log=dn_a_log, dn_dt_bias=dn_dt_bias, dn_norm_w=dn_norm_w, q_norm_w=q_norm_w,
              w_uq=w_uq, kv_norm_w=kv_norm_w, w_uk=w_uk, w_uv=w_uv, w_br_dn=w_br_dn, w_br_mla=w_br_mla, w_o=w_o,
              ln1_g=ln1_g, ln1_b=ln1_b, w_ffn_in=w_ffn_in, w_ffn_out=w_ffn_out, w_ple=w_ple, w_ple_gate=w_ple_gate,
              ln2_g=ln2_g, ln2_b=ln2_b)
    ms = dict(w_in=m_w_in, conv_w=m_conv_w, dn_a_log=m_dn_a_log, dn_dt_bias=m_dn_dt_bias, dn_norm_w=m_dn_norm_w,
              q_norm_w=m_q_norm_w, w_uq=m_w_uq, kv_norm_w=m_kv_norm_w, w_uk=m_w_uk, w_uv=m_w_uv, w_br_dn=m_w_br_dn,
              w_br_mla=m_w_br_mla, w_o=m_w_o, ln1_g=m_ln1_g, ln1_b=m_ln1_b, w_ffn_in=m_w_ffn_in, w_ffn_out=m_w_ffn_out,
              w_ple=m_w_ple, w_ple_gate=m_w_ple_gate, ln2_g=m_ln2_g, ln2_b=m_ln2_b)
    vs = dict(w_in=v_w_in, conv_w=v_conv_w, dn_a_log=v_dn_a_log, dn_dt_bias=v_dn_dt_bias, dn_norm_w=v_dn_norm_w,
              q_norm_w=v_q_norm_w, w_uq=v_w_uq, kv_norm_w=v_kv_norm_w, w_uk=v_w_uk, w_uv=v_w_uv, w_br_dn=v_w_br_dn,
              w_br_mla=v_w_br_mla, w_o=v_w_o, ln1_g=v_ln1_g, ln1_b=v_ln1_b, w_ffn_in=v_w_ffn_in, w_ffn_out=v_w_ffn_out,
              w_ple=v_w_ple, w_ple_gate=v_w_ple_gate, ln2_g=v_ln2_g, ln2_b=v_ln2_b)
    mx, my, mc = lax.axis_index("x"), lax.axis_index("y"), lax.axis_index("c")

    me_chip = 2 * mx + my
    c_arr = jnp.reshape(mc, (1,)).astype(jnp.int32)
    me_arr = jnp.reshape(me_chip, (1,)).astype(jnp.int32)
    sharded = ('w_in', 'w_ffn_in') + tuple(name for name, _, _ in _FLATB_PIECES)

    local = _pack_shards({name: ws[name][0] for name in sharded}, conv_w[0])
    (gathered_in,) = _gather_shards(local[:1], "gather_w_in")
    w_in_full, conv_full = _unpack_w_in(gathered_in, local[0], me_chip)
    small = {n: ws[n][0] for n in _SMALL_NAMES}
    small['conv_w'] = conv_full
    sp = _prep_small(small)
    cosb, sinb = _rope_tables(positions[0])
    exch = _Exchange(local[1:], me_chip, c_arr)

    loss_lanes, dx, g = _local_step(x[0], p[0, 0], cosb, sinb, loss_target[0], _prep_w_in(w_in_full), sp, exch)

    parts = [exch.part_in] + exch.parts
    arrived = exch.arrived_in + exch.arrived
    mine = [_chip_add(p_, r_, me_arr, tr, "chip_add_%d" % i) for i, (p_, r_, tr) in enumerate(zip(parts, arrived, _ADD_TILES))]
    reduced = _unpack_reduced(mine, _reduce_pair_share(mine, "reduce_pair_share"), mc)
    tot = _small_sum(exch.small_gathered, exch.small, jnp.reshape(4 * mx + 2 * my + mc, (1,)).astype(jnp.int32))
    loss = jnp.sum(tot[_LOSS_SMALL_ROW, :LANE])
    gred = {name: reduced[name][None] for name in sharded}
    for i, n in enumerate(_SMALL_NAMES):
        gred[n] = tot[i * _SMALL_GROUP, :ws[n].shape[1]][None]
    conv_tot = tot[_CONV_SMALL_ROW:_CONV_SMALL_ROW + _CONV_SMALL_ROWS].reshape(DN_CONV, QKV_W)
    gred['conv_w'] = lax.dynamic_slice_in_dim(conv_tot, (2 * mx + my) * _CONV_SHARD, _CONV_SHARD, axis=1)[None]

    deltas, new_m, new_v = {}, {}, {}
    for n in _WEIGHT_NAMES:
        if n == 'w_in':
            tr_ = lambda a: jnp.transpose(a, (0, 2, 1))
            g_t = tr_(gred[n].reshape(ws[n].shape))
            outs = _adamw(tr_(ws[n]), g_t, tr_(ms[n]), tr_(vs[n]), "adamw_" + n)
            gred[n] = tr_(g_t)
            deltas[n], new_m[n], new_v[n] = (tr_(o) for o in outs)
            continue
        gred[n] = gred[n].reshape(ws[n].shape)
        deltas[n], new_m[n], new_v[n] = _adamw(ws[n], gred[n], ms[n], vs[n], "adamw_" + n)
    return (loss, dx[None], *[gred[n] for n in _WEIGHT_NAMES], *[deltas[n] for n in _WEIGHT_NAMES],
            *[new_m[n] for n in _WEIGHT_NAMES], *[new_v[n] for n in _WEIGHT_NAMES])
```

```python
import functools

import jax
import jax.numpy as jnp
from jax import lax
from jax.experimental import pallas as pl
from jax.experimental.pallas import tpu as pltpu

F32 = jnp.float32
_CDT = jnp.bfloat16
_MESH = pl.DeviceIdType.MESH

D_MODEL = 1024
PLE_DIM = 256
HEADS = 8
DN_DK = 128
DN_CHUNK = 64
DN_CONV = 4
QKV_W = 3 * HEADS * DN_DK
Q_LORA = 384
KV_LORA = 256
NOPE = 128
ROPE = 64
ROPE_PAD = 128
FFN_HIDDEN = 2816
D_IN = 6864
ROPE_BASE = 10000.0
ALPHA = 2.0 ** 0.25
ATT_SCALE = (NOPE + ROPE) ** -0.5
NEG_BIG = -1e30
ADAM_LR, ADAM_B1, ADAM_B2, ADAM_EPS, ADAM_WD, ADAM_STEP = 0.001, 0.9, 0.999, 1e-08, 0.01, 10

LANE = 128
VMEM_LIMIT = 56 * 1024 * 1024
MM_VMEM_BUDGET = 40 * 1024 * 1024
N_SHARD = 4
FLAT_W = 1024
SMALL_ROWS = 96


def _tile(dim, cap):
    if dim <= cap:
        return dim
    t = (cap // LANE) * LANE
    while t >= LANE:
        if dim % t == 0:
            return t
        t -= LANE
    return dim


def _cparams(sem):
    return pltpu.CompilerParams(dimension_semantics=sem, vmem_limit_bytes=VMEM_LIMIT)


def _mm(a, b, *, name, ta=False, tb=False, add=None, add_scale=1.0, out_dtype=F32, heads=None,
        a_head=None, b_head=None, out_head=None, dims=None, tm=1408, tn=1408, rider=None):
    m, n, k = dims
    tm, tn = _tile(m, tm), _tile(n, tn)
    sa, sb, so = a.dtype.itemsize, b.dtype.itemsize, jnp.dtype(out_dtype).itemsize

    def vmem_need(tk_):
        acc = tm * tn * 4 if tk_ < k else 0
        extra = 2 * tm * tn * 4 if add is not None else 0
        return 2 * (tm * tk_ * sa + tk_ * tn * sb) + 2 * tm * tn * so + acc + extra

    tk = k
    while vmem_need(tk) > MM_VMEM_BUDGET and tk > LANE:
        smaller = _tile(k, tk - LANE)
        if smaller >= tk:
            break
        tk = smaller
    nk = k // tk
    hgrid = () if heads is None else (heads,)
    off = len(hgrid)

    def spec(rows, cols, rtile, ctile, rsel, csel, layout):
        def idx(*g):
            h = g[0] if off else 0
            ri, ci = g[off + rsel], g[off + csel]
            if layout == 'lead':
                return (h, ri, ci)
            if layout == 'col':
                return (ri, h * (cols // ctile) + ci)
            return (ri, ci)
        if layout == 'lead':
            return pl.BlockSpec((None, rtile, ctile), idx)
        return pl.BlockSpec((rtile, ctile), idx)

    a_spec = spec(k, m, tk, tm, 2, 0, a_head) if ta else spec(m, k, tm, tk, 0, 2, a_head)
    b_spec = spec(n, k, tn, tk, 1, 2, b_head) if tb else spec(k, n, tk, tn, 2, 1, b_head)
    o_spec = spec(m, n, tm, tn, 0, 1, out_head)
    in_specs = [a_spec, b_spec]
    args = [a, b]
    if add is not None:
        in_specs.append(spec(m, n, tm, tn, 0, 1, out_head))
        args.append(add)
    dn = (((0 if ta else 1,), (1 if tb else 0,)), ((), ()))

    def body(*refs):
        a_ref, b_ref = refs[0], refs[1]
        prod = lax.dot_general(a_ref[...].astype(_CDT), b_ref[...].astype(_CDT), dn, preferred_element_type=F32)
        if nk == 1:
            o_ref = refs[-1]
            if add is not None:
                prod = prod + refs[2][...].astype(F32) * add_scale
            o_ref[...] = prod.astype(out_dtype)
            return
        o_ref, acc_ref = refs[-2], refs[-1]
        kk = pl.program_id(off + 2)

        @pl.when(kk == 0)
        def _():
            if add is not None:
                acc_ref[...] = refs[2][...].astype(F32) * add_scale
            else:
                acc_ref[...] = jnp.zeros_like(acc_ref)

        acc_ref[...] += prod

        @pl.when(kk == nk - 1)
        def _():
            o_ref[...] = acc_ref[...].astype(out_dtype)

    if out_head == 'lead':
        oshape = (heads, m, n)
    elif out_head == 'col':
        oshape = (m, heads * n)
    else:
        oshape = (m, n)
    grid = hgrid + (m // tm, n // tn, nk)
    scratch = [pltpu.VMEM((tm, tn), F32)] if nk > 1 else []
    if rider is not None:
        (out,), carried = _carried_call(
            body, rider, *_grid_ends(grid), name=name, grid=grid, in_specs=in_specs, out_specs=[o_spec],
            out_shape=[jax.ShapeDtypeStruct(oshape, out_dtype)], scratch_shapes=scratch,
            sem=("arbitrary",) * len(grid), args=tuple(args))
        return out, carried
    sem = ("parallel",) * (off + 2) + ("arbitrary",)
    return pl.pallas_call(
        body, name=name, grid=grid, in_specs=in_specs, out_specs=o_spec,
        out_shape=jax.ShapeDtypeStruct(oshape, out_dtype), scratch_shapes=scratch,
        compiler_params=_cparams(sem))(*args)


def _mm2(a, b, **kw):
    ta, tb = kw.get('ta', False), kw.get('tb', False)
    m = a.shape[1] if ta else a.shape[0]
    k = a.shape[0] if ta else a.shape[1]
    n = b.shape[0] if tb else b.shape[1]
    return _mm(a, b, dims=(m, n, k), **kw)


def _rowwise(fn, rows, bcast, outs, reds=(), *, name, tm=256, heads=None):
    t = rows[0][0].shape[0]
    tm = min(tm, t)
    hn = 1 if heads is None else heads
    in_specs, args = [], []
    for arr, width, base, per_head in rows:
        in_specs.append(pl.BlockSpec((tm, width), functools.partial(
            lambda i, h, base, per_head: (i, base + (h if per_head else 0)), base=base, per_head=per_head)))
        args.append(arr)
    for arr in bcast:
        in_specs.append(pl.BlockSpec(arr.shape, lambda i, h: (0, 0)))
        args.append(arr)
    out_specs, out_shape = [], []
    for total, width, per_head, dt in outs:
        out_specs.append(pl.BlockSpec((tm, width), functools.partial(
            lambda i, h, per_head: (i, h if per_head else 0), per_head=per_head)))
        out_shape.append(jax.ShapeDtypeStruct((t, total), dt))
    for shp in reds:
        out_specs.append(pl.BlockSpec(shp, lambda i, h: (0, 0)))
        out_shape.append(jax.ShapeDtypeStruct(shp, F32))
    n_in, n_out, n_red = len(args), len(outs), len(reds)

    def body(*refs):
        i, h = pl.program_id(0), pl.program_id(1)
        vals = fn(h, *[r[...] for r in refs[:n_in]])
        for r, v in zip(refs[n_in:n_in + n_out], vals[:n_out]):
            r[...] = v.astype(r.dtype)
        if n_red:
            @pl.when((i == 0) & (h == 0))
            def _():
                for r in refs[n_in + n_out:]:
                    r[...] = jnp.zeros_like(r)
            for r, v in zip(refs[n_in + n_out:], vals[n_out:]):
                r[...] += v

    sem = ("arbitrary", "arbitrary") if n_red else ("parallel", "parallel")
    res = pl.pallas_call(body, name=name, grid=(t // tm, hn), in_specs=in_specs, out_specs=out_specs,
                         out_shape=out_shape, compiler_params=_cparams(sem))(*args)
    return tuple(res)


def _sigmoid(x):
    return 1.0 / (1.0 + jnp.exp(-x))


def _silu(x):
    return x * _sigmoid(x)


def _softplus(x):
    return jnp.maximum(x, 0.0) + jnp.log(1.0 + jnp.exp(-jnp.abs(x)))


def _layer_norm(t, g, b):
    mu = jnp.mean(t, axis=-1, keepdims=True)
    d = t - mu
    var = jnp.mean(d * d, axis=-1, keepdims=True)
    return d * lax.rsqrt(var + 1e-5) * g + b


def _rms_norm(t, w):
    return t * lax.rsqrt(jnp.mean(t * t, axis=-1, keepdims=True) + 1e-6) * w


def _swap_rope_halves(t):
    lane = lax.broadcasted_iota(jnp.int32, t.shape, 1) % ROPE_PAD
    n = t.shape[1]
    up = pltpu.roll(t, n - ROPE // 2, axis=1)
    dn = pltpu.roll(t, ROPE // 2, axis=1)
    return jnp.where(lane < ROPE // 2, up, jnp.where(lane < ROPE, dn, 0.0))


def _rope(t, cosb, sinb):
    reps = t.shape[1] // ROPE_PAD
    c = jnp.tile(cosb, (1, reps)) if reps > 1 else cosb
    s = jnp.tile(sinb, (1, reps)) if reps > 1 else sinb
    return t * c + _swap_rope_halves(t) * s


def _rope_bwd(d, cosb, sinb):
    reps = d.shape[1] // ROPE_PAD
    c = jnp.tile(cosb, (1, reps)) if reps > 1 else cosb
    s = jnp.tile(sinb, (1, reps)) if reps > 1 else sinb
    return d * c + _swap_rope_halves(d * s)


_CONV_ROWS = 256
_CONV_COLS = 256


def _conv_window(ref, r0, lo, hi, t):
    parts = []
    start, stop = r0 - lo, r0 + _CONV_ROWS + hi
    if start < 0:
        parts.append(jnp.zeros((-start, ref.shape[1]), F32))
        start = 0
    tail = max(stop - t, 0)
    parts.append(ref[start:stop - tail, :].astype(F32))
    if tail:
        parts.append(jnp.zeros((tail, ref.shape[1]), F32))
    return parts[0] if len(parts) == 1 else jnp.concatenate(parts, axis=0)


def _conv_taps(win, w_ref, n_out):
    acc = win[8:8 + n_out] * w_ref[DN_CONV - 1:DN_CONV, :]
    for i in range(DN_CONV - 1):
        acc = acc + pltpu.roll(win, DN_CONV - 1 - i, axis=0)[8:8 + n_out] * w_ref[i:i + 1, :]
    return acc


def _conv_silu(x, w):
    t, ch = x.shape

    def body(x_ref, w_ref, o_ref):
        for r in range(t // _CONV_ROWS):
            r0 = r * _CONV_ROWS
            c = _conv_taps(_conv_window(x_ref, r0, 8, 0, t), w_ref, _CONV_ROWS)
            o_ref[r0:r0 + _CONV_ROWS, :] = _silu(c)

    return pl.pallas_call(
        body, name="conv_silu", grid=(ch // _CONV_COLS,),
        in_specs=[pl.BlockSpec((t, _CONV_COLS), lambda j: (0, j)), pl.BlockSpec((DN_CONV, _CONV_COLS), lambda j: (0, j))],
        out_specs=pl.BlockSpec((t, _CONV_COLS), lambda j: (0, j)),
        out_shape=jax.ShapeDtypeStruct((t, ch), F32), compiler_params=_cparams(("parallel",)))(x, w)


def _conv_silu_bwd(x, w, dys):
    t, ch = x.shape
    per = ch // len(dys) // _CONV_COLS

    def body(x_ref, w_ref, *rest):
        dy_refs, (dx_ref, dw_ref) = rest[:len(dys)], rest[len(dys):]
        sec = pl.program_id(0) // per
        dws = [jnp.zeros((1, _CONV_COLS), F32) for _ in range(DN_CONV)]
        for r in range(t // _CONV_ROWS):
            r0 = r * _CONV_ROWS
            n_ext = _CONV_ROWS + 8
            xw = _conv_window(x_ref, r0, 8, 8, t)
            c = _conv_taps(xw, w_ref, n_ext)
            sg = _sigmoid(c)
            dy = _conv_window(dy_refs[-1], r0, 0, 8, t)
            for k in range(len(dys) - 2, -1, -1):
                dy = jnp.where(sec == k, _conv_window(dy_refs[k], r0, 0, 8, t), dy)
            ds = dy * (sg * (1.0 + c * (1.0 - sg)))
            x0 = xw[8:8 + _CONV_ROWS]
            dx = jnp.zeros((_CONV_ROWS, _CONV_COLS), F32)
            for i in range(DN_CONV):
                sh = DN_CONV - 1 - i
                ds_up = (ds if sh == 0 else pltpu.roll(ds, n_ext - sh, axis=0))[:_CONV_ROWS]
                dx = dx + ds_up * w_ref[i:i + 1, :]
                dws[i] = dws[i] + jnp.sum(x0 * ds_up, axis=0, keepdims=True)
            dx_ref[r0:r0 + _CONV_ROWS, :] = dx.astype(dx_ref.dtype)
        for i in range(DN_CONV):
            dw_ref[i:i + 1, :] = dws[i]

    blk = pl.BlockSpec((t, _CONV_COLS), lambda j: (0, j))
    wblk = pl.BlockSpec((DN_CONV, _CONV_COLS), lambda j: (0, j))
    dy_specs = [pl.BlockSpec((t, _CONV_COLS), functools.partial(lambda j, k: (0, jnp.clip(j - k * per, 0, per - 1)), k=k))
                for k in range(len(dys))]
    return pl.pallas_call(
        body, name="conv_silu_bwd", grid=(ch // _CONV_COLS,), in_specs=[blk, wblk] + dy_specs, out_specs=[blk, wblk],
        out_shape=[jax.ShapeDtypeStruct((t, ch), _CDT), jax.ShapeDtypeStruct((DN_CONV, ch), F32)],
        compiler_params=_cparams(("arbitrary",)))(x, w, *dys)


_PA_ROWS = 1024
_PA_ROWS_FWD = 1024


def _bmm(a, b, spec):
    return jnp.einsum(spec, a.astype(_CDT), b.astype(_CDT), preferred_element_type=F32)


def _split16(a):
    hi = a.astype(jnp.bfloat16)
    return hi, (a - hi.astype(F32)).astype(jnp.bfloat16)


def _bmm3(a, b, spec):
    ah, al = _split16(a)
    bh, bl = _split16(b)
    e = lambda p, q: jnp.einsum(spec, p, q, preferred_element_type=F32)
    return e(ah, bh) + (e(ah, bl) + e(al, bh))


def _split3(b):
    b0 = b.astype(jnp.bfloat16)
    r1 = b - b0.astype(F32)
    b1 = r1.astype(jnp.bfloat16)
    return b0, b1, (r1 - b1.astype(F32)).astype(jnp.bfloat16)


@functools.partial(jax.custom_vjp, nondiff_argnums=(2, 3))
def _select_mm(sel, b, spec, spec_t):
    return sum(jnp.einsum(spec, sel, t, preferred_element_type=F32) for t in _split3(b))


def _select_mm_fwd(sel, b, spec, spec_t):
    return _select_mm(sel, b, spec, spec_t), sel


def _select_mm_bwd(spec, spec_t, sel, ct):
    return jnp.zeros_like(sel), sum(jnp.einsum(spec_t, sel, t, preferred_element_type=F32) for t in _split3(ct))


_select_mm.defvjp(_select_mm_fwd, _select_mm_bwd)


def _tri_inverse(l_mat, eye):
    pw = -l_mat
    t_inv = eye + pw
    for _ in range(5):
        pw = _bmm3(pw, pw, 'bij,bjk->bik')
        t_inv = t_inv + _bmm3(t_inv, pw, 'bij,bjk->bik')
    return t_inv


@jax.custom_vjp
def _tri_inverse_saved(l_mat, t_saved):
    return t_saved


def _tri_inverse_saved_fwd(l_mat, t_saved):
    return t_saved, t_saved


def _tri_inverse_saved_bwd(t_saved, dt):
    left = _bmm3(t_saved, dt, 'bji,bjk->bik')
    return -_bmm3(left, t_saved, 'bij,bkj->bik'), jnp.zeros_like(t_saved)


_tri_inverse_saved.defvjp(_tri_inverse_saved_fwd, _tri_inverse_saved_bwd)


def _phase_a(h, q, k, v, ba, alog, dtb, t_saved=None):
    r = q.shape[0]
    nb = r // DN_CHUNK
    c = DN_CHUNK
    lane = lax.broadcasted_iota(jnp.int32, (1, LANE), 1)
    selb = (lane == h).astype(F32)
    sela = (lane == h + HEADS).astype(F32)
    b_raw = jnp.sum(ba * selb, axis=1, keepdims=True)
    a_raw = jnp.sum(ba * sela, axis=1, keepdims=True)
    al = jnp.sum(alog * selb, axis=1, keepdims=True)
    dt = jnp.sum(dtb * selb, axis=1, keepdims=True)
    beta = jnp.broadcast_to(_sigmoid(b_raw), (r, LANE))
    g = jnp.broadcast_to(-jnp.exp(al) * _softplus(a_raw + dt), (r, LANE))
    qn = q * lax.rsqrt(jnp.sum(q * q, -1, keepdims=True) + 1e-6) * (DN_DK ** -0.5)
    kn = k * lax.rsqrt(jnp.sum(k * k, -1, keepdims=True) + 1e-6)
    q3, k3, v3 = qn.reshape(nb, c, LANE), kn.reshape(nb, c, LANE), v.reshape(nb, c, LANE)
    b3, g3 = beta.reshape(nb, c, LANE), g.reshape(nb, c, LANE)
    ri = lax.broadcasted_iota(jnp.int32, (nb, c, c), 1)
    ci = lax.broadcasted_iota(jnp.int32, (nb, c, c), 2)
    tril, strict = ri >= ci, ri > ci
    gc = _select_mm(tril.astype(jnp.bfloat16), g3, 'bij,bjd->bid', 'bij,bid->bjd')
    onehot = (lax.broadcasted_iota(jnp.int32, (nb, c, LANE), 2) == 0).astype(jnp.bfloat16)
    g_row = _select_mm(onehot, gc, 'bid,bjd->bij', 'bid,bij->bjd')
    diff = gc[:, :, :c] - g_row
    decay = jnp.where(tril, jnp.exp(jnp.where(tril, diff, 0.0)), 0.0)
    kb = k3 * b3
    l_mat = jnp.where(strict, _bmm(kb, k3, 'bid,bjd->bij') * decay, 0.0)
    if t_saved is None:
        t_inv = _tri_inverse(l_mat, (ri == ci).astype(F32))
    else:
        t_inv = _tri_inverse_saved(l_mat, t_saved.reshape(nb, c, c))
    eg = jnp.exp(gc)
    u = _bmm(t_inv, v3 * b3, 'bij,bje->bie')
    w = _bmm(t_inv, kb * eg, 'bij,bje->bie')
    intra = jnp.where(tril, _bmm(q3, k3, 'bid,bjd->bij') * decay, 0.0)
    qd = q3 * eg
    gl = jnp.sum(g3, axis=1, keepdims=True)
    kt = k3 * jnp.exp(gl - gc)
    outs = (u.reshape(r, LANE), w.reshape(r, LANE), qd.reshape(r, LANE), kt.reshape(r, LANE),
            intra.reshape(r, c), gl.reshape(nb, LANE))
    if t_saved is not None:
        return outs
    qd2 = qd - _bmm(intra, w, 'bij,bjd->bid')
    au = _bmm(intra, u, 'bij,bje->bie')
    return outs + (t_inv.reshape(r, c), qd2.reshape(r, LANE), au.reshape(r, LANE))


def _pa_specs(t, rows):
    rr = min(rows, t)
    nb = rr // DN_CHUNK
    qkv = [pl.BlockSpec((rr, LANE), functools.partial(lambda i, h, o: (i, o + h), o=o)) for o in (0, HEADS, 2 * HEADS)]
    ba = pl.BlockSpec((rr, LANE), lambda i, h: (i, 3))
    vec = pl.BlockSpec((1, LANE), lambda i, h: (0, 0))
    row = pl.BlockSpec((rr, LANE), lambda i, h: (i, h))
    intra = pl.BlockSpec((None, rr, DN_CHUNK), lambda i, h: (h, i, 0))
    gl = pl.BlockSpec((nb, LANE), lambda i, h: (i, h))
    return rr, qkv, ba, vec, row, intra, gl


def _grid_ends(grid):
    first = lambda: functools.reduce(jnp.logical_and, [pl.program_id(a) == 0 for a in range(len(grid))])
    last = lambda: functools.reduce(jnp.logical_and, [pl.program_id(a) == n - 1 for a, n in enumerate(grid)])
    return first, last


def _delta_local(qkv_act, pm, alog, dtb, rider=None):
    t = qkv_act.shape[0]
    rr, qkv, ba, vec, row, intra, gl = _pa_specs(t, _PA_ROWS_FWD)

    def body(q, k, v, b, al, dt, *outs):
        vals = _phase_a(pl.program_id(1), q[...], k[...], v[...], b[...], al[...], dt[...])
        for o, val in zip(outs, vals):
            o[...] = val

    wide = jax.ShapeDtypeStruct((t, HEADS * LANE), F32)
    sq = jax.ShapeDtypeStruct((HEADS, t, DN_CHUNK), F32)
    grid = (t // rr, HEADS)
    return _carried_call(
        body, rider, *_grid_ends(grid), name="delta_local", grid=grid, in_specs=qkv + [ba, vec, vec],
        out_specs=[row] * 4 + [intra, gl, intra, row, row],
        out_shape=[wide] * 4 + [sq, jax.ShapeDtypeStruct((t // DN_CHUNK, HEADS * LANE), F32), sq, wide, wide],
        scratch_shapes=[], sem=("arbitrary", "arbitrary"), args=(qkv_act, qkv_act, qkv_act, pm, alog, dtb))


def _delta_local_bwd(qkv_act, pm, alog, dtb, t_inv, du, dw, dqd, dkt, dintra, dgl, rider=None):
    t = qkv_act.shape[0]
    rr, qkv, ba, vec, row, intra, gl = _pa_specs(t, _PA_ROWS)

    def body(q, k, v, b, al, dt, ti, du_r, dw_r, dqd_r, dkt_r, di_r, dgl_r, dq_o, dk_o, dv_o, dba_o, dal_o, ddt_o):
        i, h = pl.program_id(0), pl.program_id(1)
        t_saved = ti[...]
        _, vjp = jax.vjp(lambda *a: _phase_a(h, *a, t_saved=t_saved), q[...], k[...], v[...], b[...], al[...], dt[...])
        dq, dk, dv, dba, dal, ddt = vjp((du_r[...], dw_r[...], dqd_r[...], dkt_r[...], di_r[...], dgl_r[...]))
        dq_o[...], dk_o[...], dv_o[...] = dq, dk, dv

        @pl.when(h == 0)
        def _():
            dba_o[...] = jnp.zeros_like(dba_o)

        @pl.when((h == 0) & (i == 0))
        def _():
            dal_o[...] = jnp.zeros_like(dal_o)
            ddt_o[...] = jnp.zeros_like(ddt_o)

        dba_o[...] += dba
        dal_o[...] += dal
        ddt_o[...] += ddt

    wide = jax.ShapeDtypeStruct((t, HEADS * LANE), F32)
    vshape = jax.ShapeDtypeStruct((1, LANE), F32)
    grid = (t // rr, HEADS)
    return _carried_call(
        body, rider, *_grid_ends(grid), name="delta_local_bwd", grid=grid,
        in_specs=qkv + [ba, vec, vec, intra] + [row] * 4 + [intra, gl],
        out_specs=[row] * 3 + [pl.BlockSpec((rr, LANE), lambda i, h: (i, 0)), vec, vec],
        out_shape=[wide] * 3 + [jax.ShapeDtypeStruct((t, LANE), F32), vshape, vshape],
        scratch_shapes=[], sem=("arbitrary", "arbitrary"),
        args=(qkv_act, qkv_act, qkv_act, pm, alog, dtb, t_inv, du, dw, dqd, dkt, dintra, dgl))


_SCAN_ROWS = 512


def _dot(a, b, dn):
    return lax.dot_general(a.astype(_CDT), b.astype(_CDT), (dn, ((), ())), preferred_element_type=F32)


_NN = ((1,), (0,))
_NT = ((1,), (1,))
_TN = ((0,), (0,))


def _delta_scan(u, w, qd, kt, au, gl, rider=None):
    t = u.shape[0]
    rr = min(_SCAN_ROWS, t)
    nc = rr // DN_CHUNK

    def body(u_ref, w_ref, qd_ref, kt_ref, au_ref, gl_ref, o_ref, sall_ref, s_scr):
        @pl.when(pl.program_id(0) == 0)
        def _():
            s_scr[...] = jnp.zeros_like(s_scr)

        def chunk(c, carry):
            r0 = pl.multiple_of(c * DN_CHUNK, DN_CHUNK)
            rows = pl.ds(r0, DN_CHUNK)
            e = jnp.exp(gl_ref[pl.ds(c, 1), :])
            states = [s_scr[h] for h in range(HEADS)]
            u_c, w_c, qd_c, kt_c, au_c = u_ref[rows, :], w_ref[rows, :], qd_ref[rows, :], kt_ref[rows, :], au_ref[rows, :]
            o_new, s_new = [], []
            for h in range(HEADS):
                cs = slice(h * LANE, (h + 1) * LANE)
                s = states[h]
                both = _dot(jnp.concatenate([w_c[:, cs], qd_c[:, cs]], axis=0), s, _NN)
                v_new = u_c[:, cs] - both[:DN_CHUNK]
                o_new.append(both[DN_CHUNK:] + au_c[:, cs])
                s_new.append(s * e[:, cs] + _dot(kt_c[:, cs], v_new, _TN))
            o_ref[rows, :] = jnp.concatenate(o_new, axis=1)
            for h in range(HEADS):
                sall_ref[c, h] = states[h]
                s_scr[h] = s_new[h]
            return carry

        lax.fori_loop(0, nc, chunk, 0)

    row = pl.BlockSpec((rr, HEADS * LANE), lambda i: (i, 0))
    grid = (t // rr,)
    return _carried_call(
        body, rider, *_grid_ends(grid), name="delta_scan", grid=grid,
        in_specs=[row] * 5 + [pl.BlockSpec((nc, HEADS * LANE), lambda i: (i, 0))],
        out_specs=[row, pl.BlockSpec((nc, HEADS, LANE, LANE), lambda i: (i, 0, 0, 0))],
        out_shape=[jax.ShapeDtypeStruct((t, HEADS * LANE), F32),
                   jax.ShapeDtypeStruct((t // DN_CHUNK, HEADS, LANE, LANE), F32)],
        scratch_shapes=[pltpu.VMEM((HEADS, LANE, LANE), F32)], sem=("arbitrary",), args=(u, w, qd, kt, au, gl))


def _delta_scan_bwd(u, w, qd, kt, intra, gl, sall, do, rider=None):
    t = u.shape[0]
    rr = min(_SCAN_ROWS, t)
    nc = rr // DN_CHUNK
    ng = t // rr

    def body(u_ref, w_ref, qd_ref, kt_ref, a_ref, gl_ref, sall_ref, do_ref,
             du_ref, dw_ref, dqd_ref, dkt_ref, da_ref, dgl_ref, ds_scr):
        @pl.when(pl.program_id(0) == 0)
        def _():
            ds_scr[...] = jnp.zeros_like(ds_scr)

        def chunk(cc, carry):
            c = nc - 1 - cc
            r0 = pl.multiple_of(c * DN_CHUNK, DN_CHUNK)
            rows = pl.ds(r0, DN_CHUNK)
            e = jnp.exp(gl_ref[pl.ds(c, 1), :])
            states = [sall_ref[c, h] for h in range(HEADS)]
            ds_outs = [ds_scr[h] for h in range(HEADS)]
            u_a, w_a, kt_a, qd_a, do_a = u_ref[rows, :], w_ref[rows, :], kt_ref[rows, :], qd_ref[rows, :], do_ref[rows, :]
            a_a = [a_ref[h, rows, :] for h in range(HEADS)]
            da, dqd, dkt, du, dw, dgl, ds_new = [], [], [], [], [], [], []
            for h in range(HEADS):
                cs = slice(h * LANE, (h + 1) * LANE)
                s, ds_out = states[h], ds_outs[h]
                w_c, kt_c, qd_c, do_c = w_a[:, cs], kt_a[:, cs], qd_a[:, cs], do_a[:, cs]
                v_new = u_a[:, cs] - _dot(w_c, s, _NN)
                dv_new = _dot(a_a[h], do_c, _TN) + _dot(kt_c, ds_out, _NN)
                cots = jnp.concatenate([do_c, dv_new], axis=0)
                both = _dot(cots, s, _NT)
                dqd.append(both[:DN_CHUNK])
                dw.append(-both[DN_CHUNK:])
                da.append(_dot(do_c, v_new, _NT))
                dkt.append(_dot(v_new, ds_out, _NT))
                du.append(dv_new)
                eh = e[:, cs]
                dgl.append(jnp.broadcast_to(jnp.sum(ds_out * s, axis=0, keepdims=True) * eh, (8, LANE)))
                ds_new.append(ds_out * eh + _dot(jnp.concatenate([qd_c, -w_c], axis=0), cots, _TN))
            cat = lambda parts: jnp.concatenate(parts, axis=1)
            dqd_ref[rows, :], dkt_ref[rows, :], du_ref[rows, :], dw_ref[rows, :] = cat(dqd), cat(dkt), cat(du), cat(dw)
            dgl_ref[pl.ds(pl.multiple_of(c * 8, 8), 8), :] = cat(dgl)
            for h in range(HEADS):
                da_ref[h, rows, :] = da[h]
                ds_scr[h] = ds_new[h]
            return carry

        lax.fori_loop(0, nc, chunk, 0)

    rev = lambda i: (ng - 1 - i, 0)
    row = pl.BlockSpec((rr, HEADS * LANE), rev)
    a_spec = pl.BlockSpec((HEADS, rr, DN_CHUNK), lambda i: (0, ng - 1 - i, 0))
    gl_spec = pl.BlockSpec((nc, HEADS * LANE), rev)
    wide = jax.ShapeDtypeStruct((t, HEADS * LANE), F32)
    outs, carried = _carried_call(
        body, rider, *_grid_ends((ng,)), name="delta_scan_bwd", grid=(ng,),
        in_specs=[row] * 4 + [a_spec, gl_spec, pl.BlockSpec((nc, HEADS, LANE, LANE), lambda i: (ng - 1 - i, 0, 0, 0)), row],
        out_specs=[row] * 4 + [a_spec, pl.BlockSpec((nc * 8, HEADS * LANE), rev)],
        out_shape=[wide] * 4 + [jax.ShapeDtypeStruct((HEADS, t, DN_CHUNK), F32),
                                jax.ShapeDtypeStruct((t // DN_CHUNK * 8, HEADS * LANE), F32)],
        scratch_shapes=[pltpu.VMEM((HEADS, LANE, LANE), F32)], sem=("arbitrary",), args=(u, w, qd, kt, intra, gl, sall, do))
    return tuple(outs[:5]) + (outs[5].reshape(t // DN_CHUNK, 8, HEADS * LANE)[:, 0, :],), carried


_ATT_TILE = 512
_ATT_FWD_HEADS = 4
_ATT_BWD_HEADS = 4


def _kv_rows(j, tk):
    return pl.ds(pl.multiple_of(j * tk, tk), tk)


def _att_scores(ql, qr, ckv_ref, kr_ref, j, tk):
    ks = _kv_rows(j, tk)
    return (_dot(ql, ckv_ref[ks, :], _NT) + _dot(qr, kr_ref[ks, :], _NT)) * ATT_SCALE


def _diag_mask(s):
    qi = lax.broadcasted_iota(jnp.int32, s.shape, 0) % s.shape[1]
    ki = lax.broadcasted_iota(jnp.int32, s.shape, 1)
    return jnp.where(ki <= qi, s, NEG_BIG)


def _attention(qn, qr_pre, cosb, sinb, ckv, kr, wuk, wuv):
    t = ckv.shape[0]
    tq = min(_ATT_TILE, t)
    hp = _ATT_FWD_HEADS
    nl = tq // LANE

    def lane_fold(v, op):
        out = v[:, :LANE]
        for k in range(1, nl):
            out = op(out, v[:, k * LANE:(k + 1) * LANE])
        return out

    def body(qn_ref, qr_ref, cos_ref, sin_ref, ckv_ref, kr_ref, wuk_ref, wuv_ref, o_ref, lse_ref, qrope_ref, omla_ref,
             s_all, m_lanes, l_lanes, acc_scr):
        hg, qi = pl.program_id(0), pl.program_id(1)
        lanes = lambda ref, k: ref[:, k * LANE:(k + 1) * LANE]
        rows = lambda parts: jnp.concatenate(parts, axis=0)
        part = lambda v, k: v[k * tq:(k + 1) * tq]
        q_lat = rows([_dot(lanes(qn_ref, k), wuk_ref[hg * hp + k], _NT) for k in range(hp)]).astype(_CDT)
        roped = [_rope(lanes(qr_ref, k), cos_ref[...], sin_ref[...]).astype(qrope_ref.dtype) for k in range(hp)]
        qrope_ref[...] = jnp.concatenate(roped, axis=1)
        q_rope = rows(roped)
        m_lanes[...] = jnp.full_like(m_lanes, NEG_BIG)

        def scores(j, masked):
            s = _att_scores(q_lat, q_rope, ckv_ref, kr_ref, j, tq)
            if masked:
                s = _diag_mask(s)
            s_all[j] = s
            m_lanes[...] = jnp.maximum(m_lanes[...], lane_fold(s, jnp.maximum))

        def scores_body(j, carry):
            scores(j, False)
            return carry

        lax.fori_loop(0, qi, scores_body, 0)
        scores(qi, True)
        m = jnp.max(m_lanes[...], axis=-1, keepdims=True)
        mb = jnp.broadcast_to(m, (hp * tq, LANE))
        l_lanes[...] = jnp.zeros_like(l_lanes)
        acc_scr[...] = jnp.zeros_like(acc_scr)

        def weigh(j, carry):
            s = s_all[j]
            p = jnp.concatenate([jnp.exp(s[:, k * LANE:(k + 1) * LANE] - mb) for k in range(nl)], axis=1)
            l_lanes[...] += lane_fold(p, jnp.add)
            acc_scr[...] += _dot(p, ckv_ref[_kv_rows(j, tq), :], _NN)
            return carry

        lax.fori_loop(0, qi + 1, weigh, 0)
        l = jnp.sum(l_lanes[...], axis=-1, keepdims=True)
        out = acc_scr[...] / l
        lse_v = m + jnp.log(l)
        for k in range(hp):
            o_ref[k] = part(out, k)
            lse_ref[k] = part(lse_v, k)
        omla_ref[...] = jnp.concatenate(
            [_dot(part(out, k), wuv_ref[hg * hp + k], _NN) for k in range(hp)], axis=1).astype(omla_ref.dtype)

    col = pl.BlockSpec((tq, hp * LANE), lambda h, i: (i, h))
    table = pl.BlockSpec((tq, ROPE_PAD), lambda h, i: (i, 0))
    wspec = pl.BlockSpec((HEADS, KV_LORA, NOPE), lambda h, i: (0, 0, 0))
    return pl.pallas_call(
        body, name="attention", grid=(HEADS // hp, t // tq),
        in_specs=[col, col, table, table, pl.BlockSpec((t, KV_LORA), lambda h, i: (0, 0)),
                  pl.BlockSpec((t, ROPE_PAD), lambda h, i: (0, 0)), wspec, wspec],
        out_specs=[pl.BlockSpec((hp, tq, KV_LORA), lambda h, i: (h, i, 0)),
                   pl.BlockSpec((hp, tq, 1), lambda h, i: (h, i, 0)), col, col],
        out_shape=[jax.ShapeDtypeStruct((HEADS, t, KV_LORA), F32), jax.ShapeDtypeStruct((HEADS, t, 1), F32),
                   jax.ShapeDtypeStruct((t, HEADS * ROPE_PAD), _CDT), jax.ShapeDtypeStruct((t, HEADS * NOPE), _CDT)],
        scratch_shapes=[pltpu.VMEM((t // tq, hp * tq, tq), F32), pltpu.VMEM((hp * tq, LANE), F32),
                        pltpu.VMEM((hp * tq, LANE), F32), pltpu.VMEM((hp * tq, KV_LORA), F32)],
        compiler_params=_cparams(("parallel", "parallel")))(qn, qr_pre, cosb, sinb, ckv, kr, wuk, wuv)


def _attention_bwd(qn, qr, cosb, sinb, ckv, kr, wuk, wuv, out, lse, do_mla):
    t = ckv.shape[0]
    tq = min(_ATT_TILE, t)
    hp = _ATT_BWD_HEADS

    def body(qn_ref, qr_ref, cos_ref, sin_ref, ckv_ref, kr_ref, wuk_ref, wuv_ref, o_ref, lse_ref, do_ref,
             dql_ref, dqr_ref, dqn_ref, dckv_ref, dkr_ref, dql_scr, dqr_scr):
        hg, qi = pl.program_id(0), pl.program_id(1)

        @pl.when((hg == 0) & (qi == 0))
        def _():
            dckv_ref[...] = jnp.zeros_like(dckv_ref)
            dkr_ref[...] = jnp.zeros_like(dkr_ref)

        lanes = lambda ref, k: ref[:, k * LANE:(k + 1) * LANE]
        rows = lambda parts: jnp.concatenate(parts, axis=0)
        q_lat = rows([_dot(lanes(qn_ref, k), wuk_ref[hg * hp + k], _NT) for k in range(hp)]).astype(_CDT)
        q_rope = rows([lanes(qr_ref, k) for k in range(hp)])
        d_out = rows([_dot(lanes(do_ref, k), wuv_ref[hg * hp + k], _NT) for k in range(hp)])
        d_o = d_out.astype(_CDT)
        lse_v = rows([lse_ref[k] for k in range(hp)])
        dsum = jnp.sum(d_out * rows([o_ref[k] for k in range(hp)]), axis=-1, keepdims=True)
        dql_scr[...] = jnp.zeros_like(dql_scr)
        dqr_scr[...] = jnp.zeros_like(dqr_scr)

        def step(j, masked):
            ks = _kv_rows(j, tq)
            s = _att_scores(q_lat, q_rope, ckv_ref, kr_ref, j, tq)
            if masked:
                s = _diag_mask(s)
            p = jnp.exp(s - lse_v)
            kv = ckv_ref[ks, :]
            ds = (p * (_dot(d_o, kv, _NT) - dsum) * ATT_SCALE).astype(_CDT)
            pb = p.astype(_CDT)
            dql_scr[...] += _dot(ds, kv, _NN)
            dqr_scr[...] += _dot(ds, kr_ref[ks, :], _NN)
            dckv_ref[ks, :] += _dot(pb, d_o, _TN) + _dot(ds, q_lat, _TN)
            dkr_ref[ks, :] += _dot(ds, q_rope, _TN)

        def loop_body(j, carry):
            step(j, False)
            return carry

        lax.fori_loop(0, qi, loop_body, 0)
        step(qi, True)
        cols = lambda parts: jnp.concatenate(parts, axis=1)
        part = lambda v, k: v[k * tq:(k + 1) * tq]
        dql = dql_scr[...].astype(dql_ref.dtype)
        dqr = dqr_scr[...]
        for k in range(hp):
            dql_ref[k] = part(dql, k)
        dqn_ref[...] = cols([_dot(part(dql, k), wuk_ref[hg * hp + k], _NN) for k in range(hp)]).astype(dqn_ref.dtype)
        dqr_ref[...] = cols([_rope_bwd(part(dqr, k), cos_ref[...], sin_ref[...]) for k in range(hp)]).astype(dqr_ref.dtype)

    lat = pl.BlockSpec((hp, tq, KV_LORA), lambda h, i: (h, i, 0))
    col = pl.BlockSpec((tq, hp * LANE), lambda h, i: (i, h))
    table = pl.BlockSpec((tq, ROPE_PAD), lambda h, i: (i, 0))
    kfull = pl.BlockSpec((t, KV_LORA), lambda h, i: (0, 0))
    rfull = pl.BlockSpec((t, ROPE_PAD), lambda h, i: (0, 0))
    wspec = pl.BlockSpec((HEADS, KV_LORA, NOPE), lambda h, i: (0, 0, 0))
    wide = jax.ShapeDtypeStruct((t, HEADS * LANE), _CDT)
    return pl.pallas_call(
        body, name="attention_bwd", grid=(HEADS // hp, t // tq),
        in_specs=[col, col, table, table, kfull, rfull, wspec, wspec, lat,
                  pl.BlockSpec((hp, tq, 1), lambda h, i: (h, i, 0)), col],
        out_specs=[lat, col, col, kfull, rfull],
        out_shape=[jax.ShapeDtypeStruct((HEADS, t, KV_LORA), _CDT), wide, wide,
                   jax.ShapeDtypeStruct((t, KV_LORA), F32), jax.ShapeDtypeStruct((t, ROPE_PAD), F32)],
        scratch_shapes=[pltpu.VMEM((hp * tq, KV_LORA), F32), pltpu.VMEM((hp * tq, ROPE_PAD), F32)],
        compiler_params=_cparams(("arbitrary", "arbitrary")))(qn, qr, cosb, sinb, ckv, kr, wuk, wuv, out, lse, do_mla)


_DX_ROWS = 256


def _dx_fused(pairs, add, add_scale, rider=None, name="b_dx"):
    t, d = add.shape
    tm = min(_DX_ROWS, t)
    n = len(pairs)

    def body(*refs):
        acc = refs[2 * n][...] * add_scale
        for i in range(n):
            acc = acc + _dot(refs[i][...], refs[n + i][...], _NT)
        refs[2 * n + 1][...] = acc

    in_specs = [pl.BlockSpec((tm, a.shape[1]), lambda i: (i, 0)) for a, _ in pairs]
    in_specs += [pl.BlockSpec(w.shape, lambda i: (0, 0)) for _, w in pairs]
    row = pl.BlockSpec((tm, d), lambda i: (i, 0))
    grid = (t // tm,)
    (dx,), carried = _carried_call(
        body, rider, *_grid_ends(grid), name=name, grid=grid, in_specs=in_specs + [row], out_specs=[row],
        out_shape=[jax.ShapeDtypeStruct((t, d), F32)], scratch_shapes=[], sem=("arbitrary",),
        args=tuple(a for a, _ in pairs) + tuple(w for _, w in pairs) + (add,))
    return dx, carried


def _ffn_in_swiglu(a, w):
    t, k = a.shape
    hid = w.shape[1] // 2
    tm, tn = _tile(t, 1024), _tile(hid, 1408)
    nj = hid // tn

    def body(a_ref, bg_ref, bu_ref, act_ref, gt_ref, up_ref):
        av = a_ref[...].astype(_CDT)
        gt = jnp.dot(av, bg_ref[...].astype(_CDT), preferred_element_type=F32)
        up = jnp.dot(av, bu_ref[...].astype(_CDT), preferred_element_type=F32)
        act_ref[...] = _swiglu(gt, up).astype(act_ref.dtype)
        gt_ref[...] = gt.astype(gt_ref.dtype)
        up_ref[...] = up.astype(up_ref.dtype)

    out = pl.BlockSpec((tm, tn), lambda i, j: (i, j))
    return pl.pallas_call(
        body, name="f_ffn_in_swiglu", grid=(t // tm, nj),
        in_specs=[pl.BlockSpec((tm, k), lambda i, j: (i, 0)), pl.BlockSpec((k, tn), lambda i, j: (0, j)),
                  pl.BlockSpec((k, tn), lambda i, j: (0, nj + j))],
        out_specs=[out] * 3, out_shape=[jax.ShapeDtypeStruct((t, hid), _CDT)] * 3,
        compiler_params=_cparams(("parallel", "parallel")))(a, w, w)


def _gated_norm(o, z, w):
    return _rms_norm(o, w) * _silu(z)


def _gated_norm_heads(o, z, w):
    heads = [_gated_norm(o[:, h * LANE:(h + 1) * LANE], z[:, h * LANE:(h + 1) * LANE], w) for h in range(HEADS)]
    return jnp.concatenate(heads, axis=1)


def _mla_pre(ckv, krp, cq, cosb, sinb, qw, kw):
    return _rms_norm(cq, qw), _rms_norm(ckv, kw), _rope(krp, cosb, sinb)


def _merge(gg, y_dn, y_mla):
    return _sigmoid(gg[:, :D_MODEL]) * y_dn + _sigmoid(gg[:, D_MODEL:]) * y_mla


def _ln1(xv, attn_out, g, b):
    return _layer_norm(ALPHA * xv + attn_out, g, b)


def _final(h1, ffn, gate_pre, ple_proj, g, b):
    return _layer_norm(ALPHA * h1 + ffn + _sigmoid(gate_pre) * ple_proj, g, b)


def _swiglu(gt, up):
    return _silu(gt) * up


def _local_step(x, p, cosb, sinb, target, wt, sp, exch):
    t = x.shape[0]
    bf = _CDT
    xb = x.astype(bf)
    g = {}

    qkv_pre = _mm2(xb, wt['qkv'], name="f_qkv")
    z = _mm2(xb, wt['z'], name="f_z")
    gg = _mm2(xb, wt['gg'], name="f_gg")
    pm = _mm2(xb, wt['mla'], name="f_mla")
    qkv_act = _conv_silu(qkv_pre, sp['conv_w'])
    (u, w_, qd, kt, intra, gl, t_inv, qd2, au), sent = _delta_local(qkv_act, pm, sp['a_log'], sp['dt_bias'],
                                                                    rider=exch.gather_send())
    (o_dn, sall), passed = _delta_scan(u, w_, qd2, kt, au, gl, rider=exch.gather_pass(sent))
    wt = dict(wt, **exch.weights(passed))
    def gated_norm_br(h, o, zz, w, wbr):
        og_v = _gated_norm_heads(o, zz, w).astype(bf)
        return og_v, jnp.dot(og_v, wbr.astype(bf), preferred_element_type=F32)

    og, y_dn = _rowwise(gated_norm_br, [(o_dn, D_MODEL, 0, False), (z, D_MODEL, 0, False)],
                        [sp['dn_norm_w'], wt['br_dn']],
                        [(D_MODEL, D_MODEL, False, bf), (D_MODEL, D_MODEL, False, F32)], name="f_gated_norm_br", tm=512)

    def mla_pre_uq(h, ckv, krp, cq, cosv, sinv, qw, kw, wn, wr):
        c_q_v, c_kv_v, k_rope_v = _mla_pre(ckv, krp, cq, cosv, sinv, qw, kw)
        c_q_b = c_q_v.astype(bf)
        return (c_q_b, c_kv_v, k_rope_v, jnp.dot(c_q_b, wn.astype(bf), preferred_element_type=F32),
                jnp.dot(c_q_b, wr.astype(bf), preferred_element_type=F32))

    c_q, c_kv, k_rope, q_nope, q_rope_pre = _rowwise(
        mla_pre_uq,
        [(pm, KV_LORA, 0, False), (pm, ROPE_PAD, 2, False), (pm, Q_LORA, 2, False),
         (cosb, ROPE_PAD, 0, False), (sinb, ROPE_PAD, 0, False)],
        [sp['q_norm_w'], sp['kv_norm_w'], wt['uq_nope'], wt['uq_rope']],
        [(Q_LORA, Q_LORA, False, bf), (KV_LORA, KV_LORA, False, bf), (ROPE_PAD, ROPE_PAD, False, bf),
         (HEADS * NOPE, HEADS * NOPE, False, bf), (HEADS * ROPE_PAD, HEADS * ROPE_PAD, False, F32)], name="f_mla_pre_uq",
        tm=512)
    out_lat, lse, q_rope, o_mla = _attention(q_nope, q_rope_pre, cosb, sinb, c_kv, k_rope, wt['uk'], wt['uv'])
    y_mla = _mm2(o_mla, wt['br_mla'], name="f_br_mla")

    def merge_o_ln1(h, ggv, yd, ym, xv, wo, gv, bv):
        mixed_v = _merge(ggv, yd, ym).astype(bf)
        ao = jnp.dot(mixed_v, wo.astype(bf), preferred_element_type=F32)
        h1v = _ln1(xv, ao, gv, bv)
        return mixed_v, ao, h1v, h1v

    mixed, attn_out, h1, h1b = _rowwise(
        merge_o_ln1, [(gg, 2 * D_MODEL, 0, False), (y_dn, D_MODEL, 0, False), (y_mla, D_MODEL, 0, False), (x, D_MODEL, 0, False)],
        [wt['o'], sp['ln1_g'], sp['ln1_b']],
        [(D_MODEL, D_MODEL, False, bf), (D_MODEL, D_MODEL, False, F32), (D_MODEL, D_MODEL, False, F32),
         (D_MODEL, D_MODEL, False, bf)], name="f_merge_o_ln1", tm=512)
    act, ffn_gt, ffn_up = _ffn_in_swiglu(h1b, wt['ffn_in'])
    pb = p.astype(bf)

    def final_fn(h, h1v, actv, pv, tgt, gt, up, wfo, wpg, wpl, gv, bv):
        ffnv = jnp.dot(actv.astype(bf), wfo.astype(bf), preferred_element_type=F32)
        gpv = jnp.dot(h1v.astype(bf), wpg.astype(bf), preferred_element_type=F32)
        ppv = jnp.dot(pv.astype(bf), wpl.astype(bf), preferred_element_type=F32)
        y, vjp = jax.vjp(_final, h1v, ffnv, gpv, ppv, gv, bv)
        err = y - tgt
        dh1, dffn, dgp, dpp, dg, db = vjp(err * (1.0 / D_MODEL))
        sq = err * err
        lanes = sq[:, :LANE]
        for j in range(1, D_MODEL // LANE):
            lanes = lanes + sq[:, j * LANE:(j + 1) * LANE]
        loss = jnp.sum(lanes, axis=0, keepdims=True) * (0.5 / D_MODEL)
        dffn_b = dffn.astype(bf)
        dact = _dot(dffn_b, wfo, _NT).astype(bf).astype(F32)
        _, vjp_s = jax.vjp(_swiglu, gt.astype(F32), up.astype(F32))
        dgt, dup = vjp_s(dact)
        return dffn, dffn_b, dgp, dpp, jnp.concatenate([dgt, dup], axis=1), dg, db, loss

    dpre2, dpre2b, dgate_pre, dple_proj, dffn_in, g['ln2_g'], g['ln2_b'], loss_lanes = _rowwise(
        final_fn, [(h1, D_MODEL, 0, False), (act, FFN_HIDDEN, 0, False), (pb, PLE_DIM, 0, False), (target, D_MODEL, 0, False),
                   (ffn_gt, FFN_HIDDEN, 0, False), (ffn_up, FFN_HIDDEN, 0, False)],
        [wt['ffn_out'], wt['ple_gate'], wt['ple'], sp['ln2_g'], sp['ln2_b']],
        [(D_MODEL, D_MODEL, False, F32)] + [(D_MODEL, D_MODEL, False, bf)] * 3 + [(2 * FFN_HIDDEN, 2 * FFN_HIDDEN, False, bf)],
        [(1, D_MODEL), (1, D_MODEL), (1, LANE)], name="b_final")
    g['loss_lanes'] = loss_lanes

    g['ple'] = _mm2(pb, dple_proj, ta=True, name="g_ple")
    g['ple_gate'] = _mm2(h1b, dgate_pre, ta=True, name="g_ple_gate")
    g['ffn_out'] = _mm2(act, dpre2b, ta=True, name="g_ffn_out")
    g['ffn_in'] = _mm2(h1b, dffn_in, ta=True, name="g_ffn_in")
    dh1, _ = _dx_fused([(dffn_in, wt['ffn_in']), (dgate_pre, wt['ple_gate'])], dpre2, ALPHA, name="b_dh1")

    def attn_out_bwd(h, xv, ao, d, ggv, yd, ym, o, zz, wo, wbd, wbm, gv, bv, nw):
        _, vjp = jax.vjp(_ln1, xv, ao, gv, bv)
        _, dao, dg, db = vjp(d)
        dao_b = dao.astype(bf)
        _, vjp_m = jax.vjp(_merge, ggv, yd, ym)
        dggv, dyd, dym = vjp_m(_dot(dao_b, wo, _NT))
        dyd_b, dym_b = dyd.astype(bf), dym.astype(bf)
        _, vjp_n = jax.vjp(_gated_norm_heads, o, zz, nw)
        do_v, dz_v, dnw = vjp_n(_dot(dyd_b, wbd, _NT))
        return dao, dao_b, dggv, dyd_b, dym_b, do_v, dz_v, _dot(dym_b, wbm, _NT), dg, db, dnw

    row_d = lambda a: (a, D_MODEL, 0, False)
    dpre1, dpre1b, dgg, dy_dn, dy_mla, do_dn, dz, do_mla, g['ln1_g'], g['ln1_b'], g['dn_norm_w'] = _rowwise(
        attn_out_bwd, [row_d(x), row_d(attn_out), row_d(dh1), (gg, 2 * D_MODEL, 0, False), row_d(y_dn), row_d(y_mla),
                       row_d(o_dn), row_d(z)],
        [wt['o'], wt['br_dn'], wt['br_mla'], sp['ln1_g'], sp['ln1_b'], sp['dn_norm_w']],
        [(D_MODEL, D_MODEL, False, F32), (D_MODEL, D_MODEL, False, bf), (2 * D_MODEL, 2 * D_MODEL, False, bf),
         (D_MODEL, D_MODEL, False, bf), (D_MODEL, D_MODEL, False, bf), (D_MODEL, D_MODEL, False, F32),
         (D_MODEL, D_MODEL, False, bf), (D_MODEL, D_MODEL, False, bf)],
        [(1, D_MODEL), (1, D_MODEL), (1, LANE)], name="b_attn_out")
    g['o'] = _mm2(mixed, dpre1b, ta=True, name="g_o")
    g['br_dn'] = _mm2(og, dy_dn, ta=True, name="g_br_dn")
    g['br_mla'] = _mm2(o_mla, dy_mla, ta=True, name="g_br_mla")

    g['uv'] = _mm(out_lat, do_mla, name="g_uv", ta=True, heads=HEADS, a_head='lead', b_head='col', out_head='lead',
                  dims=(KV_LORA, NOPE, t))
    dq_lat, dq_rope_pre, dq_nope, dckv_att, dkr_att = _attention_bwd(
        q_nope, q_rope, cosb, sinb, c_kv, k_rope, wt['uk'], wt['uv'], out_lat, lse, do_mla)
    g['uk'] = _mm(dq_lat, q_nope, name="g_uk", ta=True, heads=HEADS, a_head='lead', b_head='col', out_head='lead',
                  dims=(KV_LORA, NOPE, t))
    g['uq_nope'] = _mm2(c_q, dq_nope, ta=True, name="g_uq_nope")
    g['uq_rope'] = _mm2(c_q, dq_rope_pre, ta=True, name="g_uq_rope")
    dc_q = _mm2(dq_nope, wt['uq_nope'], tb=True, name="b_dcq_nope")
    dc_q = _mm2(dq_rope_pre, wt['uq_rope'], tb=True, name="b_dcq_rope", add=dc_q)

    (du, dw, dqd, dkt, dintra, dgl), paired = _delta_scan_bwd(u, w_, qd, kt, intra, gl, sall, do_dn, rider=exch.pair_send(g))
    (dq_a, dk_a, dv_a, dba, g['a_log'], g['dt_bias']), arrived = _delta_local_bwd(
        qkv_act, pm, sp['a_log'], sp['dt_bias'], t_inv, du, dw, dqd, dkt, dintra, dgl, rider=exch.reduce_send(paired))
    exch.reduce_arrived(arrived)
    dqkv_pre, g['conv_w'] = _conv_silu_bwd(qkv_pre, sp['conv_w'], [dq_a, dk_a, dv_a])

    def mla_pre_bwd(h, ckv, cq, cosv, sinv, dcq, dckv, dkr, dba_v, qw, kw):
        _, vjp = jax.vjp(lambda a, c, d, e: (_rms_norm(c, d), _rms_norm(a, e)), ckv, cq, qw, kw)
        dckv_p, dcq_p, dqw, dkw = vjp((dcq, dckv))
        dkr_p = _rope_bwd(dkr, cosv, sinv)
        dpm = jnp.concatenate([dckv_p, dkr_p, dba_v, jnp.zeros((ckv.shape[0], 2 * LANE), F32), dcq_p], axis=1)
        return dpm, dqw, dkw

    dpm, g['q_norm_w'], g['kv_norm_w'] = _rowwise(
        mla_pre_bwd,
        [(pm, KV_LORA, 0, False), (pm, Q_LORA, 2, False), (cosb, ROPE_PAD, 0, False), (sinb, ROPE_PAD, 0, False),
         (dc_q, Q_LORA, 0, False), (dckv_att, KV_LORA, 0, False), (dkr_att, ROPE_PAD, 0, False), (dba, LANE, 0, False)],
        [sp['q_norm_w'], sp['kv_norm_w']], [(1152, 1152, False, bf)], [(1, Q_LORA), (1, KV_LORA)], name="b_mla_pre")

    rider = exch.small_send(g)
    if rider is None:
        g['qkv'] = _mm2(xb, dqkv_pre, ta=True, name="g_qkv")
    else:
        g['qkv'], got = _mm2(xb, dqkv_pre, ta=True, name="g_qkv", rider=rider)
        exch.small_arrived(got)
    g['z'] = _mm2(xb, dz, ta=True, name="g_z")
    g['gg'] = _mm2(xb, dgg, ta=True, name="g_gg")
    g['mla'] = _mm2(xb, dpm, ta=True, name="g_mla")
    dx, arrived = _dx_fused([(dqkv_pre, wt['qkv']), (dz, wt['z']), (dgg, wt['gg']), (dpm, wt['mla'])], dpre1, ALPHA,
                            rider=exch.in_send(g))
    exch.in_arrived(arrived)
    return loss_lanes, dx, g


_IN_SIZES = (QKV_W, HEADS * DN_DK, HEADS, HEADS, Q_LORA, KV_LORA, ROPE, D_MODEL, D_MODEL)


def _rope_tables(positions):
    inv_freq = ROPE_BASE ** (-jnp.arange(0, ROPE, 2, dtype=F32) / ROPE)
    ang = positions.astype(F32)[:, None] * inv_freq
    cos, sin = jnp.cos(ang), jnp.sin(ang)
    zeros = jnp.zeros((positions.shape[0], ROPE_PAD - ROPE), F32)
    return jnp.concatenate([cos, cos, zeros], axis=1), jnp.concatenate([-sin, sin, zeros], axis=1)


def _prep_w_in(w_in):
    dt = w_in.dtype
    offs = [0]
    for s in _IN_SIZES:
        offs.append(offs[-1] + s)
    qkv, z, wb, wa, cq, ckv, kr, gd, gm = [w_in[:, offs[i]:offs[i + 1]] for i in range(len(_IN_SIZES))]
    zc = lambda n: jnp.zeros((D_MODEL, n), dt)
    return {
        'qkv': qkv, 'z': z, 'gg': jnp.concatenate([gd, gm], axis=1),
        'mla': jnp.concatenate([ckv, kr, zc(ROPE_PAD - ROPE), wb, wa, zc(LANE - 2 * HEADS), zc(2 * LANE), cq], axis=1),
    }


def _prep_weights(full):
    w_uq = full['w_uq']
    wt = {
        'uq_nope': w_uq[:, :, :NOPE].reshape(Q_LORA, HEADS * NOPE),
        'uq_rope': jnp.pad(w_uq[:, :, NOPE:], ((0, 0), (0, 0), (0, ROPE_PAD - ROPE))).reshape(Q_LORA, HEADS * ROPE_PAD),
        'uk': jnp.transpose(full['w_uk'], (1, 0, 2)), 'uv': jnp.transpose(full['w_uv'], (1, 0, 2)),
        'br_dn': full['w_br_dn'], 'br_mla': full['w_br_mla'], 'o': full['w_o'], 'ffn_in': full['w_ffn_in'],
        'ffn_out': full['w_ffn_out'], 'ple': full['w_ple'], 'ple_gate': full['w_ple_gate'],
    }
    return wt


def _prep_small(small):
    pad = lambda v: jnp.pad(v, (0, LANE - v.shape[0]))[None, :]
    return {
        'conv_w': small['conv_w'], 'a_log': pad(small['dn_a_log']), 'dt_bias': pad(small['dn_dt_bias']),
        'dn_norm_w': small['dn_norm_w'][None, :], 'q_norm_w': small['q_norm_w'][None, :],
        'kv_norm_w': small['kv_norm_w'][None, :], 'ln1_g': small['ln1_g'][None, :], 'ln1_b': small['ln1_b'][None, :],
        'ln2_g': small['ln2_g'][None, :], 'ln2_b': small['ln2_b'][None, :],
    }


def _w_in_grad(g):
    mla = g['mla']
    ba0 = KV_LORA + ROPE_PAD
    cq0 = ba0 + 3 * LANE
    return jnp.concatenate([
        g['qkv'], g['z'], mla[:, ba0:ba0 + HEADS], mla[:, ba0 + HEADS:ba0 + 2 * HEADS], mla[:, cq0:cq0 + Q_LORA],
        mla[:, :KV_LORA], mla[:, KV_LORA:KV_LORA + ROPE], g['gg']], axis=1)


def _unprep_grads_late(g):
    return {
        'conv_w': g['conv_w'], 'dn_a_log': g['a_log'][0, :HEADS], 'dn_dt_bias': g['dt_bias'][0, :HEADS],
        'dn_norm_w': g['dn_norm_w'][0], 'q_norm_w': g['q_norm_w'][0], 'kv_norm_w': g['kv_norm_w'][0],
        'ln1_g': g['ln1_g'][0], 'ln1_b': g['ln1_b'][0], 'ln2_g': g['ln2_g'][0], 'ln2_b': g['ln2_b'][0],
    }


def _unprep_grads_early(g):
    w_uq = jnp.concatenate([g['uq_nope'].reshape(Q_LORA, HEADS, NOPE),
                            g['uq_rope'].reshape(Q_LORA, HEADS, ROPE_PAD)[:, :, :ROPE]], axis=2)
    return {
        'w_uq': w_uq, 'w_uk': jnp.transpose(g['uk'], (1, 0, 2)), 'w_uv': jnp.transpose(g['uv'], (1, 0, 2)),
        'w_br_dn': g['br_dn'], 'w_br_mla': g['br_mla'], 'w_o': g['o'],
        'w_ffn_in': g['ffn_in'], 'w_ffn_out': g['ffn_out'], 'w_ple': g['ple'], 'w_ple_gate': g['ple_gate'],
    }


_FLATB_PIECES = (
    ('w_ffn_out', 704, (704, D_MODEL)), ('w_br_dn', 256, (256, D_MODEL)), ('w_br_mla', 256, (256, D_MODEL)),
    ('w_o', 256, (256, D_MODEL)), ('w_ple_gate', 256, (256, D_MODEL)), ('w_uq', 144, (96, HEADS, NOPE + ROPE)),
    ('w_uk', 64, (64, HEADS, NOPE)), ('w_uv', 64, (64, HEADS, NOPE)), ('w_ple', 64, (PLE_DIM, 256)),
)
FLATB_ROWS = 2112
W_IN_SHARD = D_IN // N_SHARD
FFN_IN_SHARD = 2 * FFN_HIDDEN // N_SHARD
A_ROWS = D_MODEL + 32
_CONV_SHARD = QKV_W // N_SHARD
_ADD_TILES = (256, 256, 352)


def _flatb_offsets():
    offs, o = {}, 0
    for name, rows, _ in _FLATB_PIECES:
        offs[name] = o
        o += rows
    return offs, o


def _pack_shards(ws, conv_w):
    conv_bits = lax.bitcast_convert_type(conv_w, jnp.bfloat16).reshape(DN_CONV, 2 * _CONV_SHARD).astype(_CDT)
    tail = jnp.pad(conv_bits, ((0, A_ROWS - D_MODEL - DN_CONV), (0, W_IN_SHARD - 2 * _CONV_SHARD)))
    a_buf = jnp.concatenate([ws['w_in'].astype(_CDT), tail], axis=0)
    parts = [ws[name].astype(_CDT).reshape(rows, FLAT_W) for name, rows, _ in _FLATB_PIECES]
    used = sum(p.shape[0] for p in parts)
    parts.append(jnp.zeros((FLATB_ROWS - used, FLAT_W), _CDT))
    return [a_buf, ws['w_ffn_in'].astype(_CDT), jnp.concatenate(parts, axis=0)]


def _unpack_w_in(gathered, local, me):
    a = [jnp.where(me == s, local, gathered[s]) for s in range(N_SHARD)]
    conv = [lax.bitcast_convert_type(
        p[D_MODEL:D_MODEL + DN_CONV, :2 * _CONV_SHARD].astype(jnp.bfloat16).reshape(DN_CONV, _CONV_SHARD, 2), F32) for p in a]
    return jnp.concatenate([p[:D_MODEL] for p in a], axis=1), jnp.concatenate(conv, axis=1)


def _unpack_rest(gathered, local, me):
    pick = lambda b, s: jnp.where(me == s, local[b], gathered[b][s])
    full = {'w_ffn_in': jnp.concatenate([pick(0, s) for s in range(N_SHARD)], axis=1)}
    offs, _ = _flatb_offsets()
    fb = [pick(1, s) for s in range(N_SHARD)]
    for name, rows, shape in _FLATB_PIECES:
        pieces = [p[offs[name]:offs[name] + rows].reshape(shape) for p in fb]
        full[name] = jnp.concatenate(pieces, axis=1 if name == 'w_ple' else 0)
    return full


def _shard_columns(g, w):
    return jnp.stack([g[:, s * w:(s + 1) * w] for s in range(N_SHARD)])


def _pack_grads_rest(gw):
    parts = []
    for name, rows, _ in _FLATB_PIECES:
        g = gw[name]
        if name == 'w_ple':
            parts.append(_shard_columns(g, PLE_DIM).reshape(N_SHARD, rows, FLAT_W))
        else:
            parts.append(g.reshape(N_SHARD, rows, FLAT_W))
    used = sum(p.shape[1] for p in parts)
    parts.append(jnp.zeros((N_SHARD, FLATB_ROWS - used, FLAT_W), F32))
    return [_shard_columns(gw['w_ffn_in'], FFN_IN_SHARD), jnp.concatenate(parts, axis=1)]


def _unpack_reduced(mine, theirs, c):
    whole = [jnp.concatenate([jnp.where(c == 0, m, t), jnp.where(c == 0, t, m)], axis=0) for m, t in zip(mine, theirs)]
    out = {'w_in': whole[0], 'w_ffn_in': whole[1]}
    offs, _ = _flatb_offsets()
    for name, rows, shape in _FLATB_PIECES:
        out[name] = whole[2][offs[name]:offs[name] + rows].reshape(shape)
    return out


_HBM = pl.BlockSpec(memory_space=pltpu.HBM)


def _place():
    x, y, c = lax.axis_index("x"), lax.axis_index("y"), lax.axis_index("c")
    chips = [(1 - x, y), (x, 1 - y), (1 - x, 1 - y)]
    return x, y, c, chips


def _remote(src, dst, send_sems, recv_sems, k, to):
    return pltpu.make_async_remote_copy(src_ref=src, dst_ref=dst, send_sem=send_sems.at[k], recv_sem=recv_sems.at[k],
                                        device_id=to, device_id_type=_MESH)


def _half_rows(ref, half, hf, lead=None):
    rows = pl.ds(pl.multiple_of(hf * half, 16), half)
    return ref.at[rows, :] if lead is None else ref.at[lead, rows, :]


class _Rider:
    def __init__(self, inputs, out_shape, n_sems, copies, aliases=None):
        self.inputs, self.out_shape, self.n_sems, self.copies = list(inputs), list(out_shape), n_sems, copies
        self.aliases = aliases or {}


def _carried_call(body, rider, first, last, *, name, grid, in_specs, out_specs, out_shape, scratch_shapes, sem, args):
    n_in, n_out, n_scr = len(in_specs), len(out_specs), len(scratch_shapes)
    if rider is None:
        res = pl.pallas_call(body, name=name, grid=grid, in_specs=in_specs, out_specs=out_specs, out_shape=out_shape,
                             scratch_shapes=scratch_shapes, compiler_params=_cparams(sem))(*args)
        return list(res), []
    ri, ro = len(rider.inputs), len(rider.out_shape)

    def full_body(*refs):
        own_in, r_in = refs[:n_in], refs[n_in:n_in + ri]
        o0 = n_in + ri
        own_out, r_out = refs[o0:o0 + n_out], refs[o0 + n_out:o0 + n_out + ro]
        s0 = o0 + n_out + ro
        own_scr, send_sems, recv_sems = refs[s0:s0 + n_scr], refs[s0 + n_scr], refs[s0 + n_scr + 1]

        @pl.when(first())
        def _():
            sends, _ = rider.copies(r_in, r_out, send_sems, recv_sems)
            for cp in sends:
                cp.start()

        body(*own_in, *own_out, *own_scr)

        @pl.when(last())
        def _():
            sends, arrivals = rider.copies(r_in, r_out, send_sems, recv_sems)
            for cp in arrivals():
                cp.wait_recv()
            for cp in sends:
                cp.wait_send()

    res = pl.pallas_call(
        full_body, name=name, grid=grid, in_specs=list(in_specs) + [_HBM] * ri, out_specs=list(out_specs) + [_HBM] * ro,
        out_shape=list(out_shape) + rider.out_shape,
        scratch_shapes=list(scratch_shapes) + [pltpu.SemaphoreType.DMA((rider.n_sems,))] * 2,
        input_output_aliases={n_in + i: n_out + o for i, o in rider.aliases.items()},
        compiler_params=_cparams(sem))(*args, *rider.inputs)
    return list(res[:n_out]), list(res[n_out:])


def _ride_gather_send(bufs):
    n = len(bufs)
    halves = [b.shape[0] // 2 for b in bufs]

    def copies(ins, outs, send_sems, recv_sems):
        x, y, c, chips = _place()
        slot = lambda b, cx, cy: _half_rows(outs[b], halves[b], c, lead=2 * cx + cy)
        sends = [_remote(_half_rows(ins[b], halves[b], c), slot(b, x, y), send_sems, recv_sems, 3 * b + j, (cx, cy, c))
                 for b in range(n) for j, (cx, cy) in enumerate(chips)]
        arrivals = lambda: [_remote(slot(b, cx, cy), slot(b, cx, cy), send_sems, recv_sems, 3 * b + j, (x, y, c))
                            for b in range(n) for j, (cx, cy) in enumerate(chips)]
        return sends, arrivals

    return _Rider(bufs, [jax.ShapeDtypeStruct((N_SHARD,) + b.shape, b.dtype) for b in bufs], 3 * n, copies)


def _ride_gather_pass(gathered):
    n = len(gathered)
    halves = [g.shape[1] // 2 for g in gathered]

    def copies(ins, outs, send_sems, recv_sems):
        x, y, c, chips = _place()
        slot = lambda b, cx, cy, hf: _half_rows(outs[b], halves[b], hf, lead=2 * cx + cy)
        sends = [_remote(slot(b, cx, cy, c), slot(b, cx, cy, c), send_sems, recv_sems, 3 * b + j, (x, y, 1 - c))
                 for b in range(n) for j, (cx, cy) in enumerate(chips)]
        arrivals = lambda: [_remote(slot(b, cx, cy, 1 - c), slot(b, cx, cy, 1 - c), send_sems, recv_sems, 3 * b + j, (x, y, c))
                            for b in range(n) for j, (cx, cy) in enumerate(chips)]
        return sends, arrivals

    return _Rider(gathered, [jax.ShapeDtypeStruct(g.shape, g.dtype) for g in gathered], 3 * n, copies,
                  aliases={b: b for b in range(n)})


def _ride_small_gather(buf):
    def copies(ins, outs, send_sems, recv_sems):
        x, y, c, _ = _place()
        flip = lambda v, d: 1 - v if d else v
        sends, peers = [], []
        for dx in (0, 1):
            for dy in (0, 1):
                for dc in (0, 1):
                    if dx or dy or dc:
                        k = 4 * dx + 2 * dy + dc - 1
                        px, py, pc = flip(x, dx), flip(y, dy), flip(c, dc)
                        sends.append(_remote(ins[0], outs[0].at[4 * x + 2 * y + c], send_sems, recv_sems, k, (px, py, pc)))
                        peers.append((k, 4 * px + 2 * py + pc))
        arrivals = lambda: [_remote(ins[0], outs[0].at[slot], send_sems, recv_sems, k, (x, y, c)) for k, slot in peers]
        return sends, arrivals

    return _Rider([buf], [jax.ShapeDtypeStruct((8,) + buf.shape, buf.dtype)], 7, copies)


def _small_sum(gathered, buf, me_arr):
    n, r, width = gathered.shape

    def body(me_ref, g_ref, b_ref, o_ref):
        total = jnp.zeros((r, width), F32)
        for d in range(n):
            total = total + jnp.where(me_ref[0] == d, b_ref[...], g_ref[d])
        o_ref[...] = total

    return pl.pallas_call(
        body, name="small_sum", out_shape=jax.ShapeDtypeStruct((r, width), F32),
        grid_spec=pltpu.PrefetchScalarGridSpec(
            num_scalar_prefetch=1, grid=(1,),
            in_specs=[pl.BlockSpec((n, r, width), lambda i, me: (0, 0, 0)), pl.BlockSpec((r, width), lambda i, me: (0, 0))],
            out_specs=pl.BlockSpec((r, width), lambda i, me: (0, 0))),
        compiler_params=_cparams(("arbitrary",)))(me_arr, gathered, buf)


def _ride_pair_exchange(gbufs):
    n = len(gbufs)
    halves = [g.shape[1] // 2 for g in gbufs]

    def copies(ins, outs, send_sems, recv_sems):
        x, y, c, _ = _place()
        sends = [_remote(ins[b].at[:, pl.ds(pl.multiple_of((1 - c) * halves[b], 16), halves[b]), :], outs[b],
                         send_sems, recv_sems, b, (x, y, 1 - c)) for b in range(n)]
        arrivals = lambda: [_remote(outs[b], outs[b], send_sems, recv_sems, b, (x, y, c)) for b in range(n)]
        return sends, arrivals

    return _Rider(gbufs, [jax.ShapeDtypeStruct((N_SHARD, h, g.shape[2]), g.dtype) for g, h in zip(gbufs, halves)], n, copies)


def _ride_chip_exchange(parts):
    n = len(parts)

    def copies(ins, outs, send_sems, recv_sems):
        x, y, c, chips = _place()
        sends = [_remote(ins[b].at[2 * cx + cy], outs[b].at[j], send_sems, recv_sems, 3 * b + j, (cx, cy, c))
                 for b in range(n) for j, (cx, cy) in enumerate(chips)]
        arrivals = lambda: [_remote(ins[b].at[0], outs[b].at[j], send_sems, recv_sems, 3 * b + j, (x, y, c))
                            for b in range(n) for j in range(len(chips))]
        return sends, arrivals

    return _Rider(parts, [jax.ShapeDtypeStruct((3,) + p.shape[1:], p.dtype) for p in parts], 3 * n, copies)


def _gather_shards(bufs, name):
    n = len(bufs)
    halves = [b.shape[0] // 2 for b in bufs]

    def body(*refs):
        ins, outs, send_sems, recv_sems = refs[:n], refs[n:2 * n], refs[2 * n], refs[2 * n + 1]
        x, y, c, chips = _place()
        me, sibling = (x, y, c), (x, y, 1 - c)
        slot = lambda b, cx, cy, hf: _half_rows(outs[b], halves[b], hf, lead=2 * cx + cy)
        first = [_remote(_half_rows(ins[b], halves[b], c), slot(b, x, y, c), send_sems, recv_sems, 6 * b + j, (cx, cy, c))
                 for b in range(n) for j, (cx, cy) in enumerate(chips)]
        for cp in first:
            cp.start()
        passed = []
        for j, (cx, cy) in enumerate(chips):
            for b in range(n):
                _remote(slot(b, cx, cy, c), slot(b, cx, cy, c), send_sems, recv_sems, 6 * b + j, me).wait_recv()
                fwd = _remote(slot(b, cx, cy, c), slot(b, cx, cy, c), send_sems, recv_sems, 6 * b + 3 + j, sibling)
                fwd.start()
                passed.append(fwd)
        for j, (cx, cy) in enumerate(chips):
            for b in range(n):
                _remote(slot(b, cx, cy, 1 - c), slot(b, cx, cy, 1 - c), send_sems, recv_sems, 6 * b + 3 + j, me).wait_recv()
        for cp in first + passed:
            cp.wait_send()

    return pl.pallas_call(
        body, name=name, out_shape=[jax.ShapeDtypeStruct((N_SHARD,) + b.shape, b.dtype) for b in bufs],
        in_specs=[_HBM] * n, out_specs=[_HBM] * n,
        scratch_shapes=[pltpu.SemaphoreType.DMA((6 * n,)), pltpu.SemaphoreType.DMA((6 * n,))],
    )(*bufs)


def _reduce_pair_exchange(gbufs, name):
    n = len(gbufs)
    halves = [g.shape[1] // 2 for g in gbufs]

    def body(*refs):
        ins, outs, send_sems, recv_sems = refs[:n], refs[n:2 * n], refs[2 * n], refs[2 * n + 1]
        x, y, c, _ = _place()
        cps = [_remote(ins[b].at[:, pl.ds(pl.multiple_of((1 - c) * halves[b], 16), halves[b]), :], outs[b],
                       send_sems, recv_sems, b, (x, y, 1 - c)) for b in range(n)]
        for cp in cps:
            cp.start()
        for cp in cps:
            cp.wait()

    return pl.pallas_call(
        body, name=name,
        out_shape=[jax.ShapeDtypeStruct((N_SHARD, h, g.shape[2]), g.dtype) for g, h in zip(gbufs, halves)],
        in_specs=[_HBM] * n, out_specs=[_HBM] * n,
        scratch_shapes=[pltpu.SemaphoreType.DMA((n,)), pltpu.SemaphoreType.DMA((n,))],
    )(*gbufs)


def _pair_add(gbuf, recv, c_arr, tr, name):
    _, rows, width = gbuf.shape
    half = rows // 2
    nt = half // tr

    def body(c_ref, a_ref, b_ref, o_ref):
        o_ref[...] = (a_ref[...] + b_ref[...]).astype(o_ref.dtype)

    blk = lambda f: pl.BlockSpec((None, tr, width), f)
    return pl.pallas_call(
        body, name=name, out_shape=jax.ShapeDtypeStruct((N_SHARD, half, width), jnp.bfloat16),
        grid_spec=pltpu.PrefetchScalarGridSpec(
            num_scalar_prefetch=1, grid=(N_SHARD, nt),
            in_specs=[blk(lambda s, i, c: (s, c[0] * nt + i, 0)), blk(lambda s, i, c: (s, i, 0))],
            out_specs=blk(lambda s, i, c: (s, i, 0))),
        compiler_params=_cparams(("parallel", "parallel")))(c_arr, gbuf, recv)


def _chip_add(part, recv, me_arr, tr, name):
    _, half, width = part.shape

    def body(me_ref, own, a0, a1, a2, o_ref):
        f = lambda r: r[...].astype(F32)
        o_ref[...] = ((f(own) + f(a0)) + f(a1)) + f(a2)

    specs = [pl.BlockSpec((None, tr, width), lambda i, me: (me[0], i, 0))]
    specs += [pl.BlockSpec((None, tr, width), functools.partial(lambda i, me, k: (k, i, 0), k=k)) for k in range(3)]
    return pl.pallas_call(
        body, name=name, out_shape=jax.ShapeDtypeStruct((half, width), F32),
        grid_spec=pltpu.PrefetchScalarGridSpec(
            num_scalar_prefetch=1, grid=(half // tr,), in_specs=specs,
            out_specs=pl.BlockSpec((tr, width), lambda i, me: (i, 0))),
        compiler_params=_cparams(("parallel",)))(me_arr, part, recv, recv, recv)


def _reduce_pair_share(rhalves, name):
    n = len(rhalves)

    def body(*refs):
        ins, outs, send_sems, recv_sems = refs[:n], refs[n:2 * n], refs[2 * n], refs[2 * n + 1]
        x, y, c, _ = _place()
        cps = [_remote(ins[b], outs[b], send_sems, recv_sems, b, (x, y, 1 - c)) for b in range(n)]
        for cp in cps:
            cp.start()
        for cp in cps:
            cp.wait()

    return pl.pallas_call(
        body, name=name, out_shape=[jax.ShapeDtypeStruct(r.shape, r.dtype) for r in rhalves],
        in_specs=[_HBM] * n, out_specs=[_HBM] * n,
        scratch_shapes=[pltpu.SemaphoreType.DMA((n,)), pltpu.SemaphoreType.DMA((n,))],
    )(*rhalves)


def _row_tile(rows, cap):
    if rows <= cap:
        return rows
    t = (cap // 8) * 8
    while t >= 8:
        if rows % t == 0:
            return t
        t -= 8
    return rows


def _adamw(w, g, m, v, name):
    shape = w.shape
    cols = shape[-1] if len(shape) <= 3 else shape[-2] * shape[-1]
    lead = len(shape) == 3
    w2, g2, m2, v2 = (a if lead else a.reshape(-1, cols) for a in (w, g, m, v))
    rows = shape[1] if lead else w2.shape[0]
    tr, tc = _row_tile(rows, 256), cols
    if tr == rows and rows > 256:
        tc = _tile(cols, 256)

    def body(w_ref, g_ref, m_ref, v_ref, d_ref, mo_ref, vo_ref):
        gv = g_ref[...]
        mn = ADAM_B1 * m_ref[...] + (1.0 - ADAM_B1) * gv
        vn = ADAM_B2 * v_ref[...] + (1.0 - ADAM_B2) * (gv * gv)
        m_hat = mn / (1.0 - ADAM_B1 ** ADAM_STEP)
        v_hat = vn / (1.0 - ADAM_B2 ** ADAM_STEP)
        d_ref[...] = -ADAM_LR * (m_hat / (jnp.sqrt(v_hat) + ADAM_EPS) + ADAM_WD * w_ref[...])
        mo_ref[...] = mn
        vo_ref[...] = vn

    blk = (pl.BlockSpec((None, tr, tc), lambda i, j: (0, i, j)) if lead else pl.BlockSpec((tr, tc), lambda i, j: (i, j)))
    outs = pl.pallas_call(
        body, name=name, grid=(rows // tr, cols // tc), in_specs=[blk] * 4, out_specs=[blk] * 3,
        out_shape=[jax.ShapeDtypeStruct(w2.shape, F32)] * 3,
        compiler_params=_cparams(("parallel", "parallel")))(w2, g2, m2, v2)
    return tuple(o.reshape(shape) for o in outs)


_WEIGHT_NAMES = ('w_in', 'conv_w', 'dn_a_log', 'dn_dt_bias', 'dn_norm_w', 'q_norm_w', 'w_uq', 'kv_norm_w', 'w_uk',
                 'w_uv', 'w_br_dn', 'w_br_mla', 'w_o', 'ln1_g', 'ln1_b', 'w_ffn_in', 'w_ffn_out', 'w_ple',
                 'w_ple_gate', 'ln2_g', 'ln2_b')
_SMALL_NAMES = ('ln1_g', 'ln1_b', 'ln2_g', 'ln2_b', 'q_norm_w', 'kv_norm_w', 'dn_norm_w', 'dn_a_log', 'dn_dt_bias')
_SMALL_GROUP = 8
_CONV_SMALL_ROW = len(_SMALL_NAMES) * _SMALL_GROUP
_CONV_SMALL_ROWS = DN_CONV * QKV_W // FLAT_W


_LOSS_SMALL_ROW = _CONV_SMALL_ROW + 16


def _pack_small(gw, loss_lanes):
    rows = [jnp.pad(gw[n][None, :], ((0, _SMALL_GROUP - 1), (0, FLAT_W - gw[n].shape[0]))) for n in _SMALL_NAMES]
    rows.append(jnp.pad(gw['conv_w'].reshape(_CONV_SMALL_ROWS, FLAT_W), ((0, 16 - _CONV_SMALL_ROWS), (0, 0))))
    rows.append(jnp.pad(loss_lanes, ((0, _SMALL_GROUP - 1), (0, FLAT_W - LANE))))
    return jnp.concatenate(rows, axis=0)


class _Exchange:
    def __init__(self, local, me_chip, c_arr):
        self.local, self.me_chip, self.c_arr = local, me_chip, c_arr
        self.parts = self.arrived = None

    def gather_send(self):
        return _ride_gather_send(self.local)

    def gather_pass(self, sent):
        return _ride_gather_pass(sent)

    def weights(self, gathered):
        return _prep_weights(_unpack_rest(gathered, self.local, self.me_chip))

    def pair_send(self, g):
        self.gbufs = _pack_grads_rest(_unprep_grads_early(g))
        return _ride_pair_exchange(self.gbufs)

    def reduce_send(self, got):
        self.parts = [_pair_add(g_, r_, self.c_arr, tr, "pair_add_%d" % (i + 1))
                      for i, (g_, r_, tr) in enumerate(zip(self.gbufs, got, _ADD_TILES[1:]))]
        return _ride_chip_exchange(self.parts)

    def reduce_arrived(self, arrived):
        self.arrived = list(arrived)

    def in_send(self, g):
        g_in = [_shard_columns(_w_in_grad(g), W_IN_SHARD)]
        got = _reduce_pair_exchange(g_in, "reduce_pair_exchange_w_in")
        self.part_in = _pair_add(g_in[0], got[0], self.c_arr, _ADD_TILES[0], "pair_add_0")
        return _ride_chip_exchange([self.part_in])

    def in_arrived(self, arrived):
        self.arrived_in = list(arrived)

    def small_send(self, g):
        self.small = _pack_small(_unprep_grads_late(g), g['loss_lanes'])
        return _ride_small_gather(self.small)

    def small_arrived(self, arrived):
        self.small_gathered = arrived[0]


def kernel(x, p, positions, w_in, conv_w, dn_a_log, dn_dt_bias, dn_norm_w, q_norm_w, w_uq, kv_norm_w, w_uk, w_uv, w_br_dn, w_br_mla, w_o, ln1_g, ln1_b, w_ffn_in, w_ffn_out, w_ple, w_ple_gate, ln2_g, ln2_b, loss_target, m_w_in, m_conv_w, m_dn_a_log, m_dn_dt_bias, m_dn_norm_w, m_q_norm_w, m_w_uq, m_kv_norm_w, m_w_uk, m_w_uv, m_w_br_dn, m_w_br_mla, m_w_o, m_ln1_g, m_ln1_b, m_w_ffn_in, m_w_ffn_out, m_w_ple, m_w_ple_gate, m_ln2_g, m_ln2_b, v_w_in, v_conv_w, v_dn_a_log, v_dn_dt_bias, v_dn_norm_w, v_q_norm_w, v_w_uq, v_kv_norm_w, v_w_uk, v_w_uv, v_w_br_dn, v_w_br_mla, v_w_o, v_ln1_g, v_ln1_b, v_w_ffn_in, v_w_ffn_out, v_w_ple, v_w_ple_gate, v_ln2_g, v_ln2_b):
    ws = dict(w_in=w_in, conv_w=conv_w, dn_a_log=dn_a_log, dn_dt_bias=dn_dt_bias, dn_norm_w=dn_norm_w, q_norm_w=q_norm_w,
              w_uq=w_uq, kv_norm_w=kv_norm_w, w_uk=w_uk, w_uv=w_uv, w_br_dn=w_br_dn, w_br_mla=w_br_mla, w_o=w_o,
              ln1_g=ln1_g, ln1_b=ln1_b, w_ffn_in=w_ffn_in, w_ffn_out=w_ffn_out, w_ple=w_ple, w_ple_gate=w_ple_gate,
              ln2_g=ln2_g, ln2_b=ln2_b)
    ms = dict(w_in=m_w_in, conv_w=m_conv_w, dn_a_log=m_dn_a_log, dn_dt_bias=m_dn_dt_bias, dn_norm_w=m_dn_norm_w,
              q_norm_w=m_q_norm_w, w_uq=m_w_uq, kv_norm_w=m_kv_norm_w, w_uk=m_w_uk, w_uv=m_w_uv, w_br_dn=m_w_br_dn,
              w_br_mla=m_w_br_mla, w_o=m_w_o, ln1_g=m_ln1_g, ln1_b=m_ln1_b, w_ffn_in=m_w_ffn_in, w_ffn_out=m_w_ffn_out,
              w_ple=m_w_ple, w_ple_gate=m_w_ple_gate, ln2_g=m_ln2_g, ln2_b=m_ln2_b)
    vs = dict(w_in=v_w_in, conv_w=v_conv_w, dn_a_log=v_dn_a_log, dn_dt_bias=v_dn_dt_bias, dn_norm_w=v_dn_norm_w,
              q_norm_w=v_q_norm_w, w_uq=v_w_uq, kv_norm_w=v_kv_norm_w, w_uk=v_w_uk, w_uv=v_w_uv, w_br_dn=v_w_br_dn,
              w_br_mla=v_w_br_mla, w_o=v_w_o, ln1_g=v_ln1_g, ln1_b=v_ln1_b, w_ffn_in=v_w_ffn_in, w_ffn_out=v_w_ffn_out,
              w_ple=v_w_ple, w_ple_gate=v_w_ple_gate, ln2_g=v_ln2_g, ln2_b=v_ln2_b)
    mx, my, mc = lax.axis_index("x"), lax.axis_index("y"), lax.axis_index("c")

    me_chip = 2 * mx + my
    c_arr = jnp.reshape(mc, (1,)).astype(jnp.int32)
    me_arr = jnp.reshape(me_chip, (1,)).astype(jnp.int32)
    sharded = ('w_in', 'w_ffn_in') + tuple(name for name, _, _ in _FLATB_PIECES)

    local = _pack_shards({name: ws[name][0] for name in sharded}, conv_w[0])
    (gathered_in,) = _gather_shards(local[:1], "gather_w_in")
    w_in_full, conv_full = _unpack_w_in(gathered_in, local[0], me_chip)
    small = {n: ws[n][0] for n in _SMALL_NAMES}
    small['conv_w'] = conv_full
    sp = _prep_small(small)
    cosb, sinb = _rope_tables(positions[0])
    exch = _Exchange(local[1:], me_chip, c_arr)

    loss_lanes, dx, g = _local_step(x[0], p[0, 0], cosb, sinb, loss_target[0], _prep_w_in(w_in_full), sp, exch)

    parts = [exch.part_in] + exch.parts
    arrived = exch.arrived_in + exch.arrived
    mine = [_chip_add(p_, r_, me_arr, tr, "chip_add_%d" % i) for i, (p_, r_, tr) in enumerate(zip(parts, arrived, _ADD_TILES))]
    reduced = _unpack_reduced(mine, _reduce_pair_share(mine, "reduce_pair_share"), mc)
    tot = _small_sum(exch.small_gathered, exch.small, jnp.reshape(4 * mx + 2 * my + mc, (1,)).astype(jnp.int32))
    loss = jnp.sum(tot[_LOSS_SMALL_ROW, :LANE])
    gred = {name: reduced[name][None] for name in sharded}
    for i, n in enumerate(_SMALL_NAMES):
        gred[n] = tot[i * _SMALL_GROUP, :ws[n].shape[1]][None]
    conv_tot = tot[_CONV_SMALL_ROW:_CONV_SMALL_ROW + _CONV_SMALL_ROWS].reshape(DN_CONV, QKV_W)
    gred['conv_w'] = lax.dynamic_slice_in_dim(conv_tot, (2 * mx + my) * _CONV_SHARD, _CONV_SHARD, axis=1)[None]

    deltas, new_m, new_v = {}, {}, {}
    for n in _WEIGHT_NAMES:
        if n == 'w_in':
            tr_ = lambda a: jnp.transpose(a, (0, 2, 1))
            g_t = tr_(gred[n].reshape(ws[n].shape))
            outs = _adamw(tr_(ws[n]), g_t, tr_(ms[n]), tr_(vs[n]), "adamw_" + n)
            gred[n] = tr_(g_t)
            deltas[n], new_m[n], new_v[n] = (tr_(o) for o in outs)
            continue
        gred[n] = gred[n].reshape(ws[n].shape)
        deltas[n], new_m[n], new_v[n] = _adamw(ws[n], gred[n], ms[n], vs[n], "adamw_" + n)
    return (loss, dx[None], *[gred[n] for n in _WEIGHT_NAMES], *[deltas[n] for n in _WEIGHT_NAMES],
            *[new_m[n] for n in _WEIGHT_NAMES], *[new_v[n] for n in _WEIGHT_NAMES])
```

```python
import functools

import jax
import jax.numpy as jnp
from jax import lax
from jax.experimental import pallas as pl
from jax.experimental.pallas import tpu as pltpu

F32 = jnp.float32
_CDT = jnp.bfloat16
_MESH = pl.DeviceIdType.MESH

D_MODEL = 1024
PLE_DIM = 256
HEADS = 8
DN_DK = 128
DN_CHUNK = 64
DN_CONV = 4
QKV_W = 3 * HEADS * DN_DK
Q_LORA = 384
KV_LORA = 256
NOPE = 128
ROPE = 64
ROPE_PAD = 128
FFN_HIDDEN = 2816
D_IN = 6864
ROPE_BASE = 10000.0
ALPHA = 2.0 ** 0.25
ATT_SCALE = (NOPE + ROPE) ** -0.5
NEG_BIG = -1e30
ADAM_LR, ADAM_B1, ADAM_B2, ADAM_EPS, ADAM_WD, ADAM_STEP = 0.001, 0.9, 0.999, 1e-08, 0.01, 10

LANE = 128
VMEM_LIMIT = 56 * 1024 * 1024
MM_VMEM_BUDGET = 40 * 1024 * 1024
N_SHARD = 4
FLAT_W = 1024
SMALL_ROWS = 96


def _tile(dim, cap):
    if dim <= cap:
        return dim
    t = (cap // LANE) * LANE
    while t >= LANE:
        if dim % t == 0:
            return t
        t -= LANE
    return dim


def _cparams(sem):
    return pltpu.CompilerParams(dimension_semantics=sem, vmem_limit_bytes=VMEM_LIMIT)


def _mm(a, b, *, name, ta=False, tb=False, add=None, add_scale=1.0, out_dtype=F32, heads=None,
        a_head=None, b_head=None, out_head=None, dims=None, tm=1408, tn=1408, rider=None):
    m, n, k = dims
    tm, tn = _tile(m, tm), _tile(n, tn)
    if heads is None and (m // tm) * (n // tn) < 4:
        sm, sn = _tile(m, 512), _tile(n, 512)
        if sm == min(m, 512) and sn == min(n, 512):
            tm, tn = sm, sn
    sa, sb, so = a.dtype.itemsize, b.dtype.itemsize, jnp.dtype(out_dtype).itemsize

    def vmem_need(tk_):
        acc = tm * tn * 4 if tk_ < k else 0
        extra = 2 * tm * tn * 4 if add is not None else 0
        return 2 * (tm * tk_ * sa + tk_ * tn * sb) + 2 * tm * tn * so + acc + extra

    tk = k
    while vmem_need(tk) > MM_VMEM_BUDGET and tk > LANE:
        smaller = _tile(k, tk - LANE)
        if smaller >= tk:
            break
        tk = smaller
    nk = k // tk
    hgrid = () if heads is None else (heads,)
    off = len(hgrid)

    def spec(rows, cols, rtile, ctile, rsel, csel, layout):
        def idx(*g):
            h = g[0] if off else 0
            ri, ci = g[off + rsel], g[off + csel]
            if layout == 'lead':
                return (h, ri, ci)
            if layout == 'col':
                return (ri, h * (cols // ctile) + ci)
            return (ri, ci)
        if layout == 'lead':
            return pl.BlockSpec((None, rtile, ctile), idx)
        return pl.BlockSpec((rtile, ctile), idx)

    a_spec = spec(k, m, tk, tm, 2, 0, a_head) if ta else spec(m, k, tm, tk, 0, 2, a_head)
    b_spec = spec(n, k, tn, tk, 1, 2, b_head) if tb else spec(k, n, tk, tn, 2, 1, b_head)
    o_spec = spec(m, n, tm, tn, 0, 1, out_head)
    in_specs = [a_spec, b_spec]
    args = [a, b]
    if add is not None:
        in_specs.append(spec(m, n, tm, tn, 0, 1, out_head))
        args.append(add)
    dn = (((0 if ta else 1,), (1 if tb else 0,)), ((), ()))

    def body(*refs):
        a_ref, b_ref = refs[0], refs[1]
        prod = lax.dot_general(a_ref[...].astype(_CDT), b_ref[...].astype(_CDT), dn, preferred_element_type=F32)
        if nk == 1:
            o_ref = refs[-1]
            if add is not None:
                prod = prod + refs[2][...].astype(F32) * add_scale
            o_ref[...] = prod.astype(out_dtype)
            return
        o_ref, acc_ref = refs[-2], refs[-1]
        kk = pl.program_id(off + 2)

        @pl.when(kk == 0)
        def _():
            if add is not None:
                acc_ref[...] = refs[2][...].astype(F32) * add_scale
            else:
                acc_ref[...] = jnp.zeros_like(acc_ref)

        acc_ref[...] += prod

        @pl.when(kk == nk - 1)
        def _():
            o_ref[...] = acc_ref[...].astype(out_dtype)

    if out_head == 'lead':
        oshape = (heads, m, n)
    elif out_head == 'col':
        oshape = (m, heads * n)
    else:
        oshape = (m, n)
    grid = hgrid + (m // tm, n // tn, nk)
    scratch = [pltpu.VMEM((tm, tn), F32)] if nk > 1 else []
    if rider is not None:
        (out,), carried = _carried_call(
            body, rider, *_grid_ends(grid), name=name, grid=grid, in_specs=in_specs, out_specs=[o_spec],
            out_shape=[jax.ShapeDtypeStruct(oshape, out_dtype)], scratch_shapes=scratch,
            sem=("arbitrary",) * len(grid), args=tuple(args))
        return out, carried
    sem = ("parallel",) * (off + 2) + ("arbitrary",)
    return pl.pallas_call(
        body, name=name, grid=grid, in_specs=in_specs, out_specs=o_spec,
        out_shape=jax.ShapeDtypeStruct(oshape, out_dtype), scratch_shapes=scratch,
        compiler_params=_cparams(sem))(*args)


def _mm2(a, b, **kw):
    ta, tb = kw.get('ta', False), kw.get('tb', False)
    m = a.shape[1] if ta else a.shape[0]
    k = a.shape[0] if ta else a.shape[1]
    n = b.shape[0] if tb else b.shape[1]
    return _mm(a, b, dims=(m, n, k), **kw)


def _rowwise(fn, rows, bcast, outs, reds=(), *, name, tm=256, heads=None):
    t = rows[0][0].shape[0]
    tm = min(tm, t)
    hn = 1 if heads is None else heads
    in_specs, args = [], []
    for arr, width, base, per_head in rows:
        in_specs.append(pl.BlockSpec((tm, width), functools.partial(
            lambda i, h, base, per_head: (i, base + (h if per_head else 0)), base=base, per_head=per_head)))
        args.append(arr)
    for arr in bcast:
        in_specs.append(pl.BlockSpec(arr.shape, lambda i, h: (0, 0)))
        args.append(arr)
    out_specs, out_shape = [], []
    for total, width, per_head, dt in outs:
        out_specs.append(pl.BlockSpec((tm, width), functools.partial(
            lambda i, h, per_head: (i, h if per_head else 0), per_head=per_head)))
        out_shape.append(jax.ShapeDtypeStruct((t, total), dt))
    for shp in reds:
        out_specs.append(pl.BlockSpec(shp, lambda i, h: (0, 0)))
        out_shape.append(jax.ShapeDtypeStruct(shp, F32))
    n_in, n_out, n_red = len(args), len(outs), len(reds)

    def body(*refs):
        i, h = pl.program_id(0), pl.program_id(1)
        vals = fn(h, *[r[...] for r in refs[:n_in]])
        for r, v in zip(refs[n_in:n_in + n_out], vals[:n_out]):
            r[...] = v.astype(r.dtype)
        if n_red:
            @pl.when((i == 0) & (h == 0))
            def _():
                for r in refs[n_in + n_out:]:
                    r[...] = jnp.zeros_like(r)
            for r, v in zip(refs[n_in + n_out:], vals[n_out:]):
                r[...] += v

    sem = ("arbitrary", "arbitrary") if n_red else ("parallel", "parallel")
    res = pl.pallas_call(body, name=name, grid=(t // tm, hn), in_specs=in_specs, out_specs=out_specs,
                         out_shape=out_shape, compiler_params=_cparams(sem))(*args)
    return tuple(res)


def _sigmoid(x):
    return 1.0 / (1.0 + jnp.exp(-x))


def _silu(x):
    return x * _sigmoid(x)


def _softplus(x):
    return jnp.maximum(x, 0.0) + jnp.log(1.0 + jnp.exp(-jnp.abs(x)))


def _layer_norm(t, g, b):
    mu = jnp.mean(t, axis=-1, keepdims=True)
    d = t - mu
    var = jnp.mean(d * d, axis=-1, keepdims=True)
    return d * lax.rsqrt(var + 1e-5) * g + b


def _rms_norm(t, w):
    return t * lax.rsqrt(jnp.mean(t * t, axis=-1, keepdims=True) + 1e-6) * w


def _swap_rope_halves(t):
    lane = lax.broadcasted_iota(jnp.int32, t.shape, 1) % ROPE_PAD
    n = t.shape[1]
    up = pltpu.roll(t, n - ROPE // 2, axis=1)
    dn = pltpu.roll(t, ROPE // 2, axis=1)
    return jnp.where(lane < ROPE // 2, up, jnp.where(lane < ROPE, dn, 0.0))


def _rope(t, cosb, sinb):
    reps = t.shape[1] // ROPE_PAD
    c = jnp.tile(cosb, (1, reps)) if reps > 1 else cosb
    s = jnp.tile(sinb, (1, reps)) if reps > 1 else sinb
    return t * c + _swap_rope_halves(t) * s


def _rope_bwd(d, cosb, sinb):
    reps = d.shape[1] // ROPE_PAD
    c = jnp.tile(cosb, (1, reps)) if reps > 1 else cosb
    s = jnp.tile(sinb, (1, reps)) if reps > 1 else sinb
    return d * c + _swap_rope_halves(d * s)


_CONV_ROWS = 256
_CONV_COLS = 256


def _conv_window(ref, r0, lo, hi, t):
    parts = []
    start, stop = r0 - lo, r0 + _CONV_ROWS + hi
    if start < 0:
        parts.append(jnp.zeros((-start, ref.shape[1]), F32))
        start = 0
    tail = max(stop - t, 0)
    parts.append(ref[start:stop - tail, :].astype(F32))
    if tail:
        parts.append(jnp.zeros((tail, ref.shape[1]), F32))
    return parts[0] if len(parts) == 1 else jnp.concatenate(parts, axis=0)


def _conv_taps(win, w_ref, n_out):
    acc = win[8:8 + n_out] * w_ref[DN_CONV - 1:DN_CONV, :]
    for i in range(DN_CONV - 1):
        acc = acc + pltpu.roll(win, DN_CONV - 1 - i, axis=0)[8:8 + n_out] * w_ref[i:i + 1, :]
    return acc


def _conv_silu(x, w):
    t, ch = x.shape

    def body(x_ref, w_ref, o_ref):
        for r in range(t // _CONV_ROWS):
            r0 = r * _CONV_ROWS
            c = _conv_taps(_conv_window(x_ref, r0, 8, 0, t), w_ref, _CONV_ROWS)
            o_ref[r0:r0 + _CONV_ROWS, :] = _silu(c)

    return pl.pallas_call(
        body, name="conv_silu", grid=(ch // _CONV_COLS,),
        in_specs=[pl.BlockSpec((t, _CONV_COLS), lambda j: (0, j)), pl.BlockSpec((DN_CONV, _CONV_COLS), lambda j: (0, j))],
        out_specs=pl.BlockSpec((t, _CONV_COLS), lambda j: (0, j)),
        out_shape=jax.ShapeDtypeStruct((t, ch), F32), compiler_params=_cparams(("parallel",)))(x, w)


def _conv_silu_bwd(x, w, dys):
    t, ch = x.shape
    per = ch // len(dys) // _CONV_COLS

    def body(x_ref, w_ref, *rest):
        dy_refs, (dx_ref, dw_ref) = rest[:len(dys)], rest[len(dys):]
        sec = pl.program_id(0) // per
        dws = [jnp.zeros((1, _CONV_COLS), F32) for _ in range(DN_CONV)]
        for r in range(t // _CONV_ROWS):
            r0 = r * _CONV_ROWS
            n_ext = _CONV_ROWS + 8
            xw = _conv_window(x_ref, r0, 8, 8, t)
            c = _conv_taps(xw, w_ref, n_ext)
            sg = _sigmoid(c)
            dy = _conv_window(dy_refs[-1], r0, 0, 8, t)
            for k in range(len(dys) - 2, -1, -1):
                dy = jnp.where(sec == k, _conv_window(dy_refs[k], r0, 0, 8, t), dy)
            ds = dy * (sg * (1.0 + c * (1.0 - sg)))
            x0 = xw[8:8 + _CONV_ROWS]
            dx = jnp.zeros((_CONV_ROWS, _CONV_COLS), F32)
            for i in range(DN_CONV):
                sh = DN_CONV - 1 - i
                ds_up = (ds if sh == 0 else pltpu.roll(ds, n_ext - sh, axis=0))[:_CONV_ROWS]
                dx = dx + ds_up * w_ref[i:i + 1, :]
                dws[i] = dws[i] + jnp.sum(x0 * ds_up, axis=0, keepdims=True)
            dx_ref[r0:r0 + _CONV_ROWS, :] = dx.astype(dx_ref.dtype)
        for i in range(DN_CONV):
            dw_ref[i:i + 1, :] = dws[i]

    blk = pl.BlockSpec((t, _CONV_COLS), lambda j: (0, j))
    wblk = pl.BlockSpec((DN_CONV, _CONV_COLS), lambda j: (0, j))
    dy_specs = [pl.BlockSpec((t, _CONV_COLS), functools.partial(lambda j, k: (0, jnp.clip(j - k * per, 0, per - 1)), k=k))
                for k in range(len(dys))]
    return pl.pallas_call(
        body, name="conv_silu_bwd", grid=(ch // _CONV_COLS,), in_specs=[blk, wblk] + dy_specs, out_specs=[blk, wblk],
        out_shape=[jax.ShapeDtypeStruct((t, ch), _CDT), jax.ShapeDtypeStruct((DN_CONV, ch), F32)],
        compiler_params=_cparams(("arbitrary",)))(x, w, *dys)


_PA_ROWS = 1024
_PA_ROWS_FWD = 1024


def _bmm(a, b, spec):
    return jnp.einsum(spec, a.astype(_CDT), b.astype(_CDT), preferred_element_type=F32)


def _split16(a):
    hi = a.astype(jnp.bfloat16)
    return hi, (a - hi.astype(F32)).astype(jnp.bfloat16)


def _bmm3(a, b, spec):
    ah, al = _split16(a)
    bh, bl = _split16(b)
    e = lambda p, q: jnp.einsum(spec, p, q, preferred_element_type=F32)
    return e(ah, bh) + (e(ah, bl) + e(al, bh))


def _split3(b):
    b0 = b.astype(jnp.bfloat16)
    r1 = b - b0.astype(F32)
    b1 = r1.astype(jnp.bfloat16)
    return b0, b1, (r1 - b1.astype(F32)).astype(jnp.bfloat16)


@functools.partial(jax.custom_vjp, nondiff_argnums=(2, 3))
def _select_mm(sel, b, spec, spec_t):
    return sum(jnp.einsum(spec, sel, t, preferred_element_type=F32) for t in _split3(b))


def _select_mm_fwd(sel, b, spec, spec_t):
    return _select_mm(sel, b, spec, spec_t), sel


def _select_mm_bwd(spec, spec_t, sel, ct):
    return jnp.zeros_like(sel), sum(jnp.einsum(spec_t, sel, t, preferred_element_type=F32) for t in _split3(ct))


_select_mm.defvjp(_select_mm_fwd, _select_mm_bwd)


def _tri_inverse(l_mat, eye):
    pw = -l_mat
    t_inv = eye + pw
    for _ in range(5):
        pw = _bmm3(pw, pw, 'bij,bjk->bik')
        t_inv = t_inv + _bmm3(t_inv, pw, 'bij,bjk->bik')
    return t_inv


@jax.custom_vjp
def _tri_inverse_saved(l_mat, t_saved):
    return t_saved


def _tri_inverse_saved_fwd(l_mat, t_saved):
    return t_saved, t_saved


def _tri_inverse_saved_bwd(t_saved, dt):
    left = _bmm3(t_saved, dt, 'bji,bjk->bik')
    return -_bmm3(left, t_saved, 'bij,bkj->bik'), jnp.zeros_like(t_saved)


_tri_inverse_saved.defvjp(_tri_inverse_saved_fwd, _tri_inverse_saved_bwd)


def _phase_a(h, q, k, v, ba, alog, dtb, t_saved=None):
    r = q.shape[0]
    nb = r // DN_CHUNK
    c = DN_CHUNK
    lane = lax.broadcasted_iota(jnp.int32, (1, LANE), 1)
    selb = (lane == h).astype(F32)
    sela = (lane == h + HEADS).astype(F32)
    b_raw = jnp.sum(ba * selb, axis=1, keepdims=True)
    a_raw = jnp.sum(ba * sela, axis=1, keepdims=True)
    al = jnp.sum(alog * selb, axis=1, keepdims=True)
    dt = jnp.sum(dtb * selb, axis=1, keepdims=True)
    beta = jnp.broadcast_to(_sigmoid(b_raw), (r, LANE))
    g = jnp.broadcast_to(-jnp.exp(al) * _softplus(a_raw + dt), (r, LANE))
    qn = q * lax.rsqrt(jnp.sum(q * q, -1, keepdims=True) + 1e-6) * (DN_DK ** -0.5)
    kn = k * lax.rsqrt(jnp.sum(k * k, -1, keepdims=True) + 1e-6)
    q3, k3, v3 = qn.reshape(nb, c, LANE), kn.reshape(nb, c, LANE), v.reshape(nb, c, LANE)
    b3, g3 = beta.reshape(nb, c, LANE), g.reshape(nb, c, LANE)
    ri = lax.broadcasted_iota(jnp.int32, (nb, c, c), 1)
    ci = lax.broadcasted_iota(jnp.int32, (nb, c, c), 2)
    tril, strict = ri >= ci, ri > ci
    gc = _select_mm(tril.astype(jnp.bfloat16), g3, 'bij,bjd->bid', 'bij,bid->bjd')
    onehot = (lax.broadcasted_iota(jnp.int32, (nb, c, LANE), 2) == 0).astype(jnp.bfloat16)
    g_row = _select_mm(onehot, gc, 'bid,bjd->bij', 'bid,bij->bjd')
    diff = gc[:, :, :c] - g_row
    decay = jnp.where(tril, jnp.exp(jnp.where(tril, diff, 0.0)), 0.0)
    kb = k3 * b3
    l_mat = jnp.where(strict, _bmm(kb, k3, 'bid,bjd->bij') * decay, 0.0)
    if t_saved is None:
        t_inv = _tri_inverse(l_mat, (ri == ci).astype(F32))
    else:
        t_inv = _tri_inverse_saved(l_mat, t_saved.reshape(nb, c, c))
    eg = jnp.exp(gc)
    u = _bmm(t_inv, v3 * b3, 'bij,bje->bie')
    w = _bmm(t_inv, kb * eg, 'bij,bje->bie')
    intra = jnp.where(tril, _bmm(q3, k3, 'bid,bjd->bij') * decay, 0.0)
    qd = q3 * eg
    gl = jnp.sum(g3, axis=1, keepdims=True)
    kt = k3 * jnp.exp(gl - gc)
    outs = (u.reshape(r, LANE), w.reshape(r, LANE), qd.reshape(r, LANE), kt.reshape(r, LANE),
            intra.reshape(r, c), gl.reshape(nb, LANE))
    if t_saved is not None:
        return outs
    qd2 = qd - _bmm(intra, w, 'bij,bjd->bid')
    au = _bmm(intra, u, 'bij,bje->bie')
    return outs + (t_inv.reshape(r, c), qd2.reshape(r, LANE), au.reshape(r, LANE))


def _pa_specs(t, rows):
    rr = min(rows, t)
    nb = rr // DN_CHUNK
    qkv = [pl.BlockSpec((rr, LANE), functools.partial(lambda i, h, o: (i, o + h), o=o)) for o in (0, HEADS, 2 * HEADS)]
    ba = pl.BlockSpec((rr, LANE), lambda i, h: (i, 3))
    vec = pl.BlockSpec((1, LANE), lambda i, h: (0, 0))
    row = pl.BlockSpec((rr, LANE), lambda i, h: (i, h))
    intra = pl.BlockSpec((None, rr, DN_CHUNK), lambda i, h: (h, i, 0))
    gl = pl.BlockSpec((nb, LANE), lambda i, h: (i, h))
    return rr, qkv, ba, vec, row, intra, gl


def _grid_ends(grid):
    first = lambda: functools.reduce(jnp.logical_and, [pl.program_id(a) == 0 for a in range(len(grid))])
    last = lambda: functools.reduce(jnp.logical_and, [pl.program_id(a) == n - 1 for a, n in enumerate(grid)])
    return first, last


def _delta_local(qkv_act, pm, alog, dtb, rider=None):
    t = qkv_act.shape[0]
    rr, qkv, ba, vec, row, intra, gl = _pa_specs(t, _PA_ROWS_FWD)

    def body(q, k, v, b, al, dt, *outs):
        vals = _phase_a(pl.program_id(1), q[...], k[...], v[...], b[...], al[...], dt[...])
        for o, val in zip(outs, vals):
            o[...] = val

    wide = jax.ShapeDtypeStruct((t, HEADS * LANE), F32)
    sq = jax.ShapeDtypeStruct((HEADS, t, DN_CHUNK), F32)
    grid = (t // rr, HEADS)
    return _carried_call(
        body, rider, *_grid_ends(grid), name="delta_local", grid=grid, in_specs=qkv + [ba, vec, vec],
        out_specs=[row] * 4 + [intra, gl, intra, row, row],
        out_shape=[wide] * 4 + [sq, jax.ShapeDtypeStruct((t // DN_CHUNK, HEADS * LANE), F32), sq, wide, wide],
        scratch_shapes=[], sem=("arbitrary", "arbitrary"), args=(qkv_act, qkv_act, qkv_act, pm, alog, dtb))


def _delta_local_bwd(qkv_act, pm, alog, dtb, t_inv, du, dw, dqd, dkt, dintra, dgl, rider=None):
    t = qkv_act.shape[0]
    rr, qkv, ba, vec, row, intra, gl = _pa_specs(t, _PA_ROWS)

    def body(q, k, v, b, al, dt, ti, du_r, dw_r, dqd_r, dkt_r, di_r, dgl_r, dq_o, dk_o, dv_o, dba_o, dal_o, ddt_o):
        i, h = pl.program_id(0), pl.program_id(1)
        t_saved = ti[...]
        _, vjp = jax.vjp(lambda *a: _phase_a(h, *a, t_saved=t_saved), q[...], k[...], v[...], b[...], al[...], dt[...])
        dq, dk, dv, dba, dal, ddt = vjp((du_r[...], dw_r[...], dqd_r[...], dkt_r[...], di_r[...], dgl_r[...]))
        dq_o[...], dk_o[...], dv_o[...] = dq, dk, dv

        @pl.when(h == 0)
        def _():
            dba_o[...] = jnp.zeros_like(dba_o)

        @pl.when((h == 0) & (i == 0))
        def _():
            dal_o[...] = jnp.zeros_like(dal_o)
            ddt_o[...] = jnp.zeros_like(ddt_o)

        dba_o[...] += dba
        dal_o[...] += dal
        ddt_o[...] += ddt

    wide = jax.ShapeDtypeStruct((t, HEADS * LANE), F32)
    vshape = jax.ShapeDtypeStruct((1, LANE), F32)
    grid = (t // rr, HEADS)
    return _carried_call(
        body, rider, *_grid_ends(grid), name="delta_local_bwd", grid=grid,
        in_specs=qkv + [ba, vec, vec, intra] + [row] * 4 + [intra, gl],
        out_specs=[row] * 3 + [pl.BlockSpec((rr, LANE), lambda i, h: (i, 0)), vec, vec],
        out_shape=[wide] * 3 + [jax.ShapeDtypeStruct((t, LANE), F32), vshape, vshape],
        scratch_shapes=[], sem=("arbitrary", "arbitrary"),
        args=(qkv_act, qkv_act, qkv_act, pm, alog, dtb, t_inv, du, dw, dqd, dkt, dintra, dgl))


_SCAN_ROWS = 512


def _dot(a, b, dn):
    return lax.dot_general(a.astype(_CDT), b.astype(_CDT), (dn, ((), ())), preferred_element_type=F32)


_NN = ((1,), (0,))
_NT = ((1,), (1,))
_TN = ((0,), (0,))


def _delta_scan(u, w, qd, kt, au, gl, rider=None):
    t = u.shape[0]
    rr = min(_SCAN_ROWS, t)
    nc = rr // DN_CHUNK

    def body(u_ref, w_ref, qd_ref, kt_ref, au_ref, gl_ref, o_ref, sall_ref, s_scr):
        @pl.when(pl.program_id(0) == 0)
        def _():
            s_scr[...] = jnp.zeros_like(s_scr)

        def chunk(c, carry):
            r0 = pl.multiple_of(c * DN_CHUNK, DN_CHUNK)
            rows = pl.ds(r0, DN_CHUNK)
            e = jnp.exp(gl_ref[pl.ds(c, 1), :])
            states = [s_scr[h] for h in range(HEADS)]
            u_c, w_c, qd_c, kt_c, au_c = u_ref[rows, :], w_ref[rows, :], qd_ref[rows, :], kt_ref[rows, :], au_ref[rows, :]
            o_new, s_new = [], []
            for h in range(HEADS):
                cs = slice(h * LANE, (h + 1) * LANE)
                s = states[h]
                both = _dot(jnp.concatenate([w_c[:, cs], qd_c[:, cs]], axis=0), s, _NN)
                v_new = u_c[:, cs] - both[:DN_CHUNK]
                o_new.append(both[DN_CHUNK:] + au_c[:, cs])
                s_new.append(s * e[:, cs] + _dot(kt_c[:, cs], v_new, _TN))
            o_ref[rows, :] = jnp.concatenate(o_new, axis=1)
            for h in range(HEADS):
                sall_ref[c, h] = states[h]
                s_scr[h] = s_new[h]
            return carry

        lax.fori_loop(0, nc, chunk, 0)

    row = pl.BlockSpec((rr, HEADS * LANE), lambda i: (i, 0))
    grid = (t // rr,)
    return _carried_call(
        body, rider, *_grid_ends(grid), name="delta_scan", grid=grid,
        in_specs=[row] * 5 + [pl.BlockSpec((nc, HEADS * LANE), lambda i: (i, 0))],
        out_specs=[row, pl.BlockSpec((nc, HEADS, LANE, LANE), lambda i: (i, 0, 0, 0))],
        out_shape=[jax.ShapeDtypeStruct((t, HEADS * LANE), F32),
                   jax.ShapeDtypeStruct((t // DN_CHUNK, HEADS, LANE, LANE), F32)],
        scratch_shapes=[pltpu.VMEM((HEADS, LANE, LANE), F32)], sem=("arbitrary",), args=(u, w, qd, kt, au, gl))


def _delta_scan_bwd(u, w, qd, kt, intra, gl, sall, do, rider=None):
    t = u.shape[0]
    rr = min(_SCAN_ROWS, t)
    nc = rr // DN_CHUNK
    ng = t // rr

    def body(u_ref, w_ref, qd_ref, kt_ref, a_ref, gl_ref, sall_ref, do_ref,
             du_ref, dw_ref, dqd_ref, dkt_ref, da_ref, dgl_ref, ds_scr):
        @pl.when(pl.program_id(0) == 0)
        def _():
            ds_scr[...] = jnp.zeros_like(ds_scr)

        def chunk(cc, carry):
            c = nc - 1 - cc
            r0 = pl.multiple_of(c * DN_CHUNK, DN_CHUNK)
            rows = pl.ds(r0, DN_CHUNK)
            e = jnp.exp(gl_ref[pl.ds(c, 1), :])
            states = [sall_ref[c, h] for h in range(HEADS)]
            ds_outs = [ds_scr[h] for h in range(HEADS)]
            u_a, w_a, kt_a, qd_a, do_a = u_ref[rows, :], w_ref[rows, :], kt_ref[rows, :], qd_ref[rows, :], do_ref[rows, :]
            a_a = [a_ref[h, rows, :] for h in range(HEADS)]
            da, dqd, dkt, du, dw, dgl, ds_new = [], [], [], [], [], [], []
            for h in range(HEADS):
                cs = slice(h * LANE, (h + 1) * LANE)
                s, ds_out = states[h], ds_outs[h]
                w_c, kt_c, qd_c, do_c = w_a[:, cs], kt_a[:, cs], qd_a[:, cs], do_a[:, cs]
                v_new = u_a[:, cs] - _dot(w_c, s, _NN)
                dv_new = _dot(a_a[h], do_c, _TN) + _dot(kt_c, ds_out, _NN)
                cots = jnp.concatenate([do_c, dv_new], axis=0)
                both = _dot(cots, s, _NT)
                dqd.append(both[:DN_CHUNK])
                dw.append(-both[DN_CHUNK:])
                da.append(_dot(do_c, v_new, _NT))
                dkt.append(_dot(v_new, ds_out, _NT))
                du.append(dv_new)
                eh = e[:, cs]
                dgl.append(jnp.broadcast_to(jnp.sum(ds_out * s, axis=0, keepdims=True) * eh, (8, LANE)))
                ds_new.append(ds_out * eh + _dot(jnp.concatenate([qd_c, -w_c], axis=0), cots, _TN))
            cat = lambda parts: jnp.concatenate(parts, axis=1)
            dqd_ref[rows, :], dkt_ref[rows, :], du_ref[rows, :], dw_ref[rows, :] = cat(dqd), cat(dkt), cat(du), cat(dw)
            dgl_ref[pl.ds(pl.multiple_of(c * 8, 8), 8), :] = cat(dgl)
            for h in range(HEADS):
                da_ref[h, rows, :] = da[h]
                ds_scr[h] = ds_new[h]
            return carry

        lax.fori_loop(0, nc, chunk, 0)

    rev = lambda i: (ng - 1 - i, 0)
    row = pl.BlockSpec((rr, HEADS * LANE), rev)
    a_spec = pl.BlockSpec((HEADS, rr, DN_CHUNK), lambda i: (0, ng - 1 - i, 0))
    gl_spec = pl.BlockSpec((nc, HEADS * LANE), rev)
    wide = jax.ShapeDtypeStruct((t, HEADS * LANE), F32)
    outs, carried = _carried_call(
        body, rider, *_grid_ends((ng,)), name="delta_scan_bwd", grid=(ng,),
        in_specs=[row] * 4 + [a_spec, gl_spec, pl.BlockSpec((nc, HEADS, LANE, LANE), lambda i: (ng - 1 - i, 0, 0, 0)), row],
        out_specs=[row] * 4 + [a_spec, pl.BlockSpec((nc * 8, HEADS * LANE), rev)],
        out_shape=[wide] * 4 + [jax.ShapeDtypeStruct((HEADS, t, DN_CHUNK), F32),
                                jax.ShapeDtypeStruct((t // DN_CHUNK * 8, HEADS * LANE), F32)],
        scratch_shapes=[pltpu.VMEM((HEADS, LANE, LANE), F32)], sem=("arbitrary",), args=(u, w, qd, kt, intra, gl, sall, do))
    return tuple(outs[:5]) + (outs[5].reshape(t // DN_CHUNK, 8, HEADS * LANE)[:, 0, :],), carried


_ATT_TILE = 512
_ATT_FWD_HEADS = 4
_ATT_BWD_HEADS = 4


def _kv_rows(j, tk):
    return pl.ds(pl.multiple_of(j * tk, tk), tk)


def _att_scores(ql, qr, ckv_ref, kr_ref, j, tk):
    ks = _kv_rows(j, tk)
    return (_dot(ql, ckv_ref[ks, :], _NT) + _dot(qr, kr_ref[ks, :], _NT)) * ATT_SCALE


def _diag_mask(s):
    qi = lax.broadcasted_iota(jnp.int32, s.shape, 0) % s.shape[1]
    ki = lax.broadcasted_iota(jnp.int32, s.shape, 1)
    return jnp.where(ki <= qi, s, NEG_BIG)


def _attention(qn, qr_pre, cosb, sinb, ckv, kr, wuk, wuv):
    t = ckv.shape[0]
    tq = min(_ATT_TILE, t)
    hp = _ATT_FWD_HEADS
    nl = tq // LANE

    def lane_fold(v, op):
        out = v[:, :LANE]
        for k in range(1, nl):
            out = op(out, v[:, k * LANE:(k + 1) * LANE])
        return out

    def body(qn_ref, qr_ref, cos_ref, sin_ref, ckv_ref, kr_ref, wuk_ref, wuv_ref, o_ref, lse_ref, qrope_ref, omla_ref,
             s_all, m_lanes, l_lanes, acc_scr):
        hg, qi = pl.program_id(0), pl.program_id(1)
        lanes = lambda ref, k: ref[:, k * LANE:(k + 1) * LANE]
        rows = lambda parts: jnp.concatenate(parts, axis=0)
        part = lambda v, k: v[k * tq:(k + 1) * tq]
        q_lat = rows([_dot(lanes(qn_ref, k), wuk_ref[hg * hp + k], _NT) for k in range(hp)]).astype(_CDT)
        roped = [_rope(lanes(qr_ref, k), cos_ref[...], sin_ref[...]).astype(qrope_ref.dtype) for k in range(hp)]
        qrope_ref[...] = jnp.concatenate(roped, axis=1)
        q_rope = rows(roped)
        m_lanes[...] = jnp.full_like(m_lanes, NEG_BIG)

        def scores(j, masked):
            s = _att_scores(q_lat, q_rope, ckv_ref, kr_ref, j, tq)
            if masked:
                s = _diag_mask(s)
            s_all[j] = s
            m_lanes[...] = jnp.maximum(m_lanes[...], lane_fold(s, jnp.maximum))

        def scores_body(j, carry):
            scores(j, False)
            return carry

        lax.fori_loop(0, qi, scores_body, 0)
        scores(qi, True)
        m = jnp.max(m_lanes[...], axis=-1, keepdims=True)
        mb = jnp.broadcast_to(m, (hp * tq, LANE))
        l_lanes[...] = jnp.zeros_like(l_lanes)
        acc_scr[...] = jnp.zeros_like(acc_scr)

        def weigh(j, carry):
            s = s_all[j]
            p = jnp.concatenate([jnp.exp(s[:, k * LANE:(k + 1) * LANE] - mb) for k in range(nl)], axis=1)
            l_lanes[...] += lane_fold(p, jnp.add)
            acc_scr[...] += _dot(p, ckv_ref[_kv_rows(j, tq), :], _NN)
            return carry

        lax.fori_loop(0, qi + 1, weigh, 0)
        l = jnp.sum(l_lanes[...], axis=-1, keepdims=True)
        out = acc_scr[...] / l
        lse_v = m + jnp.log(l)
        for k in range(hp):
            o_ref[k] = part(out, k)
            lse_ref[k] = part(lse_v, k)
        omla_ref[...] = jnp.concatenate(
            [_dot(part(out, k), wuv_ref[hg * hp + k], _NN) for k in range(hp)], axis=1).astype(omla_ref.dtype)

    col = pl.BlockSpec((tq, hp * LANE), lambda h, i: (i, h))
    table = pl.BlockSpec((tq, ROPE_PAD), lambda h, i: (i, 0))
    wspec = pl.BlockSpec((HEADS, KV_LORA, NOPE), lambda h, i: (0, 0, 0))
    return pl.pallas_call(
        body, name="attention", grid=(HEADS // hp, t // tq),
        in_specs=[col, col, table, table, pl.BlockSpec((t, KV_LORA), lambda h, i: (0, 0)),
                  pl.BlockSpec((t, ROPE_PAD), lambda h, i: (0, 0)), wspec, wspec],
        out_specs=[pl.BlockSpec((hp, tq, KV_LORA), lambda h, i: (h, i, 0)),
                   pl.BlockSpec((hp, tq, 1), lambda h, i: (h, i, 0)), col, col],
        out_shape=[jax.ShapeDtypeStruct((HEADS, t, KV_LORA), F32), jax.ShapeDtypeStruct((HEADS, t, 1), F32),
                   jax.ShapeDtypeStruct((t, HEADS * ROPE_PAD), _CDT), jax.ShapeDtypeStruct((t, HEADS * NOPE), _CDT)],
        scratch_shapes=[pltpu.VMEM((t // tq, hp * tq, tq), F32), pltpu.VMEM((hp * tq, LANE), F32),
                        pltpu.VMEM((hp * tq, LANE), F32), pltpu.VMEM((hp * tq, KV_LORA), F32)],
        compiler_params=_cparams(("parallel", "parallel")))(qn, qr_pre, cosb, sinb, ckv, kr, wuk, wuv)


def _attention_bwd(qn, qr, cosb, sinb, ckv, kr, wuk, wuv, out, lse, do_mla):
    t = ckv.shape[0]
    tq = min(_ATT_TILE, t)
    hp = _ATT_BWD_HEADS

    def body(qn_ref, qr_ref, cos_ref, sin_ref, ckv_ref, kr_ref, wuk_ref, wuv_ref, o_ref, lse_ref, do_ref,
             dql_ref, dqr_ref, dqn_ref, dckv_ref, dkr_ref, dql_scr, dqr_scr):
        hg, qi = pl.program_id(0), pl.program_id(1)

        @pl.when((hg == 0) & (qi == 0))
        def _():
            dckv_ref[...] = jnp.zeros_like(dckv_ref)
            dkr_ref[...] = jnp.zeros_like(dkr_ref)

        lanes = lambda ref, k: ref[:, k * LANE:(k + 1) * LANE]
        rows = lambda parts: jnp.concatenate(parts, axis=0)
        q_lat = rows([_dot(lanes(qn_ref, k), wuk_ref[hg * hp + k], _NT) for k in range(hp)]).astype(_CDT)
        q_rope = rows([lanes(qr_ref, k) for k in range(hp)])
        d_out = rows([_dot(lanes(do_ref, k), wuv_ref[hg * hp + k], _NT) for k in range(hp)])
        d_o = d_out.astype(_CDT)
        lse_v = rows([lse_ref[k] for k in range(hp)])
        dsum = jnp.sum(d_out * rows([o_ref[k] for k in range(hp)]), axis=-1, keepdims=True)
        dql_scr[...] = jnp.zeros_like(dql_scr)
        dqr_scr[...] = jnp.zeros_like(dqr_scr)

        def step(j, masked):
            ks = _kv_rows(j, tq)
            s = _att_scores(q_lat, q_rope, ckv_ref, kr_ref, j, tq)
            if masked:
                s = _diag_mask(s)
            p = jnp.exp(s - lse_v)
            kv = ckv_ref[ks, :]
            ds = (p * (_dot(d_o, kv, _NT) - dsum) * ATT_SCALE).astype(_CDT)
            pb = p.astype(_CDT)
            dql_scr[...] += _dot(ds, kv, _NN)
            dqr_scr[...] += _dot(ds, kr_ref[ks, :], _NN)
            dckv_ref[ks, :] += _dot(pb, d_o, _TN) + _dot(ds, q_lat, _TN)
            dkr_ref[ks, :] += _dot(ds, q_rope, _TN)

        def loop_body(j, carry):
            step(j, False)
            return carry

        lax.fori_loop(0, qi, loop_body, 0)
        step(qi, True)
        cols = lambda parts: jnp.concatenate(parts, axis=1)
        part = lambda v, k: v[k * tq:(k + 1) * tq]
        dql = dql_scr[...].astype(dql_ref.dtype)
        dqr = dqr_scr[...]
        for k in range(hp):
            dql_ref[k] = part(dql, k)
        dqn_ref[...] = cols([_dot(part(dql, k), wuk_ref[hg * hp + k], _NN) for k in range(hp)]).astype(dqn_ref.dtype)
        dqr_ref[...] = cols([_rope_bwd(part(dqr, k), cos_ref[...], sin_ref[...]) for k in range(hp)]).astype(dqr_ref.dtype)

    lat = pl.BlockSpec((hp, tq, KV_LORA), lambda h, i: (h, i, 0))
    col = pl.BlockSpec((tq, hp * LANE), lambda h, i: (i, h))
    table = pl.BlockSpec((tq, ROPE_PAD), lambda h, i: (i, 0))
    kfull = pl.BlockSpec((t, KV_LORA), lambda h, i: (0, 0))
    rfull = pl.BlockSpec((t, ROPE_PAD), lambda h, i: (0, 0))
    wspec = pl.BlockSpec((HEADS, KV_LORA, NOPE), lambda h, i: (0, 0, 0))
    wide = jax.ShapeDtypeStruct((t, HEADS * LANE), _CDT)
    return pl.pallas_call(
        body, name="attention_bwd", grid=(HEADS // hp, t // tq),
        in_specs=[col, col, table, table, kfull, rfull, wspec, wspec, lat,
                  pl.BlockSpec((hp, tq, 1), lambda h, i: (h, i, 0)), col],
        out_specs=[lat, col, col, kfull, rfull],
        out_shape=[jax.ShapeDtypeStruct((HEADS, t, KV_LORA), _CDT), wide, wide,
                   jax.ShapeDtypeStruct((t, KV_LORA), F32), jax.ShapeDtypeStruct((t, ROPE_PAD), F32)],
        scratch_shapes=[pltpu.VMEM((hp * tq, KV_LORA), F32), pltpu.VMEM((hp * tq, ROPE_PAD), F32)],
        compiler_params=_cparams(("arbitrary", "arbitrary")))(qn, qr, cosb, sinb, ckv, kr, wuk, wuv, out, lse, do_mla)


_DX_ROWS = 256


def _dx_fused(pairs, add, add_scale, rider=None, name="b_dx"):
    t, d = add.shape
    tm = min(_DX_ROWS, t)
    n = len(pairs)

    def body(*refs):
        acc = refs[2 * n][...] * add_scale
        for i in range(n):
            acc = acc + _dot(refs[i][...], refs[n + i][...], _NT)
        refs[2 * n + 1][...] = acc

    in_specs = [pl.BlockSpec((tm, a.shape[1]), lambda i: (i, 0)) for a, _ in pairs]
    in_specs += [pl.BlockSpec(w.shape, lambda i: (0, 0)) for _, w in pairs]
    row = pl.BlockSpec((tm, d), lambda i: (i, 0))
    grid = (t // tm,)
    (dx,), carried = _carried_call(
        body, rider, *_grid_ends(grid), name=name, grid=grid, in_specs=in_specs + [row], out_specs=[row],
        out_shape=[jax.ShapeDtypeStruct((t, d), F32)], scratch_shapes=[], sem=("arbitrary",),
        args=tuple(a for a, _ in pairs) + tuple(w for _, w in pairs) + (add,))
    return dx, carried


def _ffn_in_swiglu(a, w):
    t, k = a.shape
    hid = w.shape[1] // 2
    tm, tn = _tile(t, 1024), _tile(hid, 1408)
    nj = hid // tn

    def body(a_ref, bg_ref, bu_ref, act_ref, gt_ref, up_ref):
        av = a_ref[...].astype(_CDT)
        gt = jnp.dot(av, bg_ref[...].astype(_CDT), preferred_element_type=F32)
        up = jnp.dot(av, bu_ref[...].astype(_CDT), preferred_element_type=F32)
        act_ref[...] = _swiglu(gt, up).astype(act_ref.dtype)
        gt_ref[...] = gt.astype(gt_ref.dtype)
        up_ref[...] = up.astype(up_ref.dtype)

    out = pl.BlockSpec((tm, tn), lambda i, j: (i, j))
    return pl.pallas_call(
        body, name="f_ffn_in_swiglu", grid=(t // tm, nj),
        in_specs=[pl.BlockSpec((tm, k), lambda i, j: (i, 0)), pl.BlockSpec((k, tn), lambda i, j: (0, j)),
                  pl.BlockSpec((k, tn), lambda i, j: (0, nj + j))],
        out_specs=[out] * 3, out_shape=[jax.ShapeDtypeStruct((t, hid), _CDT)] * 3,
        compiler_params=_cparams(("parallel", "parallel")))(a, w, w)


def _gated_norm(o, z, w):
    return _rms_norm(o, w) * _silu(z)


def _gated_norm_heads(o, z, w):
    heads = [_gated_norm(o[:, h * LANE:(h + 1) * LANE], z[:, h * LANE:(h + 1) * LANE], w) for h in range(HEADS)]
    return jnp.concatenate(heads, axis=1)


def _mla_pre(ckv, krp, cq, cosb, sinb, qw, kw):
    return _rms_norm(cq, qw), _rms_norm(ckv, kw), _rope(krp, cosb, sinb)


def _merge(gg, y_dn, y_mla):
    return _sigmoid(gg[:, :D_MODEL]) * y_dn + _sigmoid(gg[:, D_MODEL:]) * y_mla


def _ln1(xv, attn_out, g, b):
    return _layer_norm(ALPHA * xv + attn_out, g, b)


def _final(h1, ffn, gate_pre, ple_proj, g, b):
    return _layer_norm(ALPHA * h1 + ffn + _sigmoid(gate_pre) * ple_proj, g, b)


def _swiglu(gt, up):
    return _silu(gt) * up


def _local_step(x, p, cosb, sinb, target, wt, sp, exch):
    t = x.shape[0]
    bf = _CDT
    xb = x.astype(bf)
    g = {}

    qkv_pre = _mm2(xb, wt['qkv'], name="f_qkv")
    z = _mm2(xb, wt['z'], name="f_z")
    gg = _mm2(xb, wt['gg'], name="f_gg")
    pm = _mm2(xb, wt['mla'], name="f_mla")
    qkv_act = _conv_silu(qkv_pre, sp['conv_w'])
    (u, w_, qd, kt, intra, gl, t_inv, qd2, au), sent = _delta_local(qkv_act, pm, sp['a_log'], sp['dt_bias'],
                                                                    rider=exch.gather_send())
    (o_dn, sall), passed = _delta_scan(u, w_, qd2, kt, au, gl, rider=exch.gather_pass(sent))
    wt = dict(wt, **exch.weights(passed))
    def gated_norm_br(h, o, zz, w, wbr):
        og_v = _gated_norm_heads(o, zz, w).astype(bf)
        return og_v, jnp.dot(og_v, wbr.astype(bf), preferred_element_type=F32)

    og, y_dn = _rowwise(gated_norm_br, [(o_dn, D_MODEL, 0, False), (z, D_MODEL, 0, False)],
                        [sp['dn_norm_w'], wt['br_dn']],
                        [(D_MODEL, D_MODEL, False, bf), (D_MODEL, D_MODEL, False, F32)], name="f_gated_norm_br", tm=512)

    def mla_pre_uq(h, ckv, krp, cq, cosv, sinv, qw, kw, wn, wr):
        c_q_v, c_kv_v, k_rope_v = _mla_pre(ckv, krp, cq, cosv, sinv, qw, kw)
        c_q_b = c_q_v.astype(bf)
        return (c_q_b, c_kv_v, k_rope_v, jnp.dot(c_q_b, wn.astype(bf), preferred_element_type=F32),
                jnp.dot(c_q_b, wr.astype(bf), preferred_element_type=F32))

    c_q, c_kv, k_rope, q_nope, q_rope_pre = _rowwise(
        mla_pre_uq,
        [(pm, KV_LORA, 0, False), (pm, ROPE_PAD, 2, False), (pm, Q_LORA, 2, False),
         (cosb, ROPE_PAD, 0, False), (sinb, ROPE_PAD, 0, False)],
        [sp['q_norm_w'], sp['kv_norm_w'], wt['uq_nope'], wt['uq_rope']],
        [(Q_LORA, Q_LORA, False, bf), (KV_LORA, KV_LORA, False, bf), (ROPE_PAD, ROPE_PAD, False, bf),
         (HEADS * NOPE, HEADS * NOPE, False, bf), (HEADS * ROPE_PAD, HEADS * ROPE_PAD, False, F32)], name="f_mla_pre_uq",
        tm=512)
    out_lat, lse, q_rope, o_mla = _attention(q_nope, q_rope_pre, cosb, sinb, c_kv, k_rope, wt['uk'], wt['uv'])
    y_mla = _mm2(o_mla, wt['br_mla'], name="f_br_mla")

    def merge_o_ln1(h, ggv, yd, ym, xv, wo, gv, bv):
        mixed_v = _merge(ggv, yd, ym).astype(bf)
        ao = jnp.dot(mixed_v, wo.astype(bf), preferred_element_type=F32)
        h1v = _ln1(xv, ao, gv, bv)
        return mixed_v, ao, h1v, h1v

    mixed, attn_out, h1, h1b = _rowwise(
        merge_o_ln1, [(gg, 2 * D_MODEL, 0, False), (y_dn, D_MODEL, 0, False), (y_mla, D_MODEL, 0, False), (x, D_MODEL, 0, False)],
        [wt['o'], sp['ln1_g'], sp['ln1_b']],
        [(D_MODEL, D_MODEL, False, bf), (D_MODEL, D_MODEL, False, F32), (D_MODEL, D_MODEL, False, F32),
         (D_MODEL, D_MODEL, False, bf)], name="f_merge_o_ln1", tm=512)
    act, ffn_gt, ffn_up = _ffn_in_swiglu(h1b, wt['ffn_in'])
    pb = p.astype(bf)

    def final_fn(h, h1v, actv, pv, tgt, gt, up, wfo, wpg, wpl, gv, bv):
        ffnv = jnp.dot(actv.astype(bf), wfo.astype(bf), preferred_element_type=F32)
        gpv = jnp.dot(h1v.astype(bf), wpg.astype(bf), preferred_element_type=F32)
        ppv = jnp.dot(pv.astype(bf), wpl.astype(bf), preferred_element_type=F32)
        y, vjp = jax.vjp(_final, h1v, ffnv, gpv, ppv, gv, bv)
        err = y - tgt
        dh1, dffn, dgp, dpp, dg, db = vjp(err * (1.0 / D_MODEL))
        sq = err * err
        lanes = sq[:, :LANE]
        for j in range(1, D_MODEL // LANE):
            lanes = lanes + sq[:, j * LANE:(j + 1) * LANE]
        loss = jnp.sum(lanes, axis=0, keepdims=True) * (0.5 / D_MODEL)
        dffn_b = dffn.astype(bf)
        dact = _dot(dffn_b, wfo, _NT).astype(bf).astype(F32)
        _, vjp_s = jax.vjp(_swiglu, gt.astype(F32), up.astype(F32))
        dgt, dup = vjp_s(dact)
        return dffn, dffn_b, dgp, dpp, jnp.concatenate([dgt, dup], axis=1), dg, db, loss

    dpre2, dpre2b, dgate_pre, dple_proj, dffn_in, g['ln2_g'], g['ln2_b'], loss_lanes = _rowwise(
        final_fn, [(h1, D_MODEL, 0, False), (act, FFN_HIDDEN, 0, False), (pb, PLE_DIM, 0, False), (target, D_MODEL, 0, False),
                   (ffn_gt, FFN_HIDDEN, 0, False), (ffn_up, FFN_HIDDEN, 0, False)],
        [wt['ffn_out'], wt['ple_gate'], wt['ple'], sp['ln2_g'], sp['ln2_b']],
        [(D_MODEL, D_MODEL, False, F32)] + [(D_MODEL, D_MODEL, False, bf)] * 3 + [(2 * FFN_HIDDEN, 2 * FFN_HIDDEN, False, bf)],
        [(1, D_MODEL), (1, D_MODEL), (1, LANE)], name="b_final")
    g['loss_lanes'] = loss_lanes

    g['ple'] = _mm2(pb, dple_proj, ta=True, name="g_ple")
    g['ple_gate'] = _mm2(h1b, dgate_pre, ta=True, name="g_ple_gate")
    g['ffn_out'] = _mm2(act, dpre2b, ta=True, name="g_ffn_out")
    g['ffn_in'] = _mm2(h1b, dffn_in, ta=True, name="g_ffn_in")
    dh1, _ = _dx_fused([(dffn_in, wt['ffn_in']), (dgate_pre, wt['ple_gate'])], dpre2, ALPHA, name="b_dh1")

    def attn_out_bwd(h, xv, ao, d, ggv, yd, ym, o, zz, wo, wbd, wbm, gv, bv, nw):
        _, vjp = jax.vjp(_ln1, xv, ao, gv, bv)
        _, dao, dg, db = vjp(d)
        dao_b = dao.astype(bf)
        _, vjp_m = jax.vjp(_merge, ggv, yd, ym)
        dggv, dyd, dym = vjp_m(_dot(dao_b, wo, _NT))
        dyd_b, dym_b = dyd.astype(bf), dym.astype(bf)
        _, vjp_n = jax.vjp(_gated_norm_heads, o, zz, nw)
        do_v, dz_v, dnw = vjp_n(_dot(dyd_b, wbd, _NT))
        return dao, dao_b, dggv, dyd_b, dym_b, do_v, dz_v, _dot(dym_b, wbm, _NT), dg, db, dnw

    row_d = lambda a: (a, D_MODEL, 0, False)
    dpre1, dpre1b, dgg, dy_dn, dy_mla, do_dn, dz, do_mla, g['ln1_g'], g['ln1_b'], g['dn_norm_w'] = _rowwise(
        attn_out_bwd, [row_d(x), row_d(attn_out), row_d(dh1), (gg, 2 * D_MODEL, 0, False), row_d(y_dn), row_d(y_mla),
                       row_d(o_dn), row_d(z)],
        [wt['o'], wt['br_dn'], wt['br_mla'], sp['ln1_g'], sp['ln1_b'], sp['dn_norm_w']],
        [(D_MODEL, D_MODEL, False, F32), (D_MODEL, D_MODEL, False, bf), (2 * D_MODEL, 2 * D_MODEL, False, bf),
         (D_MODEL, D_MODEL, False, bf), (D_MODEL, D_MODEL, False, bf), (D_MODEL, D_MODEL, False, F32),
         (D_MODEL, D_MODEL, False, bf), (D_MODEL, D_MODEL, False, bf)],
        [(1, D_MODEL), (1, D_MODEL), (1, LANE)], name="b_attn_out")
    g['o'] = _mm2(mixed, dpre1b, ta=True, name="g_o")
    g['br_dn'] = _mm2(og, dy_dn, ta=True, name="g_br_dn")
    g['br_mla'] = _mm2(o_mla, dy_mla, ta=True, name="g_br_mla")

    g['uv'] = _mm(out_lat, do_mla, name="g_uv", ta=True, heads=HEADS, a_head='lead', b_head='col', out_head='lead',
                  dims=(KV_LORA, NOPE, t))
    dq_lat, dq_rope_pre, dq_nope, dckv_att, dkr_att = _attention_bwd(
        q_nope, q_rope, cosb, sinb, c_kv, k_rope, wt['uk'], wt['uv'], out_lat, lse, do_mla)
    g['uk'] = _mm(dq_lat, q_nope, name="g_uk", ta=True, heads=HEADS, a_head='lead', b_head='col', out_head='lead',
                  dims=(KV_LORA, NOPE, t))
    g['uq_nope'] = _mm2(c_q, dq_nope, ta=True, name="g_uq_nope")
    g['uq_rope'] = _mm2(c_q, dq_rope_pre, ta=True, name="g_uq_rope")
    dc_q = _mm2(dq_nope, wt['uq_nope'], tb=True, name="b_dcq_nope")
    dc_q = _mm2(dq_rope_pre, wt['uq_rope'], tb=True, name="b_dcq_rope", add=dc_q)

    (du, dw, dqd, dkt, dintra, dgl), paired = _delta_scan_bwd(u, w_, qd, kt, intra, gl, sall, do_dn, rider=exch.pair_send(g))
    (dq_a, dk_a, dv_a, dba, g['a_log'], g['dt_bias']), arrived = _delta_local_bwd(
        qkv_act, pm, sp['a_log'], sp['dt_bias'], t_inv, du, dw, dqd, dkt, dintra, dgl, rider=exch.reduce_send(paired))
    exch.reduce_arrived(arrived)
    dqkv_pre, g['conv_w'] = _conv_silu_bwd(qkv_pre, sp['conv_w'], [dq_a, dk_a, dv_a])

    def mla_pre_bwd(h, ckv, cq, cosv, sinv, dcq, dckv, dkr, dba_v, qw, kw):
        _, vjp = jax.vjp(lambda a, c, d, e: (_rms_norm(c, d), _rms_norm(a, e)), ckv, cq, qw, kw)
        dckv_p, dcq_p, dqw, dkw = vjp((dcq, dckv))
        dkr_p = _rope_bwd(dkr, cosv, sinv)
        dpm = jnp.concatenate([dckv_p, dkr_p, dba_v, jnp.zeros((ckv.shape[0], 2 * LANE), F32), dcq_p], axis=1)
        return dpm, dqw, dkw

    dpm, g['q_norm_w'], g['kv_norm_w'] = _rowwise(
        mla_pre_bwd,
        [(pm, KV_LORA, 0, False), (pm, Q_LORA, 2, False), (cosb, ROPE_PAD, 0, False), (sinb, ROPE_PAD, 0, False),
         (dc_q, Q_LORA, 0, False), (dckv_att, KV_LORA, 0, False), (dkr_att, ROPE_PAD, 0, False), (dba, LANE, 0, False)],
        [sp['q_norm_w'], sp['kv_norm_w']], [(1152, 1152, False, bf)], [(1, Q_LORA), (1, KV_LORA)], name="b_mla_pre")

    rider = exch.small_send(g)
    if rider is None:
        g['qkv'] = _mm2(xb, dqkv_pre, ta=True, name="g_qkv")
    else:
        g['qkv'], got = _mm2(xb, dqkv_pre, ta=True, name="g_qkv", rider=rider)
        exch.small_arrived(got)
    g['z'] = _mm2(xb, dz, ta=True, name="g_z")
    g['gg'] = _mm2(xb, dgg, ta=True, name="g_gg")
    g['mla'] = _mm2(xb, dpm, ta=True, name="g_mla")
    dx, arrived = _dx_fused([(dqkv_pre, wt['qkv']), (dz, wt['z']), (dgg, wt['gg']), (dpm, wt['mla'])], dpre1, ALPHA,
                            rider=exch.in_send(g))
    exch.in_arrived(arrived)
    return loss_lanes, dx, g


_IN_SIZES = (QKV_W, HEADS * DN_DK, HEADS, HEADS, Q_LORA, KV_LORA, ROPE, D_MODEL, D_MODEL)


def _rope_tables(positions):
    inv_freq = ROPE_BASE ** (-jnp.arange(0, ROPE, 2, dtype=F32) / ROPE)
    ang = positions.astype(F32)[:, None] * inv_freq
    cos, sin = jnp.cos(ang), jnp.sin(ang)
    zeros = jnp.zeros((positions.shape[0], ROPE_PAD - ROPE), F32)
    return jnp.concatenate([cos, cos, zeros], axis=1), jnp.concatenate([-sin, sin, zeros], axis=1)


def _prep_w_in(w_in):
    dt = w_in.dtype
    offs = [0]
    for s in _IN_SIZES:
        offs.append(offs[-1] + s)
    qkv, z, wb, wa, cq, ckv, kr, gd, gm = [w_in[:, offs[i]:offs[i + 1]] for i in range(len(_IN_SIZES))]
    zc = lambda n: jnp.zeros((D_MODEL, n), dt)
    return {
        'qkv': qkv, 'z': z, 'gg': jnp.concatenate([gd, gm], axis=1),
        'mla': jnp.concatenate([ckv, kr, zc(ROPE_PAD - ROPE), wb, wa, zc(LANE - 2 * HEADS), zc(2 * LANE), cq], axis=1),
    }


def _prep_weights(full):
    w_uq = full['w_uq']
    wt = {
        'uq_nope': w_uq[:, :, :NOPE].reshape(Q_LORA, HEADS * NOPE),
        'uq_rope': jnp.pad(w_uq[:, :, NOPE:], ((0, 0), (0, 0), (0, ROPE_PAD - ROPE))).reshape(Q_LORA, HEADS * ROPE_PAD),
        'uk': jnp.transpose(full['w_uk'], (1, 0, 2)), 'uv': jnp.transpose(full['w_uv'], (1, 0, 2)),
        'br_dn': full['w_br_dn'], 'br_mla': full['w_br_mla'], 'o': full['w_o'], 'ffn_in': full['w_ffn_in'],
        'ffn_out': full['w_ffn_out'], 'ple': full['w_ple'], 'ple_gate': full['w_ple_gate'],
    }
    return wt


def _prep_small(small):
    pad = lambda v: jnp.pad(v, (0, LANE - v.shape[0]))[None, :]
    return {
        'conv_w': small['conv_w'], 'a_log': pad(small['dn_a_log']), 'dt_bias': pad(small['dn_dt_bias']),
        'dn_norm_w': small['dn_norm_w'][None, :], 'q_norm_w': small['q_norm_w'][None, :],
        'kv_norm_w': small['kv_norm_w'][None, :], 'ln1_g': small['ln1_g'][None, :], 'ln1_b': small['ln1_b'][None, :],
        'ln2_g': small['ln2_g'][None, :], 'ln2_b': small['ln2_b'][None, :],
    }


def _w_in_grad(g):
    mla = g['mla']
    ba0 = KV_LORA + ROPE_PAD
    cq0 = ba0 + 3 * LANE
    return jnp.concatenate([
        g['qkv'], g['z'], mla[:, ba0:ba0 + HEADS], mla[:, ba0 + HEADS:ba0 + 2 * HEADS], mla[:, cq0:cq0 + Q_LORA],
        mla[:, :KV_LORA], mla[:, KV_LORA:KV_LORA + ROPE], g['gg']], axis=1)


def _unprep_grads_late(g):
    return {
        'conv_w': g['conv_w'], 'dn_a_log': g['a_log'][0, :HEADS], 'dn_dt_bias': g['dt_bias'][0, :HEADS],
        'dn_norm_w': g['dn_norm_w'][0], 'q_norm_w': g['q_norm_w'][0], 'kv_norm_w': g['kv_norm_w'][0],
        'ln1_g': g['ln1_g'][0], 'ln1_b': g['ln1_b'][0], 'ln2_g': g['ln2_g'][0], 'ln2_b': g['ln2_b'][0],
    }


def _unprep_grads_early(g):
    w_uq = jnp.concatenate([g['uq_nope'].reshape(Q_LORA, HEADS, NOPE),
                            g['uq_rope'].reshape(Q_LORA, HEADS, ROPE_PAD)[:, :, :ROPE]], axis=2)
    return {
        'w_uq': w_uq, 'w_uk': jnp.transpose(g['uk'], (1, 0, 2)), 'w_uv': jnp.transpose(g['uv'], (1, 0, 2)),
        'w_br_dn': g['br_dn'], 'w_br_mla': g['br_mla'], 'w_o': g['o'],
        'w_ffn_in': g['ffn_in'], 'w_ffn_out': g['ffn_out'], 'w_ple': g['ple'], 'w_ple_gate': g['ple_gate'],
    }


_FLATB_PIECES = (
    ('w_ffn_out', 704, (704, D_MODEL)), ('w_br_dn', 256, (256, D_MODEL)), ('w_br_mla', 256, (256, D_MODEL)),
    ('w_o', 256, (256, D_MODEL)), ('w_ple_gate', 256, (256, D_MODEL)), ('w_uq', 144, (96, HEADS, NOPE + ROPE)),
    ('w_uk', 64, (64, HEADS, NOPE)), ('w_uv', 64, (64, HEADS, NOPE)), ('w_ple', 64, (PLE_DIM, 256)),
)
FLATB_ROWS = 2112
W_IN_SHARD = D_IN // N_SHARD
FFN_IN_SHARD = 2 * FFN_HIDDEN // N_SHARD
A_ROWS = D_MODEL + 32
_CONV_SHARD = QKV_W // N_SHARD
_ADD_TILES = (256, 256, 352)


def _flatb_offsets():
    offs, o = {}, 0
    for name, rows, _ in _FLATB_PIECES:
        offs[name] = o
        o += rows
    return offs, o


def _pack_shards(ws, conv_w):
    conv_bits = lax.bitcast_convert_type(conv_w, jnp.bfloat16).reshape(DN_CONV, 2 * _CONV_SHARD).astype(_CDT)
    tail = jnp.pad(conv_bits, ((0, A_ROWS - D_MODEL - DN_CONV), (0, W_IN_SHARD - 2 * _CONV_SHARD)))
    a_buf = jnp.concatenate([ws['w_in'].astype(_CDT), tail], axis=0)
    parts = [ws[name].astype(_CDT).reshape(rows, FLAT_W) for name, rows, _ in _FLATB_PIECES]
    used = sum(p.shape[0] for p in parts)
    parts.append(jnp.zeros((FLATB_ROWS - used, FLAT_W), _CDT))
    return [a_buf, ws['w_ffn_in'].astype(_CDT), jnp.concatenate(parts, axis=0)]


def _unpack_w_in(gathered, local, me):
    a = [jnp.where(me == s, local, gathered[s]) for s in range(N_SHARD)]
    conv = [lax.bitcast_convert_type(
        p[D_MODEL:D_MODEL + DN_CONV, :2 * _CONV_SHARD].astype(jnp.bfloat16).reshape(DN_CONV, _CONV_SHARD, 2), F32) for p in a]
    return jnp.concatenate([p[:D_MODEL] for p in a], axis=1), jnp.concatenate(conv, axis=1)


def _unpack_rest(gathered, local, me):
    pick = lambda b, s: jnp.where(me == s, local[b], gathered[b][s])
    full = {'w_ffn_in': jnp.concatenate([pick(0, s) for s in range(N_SHARD)], axis=1)}
    offs, _ = _flatb_offsets()
    fb = [pick(1, s) for s in range(N_SHARD)]
    for name, rows, shape in _FLATB_PIECES:
        pieces = [p[offs[name]:offs[name] + rows].reshape(shape) for p in fb]
        full[name] = jnp.concatenate(pieces, axis=1 if name == 'w_ple' else 0)
    return full


def _shard_columns(g, w):
    return jnp.stack([g[:, s * w:(s + 1) * w] for s in range(N_SHARD)])


def _pack_grads_rest(gw):
    parts = []
    for name, rows, _ in _FLATB_PIECES:
        g = gw[name]
        if name == 'w_ple':
            parts.append(_shard_columns(g, PLE_DIM).reshape(N_SHARD, rows, FLAT_W))
        else:
            parts.append(g.reshape(N_SHARD, rows, FLAT_W))
    used = sum(p.shape[1] for p in parts)
    parts.append(jnp.zeros((N_SHARD, FLATB_ROWS - used, FLAT_W), F32))
    return [_shard_columns(gw['w_ffn_in'], FFN_IN_SHARD), jnp.concatenate(parts, axis=1)]


def _unpack_reduced(mine, theirs, c):
    whole = [jnp.concatenate([jnp.where(c == 0, m, t), jnp.where(c == 0, t, m)], axis=0) for m, t in zip(mine, theirs)]
    out = {'w_in': whole[0], 'w_ffn_in': whole[1]}
    offs, _ = _flatb_offsets()
    for name, rows, shape in _FLATB_PIECES:
        out[name] = whole[2][offs[name]:offs[name] + rows].reshape(shape)
    return out


_HBM = pl.BlockSpec(memory_space=pltpu.HBM)


def _place():
    x, y, c = lax.axis_index("x"), lax.axis_index("y"), lax.axis_index("c")
    chips = [(1 - x, y), (x, 1 - y), (1 - x, 1 - y)]
    return x, y, c, chips


def _remote(src, dst, send_sems, recv_sems, k, to):
    return pltpu.make_async_remote_copy(src_ref=src, dst_ref=dst, send_sem=send_sems.at[k], recv_sem=recv_sems.at[k],
                                        device_id=to, device_id_type=_MESH)


def _half_rows(ref, half, hf, lead=None):
    rows = pl.ds(pl.multiple_of(hf * half, 16), half)
    return ref.at[rows, :] if lead is None else ref.at[lead, rows, :]


class _Rider:
    def __init__(self, inputs, out_shape, n_sems, copies, aliases=None):
        self.inputs, self.out_shape, self.n_sems, self.copies = list(inputs), list(out_shape), n_sems, copies
        self.aliases = aliases or {}


def _carried_call(body, rider, first, last, *, name, grid, in_specs, out_specs, out_shape, scratch_shapes, sem, args):
    n_in, n_out, n_scr = len(in_specs), len(out_specs), len(scratch_shapes)
    if rider is None:
        res = pl.pallas_call(body, name=name, grid=grid, in_specs=in_specs, out_specs=out_specs, out_shape=out_shape,
                             scratch_shapes=scratch_shapes, compiler_params=_cparams(sem))(*args)
        return list(res), []
    ri, ro = len(rider.inputs), len(rider.out_shape)

    def full_body(*refs):
        own_in, r_in = refs[:n_in], refs[n_in:n_in + ri]
        o0 = n_in + ri
        own_out, r_out = refs[o0:o0 + n_out], refs[o0 + n_out:o0 + n_out + ro]
        s0 = o0 + n_out + ro
        own_scr, send_sems, recv_sems = refs[s0:s0 + n_scr], refs[s0 + n_scr], refs[s0 + n_scr + 1]

        @pl.when(first())
        def _():
            sends, _ = rider.copies(r_in, r_out, send_sems, recv_sems)
            for cp in sends:
                cp.start()

        body(*own_in, *own_out, *own_scr)

        @pl.when(last())
        def _():
            sends, arrivals = rider.copies(r_in, r_out, send_sems, recv_sems)
            for cp in arrivals():
                cp.wait_recv()
            for cp in sends:
                cp.wait_send()

    res = pl.pallas_call(
        full_body, name=name, grid=grid, in_specs=list(in_specs) + [_HBM] * ri, out_specs=list(out_specs) + [_HBM] * ro,
        out_shape=list(out_shape) + rider.out_shape,
        scratch_shapes=list(scratch_shapes) + [pltpu.SemaphoreType.DMA((rider.n_sems,))] * 2,
        input_output_aliases={n_in + i: n_out + o for i, o in rider.aliases.items()},
        compiler_params=_cparams(sem))(*args, *rider.inputs)
    return list(res[:n_out]), list(res[n_out:])


def _ride_gather_send(bufs):
    n = len(bufs)
    halves = [b.shape[0] // 2 for b in bufs]

    def copies(ins, outs, send_sems, recv_sems):
        x, y, c, chips = _place()
        slot = lambda b, cx, cy: _half_rows(outs[b], halves[b], c, lead=2 * cx + cy)
        sends = [_remote(_half_rows(ins[b], halves[b], c), slot(b, x, y), send_sems, recv_sems, 3 * b + j, (cx, cy, c))
                 for b in range(n) for j, (cx, cy) in enumerate(chips)]
        arrivals = lambda: [_remote(slot(b, cx, cy), slot(b, cx, cy), send_sems, recv_sems, 3 * b + j, (x, y, c))
                            for b in range(n) for j, (cx, cy) in enumerate(chips)]
        return sends, arrivals

    return _Rider(bufs, [jax.ShapeDtypeStruct((N_SHARD,) + b.shape, b.dtype) for b in bufs], 3 * n, copies)


def _ride_gather_pass(gathered):
    n = len(gathered)
    halves = [g.shape[1] // 2 for g in gathered]

    def copies(ins, outs, send_sems, recv_sems):
        x, y, c, chips = _place()
        slot = lambda b, cx, cy, hf: _half_rows(outs[b], halves[b], hf, lead=2 * cx + cy)
        sends = [_remote(slot(b, cx, cy, c), slot(b, cx, cy, c), send_sems, recv_sems, 3 * b + j, (x, y, 1 - c))
                 for b in range(n) for j, (cx, cy) in enumerate(chips)]
        arrivals = lambda: [_remote(slot(b, cx, cy, 1 - c), slot(b, cx, cy, 1 - c), send_sems, recv_sems, 3 * b + j, (x, y, c))
                            for b in range(n) for j, (cx, cy) in enumerate(chips)]
        return sends, arrivals

    return _Rider(gathered, [jax.ShapeDtypeStruct(g.shape, g.dtype) for g in gathered], 3 * n, copies,
                  aliases={b: b for b in range(n)})


def _ride_small_gather(buf):
    def copies(ins, outs, send_sems, recv_sems):
        x, y, c, _ = _place()
        flip = lambda v, d: 1 - v if d else v
        sends, peers = [], []
        for dx in (0, 1):
            for dy in (0, 1):
                for dc in (0, 1):
                    if dx or dy or dc:
                        k = 4 * dx + 2 * dy + dc - 1
                        px, py, pc = flip(x, dx), flip(y, dy), flip(c, dc)
                        sends.append(_remote(ins[0], outs[0].at[4 * x + 2 * y + c], send_sems, recv_sems, k, (px, py, pc)))
                        peers.append((k, 4 * px + 2 * py + pc))
        arrivals = lambda: [_remote(ins[0], outs[0].at[slot], send_sems, recv_sems, k, (x, y, c)) for k, slot in peers]
        return sends, arrivals

    return _Rider([buf], [jax.ShapeDtypeStruct((8,) + buf.shape, buf.dtype)], 7, copies)


def _small_sum(gathered, buf, me_arr):
    n, r, width = gathered.shape

    def body(me_ref, g_ref, b_ref, o_ref):
        total = jnp.zeros((r, width), F32)
        for d in range(n):
            total = total + jnp.where(me_ref[0] == d, b_ref[...], g_ref[d])
        o_ref[...] = total

    return pl.pallas_call(
        body, name="small_sum", out_shape=jax.ShapeDtypeStruct((r, width), F32),
        grid_spec=pltpu.PrefetchScalarGridSpec(
            num_scalar_prefetch=1, grid=(1,),
            in_specs=[pl.BlockSpec((n, r, width), lambda i, me: (0, 0, 0)), pl.BlockSpec((r, width), lambda i, me: (0, 0))],
            out_specs=pl.BlockSpec((r, width), lambda i, me: (0, 0))),
        compiler_params=_cparams(("arbitrary",)))(me_arr, gathered, buf)


def _ride_pair_exchange(gbufs):
    n = len(gbufs)
    halves = [g.shape[1] // 2 for g in gbufs]

    def copies(ins, outs, send_sems, recv_sems):
        x, y, c, _ = _place()
        sends = [_remote(ins[b].at[:, pl.ds(pl.multiple_of((1 - c) * halves[b], 16), halves[b]), :], outs[b],
                         send_sems, recv_sems, b, (x, y, 1 - c)) for b in range(n)]
        arrivals = lambda: [_remote(outs[b], outs[b], send_sems, recv_sems, b, (x, y, c)) for b in range(n)]
        return sends, arrivals

    return _Rider(gbufs, [jax.ShapeDtypeStruct((N_SHARD, h, g.shape[2]), g.dtype) for g, h in zip(gbufs, halves)], n, copies)


def _ride_chip_exchange(parts):
    n = len(parts)

    def copies(ins, outs, send_sems, recv_sems):
        x, y, c, chips = _place()
        sends = [_remote(ins[b].at[2 * cx + cy], outs[b].at[j], send_sems, recv_sems, 3 * b + j, (cx, cy, c))
                 for b in range(n) for j, (cx, cy) in enumerate(chips)]
        arrivals = lambda: [_remote(ins[b].at[0], outs[b].at[j], send_sems, recv_sems, 3 * b + j, (x, y, c))
                            for b in range(n) for j in range(len(chips))]
        return sends, arrivals

    return _Rider(parts, [jax.ShapeDtypeStruct((3,) + p.shape[1:], p.dtype) for p in parts], 3 * n, copies)


def _gather_shards(bufs, name):
    n = len(bufs)
    halves = [b.shape[0] // 2 for b in bufs]

    def body(*refs):
        ins, outs, send_sems, recv_sems = refs[:n], refs[n:2 * n], refs[2 * n], refs[2 * n + 1]
        x, y, c, chips = _place()
        me, sibling = (x, y, c), (x, y, 1 - c)
        slot = lambda b, cx, cy, hf: _half_rows(outs[b], halves[b], hf, lead=2 * cx + cy)
        first = [_remote(_half_rows(ins[b], halves[b], c), slot(b, x, y, c), send_sems, recv_sems, 6 * b + j, (cx, cy, c))
                 for b in range(n) for j, (cx, cy) in enumerate(chips)]
        for cp in first:
            cp.start()
        passed = []
        for j, (cx, cy) in enumerate(chips):
            for b in range(n):
                _remote(slot(b, cx, cy, c), slot(b, cx, cy, c), send_sems, recv_sems, 6 * b + j, me).wait_recv()
                fwd = _remote(slot(b, cx, cy, c), slot(b, cx, cy, c), send_sems, recv_sems, 6 * b + 3 + j, sibling)
                fwd.start()
                passed.append(fwd)
        for j, (cx, cy) in enumerate(chips):
            for b in range(n):
                _remote(slot(b, cx, cy, 1 - c), slot(b, cx, cy, 1 - c), send_sems, recv_sems, 6 * b + 3 + j, me).wait_recv()
        for cp in first + passed:
            cp.wait_send()

    return pl.pallas_call(
        body, name=name, out_shape=[jax.ShapeDtypeStruct((N_SHARD,) + b.shape, b.dtype) for b in bufs],
        in_specs=[_HBM] * n, out_specs=[_HBM] * n,
        scratch_shapes=[pltpu.SemaphoreType.DMA((6 * n,)), pltpu.SemaphoreType.DMA((6 * n,))],
    )(*bufs)


def _reduce_pair_exchange(gbufs, name):
    n = len(gbufs)
    halves = [g.shape[1] // 2 for g in gbufs]

    def body(*refs):
        ins, outs, send_sems, recv_sems = refs[:n], refs[n:2 * n], refs[2 * n], refs[2 * n + 1]
        x, y, c, _ = _place()
        cps = [_remote(ins[b].at[:, pl.ds(pl.multiple_of((1 - c) * halves[b], 16), halves[b]), :], outs[b],
                       send_sems, recv_sems, b, (x, y, 1 - c)) for b in range(n)]
        for cp in cps:
            cp.start()
        for cp in cps:
            cp.wait()

    return pl.pallas_call(
        body, name=name,
        out_shape=[jax.ShapeDtypeStruct((N_SHARD, h, g.shape[2]), g.dtype) for g, h in zip(gbufs, halves)],
        in_specs=[_HBM] * n, out_specs=[_HBM] * n,
        scratch_shapes=[pltpu.SemaphoreType.DMA((n,)), pltpu.SemaphoreType.DMA((n,))],
    )(*gbufs)


def _pair_add(gbuf, recv, c_arr, tr, name):
    _, rows, width = gbuf.shape
    half = rows // 2
    nt = half // tr

    def body(c_ref, a_ref, b_ref, o_ref):
        o_ref[...] = (a_ref[...] + b_ref[...]).astype(o_ref.dtype)

    blk = lambda f: pl.BlockSpec((None, tr, width), f)
    return pl.pallas_call(
        body, name=name, out_shape=jax.ShapeDtypeStruct((N_SHARD, half, width), jnp.bfloat16),
        grid_spec=pltpu.PrefetchScalarGridSpec(
            num_scalar_prefetch=1, grid=(N_SHARD, nt),
            in_specs=[blk(lambda s, i, c: (s, c[0] * nt + i, 0)), blk(lambda s, i, c: (s, i, 0))],
            out_specs=blk(lambda s, i, c: (s, i, 0))),
        compiler_params=_cparams(("parallel", "parallel")))(c_arr, gbuf, recv)


def _chip_add(part, recv, me_arr, tr, name):
    _, half, width = part.shape

    def body(me_ref, own, a0, a1, a2, o_ref):
        f = lambda r: r[...].astype(F32)
        o_ref[...] = ((f(own) + f(a0)) + f(a1)) + f(a2)

    specs = [pl.BlockSpec((None, tr, width), lambda i, me: (me[0], i, 0))]
    specs += [pl.BlockSpec((None, tr, width), functools.partial(lambda i, me, k: (k, i, 0), k=k)) for k in range(3)]
    return pl.pallas_call(
        body, name=name, out_shape=jax.ShapeDtypeStruct((half, width), F32),
        grid_spec=pltpu.PrefetchScalarGridSpec(
            num_scalar_prefetch=1, grid=(half // tr,), in_specs=specs,
            out_specs=pl.BlockSpec((tr, width), lambda i, me: (i, 0))),
        compiler_params=_cparams(("parallel",)))(me_arr, part, recv, recv, recv)


def _reduce_pair_share(rhalves, name):
    n = len(rhalves)

    def body(*refs):
        ins, outs, send_sems, recv_sems = refs[:n], refs[n:2 * n], refs[2 * n], refs[2 * n + 1]
        x, y, c, _ = _place()
        cps = [_remote(ins[b], outs[b], send_sems, recv_sems, b, (x, y, 1 - c)) for b in range(n)]
        for cp in cps:
            cp.start()
        for cp in cps:
            cp.wait()

    return pl.pallas_call(
        body, name=name, out_shape=[jax.ShapeDtypeStruct(r.shape, r.dtype) for r in rhalves],
        in_specs=[_HBM] * n, out_specs=[_HBM] * n,
        scratch_shapes=[pltpu.SemaphoreType.DMA((n,)), pltpu.SemaphoreType.DMA((n,))],
    )(*rhalves)


def _row_tile(rows, cap):
    if rows <= cap:
        return rows
    t = (cap // 8) * 8
    while t >= 8:
        if rows % t == 0:
            return t
        t -= 8
    return rows


def _adamw(w, g, m, v, name):
    shape = w.shape
    cols = shape[-1] if len(shape) <= 3 else shape[-2] * shape[-1]
    lead = len(shape) == 3
    w2, g2, m2, v2 = (a if lead else a.reshape(-1, cols) for a in (w, g, m, v))
    rows = shape[1] if lead else w2.shape[0]
    tr, tc = _row_tile(rows, 256), cols
    if tr == rows and rows > 256:
        tc = _tile(cols, 256)

    def body(w_ref, g_ref, m_ref, v_ref, d_ref, mo_ref, vo_ref):
        gv = g_ref[...]
        mn = ADAM_B1 * m_ref[...] + (1.0 - ADAM_B1) * gv
        vn = ADAM_B2 * v_ref[...] + (1.0 - ADAM_B2) * (gv * gv)
        m_hat = mn / (1.0 - ADAM_B1 ** ADAM_STEP)
        v_hat = vn / (1.0 - ADAM_B2 ** ADAM_STEP)
        d_ref[...] = -ADAM_LR * (m_hat / (jnp.sqrt(v_hat) + ADAM_EPS) + ADAM_WD * w_ref[...])
        mo_ref[...] = mn
        vo_ref[...] = vn

    blk = (pl.BlockSpec((None, tr, tc), lambda i, j: (0, i, j)) if lead else pl.BlockSpec((tr, tc), lambda i, j: (i, j)))
    outs = pl.pallas_call(
        body, name=name, grid=(rows // tr, cols // tc), in_specs=[blk] * 4, out_specs=[blk] * 3,
        out_shape=[jax.ShapeDtypeStruct(w2.shape, F32)] * 3,
        compiler_params=_cparams(("parallel", "parallel")))(w2, g2, m2, v2)
    return tuple(o.reshape(shape) for o in outs)


_WEIGHT_NAMES = ('w_in', 'conv_w', 'dn_a_log', 'dn_dt_bias', 'dn_norm_w', 'q_norm_w', 'w_uq', 'kv_norm_w', 'w_uk',
                 'w_uv', 'w_br_dn', 'w_br_mla', 'w_o', 'ln1_g', 'ln1_b', 'w_ffn_in', 'w_ffn_out', 'w_ple',
                 'w_ple_gate', 'ln2_g', 'ln2_b')
_SMALL_NAMES = ('ln1_g', 'ln1_b', 'ln2_g', 'ln2_b', 'q_norm_w', 'kv_norm_w', 'dn_norm_w', 'dn_a_log', 'dn_dt_bias')
_SMALL_GROUP = 8
_CONV_SMALL_ROW = len(_SMALL_NAMES) * _SMALL_GROUP
_CONV_SMALL_ROWS = DN_CONV * QKV_W // FLAT_W


_LOSS_SMALL_ROW = _CONV_SMALL_ROW + 16


def _pack_small(gw, loss_lanes):
    rows = [jnp.pad(gw[n][None, :], ((0, _SMALL_GROUP - 1), (0, FLAT_W - gw[n].shape[0]))) for n in _SMALL_NAMES]
    rows.append(jnp.pad(gw['conv_w'].reshape(_CONV_SMALL_ROWS, FLAT_W), ((0, 16 - _CONV_SMALL_ROWS), (0, 0))))
    rows.append(jnp.pad(loss_lanes, ((0, _SMALL_GROUP - 1), (0, FLAT_W - LANE))))
    return jnp.concatenate(rows, axis=0)


class _Exchange:
    def __init__(self, local, me_chip, c_arr):
        self.local, self.me_chip, self.c_arr = local, me_chip, c_arr
        self.parts = self.arrived = None

    def gather_send(self):
        return _ride_gather_send(self.local)

    def gather_pass(self, sent):
        return _ride_gather_pass(sent)

    def weights(self, gathered):
        return _prep_weights(_unpack_rest(gathered, self.local, self.me_chip))

    def pair_send(self, g):
        self.gbufs = _pack_grads_rest(_unprep_grads_early(g))
        return _ride_pair_exchange(self.gbufs)

    def reduce_send(self, got):
        self.parts = [_pair_add(g_, r_, self.c_arr, tr, "pair_add_%d" % (i + 1))
                      for i, (g_, r_, tr) in enumerate(zip(self.gbufs, got, _ADD_TILES[1:]))]
        return _ride_chip_exchange(self.parts)

    def reduce_arrived(self, arrived):
        self.arrived = list(arrived)

    def in_send(self, g):
        g_in = [_shard_columns(_w_in_grad(g), W_IN_SHARD)]
        got = _reduce_pair_exchange(g_in, "reduce_pair_exchange_w_in")
        self.part_in = _pair_add(g_in[0], got[0], self.c_arr, _ADD_TILES[0], "pair_add_0")
        return _ride_chip_exchange([self.part_in])

    def in_arrived(self, arrived):
        self.arrived_in = list(arrived)

    def small_send(self, g):
        self.small = _pack_small(_unprep_grads_late(g), g['loss_lanes'])
        return _ride_small_gather(self.small)

    def small_arrived(self, arrived):
        self.small_gathered = arrived[0]


def kernel(x, p, positions, w_in, conv_w, dn_a_log, dn_dt_bias, dn_norm_w, q_norm_w, w_uq, kv_norm_w, w_uk, w_uv, w_br_dn, w_br_mla, w_o, ln1_g, ln1_b, w_ffn_in, w_ffn_out, w_ple, w_ple_gate, ln2_g, ln2_b, loss_target, m_w_in, m_conv_w, m_dn_a_log, m_dn_dt_bias, m_dn_norm_w, m_q_norm_w, m_w_uq, m_kv_norm_w, m_w_uk, m_w_uv, m_w_br_dn, m_w_br_mla, m_w_o, m_ln1_g, m_ln1_b, m_w_ffn_in, m_w_ffn_out, m_w_ple, m_w_ple_gate, m_ln2_g, m_ln2_b, v_w_in, v_conv_w, v_dn_a_log, v_dn_dt_bias, v_dn_norm_w, v_q_norm_w, v_w_uq, v_kv_norm_w, v_w_uk, v_w_uv, v_w_br_dn, v_w_br_mla, v_w_o, v_ln1_g, v_ln1_b, v_w_ffn_in, v_w_ffn_out, v_w_ple, v_w_ple_gate, v_ln2_g, v_ln2_b):
    ws = dict(w_in=w_in, conv_w=conv_w, dn_a_log=dn_a_log, dn_dt_bias=dn_dt_bias, dn_norm_w=dn_norm_w, q_norm_w=q_norm_w,
              w_uq=w_uq, kv_norm_w=kv_norm_w, w_uk=w_uk, w_uv=w_uv, w_br_dn=w_br_dn, w_br_mla=w_br_mla, w_o=w_o,
              ln1_g=ln1_g, ln1_b=ln1_b, w_ffn_in=w_ffn_in, w_ffn_out=w_ffn_out, w_ple=w_ple, w_ple_gate=w_ple_gate,
              ln2_g=ln2_g, ln2_b=ln2_b)
    ms = dict(w_in=m_w_in, conv_w=m_conv_w, dn_a_log=m_dn_a_log, dn_dt_bias=m_dn_dt_bias, dn_norm_w=m_dn_norm_w,
              q_norm_w=m_q_norm_w, w_uq=m_w_uq, kv_norm_w=m_kv_norm_w, w_uk=m_w_uk, w_uv=m_w_uv, w_br_dn=m_w_br_dn,
              w_br_mla=m_w_br_mla, w_o=m_w_o, ln1_g=m_ln1_g, ln1_b=m_ln1_b, w_ffn_in=m_w_ffn_in, w_ffn_out=m_w_ffn_out,
              w_ple=m_w_ple, w_ple_gate=m_w_ple_gate, ln2_g=m_ln2_g, ln2_b=m_ln2_b)
    vs = dict(w_in=v_w_in, conv_w=v_conv_w, dn_a_log=v_dn_a_log, dn_dt_bias=v_dn_dt_bias, dn_norm_w=v_dn_norm_w,
              q_norm_w=v_q_norm_w, w_uq=v_w_uq, kv_norm_w=v_kv_norm_w, w_uk=v_w_uk, w_uv=v_w_uv, w_br_dn=v_w_br_dn,
              w_br_mla=v_w_br_mla, w_o=v_w_o, ln1_g=v_ln1_g, ln1_b=v_ln1_b, w_ffn_in=v_w_ffn_in, w_ffn_out=v_w_ffn_out,
              w_ple=v_w_ple, w_ple_gate=v_w_ple_gate, ln2_g=v_ln2_g, ln2_b=v_ln2_b)
    mx, my, mc = lax.axis_index("x"), lax.axis_index("y"), lax.axis_index("c")

    me_chip = 2 * mx + my
    c_arr = jnp.reshape(mc, (1,)).astype(jnp.int32)
    me_arr = jnp.reshape(me_chip, (1,)).astype(jnp.int32)
    sharded = ('w_in', 'w_ffn_in') + tuple(name for name, _, _ in _FLATB_PIECES)

    local = _pack_shards({name: ws[name][0] for name in sharded}, conv_w[0])
    (gathered_in,) = _gather_shards(local[:1], "gather_w_in")
    w_in_full, conv_full = _unpack_w_in(gathered_in, local[0], me_chip)
    small = {n: ws[n][0] for n in _SMALL_NAMES}
    small['conv_w'] = conv_full
    sp = _prep_small(small)
    cosb, sinb = _rope_tables(positions[0])
    exch = _Exchange(local[1:], me_chip, c_arr)

    loss_lanes, dx, g = _local_step(x[0], p[0, 0], cosb, sinb, loss_target[0], _prep_w_in(w_in_full), sp, exch)

    parts = [exch.part_in] + exch.parts
    arrived = exch.arrived_in + exch.arrived
    mine = [_chip_add(p_, r_, me_arr, tr, "chip_add_%d" % i) for i, (p_, r_, tr) in enumerate(zip(parts, arrived, _ADD_TILES))]
    reduced = _unpack_reduced(mine, _reduce_pair_share(mine, "reduce_pair_share"), mc)
    tot = _small_sum(exch.small_gathered, exch.small, jnp.reshape(4 * mx + 2 * my + mc, (1,)).astype(jnp.int32))
    loss = jnp.sum(tot[_LOSS_SMALL_ROW, :LANE])
    gred = {name: reduced[name][None] for name in sharded}
    for i, n in enumerate(_SMALL_NAMES):
        gred[n] = tot[i * _SMALL_GROUP, :ws[n].shape[1]][None]
    conv_tot = tot[_CONV_SMALL_ROW:_CONV_SMALL_ROW + _CONV_SMALL_ROWS].reshape(DN_CONV, QKV_W)
    gred['conv_w'] = lax.dynamic_slice_in_dim(conv_tot, (2 * mx + my) * _CONV_SHARD, _CONV_SHARD, axis=1)[None]

    deltas, new_m, new_v = {}, {}, {}
    for n in _WEIGHT_NAMES:
        if n == 'w_in':
            tr_ = lambda a: jnp.transpose(a, (0, 2, 1))
            g_t = tr_(gred[n].reshape(ws[n].shape))
            outs = _adamw(tr_(ws[n]), g_t, tr_(ms[n]), tr_(vs[n]), "adamw_" + n)
            gred[n] = tr_(g_t)
            deltas[n], new_m[n], new_v[n] = (tr_(o) for o in outs)
            continue
        gred[n] = gred[n].reshape(ws[n].shape)
        deltas[n], new_m[n], new_v[n] = _adamw(ws[n], gred[n], ms[n], vs[n], "adamw_" + n)
    return (loss, dx[None], *[gred[n] for n in _WEIGHT_NAMES], *[deltas[n] for n in _WEIGHT_NAMES],
            *[new_m[n] for n in _WEIGHT_NAMES], *[new_v[n] for n in _WEIGHT_NAMES])
```

```python
import functools

import jax
import jax.numpy as jnp
from jax import lax
from jax.experimental import pallas as pl
from jax.experimental.pallas import tpu as pltpu

F32 = jnp.float32
_CDT = jnp.bfloat16
_MESH = pl.DeviceIdType.MESH

D_MODEL = 1024
PLE_DIM = 256
HEADS = 8
DN_DK = 128
DN_CHUNK = 64
DN_CONV = 4
QKV_W = 3 * HEADS * DN_DK
Q_LORA = 384
KV_LORA = 256
NOPE = 128
ROPE = 64
ROPE_PAD = 128
FFN_HIDDEN = 2816
D_IN = 6864
ROPE_BASE = 10000.0
ALPHA = 2.0 ** 0.25
ATT_SCALE = (NOPE + ROPE) ** -0.5
NEG_BIG = -1e30
ADAM_LR, ADAM_B1, ADAM_B2, ADAM_EPS, ADAM_WD, ADAM_STEP = 0.001, 0.9, 0.999, 1e-08, 0.01, 10

LANE = 128
VMEM_LIMIT = 56 * 1024 * 1024
MM_VMEM_BUDGET = 40 * 1024 * 1024
N_SHARD = 4
FLAT_W = 1024
SMALL_ROWS = 96


def _tile(dim, cap):
    if dim <= cap:
        return dim
    t = (cap // LANE) * LANE
    while t >= LANE:
        if dim % t == 0:
            return t
        t -= LANE
    return dim


def _cparams(sem):
    return pltpu.CompilerParams(dimension_semantics=sem, vmem_limit_bytes=VMEM_LIMIT)


def _mm(a, b, *, name, ta=False, tb=False, add=None, add_scale=1.0, out_dtype=F32, heads=None,
        a_head=None, b_head=None, out_head=None, dims=None, tm=1408, tn=1408, rider=None):
    m, n, k = dims
    tm, tn = _tile(m, tm), _tile(n, tn)
    if heads is None and (m // tm) * (n // tn) < 4:
        sm, sn = _tile(m, 512), _tile(n, 512)
        if sm == min(m, 512) and sn == min(n, 512):
            tm, tn = sm, sn
    sa, sb, so = a.dtype.itemsize, b.dtype.itemsize, jnp.dtype(out_dtype).itemsize

    def vmem_need(tk_):
        acc = tm * tn * 4 if tk_ < k else 0
        extra = 2 * tm * tn * 4 if add is not None else 0
        return 2 * (tm * tk_ * sa + tk_ * tn * sb) + 2 * tm * tn * so + acc + extra

    tk = k
    while vmem_need(tk) > MM_VMEM_BUDGET and tk > LANE:
        smaller = _tile(k, tk - LANE)
        if smaller >= tk:
            break
        tk = smaller
    nk = k // tk
    hgrid = () if heads is None else (heads,)
    off = len(hgrid)

    def spec(rows, cols, rtile, ctile, rsel, csel, layout):
        def idx(*g):
            h = g[0] if off else 0
            ri, ci = g[off + rsel], g[off + csel]
            if layout == 'lead':
                return (h, ri, ci)
            if layout == 'col':
                return (ri, h * (cols // ctile) + ci)
            return (ri, ci)
        if layout == 'lead':
            return pl.BlockSpec((None, rtile, ctile), idx)
        return pl.BlockSpec((rtile, ctile), idx)

    a_spec = spec(k, m, tk, tm, 2, 0, a_head) if ta else spec(m, k, tm, tk, 0, 2, a_head)
    b_spec = spec(n, k, tn, tk, 1, 2, b_head) if tb else spec(k, n, tk, tn, 2, 1, b_head)
    o_spec = spec(m, n, tm, tn, 0, 1, out_head)
    in_specs = [a_spec, b_spec]
    args = [a, b]
    if add is not None:
        in_specs.append(spec(m, n, tm, tn, 0, 1, out_head))
        args.append(add)
    dn = (((0 if ta else 1,), (1 if tb else 0,)), ((), ()))

    def body(*refs):
        a_ref, b_ref = refs[0], refs[1]
        prod = lax.dot_general(a_ref[...].astype(_CDT), b_ref[...].astype(_CDT), dn, preferred_element_type=F32)
        if nk == 1:
            o_ref = refs[-1]
            if add is not None:
                prod = prod + refs[2][...].astype(F32) * add_scale
            o_ref[...] = prod.astype(out_dtype)
            return
        o_ref, acc_ref = refs[-2], refs[-1]
        kk = pl.program_id(off + 2)

        @pl.when(kk == 0)
        def _():
            if add is not None:
                acc_ref[...] = refs[2][...].astype(F32) * add_scale
            else:
                acc_ref[...] = jnp.zeros_like(acc_ref)

        acc_ref[...] += prod

        @pl.when(kk == nk - 1)
        def _():
            o_ref[...] = acc_ref[...].astype(out_dtype)

    if out_head == 'lead':
        oshape = (heads, m, n)
    elif out_head == 'col':
        oshape = (m, heads * n)
    else:
        oshape = (m, n)
    grid = hgrid + (m // tm, n // tn, nk)
    scratch = [pltpu.VMEM((tm, tn), F32)] if nk > 1 else []
    if rider is not None:
        (out,), carried = _carried_call(
            body, rider, *_grid_ends(grid), name=name, grid=grid, in_specs=in_specs, out_specs=[o_spec],
            out_shape=[jax.ShapeDtypeStruct(oshape, out_dtype)], scratch_shapes=scratch,
            sem=("arbitrary",) * len(grid), args=tuple(args))
        return out, carried
    sem = ("parallel",) * (off + 2) + ("arbitrary",)
    return pl.pallas_call(
        body, name=name, grid=grid, in_specs=in_specs, out_specs=o_spec,
        out_shape=jax.ShapeDtypeStruct(oshape, out_dtype), scratch_shapes=scratch,
        compiler_params=_cparams(sem))(*args)


def _mm2(a, b, **kw):
    ta, tb = kw.get('ta', False), kw.get('tb', False)
    m = a.shape[1] if ta else a.shape[0]
    k = a.shape[0] if ta else a.shape[1]
    n = b.shape[0] if tb else b.shape[1]
    return _mm(a, b, dims=(m, n, k), **kw)


def _rowwise(fn, rows, bcast, outs, reds=(), *, name, tm=256, heads=None):
    t = rows[0][0].shape[0]
    tm = min(tm, t)
    hn = 1 if heads is None else heads
    in_specs, args = [], []
    for arr, width, base, per_head in rows:
        in_specs.append(pl.BlockSpec((tm, width), functools.partial(
            lambda i, h, base, per_head: (i, base + (h if per_head else 0)), base=base, per_head=per_head)))
        args.append(arr)
    for arr in bcast:
        in_specs.append(pl.BlockSpec(arr.shape, lambda i, h: (0, 0)))
        args.append(arr)
    out_specs, out_shape = [], []
    for total, width, per_head, dt in outs:
        out_specs.append(pl.BlockSpec((tm, width), functools.partial(
            lambda i, h, per_head: (i, h if per_head else 0), per_head=per_head)))
        out_shape.append(jax.ShapeDtypeStruct((t, total), dt))
    for shp in reds:
        out_specs.append(pl.BlockSpec(shp, lambda i, h: (0, 0)))
        out_shape.append(jax.ShapeDtypeStruct(shp, F32))
    n_in, n_out, n_red = len(args), len(outs), len(reds)

    def body(*refs):
        i, h = pl.program_id(0), pl.program_id(1)
        vals = fn(h, *[r[...] for r in refs[:n_in]])
        for r, v in zip(refs[n_in:n_in + n_out], vals[:n_out]):
            r[...] = v.astype(r.dtype)
        if n_red:
            @pl.when((i == 0) & (h == 0))
            def _():
                for r in refs[n_in + n_out:]:
                    r[...] = jnp.zeros_like(r)
            for r, v in zip(refs[n_in + n_out:], vals[n_out:]):
                r[...] += v

    sem = ("arbitrary", "arbitrary") if n_red else ("parallel", "parallel")
    res = pl.pallas_call(body, name=name, grid=(t // tm, hn), in_specs=in_specs, out_specs=out_specs,
                         out_shape=out_shape, compiler_params=_cparams(sem))(*args)
    return tuple(res)


def _sigmoid(x):
    return 1.0 / (1.0 + jnp.exp(-x))


def _silu(x):
    return x * _sigmoid(x)


def _softplus(x):
    return jnp.maximum(x, 0.0) + jnp.log(1.0 + jnp.exp(-jnp.abs(x)))


def _layer_norm(t, g, b):
    mu = jnp.mean(t, axis=-1, keepdims=True)
    d = t - mu
    var = jnp.mean(d * d, axis=-1, keepdims=True)
    return d * lax.rsqrt(var + 1e-5) * g + b


def _rms_norm(t, w):
    return t * lax.rsqrt(jnp.mean(t * t, axis=-1, keepdims=True) + 1e-6) * w


def _swap_rope_halves(t):
    lane = lax.broadcasted_iota(jnp.int32, t.shape, 1) % ROPE_PAD
    n = t.shape[1]
    up = pltpu.roll(t, n - ROPE // 2, axis=1)
    dn = pltpu.roll(t, ROPE // 2, axis=1)
    return jnp.where(lane < ROPE // 2, up, jnp.where(lane < ROPE, dn, 0.0))


def _rope(t, cosb, sinb):
    reps = t.shape[1] // ROPE_PAD
    c = jnp.tile(cosb, (1, reps)) if reps > 1 else cosb
    s = jnp.tile(sinb, (1, reps)) if reps > 1 else sinb
    return t * c + _swap_rope_halves(t) * s


def _rope_bwd(d, cosb, sinb):
    reps = d.shape[1] // ROPE_PAD
    c = jnp.tile(cosb, (1, reps)) if reps > 1 else cosb
    s = jnp.tile(sinb, (1, reps)) if reps > 1 else sinb
    return d * c + _swap_rope_halves(d * s)


_CONV_ROWS = 256
_CONV_COLS = 256


def _conv_window(ref, r0, lo, hi, t):
    parts = []
    start, stop = r0 - lo, r0 + _CONV_ROWS + hi
    if start < 0:
        parts.append(jnp.zeros((-start, ref.shape[1]), F32))
        start = 0
    tail = max(stop - t, 0)
    parts.append(ref[start:stop - tail, :].astype(F32))
    if tail:
        parts.append(jnp.zeros((tail, ref.shape[1]), F32))
    return parts[0] if len(parts) == 1 else jnp.concatenate(parts, axis=0)


def _conv_taps(win, w_ref, n_out):
    acc = win[8:8 + n_out] * w_ref[DN_CONV - 1:DN_CONV, :]
    for i in range(DN_CONV - 1):
        acc = acc + pltpu.roll(win, DN_CONV - 1 - i, axis=0)[8:8 + n_out] * w_ref[i:i + 1, :]
    return acc


def _conv_silu(x, w):
    t, ch = x.shape

    def body(x_ref, w_ref, o_ref):
        for r in range(t // _CONV_ROWS):
            r0 = r * _CONV_ROWS
            c = _conv_taps(_conv_window(x_ref, r0, 8, 0, t), w_ref, _CONV_ROWS)
            o_ref[r0:r0 + _CONV_ROWS, :] = _silu(c)

    return pl.pallas_call(
        body, name="conv_silu", grid=(ch // _CONV_COLS,),
        in_specs=[pl.BlockSpec((t, _CONV_COLS), lambda j: (0, j)), pl.BlockSpec((DN_CONV, _CONV_COLS), lambda j: (0, j))],
        out_specs=pl.BlockSpec((t, _CONV_COLS), lambda j: (0, j)),
        out_shape=jax.ShapeDtypeStruct((t, ch), F32), compiler_params=_cparams(("parallel",)))(x, w)


def _conv_silu_bwd(x, w, dys):
    t, ch = x.shape
    per = ch // len(dys) // _CONV_COLS

    def body(x_ref, w_ref, *rest):
        dy_refs, (dx_ref, dw_ref) = rest[:len(dys)], rest[len(dys):]
        sec = pl.program_id(0) // per
        dws = [jnp.zeros((1, _CONV_COLS), F32) for _ in range(DN_CONV)]
        for r in range(t // _CONV_ROWS):
            r0 = r * _CONV_ROWS
            n_ext = _CONV_ROWS + 8
            xw = _conv_window(x_ref, r0, 8, 8, t)
            c = _conv_taps(xw, w_ref, n_ext)
            sg = _sigmoid(c)
            dy = _conv_window(dy_refs[-1], r0, 0, 8, t)
            for k in range(len(dys) - 2, -1, -1):
                dy = jnp.where(sec == k, _conv_window(dy_refs[k], r0, 0, 8, t), dy)
            ds = dy * (sg * (1.0 + c * (1.0 - sg)))
            x0 = xw[8:8 + _CONV_ROWS]
            dx = jnp.zeros((_CONV_ROWS, _CONV_COLS), F32)
            for i in range(DN_CONV):
                sh = DN_CONV - 1 - i
                ds_up = (ds if sh == 0 else pltpu.roll(ds, n_ext - sh, axis=0))[:_CONV_ROWS]
                dx = dx + ds_up * w_ref[i:i + 1, :]
                dws[i] = dws[i] + jnp.sum(x0 * ds_up, axis=0, keepdims=True)
            dx_ref[r0:r0 + _CONV_ROWS, :] = dx.astype(dx_ref.dtype)
        for i in range(DN_CONV):
            dw_ref[i:i + 1, :] = dws[i]

    blk = pl.BlockSpec((t, _CONV_COLS), lambda j: (0, j))
    wblk = pl.BlockSpec((DN_CONV, _CONV_COLS), lambda j: (0, j))
    dy_specs = [pl.BlockSpec((t, _CONV_COLS), functools.partial(lambda j, k: (0, jnp.clip(j - k * per, 0, per - 1)), k=k))
                for k in range(len(dys))]
    return pl.pallas_call(
        body, name="conv_silu_bwd", grid=(ch // _CONV_COLS,), in_specs=[blk, wblk] + dy_specs, out_specs=[blk, wblk],
        out_shape=[jax.ShapeDtypeStruct((t, ch), _CDT), jax.ShapeDtypeStruct((DN_CONV, ch), F32)],
        compiler_params=_cparams(("arbitrary",)))(x, w, *dys)


_PA_ROWS = 1024
_PA_ROWS_FWD = 1024


def _bmm(a, b, spec):
    return jnp.einsum(spec, a.astype(_CDT), b.astype(_CDT), preferred_element_type=F32)


def _split16(a):
    hi = a.astype(jnp.bfloat16)
    return hi, (a - hi.astype(F32)).astype(jnp.bfloat16)


def _bmm3(a, b, spec):
    ah, al = _split16(a)
    bh, bl = _split16(b)
    e = lambda p, q: jnp.einsum(spec, p, q, preferred_element_type=F32)
    return e(ah, bh) + (e(ah, bl) + e(al, bh))


def _split3(b):
    b0 = b.astype(jnp.bfloat16)
    r1 = b - b0.astype(F32)
    b1 = r1.astype(jnp.bfloat16)
    return b0, b1, (r1 - b1.astype(F32)).astype(jnp.bfloat16)


@functools.partial(jax.custom_vjp, nondiff_argnums=(2, 3))
def _select_mm(sel, b, spec, spec_t):
    return sum(jnp.einsum(spec, sel, t, preferred_element_type=F32) for t in _split3(b))


def _select_mm_fwd(sel, b, spec, spec_t):
    return _select_mm(sel, b, spec, spec_t), sel


def _select_mm_bwd(spec, spec_t, sel, ct):
    return jnp.zeros_like(sel), sum(jnp.einsum(spec_t, sel, t, preferred_element_type=F32) for t in _split3(ct))


_select_mm.defvjp(_select_mm_fwd, _select_mm_bwd)


def _tri_inverse(l_mat, eye):
    pw = -l_mat
    t_inv = eye + pw
    for _ in range(5):
        pw = _bmm3(pw, pw, 'bij,bjk->bik')
        t_inv = t_inv + _bmm3(t_inv, pw, 'bij,bjk->bik')
    return t_inv


@jax.custom_vjp
def _tri_inverse_saved(l_mat, t_saved):
    return t_saved


def _tri_inverse_saved_fwd(l_mat, t_saved):
    return t_saved, t_saved


def _tri_inverse_saved_bwd(t_saved, dt):
    left = _bmm3(t_saved, dt, 'bji,bjk->bik')
    return -_bmm3(left, t_saved, 'bij,bkj->bik'), jnp.zeros_like(t_saved)


_tri_inverse_saved.defvjp(_tri_inverse_saved_fwd, _tri_inverse_saved_bwd)


def _phase_a(h, q, k, v, ba, alog, dtb, t_saved=None):
    r = q.shape[0]
    nb = r // DN_CHUNK
    c = DN_CHUNK
    lane = lax.broadcasted_iota(jnp.int32, (1, LANE), 1)
    selb = (lane == h).astype(F32)
    sela = (lane == h + HEADS).astype(F32)
    b_raw = jnp.sum(ba * selb, axis=1, keepdims=True)
    a_raw = jnp.sum(ba * sela, axis=1, keepdims=True)
    al = jnp.sum(alog * selb, axis=1, keepdims=True)
    dt = jnp.sum(dtb * selb, axis=1, keepdims=True)
    beta = jnp.broadcast_to(_sigmoid(b_raw), (r, LANE))
    g = jnp.broadcast_to(-jnp.exp(al) * _softplus(a_raw + dt), (r, LANE))
    qn = q * lax.rsqrt(jnp.sum(q * q, -1, keepdims=True) + 1e-6) * (DN_DK ** -0.5)
    kn = k * lax.rsqrt(jnp.sum(k * k, -1, keepdims=True) + 1e-6)
    q3, k3, v3 = qn.reshape(nb, c, LANE), kn.reshape(nb, c, LANE), v.reshape(nb, c, LANE)
    b3, g3 = beta.reshape(nb, c, LANE), g.reshape(nb, c, LANE)
    ri = lax.broadcasted_iota(jnp.int32, (nb, c, c), 1)
    ci = lax.broadcasted_iota(jnp.int32, (nb, c, c), 2)
    tril, strict = ri >= ci, ri > ci
    gc = _select_mm(tril.astype(jnp.bfloat16), g3, 'bij,bjd->bid', 'bij,bid->bjd')
    onehot = (lax.broadcasted_iota(jnp.int32, (nb, c, LANE), 2) == 0).astype(jnp.bfloat16)
    g_row = _select_mm(onehot, gc, 'bid,bjd->bij', 'bid,bij->bjd')
    diff = gc[:, :, :c] - g_row
    decay = jnp.where(tril, jnp.exp(jnp.where(tril, diff, 0.0)), 0.0)
    kb = k3 * b3
    l_mat = jnp.where(strict, _bmm(kb, k3, 'bid,bjd->bij') * decay, 0.0)
    if t_saved is None:
        t_inv = _tri_inverse(l_mat, (ri == ci).astype(F32))
    else:
        t_inv = _tri_inverse_saved(l_mat, t_saved.reshape(nb, c, c))
    eg = jnp.exp(gc)
    u = _bmm(t_inv, v3 * b3, 'bij,bje->bie')
    w = _bmm(t_inv, kb * eg, 'bij,bje->bie')
    intra = jnp.where(tril, _bmm(q3, k3, 'bid,bjd->bij') * decay, 0.0)
    qd = q3 * eg
    gl = jnp.sum(g3, axis=1, keepdims=True)
    kt = k3 * jnp.exp(gl - gc)
    outs = (u.reshape(r, LANE), w.reshape(r, LANE), qd.reshape(r, LANE), kt.reshape(r, LANE),
            intra.reshape(r, c), gl.reshape(nb, LANE))
    if t_saved is not None:
        return outs
    qd2 = qd - _bmm(intra, w, 'bij,bjd->bid')
    au = _bmm(intra, u, 'bij,bje->bie')
    return outs + (t_inv.reshape(r, c), qd2.reshape(r, LANE), au.reshape(r, LANE))


def _pa_specs(t, rows):
    rr = min(rows, t)
    nb = rr // DN_CHUNK
    qkv = [pl.BlockSpec((rr, LANE), functools.partial(lambda i, h, o: (i, o + h), o=o)) for o in (0, HEADS, 2 * HEADS)]
    ba = pl.BlockSpec((rr, LANE), lambda i, h: (i, 3))
    vec = pl.BlockSpec((1, LANE), lambda i, h: (0, 0))
    row = pl.BlockSpec((rr, LANE), lambda i, h: (i, h))
    intra = pl.BlockSpec((None, rr, DN_CHUNK), lambda i, h: (h, i, 0))
    gl = pl.BlockSpec((nb, LANE), lambda i, h: (i, h))
    return rr, qkv, ba, vec, row, intra, gl


def _grid_ends(grid):
    first = lambda: functools.reduce(jnp.logical_and, [pl.program_id(a) == 0 for a in range(len(grid))])
    last = lambda: functools.reduce(jnp.logical_and, [pl.program_id(a) == n - 1 for a, n in enumerate(grid)])
    return first, last


def _delta_local(qkv_act, pm, alog, dtb, rider=None):
    t = qkv_act.shape[0]
    rr, qkv, ba, vec, row, intra, gl = _pa_specs(t, _PA_ROWS_FWD)

    def body(q, k, v, b, al, dt, *outs):
        vals = _phase_a(pl.program_id(1), q[...], k[...], v[...], b[...], al[...], dt[...])
        for o, val in zip(outs, vals):
            o[...] = val

    wide = jax.ShapeDtypeStruct((t, HEADS * LANE), F32)
    sq = jax.ShapeDtypeStruct((HEADS, t, DN_CHUNK), F32)
    grid = (t // rr, HEADS)
    return _carried_call(
        body, rider, *_grid_ends(grid), name="delta_local", grid=grid, in_specs=qkv + [ba, vec, vec],
        out_specs=[row] * 4 + [intra, gl, intra, row, row],
        out_shape=[wide] * 4 + [sq, jax.ShapeDtypeStruct((t // DN_CHUNK, HEADS * LANE), F32), sq, wide, wide],
        scratch_shapes=[], sem=("arbitrary", "arbitrary"), args=(qkv_act, qkv_act, qkv_act, pm, alog, dtb))


def _delta_local_bwd(qkv_act, pm, alog, dtb, t_inv, du, dw, dqd, dkt, dintra, dgl, rider=None):
    t = qkv_act.shape[0]
    rr, qkv, ba, vec, row, intra, gl = _pa_specs(t, _PA_ROWS)

    def body(q, k, v, b, al, dt, ti, du_r, dw_r, dqd_r, dkt_r, di_r, dgl_r, dq_o, dk_o, dv_o, dba_o, dal_o, ddt_o):
        i, h = pl.program_id(0), pl.program_id(1)
        t_saved = ti[...]
        _, vjp = jax.vjp(lambda *a: _phase_a(h, *a, t_saved=t_saved), q[...], k[...], v[...], b[...], al[...], dt[...])
        dq, dk, dv, dba, dal, ddt = vjp((du_r[...], dw_r[...], dqd_r[...], dkt_r[...], di_r[...], dgl_r[...]))
        dq_o[...], dk_o[...], dv_o[...] = dq, dk, dv

        @pl.when(h == 0)
        def _():
            dba_o[...] = jnp.zeros_like(dba_o)

        @pl.when((h == 0) & (i == 0))
        def _():
            dal_o[...] = jnp.zeros_like(dal_o)
            ddt_o[...] = jnp.zeros_like(ddt_o)

        dba_o[...] += dba
        dal_o[...] += dal
        ddt_o[...] += ddt

    wide = jax.ShapeDtypeStruct((t, HEADS * LANE), F32)
    vshape = jax.ShapeDtypeStruct((1, LANE), F32)
    grid = (t // rr, HEADS)
    return _carried_call(
        body, rider, *_grid_ends(grid), name="delta_local_bwd", grid=grid,
        in_specs=qkv + [ba, vec, vec, intra] + [row] * 4 + [intra, gl],
        out_specs=[row] * 3 + [pl.BlockSpec((rr, LANE), lambda i, h: (i, 0)), vec, vec],
        out_shape=[wide] * 3 + [jax.ShapeDtypeStruct((t, LANE), F32), vshape, vshape],
        scratch_shapes=[], sem=("arbitrary", "arbitrary"),
        args=(qkv_act, qkv_act, qkv_act, pm, alog, dtb, t_inv, du, dw, dqd, dkt, dintra, dgl))


_SCAN_ROWS = 512


def _dot(a, b, dn):
    return lax.dot_general(a.astype(_CDT), b.astype(_CDT), (dn, ((), ())), preferred_element_type=F32)


_NN = ((1,), (0,))
_NT = ((1,), (1,))
_TN = ((0,), (0,))


def _delta_scan(u, w, qd, kt, au, gl, rider=None):
    t = u.shape[0]
    rr = min(_SCAN_ROWS, t)
    nc = rr // DN_CHUNK

    def body(u_ref, w_ref, qd_ref, kt_ref, au_ref, gl_ref, o_ref, sall_ref, s_scr):
        @pl.when(pl.program_id(0) == 0)
        def _():
            s_scr[...] = jnp.zeros_like(s_scr)

        def chunk(c, carry):
            r0 = pl.multiple_of(c * DN_CHUNK, DN_CHUNK)
            rows = pl.ds(r0, DN_CHUNK)
            e = jnp.exp(gl_ref[pl.ds(c, 1), :])
            states = [s_scr[h] for h in range(HEADS)]
            u_c, w_c, qd_c, kt_c, au_c = u_ref[rows, :], w_ref[rows, :], qd_ref[rows, :], kt_ref[rows, :], au_ref[rows, :]
            o_new, s_new = [], []
            for h in range(HEADS):
                cs = slice(h * LANE, (h + 1) * LANE)
                s = states[h]
                both = _dot(jnp.concatenate([w_c[:, cs], qd_c[:, cs]], axis=0), s, _NN)
                v_new = u_c[:, cs] - both[:DN_CHUNK]
                o_new.append(both[DN_CHUNK:] + au_c[:, cs])
                s_new.append(s * e[:, cs] + _dot(kt_c[:, cs], v_new, _TN))
            o_ref[rows, :] = jnp.concatenate(o_new, axis=1)
            for h in range(HEADS):
                sall_ref[c, h] = states[h]
                s_scr[h] = s_new[h]
            return carry

        lax.fori_loop(0, nc, chunk, 0)

    row = pl.BlockSpec((rr, HEADS * LANE), lambda i: (i, 0))
    grid = (t // rr,)
    return _carried_call(
        body, rider, *_grid_ends(grid), name="delta_scan", grid=grid,
        in_specs=[row] * 5 + [pl.BlockSpec((nc, HEADS * LANE), lambda i: (i, 0))],
        out_specs=[row, pl.BlockSpec((nc, HEADS, LANE, LANE), lambda i: (i, 0, 0, 0))],
        out_shape=[jax.ShapeDtypeStruct((t, HEADS * LANE), F32),
                   jax.ShapeDtypeStruct((t // DN_CHUNK, HEADS, LANE, LANE), F32)],
        scratch_shapes=[pltpu.VMEM((HEADS, LANE, LANE), F32)], sem=("arbitrary",), args=(u, w, qd, kt, au, gl))


def _delta_scan_bwd(u, w, qd, kt, intra, gl, sall, do, rider=None):
    t = u.shape[0]
    rr = min(_SCAN_ROWS, t)
    nc = rr // DN_CHUNK
    ng = t // rr

    def body(u_ref, w_ref, qd_ref, kt_ref, a_ref, gl_ref, sall_ref, do_ref,
             du_ref, dw_ref, dqd_ref, dkt_ref, da_ref, dgl_ref, ds_scr):
        @pl.when(pl.program_id(0) == 0)
        def _():
            ds_scr[...] = jnp.zeros_like(ds_scr)

        def chunk(cc, carry):
            c = nc - 1 - cc
            r0 = pl.multiple_of(c * DN_CHUNK, DN_CHUNK)
            rows = pl.ds(r0, DN_CHUNK)
            e = jnp.exp(gl_ref[pl.ds(c, 1), :])
            states = [sall_ref[c, h] for h in range(HEADS)]
            ds_outs = [ds_scr[h] for h in range(HEADS)]
            u_a, w_a, kt_a, qd_a, do_a = u_ref[rows, :], w_ref[rows, :], kt_ref[rows, :], qd_ref[rows, :], do_ref[rows, :]
            a_a = [a_ref[h, rows, :] for h in range(HEADS)]
            da, dqd, dkt, du, dw, dgl, ds_new = [], [], [], [], [], [], []
            for h in range(HEADS):
                cs = slice(h * LANE, (h + 1) * LANE)
                s, ds_out = states[h], ds_outs[h]
                w_c, kt_c, qd_c, do_c = w_a[:, cs], kt_a[:, cs], qd_a[:, cs], do_a[:, cs]
                v_new = u_a[:, cs] - _dot(w_c, s, _NN)
                dv_new = _dot(a_a[h], do_c, _TN) + _dot(kt_c, ds_out, _NN)
                cots = jnp.concatenate([do_c, dv_new], axis=0)
                both = _dot(cots, s, _NT)
                dqd.append(both[:DN_CHUNK])
                dw.append(-both[DN_CHUNK:])
                da.append(_dot(do_c, v_new, _NT))
                dkt.append(_dot(v_new, ds_out, _NT))
                du.append(dv_new)
                eh = e[:, cs]
                dgl.append(jnp.broadcast_to(jnp.sum(ds_out * s, axis=0, keepdims=True) * eh, (8, LANE)))
                ds_new.append(ds_out * eh + _dot(jnp.concatenate([qd_c, -w_c], axis=0), cots, _TN))
            cat = lambda parts: jnp.concatenate(parts, axis=1)
            dqd_ref[rows, :], dkt_ref[rows, :], du_ref[rows, :], dw_ref[rows, :] = cat(dqd), cat(dkt), cat(du), cat(dw)
            dgl_ref[pl.ds(pl.multiple_of(c * 8, 8), 8), :] = cat(dgl)
            for h in range(HEADS):
                da_ref[h, rows, :] = da[h]
                ds_scr[h] = ds_new[h]
            return carry

        lax.fori_loop(0, nc, chunk, 0)

    rev = lambda i: (ng - 1 - i, 0)
    row = pl.BlockSpec((rr, HEADS * LANE), rev)
    a_spec = pl.BlockSpec((HEADS, rr, DN_CHUNK), lambda i: (0, ng - 1 - i, 0))
    gl_spec = pl.BlockSpec((nc, HEADS * LANE), rev)
    wide = jax.ShapeDtypeStruct((t, HEADS * LANE), F32)
    outs, carried = _carried_call(
        body, rider, *_grid_ends((ng,)), name="delta_scan_bwd", grid=(ng,),
        in_specs=[row] * 4 + [a_spec, gl_spec, pl.BlockSpec((nc, HEADS, LANE, LANE), lambda i: (ng - 1 - i, 0, 0, 0)), row],
        out_specs=[row] * 4 + [a_spec, pl.BlockSpec((nc * 8, HEADS * LANE), rev)],
        out_shape=[wide] * 4 + [jax.ShapeDtypeStruct((HEADS, t, DN_CHUNK), F32),
                                jax.ShapeDtypeStruct((t // DN_CHUNK * 8, HEADS * LANE), F32)],
        scratch_shapes=[pltpu.VMEM((HEADS, LANE, LANE), F32)], sem=("arbitrary",), args=(u, w, qd, kt, intra, gl, sall, do))
    return tuple(outs[:5]) + (outs[5].reshape(t // DN_CHUNK, 8, HEADS * LANE)[:, 0, :],), carried


_ATT_TILE = 512
_ATT_FWD_HEADS = 4
_ATT_BWD_HEADS = 4


def _kv_rows(j, tk):
    return pl.ds(pl.multiple_of(j * tk, tk), tk)


def _att_scores(ql, qr, ckv_ref, kr_ref, j, tk):
    ks = _kv_rows(j, tk)
    return (_dot(ql, ckv_ref[ks, :], _NT) + _dot(qr, kr_ref[ks, :], _NT)) * ATT_SCALE


def _diag_mask(s):
    qi = lax.broadcasted_iota(jnp.int32, s.shape, 0) % s.shape[1]
    ki = lax.broadcasted_iota(jnp.int32, s.shape, 1)
    return jnp.where(ki <= qi, s, NEG_BIG)


def _attention(qn, qr_pre, cosb, sinb, ckv, kr, wuk, wuv):
    t = ckv.shape[0]
    tq = min(_ATT_TILE, t)
    hp = _ATT_FWD_HEADS
    nl = tq // LANE

    def lane_fold(v, op):
        out = v[:, :LANE]
        for k in range(1, nl):
            out = op(out, v[:, k * LANE:(k + 1) * LANE])
        return out

    def body(qn_ref, qr_ref, cos_ref, sin_ref, ckv_ref, kr_ref, wuk_ref, wuv_ref, o_ref, lse_ref, qrope_ref, omla_ref,
             s_all, m_lanes, l_lanes, acc_scr):
        hg, qi = pl.program_id(0), pl.program_id(1)
        lanes = lambda ref, k: ref[:, k * LANE:(k + 1) * LANE]
        rows = lambda parts: jnp.concatenate(parts, axis=0)
        part = lambda v, k: v[k * tq:(k + 1) * tq]
        q_lat = rows([_dot(lanes(qn_ref, k), wuk_ref[hg * hp + k], _NT) for k in range(hp)]).astype(_CDT)
        roped = [_rope(lanes(qr_ref, k), cos_ref[...], sin_ref[...]).astype(qrope_ref.dtype) for k in range(hp)]
        qrope_ref[...] = jnp.concatenate(roped, axis=1)
        q_rope = rows(roped)
        m_lanes[...] = jnp.full_like(m_lanes, NEG_BIG)

        def scores(j, masked):
            s = _att_scores(q_lat, q_rope, ckv_ref, kr_ref, j, tq)
            if masked:
                s = _diag_mask(s)
            s_all[j] = s
            m_lanes[...] = jnp.maximum(m_lanes[...], lane_fold(s, jnp.maximum))

        def scores_body(j, carry):
            scores(j, False)
            return carry

        lax.fori_loop(0, qi, scores_body, 0)
        scores(qi, True)
        m = jnp.max(m_lanes[...], axis=-1, keepdims=True)
        mb = jnp.broadcast_to(m, (hp * tq, LANE))
        l_lanes[...] = jnp.zeros_like(l_lanes)
        acc_scr[...] = jnp.zeros_like(acc_scr)

        def weigh(j, carry):
            s = s_all[j]
            p = jnp.concatenate([jnp.exp(s[:, k * LANE:(k + 1) * LANE] - mb) for k in range(nl)], axis=1)
            l_lanes[...] += lane_fold(p, jnp.add)
            acc_scr[...] += _dot(p, ckv_ref[_kv_rows(j, tq), :], _NN)
            return carry

        lax.fori_loop(0, qi + 1, weigh, 0)
        l = jnp.sum(l_lanes[...], axis=-1, keepdims=True)
        out = acc_scr[...] / l
        lse_v = m + jnp.log(l)
        for k in range(hp):
            o_ref[k] = part(out, k)
            lse_ref[k] = part(lse_v, k)
        omla_ref[...] = jnp.concatenate(
            [_dot(part(out, k), wuv_ref[hg * hp + k], _NN) for k in range(hp)], axis=1).astype(omla_ref.dtype)

    col = pl.BlockSpec((tq, hp * LANE), lambda h, i: (i, h))
    table = pl.BlockSpec((tq, ROPE_PAD), lambda h, i: (i, 0))
    wspec = pl.BlockSpec((HEADS, KV_LORA, NOPE), lambda h, i: (0, 0, 0))
    return pl.pallas_call(
        body, name="attention", grid=(HEADS // hp, t // tq),
        in_specs=[col, col, table, table, pl.BlockSpec((t, KV_LORA), lambda h, i: (0, 0)),
                  pl.BlockSpec((t, ROPE_PAD), lambda h, i: (0, 0)), wspec, wspec],
        out_specs=[pl.BlockSpec((hp, tq, KV_LORA), lambda h, i: (h, i, 0)),
                   pl.BlockSpec((hp, tq, 1), lambda h, i: (h, i, 0)), col, col],
        out_shape=[jax.ShapeDtypeStruct((HEADS, t, KV_LORA), F32), jax.ShapeDtypeStruct((HEADS, t, 1), F32),
                   jax.ShapeDtypeStruct((t, HEADS * ROPE_PAD), _CDT), jax.ShapeDtypeStruct((t, HEADS * NOPE), _CDT)],
        scratch_shapes=[pltpu.VMEM((t // tq, hp * tq, tq), F32), pltpu.VMEM((hp * tq, LANE), F32),
                        pltpu.VMEM((hp * tq, LANE), F32), pltpu.VMEM((hp * tq, KV_LORA), F32)],
        compiler_params=_cparams(("parallel", "parallel")))(qn, qr_pre, cosb, sinb, ckv, kr, wuk, wuv)


def _attention_bwd(qn, qr, cosb, sinb, ckv, kr, wuk, wuv, out, lse, do_mla):
    t = ckv.shape[0]
    tq = min(_ATT_TILE, t)
    hp = _ATT_BWD_HEADS

    def body(qn_ref, qr_ref, cos_ref, sin_ref, ckv_ref, kr_ref, wuk_ref, wuv_ref, o_ref, lse_ref, do_ref,
             dql_ref, dqr_ref, dqn_ref, dckv_ref, dkr_ref, dql_scr, dqr_scr):
        hg, qi = pl.program_id(0), pl.program_id(1)

        @pl.when((hg == 0) & (qi == 0))
        def _():
            dckv_ref[...] = jnp.zeros_like(dckv_ref)
            dkr_ref[...] = jnp.zeros_like(dkr_ref)

        lanes = lambda ref, k: ref[:, k * LANE:(k + 1) * LANE]
        rows = lambda parts: jnp.concatenate(parts, axis=0)
        q_lat = rows([_dot(lanes(qn_ref, k), wuk_ref[hg * hp + k], _NT) for k in range(hp)]).astype(_CDT)
        q_rope = rows([lanes(qr_ref, k) for k in range(hp)])
        d_out = rows([_dot(lanes(do_ref, k), wuv_ref[hg * hp + k], _NT) for k in range(hp)])
        d_o = d_out.astype(_CDT)
        lse_v = rows([lse_ref[k] for k in range(hp)])
        dsum = jnp.sum(d_out * rows([o_ref[k] for k in range(hp)]), axis=-1, keepdims=True)
        dql_scr[...] = jnp.zeros_like(dql_scr)
        dqr_scr[...] = jnp.zeros_like(dqr_scr)

        def step(j, masked):
            ks = _kv_rows(j, tq)
            s = _att_scores(q_lat, q_rope, ckv_ref, kr_ref, j, tq)
            if masked:
                s = _diag_mask(s)
            p = jnp.exp(s - lse_v)
            kv = ckv_ref[ks, :]
            ds = (p * (_dot(d_o, kv, _NT) - dsum) * ATT_SCALE).astype(_CDT)
            pb = p.astype(_CDT)
            dql_scr[...] += _dot(ds, kv, _NN)
            dqr_scr[...] += _dot(ds, kr_ref[ks, :], _NN)
            dckv_ref[ks, :] += _dot(pb, d_o, _TN) + _dot(ds, q_lat, _TN)
            dkr_ref[ks, :] += _dot(ds, q_rope, _TN)

        def loop_body(j, carry):
            step(j, False)
            return carry

        lax.fori_loop(0, qi, loop_body, 0)
        step(qi, True)
        cols = lambda parts: jnp.concatenate(parts, axis=1)
        part = lambda v, k: v[k * tq:(k + 1) * tq]
        dql = dql_scr[...].astype(dql_ref.dtype)
        dqr = dqr_scr[...]
        for k in range(hp):
            dql_ref[k] = part(dql, k)
        dqn_ref[...] = cols([_dot(part(dql, k), wuk_ref[hg * hp + k], _NN) for k in range(hp)]).astype(dqn_ref.dtype)
        dqr_ref[...] = cols([_rope_bwd(part(dqr, k), cos_ref[...], sin_ref[...]) for k in range(hp)]).astype(dqr_ref.dtype)

    lat = pl.BlockSpec((hp, tq, KV_LORA), lambda h, i: (h, i, 0))
    col = pl.BlockSpec((tq, hp * LANE), lambda h, i: (i, h))
    table = pl.BlockSpec((tq, ROPE_PAD), lambda h, i: (i, 0))
    kfull = pl.BlockSpec((t, KV_LORA), lambda h, i: (0, 0))
    rfull = pl.BlockSpec((t, ROPE_PAD), lambda h, i: (0, 0))
    wspec = pl.BlockSpec((HEADS, KV_LORA, NOPE), lambda h, i: (0, 0, 0))
    wide = jax.ShapeDtypeStruct((t, HEADS * LANE), _CDT)
    return pl.pallas_call(
        body, name="attention_bwd", grid=(HEADS // hp, t // tq),
        in_specs=[col, col, table, table, kfull, rfull, wspec, wspec, lat,
                  pl.BlockSpec((hp, tq, 1), lambda h, i: (h, i, 0)), col],
        out_specs=[lat, col, col, kfull, rfull],
        out_shape=[jax.ShapeDtypeStruct((HEADS, t, KV_LORA), _CDT), wide, wide,
                   jax.ShapeDtypeStruct((t, KV_LORA), F32), jax.ShapeDtypeStruct((t, ROPE_PAD), F32)],
        scratch_shapes=[pltpu.VMEM((hp * tq, KV_LORA), F32), pltpu.VMEM((hp * tq, ROPE_PAD), F32)],
        compiler_params=_cparams(("arbitrary", "arbitrary")))(qn, qr, cosb, sinb, ckv, kr, wuk, wuv, out, lse, do_mla)


_DX_ROWS = 512


def _dx_fused(pairs, add, add_scale, rider=None, name="b_dx"):
    t, d = add.shape
    tm = min(_DX_ROWS, t)
    n = len(pairs)

    def body(*refs):
        acc = refs[2 * n][...] * add_scale
        for i in range(n):
            acc = acc + _dot(refs[i][...], refs[n + i][...], _NT)
        refs[2 * n + 1][...] = acc

    in_specs = [pl.BlockSpec((tm, a.shape[1]), lambda i: (i, 0)) for a, _ in pairs]
    in_specs += [pl.BlockSpec(w.shape, lambda i: (0, 0), pipeline_mode=pl.Buffered(1)) for _, w in pairs]
    row = pl.BlockSpec((tm, d), lambda i: (i, 0))
    grid = (t // tm,)
    (dx,), carried = _carried_call(
        body, rider, *_grid_ends(grid), name=name, grid=grid, in_specs=in_specs + [row], out_specs=[row],
        out_shape=[jax.ShapeDtypeStruct((t, d), F32)], scratch_shapes=[], sem=("arbitrary",),
        args=tuple(a for a, _ in pairs) + tuple(w for _, w in pairs) + (add,))
    return dx, carried


def _ffn_in_swiglu(a, w):
    t, k = a.shape
    hid = w.shape[1] // 2
    tm, tn = _tile(t, 1024), _tile(hid, 1408)
    nj = hid // tn

    def body(a_ref, bg_ref, bu_ref, act_ref, gt_ref, up_ref):
        av = a_ref[...].astype(_CDT)
        gt = jnp.dot(av, bg_ref[...].astype(_CDT), preferred_element_type=F32)
        up = jnp.dot(av, bu_ref[...].astype(_CDT), preferred_element_type=F32)
        act_ref[...] = _swiglu(gt, up).astype(act_ref.dtype)
        gt_ref[...] = gt.astype(gt_ref.dtype)
        up_ref[...] = up.astype(up_ref.dtype)

    out = pl.BlockSpec((tm, tn), lambda i, j: (i, j))
    return pl.pallas_call(
        body, name="f_ffn_in_swiglu", grid=(t // tm, nj),
        in_specs=[pl.BlockSpec((tm, k), lambda i, j: (i, 0)), pl.BlockSpec((k, tn), lambda i, j: (0, j)),
                  pl.BlockSpec((k, tn), lambda i, j: (0, nj + j))],
        out_specs=[out] * 3, out_shape=[jax.ShapeDtypeStruct((t, hid), _CDT)] * 3,
        compiler_params=_cparams(("parallel", "parallel")))(a, w, w)


def _gated_norm(o, z, w):
    return _rms_norm(o, w) * _silu(z)


def _gated_norm_heads(o, z, w):
    heads = [_gated_norm(o[:, h * LANE:(h + 1) * LANE], z[:, h * LANE:(h + 1) * LANE], w) for h in range(HEADS)]
    return jnp.concatenate(heads, axis=1)


def _mla_pre(ckv, krp, cq, cosb, sinb, qw, kw):
    return _rms_norm(cq, qw), _rms_norm(ckv, kw), _rope(krp, cosb, sinb)


def _merge(gg, y_dn, y_mla):
    return _sigmoid(gg[:, :D_MODEL]) * y_dn + _sigmoid(gg[:, D_MODEL:]) * y_mla


def _ln1(xv, attn_out, g, b):
    return _layer_norm(ALPHA * xv + attn_out, g, b)


def _final(h1, ffn, gate_pre, ple_proj, g, b):
    return _layer_norm(ALPHA * h1 + ffn + _sigmoid(gate_pre) * ple_proj, g, b)


def _swiglu(gt, up):
    return _silu(gt) * up


def _local_step(x, p, cosb, sinb, target, wt, sp, exch):
    t = x.shape[0]
    bf = _CDT
    xb = x.astype(bf)
    g = {}

    qkv_pre = _mm2(xb, wt['qkv'], name="f_qkv")
    z = _mm2(xb, wt['z'], name="f_z")
    gg = _mm2(xb, wt['gg'], name="f_gg")
    pm = _mm2(xb, wt['mla'], name="f_mla")
    qkv_act = _conv_silu(qkv_pre, sp['conv_w'])
    (u, w_, qd, kt, intra, gl, t_inv, qd2, au), sent = _delta_local(qkv_act, pm, sp['a_log'], sp['dt_bias'],
                                                                    rider=exch.gather_send())
    (o_dn, sall), passed = _delta_scan(u, w_, qd2, kt, au, gl, rider=exch.gather_pass(sent))
    wt = dict(wt, **exch.weights(passed))
    def gated_norm_br(h, o, zz, w, wbr):
        og_v = _gated_norm_heads(o, zz, w).astype(bf)
        return og_v, jnp.dot(og_v, wbr.astype(bf), preferred_element_type=F32)

    og, y_dn = _rowwise(gated_norm_br, [(o_dn, D_MODEL, 0, False), (z, D_MODEL, 0, False)],
                        [sp['dn_norm_w'], wt['br_dn']],
                        [(D_MODEL, D_MODEL, False, bf), (D_MODEL, D_MODEL, False, F32)], name="f_gated_norm_br", tm=512)

    def mla_pre_uq(h, ckv, krp, cq, cosv, sinv, qw, kw, wn, wr):
        c_q_v, c_kv_v, k_rope_v = _mla_pre(ckv, krp, cq, cosv, sinv, qw, kw)
        c_q_b = c_q_v.astype(bf)
        return (c_q_b, c_kv_v, k_rope_v, jnp.dot(c_q_b, wn.astype(bf), preferred_element_type=F32),
                jnp.dot(c_q_b, wr.astype(bf), preferred_element_type=F32))

    c_q, c_kv, k_rope, q_nope, q_rope_pre = _rowwise(
        mla_pre_uq,
        [(pm, KV_LORA, 0, False), (pm, ROPE_PAD, 2, False), (pm, Q_LORA, 2, False),
         (cosb, ROPE_PAD, 0, False), (sinb, ROPE_PAD, 0, False)],
        [sp['q_norm_w'], sp['kv_norm_w'], wt['uq_nope'], wt['uq_rope']],
        [(Q_LORA, Q_LORA, False, bf), (KV_LORA, KV_LORA, False, bf), (ROPE_PAD, ROPE_PAD, False, bf),
         (HEADS * NOPE, HEADS * NOPE, False, bf), (HEADS * ROPE_PAD, HEADS * ROPE_PAD, False, F32)], name="f_mla_pre_uq",
        tm=512)
    out_lat, lse, q_rope, o_mla = _attention(q_nope, q_rope_pre, cosb, sinb, c_kv, k_rope, wt['uk'], wt['uv'])
    y_mla = _mm2(o_mla, wt['br_mla'], name="f_br_mla")

    def merge_o_ln1(h, ggv, yd, ym, xv, wo, gv, bv):
        mixed_v = _merge(ggv, yd, ym).astype(bf)
        ao = jnp.dot(mixed_v, wo.astype(bf), preferred_element_type=F32)
        h1v = _ln1(xv, ao, gv, bv)
        return mixed_v, ao, h1v, h1v

    mixed, attn_out, h1, h1b = _rowwise(
        merge_o_ln1, [(gg, 2 * D_MODEL, 0, False), (y_dn, D_MODEL, 0, False), (y_mla, D_MODEL, 0, False), (x, D_MODEL, 0, False)],
        [wt['o'], sp['ln1_g'], sp['ln1_b']],
        [(D_MODEL, D_MODEL, False, bf), (D_MODEL, D_MODEL, False, F32), (D_MODEL, D_MODEL, False, F32),
         (D_MODEL, D_MODEL, False, bf)], name="f_merge_o_ln1", tm=512)
    act, ffn_gt, ffn_up = _ffn_in_swiglu(h1b, wt['ffn_in'])
    pb = p.astype(bf)

    def final_fn(h, h1v, actv, pv, tgt, gt, up, wfo, wpg, wpl, gv, bv):
        ffnv = jnp.dot(actv.astype(bf), wfo.astype(bf), preferred_element_type=F32)
        gpv = jnp.dot(h1v.astype(bf), wpg.astype(bf), preferred_element_type=F32)
        ppv = jnp.dot(pv.astype(bf), wpl.astype(bf), preferred_element_type=F32)
        y, vjp = jax.vjp(_final, h1v, ffnv, gpv, ppv, gv, bv)
        err = y - tgt
        dh1, dffn, dgp, dpp, dg, db = vjp(err * (1.0 / D_MODEL))
        sq = err * err
        lanes = sq[:, :LANE]
        for j in range(1, D_MODEL // LANE):
            lanes = lanes + sq[:, j * LANE:(j + 1) * LANE]
        loss = jnp.sum(lanes, axis=0, keepdims=True) * (0.5 / D_MODEL)
        dffn_b = dffn.astype(bf)
        dact = _dot(dffn_b, wfo, _NT).astype(bf).astype(F32)
        _, vjp_s = jax.vjp(_swiglu, gt.astype(F32), up.astype(F32))
        dgt, dup = vjp_s(dact)
        return dffn, dffn_b, dgp, dpp, jnp.concatenate([dgt, dup], axis=1), dg, db, loss

    dpre2, dpre2b, dgate_pre, dple_proj, dffn_in, g['ln2_g'], g['ln2_b'], loss_lanes = _rowwise(
        final_fn, [(h1, D_MODEL, 0, False), (act, FFN_HIDDEN, 0, False), (pb, PLE_DIM, 0, False), (target, D_MODEL, 0, False),
                   (ffn_gt, FFN_HIDDEN, 0, False), (ffn_up, FFN_HIDDEN, 0, False)],
        [wt['ffn_out'], wt['ple_gate'], wt['ple'], sp['ln2_g'], sp['ln2_b']],
        [(D_MODEL, D_MODEL, False, F32)] + [(D_MODEL, D_MODEL, False, bf)] * 3 + [(2 * FFN_HIDDEN, 2 * FFN_HIDDEN, False, bf)],
        [(1, D_MODEL), (1, D_MODEL), (1, LANE)], name="b_final")
    g['loss_lanes'] = loss_lanes

    g['ple'] = _mm2(pb, dple_proj, ta=True, name="g_ple")
    g['ple_gate'] = _mm2(h1b, dgate_pre, ta=True, name="g_ple_gate")
    g['ffn_out'] = _mm2(act, dpre2b, ta=True, name="g_ffn_out")
    g['ffn_in'] = _mm2(h1b, dffn_in, ta=True, name="g_ffn_in")
    dh1, _ = _dx_fused([(dffn_in, wt['ffn_in']), (dgate_pre, wt['ple_gate'])], dpre2, ALPHA, name="b_dh1")

    def attn_out_bwd(h, xv, ao, d, ggv, yd, ym, o, zz, wo, wbd, wbm, gv, bv, nw):
        _, vjp = jax.vjp(_ln1, xv, ao, gv, bv)
        _, dao, dg, db = vjp(d)
        dao_b = dao.astype(bf)
        _, vjp_m = jax.vjp(_merge, ggv, yd, ym)
        dggv, dyd, dym = vjp_m(_dot(dao_b, wo, _NT))
        dyd_b, dym_b = dyd.astype(bf), dym.astype(bf)
        _, vjp_n = jax.vjp(_gated_norm_heads, o, zz, nw)
        do_v, dz_v, dnw = vjp_n(_dot(dyd_b, wbd, _NT))
        return dao, dao_b, dggv, dyd_b, dym_b, do_v, dz_v, _dot(dym_b, wbm, _NT), dg, db, dnw

    row_d = lambda a: (a, D_MODEL, 0, False)
    dpre1, dpre1b, dgg, dy_dn, dy_mla, do_dn, dz, do_mla, g['ln1_g'], g['ln1_b'], g['dn_norm_w'] = _rowwise(
        attn_out_bwd, [row_d(x), row_d(attn_out), row_d(dh1), (gg, 2 * D_MODEL, 0, False), row_d(y_dn), row_d(y_mla),
                       row_d(o_dn), row_d(z)],
        [wt['o'], wt['br_dn'], wt['br_mla'], sp['ln1_g'], sp['ln1_b'], sp['dn_norm_w']],
        [(D_MODEL, D_MODEL, False, F32), (D_MODEL, D_MODEL, False, bf), (2 * D_MODEL, 2 * D_MODEL, False, bf),
         (D_MODEL, D_MODEL, False, bf), (D_MODEL, D_MODEL, False, bf), (D_MODEL, D_MODEL, False, F32),
         (D_MODEL, D_MODEL, False, bf), (D_MODEL, D_MODEL, False, bf)],
        [(1, D_MODEL), (1, D_MODEL), (1, LANE)], name="b_attn_out")
    g['o'] = _mm2(mixed, dpre1b, ta=True, name="g_o")
    g['br_dn'] = _mm2(og, dy_dn, ta=True, name="g_br_dn")
    g['br_mla'] = _mm2(o_mla, dy_mla, ta=True, name="g_br_mla")

    g['uv'] = _mm(out_lat, do_mla, name="g_uv", ta=True, heads=HEADS, a_head='lead', b_head='col', out_head='lead',
                  dims=(KV_LORA, NOPE, t))
    dq_lat, dq_rope_pre, dq_nope, dckv_att, dkr_att = _attention_bwd(
        q_nope, q_rope, cosb, sinb, c_kv, k_rope, wt['uk'], wt['uv'], out_lat, lse, do_mla)
    g['uk'] = _mm(dq_lat, q_nope, name="g_uk", ta=True, heads=HEADS, a_head='lead', b_head='col', out_head='lead',
                  dims=(KV_LORA, NOPE, t))
    g['uq_nope'] = _mm2(c_q, dq_nope, ta=True, name="g_uq_nope")
    g['uq_rope'] = _mm2(c_q, dq_rope_pre, ta=True, name="g_uq_rope")
    dc_q = _mm2(dq_nope, wt['uq_nope'], tb=True, name="b_dcq_nope")
    dc_q = _mm2(dq_rope_pre, wt['uq_rope'], tb=True, name="b_dcq_rope", add=dc_q)

    (du, dw, dqd, dkt, dintra, dgl), paired = _delta_scan_bwd(u, w_, qd, kt, intra, gl, sall, do_dn, rider=exch.pair_send(g))
    (dq_a, dk_a, dv_a, dba, g['a_log'], g['dt_bias']), arrived = _delta_local_bwd(
        qkv_act, pm, sp['a_log'], sp['dt_bias'], t_inv, du, dw, dqd, dkt, dintra, dgl, rider=exch.reduce_send(paired))
    exch.reduce_arrived(arrived)
    dqkv_pre, g['conv_w'] = _conv_silu_bwd(qkv_pre, sp['conv_w'], [dq_a, dk_a, dv_a])

    def mla_pre_bwd(h, ckv, cq, cosv, sinv, dcq, dckv, dkr, dba_v, qw, kw):
        _, vjp = jax.vjp(lambda a, c, d, e: (_rms_norm(c, d), _rms_norm(a, e)), ckv, cq, qw, kw)
        dckv_p, dcq_p, dqw, dkw = vjp((dcq, dckv))
        dkr_p = _rope_bwd(dkr, cosv, sinv)
        dpm = jnp.concatenate([dckv_p, dkr_p, dba_v, jnp.zeros((ckv.shape[0], 2 * LANE), F32), dcq_p], axis=1)
        return dpm, dqw, dkw

    dpm, g['q_norm_w'], g['kv_norm_w'] = _rowwise(
        mla_pre_bwd,
        [(pm, KV_LORA, 0, False), (pm, Q_LORA, 2, False), (cosb, ROPE_PAD, 0, False), (sinb, ROPE_PAD, 0, False),
         (dc_q, Q_LORA, 0, False), (dckv_att, KV_LORA, 0, False), (dkr_att, ROPE_PAD, 0, False), (dba, LANE, 0, False)],
        [sp['q_norm_w'], sp['kv_norm_w']], [(1152, 1152, False, bf)], [(1, Q_LORA), (1, KV_LORA)], name="b_mla_pre")

    rider = exch.small_send(g)
    if rider is None:
        g['qkv'] = _mm2(xb, dqkv_pre, ta=True, name="g_qkv")
    else:
        g['qkv'], got = _mm2(xb, dqkv_pre, ta=True, name="g_qkv", rider=rider)
        exch.small_arrived(got)
    g['z'] = _mm2(xb, dz, ta=True, name="g_z")
    g['gg'] = _mm2(xb, dgg, ta=True, name="g_gg")
    g['mla'] = _mm2(xb, dpm, ta=True, name="g_mla")
    dx, arrived = _dx_fused([(dqkv_pre, wt['qkv']), (dz, wt['z']), (dgg, wt['gg']), (dpm, wt['mla'])], dpre1, ALPHA,
                            rider=exch.in_send(g))
    exch.in_arrived(arrived)
    return loss_lanes, dx, g


_IN_SIZES = (QKV_W, HEADS * DN_DK, HEADS, HEADS, Q_LORA, KV_LORA, ROPE, D_MODEL, D_MODEL)


def _rope_tables(positions):
    inv_freq = ROPE_BASE ** (-jnp.arange(0, ROPE, 2, dtype=F32) / ROPE)
    ang = positions.astype(F32)[:, None] * inv_freq
    cos, sin = jnp.cos(ang), jnp.sin(ang)
    zeros = jnp.zeros((positions.shape[0], ROPE_PAD - ROPE), F32)
    return jnp.concatenate([cos, cos, zeros], axis=1), jnp.concatenate([-sin, sin, zeros], axis=1)


def _prep_w_in(w_in):
    dt = w_in.dtype
    offs = [0]
    for s in _IN_SIZES:
        offs.append(offs[-1] + s)
    qkv, z, wb, wa, cq, ckv, kr, gd, gm = [w_in[:, offs[i]:offs[i + 1]] for i in range(len(_IN_SIZES))]
    zc = lambda n: jnp.zeros((D_MODEL, n), dt)
    return {
        'qkv': qkv, 'z': z, 'gg': jnp.concatenate([gd, gm], axis=1),
        'mla': jnp.concatenate([ckv, kr, zc(ROPE_PAD - ROPE), wb, wa, zc(LANE - 2 * HEADS), zc(2 * LANE), cq], axis=1),
    }


def _prep_weights(full):
    w_uq = full['w_uq']
    wt = {
        'uq_nope': w_uq[:, :, :NOPE].reshape(Q_LORA, HEADS * NOPE),
        'uq_rope': jnp.pad(w_uq[:, :, NOPE:], ((0, 0), (0, 0), (0, ROPE_PAD - ROPE))).reshape(Q_LORA, HEADS * ROPE_PAD),
        'uk': jnp.transpose(full['w_uk'], (1, 0, 2)), 'uv': jnp.transpose(full['w_uv'], (1, 0, 2)),
        'br_dn': full['w_br_dn'], 'br_mla': full['w_br_mla'], 'o': full['w_o'], 'ffn_in': full['w_ffn_in'],
        'ffn_out': full['w_ffn_out'], 'ple': full['w_ple'], 'ple_gate': full['w_ple_gate'],
    }
    return wt


def _prep_small(small):
    pad = lambda v: jnp.pad(v, (0, LANE - v.shape[0]))[None, :]
    return {
        'conv_w': small['conv_w'], 'a_log': pad(small['dn_a_log']), 'dt_bias': pad(small['dn_dt_bias']),
        'dn_norm_w': small['dn_norm_w'][None, :], 'q_norm_w': small['q_norm_w'][None, :],
        'kv_norm_w': small['kv_norm_w'][None, :], 'ln1_g': small['ln1_g'][None, :], 'ln1_b': small['ln1_b'][None, :],
        'ln2_g': small['ln2_g'][None, :], 'ln2_b': small['ln2_b'][None, :],
    }


def _w_in_grad(g):
    mla = g['mla']
    ba0 = KV_LORA + ROPE_PAD
    cq0 = ba0 + 3 * LANE
    return jnp.concatenate([
        g['qkv'], g['z'], mla[:, ba0:ba0 + HEADS], mla[:, ba0 + HEADS:ba0 + 2 * HEADS], mla[:, cq0:cq0 + Q_LORA],
        mla[:, :KV_LORA], mla[:, KV_LORA:KV_LORA + ROPE], g['gg']], axis=1)


def _unprep_grads_late(g):
    return {
        'conv_w': g['conv_w'], 'dn_a_log': g['a_log'][0, :HEADS], 'dn_dt_bias': g['dt_bias'][0, :HEADS],
        'dn_norm_w': g['dn_norm_w'][0], 'q_norm_w': g['q_norm_w'][0], 'kv_norm_w': g['kv_norm_w'][0],
        'ln1_g': g['ln1_g'][0], 'ln1_b': g['ln1_b'][0], 'ln2_g': g['ln2_g'][0], 'ln2_b': g['ln2_b'][0],
    }


def _unprep_grads_early(g):
    w_uq = jnp.concatenate([g['uq_nope'].reshape(Q_LORA, HEADS, NOPE),
                            g['uq_rope'].reshape(Q_LORA, HEADS, ROPE_PAD)[:, :, :ROPE]], axis=2)
    return {
        'w_uq': w_uq, 'w_uk': jnp.transpose(g['uk'], (1, 0, 2)), 'w_uv': jnp.transpose(g['uv'], (1, 0, 2)),
        'w_br_dn': g['br_dn'], 'w_br_mla': g['br_mla'], 'w_o': g['o'],
        'w_ffn_in': g['ffn_in'], 'w_ffn_out': g['ffn_out'], 'w_ple': g['ple'], 'w_ple_gate': g['ple_gate'],
    }


_FLATB_PIECES = (
    ('w_ffn_out', 704, (704, D_MODEL)), ('w_br_dn', 256, (256, D_MODEL)), ('w_br_mla', 256, (256, D_MODEL)),
    ('w_o', 256, (256, D_MODEL)), ('w_ple_gate', 256, (256, D_MODEL)), ('w_uq', 144, (96, HEADS, NOPE + ROPE)),
    ('w_uk', 64, (64, HEADS, NOPE)), ('w_uv', 64, (64, HEADS, NOPE)), ('w_ple', 64, (PLE_DIM, 256)),
)
FLATB_ROWS = 2112
W_IN_SHARD = D_IN // N_SHARD
FFN_IN_SHARD = 2 * FFN_HIDDEN // N_SHARD
A_ROWS = D_MODEL + 32
_CONV_SHARD = QKV_W // N_SHARD
_ADD_TILES = (256, 256, 352)


def _flatb_offsets():
    offs, o = {}, 0
    for name, rows, _ in _FLATB_PIECES:
        offs[name] = o
        o += rows
    return offs, o


def _pack_shards(ws, conv_w):
    conv_bits = lax.bitcast_convert_type(conv_w, jnp.bfloat16).reshape(DN_CONV, 2 * _CONV_SHARD).astype(_CDT)
    tail = jnp.pad(conv_bits, ((0, A_ROWS - D_MODEL - DN_CONV), (0, W_IN_SHARD - 2 * _CONV_SHARD)))
    a_buf = jnp.concatenate([ws['w_in'].astype(_CDT), tail], axis=0)
    parts = [ws[name].astype(_CDT).reshape(rows, FLAT_W) for name, rows, _ in _FLATB_PIECES]
    used = sum(p.shape[0] for p in parts)
    parts.append(jnp.zeros((FLATB_ROWS - used, FLAT_W), _CDT))
    return [a_buf, ws['w_ffn_in'].astype(_CDT), jnp.concatenate(parts, axis=0)]


def _unpack_w_in(gathered, local, me):
    a = [jnp.where(me == s, local, gathered[s]) for s in range(N_SHARD)]
    conv = [lax.bitcast_convert_type(
        p[D_MODEL:D_MODEL + DN_CONV, :2 * _CONV_SHARD].astype(jnp.bfloat16).reshape(DN_CONV, _CONV_SHARD, 2), F32) for p in a]
    return jnp.concatenate([p[:D_MODEL] for p in a], axis=1), jnp.concatenate(conv, axis=1)


def _unpack_rest(gathered, local, me):
    pick = lambda b, s: jnp.where(me == s, local[b], gathered[b][s])
    full = {'w_ffn_in': jnp.concatenate([pick(0, s) for s in range(N_SHARD)], axis=1)}
    offs, _ = _flatb_offsets()
    fb = [pick(1, s) for s in range(N_SHARD)]
    for name, rows, shape in _FLATB_PIECES:
        pieces = [p[offs[name]:offs[name] + rows].reshape(shape) for p in fb]
        full[name] = jnp.concatenate(pieces, axis=1 if name == 'w_ple' else 0)
    return full


def _shard_columns(g, w):
    return jnp.stack([g[:, s * w:(s + 1) * w] for s in range(N_SHARD)])


def _pack_grads_rest(gw):
    parts = []
    for name, rows, _ in _FLATB_PIECES:
        g = gw[name]
        if name == 'w_ple':
            parts.append(_shard_columns(g, PLE_DIM).reshape(N_SHARD, rows, FLAT_W))
        else:
            parts.append(g.reshape(N_SHARD, rows, FLAT_W))
    used = sum(p.shape[1] for p in parts)
    parts.append(jnp.zeros((N_SHARD, FLATB_ROWS - used, FLAT_W), F32))
    return [_shard_columns(gw['w_ffn_in'], FFN_IN_SHARD), jnp.concatenate(parts, axis=1)]


def _unpack_reduced(mine, theirs, c):
    whole = [jnp.concatenate([jnp.where(c == 0, m, t), jnp.where(c == 0, t, m)], axis=0) for m, t in zip(mine, theirs)]
    out = {'w_in': whole[0], 'w_ffn_in': whole[1]}
    offs, _ = _flatb_offsets()
    for name, rows, shape in _FLATB_PIECES:
        out[name] = whole[2][offs[name]:offs[name] + rows].reshape(shape)
    return out


_HBM = pl.BlockSpec(memory_space=pltpu.HBM)


def _place():
    x, y, c = lax.axis_index("x"), lax.axis_index("y"), lax.axis_index("c")
    chips = [(1 - x, y), (x, 1 - y), (1 - x, 1 - y)]
    return x, y, c, chips


def _remote(src, dst, send_sems, recv_sems, k, to):
    return pltpu.make_async_remote_copy(src_ref=src, dst_ref=dst, send_sem=send_sems.at[k], recv_sem=recv_sems.at[k],
                                        device_id=to, device_id_type=_MESH)


def _half_rows(ref, half, hf, lead=None):
    rows = pl.ds(pl.multiple_of(hf * half, 16), half)
    return ref.at[rows, :] if lead is None else ref.at[lead, rows, :]


class _Rider:
    def __init__(self, inputs, out_shape, n_sems, copies, aliases=None):
        self.inputs, self.out_shape, self.n_sems, self.copies = list(inputs), list(out_shape), n_sems, copies
        self.aliases = aliases or {}


def _carried_call(body, rider, first, last, *, name, grid, in_specs, out_specs, out_shape, scratch_shapes, sem, args):
    n_in, n_out, n_scr = len(in_specs), len(out_specs), len(scratch_shapes)
    if rider is None:
        res = pl.pallas_call(body, name=name, grid=grid, in_specs=in_specs, out_specs=out_specs, out_shape=out_shape,
                             scratch_shapes=scratch_shapes, compiler_params=_cparams(sem))(*args)
        return list(res), []
    ri, ro = len(rider.inputs), len(rider.out_shape)

    def full_body(*refs):
        own_in, r_in = refs[:n_in], refs[n_in:n_in + ri]
        o0 = n_in + ri
        own_out, r_out = refs[o0:o0 + n_out], refs[o0 + n_out:o0 + n_out + ro]
        s0 = o0 + n_out + ro
        own_scr, send_sems, recv_sems = refs[s0:s0 + n_scr], refs[s0 + n_scr], refs[s0 + n_scr + 1]

        @pl.when(first())
        def _():
            sends, _ = rider.copies(r_in, r_out, send_sems, recv_sems)
            for cp in sends:
                cp.start()

        body(*own_in, *own_out, *own_scr)

        @pl.when(last())
        def _():
            sends, arrivals = rider.copies(r_in, r_out, send_sems, recv_sems)
            for cp in arrivals():
                cp.wait_recv()
            for cp in sends:
                cp.wait_send()

    res = pl.pallas_call(
        full_body, name=name, grid=grid, in_specs=list(in_specs) + [_HBM] * ri, out_specs=list(out_specs) + [_HBM] * ro,
        out_shape=list(out_shape) + rider.out_shape,
        scratch_shapes=list(scratch_shapes) + [pltpu.SemaphoreType.DMA((rider.n_sems,))] * 2,
        input_output_aliases={n_in + i: n_out + o for i, o in rider.aliases.items()},
        compiler_params=_cparams(sem))(*args, *rider.inputs)
    return list(res[:n_out]), list(res[n_out:])


def _ride_gather_send(bufs):
    n = len(bufs)
    halves = [b.shape[0] // 2 for b in bufs]

    def copies(ins, outs, send_sems, recv_sems):
        x, y, c, chips = _place()
        slot = lambda b, cx, cy: _half_rows(outs[b], halves[b], c, lead=2 * cx + cy)
        sends = [_remote(_half_rows(ins[b], halves[b], c), slot(b, x, y), send_sems, recv_sems, 3 * b + j, (cx, cy, c))
                 for b in range(n) for j, (cx, cy) in enumerate(chips)]
        arrivals = lambda: [_remote(slot(b, cx, cy), slot(b, cx, cy), send_sems, recv_sems, 3 * b + j, (x, y, c))
                            for b in range(n) for j, (cx, cy) in enumerate(chips)]
        return sends, arrivals

    return _Rider(bufs, [jax.ShapeDtypeStruct((N_SHARD,) + b.shape, b.dtype) for b in bufs], 3 * n, copies)


def _ride_gather_pass(gathered):
    n = len(gathered)
    halves = [g.shape[1] // 2 for g in gathered]

    def copies(ins, outs, send_sems, recv_sems):
        x, y, c, chips = _place()
        slot = lambda b, cx, cy, hf: _half_rows(outs[b], halves[b], hf, lead=2 * cx + cy)
        sends = [_remote(slot(b, cx, cy, c), slot(b, cx, cy, c), send_sems, recv_sems, 3 * b + j, (x, y, 1 - c))
                 for b in range(n) for j, (cx, cy) in enumerate(chips)]
        arrivals = lambda: [_remote(slot(b, cx, cy, 1 - c), slot(b, cx, cy, 1 - c), send_sems, recv_sems, 3 * b + j, (x, y, c))
                            for b in range(n) for j, (cx, cy) in enumerate(chips)]
        return sends, arrivals

    return _Rider(gathered, [jax.ShapeDtypeStruct(g.shape, g.dtype) for g in gathered], 3 * n, copies,
                  aliases={b: b for b in range(n)})


def _ride_small_gather(buf):
    def copies(ins, outs, send_sems, recv_sems):
        x, y, c, _ = _place()
        flip = lambda v, d: 1 - v if d else v
        sends, peers = [], []
        for dx in (0, 1):
            for dy in (0, 1):
                for dc in (0, 1):
                    if dx or dy or dc:
                        k = 4 * dx + 2 * dy + dc - 1
                        px, py, pc = flip(x, dx), flip(y, dy), flip(c, dc)
                        sends.append(_remote(ins[0], outs[0].at[4 * x + 2 * y + c], send_sems, recv_sems, k, (px, py, pc)))
                        peers.append((k, 4 * px + 2 * py + pc))
        arrivals = lambda: [_remote(ins[0], outs[0].at[slot], send_sems, recv_sems, k, (x, y, c)) for k, slot in peers]
        return sends, arrivals

    return _Rider([buf], [jax.ShapeDtypeStruct((8,) + buf.shape, buf.dtype)], 7, copies)


def _small_sum(gathered, buf, me_arr):
    n, r, width = gathered.shape

    def body(me_ref, g_ref, b_ref, o_ref):
        total = jnp.zeros((r, width), F32)
        for d in range(n):
            total = total + jnp.where(me_ref[0] == d, b_ref[...], g_ref[d])
        o_ref[...] = total

    return pl.pallas_call(
        body, name="small_sum", out_shape=jax.ShapeDtypeStruct((r, width), F32),
        grid_spec=pltpu.PrefetchScalarGridSpec(
            num_scalar_prefetch=1, grid=(1,),
            in_specs=[pl.BlockSpec((n, r, width), lambda i, me: (0, 0, 0)), pl.BlockSpec((r, width), lambda i, me: (0, 0))],
            out_specs=pl.BlockSpec((r, width), lambda i, me: (0, 0))),
        compiler_params=_cparams(("arbitrary",)))(me_arr, gathered, buf)


def _ride_pair_exchange(gbufs):
    n = len(gbufs)
    halves = [g.shape[1] // 2 for g in gbufs]

    def copies(ins, outs, send_sems, recv_sems):
        x, y, c, _ = _place()
        sends = [_remote(ins[b].at[:, pl.ds(pl.multiple_of((1 - c) * halves[b], 16), halves[b]), :], outs[b],
                         send_sems, recv_sems, b, (x, y, 1 - c)) for b in range(n)]
        arrivals = lambda: [_remote(outs[b], outs[b], send_sems, recv_sems, b, (x, y, c)) for b in range(n)]
        return sends, arrivals

    return _Rider(gbufs, [jax.ShapeDtypeStruct((N_SHARD, h, g.shape[2]), g.dtype) for g, h in zip(gbufs, halves)], n, copies)


def _ride_chip_exchange(parts):
    n = len(parts)

    def copies(ins, outs, send_sems, recv_sems):
        x, y, c, chips = _place()
        sends = [_remote(ins[b].at[2 * cx + cy], outs[b].at[j], send_sems, recv_sems, 3 * b + j, (cx, cy, c))
                 for b in range(n) for j, (cx, cy) in enumerate(chips)]
        arrivals = lambda: [_remote(ins[b].at[0], outs[b].at[j], send_sems, recv_sems, 3 * b + j, (x, y, c))
                            for b in range(n) for j in range(len(chips))]
        return sends, arrivals

    return _Rider(parts, [jax.ShapeDtypeStruct((3,) + p.shape[1:], p.dtype) for p in parts], 3 * n, copies)


def _gather_shards(bufs, name):
    n = len(bufs)
    halves = [b.shape[0] // 2 for b in bufs]

    def body(*refs):
        ins, outs, send_sems, recv_sems = refs[:n], refs[n:2 * n], refs[2 * n], refs[2 * n + 1]
        x, y, c, chips = _place()
        me, sibling = (x, y, c), (x, y, 1 - c)
        slot = lambda b, cx, cy, hf: _half_rows(outs[b], halves[b], hf, lead=2 * cx + cy)
        first = [_remote(_half_rows(ins[b], halves[b], c), slot(b, x, y, c), send_sems, recv_sems, 6 * b + j, (cx, cy, c))
                 for b in range(n) for j, (cx, cy) in enumerate(chips)]
        for cp in first:
            cp.start()
        passed = []
        for j, (cx, cy) in enumerate(chips):
            for b in range(n):
                _remote(slot(b, cx, cy, c), slot(b, cx, cy, c), send_sems, recv_sems, 6 * b + j, me).wait_recv()
                fwd = _remote(slot(b, cx, cy, c), slot(b, cx, cy, c), send_sems, recv_sems, 6 * b + 3 + j, sibling)
                fwd.start()
                passed.append(fwd)
        for j, (cx, cy) in enumerate(chips):
            for b in range(n):
                _remote(slot(b, cx, cy, 1 - c), slot(b, cx, cy, 1 - c), send_sems, recv_sems, 6 * b + 3 + j, me).wait_recv()
        for cp in first + passed:
            cp.wait_send()

    return pl.pallas_call(
        body, name=name, out_shape=[jax.ShapeDtypeStruct((N_SHARD,) + b.shape, b.dtype) for b in bufs],
        in_specs=[_HBM] * n, out_specs=[_HBM] * n,
        scratch_shapes=[pltpu.SemaphoreType.DMA((6 * n,)), pltpu.SemaphoreType.DMA((6 * n,))],
    )(*bufs)


def _reduce_pair_exchange(gbufs, name):
    n = len(gbufs)
    halves = [g.shape[1] // 2 for g in gbufs]

    def body(*refs):
        ins, outs, send_sems, recv_sems = refs[:n], refs[n:2 * n], refs[2 * n], refs[2 * n + 1]
        x, y, c, _ = _place()
        cps = [_remote(ins[b].at[:, pl.ds(pl.multiple_of((1 - c) * halves[b], 16), halves[b]), :], outs[b],
                       send_sems, recv_sems, b, (x, y, 1 - c)) for b in range(n)]
        for cp in cps:
            cp.start()
        for cp in cps:
            cp.wait()

    return pl.pallas_call(
        body, name=name,
        out_shape=[jax.ShapeDtypeStruct((N_SHARD, h, g.shape[2]), g.dtype) for g, h in zip(gbufs, halves)],
        in_specs=[_HBM] * n, out_specs=[_HBM] * n,
        scratch_shapes=[pltpu.SemaphoreType.DMA((n,)), pltpu.SemaphoreType.DMA((n,))],
    )(*gbufs)


def _pair_add(gbuf, recv, c_arr, tr, name):
    _, rows, width = gbuf.shape
    half = rows // 2
    nt = half // tr

    def body(c_ref, a_ref, b_ref, o_ref):
        o_ref[...] = (a_ref[...] + b_ref[...]).astype(o_ref.dtype)

    blk = lambda f: pl.BlockSpec((None, tr, width), f)
    return pl.pallas_call(
        body, name=name, out_shape=jax.ShapeDtypeStruct((N_SHARD, half, width), jnp.bfloat16),
        grid_spec=pltpu.PrefetchScalarGridSpec(
            num_scalar_prefetch=1, grid=(N_SHARD, nt),
            in_specs=[blk(lambda s, i, c: (s, c[0] * nt + i, 0)), blk(lambda s, i, c: (s, i, 0))],
            out_specs=blk(lambda s, i, c: (s, i, 0))),
        compiler_params=_cparams(("parallel", "parallel")))(c_arr, gbuf, recv)


def _chip_add(part, recv, me_arr, tr, name):
    _, half, width = part.shape

    def body(me_ref, own, a0, a1, a2, o_ref):
        f = lambda r: r[...].astype(F32)
        o_ref[...] = ((f(own) + f(a0)) + f(a1)) + f(a2)

    specs = [pl.BlockSpec((None, tr, width), lambda i, me: (me[0], i, 0))]
    specs += [pl.BlockSpec((None, tr, width), functools.partial(lambda i, me, k: (k, i, 0), k=k)) for k in range(3)]
    return pl.pallas_call(
        body, name=name, out_shape=jax.ShapeDtypeStruct((half, width), F32),
        grid_spec=pltpu.PrefetchScalarGridSpec(
            num_scalar_prefetch=1, grid=(half // tr,), in_specs=specs,
            out_specs=pl.BlockSpec((tr, width), lambda i, me: (i, 0))),
        compiler_params=_cparams(("parallel",)))(me_arr, part, recv, recv, recv)


def _reduce_pair_share(rhalves, name):
    n = len(rhalves)

    def body(*refs):
        ins, outs, send_sems, recv_sems = refs[:n], refs[n:2 * n], refs[2 * n], refs[2 * n + 1]
        x, y, c, _ = _place()
        cps = [_remote(ins[b], outs[b], send_sems, recv_sems, b, (x, y, 1 - c)) for b in range(n)]
        for cp in cps:
            cp.start()
        for cp in cps:
            cp.wait()

    return pl.pallas_call(
        body, name=name, out_shape=[jax.ShapeDtypeStruct(r.shape, r.dtype) for r in rhalves],
        in_specs=[_HBM] * n, out_specs=[_HBM] * n,
        scratch_shapes=[pltpu.SemaphoreType.DMA((n,)), pltpu.SemaphoreType.DMA((n,))],
    )(*rhalves)


def _row_tile(rows, cap):
    if rows <= cap:
        return rows
    t = (cap // 8) * 8
    while t >= 8:
        if rows % t == 0:
            return t
        t -= 8
    return rows


def _adamw(w, g, m, v, name):
    shape = w.shape
    cols = shape[-1] if len(shape) <= 3 else shape[-2] * shape[-1]
    lead = len(shape) == 3
    w2, g2, m2, v2 = (a if lead else a.reshape(-1, cols) for a in (w, g, m, v))
    rows = shape[1] if lead else w2.shape[0]
    tr, tc = _row_tile(rows, 256), cols
    if tr == rows and rows > 256:
        tc = _tile(cols, 256)

    def body(w_ref, g_ref, m_ref, v_ref, d_ref, mo_ref, vo_ref):
        gv = g_ref[...]
        mn = ADAM_B1 * m_ref[...] + (1.0 - ADAM_B1) * gv
        vn = ADAM_B2 * v_ref[...] + (1.0 - ADAM_B2) * (gv * gv)
        m_hat = mn / (1.0 - ADAM_B1 ** ADAM_STEP)
        v_hat = vn / (1.0 - ADAM_B2 ** ADAM_STEP)
        d_ref[...] = -ADAM_LR * (m_hat / (jnp.sqrt(v_hat) + ADAM_EPS) + ADAM_WD * w_ref[...])
        mo_ref[...] = mn
        vo_ref[...] = vn

    blk = (pl.BlockSpec((None, tr, tc), lambda i, j: (0, i, j)) if lead else pl.BlockSpec((tr, tc), lambda i, j: (i, j)))
    outs = pl.pallas_call(
        body, name=name, grid=(rows // tr, cols // tc), in_specs=[blk] * 4, out_specs=[blk] * 3,
        out_shape=[jax.ShapeDtypeStruct(w2.shape, F32)] * 3,
        compiler_params=_cparams(("parallel", "parallel")))(w2, g2, m2, v2)
    return tuple(o.reshape(shape) for o in outs)


_WEIGHT_NAMES = ('w_in', 'conv_w', 'dn_a_log', 'dn_dt_bias', 'dn_norm_w', 'q_norm_w', 'w_uq', 'kv_norm_w', 'w_uk',
                 'w_uv', 'w_br_dn', 'w_br_mla', 'w_o', 'ln1_g', 'ln1_b', 'w_ffn_in', 'w_ffn_out', 'w_ple',
                 'w_ple_gate', 'ln2_g', 'ln2_b')
_SMALL_NAMES = ('ln1_g', 'ln1_b', 'ln2_g', 'ln2_b', 'q_norm_w', 'kv_norm_w', 'dn_norm_w', 'dn_a_log', 'dn_dt_bias')
_SMALL_GROUP = 8
_CONV_SMALL_ROW = len(_SMALL_NAMES) * _SMALL_GROUP
_CONV_SMALL_ROWS = DN_CONV * QKV_W // FLAT_W


_LOSS_SMALL_ROW = _CONV_SMALL_ROW + 16


def _pack_small(gw, loss_lanes):
    rows = [jnp.pad(gw[n][None, :], ((0, _SMALL_GROUP - 1), (0, FLAT_W - gw[n].shape[0]))) for n in _SMALL_NAMES]
    rows.append(jnp.pad(gw['conv_w'].reshape(_CONV_SMALL_ROWS, FLAT_W), ((0, 16 - _CONV_SMALL_ROWS), (0, 0))))
    rows.append(jnp.pad(loss_lanes, ((0, _SMALL_GROUP - 1), (0, FLAT_W - LANE))))
    return jnp.concatenate(rows, axis=0)


class _Exchange:
    def __init__(self, local, me_chip, c_arr):
        self.local, self.me_chip, self.c_arr = local, me_chip, c_arr
        self.parts = self.arrived = None

    def gather_send(self):
        return _ride_gather_send(self.local)

    def gather_pass(self, sent):
        return _ride_gather_pass(sent)

    def weights(self, gathered):
        return _prep_weights(_unpack_rest(gathered, self.local, self.me_chip))

    def pair_send(self, g):
        self.gbufs = _pack_grads_rest(_unprep_grads_early(g))
        return _ride_pair_exchange(self.gbufs)

    def reduce_send(self, got):
        self.parts = [_pair_add(g_, r_, self.c_arr, tr, "pair_add_%d" % (i + 1))
                      for i, (g_, r_, tr) in enumerate(zip(self.gbufs, got, _ADD_TILES[1:]))]
        return _ride_chip_exchange(self.parts)

    def reduce_arrived(self, arrived):
        self.arrived = list(arrived)

    def in_send(self, g):
        g_in = [_shard_columns(_w_in_grad(g), W_IN_SHARD)]
        got = _reduce_pair_exchange(g_in, "reduce_pair_exchange_w_in")
        self.part_in = _pair_add(g_in[0], got[0], self.c_arr, _ADD_TILES[0], "pair_add_0")
        return _ride_chip_exchange([self.part_in])

    def in_arrived(self, arrived):
        self.arrived_in = list(arrived)

    def small_send(self, g):
        self.small = _pack_small(_unprep_grads_late(g), g['loss_lanes'])
        return _ride_small_gather(self.small)

    def small_arrived(self, arrived):
        self.small_gathered = arrived[0]


def kernel(x, p, positions, w_in, conv_w, dn_a_log, dn_dt_bias, dn_norm_w, q_norm_w, w_uq, kv_norm_w, w_uk, w_uv, w_br_dn, w_br_mla, w_o, ln1_g, ln1_b, w_ffn_in, w_ffn_out, w_ple, w_ple_gate, ln2_g, ln2_b, loss_target, m_w_in, m_conv_w, m_dn_a_log, m_dn_dt_bias, m_dn_norm_w, m_q_norm_w, m_w_uq, m_kv_norm_w, m_w_uk, m_w_uv, m_w_br_dn, m_w_br_mla, m_w_o, m_ln1_g, m_ln1_b, m_w_ffn_in, m_w_ffn_out, m_w_ple, m_w_ple_gate, m_ln2_g, m_ln2_b, v_w_in, v_conv_w, v_dn_a_log, v_dn_dt_bias, v_dn_norm_w, v_q_norm_w, v_w_uq, v_kv_norm_w, v_w_uk, v_w_uv, v_w_br_dn, v_w_br_mla, v_w_o, v_ln1_g, v_ln1_b, v_w_ffn_in, v_w_ffn_out, v_w_ple, v_w_ple_gate, v_ln2_g, v_ln2_b):
    ws = dict(w_in=w_in, conv_w=conv_w, dn_a_log=dn_a_log, dn_dt_bias=dn_dt_bias, dn_norm_w=dn_norm_w, q_norm_w=q_norm_w,
              w_uq=w_uq, kv_norm_w=kv_norm_w, w_uk=w_uk, w_uv=w_uv, w_br_dn=w_br_dn, w_br_mla=w_br_mla, w_o=w_o,
              ln1_g=ln1_g, ln1_b=ln1_b, w_ffn_in=w_ffn_in, w_ffn_out=w_ffn_out, w_ple=w_ple, w_ple_gate=w_ple_gate,
              ln2_g=ln2_g, ln2_b=ln2_b)
    ms = dict(w_in=m_w_in, conv_w=m_conv_w, dn_a_log=m_dn_a_log, dn_dt_bias=m_dn_dt_bias, dn_norm_w=m_dn_norm_w,
              q_norm_w=m_q_norm_w, w_uq=m_w_uq, kv_norm_w=m_kv_norm_w, w_uk=m_w_uk, w_uv=m_w_uv, w_br_dn=m_w_br_dn,
              w_br_mla=m_w_br_mla, w_o=m_w_o, ln1_g=m_ln1_g, ln1_b=m_ln1_b, w_ffn_in=m_w_ffn_in, w_ffn_out=m_w_ffn_out,
              w_ple=m_w_ple, w_ple_gate=m_w_ple_gate, ln2_g=m_ln2_g, ln2_b=m_ln2_b)
    vs = dict(w_in=v_w_in, conv_w=v_conv_w, dn_a_log=v_dn_a_log, dn_dt_bias=v_dn_dt_bias, dn_norm_w=v_dn_norm_w,
              q_norm_w=v_q_norm_w, w_uq=v_w_uq, kv_norm_w=v_kv_norm_w, w_uk=v_w_uk, w_uv=v_w_uv, w_br_dn=v_w_br_dn,
              w_br_mla=v_w_br_mla, w_o=v_w_o, ln1_g=v_ln1_g, ln1_b=v_ln1_b, w_ffn_in=v_w_ffn_in, w_ffn_out=v_w_ffn_out,
              w_ple=v_w_ple, w_ple_gate=v_w_ple_gate, ln2_g=v_ln2_g, ln2_b=v_ln2_b)
    mx, my, mc = lax.axis_index("x"), lax.axis_index("y"), lax.axis_index("c")

    me_chip = 2 * mx + my
    c_arr = jnp.reshape(mc, (1,)).astype(jnp.int32)
    me_arr = jnp.reshape(me_chip, (1,)).astype(jnp.int32)
    sharded = ('w_in', 'w_ffn_in') + tuple(name for name, _, _ in _FLATB_PIECES)

    local = _pack_shards({name: ws[name][0] for name in sharded}, conv_w[0])
    (gathered_in,) = _gather_shards(local[:1], "gather_w_in")
    w_in_full, conv_full = _unpack_w_in(gathered_in, local[0], me_chip)
    small = {n: ws[n][0] for n in _SMALL_NAMES}
    small['conv_w'] = conv_full
    sp = _prep_small(small)
    cosb, sinb = _rope_tables(positions[0])
    exch = _Exchange(local[1:], me_chip, c_arr)

    loss_lanes, dx, g = _local_step(x[0], p[0, 0], cosb, sinb, loss_target[0], _prep_w_in(w_in_full), sp, exch)

    parts = [exch.part_in] + exch.parts
    arrived = exch.arrived_in + exch.arrived
    mine = [_chip_add(p_, r_, me_arr, tr, "chip_add_%d" % i) for i, (p_, r_, tr) in enumerate(zip(parts, arrived, _ADD_TILES))]
    reduced = _unpack_reduced(mine, _reduce_pair_share(mine, "reduce_pair_share"), mc)
    tot = _small_sum(exch.small_gathered, exch.small, jnp.reshape(4 * mx + 2 * my + mc, (1,)).astype(jnp.int32))
    loss = jnp.sum(tot[_LOSS_SMALL_ROW, :LANE])
    gred = {name: reduced[name][None] for name in sharded}
    for i, n in enumerate(_SMALL_NAMES):
        gred[n] = tot[i * _SMALL_GROUP, :ws[n].shape[1]][None]
    conv_tot = tot[_CONV_SMALL_ROW:_CONV_SMALL_ROW + _CONV_SMALL_ROWS].reshape(DN_CONV, QKV_W)
    gred['conv_w'] = lax.dynamic_slice_in_dim(conv_tot, (2 * mx + my) * _CONV_SHARD, _CONV_SHARD, axis=1)[None]

    deltas, new_m, new_v = {}, {}, {}
    for n in _WEIGHT_NAMES:
        if n == 'w_in':
            tr_ = lambda a: jnp.transpose(a, (0, 2, 1))
            g_t = tr_(gred[n].reshape(ws[n].shape))
            outs = _adamw(tr_(ws[n]), g_t, tr_(ms[n]), tr_(vs[n]), "adamw_" + n)
            gred[n] = tr_(g_t)
            deltas[n], new_m[n], new_v[n] = (tr_(o) for o in outs)
            continue
        gred[n] = gred[n].reshape(ws[n].shape)
        deltas[n], new_m[n], new_v[n] = _adamw(ws[n], gred[n], ms[n], vs[n], "adamw_" + n)
    return (loss, dx[None], *[gred[n] for n in _WEIGHT_NAMES], *[deltas[n] for n in _WEIGHT_NAMES],
            *[new_m[n] for n in _WEIGHT_NAMES], *[new_v[n] for n in _WEIGHT_NAMES])
```

```python
import functools

import jax
import jax.numpy as jnp
from jax import lax
from jax.experimental import pallas as pl
from jax.experimental.pallas import tpu as pltpu

F32 = jnp.float32
_CDT = jnp.bfloat16
_MESH = pl.DeviceIdType.MESH

D_MODEL = 1024
PLE_DIM = 256
HEADS = 8
DN_DK = 128
DN_CHUNK = 64
DN_CONV = 4
QKV_W = 3 * HEADS * DN_DK
Q_LORA = 384
KV_LORA = 256
NOPE = 128
ROPE = 64
ROPE_PAD = 128
FFN_HIDDEN = 2816
D_IN = 6864
ROPE_BASE = 10000.0
ALPHA = 2.0 ** 0.25
ATT_SCALE = (NOPE + ROPE) ** -0.5
NEG_BIG = -1e30
ADAM_LR, ADAM_B1, ADAM_B2, ADAM_EPS, ADAM_WD, ADAM_STEP = 0.001, 0.9, 0.999, 1e-08, 0.01, 10

LANE = 128
VMEM_LIMIT = 56 * 1024 * 1024
MM_VMEM_BUDGET = 40 * 1024 * 1024
N_SHARD = 4
FLAT_W = 1024
SMALL_ROWS = 96


def _tile(dim, cap):
    if dim <= cap:
        return dim
    t = (cap // LANE) * LANE
    while t >= LANE:
        if dim % t == 0:
            return t
        t -= LANE
    return dim


def _cparams(sem):
    return pltpu.CompilerParams(dimension_semantics=sem, vmem_limit_bytes=VMEM_LIMIT)


def _mm(a, b, *, name, ta=False, tb=False, add=None, add_scale=1.0, out_dtype=F32, heads=None,
        a_head=None, b_head=None, out_head=None, dims=None, tm=1408, tn=1408, rider=None):
    m, n, k = dims
    tm, tn = _tile(m, tm), _tile(n, tn)
    if heads is None and (m // tm) * (n // tn) < 4:
        sm, sn = _tile(m, 512), _tile(n, 512)
        if sm == min(m, 512) and sn == min(n, 512):
            tm, tn = sm, sn
    sa, sb, so = a.dtype.itemsize, b.dtype.itemsize, jnp.dtype(out_dtype).itemsize

    def vmem_need(tk_):
        acc = tm * tn * 4 if tk_ < k else 0
        extra = 2 * tm * tn * 4 if add is not None else 0
        return 2 * (tm * tk_ * sa + tk_ * tn * sb) + 2 * tm * tn * so + acc + extra

    tk = k
    while vmem_need(tk) > MM_VMEM_BUDGET and tk > LANE:
        smaller = _tile(k, tk - LANE)
        if smaller >= tk:
            break
        tk = smaller
    nk = k // tk
    hgrid = () if heads is None else (heads,)
    off = len(hgrid)

    def spec(rows, cols, rtile, ctile, rsel, csel, layout):
        def idx(*g):
            h = g[0] if off else 0
            ri, ci = g[off + rsel], g[off + csel]
            if layout == 'lead':
                return (h, ri, ci)
            if layout == 'col':
                return (ri, h * (cols // ctile) + ci)
            return (ri, ci)
        if layout == 'lead':
            return pl.BlockSpec((None, rtile, ctile), idx)
        return pl.BlockSpec((rtile, ctile), idx)

    a_spec = spec(k, m, tk, tm, 2, 0, a_head) if ta else spec(m, k, tm, tk, 0, 2, a_head)
    b_spec = spec(n, k, tn, tk, 1, 2, b_head) if tb else spec(k, n, tk, tn, 2, 1, b_head)
    o_spec = spec(m, n, tm, tn, 0, 1, out_head)
    in_specs = [a_spec, b_spec]
    args = [a, b]
    if add is not None:
        in_specs.append(spec(m, n, tm, tn, 0, 1, out_head))
        args.append(add)
    dn = (((0 if ta else 1,), (1 if tb else 0,)), ((), ()))

    def body(*refs):
        a_ref, b_ref = refs[0], refs[1]
        prod = lax.dot_general(a_ref[...].astype(_CDT), b_ref[...].astype(_CDT), dn, preferred_element_type=F32)
        if nk == 1:
            o_ref = refs[-1]
            if add is not None:
                prod = prod + refs[2][...].astype(F32) * add_scale
            o_ref[...] = prod.astype(out_dtype)
            return
        o_ref, acc_ref = refs[-2], refs[-1]
        kk = pl.program_id(off + 2)

        @pl.when(kk == 0)
        def _():
            if add is not None:
                acc_ref[...] = refs[2][...].astype(F32) * add_scale
            else:
                acc_ref[...] = jnp.zeros_like(acc_ref)

        acc_ref[...] += prod

        @pl.when(kk == nk - 1)
        def _():
            o_ref[...] = acc_ref[...].astype(out_dtype)

    if out_head == 'lead':
        oshape = (heads, m, n)
    elif out_head == 'col':
        oshape = (m, heads * n)
    else:
        oshape = (m, n)
    grid = hgrid + (m // tm, n // tn, nk)
    scratch = [pltpu.VMEM((tm, tn), F32)] if nk > 1 else []
    if rider is not None:
        (out,), carried = _carried_call(
            body, rider, *_grid_ends(grid), name=name, grid=grid, in_specs=in_specs, out_specs=[o_spec],
            out_shape=[jax.ShapeDtypeStruct(oshape, out_dtype)], scratch_shapes=scratch,
            sem=("arbitrary",) * len(grid), args=tuple(args))
        return out, carried
    sem = ("parallel",) * (off + 2) + ("arbitrary",)
    return pl.pallas_call(
        body, name=name, grid=grid, in_specs=in_specs, out_specs=o_spec,
        out_shape=jax.ShapeDtypeStruct(oshape, out_dtype), scratch_shapes=scratch,
        compiler_params=_cparams(sem))(*args)


def _mm2(a, b, **kw):
    ta, tb = kw.get('ta', False), kw.get('tb', False)
    m = a.shape[1] if ta else a.shape[0]
    k = a.shape[0] if ta else a.shape[1]
    n = b.shape[0] if tb else b.shape[1]
    return _mm(a, b, dims=(m, n, k), **kw)


def _rowwise(fn, rows, bcast, outs, reds=(), *, name, tm=256, heads=None):
    t = rows[0][0].shape[0]
    tm = min(tm, t)
    hn = 1 if heads is None else heads
    in_specs, args = [], []
    for arr, width, base, per_head in rows:
        in_specs.append(pl.BlockSpec((tm, width), functools.partial(
            lambda i, h, base, per_head: (i, base + (h if per_head else 0)), base=base, per_head=per_head)))
        args.append(arr)
    for arr in bcast:
        in_specs.append(pl.BlockSpec(arr.shape, lambda i, h: (0, 0)))
        args.append(arr)
    out_specs, out_shape = [], []
    for total, width, per_head, dt in outs:
        out_specs.append(pl.BlockSpec((tm, width), functools.partial(
            lambda i, h, per_head: (i, h if per_head else 0), per_head=per_head)))
        out_shape.append(jax.ShapeDtypeStruct((t, total), dt))
    for shp in reds:
        out_specs.append(pl.BlockSpec(shp, lambda i, h: (0, 0)))
        out_shape.append(jax.ShapeDtypeStruct(shp, F32))
    n_in, n_out, n_red = len(args), len(outs), len(reds)

    def body(*refs):
        i, h = pl.program_id(0), pl.program_id(1)
        vals = fn(h, *[r[...] for r in refs[:n_in]])
        for r, v in zip(refs[n_in:n_in + n_out], vals[:n_out]):
            r[...] = v.astype(r.dtype)
        if n_red:
            @pl.when((i == 0) & (h == 0))
            def _():
                for r in refs[n_in + n_out:]:
                    r[...] = jnp.zeros_like(r)
            for r, v in zip(refs[n_in + n_out:], vals[n_out:]):
                r[...] += v

    sem = ("arbitrary", "arbitrary") if n_red else ("parallel", "parallel")
    res = pl.pallas_call(body, name=name, grid=(t // tm, hn), in_specs=in_specs, out_specs=out_specs,
                         out_shape=out_shape, compiler_params=_cparams(sem))(*args)
    return tuple(res)


def _sigmoid(x):
    return 1.0 / (1.0 + jnp.exp(-x))


def _silu(x):
    return x * _sigmoid(x)


def _softplus(x):
    return jnp.maximum(x, 0.0) + jnp.log(1.0 + jnp.exp(-jnp.abs(x)))


def _layer_norm(t, g, b):
    mu = jnp.mean(t, axis=-1, keepdims=True)
    d = t - mu
    var = jnp.mean(d * d, axis=-1, keepdims=True)
    return d * lax.rsqrt(var + 1e-5) * g + b


def _rms_norm(t, w):
    return t * lax.rsqrt(jnp.mean(t * t, axis=-1, keepdims=True) + 1e-6) * w


def _swap_rope_halves(t):
    lane = lax.broadcasted_iota(jnp.int32, t.shape, 1) % ROPE_PAD
    n = t.shape[1]
    up = pltpu.roll(t, n - ROPE // 2, axis=1)
    dn = pltpu.roll(t, ROPE // 2, axis=1)
    return jnp.where(lane < ROPE // 2, up, jnp.where(lane < ROPE, dn, 0.0))


def _rope(t, cosb, sinb):
    reps = t.shape[1] // ROPE_PAD
    c = jnp.tile(cosb, (1, reps)) if reps > 1 else cosb
    s = jnp.tile(sinb, (1, reps)) if reps > 1 else sinb
    return t * c + _swap_rope_halves(t) * s


def _rope_bwd(d, cosb, sinb):
    reps = d.shape[1] // ROPE_PAD
    c = jnp.tile(cosb, (1, reps)) if reps > 1 else cosb
    s = jnp.tile(sinb, (1, reps)) if reps > 1 else sinb
    return d * c + _swap_rope_halves(d * s)


_CONV_ROWS = 256
_CONV_COLS = 256


def _conv_window(ref, r0, lo, hi, t):
    parts = []
    start, stop = r0 - lo, r0 + _CONV_ROWS + hi
    if start < 0:
        parts.append(jnp.zeros((-start, ref.shape[1]), F32))
        start = 0
    tail = max(stop - t, 0)
    parts.append(ref[start:stop - tail, :].astype(F32))
    if tail:
        parts.append(jnp.zeros((tail, ref.shape[1]), F32))
    return parts[0] if len(parts) == 1 else jnp.concatenate(parts, axis=0)


def _conv_taps(win, w_ref, n_out):
    acc = win[8:8 + n_out] * w_ref[DN_CONV - 1:DN_CONV, :]
    for i in range(DN_CONV - 1):
        acc = acc + pltpu.roll(win, DN_CONV - 1 - i, axis=0)[8:8 + n_out] * w_ref[i:i + 1, :]
    return acc


def _conv_silu(x, w):
    t, ch = x.shape

    def body(x_ref, w_ref, o_ref):
        for r in range(t // _CONV_ROWS):
            r0 = r * _CONV_ROWS
            c = _conv_taps(_conv_window(x_ref, r0, 8, 0, t), w_ref, _CONV_ROWS)
            o_ref[r0:r0 + _CONV_ROWS, :] = _silu(c)

    return pl.pallas_call(
        body, name="conv_silu", grid=(ch // _CONV_COLS,),
        in_specs=[pl.BlockSpec((t, _CONV_COLS), lambda j: (0, j)), pl.BlockSpec((DN_CONV, _CONV_COLS), lambda j: (0, j))],
        out_specs=pl.BlockSpec((t, _CONV_COLS), lambda j: (0, j)),
        out_shape=jax.ShapeDtypeStruct((t, ch), F32), compiler_params=_cparams(("parallel",)))(x, w)


def _conv_silu_bwd(x, w, dys, rider=None):
    t, ch = x.shape
    per = ch // len(dys) // _CONV_COLS

    def body(x_ref, w_ref, *rest):
        dy_refs, (dx_ref, dw_ref) = rest[:len(dys)], rest[len(dys):]
        sec = pl.program_id(0) // per
        dws = [jnp.zeros((1, _CONV_COLS), F32) for _ in range(DN_CONV)]
        for r in range(t // _CONV_ROWS):
            r0 = r * _CONV_ROWS
            n_ext = _CONV_ROWS + 8
            xw = _conv_window(x_ref, r0, 8, 8, t)
            c = _conv_taps(xw, w_ref, n_ext)
            sg = _sigmoid(c)
            dy = _conv_window(dy_refs[-1], r0, 0, 8, t)
            for k in range(len(dys) - 2, -1, -1):
                dy = jnp.where(sec == k, _conv_window(dy_refs[k], r0, 0, 8, t), dy)
            ds = dy * (sg * (1.0 + c * (1.0 - sg)))
            x0 = xw[8:8 + _CONV_ROWS]
            dx = jnp.zeros((_CONV_ROWS, _CONV_COLS), F32)
            for i in range(DN_CONV):
                sh = DN_CONV - 1 - i
                ds_up = (ds if sh == 0 else pltpu.roll(ds, n_ext - sh, axis=0))[:_CONV_ROWS]
                dx = dx + ds_up * w_ref[i:i + 1, :]
                dws[i] = dws[i] + jnp.sum(x0 * ds_up, axis=0, keepdims=True)
            dx_ref[r0:r0 + _CONV_ROWS, :] = dx.astype(dx_ref.dtype)
        for i in range(DN_CONV):
            dw_ref[i:i + 1, :] = dws[i]

    blk = pl.BlockSpec((t, _CONV_COLS), lambda j: (0, j))
    wblk = pl.BlockSpec((DN_CONV, _CONV_COLS), lambda j: (0, j))
    dy_specs = [pl.BlockSpec((t, _CONV_COLS), functools.partial(lambda j, k: (0, jnp.clip(j - k * per, 0, per - 1)), k=k))
                for k in range(len(dys))]
    grid = (ch // _CONV_COLS,)
    return _carried_call(
        body, rider, *_grid_ends(grid), name="conv_silu_bwd", grid=grid, in_specs=[blk, wblk] + dy_specs,
        out_specs=[blk, wblk], out_shape=[jax.ShapeDtypeStruct((t, ch), _CDT), jax.ShapeDtypeStruct((DN_CONV, ch), F32)],
        scratch_shapes=[], sem=("arbitrary",), args=(x, w, *dys))


_PA_ROWS = 1024
_PA_ROWS_FWD = 1024


def _bmm(a, b, spec):
    return jnp.einsum(spec, a.astype(_CDT), b.astype(_CDT), preferred_element_type=F32)


def _split16(a):
    hi = a.astype(jnp.bfloat16)
    return hi, (a - hi.astype(F32)).astype(jnp.bfloat16)


def _bmm3(a, b, spec):
    ah, al = _split16(a)
    bh, bl = _split16(b)
    e = lambda p, q: jnp.einsum(spec, p, q, preferred_element_type=F32)
    return e(ah, bh) + (e(ah, bl) + e(al, bh))


def _split3(b):
    b0 = b.astype(jnp.bfloat16)
    r1 = b - b0.astype(F32)
    b1 = r1.astype(jnp.bfloat16)
    return b0, b1, (r1 - b1.astype(F32)).astype(jnp.bfloat16)


@functools.partial(jax.custom_vjp, nondiff_argnums=(2, 3))
def _select_mm(sel, b, spec, spec_t):
    return sum(jnp.einsum(spec, sel, t, preferred_element_type=F32) for t in _split3(b))


def _select_mm_fwd(sel, b, spec, spec_t):
    return _select_mm(sel, b, spec, spec_t), sel


def _select_mm_bwd(spec, spec_t, sel, ct):
    return jnp.zeros_like(sel), sum(jnp.einsum(spec_t, sel, t, preferred_element_type=F32) for t in _split3(ct))


_select_mm.defvjp(_select_mm_fwd, _select_mm_bwd)


def _tri_inverse(l_mat, eye):
    pw = -l_mat
    t_inv = eye + pw
    for _ in range(5):
        pw = _bmm3(pw, pw, 'bij,bjk->bik')
        t_inv = t_inv + _bmm3(t_inv, pw, 'bij,bjk->bik')
    return t_inv


@jax.custom_vjp
def _tri_inverse_saved(l_mat, t_saved):
    return t_saved


def _tri_inverse_saved_fwd(l_mat, t_saved):
    return t_saved, t_saved


def _tri_inverse_saved_bwd(t_saved, dt):
    left = _bmm3(t_saved, dt, 'bji,bjk->bik')
    return -_bmm3(left, t_saved, 'bij,bkj->bik'), jnp.zeros_like(t_saved)


_tri_inverse_saved.defvjp(_tri_inverse_saved_fwd, _tri_inverse_saved_bwd)


def _phase_a(h, q, k, v, ba, alog, dtb, t_saved=None):
    r = q.shape[0]
    nb = r // DN_CHUNK
    c = DN_CHUNK
    lane = lax.broadcasted_iota(jnp.int32, (1, LANE), 1)
    selb = (lane == h).astype(F32)
    sela = (lane == h + HEADS).astype(F32)
    b_raw = jnp.sum(ba * selb, axis=1, keepdims=True)
    a_raw = jnp.sum(ba * sela, axis=1, keepdims=True)
    al = jnp.sum(alog * selb, axis=1, keepdims=True)
    dt = jnp.sum(dtb * selb, axis=1, keepdims=True)
    beta = jnp.broadcast_to(_sigmoid(b_raw), (r, LANE))
    g = jnp.broadcast_to(-jnp.exp(al) * _softplus(a_raw + dt), (r, LANE))
    qn = q * lax.rsqrt(jnp.sum(q * q, -1, keepdims=True) + 1e-6) * (DN_DK ** -0.5)
    kn = k * lax.rsqrt(jnp.sum(k * k, -1, keepdims=True) + 1e-6)
    q3, k3, v3 = qn.reshape(nb, c, LANE), kn.reshape(nb, c, LANE), v.reshape(nb, c, LANE)
    b3, g3 = beta.reshape(nb, c, LANE), g.reshape(nb, c, LANE)
    ri = lax.broadcasted_iota(jnp.int32, (nb, c, c), 1)
    ci = lax.broadcasted_iota(jnp.int32, (nb, c, c), 2)
    tril, strict = ri >= ci, ri > ci
    gc = _select_mm(tril.astype(jnp.bfloat16), g3, 'bij,bjd->bid', 'bij,bid->bjd')
    onehot = (lax.broadcasted_iota(jnp.int32, (nb, c, LANE), 2) == 0).astype(jnp.bfloat16)
    g_row = _select_mm(onehot, gc, 'bid,bjd->bij', 'bid,bij->bjd')
    diff = gc[:, :, :c] - g_row
    decay = jnp.where(tril, jnp.exp(jnp.where(tril, diff, 0.0)), 0.0)
    kb = k3 * b3
    l_mat = jnp.where(strict, _bmm(kb, k3, 'bid,bjd->bij') * decay, 0.0)
    if t_saved is None:
        t_inv = _tri_inverse(l_mat, (ri == ci).astype(F32))
    else:
        t_inv = _tri_inverse_saved(l_mat, t_saved.reshape(nb, c, c))
    eg = jnp.exp(gc)
    u = _bmm(t_inv, v3 * b3, 'bij,bje->bie')
    w = _bmm(t_inv, kb * eg, 'bij,bje->bie')
    intra = jnp.where(tril, _bmm(q3, k3, 'bid,bjd->bij') * decay, 0.0)
    qd = q3 * eg
    gl = jnp.sum(g3, axis=1, keepdims=True)
    kt = k3 * jnp.exp(gl - gc)
    outs = (u.reshape(r, LANE), w.reshape(r, LANE), qd.reshape(r, LANE), kt.reshape(r, LANE),
            intra.reshape(r, c), gl.reshape(nb, LANE))
    if t_saved is not None:
        return outs
    qd2 = qd - _bmm(intra, w, 'bij,bjd->bid')
    au = _bmm(intra, u, 'bij,bje->bie')
    return outs + (t_inv.reshape(r, c), qd2.reshape(r, LANE), au.reshape(r, LANE))


def _pa_specs(t, rows):
    rr = min(rows, t)
    nb = rr // DN_CHUNK
    qkv = [pl.BlockSpec((rr, LANE), functools.partial(lambda i, h, o: (i, o + h), o=o)) for o in (0, HEADS, 2 * HEADS)]
    ba = pl.BlockSpec((rr, LANE), lambda i, h: (i, 3))
    vec = pl.BlockSpec((1, LANE), lambda i, h: (0, 0))
    row = pl.BlockSpec((rr, LANE), lambda i, h: (i, h))
    intra = pl.BlockSpec((None, rr, DN_CHUNK), lambda i, h: (h, i, 0))
    gl = pl.BlockSpec((nb, LANE), lambda i, h: (i, h))
    return rr, qkv, ba, vec, row, intra, gl


def _grid_ends(grid):
    first = lambda: functools.reduce(jnp.logical_and, [pl.program_id(a) == 0 for a in range(len(grid))])
    last = lambda: functools.reduce(jnp.logical_and, [pl.program_id(a) == n - 1 for a, n in enumerate(grid)])
    return first, last


def _delta_local(qkv_act, pm, alog, dtb, rider=None):
    t = qkv_act.shape[0]
    rr, qkv, ba, vec, row, intra, gl = _pa_specs(t, _PA_ROWS_FWD)

    def body(q, k, v, b, al, dt, *outs):
        vals = _phase_a(pl.program_id(1), q[...], k[...], v[...], b[...], al[...], dt[...])
        for o, val in zip(outs, vals):
            o[...] = val

    wide = jax.ShapeDtypeStruct((t, HEADS * LANE), F32)
    sq = jax.ShapeDtypeStruct((HEADS, t, DN_CHUNK), F32)
    grid = (t // rr, HEADS)
    return _carried_call(
        body, rider, *_grid_ends(grid), name="delta_local", grid=grid, in_specs=qkv + [ba, vec, vec],
        out_specs=[row] * 4 + [intra, gl, intra, row, row],
        out_shape=[wide] * 4 + [sq, jax.ShapeDtypeStruct((t // DN_CHUNK, HEADS * LANE), F32), sq, wide, wide],
        scratch_shapes=[], sem=("arbitrary", "arbitrary"), args=(qkv_act, qkv_act, qkv_act, pm, alog, dtb))


def _delta_local_bwd(qkv_act, pm, alog, dtb, t_inv, du, dw, dqd, dkt, dintra, dgl, rider=None):
    t = qkv_act.shape[0]
    rr, qkv, ba, vec, row, intra, gl = _pa_specs(t, _PA_ROWS)

    def body(q, k, v, b, al, dt, ti, du_r, dw_r, dqd_r, dkt_r, di_r, dgl_r, dq_o, dk_o, dv_o, dba_o, dal_o, ddt_o):
        i, h = pl.program_id(0), pl.program_id(1)
        t_saved = ti[...]
        _, vjp = jax.vjp(lambda *a: _phase_a(h, *a, t_saved=t_saved), q[...], k[...], v[...], b[...], al[...], dt[...])
        dq, dk, dv, dba, dal, ddt = vjp((du_r[...], dw_r[...], dqd_r[...], dkt_r[...], di_r[...], dgl_r[...]))
        dq_o[...], dk_o[...], dv_o[...] = dq, dk, dv

        @pl.when(h == 0)
        def _():
            dba_o[...] = jnp.zeros_like(dba_o)

        @pl.when((h == 0) & (i == 0))
        def _():
            dal_o[...] = jnp.zeros_like(dal_o)
            ddt_o[...] = jnp.zeros_like(ddt_o)

        dba_o[...] += dba
        dal_o[...] += dal
        ddt_o[...] += ddt

    wide = jax.ShapeDtypeStruct((t, HEADS * LANE), F32)
    vshape = jax.ShapeDtypeStruct((1, LANE), F32)
    grid = (t // rr, HEADS)
    return _carried_call(
        body, rider, *_grid_ends(grid), name="delta_local_bwd", grid=grid,
        in_specs=qkv + [ba, vec, vec, intra] + [row] * 4 + [intra, gl],
        out_specs=[row] * 3 + [pl.BlockSpec((rr, LANE), lambda i, h: (i, 0)), vec, vec],
        out_shape=[wide] * 3 + [jax.ShapeDtypeStruct((t, LANE), F32), vshape, vshape],
        scratch_shapes=[], sem=("arbitrary", "arbitrary"),
        args=(qkv_act, qkv_act, qkv_act, pm, alog, dtb, t_inv, du, dw, dqd, dkt, dintra, dgl))


_SCAN_ROWS = 512


def _dot(a, b, dn):
    return lax.dot_general(a.astype(_CDT), b.astype(_CDT), (dn, ((), ())), preferred_element_type=F32)


_NN = ((1,), (0,))
_NT = ((1,), (1,))
_TN = ((0,), (0,))


def _delta_scan(u, w, qd, kt, au, gl, rider=None):
    t = u.shape[0]
    rr = min(_SCAN_ROWS, t)
    nc = rr // DN_CHUNK

    def body(u_ref, w_ref, qd_ref, kt_ref, au_ref, gl_ref, o_ref, sall_ref, s_scr):
        @pl.when(pl.program_id(0) == 0)
        def _():
            s_scr[...] = jnp.zeros_like(s_scr)

        def chunk(c, carry):
            r0 = pl.multiple_of(c * DN_CHUNK, DN_CHUNK)
            rows = pl.ds(r0, DN_CHUNK)
            e = jnp.exp(gl_ref[pl.ds(c, 1), :])
            states = [s_scr[h] for h in range(HEADS)]
            u_c, w_c, qd_c, kt_c, au_c = u_ref[rows, :], w_ref[rows, :], qd_ref[rows, :], kt_ref[rows, :], au_ref[rows, :]
            o_new, s_new = [], []
            for h in range(HEADS):
                cs = slice(h * LANE, (h + 1) * LANE)
                s = states[h]
                both = _dot(jnp.concatenate([w_c[:, cs], qd_c[:, cs]], axis=0), s, _NN)
                v_new = u_c[:, cs] - both[:DN_CHUNK]
                o_new.append(both[DN_CHUNK:] + au_c[:, cs])
                s_new.append(s * e[:, cs] + _dot(kt_c[:, cs], v_new, _TN))
            o_ref[rows, :] = jnp.concatenate(o_new, axis=1)
            for h in range(HEADS):
                sall_ref[c, h] = states[h]
                s_scr[h] = s_new[h]
            return carry

        lax.fori_loop(0, nc, chunk, 0)

    row = pl.BlockSpec((rr, HEADS * LANE), lambda i: (i, 0))
    grid = (t // rr,)
    return _carried_call(
        body, rider, *_grid_ends(grid), name="delta_scan", grid=grid,
        in_specs=[row] * 5 + [pl.BlockSpec((nc, HEADS * LANE), lambda i: (i, 0))],
        out_specs=[row, pl.BlockSpec((nc, HEADS, LANE, LANE), lambda i: (i, 0, 0, 0))],
        out_shape=[jax.ShapeDtypeStruct((t, HEADS * LANE), F32),
                   jax.ShapeDtypeStruct((t // DN_CHUNK, HEADS, LANE, LANE), F32)],
        scratch_shapes=[pltpu.VMEM((HEADS, LANE, LANE), F32)], sem=("arbitrary",), args=(u, w, qd, kt, au, gl))


def _delta_scan_bwd(u, w, qd, kt, intra, gl, sall, do, rider=None):
    t = u.shape[0]
    rr = min(_SCAN_ROWS, t)
    nc = rr // DN_CHUNK
    ng = t // rr

    def body(u_ref, w_ref, qd_ref, kt_ref, a_ref, gl_ref, sall_ref, do_ref,
             du_ref, dw_ref, dqd_ref, dkt_ref, da_ref, dgl_ref, ds_scr):
        @pl.when(pl.program_id(0) == 0)
        def _():
            ds_scr[...] = jnp.zeros_like(ds_scr)

        def chunk(cc, carry):
            c = nc - 1 - cc
            r0 = pl.multiple_of(c * DN_CHUNK, DN_CHUNK)
            rows = pl.ds(r0, DN_CHUNK)
            e = jnp.exp(gl_ref[pl.ds(c, 1), :])
            states = [sall_ref[c, h] for h in range(HEADS)]
            ds_outs = [ds_scr[h] for h in range(HEADS)]
            u_a, w_a, kt_a, qd_a, do_a = u_ref[rows, :], w_ref[rows, :], kt_ref[rows, :], qd_ref[rows, :], do_ref[rows, :]
            a_a = [a_ref[h, rows, :] for h in range(HEADS)]
            da, dqd, dkt, du, dw, dgl, ds_new = [], [], [], [], [], [], []
            for h in range(HEADS):
                cs = slice(h * LANE, (h + 1) * LANE)
                s, ds_out = states[h], ds_outs[h]
                w_c, kt_c, qd_c, do_c = w_a[:, cs], kt_a[:, cs], qd_a[:, cs], do_a[:, cs]
                v_new = u_a[:, cs] - _dot(w_c, s, _NN)
                dv_new = _dot(a_a[h], do_c, _TN) + _dot(kt_c, ds_out, _NN)
                cots = jnp.concatenate([do_c, dv_new], axis=0)
                both = _dot(cots, s, _NT)
                dqd.append(both[:DN_CHUNK])
                dw.append(-both[DN_CHUNK:])
                da.append(_dot(do_c, v_new, _NT))
                dkt.append(_dot(v_new, ds_out, _NT))
                du.append(dv_new)
                eh = e[:, cs]
                dgl.append(jnp.broadcast_to(jnp.sum(ds_out * s, axis=0, keepdims=True) * eh, (8, LANE)))
                ds_new.append(ds_out * eh + _dot(jnp.concatenate([qd_c, -w_c], axis=0), cots, _TN))
            cat = lambda parts: jnp.concatenate(parts, axis=1)
            dqd_ref[rows, :], dkt_ref[rows, :], du_ref[rows, :], dw_ref[rows, :] = cat(dqd), cat(dkt), cat(du), cat(dw)
            dgl_ref[pl.ds(pl.multiple_of(c * 8, 8), 8), :] = cat(dgl)
            for h in range(HEADS):
                da_ref[h, rows, :] = da[h]
                ds_scr[h] = ds_new[h]
            return carry

        lax.fori_loop(0, nc, chunk, 0)

    rev = lambda i: (ng - 1 - i, 0)
    row = pl.BlockSpec((rr, HEADS * LANE), rev)
    a_spec = pl.BlockSpec((HEADS, rr, DN_CHUNK), lambda i: (0, ng - 1 - i, 0))
    gl_spec = pl.BlockSpec((nc, HEADS * LANE), rev)
    wide = jax.ShapeDtypeStruct((t, HEADS * LANE), F32)
    outs, carried = _carried_call(
        body, rider, *_grid_ends((ng,)), name="delta_scan_bwd", grid=(ng,),
        in_specs=[row] * 4 + [a_spec, gl_spec, pl.BlockSpec((nc, HEADS, LANE, LANE), lambda i: (ng - 1 - i, 0, 0, 0)), row],
        out_specs=[row] * 4 + [a_spec, pl.BlockSpec((nc * 8, HEADS * LANE), rev)],
        out_shape=[wide] * 4 + [jax.ShapeDtypeStruct((HEADS, t, DN_CHUNK), F32),
                                jax.ShapeDtypeStruct((t // DN_CHUNK * 8, HEADS * LANE), F32)],
        scratch_shapes=[pltpu.VMEM((HEADS, LANE, LANE), F32)], sem=("arbitrary",), args=(u, w, qd, kt, intra, gl, sall, do))
    return tuple(outs[:5]) + (outs[5].reshape(t // DN_CHUNK, 8, HEADS * LANE)[:, 0, :],), carried


_ATT_TILE = 512
_ATT_FWD_HEADS = 4
_ATT_BWD_HEADS = 4


def _kv_rows(j, tk):
    return pl.ds(pl.multiple_of(j * tk, tk), tk)


def _att_scores(ql, qr, ckv_ref, kr_ref, j, tk):
    ks = _kv_rows(j, tk)
    return (_dot(ql, ckv_ref[ks, :], _NT) + _dot(qr, kr_ref[ks, :], _NT)) * ATT_SCALE


def _diag_mask(s):
    qi = lax.broadcasted_iota(jnp.int32, s.shape, 0) % s.shape[1]
    ki = lax.broadcasted_iota(jnp.int32, s.shape, 1)
    return jnp.where(ki <= qi, s, NEG_BIG)


def _attention(qn, qr_pre, cosb, sinb, ckv, kr, wuk, wuv):
    t = ckv.shape[0]
    tq = min(_ATT_TILE, t)
    hp = _ATT_FWD_HEADS
    nl = tq // LANE

    def lane_fold(v, op):
        out = v[:, :LANE]
        for k in range(1, nl):
            out = op(out, v[:, k * LANE:(k + 1) * LANE])
        return out

    def body(qn_ref, qr_ref, cos_ref, sin_ref, ckv_ref, kr_ref, wuk_ref, wuv_ref, o_ref, lse_ref, qrope_ref, omla_ref,
             s_all, m_lanes, l_lanes, acc_scr):
        hg, qi = pl.program_id(0), pl.program_id(1)
        lanes = lambda ref, k: ref[:, k * LANE:(k + 1) * LANE]
        rows = lambda parts: jnp.concatenate(parts, axis=0)
        part = lambda v, k: v[k * tq:(k + 1) * tq]
        q_lat = rows([_dot(lanes(qn_ref, k), wuk_ref[hg * hp + k], _NT) for k in range(hp)]).astype(_CDT)
        roped = [_rope(lanes(qr_ref, k), cos_ref[...], sin_ref[...]).astype(qrope_ref.dtype) for k in range(hp)]
        qrope_ref[...] = jnp.concatenate(roped, axis=1)
        q_rope = rows(roped)
        m_lanes[...] = jnp.full_like(m_lanes, NEG_BIG)

        def scores(j, masked):
            s = _att_scores(q_lat, q_rope, ckv_ref, kr_ref, j, tq)
            if masked:
                s = _diag_mask(s)
            s_all[j] = s
            m_lanes[...] = jnp.maximum(m_lanes[...], lane_fold(s, jnp.maximum))

        def scores_body(j, carry):
            scores(j, False)
            return carry

        lax.fori_loop(0, qi, scores_body, 0)
        scores(qi, True)
        m = jnp.max(m_lanes[...], axis=-1, keepdims=True)
        mb = jnp.broadcast_to(m, (hp * tq, LANE))
        l_lanes[...] = jnp.zeros_like(l_lanes)
        acc_scr[...] = jnp.zeros_like(acc_scr)

        def weigh(j, carry):
            s = s_all[j]
            p = jnp.concatenate([jnp.exp(s[:, k * LANE:(k + 1) * LANE] - mb) for k in range(nl)], axis=1)
            l_lanes[...] += lane_fold(p, jnp.add)
            acc_scr[...] += _dot(p, ckv_ref[_kv_rows(j, tq), :], _NN)
            return carry

        lax.fori_loop(0, qi + 1, weigh, 0)
        l = jnp.sum(l_lanes[...], axis=-1, keepdims=True)
        out = acc_scr[...] / l
        lse_v = m + jnp.log(l)
        for k in range(hp):
            o_ref[k] = part(out, k)
            lse_ref[k] = part(lse_v, k)
        omla_ref[...] = jnp.concatenate(
            [_dot(part(out, k), wuv_ref[hg * hp + k], _NN) for k in range(hp)], axis=1).astype(omla_ref.dtype)

    col = pl.BlockSpec((tq, hp * LANE), lambda h, i: (i, h))
    table = pl.BlockSpec((tq, ROPE_PAD), lambda h, i: (i, 0))
    wspec = pl.BlockSpec((HEADS, KV_LORA, NOPE), lambda h, i: (0, 0, 0))
    return pl.pallas_call(
        body, name="attention", grid=(HEADS // hp, t // tq),
        in_specs=[col, col, table, table, pl.BlockSpec((t, KV_LORA), lambda h, i: (0, 0)),
                  pl.BlockSpec((t, ROPE_PAD), lambda h, i: (0, 0)), wspec, wspec],
        out_specs=[pl.BlockSpec((hp, tq, KV_LORA), lambda h, i: (h, i, 0)),
                   pl.BlockSpec((hp, tq, 1), lambda h, i: (h, i, 0)), col, col],
        out_shape=[jax.ShapeDtypeStruct((HEADS, t, KV_LORA), F32), jax.ShapeDtypeStruct((HEADS, t, 1), F32),
                   jax.ShapeDtypeStruct((t, HEADS * ROPE_PAD), _CDT), jax.ShapeDtypeStruct((t, HEADS * NOPE), _CDT)],
        scratch_shapes=[pltpu.VMEM((t // tq, hp * tq, tq), F32), pltpu.VMEM((hp * tq, LANE), F32),
                        pltpu.VMEM((hp * tq, LANE), F32), pltpu.VMEM((hp * tq, KV_LORA), F32)],
        compiler_params=_cparams(("parallel", "parallel")))(qn, qr_pre, cosb, sinb, ckv, kr, wuk, wuv)


def _attention_bwd(qn, qr, cosb, sinb, ckv, kr, wuk, wuv, out, lse, do_mla):
    t = ckv.shape[0]
    tq = min(_ATT_TILE, t)
    hp = _ATT_BWD_HEADS

    def body(qn_ref, qr_ref, cos_ref, sin_ref, ckv_ref, kr_ref, wuk_ref, wuv_ref, o_ref, lse_ref, do_ref,
             dql_ref, dqr_ref, dqn_ref, dckv_ref, dkr_ref, dql_scr, dqr_scr):
        hg, qi = pl.program_id(0), pl.program_id(1)

        @pl.when((hg == 0) & (qi == 0))
        def _():
            dckv_ref[...] = jnp.zeros_like(dckv_ref)
            dkr_ref[...] = jnp.zeros_like(dkr_ref)

        lanes = lambda ref, k: ref[:, k * LANE:(k + 1) * LANE]
        rows = lambda parts: jnp.concatenate(parts, axis=0)
        q_lat = rows([_dot(lanes(qn_ref, k), wuk_ref[hg * hp + k], _NT) for k in range(hp)]).astype(_CDT)
        q_rope = rows([lanes(qr_ref, k) for k in range(hp)])
        d_out = rows([_dot(lanes(do_ref, k), wuv_ref[hg * hp + k], _NT) for k in range(hp)])
        d_o = d_out.astype(_CDT)
        lse_v = rows([lse_ref[k] for k in range(hp)])
        dsum = jnp.sum(d_out * rows([o_ref[k] for k in range(hp)]), axis=-1, keepdims=True)
        dql_scr[...] = jnp.zeros_like(dql_scr)
        dqr_scr[...] = jnp.zeros_like(dqr_scr)

        def step(j, masked):
            ks = _kv_rows(j, tq)
            s = _att_scores(q_lat, q_rope, ckv_ref, kr_ref, j, tq)
            if masked:
                s = _diag_mask(s)
            p = jnp.exp(s - lse_v)
            kv = ckv_ref[ks, :]
            ds = (p * (_dot(d_o, kv, _NT) - dsum) * ATT_SCALE).astype(_CDT)
            pb = p.astype(_CDT)
            dql_scr[...] += _dot(ds, kv, _NN)
            dqr_scr[...] += _dot(ds, kr_ref[ks, :], _NN)
            dckv_ref[ks, :] += _dot(pb, d_o, _TN) + _dot(ds, q_lat, _TN)
            dkr_ref[ks, :] += _dot(ds, q_rope, _TN)

        def loop_body(j, carry):
            step(j, False)
            return carry

        lax.fori_loop(0, qi, loop_body, 0)
        step(qi, True)
        cols = lambda parts: jnp.concatenate(parts, axis=1)
        part = lambda v, k: v[k * tq:(k + 1) * tq]
        dql = dql_scr[...].astype(dql_ref.dtype)
        dqr = dqr_scr[...]
        for k in range(hp):
            dql_ref[k] = part(dql, k)
        dqn_ref[...] = cols([_dot(part(dql, k), wuk_ref[hg * hp + k], _NN) for k in range(hp)]).astype(dqn_ref.dtype)
        dqr_ref[...] = cols([_rope_bwd(part(dqr, k), cos_ref[...], sin_ref[...]) for k in range(hp)]).astype(dqr_ref.dtype)

    lat = pl.BlockSpec((hp, tq, KV_LORA), lambda h, i: (h, i, 0))
    col = pl.BlockSpec((tq, hp * LANE), lambda h, i: (i, h))
    table = pl.BlockSpec((tq, ROPE_PAD), lambda h, i: (i, 0))
    kfull = pl.BlockSpec((t, KV_LORA), lambda h, i: (0, 0))
    rfull = pl.BlockSpec((t, ROPE_PAD), lambda h, i: (0, 0))
    wspec = pl.BlockSpec((HEADS, KV_LORA, NOPE), lambda h, i: (0, 0, 0))
    wide = jax.ShapeDtypeStruct((t, HEADS * LANE), _CDT)
    return pl.pallas_call(
        body, name="attention_bwd", grid=(HEADS // hp, t // tq),
        in_specs=[col, col, table, table, kfull, rfull, wspec, wspec, lat,
                  pl.BlockSpec((hp, tq, 1), lambda h, i: (h, i, 0)), col],
        out_specs=[lat, col, col, kfull, rfull],
        out_shape=[jax.ShapeDtypeStruct((HEADS, t, KV_LORA), _CDT), wide, wide,
                   jax.ShapeDtypeStruct((t, KV_LORA), F32), jax.ShapeDtypeStruct((t, ROPE_PAD), F32)],
        scratch_shapes=[pltpu.VMEM((hp * tq, KV_LORA), F32), pltpu.VMEM((hp * tq, ROPE_PAD), F32)],
        compiler_params=_cparams(("arbitrary", "arbitrary")))(qn, qr, cosb, sinb, ckv, kr, wuk, wuv, out, lse, do_mla)


_DX_ROWS = 512


def _dx_fused(pairs, add, add_scale, rider=None, name="b_dx"):
    t, d = add.shape
    tm = min(_DX_ROWS, t)
    n = len(pairs)

    def body(*refs):
        acc = refs[2 * n][...] * add_scale
        for i in range(n):
            acc = acc + _dot(refs[i][...], refs[n + i][...], _NT)
        refs[2 * n + 1][...] = acc

    in_specs = [pl.BlockSpec((tm, a.shape[1]), lambda i: (i, 0)) for a, _ in pairs]
    in_specs += [pl.BlockSpec(w.shape, lambda i: (0, 0), pipeline_mode=pl.Buffered(1)) for _, w in pairs]
    row = pl.BlockSpec((tm, d), lambda i: (i, 0))
    grid = (t // tm,)
    (dx,), carried = _carried_call(
        body, rider, *_grid_ends(grid), name=name, grid=grid, in_specs=in_specs + [row], out_specs=[row],
        out_shape=[jax.ShapeDtypeStruct((t, d), F32)], scratch_shapes=[], sem=("arbitrary",),
        args=tuple(a for a, _ in pairs) + tuple(w for _, w in pairs) + (add,))
    return dx, carried


def _ffn_in_swiglu(a, w):
    t, k = a.shape
    hid = w.shape[1] // 2
    tm, tn = _tile(t, 1024), _tile(hid, 1408)
    nj = hid // tn

    def body(a_ref, bg_ref, bu_ref, act_ref, gt_ref, up_ref):
        av = a_ref[...].astype(_CDT)
        gt = jnp.dot(av, bg_ref[...].astype(_CDT), preferred_element_type=F32)
        up = jnp.dot(av, bu_ref[...].astype(_CDT), preferred_element_type=F32)
        act_ref[...] = _swiglu(gt, up).astype(act_ref.dtype)
        gt_ref[...] = gt.astype(gt_ref.dtype)
        up_ref[...] = up.astype(up_ref.dtype)

    out = pl.BlockSpec((tm, tn), lambda i, j: (i, j))
    return pl.pallas_call(
        body, name="f_ffn_in_swiglu", grid=(t // tm, nj),
        in_specs=[pl.BlockSpec((tm, k), lambda i, j: (i, 0)), pl.BlockSpec((k, tn), lambda i, j: (0, j)),
                  pl.BlockSpec((k, tn), lambda i, j: (0, nj + j))],
        out_specs=[out] * 3, out_shape=[jax.ShapeDtypeStruct((t, hid), _CDT)] * 3,
        compiler_params=_cparams(("parallel", "parallel")))(a, w, w)


def _gated_norm(o, z, w):
    return _rms_norm(o, w) * _silu(z)


def _gated_norm_heads(o, z, w):
    heads = [_gated_norm(o[:, h * LANE:(h + 1) * LANE], z[:, h * LANE:(h + 1) * LANE], w) for h in range(HEADS)]
    return jnp.concatenate(heads, axis=1)


def _mla_pre(ckv, krp, cq, cosb, sinb, qw, kw):
    return _rms_norm(cq, qw), _rms_norm(ckv, kw), _rope(krp, cosb, sinb)


def _merge(gg, y_dn, y_mla):
    return _sigmoid(gg[:, :D_MODEL]) * y_dn + _sigmoid(gg[:, D_MODEL:]) * y_mla


def _ln1(xv, attn_out, g, b):
    return _layer_norm(ALPHA * xv + attn_out, g, b)


def _final(h1, ffn, gate_pre, ple_proj, g, b):
    return _layer_norm(ALPHA * h1 + ffn + _sigmoid(gate_pre) * ple_proj, g, b)


def _swiglu(gt, up):
    return _silu(gt) * up


def _local_step(x, p, cosb, sinb, target, wt, sp, exch):
    t = x.shape[0]
    bf = _CDT
    xb = x.astype(bf)
    g = {}

    qkv_pre = _mm2(xb, wt['qkv'], name="f_qkv")
    z = _mm2(xb, wt['z'], name="f_z")
    gg = _mm2(xb, wt['gg'], name="f_gg")
    pm = _mm2(xb, wt['mla'], name="f_mla")
    qkv_act = _conv_silu(qkv_pre, sp['conv_w'])
    (u, w_, qd, kt, intra, gl, t_inv, qd2, au), sent = _delta_local(qkv_act, pm, sp['a_log'], sp['dt_bias'],
                                                                    rider=exch.gather_send())
    (o_dn, sall), passed = _delta_scan(u, w_, qd2, kt, au, gl, rider=exch.gather_pass(sent))
    wt = dict(wt, **exch.weights(passed))
    def gated_norm_br(h, o, zz, w, wbr):
        og_v = _gated_norm_heads(o, zz, w).astype(bf)
        return og_v, jnp.dot(og_v, wbr.astype(bf), preferred_element_type=F32)

    og, y_dn = _rowwise(gated_norm_br, [(o_dn, D_MODEL, 0, False), (z, D_MODEL, 0, False)],
                        [sp['dn_norm_w'], wt['br_dn']],
                        [(D_MODEL, D_MODEL, False, bf), (D_MODEL, D_MODEL, False, F32)], name="f_gated_norm_br", tm=512)

    def mla_pre_uq(h, ckv, krp, cq, cosv, sinv, qw, kw, wn, wr):
        c_q_v, c_kv_v, k_rope_v = _mla_pre(ckv, krp, cq, cosv, sinv, qw, kw)
        c_q_b = c_q_v.astype(bf)
        return (c_q_b, c_kv_v, k_rope_v, jnp.dot(c_q_b, wn.astype(bf), preferred_element_type=F32),
                jnp.dot(c_q_b, wr.astype(bf), preferred_element_type=F32))

    c_q, c_kv, k_rope, q_nope, q_rope_pre = _rowwise(
        mla_pre_uq,
        [(pm, KV_LORA, 0, False), (pm, ROPE_PAD, 2, False), (pm, Q_LORA, 2, False),
         (cosb, ROPE_PAD, 0, False), (sinb, ROPE_PAD, 0, False)],
        [sp['q_norm_w'], sp['kv_norm_w'], wt['uq_nope'], wt['uq_rope']],
        [(Q_LORA, Q_LORA, False, bf), (KV_LORA, KV_LORA, False, bf), (ROPE_PAD, ROPE_PAD, False, bf),
         (HEADS * NOPE, HEADS * NOPE, False, bf), (HEADS * ROPE_PAD, HEADS * ROPE_PAD, False, F32)], name="f_mla_pre_uq",
        tm=512)
    out_lat, lse, q_rope, o_mla = _attention(q_nope, q_rope_pre, cosb, sinb, c_kv, k_rope, wt['uk'], wt['uv'])
    y_mla = _mm2(o_mla, wt['br_mla'], name="f_br_mla")

    def merge_o_ln1(h, ggv, yd, ym, xv, wo, gv, bv):
        mixed_v = _merge(ggv, yd, ym).astype(bf)
        ao = jnp.dot(mixed_v, wo.astype(bf), preferred_element_type=F32)
        h1v = _ln1(xv, ao, gv, bv)
        return mixed_v, ao, h1v, h1v

    mixed, attn_out, h1, h1b = _rowwise(
        merge_o_ln1, [(gg, 2 * D_MODEL, 0, False), (y_dn, D_MODEL, 0, False), (y_mla, D_MODEL, 0, False), (x, D_MODEL, 0, False)],
        [wt['o'], sp['ln1_g'], sp['ln1_b']],
        [(D_MODEL, D_MODEL, False, bf), (D_MODEL, D_MODEL, False, F32), (D_MODEL, D_MODEL, False, F32),
         (D_MODEL, D_MODEL, False, bf)], name="f_merge_o_ln1", tm=512)
    act, ffn_gt, ffn_up = _ffn_in_swiglu(h1b, wt['ffn_in'])
    pb = p.astype(bf)

    def final_fn(h, h1v, actv, pv, tgt, gt, up, wfo, wpg, wpl, gv, bv):
        ffnv = jnp.dot(actv.astype(bf), wfo.astype(bf), preferred_element_type=F32)
        gpv = jnp.dot(h1v.astype(bf), wpg.astype(bf), preferred_element_type=F32)
        ppv = jnp.dot(pv.astype(bf), wpl.astype(bf), preferred_element_type=F32)
        y, vjp = jax.vjp(_final, h1v, ffnv, gpv, ppv, gv, bv)
        err = y - tgt
        dh1, dffn, dgp, dpp, dg, db = vjp(err * (1.0 / D_MODEL))
        sq = err * err
        lanes = sq[:, :LANE]
        for j in range(1, D_MODEL // LANE):
            lanes = lanes + sq[:, j * LANE:(j + 1) * LANE]
        loss = jnp.sum(lanes, axis=0, keepdims=True) * (0.5 / D_MODEL)
        dffn_b = dffn.astype(bf)
        dact = _dot(dffn_b, wfo, _NT).astype(bf).astype(F32)
        _, vjp_s = jax.vjp(_swiglu, gt.astype(F32), up.astype(F32))
        dgt, dup = vjp_s(dact)
        return dffn, dffn_b, dgp, dpp, jnp.concatenate([dgt, dup], axis=1), dg, db, loss

    dpre2, dpre2b, dgate_pre, dple_proj, dffn_in, g['ln2_g'], g['ln2_b'], loss_lanes = _rowwise(
        final_fn, [(h1, D_MODEL, 0, False), (act, FFN_HIDDEN, 0, False), (pb, PLE_DIM, 0, False), (target, D_MODEL, 0, False),
                   (ffn_gt, FFN_HIDDEN, 0, False), (ffn_up, FFN_HIDDEN, 0, False)],
        [wt['ffn_out'], wt['ple_gate'], wt['ple'], sp['ln2_g'], sp['ln2_b']],
        [(D_MODEL, D_MODEL, False, F32)] + [(D_MODEL, D_MODEL, False, bf)] * 3 + [(2 * FFN_HIDDEN, 2 * FFN_HIDDEN, False, bf)],
        [(1, D_MODEL), (1, D_MODEL), (1, LANE)], name="b_final")
    g['loss_lanes'] = loss_lanes

    g['ple'] = _mm2(pb, dple_proj, ta=True, name="g_ple")
    g['ple_gate'] = _mm2(h1b, dgate_pre, ta=True, name="g_ple_gate")
    g['ffn_out'] = _mm2(act, dpre2b, ta=True, name="g_ffn_out")
    g['ffn_in'] = _mm2(h1b, dffn_in, ta=True, name="g_ffn_in")
    dh1, _ = _dx_fused([(dffn_in, wt['ffn_in']), (dgate_pre, wt['ple_gate'])], dpre2, ALPHA, name="b_dh1")

    def attn_out_bwd(h, xv, ao, d, ggv, yd, ym, o, zz, wo, wbd, wbm, gv, bv, nw):
        _, vjp = jax.vjp(_ln1, xv, ao, gv, bv)
        _, dao, dg, db = vjp(d)
        dao_b = dao.astype(bf)
        _, vjp_m = jax.vjp(_merge, ggv, yd, ym)
        dggv, dyd, dym = vjp_m(_dot(dao_b, wo, _NT))
        dyd_b, dym_b = dyd.astype(bf), dym.astype(bf)
        _, vjp_n = jax.vjp(_gated_norm_heads, o, zz, nw)
        do_v, dz_v, dnw = vjp_n(_dot(dyd_b, wbd, _NT))
        return dao, dao_b, dggv, dyd_b, dym_b, do_v, dz_v, _dot(dym_b, wbm, _NT), dg, db, dnw

    row_d = lambda a: (a, D_MODEL, 0, False)
    dpre1, dpre1b, dgg, dy_dn, dy_mla, do_dn, dz, do_mla, g['ln1_g'], g['ln1_b'], g['dn_norm_w'] = _rowwise(
        attn_out_bwd, [row_d(x), row_d(attn_out), row_d(dh1), (gg, 2 * D_MODEL, 0, False), row_d(y_dn), row_d(y_mla),
                       row_d(o_dn), row_d(z)],
        [wt['o'], wt['br_dn'], wt['br_mla'], sp['ln1_g'], sp['ln1_b'], sp['dn_norm_w']],
        [(D_MODEL, D_MODEL, False, F32), (D_MODEL, D_MODEL, False, bf), (2 * D_MODEL, 2 * D_MODEL, False, bf),
         (D_MODEL, D_MODEL, False, bf), (D_MODEL, D_MODEL, False, bf), (D_MODEL, D_MODEL, False, F32),
         (D_MODEL, D_MODEL, False, bf), (D_MODEL, D_MODEL, False, bf)],
        [(1, D_MODEL), (1, D_MODEL), (1, LANE)], name="b_attn_out")
    g['o'] = _mm2(mixed, dpre1b, ta=True, name="g_o")
    g['br_dn'] = _mm2(og, dy_dn, ta=True, name="g_br_dn")
    g['br_mla'] = _mm2(o_mla, dy_mla, ta=True, name="g_br_mla")

    g['uv'] = _mm(out_lat, do_mla, name="g_uv", ta=True, heads=HEADS, a_head='lead', b_head='col', out_head='lead',
                  dims=(KV_LORA, NOPE, t))
    dq_lat, dq_rope_pre, dq_nope, dckv_att, dkr_att = _attention_bwd(
        q_nope, q_rope, cosb, sinb, c_kv, k_rope, wt['uk'], wt['uv'], out_lat, lse, do_mla)
    g['uk'] = _mm(dq_lat, q_nope, name="g_uk", ta=True, heads=HEADS, a_head='lead', b_head='col', out_head='lead',
                  dims=(KV_LORA, NOPE, t))
    g['uq_nope'] = _mm2(c_q, dq_nope, ta=True, name="g_uq_nope")
    g['uq_rope'] = _mm2(c_q, dq_rope_pre, ta=True, name="g_uq_rope")
    dc_q = _mm2(dq_nope, wt['uq_nope'], tb=True, name="b_dcq_nope")
    dc_q = _mm2(dq_rope_pre, wt['uq_rope'], tb=True, name="b_dcq_rope", add=dc_q)

    (du, dw, dqd, dkt, dintra, dgl), paired = _delta_scan_bwd(u, w_, qd, kt, intra, gl, sall, do_dn, rider=exch.pair_send(g))
    (dq_a, dk_a, dv_a, dba, g['a_log'], g['dt_bias']), arrived = _delta_local_bwd(
        qkv_act, pm, sp['a_log'], sp['dt_bias'], t_inv, du, dw, dqd, dkt, dintra, dgl, rider=exch.reduce_send(paired))
    exch.reduce_arrived(arrived)
    (dqkv_pre, g['conv_w']), shared = _conv_silu_bwd(qkv_pre, sp['conv_w'], [dq_a, dk_a, dv_a], rider=exch.share_send())
    exch.share_arrived(shared)

    def mla_pre_bwd(h, ckv, cq, cosv, sinv, dcq, dckv, dkr, dba_v, qw, kw):
        _, vjp = jax.vjp(lambda a, c, d, e: (_rms_norm(c, d), _rms_norm(a, e)), ckv, cq, qw, kw)
        dckv_p, dcq_p, dqw, dkw = vjp((dcq, dckv))
        dkr_p = _rope_bwd(dkr, cosv, sinv)
        dpm = jnp.concatenate([dckv_p, dkr_p, dba_v, jnp.zeros((ckv.shape[0], 2 * LANE), F32), dcq_p], axis=1)
        return dpm, dqw, dkw

    dpm, g['q_norm_w'], g['kv_norm_w'] = _rowwise(
        mla_pre_bwd,
        [(pm, KV_LORA, 0, False), (pm, Q_LORA, 2, False), (cosb, ROPE_PAD, 0, False), (sinb, ROPE_PAD, 0, False),
         (dc_q, Q_LORA, 0, False), (dckv_att, KV_LORA, 0, False), (dkr_att, ROPE_PAD, 0, False), (dba, LANE, 0, False)],
        [sp['q_norm_w'], sp['kv_norm_w']], [(1152, 1152, False, bf)], [(1, Q_LORA), (1, KV_LORA)], name="b_mla_pre")

    rider = exch.small_send(g)
    if rider is None:
        g['qkv'] = _mm2(xb, dqkv_pre, ta=True, name="g_qkv")
    else:
        g['qkv'], got = _mm2(xb, dqkv_pre, ta=True, name="g_qkv", rider=rider)
        exch.small_arrived(got)
    g['z'] = _mm2(xb, dz, ta=True, name="g_z")
    g['gg'] = _mm2(xb, dgg, ta=True, name="g_gg")
    g['mla'] = _mm2(xb, dpm, ta=True, name="g_mla")
    dx, arrived = _dx_fused([(dqkv_pre, wt['qkv']), (dz, wt['z']), (dgg, wt['gg']), (dpm, wt['mla'])], dpre1, ALPHA,
                            rider=exch.in_send(g))
    exch.in_arrived(arrived)
    return loss_lanes, dx, g


_IN_SIZES = (QKV_W, HEADS * DN_DK, HEADS, HEADS, Q_LORA, KV_LORA, ROPE, D_MODEL, D_MODEL)


def _rope_tables(positions):
    inv_freq = ROPE_BASE ** (-jnp.arange(0, ROPE, 2, dtype=F32) / ROPE)
    ang = positions.astype(F32)[:, None] * inv_freq
    cos, sin = jnp.cos(ang), jnp.sin(ang)
    zeros = jnp.zeros((positions.shape[0], ROPE_PAD - ROPE), F32)
    return jnp.concatenate([cos, cos, zeros], axis=1), jnp.concatenate([-sin, sin, zeros], axis=1)


def _prep_w_in(w_in):
    dt = w_in.dtype
    offs = [0]
    for s in _IN_SIZES:
        offs.append(offs[-1] + s)
    qkv, z, wb, wa, cq, ckv, kr, gd, gm = [w_in[:, offs[i]:offs[i + 1]] for i in range(len(_IN_SIZES))]
    zc = lambda n: jnp.zeros((D_MODEL, n), dt)
    return {
        'qkv': qkv, 'z': z, 'gg': jnp.concatenate([gd, gm], axis=1),
        'mla': jnp.concatenate([ckv, kr, zc(ROPE_PAD - ROPE), wb, wa, zc(LANE - 2 * HEADS), zc(2 * LANE), cq], axis=1),
    }


def _prep_weights(full):
    w_uq = full['w_uq']
    wt = {
        'uq_nope': w_uq[:, :, :NOPE].reshape(Q_LORA, HEADS * NOPE),
        'uq_rope': jnp.pad(w_uq[:, :, NOPE:], ((0, 0), (0, 0), (0, ROPE_PAD - ROPE))).reshape(Q_LORA, HEADS * ROPE_PAD),
        'uk': jnp.transpose(full['w_uk'], (1, 0, 2)), 'uv': jnp.transpose(full['w_uv'], (1, 0, 2)),
        'br_dn': full['w_br_dn'], 'br_mla': full['w_br_mla'], 'o': full['w_o'], 'ffn_in': full['w_ffn_in'],
        'ffn_out': full['w_ffn_out'], 'ple': full['w_ple'], 'ple_gate': full['w_ple_gate'],
    }
    return wt


def _prep_small(small):
    pad = lambda v: jnp.pad(v, (0, LANE - v.shape[0]))[None, :]
    return {
        'conv_w': small['conv_w'], 'a_log': pad(small['dn_a_log']), 'dt_bias': pad(small['dn_dt_bias']),
        'dn_norm_w': small['dn_norm_w'][None, :], 'q_norm_w': small['q_norm_w'][None, :],
        'kv_norm_w': small['kv_norm_w'][None, :], 'ln1_g': small['ln1_g'][None, :], 'ln1_b': small['ln1_b'][None, :],
        'ln2_g': small['ln2_g'][None, :], 'ln2_b': small['ln2_b'][None, :],
    }


def _w_in_grad(g):
    mla = g['mla']
    ba0 = KV_LORA + ROPE_PAD
    cq0 = ba0 + 3 * LANE
    return jnp.concatenate([
        g['qkv'], g['z'], mla[:, ba0:ba0 + HEADS], mla[:, ba0 + HEADS:ba0 + 2 * HEADS], mla[:, cq0:cq0 + Q_LORA],
        mla[:, :KV_LORA], mla[:, KV_LORA:KV_LORA + ROPE], g['gg']], axis=1)


def _unprep_grads_late(g):
    return {
        'conv_w': g['conv_w'], 'dn_a_log': g['a_log'][0, :HEADS], 'dn_dt_bias': g['dt_bias'][0, :HEADS],
        'dn_norm_w': g['dn_norm_w'][0], 'q_norm_w': g['q_norm_w'][0], 'kv_norm_w': g['kv_norm_w'][0],
        'ln1_g': g['ln1_g'][0], 'ln1_b': g['ln1_b'][0], 'ln2_g': g['ln2_g'][0], 'ln2_b': g['ln2_b'][0],
    }


def _unprep_grads_early(g):
    w_uq = jnp.concatenate([g['uq_nope'].reshape(Q_LORA, HEADS, NOPE),
                            g['uq_rope'].reshape(Q_LORA, HEADS, ROPE_PAD)[:, :, :ROPE]], axis=2)
    return {
        'w_uq': w_uq, 'w_uk': jnp.transpose(g['uk'], (1, 0, 2)), 'w_uv': jnp.transpose(g['uv'], (1, 0, 2)),
        'w_br_dn': g['br_dn'], 'w_br_mla': g['br_mla'], 'w_o': g['o'],
        'w_ffn_in': g['ffn_in'], 'w_ffn_out': g['ffn_out'], 'w_ple': g['ple'], 'w_ple_gate': g['ple_gate'],
    }


_FLATB_PIECES = (
    ('w_ffn_out', 704, (704, D_MODEL)), ('w_br_dn', 256, (256, D_MODEL)), ('w_br_mla', 256, (256, D_MODEL)),
    ('w_o', 256, (256, D_MODEL)), ('w_ple_gate', 256, (256, D_MODEL)), ('w_uq', 144, (96, HEADS, NOPE + ROPE)),
    ('w_uk', 64, (64, HEADS, NOPE)), ('w_uv', 64, (64, HEADS, NOPE)), ('w_ple', 64, (PLE_DIM, 256)),
)
FLATB_ROWS = 2112
W_IN_SHARD = D_IN // N_SHARD
FFN_IN_SHARD = 2 * FFN_HIDDEN // N_SHARD
A_ROWS = D_MODEL + 32
_CONV_SHARD = QKV_W // N_SHARD
_ADD_TILES = (256, 256, 352)


def _flatb_offsets():
    offs, o = {}, 0
    for name, rows, _ in _FLATB_PIECES:
        offs[name] = o
        o += rows
    return offs, o


def _pack_shards(ws, conv_w):
    conv_bits = lax.bitcast_convert_type(conv_w, jnp.bfloat16).reshape(DN_CONV, 2 * _CONV_SHARD).astype(_CDT)
    tail = jnp.pad(conv_bits, ((0, A_ROWS - D_MODEL - DN_CONV), (0, W_IN_SHARD - 2 * _CONV_SHARD)))
    a_buf = jnp.concatenate([ws['w_in'].astype(_CDT), tail], axis=0)
    parts = [ws[name].astype(_CDT).reshape(rows, FLAT_W) for name, rows, _ in _FLATB_PIECES]
    used = sum(p.shape[0] for p in parts)
    parts.append(jnp.zeros((FLATB_ROWS - used, FLAT_W), _CDT))
    return [a_buf, ws['w_ffn_in'].astype(_CDT), jnp.concatenate(parts, axis=0)]


def _unpack_w_in(gathered, local, me):
    a = [jnp.where(me == s, local, gathered[s]) for s in range(N_SHARD)]
    conv = [lax.bitcast_convert_type(
        p[D_MODEL:D_MODEL + DN_CONV, :2 * _CONV_SHARD].astype(jnp.bfloat16).reshape(DN_CONV, _CONV_SHARD, 2), F32) for p in a]
    return jnp.concatenate([p[:D_MODEL] for p in a], axis=1), jnp.concatenate(conv, axis=1)


def _unpack_rest(gathered, local, me):
    pick = lambda b, s: jnp.where(me == s, local[b], gathered[b][s])
    full = {'w_ffn_in': jnp.concatenate([pick(0, s) for s in range(N_SHARD)], axis=1)}
    offs, _ = _flatb_offsets()
    fb = [pick(1, s) for s in range(N_SHARD)]
    for name, rows, shape in _FLATB_PIECES:
        pieces = [p[offs[name]:offs[name] + rows].reshape(shape) for p in fb]
        full[name] = jnp.concatenate(pieces, axis=1 if name == 'w_ple' else 0)
    return full


def _shard_columns(g, w):
    return jnp.stack([g[:, s * w:(s + 1) * w] for s in range(N_SHARD)])


def _pack_grads_rest(gw):
    parts = []
    for name, rows, _ in _FLATB_PIECES:
        g = gw[name]
        if name == 'w_ple':
            parts.append(_shard_columns(g, PLE_DIM).reshape(N_SHARD, rows, FLAT_W))
        else:
            parts.append(g.reshape(N_SHARD, rows, FLAT_W))
    used = sum(p.shape[1] for p in parts)
    parts.append(jnp.zeros((N_SHARD, FLATB_ROWS - used, FLAT_W), F32))
    return [_shard_columns(gw['w_ffn_in'], FFN_IN_SHARD), jnp.concatenate(parts, axis=1)]


def _unpack_reduced(mine, theirs, c):
    whole = [jnp.concatenate([jnp.where(c == 0, m, t), jnp.where(c == 0, t, m)], axis=0) for m, t in zip(mine, theirs)]
    out = {'w_in': whole[0], 'w_ffn_in': whole[1]}
    offs, _ = _flatb_offsets()
    for name, rows, shape in _FLATB_PIECES:
        out[name] = whole[2][offs[name]:offs[name] + rows].reshape(shape)
    return out


_HBM = pl.BlockSpec(memory_space=pltpu.HBM)


def _place():
    x, y, c = lax.axis_index("x"), lax.axis_index("y"), lax.axis_index("c")
    chips = [(1 - x, y), (x, 1 - y), (1 - x, 1 - y)]
    return x, y, c, chips


def _remote(src, dst, send_sems, recv_sems, k, to):
    return pltpu.make_async_remote_copy(src_ref=src, dst_ref=dst, send_sem=send_sems.at[k], recv_sem=recv_sems.at[k],
                                        device_id=to, device_id_type=_MESH)


def _half_rows(ref, half, hf, lead=None):
    rows = pl.ds(pl.multiple_of(hf * half, 16), half)
    return ref.at[rows, :] if lead is None else ref.at[lead, rows, :]


class _Rider:
    def __init__(self, inputs, out_shape, n_sems, copies, aliases=None):
        self.inputs, self.out_shape, self.n_sems, self.copies = list(inputs), list(out_shape), n_sems, copies
        self.aliases = aliases or {}


def _carried_call(body, rider, first, last, *, name, grid, in_specs, out_specs, out_shape, scratch_shapes, sem, args):
    n_in, n_out, n_scr = len(in_specs), len(out_specs), len(scratch_shapes)
    if rider is None:
        res = pl.pallas_call(body, name=name, grid=grid, in_specs=in_specs, out_specs=out_specs, out_shape=out_shape,
                             scratch_shapes=scratch_shapes, compiler_params=_cparams(sem))(*args)
        return list(res), []
    ri, ro = len(rider.inputs), len(rider.out_shape)

    def full_body(*refs):
        own_in, r_in = refs[:n_in], refs[n_in:n_in + ri]
        o0 = n_in + ri
        own_out, r_out = refs[o0:o0 + n_out], refs[o0 + n_out:o0 + n_out + ro]
        s0 = o0 + n_out + ro
        own_scr, send_sems, recv_sems = refs[s0:s0 + n_scr], refs[s0 + n_scr], refs[s0 + n_scr + 1]

        @pl.when(first())
        def _():
            sends, _ = rider.copies(r_in, r_out, send_sems, recv_sems)
            for cp in sends:
                cp.start()

        body(*own_in, *own_out, *own_scr)

        @pl.when(last())
        def _():
            sends, arrivals = rider.copies(r_in, r_out, send_sems, recv_sems)
            for cp in arrivals():
                cp.wait_recv()
            for cp in sends:
                cp.wait_send()

    res = pl.pallas_call(
        full_body, name=name, grid=grid, in_specs=list(in_specs) + [_HBM] * ri, out_specs=list(out_specs) + [_HBM] * ro,
        out_shape=list(out_shape) + rider.out_shape,
        scratch_shapes=list(scratch_shapes) + [pltpu.SemaphoreType.DMA((rider.n_sems,))] * 2,
        input_output_aliases={n_in + i: n_out + o for i, o in rider.aliases.items()},
        compiler_params=_cparams(sem))(*args, *rider.inputs)
    return list(res[:n_out]), list(res[n_out:])


def _ride_gather_send(bufs):
    n = len(bufs)
    halves = [b.shape[0] // 2 for b in bufs]

    def copies(ins, outs, send_sems, recv_sems):
        x, y, c, chips = _place()
        slot = lambda b, cx, cy: _half_rows(outs[b], halves[b], c, lead=2 * cx + cy)
        sends = [_remote(_half_rows(ins[b], halves[b], c), slot(b, x, y), send_sems, recv_sems, 3 * b + j, (cx, cy, c))
                 for b in range(n) for j, (cx, cy) in enumerate(chips)]
        arrivals = lambda: [_remote(slot(b, cx, cy), slot(b, cx, cy), send_sems, recv_sems, 3 * b + j, (x, y, c))
                            for b in range(n) for j, (cx, cy) in enumerate(chips)]
        return sends, arrivals

    return _Rider(bufs, [jax.ShapeDtypeStruct((N_SHARD,) + b.shape, b.dtype) for b in bufs], 3 * n, copies)


def _ride_gather_pass(gathered):
    n = len(gathered)
    halves = [g.shape[1] // 2 for g in gathered]

    def copies(ins, outs, send_sems, recv_sems):
        x, y, c, chips = _place()
        slot = lambda b, cx, cy, hf: _half_rows(outs[b], halves[b], hf, lead=2 * cx + cy)
        sends = [_remote(slot(b, cx, cy, c), slot(b, cx, cy, c), send_sems, recv_sems, 3 * b + j, (x, y, 1 - c))
                 for b in range(n) for j, (cx, cy) in enumerate(chips)]
        arrivals = lambda: [_remote(slot(b, cx, cy, 1 - c), slot(b, cx, cy, 1 - c), send_sems, recv_sems, 3 * b + j, (x, y, c))
                            for b in range(n) for j, (cx, cy) in enumerate(chips)]
        return sends, arrivals

    return _Rider(gathered, [jax.ShapeDtypeStruct(g.shape, g.dtype) for g in gathered], 3 * n, copies,
                  aliases={b: b for b in range(n)})


def _ride_small_gather(buf):
    def copies(ins, outs, send_sems, recv_sems):
        x, y, c, _ = _place()
        flip = lambda v, d: 1 - v if d else v
        sends, peers = [], []
        for dx in (0, 1):
            for dy in (0, 1):
                for dc in (0, 1):
                    if dx or dy or dc:
                        k = 4 * dx + 2 * dy + dc - 1
                        px, py, pc = flip(x, dx), flip(y, dy), flip(c, dc)
                        sends.append(_remote(ins[0], outs[0].at[4 * x + 2 * y + c], send_sems, recv_sems, k, (px, py, pc)))
                        peers.append((k, 4 * px + 2 * py + pc))
        arrivals = lambda: [_remote(ins[0], outs[0].at[slot], send_sems, recv_sems, k, (x, y, c)) for k, slot in peers]
        return sends, arrivals

    return _Rider([buf], [jax.ShapeDtypeStruct((8,) + buf.shape, buf.dtype)], 7, copies)


def _small_sum(gathered, buf, me_arr):
    n, r, width = gathered.shape

    def body(me_ref, g_ref, b_ref, o_ref):
        total = jnp.zeros((r, width), F32)
        for d in range(n):
            total = total + jnp.where(me_ref[0] == d, b_ref[...], g_ref[d])
        o_ref[...] = total

    return pl.pallas_call(
        body, name="small_sum", out_shape=jax.ShapeDtypeStruct((r, width), F32),
        grid_spec=pltpu.PrefetchScalarGridSpec(
            num_scalar_prefetch=1, grid=(1,),
            in_specs=[pl.BlockSpec((n, r, width), lambda i, me: (0, 0, 0)), pl.BlockSpec((r, width), lambda i, me: (0, 0))],
            out_specs=pl.BlockSpec((r, width), lambda i, me: (0, 0))),
        compiler_params=_cparams(("arbitrary",)))(me_arr, gathered, buf)


def _ride_pair_share(rhalves):
    n = len(rhalves)

    def copies(ins, outs, send_sems, recv_sems):
        x, y, c, _ = _place()
        sends = [_remote(ins[b], outs[b], send_sems, recv_sems, b, (x, y, 1 - c)) for b in range(n)]
        arrivals = lambda: [_remote(outs[b], outs[b], send_sems, recv_sems, b, (x, y, c)) for b in range(n)]
        return sends, arrivals

    return _Rider(rhalves, [jax.ShapeDtypeStruct(r.shape, r.dtype) for r in rhalves], n, copies)


def _ride_pair_exchange(gbufs):
    n = len(gbufs)
    halves = [g.shape[1] // 2 for g in gbufs]

    def copies(ins, outs, send_sems, recv_sems):
        x, y, c, _ = _place()
        sends = [_remote(ins[b].at[:, pl.ds(pl.multiple_of((1 - c) * halves[b], 16), halves[b]), :], outs[b],
                         send_sems, recv_sems, b, (x, y, 1 - c)) for b in range(n)]
        arrivals = lambda: [_remote(outs[b], outs[b], send_sems, recv_sems, b, (x, y, c)) for b in range(n)]
        return sends, arrivals

    return _Rider(gbufs, [jax.ShapeDtypeStruct((N_SHARD, h, g.shape[2]), g.dtype) for g, h in zip(gbufs, halves)], n, copies)


def _ride_chip_exchange(parts):
    n = len(parts)

    def copies(ins, outs, send_sems, recv_sems):
        x, y, c, chips = _place()
        sends = [_remote(ins[b].at[2 * cx + cy], outs[b].at[j], send_sems, recv_sems, 3 * b + j, (cx, cy, c))
                 for b in range(n) for j, (cx, cy) in enumerate(chips)]
        arrivals = lambda: [_remote(ins[b].at[0], outs[b].at[j], send_sems, recv_sems, 3 * b + j, (x, y, c))
                            for b in range(n) for j in range(len(chips))]
        return sends, arrivals

    return _Rider(parts, [jax.ShapeDtypeStruct((3,) + p.shape[1:], p.dtype) for p in parts], 3 * n, copies)


def _gather_shards(bufs, name):
    n = len(bufs)
    halves = [b.shape[0] // 2 for b in bufs]

    def body(*refs):
        ins, outs, send_sems, recv_sems = refs[:n], refs[n:2 * n], refs[2 * n], refs[2 * n + 1]
        x, y, c, chips = _place()
        me, sibling = (x, y, c), (x, y, 1 - c)
        slot = lambda b, cx, cy, hf: _half_rows(outs[b], halves[b], hf, lead=2 * cx + cy)
        first = [_remote(_half_rows(ins[b], halves[b], c), slot(b, x, y, c), send_sems, recv_sems, 6 * b + j, (cx, cy, c))
                 for b in range(n) for j, (cx, cy) in enumerate(chips)]
        for cp in first:
            cp.start()
        passed = []
        for j, (cx, cy) in enumerate(chips):
            for b in range(n):
                _remote(slot(b, cx, cy, c), slot(b, cx, cy, c), send_sems, recv_sems, 6 * b + j, me).wait_recv()
                fwd = _remote(slot(b, cx, cy, c), slot(b, cx, cy, c), send_sems, recv_sems, 6 * b + 3 + j, sibling)
                fwd.start()
                passed.append(fwd)
        for j, (cx, cy) in enumerate(chips):
            for b in range(n):
                _remote(slot(b, cx, cy, 1 - c), slot(b, cx, cy, 1 - c), send_sems, recv_sems, 6 * b + 3 + j, me).wait_recv()
        for cp in first + passed:
            cp.wait_send()

    return pl.pallas_call(
        body, name=name, out_shape=[jax.ShapeDtypeStruct((N_SHARD,) + b.shape, b.dtype) for b in bufs],
        in_specs=[_HBM] * n, out_specs=[_HBM] * n,
        scratch_shapes=[pltpu.SemaphoreType.DMA((6 * n,)), pltpu.SemaphoreType.DMA((6 * n,))],
    )(*bufs)


def _reduce_pair_exchange(gbufs, name):
    n = len(gbufs)
    halves = [g.shape[1] // 2 for g in gbufs]

    def body(*refs):
        ins, outs, send_sems, recv_sems = refs[:n], refs[n:2 * n], refs[2 * n], refs[2 * n + 1]
        x, y, c, _ = _place()
        cps = [_remote(ins[b].at[:, pl.ds(pl.multiple_of((1 - c) * halves[b], 16), halves[b]), :], outs[b],
                       send_sems, recv_sems, b, (x, y, 1 - c)) for b in range(n)]
        for cp in cps:
            cp.start()
        for cp in cps:
            cp.wait()

    return pl.pallas_call(
        body, name=name,
        out_shape=[jax.ShapeDtypeStruct((N_SHARD, h, g.shape[2]), g.dtype) for g, h in zip(gbufs, halves)],
        in_specs=[_HBM] * n, out_specs=[_HBM] * n,
        scratch_shapes=[pltpu.SemaphoreType.DMA((n,)), pltpu.SemaphoreType.DMA((n,))],
    )(*gbufs)


def _pair_add(gbuf, recv, c_arr, tr, name):
    _, rows, width = gbuf.shape
    half = rows // 2
    nt = half // tr

    def body(c_ref, a_ref, b_ref, o_ref):
        o_ref[...] = (a_ref[...] + b_ref[...]).astype(o_ref.dtype)

    blk = lambda f: pl.BlockSpec((None, tr, width), f)
    return pl.pallas_call(
        body, name=name, out_shape=jax.ShapeDtypeStruct((N_SHARD, half, width), jnp.bfloat16),
        grid_spec=pltpu.PrefetchScalarGridSpec(
            num_scalar_prefetch=1, grid=(N_SHARD, nt),
            in_specs=[blk(lambda s, i, c: (s, c[0] * nt + i, 0)), blk(lambda s, i, c: (s, i, 0))],
            out_specs=blk(lambda s, i, c: (s, i, 0))),
        compiler_params=_cparams(("parallel", "parallel")))(c_arr, gbuf, recv)


def _chip_add(part, recv, me_arr, tr, name):
    _, half, width = part.shape

    def body(me_ref, own, a0, a1, a2, o_ref):
        f = lambda r: r[...].astype(F32)
        o_ref[...] = ((f(own) + f(a0)) + f(a1)) + f(a2)

    specs = [pl.BlockSpec((None, tr, width), lambda i, me: (me[0], i, 0))]
    specs += [pl.BlockSpec((None, tr, width), functools.partial(lambda i, me, k: (k, i, 0), k=k)) for k in range(3)]
    return pl.pallas_call(
        body, name=name, out_shape=jax.ShapeDtypeStruct((half, width), F32),
        grid_spec=pltpu.PrefetchScalarGridSpec(
            num_scalar_prefetch=1, grid=(half // tr,), in_specs=specs,
            out_specs=pl.BlockSpec((tr, width), lambda i, me: (i, 0))),
        compiler_params=_cparams(("parallel",)))(me_arr, part, recv, recv, recv)


def _reduce_pair_share(rhalves, name):
    n = len(rhalves)

    def body(*refs):
        ins, outs, send_sems, recv_sems = refs[:n], refs[n:2 * n], refs[2 * n], refs[2 * n + 1]
        x, y, c, _ = _place()
        cps = [_remote(ins[b], outs[b], send_sems, recv_sems, b, (x, y, 1 - c)) for b in range(n)]
        for cp in cps:
            cp.start()
        for cp in cps:
            cp.wait()

    return pl.pallas_call(
        body, name=name, out_shape=[jax.ShapeDtypeStruct(r.shape, r.dtype) for r in rhalves],
        in_specs=[_HBM] * n, out_specs=[_HBM] * n,
        scratch_shapes=[pltpu.SemaphoreType.DMA((n,)), pltpu.SemaphoreType.DMA((n,))],
    )(*rhalves)


def _row_tile(rows, cap):
    if rows <= cap:
        return rows
    t = (cap // 8) * 8
    while t >= 8:
        if rows % t == 0:
            return t
        t -= 8
    return rows


def _adamw(w, g, m, v, name):
    shape = w.shape
    cols = shape[-1] if len(shape) <= 3 else shape[-2] * shape[-1]
    lead = len(shape) == 3
    w2, g2, m2, v2 = (a if lead else a.reshape(-1, cols) for a in (w, g, m, v))
    rows = shape[1] if lead else w2.shape[0]
    tr, tc = _row_tile(rows, 256), cols
    if tr == rows and rows > 256:
        tc = _tile(cols, 256)

    def body(w_ref, g_ref, m_ref, v_ref, d_ref, mo_ref, vo_ref):
        gv = g_ref[...]
        mn = ADAM_B1 * m_ref[...] + (1.0 - ADAM_B1) * gv
        vn = ADAM_B2 * v_ref[...] + (1.0 - ADAM_B2) * (gv * gv)
        m_hat = mn / (1.0 - ADAM_B1 ** ADAM_STEP)
        v_hat = vn / (1.0 - ADAM_B2 ** ADAM_STEP)
        d_ref[...] = -ADAM_LR * (m_hat / (jnp.sqrt(v_hat) + ADAM_EPS) + ADAM_WD * w_ref[...])
        mo_ref[...] = mn
        vo_ref[...] = vn

    blk = (pl.BlockSpec((None, tr, tc), lambda i, j: (0, i, j)) if lead else pl.BlockSpec((tr, tc), lambda i, j: (i, j)))
    outs = pl.pallas_call(
        body, name=name, grid=(rows // tr, cols // tc), in_specs=[blk] * 4, out_specs=[blk] * 3,
        out_shape=[jax.ShapeDtypeStruct(w2.shape, F32)] * 3,
        compiler_params=_cparams(("parallel", "parallel")))(w2, g2, m2, v2)
    return tuple(o.reshape(shape) for o in outs)


_WEIGHT_NAMES = ('w_in', 'conv_w', 'dn_a_log', 'dn_dt_bias', 'dn_norm_w', 'q_norm_w', 'w_uq', 'kv_norm_w', 'w_uk',
                 'w_uv', 'w_br_dn', 'w_br_mla', 'w_o', 'ln1_g', 'ln1_b', 'w_ffn_in', 'w_ffn_out', 'w_ple',
                 'w_ple_gate', 'ln2_g', 'ln2_b')
_SMALL_NAMES = ('ln1_g', 'ln1_b', 'ln2_g', 'ln2_b', 'q_norm_w', 'kv_norm_w', 'dn_norm_w', 'dn_a_log', 'dn_dt_bias')
_SMALL_GROUP = 8
_CONV_SMALL_ROW = len(_SMALL_NAMES) * _SMALL_GROUP
_CONV_SMALL_ROWS = DN_CONV * QKV_W // FLAT_W


_LOSS_SMALL_ROW = _CONV_SMALL_ROW + 16


def _pack_small(gw, loss_lanes):
    rows = [jnp.pad(gw[n][None, :], ((0, _SMALL_GROUP - 1), (0, FLAT_W - gw[n].shape[0]))) for n in _SMALL_NAMES]
    rows.append(jnp.pad(gw['conv_w'].reshape(_CONV_SMALL_ROWS, FLAT_W), ((0, 16 - _CONV_SMALL_ROWS), (0, 0))))
    rows.append(jnp.pad(loss_lanes, ((0, _SMALL_GROUP - 1), (0, FLAT_W - LANE))))
    return jnp.concatenate(rows, axis=0)


class _Exchange:
    def __init__(self, local, me_chip, c_arr, me_arr):
        self.local, self.me_chip, self.c_arr, self.me_arr = local, me_chip, c_arr, me_arr
        self.parts = self.arrived = None

    def gather_send(self):
        return _ride_gather_send(self.local)

    def gather_pass(self, sent):
        return _ride_gather_pass(sent)

    def weights(self, gathered):
        return _prep_weights(_unpack_rest(gathered, self.local, self.me_chip))

    def pair_send(self, g):
        self.gbufs = _pack_grads_rest(_unprep_grads_early(g))
        return _ride_pair_exchange(self.gbufs)

    def reduce_send(self, got):
        self.parts = [_pair_add(g_, r_, self.c_arr, tr, "pair_add_%d" % (i + 1))
                      for i, (g_, r_, tr) in enumerate(zip(self.gbufs, got, _ADD_TILES[1:]))]
        return _ride_chip_exchange(self.parts)

    def reduce_arrived(self, arrived):
        self.arrived = list(arrived)

    def share_send(self):
        self.mine = [_chip_add(p_, r_, self.me_arr, tr, "chip_add_%d" % (i + 1))
                     for i, (p_, r_, tr) in enumerate(zip(self.parts, self.arrived, _ADD_TILES[1:]))]
        return _ride_pair_share(self.mine)

    def share_arrived(self, arrived):
        self.theirs = list(arrived)

    def in_send(self, g):
        g_in = [_shard_columns(_w_in_grad(g), W_IN_SHARD)]
        got = _reduce_pair_exchange(g_in, "reduce_pair_exchange_w_in")
        self.part_in = _pair_add(g_in[0], got[0], self.c_arr, _ADD_TILES[0], "pair_add_0")
        return _ride_chip_exchange([self.part_in])

    def in_arrived(self, arrived):
        self.arrived_in = list(arrived)

    def small_send(self, g):
        self.small = _pack_small(_unprep_grads_late(g), g['loss_lanes'])
        return _ride_small_gather(self.small)

    def small_arrived(self, arrived):
        self.small_gathered = arrived[0]


def kernel(x, p, positions, w_in, conv_w, dn_a_log, dn_dt_bias, dn_norm_w, q_norm_w, w_uq, kv_norm_w, w_uk, w_uv, w_br_dn, w_br_mla, w_o, ln1_g, ln1_b, w_ffn_in, w_ffn_out, w_ple, w_ple_gate, ln2_g, ln2_b, loss_target, m_w_in, m_conv_w, m_dn_a_log, m_dn_dt_bias, m_dn_norm_w, m_q_norm_w, m_w_uq, m_kv_norm_w, m_w_uk, m_w_uv, m_w_br_dn, m_w_br_mla, m_w_o, m_ln1_g, m_ln1_b, m_w_ffn_in, m_w_ffn_out, m_w_ple, m_w_ple_gate, m_ln2_g, m_ln2_b, v_w_in, v_conv_w, v_dn_a_log, v_dn_dt_bias, v_dn_norm_w, v_q_norm_w, v_w_uq, v_kv_norm_w, v_w_uk, v_w_uv, v_w_br_dn, v_w_br_mla, v_w_o, v_ln1_g, v_ln1_b, v_w_ffn_in, v_w_ffn_out, v_w_ple, v_w_ple_gate, v_ln2_g, v_ln2_b):
    ws = dict(w_in=w_in, conv_w=conv_w, dn_a_log=dn_a_log, dn_dt_bias=dn_dt_bias, dn_norm_w=dn_norm_w, q_norm_w=q_norm_w,
              w_uq=w_uq, kv_norm_w=kv_norm_w, w_uk=w_uk, w_uv=w_uv, w_br_dn=w_br_dn, w_br_mla=w_br_mla, w_o=w_o,
              ln1_g=ln1_g, ln1_b=ln1_b, w_ffn_in=w_ffn_in, w_ffn_out=w_ffn_out, w_ple=w_ple, w_ple_gate=w_ple_gate,
              ln2_g=ln2_g, ln2_b=ln2_b)
    ms = dict(w_in=m_w_in, conv_w=m_conv_w, dn_a_log=m_dn_a_log, dn_dt_bias=m_dn_dt_bias, dn_norm_w=m_dn_norm_w,
              q_norm_w=m_q_norm_w, w_uq=m_w_uq, kv_norm_w=m_kv_norm_w, w_uk=m_w_uk, w_uv=m_w_uv, w_br_dn=m_w_br_dn,
              w_br_mla=m_w_br_mla, w_o=m_w_o, ln1_g=m_ln1_g, ln1_b=m_ln1_b, w_ffn_in=m_w_ffn_in, w_ffn_out=m_w_ffn_out,
              w_ple=m_w_ple, w_ple_gate=m_w_ple_gate, ln2_g=m_ln2_g, ln2_b=m_ln2_b)
    vs = dict(w_in=v_w_in, conv_w=v_conv_w, dn_a_log=v_dn_a_log, dn_dt_bias=v_dn_dt_bias, dn_norm_w=v_dn_norm_w,
              q_norm_w=v_q_norm_w, w_uq=v_w_uq, kv_norm_w=v_kv_norm_w, w_uk=v_w_uk, w_uv=v_w_uv, w_br_dn=v_w_br_dn,
              w_br_mla=v_w_br_mla, w_o=v_w_o, ln1_g=v_ln1_g, ln1_b=v_ln1_b, w_ffn_in=v_w_ffn_in, w_ffn_out=v_w_ffn_out,
              w_ple=v_w_ple, w_ple_gate=v_w_ple_gate, ln2_g=v_ln2_g, ln2_b=v_ln2_b)
    mx, my, mc = lax.axis_index("x"), lax.axis_index("y"), lax.axis_index("c")

    me_chip = 2 * mx + my
    c_arr = jnp.reshape(mc, (1,)).astype(jnp.int32)
    me_arr = jnp.reshape(me_chip, (1,)).astype(jnp.int32)
    sharded = ('w_in', 'w_ffn_in') + tuple(name for name, _, _ in _FLATB_PIECES)

    local = _pack_shards({name: ws[name][0] for name in sharded}, conv_w[0])
    (gathered_in,) = _gather_shards(local[:1], "gather_w_in")
    w_in_full, conv_full = _unpack_w_in(gathered_in, local[0], me_chip)
    small = {n: ws[n][0] for n in _SMALL_NAMES}
    small['conv_w'] = conv_full
    sp = _prep_small(small)
    cosb, sinb = _rope_tables(positions[0])
    exch = _Exchange(local[1:], me_chip, c_arr, me_arr)

    loss_lanes, dx, g = _local_step(x[0], p[0, 0], cosb, sinb, loss_target[0], _prep_w_in(w_in_full), sp, exch)

    mine_in = _chip_add(exch.part_in, exch.arrived_in[0], me_arr, _ADD_TILES[0], "chip_add_0")
    theirs_in = list(_reduce_pair_share([mine_in], "reduce_pair_share"))
    reduced = _unpack_reduced([mine_in] + exch.mine, theirs_in + exch.theirs, mc)
    tot = _small_sum(exch.small_gathered, exch.small, jnp.reshape(4 * mx + 2 * my + mc, (1,)).astype(jnp.int32))
    loss = jnp.sum(tot[_LOSS_SMALL_ROW, :LANE])
    gred = {name: reduced[name][None] for name in sharded}
    for i, n in enumerate(_SMALL_NAMES):
        gred[n] = tot[i * _SMALL_GROUP, :ws[n].shape[1]][None]
    conv_tot = tot[_CONV_SMALL_ROW:_CONV_SMALL_ROW + _CONV_SMALL_ROWS].reshape(DN_CONV, QKV_W)
    gred['conv_w'] = lax.dynamic_slice_in_dim(conv_tot, (2 * mx + my) * _CONV_SHARD, _CONV_SHARD, axis=1)[None]

    deltas, new_m, new_v = {}, {}, {}
    for n in _WEIGHT_NAMES:
        if n == 'w_in':
            tr_ = lambda a: jnp.transpose(a, (0, 2, 1))
            g_t = tr_(gred[n].reshape(ws[n].shape))
            outs = _adamw(tr_(ws[n]), g_t, tr_(ms[n]), tr_(vs[n]), "adamw_" + n)
            gred[n] = tr_(g_t)
            deltas[n], new_m[n], new_v[n] = (tr_(o) for o in outs)
            continue
        gred[n] = gred[n].reshape(ws[n].shape)
        deltas[n], new_m[n], new_v[n] = _adamw(ws[n], gred[n], ms[n], vs[n], "adamw_" + n)
    return (loss, dx[None], *[gred[n] for n in _WEIGHT_NAMES], *[deltas[n] for n in _WEIGHT_NAMES],
            *[new_m[n] for n in _WEIGHT_NAMES], *[new_v[n] for n in _WEIGHT_NAMES])
```
